```python
import math
import jax, jax.numpy as jnp
from jax import lax
import numpy as np

D_MODEL = 2048
BATCH = 8
SEQ = 4096
DEPTH = 1

D_MIX = D_MODEL
ATTN_W = D_MIX // 2
SSM_W = D_MIX - ATTN_W
HEAD_DIM = 64
N_HEADS = ATTN_W // HEAD_DIM
N_KV_HEADS = 4
KV_REP = N_HEADS // N_KV_HEADS
KV_W = N_KV_HEADS * HEAD_DIM
WINDOW = 128
BLOCK = 128
ROPE_THETA = 10000.0
SSM_H = 16
SSM_G = SSM_W // SSM_H
SSM_P = 64
SSM_CHUNK = 128
DT_MIN = 1e-3
DT_MAX = 1e-1
NORM_EPS = 1e-6
IN_W = ATTN_W + 2 * KV_W + ATTN_W + SSM_W + SSM_W

kernel_name = "hymba_swa_sink_s5_hybrid"


def rms_norm(x, w):
    xf = x.astype(jnp.float32)
    y = xf * lax.rsqrt(jnp.mean(xf * xf, axis=-1, keepdims=True) + NORM_EPS)
    return (y * w.astype(jnp.float32)).astype(x.dtype)


def rope_tables(positions):
    inv_freq = ROPE_THETA ** (-jnp.arange(0, HEAD_DIM, 2, dtype=jnp.float32) / HEAD_DIM)
    ang = positions.astype(jnp.float32)[..., None] * inv_freq
    return jnp.cos(ang)[:, :, None, :], jnp.sin(ang)[:, :, None, :]


def apply_rope(t, cos, sin):
    tf = t.astype(jnp.float32)
    t1, t2 = tf[..., : HEAD_DIM // 2], tf[..., HEAD_DIM // 2 :]
    out = jnp.concatenate([t1 * cos - t2 * sin, t2 * cos + t1 * sin], axis=-1)
    return out.astype(t.dtype)


def swa_sink_attention(q, k, v, positions, q_norm_w, k_norm_w, sinks):
    B, L = q.shape[0], q.shape[1]
    nb = L // BLOCK
    q = q.reshape(B, L, N_HEADS, HEAD_DIM)
    k = k.reshape(B, L, N_KV_HEADS, HEAD_DIM)
    v = v.reshape(B, L, N_KV_HEADS, HEAD_DIM)
    cos, sin = rope_tables(positions)
    q = apply_rope(rms_norm(q, q_norm_w), cos, sin)
    k = apply_rope(rms_norm(k, k_norm_w), cos, sin)

    qb = q.reshape(B, nb, BLOCK, N_KV_HEADS, KV_REP, HEAD_DIM)

    def with_prev(t):
        t = t.reshape(B, nb, BLOCK, N_KV_HEADS, HEAD_DIM)
        prev = jnp.pad(t[:, :-1], ((0, 0), (1, 0), (0, 0), (0, 0), (0, 0)))
        return jnp.concatenate([prev, t], axis=2)

    kb, vb = with_prev(k), with_prev(v)
    scale = 1.0 / math.sqrt(HEAD_DIM)
    s = jnp.einsum('bnqgrd,bnkgd->bngrqk', qb, kb).astype(jnp.float32) * scale

    qi = jnp.arange(BLOCK)[:, None] + BLOCK
    ki = jnp.arange(2 * BLOCK)[None, :]
    rel = qi - ki
    band = (rel >= 0) & (rel < WINDOW)
    has_prev = (jnp.arange(nb)[:, None, None] > 0) | (ki >= BLOCK)[None]
    mask = band[None] & has_prev
    s = jnp.where(mask[None, :, None, None], s, jnp.float32(-1e30))

    sink = jnp.broadcast_to(
        sinks.astype(jnp.float32).reshape(1, 1, N_KV_HEADS, KV_REP, 1, 1),
        s.shape[:-1] + (1,))
    p = jax.nn.softmax(jnp.concatenate([s, sink], axis=-1), axis=-1)[..., :-1]
    o = jnp.einsum('bngrqk,bnkgd->bnqgrd', p.astype(vb.dtype), vb)
    return o.reshape(B, L, ATTN_W)


def s5_ssm(u, a_re, a_im, log_step, b_re, b_im, c_re, c_im, d_skip):
    B, L = u.shape[0], u.shape[1]
    nc = L // SSM_CHUNK
    f32 = jnp.float32
    lam = lax.complex(a_re.astype(f32), a_im.astype(f32))
    delta = jnp.exp(log_step.astype(f32))[:, None]
    lam_bar = jnp.exp(lam * delta)
    b = lax.complex(b_re.astype(f32), b_im.astype(f32))
    b_bar = ((lam_bar - 1.0) / lam)[..., None] * b
    c = lax.complex(c_re.astype(f32), c_im.astype(f32))

    uf = u.astype(f32)
    ug = uf.reshape(B, nc, SSM_CHUNK, SSM_G, SSM_H).transpose(1, 0, 2, 3, 4)

    def combine(left, right):
        a_l, x_l = left
        a_r, x_r = right
        return a_r * a_l, a_r * x_l + x_r

    def chunk_step(h0, u_c):
        bu = jnp.einsum('gph,bigh->bigp', b_bar, u_c.astype(jnp.complex64))
        a = jnp.broadcast_to(lam_bar, bu.shape)
        a_cum, h_loc = lax.associative_scan(combine, (a, bu), axis=1)
        h = h_loc + a_cum * h0[:, None]
        y = jnp.einsum('ghp,bigp->bigh', c, h).real
        return h[:, -1], y

    h0 = jnp.zeros((B, SSM_G, SSM_P), jnp.complex64)
    _, ys = lax.scan(chunk_step, h0, ug)
    y = ys.transpose(1, 0, 2, 3, 4).reshape(B, L, SSM_W)
    return y + d_skip.astype(f32) * uf


def _fwd_setup_inputs(seed: int = 0) -> dict:
    key = jax.random.key(seed)
    ks = jax.random.split(key, 20)
    f32 = jnp.float32
    x = jax.random.normal(ks[0], (BATCH, SEQ, D_MODEL), f32)
    offs = jax.random.randint(ks[1], (BATCH, 1), 0, 1024, dtype=jnp.int32)
    positions = (jnp.arange(SEQ, dtype=jnp.int32)[None, :] + offs).astype(jnp.int32)
    norm_w = 1.0 + 0.02 * jax.random.normal(ks[2], (D_MODEL,), f32)
    w_in = jax.random.normal(ks[3], (D_MODEL, IN_W), f32) * D_MODEL ** -0.5
    q_norm_w = 1.0 + 0.02 * jax.random.normal(ks[4], (HEAD_DIM,), f32)
    k_norm_w = 1.0 + 0.02 * jax.random.normal(ks[5], (HEAD_DIM,), f32)
    sinks = jax.random.normal(ks[6], (N_HEADS,), f32)
    n = jnp.arange(SSM_P, dtype=f32)[None, :]
    a_re = -0.5 + 0.01 * jax.random.normal(ks[7], (SSM_G, SSM_P), f32)
    a_im = math.pi * n + 0.01 * jax.random.normal(ks[8], (SSM_G, SSM_P), f32)
    log_step = jax.random.uniform(ks[9], (SSM_G,), f32, math.log(DT_MIN), math.log(DT_MAX))
    b_scale = (2.0 * SSM_H) ** -0.5
    b_re = jax.random.normal(ks[10], (SSM_G, SSM_P, SSM_H), f32) * b_scale
    b_im = jax.random.normal(ks[11], (SSM_G, SSM_P, SSM_H), f32) * b_scale
    c_scale = (2.0 * SSM_P) ** -0.5
    c_re = jax.random.normal(ks[12], (SSM_G, SSM_H, SSM_P), f32) * c_scale
    c_im = jax.random.normal(ks[13], (SSM_G, SSM_H, SSM_P), f32) * c_scale
    d_skip = jax.random.normal(ks[14], (SSM_W,), f32)
    w_glu = jax.random.normal(ks[15], (SSM_W, SSM_W), f32) * SSM_W ** -0.5
    b_glu = 0.02 * jax.random.normal(ks[16], (SSM_W,), f32)
    attn_out_norm_w = 1.0 + 0.02 * jax.random.normal(ks[17], (ATTN_W,), f32)
    ssm_out_norm_w = 1.0 + 0.02 * jax.random.normal(ks[18], (SSM_W,), f32)
    w_out = jax.random.normal(ks[19], (D_MIX, D_MODEL), f32) * D_MIX ** -0.5
    return {"x": x, "positions": positions, "norm_w": norm_w, "w_in": w_in,
            "q_norm_w": q_norm_w, "k_norm_w": k_norm_w, "sinks": sinks,
            "a_re": a_re, "a_im": a_im, "log_step": log_step,
            "b_re": b_re, "b_im": b_im, "c_re": c_re, "c_im": c_im,
            "d_skip": d_skip, "w_glu": w_glu, "b_glu": b_glu,
            "attn_out_norm_w": attn_out_norm_w, "ssm_out_norm_w": ssm_out_norm_w,
            "w_out": w_out}


def _fwd_reference(x, positions, norm_w, w_in, q_norm_w, k_norm_w, sinks,
              a_re, a_im, log_step, b_re, b_im, c_re, c_im, d_skip, w_glu, b_glu,
              attn_out_norm_w, ssm_out_norm_w, w_out):
    for _ in range(DEPTH):
        h = rms_norm(x, norm_w)
        proj = jnp.einsum('bld,de->ble', h, w_in)
        splits = np.cumsum([ATTN_W, KV_W, KV_W, ATTN_W, SSM_W])
        q, k, v, z_attn, u, z_ssm = jnp.split(proj, splits, axis=-1)

        o_attn = swa_sink_attention(q, k, v, positions, q_norm_w, k_norm_w, sinks)
        o_attn = o_attn * jax.nn.silu(z_attn)

        y = s5_ssm(u, a_re, a_im, log_step, b_re, b_im, c_re, c_im, d_skip)
        y = jax.nn.gelu(y)
        y = y * jax.nn.sigmoid(y @ w_glu.astype(jnp.float32) + b_glu.astype(jnp.float32))
        o_ssm = y.astype(x.dtype) * jax.nn.silu(z_ssm)

        merged = jnp.concatenate([rms_norm(o_attn, attn_out_norm_w),
                                  rms_norm(o_ssm, ssm_out_norm_w)], axis=-1)
        x = x + jnp.einsum('ble,ed->bld', merged, w_out).astype(x.dtype)
    return x


import jax as _jax
import jax.numpy as _jnp

TWIN_FORMAT = 'train_step'
FWD_PARAMS = ['x', 'positions', 'norm_w', 'w_in', 'q_norm_w', 'k_norm_w', 'sinks', 'a_re', 'a_im', 'log_step', 'b_re', 'b_im', 'c_re', 'c_im', 'd_skip', 'w_glu', 'b_glu', 'attn_out_norm_w', 'ssm_out_norm_w', 'w_out']
TWIN_WEIGHTS = ['norm_w', 'w_in', 'q_norm_w', 'k_norm_w', 'sinks', 'a_re', 'a_im', 'log_step', 'b_re', 'b_im', 'c_re', 'c_im', 'd_skip', 'w_glu', 'b_glu', 'attn_out_norm_w', 'ssm_out_norm_w', 'w_out']
TWIN_DIFF_INPUT = 'x'
TWIN_INPUTS = ['x', 'positions', 'norm_w', 'w_in', 'q_norm_w', 'k_norm_w', 'sinks', 'a_re', 'a_im', 'log_step', 'b_re', 'b_im', 'c_re', 'c_im', 'd_skip', 'w_glu', 'b_glu', 'attn_out_norm_w', 'ssm_out_norm_w', 'w_out', 'loss_target', 'm_norm_w', 'm_w_in', 'm_q_norm_w', 'm_k_norm_w', 'm_sinks', 'm_a_re', 'm_a_im', 'm_log_step', 'm_b_re', 'm_b_im', 'm_c_re', 'm_c_im', 'm_d_skip', 'm_w_glu', 'm_b_glu', 'm_attn_out_norm_w', 'm_ssm_out_norm_w', 'm_w_out', 'v_norm_w', 'v_w_in', 'v_q_norm_w', 'v_k_norm_w', 'v_sinks', 'v_a_re', 'v_a_im', 'v_log_step', 'v_b_re', 'v_b_im', 'v_c_re', 'v_c_im', 'v_d_skip', 'v_w_glu', 'v_b_glu', 'v_attn_out_norm_w', 'v_ssm_out_norm_w', 'v_w_out']
TWIN_OUTPUTS = ['loss', 'grad_x', 'grad_norm_w', 'grad_w_in', 'grad_q_norm_w', 'grad_k_norm_w', 'grad_sinks', 'grad_a_re', 'grad_a_im', 'grad_log_step', 'grad_b_re', 'grad_b_im', 'grad_c_re', 'grad_c_im', 'grad_d_skip', 'grad_w_glu', 'grad_b_glu', 'grad_attn_out_norm_w', 'grad_ssm_out_norm_w', 'grad_w_out', 'delta_norm_w', 'delta_w_in', 'delta_q_norm_w', 'delta_k_norm_w', 'delta_sinks', 'delta_a_re', 'delta_a_im', 'delta_log_step', 'delta_b_re', 'delta_b_im', 'delta_c_re', 'delta_c_im', 'delta_d_skip', 'delta_w_glu', 'delta_b_glu', 'delta_attn_out_norm_w', 'delta_ssm_out_norm_w', 'delta_w_out', 'new_m_norm_w', 'new_m_w_in', 'new_m_q_norm_w', 'new_m_k_norm_w', 'new_m_sinks', 'new_m_a_re', 'new_m_a_im', 'new_m_log_step', 'new_m_b_re', 'new_m_b_im', 'new_m_c_re', 'new_m_c_im', 'new_m_d_skip', 'new_m_w_glu', 'new_m_b_glu', 'new_m_attn_out_norm_w', 'new_m_ssm_out_norm_w', 'new_m_w_out', 'new_v_norm_w', 'new_v_w_in', 'new_v_q_norm_w', 'new_v_k_norm_w', 'new_v_sinks', 'new_v_a_re', 'new_v_a_im', 'new_v_log_step', 'new_v_b_re', 'new_v_b_im', 'new_v_c_re', 'new_v_c_im', 'new_v_d_skip', 'new_v_w_glu', 'new_v_b_glu', 'new_v_attn_out_norm_w', 'new_v_ssm_out_norm_w', 'new_v_w_out']
TWIN_LEAF_KINDS = {'loss': 'loss', 'grad_x': 'grad_x', 'grad_norm_w': 'grad_w', 'grad_w_in': 'grad_w', 'grad_q_norm_w': 'grad_w', 'grad_k_norm_w': 'grad_w', 'grad_sinks': 'grad_w', 'grad_a_re': 'grad_w', 'grad_a_im': 'grad_w', 'grad_log_step': 'grad_w', 'grad_b_re': 'grad_w', 'grad_b_im': 'grad_w', 'grad_c_re': 'grad_w', 'grad_c_im': 'grad_w', 'grad_d_skip': 'grad_w', 'grad_w_glu': 'grad_w', 'grad_b_glu': 'grad_w', 'grad_attn_out_norm_w': 'grad_w', 'grad_ssm_out_norm_w': 'grad_w', 'grad_w_out': 'grad_w', 'delta_norm_w': 'delta_w', 'delta_w_in': 'delta_w', 'delta_q_norm_w': 'delta_w', 'delta_k_norm_w': 'delta_w', 'delta_sinks': 'delta_w', 'delta_a_re': 'delta_w', 'delta_a_im': 'delta_w', 'delta_log_step': 'delta_w', 'delta_b_re': 'delta_w', 'delta_b_im': 'delta_w', 'delta_c_re': 'delta_w', 'delta_c_im': 'delta_w', 'delta_d_skip': 'delta_w', 'delta_w_glu': 'delta_w', 'delta_b_glu': 'delta_w', 'delta_attn_out_norm_w': 'delta_w', 'delta_ssm_out_norm_w': 'delta_w', 'delta_w_out': 'delta_w', 'new_m_norm_w': 'new_m', 'new_m_w_in': 'new_m', 'new_m_q_norm_w': 'new_m', 'new_m_k_norm_w': 'new_m', 'new_m_sinks': 'new_m', 'new_m_a_re': 'new_m', 'new_m_a_im': 'new_m', 'new_m_log_step': 'new_m', 'new_m_b_re': 'new_m', 'new_m_b_im': 'new_m', 'new_m_c_re': 'new_m', 'new_m_c_im': 'new_m', 'new_m_d_skip': 'new_m', 'new_m_w_glu': 'new_m', 'new_m_b_glu': 'new_m', 'new_m_attn_out_norm_w': 'new_m', 'new_m_ssm_out_norm_w': 'new_m', 'new_m_w_out': 'new_m', 'new_v_norm_w': 'new_v', 'new_v_w_in': 'new_v', 'new_v_q_norm_w': 'new_v', 'new_v_k_norm_w': 'new_v', 'new_v_sinks': 'new_v', 'new_v_a_re': 'new_v', 'new_v_a_im': 'new_v', 'new_v_log_step': 'new_v', 'new_v_b_re': 'new_v', 'new_v_b_im': 'new_v', 'new_v_c_re': 'new_v', 'new_v_c_im': 'new_v', 'new_v_d_skip': 'new_v', 'new_v_w_glu': 'new_v', 'new_v_b_glu': 'new_v', 'new_v_attn_out_norm_w': 'new_v', 'new_v_ssm_out_norm_w': 'new_v', 'new_v_w_out': 'new_v'}


def _forward(args):
    return _fwd_reference(*[args[k] for k in FWD_PARAMS])


def _output_shape():
    def fwd():
        inp = _fwd_setup_inputs(0)
        return _fwd_reference(*[inp[k] for k in FWD_PARAMS])
    out = _jax.eval_shape(fwd)
    return out.shape, out.dtype

N_MICROBATCH = 1
ADAM_LR = 0.001
ADAM_B1 = 0.9
ADAM_B2 = 0.999
ADAM_EPS = 1e-08
ADAM_WD = 0.01
ADAM_STEP = 10
PER_EXAMPLE_BATCH_AXIS = {'x': 0, 'positions': 0, 'loss_target': 0}
SHARED_INPUTS = []
_WEIGHT_DTYPES = {'norm_w': _jnp.float32, 'w_in': _jnp.float32, 'q_norm_w': _jnp.float32, 'k_norm_w': _jnp.float32, 'sinks': _jnp.float32, 'a_re': _jnp.float32, 'a_im': _jnp.float32, 'log_step': _jnp.float32, 'b_re': _jnp.float32, 'b_im': _jnp.float32, 'c_re': _jnp.float32, 'c_im': _jnp.float32, 'd_skip': _jnp.float32, 'w_glu': _jnp.float32, 'b_glu': _jnp.float32, 'attn_out_norm_w': _jnp.float32, 'ssm_out_norm_w': _jnp.float32, 'w_out': _jnp.float32}
MOMENT_SCALE = {'norm_w': 3.097847e-01, 'w_in': 1.931440e-01, 'q_norm_w': 6.143489e-01, 'k_norm_w': 6.213515e-01, 'sinks': 6.733666e-02, 'a_re': 7.700832e-03, 'a_im': 8.048846e-03, 'log_step': 9.748352e+00, 'b_re': 5.520516e-03, 'b_im': 5.527718e-03, 'c_re': 1.114468e-02, 'c_im': 1.123905e-02, 'd_skip': 8.127332e-01, 'w_glu': 1.456363e-01, 'b_glu': 4.733383e-01, 'attn_out_norm_w': 1.589198e+01, 'ssm_out_norm_w': 2.731906e+01, 'w_out': 4.767608e-01}


def _to_microbatches(a, axis):
    t = _jnp.moveaxis(a, axis, 0)
    t = t.reshape((N_MICROBATCH, t.shape[0] // N_MICROBATCH) + t.shape[1:])
    return _jnp.moveaxis(t, 1, axis + 1)


def setup_inputs(seed: int = 0) -> dict:
    inp = _fwd_setup_inputs(seed)
    key = _jax.random.fold_in(_jax.random.key(seed), 7919)
    shape, _ = _output_shape()
    out = dict(inp)
    out["loss_target"] = _jax.random.normal(_jax.random.fold_in(key, 0), shape, _jnp.float32)
    for i, name in enumerate(TWIN_WEIGHTS):
        w = inp[name].astype(_jnp.float32)
        if MOMENT_SCALE is None:
            s = _jnp.sqrt(_jnp.mean(_jnp.square(w)) + 1e-30)
        else:
            s = MOMENT_SCALE[name]
        km, kv = _jax.random.split(_jax.random.fold_in(key, i + 1))
        out[name] = w
        out["m_" + name] = s * _jax.random.normal(km, w.shape, _jnp.float32)
        out["v_" + name] = (s * s) * _jax.random.uniform(kv, w.shape, _jnp.float32, 0.5, 1.5)
    if N_MICROBATCH > 1:
        for name, axis in PER_EXAMPLE_BATCH_AXIS.items():
            out[name] = _to_microbatches(out[name], axis)
    return {'x': out['x'], 'positions': out['positions'], 'norm_w': out['norm_w'], 'w_in': out['w_in'], 'q_norm_w': out['q_norm_w'], 'k_norm_w': out['k_norm_w'], 'sinks': out['sinks'], 'a_re': out['a_re'], 'a_im': out['a_im'], 'log_step': out['log_step'], 'b_re': out['b_re'], 'b_im': out['b_im'], 'c_re': out['c_re'], 'c_im': out['c_im'], 'd_skip': out['d_skip'], 'w_glu': out['w_glu'], 'b_glu': out['b_glu'], 'attn_out_norm_w': out['attn_out_norm_w'], 'ssm_out_norm_w': out['ssm_out_norm_w'], 'w_out': out['w_out'], 'loss_target': out['loss_target'], 'm_norm_w': out['m_norm_w'], 'm_w_in': out['m_w_in'], 'm_q_norm_w': out['m_q_norm_w'], 'm_k_norm_w': out['m_k_norm_w'], 'm_sinks': out['m_sinks'], 'm_a_re': out['m_a_re'], 'm_a_im': out['m_a_im'], 'm_log_step': out['m_log_step'], 'm_b_re': out['m_b_re'], 'm_b_im': out['m_b_im'], 'm_c_re': out['m_c_re'], 'm_c_im': out['m_c_im'], 'm_d_skip': out['m_d_skip'], 'm_w_glu': out['m_w_glu'], 'm_b_glu': out['m_b_glu'], 'm_attn_out_norm_w': out['m_attn_out_norm_w'], 'm_ssm_out_norm_w': out['m_ssm_out_norm_w'], 'm_w_out': out['m_w_out'], 'v_norm_w': out['v_norm_w'], 'v_w_in': out['v_w_in'], 'v_q_norm_w': out['v_q_norm_w'], 'v_k_norm_w': out['v_k_norm_w'], 'v_sinks': out['v_sinks'], 'v_a_re': out['v_a_re'], 'v_a_im': out['v_a_im'], 'v_log_step': out['v_log_step'], 'v_b_re': out['v_b_re'], 'v_b_im': out['v_b_im'], 'v_c_re': out['v_c_re'], 'v_c_im': out['v_c_im'], 'v_d_skip': out['v_d_skip'], 'v_w_glu': out['v_w_glu'], 'v_b_glu': out['v_b_glu'], 'v_attn_out_norm_w': out['v_attn_out_norm_w'], 'v_ssm_out_norm_w': out['v_ssm_out_norm_w'], 'v_w_out': out['v_w_out']}


def _loss(weights, diff, rest, loss_target):
    with _jax.named_scope("forward"):
        args = {**rest, TWIN_DIFF_INPUT: diff, **{k: w.astype(_WEIGHT_DTYPES[k]) for k, w in weights.items()}}
        y = _forward(args)
    with _jax.named_scope("loss_head"):
        err = _jnp.square(y.astype(_jnp.float32) - loss_target)
        return 0.5 * _jnp.sum(_jnp.mean(err, axis=-1)) if err.ndim else 0.5 * err


def _adamw(w, g, m, v):
    m = ADAM_B1 * m + (1.0 - ADAM_B1) * g
    v = ADAM_B2 * v + (1.0 - ADAM_B2) * _jnp.square(g)
    m_hat = m / (1.0 - ADAM_B1 ** ADAM_STEP)
    v_hat = v / (1.0 - ADAM_B2 ** ADAM_STEP)
    delta = -ADAM_LR * (m_hat / (_jnp.sqrt(v_hat) + ADAM_EPS) + ADAM_WD * w)
    return delta, m, v


def reference(x, positions, norm_w, w_in, q_norm_w, k_norm_w, sinks, a_re, a_im, log_step, b_re, b_im, c_re, c_im, d_skip, w_glu, b_glu, attn_out_norm_w, ssm_out_norm_w, w_out, loss_target, m_norm_w, m_w_in, m_q_norm_w, m_k_norm_w, m_sinks, m_a_re, m_a_im, m_log_step, m_b_re, m_b_im, m_c_re, m_c_im, m_d_skip, m_w_glu, m_b_glu, m_attn_out_norm_w, m_ssm_out_norm_w, m_w_out, v_norm_w, v_w_in, v_q_norm_w, v_k_norm_w, v_sinks, v_a_re, v_a_im, v_log_step, v_b_re, v_b_im, v_c_re, v_c_im, v_d_skip, v_w_glu, v_b_glu, v_attn_out_norm_w, v_ssm_out_norm_w, v_w_out):
    given = dict(x=x, positions=positions, norm_w=norm_w, w_in=w_in, q_norm_w=q_norm_w, k_norm_w=k_norm_w, sinks=sinks, a_re=a_re, a_im=a_im, log_step=log_step, b_re=b_re, b_im=b_im, c_re=c_re, c_im=c_im, d_skip=d_skip, w_glu=w_glu, b_glu=b_glu, attn_out_norm_w=attn_out_norm_w, ssm_out_norm_w=ssm_out_norm_w, w_out=w_out, loss_target=loss_target, m_norm_w=m_norm_w, m_w_in=m_w_in, m_q_norm_w=m_q_norm_w, m_k_norm_w=m_k_norm_w, m_sinks=m_sinks, m_a_re=m_a_re, m_a_im=m_a_im, m_log_step=m_log_step, m_b_re=m_b_re, m_b_im=m_b_im, m_c_re=m_c_re, m_c_im=m_c_im, m_d_skip=m_d_skip, m_w_glu=m_w_glu, m_b_glu=m_b_glu, m_attn_out_norm_w=m_attn_out_norm_w, m_ssm_out_norm_w=m_ssm_out_norm_w, m_w_out=m_w_out, v_norm_w=v_norm_w, v_w_in=v_w_in, v_q_norm_w=v_q_norm_w, v_k_norm_w=v_k_norm_w, v_sinks=v_sinks, v_a_re=v_a_re, v_a_im=v_a_im, v_log_step=v_log_step, v_b_re=v_b_re, v_b_im=v_b_im, v_c_re=v_c_re, v_c_im=v_c_im, v_d_skip=v_d_skip, v_w_glu=v_w_glu, v_b_glu=v_b_glu, v_attn_out_norm_w=v_attn_out_norm_w, v_ssm_out_norm_w=v_ssm_out_norm_w, v_w_out=v_w_out)
    weights = {n: given[n] for n in TWIN_WEIGHTS}
    shared = {n: given[n] for n in SHARED_INPUTS}
    per_example = {n: given[n] for n in ['x', 'positions']}
    grad_fn = _jax.value_and_grad(_loss, argnums=(0, 1))

    def one_microbatch(ex, loss_target):
        ex = dict(ex)
        diff = ex.pop(TWIN_DIFF_INPUT)
        return grad_fn(weights, diff, {**shared, **ex}, loss_target)

    if N_MICROBATCH == 1:
        loss, (grad_w, grad_x) = one_microbatch(per_example, given["loss_target"])
    else:
        def body(carry, xs):
            loss_sum, grad_sum = carry
            l_k, (gw_k, gx_k) = one_microbatch(xs[0], xs[1])
            with _jax.named_scope("update"):
                return (loss_sum + l_k, _jax.tree.map(_jnp.add, grad_sum, gw_k)), gx_k

        init = (_jnp.zeros((), _jnp.float32), _jax.tree.map(_jnp.zeros_like, weights))
        (loss, grad_w), grad_x = _jax.lax.scan(body, init, (per_example, given["loss_target"]))
    with _jax.named_scope("update"):
        delta_w, new_m, new_v = {}, {}, {}
        for n in TWIN_WEIGHTS:
            delta_w[n], new_m[n], new_v[n] = _adamw(weights[n], grad_w[n], given["m_" + n], given["v_" + n])
    return (loss, grad_x, *[grad_w[n] for n in TWIN_WEIGHTS], *[delta_w[n] for n in TWIN_WEIGHTS],
            *[new_m[n] for n in TWIN_WEIGHTS], *[new_v[n] for n in TWIN_WEIGHTS])
```

```python
import functools
import math

import jax
import jax.numpy as jnp
from jax import lax
from jax.experimental import pallas as pl
from jax.experimental.pallas import tpu as pltpu

F32 = jnp.float32
BF16 = jnp.bfloat16

D_MODEL = 2048
ATTN_W = 1024
KV_W = 256
SSM_W = 1024
HEAD_DIM = 64
N_HEADS = 16
N_KV = 4
KV_REP = 4
IN_W = 4608
BLOCK = 128
ROPE_THETA = 10000.0
NORM_EPS = 1e-6
SSM_G = 64
SSM_P = 64
SSM_H = 16
CHUNK = 16
CW = CHUNK * SSM_H
N_DEV = 8

ADAM_LR = 0.001
ADAM_B1 = 0.9
ADAM_B2 = 0.999
ADAM_EPS = 1e-08
ADAM_WD = 0.01
ADAM_STEP = 10

VMEM_LIMIT = 56 * 1024 * 1024
MESH = pl.DeviceIdType.MESH


def _cp(sem=None):
    if sem is None:
        return pltpu.CompilerParams(vmem_limit_bytes=VMEM_LIMIT)
    return pltpu.CompilerParams(vmem_limit_bytes=VMEM_LIMIT, dimension_semantics=sem)


def _sigmoid(x):
    return 1.0 / (1.0 + jnp.exp(-x))


def _silu(x):
    return x * _sigmoid(x)


def _dsilu(x):
    s = _sigmoid(x)
    return s * (1.0 + x * (1.0 - s))


_GELU_C = math.sqrt(2.0 / math.pi)


def _gelu(y):
    t = jnp.tanh(_GELU_C * (y + 0.044715 * y * y * y))
    return 0.5 * y * (1.0 + t)


def _dgelu(y):
    t = jnp.tanh(_GELU_C * (y + 0.044715 * y * y * y))
    return 0.5 * (1.0 + t) + 0.5 * y * (1.0 - t * t) * _GELU_C * (1.0 + 3.0 * 0.044715 * y * y)


def _tile(n, want):
    if n <= want:
        return n
    for t in range(want - want % 16, 0, -16):
        if n % t == 0:
            return t
    raise ValueError((n, want))


def _mm(a, b, mode, out_dtype, name, tm=512, tn=512, tk=512):
    if mode == "nn":
        (M, K), (K2, N) = a.shape, b.shape
    elif mode == "nt":
        (M, K), (N, K2) = a.shape, b.shape
    else:
        (K, M), (K2, N) = a.shape, b.shape
    assert K == K2
    tm, tn, tk = _tile(M, tm), _tile(N, tn), _tile(K, tk)
    nk = K // tk

    def body(a_ref, b_ref, o_ref, acc_ref):
        k = pl.program_id(2)

        @pl.when(k == 0)
        def _():
            acc_ref[...] = jnp.zeros_like(acc_ref)

        av = a_ref[...].astype(BF16)
        bv = b_ref[...].astype(BF16)
        if mode == "nn":
            dn = (((1,), (0,)), ((), ()))
        elif mode == "nt":
            dn = (((1,), (1,)), ((), ()))
        else:
            dn = (((0,), (0,)), ((), ()))
        acc_ref[...] += lax.dot_general(av, bv, dn, preferred_element_type=F32)

        @pl.when(k == nk - 1)
        def _():
            o_ref[...] = acc_ref[...].astype(o_ref.dtype)

    if mode == "nn":
        a_spec = pl.BlockSpec((tm, tk), lambda i, j, k: (i, k))
        b_spec = pl.BlockSpec((tk, tn), lambda i, j, k: (k, j))
    elif mode == "nt":
        a_spec = pl.BlockSpec((tm, tk), lambda i, j, k: (i, k))
        b_spec = pl.BlockSpec((tn, tk), lambda i, j, k: (j, k))
    else:
        a_spec = pl.BlockSpec((tk, tm), lambda i, j, k: (k, i))
        b_spec = pl.BlockSpec((tk, tn), lambda i, j, k: (k, j))
    return pl.pallas_call(
        body,
        name=name,
        grid=(M // tm, N // tn, nk),
        in_specs=[a_spec, b_spec],
        out_specs=pl.BlockSpec((tm, tn), lambda i, j, k: (i, j)),
        out_shape=jax.ShapeDtypeStruct((M, N), out_dtype),
        scratch_shapes=[pltpu.VMEM((tm, tn), F32)],
        compiler_params=_cp(("parallel", "parallel", "arbitrary")),
    )(a, b)


def _rms_inproj(x, norm_w, wt_in):
    L = x.shape[0]
    tm, tn = _tile(L, 512), 768
    nj = IN_W // tn

    def body(x_ref, w_ref, wt_ref, proj_ref, hn_ref, hn_scr):
        j = pl.program_id(1)

        @pl.when(j == 0)
        def _():
            xv = x_ref[...]
            r = lax.rsqrt(jnp.mean(xv * xv, axis=-1, keepdims=True) + NORM_EPS)
            hn = (xv * r * w_ref[...]).astype(BF16)
            hn_scr[...] = hn
            hn_ref[...] = hn

        proj_ref[...] = lax.dot_general(hn_scr[...], wt_ref[...], (((1,), (1,)), ((), ())),
                                        preferred_element_type=F32)

    return pl.pallas_call(
        body,
        name="rms_inproj",
        grid=(L // tm, nj),
        in_specs=[pl.BlockSpec((tm, D_MODEL), lambda i, j: (i, 0)),
                  pl.BlockSpec((1, D_MODEL), lambda i, j: (0, 0)),
                  pl.BlockSpec((tn, D_MODEL), lambda i, j: (j, 0))],
        out_specs=[pl.BlockSpec((tm, tn), lambda i, j: (i, j)),
                   pl.BlockSpec((tm, D_MODEL), lambda i, j: (i, 0))],
        out_shape=[jax.ShapeDtypeStruct((L, IN_W), F32), jax.ShapeDtypeStruct((L, D_MODEL), BF16)],
        scratch_shapes=[pltpu.VMEM((tm, D_MODEL), BF16)],
        compiler_params=_cp(("parallel", "arbitrary")),
    )(x, norm_w.reshape(1, D_MODEL), wt_in)


def _rot_half(t):
    return jnp.concatenate([t[:, HEAD_DIM // 2:], t[:, :HEAD_DIM // 2]], axis=1)


def _norm_rope(raw, w, tab):
    r = lax.rsqrt(jnp.mean(raw * raw, axis=-1, keepdims=True) + NORM_EPS)
    tn = raw * r * w
    return r, tn * tab[:, :HEAD_DIM] + _rot_half(tn) * tab[:, HEAD_DIM:]


def _norm_rope_bwd(d_rot, raw, r, w, tab):
    d_tn = d_rot * tab[:, :HEAD_DIM] + _rot_half(d_rot * tab[:, HEAD_DIM:])
    xh = raw * r
    gw = d_tn * w
    d_raw = r * (gw - xh * jnp.mean(gw * xh, axis=-1, keepdims=True))
    return d_raw, d_tn * xh


def _band_mask_cat(has_prev):
    qi = lax.broadcasted_iota(jnp.int32, (BLOCK, 2 * BLOCK), 0) + BLOCK
    kj = lax.broadcasted_iota(jnp.int32, (BLOCK, 2 * BLOCK), 1)
    rel = qi - kj
    return (rel >= 0) & (rel < BLOCK) & ((kj >= BLOCK) | has_prev)


def _lane_col(mat, h):
    lane = lax.broadcasted_iota(jnp.int32, mat.shape, 1)
    return jnp.sum(jnp.where(lane == h, mat, 0.0), axis=1, keepdims=True)


_SCALE = 1.0 / math.sqrt(HEAD_DIM)
_NT = (((1,), (1,)), ((), ()))
_NN = (((1,), (0,)), ((), ()))
_TN = (((0,), (0,)), ((), ()))


def _dot(a, b, dn):
    return lax.dot_general(a.astype(BF16), b.astype(BF16), dn, preferred_element_type=F32)


def _attn_fwd(proj, tab, qw, kw, sinks):
    L = proj.shape[0]
    nb = L // BLOCK

    def body(q_ref, kc_ref, kp_ref, vc_ref, vp_ref, z0_ref, z1_ref, tc_ref, tp_ref, qw_ref, kw_ref, sink_ref,
             og_ref, o_ref, lse_ref):
        i = pl.program_id(0)
        mask = _band_mask_cat(i > 0)
        tab_c = tc_ref[...]
        tab_cat = jnp.concatenate([tp_ref[...], tab_c], axis=0)
        z = jnp.concatenate([z0_ref[...], z1_ref[...]], axis=1)
        lane = lax.broadcasted_iota(jnp.int32, (BLOCK, 128), 1)
        lse_mat = jnp.zeros((BLOCK, 128), F32)
        outs = []
        for g in range(N_KV):
            ks = slice(g * HEAD_DIM, (g + 1) * HEAD_DIM)
            k_raw = jnp.concatenate([kp_ref[:, ks], kc_ref[:, ks]], axis=0)
            v_cat = jnp.concatenate([vp_ref[:, ks], vc_ref[:, ks]], axis=0)
            _, kr = _norm_rope(k_raw, kw_ref[...], tab_cat)
            for rr in range(KV_REP):
                h = g * KV_REP + rr
                _, qr = _norm_rope(q_ref[:, h * HEAD_DIM:(h + 1) * HEAD_DIM], qw_ref[...], tab_c)
                s = _dot(qr, kr, _NT) * _SCALE
                s = jnp.where(mask, s, -1e30)
                sink = sink_ref[h]
                m = jnp.maximum(jnp.max(s, axis=-1, keepdims=True), sink)
                e = jnp.exp(s - m)
                den = jnp.sum(e, axis=-1, keepdims=True) + jnp.exp(sink - m)
                p = e / den
                outs.append(_dot(p, v_cat, _NN))
                lse_mat = jnp.where(lane == h, m + jnp.log(den), lse_mat)
        o = jnp.concatenate(outs, axis=1)
        o_ref[...] = o
        og_ref[...] = o * _silu(z)
        lse_ref[...] = lse_mat

    prev = lambda i: jnp.maximum(i - 1, 0)
    return pl.pallas_call(
        body,
        name="attn_fwd",
        grid=(nb,),
        in_specs=[pl.BlockSpec((BLOCK, ATTN_W), lambda i: (i, 0)),
                  pl.BlockSpec((BLOCK, KV_W), lambda i: (i, 4)),
                  pl.BlockSpec((BLOCK, KV_W), lambda i: (prev(i), 4)),
                  pl.BlockSpec((BLOCK, KV_W), lambda i: (i, 5)),
                  pl.BlockSpec((BLOCK, KV_W), lambda i: (prev(i), 5)),
                  pl.BlockSpec((BLOCK, 512), lambda i: (i, 3)),
                  pl.BlockSpec((BLOCK, 512), lambda i: (i, 4)),
                  pl.BlockSpec((BLOCK, 128), lambda i: (i, 0)),
                  pl.BlockSpec((BLOCK, 128), lambda i: (prev(i), 0)),
                  pl.BlockSpec((1, HEAD_DIM), lambda i: (0, 0)),
                  pl.BlockSpec((1, HEAD_DIM), lambda i: (0, 0)),
                  pl.BlockSpec(memory_space=pltpu.SMEM)],
        out_specs=[pl.BlockSpec((BLOCK, ATTN_W), lambda i: (i, 0)),
                   pl.BlockSpec((BLOCK, ATTN_W), lambda i: (i, 0)),
                   pl.BlockSpec((BLOCK, 128), lambda i: (i, 0))],
        out_shape=[jax.ShapeDtypeStruct((L, ATTN_W), F32), jax.ShapeDtypeStruct((L, ATTN_W), F32),
                   jax.ShapeDtypeStruct((L, 128), F32)],
        compiler_params=_cp(("parallel",)),
    )(proj, proj, proj, proj, proj, proj, proj, tab, tab, qw.reshape(1, HEAD_DIM), kw.reshape(1, HEAD_DIM), sinks)


def _attn_bwd(proj, tab, qw, kw, sinks, d_o, o, lse):
    L = proj.shape[0]
    nb = L // BLOCK

    def body(q_ref, qn_ref, kc_ref, kp_ref, vc_ref, vp_ref, do_ref, don_ref, o_ref, on_ref, lse_ref, lsen_ref,
             tc_ref, tp_ref, tn_ref, qw_ref, kw_ref, sink_ref, dqkv_ref, gqw_ref, gkw_ref, gs_ref):
        i = pl.program_id(0)

        @pl.when(i == 0)
        def _():
            gqw_ref[...] = jnp.zeros_like(gqw_ref)
            gkw_ref[...] = jnp.zeros_like(gkw_ref)
            gs_ref[...] = jnp.zeros_like(gs_ref)

        mask = _band_mask_cat(i > 0)
        qi = lax.broadcasted_iota(jnp.int32, (BLOCK, BLOCK), 0)
        kj = lax.broadcasted_iota(jnp.int32, (BLOCK, BLOCK), 1)
        mask_n = (kj > qi) & (i < nb - 1)
        tab_c, tab_n = tc_ref[...], tn_ref[...]
        tab_cat = jnp.concatenate([tp_ref[...], tab_c], axis=0)
        lse_c, lse_n = lse_ref[...], lsen_ref[...]
        lane = lax.broadcasted_iota(jnp.int32, (1, 128), 1)
        gqw = jnp.zeros((1, HEAD_DIM), F32)
        gkw = jnp.zeros((1, HEAD_DIM), F32)
        gs = jnp.zeros((1, 128), F32)
        dq_parts, dk_parts, dv_parts = [], [], []
        for g in range(N_KV):
            ks = slice(g * HEAD_DIM, (g + 1) * HEAD_DIM)
            kc_raw = kc_ref[:, ks]
            k_raw = jnp.concatenate([kp_ref[:, ks], kc_raw], axis=0)
            v_c = vc_ref[:, ks]
            v_cat = jnp.concatenate([vp_ref[:, ks], v_c], axis=0)
            rk, kr = _norm_rope(k_raw, kw_ref[...], tab_cat)
            kr_c = kr[BLOCK:]
            dkr = jnp.zeros((BLOCK, HEAD_DIM), F32)
            dv = jnp.zeros((BLOCK, HEAD_DIM), F32)
            for rr in range(KV_REP):
                h = g * KV_REP + rr
                hs = slice(h * HEAD_DIM, (h + 1) * HEAD_DIM)
                sink = sink_ref[h]
                q_raw = q_ref[:, hs]
                rq, qr = _norm_rope(q_raw, qw_ref[...], tab_c)
                do_h, o_h = do_ref[:, hs], o_ref[:, hs]
                lse_h = _lane_col(lse_c, h)
                delta = jnp.sum(do_h * o_h, axis=-1, keepdims=True)
                s = jnp.where(mask, _dot(qr, kr, _NT) * _SCALE, -1e30)
                p = jnp.exp(s - lse_h)
                dp = _dot(do_h, v_cat, _NT)
                ds = p * (dp - delta) * _SCALE
                dqr = _dot(ds, kr, _NN)
                dq_raw, gq = _norm_rope_bwd(dqr, q_raw, rq, qw_ref[...], tab_c)
                dq_parts.append(dq_raw)
                gqw = gqw + jnp.sum(gq, axis=0, keepdims=True)
                gs = gs + jnp.where(lane == h, jnp.sum(-jnp.exp(sink - lse_h) * delta), 0.0)
                dkr = dkr + _dot(ds[:, BLOCK:], qr, _TN)
                dv = dv + _dot(p[:, BLOCK:], do_h, _TN)
                _, qr_n = _norm_rope(qn_ref[:, hs], qw_ref[...], tab_n)
                don_h = don_ref[:, hs]
                delta_n = jnp.sum(don_h * on_ref[:, hs], axis=-1, keepdims=True)
                s_n = _dot(qr_n, kr_c, _NT) * _SCALE
                p_n = jnp.where(mask_n, jnp.exp(jnp.where(mask_n, s_n, -1e30) - _lane_col(lse_n, h)), 0.0)
                ds_n = p_n * (_dot(don_h, v_c, _NT) - delta_n) * _SCALE
                dkr = dkr + _dot(ds_n, qr_n, _TN)
                dv = dv + _dot(p_n, don_h, _TN)
            dk_raw, gk = _norm_rope_bwd(dkr, kc_raw, rk[BLOCK:], kw_ref[...], tab_c)
            gkw = gkw + jnp.sum(gk, axis=0, keepdims=True)
            dk_parts.append(dk_raw)
            dv_parts.append(dv)
        dqkv_ref[...] = jnp.concatenate(dq_parts + dk_parts + dv_parts, axis=1).astype(BF16)
        gqw_ref[...] += gqw
        gkw_ref[...] += gkw
        gs_ref[...] += gs

    prev = lambda i: jnp.maximum(i - 1, 0)
    nxt = lambda i: jnp.minimum(i + 1, nb - 1)
    bs = pl.BlockSpec
    return pl.pallas_call(
        body,
        name="attn_bwd",
        grid=(nb,),
        in_specs=[bs((BLOCK, ATTN_W), lambda i: (i, 0)), bs((BLOCK, ATTN_W), lambda i: (nxt(i), 0)),
                  bs((BLOCK, KV_W), lambda i: (i, 4)), bs((BLOCK, KV_W), lambda i: (prev(i), 4)),
                  bs((BLOCK, KV_W), lambda i: (i, 5)), bs((BLOCK, KV_W), lambda i: (prev(i), 5)),
                  bs((BLOCK, ATTN_W), lambda i: (i, 0)), bs((BLOCK, ATTN_W), lambda i: (nxt(i), 0)),
                  bs((BLOCK, ATTN_W), lambda i: (i, 0)), bs((BLOCK, ATTN_W), lambda i: (nxt(i), 0)),
                  bs((BLOCK, 128), lambda i: (i, 0)), bs((BLOCK, 128), lambda i: (nxt(i), 0)),
                  bs((BLOCK, 128), lambda i: (i, 0)), bs((BLOCK, 128), lambda i: (prev(i), 0)),
                  bs((BLOCK, 128), lambda i: (nxt(i), 0)),
                  bs((1, HEAD_DIM), lambda i: (0, 0)), bs((1, HEAD_DIM), lambda i: (0, 0)),
                  bs(memory_space=pltpu.SMEM)],
        out_specs=[bs((BLOCK, ATTN_W + 2 * KV_W), lambda i: (i, 0)),
                   bs((1, HEAD_DIM), lambda i: (0, 0)), bs((1, HEAD_DIM), lambda i: (0, 0)),
                   bs((1, 128), lambda i: (0, 0))],
        out_shape=[jax.ShapeDtypeStruct((L, ATTN_W + 2 * KV_W), BF16),
                   jax.ShapeDtypeStruct((1, HEAD_DIM), F32), jax.ShapeDtypeStruct((1, HEAD_DIM), F32),
                   jax.ShapeDtypeStruct((1, 128), F32)],
        compiler_params=_cp(("arbitrary",)),
    )(proj, proj, proj, proj, proj, proj, d_o, d_o, o, o, lse, lse, tab, tab, tab,
      qw.reshape(1, HEAD_DIM), kw.reshape(1, HEAD_DIM), sinks)


def _ssm_ops(a_re, a_im, log_step, b_re, b_im, c_re, c_im):
    hp = lax.Precision.HIGHEST
    delta = jnp.exp(log_step)[:, None]
    xr, xi = a_re * delta, a_im * delta
    er = jnp.exp(xr)
    lbr, lbi = er * jnp.cos(xi), er * jnp.sin(xi)
    nr, ni = lbr - 1.0, lbi
    den = a_re * a_re + a_im * a_im
    cr, ci = (nr * a_re + ni * a_im) / den, (ni * a_re - nr * a_im) / den
    bbr = cr[..., None] * b_re - ci[..., None] * b_im
    bbi = cr[..., None] * b_im + ci[..., None] * b_re
    pr, pi = [jnp.ones_like(lbr)], [jnp.zeros_like(lbr)]
    for _ in range(CHUNK):
        pr.append(pr[-1] * lbr - pi[-1] * lbi)
        pi.append(pr[-2] * lbi + pi[-1] * lbr)
    pr, pi = jnp.stack(pr), jnp.stack(pi)
    clr = c_re[None] * pr[:CHUNK, :, None, :] - c_im[None] * pi[:CHUNK, :, None, :]
    cli = c_re[None] * pi[:CHUNK, :, None, :] + c_im[None] * pr[:CHUNK, :, None, :]
    kk = (jnp.einsum("lghp,gpk->lghk", clr, bbr, precision=hp)
          - jnp.einsum("lghp,gpk->lghk", cli, bbi, precision=hp))
    kz = jnp.concatenate([kk, jnp.zeros_like(kk[:1])], axis=0)
    s_i = jnp.arange(CHUNK)[:, None]
    t_i = jnp.arange(CHUNK)[None, :]
    idx = jnp.where(t_i >= s_i, t_i - s_i, CHUNK)
    mt = kz[idx]
    mt = mt.transpose(2, 0, 4, 1, 3).reshape(SSM_G, CW, CW)
    bpr = pr[:CHUNK, :, :, None] * bbr[None] - pi[:CHUNK, :, :, None] * bbi[None]
    bpi = pr[:CHUNK, :, :, None] * bbi[None] + pi[:CHUNK, :, :, None] * bbr[None]
    sr = bpr[::-1].transpose(1, 0, 3, 2).reshape(SSM_G, CW, SSM_P)
    si = bpi[::-1].transpose(1, 0, 3, 2).reshape(SSM_G, CW, SSM_P)
    scat = jnp.concatenate([sr, si], axis=-1)
    ctr, cti = c_re.transpose(0, 2, 1), c_im.transpose(0, 2, 1)
    p1r, p1i = pr[1:].transpose(1, 2, 0), pi[1:].transpose(1, 2, 0)
    o_r = (ctr[:, :, None, :] * p1r[..., None] - cti[:, :, None, :] * p1i[..., None]).reshape(SSM_G, SSM_P, CW)
    o_i = (ctr[:, :, None, :] * p1i[..., None] + cti[:, :, None, :] * p1r[..., None]).reshape(SSM_G, SSM_P, CW)
    ocat = jnp.concatenate([o_r, -o_i], axis=1)
    a16 = jnp.concatenate([pr[CHUNK], pi[CHUNK]], axis=-1)[:, None, :]
    return mt, scat, ocat, a16


def _cmul_const(xv, ar, ai):
    return xv * ar + pltpu.roll(xv, SSM_P, 1) * ai


def _chunk_scan(inc, a_row, reverse):
    n = inc.shape[0]
    lane = lax.broadcasted_iota(jnp.int32, (1, 2 * SSM_P), 1)
    row = lax.broadcasted_iota(jnp.int32, inc.shape, 0)
    sign = jnp.where(lane < SSM_P, -1.0, 1.0)
    ar = jnp.where(lane < SSM_P, a_row, pltpu.roll(a_row, SSM_P, 1))
    ai = jnp.where(lane < SSM_P, pltpu.roll(a_row, SSM_P, 1), a_row)
    if reverse:
        ai = -ai
    xv = inc
    s = 1
    while s < n:
        if reverse:
            sh = jnp.where(row < n - s, pltpu.roll(xv, n - s, 0), 0.0)
        else:
            sh = jnp.where(row >= s, pltpu.roll(xv, s, 0), 0.0)
        xv = xv + _cmul_const(sh, ar, ai * sign)
        ar, ai = ar * ar - ai * ai, 2.0 * ar * ai
        s *= 2
    return xv


def _shift_rows(xv, reverse):
    n = xv.shape[0]
    row = lax.broadcasted_iota(jnp.int32, xv.shape, 0)
    if reverse:
        return jnp.where(row < n - 1, pltpu.roll(xv, n - 1, 0), 0.0)
    return jnp.where(row >= 1, pltpu.roll(xv, 1, 0), 0.0)


def _ssm_fwd(ua, mt, scat, ocat, a16):
    nc = ua.shape[1]

    def body(u_ref, mt_ref, s_ref, o_ref, a_ref, y_ref, h_ref):
        uv = u_ref[0]
        inc = jnp.dot(uv, s_ref[0], preferred_element_type=F32)
        hx = _shift_rows(_chunk_scan(inc, a_ref[0], False), False)
        h_ref[0] = hx
        y_ref[0] = (jnp.dot(uv, mt_ref[0], preferred_element_type=F32)
                    + jnp.dot(hx.astype(BF16), o_ref[0], preferred_element_type=F32))

    g3 = lambda r, c: pl.BlockSpec((1, r, c), lambda g: (g, 0, 0))
    return pl.pallas_call(
        body,
        name="ssm_fwd",
        grid=(SSM_G,),
        in_specs=[g3(nc, CW), g3(CW, CW), g3(CW, 2 * SSM_P), g3(2 * SSM_P, CW), g3(1, 2 * SSM_P)],
        out_specs=[g3(nc, CW), g3(nc, 2 * SSM_P)],
        out_shape=[jax.ShapeDtypeStruct((SSM_G, nc, CW), F32), jax.ShapeDtypeStruct((SSM_G, nc, 2 * SSM_P), F32)],
        compiler_params=_cp(("parallel",)),
    )(ua, mt, scat, ocat, a16)


def _ssm_bwd(dya, ua, hx, mt, scat, ocat, a16):
    nc = ua.shape[1]

    def body(dy_ref, u_ref, h_ref, mt_ref, s_ref, o_ref, a_ref, du_ref, gmt_ref, gs_ref, go_ref, ga_ref):
        dy, uv, hx_v = dy_ref[0], u_ref[0], h_ref[0]
        dh = lax.dot_general(dy, o_ref[0], _NT, preferred_element_type=F32)
        dinc = _shift_rows(_chunk_scan(dh, a_ref[0], True), True)
        dinc_b = dinc.astype(BF16)
        du_ref[0] = (lax.dot_general(dy, mt_ref[0], _NT, preferred_element_type=F32)
                     + lax.dot_general(dinc_b, s_ref[0], _NT, preferred_element_type=F32))
        gmt_ref[0] = lax.dot_general(uv, dy, _TN, preferred_element_type=F32)
        gs_ref[0] = lax.dot_general(uv, dinc_b, _TN, preferred_element_type=F32)
        go_ref[0] = lax.dot_general(hx_v.astype(BF16), dy, _TN, preferred_element_type=F32)
        lane = lax.broadcasted_iota(jnp.int32, (1, 2 * SSM_P), 1)
        p1 = dinc * hx_v
        p2 = pltpu.roll(dinc, SSM_P, 1) * hx_v
        t1 = jnp.sum(p1 + pltpu.roll(p1, SSM_P, 1), axis=0, keepdims=True)
        t2 = jnp.sum(p2 - pltpu.roll(p2, SSM_P, 1), axis=0, keepdims=True)
        ga_ref[0] = jnp.where(lane < SSM_P, t1, pltpu.roll(t2, SSM_P, 1))

    g3 = lambda r, c: pl.BlockSpec((1, r, c), lambda g: (g, 0, 0))
    return pl.pallas_call(
        body,
        name="ssm_bwd",
        grid=(SSM_G,),
        in_specs=[g3(nc, CW), g3(nc, CW), g3(nc, 2 * SSM_P), g3(CW, CW), g3(CW, 2 * SSM_P), g3(2 * SSM_P, CW),
                  g3(1, 2 * SSM_P)],
        out_specs=[g3(nc, CW), g3(CW, CW), g3(CW, 2 * SSM_P), g3(2 * SSM_P, CW), g3(1, 2 * SSM_P)],
        out_shape=[jax.ShapeDtypeStruct((SSM_G, nc, CW), F32), jax.ShapeDtypeStruct((SSM_G, CW, CW), F32),
                   jax.ShapeDtypeStruct((SSM_G, CW, 2 * SSM_P), F32),
                   jax.ShapeDtypeStruct((SSM_G, 2 * SSM_P, CW), F32),
                   jax.ShapeDtypeStruct((SSM_G, 1, 2 * SSM_P), F32)],
        compiler_params=_cp(("parallel",)),
    )(dya, ua, hx, mt, scat, ocat, a16)


def _to_chunks(t):
    L = t.shape[0]
    return t.reshape(L // CHUNK, CHUNK, SSM_G, SSM_H).transpose(2, 0, 1, 3).reshape(SSM_G, L // CHUNK, CW)


def _from_chunks(t):
    nc = t.shape[1]
    return t.reshape(SSM_G, nc, CHUNK, SSM_H).transpose(1, 2, 0, 3).reshape(nc * CHUNK, SSM_W)


def _ssm_mid(ys, proj, d_skip):
    L = ys.shape[0]
    tm = _tile(L, 512)

    def body(ys_ref, u_ref, d_ref, yg_ref, ygb_ref):
        yg = _gelu(ys_ref[...] + d_ref[...] * u_ref[...])
        yg_ref[...] = yg
        ygb_ref[...] = yg.astype(BF16)

    return pl.pallas_call(
        body,
        name="ssm_mid",
        grid=(L // tm, 2),
        in_specs=[pl.BlockSpec((tm, 512), lambda i, j: (i, j)),
                  pl.BlockSpec((tm, 512), lambda i, j: (i, 5 + j)),
                  pl.BlockSpec((1, 512), lambda i, j: (0, j))],
        out_specs=[pl.BlockSpec((tm, 512), lambda i, j: (i, j))] * 2,
        out_shape=[jax.ShapeDtypeStruct((L, SSM_W), F32), jax.ShapeDtypeStruct((L, SSM_W), BF16)],
        compiler_params=_cp(("parallel", "parallel")),
    )(ys, proj, d_skip.reshape(1, SSM_W))


def _merge(og, yg, gpre, proj, b_glu, wa, ws):
    L = og.shape[0]
    tm = _tile(L, 256)

    def body(og_ref, yg_ref, gp_ref, z0_ref, z1_ref, b_ref, wa_ref, ws_ref, m_ref):
        zs = jnp.concatenate([z0_ref[...], z1_ref[...]], axis=1)
        os_ = yg_ref[...] * _sigmoid(gp_ref[...] + b_ref[...]) * _silu(zs)
        ogv = og_ref[...]
        ra = lax.rsqrt(jnp.mean(ogv * ogv, axis=-1, keepdims=True) + NORM_EPS)
        rs = lax.rsqrt(jnp.mean(os_ * os_, axis=-1, keepdims=True) + NORM_EPS)
        m_ref[:, :ATTN_W] = (ogv * ra * wa_ref[...]).astype(BF16)
        m_ref[:, ATTN_W:] = (os_ * rs * ws_ref[...]).astype(BF16)

    row = lambda w: pl.BlockSpec((1, w), lambda i: (0, 0))
    return pl.pallas_call(
        body,
        name="merge",
        grid=(L // tm,),
        in_specs=[pl.BlockSpec((tm, ATTN_W), lambda i: (i, 0)), pl.BlockSpec((tm, SSM_W), lambda i: (i, 0)),
                  pl.BlockSpec((tm, SSM_W), lambda i: (i, 0)),
                  pl.BlockSpec((tm, 512), lambda i: (i, 7)), pl.BlockSpec((tm, 512), lambda i: (i, 8)),
                  row(SSM_W), row(ATTN_W), row(SSM_W)],
        out_specs=pl.BlockSpec((tm, D_MODEL), lambda i: (i, 0)),
        out_shape=jax.ShapeDtypeStruct((L, D_MODEL), BF16),
        compiler_params=_cp(("parallel",)),
    )(og, yg, gpre, proj, proj, b_glu.reshape(1, SSM_W), wa.reshape(1, ATTN_W), ws.reshape(1, SSM_W))


def _outproj_loss(merged, w_out, x, target):
    L = x.shape[0]
    tm, tn = _tile(L, 512), 512
    ni, nj = L // tm, D_MODEL // tn

    def body(m_ref, w_ref, x_ref, t_ref, d_ref, db_ref, l_ref):
        out = x_ref[...] + jnp.dot(m_ref[...], w_ref[...], preferred_element_type=F32)
        diff = out - t_ref[...]
        d = diff * (1.0 / D_MODEL)
        d_ref[...] = d
        db_ref[...] = d.astype(BF16)
        l_ref[...] = jnp.full((1, 8, 128), jnp.sum(diff * diff), F32)

    return pl.pallas_call(
        body,
        name="outproj_loss",
        grid=(ni, nj),
        in_specs=[pl.BlockSpec((tm, D_MODEL), lambda i, j: (i, 0)),
                  pl.BlockSpec((D_MODEL, tn), lambda i, j: (0, j)),
                  pl.BlockSpec((tm, tn), lambda i, j: (i, j)),
                  pl.BlockSpec((tm, tn), lambda i, j: (i, j))],
        out_specs=[pl.BlockSpec((tm, tn), lambda i, j: (i, j)), pl.BlockSpec((tm, tn), lambda i, j: (i, j)),
                   pl.BlockSpec((1, 8, 128), lambda i, j: (i * nj + j, 0, 0))],
        out_shape=[jax.ShapeDtypeStruct((L, D_MODEL), F32), jax.ShapeDtypeStruct((L, D_MODEL), BF16),
                   jax.ShapeDtypeStruct((ni * nj, 8, 128), F32)],
        compiler_params=_cp(("parallel", "parallel")),
    )(merged, w_out, x, target)


def _merge_bwd(d_m, og, o, yg, gpre, proj, b_glu, wa, ws):
    L = og.shape[0]
    tm = _tile(L, 256)

    def body(dm_ref, og_ref, o_ref, yg_ref, gp_ref, za0_ref, za1_ref, zs0_ref, zs1_ref, b_ref, wa_ref, ws_ref,
             do_ref, dza_ref, dzs_ref, dg_ref, dyg_ref, gwa_ref, gws_ref, gb_ref):
        i = pl.program_id(0)

        @pl.when(i == 0)
        def _():
            gwa_ref[...] = jnp.zeros_like(gwa_ref)
            gws_ref[...] = jnp.zeros_like(gws_ref)
            gb_ref[...] = jnp.zeros_like(gb_ref)

        za = jnp.concatenate([za0_ref[...], za1_ref[...]], axis=1)
        zs = jnp.concatenate([zs0_ref[...], zs1_ref[...]], axis=1)
        ogv, dma = og_ref[...], dm_ref[:, :ATTN_W]
        ra = lax.rsqrt(jnp.mean(ogv * ogv, axis=-1, keepdims=True) + NORM_EPS)
        xh = ogv * ra
        gwa_ref[...] += jnp.sum(dma * xh, axis=0, keepdims=True)
        gx = dma * wa_ref[...]
        d_og = ra * (gx - xh * jnp.mean(gx * xh, axis=-1, keepdims=True))
        do_ref[...] = d_og * _silu(za)
        dza_ref[...] = (d_og * o_ref[...] * _dsilu(za)).astype(BF16)
        ygv = yg_ref[...]
        sg = _sigmoid(gp_ref[...] + b_ref[...])
        y2 = ygv * sg
        sz = _silu(zs)
        os_ = y2 * sz
        dms = dm_ref[:, ATTN_W:]
        rs = lax.rsqrt(jnp.mean(os_ * os_, axis=-1, keepdims=True) + NORM_EPS)
        xs = os_ * rs
        gws_ref[...] += jnp.sum(dms * xs, axis=0, keepdims=True)
        gxs = dms * ws_ref[...]
        d_os = rs * (gxs - xs * jnp.mean(gxs * xs, axis=-1, keepdims=True))
        dzs_ref[...] = (d_os * y2 * _dsilu(zs)).astype(BF16)
        d_y2 = d_os * sz
        d_g = d_y2 * ygv * sg * (1.0 - sg)
        dg_ref[...] = d_g.astype(BF16)
        gb_ref[...] += jnp.sum(d_g, axis=0, keepdims=True)
        dyg_ref[...] = d_y2 * sg

    row = lambda w: pl.BlockSpec((1, w), lambda i: (0, 0))
    full = lambda w: pl.BlockSpec((tm, w), lambda i: (i, 0))
    half = lambda c: pl.BlockSpec((tm, 512), lambda i: (i, c))
    return pl.pallas_call(
        body,
        name="merge_bwd",
        grid=(L // tm,),
        in_specs=[full(D_MODEL), full(ATTN_W), full(ATTN_W), full(SSM_W), full(SSM_W),
                  half(3), half(4), half(7), half(8), row(SSM_W), row(ATTN_W), row(SSM_W)],
        out_specs=[full(ATTN_W), full(ATTN_W), full(SSM_W), full(SSM_W), full(SSM_W),
                   row(ATTN_W), row(SSM_W), row(SSM_W)],
        out_shape=[jax.ShapeDtypeStruct((L, ATTN_W), F32), jax.ShapeDtypeStruct((L, ATTN_W), BF16),
                   jax.ShapeDtypeStruct((L, SSM_W), BF16), jax.ShapeDtypeStruct((L, SSM_W), BF16),
                   jax.ShapeDtypeStruct((L, SSM_W), F32),
                   jax.ShapeDtypeStruct((1, ATTN_W), F32), jax.ShapeDtypeStruct((1, SSM_W), F32),
                   jax.ShapeDtypeStruct((1, SSM_W), F32)],
        compiler_params=_cp(("arbitrary",)),
    )(d_m, og, o, yg, gpre, proj, proj, proj, proj, b_glu.reshape(1, SSM_W), wa.reshape(1, ATTN_W),
      ws.reshape(1, SSM_W))


def _gelu_bwd(d_yg1, d_yg2, ys, proj, d_skip):
    L = ys.shape[0]
    tm = _tile(L, 512)

    def body(a_ref, b_ref, ys_ref, u_ref, d_ref, dy_ref, du_ref, gd_ref):
        i = pl.program_id(1)

        @pl.when(i == 0)
        def _():
            gd_ref[...] = jnp.zeros_like(gd_ref)

        uv = u_ref[...]
        dy = (a_ref[...] + b_ref[...]) * _dgelu(ys_ref[...] + d_ref[...] * uv)
        dy_ref[...] = dy.astype(BF16)
        du_ref[...] = d_ref[...] * dy
        gd_ref[...] += jnp.sum(dy * uv, axis=0, keepdims=True)

    blk = pl.BlockSpec((tm, 512), lambda j, i: (i, j))
    return pl.pallas_call(
        body,
        name="gelu_bwd",
        grid=(2, L // tm),
        in_specs=[blk, blk, blk, pl.BlockSpec((tm, 512), lambda j, i: (i, 5 + j)),
                  pl.BlockSpec((1, 512), lambda j, i: (0, j))],
        out_specs=[blk, blk, pl.BlockSpec((1, 512), lambda j, i: (0, j))],
        out_shape=[jax.ShapeDtypeStruct((L, SSM_W), BF16), jax.ShapeDtypeStruct((L, SSM_W), F32),
                   jax.ShapeDtypeStruct((1, SSM_W), F32)],
        compiler_params=_cp(("parallel", "arbitrary")),
    )(d_yg1, d_yg2, ys, proj, d_skip.reshape(1, SSM_W))


def _add_cast(a, b):
    L, W = a.shape
    tm = _tile(L, 512)

    def body(a_ref, b_ref, o_ref):
        o_ref[...] = (a_ref[...] + b_ref[...]).astype(BF16)

    blk = pl.BlockSpec((tm, W), lambda i: (i, 0))
    return pl.pallas_call(
        body, name="add_cast", grid=(L // tm,), in_specs=[blk, blk], out_specs=blk,
        out_shape=jax.ShapeDtypeStruct((L, W), BF16), compiler_params=_cp(("parallel",)),
    )(a, b)


def _rms_bwd_x(x, norm_w, d_hn, d_out):
    L = x.shape[0]
    tm = _tile(L, 256)

    def body(x_ref, w_ref, dh_ref, do_ref, gx_ref, gw_ref):
        i = pl.program_id(0)

        @pl.when(i == 0)
        def _():
            gw_ref[...] = jnp.zeros_like(gw_ref)

        xv, dh = x_ref[...], dh_ref[...]
        r = lax.rsqrt(jnp.mean(xv * xv, axis=-1, keepdims=True) + NORM_EPS)
        xh = xv * r
        gw_ref[...] += jnp.sum(dh * xh, axis=0, keepdims=True)
        gx = dh * w_ref[...]
        gx_ref[...] = do_ref[...] + r * (gx - xh * jnp.mean(gx * xh, axis=-1, keepdims=True))

    blk = pl.BlockSpec((tm, D_MODEL), lambda i: (i, 0))
    row = pl.BlockSpec((1, D_MODEL), lambda i: (0, 0))
    return pl.pallas_call(
        body, name="rms_bwd_x", grid=(L // tm,), in_specs=[blk, row, blk, blk], out_specs=[blk, row],
        out_shape=[jax.ShapeDtypeStruct((L, D_MODEL), F32), jax.ShapeDtypeStruct((1, D_MODEL), F32)],
        compiler_params=_cp(("arbitrary",)),
    )(x, norm_w.reshape(1, D_MODEL), d_hn, d_out)


def _rope_table(positions):
    inv_freq = ROPE_THETA ** (-jnp.arange(0, HEAD_DIM, 2, dtype=F32) / HEAD_DIM)
    ang = positions.astype(F32)[:, None] * inv_freq
    c, s = jnp.cos(ang), jnp.sin(ang)
    return jnp.concatenate([c, c, -s, s], axis=1)


def _local_step(x, positions, target, small, wt_in, w_glu, w_out):
    tab = _rope_table(positions)
    ssm_names = ("a_re", "a_im", "log_step", "b_re", "b_im", "c_re", "c_im")
    ops, ops_vjp = jax.vjp(_ssm_ops, *[small[n] for n in ssm_names])
    mt, scat, ocat, a16 = ops
    mt_b, scat_b, ocat_b = mt.astype(BF16), scat.astype(BF16), ocat.astype(BF16)

    proj, hn = _rms_inproj(x, small["norm_w"], wt_in)
    og, o, lse = _attn_fwd(proj, tab, small["q_norm_w"], small["k_norm_w"], small["sinks"])
    ua = _to_chunks(proj[:, 2560:3584].astype(BF16))
    ya, hx = _ssm_fwd(ua, mt_b, scat_b, ocat_b, a16)
    ys = _from_chunks(ya)
    yg, yg_b = _ssm_mid(ys, proj, small["d_skip"])
    gpre = _mm(yg_b, w_glu, "nn", F32, "glu_fwd")
    merged = _merge(og, yg, gpre, proj, small["b_glu"], small["attn_out_norm_w"], small["ssm_out_norm_w"])
    d_out, d_out_b, loss_parts = _outproj_loss(merged, w_out, x, target)
    loss = 0.5 * jnp.sum(loss_parts[:, 0, 0]) / D_MODEL

    g_w_out = _mm(merged, d_out_b, "tn", F32, "grad_w_out")
    d_m = _mm(d_out_b, w_out, "nt", F32, "d_merged")
    d_o, d_za, d_zs, d_g, d_yg1, g_wa, g_ws, g_bglu = _merge_bwd(
        d_m, og, o, yg, gpre, proj, small["b_glu"], small["attn_out_norm_w"], small["ssm_out_norm_w"])
    g_w_glu = _mm(yg_b, d_g, "tn", F32, "grad_w_glu")
    d_yg2 = _mm(d_g, w_glu, "nt", F32, "d_yg")
    d_y, d_u1, g_dskip = _gelu_bwd(d_yg1, d_yg2, ys, proj, small["d_skip"])
    dua, g_mt, g_scat, g_ocat, g_a16 = _ssm_bwd(_to_chunks(d_y), ua, hx, mt_b, scat_b, ocat_b, a16)
    g_ssm = ops_vjp((g_mt, g_scat, g_ocat, g_a16))
    d_u = _add_cast(d_u1, _from_chunks(dua))
    d_qkv, g_qw, g_kw, g_sinks = _attn_bwd(proj, tab, small["q_norm_w"], small["k_norm_w"], small["sinks"],
                                           d_o, o, lse)
    d_proj = jnp.concatenate([d_qkv, d_za, d_u, d_zs], axis=1)
    g_wt_in = _mm(d_proj, hn, "tn", F32, "grad_w_in")
    d_hn = _mm(d_proj, wt_in, "nn", F32, "d_hn")
    grad_x, g_nw = _rms_bwd_x(x, small["norm_w"], d_hn, d_out)

    g_small = dict(zip(ssm_names, g_ssm))
    g_small.update(norm_w=g_nw.reshape(-1), q_norm_w=g_qw.reshape(-1), k_norm_w=g_kw.reshape(-1),
                   sinks=g_sinks[0, :N_HEADS], d_skip=g_dskip.reshape(-1), b_glu=g_bglu.reshape(-1),
                   attn_out_norm_w=g_wa.reshape(-1), ssm_out_norm_w=g_ws.reshape(-1))
    return loss, grad_x, g_wt_in, g_w_glu, g_w_out, g_small


_ANY = pl.BlockSpec(memory_space=pl.ANY)


def _all_gather_rows(blocks, name):
    n = len(blocks)

    def body(*refs):
        ins, outs = refs[:n], refs[n:2 * n]
        send_sems, recv_sems, local_sems = refs[2 * n:]
        x, y, c = lax.axis_index("x"), lax.axis_index("y"), lax.axis_index("c")
        me, sibling = (x, y, c), (x, y, 1 - c)
        chips = [(1 - x, y), (x, 1 - y), (1 - x, 1 - y)]

        def slot(k, dev):
            return outs[k].at[4 * dev[0] + 2 * dev[1] + dev[2]]

        def copy(k, q, block, to, src=None):
            return pltpu.make_async_remote_copy(
                src_ref=slot(k, block) if src is None else src, dst_ref=slot(k, block),
                send_sem=send_sems.at[k, q], recv_sem=recv_sems.at[k, q], device_id=to, device_id_type=MESH)

        mine = [pltpu.make_async_copy(ins[k], slot(k, me), local_sems.at[k]) for k in range(n)]
        for cp in mine:
            cp.start()
        first = []
        for k in range(n):
            first.append(copy(k, 0, me, sibling, src=ins[k]))
            first += [copy(k, 1 + j, me, (*chip, c), src=ins[k]) for j, chip in enumerate(chips)]
        for cp in first:
            cp.start()
        passed = []
        for j, chip in enumerate(chips):
            for k in range(n):
                copy(k, 1 + j, (*chip, c), me).wait_recv()
                fwd = copy(k, 4 + j, (*chip, c), sibling)
                fwd.start()
                passed.append(fwd)
        for k in range(n):
            copy(k, 0, sibling, me).wait_recv()
            for j, chip in enumerate(chips):
                copy(k, 4 + j, (*chip, 1 - c), me).wait_recv()
        for cp in first + passed:
            cp.wait_send()
        for cp in mine:
            cp.wait()

    outs = pl.pallas_call(
        body,
        name=name,
        in_specs=[_ANY] * n,
        out_specs=[_ANY] * n,
        out_shape=[jax.ShapeDtypeStruct((N_DEV,) + b.shape, b.dtype) for b in blocks],
        scratch_shapes=[pltpu.SemaphoreType.DMA((n, 7)), pltpu.SemaphoreType.DMA((n, 7)),
                        pltpu.SemaphoreType.DMA((n,))],
    )(*blocks)
    return list(outs)


def _pair_exchange(grads, name):
    n = len(grads)

    def body(*refs):
        ins, outs = refs[:n], refs[n:2 * n]
        send_sems, recv_sems = refs[2 * n:]
        x, y, c = lax.axis_index("x"), lax.axis_index("y"), lax.axis_index("c")
        copies = []
        for k in range(n):
            for chip in range(4):
                copies.append(pltpu.make_async_remote_copy(
                    src_ref=ins[k].at[2 * chip + (1 - c)], dst_ref=outs[k].at[chip],
                    send_sem=send_sems.at[k, chip], recv_sem=recv_sems.at[k, chip],
                    device_id=(x, y, 1 - c), device_id_type=MESH))
        for cp in copies:
            cp.start()
        for cp in copies:
            cp.wait()

    outs = pl.pallas_call(
        body,
        name=name,
        in_specs=[_ANY] * n,
        out_specs=[_ANY] * n,
        out_shape=[jax.ShapeDtypeStruct((4,) + g.shape[1:], g.dtype) for g in grads],
        scratch_shapes=[pltpu.SemaphoreType.DMA((n, 4)), pltpu.SemaphoreType.DMA((n, 4))],
    )(*grads)
    return list(outs)


def _pair_sum(g, ra, core, out_dtype, name):
    _, r, C = g.shape
    tr = _tile(r, 128)

    def body(c_ref, g_ref, ra_ref, p_ref):
        p_ref[...] = (g_ref[...] + ra_ref[...]).astype(p_ref.dtype)

    return pl.pallas_call(
        body,
        name=name,
        grid_spec=pltpu.PrefetchScalarGridSpec(
            num_scalar_prefetch=1,
            grid=(4, r // tr),
            in_specs=[pl.BlockSpec((1, tr, C), lambda j, t, c_ref: (2 * j + c_ref[0], t, 0)),
                      pl.BlockSpec((1, tr, C), lambda j, t, c_ref: (j, t, 0))],
            out_specs=pl.BlockSpec((1, tr, C), lambda j, t, c_ref: (j, t, 0)),
        ),
        out_shape=jax.ShapeDtypeStruct((4, r, C), out_dtype),
        compiler_params=_cp(("parallel", "parallel")),
    )(core, g, ra)


def _chip_exchange(parts, name):
    n = len(parts)

    def body(*refs):
        ins, outs = refs[:n], refs[n:2 * n]
        send_sems, recv_sems = refs[2 * n:]
        x, y, c = lax.axis_index("x"), lax.axis_index("y"), lax.axis_index("c")
        chips = [(1 - x, y), (x, 1 - y), (1 - x, 1 - y)]
        copies = []
        for k in range(n):
            for q, chip in enumerate(chips):
                copies.append(pltpu.make_async_remote_copy(
                    src_ref=ins[k].at[2 * chip[0] + chip[1]], dst_ref=outs[k].at[q],
                    send_sem=send_sems.at[k, q], recv_sem=recv_sems.at[k, q],
                    device_id=(*chip, c), device_id_type=MESH))
        for cp in copies:
            cp.start()
        for cp in copies:
            cp.wait()

    outs = pl.pallas_call(
        body,
        name=name,
        in_specs=[_ANY] * n,
        out_specs=[_ANY] * n,
        out_shape=[jax.ShapeDtypeStruct((3,) + p.shape[1:], p.dtype) for p in parts],
        scratch_shapes=[pltpu.SemaphoreType.DMA((n, 3)), pltpu.SemaphoreType.DMA((n, 3))],
    )(*parts)
    return list(outs)


def _chip_sum(p, rb, chip, name):
    _, r, C = p.shape
    tr = _tile(r, 128)

    def body(c_ref, p_ref, rb_ref, o_ref):
        acc = p_ref[0].astype(F32) + rb_ref[0].astype(F32)
        acc = acc + rb_ref[1].astype(F32)
        o_ref[...] = acc + rb_ref[2].astype(F32)

    return pl.pallas_call(
        body,
        name=name,
        grid_spec=pltpu.PrefetchScalarGridSpec(
            num_scalar_prefetch=1,
            grid=(r // tr,),
            in_specs=[pl.BlockSpec((1, tr, C), lambda t, c_ref: (c_ref[0], t, 0)),
                      pl.BlockSpec((3, tr, C), lambda t, c_ref: (0, t, 0))],
            out_specs=pl.BlockSpec((tr, C), lambda t, c_ref: (t, 0)),
        ),
        out_shape=jax.ShapeDtypeStruct((r, C), F32),
        compiler_params=_cp(("parallel",)),
    )(chip, p, rb)


def _adamw(g, w, m, v, name):
    R, C = g.shape
    tr = _tile(R, 256)
    c1 = 1.0 - ADAM_B1 ** ADAM_STEP
    c2 = 1.0 - ADAM_B2 ** ADAM_STEP

    def body(g_ref, w_ref, m_ref, v_ref, d_ref, nm_ref, nv_ref):
        gv = g_ref[...]
        nm = ADAM_B1 * m_ref[...] + (1.0 - ADAM_B1) * gv
        nv = ADAM_B2 * v_ref[...] + (1.0 - ADAM_B2) * (gv * gv)
        nm_ref[...] = nm
        nv_ref[...] = nv
        d_ref[...] = -ADAM_LR * ((nm / c1) / (jnp.sqrt(nv / c2) + ADAM_EPS) + ADAM_WD * w_ref[...])

    blk = pl.BlockSpec((tr, C), lambda i: (i, 0))
    return pl.pallas_call(
        body, name=name, grid=(R // tr,), in_specs=[blk] * 4, out_specs=[blk] * 3,
        out_shape=[jax.ShapeDtypeStruct((R, C), F32)] * 3, compiler_params=_cp(("parallel",)),
    )(g, w, m, v)


_SMALL = ("norm_w", "q_norm_w", "k_norm_w", "sinks", "a_re", "a_im", "log_step", "b_re", "b_im", "c_re", "c_im",
          "d_skip", "b_glu", "attn_out_norm_w", "ssm_out_norm_w")
_WEIGHTS = ("norm_w", "w_in", "q_norm_w", "k_norm_w", "sinks", "a_re", "a_im", "log_step", "b_re", "b_im", "c_re",
            "c_im", "d_skip", "w_glu", "b_glu", "attn_out_norm_w", "ssm_out_norm_w", "w_out")
_PACK_ROWS = 2176


def _pack(d):
    flat = jnp.concatenate([d[n].reshape(-1).astype(F32) for n in _SMALL])
    return jnp.pad(flat, (0, _PACK_ROWS * 128 - flat.shape[0])).reshape(_PACK_ROWS, 128)


def _unpack(packed, like):
    flat = packed.reshape(-1)
    out, off = {}, 0
    for n in _SMALL:
        size = math.prod(like[n].shape)
        out[n] = flat[off:off + size].reshape(like[n].shape)
        off += size
    return out


def kernel(x, positions, norm_w, w_in, q_norm_w, k_norm_w, sinks, a_re, a_im, log_step, b_re, b_im, c_re, c_im, d_skip, w_glu, b_glu, attn_out_norm_w, ssm_out_norm_w, w_out, loss_target, m_norm_w, m_w_in, m_q_norm_w, m_k_norm_w, m_sinks, m_a_re, m_a_im, m_log_step, m_b_re, m_b_im, m_c_re, m_c_im, m_d_skip, m_w_glu, m_b_glu, m_attn_out_norm_w, m_ssm_out_norm_w, m_w_out, v_norm_w, v_w_in, v_q_norm_w, v_k_norm_w, v_sinks, v_a_re, v_a_im, v_log_step, v_b_re, v_b_im, v_c_re, v_c_im, v_d_skip, v_w_glu, v_b_glu, v_attn_out_norm_w, v_ssm_out_norm_w, v_w_out):
    w = dict(norm_w=norm_w, w_in=w_in, q_norm_w=q_norm_w, k_norm_w=k_norm_w, sinks=sinks, a_re=a_re, a_im=a_im,
             log_step=log_step, b_re=b_re, b_im=b_im, c_re=c_re, c_im=c_im, d_skip=d_skip, w_glu=w_glu, b_glu=b_glu,
             attn_out_norm_w=attn_out_norm_w, ssm_out_norm_w=ssm_out_norm_w, w_out=w_out)
    m = dict(norm_w=m_norm_w, w_in=m_w_in, q_norm_w=m_q_norm_w, k_norm_w=m_k_norm_w, sinks=m_sinks, a_re=m_a_re,
             a_im=m_a_im, log_step=m_log_step, b_re=m_b_re, b_im=m_b_im, c_re=m_c_re, c_im=m_c_im, d_skip=m_d_skip,
             w_glu=m_w_glu, b_glu=m_b_glu, attn_out_norm_w=m_attn_out_norm_w, ssm_out_norm_w=m_ssm_out_norm_w,
             w_out=m_w_out)
    v = dict(norm_w=v_norm_w, w_in=v_w_in, q_norm_w=v_q_norm_w, k_norm_w=v_k_norm_w, sinks=v_sinks, a_re=v_a_re,
             a_im=v_a_im, log_step=v_log_step, b_re=v_b_re, b_im=v_b_im, c_re=v_c_re, c_im=v_c_im, d_skip=v_d_skip,
             w_glu=v_w_glu, b_glu=v_b_glu, attn_out_norm_w=v_attn_out_norm_w, ssm_out_norm_w=v_ssm_out_norm_w,
             w_out=v_w_out)
    core = lax.axis_index("c").astype(jnp.int32).reshape(1)
    chip = (2 * lax.axis_index("x") + lax.axis_index("y")).astype(jnp.int32).reshape(1)

    wt_in, wf_glu, wf_out = _all_gather_rows(
        [w_in.T.astype(BF16), w_glu.astype(BF16), w_out.astype(BF16)], "gather_weights")
    wt_in = wt_in.reshape(IN_W, D_MODEL)
    wf_glu = wf_glu.reshape(SSM_W, SSM_W)
    wf_out = wf_out.reshape(D_MODEL, D_MODEL)

    small = {n: w[n] for n in _SMALL}
    loss, grad_x, g_wt_in, g_w_glu, g_w_out, g_small = _local_step(
        x[0], positions[0], loss_target[0], small, wt_in, wf_glu, wf_out)
    loss = lax.psum(loss, ("x", "y", "c"))

    full = [g_wt_in.reshape(N_DEV, IN_W // N_DEV, D_MODEL), g_w_glu.reshape(N_DEV, SSM_W // N_DEV, SSM_W),
            g_w_out.reshape(N_DEV, D_MODEL // N_DEV, D_MODEL), _pack(g_small).reshape(N_DEV, _PACK_ROWS // N_DEV, 128)]
    from_sibling = _pair_exchange(full, "pair_exchange")
    wire = (BF16, BF16, BF16, F32)
    parts = [_pair_sum(g, ra, core, dt, f"pair_sum_{k}") for k, (g, ra, dt) in enumerate(zip(full, from_sibling, wire))]
    from_chips = _chip_exchange(parts, "chip_exchange")
    red = [_chip_sum(p, rb, chip, f"chip_sum_{k}") for k, (p, rb) in enumerate(zip(parts, from_chips))]
    g_in, g_glu, g_out = red[0].T, red[1], red[2]
    (g_packed,) = _all_gather_rows([red[3]], "gather_small")
    g_packed = g_packed.reshape(_PACK_ROWS, 128)

    grads = _unpack(g_packed, w)
    grads.update(w_in=g_in, w_glu=g_glu, w_out=g_out)
    delta, new_m, new_v = {}, {}, {}
    for n in ("w_in", "w_glu", "w_out"):
        delta[n], new_m[n], new_v[n] = _adamw(grads[n], w[n], m[n], v[n], f"adamw_{n}")
    d_p, m_p, v_p = _adamw(g_packed, _pack(w), _pack(m), _pack(v), "adamw_small")
    delta.update(_unpack(d_p, w))
    new_m.update(_unpack(m_p, w))
    new_v.update(_unpack(v_p, w))

    return (loss, grad_x[None], *[grads[n] for n in _WEIGHTS], *[delta[n] for n in _WEIGHTS],
            *[new_m[n] for n in _WEIGHTS], *[new_v[n] for n in _WEIGHTS])
```

```python
import functools
import math

import jax
import jax.numpy as jnp
from jax import lax
from jax.experimental import pallas as pl
from jax.experimental.pallas import tpu as pltpu

F32 = jnp.float32
BF16 = jnp.bfloat16

D_MODEL = 2048
ATTN_W = 1024
KV_W = 256
SSM_W = 1024
HEAD_DIM = 64
N_HEADS = 16
N_KV = 4
KV_REP = 4
IN_W = 4608
BLOCK = 128
ROPE_THETA = 10000.0
NORM_EPS = 1e-6
SSM_G = 64
SSM_P = 64
SSM_H = 16
CHUNK = 16
CW = CHUNK * SSM_H
N_DEV = 8

ADAM_LR = 0.001
ADAM_B1 = 0.9
ADAM_B2 = 0.999
ADAM_EPS = 1e-08
ADAM_WD = 0.01
ADAM_STEP = 10

VMEM_LIMIT = 56 * 1024 * 1024
MESH = pl.DeviceIdType.MESH


def _cp(sem=None):
    if sem is None:
        return pltpu.CompilerParams(vmem_limit_bytes=VMEM_LIMIT)
    return pltpu.CompilerParams(vmem_limit_bytes=VMEM_LIMIT, dimension_semantics=sem)


def _sigmoid(x):
    return 1.0 / (1.0 + jnp.exp(-x))


def _silu(x):
    return x * _sigmoid(x)


def _dsilu(x):
    s = _sigmoid(x)
    return s * (1.0 + x * (1.0 - s))


_GELU_C = math.sqrt(2.0 / math.pi)


def _gelu(y):
    t = jnp.tanh(_GELU_C * (y + 0.044715 * y * y * y))
    return 0.5 * y * (1.0 + t)


def _dgelu(y):
    t = jnp.tanh(_GELU_C * (y + 0.044715 * y * y * y))
    return 0.5 * (1.0 + t) + 0.5 * y * (1.0 - t * t) * _GELU_C * (1.0 + 3.0 * 0.044715 * y * y)


def _tile(n, want):
    if n <= want:
        return n
    for t in range(want - want % 16, 0, -16):
        if n % t == 0:
            return t
    raise ValueError((n, want))


def _mm(a, b, mode, out_dtype, name, tm=512, tn=1024, add=None):
    if mode == "nn":
        (M, K), (K2, N) = a.shape, b.shape
    elif mode == "nt":
        (M, K), (N, K2) = a.shape, b.shape
    else:
        (K, M), (K2, N) = a.shape, b.shape
    assert K == K2
    tm, tn = _tile(M, tm), _tile(N, tn)
    dn = {"nn": _NN, "nt": _NT, "tn": _TN}[mode]

    def body(a_ref, b_ref, *rest):
        o_ref = rest[-1]
        acc = lax.dot_general(a_ref[...].astype(BF16), b_ref[...].astype(BF16), dn, preferred_element_type=F32)
        if add is not None:
            acc = acc + rest[0][...]
        o_ref[...] = acc.astype(o_ref.dtype)

    a_spec = pl.BlockSpec((K, tm), lambda j, i: (0, i)) if mode == "tn" else pl.BlockSpec((tm, K), lambda j, i: (i, 0))
    b_spec = pl.BlockSpec((tn, K), lambda j, i: (j, 0)) if mode == "nt" else pl.BlockSpec((K, tn), lambda j, i: (0, j))
    o_spec = pl.BlockSpec((tm, tn), lambda j, i: (i, j))
    extra = () if add is None else (add,)
    return pl.pallas_call(
        body,
        name=name,
        grid=(N // tn, M // tm),
        in_specs=[a_spec, b_spec] + [o_spec] * len(extra),
        out_specs=o_spec,
        out_shape=jax.ShapeDtypeStruct((M, N), out_dtype),
        compiler_params=_cp(("parallel", "parallel")),
    )(a, b, *extra)


def _rms_inproj(x, norm_w, wt_in):
    L = x.shape[0]
    tm, tn = _tile(L, 1024), 768
    nj = IN_W // tn

    def body(x_ref, w_ref, wt_ref, proj_ref, hn_ref, hn_scr):
        j = pl.program_id(1)

        @pl.when(j == 0)
        def _():
            xv = x_ref[...]
            r = lax.rsqrt(jnp.mean(xv * xv, axis=-1, keepdims=True) + NORM_EPS)
            hn = (xv * r * w_ref[...]).astype(BF16)
            hn_scr[...] = hn
            hn_ref[...] = hn

        proj_ref[...] = lax.dot_general(hn_scr[...], wt_ref[...], (((1,), (1,)), ((), ())),
                                        preferred_element_type=F32)

    return pl.pallas_call(
        body,
        name="rms_inproj",
        grid=(L // tm, nj),
        in_specs=[pl.BlockSpec((tm, D_MODEL), lambda i, j: (i, 0)),
                  pl.BlockSpec((1, D_MODEL), lambda i, j: (0, 0)),
                  pl.BlockSpec((tn, D_MODEL), lambda i, j: (j, 0))],
        out_specs=[pl.BlockSpec((tm, tn), lambda i, j: (i, j)),
                   pl.BlockSpec((tm, D_MODEL), lambda i, j: (i, 0))],
        out_shape=[jax.ShapeDtypeStruct((L, IN_W), F32), jax.ShapeDtypeStruct((L, D_MODEL), BF16)],
        scratch_shapes=[pltpu.VMEM((tm, D_MODEL), BF16)],
        compiler_params=_cp(("parallel", "arbitrary")),
    )(x, norm_w.reshape(1, D_MODEL), wt_in)


def _rot_half(t):
    return jnp.concatenate([t[:, HEAD_DIM // 2:], t[:, :HEAD_DIM // 2]], axis=1)


def _norm_rope(raw, w, tab):
    r = lax.rsqrt(jnp.mean(raw * raw, axis=-1, keepdims=True) + NORM_EPS)
    tn = raw * r * w
    return r, tn * tab[:, :HEAD_DIM] + _rot_half(tn) * tab[:, HEAD_DIM:]


def _norm_rope_bwd(d_rot, raw, r, w, tab):
    d_tn = d_rot * tab[:, :HEAD_DIM] + _rot_half(d_rot * tab[:, HEAD_DIM:])
    xh = raw * r
    gw = d_tn * w
    d_raw = r * (gw - xh * jnp.mean(gw * xh, axis=-1, keepdims=True))
    return d_raw, d_tn * xh


def _band_mask_cat(has_prev):
    qi = lax.broadcasted_iota(jnp.int32, (BLOCK, 2 * BLOCK), 0) + BLOCK
    kj = lax.broadcasted_iota(jnp.int32, (BLOCK, 2 * BLOCK), 1)
    rel = qi - kj
    return (rel >= 0) & (rel < BLOCK) & ((kj >= BLOCK) | has_prev)


def _lane_col(mat, h):
    lane = lax.broadcasted_iota(jnp.int32, mat.shape, 1)
    return jnp.sum(jnp.where(lane == h, mat, 0.0), axis=1, keepdims=True)


_SCALE = 1.0 / math.sqrt(HEAD_DIM)
_NT = (((1,), (1,)), ((), ()))
_NN = (((1,), (0,)), ((), ()))
_TN = (((0,), (0,)), ((), ()))


def _dot(a, b, dn):
    return lax.dot_general(a.astype(BF16), b.astype(BF16), dn, preferred_element_type=F32)


def _attn_fwd(proj, tab, qw, kw, sinks):
    L = proj.shape[0]
    nb = L // BLOCK

    def body(q_ref, kc_ref, kp_ref, vc_ref, vp_ref, z0_ref, z1_ref, tc_ref, tp_ref, qw_ref, kw_ref, sink_ref,
             og_ref, o_ref, lse_ref):
        i = pl.program_id(0)
        mask = _band_mask_cat(i > 0)
        tab_c = tc_ref[...]
        tab_cat = jnp.concatenate([tp_ref[...], tab_c], axis=0)
        z = jnp.concatenate([z0_ref[...], z1_ref[...]], axis=1)
        lane = lax.broadcasted_iota(jnp.int32, (BLOCK, 128), 1)
        lse_mat = jnp.zeros((BLOCK, 128), F32)
        outs = []
        for g in range(N_KV):
            ks = slice(g * HEAD_DIM, (g + 1) * HEAD_DIM)
            k_raw = jnp.concatenate([kp_ref[:, ks], kc_ref[:, ks]], axis=0)
            v_cat = jnp.concatenate([vp_ref[:, ks], vc_ref[:, ks]], axis=0)
            _, kr = _norm_rope(k_raw, kw_ref[...], tab_cat)
            for rr in range(KV_REP):
                h = g * KV_REP + rr
                _, qr = _norm_rope(q_ref[:, h * HEAD_DIM:(h + 1) * HEAD_DIM], qw_ref[...], tab_c)
                s = _dot(qr, kr, _NT) * _SCALE
                s = jnp.where(mask, s, -1e30)
                sink = sink_ref[h]
                m = jnp.maximum(jnp.max(s, axis=-1, keepdims=True), sink)
                e = jnp.exp(s - m)
                den = jnp.sum(e, axis=-1, keepdims=True) + jnp.exp(sink - m)
                p = e / den
                outs.append(_dot(p, v_cat, _NN))
                lse_mat = jnp.where(lane == h, m + jnp.log(den), lse_mat)
        o = jnp.concatenate(outs, axis=1)
        o_ref[...] = o
        og_ref[...] = o * _silu(z)
        lse_ref[...] = lse_mat

    prev = lambda i: jnp.maximum(i - 1, 0)
    return pl.pallas_call(
        body,
        name="attn_fwd",
        grid=(nb,),
        in_specs=[pl.BlockSpec((BLOCK, ATTN_W), lambda i: (i, 0)),
                  pl.BlockSpec((BLOCK, KV_W), lambda i: (i, 4)),
                  pl.BlockSpec((BLOCK, KV_W), lambda i: (prev(i), 4)),
                  pl.BlockSpec((BLOCK, KV_W), lambda i: (i, 5)),
                  pl.BlockSpec((BLOCK, KV_W), lambda i: (prev(i), 5)),
                  pl.BlockSpec((BLOCK, 512), lambda i: (i, 3)),
                  pl.BlockSpec((BLOCK, 512), lambda i: (i, 4)),
                  pl.BlockSpec((BLOCK, 128), lambda i: (i, 0)),
                  pl.BlockSpec((BLOCK, 128), lambda i: (prev(i), 0)),
                  pl.BlockSpec((1, HEAD_DIM), lambda i: (0, 0)),
                  pl.BlockSpec((1, HEAD_DIM), lambda i: (0, 0)),
                  pl.BlockSpec(memory_space=pltpu.SMEM)],
        out_specs=[pl.BlockSpec((BLOCK, ATTN_W), lambda i: (i, 0)),
                   pl.BlockSpec((BLOCK, ATTN_W), lambda i: (i, 0)),
                   pl.BlockSpec((BLOCK, 128), lambda i: (i, 0))],
        out_shape=[jax.ShapeDtypeStruct((L, ATTN_W), F32), jax.ShapeDtypeStruct((L, ATTN_W), F32),
                   jax.ShapeDtypeStruct((L, 128), F32)],
        compiler_params=_cp(("parallel",)),
    )(proj, proj, proj, proj, proj, proj, proj, tab, tab, qw.reshape(1, HEAD_DIM), kw.reshape(1, HEAD_DIM), sinks)


def _attn_bwd(proj, tab, qw, kw, sinks, d_o, o, lse):
    L = proj.shape[0]
    nb = L // BLOCK

    def body(q_ref, qn_ref, kc_ref, kp_ref, vc_ref, vp_ref, do_ref, don_ref, o_ref, on_ref, lse_ref, lsen_ref,
             tc_ref, tp_ref, tn_ref, qw_ref, kw_ref, sink_ref, dqkv_ref, gqw_ref, gkw_ref, gs_ref):
        i = pl.program_id(0)

        @pl.when(i == 0)
        def _():
            gqw_ref[...] = jnp.zeros_like(gqw_ref)
            gkw_ref[...] = jnp.zeros_like(gkw_ref)
            gs_ref[...] = jnp.zeros_like(gs_ref)

        mask = _band_mask_cat(i > 0)
        qi = lax.broadcasted_iota(jnp.int32, (BLOCK, BLOCK), 0)
        kj = lax.broadcasted_iota(jnp.int32, (BLOCK, BLOCK), 1)
        mask_n = (kj > qi) & (i < nb - 1)
        tab_c, tab_n = tc_ref[...], tn_ref[...]
        tab_cat = jnp.concatenate([tp_ref[...], tab_c], axis=0)
        lse_c, lse_n = lse_ref[...], lsen_ref[...]
        lane = lax.broadcasted_iota(jnp.int32, (1, 128), 1)
        gqw = jnp.zeros((1, HEAD_DIM), F32)
        gkw = jnp.zeros((1, HEAD_DIM), F32)
        gs = jnp.zeros((1, 128), F32)
        dq_parts, dk_parts, dv_parts = [], [], []
        for g in range(N_KV):
            ks = slice(g * HEAD_DIM, (g + 1) * HEAD_DIM)
            kc_raw = kc_ref[:, ks]
            k_raw = jnp.concatenate([kp_ref[:, ks], kc_raw], axis=0)
            v_c = vc_ref[:, ks]
            v_cat = jnp.concatenate([vp_ref[:, ks], v_c], axis=0)
            rk, kr = _norm_rope(k_raw, kw_ref[...], tab_cat)
            kr_c = kr[BLOCK:]
            dkr = jnp.zeros((BLOCK, HEAD_DIM), F32)
            dv = jnp.zeros((BLOCK, HEAD_DIM), F32)
            for rr in range(KV_REP):
                h = g * KV_REP + rr
                hs = slice(h * HEAD_DIM, (h + 1) * HEAD_DIM)
                sink = sink_ref[h]
                q_raw = q_ref[:, hs]
                rq, qr = _norm_rope(q_raw, qw_ref[...], tab_c)
                do_h, o_h = do_ref[:, hs], o_ref[:, hs]
                lse_h = _lane_col(lse_c, h)
                delta = jnp.sum(do_h * o_h, axis=-1, keepdims=True)
                s = jnp.where(mask, _dot(qr, kr, _NT) * _SCALE, -1e30)
                p = jnp.exp(s - lse_h)
                dp = _dot(do_h, v_cat, _NT)
                ds = p * (dp - delta) * _SCALE
                dqr = _dot(ds, kr, _NN)
                dq_raw, gq = _norm_rope_bwd(dqr, q_raw, rq, qw_ref[...], tab_c)
                dq_parts.append(dq_raw)
                gqw = gqw + jnp.sum(gq, axis=0, keepdims=True)
                gs = gs + jnp.where(lane == h, jnp.sum(-jnp.exp(sink - lse_h) * delta), 0.0)
                dkr = dkr + _dot(ds[:, BLOCK:], qr, _TN)
                dv = dv + _dot(p[:, BLOCK:], do_h, _TN)
                _, qr_n = _norm_rope(qn_ref[:, hs], qw_ref[...], tab_n)
                don_h = don_ref[:, hs]
                delta_n = jnp.sum(don_h * on_ref[:, hs], axis=-1, keepdims=True)
                s_n = _dot(qr_n, kr_c, _NT) * _SCALE
                p_n = jnp.where(mask_n, jnp.exp(jnp.where(mask_n, s_n, -1e30) - _lane_col(lse_n, h)), 0.0)
                ds_n = p_n * (_dot(don_h, v_c, _NT) - delta_n) * _SCALE
                dkr = dkr + _dot(ds_n, qr_n, _TN)
                dv = dv + _dot(p_n, don_h, _TN)
            dk_raw, gk = _norm_rope_bwd(dkr, kc_raw, rk[BLOCK:], kw_ref[...], tab_c)
            gkw = gkw + jnp.sum(gk, axis=0, keepdims=True)
            dk_parts.append(dk_raw)
            dv_parts.append(dv)
        dqkv_ref[...] = jnp.concatenate(dq_parts + dk_parts + dv_parts, axis=1).astype(BF16)
        gqw_ref[...] += gqw
        gkw_ref[...] += gkw
        gs_ref[...] += gs

    prev = lambda i: jnp.maximum(i - 1, 0)
    nxt = lambda i: jnp.minimum(i + 1, nb - 1)
    bs = pl.BlockSpec
    return pl.pallas_call(
        body,
        name="attn_bwd",
        grid=(nb,),
        in_specs=[bs((BLOCK, ATTN_W), lambda i: (i, 0)), bs((BLOCK, ATTN_W), lambda i: (nxt(i), 0)),
                  bs((BLOCK, KV_W), lambda i: (i, 4)), bs((BLOCK, KV_W), lambda i: (prev(i), 4)),
                  bs((BLOCK, KV_W), lambda i: (i, 5)), bs((BLOCK, KV_W), lambda i: (prev(i), 5)),
                  bs((BLOCK, ATTN_W), lambda i: (i, 0)), bs((BLOCK, ATTN_W), lambda i: (nxt(i), 0)),
                  bs((BLOCK, ATTN_W), lambda i: (i, 0)), bs((BLOCK, ATTN_W), lambda i: (nxt(i), 0)),
                  bs((BLOCK, 128), lambda i: (i, 0)), bs((BLOCK, 128), lambda i: (nxt(i), 0)),
                  bs((BLOCK, 128), lambda i: (i, 0)), bs((BLOCK, 128), lambda i: (prev(i), 0)),
                  bs((BLOCK, 128), lambda i: (nxt(i), 0)),
                  bs((1, HEAD_DIM), lambda i: (0, 0)), bs((1, HEAD_DIM), lambda i: (0, 0)),
                  bs(memory_space=pltpu.SMEM)],
        out_specs=[bs((BLOCK, ATTN_W + 2 * KV_W), lambda i: (i, 0)),
                   bs((1, HEAD_DIM), lambda i: (0, 0)), bs((1, HEAD_DIM), lambda i: (0, 0)),
                   bs((1, 128), lambda i: (0, 0))],
        out_shape=[jax.ShapeDtypeStruct((L, ATTN_W + 2 * KV_W), BF16),
                   jax.ShapeDtypeStruct((1, HEAD_DIM), F32), jax.ShapeDtypeStruct((1, HEAD_DIM), F32),
                   jax.ShapeDtypeStruct((1, 128), F32)],
        compiler_params=_cp(("arbitrary",)),
    )(proj, proj, proj, proj, proj, proj, d_o, d_o, o, o, lse, lse, tab, tab, tab,
      qw.reshape(1, HEAD_DIM), kw.reshape(1, HEAD_DIM), sinks)


def _ssm_ops(a_re, a_im, log_step, b_re, b_im, c_re, c_im):
    hp = lax.Precision.HIGHEST
    delta = jnp.exp(log_step)[:, None]
    xr, xi = a_re * delta, a_im * delta
    er = jnp.exp(xr)
    lbr, lbi = er * jnp.cos(xi), er * jnp.sin(xi)
    nr, ni = lbr - 1.0, lbi
    den = a_re * a_re + a_im * a_im
    cr, ci = (nr * a_re + ni * a_im) / den, (ni * a_re - nr * a_im) / den
    bbr = cr[..., None] * b_re - ci[..., None] * b_im
    bbi = cr[..., None] * b_im + ci[..., None] * b_re
    pr, pi = [jnp.ones_like(lbr)], [jnp.zeros_like(lbr)]
    for _ in range(CHUNK):
        pr.append(pr[-1] * lbr - pi[-1] * lbi)
        pi.append(pr[-2] * lbi + pi[-1] * lbr)
    pr, pi = jnp.stack(pr, axis=1), jnp.stack(pi, axis=1)
    p0r, p0i = pr[:, :CHUNK, None, :], pi[:, :CHUNK, None, :]
    clr = c_re[:, None] * p0r - c_im[:, None] * p0i
    cli = c_re[:, None] * p0i + c_im[:, None] * p0r
    clcat = jnp.concatenate([clr, -cli], axis=-1).reshape(SSM_G, CW, 2 * SSM_P)
    kk = jnp.einsum("gmp,gpk->gmk", clcat, jnp.concatenate([bbr, bbi], axis=1), precision=hp)
    kl = kk.reshape(SSM_G, CHUNK, SSM_H, SSM_H).transpose(0, 3, 2, 1)
    wide = jnp.concatenate([jnp.zeros_like(kl), kl, jnp.zeros_like(kl[..., :1])], axis=-1)
    skew = jnp.tile(wide, (1, 1, 1, CHUNK))[..., :2 * CHUNK * CHUNK]
    skew = skew.reshape(SSM_G, SSM_H, SSM_H, CHUNK, 2 * CHUNK)[..., CHUNK:]
    mt = skew.transpose(0, 3, 1, 4, 2).reshape(SSM_G, CW, CW)
    rr, ri = pr[:, CHUNK - 1::-1][:, :, None, :], pi[:, CHUNK - 1::-1][:, :, None, :]
    bbrt, bbit = bbr.transpose(0, 2, 1)[:, None], bbi.transpose(0, 2, 1)[:, None]
    sr = (rr * bbrt - ri * bbit).reshape(SSM_G, CW, SSM_P)
    si = (rr * bbit + ri * bbrt).reshape(SSM_G, CW, SSM_P)
    scat = jnp.concatenate([sr, si], axis=-1)
    ctr, cti = c_re.transpose(0, 2, 1), c_im.transpose(0, 2, 1)
    p1r, p1i = pr[:, 1:].transpose(0, 2, 1), pi[:, 1:].transpose(0, 2, 1)
    o_r = (ctr[:, :, None, :] * p1r[..., None] - cti[:, :, None, :] * p1i[..., None]).reshape(SSM_G, SSM_P, CW)
    o_i = (ctr[:, :, None, :] * p1i[..., None] + cti[:, :, None, :] * p1r[..., None]).reshape(SSM_G, SSM_P, CW)
    ocat = jnp.concatenate([o_r, -o_i], axis=1)
    a16 = jnp.concatenate([pr[:, CHUNK], pi[:, CHUNK]], axis=-1)[:, None, :]
    return mt, scat, ocat, a16


def _cmul_const(xv, ar, ai):
    return xv * ar + pltpu.roll(xv, SSM_P, 1) * ai


def _chunk_scan(inc, a_row, reverse):
    n = inc.shape[0]
    lane = lax.broadcasted_iota(jnp.int32, (1, 2 * SSM_P), 1)
    row = lax.broadcasted_iota(jnp.int32, inc.shape, 0)
    sign = jnp.where(lane < SSM_P, -1.0, 1.0)
    ar = jnp.where(lane < SSM_P, a_row, pltpu.roll(a_row, SSM_P, 1))
    ai = jnp.where(lane < SSM_P, pltpu.roll(a_row, SSM_P, 1), a_row)
    if reverse:
        ai = -ai
    xv = inc
    s = 1
    while s < n:
        if reverse:
            sh = jnp.where(row < n - s, pltpu.roll(xv, n - s, 0), 0.0)
        else:
            sh = jnp.where(row >= s, pltpu.roll(xv, s, 0), 0.0)
        xv = xv + _cmul_const(sh, ar, ai * sign)
        ar, ai = ar * ar - ai * ai, 2.0 * ar * ai
        s *= 2
    return xv


def _shift_rows(xv, reverse):
    n = xv.shape[0]
    row = lax.broadcasted_iota(jnp.int32, xv.shape, 0)
    if reverse:
        return jnp.where(row < n - 1, pltpu.roll(xv, n - 1, 0), 0.0)
    return jnp.where(row >= 1, pltpu.roll(xv, 1, 0), 0.0)


GB = 128 // SSM_H
U_COL0 = (ATTN_W + 2 * KV_W + ATTN_W) // 128


def _chunk_perm():
    r = jnp.arange(CHUNK * 128)
    t, g8, h = r // 128, (r % 128) // SSM_H, r % SSM_H
    return ((g8 * CW + t * SSM_H + h)[:, None] == jnp.arange(GB * CW)[None, :]).astype(BF16)


def _load_perm(p_hbm, p_scr, sem):
    @pl.when(pl.program_id(0) == 0)
    def _():
        cp = pltpu.make_async_copy(p_hbm, p_scr, sem)
        cp.start()
        cp.wait()


def _rows_to_chunks(pieces, perm):
    z = jnp.concatenate(pieces, axis=1).astype(BF16)
    return jnp.dot(z, perm, preferred_element_type=F32).astype(BF16)


def _chunks_to_rows(ya, perm):
    hi = ya.astype(BF16)
    lo = (ya - hi.astype(F32)).astype(BF16)
    return (lax.dot_general(hi, perm, _NT, preferred_element_type=F32)
            + lax.dot_general(lo, perm, _NT, preferred_element_type=F32))


def _ssm_fwd(proj, perm, mt, scat, ocat, a16, d_skip):
    L = proj.shape[0]
    nc = L // CHUNK

    def body(u_ref, p_hbm, mt_ref, s_ref, o_ref, a_ref, d_ref, y_ref, yg_ref, h_ref, p_scr, sem):
        _load_perm(p_hbm, p_scr, sem)
        perm = p_scr[...]
        rows = [pl.ds(t, nc, stride=CHUNK) for t in range(CHUNK)]
        ua = _rows_to_chunks([u_ref[r, :] for r in rows], perm)
        ys = []
        for g in range(GB):
            uv = ua[:, g * CW:(g + 1) * CW]
            inc = jnp.dot(uv, s_ref[g], preferred_element_type=F32)
            hx = _shift_rows(_chunk_scan(inc, a_ref[g], False), False)
            h_ref[g] = hx
            ys.append(jnp.dot(uv, mt_ref[g], preferred_element_type=F32)
                      + jnp.dot(hx.astype(BF16), o_ref[g], preferred_element_type=F32))
        yp = _chunks_to_rows(jnp.concatenate(ys, axis=1), perm)
        for t, r in enumerate(rows):
            y = yp[:, t * 128:(t + 1) * 128] + d_ref[...] * u_ref[r, :]
            y_ref[r, :] = y
            yg_ref[r, :] = _gelu(y)

    g3 = lambda r, c: pl.BlockSpec((GB, r, c), lambda g: (g, 0, 0))
    col = pl.BlockSpec((L, 128), lambda g: (0, g))
    return pl.pallas_call(
        body,
        name="ssm_fwd",
        grid=(SSM_G // GB,),
        in_specs=[pl.BlockSpec((L, 128), lambda g: (0, U_COL0 + g)), _ANY,
                  g3(CW, CW), g3(CW, 2 * SSM_P), g3(2 * SSM_P, CW), g3(1, 2 * SSM_P),
                  pl.BlockSpec((1, 128), lambda g: (0, g))],
        out_specs=[col, col, g3(nc, 2 * SSM_P)],
        out_shape=[jax.ShapeDtypeStruct((L, SSM_W), F32), jax.ShapeDtypeStruct((L, SSM_W), F32),
                   jax.ShapeDtypeStruct((SSM_G, nc, 2 * SSM_P), F32)],
        scratch_shapes=[pltpu.VMEM((CHUNK * 128, GB * CW), BF16), pltpu.SemaphoreType.DMA],
        compiler_params=_cp(("arbitrary",)),
    )(proj, perm, mt, scat, ocat, a16, d_skip.reshape(1, SSM_W))


def _ssm_bwd(d_yg, y, proj, hx, perm, mt, scat, ocat, a16, d_skip):
    L = proj.shape[0]
    nc = L // CHUNK

    def body(dg_ref, y_ref, u_ref, h_ref, p_hbm, mt_ref, s_ref, o_ref, a_ref, d_ref,
             du_ref, gmt_ref, gs_ref, go_ref, ga_ref, gd_ref, p_scr, sem):
        _load_perm(p_hbm, p_scr, sem)
        perm = p_scr[...]
        rows = [pl.ds(t, nc, stride=CHUNK) for t in range(CHUNK)]
        us = [u_ref[r, :] for r in rows]
        dys = [dg_ref[r, :] * _dgelu(y_ref[r, :]) for r in rows]
        gd = jnp.zeros((1, 128), F32)
        for uv, dy in zip(us, dys):
            gd = gd + jnp.sum(dy * uv, axis=0, keepdims=True)
        gd_ref[...] = gd
        ua = _rows_to_chunks(us, perm)
        dya = _rows_to_chunks(dys, perm)
        lane = lax.broadcasted_iota(jnp.int32, (1, 2 * SSM_P), 1)
        dus = []
        for g in range(GB):
            uv, dy, hx_v = ua[:, g * CW:(g + 1) * CW], dya[:, g * CW:(g + 1) * CW], h_ref[g]
            dh = lax.dot_general(dy, o_ref[g], _NT, preferred_element_type=F32)
            dinc = _shift_rows(_chunk_scan(dh, a_ref[g], True), True)
            dinc_b = dinc.astype(BF16)
            dus.append(lax.dot_general(dy, mt_ref[g], _NT, preferred_element_type=F32)
                       + lax.dot_general(dinc_b, s_ref[g], _NT, preferred_element_type=F32))
            gmt_ref[g] = lax.dot_general(uv, dy, _TN, preferred_element_type=F32)
            gs_ref[g] = lax.dot_general(uv, dinc_b, _TN, preferred_element_type=F32)
            go_ref[g] = lax.dot_general(hx_v.astype(BF16), dy, _TN, preferred_element_type=F32)
            p1 = dinc * hx_v
            p2 = pltpu.roll(dinc, SSM_P, 1) * hx_v
            t1 = jnp.sum(p1 + pltpu.roll(p1, SSM_P, 1), axis=0, keepdims=True)
            t2 = jnp.sum(p2 - pltpu.roll(p2, SSM_P, 1), axis=0, keepdims=True)
            ga_ref[g] = jnp.where(lane < SSM_P, t1, pltpu.roll(t2, SSM_P, 1))
        dup = _chunks_to_rows(jnp.concatenate(dus, axis=1), perm)
        for t, r in enumerate(rows):
            du_ref[r, :] = dup[:, t * 128:(t + 1) * 128] + d_ref[...] * dys[t]

    g3 = lambda r, c: pl.BlockSpec((GB, r, c), lambda g: (g, 0, 0))
    col = pl.BlockSpec((L, 128), lambda g: (0, g))
    row = pl.BlockSpec((1, 128), lambda g: (0, g))
    return pl.pallas_call(
        body,
        name="ssm_bwd",
        grid=(SSM_G // GB,),
        in_specs=[col, col, pl.BlockSpec((L, 128), lambda g: (0, U_COL0 + g)), g3(nc, 2 * SSM_P), _ANY,
                  g3(CW, CW), g3(CW, 2 * SSM_P), g3(2 * SSM_P, CW), g3(1, 2 * SSM_P), row],
        out_specs=[col, g3(CW, CW), g3(CW, 2 * SSM_P), g3(2 * SSM_P, CW), g3(1, 2 * SSM_P), row],
        out_shape=[jax.ShapeDtypeStruct((L, SSM_W), F32), jax.ShapeDtypeStruct((SSM_G, CW, CW), F32),
                   jax.ShapeDtypeStruct((SSM_G, CW, 2 * SSM_P), F32),
                   jax.ShapeDtypeStruct((SSM_G, 2 * SSM_P, CW), F32),
                   jax.ShapeDtypeStruct((SSM_G, 1, 2 * SSM_P), F32),
                   jax.ShapeDtypeStruct((1, SSM_W), F32)],
        scratch_shapes=[pltpu.VMEM((CHUNK * 128, GB * CW), BF16), pltpu.SemaphoreType.DMA],
        compiler_params=_cp(("arbitrary",)),
    )(d_yg, y, proj, hx, perm, mt, scat, ocat, a16, d_skip.reshape(1, SSM_W))


def _merge(og, yg, gpre, proj, b_glu, wa, ws):
    L = og.shape[0]
    tm = _tile(L, 256)

    def body(og_ref, yg_ref, gp_ref, z0_ref, z1_ref, b_ref, wa_ref, ws_ref, m_ref):
        zs = jnp.concatenate([z0_ref[...], z1_ref[...]], axis=1)
        os_ = yg_ref[...] * _sigmoid(gp_ref[...] + b_ref[...]) * _silu(zs)
        ogv = og_ref[...]
        ra = lax.rsqrt(jnp.mean(ogv * ogv, axis=-1, keepdims=True) + NORM_EPS)
        rs = lax.rsqrt(jnp.mean(os_ * os_, axis=-1, keepdims=True) + NORM_EPS)
        m_ref[:, :ATTN_W] = (ogv * ra * wa_ref[...]).astype(BF16)
        m_ref[:, ATTN_W:] = (os_ * rs * ws_ref[...]).astype(BF16)

    row = lambda w: pl.BlockSpec((1, w), lambda i: (0, 0))
    return pl.pallas_call(
        body,
        name="merge",
        grid=(L // tm,),
        in_specs=[pl.BlockSpec((tm, ATTN_W), lambda i: (i, 0)), pl.BlockSpec((tm, SSM_W), lambda i: (i, 0)),
                  pl.BlockSpec((tm, SSM_W), lambda i: (i, 0)),
                  pl.BlockSpec((tm, 512), lambda i: (i, 7)), pl.BlockSpec((tm, 512), lambda i: (i, 8)),
                  row(SSM_W), row(ATTN_W), row(SSM_W)],
        out_specs=pl.BlockSpec((tm, D_MODEL), lambda i: (i, 0)),
        out_shape=jax.ShapeDtypeStruct((L, D_MODEL), BF16),
        compiler_params=_cp(("parallel",)),
    )(og, yg, gpre, proj, proj, b_glu.reshape(1, SSM_W), wa.reshape(1, ATTN_W), ws.reshape(1, SSM_W))


def _outproj_loss(merged, w_out, x, target):
    L = x.shape[0]
    tm, tn = _tile(L, 512), 1024
    ni, nj = L // tm, D_MODEL // tn

    def body(m_ref, w_ref, x_ref, t_ref, d_ref, db_ref, l_ref):
        out = x_ref[...] + jnp.dot(m_ref[...], w_ref[...], preferred_element_type=F32)
        diff = out - t_ref[...]
        d = diff * (1.0 / D_MODEL)
        d_ref[...] = d
        db_ref[...] = d.astype(BF16)
        l_ref[...] = jnp.full((1, 8, 128), jnp.sum(diff * diff), F32)

    return pl.pallas_call(
        body,
        name="outproj_loss",
        grid=(nj, ni),
        in_specs=[pl.BlockSpec((tm, D_MODEL), lambda j, i: (i, 0)),
                  pl.BlockSpec((D_MODEL, tn), lambda j, i: (0, j)),
                  pl.BlockSpec((tm, tn), lambda j, i: (i, j)),
                  pl.BlockSpec((tm, tn), lambda j, i: (i, j))],
        out_specs=[pl.BlockSpec((tm, tn), lambda j, i: (i, j)), pl.BlockSpec((tm, tn), lambda j, i: (i, j)),
                   pl.BlockSpec((1, 8, 128), lambda j, i: (i * nj + j, 0, 0))],
        out_shape=[jax.ShapeDtypeStruct((L, D_MODEL), F32), jax.ShapeDtypeStruct((L, D_MODEL), BF16),
                   jax.ShapeDtypeStruct((ni * nj, 8, 128), F32)],
        compiler_params=_cp(("parallel", "parallel")),
    )(merged, w_out, x, target)


def _merge_bwd(d_m, og, o, yg, gpre, proj, b_glu, wa, ws):
    L = og.shape[0]
    tm = _tile(L, 256)

    def body(dm_ref, og_ref, o_ref, yg_ref, gp_ref, za0_ref, za1_ref, zs0_ref, zs1_ref, b_ref, wa_ref, ws_ref,
             do_ref, dza_ref, dzs_ref, dg_ref, dyg_ref, gwa_ref, gws_ref, gb_ref):
        i = pl.program_id(0)

        @pl.when(i == 0)
        def _():
            gwa_ref[...] = jnp.zeros_like(gwa_ref)
            gws_ref[...] = jnp.zeros_like(gws_ref)
            gb_ref[...] = jnp.zeros_like(gb_ref)

        za = jnp.concatenate([za0_ref[...], za1_ref[...]], axis=1)
        zs = jnp.concatenate([zs0_ref[...], zs1_ref[...]], axis=1)
        ogv, dma = og_ref[...], dm_ref[:, :ATTN_W]
        ra = lax.rsqrt(jnp.mean(ogv * ogv, axis=-1, keepdims=True) + NORM_EPS)
        xh = ogv * ra
        gwa_ref[...] += jnp.sum(dma * xh, axis=0, keepdims=True)
        gx = dma * wa_ref[...]
        d_og = ra * (gx - xh * jnp.mean(gx * xh, axis=-1, keepdims=True))
        do_ref[...] = d_og * _silu(za)
        dza_ref[...] = (d_og * o_ref[...] * _dsilu(za)).astype(BF16)
        ygv = yg_ref[...]
        sg = _sigmoid(gp_ref[...] + b_ref[...])
        y2 = ygv * sg
        sz = _silu(zs)
        os_ = y2 * sz
        dms = dm_ref[:, ATTN_W:]
        rs = lax.rsqrt(jnp.mean(os_ * os_, axis=-1, keepdims=True) + NORM_EPS)
        xs = os_ * rs
        gws_ref[...] += jnp.sum(dms * xs, axis=0, keepdims=True)
        gxs = dms * ws_ref[...]
        d_os = rs * (gxs - xs * jnp.mean(gxs * xs, axis=-1, keepdims=True))
        dzs_ref[...] = (d_os * y2 * _dsilu(zs)).astype(BF16)
        d_y2 = d_os * sz
        d_g = d_y2 * ygv * sg * (1.0 - sg)
        dg_ref[...] = d_g.astype(BF16)
        gb_ref[...] += jnp.sum(d_g, axis=0, keepdims=True)
        dyg_ref[...] = d_y2 * sg

    row = lambda w: pl.BlockSpec((1, w), lambda i: (0, 0))
    full = lambda w: pl.BlockSpec((tm, w), lambda i: (i, 0))
    half = lambda c: pl.BlockSpec((tm, 512), lambda i: (i, c))
    return pl.pallas_call(
        body,
        name="merge_bwd",
        grid=(L // tm,),
        in_specs=[full(D_MODEL), full(ATTN_W), full(ATTN_W), full(SSM_W), full(SSM_W),
                  half(3), half(4), half(7), half(8), row(SSM_W), row(ATTN_W), row(SSM_W)],
        out_specs=[full(ATTN_W), full(ATTN_W), full(SSM_W), full(SSM_W), full(SSM_W),
                   row(ATTN_W), row(SSM_W), row(SSM_W)],
        out_shape=[jax.ShapeDtypeStruct((L, ATTN_W), F32), jax.ShapeDtypeStruct((L, ATTN_W), BF16),
                   jax.ShapeDtypeStruct((L, SSM_W), BF16), jax.ShapeDtypeStruct((L, SSM_W), BF16),
                   jax.ShapeDtypeStruct((L, SSM_W), F32),
                   jax.ShapeDtypeStruct((1, ATTN_W), F32), jax.ShapeDtypeStruct((1, SSM_W), F32),
                   jax.ShapeDtypeStruct((1, SSM_W), F32)],
        compiler_params=_cp(("arbitrary",)),
    )(d_m, og, o, yg, gpre, proj, proj, proj, proj, b_glu.reshape(1, SSM_W), wa.reshape(1, ATTN_W),
      ws.reshape(1, SSM_W))


def _rms_bwd_x(x, norm_w, d_hn, d_out):
    L = x.shape[0]
    tm = _tile(L, 256)

    def body(x_ref, w_ref, dh_ref, do_ref, gx_ref, gw_ref):
        i = pl.program_id(0)

        @pl.when(i == 0)
        def _():
            gw_ref[...] = jnp.zeros_like(gw_ref)

        xv, dh = x_ref[...], dh_ref[...]
        r = lax.rsqrt(jnp.mean(xv * xv, axis=-1, keepdims=True) + NORM_EPS)
        xh = xv * r
        gw_ref[...] += jnp.sum(dh * xh, axis=0, keepdims=True)
        gx = dh * w_ref[...]
        gx_ref[...] = do_ref[...] + r * (gx - xh * jnp.mean(gx * xh, axis=-1, keepdims=True))

    blk = pl.BlockSpec((tm, D_MODEL), lambda i: (i, 0))
    row = pl.BlockSpec((1, D_MODEL), lambda i: (0, 0))
    return pl.pallas_call(
        body, name="rms_bwd_x", grid=(L // tm,), in_specs=[blk, row, blk, blk], out_specs=[blk, row],
        out_shape=[jax.ShapeDtypeStruct((L, D_MODEL), F32), jax.ShapeDtypeStruct((1, D_MODEL), F32)],
        compiler_params=_cp(("arbitrary",)),
    )(x, norm_w.reshape(1, D_MODEL), d_hn, d_out)


def _rope_table(positions):
    inv_freq = ROPE_THETA ** (-jnp.arange(0, HEAD_DIM, 2, dtype=F32) / HEAD_DIM)
    ang = positions.astype(F32)[:, None] * inv_freq
    c, s = jnp.cos(ang), jnp.sin(ang)
    return jnp.concatenate([c, c, -s, s], axis=1)


def _local_step(x, positions, target, small, wt_in, w_glu, w_out):
    tab = _rope_table(positions)
    ssm_names = ("a_re", "a_im", "log_step", "b_re", "b_im", "c_re", "c_im")
    ops, ops_vjp = jax.vjp(_ssm_ops, *[small[n] for n in ssm_names])
    mt, scat, ocat, a16 = ops
    mt_b, scat_b, ocat_b = mt.astype(BF16), scat.astype(BF16), ocat.astype(BF16)
    perm = _chunk_perm()

    proj, hn = _rms_inproj(x, small["norm_w"], wt_in)
    og, o, lse = _attn_fwd(proj, tab, small["q_norm_w"], small["k_norm_w"], small["sinks"])
    y, yg, hx = _ssm_fwd(proj, perm, mt_b, scat_b, ocat_b, a16, small["d_skip"])
    gpre = _mm(yg, w_glu, "nn", F32, "glu_fwd")
    merged = _merge(og, yg, gpre, proj, small["b_glu"], small["attn_out_norm_w"], small["ssm_out_norm_w"])
    d_out, d_out_b, loss_parts = _outproj_loss(merged, w_out, x, target)
    loss = 0.5 * jnp.sum(loss_parts[:, 0, 0]) / D_MODEL

    g_w_out = _mm(merged, d_out_b, "tn", F32, "grad_w_out")
    d_m = _mm(d_out_b, w_out, "nt", F32, "d_merged")
    d_o, d_za, d_zs, d_g, d_yg1, g_wa, g_ws, g_bglu = _merge_bwd(
        d_m, og, o, yg, gpre, proj, small["b_glu"], small["attn_out_norm_w"], small["ssm_out_norm_w"])
    g_w_glu = _mm(yg, d_g, "tn", F32, "grad_w_glu")
    d_yg = _mm(d_g, w_glu, "nt", F32, "d_yg", add=d_yg1)
    d_u, g_mt, g_scat, g_ocat, g_a16, g_dskip = _ssm_bwd(d_yg, y, proj, hx, perm, mt_b, scat_b, ocat_b, a16,
                                                         small["d_skip"])
    g_ssm = ops_vjp((g_mt, g_scat, g_ocat, g_a16))
    d_qkv, g_qw, g_kw, g_sinks = _attn_bwd(proj, tab, small["q_norm_w"], small["k_norm_w"], small["sinks"],
                                           d_o, o, lse)
    d_proj = jnp.concatenate([d_qkv, d_za, d_u.astype(BF16), d_zs], axis=1)
    g_wt_in = _mm(d_proj, hn, "tn", F32, "grad_w_in")
    d_hn = _mm(d_proj, wt_in, "nn", F32, "d_hn")
    grad_x, g_nw = _rms_bwd_x(x, small["norm_w"], d_hn, d_out)

    g_small = dict(zip(ssm_names, g_ssm))
    g_small.update(norm_w=g_nw.reshape(-1), q_norm_w=g_qw.reshape(-1), k_norm_w=g_kw.reshape(-1),
                   sinks=g_sinks[0, :N_HEADS], d_skip=g_dskip.reshape(-1), b_glu=g_bglu.reshape(-1),
                   attn_out_norm_w=g_wa.reshape(-1), ssm_out_norm_w=g_ws.reshape(-1))
    return loss, grad_x, g_wt_in, g_w_glu, g_w_out, g_small


_ANY = pl.BlockSpec(memory_space=pl.ANY)


def _all_gather_rows(blocks, name):
    n = len(blocks)

    def body(*refs):
        ins, outs = refs[:n], refs[n:2 * n]
        send_sems, recv_sems, local_sems = refs[2 * n:]
        x, y, c = lax.axis_index("x"), lax.axis_index("y"), lax.axis_index("c")
        me, sibling = (x, y, c), (x, y, 1 - c)
        chips = [(1 - x, y), (x, 1 - y), (1 - x, 1 - y)]

        def slot(k, dev):
            return outs[k].at[4 * dev[0] + 2 * dev[1] + dev[2]]

        def copy(k, q, block, to, src=None):
            return pltpu.make_async_remote_copy(
                src_ref=slot(k, block) if src is None else src, dst_ref=slot(k, block),
                send_sem=send_sems.at[k, q], recv_sem=recv_sems.at[k, q], device_id=to, device_id_type=MESH)

        mine = [pltpu.make_async_copy(ins[k], slot(k, me), local_sems.at[k]) for k in range(n)]
        for cp in mine:
            cp.start()
        first = []
        for k in range(n):
            first.append(copy(k, 0, me, sibling, src=ins[k]))
            first += [copy(k, 1 + j, me, (*chip, c), src=ins[k]) for j, chip in enumerate(chips)]
        for cp in first:
            cp.start()
        passed = []
        for j, chip in enumerate(chips):
            for k in range(n):
                copy(k, 1 + j, (*chip, c), me).wait_recv()
                fwd = copy(k, 4 + j, (*chip, c), sibling)
                fwd.start()
                passed.append(fwd)
        for k in range(n):
            copy(k, 0, sibling, me).wait_recv()
            for j, chip in enumerate(chips):
                copy(k, 4 + j, (*chip, 1 - c), me).wait_recv()
        for cp in first + passed:
            cp.wait_send()
        for cp in mine:
            cp.wait()

    outs = pl.pallas_call(
        body,
        name=name,
        in_specs=[_ANY] * n,
        out_specs=[_ANY] * n,
        out_shape=[jax.ShapeDtypeStruct((N_DEV,) + b.shape, b.dtype) for b in blocks],
        scratch_shapes=[pltpu.SemaphoreType.DMA((n, 7)), pltpu.SemaphoreType.DMA((n, 7)),
                        pltpu.SemaphoreType.DMA((n,))],
    )(*blocks)
    return list(outs)


def _pair_exchange(grads, name):
    n = len(grads)

    def body(*refs):
        ins, outs = refs[:n], refs[n:2 * n]
        send_sems, recv_sems = refs[2 * n:]
        x, y, c = lax.axis_index("x"), lax.axis_index("y"), lax.axis_index("c")
        copies = []
        for k in range(n):
            for chip in range(4):
                copies.append(pltpu.make_async_remote_copy(
                    src_ref=ins[k].at[2 * chip + (1 - c)], dst_ref=outs[k].at[chip],
                    send_sem=send_sems.at[k, chip], recv_sem=recv_sems.at[k, chip],
                    device_id=(x, y, 1 - c), device_id_type=MESH))
        for cp in copies:
            cp.start()
        for cp in copies:
            cp.wait()

    outs = pl.pallas_call(
        body,
        name=name,
        in_specs=[_ANY] * n,
        out_specs=[_ANY] * n,
        out_shape=[jax.ShapeDtypeStruct((4,) + g.shape[1:], g.dtype) for g in grads],
        scratch_shapes=[pltpu.SemaphoreType.DMA((n, 4)), pltpu.SemaphoreType.DMA((n, 4))],
    )(*grads)
    return list(outs)


def _pair_sum(g, ra, core, out_dtype, name):
    _, r, C = g.shape
    tr = _tile(r, 128)

    def body(c_ref, g_ref, ra_ref, p_ref):
        p_ref[...] = (g_ref[...] + ra_ref[...]).astype(p_ref.dtype)

    return pl.pallas_call(
        body,
        name=name,
        grid_spec=pltpu.PrefetchScalarGridSpec(
            num_scalar_prefetch=1,
            grid=(4, r // tr),
            in_specs=[pl.BlockSpec((1, tr, C), lambda j, t, c_ref: (2 * j + c_ref[0], t, 0)),
                      pl.BlockSpec((1, tr, C), lambda j, t, c_ref: (j, t, 0))],
            out_specs=pl.BlockSpec((1, tr, C), lambda j, t, c_ref: (j, t, 0)),
        ),
        out_shape=jax.ShapeDtypeStruct((4, r, C), out_dtype),
        compiler_params=_cp(("parallel", "parallel")),
    )(core, g, ra)


def _chip_exchange(parts, name):
    n = len(parts)

    def body(*refs):
        ins, outs = refs[:n], refs[n:2 * n]
        send_sems, recv_sems = refs[2 * n:]
        x, y, c = lax.axis_index("x"), lax.axis_index("y"), lax.axis_index("c")
        chips = [(1 - x, y), (x, 1 - y), (1 - x, 1 - y)]
        copies = []
        for k in range(n):
            for q, chip in enumerate(chips):
                copies.append(pltpu.make_async_remote_copy(
                    src_ref=ins[k].at[2 * chip[0] + chip[1]], dst_ref=outs[k].at[q],
                    send_sem=send_sems.at[k, q], recv_sem=recv_sems.at[k, q],
                    device_id=(*chip, c), device_id_type=MESH))
        for cp in copies:
            cp.start()
        for cp in copies:
            cp.wait()

    outs = pl.pallas_call(
        body,
        name=name,
        in_specs=[_ANY] * n,
        out_specs=[_ANY] * n,
        out_shape=[jax.ShapeDtypeStruct((3,) + p.shape[1:], p.dtype) for p in parts],
        scratch_shapes=[pltpu.SemaphoreType.DMA((n, 3)), pltpu.SemaphoreType.DMA((n, 3))],
    )(*parts)
    return list(outs)


def _chip_sum(p, rb, chip, name):
    _, r, C = p.shape
    tr = _tile(r, 128)

    def body(c_ref, p_ref, rb_ref, o_ref):
        acc = p_ref[0].astype(F32) + rb_ref[0].astype(F32)
        acc = acc + rb_ref[1].astype(F32)
        o_ref[...] = acc + rb_ref[2].astype(F32)

    return pl.pallas_call(
        body,
        name=name,
        grid_spec=pltpu.PrefetchScalarGridSpec(
            num_scalar_prefetch=1,
            grid=(r // tr,),
            in_specs=[pl.BlockSpec((1, tr, C), lambda t, c_ref: (c_ref[0], t, 0)),
                      pl.BlockSpec((3, tr, C), lambda t, c_ref: (0, t, 0))],
            out_specs=pl.BlockSpec((tr, C), lambda t, c_ref: (t, 0)),
        ),
        out_shape=jax.ShapeDtypeStruct((r, C), F32),
        compiler_params=_cp(("parallel",)),
    )(chip, p, rb)


def _adamw(g, w, m, v, name):
    R, C = g.shape
    tr = _tile(R, 256)
    c1 = 1.0 - ADAM_B1 ** ADAM_STEP
    c2 = 1.0 - ADAM_B2 ** ADAM_STEP

    def body(g_ref, w_ref, m_ref, v_ref, d_ref, nm_ref, nv_ref):
        gv = g_ref[...]
        nm = ADAM_B1 * m_ref[...] + (1.0 - ADAM_B1) * gv
        nv = ADAM_B2 * v_ref[...] + (1.0 - ADAM_B2) * (gv * gv)
        nm_ref[...] = nm
        nv_ref[...] = nv
        d_ref[...] = -ADAM_LR * ((nm / c1) / (jnp.sqrt(nv / c2) + ADAM_EPS) + ADAM_WD * w_ref[...])

    blk = pl.BlockSpec((tr, C), lambda i: (i, 0))
    return pl.pallas_call(
        body, name=name, grid=(R // tr,), in_specs=[blk] * 4, out_specs=[blk] * 3,
        out_shape=[jax.ShapeDtypeStruct((R, C), F32)] * 3, compiler_params=_cp(("parallel",)),
    )(g, w, m, v)


_SMALL = ("norm_w", "q_norm_w", "k_norm_w", "sinks", "a_re", "a_im", "log_step", "b_re", "b_im", "c_re", "c_im",
          "d_skip", "b_glu", "attn_out_norm_w", "ssm_out_norm_w")
_WEIGHTS = ("norm_w", "w_in", "q_norm_w", "k_norm_w", "sinks", "a_re", "a_im", "log_step", "b_re", "b_im", "c_re",
            "c_im", "d_skip", "w_glu", "b_glu", "attn_out_norm_w", "ssm_out_norm_w", "w_out")
_PACK_ROWS = 2176


def _pack(d):
    flat = jnp.concatenate([d[n].reshape(-1).astype(F32) for n in _SMALL])
    return jnp.pad(flat, (0, _PACK_ROWS * 128 - flat.shape[0])).reshape(_PACK_ROWS, 128)


def _unpack(packed, like):
    flat = packed.reshape(-1)
    out, off = {}, 0
    for n in _SMALL:
        size = math.prod(like[n].shape)
        out[n] = flat[off:off + size].reshape(like[n].shape)
        off += size
    return out


def kernel(x, positions, norm_w, w_in, q_norm_w, k_norm_w, sinks, a_re, a_im, log_step, b_re, b_im, c_re, c_im, d_skip, w_glu, b_glu, attn_out_norm_w, ssm_out_norm_w, w_out, loss_target, m_norm_w, m_w_in, m_q_norm_w, m_k_norm_w, m_sinks, m_a_re, m_a_im, m_log_step, m_b_re, m_b_im, m_c_re, m_c_im, m_d_skip, m_w_glu, m_b_glu, m_attn_out_norm_w, m_ssm_out_norm_w, m_w_out, v_norm_w, v_w_in, v_q_norm_w, v_k_norm_w, v_sinks, v_a_re, v_a_im, v_log_step, v_b_re, v_b_im, v_c_re, v_c_im, v_d_skip, v_w_glu, v_b_glu, v_attn_out_norm_w, v_ssm_out_norm_w, v_w_out):
    w = dict(norm_w=norm_w, w_in=w_in, q_norm_w=q_norm_w, k_norm_w=k_norm_w, sinks=sinks, a_re=a_re, a_im=a_im,
             log_step=log_step, b_re=b_re, b_im=b_im, c_re=c_re, c_im=c_im, d_skip=d_skip, w_glu=w_glu, b_glu=b_glu,
             attn_out_norm_w=attn_out_norm_w, ssm_out_norm_w=ssm_out_norm_w, w_out=w_out)
    m = dict(norm_w=m_norm_w, w_in=m_w_in, q_norm_w=m_q_norm_w, k_norm_w=m_k_norm_w, sinks=m_sinks, a_re=m_a_re,
             a_im=m_a_im, log_step=m_log_step, b_re=m_b_re, b_im=m_b_im, c_re=m_c_re, c_im=m_c_im, d_skip=m_d_skip,
             w_glu=m_w_glu, b_glu=m_b_glu, attn_out_norm_w=m_attn_out_norm_w, ssm_out_norm_w=m_ssm_out_norm_w,
             w_out=m_w_out)
    v = dict(norm_w=v_norm_w, w_in=v_w_in, q_norm_w=v_q_norm_w, k_norm_w=v_k_norm_w, sinks=v_sinks, a_re=v_a_re,
             a_im=v_a_im, log_step=v_log_step, b_re=v_b_re, b_im=v_b_im, c_re=v_c_re, c_im=v_c_im, d_skip=v_d_skip,
             w_glu=v_w_glu, b_glu=v_b_glu, attn_out_norm_w=v_attn_out_norm_w, ssm_out_norm_w=v_ssm_out_norm_w,
             w_out=v_w_out)
    core = lax.axis_index("c").astype(jnp.int32).reshape(1)
    chip = (2 * lax.axis_index("x") + lax.axis_index("y")).astype(jnp.int32).reshape(1)

    wt_in, wf_glu, wf_out = _all_gather_rows(
        [w_in.T.astype(BF16), w_glu.astype(BF16), w_out.astype(BF16)], "gather_weights")
    wt_in = wt_in.reshape(IN_W, D_MODEL)
    wf_glu = wf_glu.reshape(SSM_W, SSM_W)
    wf_out = wf_out.reshape(D_MODEL, D_MODEL)

    small = {n: w[n] for n in _SMALL}
    loss, grad_x, g_wt_in, g_w_glu, g_w_out, g_small = _local_step(
        x[0], positions[0], loss_target[0], small, wt_in, wf_glu, wf_out)
    loss = lax.psum(loss, ("x", "y", "c"))

    full = [g_wt_in.reshape(N_DEV, IN_W // N_DEV, D_MODEL), g_w_glu.reshape(N_DEV, SSM_W // N_DEV, SSM_W),
            g_w_out.reshape(N_DEV, D_MODEL // N_DEV, D_MODEL), _pack(g_small).reshape(N_DEV, _PACK_ROWS // N_DEV, 128)]
    from_sibling = _pair_exchange(full, "pair_exchange")
    wire = (BF16, BF16, BF16, F32)
    parts = [_pair_sum(g, ra, core, dt, f"pair_sum_{k}") for k, (g, ra, dt) in enumerate(zip(full, from_sibling, wire))]
    from_chips = _chip_exchange(parts, "chip_exchange")
    red = [_chip_sum(p, rb, chip, f"chip_sum_{k}") for k, (p, rb) in enumerate(zip(parts, from_chips))]
    g_in, g_glu, g_out = red[0].T, red[1], red[2]
    (g_packed,) = _all_gather_rows([red[3]], "gather_small")
    g_packed = g_packed.reshape(_PACK_ROWS, 128)

    grads = _unpack(g_packed, w)
    grads.update(w_in=g_in, w_glu=g_glu, w_out=g_out)
    delta, new_m, new_v = {}, {}, {}
    for n in ("w_in", "w_glu", "w_out"):
        delta[n], new_m[n], new_v[n] = _adamw(grads[n], w[n], m[n], v[n], f"adamw_{n}")
    d_p, m_p, v_p = _adamw(g_packed, _pack(w), _pack(m), _pack(v), "adamw_small")
    delta.update(_unpack(d_p, w))
    new_m.update(_unpack(m_p, w))
    new_v.update(_unpack(v_p, w))

    return (loss, grad_x[None], *[grads[n] for n in _WEIGHTS], *[delta[n] for n in _WEIGHTS],
            *[new_m[n] for n in _WEIGHTS], *[new_v[n] for n in _WEIGHTS])
```

```python
import functools
import math

import jax
import jax.numpy as jnp
from jax import lax
from jax.experimental import pallas as pl
from jax.experimental.pallas import tpu as pltpu

F32 = jnp.float32
BF16 = jnp.bfloat16

D_MODEL = 2048
ATTN_W = 1024
KV_W = 256
SSM_W = 1024
HEAD_DIM = 64
N_HEADS = 16
N_KV = 4
KV_REP = 4
IN_W = 4608
BLOCK = 128
ROPE_THETA = 10000.0
NORM_EPS = 1e-6
SSM_G = 64
SSM_P = 64
SSM_H = 16
CHUNK = 16
CW = CHUNK * SSM_H
N_DEV = 8

ADAM_LR = 0.001
ADAM_B1 = 0.9
ADAM_B2 = 0.999
ADAM_EPS = 1e-08
ADAM_WD = 0.01
ADAM_STEP = 10

VMEM_LIMIT = 56 * 1024 * 1024
MESH = pl.DeviceIdType.MESH


def _cp(sem=None):
    if sem is None:
        return pltpu.CompilerParams(vmem_limit_bytes=VMEM_LIMIT)
    return pltpu.CompilerParams(vmem_limit_bytes=VMEM_LIMIT, dimension_semantics=sem)


def _sigmoid(x):
    return 1.0 / (1.0 + jnp.exp(-x))


def _silu(x):
    return x * _sigmoid(x)


def _dsilu(x):
    s = _sigmoid(x)
    return s * (1.0 + x * (1.0 - s))


_GELU_C = math.sqrt(2.0 / math.pi)


def _gelu(y):
    t = jnp.tanh(_GELU_C * (y + 0.044715 * y * y * y))
    return 0.5 * y * (1.0 + t)


def _dgelu(y):
    t = jnp.tanh(_GELU_C * (y + 0.044715 * y * y * y))
    return 0.5 * (1.0 + t) + 0.5 * y * (1.0 - t * t) * _GELU_C * (1.0 + 3.0 * 0.044715 * y * y)


def _tile(n, want):
    if n <= want:
        return n
    for t in range(want - want % 16, 0, -16):
        if n % t == 0:
            return t
    raise ValueError((n, want))


def _mm(a, b, mode, out_dtype, name, tm=512, tn=1024, add=None):
    if mode == "nn":
        (M, K), (K2, N) = a.shape, b.shape
    elif mode == "nt":
        (M, K), (N, K2) = a.shape, b.shape
    else:
        (K, M), (K2, N) = a.shape, b.shape
    assert K == K2
    tm, tn = _tile(M, tm), _tile(N, tn)
    dn = {"nn": _NN, "nt": _NT, "tn": _TN}[mode]

    def body(a_ref, b_ref, *rest):
        o_ref = rest[-1]
        acc = lax.dot_general(a_ref[...].astype(BF16), b_ref[...].astype(BF16), dn, preferred_element_type=F32)
        if add is not None:
            acc = acc + rest[0][...]
        o_ref[...] = acc.astype(o_ref.dtype)

    a_spec = pl.BlockSpec((K, tm), lambda j, i: (0, i)) if mode == "tn" else pl.BlockSpec((tm, K), lambda j, i: (i, 0))
    b_spec = pl.BlockSpec((tn, K), lambda j, i: (j, 0)) if mode == "nt" else pl.BlockSpec((K, tn), lambda j, i: (0, j))
    o_spec = pl.BlockSpec((tm, tn), lambda j, i: (i, j))
    extra = () if add is None else (add,)
    return pl.pallas_call(
        body,
        name=name,
        grid=(N // tn, M // tm),
        in_specs=[a_spec, b_spec] + [o_spec] * len(extra),
        out_specs=o_spec,
        out_shape=jax.ShapeDtypeStruct((M, N), out_dtype),
        compiler_params=_cp(("parallel", "parallel")),
    )(a, b, *extra)


def _rms_inproj(x, norm_w, wt_in):
    L = x.shape[0]
    tm, tn = _tile(L, 1024), 768
    nj = IN_W // tn

    def body(x_ref, w_ref, wt_ref, proj_ref, hn_ref, hn_scr):
        j = pl.program_id(1)

        @pl.when(j == 0)
        def _():
            xv = x_ref[...]
            r = lax.rsqrt(jnp.mean(xv * xv, axis=-1, keepdims=True) + NORM_EPS)
            hn = (xv * r * w_ref[...]).astype(BF16)
            hn_scr[...] = hn
            hn_ref[...] = hn

        proj_ref[...] = lax.dot_general(hn_scr[...], wt_ref[...], (((1,), (1,)), ((), ())),
                                        preferred_element_type=F32)

    return pl.pallas_call(
        body,
        name="rms_inproj",
        grid=(L // tm, nj),
        in_specs=[pl.BlockSpec((tm, D_MODEL), lambda i, j: (i, 0)),
                  pl.BlockSpec((1, D_MODEL), lambda i, j: (0, 0)),
                  pl.BlockSpec((tn, D_MODEL), lambda i, j: (j, 0))],
        out_specs=[pl.BlockSpec((tm, tn), lambda i, j: (i, j)),
                   pl.BlockSpec((tm, D_MODEL), lambda i, j: (i, 0))],
        out_shape=[jax.ShapeDtypeStruct((L, IN_W), F32), jax.ShapeDtypeStruct((L, D_MODEL), BF16)],
        scratch_shapes=[pltpu.VMEM((tm, D_MODEL), BF16)],
        compiler_params=_cp(("parallel", "arbitrary")),
    )(x, norm_w.reshape(1, D_MODEL), wt_in)


def _seg_sum(v):
    a = lax.broadcasted_iota(jnp.int32, (128, 128), 0) // HEAD_DIM
    b = lax.broadcasted_iota(jnp.int32, (128, 128), 1) // HEAD_DIM
    ones = jnp.where(a == b, 1.0, 0.0).astype(BF16)
    hi = v.astype(BF16)
    lo = (v - hi.astype(F32)).astype(BF16)
    return jnp.dot(hi, ones, preferred_element_type=F32) + jnp.dot(lo, ones, preferred_element_type=F32)


def _rot_half(t):
    lane = lax.broadcasted_iota(jnp.int32, t.shape, 1)
    return jnp.where(lane % HEAD_DIM < HEAD_DIM // 2, pltpu.roll(t, 128 - HEAD_DIM // 2, 1),
                     pltpu.roll(t, HEAD_DIM // 2, 1))


def _norm_rope(raw, w, cos, sin):
    r = lax.rsqrt(_seg_sum(raw * raw) * (1.0 / HEAD_DIM) + NORM_EPS)
    tn = raw * r * w
    return r, tn * cos + _rot_half(tn) * sin


def _norm_rope_bwd(d_rot, raw, w, cos, sin):
    r = lax.rsqrt(_seg_sum(raw * raw) * (1.0 / HEAD_DIM) + NORM_EPS)
    d_tn = d_rot * cos + _rot_half(d_rot * sin)
    xh = raw * r
    gw = d_tn * w
    d_raw = r * (gw - xh * (_seg_sum(gw * xh) * (1.0 / HEAD_DIM)))
    return d_raw, d_tn * xh


def _band_mask2(has_prev):
    qi = lax.broadcasted_iota(jnp.int32, (2 * BLOCK, 2 * BLOCK), 0) % BLOCK + BLOCK
    kj = lax.broadcasted_iota(jnp.int32, (2 * BLOCK, 2 * BLOCK), 1)
    rel = qi - kj
    return (rel >= 0) & (rel < BLOCK) & ((kj >= BLOCK) | has_prev)


def _half_tiles(pair):
    lo = lax.broadcasted_iota(jnp.int32, pair.shape, 1) < HEAD_DIM
    sw = pltpu.roll(pair, HEAD_DIM, 1)
    z = jnp.zeros_like(pair)
    return (jnp.where(lo, pair, z).astype(BF16), jnp.where(lo, z, sw).astype(BF16),
            jnp.where(lo, sw, z).astype(BF16), jnp.where(lo, z, pair).astype(BF16))


def _two_rows(top, bottom):
    row = lax.broadcasted_iota(jnp.int32, (2 * BLOCK, 1), 0)
    return jnp.where(row < BLOCK, top, bottom)


def _lane_col(mat, h):
    lane = lax.broadcasted_iota(jnp.int32, mat.shape, 1)
    return jnp.sum(jnp.where(lane == h, mat, 0.0), axis=1, keepdims=True)


_SCALE = 1.0 / math.sqrt(HEAD_DIM)
_NT = (((1,), (1,)), ((), ()))
_NN = (((1,), (0,)), ((), ()))
_TN = (((0,), (0,)), ((), ()))


def _qk_prep(proj, tab, qw, kw):
    L = proj.shape[0]
    tm = _tile(L, 512)

    def body(q_ref, k_ref, t_ref, qw_ref, kw_ref, qo_ref, ko_ref):
        cos, sin = t_ref[:, :128], t_ref[:, 128:]
        for c in range(ATTN_W // 128):
            _, qr = _norm_rope(q_ref[:, c * 128:(c + 1) * 128], qw_ref[...], cos, sin)
            qo_ref[:, c * 128:(c + 1) * 128] = (qr * _SCALE).astype(BF16)
        for c in range(KV_W // 128):
            _, kr = _norm_rope(k_ref[:, c * 128:(c + 1) * 128], kw_ref[...], cos, sin)
            ko_ref[:, c * 128:(c + 1) * 128] = kr.astype(BF16)

    row = pl.BlockSpec((1, 128), lambda i: (0, 0))
    return pl.pallas_call(
        body,
        name="qk_prep",
        grid=(L // tm,),
        in_specs=[pl.BlockSpec((tm, ATTN_W), lambda i: (i, 0)), pl.BlockSpec((tm, KV_W), lambda i: (i, 4)),
                  pl.BlockSpec((tm, 256), lambda i: (i, 0)), row, row],
        out_specs=[pl.BlockSpec((tm, ATTN_W), lambda i: (i, 0)), pl.BlockSpec((tm, KV_W), lambda i: (i, 0))],
        out_shape=[jax.ShapeDtypeStruct((L, ATTN_W), BF16), jax.ShapeDtypeStruct((L, KV_W), BF16)],
        compiler_params=_cp(("parallel",)),
    )(proj, proj, tab, jnp.tile(qw, 2).reshape(1, 128), jnp.tile(kw, 2).reshape(1, 128))


def _group_tiles(g, kt, vt):
    a, b = divmod(g, 2)
    return kt[a][2 * b], kt[a][2 * b + 1], vt[a][2 * b], vt[a][2 * b + 1]


def _attn_fwd(q, k, proj, sinks):
    L = proj.shape[0]
    nb = L // BLOCK

    def body(q_ref, kc_ref, kp_ref, vc_ref, vp_ref, z0_ref, z1_ref, sink_ref, og_ref, o_ref, lse_ref):
        i = pl.program_id(0)
        mask = _band_mask2(i > 0)
        z = jnp.concatenate([z0_ref[...], z1_ref[...]], axis=1)
        lane = lax.broadcasted_iota(jnp.int32, (BLOCK, 128), 1)
        kt = [_half_tiles(jnp.concatenate([kp_ref[:, a * 128:(a + 1) * 128], kc_ref[:, a * 128:(a + 1) * 128]],
                                          axis=0).astype(F32)) for a in range(2)]
        vt = [_half_tiles(jnp.concatenate([vp_ref[:, a * 128:(a + 1) * 128], vc_ref[:, a * 128:(a + 1) * 128]],
                                          axis=0)) for a in range(2)]
        lse_mat = jnp.zeros((BLOCK, 128), F32)
        outs = []
        for g in range(N_KV):
            k_lo, k_hi, v_lo, v_hi = _group_tiles(g, kt, vt)
            q2 = jnp.concatenate([q_ref[:, 2 * g * 128:(2 * g + 1) * 128],
                                  q_ref[:, (2 * g + 1) * 128:(2 * g + 2) * 128]], axis=0)
            acc = jnp.zeros((2 * BLOCK, 128), F32)
            for half, (kh, vh) in enumerate(((k_lo, v_lo), (k_hi, v_hi))):
                h_top, h_bot = 4 * g + half, 4 * g + 2 + half
                s = jnp.where(mask, lax.dot_general(q2, kh, _NT, preferred_element_type=F32), -1e30)
                sink = _two_rows(sink_ref[h_top], sink_ref[h_bot])
                m = jnp.maximum(jnp.max(s, axis=-1, keepdims=True), sink)
                e = jnp.exp(s - m)
                den = jnp.sum(e, axis=-1, keepdims=True) + jnp.exp(sink - m)
                p = e / den
                acc = acc + jnp.dot(p.astype(BF16), vh, preferred_element_type=F32)
                lse = m + jnp.log(den)
                lse_mat = jnp.where(lane == h_top, lse[:BLOCK], lse_mat)
                lse_mat = jnp.where(lane == h_bot, lse[BLOCK:], lse_mat)
            outs += [acc[:BLOCK], acc[BLOCK:]]
        o = jnp.concatenate(outs, axis=1)
        o_ref[...] = o
        og_ref[...] = o * _silu(z)
        lse_ref[...] = lse_mat

    prev = lambda i: jnp.maximum(i - 1, 0)
    return pl.pallas_call(
        body,
        name="attn_fwd",
        grid=(nb,),
        in_specs=[pl.BlockSpec((BLOCK, ATTN_W), lambda i: (i, 0)),
                  pl.BlockSpec((BLOCK, KV_W), lambda i: (i, 0)),
                  pl.BlockSpec((BLOCK, KV_W), lambda i: (prev(i), 0)),
                  pl.BlockSpec((BLOCK, KV_W), lambda i: (i, 5)),
                  pl.BlockSpec((BLOCK, KV_W), lambda i: (prev(i), 5)),
                  pl.BlockSpec((BLOCK, 512), lambda i: (i, 3)),
                  pl.BlockSpec((BLOCK, 512), lambda i: (i, 4)),
                  pl.BlockSpec(memory_space=pltpu.SMEM)],
        out_specs=[pl.BlockSpec((BLOCK, ATTN_W), lambda i: (i, 0)),
                   pl.BlockSpec((BLOCK, ATTN_W), lambda i: (i, 0)),
                   pl.BlockSpec((BLOCK, 128), lambda i: (i, 0))],
        out_shape=[jax.ShapeDtypeStruct((L, ATTN_W), F32), jax.ShapeDtypeStruct((L, ATTN_W), F32),
                   jax.ShapeDtypeStruct((L, 128), F32)],
        compiler_params=_cp(("parallel",)),
    )(q, k, k, proj, proj, proj, proj, sinks)


def _attn_bwd(q, k, proj, sinks, d_o, o, lse):
    L = proj.shape[0]
    nb = L // BLOCK

    def body(q_ref, kc_ref, kp_ref, vc_ref, vp_ref, do_ref, o_ref, lse_ref, sink_ref,
             dq_ref, dk_ref, dv_ref, gs_ref, ck_scr, cv_scr):
        i = pl.program_id(0)

        @pl.when(i == 0)
        def _():
            gs_ref[...] = jnp.zeros_like(gs_ref)
            ck_scr[...] = jnp.zeros_like(ck_scr)
            cv_scr[...] = jnp.zeros_like(cv_scr)

        @pl.when(i == nb)
        def _():
            dk_ref[...] = ck_scr[...]
            dv_ref[...] = cv_scr[...]

        @pl.when(i < nb)
        def _():
            mask = _band_mask2(i > 0)
            lane = lax.broadcasted_iota(jnp.int32, (1, 128), 1)
            lo = lax.broadcasted_iota(jnp.int32, (2 * BLOCK, 128), 1) < HEAD_DIM
            lse_c = lse_ref[...]
            kt = [_half_tiles(jnp.concatenate([kp_ref[:, a * 128:(a + 1) * 128], kc_ref[:, a * 128:(a + 1) * 128]],
                                              axis=0).astype(F32)) for a in range(2)]
            vt = [_half_tiles(jnp.concatenate([vp_ref[:, a * 128:(a + 1) * 128], vc_ref[:, a * 128:(a + 1) * 128]],
                                              axis=0)) for a in range(2)]
            gs = jnp.zeros((1, 128), F32)
            dq_parts = []
            dk_acc = [jnp.zeros((2 * BLOCK, 128), F32) for _ in range(2)]
            dv_acc = [jnp.zeros((2 * BLOCK, 128), F32) for _ in range(2)]
            for g in range(N_KV):
                a, b = divmod(g, 2)
                k_lo, k_hi, v_lo, v_hi = _group_tiles(g, kt, vt)
                t0, t1 = slice(2 * g * 128, (2 * g + 1) * 128), slice((2 * g + 1) * 128, (2 * g + 2) * 128)
                q2 = jnp.concatenate([q_ref[:, t0], q_ref[:, t1]], axis=0)
                do2 = jnp.concatenate([do_ref[:, t0], do_ref[:, t1]], axis=0)
                prod = do2 * jnp.concatenate([o_ref[:, t0], o_ref[:, t1]], axis=0)
                do2_b = do2.astype(BF16)
                dq2 = jnp.zeros((2 * BLOCK, 128), F32)
                dk_h, dv_h = [], []
                for half, (kh, vh) in enumerate(((k_lo, v_lo), (k_hi, v_hi))):
                    h_top, h_bot = 4 * g + half, 4 * g + 2 + half
                    lse = jnp.concatenate([_lane_col(lse_c, h_top), _lane_col(lse_c, h_bot)], axis=0)
                    sink = _two_rows(sink_ref[h_top], sink_ref[h_bot])
                    delta = jnp.sum(jnp.where(lo == (half == 0), prod, 0.0), axis=1, keepdims=True)
                    s = jnp.where(mask, lax.dot_general(q2, kh, _NT, preferred_element_type=F32), -1e30)
                    p = jnp.exp(s - lse)
                    dp = lax.dot_general(do2_b, vh, _NT, preferred_element_type=F32)
                    ds_b = (p * (dp - delta)).astype(BF16)
                    p_b = p.astype(BF16)
                    dq2 = dq2 + jnp.dot(ds_b, kh, preferred_element_type=F32)
                    dk_h.append(lax.dot_general(ds_b, q2, _TN, preferred_element_type=F32))
                    dv_h.append(lax.dot_general(p_b, do2_b, _TN, preferred_element_type=F32))
                    gsink = -jnp.exp(sink - lse) * delta
                    row = lax.broadcasted_iota(jnp.int32, (2 * BLOCK, 1), 0)
                    gs = gs + jnp.where(lane == h_top, jnp.sum(jnp.where(row < BLOCK, gsink, 0.0)), 0.0)
                    gs = gs + jnp.where(lane == h_bot, jnp.sum(jnp.where(row >= BLOCK, gsink, 0.0)), 0.0)
                dq_parts += [dq2[:BLOCK], dq2[BLOCK:]]
                for acc, parts in ((dk_acc, dk_h), (dv_acc, dv_h)):
                    t = jnp.where(lo, parts[0], parts[1])
                    t = t + pltpu.roll(t, HEAD_DIM, 1)
                    acc[a] = acc[a] + jnp.where(lo == (b == 0), t, 0.0)
            dq_ref[...] = jnp.concatenate(dq_parts, axis=1)
            dk_full = jnp.concatenate(dk_acc, axis=1)
            dv_full = jnp.concatenate(dv_acc, axis=1)
            dk_ref[...] = ck_scr[...] + dk_full[:BLOCK]
            dv_ref[...] = cv_scr[...] + dv_full[:BLOCK]
            ck_scr[...] = dk_full[BLOCK:]
            cv_scr[...] = dv_full[BLOCK:]
            gs_ref[...] += gs

    cur = lambda i: jnp.minimum(i, nb - 1)
    prev = lambda i: jnp.maximum(jnp.minimum(i, nb - 1) - 1, 0)
    done = lambda i: jnp.maximum(i - 1, 0)
    bs = pl.BlockSpec
    return pl.pallas_call(
        body,
        name="attn_bwd",
        grid=(nb + 1,),
        in_specs=[bs((BLOCK, ATTN_W), lambda i: (cur(i), 0)),
                  bs((BLOCK, KV_W), lambda i: (cur(i), 0)), bs((BLOCK, KV_W), lambda i: (prev(i), 0)),
                  bs((BLOCK, KV_W), lambda i: (cur(i), 5)), bs((BLOCK, KV_W), lambda i: (prev(i), 5)),
                  bs((BLOCK, ATTN_W), lambda i: (cur(i), 0)), bs((BLOCK, ATTN_W), lambda i: (cur(i), 0)),
                  bs((BLOCK, 128), lambda i: (cur(i), 0)), bs(memory_space=pltpu.SMEM)],
        out_specs=[bs((BLOCK, ATTN_W), lambda i: (cur(i), 0)),
                   bs((BLOCK, KV_W), lambda i: (done(i), 0)), bs((BLOCK, KV_W), lambda i: (done(i), 0)),
                   bs((1, 128), lambda i: (0, 0))],
        out_shape=[jax.ShapeDtypeStruct((L, ATTN_W), F32), jax.ShapeDtypeStruct((L, KV_W), F32),
                   jax.ShapeDtypeStruct((L, KV_W), F32), jax.ShapeDtypeStruct((1, 128), F32)],
        scratch_shapes=[pltpu.VMEM((BLOCK, KV_W), F32), pltpu.VMEM((BLOCK, KV_W), F32)],
        compiler_params=_cp(("arbitrary",)),
    )(q, k, k, proj, proj, d_o, o, lse, sinks)


def _qk_prep_bwd(proj, tab, qw, kw, d_q, d_k, d_v):
    L = proj.shape[0]
    tm = _tile(L, 512)

    def body(q_ref, k_ref, t_ref, qw_ref, kw_ref, dq_ref, dk_ref, dv_ref, out_ref, gq_ref, gk_ref):
        i = pl.program_id(0)

        @pl.when(i == 0)
        def _():
            gq_ref[...] = jnp.zeros_like(gq_ref)
            gk_ref[...] = jnp.zeros_like(gk_ref)

        cos, sin = t_ref[:, :128], t_ref[:, 128:]
        gq = jnp.zeros((1, 128), F32)
        gk = jnp.zeros((1, 128), F32)
        for c in range(ATTN_W // 128):
            cs = slice(c * 128, (c + 1) * 128)
            d_raw, gw = _norm_rope_bwd(dq_ref[:, cs] * _SCALE, q_ref[:, cs], qw_ref[...], cos, sin)
            out_ref[:, cs] = d_raw.astype(BF16)
            gq = gq + jnp.sum(gw, axis=0, keepdims=True)
        for c in range(KV_W // 128):
            cs = slice(c * 128, (c + 1) * 128)
            d_raw, gw = _norm_rope_bwd(dk_ref[:, cs], k_ref[:, cs], kw_ref[...], cos, sin)
            out_ref[:, ATTN_W + c * 128:ATTN_W + (c + 1) * 128] = d_raw.astype(BF16)
            gk = gk + jnp.sum(gw, axis=0, keepdims=True)
        out_ref[:, ATTN_W + KV_W:] = dv_ref[...].astype(BF16)
        gq_ref[...] += gq
        gk_ref[...] += gk

    row = pl.BlockSpec((1, 128), lambda i: (0, 0))
    blk = lambda w, c: pl.BlockSpec((tm, w), lambda i: (i, c))
    return pl.pallas_call(
        body,
        name="qk_prep_bwd",
        grid=(L // tm,),
        in_specs=[blk(ATTN_W, 0), blk(KV_W, 4), blk(256, 0), row, row, blk(ATTN_W, 0), blk(KV_W, 0), blk(KV_W, 0)],
        out_specs=[blk(ATTN_W + 2 * KV_W, 0), row, row],
        out_shape=[jax.ShapeDtypeStruct((L, ATTN_W + 2 * KV_W), BF16), jax.ShapeDtypeStruct((1, 128), F32),
                   jax.ShapeDtypeStruct((1, 128), F32)],
        compiler_params=_cp(("arbitrary",)),
    )(proj, proj, tab, jnp.tile(qw, 2).reshape(1, 128), jnp.tile(kw, 2).reshape(1, 128), d_q, d_k, d_v)


def _ssm_ops(a_re, a_im, log_step, b_re, b_im, c_re, c_im):
    hp = lax.Precision.HIGHEST
    delta = jnp.exp(log_step)[:, None]
    xr, xi = a_re * delta, a_im * delta
    er = jnp.exp(xr)
    lbr, lbi = er * jnp.cos(xi), er * jnp.sin(xi)
    nr, ni = lbr - 1.0, lbi
    den = a_re * a_re + a_im * a_im
    cr, ci = (nr * a_re + ni * a_im) / den, (ni * a_re - nr * a_im) / den
    bbr = cr[..., None] * b_re - ci[..., None] * b_im
    bbi = cr[..., None] * b_im + ci[..., None] * b_re
    pr, pi = [jnp.ones_like(lbr)], [jnp.zeros_like(lbr)]
    for _ in range(CHUNK):
        pr.append(pr[-1] * lbr - pi[-1] * lbi)
        pi.append(pr[-2] * lbi + pi[-1] * lbr)
    pr, pi = jnp.stack(pr, axis=1), jnp.stack(pi, axis=1)
    p0r, p0i = pr[:, :CHUNK, None, :], pi[:, :CHUNK, None, :]
    clr = c_re[:, None] * p0r - c_im[:, None] * p0i
    cli = c_re[:, None] * p0i + c_im[:, None] * p0r
    clcat = jnp.concatenate([clr, -cli], axis=-1).reshape(SSM_G, CW, 2 * SSM_P)
    kk = jnp.einsum("gmp,gpk->gmk", clcat, jnp.concatenate([bbr, bbi], axis=1), precision=hp)
    kl = kk.reshape(SSM_G, CHUNK, SSM_H, SSM_H).transpose(0, 3, 2, 1)
    wide = jnp.concatenate([jnp.zeros_like(kl), kl, jnp.zeros_like(kl[..., :1])], axis=-1)
    skew = jnp.tile(wide, (1, 1, 1, CHUNK))[..., :2 * CHUNK * CHUNK]
    skew = skew.reshape(SSM_G, SSM_H, SSM_H, CHUNK, 2 * CHUNK)[..., CHUNK:]
    mt = skew.transpose(0, 3, 1, 4, 2).reshape(SSM_G, CW, CW)
    rr, ri = pr[:, CHUNK - 1::-1][:, :, None, :], pi[:, CHUNK - 1::-1][:, :, None, :]
    bbrt, bbit = bbr.transpose(0, 2, 1)[:, None], bbi.transpose(0, 2, 1)[:, None]
    sr = (rr * bbrt - ri * bbit).reshape(SSM_G, CW, SSM_P)
    si = (rr * bbit + ri * bbrt).reshape(SSM_G, CW, SSM_P)
    scat = jnp.concatenate([sr, si], axis=-1)
    ctr, cti = c_re.transpose(0, 2, 1), c_im.transpose(0, 2, 1)
    p1r, p1i = pr[:, 1:].transpose(0, 2, 1), pi[:, 1:].transpose(0, 2, 1)
    o_r = (ctr[:, :, None, :] * p1r[..., None] - cti[:, :, None, :] * p1i[..., None]).reshape(SSM_G, SSM_P, CW)
    o_i = (ctr[:, :, None, :] * p1i[..., None] + cti[:, :, None, :] * p1r[..., None]).reshape(SSM_G, SSM_P, CW)
    ocat = jnp.concatenate([o_r, -o_i], axis=1)
    a16 = jnp.concatenate([pr[:, CHUNK], pi[:, CHUNK]], axis=-1)[:, None, :]
    return mt, scat, ocat, a16


def _cmul_const(xv, ar, ai):
    return xv * ar + pltpu.roll(xv, SSM_P, 1) * ai


def _chunk_scan(inc, a_row, reverse):
    n = inc.shape[0]
    lane = lax.broadcasted_iota(jnp.int32, (1, 2 * SSM_P), 1)
    row = lax.broadcasted_iota(jnp.int32, inc.shape, 0)
    sign = jnp.where(lane < SSM_P, -1.0, 1.0)
    ar = jnp.where(lane < SSM_P, a_row, pltpu.roll(a_row, SSM_P, 1))
    ai = jnp.where(lane < SSM_P, pltpu.roll(a_row, SSM_P, 1), a_row)
    if reverse:
        ai = -ai
    xv = inc
    s = 1
    while s < n:
        if reverse:
            sh = jnp.where(row < n - s, pltpu.roll(xv, n - s, 0), 0.0)
        else:
            sh = jnp.where(row >= s, pltpu.roll(xv, s, 0), 0.0)
        xv = xv + _cmul_const(sh, ar, ai * sign)
        ar, ai = ar * ar - ai * ai, 2.0 * ar * ai
        s *= 2
    return xv


def _shift_rows(xv, reverse):
    n = xv.shape[0]
    row = lax.broadcasted_iota(jnp.int32, xv.shape, 0)
    if reverse:
        return jnp.where(row < n - 1, pltpu.roll(xv, n - 1, 0), 0.0)
    return jnp.where(row >= 1, pltpu.roll(xv, 1, 0), 0.0)


GB = 128 // SSM_H
U_COL0 = (ATTN_W + 2 * KV_W + ATTN_W) // 128


def _chunk_perm():
    r = jnp.arange(CHUNK * 128)
    t, g8, h = r // 128, (r % 128) // SSM_H, r % SSM_H
    return ((g8 * CW + t * SSM_H + h)[:, None] == jnp.arange(GB * CW)[None, :]).astype(BF16)


def _load_perm(p_hbm, p_scr, sem):
    @pl.when(pl.program_id(0) == 0)
    def _():
        cp = pltpu.make_async_copy(p_hbm, p_scr, sem)
        cp.start()
        cp.wait()


def _rows_to_chunks(pieces, perm):
    z = jnp.concatenate(pieces, axis=1).astype(BF16)
    return jnp.dot(z, perm, preferred_element_type=F32).astype(BF16)


def _chunks_to_rows(ya, perm):
    hi = ya.astype(BF16)
    lo = (ya - hi.astype(F32)).astype(BF16)
    return (lax.dot_general(hi, perm, _NT, preferred_element_type=F32)
            + lax.dot_general(lo, perm, _NT, preferred_element_type=F32))


def _ssm_fwd(proj, perm, mt, scat, ocat, a16, d_skip):
    L = proj.shape[0]
    nc = L // CHUNK

    def body(u_ref, p_hbm, mt_ref, s_ref, o_ref, a_ref, d_ref, y_ref, yg_ref, h_ref, p_scr, sem):
        _load_perm(p_hbm, p_scr, sem)
        perm = p_scr[...]
        rows = [pl.ds(t, nc, stride=CHUNK) for t in range(CHUNK)]
        ua = _rows_to_chunks([u_ref[r, :] for r in rows], perm)
        ys = []
        for g in range(GB):
            uv = ua[:, g * CW:(g + 1) * CW]
            inc = jnp.dot(uv, s_ref[g], preferred_element_type=F32)
            hx = _shift_rows(_chunk_scan(inc, a_ref[g], False), False)
            h_ref[g] = hx
            ys.append(jnp.dot(uv, mt_ref[g], preferred_element_type=F32)
                      + jnp.dot(hx.astype(BF16), o_ref[g], preferred_element_type=F32))
        yp = _chunks_to_rows(jnp.concatenate(ys, axis=1), perm)
        for t, r in enumerate(rows):
            y = yp[:, t * 128:(t + 1) * 128] + d_ref[...] * u_ref[r, :]
            y_ref[r, :] = y
            yg_ref[r, :] = _gelu(y)

    g3 = lambda r, c: pl.BlockSpec((GB, r, c), lambda g: (g, 0, 0))
    col = pl.BlockSpec((L, 128), lambda g: (0, g))
    return pl.pallas_call(
        body,
        name="ssm_fwd",
        grid=(SSM_G // GB,),
        in_specs=[pl.BlockSpec((L, 128), lambda g: (0, U_COL0 + g)), _ANY,
                  g3(CW, CW), g3(CW, 2 * SSM_P), g3(2 * SSM_P, CW), g3(1, 2 * SSM_P),
                  pl.BlockSpec((1, 128), lambda g: (0, g))],
        out_specs=[col, col, g3(nc, 2 * SSM_P)],
        out_shape=[jax.ShapeDtypeStruct((L, SSM_W), F32), jax.ShapeDtypeStruct((L, SSM_W), F32),
                   jax.ShapeDtypeStruct((SSM_G, nc, 2 * SSM_P), F32)],
        scratch_shapes=[pltpu.VMEM((CHUNK * 128, GB * CW), BF16), pltpu.SemaphoreType.DMA],
        compiler_params=_cp(("arbitrary",)),
    )(proj, perm, mt, scat, ocat, a16, d_skip.reshape(1, SSM_W))


def _ssm_bwd(d_yg, y, proj, hx, perm, mt, scat, ocat, a16, d_skip):
    L = proj.shape[0]
    nc = L // CHUNK

    def body(dg_ref, y_ref, u_ref, h_ref, p_hbm, mt_ref, s_ref, o_ref, a_ref, d_ref,
             du_ref, gmt_ref, gs_ref, go_ref, ga_ref, gd_ref, p_scr, sem):
        _load_perm(p_hbm, p_scr, sem)
        perm = p_scr[...]
        rows = [pl.ds(t, nc, stride=CHUNK) for t in range(CHUNK)]
        us = [u_ref[r, :] for r in rows]
        dys = [dg_ref[r, :] * _dgelu(y_ref[r, :]) for r in rows]
        gd = jnp.zeros((1, 128), F32)
        for uv, dy in zip(us, dys):
            gd = gd + jnp.sum(dy * uv, axis=0, keepdims=True)
        gd_ref[...] = gd
        ua = _rows_to_chunks(us, perm)
        dya = _rows_to_chunks(dys, perm)
        lane = lax.broadcasted_iota(jnp.int32, (1, 2 * SSM_P), 1)
        dus = []
        for g in range(GB):
            uv, dy, hx_v = ua[:, g * CW:(g + 1) * CW], dya[:, g * CW:(g + 1) * CW], h_ref[g]
            dh = lax.dot_general(dy, o_ref[g], _NT, preferred_element_type=F32)
            dinc = _shift_rows(_chunk_scan(dh, a_ref[g], True), True)
            dinc_b = dinc.astype(BF16)
            dus.append(lax.dot_general(dy, mt_ref[g], _NT, preferred_element_type=F32)
                       + lax.dot_general(dinc_b, s_ref[g], _NT, preferred_element_type=F32))
            gmt_ref[g] = lax.dot_general(uv, dy, _TN, preferred_element_type=F32)
            gs_ref[g] = lax.dot_general(uv, dinc_b, _TN, preferred_element_type=F32)
            go_ref[g] = lax.dot_general(hx_v.astype(BF16), dy, _TN, preferred_element_type=F32)
            p1 = dinc * hx_v
            p2 = pltpu.roll(dinc, SSM_P, 1) * hx_v
            t1 = jnp.sum(p1 + pltpu.roll(p1, SSM_P, 1), axis=0, keepdims=True)
            t2 = jnp.sum(p2 - pltpu.roll(p2, SSM_P, 1), axis=0, keepdims=True)
            ga_ref[g] = jnp.where(lane < SSM_P, t1, pltpu.roll(t2, SSM_P, 1))
        dup = _chunks_to_rows(jnp.concatenate(dus, axis=1), perm)
        for t, r in enumerate(rows):
            du_ref[r, :] = dup[:, t * 128:(t + 1) * 128] + d_ref[...] * dys[t]

    g3 = lambda r, c: pl.BlockSpec((GB, r, c), lambda g: (g, 0, 0))
    col = pl.BlockSpec((L, 128), lambda g: (0, g))
    row = pl.BlockSpec((1, 128), lambda g: (0, g))
    return pl.pallas_call(
        body,
        name="ssm_bwd",
        grid=(SSM_G // GB,),
        in_specs=[col, col, pl.BlockSpec((L, 128), lambda g: (0, U_COL0 + g)), g3(nc, 2 * SSM_P), _ANY,
                  g3(CW, CW), g3(CW, 2 * SSM_P), g3(2 * SSM_P, CW), g3(1, 2 * SSM_P), row],
        out_specs=[col, g3(CW, CW), g3(CW, 2 * SSM_P), g3(2 * SSM_P, CW), g3(1, 2 * SSM_P), row],
        out_shape=[jax.ShapeDtypeStruct((L, SSM_W), F32), jax.ShapeDtypeStruct((SSM_G, CW, CW), F32),
                   jax.ShapeDtypeStruct((SSM_G, CW, 2 * SSM_P), F32),
                   jax.ShapeDtypeStruct((SSM_G, 2 * SSM_P, CW), F32),
                   jax.ShapeDtypeStruct((SSM_G, 1, 2 * SSM_P), F32),
                   jax.ShapeDtypeStruct((1, SSM_W), F32)],
        scratch_shapes=[pltpu.VMEM((CHUNK * 128, GB * CW), BF16), pltpu.SemaphoreType.DMA],
        compiler_params=_cp(("arbitrary",)),
    )(d_yg, y, proj, hx, perm, mt, scat, ocat, a16, d_skip.reshape(1, SSM_W))


def _merge(og, yg, gpre, proj, b_glu, wa, ws):
    L = og.shape[0]
    tm = _tile(L, 256)

    def body(og_ref, yg_ref, gp_ref, z0_ref, z1_ref, b_ref, wa_ref, ws_ref, m_ref):
        zs = jnp.concatenate([z0_ref[...], z1_ref[...]], axis=1)
        os_ = yg_ref[...] * _sigmoid(gp_ref[...] + b_ref[...]) * _silu(zs)
        ogv = og_ref[...]
        ra = lax.rsqrt(jnp.mean(ogv * ogv, axis=-1, keepdims=True) + NORM_EPS)
        rs = lax.rsqrt(jnp.mean(os_ * os_, axis=-1, keepdims=True) + NORM_EPS)
        m_ref[:, :ATTN_W] = (ogv * ra * wa_ref[...]).astype(BF16)
        m_ref[:, ATTN_W:] = (os_ * rs * ws_ref[...]).astype(BF16)

    row = lambda w: pl.BlockSpec((1, w), lambda i: (0, 0))
    return pl.pallas_call(
        body,
        name="merge",
        grid=(L // tm,),
        in_specs=[pl.BlockSpec((tm, ATTN_W), lambda i: (i, 0)), pl.BlockSpec((tm, SSM_W), lambda i: (i, 0)),
                  pl.BlockSpec((tm, SSM_W), lambda i: (i, 0)),
                  pl.BlockSpec((tm, 512), lambda i: (i, 7)), pl.BlockSpec((tm, 512), lambda i: (i, 8)),
                  row(SSM_W), row(ATTN_W), row(SSM_W)],
        out_specs=pl.BlockSpec((tm, D_MODEL), lambda i: (i, 0)),
        out_shape=jax.ShapeDtypeStruct((L, D_MODEL), BF16),
        compiler_params=_cp(("parallel",)),
    )(og, yg, gpre, proj, proj, b_glu.reshape(1, SSM_W), wa.reshape(1, ATTN_W), ws.reshape(1, SSM_W))


def _outproj_loss(merged, w_out, x, target):
    L = x.shape[0]
    tm, tn = _tile(L, 512), 1024
    ni, nj = L // tm, D_MODEL // tn

    def body(m_ref, w_ref, x_ref, t_ref, d_ref, db_ref, l_ref):
        out = x_ref[...] + jnp.dot(m_ref[...], w_ref[...], preferred_element_type=F32)
        diff = out - t_ref[...]
        d = diff * (1.0 / D_MODEL)
        d_ref[...] = d
        db_ref[...] = d.astype(BF16)
        l_ref[...] = jnp.full((1, 8, 128), jnp.sum(diff * diff), F32)

    return pl.pallas_call(
        body,
        name="outproj_loss",
        grid=(nj, ni),
        in_specs=[pl.BlockSpec((tm, D_MODEL), lambda j, i: (i, 0)),
                  pl.BlockSpec((D_MODEL, tn), lambda j, i: (0, j)),
                  pl.BlockSpec((tm, tn), lambda j, i: (i, j)),
                  pl.BlockSpec((tm, tn), lambda j, i: (i, j))],
        out_specs=[pl.BlockSpec((tm, tn), lambda j, i: (i, j)), pl.BlockSpec((tm, tn), lambda j, i: (i, j)),
                   pl.BlockSpec((1, 8, 128), lambda j, i: (i * nj + j, 0, 0))],
        out_shape=[jax.ShapeDtypeStruct((L, D_MODEL), F32), jax.ShapeDtypeStruct((L, D_MODEL), BF16),
                   jax.ShapeDtypeStruct((ni * nj, 8, 128), F32)],
        compiler_params=_cp(("parallel", "parallel")),
    )(merged, w_out, x, target)


def _merge_bwd(d_m, og, o, yg, gpre, proj, b_glu, wa, ws):
    L = og.shape[0]
    tm = _tile(L, 256)

    def body(dm_ref, og_ref, o_ref, yg_ref, gp_ref, za0_ref, za1_ref, zs0_ref, zs1_ref, b_ref, wa_ref, ws_ref,
             do_ref, dza_ref, dzs_ref, dg_ref, dyg_ref, gwa_ref, gws_ref, gb_ref):
        i = pl.program_id(0)

        @pl.when(i == 0)
        def _():
            gwa_ref[...] = jnp.zeros_like(gwa_ref)
            gws_ref[...] = jnp.zeros_like(gws_ref)
            gb_ref[...] = jnp.zeros_like(gb_ref)

        za = jnp.concatenate([za0_ref[...], za1_ref[...]], axis=1)
        zs = jnp.concatenate([zs0_ref[...], zs1_ref[...]], axis=1)
        ogv, dma = og_ref[...], dm_ref[:, :ATTN_W]
        ra = lax.rsqrt(jnp.mean(ogv * ogv, axis=-1, keepdims=True) + NORM_EPS)
        xh = ogv * ra
        gwa_ref[...] += jnp.sum(dma * xh, axis=0, keepdims=True)
        gx = dma * wa_ref[...]
        d_og = ra * (gx - xh * jnp.mean(gx * xh, axis=-1, keepdims=True))
        do_ref[...] = d_og * _silu(za)
        dza_ref[...] = (d_og * o_ref[...] * _dsilu(za)).astype(BF16)
        ygv = yg_ref[...]
        sg = _sigmoid(gp_ref[...] + b_ref[...])
        y2 = ygv * sg
        sz = _silu(zs)
        os_ = y2 * sz
        dms = dm_ref[:, ATTN_W:]
        rs = lax.rsqrt(jnp.mean(os_ * os_, axis=-1, keepdims=True) + NORM_EPS)
        xs = os_ * rs
        gws_ref[...] += jnp.sum(dms * xs, axis=0, keepdims=True)
        gxs = dms * ws_ref[...]
        d_os = rs * (gxs - xs * jnp.mean(gxs * xs, axis=-1, keepdims=True))
        dzs_ref[...] = (d_os * y2 * _dsilu(zs)).astype(BF16)
        d_y2 = d_os * sz
        d_g = d_y2 * ygv * sg * (1.0 - sg)
        dg_ref[...] = d_g.astype(BF16)
        gb_ref[...] += jnp.sum(d_g, axis=0, keepdims=True)
        dyg_ref[...] = d_y2 * sg

    row = lambda w: pl.BlockSpec((1, w), lambda i: (0, 0))
    full = lambda w: pl.BlockSpec((tm, w), lambda i: (i, 0))
    half = lambda c: pl.BlockSpec((tm, 512), lambda i: (i, c))
    return pl.pallas_call(
        body,
        name="merge_bwd",
        grid=(L // tm,),
        in_specs=[full(D_MODEL), full(ATTN_W), full(ATTN_W), full(SSM_W), full(SSM_W),
                  half(3), half(4), half(7), half(8), row(SSM_W), row(ATTN_W), row(SSM_W)],
        out_specs=[full(ATTN_W), full(ATTN_W), full(SSM_W), full(SSM_W), full(SSM_W),
                   row(ATTN_W), row(SSM_W), row(SSM_W)],
        out_shape=[jax.ShapeDtypeStruct((L, ATTN_W), F32), jax.ShapeDtypeStruct((L, ATTN_W), BF16),
                   jax.ShapeDtypeStruct((L, SSM_W), BF16), jax.ShapeDtypeStruct((L, SSM_W), BF16),
                   jax.ShapeDtypeStruct((L, SSM_W), F32),
                   jax.ShapeDtypeStruct((1, ATTN_W), F32), jax.ShapeDtypeStruct((1, SSM_W), F32),
                   jax.ShapeDtypeStruct((1, SSM_W), F32)],
        compiler_params=_cp(("arbitrary",)),
    )(d_m, og, o, yg, gpre, proj, proj, proj, proj, b_glu.reshape(1, SSM_W), wa.reshape(1, ATTN_W),
      ws.reshape(1, SSM_W))


def _rms_bwd_x(x, norm_w, d_hn, d_out):
    L = x.shape[0]
    tm = _tile(L, 256)

    def body(x_ref, w_ref, dh_ref, do_ref, gx_ref, gw_ref):
        i = pl.program_id(0)

        @pl.when(i == 0)
        def _():
            gw_ref[...] = jnp.zeros_like(gw_ref)

        xv, dh = x_ref[...], dh_ref[...]
        r = lax.rsqrt(jnp.mean(xv * xv, axis=-1, keepdims=True) + NORM_EPS)
        xh = xv * r
        gw_ref[...] += jnp.sum(dh * xh, axis=0, keepdims=True)
        gx = dh * w_ref[...]
        gx_ref[...] = do_ref[...] + r * (gx - xh * jnp.mean(gx * xh, axis=-1, keepdims=True))

    blk = pl.BlockSpec((tm, D_MODEL), lambda i: (i, 0))
    row = pl.BlockSpec((1, D_MODEL), lambda i: (0, 0))
    return pl.pallas_call(
        body, name="rms_bwd_x", grid=(L // tm,), in_specs=[blk, row, blk, blk], out_specs=[blk, row],
        out_shape=[jax.ShapeDtypeStruct((L, D_MODEL), F32), jax.ShapeDtypeStruct((1, D_MODEL), F32)],
        compiler_params=_cp(("arbitrary",)),
    )(x, norm_w.reshape(1, D_MODEL), d_hn, d_out)


def _rope_table(positions):
    inv_freq = ROPE_THETA ** (-jnp.arange(0, HEAD_DIM, 2, dtype=F32) / HEAD_DIM)
    ang = positions.astype(F32)[:, None] * inv_freq
    c, s = jnp.cos(ang), jnp.sin(ang)
    return jnp.concatenate([c, c, c, c, -s, s, -s, s], axis=1)


def _local_step(x, positions, target, small, wt_in, w_glu, w_out):
    tab = _rope_table(positions)
    ssm_names = ("a_re", "a_im", "log_step", "b_re", "b_im", "c_re", "c_im")
    ops, ops_vjp = jax.vjp(_ssm_ops, *[small[n] for n in ssm_names])
    mt, scat, ocat, a16 = ops
    mt_b, scat_b, ocat_b = mt.astype(BF16), scat.astype(BF16), ocat.astype(BF16)
    perm = _chunk_perm()

    proj, hn = _rms_inproj(x, small["norm_w"], wt_in)
    q_rot, k_rot = _qk_prep(proj, tab, small["q_norm_w"], small["k_norm_w"])
    og, o, lse = _attn_fwd(q_rot, k_rot, proj, small["sinks"])
    y, yg, hx = _ssm_fwd(proj, perm, mt_b, scat_b, ocat_b, a16, small["d_skip"])
    gpre = _mm(yg, w_glu, "nn", F32, "glu_fwd")
    merged = _merge(og, yg, gpre, proj, small["b_glu"], small["attn_out_norm_w"], small["ssm_out_norm_w"])
    d_out, d_out_b, loss_parts = _outproj_loss(merged, w_out, x, target)
    loss = 0.5 * jnp.sum(loss_parts[:, 0, 0]) / D_MODEL

    g_w_out = _mm(merged, d_out_b, "tn", F32, "grad_w_out")
    d_m = _mm(d_out_b, w_out, "nt", F32, "d_merged")
    d_o, d_za, d_zs, d_g, d_yg1, g_wa, g_ws, g_bglu = _merge_bwd(
        d_m, og, o, yg, gpre, proj, small["b_glu"], small["attn_out_norm_w"], small["ssm_out_norm_w"])
    g_w_glu = _mm(yg, d_g, "tn", F32, "grad_w_glu")
    d_yg = _mm(d_g, w_glu, "nt", F32, "d_yg", add=d_yg1)
    d_u, g_mt, g_scat, g_ocat, g_a16, g_dskip = _ssm_bwd(d_yg, y, proj, hx, perm, mt_b, scat_b, ocat_b, a16,
                                                         small["d_skip"])
    g_ssm = ops_vjp((g_mt, g_scat, g_ocat, g_a16))
    d_q, d_k, d_v, g_sinks = _attn_bwd(q_rot, k_rot, proj, small["sinks"], d_o, o, lse)
    d_qkv, g_qw, g_kw = _qk_prep_bwd(proj, tab, small["q_norm_w"], small["k_norm_w"], d_q, d_k, d_v)
    g_qw = g_qw[0, :HEAD_DIM] + g_qw[0, HEAD_DIM:]
    g_kw = g_kw[0, :HEAD_DIM] + g_kw[0, HEAD_DIM:]
    d_proj = jnp.concatenate([d_qkv, d_za, d_u.astype(BF16), d_zs], axis=1)
    g_wt_in = _mm(d_proj, hn, "tn", F32, "grad_w_in")
    d_hn = _mm(d_proj, wt_in, "nn", F32, "d_hn")
    grad_x, g_nw = _rms_bwd_x(x, small["norm_w"], d_hn, d_out)

    g_small = dict(zip(ssm_names, g_ssm))
    g_small.update(norm_w=g_nw.reshape(-1), q_norm_w=g_qw.reshape(-1), k_norm_w=g_kw.reshape(-1),
                   sinks=g_sinks[0, :N_HEADS], d_skip=g_dskip.reshape(-1), b_glu=g_bglu.reshape(-1),
                   attn_out_norm_w=g_wa.reshape(-1), ssm_out_norm_w=g_ws.reshape(-1))
    return loss, grad_x, g_wt_in, g_w_glu, g_w_out, g_small


_ANY = pl.BlockSpec(memory_space=pl.ANY)


def _all_gather_rows(blocks, name):
    n = len(blocks)

    def body(*refs):
        ins, outs = refs[:n], refs[n:2 * n]
        send_sems, recv_sems, local_sems = refs[2 * n:]
        x, y, c = lax.axis_index("x"), lax.axis_index("y"), lax.axis_index("c")
        me, sibling = (x, y, c), (x, y, 1 - c)
        chips = [(1 - x, y), (x, 1 - y), (1 - x, 1 - y)]

        def slot(k, dev):
            return outs[k].at[4 * dev[0] + 2 * dev[1] + dev[2]]

        def copy(k, q, block, to, src=None):
            return pltpu.make_async_remote_copy(
                src_ref=slot(k, block) if src is None else src, dst_ref=slot(k, block),
                send_sem=send_sems.at[k, q], recv_sem=recv_sems.at[k, q], device_id=to, device_id_type=MESH)

        mine = [pltpu.make_async_copy(ins[k], slot(k, me), local_sems.at[k]) for k in range(n)]
        for cp in mine:
            cp.start()
        first = []
        for k in range(n):
            first.append(copy(k, 0, me, sibling, src=ins[k]))
            first += [copy(k, 1 + j, me, (*chip, c), src=ins[k]) for j, chip in enumerate(chips)]
        for cp in first:
            cp.start()
        passed = []
        for j, chip in enumerate(chips):
            for k in range(n):
                copy(k, 1 + j, (*chip, c), me).wait_recv()
                fwd = copy(k, 4 + j, (*chip, c), sibling)
                fwd.start()
                passed.append(fwd)
        for k in range(n):
            copy(k, 0, sibling, me).wait_recv()
            for j, chip in enumerate(chips):
                copy(k, 4 + j, (*chip, 1 - c), me).wait_recv()
        for cp in first + passed:
            cp.wait_send()
        for cp in mine:
            cp.wait()

    outs = pl.pallas_call(
        body,
        name=name,
        in_specs=[_ANY] * n,
        out_specs=[_ANY] * n,
        out_shape=[jax.ShapeDtypeStruct((N_DEV,) + b.shape, b.dtype) for b in blocks],
        scratch_shapes=[pltpu.SemaphoreType.DMA((n, 7)), pltpu.SemaphoreType.DMA((n, 7)),
                        pltpu.SemaphoreType.DMA((n,))],
    )(*blocks)
    return list(outs)


def _pair_exchange(grads, name):
    n = len(grads)

    def body(*refs):
        ins, outs = refs[:n], refs[n:2 * n]
        send_sems, recv_sems = refs[2 * n:]
        x, y, c = lax.axis_index("x"), lax.axis_index("y"), lax.axis_index("c")
        copies = []
        for k in range(n):
            for chip in range(4):
                copies.append(pltpu.make_async_remote_copy(
                    src_ref=ins[k].at[2 * chip + (1 - c)], dst_ref=outs[k].at[chip],
                    send_sem=send_sems.at[k, chip], recv_sem=recv_sems.at[k, chip],
                    device_id=(x, y, 1 - c), device_id_type=MESH))
        for cp in copies:
            cp.start()
        for cp in copies:
            cp.wait()

    outs = pl.pallas_call(
        body,
        name=name,
        in_specs=[_ANY] * n,
        out_specs=[_ANY] * n,
        out_shape=[jax.ShapeDtypeStruct((4,) + g.shape[1:], g.dtype) for g in grads],
        scratch_shapes=[pltpu.SemaphoreType.DMA((n, 4)), pltpu.SemaphoreType.DMA((n, 4))],
    )(*grads)
    return list(outs)


def _pair_sum(g, ra, core, out_dtype, name):
    _, r, C = g.shape
    tr = _tile(r, 128)

    def body(c_ref, g_ref, ra_ref, p_ref):
        p_ref[...] = (g_ref[...] + ra_ref[...]).astype(p_ref.dtype)

    return pl.pallas_call(
        body,
        name=name,
        grid_spec=pltpu.PrefetchScalarGridSpec(
            num_scalar_prefetch=1,
            grid=(4, r // tr),
            in_specs=[pl.BlockSpec((1, tr, C), lambda j, t, c_ref: (2 * j + c_ref[0], t, 0)),
                      pl.BlockSpec((1, tr, C), lambda j, t, c_ref: (j, t, 0))],
            out_specs=pl.BlockSpec((1, tr, C), lambda j, t, c_ref: (j, t, 0)),
        ),
        out_shape=jax.ShapeDtypeStruct((4, r, C), out_dtype),
        compiler_params=_cp(("parallel", "parallel")),
    )(core, g, ra)


def _chip_exchange(parts, name):
    n = len(parts)

    def body(*refs):
        ins, outs = refs[:n], refs[n:2 * n]
        send_sems, recv_sems = refs[2 * n:]
        x, y, c = lax.axis_index("x"), lax.axis_index("y"), lax.axis_index("c")
        chips = [(1 - x, y), (x, 1 - y), (1 - x, 1 - y)]
        copies = []
        for k in range(n):
            for q, chip in enumerate(chips):
                copies.append(pltpu.make_async_remote_copy(
                    src_ref=ins[k].at[2 * chip[0] + chip[1]], dst_ref=outs[k].at[q],
                    send_sem=send_sems.at[k, q], recv_sem=recv_sems.at[k, q],
                    device_id=(*chip, c), device_id_type=MESH))
        for cp in copies:
            cp.start()
        for cp in copies:
            cp.wait()

    outs = pl.pallas_call(
        body,
        name=name,
        in_specs=[_ANY] * n,
        out_specs=[_ANY] * n,
        out_shape=[jax.ShapeDtypeStruct((3,) + p.shape[1:], p.dtype) for p in parts],
        scratch_shapes=[pltpu.SemaphoreType.DMA((n, 3)), pltpu.SemaphoreType.DMA((n, 3))],
    )(*parts)
    return list(outs)


def _chip_sum(p, rb, chip, name):
    _, r, C = p.shape
    tr = _tile(r, 128)

    def body(c_ref, p_ref, rb_ref, o_ref):
        acc = p_ref[0].astype(F32) + rb_ref[0].astype(F32)
        acc = acc + rb_ref[1].astype(F32)
        o_ref[...] = acc + rb_ref[2].astype(F32)

    return pl.pallas_call(
        body,
        name=name,
        grid_spec=pltpu.PrefetchScalarGridSpec(
            num_scalar_prefetch=1,
            grid=(r // tr,),
            in_specs=[pl.BlockSpec((1, tr, C), lambda t, c_ref: (c_ref[0], t, 0)),
                      pl.BlockSpec((3, tr, C), lambda t, c_ref: (0, t, 0))],
            out_specs=pl.BlockSpec((tr, C), lambda t, c_ref: (t, 0)),
        ),
        out_shape=jax.ShapeDtypeStruct((r, C), F32),
        compiler_params=_cp(("parallel",)),
    )(chip, p, rb)


def _adamw(g, w, m, v, name):
    R, C = g.shape
    tr = _tile(R, 256)
    c1 = 1.0 - ADAM_B1 ** ADAM_STEP
    c2 = 1.0 - ADAM_B2 ** ADAM_STEP

    def body(g_ref, w_ref, m_ref, v_ref, d_ref, nm_ref, nv_ref):
        gv = g_ref[...]
        nm = ADAM_B1 * m_ref[...] + (1.0 - ADAM_B1) * gv
        nv = ADAM_B2 * v_ref[...] + (1.0 - ADAM_B2) * (gv * gv)
        nm_ref[...] = nm
        nv_ref[...] = nv
        d_ref[...] = -ADAM_LR * ((nm / c1) / (jnp.sqrt(nv / c2) + ADAM_EPS) + ADAM_WD * w_ref[...])

    blk = pl.BlockSpec((tr, C), lambda i: (i, 0))
    return pl.pallas_call(
        body, name=name, grid=(R // tr,), in_specs=[blk] * 4, out_specs=[blk] * 3,
        out_shape=[jax.ShapeDtypeStruct((R, C), F32)] * 3, compiler_params=_cp(("parallel",)),
    )(g, w, m, v)


_SMALL = ("norm_w", "q_norm_w", "k_norm_w", "sinks", "a_re", "a_im", "log_step", "b_re", "b_im", "c_re", "c_im",
          "d_skip", "b_glu", "attn_out_norm_w", "ssm_out_norm_w")
_WEIGHTS = ("norm_w", "w_in", "q_norm_w", "k_norm_w", "sinks", "a_re", "a_im", "log_step", "b_re", "b_im", "c_re",
            "c_im", "d_skip", "w_glu", "b_glu", "attn_out_norm_w", "ssm_out_norm_w", "w_out")
_PACK_ROWS = 2176


def _pack(d):
    flat = jnp.concatenate([d[n].reshape(-1).astype(F32) for n in _SMALL])
    return jnp.pad(flat, (0, _PACK_ROWS * 128 - flat.shape[0])).reshape(_PACK_ROWS, 128)


def _unpack(packed, like):
    flat = packed.reshape(-1)
    out, off = {}, 0
    for n in _SMALL:
        size = math.prod(like[n].shape)
        out[n] = flat[off:off + size].reshape(like[n].shape)
        off += size
    return out


def kernel(x, positions, norm_w, w_in, q_norm_w, k_norm_w, sinks, a_re, a_im, log_step, b_re, b_im, c_re, c_im, d_skip, w_glu, b_glu, attn_out_norm_w, ssm_out_norm_w, w_out, loss_target, m_norm_w, m_w_in, m_q_norm_w, m_k_norm_w, m_sinks, m_a_re, m_a_im, m_log_step, m_b_re, m_b_im, m_c_re, m_c_im, m_d_skip, m_w_glu, m_b_glu, m_attn_out_norm_w, m_ssm_out_norm_w, m_w_out, v_norm_w, v_w_in, v_q_norm_w, v_k_norm_w, v_sinks, v_a_re, v_a_im, v_log_step, v_b_re, v_b_im, v_c_re, v_c_im, v_d_skip, v_w_glu, v_b_glu, v_attn_out_norm_w, v_ssm_out_norm_w, v_w_out):
    w = dict(norm_w=norm_w, w_in=w_in, q_norm_w=q_norm_w, k_norm_w=k_norm_w, sinks=sinks, a_re=a_re, a_im=a_im,
             log_step=log_step, b_re=b_re, b_im=b_im, c_re=c_re, c_im=c_im, d_skip=d_skip, w_glu=w_glu, b_glu=b_glu,
             attn_out_norm_w=attn_out_norm_w, ssm_out_norm_w=ssm_out_norm_w, w_out=w_out)
    m = dict(norm_w=m_norm_w, w_in=m_w_in, q_norm_w=m_q_norm_w, k_norm_w=m_k_norm_w, sinks=m_sinks, a_re=m_a_re,
             a_im=m_a_im, log_step=m_log_step, b_re=m_b_re, b_im=m_b_im, c_re=m_c_re, c_im=m_c_im, d_skip=m_d_skip,
             w_glu=m_w_glu, b_glu=m_b_glu, attn_out_norm_w=m_attn_out_norm_w, ssm_out_norm_w=m_ssm_out_norm_w,
             w_out=m_w_out)
    v = dict(norm_w=v_norm_w, w_in=v_w_in, q_norm_w=v_q_norm_w, k_norm_w=v_k_norm_w, sinks=v_sinks, a_re=v_a_re,
             a_im=v_a_im, log_step=v_log_step, b_re=v_b_re, b_im=v_b_im, c_re=v_c_re, c_im=v_c_im, d_skip=v_d_skip,
             w_glu=v_w_glu, b_glu=v_b_glu, attn_out_norm_w=v_attn_out_norm_w, ssm_out_norm_w=v_ssm_out_norm_w,
             w_out=v_w_out)
    core = lax.axis_index("c").astype(jnp.int32).reshape(1)
    chip = (2 * lax.axis_index("x") + lax.axis_index("y")).astype(jnp.int32).reshape(1)

    wt_in, wf_glu, wf_out = _all_gather_rows(
        [w_in.T.astype(BF16), w_glu.astype(BF16), w_out.astype(BF16)], "gather_weights")
    wt_in = wt_in.reshape(IN_W, D_MODEL)
    wf_glu = wf_glu.reshape(SSM_W, SSM_W)
    wf_out = wf_out.reshape(D_MODEL, D_MODEL)

    small = {n: w[n] for n in _SMALL}
    loss, grad_x, g_wt_in, g_w_glu, g_w_out, g_small = _local_step(
        x[0], positions[0], loss_target[0], small, wt_in, wf_glu, wf_out)
    loss = lax.psum(loss, ("x", "y", "c"))

    full = [g_wt_in.reshape(N_DEV, IN_W // N_DEV, D_MODEL), g_w_glu.reshape(N_DEV, SSM_W // N_DEV, SSM_W),
            g_w_out.reshape(N_DEV, D_MODEL // N_DEV, D_MODEL), _pack(g_small).reshape(N_DEV, _PACK_ROWS // N_DEV, 128)]
    from_sibling = _pair_exchange(full, "pair_exchange")
    wire = (BF16, BF16, BF16, F32)
    parts = [_pair_sum(g, ra, core, dt, f"pair_sum_{k}") for k, (g, ra, dt) in enumerate(zip(full, from_sibling, wire))]
    from_chips = _chip_exchange(parts, "chip_exchange")
    red = [_chip_sum(p, rb, chip, f"chip_sum_{k}") for k, (p, rb) in enumerate(zip(parts, from_chips))]
    g_in, g_glu, g_out = red[0].T, red[1], red[2]
    (g_packed,) = _all_gather_rows([red[3]], "gather_small")
    g_packed = g_packed.reshape(_PACK_ROWS, 128)

    grads = _unpack(g_packed, w)
    grads.update(w_in=g_in, w_glu=g_glu, w_out=g_out)
    delta, new_m, new_v = {}, {}, {}
    for n in ("w_in", "w_glu", "w_out"):
        delta[n], new_m[n], new_v[n] = _adamw(grads[n], w[n], m[n], v[n], f"adamw_{n}")
    d_p, m_p, v_p = _adamw(g_packed, _pack(w), _pack(m), _pack(v), "adamw_small")
    delta.update(_unpack(d_p, w))
    new_m.update(_unpack(m_p, w))
    new_v.update(_unpack(v_p, w))

    return (loss, grad_x[None], *[grads[n] for n in _WEIGHTS], *[delta[n] for n in _WEIGHTS],
            *[new_m[n] for n in _WEIGHTS], *[new_v[n] for n in _WEIGHTS])
```

```python
import functools
import math

import jax
import jax.numpy as jnp
from jax import lax
from jax.experimental import pallas as pl
from jax.experimental.pallas import tpu as pltpu

F32 = jnp.float32
BF16 = jnp.bfloat16

D_MODEL = 2048
ATTN_W = 1024
KV_W = 256
SSM_W = 1024
HEAD_DIM = 64
N_HEADS = 16
N_KV = 4
KV_REP = 4
IN_W = 4608
BLOCK = 128
ROPE_THETA = 10000.0
NORM_EPS = 1e-6
SSM_G = 64
SSM_P = 64
SSM_H = 16
CHUNK = 16
CW = CHUNK * SSM_H
N_DEV = 8

ADAM_LR = 0.001
ADAM_B1 = 0.9
ADAM_B2 = 0.999
ADAM_EPS = 1e-08
ADAM_WD = 0.01
ADAM_STEP = 10

VMEM_LIMIT = 56 * 1024 * 1024
MESH = pl.DeviceIdType.MESH


def _cp(sem=None):
    if sem is None:
        return pltpu.CompilerParams(vmem_limit_bytes=VMEM_LIMIT)
    return pltpu.CompilerParams(vmem_limit_bytes=VMEM_LIMIT, dimension_semantics=sem)


def _sigmoid(x):
    return 1.0 / (1.0 + jnp.exp(-x))


def _silu(x):
    return x * _sigmoid(x)


def _dsilu(x):
    s = _sigmoid(x)
    return s * (1.0 + x * (1.0 - s))


_GELU_C = math.sqrt(2.0 / math.pi)


def _gelu(y):
    t = jnp.tanh(_GELU_C * (y + 0.044715 * y * y * y))
    return 0.5 * y * (1.0 + t)


def _dgelu(y):
    t = jnp.tanh(_GELU_C * (y + 0.044715 * y * y * y))
    return 0.5 * (1.0 + t) + 0.5 * y * (1.0 - t * t) * _GELU_C * (1.0 + 3.0 * 0.044715 * y * y)


def _tile(n, want):
    if n <= want:
        return n
    for t in range(want - want % 16, 0, -16):
        if n % t == 0:
            return t
    raise ValueError((n, want))


def _mm(a, b, mode, out_dtype, name, tm=512, tn=1024, add=None):
    if mode == "nn":
        (M, K), (K2, N) = a.shape, b.shape
    elif mode == "nt":
        (M, K), (N, K2) = a.shape, b.shape
    else:
        (K, M), (K2, N) = a.shape, b.shape
    assert K == K2
    tm, tn = _tile(M, tm), _tile(N, tn)
    dn = {"nn": _NN, "nt": _NT, "tn": _TN}[mode]

    def body(a_ref, b_ref, *rest):
        o_ref = rest[-1]
        acc = lax.dot_general(a_ref[...].astype(BF16), b_ref[...].astype(BF16), dn, preferred_element_type=F32)
        if add is not None:
            acc = acc + rest[0][...]
        o_ref[...] = acc.astype(o_ref.dtype)

    a_spec = pl.BlockSpec((K, tm), lambda j, i: (0, i)) if mode == "tn" else pl.BlockSpec((tm, K), lambda j, i: (i, 0))
    b_spec = pl.BlockSpec((tn, K), lambda j, i: (j, 0)) if mode == "nt" else pl.BlockSpec((K, tn), lambda j, i: (0, j))
    o_spec = pl.BlockSpec((tm, tn), lambda j, i: (i, j))
    extra = () if add is None else (add,)
    return pl.pallas_call(
        body,
        name=name,
        grid=(N // tn, M // tm),
        in_specs=[a_spec, b_spec] + [o_spec] * len(extra),
        out_specs=o_spec,
        out_shape=jax.ShapeDtypeStruct((M, N), out_dtype),
        compiler_params=_cp(("parallel", "parallel")),
    )(a, b, *extra)


def _rms_inproj(x, norm_w, wt_in):
    L = x.shape[0]
    tm, tn = _tile(L, 1024), 768
    nj = IN_W // tn

    def body(x_ref, w_ref, wt_ref, proj_ref, hn_ref, hn_scr):
        j = pl.program_id(1)

        @pl.when(j == 0)
        def _():
            xv = x_ref[...]
            r = lax.rsqrt(jnp.mean(xv * xv, axis=-1, keepdims=True) + NORM_EPS)
            hn = (xv * r * w_ref[...]).astype(BF16)
            hn_scr[...] = hn
            hn_ref[...] = hn

        proj_ref[...] = lax.dot_general(hn_scr[...], wt_ref[...], (((1,), (1,)), ((), ())),
                                        preferred_element_type=F32)

    return pl.pallas_call(
        body,
        name="rms_inproj",
        grid=(L // tm, nj),
        in_specs=[pl.BlockSpec((tm, D_MODEL), lambda i, j: (i, 0)),
                  pl.BlockSpec((1, D_MODEL), lambda i, j: (0, 0)),
                  pl.BlockSpec((tn, D_MODEL), lambda i, j: (j, 0))],
        out_specs=[pl.BlockSpec((tm, tn), lambda i, j: (i, j)),
                   pl.BlockSpec((tm, D_MODEL), lambda i, j: (i, 0))],
        out_shape=[jax.ShapeDtypeStruct((L, IN_W), F32), jax.ShapeDtypeStruct((L, D_MODEL), BF16)],
        scratch_shapes=[pltpu.VMEM((tm, D_MODEL), BF16)],
        compiler_params=_cp(("parallel", "arbitrary")),
    )(x, norm_w.reshape(1, D_MODEL), wt_in)


def _seg_sum(v):
    a = lax.broadcasted_iota(jnp.int32, (128, 128), 0) // HEAD_DIM
    b = lax.broadcasted_iota(jnp.int32, (128, 128), 1) // HEAD_DIM
    ones = jnp.where(a == b, 1.0, 0.0).astype(BF16)
    hi = v.astype(BF16)
    lo = (v - hi.astype(F32)).astype(BF16)
    return jnp.dot(hi, ones, preferred_element_type=F32) + jnp.dot(lo, ones, preferred_element_type=F32)


def _rot_half(t):
    lane = lax.broadcasted_iota(jnp.int32, t.shape, 1)
    return jnp.where(lane % HEAD_DIM < HEAD_DIM // 2, pltpu.roll(t, 128 - HEAD_DIM // 2, 1),
                     pltpu.roll(t, HEAD_DIM // 2, 1))


def _norm_rope(raw, w, cos, sin):
    r = lax.rsqrt(_seg_sum(raw * raw) * (1.0 / HEAD_DIM) + NORM_EPS)
    tn = raw * r * w
    return r, tn * cos + _rot_half(tn) * sin


def _norm_rope_bwd(d_rot, raw, w, cos, sin):
    r = lax.rsqrt(_seg_sum(raw * raw) * (1.0 / HEAD_DIM) + NORM_EPS)
    d_tn = d_rot * cos + _rot_half(d_rot * sin)
    xh = raw * r
    gw = d_tn * w
    d_raw = r * (gw - xh * (_seg_sum(gw * xh) * (1.0 / HEAD_DIM)))
    return d_raw, d_tn * xh


def _band_mask2(has_prev):
    qi = lax.broadcasted_iota(jnp.int32, (2 * BLOCK, 2 * BLOCK), 0) % BLOCK + BLOCK
    kj = lax.broadcasted_iota(jnp.int32, (2 * BLOCK, 2 * BLOCK), 1)
    rel = qi - kj
    return (rel >= 0) & (rel < BLOCK) & ((kj >= BLOCK) | has_prev)


def _half_tiles(pair):
    lo = lax.broadcasted_iota(jnp.int32, pair.shape, 1) < HEAD_DIM
    sw = pltpu.roll(pair, HEAD_DIM, 1)
    z = jnp.zeros_like(pair)
    return (jnp.where(lo, pair, z).astype(BF16), jnp.where(lo, z, sw).astype(BF16),
            jnp.where(lo, sw, z).astype(BF16), jnp.where(lo, z, pair).astype(BF16))


def _two_rows(top, bottom):
    row = lax.broadcasted_iota(jnp.int32, (2 * BLOCK, 1), 0)
    return jnp.where(row < BLOCK, top, bottom)


def _lane_col(mat, h):
    lane = lax.broadcasted_iota(jnp.int32, mat.shape, 1)
    return jnp.sum(jnp.where(lane == h, mat, 0.0), axis=1, keepdims=True)


_SCALE = 1.0 / math.sqrt(HEAD_DIM)
_NT = (((1,), (1,)), ((), ()))
_NN = (((1,), (0,)), ((), ()))
_TN = (((0,), (0,)), ((), ()))


def _qk_prep(proj, tab, qw, kw):
    L = proj.shape[0]
    tm = _tile(L, 512)

    def body(q_ref, k_ref, t_ref, qw_ref, kw_ref, qo_ref, ko_ref):
        cos, sin = t_ref[:, :128], t_ref[:, 128:]
        for c in range(ATTN_W // 128):
            _, qr = _norm_rope(q_ref[:, c * 128:(c + 1) * 128], qw_ref[...], cos, sin)
            qo_ref[:, c * 128:(c + 1) * 128] = (qr * _SCALE).astype(BF16)
        for c in range(KV_W // 128):
            _, kr = _norm_rope(k_ref[:, c * 128:(c + 1) * 128], kw_ref[...], cos, sin)
            ko_ref[:, c * 128:(c + 1) * 128] = kr.astype(BF16)

    row = pl.BlockSpec((1, 128), lambda i: (0, 0))
    return pl.pallas_call(
        body,
        name="qk_prep",
        grid=(L // tm,),
        in_specs=[pl.BlockSpec((tm, ATTN_W), lambda i: (i, 0)), pl.BlockSpec((tm, KV_W), lambda i: (i, 4)),
                  pl.BlockSpec((tm, 256), lambda i: (i, 0)), row, row],
        out_specs=[pl.BlockSpec((tm, ATTN_W), lambda i: (i, 0)), pl.BlockSpec((tm, KV_W), lambda i: (i, 0))],
        out_shape=[jax.ShapeDtypeStruct((L, ATTN_W), BF16), jax.ShapeDtypeStruct((L, KV_W), BF16)],
        compiler_params=_cp(("parallel",)),
    )(proj, proj, tab, jnp.tile(qw, 2).reshape(1, 128), jnp.tile(kw, 2).reshape(1, 128))


def _group_tiles(g, kt, vt):
    a, b = divmod(g, 2)
    return kt[a][2 * b], kt[a][2 * b + 1], vt[a][2 * b], vt[a][2 * b + 1]


def _attn_fwd(q, k, proj, sinks):
    L = proj.shape[0]
    nb = L // BLOCK

    def body(q_ref, kc_ref, kp_ref, vc_ref, vp_ref, z0_ref, z1_ref, sink_ref, og_ref, o_ref, lse_ref):
        i = pl.program_id(0)
        mask = _band_mask2(i > 0)
        z = jnp.concatenate([z0_ref[...], z1_ref[...]], axis=1)
        lane = lax.broadcasted_iota(jnp.int32, (BLOCK, 128), 1)
        kt = [_half_tiles(jnp.concatenate([kp_ref[:, a * 128:(a + 1) * 128], kc_ref[:, a * 128:(a + 1) * 128]],
                                          axis=0).astype(F32)) for a in range(2)]
        vt = [_half_tiles(jnp.concatenate([vp_ref[:, a * 128:(a + 1) * 128], vc_ref[:, a * 128:(a + 1) * 128]],
                                          axis=0)) for a in range(2)]
        lse_mat = jnp.zeros((BLOCK, 128), F32)
        outs = []
        for g in range(N_KV):
            k_lo, k_hi, v_lo, v_hi = _group_tiles(g, kt, vt)
            q2 = jnp.concatenate([q_ref[:, 2 * g * 128:(2 * g + 1) * 128],
                                  q_ref[:, (2 * g + 1) * 128:(2 * g + 2) * 128]], axis=0)
            acc = jnp.zeros((2 * BLOCK, 128), F32)
            for half, (kh, vh) in enumerate(((k_lo, v_lo), (k_hi, v_hi))):
                h_top, h_bot = 4 * g + half, 4 * g + 2 + half
                s = jnp.where(mask, lax.dot_general(q2, kh, _NT, preferred_element_type=F32), -1e30)
                sink = _two_rows(sink_ref[h_top], sink_ref[h_bot])
                m = jnp.maximum(jnp.max(s, axis=-1, keepdims=True), sink)
                e = jnp.exp(s - m)
                den = jnp.sum(e, axis=-1, keepdims=True) + jnp.exp(sink - m)
                p = e / den
                acc = acc + jnp.dot(p.astype(BF16), vh, preferred_element_type=F32)
                lse = m + jnp.log(den)
                lse_mat = jnp.where(lane == h_top, lse[:BLOCK], lse_mat)
                lse_mat = jnp.where(lane == h_bot, lse[BLOCK:], lse_mat)
            outs += [acc[:BLOCK], acc[BLOCK:]]
        o = jnp.concatenate(outs, axis=1)
        o_ref[...] = o
        og_ref[...] = o * _silu(z)
        lse_ref[...] = lse_mat

    prev = lambda i: jnp.maximum(i - 1, 0)
    return pl.pallas_call(
        body,
        name="attn_fwd",
        grid=(nb,),
        in_specs=[pl.BlockSpec((BLOCK, ATTN_W), lambda i: (i, 0)),
                  pl.BlockSpec((BLOCK, KV_W), lambda i: (i, 0)),
                  pl.BlockSpec((BLOCK, KV_W), lambda i: (prev(i), 0)),
                  pl.BlockSpec((BLOCK, KV_W), lambda i: (i, 5)),
                  pl.BlockSpec((BLOCK, KV_W), lambda i: (prev(i), 5)),
                  pl.BlockSpec((BLOCK, 512), lambda i: (i, 3)),
                  pl.BlockSpec((BLOCK, 512), lambda i: (i, 4)),
                  pl.BlockSpec(memory_space=pltpu.SMEM)],
        out_specs=[pl.BlockSpec((BLOCK, ATTN_W), lambda i: (i, 0)),
                   pl.BlockSpec((BLOCK, ATTN_W), lambda i: (i, 0)),
                   pl.BlockSpec((BLOCK, 128), lambda i: (i, 0))],
        out_shape=[jax.ShapeDtypeStruct((L, ATTN_W), F32), jax.ShapeDtypeStruct((L, ATTN_W), F32),
                   jax.ShapeDtypeStruct((L, 128), F32)],
        compiler_params=_cp(("parallel",)),
    )(q, k, k, proj, proj, proj, proj, sinks)


def _attn_bwd(q, k, proj, sinks, d_o, o, lse):
    L = proj.shape[0]
    nb = L // BLOCK

    def body(q_ref, kc_ref, kp_ref, vc_ref, vp_ref, do_ref, o_ref, lse_ref, sink_ref,
             dq_ref, dk_ref, dv_ref, gs_ref, ck_scr, cv_scr):
        i = pl.program_id(0)

        @pl.when(i == 0)
        def _():
            gs_ref[...] = jnp.zeros_like(gs_ref)
            ck_scr[...] = jnp.zeros_like(ck_scr)
            cv_scr[...] = jnp.zeros_like(cv_scr)

        @pl.when(i == nb)
        def _():
            dk_ref[...] = ck_scr[...]
            dv_ref[...] = cv_scr[...]

        @pl.when(i < nb)
        def _():
            mask = _band_mask2(i > 0)
            lane = lax.broadcasted_iota(jnp.int32, (1, 128), 1)
            lo = lax.broadcasted_iota(jnp.int32, (2 * BLOCK, 128), 1) < HEAD_DIM
            lse_c = lse_ref[...]
            kt = [_half_tiles(jnp.concatenate([kp_ref[:, a * 128:(a + 1) * 128], kc_ref[:, a * 128:(a + 1) * 128]],
                                              axis=0).astype(F32)) for a in range(2)]
            vt = [_half_tiles(jnp.concatenate([vp_ref[:, a * 128:(a + 1) * 128], vc_ref[:, a * 128:(a + 1) * 128]],
                                              axis=0)) for a in range(2)]
            gs = jnp.zeros((1, 128), F32)
            dq_parts = []
            dk_acc = [jnp.zeros((2 * BLOCK, 128), F32) for _ in range(2)]
            dv_acc = [jnp.zeros((2 * BLOCK, 128), F32) for _ in range(2)]
            for g in range(N_KV):
                a, b = divmod(g, 2)
                k_lo, k_hi, v_lo, v_hi = _group_tiles(g, kt, vt)
                t0, t1 = slice(2 * g * 128, (2 * g + 1) * 128), slice((2 * g + 1) * 128, (2 * g + 2) * 128)
                q2 = jnp.concatenate([q_ref[:, t0], q_ref[:, t1]], axis=0)
                do2 = jnp.concatenate([do_ref[:, t0], do_ref[:, t1]], axis=0)
                prod = do2 * jnp.concatenate([o_ref[:, t0], o_ref[:, t1]], axis=0)
                do2_b = do2.astype(BF16)
                dq2 = jnp.zeros((2 * BLOCK, 128), F32)
                dk_h, dv_h = [], []
                for half, (kh, vh) in enumerate(((k_lo, v_lo), (k_hi, v_hi))):
                    h_top, h_bot = 4 * g + half, 4 * g + 2 + half
                    lse = jnp.concatenate([_lane_col(lse_c, h_top), _lane_col(lse_c, h_bot)], axis=0)
                    sink = _two_rows(sink_ref[h_top], sink_ref[h_bot])
                    delta = jnp.sum(jnp.where(lo == (half == 0), prod, 0.0), axis=1, keepdims=True)
                    s = jnp.where(mask, lax.dot_general(q2, kh, _NT, preferred_element_type=F32), -1e30)
                    p = jnp.exp(s - lse)
                    dp = lax.dot_general(do2_b, vh, _NT, preferred_element_type=F32)
                    ds_b = (p * (dp - delta)).astype(BF16)
                    p_b = p.astype(BF16)
                    dq2 = dq2 + jnp.dot(ds_b, kh, preferred_element_type=F32)
                    dk_h.append(lax.dot_general(ds_b, q2, _TN, preferred_element_type=F32))
                    dv_h.append(lax.dot_general(p_b, do2_b, _TN, preferred_element_type=F32))
                    gsink = -jnp.exp(sink - lse) * delta
                    row = lax.broadcasted_iota(jnp.int32, (2 * BLOCK, 1), 0)
                    gs = gs + jnp.where(lane == h_top, jnp.sum(jnp.where(row < BLOCK, gsink, 0.0)), 0.0)
                    gs = gs + jnp.where(lane == h_bot, jnp.sum(jnp.where(row >= BLOCK, gsink, 0.0)), 0.0)
                dq_parts += [dq2[:BLOCK], dq2[BLOCK:]]
                for acc, parts in ((dk_acc, dk_h), (dv_acc, dv_h)):
                    t = jnp.where(lo, parts[0], parts[1])
                    t = t + pltpu.roll(t, HEAD_DIM, 1)
                    acc[a] = acc[a] + jnp.where(lo == (b == 0), t, 0.0)
            dq_ref[...] = jnp.concatenate(dq_parts, axis=1)
            dk_full = jnp.concatenate(dk_acc, axis=1)
            dv_full = jnp.concatenate(dv_acc, axis=1)
            dk_ref[...] = ck_scr[...] + dk_full[:BLOCK]
            dv_ref[...] = cv_scr[...] + dv_full[:BLOCK]
            ck_scr[...] = dk_full[BLOCK:]
            cv_scr[...] = dv_full[BLOCK:]
            gs_ref[...] += gs

    cur = lambda i: jnp.minimum(i, nb - 1)
    prev = lambda i: jnp.maximum(jnp.minimum(i, nb - 1) - 1, 0)
    done = lambda i: jnp.maximum(i - 1, 0)
    bs = pl.BlockSpec
    return pl.pallas_call(
        body,
        name="attn_bwd",
        grid=(nb + 1,),
        in_specs=[bs((BLOCK, ATTN_W), lambda i: (cur(i), 0)),
                  bs((BLOCK, KV_W), lambda i: (cur(i), 0)), bs((BLOCK, KV_W), lambda i: (prev(i), 0)),
                  bs((BLOCK, KV_W), lambda i: (cur(i), 5)), bs((BLOCK, KV_W), lambda i: (prev(i), 5)),
                  bs((BLOCK, ATTN_W), lambda i: (cur(i), 0)), bs((BLOCK, ATTN_W), lambda i: (cur(i), 0)),
                  bs((BLOCK, 128), lambda i: (cur(i), 0)), bs(memory_space=pltpu.SMEM)],
        out_specs=[bs((BLOCK, ATTN_W), lambda i: (cur(i), 0)),
                   bs((BLOCK, KV_W), lambda i: (done(i), 0)), bs((BLOCK, KV_W), lambda i: (done(i), 0)),
                   bs((1, 128), lambda i: (0, 0))],
        out_shape=[jax.ShapeDtypeStruct((L, ATTN_W), F32), jax.ShapeDtypeStruct((L, KV_W), F32),
                   jax.ShapeDtypeStruct((L, KV_W), F32), jax.ShapeDtypeStruct((1, 128), F32)],
        scratch_shapes=[pltpu.VMEM((BLOCK, KV_W), F32), pltpu.VMEM((BLOCK, KV_W), F32)],
        compiler_params=_cp(("arbitrary",)),
    )(q, k, k, proj, proj, d_o, o, lse, sinks)


def _qk_prep_bwd(proj, tab, qw, kw, d_q, d_k, d_v):
    L = proj.shape[0]
    tm = _tile(L, 512)

    def body(q_ref, k_ref, t_ref, qw_ref, kw_ref, dq_ref, dk_ref, dv_ref, out_ref, gq_ref, gk_ref):
        i = pl.program_id(0)

        @pl.when(i == 0)
        def _():
            gq_ref[...] = jnp.zeros_like(gq_ref)
            gk_ref[...] = jnp.zeros_like(gk_ref)

        cos, sin = t_ref[:, :128], t_ref[:, 128:]
        gq = jnp.zeros((1, 128), F32)
        gk = jnp.zeros((1, 128), F32)
        for c in range(ATTN_W // 128):
            cs = slice(c * 128, (c + 1) * 128)
            d_raw, gw = _norm_rope_bwd(dq_ref[:, cs] * _SCALE, q_ref[:, cs], qw_ref[...], cos, sin)
            out_ref[:, cs] = d_raw.astype(BF16)
            gq = gq + jnp.sum(gw, axis=0, keepdims=True)
        for c in range(KV_W // 128):
            cs = slice(c * 128, (c + 1) * 128)
            d_raw, gw = _norm_rope_bwd(dk_ref[:, cs], k_ref[:, cs], kw_ref[...], cos, sin)
            out_ref[:, ATTN_W + c * 128:ATTN_W + (c + 1) * 128] = d_raw.astype(BF16)
            gk = gk + jnp.sum(gw, axis=0, keepdims=True)
        out_ref[:, ATTN_W + KV_W:] = dv_ref[...].astype(BF16)
        gq_ref[...] += gq
        gk_ref[...] += gk

    row = pl.BlockSpec((1, 128), lambda i: (0, 0))
    blk = lambda w, c: pl.BlockSpec((tm, w), lambda i: (i, c))
    return pl.pallas_call(
        body,
        name="qk_prep_bwd",
        grid=(L // tm,),
        in_specs=[blk(ATTN_W, 0), blk(KV_W, 4), blk(256, 0), row, row, blk(ATTN_W, 0), blk(KV_W, 0), blk(KV_W, 0)],
        out_specs=[blk(ATTN_W + 2 * KV_W, 0), row, row],
        out_shape=[jax.ShapeDtypeStruct((L, ATTN_W + 2 * KV_W), BF16), jax.ShapeDtypeStruct((1, 128), F32),
                   jax.ShapeDtypeStruct((1, 128), F32)],
        compiler_params=_cp(("arbitrary",)),
    )(proj, proj, tab, jnp.tile(qw, 2).reshape(1, 128), jnp.tile(kw, 2).reshape(1, 128), d_q, d_k, d_v)


def _ssm_ops(a_re, a_im, log_step, b_re, b_im, c_re, c_im):
    hp = lax.Precision.HIGHEST
    delta = jnp.exp(log_step)[:, None]
    xr, xi = a_re * delta, a_im * delta
    er = jnp.exp(xr)
    lbr, lbi = er * jnp.cos(xi), er * jnp.sin(xi)
    nr, ni = lbr - 1.0, lbi
    den = a_re * a_re + a_im * a_im
    cr, ci = (nr * a_re + ni * a_im) / den, (ni * a_re - nr * a_im) / den
    bbr = cr[..., None] * b_re - ci[..., None] * b_im
    bbi = cr[..., None] * b_im + ci[..., None] * b_re
    pr, pi = [jnp.ones_like(lbr)], [jnp.zeros_like(lbr)]
    for _ in range(CHUNK):
        pr.append(pr[-1] * lbr - pi[-1] * lbi)
        pi.append(pr[-2] * lbi + pi[-1] * lbr)
    pr, pi = jnp.stack(pr, axis=1), jnp.stack(pi, axis=1)
    p0r, p0i = pr[:, :CHUNK, None, :], pi[:, :CHUNK, None, :]
    clr = c_re[:, None] * p0r - c_im[:, None] * p0i
    cli = c_re[:, None] * p0i + c_im[:, None] * p0r
    clcat = jnp.concatenate([clr, -cli], axis=-1).reshape(SSM_G, CW, 2 * SSM_P)
    kk = jnp.einsum("gmp,gpk->gmk", clcat, jnp.concatenate([bbr, bbi], axis=1), precision=hp)
    kl = kk.reshape(SSM_G, CHUNK, SSM_H, SSM_H).transpose(0, 3, 2, 1)
    wide = jnp.concatenate([jnp.zeros_like(kl), kl, jnp.zeros_like(kl[..., :1])], axis=-1)
    skew = jnp.tile(wide, (1, 1, 1, CHUNK))[..., :2 * CHUNK * CHUNK]
    skew = skew.reshape(SSM_G, SSM_H, SSM_H, CHUNK, 2 * CHUNK)[..., CHUNK:]
    mt = skew.transpose(0, 3, 1, 4, 2).reshape(SSM_G, CW, CW)
    rr, ri = pr[:, CHUNK - 1::-1][:, :, None, :], pi[:, CHUNK - 1::-1][:, :, None, :]
    bbrt, bbit = bbr.transpose(0, 2, 1)[:, None], bbi.transpose(0, 2, 1)[:, None]
    sr = (rr * bbrt - ri * bbit).reshape(SSM_G, CW, SSM_P)
    si = (rr * bbit + ri * bbrt).reshape(SSM_G, CW, SSM_P)
    scat = jnp.concatenate([sr, si], axis=-1)
    ctr, cti = c_re.transpose(0, 2, 1), c_im.transpose(0, 2, 1)
    p1r, p1i = pr[:, 1:].transpose(0, 2, 1), pi[:, 1:].transpose(0, 2, 1)
    o_r = (ctr[:, :, None, :] * p1r[..., None] - cti[:, :, None, :] * p1i[..., None]).reshape(SSM_G, SSM_P, CW)
    o_i = (ctr[:, :, None, :] * p1i[..., None] + cti[:, :, None, :] * p1r[..., None]).reshape(SSM_G, SSM_P, CW)
    ocat = jnp.concatenate([o_r, -o_i], axis=1)
    a16 = jnp.concatenate([pr[:, CHUNK], pi[:, CHUNK]], axis=-1)[:, None, :]
    return mt, scat, ocat, a16


def _cmul_const(xv, ar, ai):
    return xv * ar + pltpu.roll(xv, SSM_P, 1) * ai


def _chunk_scan(inc, a_row, reverse):
    n = inc.shape[0]
    lane = lax.broadcasted_iota(jnp.int32, (1, 2 * SSM_P), 1)
    row = lax.broadcasted_iota(jnp.int32, inc.shape, 0)
    sign = jnp.where(lane < SSM_P, -1.0, 1.0)
    ar = jnp.where(lane < SSM_P, a_row, pltpu.roll(a_row, SSM_P, 1))
    ai = jnp.where(lane < SSM_P, pltpu.roll(a_row, SSM_P, 1), a_row)
    if reverse:
        ai = -ai
    xv = inc
    s = 1
    while s < n:
        if reverse:
            sh = jnp.where(row < n - s, pltpu.roll(xv, n - s, 0), 0.0)
        else:
            sh = jnp.where(row >= s, pltpu.roll(xv, s, 0), 0.0)
        xv = xv + _cmul_const(sh, ar, ai * sign)
        ar, ai = ar * ar - ai * ai, 2.0 * ar * ai
        s *= 2
    return xv


def _shift_rows(xv, reverse):
    n = xv.shape[0]
    row = lax.broadcasted_iota(jnp.int32, xv.shape, 0)
    if reverse:
        return jnp.where(row < n - 1, pltpu.roll(xv, n - 1, 0), 0.0)
    return jnp.where(row >= 1, pltpu.roll(xv, 1, 0), 0.0)


GB = 128 // SSM_H
U_COL0 = (ATTN_W + 2 * KV_W + ATTN_W) // 128


HALF = CHUNK // 2


def _chunk_perm():
    r = jnp.arange(HALF * 128)
    t, g8, h = r // 128, (r % 128) // SSM_H, r % SSM_H
    return ((g8 * 128 + t * SSM_H + h)[:, None] == jnp.arange(GB * 128)[None, :]).astype(BF16)


def _load_perm(p_hbm, p_scr, sem):
    @pl.when(pl.program_id(0) == 0)
    def _():
        cp = pltpu.make_async_copy(p_hbm, p_scr, sem)
        cp.start()
        cp.wait()


def _rows_to_chunks(pieces, perm):
    halves = [jnp.dot(jnp.concatenate(pieces[k * HALF:(k + 1) * HALF], axis=1).astype(BF16), perm,
                      preferred_element_type=F32).astype(BF16) for k in range(2)]
    return [jnp.concatenate([hv[:, g * 128:(g + 1) * 128] for hv in halves], axis=1) for g in range(GB)]


def _chunks_to_rows(groups, perm, two_pass):
    pieces = []
    for k in range(2):
        v = jnp.concatenate([gv[:, k * 128:(k + 1) * 128] for gv in groups], axis=1)
        hi = v.astype(BF16)
        out = lax.dot_general(hi, perm, _NT, preferred_element_type=F32)
        if two_pass:
            lo = (v - hi.astype(F32)).astype(BF16)
            out = out + lax.dot_general(lo, perm, _NT, preferred_element_type=F32)
        pieces += [out[:, t * 128:(t + 1) * 128] for t in range(HALF)]
    return pieces


def _ssm_fwd(proj, perm, mt, scat, ocat, a16, d_skip):
    L = proj.shape[0]
    nc = L // CHUNK

    def body(u_ref, p_hbm, mt_ref, s_ref, o_ref, a_ref, d_ref, y_ref, yg_ref, h_ref, p_scr, sem):
        _load_perm(p_hbm, p_scr, sem)
        perm = p_scr[...]
        rows = [pl.ds(t, nc, stride=CHUNK) for t in range(CHUNK)]
        ua = _rows_to_chunks([u_ref[r, :] for r in rows], perm)
        ys = []
        for g in range(GB):
            uv = ua[g]
            inc = jnp.dot(uv, s_ref[g], preferred_element_type=F32)
            hx = _shift_rows(_chunk_scan(inc, a_ref[g], False), False)
            h_ref[g] = hx
            ys.append(jnp.dot(uv, mt_ref[g], preferred_element_type=F32)
                      + jnp.dot(hx.astype(BF16), o_ref[g], preferred_element_type=F32))
        yp = _chunks_to_rows(ys, perm, True)
        for t, r in enumerate(rows):
            y = yp[t] + d_ref[...] * u_ref[r, :]
            y_ref[r, :] = y
            yg_ref[r, :] = _gelu(y)

    g3 = lambda r, c: pl.BlockSpec((GB, r, c), lambda g: (g, 0, 0))
    col = pl.BlockSpec((L, 128), lambda g: (0, g))
    return pl.pallas_call(
        body,
        name="ssm_fwd",
        grid=(SSM_G // GB,),
        in_specs=[pl.BlockSpec((L, 128), lambda g: (0, U_COL0 + g)), _ANY,
                  g3(CW, CW), g3(CW, 2 * SSM_P), g3(2 * SSM_P, CW), g3(1, 2 * SSM_P),
                  pl.BlockSpec((1, 128), lambda g: (0, g))],
        out_specs=[col, col, g3(nc, 2 * SSM_P)],
        out_shape=[jax.ShapeDtypeStruct((L, SSM_W), F32), jax.ShapeDtypeStruct((L, SSM_W), F32),
                   jax.ShapeDtypeStruct((SSM_G, nc, 2 * SSM_P), F32)],
        scratch_shapes=[pltpu.VMEM((HALF * 128, GB * 128), BF16), pltpu.SemaphoreType.DMA],
        compiler_params=_cp(("arbitrary",)),
    )(proj, perm, mt, scat, ocat, a16, d_skip.reshape(1, SSM_W))


def _ssm_bwd(d_yg, y, proj, hx, perm, mt, scat, ocat, a16, d_skip):
    L = proj.shape[0]
    nc = L // CHUNK

    def body(dg_ref, y_ref, u_ref, h_ref, p_hbm, mt_ref, s_ref, o_ref, a_ref, d_ref,
             du_ref, gmt_ref, gs_ref, go_ref, ga_ref, gd_ref, p_scr, sem):
        _load_perm(p_hbm, p_scr, sem)
        perm = p_scr[...]
        rows = [pl.ds(t, nc, stride=CHUNK) for t in range(CHUNK)]
        us = [u_ref[r, :] for r in rows]
        dys = [dg_ref[r, :] * _dgelu(y_ref[r, :]) for r in rows]
        gd = jnp.zeros((1, 128), F32)
        for uv, dy in zip(us, dys):
            gd = gd + jnp.sum(dy * uv, axis=0, keepdims=True)
        gd_ref[...] = gd
        ua = _rows_to_chunks(us, perm)
        dya = _rows_to_chunks(dys, perm)
        lane = lax.broadcasted_iota(jnp.int32, (1, 2 * SSM_P), 1)
        dus = []
        for g in range(GB):
            uv, dy, hx_v = ua[g], dya[g], h_ref[g]
            dh = lax.dot_general(dy, o_ref[g], _NT, preferred_element_type=F32)
            dinc = _shift_rows(_chunk_scan(dh, a_ref[g], True), True)
            dinc_b = dinc.astype(BF16)
            dus.append(lax.dot_general(dy, mt_ref[g], _NT, preferred_element_type=F32)
                       + lax.dot_general(dinc_b, s_ref[g], _NT, preferred_element_type=F32))
            gmt_ref[g] = lax.dot_general(uv, dy, _TN, preferred_element_type=F32)
            gs_ref[g] = lax.dot_general(uv, dinc_b, _TN, preferred_element_type=F32)
            go_ref[g] = lax.dot_general(hx_v.astype(BF16), dy, _TN, preferred_element_type=F32)
            p1 = dinc * hx_v
            p2 = pltpu.roll(dinc, SSM_P, 1) * hx_v
            t1 = jnp.sum(p1 + pltpu.roll(p1, SSM_P, 1), axis=0, keepdims=True)
            t2 = jnp.sum(p2 - pltpu.roll(p2, SSM_P, 1), axis=0, keepdims=True)
            ga_ref[g] = jnp.where(lane < SSM_P, t1, pltpu.roll(t2, SSM_P, 1))
        dup = _chunks_to_rows(dus, perm, False)
        for t, r in enumerate(rows):
            du_ref[r, :] = dup[t] + d_ref[...] * dys[t]

    g3 = lambda r, c: pl.BlockSpec((GB, r, c), lambda g: (g, 0, 0))
    col = pl.BlockSpec((L, 128), lambda g: (0, g))
    row = pl.BlockSpec((1, 128), lambda g: (0, g))
    return pl.pallas_call(
        body,
        name="ssm_bwd",
        grid=(SSM_G // GB,),
        in_specs=[col, col, pl.BlockSpec((L, 128), lambda g: (0, U_COL0 + g)), g3(nc, 2 * SSM_P), _ANY,
                  g3(CW, CW), g3(CW, 2 * SSM_P), g3(2 * SSM_P, CW), g3(1, 2 * SSM_P), row],
        out_specs=[col, g3(CW, CW), g3(CW, 2 * SSM_P), g3(2 * SSM_P, CW), g3(1, 2 * SSM_P), row],
        out_shape=[jax.ShapeDtypeStruct((L, SSM_W), F32), jax.ShapeDtypeStruct((SSM_G, CW, CW), F32),
                   jax.ShapeDtypeStruct((SSM_G, CW, 2 * SSM_P), F32),
                   jax.ShapeDtypeStruct((SSM_G, 2 * SSM_P, CW), F32),
                   jax.ShapeDtypeStruct((SSM_G, 1, 2 * SSM_P), F32),
                   jax.ShapeDtypeStruct((1, SSM_W), F32)],
        scratch_shapes=[pltpu.VMEM((HALF * 128, GB * 128), BF16), pltpu.SemaphoreType.DMA],
        compiler_params=_cp(("arbitrary",)),
    )(d_yg, y, proj, hx, perm, mt, scat, ocat, a16, d_skip.reshape(1, SSM_W))


def _merge(og, yg, gpre, proj, b_glu, wa, ws):
    L = og.shape[0]
    tm = _tile(L, 256)

    def body(og_ref, yg_ref, gp_ref, z0_ref, z1_ref, b_ref, wa_ref, ws_ref, m_ref):
        zs = jnp.concatenate([z0_ref[...], z1_ref[...]], axis=1)
        os_ = yg_ref[...] * _sigmoid(gp_ref[...] + b_ref[...]) * _silu(zs)
        ogv = og_ref[...]
        ra = lax.rsqrt(jnp.mean(ogv * ogv, axis=-1, keepdims=True) + NORM_EPS)
        rs = lax.rsqrt(jnp.mean(os_ * os_, axis=-1, keepdims=True) + NORM_EPS)
        m_ref[:, :ATTN_W] = (ogv * ra * wa_ref[...]).astype(BF16)
        m_ref[:, ATTN_W:] = (os_ * rs * ws_ref[...]).astype(BF16)

    row = lambda w: pl.BlockSpec((1, w), lambda i: (0, 0))
    return pl.pallas_call(
        body,
        name="merge",
        grid=(L // tm,),
        in_specs=[pl.BlockSpec((tm, ATTN_W), lambda i: (i, 0)), pl.BlockSpec((tm, SSM_W), lambda i: (i, 0)),
                  pl.BlockSpec((tm, SSM_W), lambda i: (i, 0)),
                  pl.BlockSpec((tm, 512), lambda i: (i, 7)), pl.BlockSpec((tm, 512), lambda i: (i, 8)),
                  row(SSM_W), row(ATTN_W), row(SSM_W)],
        out_specs=pl.BlockSpec((tm, D_MODEL), lambda i: (i, 0)),
        out_shape=jax.ShapeDtypeStruct((L, D_MODEL), BF16),
        compiler_params=_cp(("parallel",)),
    )(og, yg, gpre, proj, proj, b_glu.reshape(1, SSM_W), wa.reshape(1, ATTN_W), ws.reshape(1, SSM_W))


def _outproj_loss(merged, w_out, x, target):
    L = x.shape[0]
    tm, tn = _tile(L, 512), 1024
    ni, nj = L // tm, D_MODEL // tn

    def body(m_ref, w_ref, x_ref, t_ref, d_ref, db_ref, l_ref):
        out = x_ref[...] + jnp.dot(m_ref[...], w_ref[...], preferred_element_type=F32)
        diff = out - t_ref[...]
        d = diff * (1.0 / D_MODEL)
        d_ref[...] = d
        db_ref[...] = d.astype(BF16)
        l_ref[...] = jnp.full((1, 8, 128), jnp.sum(diff * diff), F32)

    return pl.pallas_call(
        body,
        name="outproj_loss",
        grid=(nj, ni),
        in_specs=[pl.BlockSpec((tm, D_MODEL), lambda j, i: (i, 0)),
                  pl.BlockSpec((D_MODEL, tn), lambda j, i: (0, j)),
                  pl.BlockSpec((tm, tn), lambda j, i: (i, j)),
                  pl.BlockSpec((tm, tn), lambda j, i: (i, j))],
        out_specs=[pl.BlockSpec((tm, tn), lambda j, i: (i, j)), pl.BlockSpec((tm, tn), lambda j, i: (i, j)),
                   pl.BlockSpec((1, 8, 128), lambda j, i: (i * nj + j, 0, 0))],
        out_shape=[jax.ShapeDtypeStruct((L, D_MODEL), F32), jax.ShapeDtypeStruct((L, D_MODEL), BF16),
                   jax.ShapeDtypeStruct((ni * nj, 8, 128), F32)],
        compiler_params=_cp(("parallel", "parallel")),
    )(merged, w_out, x, target)


def _merge_bwd(d_m, og, o, yg, gpre, proj, b_glu, wa, ws):
    L = og.shape[0]
    tm = _tile(L, 256)

    def body(dm_ref, og_ref, o_ref, yg_ref, gp_ref, za0_ref, za1_ref, zs0_ref, zs1_ref, b_ref, wa_ref, ws_ref,
             do_ref, dza_ref, dzs_ref, dg_ref, dyg_ref, gwa_ref, gws_ref, gb_ref):
        i = pl.program_id(0)

        @pl.when(i == 0)
        def _():
            gwa_ref[...] = jnp.zeros_like(gwa_ref)
            gws_ref[...] = jnp.zeros_like(gws_ref)
            gb_ref[...] = jnp.zeros_like(gb_ref)

        za = jnp.concatenate([za0_ref[...], za1_ref[...]], axis=1)
        zs = jnp.concatenate([zs0_ref[...], zs1_ref[...]], axis=1)
        ogv, dma = og_ref[...], dm_ref[:, :ATTN_W]
        ra = lax.rsqrt(jnp.mean(ogv * ogv, axis=-1, keepdims=True) + NORM_EPS)
        xh = ogv * ra
        gwa_ref[...] += jnp.sum(dma * xh, axis=0, keepdims=True)
        gx = dma * wa_ref[...]
        d_og = ra * (gx - xh * jnp.mean(gx * xh, axis=-1, keepdims=True))
        do_ref[...] = d_og * _silu(za)
        dza_ref[...] = (d_og * o_ref[...] * _dsilu(za)).astype(BF16)
        ygv = yg_ref[...]
        sg = _sigmoid(gp_ref[...] + b_ref[...])
        y2 = ygv * sg
        sz = _silu(zs)
        os_ = y2 * sz
        dms = dm_ref[:, ATTN_W:]
        rs = lax.rsqrt(jnp.mean(os_ * os_, axis=-1, keepdims=True) + NORM_EPS)
        xs = os_ * rs
        gws_ref[...] += jnp.sum(dms * xs, axis=0, keepdims=True)
        gxs = dms * ws_ref[...]
        d_os = rs * (gxs - xs * jnp.mean(gxs * xs, axis=-1, keepdims=True))
        dzs_ref[...] = (d_os * y2 * _dsilu(zs)).astype(BF16)
        d_y2 = d_os * sz
        d_g = d_y2 * ygv * sg * (1.0 - sg)
        dg_ref[...] = d_g.astype(BF16)
        gb_ref[...] += jnp.sum(d_g, axis=0, keepdims=True)
        dyg_ref[...] = d_y2 * sg

    row = lambda w: pl.BlockSpec((1, w), lambda i: (0, 0))
    full = lambda w: pl.BlockSpec((tm, w), lambda i: (i, 0))
    half = lambda c: pl.BlockSpec((tm, 512), lambda i: (i, c))
    return pl.pallas_call(
        body,
        name="merge_bwd",
        grid=(L // tm,),
        in_specs=[full(D_MODEL), full(ATTN_W), full(ATTN_W), full(SSM_W), full(SSM_W),
                  half(3), half(4), half(7), half(8), row(SSM_W), row(ATTN_W), row(SSM_W)],
        out_specs=[full(ATTN_W), full(ATTN_W), full(SSM_W), full(SSM_W), full(SSM_W),
                   row(ATTN_W), row(SSM_W), row(SSM_W)],
        out_shape=[jax.ShapeDtypeStruct((L, ATTN_W), F32), jax.ShapeDtypeStruct((L, ATTN_W), BF16),
                   jax.ShapeDtypeStruct((L, SSM_W), BF16), jax.ShapeDtypeStruct((L, SSM_W), BF16),
                   jax.ShapeDtypeStruct((L, SSM_W), F32),
                   jax.ShapeDtypeStruct((1, ATTN_W), F32), jax.ShapeDtypeStruct((1, SSM_W), F32),
                   jax.ShapeDtypeStruct((1, SSM_W), F32)],
        compiler_params=_cp(("arbitrary",)),
    )(d_m, og, o, yg, gpre, proj, proj, proj, proj, b_glu.reshape(1, SSM_W), wa.reshape(1, ATTN_W),
      ws.reshape(1, SSM_W))


def _rms_bwd_x(x, norm_w, d_hn, d_out):
    L = x.shape[0]
    tm = _tile(L, 256)

    def body(x_ref, w_ref, dh_ref, do_ref, gx_ref, gw_ref):
        i = pl.program_id(0)

        @pl.when(i == 0)
        def _():
            gw_ref[...] = jnp.zeros_like(gw_ref)

        xv, dh = x_ref[...], dh_ref[...]
        r = lax.rsqrt(jnp.mean(xv * xv, axis=-1, keepdims=True) + NORM_EPS)
        xh = xv * r
        gw_ref[...] += jnp.sum(dh * xh, axis=0, keepdims=True)
        gx = dh * w_ref[...]
        gx_ref[...] = do_ref[...] + r * (gx - xh * jnp.mean(gx * xh, axis=-1, keepdims=True))

    blk = pl.BlockSpec((tm, D_MODEL), lambda i: (i, 0))
    row = pl.BlockSpec((1, D_MODEL), lambda i: (0, 0))
    return pl.pallas_call(
        body, name="rms_bwd_x", grid=(L // tm,), in_specs=[blk, row, blk, blk], out_specs=[blk, row],
        out_shape=[jax.ShapeDtypeStruct((L, D_MODEL), F32), jax.ShapeDtypeStruct((1, D_MODEL), F32)],
        compiler_params=_cp(("arbitrary",)),
    )(x, norm_w.reshape(1, D_MODEL), d_hn, d_out)


def _rope_table(positions):
    inv_freq = ROPE_THETA ** (-jnp.arange(0, HEAD_DIM, 2, dtype=F32) / HEAD_DIM)
    ang = positions.astype(F32)[:, None] * inv_freq
    c, s = jnp.cos(ang), jnp.sin(ang)
    return jnp.concatenate([c, c, c, c, -s, s, -s, s], axis=1)


def _local_step(x, positions, target, small, wt_in, w_glu, w_out):
    tab = _rope_table(positions)
    ssm_names = ("a_re", "a_im", "log_step", "b_re", "b_im", "c_re", "c_im")
    ops, ops_vjp = jax.vjp(_ssm_ops, *[small[n] for n in ssm_names])
    mt, scat, ocat, a16 = ops
    mt_b, scat_b, ocat_b = mt.astype(BF16), scat.astype(BF16), ocat.astype(BF16)
    perm = _chunk_perm()

    proj, hn = _rms_inproj(x, small["norm_w"], wt_in)
    q_rot, k_rot = _qk_prep(proj, tab, small["q_norm_w"], small["k_norm_w"])
    og, o, lse = _attn_fwd(q_rot, k_rot, proj, small["sinks"])
    y, yg, hx = _ssm_fwd(proj, perm, mt_b, scat_b, ocat_b, a16, small["d_skip"])
    gpre = _mm(yg, w_glu, "nn", F32, "glu_fwd")
    merged = _merge(og, yg, gpre, proj, small["b_glu"], small["attn_out_norm_w"], small["ssm_out_norm_w"])
    d_out, d_out_b, loss_parts = _outproj_loss(merged, w_out, x, target)
    loss = 0.5 * jnp.sum(loss_parts[:, 0, 0]) / D_MODEL

    g_w_out = _mm(merged, d_out_b, "tn", F32, "grad_w_out")
    d_m = _mm(d_out_b, w_out, "nt", F32, "d_merged")
    d_o, d_za, d_zs, d_g, d_yg1, g_wa, g_ws, g_bglu = _merge_bwd(
        d_m, og, o, yg, gpre, proj, small["b_glu"], small["attn_out_norm_w"], small["ssm_out_norm_w"])
    g_w_glu = _mm(yg, d_g, "tn", F32, "grad_w_glu")
    d_yg = _mm(d_g, w_glu, "nt", F32, "d_yg", add=d_yg1)
    d_u, g_mt, g_scat, g_ocat, g_a16, g_dskip = _ssm_bwd(d_yg, y, proj, hx, perm, mt_b, scat_b, ocat_b, a16,
                                                         small["d_skip"])
    g_ssm = ops_vjp((g_mt, g_scat, g_ocat, g_a16))
    d_q, d_k, d_v, g_sinks = _attn_bwd(q_rot, k_rot, proj, small["sinks"], d_o, o, lse)
    d_qkv, g_qw, g_kw = _qk_prep_bwd(proj, tab, small["q_norm_w"], small["k_norm_w"], d_q, d_k, d_v)
    g_qw = g_qw[0, :HEAD_DIM] + g_qw[0, HEAD_DIM:]
    g_kw = g_kw[0, :HEAD_DIM] + g_kw[0, HEAD_DIM:]
    d_proj = jnp.concatenate([d_qkv, d_za, d_u.astype(BF16), d_zs], axis=1)
    g_wt_in = _mm(d_proj, hn, "tn", F32, "grad_w_in")
    d_hn = _mm(d_proj, wt_in, "nn", F32, "d_hn")
    grad_x, g_nw = _rms_bwd_x(x, small["norm_w"], d_hn, d_out)

    g_small = dict(zip(ssm_names, g_ssm))
    g_small.update(norm_w=g_nw.reshape(-1), q_norm_w=g_qw.reshape(-1), k_norm_w=g_kw.reshape(-1),
                   sinks=g_sinks[0, :N_HEADS], d_skip=g_dskip.reshape(-1), b_glu=g_bglu.reshape(-1),
                   attn_out_norm_w=g_wa.reshape(-1), ssm_out_norm_w=g_ws.reshape(-1))
    return loss, grad_x, g_wt_in, g_w_glu, g_w_out, g_small


_ANY = pl.BlockSpec(memory_space=pl.ANY)


def _all_gather_rows(blocks, name):
    n = len(blocks)

    def body(*refs):
        ins, outs = refs[:n], refs[n:2 * n]
        send_sems, recv_sems, local_sems = refs[2 * n:]
        x, y, c = lax.axis_index("x"), lax.axis_index("y"), lax.axis_index("c")
        me, sibling = (x, y, c), (x, y, 1 - c)
        chips = [(1 - x, y), (x, 1 - y), (1 - x, 1 - y)]

        def slot(k, dev):
            return outs[k].at[4 * dev[0] + 2 * dev[1] + dev[2]]

        def copy(k, q, block, to, src=None):
            return pltpu.make_async_remote_copy(
                src_ref=slot(k, block) if src is None else src, dst_ref=slot(k, block),
                send_sem=send_sems.at[k, q], recv_sem=recv_sems.at[k, q], device_id=to, device_id_type=MESH)

        mine = [pltpu.make_async_copy(ins[k], slot(k, me), local_sems.at[k]) for k in range(n)]
        for cp in mine:
            cp.start()
        first = []
        for k in range(n):
            first.append(copy(k, 0, me, sibling, src=ins[k]))
            first += [copy(k, 1 + j, me, (*chip, c), src=ins[k]) for j, chip in enumerate(chips)]
        for cp in first:
            cp.start()
        passed = []
        for j, chip in enumerate(chips):
            for k in range(n):
                copy(k, 1 + j, (*chip, c), me).wait_recv()
                fwd = copy(k, 4 + j, (*chip, c), sibling)
                fwd.start()
                passed.append(fwd)
        for k in range(n):
            copy(k, 0, sibling, me).wait_recv()
            for j, chip in enumerate(chips):
                copy(k, 4 + j, (*chip, 1 - c), me).wait_recv()
        for cp in first + passed:
            cp.wait_send()
        for cp in mine:
            cp.wait()

    outs = pl.pallas_call(
        body,
        name=name,
        in_specs=[_ANY] * n,
        out_specs=[_ANY] * n,
        out_shape=[jax.ShapeDtypeStruct((N_DEV,) + b.shape, b.dtype) for b in blocks],
        scratch_shapes=[pltpu.SemaphoreType.DMA((n, 7)), pltpu.SemaphoreType.DMA((n, 7)),
                        pltpu.SemaphoreType.DMA((n,))],
    )(*blocks)
    return list(outs)


def _pair_exchange(grads, name):
    n = len(grads)

    def body(*refs):
        ins, outs = refs[:n], refs[n:2 * n]
        send_sems, recv_sems = refs[2 * n:]
        x, y, c = lax.axis_index("x"), lax.axis_index("y"), lax.axis_index("c")
        copies = []
        for k in range(n):
            for chip in range(4):
                copies.append(pltpu.make_async_remote_copy(
                    src_ref=ins[k].at[2 * chip + (1 - c)], dst_ref=outs[k].at[chip],
                    send_sem=send_sems.at[k, chip], recv_sem=recv_sems.at[k, chip],
                    device_id=(x, y, 1 - c), device_id_type=MESH))
        for cp in copies:
            cp.start()
        for cp in copies:
            cp.wait()

    outs = pl.pallas_call(
        body,
        name=name,
        in_specs=[_ANY] * n,
        out_specs=[_ANY] * n,
        out_shape=[jax.ShapeDtypeStruct((4,) + g.shape[1:], g.dtype) for g in grads],
        scratch_shapes=[pltpu.SemaphoreType.DMA((n, 4)), pltpu.SemaphoreType.DMA((n, 4))],
    )(*grads)
    return list(outs)


def _pair_sum(g, ra, core, out_dtype, name):
    _, r, C = g.shape
    tr = _tile(r, 128)

    def body(c_ref, g_ref, ra_ref, p_ref):
        p_ref[...] = (g_ref[...] + ra_ref[...]).astype(p_ref.dtype)

    return pl.pallas_call(
        body,
        name=name,
        grid_spec=pltpu.PrefetchScalarGridSpec(
            num_scalar_prefetch=1,
            grid=(4, r // tr),
            in_specs=[pl.BlockSpec((1, tr, C), lambda j, t, c_ref: (2 * j + c_ref[0], t, 0)),
                      pl.BlockSpec((1, tr, C), lambda j, t, c_ref: (j, t, 0))],
            out_specs=pl.BlockSpec((1, tr, C), lambda j, t, c_ref: (j, t, 0)),
        ),
        out_shape=jax.ShapeDtypeStruct((4, r, C), out_dtype),
        compiler_params=_cp(("parallel", "parallel")),
    )(core, g, ra)


def _chip_exchange(parts, name):
    n = len(parts)

    def body(*refs):
        ins, outs = refs[:n], refs[n:2 * n]
        send_sems, recv_sems = refs[2 * n:]
        x, y, c = lax.axis_index("x"), lax.axis_index("y"), lax.axis_index("c")
        chips = [(1 - x, y), (x, 1 - y), (1 - x, 1 - y)]
        copies = []
        for k in range(n):
            for q, chip in enumerate(chips):
                copies.append(pltpu.make_async_remote_copy(
                    src_ref=ins[k].at[2 * chip[0] + chip[1]], dst_ref=outs[k].at[q],
                    send_sem=send_sems.at[k, q], recv_sem=recv_sems.at[k, q],
                    device_id=(*chip, c), device_id_type=MESH))
        for cp in copies:
            cp.start()
        for cp in copies:
            cp.wait()

    outs = pl.pallas_call(
        body,
        name=name,
        in_specs=[_ANY] * n,
        out_specs=[_ANY] * n,
        out_shape=[jax.ShapeDtypeStruct((3,) + p.shape[1:], p.dtype) for p in parts],
        scratch_shapes=[pltpu.SemaphoreType.DMA((n, 3)), pltpu.SemaphoreType.DMA((n, 3))],
    )(*parts)
    return list(outs)


def _chip_sum(p, rb, chip, name):
    _, r, C = p.shape
    tr = _tile(r, 128)

    def body(c_ref, p_ref, rb_ref, o_ref):
        acc = p_ref[0].astype(F32) + rb_ref[0].astype(F32)
        acc = acc + rb_ref[1].astype(F32)
        o_ref[...] = acc + rb_ref[2].astype(F32)

    return pl.pallas_call(
        body,
        name=name,
        grid_spec=pltpu.PrefetchScalarGridSpec(
            num_scalar_prefetch=1,
            grid=(r // tr,),
            in_specs=[pl.BlockSpec((1, tr, C), lambda t, c_ref: (c_ref[0], t, 0)),
                      pl.BlockSpec((3, tr, C), lambda t, c_ref: (0, t, 0))],
            out_specs=pl.BlockSpec((tr, C), lambda t, c_ref: (t, 0)),
        ),
        out_shape=jax.ShapeDtypeStruct((r, C), F32),
        compiler_params=_cp(("parallel",)),
    )(chip, p, rb)


def _adamw(g, w, m, v, name):
    R, C = g.shape
    tr = _tile(R, 256)
    c1 = 1.0 - ADAM_B1 ** ADAM_STEP
    c2 = 1.0 - ADAM_B2 ** ADAM_STEP

    def body(g_ref, w_ref, m_ref, v_ref, d_ref, nm_ref, nv_ref):
        gv = g_ref[...]
        nm = ADAM_B1 * m_ref[...] + (1.0 - ADAM_B1) * gv
        nv = ADAM_B2 * v_ref[...] + (1.0 - ADAM_B2) * (gv * gv)
        nm_ref[...] = nm
        nv_ref[...] = nv
        d_ref[...] = -ADAM_LR * ((nm / c1) / (jnp.sqrt(nv / c2) + ADAM_EPS) + ADAM_WD * w_ref[...])

    blk = pl.BlockSpec((tr, C), lambda i: (i, 0))
    return pl.pallas_call(
        body, name=name, grid=(R // tr,), in_specs=[blk] * 4, out_specs=[blk] * 3,
        out_shape=[jax.ShapeDtypeStruct((R, C), F32)] * 3, compiler_params=_cp(("parallel",)),
    )(g, w, m, v)


_SMALL = ("norm_w", "q_norm_w", "k_norm_w", "sinks", "a_re", "a_im", "log_step", "b_re", "b_im", "c_re", "c_im",
          "d_skip", "b_glu", "attn_out_norm_w", "ssm_out_norm_w")
_WEIGHTS = ("norm_w", "w_in", "q_norm_w", "k_norm_w", "sinks", "a_re", "a_im", "log_step", "b_re", "b_im", "c_re",
            "c_im", "d_skip", "w_glu", "b_glu", "attn_out_norm_w", "ssm_out_norm_w", "w_out")
_SMALL_2D = dict(norm_w=(1, 2048), q_norm_w=(1, 64), k_norm_w=(1, 64), sinks=(1, 16), a_re=(64, 64), a_im=(64, 64),
                 log_step=(1, 64), b_re=(4096, 16), b_im=(4096, 16), c_re=(1024, 64), c_im=(1024, 64),
                 d_skip=(1, 1024), b_glu=(1, 1024), attn_out_norm_w=(1, 1024), ssm_out_norm_w=(1, 1024))


def _slab_rows(n):
    return -(-n // 1024) * 8


_PACK_ROWS = 2304


def _pack(d):
    parts = []
    for n in _SMALL:
        flat = d[n].reshape(-1).astype(F32)
        rows = _slab_rows(flat.shape[0])
        parts.append(jnp.pad(flat, (0, rows * 128 - flat.shape[0])).reshape(rows, 128))
    used = sum(p.shape[0] for p in parts)
    parts.append(jnp.zeros((_PACK_ROWS - used, 128), F32))
    return jnp.concatenate(parts, axis=0)


def _unpack(packed, like):
    out, off = {}, 0
    for n in _SMALL:
        size = math.prod(like[n].shape)
        rows = _slab_rows(size)
        out[n] = packed[off:off + rows].reshape(-1)[:size].reshape(like[n].shape)
        off += rows
    return out


def _adamw_small(g, w, m, v):
    c1 = 1.0 - ADAM_B1 ** ADAM_STEP
    c2 = 1.0 - ADAM_B2 ** ADAM_STEP
    k = len(_SMALL)

    def body(*refs):
        ins, outs = refs[:4 * k], refs[4 * k:]
        for j in range(k):
            gv, wv, mv, vv = (ins[q * k + j][...] for q in range(4))
            nm = ADAM_B1 * mv + (1.0 - ADAM_B1) * gv
            nv = ADAM_B2 * vv + (1.0 - ADAM_B2) * (gv * gv)
            outs[j][...] = -ADAM_LR * ((nm / c1) / (jnp.sqrt(nv / c2) + ADAM_EPS) + ADAM_WD * wv)
            outs[k + j][...] = nm
            outs[2 * k + j][...] = nv

    args = [d[n].reshape(_SMALL_2D[n]) for d in (g, w, m, v) for n in _SMALL]
    shapes = [jax.ShapeDtypeStruct(_SMALL_2D[n], F32) for _ in range(3) for n in _SMALL]
    outs = pl.pallas_call(body, name="adamw_small", out_shape=shapes, compiler_params=_cp())(*args)
    res = []
    for q in range(3):
        res.append({n: outs[q * k + j].reshape(w[n].shape) for j, n in enumerate(_SMALL)})
    return res


def kernel(x, positions, norm_w, w_in, q_norm_w, k_norm_w, sinks, a_re, a_im, log_step, b_re, b_im, c_re, c_im, d_skip, w_glu, b_glu, attn_out_norm_w, ssm_out_norm_w, w_out, loss_target, m_norm_w, m_w_in, m_q_norm_w, m_k_norm_w, m_sinks, m_a_re, m_a_im, m_log_step, m_b_re, m_b_im, m_c_re, m_c_im, m_d_skip, m_w_glu, m_b_glu, m_attn_out_norm_w, m_ssm_out_norm_w, m_w_out, v_norm_w, v_w_in, v_q_norm_w, v_k_norm_w, v_sinks, v_a_re, v_a_im, v_log_step, v_b_re, v_b_im, v_c_re, v_c_im, v_d_skip, v_w_glu, v_b_glu, v_attn_out_norm_w, v_ssm_out_norm_w, v_w_out):
    w = dict(norm_w=norm_w, w_in=w_in, q_norm_w=q_norm_w, k_norm_w=k_norm_w, sinks=sinks, a_re=a_re, a_im=a_im,
             log_step=log_step, b_re=b_re, b_im=b_im, c_re=c_re, c_im=c_im, d_skip=d_skip, w_glu=w_glu, b_glu=b_glu,
             attn_out_norm_w=attn_out_norm_w, ssm_out_norm_w=ssm_out_norm_w, w_out=w_out)
    m = dict(norm_w=m_norm_w, w_in=m_w_in, q_norm_w=m_q_norm_w, k_norm_w=m_k_norm_w, sinks=m_sinks, a_re=m_a_re,
             a_im=m_a_im, log_step=m_log_step, b_re=m_b_re, b_im=m_b_im, c_re=m_c_re, c_im=m_c_im, d_skip=m_d_skip,
             w_glu=m_w_glu, b_glu=m_b_glu, attn_out_norm_w=m_attn_out_norm_w, ssm_out_norm_w=m_ssm_out_norm_w,
             w_out=m_w_out)
    v = dict(norm_w=v_norm_w, w_in=v_w_in, q_norm_w=v_q_norm_w, k_norm_w=v_k_norm_w, sinks=v_sinks, a_re=v_a_re,
             a_im=v_a_im, log_step=v_log_step, b_re=v_b_re, b_im=v_b_im, c_re=v_c_re, c_im=v_c_im, d_skip=v_d_skip,
             w_glu=v_w_glu, b_glu=v_b_glu, attn_out_norm_w=v_attn_out_norm_w, ssm_out_norm_w=v_ssm_out_norm_w,
             w_out=v_w_out)
    core = lax.axis_index("c").astype(jnp.int32).reshape(1)
    chip = (2 * lax.axis_index("x") + lax.axis_index("y")).astype(jnp.int32).reshape(1)

    wt_in, wf_glu, wf_out = _all_gather_rows(
        [w_in.T.astype(BF16), w_glu.astype(BF16), w_out.astype(BF16)], "gather_weights")
    wt_in = wt_in.reshape(IN_W, D_MODEL)
    wf_glu = wf_glu.reshape(SSM_W, SSM_W)
    wf_out = wf_out.reshape(D_MODEL, D_MODEL)

    small = {n: w[n] for n in _SMALL}
    loss, grad_x, g_wt_in, g_w_glu, g_w_out, g_small = _local_step(
        x[0], positions[0], loss_target[0], small, wt_in, wf_glu, wf_out)
    loss = lax.psum(loss, ("x", "y", "c"))

    full = [g_wt_in.reshape(N_DEV, IN_W // N_DEV, D_MODEL), g_w_glu.reshape(N_DEV, SSM_W // N_DEV, SSM_W),
            g_w_out.reshape(N_DEV, D_MODEL // N_DEV, D_MODEL), _pack(g_small).reshape(N_DEV, _PACK_ROWS // N_DEV, 128)]
    from_sibling = _pair_exchange(full, "pair_exchange")
    wire = (BF16, BF16, BF16, F32)
    parts = [_pair_sum(g, ra, core, dt, f"pair_sum_{k}") for k, (g, ra, dt) in enumerate(zip(full, from_sibling, wire))]
    from_chips = _chip_exchange(parts, "chip_exchange")
    red = [_chip_sum(p, rb, chip, f"chip_sum_{k}") for k, (p, rb) in enumerate(zip(parts, from_chips))]
    g_in, g_glu, g_out = red[0].T, red[1], red[2]
    (g_packed,) = _all_gather_rows([red[3]], "gather_small")
    g_packed = g_packed.reshape(_PACK_ROWS, 128)

    grads = _unpack(g_packed, w)
    grads.update(w_in=g_in, w_glu=g_glu, w_out=g_out)
    delta, new_m, new_v = {}, {}, {}
    for n in ("w_in", "w_glu", "w_out"):
        delta[n], new_m[n], new_v[n] = _adamw(grads[n], w[n], m[n], v[n], f"adamw_{n}")
    d_s, m_s, v_s = _adamw_small(grads, w, m, v)
    delta.update(d_s)
    new_m.update(m_s)
    new_v.update(v_s)

    return (loss, grad_x[None], *[grads[n] for n in _WEIGHTS], *[delta[n] for n in _WEIGHTS],
            *[new_m[n] for n in _WEIGHTS], *[new_v[n] for n in _WEIGHTS])
```

```python
import functools
import math

import jax
import jax.numpy as jnp
from jax import lax
from jax.experimental import pallas as pl
from jax.experimental.pallas import tpu as pltpu

F32 = jnp.float32
BF16 = jnp.bfloat16

D_MODEL = 2048
ATTN_W = 1024
KV_W = 256
SSM_W = 1024
HEAD_DIM = 64
N_HEADS = 16
N_KV = 4
KV_REP = 4
IN_W = 4608
BLOCK = 128
ROPE_THETA = 10000.0
NORM_EPS = 1e-6
SSM_G = 64
SSM_P = 64
SSM_H = 16
CHUNK = 16
CW = CHUNK * SSM_H
N_DEV = 8

ADAM_LR = 0.001
ADAM_B1 = 0.9
ADAM_B2 = 0.999
ADAM_EPS = 1e-08
ADAM_WD = 0.01
ADAM_STEP = 10

VMEM_LIMIT = 56 * 1024 * 1024
MESH = pl.DeviceIdType.MESH


def _cp(sem=None):
    if sem is None:
        return pltpu.CompilerParams(vmem_limit_bytes=VMEM_LIMIT)
    return pltpu.CompilerParams(vmem_limit_bytes=VMEM_LIMIT, dimension_semantics=sem)


def _sigmoid(x):
    return 1.0 / (1.0 + jnp.exp(-x))


def _silu(x):
    return x * _sigmoid(x)


def _dsilu(x):
    s = _sigmoid(x)
    return s * (1.0 + x * (1.0 - s))


_GELU_C = math.sqrt(2.0 / math.pi)


def _gelu(y):
    t = jnp.tanh(_GELU_C * (y + 0.044715 * y * y * y))
    return 0.5 * y * (1.0 + t)


def _dgelu(y):
    t = jnp.tanh(_GELU_C * (y + 0.044715 * y * y * y))
    return 0.5 * (1.0 + t) + 0.5 * y * (1.0 - t * t) * _GELU_C * (1.0 + 3.0 * 0.044715 * y * y)


def _tile(n, want):
    if n <= want:
        return n
    for t in range(want - want % 16, 0, -16):
        if n % t == 0:
            return t
    raise ValueError((n, want))


def _mm(a, b, mode, out_dtype, name, tm=512, tn=1024, add=None):
    if mode == "nn":
        (M, K), (K2, N) = a.shape, b.shape
    elif mode == "nt":
        (M, K), (N, K2) = a.shape, b.shape
    else:
        (K, M), (K2, N) = a.shape, b.shape
    assert K == K2
    tm, tn = _tile(M, tm), _tile(N, tn)
    dn = {"nn": _NN, "nt": _NT, "tn": _TN}[mode]

    def body(a_ref, b_ref, *rest):
        o_ref = rest[-1]
        acc = lax.dot_general(a_ref[...].astype(BF16), b_ref[...].astype(BF16), dn, preferred_element_type=F32)
        if add is not None:
            acc = acc + rest[0][...]
        o_ref[...] = acc.astype(o_ref.dtype)

    a_spec = pl.BlockSpec((K, tm), lambda j, i: (0, i)) if mode == "tn" else pl.BlockSpec((tm, K), lambda j, i: (i, 0))
    b_spec = pl.BlockSpec((tn, K), lambda j, i: (j, 0)) if mode == "nt" else pl.BlockSpec((K, tn), lambda j, i: (0, j))
    o_spec = pl.BlockSpec((tm, tn), lambda j, i: (i, j))
    extra = () if add is None else (add,)
    return pl.pallas_call(
        body,
        name=name,
        grid=(N // tn, M // tm),
        in_specs=[a_spec, b_spec] + [o_spec] * len(extra),
        out_specs=o_spec,
        out_shape=jax.ShapeDtypeStruct((M, N), out_dtype),
        compiler_params=_cp(("parallel", "parallel")),
    )(a, b, *extra)


def _rms_inproj(x, norm_w, wt_in):
    L = x.shape[0]
    tm, tn = _tile(L, 1024), 768
    nj = IN_W // tn

    def body(x_ref, w_ref, wt_ref, proj_ref, hn_ref, hn_scr):
        j = pl.program_id(1)

        @pl.when(j == 0)
        def _():
            xv = x_ref[...]
            r = lax.rsqrt(jnp.mean(xv * xv, axis=-1, keepdims=True) + NORM_EPS)
            hn = (xv * r * w_ref[...]).astype(BF16)
            hn_scr[...] = hn
            hn_ref[...] = hn

        proj_ref[...] = lax.dot_general(hn_scr[...], wt_ref[...], (((1,), (1,)), ((), ())),
                                        preferred_element_type=F32)

    return pl.pallas_call(
        body,
        name="rms_inproj",
        grid=(L // tm, nj),
        in_specs=[pl.BlockSpec((tm, D_MODEL), lambda i, j: (i, 0)),
                  pl.BlockSpec((1, D_MODEL), lambda i, j: (0, 0)),
                  pl.BlockSpec((tn, D_MODEL), lambda i, j: (j, 0))],
        out_specs=[pl.BlockSpec((tm, tn), lambda i, j: (i, j)),
                   pl.BlockSpec((tm, D_MODEL), lambda i, j: (i, 0))],
        out_shape=[jax.ShapeDtypeStruct((L, IN_W), F32), jax.ShapeDtypeStruct((L, D_MODEL), BF16)],
        scratch_shapes=[pltpu.VMEM((tm, D_MODEL), BF16)],
        compiler_params=_cp(("parallel", "arbitrary")),
    )(x, norm_w.reshape(1, D_MODEL), wt_in)


def _seg_sum(v):
    a = lax.broadcasted_iota(jnp.int32, (128, 128), 0) // HEAD_DIM
    b = lax.broadcasted_iota(jnp.int32, (128, 128), 1) // HEAD_DIM
    ones = jnp.where(a == b, 1.0, 0.0).astype(BF16)
    hi = v.astype(BF16)
    lo = (v - hi.astype(F32)).astype(BF16)
    return jnp.dot(hi, ones, preferred_element_type=F32) + jnp.dot(lo, ones, preferred_element_type=F32)


def _rot_half(t):
    lane = lax.broadcasted_iota(jnp.int32, t.shape, 1)
    return jnp.where(lane % HEAD_DIM < HEAD_DIM // 2, pltpu.roll(t, 128 - HEAD_DIM // 2, 1),
                     pltpu.roll(t, HEAD_DIM // 2, 1))


def _norm_rope(raw, w, cos, sin):
    r = lax.rsqrt(_seg_sum(raw * raw) * (1.0 / HEAD_DIM) + NORM_EPS)
    tn = raw * r * w
    return r, tn * cos + _rot_half(tn) * sin


def _norm_rope_bwd(d_rot, raw, w, cos, sin):
    r = lax.rsqrt(_seg_sum(raw * raw) * (1.0 / HEAD_DIM) + NORM_EPS)
    d_tn = d_rot * cos + _rot_half(d_rot * sin)
    xh = raw * r
    gw = d_tn * w
    d_raw = r * (gw - xh * (_seg_sum(gw * xh) * (1.0 / HEAD_DIM)))
    return d_raw, d_tn * xh


def _band_mask2(has_prev):
    qi = lax.broadcasted_iota(jnp.int32, (2 * BLOCK, 2 * BLOCK), 0) % BLOCK + BLOCK
    kj = lax.broadcasted_iota(jnp.int32, (2 * BLOCK, 2 * BLOCK), 1)
    rel = qi - kj
    return (rel >= 0) & (rel < BLOCK) & ((kj >= BLOCK) | has_prev)


def _half_tiles(pair):
    lo = lax.broadcasted_iota(jnp.int32, pair.shape, 1) < HEAD_DIM
    sw = pltpu.roll(pair, HEAD_DIM, 1)
    z = jnp.zeros_like(pair)
    return (jnp.where(lo, pair, z).astype(BF16), jnp.where(lo, z, sw).astype(BF16),
            jnp.where(lo, sw, z).astype(BF16), jnp.where(lo, z, pair).astype(BF16))


def _two_rows(top, bottom):
    row = lax.broadcasted_iota(jnp.int32, (2 * BLOCK, 1), 0)
    return jnp.where(row < BLOCK, top, bottom)


def _lane_col(mat, h):
    lane = lax.broadcasted_iota(jnp.int32, mat.shape, 1)
    return jnp.sum(jnp.where(lane == h, mat, 0.0), axis=1, keepdims=True)


_SCALE = 1.0 / math.sqrt(HEAD_DIM)
_NT = (((1,), (1,)), ((), ()))
_NN = (((1,), (0,)), ((), ()))
_TN = (((0,), (0,)), ((), ()))


def _qk_prep(proj, tab, qw, kw):
    L = proj.shape[0]
    tm = _tile(L, 512)

    def body(q_ref, k_ref, t_ref, qw_ref, kw_ref, qo_ref, ko_ref):
        cos, sin = t_ref[:, :128], t_ref[:, 128:]
        for c in range(ATTN_W // 128):
            _, qr = _norm_rope(q_ref[:, c * 128:(c + 1) * 128], qw_ref[...], cos, sin)
            qo_ref[:, c * 128:(c + 1) * 128] = (qr * _SCALE).astype(BF16)
        for c in range(KV_W // 128):
            _, kr = _norm_rope(k_ref[:, c * 128:(c + 1) * 128], kw_ref[...], cos, sin)
            ko_ref[:, c * 128:(c + 1) * 128] = kr.astype(BF16)

    row = pl.BlockSpec((1, 128), lambda i: (0, 0))
    return pl.pallas_call(
        body,
        name="qk_prep",
        grid=(L // tm,),
        in_specs=[pl.BlockSpec((tm, ATTN_W), lambda i: (i, 0)), pl.BlockSpec((tm, KV_W), lambda i: (i, 4)),
                  pl.BlockSpec((tm, 256), lambda i: (i, 0)), row, row],
        out_specs=[pl.BlockSpec((tm, ATTN_W), lambda i: (i, 0)), pl.BlockSpec((tm, KV_W), lambda i: (i, 0))],
        out_shape=[jax.ShapeDtypeStruct((L, ATTN_W), BF16), jax.ShapeDtypeStruct((L, KV_W), BF16)],
        compiler_params=_cp(("parallel",)),
    )(proj, proj, tab, jnp.tile(qw, 2).reshape(1, 128), jnp.tile(kw, 2).reshape(1, 128))


def _group_tiles(g, kt, vt):
    a, b = divmod(g, 2)
    return kt[a][2 * b], kt[a][2 * b + 1], vt[a][2 * b], vt[a][2 * b + 1]


def _attn_fwd(q, k, proj, sinks):
    L = proj.shape[0]
    nb = L // BLOCK

    def body(q_ref, kc_ref, kp_ref, vc_ref, vp_ref, z0_ref, z1_ref, sink_ref, og_ref, o_ref, lse_ref):
        i = pl.program_id(0)
        mask = _band_mask2(i > 0)
        z = jnp.concatenate([z0_ref[...], z1_ref[...]], axis=1)
        lane = lax.broadcasted_iota(jnp.int32, (BLOCK, 128), 1)
        kt = [_half_tiles(jnp.concatenate([kp_ref[:, a * 128:(a + 1) * 128], kc_ref[:, a * 128:(a + 1) * 128]],
                                          axis=0).astype(F32)) for a in range(2)]
        vt = [_half_tiles(jnp.concatenate([vp_ref[:, a * 128:(a + 1) * 128], vc_ref[:, a * 128:(a + 1) * 128]],
                                          axis=0)) for a in range(2)]
        lse_mat = jnp.zeros((BLOCK, 128), F32)
        outs = []
        for g in range(N_KV):
            k_lo, k_hi, v_lo, v_hi = _group_tiles(g, kt, vt)
            q2 = jnp.concatenate([q_ref[:, 2 * g * 128:(2 * g + 1) * 128],
                                  q_ref[:, (2 * g + 1) * 128:(2 * g + 2) * 128]], axis=0)
            acc = jnp.zeros((2 * BLOCK, 128), F32)
            for half, (kh, vh) in enumerate(((k_lo, v_lo), (k_hi, v_hi))):
                h_top, h_bot = 4 * g + half, 4 * g + 2 + half
                s = jnp.where(mask, lax.dot_general(q2, kh, _NT, preferred_element_type=F32), -1e30)
                sink = _two_rows(sink_ref[h_top], sink_ref[h_bot])
                m = jnp.maximum(jnp.max(s, axis=-1, keepdims=True), sink)
                e = jnp.exp(s - m)
                den = jnp.sum(e, axis=-1, keepdims=True) + jnp.exp(sink - m)
                p = e / den
                acc = acc + jnp.dot(p.astype(BF16), vh, preferred_element_type=F32)
                lse = m + jnp.log(den)
                lse_mat = jnp.where(lane == h_top, lse[:BLOCK], lse_mat)
                lse_mat = jnp.where(lane == h_bot, lse[BLOCK:], lse_mat)
            outs += [acc[:BLOCK], acc[BLOCK:]]
        o = jnp.concatenate(outs, axis=1)
        o_ref[...] = o
        og_ref[...] = o * _silu(z)
        lse_ref[...] = lse_mat

    prev = lambda i: jnp.maximum(i - 1, 0)
    return pl.pallas_call(
        body,
        name="attn_fwd",
        grid=(nb,),
        in_specs=[pl.BlockSpec((BLOCK, ATTN_W), lambda i: (i, 0)),
                  pl.BlockSpec((BLOCK, KV_W), lambda i: (i, 0)),
                  pl.BlockSpec((BLOCK, KV_W), lambda i: (prev(i), 0)),
                  pl.BlockSpec((BLOCK, KV_W), lambda i: (i, 5)),
                  pl.BlockSpec((BLOCK, KV_W), lambda i: (prev(i), 5)),
                  pl.BlockSpec((BLOCK, 512), lambda i: (i, 3)),
                  pl.BlockSpec((BLOCK, 512), lambda i: (i, 4)),
                  pl.BlockSpec(memory_space=pltpu.SMEM)],
        out_specs=[pl.BlockSpec((BLOCK, ATTN_W), lambda i: (i, 0)),
                   pl.BlockSpec((BLOCK, ATTN_W), lambda i: (i, 0)),
                   pl.BlockSpec((BLOCK, 128), lambda i: (i, 0))],
        out_shape=[jax.ShapeDtypeStruct((L, ATTN_W), F32), jax.ShapeDtypeStruct((L, ATTN_W), F32),
                   jax.ShapeDtypeStruct((L, 128), F32)],
        compiler_params=_cp(("parallel",)),
    )(q, k, k, proj, proj, proj, proj, sinks)


def _attn_bwd(q, k, proj, sinks, d_o, o, lse):
    L = proj.shape[0]
    nb = L // BLOCK

    def body(q_ref, kc_ref, kp_ref, vc_ref, vp_ref, do_ref, o_ref, lse_ref, sink_ref,
             dq_ref, dk_ref, dv_ref, gs_ref, ck_scr, cv_scr):
        i = pl.program_id(0)

        @pl.when(i == 0)
        def _():
            gs_ref[...] = jnp.zeros_like(gs_ref)
            ck_scr[...] = jnp.zeros_like(ck_scr)
            cv_scr[...] = jnp.zeros_like(cv_scr)

        @pl.when(i == nb)
        def _():
            dk_ref[...] = ck_scr[...]
            dv_ref[...] = cv_scr[...]

        @pl.when(i < nb)
        def _():
            mask = _band_mask2(i > 0)
            lane = lax.broadcasted_iota(jnp.int32, (1, 128), 1)
            lo = lax.broadcasted_iota(jnp.int32, (2 * BLOCK, 128), 1) < HEAD_DIM
            lse_c = lse_ref[...]
            kt = [_half_tiles(jnp.concatenate([kp_ref[:, a * 128:(a + 1) * 128], kc_ref[:, a * 128:(a + 1) * 128]],
                                              axis=0).astype(F32)) for a in range(2)]
            vt = [_half_tiles(jnp.concatenate([vp_ref[:, a * 128:(a + 1) * 128], vc_ref[:, a * 128:(a + 1) * 128]],
                                              axis=0)) for a in range(2)]
            gs = jnp.zeros((1, 128), F32)
            dq_parts = []
            dk_acc = [jnp.zeros((2 * BLOCK, 128), F32) for _ in range(2)]
            dv_acc = [jnp.zeros((2 * BLOCK, 128), F32) for _ in range(2)]
            for g in range(N_KV):
                a, b = divmod(g, 2)
                k_lo, k_hi, v_lo, v_hi = _group_tiles(g, kt, vt)
                t0, t1 = slice(2 * g * 128, (2 * g + 1) * 128), slice((2 * g + 1) * 128, (2 * g + 2) * 128)
                q2 = jnp.concatenate([q_ref[:, t0], q_ref[:, t1]], axis=0)
                do2 = jnp.concatenate([do_ref[:, t0], do_ref[:, t1]], axis=0)
                prod = do2 * jnp.concatenate([o_ref[:, t0], o_ref[:, t1]], axis=0)
                do2_b = do2.astype(BF16)
                dq2 = jnp.zeros((2 * BLOCK, 128), F32)
                dk_h, dv_h = [], []
                for half, (kh, vh) in enumerate(((k_lo, v_lo), (k_hi, v_hi))):
                    h_top, h_bot = 4 * g + half, 4 * g + 2 + half
                    lse = jnp.concatenate([_lane_col(lse_c, h_top), _lane_col(lse_c, h_bot)], axis=0)
                    sink = _two_rows(sink_ref[h_top], sink_ref[h_bot])
                    delta = jnp.sum(jnp.where(lo == (half == 0), prod, 0.0), axis=1, keepdims=True)
                    s = jnp.where(mask, lax.dot_general(q2, kh, _NT, preferred_element_type=F32), -1e30)
                    p = jnp.exp(s - lse)
                    dp = lax.dot_general(do2_b, vh, _NT, preferred_element_type=F32)
                    ds_b = (p * (dp - delta)).astype(BF16)
                    p_b = p.astype(BF16)
                    dq2 = dq2 + jnp.dot(ds_b, kh, preferred_element_type=F32)
                    dk_h.append(lax.dot_general(ds_b, q2, _TN, preferred_element_type=F32))
                    dv_h.append(lax.dot_general(p_b, do2_b, _TN, preferred_element_type=F32))
                    gsink = -jnp.exp(sink - lse) * delta
                    row = lax.broadcasted_iota(jnp.int32, (2 * BLOCK, 1), 0)
                    gs = gs + jnp.where(lane == h_top, jnp.sum(jnp.where(row < BLOCK, gsink, 0.0)), 0.0)
                    gs = gs + jnp.where(lane == h_bot, jnp.sum(jnp.where(row >= BLOCK, gsink, 0.0)), 0.0)
                dq_parts += [dq2[:BLOCK], dq2[BLOCK:]]
                for acc, parts in ((dk_acc, dk_h), (dv_acc, dv_h)):
                    t = jnp.where(lo, parts[0], parts[1])
                    t = t + pltpu.roll(t, HEAD_DIM, 1)
                    acc[a] = acc[a] + jnp.where(lo == (b == 0), t, 0.0)
            dq_ref[...] = jnp.concatenate(dq_parts, axis=1)
            dk_full = jnp.concatenate(dk_acc, axis=1)
            dv_full = jnp.concatenate(dv_acc, axis=1)
            dk_ref[...] = ck_scr[...] + dk_full[:BLOCK]
            dv_ref[...] = cv_scr[...] + dv_full[:BLOCK]
            ck_scr[...] = dk_full[BLOCK:]
            cv_scr[...] = dv_full[BLOCK:]
            gs_ref[...] += gs

    cur = lambda i: jnp.minimum(i, nb - 1)
    prev = lambda i: jnp.maximum(jnp.minimum(i, nb - 1) - 1, 0)
    done = lambda i: jnp.maximum(i - 1, 0)
    bs = pl.BlockSpec
    return pl.pallas_call(
        body,
        name="attn_bwd",
        grid=(nb + 1,),
        in_specs=[bs((BLOCK, ATTN_W), lambda i: (cur(i), 0)),
                  bs((BLOCK, KV_W), lambda i: (cur(i), 0)), bs((BLOCK, KV_W), lambda i: (prev(i), 0)),
                  bs((BLOCK, KV_W), lambda i: (cur(i), 5)), bs((BLOCK, KV_W), lambda i: (prev(i), 5)),
                  bs((BLOCK, ATTN_W), lambda i: (cur(i), 0)), bs((BLOCK, ATTN_W), lambda i: (cur(i), 0)),
                  bs((BLOCK, 128), lambda i: (cur(i), 0)), bs(memory_space=pltpu.SMEM)],
        out_specs=[bs((BLOCK, ATTN_W), lambda i: (cur(i), 0)),
                   bs((BLOCK, KV_W), lambda i: (done(i), 0)), bs((BLOCK, KV_W), lambda i: (done(i), 0)),
                   bs((1, 128), lambda i: (0, 0))],
        out_shape=[jax.ShapeDtypeStruct((L, ATTN_W), F32), jax.ShapeDtypeStruct((L, KV_W), F32),
                   jax.ShapeDtypeStruct((L, KV_W), F32), jax.ShapeDtypeStruct((1, 128), F32)],
        scratch_shapes=[pltpu.VMEM((BLOCK, KV_W), F32), pltpu.VMEM((BLOCK, KV_W), F32)],
        compiler_params=_cp(("arbitrary",)),
    )(q, k, k, proj, proj, d_o, o, lse, sinks)


def _qk_prep_bwd(proj, tab, qw, kw, d_q, d_k, d_v, d_za, d_u, d_zs):
    L = proj.shape[0]
    tm = _tile(L, 512)
    z0 = ATTN_W + 2 * KV_W

    def body(q_ref, k_ref, t_ref, qw_ref, kw_ref, dq_ref, dk_ref, dv_ref, dza_ref, du_ref, dzs_ref,
             out_ref, gq_ref, gk_ref):
        i = pl.program_id(0)

        @pl.when(i == 0)
        def _():
            gq_ref[...] = jnp.zeros_like(gq_ref)
            gk_ref[...] = jnp.zeros_like(gk_ref)

        cos, sin = t_ref[:, :128], t_ref[:, 128:]
        gq = jnp.zeros((1, 128), F32)
        gk = jnp.zeros((1, 128), F32)
        for c in range(ATTN_W // 128):
            cs = slice(c * 128, (c + 1) * 128)
            d_raw, gw = _norm_rope_bwd(dq_ref[:, cs] * _SCALE, q_ref[:, cs], qw_ref[...], cos, sin)
            out_ref[:, cs] = d_raw.astype(BF16)
            gq = gq + jnp.sum(gw, axis=0, keepdims=True)
        for c in range(KV_W // 128):
            cs = slice(c * 128, (c + 1) * 128)
            d_raw, gw = _norm_rope_bwd(dk_ref[:, cs], k_ref[:, cs], kw_ref[...], cos, sin)
            out_ref[:, ATTN_W + c * 128:ATTN_W + (c + 1) * 128] = d_raw.astype(BF16)
            gk = gk + jnp.sum(gw, axis=0, keepdims=True)
        out_ref[:, ATTN_W + KV_W:z0] = dv_ref[...].astype(BF16)
        out_ref[:, z0:z0 + ATTN_W] = dza_ref[...]
        out_ref[:, z0 + ATTN_W:z0 + ATTN_W + SSM_W] = du_ref[...].astype(BF16)
        out_ref[:, z0 + ATTN_W + SSM_W:] = dzs_ref[...]
        gq_ref[...] += gq
        gk_ref[...] += gk

    row = pl.BlockSpec((1, 128), lambda i: (0, 0))
    blk = lambda w, c: pl.BlockSpec((tm, w), lambda i: (i, c))
    return pl.pallas_call(
        body,
        name="qk_prep_bwd",
        grid=(L // tm,),
        in_specs=[blk(ATTN_W, 0), blk(KV_W, 4), blk(256, 0), row, row, blk(ATTN_W, 0), blk(KV_W, 0), blk(KV_W, 0),
                  blk(ATTN_W, 0), blk(SSM_W, 0), blk(SSM_W, 0)],
        out_specs=[blk(IN_W, 0), row, row],
        out_shape=[jax.ShapeDtypeStruct((L, IN_W), BF16), jax.ShapeDtypeStruct((1, 128), F32),
                   jax.ShapeDtypeStruct((1, 128), F32)],
        compiler_params=_cp(("arbitrary",)),
    )(proj, proj, tab, jnp.tile(qw, 2).reshape(1, 128), jnp.tile(kw, 2).reshape(1, 128), d_q, d_k, d_v,
      d_za, d_u, d_zs)


def _cmul(a, b):
    return a[0] * b[0] - a[1] * b[1], a[0] * b[1] + a[1] * b[0]


def _cmul_conj(a, b):
    return a[0] * b[0] + a[1] * b[1], a[1] * b[0] - a[0] * b[1]


def _cadd(a, b):
    return a[0] + b[0], a[1] + b[1]


def _dot3(a, b, dn):
    ah, bh = a.astype(BF16), b.astype(BF16)
    al, bl = (a - ah.astype(F32)).astype(BF16), (b - bh.astype(F32)).astype(BF16)
    d = lambda u, v: lax.dot_general(u, v, dn, preferred_element_type=F32)
    return d(ah, bh) + d(ah, bl) + d(al, bh)


def _s5_discretise(a_re, a_im, ls, cosx, sinx, bt):
    delta = jnp.exp(ls)
    er = jnp.exp(a_re * delta)
    lb = (er * cosx, er * sinx)
    den = a_re * a_re + a_im * a_im
    coef = _cmul_conj((lb[0] - 1.0, lb[1]), (a_re, a_im))
    coef = (coef[0] / den, coef[1] / den)
    return delta, lb, coef, den, _cmul(coef, bt)


def _powers(lb):
    pw = [(jnp.ones_like(lb[0]), jnp.zeros_like(lb[0]))]
    for _ in range(CHUNK):
        pw.append(_cmul(pw[-1], lb))
    return pw


def _block_rows(a, pw, idx):
    blocks = [_cmul(a, pw[i]) for i in idx]
    return (jnp.concatenate([b[0] for b in blocks], axis=0), jnp.concatenate([b[1] for b in blocks], axis=0))


def _block_rows_bwd(g, a, pw, idx, g_pw):
    g_a = (jnp.zeros_like(a[0]), jnp.zeros_like(a[0]))
    for j, i in enumerate(idx):
        gj = (g[0][j * SSM_H:(j + 1) * SSM_H], g[1][j * SSM_H:(j + 1) * SSM_H])
        g_a = _cadd(g_a, _cmul_conj(gj, pw[i]))
        gp = _cmul_conj(gj, a)
        g_pw[i] = _cadd(g_pw[i], (jnp.sum(gp[0], axis=0, keepdims=True), jnp.sum(gp[1], axis=0, keepdims=True)))
    return g_a


_IDX_S = [CHUNK - 1 - s for s in range(CHUNK)]
_IDX_O = [t + 1 for t in range(CHUNK)]
_IDX_K = list(range(CHUNK))
_PREP_IN = 9


def _prep_args(p):
    row = lambda t: t.reshape(SSM_G, 1, SSM_P)
    xi = p["a_im"] * jnp.exp(p["log_step"])[:, None]
    return (row(p["a_re"]), row(p["a_im"]), row(jnp.broadcast_to(p["log_step"][:, None], (SSM_G, SSM_P))),
            row(jnp.cos(xi)), row(jnp.sin(xi)), p["b_re"].transpose(0, 2, 1), p["b_im"].transpose(0, 2, 1),
            p["c_re"], p["c_im"])


def _prep_specs():
    r1 = pl.BlockSpec((1, 1, SSM_P), lambda g: (g, 0, 0))
    r16 = pl.BlockSpec((1, SSM_H, SSM_P), lambda g: (g, 0, 0))
    return [r1] * 5 + [r16] * 4, r1, r16


def _ssm_prep(p):
    def body(are, aim, ls, cosx, sinx, btr, bti, cre, cim, mt_ref, s_ref, o_ref, a_ref):
        _, lb, _, _, bb = _s5_discretise(are[0], aim[0], ls[0], cosx[0], sinx[0], (btr[0], bti[0]))
        pw = _powers(lb)
        c = (cre[0], cim[0])
        sc = _block_rows(bb, pw, _IDX_S)
        ot = _block_rows(c, pw, _IDX_O)
        ok = _block_rows(c, pw, _IDX_K)
        s_ref[0] = jnp.concatenate([sc[0], sc[1]], axis=1).astype(BF16)
        o_ref[0] = jnp.concatenate([ot[0], -ot[1]], axis=1).astype(BF16)
        a_ref[0] = jnp.concatenate([pw[CHUNK][0], pw[CHUNK][1]], axis=1)
        kt = _dot3(jnp.concatenate([bb[0], -bb[1]], axis=1), jnp.concatenate([ok[0], ok[1]], axis=1), _NT)
        lane = lax.broadcasted_iota(jnp.int32, kt.shape, 1)
        for s in range(CHUNK):
            blk = kt if s == 0 else jnp.where(lane >= SSM_H * s, pltpu.roll(kt, SSM_H * s, 1), 0.0)
            mt_ref[0, s * SSM_H:(s + 1) * SSM_H, :] = blk.astype(BF16)

    in_specs, r1, _ = _prep_specs()
    g3 = lambda r, c: pl.BlockSpec((1, r, c), lambda g: (g, 0, 0))
    return pl.pallas_call(
        body,
        name="ssm_prep",
        grid=(SSM_G,),
        in_specs=in_specs,
        out_specs=[g3(CW, CW), g3(CW, 2 * SSM_P), g3(CW, 2 * SSM_P), g3(1, 2 * SSM_P)],
        out_shape=[jax.ShapeDtypeStruct((SSM_G, CW, CW), BF16), jax.ShapeDtypeStruct((SSM_G, CW, 2 * SSM_P), BF16),
                   jax.ShapeDtypeStruct((SSM_G, CW, 2 * SSM_P), BF16),
                   jax.ShapeDtypeStruct((SSM_G, 1, 2 * SSM_P), F32)],
        compiler_params=_cp(("parallel",)),
    )(*_prep_args(p))


def _ssm_prep_bwd(p, g_mt, g_scat, g_ocat, g_a16):
    def body(are, aim, ls, cosx, sinx, btr, bti, cre, cim, gmt_ref, gs_ref, go_ref, ga_ref,
             g_are, g_aim, g_ls, g_btr, g_bti, g_cre, g_cim):
        lam = (are[0], aim[0])
        bt = (btr[0], bti[0])
        delta, lb, coef, den, bb = _s5_discretise(lam[0], lam[1], ls[0], cosx[0], sinx[0], bt)
        pw = _powers(lb)
        c = (cre[0], cim[0])
        ok = _block_rows(c, pw, _IDX_K)
        g_pw = [(jnp.zeros_like(lb[0]), jnp.zeros_like(lb[0])) for _ in range(CHUNK + 1)]
        lane = lax.broadcasted_iota(jnp.int32, (SSM_H, CW), 1)
        g_kt = gmt_ref[0, :SSM_H, :]
        for s in range(1, CHUNK):
            blk = gmt_ref[0, s * SSM_H:(s + 1) * SSM_H, :]
            g_kt = g_kt + jnp.where(lane < CW - SSM_H * s, pltpu.roll(blk, CW - SSM_H * s, 1), 0.0)
        a1 = jnp.concatenate([bb[0], -bb[1]], axis=1)
        b1 = jnp.concatenate([ok[0], ok[1]], axis=1)
        g_a1 = _dot3(g_kt, b1, _NN)
        g_b1 = _dot3(g_kt, a1, _TN)
        g_bb = (g_a1[:, :SSM_P], -g_a1[:, SSM_P:])
        g_c = _block_rows_bwd((g_b1[:, :SSM_P], g_b1[:, SSM_P:]), c, pw, _IDX_K, g_pw)
        gs = gs_ref[0]
        g_bb = _cadd(g_bb, _block_rows_bwd((gs[:, :SSM_P], gs[:, SSM_P:]), bb, pw, _IDX_S, g_pw))
        go = go_ref[0]
        g_c = _cadd(g_c, _block_rows_bwd((go[:, :SSM_P], -go[:, SSM_P:]), c, pw, _IDX_O, g_pw))
        ga = ga_ref[0]
        g_pw[CHUNK] = _cadd(g_pw[CHUNK], (ga[:, :SSM_P], ga[:, SSM_P:]))
        g_lb = (jnp.zeros_like(lb[0]), jnp.zeros_like(lb[0]))
        for l in range(CHUNK - 1, -1, -1):
            g_lb = _cadd(g_lb, _cmul_conj(g_pw[l + 1], pw[l]))
            g_pw[l] = _cadd(g_pw[l], _cmul_conj(g_pw[l + 1], lb))
        g_bt = _cmul_conj(g_bb, coef)
        gc = _cmul_conj(g_bb, bt)
        g_coef = (jnp.sum(gc[0], axis=0, keepdims=True), jnp.sum(gc[1], axis=0, keepdims=True))
        lam_den = (lam[0] / den, lam[1] / den)
        g_lb = _cadd(g_lb, _cmul(g_coef, lam_den))
        t = _cmul(_cmul_conj(g_coef, coef), lam_den)
        g_x = _cmul_conj(g_lb, lb)
        g_lam = (g_x[0] * delta - t[0], g_x[1] * delta - t[1])
        g_are[0] = g_lam[0]
        g_aim[0] = g_lam[1]
        g_ls[0] = (g_x[0] * lam[0] + g_x[1] * lam[1]) * delta
        g_btr[0] = g_bt[0]
        g_bti[0] = g_bt[1]
        g_cre[0] = g_c[0]
        g_cim[0] = g_c[1]

    in_specs, r1, r16 = _prep_specs()
    g3 = lambda r, c: pl.BlockSpec((1, r, c), lambda g: (g, 0, 0))
    rows = jax.ShapeDtypeStruct((SSM_G, 1, SSM_P), F32)
    mats = jax.ShapeDtypeStruct((SSM_G, SSM_H, SSM_P), F32)
    g_are, g_aim, g_ls, g_btr, g_bti, g_cre, g_cim = pl.pallas_call(
        body,
        name="ssm_prep_bwd",
        grid=(SSM_G,),
        in_specs=in_specs + [g3(CW, CW), g3(CW, 2 * SSM_P), g3(CW, 2 * SSM_P), g3(1, 2 * SSM_P)],
        out_specs=[r1] * 3 + [r16] * 4,
        out_shape=[rows] * 3 + [mats] * 4,
        compiler_params=_cp(("parallel",)),
    )(*_prep_args(p), g_mt, g_scat, g_ocat, g_a16)
    return dict(a_re=g_are.reshape(SSM_G, SSM_P), a_im=g_aim.reshape(SSM_G, SSM_P),
                log_step=jnp.sum(g_ls.reshape(SSM_G, SSM_P), axis=1),
                b_re=g_btr.transpose(0, 2, 1), b_im=g_bti.transpose(0, 2, 1), c_re=g_cre, c_im=g_cim)


def _cmul_const(xv, ar, ai):
    return xv * ar + pltpu.roll(xv, SSM_P, 1) * ai


def _chunk_scan(inc, a_row, reverse):
    n = inc.shape[0]
    lane = lax.broadcasted_iota(jnp.int32, (1, 2 * SSM_P), 1)
    row = lax.broadcasted_iota(jnp.int32, inc.shape, 0)
    sign = jnp.where(lane < SSM_P, -1.0, 1.0)
    ar = jnp.where(lane < SSM_P, a_row, pltpu.roll(a_row, SSM_P, 1))
    ai = jnp.where(lane < SSM_P, pltpu.roll(a_row, SSM_P, 1), a_row)
    if reverse:
        ai = -ai
    xv = inc
    s = 1
    while s < n:
        if reverse:
            sh = jnp.where(row < n - s, pltpu.roll(xv, n - s, 0), 0.0)
        else:
            sh = jnp.where(row >= s, pltpu.roll(xv, s, 0), 0.0)
        xv = xv + _cmul_const(sh, ar, ai * sign)
        ar, ai = ar * ar - ai * ai, 2.0 * ar * ai
        s *= 2
    return xv


def _shift_rows(xv, reverse):
    n = xv.shape[0]
    row = lax.broadcasted_iota(jnp.int32, xv.shape, 0)
    if reverse:
        return jnp.where(row < n - 1, pltpu.roll(xv, n - 1, 0), 0.0)
    return jnp.where(row >= 1, pltpu.roll(xv, 1, 0), 0.0)


GB = 128 // SSM_H
U_COL0 = (ATTN_W + 2 * KV_W + ATTN_W) // 128


HALF = CHUNK // 2


def _chunk_perm():
    r = jnp.arange(HALF * 128)
    t, g8, h = r // 128, (r % 128) // SSM_H, r % SSM_H
    return ((g8 * 128 + t * SSM_H + h)[:, None] == jnp.arange(GB * 128)[None, :]).astype(BF16)


def _load_perm(p_hbm, p_scr, sem):
    @pl.when(pl.program_id(0) == 0)
    def _():
        cp = pltpu.make_async_copy(p_hbm, p_scr, sem)
        cp.start()
        cp.wait()


def _rows_to_chunks(pieces, perm):
    halves = [jnp.dot(jnp.concatenate(pieces[k * HALF:(k + 1) * HALF], axis=1).astype(BF16), perm,
                      preferred_element_type=F32).astype(BF16) for k in range(2)]
    return [jnp.concatenate([hv[:, g * 128:(g + 1) * 128] for hv in halves], axis=1) for g in range(GB)]


def _chunks_to_rows(groups, perm, two_pass):
    pieces = []
    for k in range(2):
        v = jnp.concatenate([gv[:, k * 128:(k + 1) * 128] for gv in groups], axis=1)
        hi = v.astype(BF16)
        out = lax.dot_general(hi, perm, _NT, preferred_element_type=F32)
        if two_pass:
            lo = (v - hi.astype(F32)).astype(BF16)
            out = out + lax.dot_general(lo, perm, _NT, preferred_element_type=F32)
        pieces += [out[:, t * 128:(t + 1) * 128] for t in range(HALF)]
    return pieces


def _ssm_fwd(proj, perm, mt, scat, ocat, a16, d_skip):
    L = proj.shape[0]
    nc = L // CHUNK

    def body(u_ref, p_hbm, mt_ref, s_ref, o_ref, a_ref, d_ref, y_ref, yg_ref, h_ref, p_scr, sem):
        _load_perm(p_hbm, p_scr, sem)
        perm = p_scr[...]
        rows = [pl.ds(t, nc, stride=CHUNK) for t in range(CHUNK)]
        ua = _rows_to_chunks([u_ref[r, :] for r in rows], perm)
        ys = []
        for g in range(GB):
            uv = ua[g]
            inc = jnp.dot(uv, s_ref[g], preferred_element_type=F32)
            hx = _shift_rows(_chunk_scan(inc, a_ref[g], False), False)
            h_ref[g] = hx
            ys.append(jnp.dot(uv, mt_ref[g], preferred_element_type=F32)
                      + lax.dot_general(hx.astype(BF16), o_ref[g], _NT, preferred_element_type=F32))
        yp = _chunks_to_rows(ys, perm, True)
        for t, r in enumerate(rows):
            y = yp[t] + d_ref[...] * u_ref[r, :]
            y_ref[r, :] = y
            yg_ref[r, :] = _gelu(y)

    g3 = lambda r, c: pl.BlockSpec((GB, r, c), lambda g: (g, 0, 0))
    col = pl.BlockSpec((L, 128), lambda g: (0, g))
    return pl.pallas_call(
        body,
        name="ssm_fwd",
        grid=(SSM_G // GB,),
        in_specs=[pl.BlockSpec((L, 128), lambda g: (0, U_COL0 + g)), _ANY,
                  g3(CW, CW), g3(CW, 2 * SSM_P), g3(CW, 2 * SSM_P), g3(1, 2 * SSM_P),
                  pl.BlockSpec((1, 128), lambda g: (0, g))],
        out_specs=[col, col, g3(nc, 2 * SSM_P)],
        out_shape=[jax.ShapeDtypeStruct((L, SSM_W), F32), jax.ShapeDtypeStruct((L, SSM_W), F32),
                   jax.ShapeDtypeStruct((SSM_G, nc, 2 * SSM_P), F32)],
        scratch_shapes=[pltpu.VMEM((HALF * 128, GB * 128), BF16), pltpu.SemaphoreType.DMA],
        compiler_params=_cp(("arbitrary",)),
    )(proj, perm, mt, scat, ocat, a16, d_skip.reshape(1, SSM_W))


def _ssm_bwd(d_yg, y, proj, hx, perm, mt, scat, ocat, a16, d_skip):
    L = proj.shape[0]
    nc = L // CHUNK

    def body(dg_ref, y_ref, u_ref, h_ref, p_hbm, mt_ref, s_ref, o_ref, a_ref, d_ref,
             du_ref, gmt_ref, gs_ref, go_ref, ga_ref, gd_ref, p_scr, sem):
        _load_perm(p_hbm, p_scr, sem)
        perm = p_scr[...]
        rows = [pl.ds(t, nc, stride=CHUNK) for t in range(CHUNK)]
        us = [u_ref[r, :] for r in rows]
        dys = [dg_ref[r, :] * _dgelu(y_ref[r, :]) for r in rows]
        gd = jnp.zeros((1, 128), F32)
        for uv, dy in zip(us, dys):
            gd = gd + jnp.sum(dy * uv, axis=0, keepdims=True)
        gd_ref[...] = gd
        ua = _rows_to_chunks(us, perm)
        dya = _rows_to_chunks(dys, perm)
        lane = lax.broadcasted_iota(jnp.int32, (1, 2 * SSM_P), 1)
        dus = []
        for g in range(GB):
            uv, dy, hx_v = ua[g], dya[g], h_ref[g]
            dh = jnp.dot(dy, o_ref[g], preferred_element_type=F32)
            dinc = _shift_rows(_chunk_scan(dh, a_ref[g], True), True)
            dinc_b = dinc.astype(BF16)
            dus.append(lax.dot_general(dy, mt_ref[g], _NT, preferred_element_type=F32)
                       + lax.dot_general(dinc_b, s_ref[g], _NT, preferred_element_type=F32))
            gmt_ref[g] = lax.dot_general(uv, dy, _TN, preferred_element_type=F32)
            gs_ref[g] = lax.dot_general(uv, dinc_b, _TN, preferred_element_type=F32)
            go_ref[g] = lax.dot_general(dy, hx_v.astype(BF16), _TN, preferred_element_type=F32)
            p1 = dinc * hx_v
            p2 = pltpu.roll(dinc, SSM_P, 1) * hx_v
            t1 = jnp.sum(p1 + pltpu.roll(p1, SSM_P, 1), axis=0, keepdims=True)
            t2 = jnp.sum(p2 - pltpu.roll(p2, SSM_P, 1), axis=0, keepdims=True)
            ga_ref[g] = jnp.where(lane < SSM_P, t1, pltpu.roll(t2, SSM_P, 1))
        dup = _chunks_to_rows(dus, perm, False)
        for t, r in enumerate(rows):
            du_ref[r, :] = dup[t] + d_ref[...] * dys[t]

    g3 = lambda r, c: pl.BlockSpec((GB, r, c), lambda g: (g, 0, 0))
    col = pl.BlockSpec((L, 128), lambda g: (0, g))
    row = pl.BlockSpec((1, 128), lambda g: (0, g))
    return pl.pallas_call(
        body,
        name="ssm_bwd",
        grid=(SSM_G // GB,),
        in_specs=[col, col, pl.BlockSpec((L, 128), lambda g: (0, U_COL0 + g)), g3(nc, 2 * SSM_P), _ANY,
                  g3(CW, CW), g3(CW, 2 * SSM_P), g3(CW, 2 * SSM_P), g3(1, 2 * SSM_P), row],
        out_specs=[col, g3(CW, CW), g3(CW, 2 * SSM_P), g3(CW, 2 * SSM_P), g3(1, 2 * SSM_P), row],
        out_shape=[jax.ShapeDtypeStruct((L, SSM_W), F32), jax.ShapeDtypeStruct((SSM_G, CW, CW), F32),
                   jax.ShapeDtypeStruct((SSM_G, CW, 2 * SSM_P), F32),
                   jax.ShapeDtypeStruct((SSM_G, CW, 2 * SSM_P), F32),
                   jax.ShapeDtypeStruct((SSM_G, 1, 2 * SSM_P), F32),
                   jax.ShapeDtypeStruct((1, SSM_W), F32)],
        scratch_shapes=[pltpu.VMEM((HALF * 128, GB * 128), BF16), pltpu.SemaphoreType.DMA],
        compiler_params=_cp(("arbitrary",)),
    )(d_yg, y, proj, hx, perm, mt, scat, ocat, a16, d_skip.reshape(1, SSM_W))


def _merge(og, yg, gpre, proj, b_glu, wa, ws):
    L = og.shape[0]
    tm = _tile(L, 256)

    def body(og_ref, yg_ref, gp_ref, z0_ref, z1_ref, b_ref, wa_ref, ws_ref, m_ref):
        zs = jnp.concatenate([z0_ref[...], z1_ref[...]], axis=1)
        os_ = yg_ref[...] * _sigmoid(gp_ref[...] + b_ref[...]) * _silu(zs)
        ogv = og_ref[...]
        ra = lax.rsqrt(jnp.mean(ogv * ogv, axis=-1, keepdims=True) + NORM_EPS)
        rs = lax.rsqrt(jnp.mean(os_ * os_, axis=-1, keepdims=True) + NORM_EPS)
        m_ref[:, :ATTN_W] = (ogv * ra * wa_ref[...]).astype(BF16)
        m_ref[:, ATTN_W:] = (os_ * rs * ws_ref[...]).astype(BF16)

    row = lambda w: pl.BlockSpec((1, w), lambda i: (0, 0))
    return pl.pallas_call(
        body,
        name="merge",
        grid=(L // tm,),
        in_specs=[pl.BlockSpec((tm, ATTN_W), lambda i: (i, 0)), pl.BlockSpec((tm, SSM_W), lambda i: (i, 0)),
                  pl.BlockSpec((tm, SSM_W), lambda i: (i, 0)),
                  pl.BlockSpec((tm, 512), lambda i: (i, 7)), pl.BlockSpec((tm, 512), lambda i: (i, 8)),
                  row(SSM_W), row(ATTN_W), row(SSM_W)],
        out_specs=pl.BlockSpec((tm, D_MODEL), lambda i: (i, 0)),
        out_shape=jax.ShapeDtypeStruct((L, D_MODEL), BF16),
        compiler_params=_cp(("parallel",)),
    )(og, yg, gpre, proj, proj, b_glu.reshape(1, SSM_W), wa.reshape(1, ATTN_W), ws.reshape(1, SSM_W))


def _outproj_loss(merged, w_out, x, target):
    L = x.shape[0]
    tm, tn = _tile(L, 512), 1024
    ni, nj = L // tm, D_MODEL // tn

    def body(m_ref, w_ref, x_ref, t_ref, d_ref, db_ref, l_ref):
        out = x_ref[...] + jnp.dot(m_ref[...], w_ref[...], preferred_element_type=F32)
        diff = out - t_ref[...]
        d = diff * (1.0 / D_MODEL)
        d_ref[...] = d
        db_ref[...] = d.astype(BF16)
        l_ref[...] = jnp.full((1, 8, 128), jnp.sum(diff * diff), F32)

    return pl.pallas_call(
        body,
        name="outproj_loss",
        grid=(nj, ni),
        in_specs=[pl.BlockSpec((tm, D_MODEL), lambda j, i: (i, 0)),
                  pl.BlockSpec((D_MODEL, tn), lambda j, i: (0, j)),
                  pl.BlockSpec((tm, tn), lambda j, i: (i, j)),
                  pl.BlockSpec((tm, tn), lambda j, i: (i, j))],
        out_specs=[pl.BlockSpec((tm, tn), lambda j, i: (i, j)), pl.BlockSpec((tm, tn), lambda j, i: (i, j)),
                   pl.BlockSpec((1, 8, 128), lambda j, i: (i * nj + j, 0, 0))],
        out_shape=[jax.ShapeDtypeStruct((L, D_MODEL), F32), jax.ShapeDtypeStruct((L, D_MODEL), BF16),
                   jax.ShapeDtypeStruct((ni * nj, 8, 128), F32)],
        compiler_params=_cp(("parallel", "parallel")),
    )(merged, w_out, x, target)


def _merge_bwd(d_m, og, o, yg, gpre, proj, b_glu, wa, ws):
    L = og.shape[0]
    tm = _tile(L, 256)

    def body(dm_ref, og_ref, o_ref, yg_ref, gp_ref, za0_ref, za1_ref, zs0_ref, zs1_ref, b_ref, wa_ref, ws_ref,
             do_ref, dza_ref, dzs_ref, dg_ref, dyg_ref, gwa_ref, gws_ref, gb_ref):
        i = pl.program_id(0)

        @pl.when(i == 0)
        def _():
            gwa_ref[...] = jnp.zeros_like(gwa_ref)
            gws_ref[...] = jnp.zeros_like(gws_ref)
            gb_ref[...] = jnp.zeros_like(gb_ref)

        za = jnp.concatenate([za0_ref[...], za1_ref[...]], axis=1)
        zs = jnp.concatenate([zs0_ref[...], zs1_ref[...]], axis=1)
        ogv, dma = og_ref[...], dm_ref[:, :ATTN_W]
        ra = lax.rsqrt(jnp.mean(ogv * ogv, axis=-1, keepdims=True) + NORM_EPS)
        xh = ogv * ra
        gwa_ref[...] += jnp.sum(dma * xh, axis=0, keepdims=True)
        gx = dma * wa_ref[...]
        d_og = ra * (gx - xh * jnp.mean(gx * xh, axis=-1, keepdims=True))
        do_ref[...] = d_og * _silu(za)
        dza_ref[...] = (d_og * o_ref[...] * _dsilu(za)).astype(BF16)
        ygv = yg_ref[...]
        sg = _sigmoid(gp_ref[...] + b_ref[...])
        y2 = ygv * sg
        sz = _silu(zs)
        os_ = y2 * sz
        dms = dm_ref[:, ATTN_W:]
        rs = lax.rsqrt(jnp.mean(os_ * os_, axis=-1, keepdims=True) + NORM_EPS)
        xs = os_ * rs
        gws_ref[...] += jnp.sum(dms * xs, axis=0, keepdims=True)
        gxs = dms * ws_ref[...]
        d_os = rs * (gxs - xs * jnp.mean(gxs * xs, axis=-1, keepdims=True))
        dzs_ref[...] = (d_os * y2 * _dsilu(zs)).astype(BF16)
        d_y2 = d_os * sz
        d_g = d_y2 * ygv * sg * (1.0 - sg)
        dg_ref[...] = d_g.astype(BF16)
        gb_ref[...] += jnp.sum(d_g, axis=0, keepdims=True)
        dyg_ref[...] = d_y2 * sg

    row = lambda w: pl.BlockSpec((1, w), lambda i: (0, 0))
    full = lambda w: pl.BlockSpec((tm, w), lambda i: (i, 0))
    half = lambda c: pl.BlockSpec((tm, 512), lambda i: (i, c))
    return pl.pallas_call(
        body,
        name="merge_bwd",
        grid=(L // tm,),
        in_specs=[full(D_MODEL), full(ATTN_W), full(ATTN_W), full(SSM_W), full(SSM_W),
                  half(3), half(4), half(7), half(8), row(SSM_W), row(ATTN_W), row(SSM_W)],
        out_specs=[full(ATTN_W), full(ATTN_W), full(SSM_W), full(SSM_W), full(SSM_W),
                   row(ATTN_W), row(SSM_W), row(SSM_W)],
        out_shape=[jax.ShapeDtypeStruct((L, ATTN_W), F32), jax.ShapeDtypeStruct((L, ATTN_W), BF16),
                   jax.ShapeDtypeStruct((L, SSM_W), BF16), jax.ShapeDtypeStruct((L, SSM_W), BF16),
                   jax.ShapeDtypeStruct((L, SSM_W), F32),
                   jax.ShapeDtypeStruct((1, ATTN_W), F32), jax.ShapeDtypeStruct((1, SSM_W), F32),
                   jax.ShapeDtypeStruct((1, SSM_W), F32)],
        compiler_params=_cp(("arbitrary",)),
    )(d_m, og, o, yg, gpre, proj, proj, proj, proj, b_glu.reshape(1, SSM_W), wa.reshape(1, ATTN_W),
      ws.reshape(1, SSM_W))


def _rms_bwd_x(x, norm_w, d_hn, d_out):
    L = x.shape[0]
    tm = _tile(L, 256)

    def body(x_ref, w_ref, dh_ref, do_ref, gx_ref, gw_ref):
        i = pl.program_id(0)

        @pl.when(i == 0)
        def _():
            gw_ref[...] = jnp.zeros_like(gw_ref)

        xv, dh = x_ref[...], dh_ref[...]
        r = lax.rsqrt(jnp.mean(xv * xv, axis=-1, keepdims=True) + NORM_EPS)
        xh = xv * r
        gw_ref[...] += jnp.sum(dh * xh, axis=0, keepdims=True)
        gx = dh * w_ref[...]
        gx_ref[...] = do_ref[...] + r * (gx - xh * jnp.mean(gx * xh, axis=-1, keepdims=True))

    blk = pl.BlockSpec((tm, D_MODEL), lambda i: (i, 0))
    row = pl.BlockSpec((1, D_MODEL), lambda i: (0, 0))
    return pl.pallas_call(
        body, name="rms_bwd_x", grid=(L // tm,), in_specs=[blk, row, blk, blk], out_specs=[blk, row],
        out_shape=[jax.ShapeDtypeStruct((L, D_MODEL), F32), jax.ShapeDtypeStruct((1, D_MODEL), F32)],
        compiler_params=_cp(("arbitrary",)),
    )(x, norm_w.reshape(1, D_MODEL), d_hn, d_out)


def _rope_table(positions):
    inv_freq = ROPE_THETA ** (-jnp.arange(0, HEAD_DIM, 2, dtype=F32) / HEAD_DIM)
    ang = positions.astype(F32)[:, None] * inv_freq
    c, s = jnp.cos(ang), jnp.sin(ang)
    return jnp.concatenate([c, c, c, c, -s, s, -s, s], axis=1)


def _local_step(x, positions, target, small, wt_in, w_glu, w_out):
    tab = _rope_table(positions)
    mt_b, scat_b, ocat_b, a16 = _ssm_prep(small)
    perm = _chunk_perm()

    proj, hn = _rms_inproj(x, small["norm_w"], wt_in)
    q_rot, k_rot = _qk_prep(proj, tab, small["q_norm_w"], small["k_norm_w"])
    og, o, lse = _attn_fwd(q_rot, k_rot, proj, small["sinks"])
    y, yg, hx = _ssm_fwd(proj, perm, mt_b, scat_b, ocat_b, a16, small["d_skip"])
    gpre = _mm(yg, w_glu, "nn", F32, "glu_fwd")
    merged = _merge(og, yg, gpre, proj, small["b_glu"], small["attn_out_norm_w"], small["ssm_out_norm_w"])
    d_out, d_out_b, loss_parts = _outproj_loss(merged, w_out, x, target)
    loss = 0.5 * jnp.sum(loss_parts[:, 0, 0]) / D_MODEL

    g_w_out = _mm(merged, d_out_b, "tn", F32, "grad_w_out")
    d_m = _mm(d_out_b, w_out, "nt", F32, "d_merged")
    d_o, d_za, d_zs, d_g, d_yg1, g_wa, g_ws, g_bglu = _merge_bwd(
        d_m, og, o, yg, gpre, proj, small["b_glu"], small["attn_out_norm_w"], small["ssm_out_norm_w"])
    g_w_glu = _mm(yg, d_g, "tn", F32, "grad_w_glu")
    d_yg = _mm(d_g, w_glu, "nt", F32, "d_yg", add=d_yg1)
    d_u, g_mt, g_scat, g_ocat, g_a16, g_dskip = _ssm_bwd(d_yg, y, proj, hx, perm, mt_b, scat_b, ocat_b, a16,
                                                         small["d_skip"])
    g_small = _ssm_prep_bwd(small, g_mt, g_scat, g_ocat, g_a16)
    d_q, d_k, d_v, g_sinks = _attn_bwd(q_rot, k_rot, proj, small["sinks"], d_o, o, lse)
    d_proj, g_qw, g_kw = _qk_prep_bwd(proj, tab, small["q_norm_w"], small["k_norm_w"], d_q, d_k, d_v,
                                      d_za, d_u, d_zs)
    g_qw = g_qw[0, :HEAD_DIM] + g_qw[0, HEAD_DIM:]
    g_kw = g_kw[0, :HEAD_DIM] + g_kw[0, HEAD_DIM:]
    g_wt_in = _mm(d_proj, hn, "tn", F32, "grad_w_in")
    d_hn = _mm(d_proj, wt_in, "nn", F32, "d_hn")
    grad_x, g_nw = _rms_bwd_x(x, small["norm_w"], d_hn, d_out)

    g_small.update(norm_w=g_nw.reshape(-1), q_norm_w=g_qw.reshape(-1), k_norm_w=g_kw.reshape(-1),
                   sinks=g_sinks[0, :N_HEADS], d_skip=g_dskip.reshape(-1), b_glu=g_bglu.reshape(-1),
                   attn_out_norm_w=g_wa.reshape(-1), ssm_out_norm_w=g_ws.reshape(-1))
    return loss, grad_x, g_wt_in, g_w_glu, g_w_out, g_small


_ANY = pl.BlockSpec(memory_space=pl.ANY)


def _all_gather_rows(blocks, name):
    n = len(blocks)

    def body(*refs):
        ins, outs = refs[:n], refs[n:2 * n]
        send_sems, recv_sems, local_sems = refs[2 * n:]
        x, y, c = lax.axis_index("x"), lax.axis_index("y"), lax.axis_index("c")
        me, sibling = (x, y, c), (x, y, 1 - c)
        chips = [(1 - x, y), (x, 1 - y), (1 - x, 1 - y)]

        def slot(k, dev):
            return outs[k].at[4 * dev[0] + 2 * dev[1] + dev[2]]

        def copy(k, q, block, to, src=None):
            return pltpu.make_async_remote_copy(
                src_ref=slot(k, block) if src is None else src, dst_ref=slot(k, block),
                send_sem=send_sems.at[k, q], recv_sem=recv_sems.at[k, q], device_id=to, device_id_type=MESH)

        mine = [pltpu.make_async_copy(ins[k], slot(k, me), local_sems.at[k]) for k in range(n)]
        for cp in mine:
            cp.start()
        first = []
        for k in range(n):
            first.append(copy(k, 0, me, sibling, src=ins[k]))
            first += [copy(k, 1 + j, me, (*chip, c), src=ins[k]) for j, chip in enumerate(chips)]
        for cp in first:
            cp.start()
        passed = []
        for j, chip in enumerate(chips):
            for k in range(n):
                copy(k, 1 + j, (*chip, c), me).wait_recv()
                fwd = copy(k, 4 + j, (*chip, c), sibling)
                fwd.start()
                passed.append(fwd)
        for k in range(n):
            copy(k, 0, sibling, me).wait_recv()
            for j, chip in enumerate(chips):
                copy(k, 4 + j, (*chip, 1 - c), me).wait_recv()
        for cp in first + passed:
            cp.wait_send()
        for cp in mine:
            cp.wait()

    outs = pl.pallas_call(
        body,
        name=name,
        in_specs=[_ANY] * n,
        out_specs=[_ANY] * n,
        out_shape=[jax.ShapeDtypeStruct((N_DEV,) + b.shape, b.dtype) for b in blocks],
        scratch_shapes=[pltpu.SemaphoreType.DMA((n, 7)), pltpu.SemaphoreType.DMA((n, 7)),
                        pltpu.SemaphoreType.DMA((n,))],
    )(*blocks)
    return list(outs)


def _pair_exchange(grads, name):
    n = len(grads)

    def body(*refs):
        ins, outs = refs[:n], refs[n:2 * n]
        send_sems, recv_sems = refs[2 * n:]
        x, y, c = lax.axis_index("x"), lax.axis_index("y"), lax.axis_index("c")
        copies = []
        for k in range(n):
            for chip in range(4):
                copies.append(pltpu.make_async_remote_copy(
                    src_ref=ins[k].at[2 * chip + (1 - c)], dst_ref=outs[k].at[chip],
                    send_sem=send_sems.at[k, chip], recv_sem=recv_sems.at[k, chip],
                    device_id=(x, y, 1 - c), device_id_type=MESH))
        for cp in copies:
            cp.start()
        for cp in copies:
            cp.wait()

    outs = pl.pallas_call(
        body,
        name=name,
        in_specs=[_ANY] * n,
        out_specs=[_ANY] * n,
        out_shape=[jax.ShapeDtypeStruct((4,) + g.shape[1:], g.dtype) for g in grads],
        scratch_shapes=[pltpu.SemaphoreType.DMA((n, 4)), pltpu.SemaphoreType.DMA((n, 4))],
    )(*grads)
    return list(outs)


def _pair_sum(g, ra, core, out_dtype, name):
    _, r, C = g.shape
    tr = _tile(r, 128)

    def body(c_ref, g_ref, ra_ref, p_ref):
        p_ref[...] = (g_ref[...] + ra_ref[...]).astype(p_ref.dtype)

    return pl.pallas_call(
        body,
        name=name,
        grid_spec=pltpu.PrefetchScalarGridSpec(
            num_scalar_prefetch=1,
            grid=(4, r // tr),
            in_specs=[pl.BlockSpec((1, tr, C), lambda j, t, c_ref: (2 * j + c_ref[0], t, 0)),
                      pl.BlockSpec((1, tr, C), lambda j, t, c_ref: (j, t, 0))],
            out_specs=pl.BlockSpec((1, tr, C), lambda j, t, c_ref: (j, t, 0)),
        ),
        out_shape=jax.ShapeDtypeStruct((4, r, C), out_dtype),
        compiler_params=_cp(("parallel", "parallel")),
    )(core, g, ra)


def _chip_exchange(parts, name):
    n = len(parts)

    def body(*refs):
        ins, outs = refs[:n], refs[n:2 * n]
        send_sems, recv_sems = refs[2 * n:]
        x, y, c = lax.axis_index("x"), lax.axis_index("y"), lax.axis_index("c")
        chips = [(1 - x, y), (x, 1 - y), (1 - x, 1 - y)]
        copies = []
        for k in range(n):
            for q, chip in enumerate(chips):
                copies.append(pltpu.make_async_remote_copy(
                    src_ref=ins[k].at[2 * chip[0] + chip[1]], dst_ref=outs[k].at[q],
                    send_sem=send_sems.at[k, q], recv_sem=recv_sems.at[k, q],
                    device_id=(*chip, c), device_id_type=MESH))
        for cp in copies:
            cp.start()
        for cp in copies:
            cp.wait()

    outs = pl.pallas_call(
        body,
        name=name,
        in_specs=[_ANY] * n,
        out_specs=[_ANY] * n,
        out_shape=[jax.ShapeDtypeStruct((3,) + p.shape[1:], p.dtype) for p in parts],
        scratch_shapes=[pltpu.SemaphoreType.DMA((n, 3)), pltpu.SemaphoreType.DMA((n, 3))],
    )(*parts)
    return list(outs)


def _chip_sum(p, rb, chip, name):
    _, r, C = p.shape
    tr = _tile(r, 128)

    def body(c_ref, p_ref, rb_ref, o_ref):
        acc = p_ref[0].astype(F32) + rb_ref[0].astype(F32)
        acc = acc + rb_ref[1].astype(F32)
        o_ref[...] = acc + rb_ref[2].astype(F32)

    return pl.pallas_call(
        body,
        name=name,
        grid_spec=pltpu.PrefetchScalarGridSpec(
            num_scalar_prefetch=1,
            grid=(r // tr,),
            in_specs=[pl.BlockSpec((1, tr, C), lambda t, c_ref: (c_ref[0], t, 0)),
                      pl.BlockSpec((3, tr, C), lambda t, c_ref: (0, t, 0))],
            out_specs=pl.BlockSpec((tr, C), lambda t, c_ref: (t, 0)),
        ),
        out_shape=jax.ShapeDtypeStruct((r, C), F32),
        compiler_params=_cp(("parallel",)),
    )(chip, p, rb)


def _adamw(g, w, m, v, name):
    R, C = g.shape
    tr = _tile(R, 256)
    c1 = 1.0 - ADAM_B1 ** ADAM_STEP
    c2 = 1.0 - ADAM_B2 ** ADAM_STEP

    def body(g_ref, w_ref, m_ref, v_ref, d_ref, nm_ref, nv_ref):
        gv = g_ref[...]
        nm = ADAM_B1 * m_ref[...] + (1.0 - ADAM_B1) * gv
        nv = ADAM_B2 * v_ref[...] + (1.0 - ADAM_B2) * (gv * gv)
        nm_ref[...] = nm
        nv_ref[...] = nv
        d_ref[...] = -ADAM_LR * ((nm / c1) / (jnp.sqrt(nv / c2) + ADAM_EPS) + ADAM_WD * w_ref[...])

    blk = pl.BlockSpec((tr, C), lambda i: (i, 0))
    return pl.pallas_call(
        body, name=name, grid=(R // tr,), in_specs=[blk] * 4, out_specs=[blk] * 3,
        out_shape=[jax.ShapeDtypeStruct((R, C), F32)] * 3, compiler_params=_cp(("parallel",)),
    )(g, w, m, v)


_SMALL = ("norm_w", "q_norm_w", "k_norm_w", "sinks", "a_re", "a_im", "log_step", "b_re", "b_im", "c_re", "c_im",
          "d_skip", "b_glu", "attn_out_norm_w", "ssm_out_norm_w")
_WEIGHTS = ("norm_w", "w_in", "q_norm_w", "k_norm_w", "sinks", "a_re", "a_im", "log_step", "b_re", "b_im", "c_re",
            "c_im", "d_skip", "w_glu", "b_glu", "attn_out_norm_w", "ssm_out_norm_w", "w_out")
_SMALL_2D = dict(norm_w=(1, 2048), q_norm_w=(1, 64), k_norm_w=(1, 64), sinks=(1, 16), a_re=(64, 64), a_im=(64, 64),
                 log_step=(1, 64), b_re=(4096, 16), b_im=(4096, 16), c_re=(1024, 64), c_im=(1024, 64),
                 d_skip=(1, 1024), b_glu=(1, 1024), attn_out_norm_w=(1, 1024), ssm_out_norm_w=(1, 1024))


def _slab_rows(n):
    return -(-n // 1024) * 8


_PACK_ROWS = 2304


def _pack(d):
    parts = []
    for n in _SMALL:
        flat = d[n].reshape(-1).astype(F32)
        rows = _slab_rows(flat.shape[0])
        parts.append(jnp.pad(flat, (0, rows * 128 - flat.shape[0])).reshape(rows, 128))
    used = sum(p.shape[0] for p in parts)
    parts.append(jnp.zeros((_PACK_ROWS - used, 128), F32))
    return jnp.concatenate(parts, axis=0)


def _unpack(packed, like):
    out, off = {}, 0
    for n in _SMALL:
        size = math.prod(like[n].shape)
        rows = _slab_rows(size)
        out[n] = packed[off:off + rows].reshape(-1)[:size].reshape(like[n].shape)
        off += rows
    return out


def _adamw_small(g, w, m, v):
    c1 = 1.0 - ADAM_B1 ** ADAM_STEP
    c2 = 1.0 - ADAM_B2 ** ADAM_STEP
    k = len(_SMALL)

    def body(*refs):
        ins, outs = refs[:4 * k], refs[4 * k:]
        for j in range(k):
            gv, wv, mv, vv = (ins[q * k + j][...] for q in range(4))
            nm = ADAM_B1 * mv + (1.0 - ADAM_B1) * gv
            nv = ADAM_B2 * vv + (1.0 - ADAM_B2) * (gv * gv)
            outs[j][...] = -ADAM_LR * ((nm / c1) / (jnp.sqrt(nv / c2) + ADAM_EPS) + ADAM_WD * wv)
            outs[k + j][...] = nm
            outs[2 * k + j][...] = nv

    args = [d[n].reshape(_SMALL_2D[n]) for d in (g, w, m, v) for n in _SMALL]
    shapes = [jax.ShapeDtypeStruct(_SMALL_2D[n], F32) for _ in range(3) for n in _SMALL]
    outs = pl.pallas_call(body, name="adamw_small", out_shape=shapes, compiler_params=_cp())(*args)
    res = []
    for q in range(3):
        res.append({n: outs[q * k + j].reshape(w[n].shape) for j, n in enumerate(_SMALL)})
    return res


def kernel(x, positions, norm_w, w_in, q_norm_w, k_norm_w, sinks, a_re, a_im, log_step, b_re, b_im, c_re, c_im, d_skip, w_glu, b_glu, attn_out_norm_w, ssm_out_norm_w, w_out, loss_target, m_norm_w, m_w_in, m_q_norm_w, m_k_norm_w, m_sinks, m_a_re, m_a_im, m_log_step, m_b_re, m_b_im, m_c_re, m_c_im, m_d_skip, m_w_glu, m_b_glu, m_attn_out_norm_w, m_ssm_out_norm_w, m_w_out, v_norm_w, v_w_in, v_q_norm_w, v_k_norm_w, v_sinks, v_a_re, v_a_im, v_log_step, v_b_re, v_b_im, v_c_re, v_c_im, v_d_skip, v_w_glu, v_b_glu, v_attn_out_norm_w, v_ssm_out_norm_w, v_w_out):
    w = dict(norm_w=norm_w, w_in=w_in, q_norm_w=q_norm_w, k_norm_w=k_norm_w, sinks=sinks, a_re=a_re, a_im=a_im,
             log_step=log_step, b_re=b_re, b_im=b_im, c_re=c_re, c_im=c_im, d_skip=d_skip, w_glu=w_glu, b_glu=b_glu,
             attn_out_norm_w=attn_out_norm_w, ssm_out_norm_w=ssm_out_norm_w, w_out=w_out)
    m = dict(norm_w=m_norm_w, w_in=m_w_in, q_norm_w=m_q_norm_w, k_norm_w=m_k_norm_w, sinks=m_sinks, a_re=m_a_re,
             a_im=m_a_im, log_step=m_log_step, b_re=m_b_re, b_im=m_b_im, c_re=m_c_re, c_im=m_c_im, d_skip=m_d_skip,
             w_glu=m_w_glu, b_glu=m_b_glu, attn_out_norm_w=m_attn_out_norm_w, ssm_out_norm_w=m_ssm_out_norm_w,
             w_out=m_w_out)
    v = dict(norm_w=v_norm_w, w_in=v_w_in, q_norm_w=v_q_norm_w, k_norm_w=v_k_norm_w, sinks=v_sinks, a_re=v_a_re,
             a_im=v_a_im, log_step=v_log_step, b_re=v_b_re, b_im=v_b_im, c_re=v_c_re, c_im=v_c_im, d_skip=v_d_skip,
             w_glu=v_w_glu, b_glu=v_b_glu, attn_out_norm_w=v_attn_out_norm_w, ssm_out_norm_w=v_ssm_out_norm_w,
             w_out=v_w_out)
    core = lax.axis_index("c").astype(jnp.int32).reshape(1)
    chip = (2 * lax.axis_index("x") + lax.axis_index("y")).astype(jnp.int32).reshape(1)

    wt_in, wf_glu, wf_out = _all_gather_rows(
        [w_in.T.astype(BF16), w_glu.astype(BF16), w_out.astype(BF16)], "gather_weights")
    wt_in = wt_in.reshape(IN_W, D_MODEL)
    wf_glu = wf_glu.reshape(SSM_W, SSM_W)
    wf_out = wf_out.reshape(D_MODEL, D_MODEL)

    small = {n: w[n] for n in _SMALL}
    loss, grad_x, g_wt_in, g_w_glu, g_w_out, g_small = _local_step(
        x[0], positions[0], loss_target[0], small, wt_in, wf_glu, wf_out)
    loss = lax.psum(loss, ("x", "y", "c"))

    full = [g_wt_in.reshape(N_DEV, IN_W // N_DEV, D_MODEL), g_w_glu.reshape(N_DEV, SSM_W // N_DEV, SSM_W),
            g_w_out.reshape(N_DEV, D_MODEL // N_DEV, D_MODEL), _pack(g_small).reshape(N_DEV, _PACK_ROWS // N_DEV, 128)]
    from_sibling = _pair_exchange(full, "pair_exchange")
    wire = (BF16, BF16, BF16, F32)
    parts = [_pair_sum(g, ra, core, dt, f"pair_sum_{k}") for k, (g, ra, dt) in enumerate(zip(full, from_sibling, wire))]
    from_chips = _chip_exchange(parts, "chip_exchange")
    red = [_chip_sum(p, rb, chip, f"chip_sum_{k}") for k, (p, rb) in enumerate(zip(parts, from_chips))]
    g_in, g_glu, g_out = red[0].T, red[1], red[2]
    (g_packed,) = _all_gather_rows([red[3]], "gather_small")
    g_packed = g_packed.reshape(_PACK_ROWS, 128)

    grads = _unpack(g_packed, w)
    grads.update(w_in=g_in, w_glu=g_glu, w_out=g_out)
    delta, new_m, new_v = {}, {}, {}
    for n in ("w_in", "w_glu", "w_out"):
        delta[n], new_m[n], new_v[n] = _adamw(grads[n], w[n], m[n], v[n], f"adamw_{n}")
    d_s, m_s, v_s = _adamw_small(grads, w, m, v)
    delta.update(d_s)
    new_m.update(m_s)
    new_v.update(v_s)

    return (loss, grad_x[None], *[grads[n] for n in _WEIGHTS], *[delta[n] for n in _WEIGHTS],
            *[new_m[n] for n in _WEIGHTS], *[new_v[n] for n in _WEIGHTS])
```

```python
import functools
import math

import jax
import jax.numpy as jnp
from jax import lax
from jax.experimental import pallas as pl
from jax.experimental.pallas import tpu as pltpu

F32 = jnp.float32
BF16 = jnp.bfloat16

D_MODEL = 2048
ATTN_W = 1024
KV_W = 256
SSM_W = 1024
HEAD_DIM = 64
N_HEADS = 16
N_KV = 4
KV_REP = 4
IN_W = 4608
BLOCK = 128
ROPE_THETA = 10000.0
NORM_EPS = 1e-6
SSM_G = 64
SSM_P = 64
SSM_H = 16
CHUNK = 16
CW = CHUNK * SSM_H
N_DEV = 8

ADAM_LR = 0.001
ADAM_B1 = 0.9
ADAM_B2 = 0.999
ADAM_EPS = 1e-08
ADAM_WD = 0.01
ADAM_STEP = 10

VMEM_LIMIT = 56 * 1024 * 1024
MESH = pl.DeviceIdType.MESH


def _cp(sem=None):
    if sem is None:
        return pltpu.CompilerParams(vmem_limit_bytes=VMEM_LIMIT)
    return pltpu.CompilerParams(vmem_limit_bytes=VMEM_LIMIT, dimension_semantics=sem)


def _sigmoid(x):
    return 1.0 / (1.0 + jnp.exp(-x))


def _silu(x):
    return x * _sigmoid(x)


def _dsilu(x):
    s = _sigmoid(x)
    return s * (1.0 + x * (1.0 - s))


_GELU_C = math.sqrt(2.0 / math.pi)


def _gelu(y):
    t = jnp.tanh(_GELU_C * (y + 0.044715 * y * y * y))
    return 0.5 * y * (1.0 + t)


def _dgelu(y):
    t = jnp.tanh(_GELU_C * (y + 0.044715 * y * y * y))
    return 0.5 * (1.0 + t) + 0.5 * y * (1.0 - t * t) * _GELU_C * (1.0 + 3.0 * 0.044715 * y * y)


def _tile(n, want):
    if n <= want:
        return n
    for t in range(want - want % 16, 0, -16):
        if n % t == 0:
            return t
    raise ValueError((n, want))


def _mm(a, b, mode, out_dtype, name, tm=512, tn=1024, add=None):
    if mode == "nn":
        (M, K), (K2, N) = a.shape, b.shape
    elif mode == "nt":
        (M, K), (N, K2) = a.shape, b.shape
    else:
        (K, M), (K2, N) = a.shape, b.shape
    assert K == K2
    tm, tn = _tile(M, tm), _tile(N, tn)
    dn = {"nn": _NN, "nt": _NT, "tn": _TN}[mode]

    def body(a_ref, b_ref, *rest):
        o_ref = rest[-1]
        acc = lax.dot_general(a_ref[...].astype(BF16), b_ref[...].astype(BF16), dn, preferred_element_type=F32)
        if add is not None:
            acc = acc + rest[0][...]
        o_ref[...] = acc.astype(o_ref.dtype)

    a_spec = pl.BlockSpec((K, tm), lambda j, i: (0, i)) if mode == "tn" else pl.BlockSpec((tm, K), lambda j, i: (i, 0))
    b_spec = pl.BlockSpec((tn, K), lambda j, i: (j, 0)) if mode == "nt" else pl.BlockSpec((K, tn), lambda j, i: (0, j))
    o_spec = pl.BlockSpec((tm, tn), lambda j, i: (i, j))
    extra = () if add is None else (add,)
    return pl.pallas_call(
        body,
        name=name,
        grid=(N // tn, M // tm),
        in_specs=[a_spec, b_spec] + [o_spec] * len(extra),
        out_specs=o_spec,
        out_shape=jax.ShapeDtypeStruct((M, N), out_dtype),
        compiler_params=_cp(("parallel", "parallel")),
    )(a, b, *extra)


def _rms_inproj(x, norm_w, wt_in):
    L = x.shape[0]
    tm, tn = _tile(L, 1024), 768
    nj = IN_W // tn

    def body(x_ref, w_ref, wt_ref, proj_ref, hn_ref, hn_scr):
        j = pl.program_id(1)

        @pl.when(j == 0)
        def _():
            xv = x_ref[...]
            r = lax.rsqrt(jnp.mean(xv * xv, axis=-1, keepdims=True) + NORM_EPS)
            hn = (xv * r * w_ref[...]).astype(BF16)
            hn_scr[...] = hn
            hn_ref[...] = hn

        proj_ref[...] = lax.dot_general(hn_scr[...], wt_ref[...], (((1,), (1,)), ((), ())),
                                        preferred_element_type=F32)

    return pl.pallas_call(
        body,
        name="rms_inproj",
        grid=(L // tm, nj),
        in_specs=[pl.BlockSpec((tm, D_MODEL), lambda i, j: (i, 0)),
                  pl.BlockSpec((1, D_MODEL), lambda i, j: (0, 0)),
                  pl.BlockSpec((tn, D_MODEL), lambda i, j: (j, 0))],
        out_specs=[pl.BlockSpec((tm, tn), lambda i, j: (i, j)),
                   pl.BlockSpec((tm, D_MODEL), lambda i, j: (i, 0))],
        out_shape=[jax.ShapeDtypeStruct((L, IN_W), F32), jax.ShapeDtypeStruct((L, D_MODEL), BF16)],
        scratch_shapes=[pltpu.VMEM((tm, D_MODEL), BF16)],
        compiler_params=_cp(("parallel", "arbitrary")),
    )(x, norm_w.reshape(1, D_MODEL), wt_in)


def _seg_sum(v):
    a = lax.broadcasted_iota(jnp.int32, (128, 128), 0) // HEAD_DIM
    b = lax.broadcasted_iota(jnp.int32, (128, 128), 1) // HEAD_DIM
    ones = jnp.where(a == b, 1.0, 0.0).astype(BF16)
    hi = v.astype(BF16)
    lo = (v - hi.astype(F32)).astype(BF16)
    return jnp.dot(hi, ones, preferred_element_type=F32) + jnp.dot(lo, ones, preferred_element_type=F32)


def _rot_half(t):
    lane = lax.broadcasted_iota(jnp.int32, t.shape, 1)
    return jnp.where(lane % HEAD_DIM < HEAD_DIM // 2, pltpu.roll(t, 128 - HEAD_DIM // 2, 1),
                     pltpu.roll(t, HEAD_DIM // 2, 1))


def _norm_rope(raw, w, cos, sin):
    r = lax.rsqrt(_seg_sum(raw * raw) * (1.0 / HEAD_DIM) + NORM_EPS)
    tn = raw * r * w
    return r, tn * cos + _rot_half(tn) * sin


def _norm_rope_bwd(d_rot, raw, w, cos, sin):
    r = lax.rsqrt(_seg_sum(raw * raw) * (1.0 / HEAD_DIM) + NORM_EPS)
    d_tn = d_rot * cos + _rot_half(d_rot * sin)
    xh = raw * r
    gw = d_tn * w
    d_raw = r * (gw - xh * (_seg_sum(gw * xh) * (1.0 / HEAD_DIM)))
    return d_raw, d_tn * xh


def _band_mask2(has_prev):
    qi = lax.broadcasted_iota(jnp.int32, (2 * BLOCK, 2 * BLOCK), 0) % BLOCK + BLOCK
    kj = lax.broadcasted_iota(jnp.int32, (2 * BLOCK, 2 * BLOCK), 1)
    rel = qi - kj
    return (rel >= 0) & (rel < BLOCK) & ((kj >= BLOCK) | has_prev)


def _half_tiles(pair):
    lo = lax.broadcasted_iota(jnp.int32, pair.shape, 1) < HEAD_DIM
    sw = pltpu.roll(pair, HEAD_DIM, 1)
    z = jnp.zeros_like(pair)
    return (jnp.where(lo, pair, z).astype(BF16), jnp.where(lo, z, sw).astype(BF16),
            jnp.where(lo, sw, z).astype(BF16), jnp.where(lo, z, pair).astype(BF16))


def _two_rows(top, bottom):
    row = lax.broadcasted_iota(jnp.int32, (2 * BLOCK, 1), 0)
    return jnp.where(row < BLOCK, top, bottom)


def _lane_col(mat, h):
    lane = lax.broadcasted_iota(jnp.int32, mat.shape, 1)
    return jnp.sum(jnp.where(lane == h, mat, 0.0), axis=1, keepdims=True)


_SCALE = 1.0 / math.sqrt(HEAD_DIM)
_NT = (((1,), (1,)), ((), ()))
_NN = (((1,), (0,)), ((), ()))
_TN = (((0,), (0,)), ((), ()))


def _qk_prep(proj, tab, qw, kw):
    L = proj.shape[0]
    tm = _tile(L, 512)

    def body(q_ref, k_ref, t_ref, qw_ref, kw_ref, qo_ref, ko_ref):
        cos, sin = t_ref[:, :128], t_ref[:, 128:]
        for c in range(ATTN_W // 128):
            _, qr = _norm_rope(q_ref[:, c * 128:(c + 1) * 128], qw_ref[...], cos, sin)
            qo_ref[:, c * 128:(c + 1) * 128] = (qr * _SCALE).astype(BF16)
        for c in range(KV_W // 128):
            _, kr = _norm_rope(k_ref[:, c * 128:(c + 1) * 128], kw_ref[...], cos, sin)
            ko_ref[:, c * 128:(c + 1) * 128] = kr.astype(BF16)

    row = pl.BlockSpec((1, 128), lambda i: (0, 0))
    return pl.pallas_call(
        body,
        name="qk_prep",
        grid=(L // tm,),
        in_specs=[pl.BlockSpec((tm, ATTN_W), lambda i: (i, 0)), pl.BlockSpec((tm, KV_W), lambda i: (i, 4)),
                  pl.BlockSpec((tm, 256), lambda i: (i, 0)), row, row],
        out_specs=[pl.BlockSpec((tm, ATTN_W), lambda i: (i, 0)), pl.BlockSpec((tm, KV_W), lambda i: (i, 0))],
        out_shape=[jax.ShapeDtypeStruct((L, ATTN_W), BF16), jax.ShapeDtypeStruct((L, KV_W), BF16)],
        compiler_params=_cp(("parallel",)),
    )(proj, proj, tab, jnp.tile(qw, 2).reshape(1, 128), jnp.tile(kw, 2).reshape(1, 128))


def _group_tiles(g, kt, vt):
    a, b = divmod(g, 2)
    return kt[a][2 * b], kt[a][2 * b + 1], vt[a][2 * b], vt[a][2 * b + 1]


def _attn_fwd(q, k, proj, sinks):
    L = proj.shape[0]
    nb = L // BLOCK

    def body(q_ref, kc_ref, kp_ref, vc_ref, vp_ref, z0_ref, z1_ref, sink_ref, og_ref, o_ref, lse_ref):
        i = pl.program_id(0)
        mask = _band_mask2(i > 0)
        z = jnp.concatenate([z0_ref[...], z1_ref[...]], axis=1)
        lane = lax.broadcasted_iota(jnp.int32, (BLOCK, 128), 1)
        kt = [_half_tiles(jnp.concatenate([kp_ref[:, a * 128:(a + 1) * 128], kc_ref[:, a * 128:(a + 1) * 128]],
                                          axis=0).astype(F32)) for a in range(2)]
        vt = [_half_tiles(jnp.concatenate([vp_ref[:, a * 128:(a + 1) * 128], vc_ref[:, a * 128:(a + 1) * 128]],
                                          axis=0)) for a in range(2)]
        lse_mat = jnp.zeros((BLOCK, 128), F32)
        outs = []
        for g in range(N_KV):
            k_lo, k_hi, v_lo, v_hi = _group_tiles(g, kt, vt)
            q2 = jnp.concatenate([q_ref[:, 2 * g * 128:(2 * g + 1) * 128],
                                  q_ref[:, (2 * g + 1) * 128:(2 * g + 2) * 128]], axis=0)
            acc = jnp.zeros((2 * BLOCK, 128), F32)
            for half, (kh, vh) in enumerate(((k_lo, v_lo), (k_hi, v_hi))):
                h_top, h_bot = 4 * g + half, 4 * g + 2 + half
                s = jnp.where(mask, lax.dot_general(q2, kh, _NT, preferred_element_type=F32), -1e30)
                sink = _two_rows(sink_ref[h_top], sink_ref[h_bot])
                m = jnp.maximum(jnp.max(s, axis=-1, keepdims=True), sink)
                e = jnp.exp(s - m)
                den = jnp.sum(e, axis=-1, keepdims=True) + jnp.exp(sink - m)
                p = e / den
                acc = acc + jnp.dot(p.astype(BF16), vh, preferred_element_type=F32)
                lse = m + jnp.log(den)
                lse_mat = jnp.where(lane == h_top, lse[:BLOCK], lse_mat)
                lse_mat = jnp.where(lane == h_bot, lse[BLOCK:], lse_mat)
            outs += [acc[:BLOCK], acc[BLOCK:]]
        o = jnp.concatenate(outs, axis=1)
        o_ref[...] = o
        og_ref[...] = o * _silu(z)
        lse_ref[...] = lse_mat

    prev = lambda i: jnp.maximum(i - 1, 0)
    return pl.pallas_call(
        body,
        name="attn_fwd",
        grid=(nb,),
        in_specs=[pl.BlockSpec((BLOCK, ATTN_W), lambda i: (i, 0)),
                  pl.BlockSpec((BLOCK, KV_W), lambda i: (i, 0)),
                  pl.BlockSpec((BLOCK, KV_W), lambda i: (prev(i), 0)),
                  pl.BlockSpec((BLOCK, KV_W), lambda i: (i, 5)),
                  pl.BlockSpec((BLOCK, KV_W), lambda i: (prev(i), 5)),
                  pl.BlockSpec((BLOCK, 512), lambda i: (i, 3)),
                  pl.BlockSpec((BLOCK, 512), lambda i: (i, 4)),
                  pl.BlockSpec(memory_space=pltpu.SMEM)],
        out_specs=[pl.BlockSpec((BLOCK, ATTN_W), lambda i: (i, 0)),
                   pl.BlockSpec((BLOCK, ATTN_W), lambda i: (i, 0)),
                   pl.BlockSpec((BLOCK, 128), lambda i: (i, 0))],
        out_shape=[jax.ShapeDtypeStruct((L, ATTN_W), F32), jax.ShapeDtypeStruct((L, ATTN_W), F32),
                   jax.ShapeDtypeStruct((L, 128), F32)],
        compiler_params=_cp(("parallel",)),
    )(q, k, k, proj, proj, proj, proj, sinks)


def _attn_bwd(q, k, proj, sinks, d_o, o, lse):
    L = proj.shape[0]
    nb = L // BLOCK

    def body(q_ref, kc_ref, kp_ref, vc_ref, vp_ref, do_ref, o_ref, lse_ref, sink_ref,
             dq_ref, dk_ref, dv_ref, gs_ref, ck_scr, cv_scr):
        i = pl.program_id(0)

        @pl.when(i == 0)
        def _():
            gs_ref[...] = jnp.zeros_like(gs_ref)
            ck_scr[...] = jnp.zeros_like(ck_scr)
            cv_scr[...] = jnp.zeros_like(cv_scr)

        @pl.when(i == nb)
        def _():
            dk_ref[...] = ck_scr[...]
            dv_ref[...] = cv_scr[...]

        @pl.when(i < nb)
        def _():
            mask = _band_mask2(i > 0)
            lane = lax.broadcasted_iota(jnp.int32, (1, 128), 1)
            lo = lax.broadcasted_iota(jnp.int32, (2 * BLOCK, 128), 1) < HEAD_DIM
            lse_c = lse_ref[...]
            kt = [_half_tiles(jnp.concatenate([kp_ref[:, a * 128:(a + 1) * 128], kc_ref[:, a * 128:(a + 1) * 128]],
                                              axis=0).astype(F32)) for a in range(2)]
            vt = [_half_tiles(jnp.concatenate([vp_ref[:, a * 128:(a + 1) * 128], vc_ref[:, a * 128:(a + 1) * 128]],
                                              axis=0)) for a in range(2)]
            gs = jnp.zeros((1, 128), F32)
            dq_parts = []
            dk_acc = [jnp.zeros((2 * BLOCK, 128), F32) for _ in range(2)]
            dv_acc = [jnp.zeros((2 * BLOCK, 128), F32) for _ in range(2)]
            for g in range(N_KV):
                a, b = divmod(g, 2)
                k_lo, k_hi, v_lo, v_hi = _group_tiles(g, kt, vt)
                t0, t1 = slice(2 * g * 128, (2 * g + 1) * 128), slice((2 * g + 1) * 128, (2 * g + 2) * 128)
                q2 = jnp.concatenate([q_ref[:, t0], q_ref[:, t1]], axis=0)
                do2 = jnp.concatenate([do_ref[:, t0], do_ref[:, t1]], axis=0)
                prod = do2 * jnp.concatenate([o_ref[:, t0], o_ref[:, t1]], axis=0)
                do2_b = do2.astype(BF16)
                dq2 = jnp.zeros((2 * BLOCK, 128), F32)
                dk_h, dv_h = [], []
                for half, (kh, vh) in enumerate(((k_lo, v_lo), (k_hi, v_hi))):
                    h_top, h_bot = 4 * g + half, 4 * g + 2 + half
                    lse = jnp.concatenate([_lane_col(lse_c, h_top), _lane_col(lse_c, h_bot)], axis=0)
                    sink = _two_rows(sink_ref[h_top], sink_ref[h_bot])
                    delta = jnp.sum(jnp.where(lo == (half == 0), prod, 0.0), axis=1, keepdims=True)
                    s = jnp.where(mask, lax.dot_general(q2, kh, _NT, preferred_element_type=F32), -1e30)
                    p = jnp.exp(s - lse)
                    dp = lax.dot_general(do2_b, vh, _NT, preferred_element_type=F32)
                    ds_b = (p * (dp - delta)).astype(BF16)
                    p_b = p.astype(BF16)
                    dq2 = dq2 + jnp.dot(ds_b, kh, preferred_element_type=F32)
                    dk_h.append(lax.dot_general(ds_b, q2, _TN, preferred_element_type=F32))
                    dv_h.append(lax.dot_general(p_b, do2_b, _TN, preferred_element_type=F32))
                    gsink = -jnp.exp(sink - lse) * delta
                    row = lax.broadcasted_iota(jnp.int32, (2 * BLOCK, 1), 0)
                    gs = gs + jnp.where(lane == h_top, jnp.sum(jnp.where(row < BLOCK, gsink, 0.0)), 0.0)
                    gs = gs + jnp.where(lane == h_bot, jnp.sum(jnp.where(row >= BLOCK, gsink, 0.0)), 0.0)
                dq_parts += [dq2[:BLOCK], dq2[BLOCK:]]
                for acc, parts in ((dk_acc, dk_h), (dv_acc, dv_h)):
                    t = jnp.where(lo, parts[0], parts[1])
                    t = t + pltpu.roll(t, HEAD_DIM, 1)
                    acc[a] = acc[a] + jnp.where(lo == (b == 0), t, 0.0)
            dq_ref[...] = jnp.concatenate(dq_parts, axis=1)
            dk_full = jnp.concatenate(dk_acc, axis=1)
            dv_full = jnp.concatenate(dv_acc, axis=1)
            dk_ref[...] = ck_scr[...] + dk_full[:BLOCK]
            dv_ref[...] = cv_scr[...] + dv_full[:BLOCK]
            ck_scr[...] = dk_full[BLOCK:]
            cv_scr[...] = dv_full[BLOCK:]
            gs_ref[...] += gs

    cur = lambda i: jnp.minimum(i, nb - 1)
    prev = lambda i: jnp.maximum(jnp.minimum(i, nb - 1) - 1, 0)
    done = lambda i: jnp.maximum(i - 1, 0)
    bs = pl.BlockSpec
    return pl.pallas_call(
        body,
        name="attn_bwd",
        grid=(nb + 1,),
        in_specs=[bs((BLOCK, ATTN_W), lambda i: (cur(i), 0)),
                  bs((BLOCK, KV_W), lambda i: (cur(i), 0)), bs((BLOCK, KV_W), lambda i: (prev(i), 0)),
                  bs((BLOCK, KV_W), lambda i: (cur(i), 5)), bs((BLOCK, KV_W), lambda i: (prev(i), 5)),
                  bs((BLOCK, ATTN_W), lambda i: (cur(i), 0)), bs((BLOCK, ATTN_W), lambda i: (cur(i), 0)),
                  bs((BLOCK, 128), lambda i: (cur(i), 0)), bs(memory_space=pltpu.SMEM)],
        out_specs=[bs((BLOCK, ATTN_W), lambda i: (cur(i), 0)),
                   bs((BLOCK, KV_W), lambda i: (done(i), 0)), bs((BLOCK, KV_W), lambda i: (done(i), 0)),
                   bs((1, 128), lambda i: (0, 0))],
        out_shape=[jax.ShapeDtypeStruct((L, ATTN_W), F32), jax.ShapeDtypeStruct((L, KV_W), F32),
                   jax.ShapeDtypeStruct((L, KV_W), F32), jax.ShapeDtypeStruct((1, 128), F32)],
        scratch_shapes=[pltpu.VMEM((BLOCK, KV_W), F32), pltpu.VMEM((BLOCK, KV_W), F32)],
        compiler_params=_cp(("arbitrary",)),
    )(q, k, k, proj, proj, d_o, o, lse, sinks)


def _qk_prep_bwd(proj, tab, qw, kw, d_q, d_k, d_v, d_za, d_u, d_zs):
    L = proj.shape[0]
    tm = _tile(L, 512)
    z0 = ATTN_W + 2 * KV_W

    def body(q_ref, k_ref, t_ref, qw_ref, kw_ref, dq_ref, dk_ref, dv_ref, dza_ref, du_ref, dzs_ref,
             out_ref, gq_ref, gk_ref):
        i = pl.program_id(0)

        @pl.when(i == 0)
        def _():
            gq_ref[...] = jnp.zeros_like(gq_ref)
            gk_ref[...] = jnp.zeros_like(gk_ref)

        cos, sin = t_ref[:, :128], t_ref[:, 128:]
        gq = jnp.zeros((1, 128), F32)
        gk = jnp.zeros((1, 128), F32)
        for c in range(ATTN_W // 128):
            cs = slice(c * 128, (c + 1) * 128)
            d_raw, gw = _norm_rope_bwd(dq_ref[:, cs] * _SCALE, q_ref[:, cs], qw_ref[...], cos, sin)
            out_ref[:, cs] = d_raw.astype(BF16)
            gq = gq + jnp.sum(gw, axis=0, keepdims=True)
        for c in range(KV_W // 128):
            cs = slice(c * 128, (c + 1) * 128)
            d_raw, gw = _norm_rope_bwd(dk_ref[:, cs], k_ref[:, cs], kw_ref[...], cos, sin)
            out_ref[:, ATTN_W + c * 128:ATTN_W + (c + 1) * 128] = d_raw.astype(BF16)
            gk = gk + jnp.sum(gw, axis=0, keepdims=True)
        out_ref[:, ATTN_W + KV_W:z0] = dv_ref[...].astype(BF16)
        out_ref[:, z0:z0 + ATTN_W] = dza_ref[...]
        out_ref[:, z0 + ATTN_W:z0 + ATTN_W + SSM_W] = du_ref[...].astype(BF16)
        out_ref[:, z0 + ATTN_W + SSM_W:] = dzs_ref[...]
        gq_ref[...] += gq
        gk_ref[...] += gk

    row = pl.BlockSpec((1, 128), lambda i: (0, 0))
    blk = lambda w, c: pl.BlockSpec((tm, w), lambda i: (i, c))
    return pl.pallas_call(
        body,
        name="qk_prep_bwd",
        grid=(L // tm,),
        in_specs=[blk(ATTN_W, 0), blk(KV_W, 4), blk(256, 0), row, row, blk(ATTN_W, 0), blk(KV_W, 0), blk(KV_W, 0),
                  blk(ATTN_W, 0), blk(SSM_W, 0), blk(SSM_W, 0)],
        out_specs=[blk(IN_W, 0), row, row],
        out_shape=[jax.ShapeDtypeStruct((L, IN_W), BF16), jax.ShapeDtypeStruct((1, 128), F32),
                   jax.ShapeDtypeStruct((1, 128), F32)],
        compiler_params=_cp(("arbitrary",)),
    )(proj, proj, tab, jnp.tile(qw, 2).reshape(1, 128), jnp.tile(kw, 2).reshape(1, 128), d_q, d_k, d_v,
      d_za, d_u, d_zs)


def _cmul(a, b):
    return a[0] * b[0] - a[1] * b[1], a[0] * b[1] + a[1] * b[0]


def _cmul_conj(a, b):
    return a[0] * b[0] + a[1] * b[1], a[1] * b[0] - a[0] * b[1]


def _cadd(a, b):
    return a[0] + b[0], a[1] + b[1]


def _dot3(a, b, dn):
    ah, bh = a.astype(BF16), b.astype(BF16)
    al, bl = (a - ah.astype(F32)).astype(BF16), (b - bh.astype(F32)).astype(BF16)
    d = lambda u, v: lax.dot_general(u, v, dn, preferred_element_type=F32)
    return d(ah, bh) + d(ah, bl) + d(al, bh)


def _s5_discretise(a_re, a_im, ls, cosx, sinx, bt):
    delta = jnp.exp(ls)
    er = jnp.exp(a_re * delta)
    lb = (er * cosx, er * sinx)
    den = a_re * a_re + a_im * a_im
    coef = _cmul_conj((lb[0] - 1.0, lb[1]), (a_re, a_im))
    coef = (coef[0] / den, coef[1] / den)
    return delta, lb, coef, den, _cmul(coef, bt)


def _powers(lb):
    pw = [(jnp.ones_like(lb[0]), jnp.zeros_like(lb[0]))]
    for _ in range(CHUNK):
        pw.append(_cmul(pw[-1], lb))
    return pw


def _block_rows(a, pw, idx):
    blocks = [_cmul(a, pw[i]) for i in idx]
    return (jnp.concatenate([b[0] for b in blocks], axis=0), jnp.concatenate([b[1] for b in blocks], axis=0))


def _block_rows_bwd(g, a, pw, idx, g_pw):
    g_a = (jnp.zeros_like(a[0]), jnp.zeros_like(a[0]))
    for j, i in enumerate(idx):
        gj = (g[0][j * SSM_H:(j + 1) * SSM_H], g[1][j * SSM_H:(j + 1) * SSM_H])
        g_a = _cadd(g_a, _cmul_conj(gj, pw[i]))
        gp = _cmul_conj(gj, a)
        g_pw[i] = _cadd(g_pw[i], (jnp.sum(gp[0], axis=0, keepdims=True), jnp.sum(gp[1], axis=0, keepdims=True)))
    return g_a


_IDX_S = [CHUNK - 1 - s for s in range(CHUNK)]
_IDX_O = [t + 1 for t in range(CHUNK)]
_IDX_K = list(range(CHUNK))
_PREP_IN = 9


def _prep_args(p):
    row = lambda t: t.reshape(SSM_G, 1, SSM_P)
    xi = p["a_im"] * jnp.exp(p["log_step"])[:, None]
    return (row(p["a_re"]), row(p["a_im"]), row(jnp.broadcast_to(p["log_step"][:, None], (SSM_G, SSM_P))),
            row(jnp.cos(xi)), row(jnp.sin(xi)), p["b_re"].transpose(0, 2, 1), p["b_im"].transpose(0, 2, 1),
            p["c_re"], p["c_im"])


PREP_GROUPS = 4


def _prep_specs():
    r1 = pl.BlockSpec((PREP_GROUPS, 1, SSM_P), lambda g: (g, 0, 0))
    r16 = pl.BlockSpec((PREP_GROUPS, SSM_H, SSM_P), lambda g: (g, 0, 0))
    return [r1] * 5 + [r16] * 4, r1, r16


def _ssm_prep(p):
    def one_group(q, are, aim, ls, cosx, sinx, btr, bti, cre, cim, mt_ref, s_ref, o_ref, a_ref):
        _, lb, _, _, bb = _s5_discretise(are[q], aim[q], ls[q], cosx[q], sinx[q], (btr[q], bti[q]))
        pw = _powers(lb)
        c = (cre[q], cim[q])
        sc = _block_rows(bb, pw, _IDX_S)
        ot = _block_rows(c, pw, _IDX_O)
        ok = _block_rows(c, pw, _IDX_K)
        s_ref[q] = jnp.concatenate([sc[0], sc[1]], axis=1).astype(BF16)
        o_ref[q] = jnp.concatenate([ot[0], -ot[1]], axis=1).astype(BF16)
        a_ref[q] = jnp.concatenate([pw[CHUNK][0], pw[CHUNK][1]], axis=1)
        kt = _dot3(jnp.concatenate([bb[0], -bb[1]], axis=1), jnp.concatenate([ok[0], ok[1]], axis=1), _NT)
        lane = lax.broadcasted_iota(jnp.int32, kt.shape, 1)
        for s in range(CHUNK):
            blk = kt if s == 0 else jnp.where(lane >= SSM_H * s, pltpu.roll(kt, SSM_H * s, 1), 0.0)
            mt_ref[q, s * SSM_H:(s + 1) * SSM_H, :] = blk.astype(BF16)

    def body(*refs):
        for q in range(PREP_GROUPS):
            one_group(q, *refs)

    in_specs, r1, _ = _prep_specs()
    g3 = lambda r, c: pl.BlockSpec((PREP_GROUPS, r, c), lambda g: (g, 0, 0))
    return pl.pallas_call(
        body,
        name="ssm_prep",
        grid=(SSM_G // PREP_GROUPS,),
        in_specs=in_specs,
        out_specs=[g3(CW, CW), g3(CW, 2 * SSM_P), g3(CW, 2 * SSM_P), g3(1, 2 * SSM_P)],
        out_shape=[jax.ShapeDtypeStruct((SSM_G, CW, CW), BF16), jax.ShapeDtypeStruct((SSM_G, CW, 2 * SSM_P), BF16),
                   jax.ShapeDtypeStruct((SSM_G, CW, 2 * SSM_P), BF16),
                   jax.ShapeDtypeStruct((SSM_G, 1, 2 * SSM_P), F32)],
        compiler_params=_cp(("parallel",)),
    )(*_prep_args(p))


def _ssm_prep_bwd(p, g_mt, g_scat, g_ocat, g_a16):
    def one_group(q, are, aim, ls, cosx, sinx, btr, bti, cre, cim, gmt_ref, gs_ref, go_ref, ga_ref,
                  g_are, g_aim, g_ls, g_btr, g_bti, g_cre, g_cim):
        lam = (are[q], aim[q])
        bt = (btr[q], bti[q])
        delta, lb, coef, den, bb = _s5_discretise(lam[0], lam[1], ls[q], cosx[q], sinx[q], bt)
        pw = _powers(lb)
        c = (cre[q], cim[q])
        ok = _block_rows(c, pw, _IDX_K)
        g_pw = [(jnp.zeros_like(lb[0]), jnp.zeros_like(lb[0])) for _ in range(CHUNK + 1)]
        lane = lax.broadcasted_iota(jnp.int32, (SSM_H, CW), 1)
        g_kt = gmt_ref[q, :SSM_H, :]
        for s in range(1, CHUNK):
            blk = gmt_ref[q, s * SSM_H:(s + 1) * SSM_H, :]
            g_kt = g_kt + jnp.where(lane < CW - SSM_H * s, pltpu.roll(blk, CW - SSM_H * s, 1), 0.0)
        a1 = jnp.concatenate([bb[0], -bb[1]], axis=1)
        b1 = jnp.concatenate([ok[0], ok[1]], axis=1)
        g_a1 = _dot3(g_kt, b1, _NN)
        g_b1 = _dot3(g_kt, a1, _TN)
        g_bb = (g_a1[:, :SSM_P], -g_a1[:, SSM_P:])
        g_c = _block_rows_bwd((g_b1[:, :SSM_P], g_b1[:, SSM_P:]), c, pw, _IDX_K, g_pw)
        gs = gs_ref[q]
        g_bb = _cadd(g_bb, _block_rows_bwd((gs[:, :SSM_P], gs[:, SSM_P:]), bb, pw, _IDX_S, g_pw))
        go = go_ref[q]
        g_c = _cadd(g_c, _block_rows_bwd((go[:, :SSM_P], -go[:, SSM_P:]), c, pw, _IDX_O, g_pw))
        ga = ga_ref[q]
        g_pw[CHUNK] = _cadd(g_pw[CHUNK], (ga[:, :SSM_P], ga[:, SSM_P:]))
        g_lb = (jnp.zeros_like(lb[0]), jnp.zeros_like(lb[0]))
        for l in range(CHUNK - 1, -1, -1):
            g_lb = _cadd(g_lb, _cmul_conj(g_pw[l + 1], pw[l]))
            g_pw[l] = _cadd(g_pw[l], _cmul_conj(g_pw[l + 1], lb))
        g_bt = _cmul_conj(g_bb, coef)
        gc = _cmul_conj(g_bb, bt)
        g_coef = (jnp.sum(gc[0], axis=0, keepdims=True), jnp.sum(gc[1], axis=0, keepdims=True))
        lam_den = (lam[0] / den, lam[1] / den)
        g_lb = _cadd(g_lb, _cmul(g_coef, lam_den))
        t = _cmul(_cmul_conj(g_coef, coef), lam_den)
        g_x = _cmul_conj(g_lb, lb)
        g_lam = (g_x[0] * delta - t[0], g_x[1] * delta - t[1])
        g_are[q] = g_lam[0]
        g_aim[q] = g_lam[1]
        g_ls[q] = (g_x[0] * lam[0] + g_x[1] * lam[1]) * delta
        g_btr[q] = g_bt[0]
        g_bti[q] = g_bt[1]
        g_cre[q] = g_c[0]
        g_cim[q] = g_c[1]

    def body(*refs):
        for q in range(PREP_GROUPS):
            one_group(q, *refs)

    in_specs, r1, r16 = _prep_specs()
    g3 = lambda r, c: pl.BlockSpec((PREP_GROUPS, r, c), lambda g: (g, 0, 0))
    rows = jax.ShapeDtypeStruct((SSM_G, 1, SSM_P), F32)
    mats = jax.ShapeDtypeStruct((SSM_G, SSM_H, SSM_P), F32)
    g_are, g_aim, g_ls, g_btr, g_bti, g_cre, g_cim = pl.pallas_call(
        body,
        name="ssm_prep_bwd",
        grid=(SSM_G // PREP_GROUPS,),
        in_specs=in_specs + [g3(CW, CW), g3(CW, 2 * SSM_P), g3(CW, 2 * SSM_P), g3(1, 2 * SSM_P)],
        out_specs=[r1] * 3 + [r16] * 4,
        out_shape=[rows] * 3 + [mats] * 4,
        compiler_params=_cp(("parallel",)),
    )(*_prep_args(p), g_mt, g_scat, g_ocat, g_a16)
    return dict(a_re=g_are.reshape(SSM_G, SSM_P), a_im=g_aim.reshape(SSM_G, SSM_P),
                log_step=jnp.sum(g_ls.reshape(SSM_G, SSM_P), axis=1),
                b_re=g_btr.transpose(0, 2, 1), b_im=g_bti.transpose(0, 2, 1), c_re=g_cre, c_im=g_cim)


def _cmul_const(xv, ar, ai):
    return xv * ar + pltpu.roll(xv, SSM_P, 1) * ai


def _chunk_scan(inc, a_row, reverse):
    n = inc.shape[0]
    lane = lax.broadcasted_iota(jnp.int32, (1, 2 * SSM_P), 1)
    row = lax.broadcasted_iota(jnp.int32, inc.shape, 0)
    sign = jnp.where(lane < SSM_P, -1.0, 1.0)
    ar = jnp.where(lane < SSM_P, a_row, pltpu.roll(a_row, SSM_P, 1))
    ai = jnp.where(lane < SSM_P, pltpu.roll(a_row, SSM_P, 1), a_row)
    if reverse:
        ai = -ai
    xv = inc
    s = 1
    while s < n:
        if reverse:
            sh = jnp.where(row < n - s, pltpu.roll(xv, n - s, 0), 0.0)
        else:
            sh = jnp.where(row >= s, pltpu.roll(xv, s, 0), 0.0)
        xv = xv + _cmul_const(sh, ar, ai * sign)
        ar, ai = ar * ar - ai * ai, 2.0 * ar * ai
        s *= 2
    return xv


def _shift_rows(xv, reverse):
    n = xv.shape[0]
    row = lax.broadcasted_iota(jnp.int32, xv.shape, 0)
    if reverse:
        return jnp.where(row < n - 1, pltpu.roll(xv, n - 1, 0), 0.0)
    return jnp.where(row >= 1, pltpu.roll(xv, 1, 0), 0.0)


GB = 128 // SSM_H
U_COL0 = (ATTN_W + 2 * KV_W + ATTN_W) // 128


HALF = CHUNK // 2


def _chunk_perm():
    r = jnp.arange(HALF * 128)
    t, g8, h = r // 128, (r % 128) // SSM_H, r % SSM_H
    return ((g8 * 128 + t * SSM_H + h)[:, None] == jnp.arange(GB * 128)[None, :]).astype(BF16)


def _load_perm(p_hbm, p_scr, sem):
    @pl.when(pl.program_id(0) == 0)
    def _():
        cp = pltpu.make_async_copy(p_hbm, p_scr, sem)
        cp.start()
        cp.wait()


def _rows_to_chunks(pieces, perm):
    halves = [jnp.dot(jnp.concatenate(pieces[k * HALF:(k + 1) * HALF], axis=1).astype(BF16), perm,
                      preferred_element_type=F32).astype(BF16) for k in range(2)]
    return [jnp.concatenate([hv[:, g * 128:(g + 1) * 128] for hv in halves], axis=1) for g in range(GB)]


def _chunks_to_rows(groups, perm, two_pass):
    pieces = []
    for k in range(2):
        v = jnp.concatenate([gv[:, k * 128:(k + 1) * 128] for gv in groups], axis=1)
        hi = v.astype(BF16)
        out = lax.dot_general(hi, perm, _NT, preferred_element_type=F32)
        if two_pass:
            lo = (v - hi.astype(F32)).astype(BF16)
            out = out + lax.dot_general(lo, perm, _NT, preferred_element_type=F32)
        pieces += [out[:, t * 128:(t + 1) * 128] for t in range(HALF)]
    return pieces


def _ssm_fwd(proj, perm, mt, scat, ocat, a16, d_skip):
    L = proj.shape[0]
    nc = L // CHUNK

    def body(u_ref, p_hbm, mt_ref, s_ref, o_ref, a_ref, d_ref, y_ref, yg_ref, h_ref, p_scr, sem):
        _load_perm(p_hbm, p_scr, sem)
        perm = p_scr[...]
        rows = [pl.ds(t, nc, stride=CHUNK) for t in range(CHUNK)]
        ua = _rows_to_chunks([u_ref[r, :] for r in rows], perm)
        ys = []
        for g in range(GB):
            uv = ua[g]
            inc = jnp.dot(uv, s_ref[g], preferred_element_type=F32)
            hx = _shift_rows(_chunk_scan(inc, a_ref[g], False), False)
            h_ref[g] = hx
            ys.append(jnp.dot(uv, mt_ref[g], preferred_element_type=F32)
                      + lax.dot_general(hx.astype(BF16), o_ref[g], _NT, preferred_element_type=F32))
        yp = _chunks_to_rows(ys, perm, True)
        for t, r in enumerate(rows):
            y = yp[t] + d_ref[...] * u_ref[r, :]
            y_ref[r, :] = y
            yg_ref[r, :] = _gelu(y)

    g3 = lambda r, c: pl.BlockSpec((GB, r, c), lambda g: (g, 0, 0))
    col = pl.BlockSpec((L, 128), lambda g: (0, g))
    return pl.pallas_call(
        body,
        name="ssm_fwd",
        grid=(SSM_G // GB,),
        in_specs=[pl.BlockSpec((L, 128), lambda g: (0, U_COL0 + g)), _ANY,
                  g3(CW, CW), g3(CW, 2 * SSM_P), g3(CW, 2 * SSM_P), g3(1, 2 * SSM_P),
                  pl.BlockSpec((1, 128), lambda g: (0, g))],
        out_specs=[col, col, g3(nc, 2 * SSM_P)],
        out_shape=[jax.ShapeDtypeStruct((L, SSM_W), F32), jax.ShapeDtypeStruct((L, SSM_W), F32),
                   jax.ShapeDtypeStruct((SSM_G, nc, 2 * SSM_P), F32)],
        scratch_shapes=[pltpu.VMEM((HALF * 128, GB * 128), BF16), pltpu.SemaphoreType.DMA],
        compiler_params=_cp(("arbitrary",)),
    )(proj, perm, mt, scat, ocat, a16, d_skip.reshape(1, SSM_W))


def _ssm_bwd(d_yg, y, proj, hx, perm, mt, scat, ocat, a16, d_skip):
    L = proj.shape[0]
    nc = L // CHUNK

    def body(dg_ref, y_ref, u_ref, h_ref, p_hbm, mt_ref, s_ref, o_ref, a_ref, d_ref,
             du_ref, gmt_ref, gs_ref, go_ref, ga_ref, gd_ref, p_scr, sem):
        _load_perm(p_hbm, p_scr, sem)
        perm = p_scr[...]
        rows = [pl.ds(t, nc, stride=CHUNK) for t in range(CHUNK)]
        us = [u_ref[r, :] for r in rows]
        dys = [dg_ref[r, :] * _dgelu(y_ref[r, :]) for r in rows]
        gd = jnp.zeros((1, 128), F32)
        for uv, dy in zip(us, dys):
            gd = gd + jnp.sum(dy * uv, axis=0, keepdims=True)
        gd_ref[...] = gd
        ua = _rows_to_chunks(us, perm)
        dya = _rows_to_chunks(dys, perm)
        lane = lax.broadcasted_iota(jnp.int32, (1, 2 * SSM_P), 1)
        dus = []
        for g in range(GB):
            uv, dy, hx_v = ua[g], dya[g], h_ref[g]
            dh = jnp.dot(dy, o_ref[g], preferred_element_type=F32)
            dinc = _shift_rows(_chunk_scan(dh, a_ref[g], True), True)
            dinc_b = dinc.astype(BF16)
            dus.append(lax.dot_general(dy, mt_ref[g], _NT, preferred_element_type=F32)
                       + lax.dot_general(dinc_b, s_ref[g], _NT, preferred_element_type=F32))
            gmt_ref[g] = lax.dot_general(uv, dy, _TN, preferred_element_type=F32)
            gs_ref[g] = lax.dot_general(uv, dinc_b, _TN, preferred_element_type=F32)
            go_ref[g] = lax.dot_general(dy, hx_v.astype(BF16), _TN, preferred_element_type=F32)
            p1 = dinc * hx_v
            p2 = pltpu.roll(dinc, SSM_P, 1) * hx_v
            t1 = jnp.sum(p1 + pltpu.roll(p1, SSM_P, 1), axis=0, keepdims=True)
            t2 = jnp.sum(p2 - pltpu.roll(p2, SSM_P, 1), axis=0, keepdims=True)
            ga_ref[g] = jnp.where(lane < SSM_P, t1, pltpu.roll(t2, SSM_P, 1))
        dup = _chunks_to_rows(dus, perm, False)
        for t, r in enumerate(rows):
            du_ref[r, :] = dup[t] + d_ref[...] * dys[t]

    g3 = lambda r, c: pl.BlockSpec((GB, r, c), lambda g: (g, 0, 0))
    col = pl.BlockSpec((L, 128), lambda g: (0, g))
    row = pl.BlockSpec((1, 128), lambda g: (0, g))
    return pl.pallas_call(
        body,
        name="ssm_bwd",
        grid=(SSM_G // GB,),
        in_specs=[col, col, pl.BlockSpec((L, 128), lambda g: (0, U_COL0 + g)), g3(nc, 2 * SSM_P), _ANY,
                  g3(CW, CW), g3(CW, 2 * SSM_P), g3(CW, 2 * SSM_P), g3(1, 2 * SSM_P), row],
        out_specs=[col, g3(CW, CW), g3(CW, 2 * SSM_P), g3(CW, 2 * SSM_P), g3(1, 2 * SSM_P), row],
        out_shape=[jax.ShapeDtypeStruct((L, SSM_W), F32), jax.ShapeDtypeStruct((SSM_G, CW, CW), F32),
                   jax.ShapeDtypeStruct((SSM_G, CW, 2 * SSM_P), F32),
                   jax.ShapeDtypeStruct((SSM_G, CW, 2 * SSM_P), F32),
                   jax.ShapeDtypeStruct((SSM_G, 1, 2 * SSM_P), F32),
                   jax.ShapeDtypeStruct((1, SSM_W), F32)],
        scratch_shapes=[pltpu.VMEM((HALF * 128, GB * 128), BF16), pltpu.SemaphoreType.DMA],
        compiler_params=_cp(("arbitrary",)),
    )(d_yg, y, proj, hx, perm, mt, scat, ocat, a16, d_skip.reshape(1, SSM_W))


def _merge(og, yg, gpre, proj, b_glu, wa, ws):
    L = og.shape[0]
    tm = _tile(L, 256)

    def body(og_ref, yg_ref, gp_ref, z0_ref, z1_ref, b_ref, wa_ref, ws_ref, m_ref):
        zs = jnp.concatenate([z0_ref[...], z1_ref[...]], axis=1)
        os_ = yg_ref[...] * _sigmoid(gp_ref[...] + b_ref[...]) * _silu(zs)
        ogv = og_ref[...]
        ra = lax.rsqrt(jnp.mean(ogv * ogv, axis=-1, keepdims=True) + NORM_EPS)
        rs = lax.rsqrt(jnp.mean(os_ * os_, axis=-1, keepdims=True) + NORM_EPS)
        m_ref[:, :ATTN_W] = (ogv * ra * wa_ref[...]).astype(BF16)
        m_ref[:, ATTN_W:] = (os_ * rs * ws_ref[...]).astype(BF16)

    row = lambda w: pl.BlockSpec((1, w), lambda i: (0, 0))
    return pl.pallas_call(
        body,
        name="merge",
        grid=(L // tm,),
        in_specs=[pl.BlockSpec((tm, ATTN_W), lambda i: (i, 0)), pl.BlockSpec((tm, SSM_W), lambda i: (i, 0)),
                  pl.BlockSpec((tm, SSM_W), lambda i: (i, 0)),
                  pl.BlockSpec((tm, 512), lambda i: (i, 7)), pl.BlockSpec((tm, 512), lambda i: (i, 8)),
                  row(SSM_W), row(ATTN_W), row(SSM_W)],
        out_specs=pl.BlockSpec((tm, D_MODEL), lambda i: (i, 0)),
        out_shape=jax.ShapeDtypeStruct((L, D_MODEL), BF16),
        compiler_params=_cp(("parallel",)),
    )(og, yg, gpre, proj, proj, b_glu.reshape(1, SSM_W), wa.reshape(1, ATTN_W), ws.reshape(1, SSM_W))


def _outproj_loss(merged, w_out, x, target):
    L = x.shape[0]
    tm, tn = _tile(L, 512), 1024
    ni, nj = L // tm, D_MODEL // tn

    def body(m_ref, w_ref, x_ref, t_ref, d_ref, db_ref, l_ref):
        out = x_ref[...] + jnp.dot(m_ref[...], w_ref[...], preferred_element_type=F32)
        diff = out - t_ref[...]
        d = diff * (1.0 / D_MODEL)
        d_ref[...] = d
        db_ref[...] = d.astype(BF16)
        l_ref[...] = jnp.full((1, 8, 128), jnp.sum(diff * diff), F32)

    return pl.pallas_call(
        body,
        name="outproj_loss",
        grid=(nj, ni),
        in_specs=[pl.BlockSpec((tm, D_MODEL), lambda j, i: (i, 0)),
                  pl.BlockSpec((D_MODEL, tn), lambda j, i: (0, j)),
                  pl.BlockSpec((tm, tn), lambda j, i: (i, j)),
                  pl.BlockSpec((tm, tn), lambda j, i: (i, j))],
        out_specs=[pl.BlockSpec((tm, tn), lambda j, i: (i, j)), pl.BlockSpec((tm, tn), lambda j, i: (i, j)),
                   pl.BlockSpec((1, 8, 128), lambda j, i: (i * nj + j, 0, 0))],
        out_shape=[jax.ShapeDtypeStruct((L, D_MODEL), F32), jax.ShapeDtypeStruct((L, D_MODEL), BF16),
                   jax.ShapeDtypeStruct((ni * nj, 8, 128), F32)],
        compiler_params=_cp(("parallel", "parallel")),
    )(merged, w_out, x, target)


def _merge_bwd(d_m, og, o, yg, gpre, proj, b_glu, wa, ws):
    L = og.shape[0]
    tm = _tile(L, 256)

    def body(dm_ref, og_ref, o_ref, yg_ref, gp_ref, za0_ref, za1_ref, zs0_ref, zs1_ref, b_ref, wa_ref, ws_ref,
             do_ref, dza_ref, dzs_ref, dg_ref, dyg_ref, gwa_ref, gws_ref, gb_ref):
        i = pl.program_id(0)

        @pl.when(i == 0)
        def _():
            gwa_ref[...] = jnp.zeros_like(gwa_ref)
            gws_ref[...] = jnp.zeros_like(gws_ref)
            gb_ref[...] = jnp.zeros_like(gb_ref)

        za = jnp.concatenate([za0_ref[...], za1_ref[...]], axis=1)
        zs = jnp.concatenate([zs0_ref[...], zs1_ref[...]], axis=1)
        ogv, dma = og_ref[...], dm_ref[:, :ATTN_W]
        ra = lax.rsqrt(jnp.mean(ogv * ogv, axis=-1, keepdims=True) + NORM_EPS)
        xh = ogv * ra
        gwa_ref[...] += jnp.sum(dma * xh, axis=0, keepdims=True)
        gx = dma * wa_ref[...]
        d_og = ra * (gx - xh * jnp.mean(gx * xh, axis=-1, keepdims=True))
        do_ref[...] = d_og * _silu(za)
        dza_ref[...] = (d_og * o_ref[...] * _dsilu(za)).astype(BF16)
        ygv = yg_ref[...]
        sg = _sigmoid(gp_ref[...] + b_ref[...])
        y2 = ygv * sg
        sz = _silu(zs)
        os_ = y2 * sz
        dms = dm_ref[:, ATTN_W:]
        rs = lax.rsqrt(jnp.mean(os_ * os_, axis=-1, keepdims=True) + NORM_EPS)
        xs = os_ * rs
        gws_ref[...] += jnp.sum(dms * xs, axis=0, keepdims=True)
        gxs = dms * ws_ref[...]
        d_os = rs * (gxs - xs * jnp.mean(gxs * xs, axis=-1, keepdims=True))
        dzs_ref[...] = (d_os * y2 * _dsilu(zs)).astype(BF16)
        d_y2 = d_os * sz
        d_g = d_y2 * ygv * sg * (1.0 - sg)
        dg_ref[...] = d_g.astype(BF16)
        gb_ref[...] += jnp.sum(d_g, axis=0, keepdims=True)
        dyg_ref[...] = d_y2 * sg

    row = lambda w: pl.BlockSpec((1, w), lambda i: (0, 0))
    full = lambda w: pl.BlockSpec((tm, w), lambda i: (i, 0))
    half = lambda c: pl.BlockSpec((tm, 512), lambda i: (i, c))
    return pl.pallas_call(
        body,
        name="merge_bwd",
        grid=(L // tm,),
        in_specs=[full(D_MODEL), full(ATTN_W), full(ATTN_W), full(SSM_W), full(SSM_W),
                  half(3), half(4), half(7), half(8), row(SSM_W), row(ATTN_W), row(SSM_W)],
        out_specs=[full(ATTN_W), full(ATTN_W), full(SSM_W), full(SSM_W), full(SSM_W),
                   row(ATTN_W), row(SSM_W), row(SSM_W)],
        out_shape=[jax.ShapeDtypeStruct((L, ATTN_W), F32), jax.ShapeDtypeStruct((L, ATTN_W), BF16),
                   jax.ShapeDtypeStruct((L, SSM_W), BF16), jax.ShapeDtypeStruct((L, SSM_W), BF16),
                   jax.ShapeDtypeStruct((L, SSM_W), F32),
                   jax.ShapeDtypeStruct((1, ATTN_W), F32), jax.ShapeDtypeStruct((1, SSM_W), F32),
                   jax.ShapeDtypeStruct((1, SSM_W), F32)],
        compiler_params=_cp(("arbitrary",)),
    )(d_m, og, o, yg, gpre, proj, proj, proj, proj, b_glu.reshape(1, SSM_W), wa.reshape(1, ATTN_W),
      ws.reshape(1, SSM_W))


def _rms_bwd_x(x, norm_w, d_hn, d_out):
    L = x.shape[0]
    tm = _tile(L, 256)

    def body(x_ref, w_ref, dh_ref, do_ref, gx_ref, gw_ref):
        i = pl.program_id(0)

        @pl.when(i == 0)
        def _():
            gw_ref[...] = jnp.zeros_like(gw_ref)

        xv, dh = x_ref[...], dh_ref[...]
        r = lax.rsqrt(jnp.mean(xv * xv, axis=-1, keepdims=True) + NORM_EPS)
        xh = xv * r
        gw_ref[...] += jnp.sum(dh * xh, axis=0, keepdims=True)
        gx = dh * w_ref[...]
        gx_ref[...] = do_ref[...] + r * (gx - xh * jnp.mean(gx * xh, axis=-1, keepdims=True))

    blk = pl.BlockSpec((tm, D_MODEL), lambda i: (i, 0))
    row = pl.BlockSpec((1, D_MODEL), lambda i: (0, 0))
    return pl.pallas_call(
        body, name="rms_bwd_x", grid=(L // tm,), in_specs=[blk, row, blk, blk], out_specs=[blk, row],
        out_shape=[jax.ShapeDtypeStruct((L, D_MODEL), F32), jax.ShapeDtypeStruct((1, D_MODEL), F32)],
        compiler_params=_cp(("arbitrary",)),
    )(x, norm_w.reshape(1, D_MODEL), d_hn, d_out)


def _rope_table(positions):
    inv_freq = ROPE_THETA ** (-jnp.arange(0, HEAD_DIM, 2, dtype=F32) / HEAD_DIM)
    ang = positions.astype(F32)[:, None] * inv_freq
    c, s = jnp.cos(ang), jnp.sin(ang)
    return jnp.concatenate([c, c, c, c, -s, s, -s, s], axis=1)


def _local_step(x, positions, target, small, wt_in, w_glu, w_out):
    tab = _rope_table(positions)
    mt_b, scat_b, ocat_b, a16 = _ssm_prep(small)
    perm = _chunk_perm()

    proj, hn = _rms_inproj(x, small["norm_w"], wt_in)
    q_rot, k_rot = _qk_prep(proj, tab, small["q_norm_w"], small["k_norm_w"])
    og, o, lse = _attn_fwd(q_rot, k_rot, proj, small["sinks"])
    y, yg, hx = _ssm_fwd(proj, perm, mt_b, scat_b, ocat_b, a16, small["d_skip"])
    gpre = _mm(yg, w_glu, "nn", F32, "glu_fwd")
    merged = _merge(og, yg, gpre, proj, small["b_glu"], small["attn_out_norm_w"], small["ssm_out_norm_w"])
    d_out, d_out_b, loss_parts = _outproj_loss(merged, w_out, x, target)
    loss = 0.5 * jnp.sum(loss_parts[:, 0, 0]) / D_MODEL

    g_w_out = _mm(merged, d_out_b, "tn", F32, "grad_w_out")
    d_m = _mm(d_out_b, w_out, "nt", F32, "d_merged")
    d_o, d_za, d_zs, d_g, d_yg1, g_wa, g_ws, g_bglu = _merge_bwd(
        d_m, og, o, yg, gpre, proj, small["b_glu"], small["attn_out_norm_w"], small["ssm_out_norm_w"])
    g_w_glu = _mm(yg, d_g, "tn", F32, "grad_w_glu")
    d_yg = _mm(d_g, w_glu, "nt", F32, "d_yg", add=d_yg1)
    d_u, g_mt, g_scat, g_ocat, g_a16, g_dskip = _ssm_bwd(d_yg, y, proj, hx, perm, mt_b, scat_b, ocat_b, a16,
                                                         small["d_skip"])
    g_small = _ssm_prep_bwd(small, g_mt, g_scat, g_ocat, g_a16)
    d_q, d_k, d_v, g_sinks = _attn_bwd(q_rot, k_rot, proj, small["sinks"], d_o, o, lse)
    d_proj, g_qw, g_kw = _qk_prep_bwd(proj, tab, small["q_norm_w"], small["k_norm_w"], d_q, d_k, d_v,
                                      d_za, d_u, d_zs)
    g_qw = g_qw[0, :HEAD_DIM] + g_qw[0, HEAD_DIM:]
    g_kw = g_kw[0, :HEAD_DIM] + g_kw[0, HEAD_DIM:]
    g_wt_in = _mm(d_proj, hn, "tn", F32, "grad_w_in")
    d_hn = _mm(d_proj, wt_in, "nn", F32, "d_hn")
    grad_x, g_nw = _rms_bwd_x(x, small["norm_w"], d_hn, d_out)

    g_small.update(norm_w=g_nw.reshape(-1), q_norm_w=g_qw.reshape(-1), k_norm_w=g_kw.reshape(-1),
                   sinks=g_sinks[0, :N_HEADS], d_skip=g_dskip.reshape(-1), b_glu=g_bglu.reshape(-1),
                   attn_out_norm_w=g_wa.reshape(-1), ssm_out_norm_w=g_ws.reshape(-1))
    return loss, grad_x, g_wt_in, g_w_glu, g_w_out, g_small


_ANY = pl.BlockSpec(memory_space=pl.ANY)


def _all_gather_rows(blocks, name):
    n = len(blocks)

    def body(*refs):
        ins, outs = refs[:n], refs[n:2 * n]
        send_sems, recv_sems, local_sems = refs[2 * n:]
        x, y, c = lax.axis_index("x"), lax.axis_index("y"), lax.axis_index("c")
        me, sibling = (x, y, c), (x, y, 1 - c)
        chips = [(1 - x, y), (x, 1 - y), (1 - x, 1 - y)]

        def slot(k, dev):
            return outs[k].at[4 * dev[0] + 2 * dev[1] + dev[2]]

        def copy(k, q, block, to, src=None):
            return pltpu.make_async_remote_copy(
                src_ref=slot(k, block) if src is None else src, dst_ref=slot(k, block),
                send_sem=send_sems.at[k, q], recv_sem=recv_sems.at[k, q], device_id=to, device_id_type=MESH)

        mine = [pltpu.make_async_copy(ins[k], slot(k, me), local_sems.at[k]) for k in range(n)]
        for cp in mine:
            cp.start()
        first = []
        for k in range(n):
            first.append(copy(k, 0, me, sibling, src=ins[k]))
            first += [copy(k, 1 + j, me, (*chip, c), src=ins[k]) for j, chip in enumerate(chips)]
        for cp in first:
            cp.start()
        passed = []
        for j, chip in enumerate(chips):
            for k in range(n):
                copy(k, 1 + j, (*chip, c), me).wait_recv()
                fwd = copy(k, 4 + j, (*chip, c), sibling)
                fwd.start()
                passed.append(fwd)
        for k in range(n):
            copy(k, 0, sibling, me).wait_recv()
            for j, chip in enumerate(chips):
                copy(k, 4 + j, (*chip, 1 - c), me).wait_recv()
        for cp in first + passed:
            cp.wait_send()
        for cp in mine:
            cp.wait()

    outs = pl.pallas_call(
        body,
        name=name,
        in_specs=[_ANY] * n,
        out_specs=[_ANY] * n,
        out_shape=[jax.ShapeDtypeStruct((N_DEV,) + b.shape, b.dtype) for b in blocks],
        scratch_shapes=[pltpu.SemaphoreType.DMA((n, 7)), pltpu.SemaphoreType.DMA((n, 7)),
                        pltpu.SemaphoreType.DMA((n,))],
    )(*blocks)
    return list(outs)


def _pair_exchange(grads, name):
    n = len(grads)

    def body(*refs):
        ins, outs = refs[:n], refs[n:2 * n]
        send_sems, recv_sems = refs[2 * n:]
        x, y, c = lax.axis_index("x"), lax.axis_index("y"), lax.axis_index("c")
        copies = []
        for k in range(n):
            for chip in range(4):
                copies.append(pltpu.make_async_remote_copy(
                    src_ref=ins[k].at[2 * chip + (1 - c)], dst_ref=outs[k].at[chip],
                    send_sem=send_sems.at[k, chip], recv_sem=recv_sems.at[k, chip],
                    device_id=(x, y, 1 - c), device_id_type=MESH))
        for cp in copies:
            cp.start()
        for cp in copies:
            cp.wait()

    outs = pl.pallas_call(
        body,
        name=name,
        in_specs=[_ANY] * n,
        out_specs=[_ANY] * n,
        out_shape=[jax.ShapeDtypeStruct((4,) + g.shape[1:], g.dtype) for g in grads],
        scratch_shapes=[pltpu.SemaphoreType.DMA((n, 4)), pltpu.SemaphoreType.DMA((n, 4))],
    )(*grads)
    return list(outs)


def _pair_sum(g, ra, core, out_dtype, name):
    _, r, C = g.shape
    tr = _tile(r, 128)

    def body(c_ref, g_ref, ra_ref, p_ref):
        p_ref[...] = (g_ref[...] + ra_ref[...]).astype(p_ref.dtype)

    return pl.pallas_call(
        body,
        name=name,
        grid_spec=pltpu.PrefetchScalarGridSpec(
            num_scalar_prefetch=1,
            grid=(4, r // tr),
            in_specs=[pl.BlockSpec((1, tr, C), lambda j, t, c_ref: (2 * j + c_ref[0], t, 0)),
                      pl.BlockSpec((1, tr, C), lambda j, t, c_ref: (j, t, 0))],
            out_specs=pl.BlockSpec((1, tr, C), lambda j, t, c_ref: (j, t, 0)),
        ),
        out_shape=jax.ShapeDtypeStruct((4, r, C), out_dtype),
        compiler_params=_cp(("parallel", "parallel")),
    )(core, g, ra)


def _chip_exchange(parts, name):
    n = len(parts)

    def body(*refs):
        ins, outs = refs[:n], refs[n:2 * n]
        send_sems, recv_sems = refs[2 * n:]
        x, y, c = lax.axis_index("x"), lax.axis_index("y"), lax.axis_index("c")
        chips = [(1 - x, y), (x, 1 - y), (1 - x, 1 - y)]
        copies = []
        for k in range(n):
            for q, chip in enumerate(chips):
                copies.append(pltpu.make_async_remote_copy(
                    src_ref=ins[k].at[2 * chip[0] + chip[1]], dst_ref=outs[k].at[q],
                    send_sem=send_sems.at[k, q], recv_sem=recv_sems.at[k, q],
                    device_id=(*chip, c), device_id_type=MESH))
        for cp in copies:
            cp.start()
        for cp in copies:
            cp.wait()

    outs = pl.pallas_call(
        body,
        name=name,
        in_specs=[_ANY] * n,
        out_specs=[_ANY] * n,
        out_shape=[jax.ShapeDtypeStruct((3,) + p.shape[1:], p.dtype) for p in parts],
        scratch_shapes=[pltpu.SemaphoreType.DMA((n, 3)), pltpu.SemaphoreType.DMA((n, 3))],
    )(*parts)
    return list(outs)


def _chip_sum(p, rb, chip, name):
    _, r, C = p.shape
    tr = _tile(r, 128)

    def body(c_ref, p_ref, rb_ref, o_ref):
        acc = p_ref[0].astype(F32) + rb_ref[0].astype(F32)
        acc = acc + rb_ref[1].astype(F32)
        o_ref[...] = acc + rb_ref[2].astype(F32)

    return pl.pallas_call(
        body,
        name=name,
        grid_spec=pltpu.PrefetchScalarGridSpec(
            num_scalar_prefetch=1,
            grid=(r // tr,),
            in_specs=[pl.BlockSpec((1, tr, C), lambda t, c_ref: (c_ref[0], t, 0)),
                      pl.BlockSpec((3, tr, C), lambda t, c_ref: (0, t, 0))],
            out_specs=pl.BlockSpec((tr, C), lambda t, c_ref: (t, 0)),
        ),
        out_shape=jax.ShapeDtypeStruct((r, C), F32),
        compiler_params=_cp(("parallel",)),
    )(chip, p, rb)


def _adamw(g, w, m, v, name):
    R, C = g.shape
    tr = _tile(R, 256)
    c1 = 1.0 - ADAM_B1 ** ADAM_STEP
    c2 = 1.0 - ADAM_B2 ** ADAM_STEP

    def body(g_ref, w_ref, m_ref, v_ref, d_ref, nm_ref, nv_ref):
        gv = g_ref[...]
        nm = ADAM_B1 * m_ref[...] + (1.0 - ADAM_B1) * gv
        nv = ADAM_B2 * v_ref[...] + (1.0 - ADAM_B2) * (gv * gv)
        nm_ref[...] = nm
        nv_ref[...] = nv
        d_ref[...] = -ADAM_LR * ((nm / c1) / (jnp.sqrt(nv / c2) + ADAM_EPS) + ADAM_WD * w_ref[...])

    blk = pl.BlockSpec((tr, C), lambda i: (i, 0))
    return pl.pallas_call(
        body, name=name, grid=(R // tr,), in_specs=[blk] * 4, out_specs=[blk] * 3,
        out_shape=[jax.ShapeDtypeStruct((R, C), F32)] * 3, compiler_params=_cp(("parallel",)),
    )(g, w, m, v)


_SMALL = ("norm_w", "q_norm_w", "k_norm_w", "sinks", "a_re", "a_im", "log_step", "b_re", "b_im", "c_re", "c_im",
          "d_skip", "b_glu", "attn_out_norm_w", "ssm_out_norm_w")
_WEIGHTS = ("norm_w", "w_in", "q_norm_w", "k_norm_w", "sinks", "a_re", "a_im", "log_step", "b_re", "b_im", "c_re",
            "c_im", "d_skip", "w_glu", "b_glu", "attn_out_norm_w", "ssm_out_norm_w", "w_out")
_SMALL_2D = dict(norm_w=(1, 2048), q_norm_w=(1, 64), k_norm_w=(1, 64), sinks=(1, 16), a_re=(64, 64), a_im=(64, 64),
                 log_step=(1, 64), b_re=(4096, 16), b_im=(4096, 16), c_re=(1024, 64), c_im=(1024, 64),
                 d_skip=(1, 1024), b_glu=(1, 1024), attn_out_norm_w=(1, 1024), ssm_out_norm_w=(1, 1024))


def _slab_rows(n):
    return -(-n // 1024) * 8


_PACK_ROWS = 2304


def _pack(d):
    parts = []
    for n in _SMALL:
        flat = d[n].reshape(-1).astype(F32)
        rows = _slab_rows(flat.shape[0])
        parts.append(jnp.pad(flat, (0, rows * 128 - flat.shape[0])).reshape(rows, 128))
    used = sum(p.shape[0] for p in parts)
    parts.append(jnp.zeros((_PACK_ROWS - used, 128), F32))
    return jnp.concatenate(parts, axis=0)


def _unpack(packed, like):
    out, off = {}, 0
    for n in _SMALL:
        size = math.prod(like[n].shape)
        rows = _slab_rows(size)
        out[n] = packed[off:off + rows].reshape(-1)[:size].reshape(like[n].shape)
        off += rows
    return out


def _adamw_small(g, w, m, v):
    c1 = 1.0 - ADAM_B1 ** ADAM_STEP
    c2 = 1.0 - ADAM_B2 ** ADAM_STEP
    k = len(_SMALL)

    def body(*refs):
        ins, outs = refs[:4 * k], refs[4 * k:]
        for j in range(k):
            gv, wv, mv, vv = (ins[q * k + j][...] for q in range(4))
            nm = ADAM_B1 * mv + (1.0 - ADAM_B1) * gv
            nv = ADAM_B2 * vv + (1.0 - ADAM_B2) * (gv * gv)
            outs[j][...] = -ADAM_LR * ((nm / c1) / (jnp.sqrt(nv / c2) + ADAM_EPS) + ADAM_WD * wv)
            outs[k + j][...] = nm
            outs[2 * k + j][...] = nv

    args = [d[n].reshape(_SMALL_2D[n]) for d in (g, w, m, v) for n in _SMALL]
    shapes = [jax.ShapeDtypeStruct(_SMALL_2D[n], F32) for _ in range(3) for n in _SMALL]
    outs = pl.pallas_call(body, name="adamw_small", out_shape=shapes, compiler_params=_cp())(*args)
    res = []
    for q in range(3):
        res.append({n: outs[q * k + j].reshape(w[n].shape) for j, n in enumerate(_SMALL)})
    return res


def kernel(x, positions, norm_w, w_in, q_norm_w, k_norm_w, sinks, a_re, a_im, log_step, b_re, b_im, c_re, c_im, d_skip, w_glu, b_glu, attn_out_norm_w, ssm_out_norm_w, w_out, loss_target, m_norm_w, m_w_in, m_q_norm_w, m_k_norm_w, m_sinks, m_a_re, m_a_im, m_log_step, m_b_re, m_b_im, m_c_re, m_c_im, m_d_skip, m_w_glu, m_b_glu, m_attn_out_norm_w, m_ssm_out_norm_w, m_w_out, v_norm_w, v_w_in, v_q_norm_w, v_k_norm_w, v_sinks, v_a_re, v_a_im, v_log_step, v_b_re, v_b_im, v_c_re, v_c_im, v_d_skip, v_w_glu, v_b_glu, v_attn_out_norm_w, v_ssm_out_norm_w, v_w_out):
    w = dict(norm_w=norm_w, w_in=w_in, q_norm_w=q_norm_w, k_norm_w=k_norm_w, sinks=sinks, a_re=a_re, a_im=a_im,
             log_step=log_step, b_re=b_re, b_im=b_im, c_re=c_re, c_im=c_im, d_skip=d_skip, w_glu=w_glu, b_glu=b_glu,
             attn_out_norm_w=attn_out_norm_w, ssm_out_norm_w=ssm_out_norm_w, w_out=w_out)
    m = dict(norm_w=m_norm_w, w_in=m_w_in, q_norm_w=m_q_norm_w, k_norm_w=m_k_norm_w, sinks=m_sinks, a_re=m_a_re,
             a_im=m_a_im, log_step=m_log_step, b_re=m_b_re, b_im=m_b_im, c_re=m_c_re, c_im=m_c_im, d_skip=m_d_skip,
             w_glu=m_w_glu, b_glu=m_b_glu, attn_out_norm_w=m_attn_out_norm_w, ssm_out_norm_w=m_ssm_out_norm_w,
             w_out=m_w_out)
    v = dict(norm_w=v_norm_w, w_in=v_w_in, q_norm_w=v_q_norm_w, k_norm_w=v_k_norm_w, sinks=v_sinks, a_re=v_a_re,
             a_im=v_a_im, log_step=v_log_step, b_re=v_b_re, b_im=v_b_im, c_re=v_c_re, c_im=v_c_im, d_skip=v_d_skip,
             w_glu=v_w_glu, b_glu=v_b_glu, attn_out_norm_w=v_attn_out_norm_w, ssm_out_norm_w=v_ssm_out_norm_w,
             w_out=v_w_out)
    core = lax.axis_index("c").astype(jnp.int32).reshape(1)
    chip = (2 * lax.axis_index("x") + lax.axis_index("y")).astype(jnp.int32).reshape(1)

    wt_in, wf_glu, wf_out = _all_gather_rows(
        [w_in.T.astype(BF16), w_glu.astype(BF16), w_out.astype(BF16)], "gather_weights")
    wt_in = wt_in.reshape(IN_W, D_MODEL)
    wf_glu = wf_glu.reshape(SSM_W, SSM_W)
    wf_out = wf_out.reshape(D_MODEL, D_MODEL)

    small = {n: w[n] for n in _SMALL}
    loss, grad_x, g_wt_in, g_w_glu, g_w_out, g_small = _local_step(
        x[0], positions[0], loss_target[0], small, wt_in, wf_glu, wf_out)
    loss = lax.psum(loss, ("x", "y", "c"))

    full = [g_wt_in.reshape(N_DEV, IN_W // N_DEV, D_MODEL), g_w_glu.reshape(N_DEV, SSM_W // N_DEV, SSM_W),
            g_w_out.reshape(N_DEV, D_MODEL // N_DEV, D_MODEL), _pack(g_small).reshape(N_DEV, _PACK_ROWS // N_DEV, 128)]
    from_sibling = _pair_exchange(full, "pair_exchange")
    wire = (BF16, BF16, BF16, F32)
    parts = [_pair_sum(g, ra, core, dt, f"pair_sum_{k}") for k, (g, ra, dt) in enumerate(zip(full, from_sibling, wire))]
    from_chips = _chip_exchange(parts, "chip_exchange")
    red = [_chip_sum(p, rb, chip, f"chip_sum_{k}") for k, (p, rb) in enumerate(zip(parts, from_chips))]
    g_in, g_glu, g_out = red[0].T, red[1], red[2]
    (g_packed,) = _all_gather_rows([red[3]], "gather_small")
    g_packed = g_packed.reshape(_PACK_ROWS, 128)

    grads = _unpack(g_packed, w)
    grads.update(w_in=g_in, w_glu=g_glu, w_out=g_out)
    delta, new_m, new_v = {}, {}, {}
    for n in ("w_in", "w_glu", "w_out"):
        delta[n], new_m[n], new_v[n] = _adamw(grads[n], w[n], m[n], v[n], f"adamw_{n}")
    d_s, m_s, v_s = _adamw_small(grads, w, m, v)
    delta.update(d_s)
    new_m.update(m_s)
    new_v.update(v_s)

    return (loss, grad_x[None], *[grads[n] for n in _WEIGHTS], *[delta[n] for n in _WEIGHTS],
            *[new_m[n] for n in _WEIGHTS], *[new_v[n] for n in _WEIGHTS])
```

```python
import functools
import math

import jax
import jax.numpy as jnp
from jax import lax
from jax.experimental import pallas as pl
from jax.experimental.pallas import tpu as pltpu

F32 = jnp.float32
BF16 = jnp.bfloat16

D_MODEL = 2048
ATTN_W = 1024
KV_W = 256
SSM_W = 1024
HEAD_DIM = 64
N_HEADS = 16
N_KV = 4
KV_REP = 4
IN_W = 4608
BLOCK = 128
ROPE_THETA = 10000.0
NORM_EPS = 1e-6
SSM_G = 64
SSM_P = 64
SSM_H = 16
CHUNK = 16
CW = CHUNK * SSM_H
N_DEV = 8

ADAM_LR = 0.001
ADAM_B1 = 0.9
ADAM_B2 = 0.999
ADAM_EPS = 1e-08
ADAM_WD = 0.01
ADAM_STEP = 10

VMEM_LIMIT = 56 * 1024 * 1024
MESH = pl.DeviceIdType.MESH


def _cp(sem=None):
    if sem is None:
        return pltpu.CompilerParams(vmem_limit_bytes=VMEM_LIMIT)
    return pltpu.CompilerParams(vmem_limit_bytes=VMEM_LIMIT, dimension_semantics=sem)


def _sigmoid(x):
    return 1.0 / (1.0 + jnp.exp(-x))


def _silu(x):
    return x * _sigmoid(x)


def _dsilu(x):
    s = _sigmoid(x)
    return s * (1.0 + x * (1.0 - s))


_GELU_C = math.sqrt(2.0 / math.pi)


def _gelu(y):
    t = jnp.tanh(_GELU_C * (y + 0.044715 * y * y * y))
    return 0.5 * y * (1.0 + t)


def _dgelu(y):
    t = jnp.tanh(_GELU_C * (y + 0.044715 * y * y * y))
    return 0.5 * (1.0 + t) + 0.5 * y * (1.0 - t * t) * _GELU_C * (1.0 + 3.0 * 0.044715 * y * y)


def _tile(n, want):
    if n <= want:
        return n
    for t in range(want - want % 16, 0, -16):
        if n % t == 0:
            return t
    raise ValueError((n, want))


def _mm(a, b, mode, out_dtype, name, tm=512, tn=1024, add=None, ride=None):
    if mode == "nn":
        (M, K), (K2, N) = a.shape, b.shape
    elif mode == "nt":
        (M, K), (N, K2) = a.shape, b.shape
    else:
        (K, M), (K2, N) = a.shape, b.shape
    assert K == K2
    tm, tn = _tile(M, tm), _tile(N, tn)
    dn = {"nn": _NN, "nt": _NT, "tn": _TN}[mode]

    def body(a_ref, b_ref, *rest):
        o_ref = rest[-1]
        acc = lax.dot_general(a_ref[...].astype(BF16), b_ref[...].astype(BF16), dn, preferred_element_type=F32)
        if add is not None:
            acc = acc + rest[0][...]
        o_ref[...] = acc.astype(o_ref.dtype)

    a_spec = pl.BlockSpec((K, tm), lambda j, i: (0, i)) if mode == "tn" else pl.BlockSpec((tm, K), lambda j, i: (i, 0))
    b_spec = pl.BlockSpec((tn, K), lambda j, i: (j, 0)) if mode == "nt" else pl.BlockSpec((K, tn), lambda j, i: (0, j))
    o_spec = pl.BlockSpec((tm, tn), lambda j, i: (i, j))
    extra = () if add is None else (add,)
    if ride is not None:
        (out,), landed = _call(body, name, (N // tn, M // tm), [a_spec, b_spec] + [o_spec] * len(extra), [o_spec],
                               [jax.ShapeDtypeStruct((M, N), out_dtype)], (a, b, *extra), ride=ride)
        return out, landed
    return pl.pallas_call(
        body,
        name=name,
        grid=(N // tn, M // tm),
        in_specs=[a_spec, b_spec] + [o_spec] * len(extra),
        out_specs=o_spec,
        out_shape=jax.ShapeDtypeStruct((M, N), out_dtype),
        compiler_params=_cp(("parallel", "parallel")),
    )(a, b, *extra)


def _rms_inproj(x, norm_w, wt_in, ride):
    L = x.shape[0]
    tm, tn = _tile(L, 1024), 768
    nj = IN_W // tn

    def body(x_ref, w_ref, wt_ref, proj_ref, hn_ref, hn_scr):
        j = pl.program_id(1)

        @pl.when(j == 0)
        def _():
            xv = x_ref[...]
            r = lax.rsqrt(jnp.mean(xv * xv, axis=-1, keepdims=True) + NORM_EPS)
            hn = (xv * r * w_ref[...]).astype(BF16)
            hn_scr[...] = hn
            hn_ref[...] = hn

        proj_ref[...] = lax.dot_general(hn_scr[...], wt_ref[...], (((1,), (1,)), ((), ())),
                                        preferred_element_type=F32)

    return _call(
        body, "rms_inproj", (L // tm, nj),
        [pl.BlockSpec((tm, D_MODEL), lambda i, j: (i, 0)),
         pl.BlockSpec((1, D_MODEL), lambda i, j: (0, 0)),
         pl.BlockSpec((tn, D_MODEL), lambda i, j: (j, 0))],
        [pl.BlockSpec((tm, tn), lambda i, j: (i, j)),
         pl.BlockSpec((tm, D_MODEL), lambda i, j: (i, 0))],
        [jax.ShapeDtypeStruct((L, IN_W), F32), jax.ShapeDtypeStruct((L, D_MODEL), BF16)],
        (x, norm_w.reshape(1, D_MODEL), wt_in), [pltpu.VMEM((tm, D_MODEL), BF16)], ride)


def _seg_sum(v):
    a = lax.broadcasted_iota(jnp.int32, (128, 128), 0) // HEAD_DIM
    b = lax.broadcasted_iota(jnp.int32, (128, 128), 1) // HEAD_DIM
    ones = jnp.where(a == b, 1.0, 0.0).astype(BF16)
    hi = v.astype(BF16)
    lo = (v - hi.astype(F32)).astype(BF16)
    return jnp.dot(hi, ones, preferred_element_type=F32) + jnp.dot(lo, ones, preferred_element_type=F32)


def _rot_half(t):
    lane = lax.broadcasted_iota(jnp.int32, t.shape, 1)
    return jnp.where(lane % HEAD_DIM < HEAD_DIM // 2, pltpu.roll(t, 128 - HEAD_DIM // 2, 1),
                     pltpu.roll(t, HEAD_DIM // 2, 1))


def _norm_rope(raw, w, cos, sin):
    r = lax.rsqrt(_seg_sum(raw * raw) * (1.0 / HEAD_DIM) + NORM_EPS)
    tn = raw * r * w
    return r, tn * cos + _rot_half(tn) * sin


def _norm_rope_bwd(d_rot, raw, w, cos, sin):
    r = lax.rsqrt(_seg_sum(raw * raw) * (1.0 / HEAD_DIM) + NORM_EPS)
    d_tn = d_rot * cos + _rot_half(d_rot * sin)
    xh = raw * r
    gw = d_tn * w
    d_raw = r * (gw - xh * (_seg_sum(gw * xh) * (1.0 / HEAD_DIM)))
    return d_raw, d_tn * xh


def _band_mask2(has_prev):
    qi = lax.broadcasted_iota(jnp.int32, (2 * BLOCK, 2 * BLOCK), 0) % BLOCK + BLOCK
    kj = lax.broadcasted_iota(jnp.int32, (2 * BLOCK, 2 * BLOCK), 1)
    rel = qi - kj
    return (rel >= 0) & (rel < BLOCK) & ((kj >= BLOCK) | has_prev)


def _half_tiles(pair):
    lo = lax.broadcasted_iota(jnp.int32, pair.shape, 1) < HEAD_DIM
    sw = pltpu.roll(pair, HEAD_DIM, 1)
    z = jnp.zeros_like(pair)
    return (jnp.where(lo, pair, z).astype(BF16), jnp.where(lo, z, sw).astype(BF16),
            jnp.where(lo, sw, z).astype(BF16), jnp.where(lo, z, pair).astype(BF16))


def _two_rows(top, bottom):
    row = lax.broadcasted_iota(jnp.int32, (2 * BLOCK, 1), 0)
    return jnp.where(row < BLOCK, top, bottom)


def _lane_col(mat, h):
    lane = lax.broadcasted_iota(jnp.int32, mat.shape, 1)
    return jnp.sum(jnp.where(lane == h, mat, 0.0), axis=1, keepdims=True)


_SCALE = 1.0 / math.sqrt(HEAD_DIM)
_NT = (((1,), (1,)), ((), ()))
_NN = (((1,), (0,)), ((), ()))
_TN = (((0,), (0,)), ((), ()))


def _qk_prep(proj, tab, qw, kw):
    L = proj.shape[0]
    tm = _tile(L, 512)

    def body(q_ref, k_ref, t_ref, qw_ref, kw_ref, qo_ref, ko_ref):
        cos, sin = t_ref[:, :128], t_ref[:, 128:]
        for c in range(ATTN_W // 128):
            _, qr = _norm_rope(q_ref[:, c * 128:(c + 1) * 128], qw_ref[...], cos, sin)
            qo_ref[:, c * 128:(c + 1) * 128] = (qr * _SCALE).astype(BF16)
        for c in range(KV_W // 128):
            _, kr = _norm_rope(k_ref[:, c * 128:(c + 1) * 128], kw_ref[...], cos, sin)
            ko_ref[:, c * 128:(c + 1) * 128] = kr.astype(BF16)

    row = pl.BlockSpec((1, 128), lambda i: (0, 0))
    return pl.pallas_call(
        body,
        name="qk_prep",
        grid=(L // tm,),
        in_specs=[pl.BlockSpec((tm, ATTN_W), lambda i: (i, 0)), pl.BlockSpec((tm, KV_W), lambda i: (i, 4)),
                  pl.BlockSpec((tm, 256), lambda i: (i, 0)), row, row],
        out_specs=[pl.BlockSpec((tm, ATTN_W), lambda i: (i, 0)), pl.BlockSpec((tm, KV_W), lambda i: (i, 0))],
        out_shape=[jax.ShapeDtypeStruct((L, ATTN_W), BF16), jax.ShapeDtypeStruct((L, KV_W), BF16)],
        compiler_params=_cp(("parallel",)),
    )(proj, proj, tab, jnp.tile(qw, 2).reshape(1, 128), jnp.tile(kw, 2).reshape(1, 128))


def _group_tiles(g, kt, vt):
    a, b = divmod(g, 2)
    return kt[a][2 * b], kt[a][2 * b + 1], vt[a][2 * b], vt[a][2 * b + 1]


def _attn_fwd(q, k, proj, sinks):
    L = proj.shape[0]
    nb = L // BLOCK

    def body(q_ref, kc_ref, kp_ref, vc_ref, vp_ref, z0_ref, z1_ref, sink_ref, og_ref, o_ref, lse_ref):
        i = pl.program_id(0)
        mask = _band_mask2(i > 0)
        z = jnp.concatenate([z0_ref[...], z1_ref[...]], axis=1)
        lane = lax.broadcasted_iota(jnp.int32, (BLOCK, 128), 1)
        kt = [_half_tiles(jnp.concatenate([kp_ref[:, a * 128:(a + 1) * 128], kc_ref[:, a * 128:(a + 1) * 128]],
                                          axis=0).astype(F32)) for a in range(2)]
        vt = [_half_tiles(jnp.concatenate([vp_ref[:, a * 128:(a + 1) * 128], vc_ref[:, a * 128:(a + 1) * 128]],
                                          axis=0)) for a in range(2)]
        lse_mat = jnp.zeros((BLOCK, 128), F32)
        outs = []
        for g in range(N_KV):
            k_lo, k_hi, v_lo, v_hi = _group_tiles(g, kt, vt)
            q2 = jnp.concatenate([q_ref[:, 2 * g * 128:(2 * g + 1) * 128],
                                  q_ref[:, (2 * g + 1) * 128:(2 * g + 2) * 128]], axis=0)
            acc = jnp.zeros((2 * BLOCK, 128), F32)
            for half, (kh, vh) in enumerate(((k_lo, v_lo), (k_hi, v_hi))):
                h_top, h_bot = 4 * g + half, 4 * g + 2 + half
                s = jnp.where(mask, lax.dot_general(q2, kh, _NT, preferred_element_type=F32), -1e30)
                sink = _two_rows(sink_ref[h_top], sink_ref[h_bot])
                m = jnp.maximum(jnp.max(s, axis=-1, keepdims=True), sink)
                e = jnp.exp(s - m)
                den = jnp.sum(e, axis=-1, keepdims=True) + jnp.exp(sink - m)
                p = e / den
                acc = acc + jnp.dot(p.astype(BF16), vh, preferred_element_type=F32)
                lse = m + jnp.log(den)
                lse_mat = jnp.where(lane == h_top, lse[:BLOCK], lse_mat)
                lse_mat = jnp.where(lane == h_bot, lse[BLOCK:], lse_mat)
            outs += [acc[:BLOCK], acc[BLOCK:]]
        o = jnp.concatenate(outs, axis=1)
        o_ref[...] = o
        og_ref[...] = o * _silu(z)
        lse_ref[...] = lse_mat

    prev = lambda i: jnp.maximum(i - 1, 0)
    return pl.pallas_call(
        body,
        name="attn_fwd",
        grid=(nb,),
        in_specs=[pl.BlockSpec((BLOCK, ATTN_W), lambda i: (i, 0)),
                  pl.BlockSpec((BLOCK, KV_W), lambda i: (i, 0)),
                  pl.BlockSpec((BLOCK, KV_W), lambda i: (prev(i), 0)),
                  pl.BlockSpec((BLOCK, KV_W), lambda i: (i, 5)),
                  pl.BlockSpec((BLOCK, KV_W), lambda i: (prev(i), 5)),
                  pl.BlockSpec((BLOCK, 512), lambda i: (i, 3)),
                  pl.BlockSpec((BLOCK, 512), lambda i: (i, 4)),
                  pl.BlockSpec(memory_space=pltpu.SMEM)],
        out_specs=[pl.BlockSpec((BLOCK, ATTN_W), lambda i: (i, 0)),
                   pl.BlockSpec((BLOCK, ATTN_W), lambda i: (i, 0)),
                   pl.BlockSpec((BLOCK, 128), lambda i: (i, 0))],
        out_shape=[jax.ShapeDtypeStruct((L, ATTN_W), F32), jax.ShapeDtypeStruct((L, ATTN_W), F32),
                   jax.ShapeDtypeStruct((L, 128), F32)],
        compiler_params=_cp(("parallel",)),
    )(q, k, k, proj, proj, proj, proj, sinks)


def _attn_bwd(q, k, proj, sinks, d_o, o, lse, ride):
    L = proj.shape[0]
    nb = L // BLOCK

    def body(q_ref, kc_ref, kp_ref, vc_ref, vp_ref, do_ref, o_ref, lse_ref, sink_ref,
             dq_ref, dk_ref, dv_ref, gs_ref, ck_scr, cv_scr):
        i = pl.program_id(0)

        @pl.when(i == 0)
        def _():
            gs_ref[...] = jnp.zeros_like(gs_ref)
            ck_scr[...] = jnp.zeros_like(ck_scr)
            cv_scr[...] = jnp.zeros_like(cv_scr)

        @pl.when(i == nb)
        def _():
            dk_ref[...] = ck_scr[...]
            dv_ref[...] = cv_scr[...]

        @pl.when(i < nb)
        def _():
            mask = _band_mask2(i > 0)
            lane = lax.broadcasted_iota(jnp.int32, (1, 128), 1)
            lo = lax.broadcasted_iota(jnp.int32, (2 * BLOCK, 128), 1) < HEAD_DIM
            lse_c = lse_ref[...]
            kt = [_half_tiles(jnp.concatenate([kp_ref[:, a * 128:(a + 1) * 128], kc_ref[:, a * 128:(a + 1) * 128]],
                                              axis=0).astype(F32)) for a in range(2)]
            vt = [_half_tiles(jnp.concatenate([vp_ref[:, a * 128:(a + 1) * 128], vc_ref[:, a * 128:(a + 1) * 128]],
                                              axis=0)) for a in range(2)]
            gs = jnp.zeros((1, 128), F32)
            dq_parts = []
            dk_acc = [jnp.zeros((2 * BLOCK, 128), F32) for _ in range(2)]
            dv_acc = [jnp.zeros((2 * BLOCK, 128), F32) for _ in range(2)]
            for g in range(N_KV):
                a, b = divmod(g, 2)
                k_lo, k_hi, v_lo, v_hi = _group_tiles(g, kt, vt)
                t0, t1 = slice(2 * g * 128, (2 * g + 1) * 128), slice((2 * g + 1) * 128, (2 * g + 2) * 128)
                q2 = jnp.concatenate([q_ref[:, t0], q_ref[:, t1]], axis=0)
                do2 = jnp.concatenate([do_ref[:, t0], do_ref[:, t1]], axis=0)
                prod = do2 * jnp.concatenate([o_ref[:, t0], o_ref[:, t1]], axis=0)
                do2_b = do2.astype(BF16)
                dq2 = jnp.zeros((2 * BLOCK, 128), F32)
                dk_h, dv_h = [], []
                for half, (kh, vh) in enumerate(((k_lo, v_lo), (k_hi, v_hi))):
                    h_top, h_bot = 4 * g + half, 4 * g + 2 + half
                    lse = jnp.concatenate([_lane_col(lse_c, h_top), _lane_col(lse_c, h_bot)], axis=0)
                    sink = _two_rows(sink_ref[h_top], sink_ref[h_bot])
                    delta = jnp.sum(jnp.where(lo == (half == 0), prod, 0.0), axis=1, keepdims=True)
                    s = jnp.where(mask, lax.dot_general(q2, kh, _NT, preferred_element_type=F32), -1e30)
                    p = jnp.exp(s - lse)
                    dp = lax.dot_general(do2_b, vh, _NT, preferred_element_type=F32)
                    ds_b = (p * (dp - delta)).astype(BF16)
                    p_b = p.astype(BF16)
                    dq2 = dq2 + jnp.dot(ds_b, kh, preferred_element_type=F32)
                    dk_h.append(lax.dot_general(ds_b, q2, _TN, preferred_element_type=F32))
                    dv_h.append(lax.dot_general(p_b, do2_b, _TN, preferred_element_type=F32))
                    gsink = -jnp.exp(sink - lse) * delta
                    row = lax.broadcasted_iota(jnp.int32, (2 * BLOCK, 1), 0)
                    gs = gs + jnp.where(lane == h_top, jnp.sum(jnp.where(row < BLOCK, gsink, 0.0)), 0.0)
                    gs = gs + jnp.where(lane == h_bot, jnp.sum(jnp.where(row >= BLOCK, gsink, 0.0)), 0.0)
                dq_parts += [dq2[:BLOCK], dq2[BLOCK:]]
                for acc, parts in ((dk_acc, dk_h), (dv_acc, dv_h)):
                    t = jnp.where(lo, parts[0], parts[1])
                    t = t + pltpu.roll(t, HEAD_DIM, 1)
                    acc[a] = acc[a] + jnp.where(lo == (b == 0), t, 0.0)
            dq_ref[...] = jnp.concatenate(dq_parts, axis=1)
            dk_full = jnp.concatenate(dk_acc, axis=1)
            dv_full = jnp.concatenate(dv_acc, axis=1)
            dk_ref[...] = ck_scr[...] + dk_full[:BLOCK]
            dv_ref[...] = cv_scr[...] + dv_full[:BLOCK]
            ck_scr[...] = dk_full[BLOCK:]
            cv_scr[...] = dv_full[BLOCK:]
            gs_ref[...] += gs

    cur = lambda i: jnp.minimum(i, nb - 1)
    prev = lambda i: jnp.maximum(jnp.minimum(i, nb - 1) - 1, 0)
    done = lambda i: jnp.maximum(i - 1, 0)
    bs = pl.BlockSpec
    return _call(
        body, "attn_bwd", (nb + 1,),
        [bs((BLOCK, ATTN_W), lambda i: (cur(i), 0)),
         bs((BLOCK, KV_W), lambda i: (cur(i), 0)), bs((BLOCK, KV_W), lambda i: (prev(i), 0)),
         bs((BLOCK, KV_W), lambda i: (cur(i), 5)), bs((BLOCK, KV_W), lambda i: (prev(i), 5)),
         bs((BLOCK, ATTN_W), lambda i: (cur(i), 0)), bs((BLOCK, ATTN_W), lambda i: (cur(i), 0)),
         bs((BLOCK, 128), lambda i: (cur(i), 0)), bs(memory_space=pltpu.SMEM)],
        [bs((BLOCK, ATTN_W), lambda i: (cur(i), 0)),
         bs((BLOCK, KV_W), lambda i: (done(i), 0)), bs((BLOCK, KV_W), lambda i: (done(i), 0)),
         bs((1, 128), lambda i: (0, 0))],
        [jax.ShapeDtypeStruct((L, ATTN_W), F32), jax.ShapeDtypeStruct((L, KV_W), F32),
         jax.ShapeDtypeStruct((L, KV_W), F32), jax.ShapeDtypeStruct((1, 128), F32)],
        (q, k, k, proj, proj, d_o, o, lse, sinks),
        [pltpu.VMEM((BLOCK, KV_W), F32), pltpu.VMEM((BLOCK, KV_W), F32)], ride)


def _qk_prep_bwd(proj, tab, qw, kw, d_q, d_k, d_v, d_za, d_u, d_zs):
    L = proj.shape[0]
    tm = _tile(L, 512)
    z0 = ATTN_W + 2 * KV_W

    def body(q_ref, k_ref, t_ref, qw_ref, kw_ref, dq_ref, dk_ref, dv_ref, dza_ref, du_ref, dzs_ref,
             out_ref, gq_ref, gk_ref):
        i = pl.program_id(0)

        @pl.when(i == 0)
        def _():
            gq_ref[...] = jnp.zeros_like(gq_ref)
            gk_ref[...] = jnp.zeros_like(gk_ref)

        cos, sin = t_ref[:, :128], t_ref[:, 128:]
        gq = jnp.zeros((1, 128), F32)
        gk = jnp.zeros((1, 128), F32)
        for c in range(ATTN_W // 128):
            cs = slice(c * 128, (c + 1) * 128)
            d_raw, gw = _norm_rope_bwd(dq_ref[:, cs] * _SCALE, q_ref[:, cs], qw_ref[...], cos, sin)
            out_ref[:, cs] = d_raw.astype(BF16)
            gq = gq + jnp.sum(gw, axis=0, keepdims=True)
        for c in range(KV_W // 128):
            cs = slice(c * 128, (c + 1) * 128)
            d_raw, gw = _norm_rope_bwd(dk_ref[:, cs], k_ref[:, cs], kw_ref[...], cos, sin)
            out_ref[:, ATTN_W + c * 128:ATTN_W + (c + 1) * 128] = d_raw.astype(BF16)
            gk = gk + jnp.sum(gw, axis=0, keepdims=True)
        out_ref[:, ATTN_W + KV_W:z0] = dv_ref[...].astype(BF16)
        out_ref[:, z0:z0 + ATTN_W] = dza_ref[...]
        out_ref[:, z0 + ATTN_W:z0 + ATTN_W + SSM_W] = du_ref[...].astype(BF16)
        out_ref[:, z0 + ATTN_W + SSM_W:] = dzs_ref[...]
        gq_ref[...] += gq
        gk_ref[...] += gk

    row = pl.BlockSpec((1, 128), lambda i: (0, 0))
    blk = lambda w, c: pl.BlockSpec((tm, w), lambda i: (i, c))
    return pl.pallas_call(
        body,
        name="qk_prep_bwd",
        grid=(L // tm,),
        in_specs=[blk(ATTN_W, 0), blk(KV_W, 4), blk(256, 0), row, row, blk(ATTN_W, 0), blk(KV_W, 0), blk(KV_W, 0),
                  blk(ATTN_W, 0), blk(SSM_W, 0), blk(SSM_W, 0)],
        out_specs=[blk(IN_W, 0), row, row],
        out_shape=[jax.ShapeDtypeStruct((L, IN_W), BF16), jax.ShapeDtypeStruct((1, 128), F32),
                   jax.ShapeDtypeStruct((1, 128), F32)],
        compiler_params=_cp(("arbitrary",)),
    )(proj, proj, tab, jnp.tile(qw, 2).reshape(1, 128), jnp.tile(kw, 2).reshape(1, 128), d_q, d_k, d_v,
      d_za, d_u, d_zs)


def _cmul(a, b):
    return a[0] * b[0] - a[1] * b[1], a[0] * b[1] + a[1] * b[0]


def _cmul_conj(a, b):
    return a[0] * b[0] + a[1] * b[1], a[1] * b[0] - a[0] * b[1]


def _cadd(a, b):
    return a[0] + b[0], a[1] + b[1]


def _dot3(a, b, dn):
    ah, bh = a.astype(BF16), b.astype(BF16)
    al, bl = (a - ah.astype(F32)).astype(BF16), (b - bh.astype(F32)).astype(BF16)
    d = lambda u, v: lax.dot_general(u, v, dn, preferred_element_type=F32)
    return d(ah, bh) + d(ah, bl) + d(al, bh)


def _s5_discretise(a_re, a_im, ls, cosx, sinx, bt):
    delta = jnp.exp(ls)
    er = jnp.exp(a_re * delta)
    lb = (er * cosx, er * sinx)
    den = a_re * a_re + a_im * a_im
    coef = _cmul_conj((lb[0] - 1.0, lb[1]), (a_re, a_im))
    coef = (coef[0] / den, coef[1] / den)
    return delta, lb, coef, den, _cmul(coef, bt)


def _powers(lb):
    pw = [(jnp.ones_like(lb[0]), jnp.zeros_like(lb[0]))]
    for _ in range(CHUNK):
        pw.append(_cmul(pw[-1], lb))
    return pw


def _block_rows(a, pw, idx):
    blocks = [_cmul(a, pw[i]) for i in idx]
    return (jnp.concatenate([b[0] for b in blocks], axis=0), jnp.concatenate([b[1] for b in blocks], axis=0))


def _block_rows_bwd(g, a, pw, idx, g_pw):
    g_a = (jnp.zeros_like(a[0]), jnp.zeros_like(a[0]))
    for j, i in enumerate(idx):
        gj = (g[0][j * SSM_H:(j + 1) * SSM_H], g[1][j * SSM_H:(j + 1) * SSM_H])
        g_a = _cadd(g_a, _cmul_conj(gj, pw[i]))
        gp = _cmul_conj(gj, a)
        g_pw[i] = _cadd(g_pw[i], (jnp.sum(gp[0], axis=0, keepdims=True), jnp.sum(gp[1], axis=0, keepdims=True)))
    return g_a


_IDX_S = [CHUNK - 1 - s for s in range(CHUNK)]
_IDX_O = [t + 1 for t in range(CHUNK)]
_IDX_K = list(range(CHUNK))
_PREP_IN = 9


def _prep_args(p):
    row = lambda t: t.reshape(SSM_G, 1, SSM_P)
    xi = p["a_im"] * jnp.exp(p["log_step"])[:, None]
    return (row(p["a_re"]), row(p["a_im"]), row(jnp.broadcast_to(p["log_step"][:, None], (SSM_G, SSM_P))),
            row(jnp.cos(xi)), row(jnp.sin(xi)), p["b_re"].transpose(0, 2, 1), p["b_im"].transpose(0, 2, 1),
            p["c_re"], p["c_im"])


PREP_GROUPS = 4


def _prep_specs():
    r1 = pl.BlockSpec((PREP_GROUPS, 1, SSM_P), lambda g: (g, 0, 0))
    r16 = pl.BlockSpec((PREP_GROUPS, SSM_H, SSM_P), lambda g: (g, 0, 0))
    return [r1] * 5 + [r16] * 4, r1, r16


def _ssm_prep(p):
    def one_group(q, are, aim, ls, cosx, sinx, btr, bti, cre, cim, mt_ref, s_ref, o_ref, a_ref):
        _, lb, _, _, bb = _s5_discretise(are[q], aim[q], ls[q], cosx[q], sinx[q], (btr[q], bti[q]))
        pw = _powers(lb)
        c = (cre[q], cim[q])
        sc = _block_rows(bb, pw, _IDX_S)
        ot = _block_rows(c, pw, _IDX_O)
        ok = _block_rows(c, pw, _IDX_K)
        s_ref[q] = jnp.concatenate([sc[0], sc[1]], axis=1).astype(BF16)
        o_ref[q] = jnp.concatenate([ot[0], -ot[1]], axis=1).astype(BF16)
        a_ref[q] = jnp.concatenate([pw[CHUNK][0], pw[CHUNK][1]], axis=1)
        kt = _dot3(jnp.concatenate([bb[0], -bb[1]], axis=1), jnp.concatenate([ok[0], ok[1]], axis=1), _NT)
        lane = lax.broadcasted_iota(jnp.int32, kt.shape, 1)
        for s in range(CHUNK):
            blk = kt if s == 0 else jnp.where(lane >= SSM_H * s, pltpu.roll(kt, SSM_H * s, 1), 0.0)
            mt_ref[q, s * SSM_H:(s + 1) * SSM_H, :] = blk.astype(BF16)

    def body(*refs):
        for q in range(PREP_GROUPS):
            one_group(q, *refs)

    in_specs, r1, _ = _prep_specs()
    g3 = lambda r, c: pl.BlockSpec((PREP_GROUPS, r, c), lambda g: (g, 0, 0))
    return pl.pallas_call(
        body,
        name="ssm_prep",
        grid=(SSM_G // PREP_GROUPS,),
        in_specs=in_specs,
        out_specs=[g3(CW, CW), g3(CW, 2 * SSM_P), g3(CW, 2 * SSM_P), g3(1, 2 * SSM_P)],
        out_shape=[jax.ShapeDtypeStruct((SSM_G, CW, CW), BF16), jax.ShapeDtypeStruct((SSM_G, CW, 2 * SSM_P), BF16),
                   jax.ShapeDtypeStruct((SSM_G, CW, 2 * SSM_P), BF16),
                   jax.ShapeDtypeStruct((SSM_G, 1, 2 * SSM_P), F32)],
        compiler_params=_cp(("parallel",)),
    )(*_prep_args(p))


def _ssm_prep_bwd(p, g_mt, g_scat, g_ocat, g_a16):
    def one_group(q, are, aim, ls, cosx, sinx, btr, bti, cre, cim, gmt_ref, gs_ref, go_ref, ga_ref,
                  g_are, g_aim, g_ls, g_btr, g_bti, g_cre, g_cim):
        lam = (are[q], aim[q])
        bt = (btr[q], bti[q])
        delta, lb, coef, den, bb = _s5_discretise(lam[0], lam[1], ls[q], cosx[q], sinx[q], bt)
        pw = _powers(lb)
        c = (cre[q], cim[q])
        ok = _block_rows(c, pw, _IDX_K)
        g_pw = [(jnp.zeros_like(lb[0]), jnp.zeros_like(lb[0])) for _ in range(CHUNK + 1)]
        lane = lax.broadcasted_iota(jnp.int32, (SSM_H, CW), 1)
        g_kt = gmt_ref[q, :SSM_H, :]
        for s in range(1, CHUNK):
            blk = gmt_ref[q, s * SSM_H:(s + 1) * SSM_H, :]
            g_kt = g_kt + jnp.where(lane < CW - SSM_H * s, pltpu.roll(blk, CW - SSM_H * s, 1), 0.0)
        a1 = jnp.concatenate([bb[0], -bb[1]], axis=1)
        b1 = jnp.concatenate([ok[0], ok[1]], axis=1)
        g_a1 = _dot3(g_kt, b1, _NN)
        g_b1 = _dot3(g_kt, a1, _TN)
        g_bb = (g_a1[:, :SSM_P], -g_a1[:, SSM_P:])
        g_c = _block_rows_bwd((g_b1[:, :SSM_P], g_b1[:, SSM_P:]), c, pw, _IDX_K, g_pw)
        gs = gs_ref[q]
        g_bb = _cadd(g_bb, _block_rows_bwd((gs[:, :SSM_P], gs[:, SSM_P:]), bb, pw, _IDX_S, g_pw))
        go = go_ref[q]
        g_c = _cadd(g_c, _block_rows_bwd((go[:, :SSM_P], -go[:, SSM_P:]), c, pw, _IDX_O, g_pw))
        ga = ga_ref[q]
        g_pw[CHUNK] = _cadd(g_pw[CHUNK], (ga[:, :SSM_P], ga[:, SSM_P:]))
        g_lb = (jnp.zeros_like(lb[0]), jnp.zeros_like(lb[0]))
        for l in range(CHUNK - 1, -1, -1):
            g_lb = _cadd(g_lb, _cmul_conj(g_pw[l + 1], pw[l]))
            g_pw[l] = _cadd(g_pw[l], _cmul_conj(g_pw[l + 1], lb))
        g_bt = _cmul_conj(g_bb, coef)
        gc = _cmul_conj(g_bb, bt)
        g_coef = (jnp.sum(gc[0], axis=0, keepdims=True), jnp.sum(gc[1], axis=0, keepdims=True))
        lam_den = (lam[0] / den, lam[1] / den)
        g_lb = _cadd(g_lb, _cmul(g_coef, lam_den))
        t = _cmul(_cmul_conj(g_coef, coef), lam_den)
        g_x = _cmul_conj(g_lb, lb)
        g_lam = (g_x[0] * delta - t[0], g_x[1] * delta - t[1])
        g_are[q] = g_lam[0]
        g_aim[q] = g_lam[1]
        g_ls[q] = (g_x[0] * lam[0] + g_x[1] * lam[1]) * delta
        g_btr[q] = g_bt[0]
        g_bti[q] = g_bt[1]
        g_cre[q] = g_c[0]
        g_cim[q] = g_c[1]

    def body(*refs):
        for q in range(PREP_GROUPS):
            one_group(q, *refs)

    in_specs, r1, r16 = _prep_specs()
    g3 = lambda r, c: pl.BlockSpec((PREP_GROUPS, r, c), lambda g: (g, 0, 0))
    rows = jax.ShapeDtypeStruct((SSM_G, 1, SSM_P), F32)
    mats = jax.ShapeDtypeStruct((SSM_G, SSM_H, SSM_P), F32)
    g_are, g_aim, g_ls, g_btr, g_bti, g_cre, g_cim = pl.pallas_call(
        body,
        name="ssm_prep_bwd",
        grid=(SSM_G // PREP_GROUPS,),
        in_specs=in_specs + [g3(CW, CW), g3(CW, 2 * SSM_P), g3(CW, 2 * SSM_P), g3(1, 2 * SSM_P)],
        out_specs=[r1] * 3 + [r16] * 4,
        out_shape=[rows] * 3 + [mats] * 4,
        compiler_params=_cp(("parallel",)),
    )(*_prep_args(p), g_mt, g_scat, g_ocat, g_a16)
    return dict(a_re=g_are.reshape(SSM_G, SSM_P), a_im=g_aim.reshape(SSM_G, SSM_P),
                log_step=jnp.sum(g_ls.reshape(SSM_G, SSM_P), axis=1),
                b_re=g_btr.transpose(0, 2, 1), b_im=g_bti.transpose(0, 2, 1), c_re=g_cre, c_im=g_cim)


def _cmul_const(xv, ar, ai):
    return xv * ar + pltpu.roll(xv, SSM_P, 1) * ai


def _chunk_scan(inc, a_row, reverse):
    n = inc.shape[0]
    lane = lax.broadcasted_iota(jnp.int32, (1, 2 * SSM_P), 1)
    row = lax.broadcasted_iota(jnp.int32, inc.shape, 0)
    sign = jnp.where(lane < SSM_P, -1.0, 1.0)
    ar = jnp.where(lane < SSM_P, a_row, pltpu.roll(a_row, SSM_P, 1))
    ai = jnp.where(lane < SSM_P, pltpu.roll(a_row, SSM_P, 1), a_row)
    if reverse:
        ai = -ai
    xv = inc
    s = 1
    while s < n:
        if reverse:
            sh = jnp.where(row < n - s, pltpu.roll(xv, n - s, 0), 0.0)
        else:
            sh = jnp.where(row >= s, pltpu.roll(xv, s, 0), 0.0)
        xv = xv + _cmul_const(sh, ar, ai * sign)
        ar, ai = ar * ar - ai * ai, 2.0 * ar * ai
        s *= 2
    return xv


def _shift_rows(xv, reverse):
    n = xv.shape[0]
    row = lax.broadcasted_iota(jnp.int32, xv.shape, 0)
    if reverse:
        return jnp.where(row < n - 1, pltpu.roll(xv, n - 1, 0), 0.0)
    return jnp.where(row >= 1, pltpu.roll(xv, 1, 0), 0.0)


GB = 128 // SSM_H
U_COL0 = (ATTN_W + 2 * KV_W + ATTN_W) // 128


HALF = CHUNK // 2


def _chunk_perm():
    r = jnp.arange(HALF * 128)
    t, g8, h = r // 128, (r % 128) // SSM_H, r % SSM_H
    return ((g8 * 128 + t * SSM_H + h)[:, None] == jnp.arange(GB * 128)[None, :]).astype(BF16)


def _load_perm(p_hbm, p_scr, sem):
    @pl.when(pl.program_id(0) == 0)
    def _():
        cp = pltpu.make_async_copy(p_hbm, p_scr, sem)
        cp.start()
        cp.wait()


def _rows_to_chunks(pieces, perm):
    halves = [jnp.dot(jnp.concatenate(pieces[k * HALF:(k + 1) * HALF], axis=1).astype(BF16), perm,
                      preferred_element_type=F32).astype(BF16) for k in range(2)]
    return [jnp.concatenate([hv[:, g * 128:(g + 1) * 128] for hv in halves], axis=1) for g in range(GB)]


def _chunks_to_rows(groups, perm, two_pass):
    pieces = []
    for k in range(2):
        v = jnp.concatenate([gv[:, k * 128:(k + 1) * 128] for gv in groups], axis=1)
        hi = v.astype(BF16)
        out = lax.dot_general(hi, perm, _NT, preferred_element_type=F32)
        if two_pass:
            lo = (v - hi.astype(F32)).astype(BF16)
            out = out + lax.dot_general(lo, perm, _NT, preferred_element_type=F32)
        pieces += [out[:, t * 128:(t + 1) * 128] for t in range(HALF)]
    return pieces


def _ssm_fwd(proj, perm, mt, scat, ocat, a16, d_skip):
    L = proj.shape[0]
    nc = L // CHUNK

    def body(u_ref, p_hbm, mt_ref, s_ref, o_ref, a_ref, d_ref, y_ref, yg_ref, h_ref, p_scr, sem):
        _load_perm(p_hbm, p_scr, sem)
        perm = p_scr[...]
        rows = [pl.ds(t, nc, stride=CHUNK) for t in range(CHUNK)]
        ua = _rows_to_chunks([u_ref[r, :] for r in rows], perm)
        ys = []
        for g in range(GB):
            uv = ua[g]
            inc = jnp.dot(uv, s_ref[g], preferred_element_type=F32)
            hx = _shift_rows(_chunk_scan(inc, a_ref[g], False), False)
            h_ref[g] = hx
            ys.append(jnp.dot(uv, mt_ref[g], preferred_element_type=F32)
                      + lax.dot_general(hx.astype(BF16), o_ref[g], _NT, preferred_element_type=F32))
        yp = _chunks_to_rows(ys, perm, True)
        for t, r in enumerate(rows):
            y = yp[t] + d_ref[...] * u_ref[r, :]
            y_ref[r, :] = y
            yg_ref[r, :] = _gelu(y)

    g3 = lambda r, c: pl.BlockSpec((GB, r, c), lambda g: (g, 0, 0))
    col = pl.BlockSpec((L, 128), lambda g: (0, g))
    return pl.pallas_call(
        body,
        name="ssm_fwd",
        grid=(SSM_G // GB,),
        in_specs=[pl.BlockSpec((L, 128), lambda g: (0, U_COL0 + g)), _ANY,
                  g3(CW, CW), g3(CW, 2 * SSM_P), g3(CW, 2 * SSM_P), g3(1, 2 * SSM_P),
                  pl.BlockSpec((1, 128), lambda g: (0, g))],
        out_specs=[col, col, g3(nc, 2 * SSM_P)],
        out_shape=[jax.ShapeDtypeStruct((L, SSM_W), F32), jax.ShapeDtypeStruct((L, SSM_W), F32),
                   jax.ShapeDtypeStruct((SSM_G, nc, 2 * SSM_P), F32)],
        scratch_shapes=[pltpu.VMEM((HALF * 128, GB * 128), BF16), pltpu.SemaphoreType.DMA],
        compiler_params=_cp(("arbitrary",)),
    )(proj, perm, mt, scat, ocat, a16, d_skip.reshape(1, SSM_W))


def _ssm_bwd(d_yg, y, proj, hx, perm, mt, scat, ocat, a16, d_skip, ride):
    L = proj.shape[0]
    nc = L // CHUNK

    def body(dg_ref, y_ref, u_ref, h_ref, p_hbm, mt_ref, s_ref, o_ref, a_ref, d_ref,
             du_ref, gmt_ref, gs_ref, go_ref, ga_ref, gd_ref, p_scr, sem):
        _load_perm(p_hbm, p_scr, sem)
        perm = p_scr[...]
        rows = [pl.ds(t, nc, stride=CHUNK) for t in range(CHUNK)]
        us = [u_ref[r, :] for r in rows]
        dys = [dg_ref[r, :] * _dgelu(y_ref[r, :]) for r in rows]
        gd = jnp.zeros((1, 128), F32)
        for uv, dy in zip(us, dys):
            gd = gd + jnp.sum(dy * uv, axis=0, keepdims=True)
        gd_ref[...] = gd
        ua = _rows_to_chunks(us, perm)
        dya = _rows_to_chunks(dys, perm)
        lane = lax.broadcasted_iota(jnp.int32, (1, 2 * SSM_P), 1)
        dus = []
        for g in range(GB):
            uv, dy, hx_v = ua[g], dya[g], h_ref[g]
            dh = jnp.dot(dy, o_ref[g], preferred_element_type=F32)
            dinc = _shift_rows(_chunk_scan(dh, a_ref[g], True), True)
            dinc_b = dinc.astype(BF16)
            dus.append(lax.dot_general(dy, mt_ref[g], _NT, preferred_element_type=F32)
                       + lax.dot_general(dinc_b, s_ref[g], _NT, preferred_element_type=F32))
            gmt_ref[g] = lax.dot_general(uv, dy, _TN, preferred_element_type=F32)
            gs_ref[g] = lax.dot_general(uv, dinc_b, _TN, preferred_element_type=F32)
            go_ref[g] = lax.dot_general(dy, hx_v.astype(BF16), _TN, preferred_element_type=F32)
            p1 = dinc * hx_v
            p2 = pltpu.roll(dinc, SSM_P, 1) * hx_v
            t1 = jnp.sum(p1 + pltpu.roll(p1, SSM_P, 1), axis=0, keepdims=True)
            t2 = jnp.sum(p2 - pltpu.roll(p2, SSM_P, 1), axis=0, keepdims=True)
            ga_ref[g] = jnp.where(lane < SSM_P, t1, pltpu.roll(t2, SSM_P, 1))
        dup = _chunks_to_rows(dus, perm, False)
        for t, r in enumerate(rows):
            du_ref[r, :] = dup[t] + d_ref[...] * dys[t]

    g3 = lambda r, c: pl.BlockSpec((GB, r, c), lambda g: (g, 0, 0))
    col = pl.BlockSpec((L, 128), lambda g: (0, g))
    row = pl.BlockSpec((1, 128), lambda g: (0, g))
    return _call(
        body, "ssm_bwd", (SSM_G // GB,),
        [col, col, pl.BlockSpec((L, 128), lambda g: (0, U_COL0 + g)), g3(nc, 2 * SSM_P), _ANY,
         g3(CW, CW), g3(CW, 2 * SSM_P), g3(CW, 2 * SSM_P), g3(1, 2 * SSM_P), row],
        [col, g3(CW, CW), g3(CW, 2 * SSM_P), g3(CW, 2 * SSM_P), g3(1, 2 * SSM_P), row],
        [jax.ShapeDtypeStruct((L, SSM_W), F32), jax.ShapeDtypeStruct((SSM_G, CW, CW), F32),
         jax.ShapeDtypeStruct((SSM_G, CW, 2 * SSM_P), F32), jax.ShapeDtypeStruct((SSM_G, CW, 2 * SSM_P), F32),
         jax.ShapeDtypeStruct((SSM_G, 1, 2 * SSM_P), F32), jax.ShapeDtypeStruct((1, SSM_W), F32)],
        (d_yg, y, proj, hx, perm, mt, scat, ocat, a16, d_skip.reshape(1, SSM_W)),
        [pltpu.VMEM((HALF * 128, GB * 128), BF16), pltpu.SemaphoreType.DMA], ride)


def _merge(og, yg, gpre, proj, b_glu, wa, ws):
    L = og.shape[0]
    tm = _tile(L, 256)

    def body(og_ref, yg_ref, gp_ref, z0_ref, z1_ref, b_ref, wa_ref, ws_ref, m_ref):
        zs = jnp.concatenate([z0_ref[...], z1_ref[...]], axis=1)
        os_ = yg_ref[...] * _sigmoid(gp_ref[...] + b_ref[...]) * _silu(zs)
        ogv = og_ref[...]
        ra = lax.rsqrt(jnp.mean(ogv * ogv, axis=-1, keepdims=True) + NORM_EPS)
        rs = lax.rsqrt(jnp.mean(os_ * os_, axis=-1, keepdims=True) + NORM_EPS)
        m_ref[:, :ATTN_W] = (ogv * ra * wa_ref[...]).astype(BF16)
        m_ref[:, ATTN_W:] = (os_ * rs * ws_ref[...]).astype(BF16)

    row = lambda w: pl.BlockSpec((1, w), lambda i: (0, 0))
    return pl.pallas_call(
        body,
        name="merge",
        grid=(L // tm,),
        in_specs=[pl.BlockSpec((tm, ATTN_W), lambda i: (i, 0)), pl.BlockSpec((tm, SSM_W), lambda i: (i, 0)),
                  pl.BlockSpec((tm, SSM_W), lambda i: (i, 0)),
                  pl.BlockSpec((tm, 512), lambda i: (i, 7)), pl.BlockSpec((tm, 512), lambda i: (i, 8)),
                  row(SSM_W), row(ATTN_W), row(SSM_W)],
        out_specs=pl.BlockSpec((tm, D_MODEL), lambda i: (i, 0)),
        out_shape=jax.ShapeDtypeStruct((L, D_MODEL), BF16),
        compiler_params=_cp(("parallel",)),
    )(og, yg, gpre, proj, proj, b_glu.reshape(1, SSM_W), wa.reshape(1, ATTN_W), ws.reshape(1, SSM_W))


def _outproj_loss(merged, w_out, x, target):
    L = x.shape[0]
    tm, tn = _tile(L, 512), 1024
    ni, nj = L // tm, D_MODEL // tn

    def body(m_ref, w_ref, x_ref, t_ref, d_ref, db_ref, l_ref):
        out = x_ref[...] + jnp.dot(m_ref[...], w_ref[...], preferred_element_type=F32)
        diff = out - t_ref[...]
        d = diff * (1.0 / D_MODEL)
        d_ref[...] = d
        db_ref[...] = d.astype(BF16)
        l_ref[...] = jnp.full((1, 8, 128), jnp.sum(diff * diff), F32)

    return pl.pallas_call(
        body,
        name="outproj_loss",
        grid=(nj, ni),
        in_specs=[pl.BlockSpec((tm, D_MODEL), lambda j, i: (i, 0)),
                  pl.BlockSpec((D_MODEL, tn), lambda j, i: (0, j)),
                  pl.BlockSpec((tm, tn), lambda j, i: (i, j)),
                  pl.BlockSpec((tm, tn), lambda j, i: (i, j))],
        out_specs=[pl.BlockSpec((tm, tn), lambda j, i: (i, j)), pl.BlockSpec((tm, tn), lambda j, i: (i, j)),
                   pl.BlockSpec((1, 8, 128), lambda j, i: (i * nj + j, 0, 0))],
        out_shape=[jax.ShapeDtypeStruct((L, D_MODEL), F32), jax.ShapeDtypeStruct((L, D_MODEL), BF16),
                   jax.ShapeDtypeStruct((ni * nj, 8, 128), F32)],
        compiler_params=_cp(("parallel", "parallel")),
    )(merged, w_out, x, target)


def _merge_bwd(d_m, og, o, yg, gpre, proj, b_glu, wa, ws):
    L = og.shape[0]
    tm = _tile(L, 256)

    def body(dm_ref, og_ref, o_ref, yg_ref, gp_ref, za0_ref, za1_ref, zs0_ref, zs1_ref, b_ref, wa_ref, ws_ref,
             do_ref, dza_ref, dzs_ref, dg_ref, dyg_ref, gwa_ref, gws_ref, gb_ref):
        i = pl.program_id(0)

        @pl.when(i == 0)
        def _():
            gwa_ref[...] = jnp.zeros_like(gwa_ref)
            gws_ref[...] = jnp.zeros_like(gws_ref)
            gb_ref[...] = jnp.zeros_like(gb_ref)

        za = jnp.concatenate([za0_ref[...], za1_ref[...]], axis=1)
        zs = jnp.concatenate([zs0_ref[...], zs1_ref[...]], axis=1)
        ogv, dma = og_ref[...], dm_ref[:, :ATTN_W]
        ra = lax.rsqrt(jnp.mean(ogv * ogv, axis=-1, keepdims=True) + NORM_EPS)
        xh = ogv * ra
        gwa_ref[...] += jnp.sum(dma * xh, axis=0, keepdims=True)
        gx = dma * wa_ref[...]
        d_og = ra * (gx - xh * jnp.mean(gx * xh, axis=-1, keepdims=True))
        do_ref[...] = d_og * _silu(za)
        dza_ref[...] = (d_og * o_ref[...] * _dsilu(za)).astype(BF16)
        ygv = yg_ref[...]
        sg = _sigmoid(gp_ref[...] + b_ref[...])
        y2 = ygv * sg
        sz = _silu(zs)
        os_ = y2 * sz
        dms = dm_ref[:, ATTN_W:]
        rs = lax.rsqrt(jnp.mean(os_ * os_, axis=-1, keepdims=True) + NORM_EPS)
        xs = os_ * rs
        gws_ref[...] += jnp.sum(dms * xs, axis=0, keepdims=True)
        gxs = dms * ws_ref[...]
        d_os = rs * (gxs - xs * jnp.mean(gxs * xs, axis=-1, keepdims=True))
        dzs_ref[...] = (d_os * y2 * _dsilu(zs)).astype(BF16)
        d_y2 = d_os * sz
        d_g = d_y2 * ygv * sg * (1.0 - sg)
        dg_ref[...] = d_g.astype(BF16)
        gb_ref[...] += jnp.sum(d_g, axis=0, keepdims=True)
        dyg_ref[...] = d_y2 * sg

    row = lambda w: pl.BlockSpec((1, w), lambda i: (0, 0))
    full = lambda w: pl.BlockSpec((tm, w), lambda i: (i, 0))
    half = lambda c: pl.BlockSpec((tm, 512), lambda i: (i, c))
    return pl.pallas_call(
        body,
        name="merge_bwd",
        grid=(L // tm,),
        in_specs=[full(D_MODEL), full(ATTN_W), full(ATTN_W), full(SSM_W), full(SSM_W),
                  half(3), half(4), half(7), half(8), row(SSM_W), row(ATTN_W), row(SSM_W)],
        out_specs=[full(ATTN_W), full(ATTN_W), full(SSM_W), full(SSM_W), full(SSM_W),
                   row(ATTN_W), row(SSM_W), row(SSM_W)],
        out_shape=[jax.ShapeDtypeStruct((L, ATTN_W), F32), jax.ShapeDtypeStruct((L, ATTN_W), BF16),
                   jax.ShapeDtypeStruct((L, SSM_W), BF16), jax.ShapeDtypeStruct((L, SSM_W), BF16),
                   jax.ShapeDtypeStruct((L, SSM_W), F32),
                   jax.ShapeDtypeStruct((1, ATTN_W), F32), jax.ShapeDtypeStruct((1, SSM_W), F32),
                   jax.ShapeDtypeStruct((1, SSM_W), F32)],
        compiler_params=_cp(("arbitrary",)),
    )(d_m, og, o, yg, gpre, proj, proj, proj, proj, b_glu.reshape(1, SSM_W), wa.reshape(1, ATTN_W),
      ws.reshape(1, SSM_W))


def _rms_bwd_x(x, norm_w, d_hn, d_out, ride):
    L = x.shape[0]
    tm = _tile(L, 256)

    def body(x_ref, w_ref, dh_ref, do_ref, gx_ref, gw_ref):
        i = pl.program_id(0)

        @pl.when(i == 0)
        def _():
            gw_ref[...] = jnp.zeros_like(gw_ref)

        xv, dh = x_ref[...], dh_ref[...]
        r = lax.rsqrt(jnp.mean(xv * xv, axis=-1, keepdims=True) + NORM_EPS)
        xh = xv * r
        gw_ref[...] += jnp.sum(dh * xh, axis=0, keepdims=True)
        gx = dh * w_ref[...]
        gx_ref[...] = do_ref[...] + r * (gx - xh * jnp.mean(gx * xh, axis=-1, keepdims=True))

    blk = pl.BlockSpec((tm, D_MODEL), lambda i: (i, 0))
    row = pl.BlockSpec((1, D_MODEL), lambda i: (0, 0))
    return _call(body, "rms_bwd_x", (L // tm,), [blk, row, blk, blk], [blk, row],
                 [jax.ShapeDtypeStruct((L, D_MODEL), F32), jax.ShapeDtypeStruct((1, D_MODEL), F32)],
                 (x, norm_w.reshape(1, D_MODEL), d_hn, d_out), ride=ride)


def _rope_table(positions):
    inv_freq = ROPE_THETA ** (-jnp.arange(0, HEAD_DIM, 2, dtype=F32) / HEAD_DIM)
    ang = positions.astype(F32)[:, None] * inv_freq
    c, s = jnp.cos(ang), jnp.sin(ang)
    return jnp.concatenate([c, c, c, c, -s, s, -s, s], axis=1)


def _step(x, positions, target, w, core, chip):
    small = {n: w[n] for n in _SMALL}
    tab = _rope_table(positions)
    mt_b, scat_b, ocat_b, a16 = _ssm_prep(small)
    perm = _chunk_perm()
    blocks = lambda t: t.reshape(N_DEV, t.shape[0] // N_DEV, t.shape[1])

    (wt_in,) = _run_exchange(_gather_exchange([w["w_in"].T.astype(BF16)]), "gather_w_in")
    wt_in = wt_in.reshape(IN_W, D_MODEL)

    (proj, hn), (w_glu, w_out) = _rms_inproj(
        x, small["norm_w"], wt_in, _gather_exchange([w["w_glu"].astype(BF16), w["w_out"].astype(BF16)]))
    w_glu, w_out = w_glu.reshape(SSM_W, SSM_W), w_out.reshape(D_MODEL, D_MODEL)
    q_rot, k_rot = _qk_prep(proj, tab, small["q_norm_w"], small["k_norm_w"])
    og, o, lse = _attn_fwd(q_rot, k_rot, proj, small["sinks"])
    y, yg, hx = _ssm_fwd(proj, perm, mt_b, scat_b, ocat_b, a16, small["d_skip"])
    gpre = _mm(yg, w_glu, "nn", F32, "glu_fwd")
    merged = _merge(og, yg, gpre, proj, small["b_glu"], small["attn_out_norm_w"], small["ssm_out_norm_w"])
    d_out, d_out_b, loss_parts = _outproj_loss(merged, w_out, x, target)
    loss = 0.5 * jnp.sum(loss_parts[:, 0, 0]) / D_MODEL

    g_w_out = blocks(_mm(merged, d_out_b, "tn", F32, "grad_w_out"))
    d_m = _mm(d_out_b, w_out, "nt", F32, "d_merged")
    d_o, d_za, d_zs, d_g, d_yg1, g_wa, g_ws, g_bglu = _merge_bwd(
        d_m, og, o, yg, gpre, proj, small["b_glu"], small["attn_out_norm_w"], small["ssm_out_norm_w"])
    g_w_glu = blocks(_mm(yg, d_g, "tn", F32, "grad_w_glu"))
    d_yg = _mm(d_g, w_glu, "nt", F32, "d_yg", add=d_yg1)
    (d_u, g_mt, g_scat, g_ocat, g_a16, g_dskip), (ra_out, ra_glu) = _ssm_bwd(
        d_yg, y, proj, hx, perm, mt_b, scat_b, ocat_b, a16, small["d_skip"], _pair_exchange([g_w_out, g_w_glu]))
    p_out = _pair_sum(g_w_out, ra_out, core, BF16, "pair_sum_out")
    p_glu = _pair_sum(g_w_glu, ra_glu, core, BF16, "pair_sum_glu")
    g_small = _ssm_prep_bwd(small, g_mt, g_scat, g_ocat, g_a16)
    (d_q, d_k, d_v, g_sinks), (rb_out, rb_glu) = _attn_bwd(
        q_rot, k_rot, proj, small["sinks"], d_o, o, lse, _chip_exchange([p_out, p_glu]))
    d_proj, g_qw, g_kw = _qk_prep_bwd(proj, tab, small["q_norm_w"], small["k_norm_w"], d_q, d_k, d_v,
                                      d_za, d_u, d_zs)
    g_qw = g_qw[0, :HEAD_DIM] + g_qw[0, HEAD_DIM:]
    g_kw = g_kw[0, :HEAD_DIM] + g_kw[0, HEAD_DIM:]
    g_wt_in = blocks(_mm(d_proj, hn, "tn", F32, "grad_w_in"))
    d_hn, (ra_in,) = _mm(d_proj, wt_in, "nn", F32, "d_hn", ride=_pair_exchange([g_wt_in]))
    p_in = _pair_sum(g_wt_in, ra_in, core, BF16, "pair_sum_in")
    (grad_x, g_nw), (rb_in,) = _rms_bwd_x(x, small["norm_w"], d_hn, d_out, _chip_exchange([p_in]))

    g_small.update(norm_w=g_nw.reshape(-1), q_norm_w=g_qw.reshape(-1), k_norm_w=g_kw.reshape(-1),
                   sinks=g_sinks[0, :N_HEADS], d_skip=g_dskip.reshape(-1), b_glu=g_bglu.reshape(-1),
                   attn_out_norm_w=g_wa.reshape(-1), ssm_out_norm_w=g_ws.reshape(-1))
    slab = _pack(g_small).reshape(N_DEV, _PACK_ROWS // N_DEV, 128)
    (ra_s,) = _run_exchange(_pair_exchange([slab]), "pair_exchange_small")
    p_s = _pair_sum(slab, ra_s, core, F32, "pair_sum_small")
    (rb_s,) = _run_exchange(_chip_exchange([p_s]), "chip_exchange_small")
    (g_packed,) = _run_exchange(_gather_exchange([_chip_sum(p_s, rb_s, chip, "chip_sum_small")]), "gather_small")

    grads = _unpack(g_packed.reshape(_PACK_ROWS, 128), w)
    grads.update(w_in=_chip_sum(p_in, rb_in, chip, "chip_sum_in").T,
                 w_glu=_chip_sum(p_glu, rb_glu, chip, "chip_sum_glu"),
                 w_out=_chip_sum(p_out, rb_out, chip, "chip_sum_out"))
    return loss, grad_x, grads


_ANY = pl.BlockSpec(memory_space=pl.ANY)


class _Exchange:
    def __init__(self, arrays, out_shape, sems, start, finish):
        self.arrays, self.out_shape, self.sems, self.start, self.finish = arrays, out_shape, sems, start, finish


def _gather_exchange(blocks):
    n = len(blocks)

    def parts(ins, outs, sems):
        send_sems, recv_sems, local_sems = sems
        x, y, c = lax.axis_index("x"), lax.axis_index("y"), lax.axis_index("c")
        me, sibling = (x, y, c), (x, y, 1 - c)
        chips = [(1 - x, y), (x, 1 - y), (1 - x, 1 - y)]

        def slot(k, dev):
            return outs[k].at[4 * dev[0] + 2 * dev[1] + dev[2]]

        def copy(k, q, block, to, src=None):
            return pltpu.make_async_remote_copy(
                src_ref=slot(k, block) if src is None else src, dst_ref=slot(k, block),
                send_sem=send_sems.at[k, q], recv_sem=recv_sems.at[k, q], device_id=to, device_id_type=MESH)

        mine = [pltpu.make_async_copy(ins[k], slot(k, me), local_sems.at[k]) for k in range(n)]
        first = []
        for k in range(n):
            first.append(copy(k, 0, me, sibling, src=ins[k]))
            first += [copy(k, 1 + j, me, (*chip, c), src=ins[k]) for j, chip in enumerate(chips)]
        return me, sibling, chips, c, copy, mine, first

    def start(ins, outs, sems):
        *_, mine, first = parts(ins, outs, sems)
        for cp in mine + first:
            cp.start()

    def finish(ins, outs, sems):
        me, sibling, chips, c, copy, mine, first = parts(ins, outs, sems)
        passed = []
        for j, chip in enumerate(chips):
            for k in range(n):
                copy(k, 1 + j, (*chip, c), me).wait_recv()
                fwd = copy(k, 4 + j, (*chip, c), sibling)
                fwd.start()
                passed.append(fwd)
        for k in range(n):
            copy(k, 0, sibling, me).wait_recv()
            for j, chip in enumerate(chips):
                copy(k, 4 + j, (*chip, 1 - c), me).wait_recv()
        for cp in first + passed:
            cp.wait_send()
        for cp in mine:
            cp.wait()

    return _Exchange(blocks, [jax.ShapeDtypeStruct((N_DEV,) + b.shape, b.dtype) for b in blocks],
                     [pltpu.SemaphoreType.DMA((n, 7)), pltpu.SemaphoreType.DMA((n, 7)), pltpu.SemaphoreType.DMA((n,))],
                     start, finish)


def _direct_exchange(arrays, out_lead, fan, route):
    n = len(arrays)

    def copies(ins, outs, sems):
        send_sems, recv_sems = sems
        legs = route(lax.axis_index("x"), lax.axis_index("y"), lax.axis_index("c"))
        return [pltpu.make_async_remote_copy(
            src_ref=ins[k].at[src], dst_ref=outs[k].at[q], send_sem=send_sems.at[k, q], recv_sem=recv_sems.at[k, q],
            device_id=to, device_id_type=MESH) for k in range(n) for src, q, to in legs]

    def start(ins, outs, sems):
        for cp in copies(ins, outs, sems):
            cp.start()

    def finish(ins, outs, sems):
        for cp in copies(ins, outs, sems):
            cp.wait()

    return _Exchange(arrays, [jax.ShapeDtypeStruct((out_lead,) + a.shape[1:], a.dtype) for a in arrays],
                     [pltpu.SemaphoreType.DMA((n, fan)), pltpu.SemaphoreType.DMA((n, fan))], start, finish)


def _pair_exchange(grads):
    return _direct_exchange(grads, 4, 4, lambda x, y, c: [(2 * chip + (1 - c), chip, (x, y, 1 - c))
                                                          for chip in range(4)])


def _chip_exchange(parts):
    def route(x, y, c):
        chips = [(1 - x, y), (x, 1 - y), (1 - x, 1 - y)]
        return [(2 * chip[0] + chip[1], q, (*chip, c)) for q, chip in enumerate(chips)]
    return _direct_exchange(parts, 3, 3, route)


def _run_exchange(ex, name):
    n = len(ex.arrays)

    def body(*refs):
        ins, outs, sems = refs[:n], refs[n:2 * n], refs[2 * n:]
        ex.start(ins, outs, sems)
        ex.finish(ins, outs, sems)

    return list(pl.pallas_call(body, name=name, in_specs=[_ANY] * n, out_specs=[_ANY] * n, out_shape=ex.out_shape,
                               scratch_shapes=ex.sems)(*ex.arrays))


def _call(body, name, grid, in_specs, out_specs, out_shape, args, scratch_shapes=(), ride=None):
    if ride is None:
        sem = ("arbitrary",) * len(grid)
        return pl.pallas_call(body, name=name, grid=grid, in_specs=in_specs, out_specs=out_specs, out_shape=out_shape,
                              scratch_shapes=list(scratch_shapes), compiler_params=_cp(sem))(*args), None
    n_in, n_out, n_scr, n_x = len(in_specs), len(out_specs), len(scratch_shapes), len(ride.arrays)

    def wrapped(*refs):
        ins, refs = refs[:n_in], refs[n_in:]
        x_in, refs = refs[:n_x], refs[n_x:]
        outs, refs = refs[:n_out], refs[n_out:]
        x_out, refs = refs[:n_x], refs[n_x:]
        scr, sems = refs[:n_scr], refs[n_scr:]
        first = pl.program_id(0) == 0
        last = pl.program_id(0) == grid[0] - 1
        for a in range(1, len(grid)):
            first = jnp.logical_and(first, pl.program_id(a) == 0)
            last = jnp.logical_and(last, pl.program_id(a) == grid[a] - 1)

        @pl.when(first)
        def _():
            ride.start(x_in, x_out, sems)

        body(*ins, *outs, *scr)

        @pl.when(last)
        def _():
            ride.finish(x_in, x_out, sems)

    res = pl.pallas_call(
        wrapped, name=name, grid=grid, in_specs=list(in_specs) + [_ANY] * n_x,
        out_specs=list(out_specs) + [_ANY] * n_x, out_shape=list(out_shape) + list(ride.out_shape),
        scratch_shapes=list(scratch_shapes) + list(ride.sems),
        compiler_params=_cp(("arbitrary",) * len(grid)))(*args, *ride.arrays)
    return res[:n_out], list(res[n_out:])


def _pair_sum(g, ra, core, out_dtype, name):
    _, r, C = g.shape
    tr = _tile(r, 128)

    def body(c_ref, g_ref, ra_ref, p_ref):
        p_ref[...] = (g_ref[...] + ra_ref[...]).astype(p_ref.dtype)

    return pl.pallas_call(
        body,
        name=name,
        grid_spec=pltpu.PrefetchScalarGridSpec(
            num_scalar_prefetch=1,
            grid=(4, r // tr),
            in_specs=[pl.BlockSpec((1, tr, C), lambda j, t, c_ref: (2 * j + c_ref[0], t, 0)),
                      pl.BlockSpec((1, tr, C), lambda j, t, c_ref: (j, t, 0))],
            out_specs=pl.BlockSpec((1, tr, C), lambda j, t, c_ref: (j, t, 0)),
        ),
        out_shape=jax.ShapeDtypeStruct((4, r, C), out_dtype),
        compiler_params=_cp(("parallel", "parallel")),
    )(core, g, ra)


def _chip_sum(p, rb, chip, name):
    _, r, C = p.shape
    tr = _tile(r, 128)

    def body(c_ref, p_ref, rb_ref, o_ref):
        acc = p_ref[0].astype(F32) + rb_ref[0].astype(F32)
        acc = acc + rb_ref[1].astype(F32)
        o_ref[...] = acc + rb_ref[2].astype(F32)

    return pl.pallas_call(
        body,
        name=name,
        grid_spec=pltpu.PrefetchScalarGridSpec(
            num_scalar_prefetch=1,
            grid=(r // tr,),
            in_specs=[pl.BlockSpec((1, tr, C), lambda t, c_ref: (c_ref[0], t, 0)),
                      pl.BlockSpec((3, tr, C), lambda t, c_ref: (0, t, 0))],
            out_specs=pl.BlockSpec((tr, C), lambda t, c_ref: (t, 0)),
        ),
        out_shape=jax.ShapeDtypeStruct((r, C), F32),
        compiler_params=_cp(("parallel",)),
    )(chip, p, rb)


def _adamw(g, w, m, v, name):
    R, C = g.shape
    tr = _tile(R, 256)
    c1 = 1.0 - ADAM_B1 ** ADAM_STEP
    c2 = 1.0 - ADAM_B2 ** ADAM_STEP

    def body(g_ref, w_ref, m_ref, v_ref, d_ref, nm_ref, nv_ref):
        gv = g_ref[...]
        nm = ADAM_B1 * m_ref[...] + (1.0 - ADAM_B1) * gv
        nv = ADAM_B2 * v_ref[...] + (1.0 - ADAM_B2) * (gv * gv)
        nm_ref[...] = nm
        nv_ref[...] = nv
        d_ref[...] = -ADAM_LR * ((nm / c1) / (jnp.sqrt(nv / c2) + ADAM_EPS) + ADAM_WD * w_ref[...])

    blk = pl.BlockSpec((tr, C), lambda i: (i, 0))
    return pl.pallas_call(
        body, name=name, grid=(R // tr,), in_specs=[blk] * 4, out_specs=[blk] * 3,
        out_shape=[jax.ShapeDtypeStruct((R, C), F32)] * 3, compiler_params=_cp(("parallel",)),
    )(g, w, m, v)


_SMALL = ("norm_w", "q_norm_w", "k_norm_w", "sinks", "a_re", "a_im", "log_step", "b_re", "b_im", "c_re", "c_im",
          "d_skip", "b_glu", "attn_out_norm_w", "ssm_out_norm_w")
_WEIGHTS = ("norm_w", "w_in", "q_norm_w", "k_norm_w", "sinks", "a_re", "a_im", "log_step", "b_re", "b_im", "c_re",
            "c_im", "d_skip", "w_glu", "b_glu", "attn_out_norm_w", "ssm_out_norm_w", "w_out")
_SMALL_2D = dict(norm_w=(1, 2048), q_norm_w=(1, 64), k_norm_w=(1, 64), sinks=(1, 16), a_re=(64, 64), a_im=(64, 64),
                 log_step=(1, 64), b_re=(4096, 16), b_im=(4096, 16), c_re=(1024, 64), c_im=(1024, 64),
                 d_skip=(1, 1024), b_glu=(1, 1024), attn_out_norm_w=(1, 1024), ssm_out_norm_w=(1, 1024))


def _slab_rows(n):
    return -(-n // 1024) * 8


_PACK_ROWS = 2304


def _pack(d):
    parts = []
    for n in _SMALL:
        flat = d[n].reshape(-1).astype(F32)
        rows = _slab_rows(flat.shape[0])
        parts.append(jnp.pad(flat, (0, rows * 128 - flat.shape[0])).reshape(rows, 128))
    used = sum(p.shape[0] for p in parts)
    parts.append(jnp.zeros((_PACK_ROWS - used, 128), F32))
    return jnp.concatenate(parts, axis=0)


def _unpack(packed, like):
    out, off = {}, 0
    for n in _SMALL:
        size = math.prod(like[n].shape)
        rows = _slab_rows(size)
        out[n] = packed[off:off + rows].reshape(-1)[:size].reshape(like[n].shape)
        off += rows
    return out


def _adamw_small(g, w, m, v):
    c1 = 1.0 - ADAM_B1 ** ADAM_STEP
    c2 = 1.0 - ADAM_B2 ** ADAM_STEP
    k = len(_SMALL)

    def body(*refs):
        ins, outs = refs[:4 * k], refs[4 * k:]
        for j in range(k):
            gv, wv, mv, vv = (ins[q * k + j][...] for q in range(4))
            nm = ADAM_B1 * mv + (1.0 - ADAM_B1) * gv
            nv = ADAM_B2 * vv + (1.0 - ADAM_B2) * (gv * gv)
            outs[j][...] = -ADAM_LR * ((nm / c1) / (jnp.sqrt(nv / c2) + ADAM_EPS) + ADAM_WD * wv)
            outs[k + j][...] = nm
            outs[2 * k + j][...] = nv

    args = [d[n].reshape(_SMALL_2D[n]) for d in (g, w, m, v) for n in _SMALL]
    shapes = [jax.ShapeDtypeStruct(_SMALL_2D[n], F32) for _ in range(3) for n in _SMALL]
    outs = pl.pallas_call(body, name="adamw_small", out_shape=shapes, compiler_params=_cp())(*args)
    res = []
    for q in range(3):
        res.append({n: outs[q * k + j].reshape(w[n].shape) for j, n in enumerate(_SMALL)})
    return res


def kernel(x, positions, norm_w, w_in, q_norm_w, k_norm_w, sinks, a_re, a_im, log_step, b_re, b_im, c_re, c_im, d_skip, w_glu, b_glu, attn_out_norm_w, ssm_out_norm_w, w_out, loss_target, m_norm_w, m_w_in, m_q_norm_w, m_k_norm_w, m_sinks, m_a_re, m_a_im, m_log_step, m_b_re, m_b_im, m_c_re, m_c_im, m_d_skip, m_w_glu, m_b_glu, m_attn_out_norm_w, m_ssm_out_norm_w, m_w_out, v_norm_w, v_w_in, v_q_norm_w, v_k_norm_w, v_sinks, v_a_re, v_a_im, v_log_step, v_b_re, v_b_im, v_c_re, v_c_im, v_d_skip, v_w_glu, v_b_glu, v_attn_out_norm_w, v_ssm_out_norm_w, v_w_out):
    w = dict(norm_w=norm_w, w_in=w_in, q_norm_w=q_norm_w, k_norm_w=k_norm_w, sinks=sinks, a_re=a_re, a_im=a_im,
             log_step=log_step, b_re=b_re, b_im=b_im, c_re=c_re, c_im=c_im, d_skip=d_skip, w_glu=w_glu, b_glu=b_glu,
             attn_out_norm_w=attn_out_norm_w, ssm_out_norm_w=ssm_out_norm_w, w_out=w_out)
    m = dict(norm_w=m_norm_w, w_in=m_w_in, q_norm_w=m_q_norm_w, k_norm_w=m_k_norm_w, sinks=m_sinks, a_re=m_a_re,
             a_im=m_a_im, log_step=m_log_step, b_re=m_b_re, b_im=m_b_im, c_re=m_c_re, c_im=m_c_im, d_skip=m_d_skip,
             w_glu=m_w_glu, b_glu=m_b_glu, attn_out_norm_w=m_attn_out_norm_w, ssm_out_norm_w=m_ssm_out_norm_w,
             w_out=m_w_out)
    v = dict(norm_w=v_norm_w, w_in=v_w_in, q_norm_w=v_q_norm_w, k_norm_w=v_k_norm_w, sinks=v_sinks, a_re=v_a_re,
             a_im=v_a_im, log_step=v_log_step, b_re=v_b_re, b_im=v_b_im, c_re=v_c_re, c_im=v_c_im, d_skip=v_d_skip,
             w_glu=v_w_glu, b_glu=v_b_glu, attn_out_norm_w=v_attn_out_norm_w, ssm_out_norm_w=v_ssm_out_norm_w,
             w_out=v_w_out)
    core = lax.axis_index("c").astype(jnp.int32).reshape(1)
    chip = (2 * lax.axis_index("x") + lax.axis_index("y")).astype(jnp.int32).reshape(1)

    loss, grad_x, grads = _step(x[0], positions[0], loss_target[0], w, core, chip)
    loss = lax.psum(loss, ("x", "y", "c"))
    delta, new_m, new_v = {}, {}, {}
    for n in ("w_in", "w_glu", "w_out"):
        delta[n], new_m[n], new_v[n] = _adamw(grads[n], w[n], m[n], v[n], f"adamw_{n}")
    d_s, m_s, v_s = _adamw_small(grads, w, m, v)
    delta.update(d_s)
    new_m.update(m_s)
    new_v.update(v_s)

    return (loss, grad_x[None], *[grads[n] for n in _WEIGHTS], *[delta[n] for n in _WEIGHTS],
            *[new_m[n] for n in _WEIGHTS], *[new_v[n] for n in _WEIGHTS])
```

```python
import functools
import math

import jax
import jax.numpy as jnp
from jax import lax
from jax.experimental import pallas as pl
from jax.experimental.pallas import tpu as pltpu

F32 = jnp.float32
BF16 = jnp.bfloat16

D_MODEL = 2048
ATTN_W = 1024
KV_W = 256
SSM_W = 1024
HEAD_DIM = 64
N_HEADS = 16
N_KV = 4
KV_REP = 4
IN_W = 4608
BLOCK = 128
ROPE_THETA = 10000.0
NORM_EPS = 1e-6
SSM_G = 64
SSM_P = 64
SSM_H = 16
CHUNK = 16
CW = CHUNK * SSM_H
N_DEV = 8

ADAM_LR = 0.001
ADAM_B1 = 0.9
ADAM_B2 = 0.999
ADAM_EPS = 1e-08
ADAM_WD = 0.01
ADAM_STEP = 10

VMEM_LIMIT = 56 * 1024 * 1024
MESH = pl.DeviceIdType.MESH


def _cp(sem=None):
    if sem is None:
        return pltpu.CompilerParams(vmem_limit_bytes=VMEM_LIMIT)
    return pltpu.CompilerParams(vmem_limit_bytes=VMEM_LIMIT, dimension_semantics=sem)


def _sigmoid(x):
    return 1.0 / (1.0 + jnp.exp(-x))


def _silu(x):
    return x * _sigmoid(x)


def _dsilu(x):
    s = _sigmoid(x)
    return s * (1.0 + x * (1.0 - s))


_GELU_C = math.sqrt(2.0 / math.pi)


def _gelu(y):
    t = jnp.tanh(_GELU_C * (y + 0.044715 * y * y * y))
    return 0.5 * y * (1.0 + t)


def _dgelu(y):
    t = jnp.tanh(_GELU_C * (y + 0.044715 * y * y * y))
    return 0.5 * (1.0 + t) + 0.5 * y * (1.0 - t * t) * _GELU_C * (1.0 + 3.0 * 0.044715 * y * y)


def _tile(n, want):
    if n <= want:
        return n
    for t in range(want - want % 16, 0, -16):
        if n % t == 0:
            return t
    raise ValueError((n, want))


def _mm(a, b, mode, out_dtype, name, tm=512, tn=1024, add=None, ride=None, panel=None):
    if mode == "nn":
        (M, K), (K2, N) = a.shape, b.shape
    elif mode == "nt":
        (M, K), (N, K2) = a.shape, b.shape
    else:
        (K, M), (K2, N) = a.shape, b.shape
    assert K == K2
    tm, tn = _tile(M, tm), _tile(N, tn)
    p0 = 0
    if panel is not None:
        assert mode != "nt" and add is None
        p0, N = panel, tn
    dn = {"nn": _NN, "nt": _NT, "tn": _TN}[mode]

    def body(a_ref, b_ref, *rest):
        o_ref = rest[-1]
        acc = lax.dot_general(a_ref[...].astype(BF16), b_ref[...].astype(BF16), dn, preferred_element_type=F32)
        if add is not None:
            acc = acc + rest[0][...]
        o_ref[...] = acc.astype(o_ref.dtype)

    a_spec = pl.BlockSpec((K, tm), lambda j, i: (0, i)) if mode == "tn" else pl.BlockSpec((tm, K), lambda j, i: (i, 0))
    b_spec = (pl.BlockSpec((tn, K), lambda j, i: (j, 0)) if mode == "nt"
              else pl.BlockSpec((K, tn), lambda j, i: (0, j + p0)))
    o_spec = pl.BlockSpec((tm, tn), lambda j, i: (i, j))
    extra = () if add is None else (add,)
    if ride is not None:
        (out,), landed = _call(body, name, (N // tn, M // tm), [a_spec, b_spec] + [o_spec] * len(extra), [o_spec],
                               [jax.ShapeDtypeStruct((M, N), out_dtype)], (a, b, *extra), ride=ride)
        return out, landed
    return pl.pallas_call(
        body,
        name=name,
        grid=(N // tn, M // tm),
        in_specs=[a_spec, b_spec] + [o_spec] * len(extra),
        out_specs=o_spec,
        out_shape=jax.ShapeDtypeStruct((M, N), out_dtype),
        compiler_params=_cp(("parallel", "parallel")),
    )(a, b, *extra)


def _rms_inproj(x, norm_w, wt_in, ride):
    L = x.shape[0]
    tm, tn = _tile(L, 1024), 768
    nj = IN_W // tn

    def body(x_ref, w_ref, wt_ref, proj_ref, hn_ref, hn_scr):
        j = pl.program_id(1)

        @pl.when(j == 0)
        def _():
            xv = x_ref[...]
            r = lax.rsqrt(jnp.mean(xv * xv, axis=-1, keepdims=True) + NORM_EPS)
            hn = (xv * r * w_ref[...]).astype(BF16)
            hn_scr[...] = hn
            hn_ref[...] = hn

        proj_ref[...] = lax.dot_general(hn_scr[...], wt_ref[...], (((1,), (1,)), ((), ())),
                                        preferred_element_type=F32)

    return _call(
        body, "rms_inproj", (L // tm, nj),
        [pl.BlockSpec((tm, D_MODEL), lambda i, j: (i, 0)),
         pl.BlockSpec((1, D_MODEL), lambda i, j: (0, 0)),
         pl.BlockSpec((tn, D_MODEL), lambda i, j: (j, 0))],
        [pl.BlockSpec((tm, tn), lambda i, j: (i, j)),
         pl.BlockSpec((tm, D_MODEL), lambda i, j: (i, 0))],
        [jax.ShapeDtypeStruct((L, IN_W), F32), jax.ShapeDtypeStruct((L, D_MODEL), BF16)],
        (x, norm_w.reshape(1, D_MODEL), wt_in), [pltpu.VMEM((tm, D_MODEL), BF16)], ride)


def _seg_sum(v):
    a = lax.broadcasted_iota(jnp.int32, (128, 128), 0) // HEAD_DIM
    b = lax.broadcasted_iota(jnp.int32, (128, 128), 1) // HEAD_DIM
    ones = jnp.where(a == b, 1.0, 0.0).astype(BF16)
    hi = v.astype(BF16)
    lo = (v - hi.astype(F32)).astype(BF16)
    return jnp.dot(hi, ones, preferred_element_type=F32) + jnp.dot(lo, ones, preferred_element_type=F32)


def _rot_half(t):
    lane = lax.broadcasted_iota(jnp.int32, t.shape, 1)
    return jnp.where(lane % HEAD_DIM < HEAD_DIM // 2, pltpu.roll(t, 128 - HEAD_DIM // 2, 1),
                     pltpu.roll(t, HEAD_DIM // 2, 1))


def _norm_rope(raw, w, cos, sin):
    r = lax.rsqrt(_seg_sum(raw * raw) * (1.0 / HEAD_DIM) + NORM_EPS)
    tn = raw * r * w
    return r, tn * cos + _rot_half(tn) * sin


def _norm_rope_bwd(d_rot, raw, w, cos, sin):
    r = lax.rsqrt(_seg_sum(raw * raw) * (1.0 / HEAD_DIM) + NORM_EPS)
    d_tn = d_rot * cos + _rot_half(d_rot * sin)
    xh = raw * r
    gw = d_tn * w
    d_raw = r * (gw - xh * (_seg_sum(gw * xh) * (1.0 / HEAD_DIM)))
    return d_raw, d_tn * xh


def _band_mask2(has_prev):
    qi = lax.broadcasted_iota(jnp.int32, (2 * BLOCK, 2 * BLOCK), 0) % BLOCK + BLOCK
    kj = lax.broadcasted_iota(jnp.int32, (2 * BLOCK, 2 * BLOCK), 1)
    rel = qi - kj
    return (rel >= 0) & (rel < BLOCK) & ((kj >= BLOCK) | has_prev)


def _half_tiles(pair):
    lo = lax.broadcasted_iota(jnp.int32, pair.shape, 1) < HEAD_DIM
    sw = pltpu.roll(pair, HEAD_DIM, 1)
    z = jnp.zeros_like(pair)
    return (jnp.where(lo, pair, z).astype(BF16), jnp.where(lo, z, sw).astype(BF16),
            jnp.where(lo, sw, z).astype(BF16), jnp.where(lo, z, pair).astype(BF16))


def _two_rows(top, bottom):
    row = lax.broadcasted_iota(jnp.int32, (2 * BLOCK, 1), 0)
    return jnp.where(row < BLOCK, top, bottom)


def _lane_col(mat, h):
    lane = lax.broadcasted_iota(jnp.int32, mat.shape, 1)
    return jnp.sum(jnp.where(lane == h, mat, 0.0), axis=1, keepdims=True)


_SCALE = 1.0 / math.sqrt(HEAD_DIM)
_NT = (((1,), (1,)), ((), ()))
_NN = (((1,), (0,)), ((), ()))
_TN = (((0,), (0,)), ((), ()))


def _qk_prep(proj, tab, qw, kw):
    L = proj.shape[0]
    tm = _tile(L, 512)

    def body(q_ref, k_ref, t_ref, qw_ref, kw_ref, qo_ref, ko_ref):
        cos, sin = t_ref[:, :128], t_ref[:, 128:]
        for c in range(ATTN_W // 128):
            _, qr = _norm_rope(q_ref[:, c * 128:(c + 1) * 128], qw_ref[...], cos, sin)
            qo_ref[:, c * 128:(c + 1) * 128] = (qr * _SCALE).astype(BF16)
        for c in range(KV_W // 128):
            _, kr = _norm_rope(k_ref[:, c * 128:(c + 1) * 128], kw_ref[...], cos, sin)
            ko_ref[:, c * 128:(c + 1) * 128] = kr.astype(BF16)

    row = pl.BlockSpec((1, 128), lambda i: (0, 0))
    return pl.pallas_call(
        body,
        name="qk_prep",
        grid=(L // tm,),
        in_specs=[pl.BlockSpec((tm, ATTN_W), lambda i: (i, 0)), pl.BlockSpec((tm, KV_W), lambda i: (i, 4)),
                  pl.BlockSpec((tm, 256), lambda i: (i, 0)), row, row],
        out_specs=[pl.BlockSpec((tm, ATTN_W), lambda i: (i, 0)), pl.BlockSpec((tm, KV_W), lambda i: (i, 0))],
        out_shape=[jax.ShapeDtypeStruct((L, ATTN_W), BF16), jax.ShapeDtypeStruct((L, KV_W), BF16)],
        compiler_params=_cp(("parallel",)),
    )(proj, proj, tab, jnp.tile(qw, 2).reshape(1, 128), jnp.tile(kw, 2).reshape(1, 128))


def _group_tiles(g, kt, vt):
    a, b = divmod(g, 2)
    return kt[a][2 * b], kt[a][2 * b + 1], vt[a][2 * b], vt[a][2 * b + 1]


def _attn_fwd(q, k, proj, sinks):
    L = proj.shape[0]
    nb = L // BLOCK

    def body(q_ref, kc_ref, kp_ref, vc_ref, vp_ref, z0_ref, z1_ref, sink_ref, og_ref, o_ref, lse_ref):
        i = pl.program_id(0)
        mask = _band_mask2(i > 0)
        z = jnp.concatenate([z0_ref[...], z1_ref[...]], axis=1)
        lane = lax.broadcasted_iota(jnp.int32, (BLOCK, 128), 1)
        kt = [_half_tiles(jnp.concatenate([kp_ref[:, a * 128:(a + 1) * 128], kc_ref[:, a * 128:(a + 1) * 128]],
                                          axis=0).astype(F32)) for a in range(2)]
        vt = [_half_tiles(jnp.concatenate([vp_ref[:, a * 128:(a + 1) * 128], vc_ref[:, a * 128:(a + 1) * 128]],
                                          axis=0)) for a in range(2)]
        lse_mat = jnp.zeros((BLOCK, 128), F32)
        outs = []
        for g in range(N_KV):
            k_lo, k_hi, v_lo, v_hi = _group_tiles(g, kt, vt)
            q2 = jnp.concatenate([q_ref[:, 2 * g * 128:(2 * g + 1) * 128],
                                  q_ref[:, (2 * g + 1) * 128:(2 * g + 2) * 128]], axis=0)
            acc = jnp.zeros((2 * BLOCK, 128), F32)
            for half, (kh, vh) in enumerate(((k_lo, v_lo), (k_hi, v_hi))):
                h_top, h_bot = 4 * g + half, 4 * g + 2 + half
                s = jnp.where(mask, lax.dot_general(q2, kh, _NT, preferred_element_type=F32), -1e30)
                sink = _two_rows(sink_ref[h_top], sink_ref[h_bot])
                m = jnp.maximum(jnp.max(s, axis=-1, keepdims=True), sink)
                e = jnp.exp(s - m)
                den = jnp.sum(e, axis=-1, keepdims=True) + jnp.exp(sink - m)
                p = e / den
                acc = acc + jnp.dot(p.astype(BF16), vh, preferred_element_type=F32)
                lse = m + jnp.log(den)
                lse_mat = jnp.where(lane == h_top, lse[:BLOCK], lse_mat)
                lse_mat = jnp.where(lane == h_bot, lse[BLOCK:], lse_mat)
            outs += [acc[:BLOCK], acc[BLOCK:]]
        o = jnp.concatenate(outs, axis=1)
        o_ref[...] = o
        og_ref[...] = o * _silu(z)
        lse_ref[...] = lse_mat

    prev = lambda i: jnp.maximum(i - 1, 0)
    return pl.pallas_call(
        body,
        name="attn_fwd",
        grid=(nb,),
        in_specs=[pl.BlockSpec((BLOCK, ATTN_W), lambda i: (i, 0)),
                  pl.BlockSpec((BLOCK, KV_W), lambda i: (i, 0)),
                  pl.BlockSpec((BLOCK, KV_W), lambda i: (prev(i), 0)),
                  pl.BlockSpec((BLOCK, KV_W), lambda i: (i, 5)),
                  pl.BlockSpec((BLOCK, KV_W), lambda i: (prev(i), 5)),
                  pl.BlockSpec((BLOCK, 512), lambda i: (i, 3)),
                  pl.BlockSpec((BLOCK, 512), lambda i: (i, 4)),
                  pl.BlockSpec(memory_space=pltpu.SMEM)],
        out_specs=[pl.BlockSpec((BLOCK, ATTN_W), lambda i: (i, 0)),
                   pl.BlockSpec((BLOCK, ATTN_W), lambda i: (i, 0)),
                   pl.BlockSpec((BLOCK, 128), lambda i: (i, 0))],
        out_shape=[jax.ShapeDtypeStruct((L, ATTN_W), F32), jax.ShapeDtypeStruct((L, ATTN_W), F32),
                   jax.ShapeDtypeStruct((L, 128), F32)],
        compiler_params=_cp(("parallel",)),
    )(q, k, k, proj, proj, proj, proj, sinks)


def _attn_bwd(q, k, proj, sinks, d_o, o, lse, ride):
    L = proj.shape[0]
    nb = L // BLOCK

    def body(q_ref, kc_ref, kp_ref, vc_ref, vp_ref, do_ref, o_ref, lse_ref, sink_ref,
             dq_ref, dk_ref, dv_ref, gs_ref, ck_scr, cv_scr):
        i = pl.program_id(0)

        @pl.when(i == 0)
        def _():
            gs_ref[...] = jnp.zeros_like(gs_ref)
            ck_scr[...] = jnp.zeros_like(ck_scr)
            cv_scr[...] = jnp.zeros_like(cv_scr)

        @pl.when(i == nb)
        def _():
            dk_ref[...] = ck_scr[...]
            dv_ref[...] = cv_scr[...]

        @pl.when(i < nb)
        def _():
            mask = _band_mask2(i > 0)
            lane = lax.broadcasted_iota(jnp.int32, (1, 128), 1)
            lo = lax.broadcasted_iota(jnp.int32, (2 * BLOCK, 128), 1) < HEAD_DIM
            lse_c = lse_ref[...]
            kt = [_half_tiles(jnp.concatenate([kp_ref[:, a * 128:(a + 1) * 128], kc_ref[:, a * 128:(a + 1) * 128]],
                                              axis=0).astype(F32)) for a in range(2)]
            vt = [_half_tiles(jnp.concatenate([vp_ref[:, a * 128:(a + 1) * 128], vc_ref[:, a * 128:(a + 1) * 128]],
                                              axis=0)) for a in range(2)]
            gs = jnp.zeros((1, 128), F32)
            dq_parts = []
            dk_acc = [jnp.zeros((2 * BLOCK, 128), F32) for _ in range(2)]
            dv_acc = [jnp.zeros((2 * BLOCK, 128), F32) for _ in range(2)]
            for g in range(N_KV):
                a, b = divmod(g, 2)
                k_lo, k_hi, v_lo, v_hi = _group_tiles(g, kt, vt)
                t0, t1 = slice(2 * g * 128, (2 * g + 1) * 128), slice((2 * g + 1) * 128, (2 * g + 2) * 128)
                q2 = jnp.concatenate([q_ref[:, t0], q_ref[:, t1]], axis=0)
                do2 = jnp.concatenate([do_ref[:, t0], do_ref[:, t1]], axis=0)
                prod = do2 * jnp.concatenate([o_ref[:, t0], o_ref[:, t1]], axis=0)
                do2_b = do2.astype(BF16)
                dq2 = jnp.zeros((2 * BLOCK, 128), F32)
                dk_h, dv_h = [], []
                for half, (kh, vh) in enumerate(((k_lo, v_lo), (k_hi, v_hi))):
                    h_top, h_bot = 4 * g + half, 4 * g + 2 + half
                    lse = jnp.concatenate([_lane_col(lse_c, h_top), _lane_col(lse_c, h_bot)], axis=0)
                    sink = _two_rows(sink_ref[h_top], sink_ref[h_bot])
                    delta = jnp.sum(jnp.where(lo == (half == 0), prod, 0.0), axis=1, keepdims=True)
                    s = jnp.where(mask, lax.dot_general(q2, kh, _NT, preferred_element_type=F32), -1e30)
                    p = jnp.exp(s - lse)
                    dp = lax.dot_general(do2_b, vh, _NT, preferred_element_type=F32)
                    ds_b = (p * (dp - delta)).astype(BF16)
                    p_b = p.astype(BF16)
                    dq2 = dq2 + jnp.dot(ds_b, kh, preferred_element_type=F32)
                    dk_h.append(lax.dot_general(ds_b, q2, _TN, preferred_element_type=F32))
                    dv_h.append(lax.dot_general(p_b, do2_b, _TN, preferred_element_type=F32))
                    gsink = -jnp.exp(sink - lse) * delta
                    row = lax.broadcasted_iota(jnp.int32, (2 * BLOCK, 1), 0)
                    gs = gs + jnp.where(lane == h_top, jnp.sum(jnp.where(row < BLOCK, gsink, 0.0)), 0.0)
                    gs = gs + jnp.where(lane == h_bot, jnp.sum(jnp.where(row >= BLOCK, gsink, 0.0)), 0.0)
                dq_parts += [dq2[:BLOCK], dq2[BLOCK:]]
                for acc, parts in ((dk_acc, dk_h), (dv_acc, dv_h)):
                    t = jnp.where(lo, parts[0], parts[1])
                    t = t + pltpu.roll(t, HEAD_DIM, 1)
                    acc[a] = acc[a] + jnp.where(lo == (b == 0), t, 0.0)
            dq_ref[...] = jnp.concatenate(dq_parts, axis=1)
            dk_full = jnp.concatenate(dk_acc, axis=1)
            dv_full = jnp.concatenate(dv_acc, axis=1)
            dk_ref[...] = ck_scr[...] + dk_full[:BLOCK]
            dv_ref[...] = cv_scr[...] + dv_full[:BLOCK]
            ck_scr[...] = dk_full[BLOCK:]
            cv_scr[...] = dv_full[BLOCK:]
            gs_ref[...] += gs

    cur = lambda i: jnp.minimum(i, nb - 1)
    prev = lambda i: jnp.maximum(jnp.minimum(i, nb - 1) - 1, 0)
    done = lambda i: jnp.maximum(i - 1, 0)
    bs = pl.BlockSpec
    return _call(
        body, "attn_bwd", (nb + 1,),
        [bs((BLOCK, ATTN_W), lambda i: (cur(i), 0)),
         bs((BLOCK, KV_W), lambda i: (cur(i), 0)), bs((BLOCK, KV_W), lambda i: (prev(i), 0)),
         bs((BLOCK, KV_W), lambda i: (cur(i), 5)), bs((BLOCK, KV_W), lambda i: (prev(i), 5)),
         bs((BLOCK, ATTN_W), lambda i: (cur(i), 0)), bs((BLOCK, ATTN_W), lambda i: (cur(i), 0)),
         bs((BLOCK, 128), lambda i: (cur(i), 0)), bs(memory_space=pltpu.SMEM)],
        [bs((BLOCK, ATTN_W), lambda i: (cur(i), 0)),
         bs((BLOCK, KV_W), lambda i: (done(i), 0)), bs((BLOCK, KV_W), lambda i: (done(i), 0)),
         bs((1, 128), lambda i: (0, 0))],
        [jax.ShapeDtypeStruct((L, ATTN_W), F32), jax.ShapeDtypeStruct((L, KV_W), F32),
         jax.ShapeDtypeStruct((L, KV_W), F32), jax.ShapeDtypeStruct((1, 128), F32)],
        (q, k, k, proj, proj, d_o, o, lse, sinks),
        [pltpu.VMEM((BLOCK, KV_W), F32), pltpu.VMEM((BLOCK, KV_W), F32)], ride)


def _qk_prep_bwd(proj, tab, qw, kw, d_q, d_k, d_v, d_za, d_u, d_zs):
    L = proj.shape[0]
    tm = _tile(L, 512)
    z0 = ATTN_W + 2 * KV_W

    def body(q_ref, k_ref, t_ref, qw_ref, kw_ref, dq_ref, dk_ref, dv_ref, dza_ref, du_ref, dzs_ref,
             out_ref, gq_ref, gk_ref):
        i = pl.program_id(0)

        @pl.when(i == 0)
        def _():
            gq_ref[...] = jnp.zeros_like(gq_ref)
            gk_ref[...] = jnp.zeros_like(gk_ref)

        cos, sin = t_ref[:, :128], t_ref[:, 128:]
        gq = jnp.zeros((1, 128), F32)
        gk = jnp.zeros((1, 128), F32)
        for c in range(ATTN_W // 128):
            cs = slice(c * 128, (c + 1) * 128)
            d_raw, gw = _norm_rope_bwd(dq_ref[:, cs] * _SCALE, q_ref[:, cs], qw_ref[...], cos, sin)
            out_ref[:, cs] = d_raw.astype(BF16)
            gq = gq + jnp.sum(gw, axis=0, keepdims=True)
        for c in range(KV_W // 128):
            cs = slice(c * 128, (c + 1) * 128)
            d_raw, gw = _norm_rope_bwd(dk_ref[:, cs], k_ref[:, cs], kw_ref[...], cos, sin)
            out_ref[:, ATTN_W + c * 128:ATTN_W + (c + 1) * 128] = d_raw.astype(BF16)
            gk = gk + jnp.sum(gw, axis=0, keepdims=True)
        out_ref[:, ATTN_W + KV_W:z0] = dv_ref[...].astype(BF16)
        out_ref[:, z0:z0 + ATTN_W] = dza_ref[...]
        out_ref[:, z0 + ATTN_W:z0 + ATTN_W + SSM_W] = du_ref[...].astype(BF16)
        out_ref[:, z0 + ATTN_W + SSM_W:] = dzs_ref[...]
        gq_ref[...] += gq
        gk_ref[...] += gk

    row = pl.BlockSpec((1, 128), lambda i: (0, 0))
    blk = lambda w, c: pl.BlockSpec((tm, w), lambda i: (i, c))
    return pl.pallas_call(
        body,
        name="qk_prep_bwd",
        grid=(L // tm,),
        in_specs=[blk(ATTN_W, 0), blk(KV_W, 4), blk(256, 0), row, row, blk(ATTN_W, 0), blk(KV_W, 0), blk(KV_W, 0),
                  blk(ATTN_W, 0), blk(SSM_W, 0), blk(SSM_W, 0)],
        out_specs=[blk(IN_W, 0), row, row],
        out_shape=[jax.ShapeDtypeStruct((L, IN_W), BF16), jax.ShapeDtypeStruct((1, 128), F32),
                   jax.ShapeDtypeStruct((1, 128), F32)],
        compiler_params=_cp(("arbitrary",)),
    )(proj, proj, tab, jnp.tile(qw, 2).reshape(1, 128), jnp.tile(kw, 2).reshape(1, 128), d_q, d_k, d_v,
      d_za, d_u, d_zs)


def _cmul(a, b):
    return a[0] * b[0] - a[1] * b[1], a[0] * b[1] + a[1] * b[0]


def _cmul_conj(a, b):
    return a[0] * b[0] + a[1] * b[1], a[1] * b[0] - a[0] * b[1]


def _cadd(a, b):
    return a[0] + b[0], a[1] + b[1]


def _dot3(a, b, dn):
    ah, bh = a.astype(BF16), b.astype(BF16)
    al, bl = (a - ah.astype(F32)).astype(BF16), (b - bh.astype(F32)).astype(BF16)
    d = lambda u, v: lax.dot_general(u, v, dn, preferred_element_type=F32)
    return d(ah, bh) + d(ah, bl) + d(al, bh)


def _s5_discretise(a_re, a_im, ls, cosx, sinx, bt):
    delta = jnp.exp(ls)
    er = jnp.exp(a_re * delta)
    lb = (er * cosx, er * sinx)
    den = a_re * a_re + a_im * a_im
    coef = _cmul_conj((lb[0] - 1.0, lb[1]), (a_re, a_im))
    coef = (coef[0] / den, coef[1] / den)
    return delta, lb, coef, den, _cmul(coef, bt)


def _powers(lb):
    pw = [(jnp.ones_like(lb[0]), jnp.zeros_like(lb[0]))]
    for _ in range(CHUNK):
        pw.append(_cmul(pw[-1], lb))
    return pw


def _block_rows(a, pw, idx):
    blocks = [_cmul(a, pw[i]) for i in idx]
    return (jnp.concatenate([b[0] for b in blocks], axis=-2), jnp.concatenate([b[1] for b in blocks], axis=-2))


def _block_rows_bwd(g, a, pw, idx, g_pw):
    g_a = (jnp.zeros_like(a[0]), jnp.zeros_like(a[0]))
    for j, i in enumerate(idx):
        gj = (g[0][..., j * SSM_H:(j + 1) * SSM_H, :], g[1][..., j * SSM_H:(j + 1) * SSM_H, :])
        g_a = _cadd(g_a, _cmul_conj(gj, pw[i]))
        gp = _cmul_conj(gj, a)
        g_pw[i] = _cadd(g_pw[i], (jnp.sum(gp[0], axis=-2, keepdims=True), jnp.sum(gp[1], axis=-2, keepdims=True)))
    return g_a


_IDX_S = [CHUNK - 1 - s for s in range(CHUNK)]
_IDX_O = [t + 1 for t in range(CHUNK)]
_IDX_K = list(range(CHUNK))
_PREP_IN = 9


def _prep_args(p):
    row = lambda t: t.reshape(SSM_G, 1, SSM_P)
    xi = p["a_im"] * jnp.exp(p["log_step"])[:, None]
    return (row(p["a_re"]), row(p["a_im"]), row(jnp.broadcast_to(p["log_step"][:, None], (SSM_G, SSM_P))),
            row(jnp.cos(xi)), row(jnp.sin(xi)), p["b_re"].transpose(0, 2, 1), p["b_im"].transpose(0, 2, 1),
            p["c_re"], p["c_im"])


PREP_GROUPS = 8


def _prep_specs():
    r1 = pl.BlockSpec((PREP_GROUPS, 1, SSM_P), lambda g: (g, 0, 0))
    r16 = pl.BlockSpec((PREP_GROUPS, SSM_H, SSM_P), lambda g: (g, 0, 0))
    return [r1] * 5 + [r16] * 4, r1, r16


def _ssm_prep(p):
    def one_group(q, are, aim, ls, cosx, sinx, btr, bti, cre, cim, mt_ref, s_ref, o_ref, a_ref):
        _, lb, _, _, bb = _s5_discretise(are[q], aim[q], ls[q], cosx[q], sinx[q], (btr[q], bti[q]))
        pw = _powers(lb)
        c = (cre[q], cim[q])
        sc = _block_rows(bb, pw, _IDX_S)
        ot = _block_rows(c, pw, _IDX_O)
        ok = _block_rows(c, pw, _IDX_K)
        s_ref[q] = jnp.concatenate([sc[0], sc[1]], axis=1).astype(BF16)
        o_ref[q] = jnp.concatenate([ot[0], -ot[1]], axis=1).astype(BF16)
        a_ref[q] = jnp.concatenate([pw[CHUNK][0], pw[CHUNK][1]], axis=1)
        kt = _dot3(jnp.concatenate([bb[0], -bb[1]], axis=1), jnp.concatenate([ok[0], ok[1]], axis=1), _NT)
        lane = lax.broadcasted_iota(jnp.int32, kt.shape, 1)
        for s in range(CHUNK):
            blk = kt if s == 0 else jnp.where(lane >= SSM_H * s, pltpu.roll(kt, SSM_H * s, 1), 0.0)
            mt_ref[q, s * SSM_H:(s + 1) * SSM_H, :] = blk.astype(BF16)

    def body(*refs):
        for q in range(PREP_GROUPS):
            one_group(q, *refs)

    in_specs, r1, _ = _prep_specs()
    g3 = lambda r, c: pl.BlockSpec((PREP_GROUPS, r, c), lambda g: (g, 0, 0))
    return pl.pallas_call(
        body,
        name="ssm_prep",
        grid=(SSM_G // PREP_GROUPS,),
        in_specs=in_specs,
        out_specs=[g3(CW, CW), g3(CW, 2 * SSM_P), g3(CW, 2 * SSM_P), g3(1, 2 * SSM_P)],
        out_shape=[jax.ShapeDtypeStruct((SSM_G, CW, CW), BF16), jax.ShapeDtypeStruct((SSM_G, CW, 2 * SSM_P), BF16),
                   jax.ShapeDtypeStruct((SSM_G, CW, 2 * SSM_P), BF16),
                   jax.ShapeDtypeStruct((SSM_G, 1, 2 * SSM_P), F32)],
        compiler_params=_cp(("parallel",)),
    )(*_prep_args(p))


def _ssm_prep_bwd(p, g_mt, g_scat, g_ocat, g_a16, ride):
    def body(are, aim, ls, cosx, sinx, btr, bti, cre, cim, gmt_ref, gs_ref, go_ref, ga_ref,
             g_are, g_aim, g_ls, g_btr, g_bti, g_cre, g_cim, ga1_scr, gb1_scr):
        lam = (are[...], aim[...])
        bt = (btr[...], bti[...])
        delta, lb, coef, den, bb = _s5_discretise(lam[0], lam[1], ls[...], cosx[...], sinx[...], bt)
        pw = _powers(lb)
        c = (cre[...], cim[...])
        ok = _block_rows(c, pw, _IDX_K)
        g_pw = [(jnp.zeros_like(lb[0]), jnp.zeros_like(lb[0])) for _ in range(CHUNK + 1)]
        lane = lax.broadcasted_iota(jnp.int32, (SSM_H, CW), 1)
        for q in range(PREP_GROUPS):
            g_kt = gmt_ref[q, :SSM_H, :]
            for s in range(1, CHUNK):
                blk = gmt_ref[q, s * SSM_H:(s + 1) * SSM_H, :]
                g_kt = g_kt + jnp.where(lane < CW - SSM_H * s, pltpu.roll(blk, CW - SSM_H * s, 1), 0.0)
            a1 = jnp.concatenate([bb[0][q], -bb[1][q]], axis=1)
            b1 = jnp.concatenate([ok[0][q], ok[1][q]], axis=1)
            ga1_scr[q] = _dot3(g_kt, b1, _NN)
            gb1_scr[q] = _dot3(g_kt, a1, _TN)
        g_a1, g_b1 = ga1_scr[...], gb1_scr[...]
        g_bb = (g_a1[..., :SSM_P], -g_a1[..., SSM_P:])
        g_c = _block_rows_bwd((g_b1[..., :SSM_P], g_b1[..., SSM_P:]), c, pw, _IDX_K, g_pw)
        gs = gs_ref[...]
        g_bb = _cadd(g_bb, _block_rows_bwd((gs[..., :SSM_P], gs[..., SSM_P:]), bb, pw, _IDX_S, g_pw))
        go = go_ref[...]
        g_c = _cadd(g_c, _block_rows_bwd((go[..., :SSM_P], -go[..., SSM_P:]), c, pw, _IDX_O, g_pw))
        ga = ga_ref[...]
        g_pw[CHUNK] = _cadd(g_pw[CHUNK], (ga[..., :SSM_P], ga[..., SSM_P:]))
        g_lb = (jnp.zeros_like(lb[0]), jnp.zeros_like(lb[0]))
        for l in range(CHUNK - 1, -1, -1):
            g_lb = _cadd(g_lb, _cmul_conj(g_pw[l + 1], pw[l]))
            g_pw[l] = _cadd(g_pw[l], _cmul_conj(g_pw[l + 1], lb))
        g_bt = _cmul_conj(g_bb, coef)
        gc = _cmul_conj(g_bb, bt)
        g_coef = (jnp.sum(gc[0], axis=-2, keepdims=True), jnp.sum(gc[1], axis=-2, keepdims=True))
        lam_den = (lam[0] / den, lam[1] / den)
        g_lb = _cadd(g_lb, _cmul(g_coef, lam_den))
        t = _cmul(_cmul_conj(g_coef, coef), lam_den)
        g_x = _cmul_conj(g_lb, lb)
        g_are[...] = g_x[0] * delta - t[0]
        g_aim[...] = g_x[1] * delta - t[1]
        g_ls[...] = (g_x[0] * lam[0] + g_x[1] * lam[1]) * delta
        g_btr[...] = g_bt[0]
        g_bti[...] = g_bt[1]
        g_cre[...] = g_c[0]
        g_cim[...] = g_c[1]

    in_specs, r1, r16 = _prep_specs()
    g3 = lambda r, c: pl.BlockSpec((PREP_GROUPS, r, c), lambda g: (g, 0, 0))
    rows = jax.ShapeDtypeStruct((SSM_G, 1, SSM_P), F32)
    mats = jax.ShapeDtypeStruct((SSM_G, SSM_H, SSM_P), F32)
    (g_are, g_aim, g_ls, g_btr, g_bti, g_cre, g_cim), landed = _call(
        body, "ssm_prep_bwd", (SSM_G // PREP_GROUPS,),
        in_specs + [g3(CW, CW), g3(CW, 2 * SSM_P), g3(CW, 2 * SSM_P), g3(1, 2 * SSM_P)],
        [r1] * 3 + [r16] * 4, [rows] * 3 + [mats] * 4, (*_prep_args(p), g_mt, g_scat, g_ocat, g_a16),
        [pltpu.VMEM((PREP_GROUPS, SSM_H, 2 * SSM_P), F32), pltpu.VMEM((PREP_GROUPS, CW, 2 * SSM_P), F32)], ride)
    grads = dict(a_re=g_are.reshape(SSM_G, SSM_P), a_im=g_aim.reshape(SSM_G, SSM_P),
                 log_step=jnp.sum(g_ls.reshape(SSM_G, SSM_P), axis=1),
                 b_re=g_btr.transpose(0, 2, 1), b_im=g_bti.transpose(0, 2, 1), c_re=g_cre, c_im=g_cim)
    return grads, landed


def _cmul_const(xv, ar, ai):
    return xv * ar + pltpu.roll(xv, SSM_P, 1) * ai


def _chunk_scan(inc, a_row, reverse):
    n = inc.shape[0]
    lane = lax.broadcasted_iota(jnp.int32, (1, 2 * SSM_P), 1)
    row = lax.broadcasted_iota(jnp.int32, inc.shape, 0)
    sign = jnp.where(lane < SSM_P, -1.0, 1.0)
    ar = jnp.where(lane < SSM_P, a_row, pltpu.roll(a_row, SSM_P, 1))
    ai = jnp.where(lane < SSM_P, pltpu.roll(a_row, SSM_P, 1), a_row)
    if reverse:
        ai = -ai
    xv = inc
    s = 1
    while s < n:
        if reverse:
            sh = jnp.where(row < n - s, pltpu.roll(xv, n - s, 0), 0.0)
        else:
            sh = jnp.where(row >= s, pltpu.roll(xv, s, 0), 0.0)
        xv = xv + _cmul_const(sh, ar, ai * sign)
        ar, ai = ar * ar - ai * ai, 2.0 * ar * ai
        s *= 2
    return xv


def _shift_rows(xv, reverse):
    n = xv.shape[0]
    row = lax.broadcasted_iota(jnp.int32, xv.shape, 0)
    if reverse:
        return jnp.where(row < n - 1, pltpu.roll(xv, n - 1, 0), 0.0)
    return jnp.where(row >= 1, pltpu.roll(xv, 1, 0), 0.0)


GB = 128 // SSM_H
U_COL0 = (ATTN_W + 2 * KV_W + ATTN_W) // 128


HALF = CHUNK // 2


def _chunk_perm():
    r = jnp.arange(HALF * 128)
    t, g8, h = r // 128, (r % 128) // SSM_H, r % SSM_H
    return ((g8 * 128 + t * SSM_H + h)[:, None] == jnp.arange(GB * 128)[None, :]).astype(BF16)


def _load_perm(p_hbm, p_scr, sem):
    @pl.when(pl.program_id(0) == 0)
    def _():
        cp = pltpu.make_async_copy(p_hbm, p_scr, sem)
        cp.start()
        cp.wait()


def _rows_to_chunks(pieces, perm):
    halves = [jnp.dot(jnp.concatenate(pieces[k * HALF:(k + 1) * HALF], axis=1).astype(BF16), perm,
                      preferred_element_type=F32).astype(BF16) for k in range(2)]
    return [jnp.concatenate([hv[:, g * 128:(g + 1) * 128] for hv in halves], axis=1) for g in range(GB)]


def _chunks_to_rows(groups, perm, two_pass):
    pieces = []
    for k in range(2):
        v = jnp.concatenate([gv[:, k * 128:(k + 1) * 128] for gv in groups], axis=1)
        hi = v.astype(BF16)
        out = lax.dot_general(hi, perm, _NT, preferred_element_type=F32)
        if two_pass:
            lo = (v - hi.astype(F32)).astype(BF16)
            out = out + lax.dot_general(lo, perm, _NT, preferred_element_type=F32)
        pieces += [out[:, t * 128:(t + 1) * 128] for t in range(HALF)]
    return pieces


def _ssm_fwd(proj, perm, mt, scat, ocat, a16, d_skip):
    L = proj.shape[0]
    nc = L // CHUNK

    def body(u_ref, p_hbm, mt_ref, s_ref, o_ref, a_ref, d_ref, y_ref, yg_ref, h_ref, p_scr, sem):
        _load_perm(p_hbm, p_scr, sem)
        perm = p_scr[...]
        rows = [pl.ds(t, nc, stride=CHUNK) for t in range(CHUNK)]
        ua = _rows_to_chunks([u_ref[r, :] for r in rows], perm)
        ys = []
        for g in range(GB):
            uv = ua[g]
            inc = jnp.dot(uv, s_ref[g], preferred_element_type=F32)
            hx = _shift_rows(_chunk_scan(inc, a_ref[g], False), False)
            h_ref[g] = hx
            ys.append(jnp.dot(uv, mt_ref[g], preferred_element_type=F32)
                      + lax.dot_general(hx.astype(BF16), o_ref[g], _NT, preferred_element_type=F32))
        yp = _chunks_to_rows(ys, perm, True)
        for t, r in enumerate(rows):
            y = yp[t] + d_ref[...] * u_ref[r, :]
            y_ref[r, :] = y
            yg_ref[r, :] = _gelu(y)

    g3 = lambda r, c: pl.BlockSpec((GB, r, c), lambda g: (g, 0, 0))
    col = pl.BlockSpec((L, 128), lambda g: (0, g))
    return pl.pallas_call(
        body,
        name="ssm_fwd",
        grid=(SSM_G // GB,),
        in_specs=[pl.BlockSpec((L, 128), lambda g: (0, U_COL0 + g)), _ANY,
                  g3(CW, CW), g3(CW, 2 * SSM_P), g3(CW, 2 * SSM_P), g3(1, 2 * SSM_P),
                  pl.BlockSpec((1, 128), lambda g: (0, g))],
        out_specs=[col, col, g3(nc, 2 * SSM_P)],
        out_shape=[jax.ShapeDtypeStruct((L, SSM_W), F32), jax.ShapeDtypeStruct((L, SSM_W), F32),
                   jax.ShapeDtypeStruct((SSM_G, nc, 2 * SSM_P), F32)],
        scratch_shapes=[pltpu.VMEM((HALF * 128, GB * 128), BF16), pltpu.SemaphoreType.DMA],
        compiler_params=_cp(("arbitrary",)),
    )(proj, perm, mt, scat, ocat, a16, d_skip.reshape(1, SSM_W))


def _ssm_bwd(d_yg, y, proj, hx, perm, mt, scat, ocat, a16, d_skip, ride):
    L = proj.shape[0]
    nc = L // CHUNK

    def body(dg_ref, y_ref, u_ref, h_ref, p_hbm, mt_ref, s_ref, o_ref, a_ref, d_ref,
             du_ref, gmt_ref, gs_ref, go_ref, ga_ref, gd_ref, p_scr, sem):
        _load_perm(p_hbm, p_scr, sem)
        perm = p_scr[...]
        rows = [pl.ds(t, nc, stride=CHUNK) for t in range(CHUNK)]
        us = [u_ref[r, :] for r in rows]
        dys = [dg_ref[r, :] * _dgelu(y_ref[r, :]) for r in rows]
        gd = jnp.zeros((1, 128), F32)
        for uv, dy in zip(us, dys):
            gd = gd + jnp.sum(dy * uv, axis=0, keepdims=True)
        gd_ref[...] = gd
        ua = _rows_to_chunks(us, perm)
        dya = _rows_to_chunks(dys, perm)
        lane = lax.broadcasted_iota(jnp.int32, (1, 2 * SSM_P), 1)
        dus = []
        for g in range(GB):
            uv, dy, hx_v = ua[g], dya[g], h_ref[g]
            dh = jnp.dot(dy, o_ref[g], preferred_element_type=F32)
            dinc = _shift_rows(_chunk_scan(dh, a_ref[g], True), True)
            dinc_b = dinc.astype(BF16)
            dus.append(lax.dot_general(dy, mt_ref[g], _NT, preferred_element_type=F32)
                       + lax.dot_general(dinc_b, s_ref[g], _NT, preferred_element_type=F32))
            gmt_ref[g] = lax.dot_general(uv, dy, _TN, preferred_element_type=F32)
            gs_ref[g] = lax.dot_general(uv, dinc_b, _TN, preferred_element_type=F32)
            go_ref[g] = lax.dot_general(dy, hx_v.astype(BF16), _TN, preferred_element_type=F32)
            p1 = dinc * hx_v
            p2 = pltpu.roll(dinc, SSM_P, 1) * hx_v
            t1 = jnp.sum(p1 + pltpu.roll(p1, SSM_P, 1), axis=0, keepdims=True)
            t2 = jnp.sum(p2 - pltpu.roll(p2, SSM_P, 1), axis=0, keepdims=True)
            ga_ref[g] = jnp.where(lane < SSM_P, t1, pltpu.roll(t2, SSM_P, 1))
        dup = _chunks_to_rows(dus, perm, False)
        for t, r in enumerate(rows):
            du_ref[r, :] = dup[t] + d_ref[...] * dys[t]

    g3 = lambda r, c: pl.BlockSpec((GB, r, c), lambda g: (g, 0, 0))
    col = pl.BlockSpec((L, 128), lambda g: (0, g))
    row = pl.BlockSpec((1, 128), lambda g: (0, g))
    return _call(
        body, "ssm_bwd", (SSM_G // GB,),
        [col, col, pl.BlockSpec((L, 128), lambda g: (0, U_COL0 + g)), g3(nc, 2 * SSM_P), _ANY,
         g3(CW, CW), g3(CW, 2 * SSM_P), g3(CW, 2 * SSM_P), g3(1, 2 * SSM_P), row],
        [col, g3(CW, CW), g3(CW, 2 * SSM_P), g3(CW, 2 * SSM_P), g3(1, 2 * SSM_P), row],
        [jax.ShapeDtypeStruct((L, SSM_W), F32), jax.ShapeDtypeStruct((SSM_G, CW, CW), F32),
         jax.ShapeDtypeStruct((SSM_G, CW, 2 * SSM_P), F32), jax.ShapeDtypeStruct((SSM_G, CW, 2 * SSM_P), F32),
         jax.ShapeDtypeStruct((SSM_G, 1, 2 * SSM_P), F32), jax.ShapeDtypeStruct((1, SSM_W), F32)],
        (d_yg, y, proj, hx, perm, mt, scat, ocat, a16, d_skip.reshape(1, SSM_W)),
        [pltpu.VMEM((HALF * 128, GB * 128), BF16), pltpu.SemaphoreType.DMA], ride)


def _merge(og, yg, gpre, proj, b_glu, wa, ws):
    L = og.shape[0]
    tm = _tile(L, 256)

    def body(og_ref, yg_ref, gp_ref, z0_ref, z1_ref, b_ref, wa_ref, ws_ref, m_ref):
        zs = jnp.concatenate([z0_ref[...], z1_ref[...]], axis=1)
        os_ = yg_ref[...] * _sigmoid(gp_ref[...] + b_ref[...]) * _silu(zs)
        ogv = og_ref[...]
        ra = lax.rsqrt(jnp.mean(ogv * ogv, axis=-1, keepdims=True) + NORM_EPS)
        rs = lax.rsqrt(jnp.mean(os_ * os_, axis=-1, keepdims=True) + NORM_EPS)
        m_ref[:, :ATTN_W] = (ogv * ra * wa_ref[...]).astype(BF16)
        m_ref[:, ATTN_W:] = (os_ * rs * ws_ref[...]).astype(BF16)

    row = lambda w: pl.BlockSpec((1, w), lambda i: (0, 0))
    return pl.pallas_call(
        body,
        name="merge",
        grid=(L // tm,),
        in_specs=[pl.BlockSpec((tm, ATTN_W), lambda i: (i, 0)), pl.BlockSpec((tm, SSM_W), lambda i: (i, 0)),
                  pl.BlockSpec((tm, SSM_W), lambda i: (i, 0)),
                  pl.BlockSpec((tm, 512), lambda i: (i, 7)), pl.BlockSpec((tm, 512), lambda i: (i, 8)),
                  row(SSM_W), row(ATTN_W), row(SSM_W)],
        out_specs=pl.BlockSpec((tm, D_MODEL), lambda i: (i, 0)),
        out_shape=jax.ShapeDtypeStruct((L, D_MODEL), BF16),
        compiler_params=_cp(("parallel",)),
    )(og, yg, gpre, proj, proj, b_glu.reshape(1, SSM_W), wa.reshape(1, ATTN_W), ws.reshape(1, SSM_W))


def _outproj_loss(merged, w_out, x, target):
    L = x.shape[0]
    tm, tn = _tile(L, 512), 1024
    ni, nj = L // tm, D_MODEL // tn

    def body(m_ref, w_ref, x_ref, t_ref, d_ref, db_ref, l_ref):
        out = x_ref[...] + jnp.dot(m_ref[...], w_ref[...], preferred_element_type=F32)
        diff = out - t_ref[...]
        d = diff * (1.0 / D_MODEL)
        d_ref[...] = d
        db_ref[...] = d.astype(BF16)
        l_ref[...] = jnp.full((1, 8, 128), jnp.sum(diff * diff), F32)

    return pl.pallas_call(
        body,
        name="outproj_loss",
        grid=(nj, ni),
        in_specs=[pl.BlockSpec((tm, D_MODEL), lambda j, i: (i, 0)),
                  pl.BlockSpec((D_MODEL, tn), lambda j, i: (0, j)),
                  pl.BlockSpec((tm, tn), lambda j, i: (i, j)),
                  pl.BlockSpec((tm, tn), lambda j, i: (i, j))],
        out_specs=[pl.BlockSpec((tm, tn), lambda j, i: (i, j)), pl.BlockSpec((tm, tn), lambda j, i: (i, j)),
                   pl.BlockSpec((1, 8, 128), lambda j, i: (i * nj + j, 0, 0))],
        out_shape=[jax.ShapeDtypeStruct((L, D_MODEL), F32), jax.ShapeDtypeStruct((L, D_MODEL), BF16),
                   jax.ShapeDtypeStruct((ni * nj, 8, 128), F32)],
        compiler_params=_cp(("parallel", "parallel")),
    )(merged, w_out, x, target)


def _merge_bwd(d_m, og, o, yg, gpre, proj, b_glu, wa, ws):
    L = og.shape[0]
    tm = _tile(L, 256)

    def body(dm_ref, og_ref, o_ref, yg_ref, gp_ref, za0_ref, za1_ref, zs0_ref, zs1_ref, b_ref, wa_ref, ws_ref,
             do_ref, dza_ref, dzs_ref, dg_ref, dyg_ref, gwa_ref, gws_ref, gb_ref):
        i = pl.program_id(0)

        @pl.when(i == 0)
        def _():
            gwa_ref[...] = jnp.zeros_like(gwa_ref)
            gws_ref[...] = jnp.zeros_like(gws_ref)
            gb_ref[...] = jnp.zeros_like(gb_ref)

        za = jnp.concatenate([za0_ref[...], za1_ref[...]], axis=1)
        zs = jnp.concatenate([zs0_ref[...], zs1_ref[...]], axis=1)
        ogv, dma = og_ref[...], dm_ref[:, :ATTN_W]
        ra = lax.rsqrt(jnp.mean(ogv * ogv, axis=-1, keepdims=True) + NORM_EPS)
        xh = ogv * ra
        gwa_ref[...] += jnp.sum(dma * xh, axis=0, keepdims=True)
        gx = dma * wa_ref[...]
        d_og = ra * (gx - xh * jnp.mean(gx * xh, axis=-1, keepdims=True))
        do_ref[...] = d_og * _silu(za)
        dza_ref[...] = (d_og * o_ref[...] * _dsilu(za)).astype(BF16)
        ygv = yg_ref[...]
        sg = _sigmoid(gp_ref[...] + b_ref[...])
        y2 = ygv * sg
        sz = _silu(zs)
        os_ = y2 * sz
        dms = dm_ref[:, ATTN_W:]
        rs = lax.rsqrt(jnp.mean(os_ * os_, axis=-1, keepdims=True) + NORM_EPS)
        xs = os_ * rs
        gws_ref[...] += jnp.sum(dms * xs, axis=0, keepdims=True)
        gxs = dms * ws_ref[...]
        d_os = rs * (gxs - xs * jnp.mean(gxs * xs, axis=-1, keepdims=True))
        dzs_ref[...] = (d_os * y2 * _dsilu(zs)).astype(BF16)
        d_y2 = d_os * sz
        d_g = d_y2 * ygv * sg * (1.0 - sg)
        dg_ref[...] = d_g.astype(BF16)
        gb_ref[...] += jnp.sum(d_g, axis=0, keepdims=True)
        dyg_ref[...] = d_y2 * sg

    row = lambda w: pl.BlockSpec((1, w), lambda i: (0, 0))
    full = lambda w: pl.BlockSpec((tm, w), lambda i: (i, 0))
    half = lambda c: pl.BlockSpec((tm, 512), lambda i: (i, c))
    return pl.pallas_call(
        body,
        name="merge_bwd",
        grid=(L // tm,),
        in_specs=[full(D_MODEL), full(ATTN_W), full(ATTN_W), full(SSM_W), full(SSM_W),
                  half(3), half(4), half(7), half(8), row(SSM_W), row(ATTN_W), row(SSM_W)],
        out_specs=[full(ATTN_W), full(ATTN_W), full(SSM_W), full(SSM_W), full(SSM_W),
                   row(ATTN_W), row(SSM_W), row(SSM_W)],
        out_shape=[jax.ShapeDtypeStruct((L, ATTN_W), F32), jax.ShapeDtypeStruct((L, ATTN_W), BF16),
                   jax.ShapeDtypeStruct((L, SSM_W), BF16), jax.ShapeDtypeStruct((L, SSM_W), BF16),
                   jax.ShapeDtypeStruct((L, SSM_W), F32),
                   jax.ShapeDtypeStruct((1, ATTN_W), F32), jax.ShapeDtypeStruct((1, SSM_W), F32),
                   jax.ShapeDtypeStruct((1, SSM_W), F32)],
        compiler_params=_cp(("arbitrary",)),
    )(d_m, og, o, yg, gpre, proj, proj, proj, proj, b_glu.reshape(1, SSM_W), wa.reshape(1, ATTN_W),
      ws.reshape(1, SSM_W))


def _rms_bwd_x(x, norm_w, d_hn, d_out, ride):
    L = x.shape[0]
    tm = _tile(L, 256)

    def body(x_ref, w_ref, dh_ref, do_ref, gx_ref, gw_ref):
        i = pl.program_id(0)

        @pl.when(i == 0)
        def _():
            gw_ref[...] = jnp.zeros_like(gw_ref)

        xv, dh = x_ref[...], dh_ref[...]
        r = lax.rsqrt(jnp.mean(xv * xv, axis=-1, keepdims=True) + NORM_EPS)
        xh = xv * r
        gw_ref[...] += jnp.sum(dh * xh, axis=0, keepdims=True)
        gx = dh * w_ref[...]
        gx_ref[...] = do_ref[...] + r * (gx - xh * jnp.mean(gx * xh, axis=-1, keepdims=True))

    blk = pl.BlockSpec((tm, D_MODEL), lambda i: (i, 0))
    row = pl.BlockSpec((1, D_MODEL), lambda i: (0, 0))
    return _call(body, "rms_bwd_x", (L // tm,), [blk, row, blk, blk], [blk, row],
                 [jax.ShapeDtypeStruct((L, D_MODEL), F32), jax.ShapeDtypeStruct((1, D_MODEL), F32)],
                 (x, norm_w.reshape(1, D_MODEL), d_hn, d_out), ride=ride)


def _rope_table(positions):
    inv_freq = ROPE_THETA ** (-jnp.arange(0, HEAD_DIM, 2, dtype=F32) / HEAD_DIM)
    ang = positions.astype(F32)[:, None] * inv_freq
    c, s = jnp.cos(ang), jnp.sin(ang)
    return jnp.concatenate([c, c, c, c, -s, s, -s, s], axis=1)


def _step(x, positions, target, w, core, chip):
    small = {n: w[n] for n in _SMALL}
    tab = _rope_table(positions)
    mt_b, scat_b, ocat_b, a16 = _ssm_prep(small)
    perm = _chunk_perm()
    blocks = lambda t: t.reshape(N_DEV, t.shape[0] // N_DEV, t.shape[1])

    (wt_in,) = _run_exchange(_gather_exchange([w["w_in"].T.astype(BF16)]), "gather_w_in")
    wt_in = wt_in.reshape(IN_W, D_MODEL)

    (proj, hn), (w_glu, w_out) = _rms_inproj(
        x, small["norm_w"], wt_in, _gather_exchange([w["w_glu"].astype(BF16), w["w_out"].astype(BF16)]))
    w_glu, w_out = w_glu.reshape(SSM_W, SSM_W), w_out.reshape(D_MODEL, D_MODEL)
    q_rot, k_rot = _qk_prep(proj, tab, small["q_norm_w"], small["k_norm_w"])
    og, o, lse = _attn_fwd(q_rot, k_rot, proj, small["sinks"])
    y, yg, hx = _ssm_fwd(proj, perm, mt_b, scat_b, ocat_b, a16, small["d_skip"])
    gpre = _mm(yg, w_glu, "nn", F32, "glu_fwd")
    merged = _merge(og, yg, gpre, proj, small["b_glu"], small["attn_out_norm_w"], small["ssm_out_norm_w"])
    d_out, d_out_b, loss_parts = _outproj_loss(merged, w_out, x, target)
    loss = 0.5 * jnp.sum(loss_parts[:, 0, 0]) / D_MODEL

    g_w_out = blocks(_mm(merged, d_out_b, "tn", F32, "grad_w_out"))
    d_m = _mm(d_out_b, w_out, "nt", F32, "d_merged")
    d_o, d_za, d_zs, d_g, d_yg1, g_wa, g_ws, g_bglu = _merge_bwd(
        d_m, og, o, yg, gpre, proj, small["b_glu"], small["attn_out_norm_w"], small["ssm_out_norm_w"])
    g_w_glu = blocks(_mm(yg, d_g, "tn", F32, "grad_w_glu"))
    d_yg = _mm(d_g, w_glu, "nt", F32, "d_yg", add=d_yg1)
    (d_u, g_mt, g_scat, g_ocat, g_a16, g_dskip), (ra_out, ra_glu) = _ssm_bwd(
        d_yg, y, proj, hx, perm, mt_b, scat_b, ocat_b, a16, small["d_skip"], _pair_exchange([g_w_out, g_w_glu]))
    p_out = _pair_sum(g_w_out, ra_out, core, BF16, "pair_sum_out")
    p_glu = _pair_sum(g_w_glu, ra_glu, core, BF16, "pair_sum_glu")
    (d_q, d_k, d_v, g_sinks), (rb_out, rb_glu) = _attn_bwd(
        q_rot, k_rot, proj, small["sinks"], d_o, o, lse, _chip_exchange([p_out, p_glu]))
    d_proj, g_qw, g_kw = _qk_prep_bwd(proj, tab, small["q_norm_w"], small["k_norm_w"], d_q, d_k, d_v,
                                      d_za, d_u, d_zs)
    g_qw = g_qw[0, :HEAD_DIM] + g_qw[0, HEAD_DIM:]
    g_kw = g_kw[0, :HEAD_DIM] + g_kw[0, HEAD_DIM:]
    g_in_a = blocks(_mm(d_proj, hn, "tn", F32, "grad_w_in_a", panel=0))
    g_in_b, (ra_a,) = _mm(d_proj, hn, "tn", F32, "grad_w_in_b", panel=1, ride=_pair_exchange([g_in_a]))
    g_in_b = blocks(g_in_b)
    p_a = _pair_sum(g_in_a, ra_a, core, BF16, "pair_sum_in_a")
    d_hn, (rb_a, ra_b) = _mm(d_proj, wt_in, "nn", F32, "d_hn",
                             ride=_both(_chip_exchange([p_a]), _pair_exchange([g_in_b])))
    p_b = _pair_sum(g_in_b, ra_b, core, BF16, "pair_sum_in_b")
    g_small, (rb_b,) = _ssm_prep_bwd(small, g_mt, g_scat, g_ocat, g_a16, _chip_exchange([p_b]))
    (grad_x, g_nw), _ = _rms_bwd_x(x, small["norm_w"], d_hn, d_out, None)
    g_wt_in = jnp.concatenate([_chip_sum(p_a, rb_a, chip, "chip_sum_in_a"),
                               _chip_sum(p_b, rb_b, chip, "chip_sum_in_b")], axis=1)

    g_small.update(norm_w=g_nw.reshape(-1), q_norm_w=g_qw.reshape(-1), k_norm_w=g_kw.reshape(-1),
                   sinks=g_sinks[0, :N_HEADS], d_skip=g_dskip.reshape(-1), b_glu=g_bglu.reshape(-1),
                   attn_out_norm_w=g_wa.reshape(-1), ssm_out_norm_w=g_ws.reshape(-1))
    slab = _pack(g_small).reshape(N_DEV, _PACK_ROWS // N_DEV, 128)
    (ra_s,) = _run_exchange(_pair_exchange([slab]), "pair_exchange_small")
    p_s = _pair_sum(slab, ra_s, core, F32, "pair_sum_small")
    (rb_s,) = _run_exchange(_chip_exchange([p_s]), "chip_exchange_small")
    (g_packed,) = _run_exchange(_gather_exchange([_chip_sum(p_s, rb_s, chip, "chip_sum_small")]), "gather_small")

    grads = _unpack(g_packed.reshape(_PACK_ROWS, 128), w)
    grads.update(w_in=g_wt_in.T,
                 w_glu=_chip_sum(p_glu, rb_glu, chip, "chip_sum_glu"),
                 w_out=_chip_sum(p_out, rb_out, chip, "chip_sum_out"))
    return loss, grad_x, grads


_ANY = pl.BlockSpec(memory_space=pl.ANY)


class _Exchange:
    def __init__(self, arrays, out_shape, sems, start, finish):
        self.arrays, self.out_shape, self.sems, self.start, self.finish = arrays, out_shape, sems, start, finish


def _gather_exchange(blocks):
    n = len(blocks)

    def parts(ins, outs, sems):
        send_sems, recv_sems, local_sems = sems
        x, y, c = lax.axis_index("x"), lax.axis_index("y"), lax.axis_index("c")
        me, sibling = (x, y, c), (x, y, 1 - c)
        chips = [(1 - x, y), (x, 1 - y), (1 - x, 1 - y)]

        def slot(k, dev):
            return outs[k].at[4 * dev[0] + 2 * dev[1] + dev[2]]

        def copy(k, q, block, to, src=None):
            return pltpu.make_async_remote_copy(
                src_ref=slot(k, block) if src is None else src, dst_ref=slot(k, block),
                send_sem=send_sems.at[k, q], recv_sem=recv_sems.at[k, q], device_id=to, device_id_type=MESH)

        mine = [pltpu.make_async_copy(ins[k], slot(k, me), local_sems.at[k]) for k in range(n)]
        first = []
        for k in range(n):
            first.append(copy(k, 0, me, sibling, src=ins[k]))
            first += [copy(k, 1 + j, me, (*chip, c), src=ins[k]) for j, chip in enumerate(chips)]
        return me, sibling, chips, c, copy, mine, first

    def start(ins, outs, sems):
        *_, mine, first = parts(ins, outs, sems)
        for cp in mine + first:
            cp.start()

    def finish(ins, outs, sems):
        me, sibling, chips, c, copy, mine, first = parts(ins, outs, sems)
        passed = []
        for j, chip in enumerate(chips):
            for k in range(n):
                copy(k, 1 + j, (*chip, c), me).wait_recv()
                fwd = copy(k, 4 + j, (*chip, c), sibling)
                fwd.start()
                passed.append(fwd)
        for k in range(n):
            copy(k, 0, sibling, me).wait_recv()
            for j, chip in enumerate(chips):
                copy(k, 4 + j, (*chip, 1 - c), me).wait_recv()
        for cp in first + passed:
            cp.wait_send()
        for cp in mine:
            cp.wait()

    return _Exchange(blocks, [jax.ShapeDtypeStruct((N_DEV,) + b.shape, b.dtype) for b in blocks],
                     [pltpu.SemaphoreType.DMA((n, 7)), pltpu.SemaphoreType.DMA((n, 7)), pltpu.SemaphoreType.DMA((n,))],
                     start, finish)


def _direct_exchange(arrays, out_lead, fan, route):
    n = len(arrays)

    def copies(ins, outs, sems):
        send_sems, recv_sems = sems
        legs = route(lax.axis_index("x"), lax.axis_index("y"), lax.axis_index("c"))
        return [pltpu.make_async_remote_copy(
            src_ref=ins[k].at[src], dst_ref=outs[k].at[q], send_sem=send_sems.at[k, q], recv_sem=recv_sems.at[k, q],
            device_id=to, device_id_type=MESH) for k in range(n) for src, q, to in legs]

    def start(ins, outs, sems):
        for cp in copies(ins, outs, sems):
            cp.start()

    def finish(ins, outs, sems):
        for cp in copies(ins, outs, sems):
            cp.wait()

    return _Exchange(arrays, [jax.ShapeDtypeStruct((out_lead,) + a.shape[1:], a.dtype) for a in arrays],
                     [pltpu.SemaphoreType.DMA((n, fan)), pltpu.SemaphoreType.DMA((n, fan))], start, finish)


def _pair_exchange(grads):
    return _direct_exchange(grads, 4, 4, lambda x, y, c: [(2 * chip + (1 - c), chip, (x, y, 1 - c))
                                                          for chip in range(4)])


def _chip_exchange(parts):
    def route(x, y, c):
        chips = [(1 - x, y), (x, 1 - y), (1 - x, 1 - y)]
        return [(2 * chip[0] + chip[1], q, (*chip, c)) for q, chip in enumerate(chips)]
    return _direct_exchange(parts, 3, 3, route)


def _both(ex1, ex2):
    n1, s1 = len(ex1.arrays), len(ex1.sems)

    def halves(ins, outs, sems):
        return (ins[:n1], outs[:n1], sems[:s1]), (ins[n1:], outs[n1:], sems[s1:])

    def start(ins, outs, sems):
        h1, h2 = halves(ins, outs, sems)
        ex1.start(*h1)
        ex2.start(*h2)

    def finish(ins, outs, sems):
        h1, h2 = halves(ins, outs, sems)
        ex1.finish(*h1)
        ex2.finish(*h2)

    return _Exchange(list(ex1.arrays) + list(ex2.arrays), list(ex1.out_shape) + list(ex2.out_shape),
                     list(ex1.sems) + list(ex2.sems), start, finish)


def _run_exchange(ex, name):
    n = len(ex.arrays)

    def body(*refs):
        ins, outs, sems = refs[:n], refs[n:2 * n], refs[2 * n:]
        ex.start(ins, outs, sems)
        ex.finish(ins, outs, sems)

    return list(pl.pallas_call(body, name=name, in_specs=[_ANY] * n, out_specs=[_ANY] * n, out_shape=ex.out_shape,
                               scratch_shapes=ex.sems)(*ex.arrays))


def _call(body, name, grid, in_specs, out_specs, out_shape, args, scratch_shapes=(), ride=None):
    if ride is None:
        sem = ("arbitrary",) * len(grid)
        return pl.pallas_call(body, name=name, grid=grid, in_specs=in_specs, out_specs=out_specs, out_shape=out_shape,
                              scratch_shapes=list(scratch_shapes), compiler_params=_cp(sem))(*args), None
    n_in, n_out, n_scr, n_x = len(in_specs), len(out_specs), len(scratch_shapes), len(ride.arrays)

    def wrapped(*refs):
        ins, refs = refs[:n_in], refs[n_in:]
        x_in, refs = refs[:n_x], refs[n_x:]
        outs, refs = refs[:n_out], refs[n_out:]
        x_out, refs = refs[:n_x], refs[n_x:]
        scr, sems = refs[:n_scr], refs[n_scr:]
        first = pl.program_id(0) == 0
        last = pl.program_id(0) == grid[0] - 1
        for a in range(1, len(grid)):
            first = jnp.logical_and(first, pl.program_id(a) == 0)
            last = jnp.logical_and(last, pl.program_id(a) == grid[a] - 1)

        @pl.when(first)
        def _():
            ride.start(x_in, x_out, sems)

        body(*ins, *outs, *scr)

        @pl.when(last)
        def _():
            ride.finish(x_in, x_out, sems)

    res = pl.pallas_call(
        wrapped, name=name, grid=grid, in_specs=list(in_specs) + [_ANY] * n_x,
        out_specs=list(out_specs) + [_ANY] * n_x, out_shape=list(out_shape) + list(ride.out_shape),
        scratch_shapes=list(scratch_shapes) + list(ride.sems),
        compiler_params=_cp(("arbitrary",) * len(grid)))(*args, *ride.arrays)
    return res[:n_out], list(res[n_out:])


def _pair_sum(g, ra, core, out_dtype, name):
    _, r, C = g.shape
    tr = _tile(r, 128)

    def body(c_ref, g_ref, ra_ref, p_ref):
        p_ref[...] = (g_ref[...] + ra_ref[...]).astype(p_ref.dtype)

    return pl.pallas_call(
        body,
        name=name,
        grid_spec=pltpu.PrefetchScalarGridSpec(
            num_scalar_prefetch=1,
            grid=(4, r // tr),
            in_specs=[pl.BlockSpec((1, tr, C), lambda j, t, c_ref: (2 * j + c_ref[0], t, 0)),
                      pl.BlockSpec((1, tr, C), lambda j, t, c_ref: (j, t, 0))],
            out_specs=pl.BlockSpec((1, tr, C), lambda j, t, c_ref: (j, t, 0)),
        ),
        out_shape=jax.ShapeDtypeStruct((4, r, C), out_dtype),
        compiler_params=_cp(("parallel", "parallel")),
    )(core, g, ra)


def _chip_sum(p, rb, chip, name):
    _, r, C = p.shape
    tr = _tile(r, 128)

    def body(c_ref, p_ref, rb_ref, o_ref):
        acc = p_ref[0].astype(F32) + rb_ref[0].astype(F32)
        acc = acc + rb_ref[1].astype(F32)
        o_ref[...] = acc + rb_ref[2].astype(F32)

    return pl.pallas_call(
        body,
        name=name,
        grid_spec=pltpu.PrefetchScalarGridSpec(
            num_scalar_prefetch=1,
            grid=(r // tr,),
            in_specs=[pl.BlockSpec((1, tr, C), lambda t, c_ref: (c_ref[0], t, 0)),
                      pl.BlockSpec((3, tr, C), lambda t, c_ref: (0, t, 0))],
            out_specs=pl.BlockSpec((tr, C), lambda t, c_ref: (t, 0)),
        ),
        out_shape=jax.ShapeDtypeStruct((r, C), F32),
        compiler_params=_cp(("parallel",)),
    )(chip, p, rb)


def _adamw(g, w, m, v, name):
    R, C = g.shape
    tr = _tile(R, 256)
    c1 = 1.0 - ADAM_B1 ** ADAM_STEP
    c2 = 1.0 - ADAM_B2 ** ADAM_STEP

    def body(g_ref, w_ref, m_ref, v_ref, d_ref, nm_ref, nv_ref):
        gv = g_ref[...]
        nm = ADAM_B1 * m_ref[...] + (1.0 - ADAM_B1) * gv
        nv = ADAM_B2 * v_ref[...] + (1.0 - ADAM_B2) * (gv * gv)
        nm_ref[...] = nm
        nv_ref[...] = nv
        d_ref[...] = -ADAM_LR * ((nm / c1) / (jnp.sqrt(nv / c2) + ADAM_EPS) + ADAM_WD * w_ref[...])

    blk = pl.BlockSpec((tr, C), lambda i: (i, 0))
    return pl.pallas_call(
        body, name=name, grid=(R // tr,), in_specs=[blk] * 4, out_specs=[blk] * 3,
        out_shape=[jax.ShapeDtypeStruct((R, C), F32)] * 3, compiler_params=_cp(("parallel",)),
    )(g, w, m, v)


_SMALL = ("norm_w", "q_norm_w", "k_norm_w", "sinks", "a_re", "a_im", "log_step", "b_re", "b_im", "c_re", "c_im",
          "d_skip", "b_glu", "attn_out_norm_w", "ssm_out_norm_w")
_WEIGHTS = ("norm_w", "w_in", "q_norm_w", "k_norm_w", "sinks", "a_re", "a_im", "log_step", "b_re", "b_im", "c_re",
            "c_im", "d_skip", "w_glu", "b_glu", "attn_out_norm_w", "ssm_out_norm_w", "w_out")
_SMALL_2D = dict(norm_w=(1, 2048), q_norm_w=(1, 64), k_norm_w=(1, 64), sinks=(1, 16), a_re=(64, 64), a_im=(64, 64),
                 log_step=(1, 64), b_re=(4096, 16), b_im=(4096, 16), c_re=(1024, 64), c_im=(1024, 64),
                 d_skip=(1, 1024), b_glu=(1, 1024), attn_out_norm_w=(1, 1024), ssm_out_norm_w=(1, 1024))


def _slab_rows(n):
    return -(-n // 1024) * 8


_PACK_ROWS = 2304


def _pack(d):
    parts = []
    for n in _SMALL:
        flat = d[n].reshape(-1).astype(F32)
        rows = _slab_rows(flat.shape[0])
        parts.append(jnp.pad(flat, (0, rows * 128 - flat.shape[0])).reshape(rows, 128))
    used = sum(p.shape[0] for p in parts)
    parts.append(jnp.zeros((_PACK_ROWS - used, 128), F32))
    return jnp.concatenate(parts, axis=0)


def _unpack(packed, like):
    out, off = {}, 0
    for n in _SMALL:
        size = math.prod(like[n].shape)
        rows = _slab_rows(size)
        out[n] = packed[off:off + rows].reshape(-1)[:size].reshape(like[n].shape)
        off += rows
    return out


def _adamw_small(g, w, m, v):
    c1 = 1.0 - ADAM_B1 ** ADAM_STEP
    c2 = 1.0 - ADAM_B2 ** ADAM_STEP
    k = len(_SMALL)

    def body(*refs):
        ins, outs = refs[:4 * k], refs[4 * k:]
        for j in range(k):
            gv, wv, mv, vv = (ins[q * k + j][...] for q in range(4))
            nm = ADAM_B1 * mv + (1.0 - ADAM_B1) * gv
            nv = ADAM_B2 * vv + (1.0 - ADAM_B2) * (gv * gv)
            outs[j][...] = -ADAM_LR * ((nm / c1) / (jnp.sqrt(nv / c2) + ADAM_EPS) + ADAM_WD * wv)
            outs[k + j][...] = nm
            outs[2 * k + j][...] = nv

    args = [d[n].reshape(_SMALL_2D[n]) for d in (g, w, m, v) for n in _SMALL]
    shapes = [jax.ShapeDtypeStruct(_SMALL_2D[n], F32) for _ in range(3) for n in _SMALL]
    outs = pl.pallas_call(body, name="adamw_small", out_shape=shapes, compiler_params=_cp())(*args)
    res = []
    for q in range(3):
        res.append({n: outs[q * k + j].reshape(w[n].shape) for j, n in enumerate(_SMALL)})
    return res


def kernel(x, positions, norm_w, w_in, q_norm_w, k_norm_w, sinks, a_re, a_im, log_step, b_re, b_im, c_re, c_im, d_skip, w_glu, b_glu, attn_out_norm_w, ssm_out_norm_w, w_out, loss_target, m_norm_w, m_w_in, m_q_norm_w, m_k_norm_w, m_sinks, m_a_re, m_a_im, m_log_step, m_b_re, m_b_im, m_c_re, m_c_im, m_d_skip, m_w_glu, m_b_glu, m_attn_out_norm_w, m_ssm_out_norm_w, m_w_out, v_norm_w, v_w_in, v_q_norm_w, v_k_norm_w, v_sinks, v_a_re, v_a_im, v_log_step, v_b_re, v_b_im, v_c_re, v_c_im, v_d_skip, v_w_glu, v_b_glu, v_attn_out_norm_w, v_ssm_out_norm_w, v_w_out):
    w = dict(norm_w=norm_w, w_in=w_in, q_norm_w=q_norm_w, k_norm_w=k_norm_w, sinks=sinks, a_re=a_re, a_im=a_im,
             log_step=log_step, b_re=b_re, b_im=b_im, c_re=c_re, c_im=c_im, d_skip=d_skip, w_glu=w_glu, b_glu=b_glu,
             attn_out_norm_w=attn_out_norm_w, ssm_out_norm_w=ssm_out_norm_w, w_out=w_out)
    m = dict(norm_w=m_norm_w, w_in=m_w_in, q_norm_w=m_q_norm_w, k_norm_w=m_k_norm_w, sinks=m_sinks, a_re=m_a_re,
             a_im=m_a_im, log_step=m_log_step, b_re=m_b_re, b_im=m_b_im, c_re=m_c_re, c_im=m_c_im, d_skip=m_d_skip,
             w_glu=m_w_glu, b_glu=m_b_glu, attn_out_norm_w=m_attn_out_norm_w, ssm_out_norm_w=m_ssm_out_norm_w,
             w_out=m_w_out)
    v = dict(norm_w=v_norm_w, w_in=v_w_in, q_norm_w=v_q_norm_w, k_norm_w=v_k_norm_w, sinks=v_sinks, a_re=v_a_re,
             a_im=v_a_im, log_step=v_log_step, b_re=v_b_re, b_im=v_b_im, c_re=v_c_re, c_im=v_c_im, d_skip=v_d_skip,
             w_glu=v_w_glu, b_glu=v_b_glu, attn_out_norm_w=v_attn_out_norm_w, ssm_out_norm_w=v_ssm_out_norm_w,
             w_out=v_w_out)
    core = lax.axis_index("c").astype(jnp.int32).reshape(1)
    chip = (2 * lax.axis_index("x") + lax.axis_index("y")).astype(jnp.int32).reshape(1)

    loss, grad_x, grads = _step(x[0], positions[0], loss_target[0], w, core, chip)
    loss = lax.psum(loss, ("x", "y", "c"))
    delta, new_m, new_v = {}, {}, {}
    for n in ("w_in", "w_glu", "w_out"):
        delta[n], new_m[n], new_v[n] = _adamw(grads[n], w[n], m[n], v[n], f"adamw_{n}")
    d_s, m_s, v_s = _adamw_small(grads, w, m, v)
    delta.update(d_s)
    new_m.update(m_s)
    new_v.update(v_s)

    return (loss, grad_x[None], *[grads[n] for n in _WEIGHTS], *[delta[n] for n in _WEIGHTS],
            *[new_m[n] for n in _WEIGHTS], *[new_v[n] for n in _WEIGHTS])
```

```python
import functools
import math

import jax
import jax.numpy as jnp
from jax import lax
from jax.experimental import pallas as pl
from jax.experimental.pallas import tpu as pltpu

F32 = jnp.float32
BF16 = jnp.bfloat16

D_MODEL = 2048
ATTN_W = 1024
KV_W = 256
SSM_W = 1024
HEAD_DIM = 64
N_HEADS = 16
N_KV = 4
KV_REP = 4
IN_W = 4608
BLOCK = 128
ROPE_THETA = 10000.0
NORM_EPS = 1e-6
SSM_G = 64
SSM_P = 64
SSM_H = 16
CHUNK = 16
CW = CHUNK * SSM_H
N_DEV = 8

ADAM_LR = 0.001
ADAM_B1 = 0.9
ADAM_B2 = 0.999
ADAM_EPS = 1e-08
ADAM_WD = 0.01
ADAM_STEP = 10

VMEM_LIMIT = 56 * 1024 * 1024
MESH = pl.DeviceIdType.MESH


def _cp(sem=None):
    if sem is None:
        return pltpu.CompilerParams(vmem_limit_bytes=VMEM_LIMIT)
    return pltpu.CompilerParams(vmem_limit_bytes=VMEM_LIMIT, dimension_semantics=sem)


def _sigmoid(x):
    return 0.5 * jnp.tanh(0.5 * x) + 0.5


def _silu(x):
    return x * _sigmoid(x)


def _dsilu(x):
    s = _sigmoid(x)
    return s * (1.0 + x * (1.0 - s))


_GELU_C = math.sqrt(2.0 / math.pi)


def _gelu(y):
    t = jnp.tanh(_GELU_C * (y + 0.044715 * y * y * y))
    return 0.5 * y * (1.0 + t)


def _dgelu(y):
    t = jnp.tanh(_GELU_C * (y + 0.044715 * y * y * y))
    return 0.5 * (1.0 + t) + 0.5 * y * (1.0 - t * t) * _GELU_C * (1.0 + 3.0 * 0.044715 * y * y)


def _tile(n, want):
    if n <= want:
        return n
    for t in range(want - want % 16, 0, -16):
        if n % t == 0:
            return t
    raise ValueError((n, want))


def _mm(a, b, mode, out_dtype, name, tm=512, tn=1024, add=None, ride=None, panel=None):
    if mode == "nn":
        (M, K), (K2, N) = a.shape, b.shape
    elif mode == "nt":
        (M, K), (N, K2) = a.shape, b.shape
    else:
        (K, M), (K2, N) = a.shape, b.shape
    assert K == K2
    tm, tn = _tile(M, tm), _tile(N, tn)
    p0 = 0
    if panel is not None:
        assert mode != "nt" and add is None
        p0, N = panel, tn
    dn = {"nn": _NN, "nt": _NT, "tn": _TN}[mode]

    def body(a_ref, b_ref, *rest):
        o_ref = rest[-1]
        acc = lax.dot_general(a_ref[...].astype(BF16), b_ref[...].astype(BF16), dn, preferred_element_type=F32)
        if add is not None:
            acc = acc + rest[0][...]
        o_ref[...] = acc.astype(o_ref.dtype)

    a_spec = pl.BlockSpec((K, tm), lambda j, i: (0, i)) if mode == "tn" else pl.BlockSpec((tm, K), lambda j, i: (i, 0))
    b_spec = (pl.BlockSpec((tn, K), lambda j, i: (j, 0)) if mode == "nt"
              else pl.BlockSpec((K, tn), lambda j, i: (0, j + p0)))
    o_spec = pl.BlockSpec((tm, tn), lambda j, i: (i, j))
    extra = () if add is None else (add,)
    if ride is not None:
        (out,), landed = _call(body, name, (N // tn, M // tm), [a_spec, b_spec] + [o_spec] * len(extra), [o_spec],
                               [jax.ShapeDtypeStruct((M, N), out_dtype)], (a, b, *extra), ride=ride)
        return out, landed
    return pl.pallas_call(
        body,
        name=name,
        grid=(N // tn, M // tm),
        in_specs=[a_spec, b_spec] + [o_spec] * len(extra),
        out_specs=o_spec,
        out_shape=jax.ShapeDtypeStruct((M, N), out_dtype),
        compiler_params=_cp(("parallel", "parallel")),
    )(a, b, *extra)


def _rms_inproj(x, norm_w, wt_in, ride):
    L = x.shape[0]
    tm, tn = _tile(L, 1024), 768
    nj = IN_W // tn

    def body(x_ref, w_ref, wt_ref, proj_ref, hn_ref, hn_scr):
        j = pl.program_id(1)

        @pl.when(j == 0)
        def _():
            xv = x_ref[...]
            r = lax.rsqrt(jnp.mean(xv * xv, axis=-1, keepdims=True) + NORM_EPS)
            hn = (xv * r * w_ref[...]).astype(BF16)
            hn_scr[...] = hn
            hn_ref[...] = hn

        proj_ref[...] = lax.dot_general(hn_scr[...], wt_ref[...], (((1,), (1,)), ((), ())),
                                        preferred_element_type=F32)

    return _call(
        body, "rms_inproj", (L // tm, nj),
        [pl.BlockSpec((tm, D_MODEL), lambda i, j: (i, 0)),
         pl.BlockSpec((1, D_MODEL), lambda i, j: (0, 0)),
         pl.BlockSpec((tn, D_MODEL), lambda i, j: (j, 0))],
        [pl.BlockSpec((tm, tn), lambda i, j: (i, j)),
         pl.BlockSpec((tm, D_MODEL), lambda i, j: (i, 0))],
        [jax.ShapeDtypeStruct((L, IN_W), F32), jax.ShapeDtypeStruct((L, D_MODEL), BF16)],
        (x, norm_w.reshape(1, D_MODEL), wt_in), [pltpu.VMEM((tm, D_MODEL), BF16)], ride)


def _seg_sum(v):
    a = lax.broadcasted_iota(jnp.int32, (128, 128), 0) // HEAD_DIM
    b = lax.broadcasted_iota(jnp.int32, (128, 128), 1) // HEAD_DIM
    ones = jnp.where(a == b, 1.0, 0.0).astype(BF16)
    hi = v.astype(BF16)
    lo = (v - hi.astype(F32)).astype(BF16)
    return jnp.dot(hi, ones, preferred_element_type=F32) + jnp.dot(lo, ones, preferred_element_type=F32)


def _rot_half(t):
    lane = lax.broadcasted_iota(jnp.int32, t.shape, 1)
    return jnp.where(lane % HEAD_DIM < HEAD_DIM // 2, pltpu.roll(t, 128 - HEAD_DIM // 2, 1),
                     pltpu.roll(t, HEAD_DIM // 2, 1))


def _norm_rope(raw, w, cos, sin):
    r = lax.rsqrt(_seg_sum(raw * raw) * (1.0 / HEAD_DIM) + NORM_EPS)
    tn = raw * r * w
    return r, tn * cos + _rot_half(tn) * sin


def _norm_rope_bwd(d_rot, raw, w, cos, sin):
    r = lax.rsqrt(_seg_sum(raw * raw) * (1.0 / HEAD_DIM) + NORM_EPS)
    d_tn = d_rot * cos + _rot_half(d_rot * sin)
    xh = raw * r
    gw = d_tn * w
    d_raw = r * (gw - xh * (_seg_sum(gw * xh) * (1.0 / HEAD_DIM)))
    return d_raw, d_tn * xh


def _band_mask2(has_prev):
    qi = lax.broadcasted_iota(jnp.int32, (2 * BLOCK, 2 * BLOCK), 0) % BLOCK + BLOCK
    kj = lax.broadcasted_iota(jnp.int32, (2 * BLOCK, 2 * BLOCK), 1)
    rel = qi - kj
    return (rel >= 0) & (rel < BLOCK) & ((kj >= BLOCK) | has_prev)


def _half_tiles(pair):
    lo = lax.broadcasted_iota(jnp.int32, pair.shape, 1) < HEAD_DIM
    sw = pltpu.roll(pair, HEAD_DIM, 1)
    z = jnp.zeros_like(pair)
    return (jnp.where(lo, pair, z).astype(BF16), jnp.where(lo, z, sw).astype(BF16),
            jnp.where(lo, sw, z).astype(BF16), jnp.where(lo, z, pair).astype(BF16))


def _two_rows(top, bottom):
    row = lax.broadcasted_iota(jnp.int32, (2 * BLOCK, 1), 0)
    return jnp.where(row < BLOCK, top, bottom)


def _lane_col(mat, h):
    lane = lax.broadcasted_iota(jnp.int32, mat.shape, 1)
    return jnp.sum(jnp.where(lane == h, mat, 0.0), axis=1, keepdims=True)


_SCALE = 1.0 / math.sqrt(HEAD_DIM)
_NT = (((1,), (1,)), ((), ()))
_NN = (((1,), (0,)), ((), ()))
_TN = (((0,), (0,)), ((), ()))


def _qk_prep(proj, tab, qw, kw):
    L = proj.shape[0]
    tm = _tile(L, 512)

    def body(q_ref, k_ref, t_ref, qw_ref, kw_ref, qo_ref, ko_ref):
        cos, sin = t_ref[:, :128], t_ref[:, 128:]
        for c in range(ATTN_W // 128):
            _, qr = _norm_rope(q_ref[:, c * 128:(c + 1) * 128], qw_ref[...], cos, sin)
            qo_ref[:, c * 128:(c + 1) * 128] = (qr * _SCALE).astype(BF16)
        for c in range(KV_W // 128):
            _, kr = _norm_rope(k_ref[:, c * 128:(c + 1) * 128], kw_ref[...], cos, sin)
            ko_ref[:, c * 128:(c + 1) * 128] = kr.astype(BF16)

    row = pl.BlockSpec((1, 128), lambda i: (0, 0))
    return pl.pallas_call(
        body,
        name="qk_prep",
        grid=(L // tm,),
        in_specs=[pl.BlockSpec((tm, ATTN_W), lambda i: (i, 0)), pl.BlockSpec((tm, KV_W), lambda i: (i, 4)),
                  pl.BlockSpec((tm, 256), lambda i: (i, 0)), row, row],
        out_specs=[pl.BlockSpec((tm, ATTN_W), lambda i: (i, 0)), pl.BlockSpec((tm, KV_W), lambda i: (i, 0))],
        out_shape=[jax.ShapeDtypeStruct((L, ATTN_W), BF16), jax.ShapeDtypeStruct((L, KV_W), BF16)],
        compiler_params=_cp(("parallel",)),
    )(proj, proj, tab, jnp.tile(qw, 2).reshape(1, 128), jnp.tile(kw, 2).reshape(1, 128))


def _group_tiles(g, kt, vt):
    a, b = divmod(g, 2)
    return kt[a][2 * b], kt[a][2 * b + 1], vt[a][2 * b], vt[a][2 * b + 1]


def _attn_fwd(q, k, proj, sinks):
    L = proj.shape[0]
    nb = L // BLOCK

    def body(q_ref, kc_ref, kp_ref, vc_ref, vp_ref, z0_ref, z1_ref, sink_ref, og_ref, o_ref, lse_ref):
        i = pl.program_id(0)
        mask = _band_mask2(i > 0)
        z = jnp.concatenate([z0_ref[...], z1_ref[...]], axis=1)
        lane = lax.broadcasted_iota(jnp.int32, (BLOCK, 128), 1)
        kt = [_half_tiles(jnp.concatenate([kp_ref[:, a * 128:(a + 1) * 128], kc_ref[:, a * 128:(a + 1) * 128]],
                                          axis=0).astype(F32)) for a in range(2)]
        vt = [_half_tiles(jnp.concatenate([vp_ref[:, a * 128:(a + 1) * 128], vc_ref[:, a * 128:(a + 1) * 128]],
                                          axis=0)) for a in range(2)]
        lse_mat = jnp.zeros((BLOCK, 128), F32)
        outs = []
        for g in range(N_KV):
            k_lo, k_hi, v_lo, v_hi = _group_tiles(g, kt, vt)
            q2 = jnp.concatenate([q_ref[:, 2 * g * 128:(2 * g + 1) * 128],
                                  q_ref[:, (2 * g + 1) * 128:(2 * g + 2) * 128]], axis=0)
            acc = jnp.zeros((2 * BLOCK, 128), F32)
            for half, (kh, vh) in enumerate(((k_lo, v_lo), (k_hi, v_hi))):
                h_top, h_bot = 4 * g + half, 4 * g + 2 + half
                s = jnp.where(mask, lax.dot_general(q2, kh, _NT, preferred_element_type=F32), -1e30)
                sink = _two_rows(sink_ref[h_top], sink_ref[h_bot])
                m = jnp.maximum(jnp.max(s, axis=-1, keepdims=True), sink)
                e = jnp.exp(s - m)
                den = jnp.sum(e, axis=-1, keepdims=True) + jnp.exp(sink - m)
                p = e * (1.0 / den)
                acc = acc + jnp.dot(p.astype(BF16), vh, preferred_element_type=F32)
                lse = m + jnp.log(den)
                lse_mat = jnp.where(lane == h_top, lse[:BLOCK], lse_mat)
                lse_mat = jnp.where(lane == h_bot, lse[BLOCK:], lse_mat)
            outs += [acc[:BLOCK], acc[BLOCK:]]
        o = jnp.concatenate(outs, axis=1)
        o_ref[...] = o
        og_ref[...] = o * _silu(z)
        lse_ref[...] = lse_mat

    prev = lambda i: jnp.maximum(i - 1, 0)
    return pl.pallas_call(
        body,
        name="attn_fwd",
        grid=(nb,),
        in_specs=[pl.BlockSpec((BLOCK, ATTN_W), lambda i: (i, 0)),
                  pl.BlockSpec((BLOCK, KV_W), lambda i: (i, 0)),
                  pl.BlockSpec((BLOCK, KV_W), lambda i: (prev(i), 0)),
                  pl.BlockSpec((BLOCK, KV_W), lambda i: (i, 5)),
                  pl.BlockSpec((BLOCK, KV_W), lambda i: (prev(i), 5)),
                  pl.BlockSpec((BLOCK, 512), lambda i: (i, 3)),
                  pl.BlockSpec((BLOCK, 512), lambda i: (i, 4)),
                  pl.BlockSpec(memory_space=pltpu.SMEM)],
        out_specs=[pl.BlockSpec((BLOCK, ATTN_W), lambda i: (i, 0)),
                   pl.BlockSpec((BLOCK, ATTN_W), lambda i: (i, 0)),
                   pl.BlockSpec((BLOCK, 128), lambda i: (i, 0))],
        out_shape=[jax.ShapeDtypeStruct((L, ATTN_W), F32), jax.ShapeDtypeStruct((L, ATTN_W), F32),
                   jax.ShapeDtypeStruct((L, 128), F32)],
        compiler_params=_cp(("parallel",)),
    )(q, k, k, proj, proj, proj, proj, sinks)


def _attn_bwd(q, k, proj, sinks, d_o, o, lse, ride):
    L = proj.shape[0]
    nb = L // BLOCK

    def body(q_ref, kc_ref, kp_ref, vc_ref, vp_ref, do_ref, o_ref, lse_ref, sink_ref,
             dq_ref, dk_ref, dv_ref, gs_ref, ck_scr, cv_scr):
        i = pl.program_id(0)

        @pl.when(i == 0)
        def _():
            gs_ref[...] = jnp.zeros_like(gs_ref)
            ck_scr[...] = jnp.zeros_like(ck_scr)
            cv_scr[...] = jnp.zeros_like(cv_scr)

        @pl.when(i == nb)
        def _():
            dk_ref[...] = ck_scr[...]
            dv_ref[...] = cv_scr[...]

        @pl.when(i < nb)
        def _():
            mask = _band_mask2(i > 0)
            lane = lax.broadcasted_iota(jnp.int32, (1, 128), 1)
            lo = lax.broadcasted_iota(jnp.int32, (2 * BLOCK, 128), 1) < HEAD_DIM
            lse_c = lse_ref[...]
            kt = [_half_tiles(jnp.concatenate([kp_ref[:, a * 128:(a + 1) * 128], kc_ref[:, a * 128:(a + 1) * 128]],
                                              axis=0).astype(F32)) for a in range(2)]
            vt = [_half_tiles(jnp.concatenate([vp_ref[:, a * 128:(a + 1) * 128], vc_ref[:, a * 128:(a + 1) * 128]],
                                              axis=0)) for a in range(2)]
            gs = jnp.zeros((1, 128), F32)
            dq_parts = []
            dk_acc = [jnp.zeros((2 * BLOCK, 128), F32) for _ in range(2)]
            dv_acc = [jnp.zeros((2 * BLOCK, 128), F32) for _ in range(2)]
            for g in range(N_KV):
                a, b = divmod(g, 2)
                k_lo, k_hi, v_lo, v_hi = _group_tiles(g, kt, vt)
                t0, t1 = slice(2 * g * 128, (2 * g + 1) * 128), slice((2 * g + 1) * 128, (2 * g + 2) * 128)
                q2 = jnp.concatenate([q_ref[:, t0], q_ref[:, t1]], axis=0)
                do2 = jnp.concatenate([do_ref[:, t0], do_ref[:, t1]], axis=0)
                prod = do2 * jnp.concatenate([o_ref[:, t0], o_ref[:, t1]], axis=0)
                do2_b = do2.astype(BF16)
                dq2 = jnp.zeros((2 * BLOCK, 128), F32)
                dk_h, dv_h = [], []
                for half, (kh, vh) in enumerate(((k_lo, v_lo), (k_hi, v_hi))):
                    h_top, h_bot = 4 * g + half, 4 * g + 2 + half
                    lse = jnp.concatenate([_lane_col(lse_c, h_top), _lane_col(lse_c, h_bot)], axis=0)
                    sink = _two_rows(sink_ref[h_top], sink_ref[h_bot])
                    delta = jnp.sum(jnp.where(lo == (half == 0), prod, 0.0), axis=1, keepdims=True)
                    s = jnp.where(mask, lax.dot_general(q2, kh, _NT, preferred_element_type=F32), -1e30)
                    p = jnp.exp(s - lse)
                    dp = lax.dot_general(do2_b, vh, _NT, preferred_element_type=F32)
                    ds_b = (p * (dp - delta)).astype(BF16)
                    p_b = p.astype(BF16)
                    dq2 = dq2 + jnp.dot(ds_b, kh, preferred_element_type=F32)
                    dk_h.append(lax.dot_general(ds_b, q2, _TN, preferred_element_type=F32))
                    dv_h.append(lax.dot_general(p_b, do2_b, _TN, preferred_element_type=F32))
                    gsink = -jnp.exp(sink - lse) * delta
                    row = lax.broadcasted_iota(jnp.int32, (2 * BLOCK, 1), 0)
                    gs = gs + jnp.where(lane == h_top, jnp.sum(jnp.where(row < BLOCK, gsink, 0.0)), 0.0)
                    gs = gs + jnp.where(lane == h_bot, jnp.sum(jnp.where(row >= BLOCK, gsink, 0.0)), 0.0)
                dq_parts += [dq2[:BLOCK], dq2[BLOCK:]]
                for acc, parts in ((dk_acc, dk_h), (dv_acc, dv_h)):
                    t = jnp.where(lo, parts[0], parts[1])
                    t = t + pltpu.roll(t, HEAD_DIM, 1)
                    acc[a] = acc[a] + jnp.where(lo == (b == 0), t, 0.0)
            dq_ref[...] = jnp.concatenate(dq_parts, axis=1)
            dk_full = jnp.concatenate(dk_acc, axis=1)
            dv_full = jnp.concatenate(dv_acc, axis=1)
            dk_ref[...] = ck_scr[...] + dk_full[:BLOCK]
            dv_ref[...] = cv_scr[...] + dv_full[:BLOCK]
            ck_scr[...] = dk_full[BLOCK:]
            cv_scr[...] = dv_full[BLOCK:]
            gs_ref[...] += gs

    cur = lambda i: jnp.minimum(i, nb - 1)
    prev = lambda i: jnp.maximum(jnp.minimum(i, nb - 1) - 1, 0)
    done = lambda i: jnp.maximum(i - 1, 0)
    bs = pl.BlockSpec
    return _call(
        body, "attn_bwd", (nb + 1,),
        [bs((BLOCK, ATTN_W), lambda i: (cur(i), 0)),
         bs((BLOCK, KV_W), lambda i: (cur(i), 0)), bs((BLOCK, KV_W), lambda i: (prev(i), 0)),
         bs((BLOCK, KV_W), lambda i: (cur(i), 5)), bs((BLOCK, KV_W), lambda i: (prev(i), 5)),
         bs((BLOCK, ATTN_W), lambda i: (cur(i), 0)), bs((BLOCK, ATTN_W), lambda i: (cur(i), 0)),
         bs((BLOCK, 128), lambda i: (cur(i), 0)), bs(memory_space=pltpu.SMEM)],
        [bs((BLOCK, ATTN_W), lambda i: (cur(i), 0)),
         bs((BLOCK, KV_W), lambda i: (done(i), 0)), bs((BLOCK, KV_W), lambda i: (done(i), 0)),
         bs((1, 128), lambda i: (0, 0))],
        [jax.ShapeDtypeStruct((L, ATTN_W), F32), jax.ShapeDtypeStruct((L, KV_W), F32),
         jax.ShapeDtypeStruct((L, KV_W), F32), jax.ShapeDtypeStruct((1, 128), F32)],
        (q, k, k, proj, proj, d_o, o, lse, sinks),
        [pltpu.VMEM((BLOCK, KV_W), F32), pltpu.VMEM((BLOCK, KV_W), F32)], ride)


def _qk_prep_bwd(proj, tab, qw, kw, d_q, d_k, d_v, d_za, d_u, d_zs):
    L = proj.shape[0]
    tm = _tile(L, 512)
    z0 = ATTN_W + 2 * KV_W

    def body(q_ref, k_ref, t_ref, qw_ref, kw_ref, dq_ref, dk_ref, dv_ref, dza_ref, du_ref, dzs_ref,
             out_ref, gq_ref, gk_ref):
        i = pl.program_id(0)

        @pl.when(i == 0)
        def _():
            gq_ref[...] = jnp.zeros_like(gq_ref)
            gk_ref[...] = jnp.zeros_like(gk_ref)

        cos, sin = t_ref[:, :128], t_ref[:, 128:]
        gq = jnp.zeros((1, 128), F32)
        gk = jnp.zeros((1, 128), F32)
        for c in range(ATTN_W // 128):
            cs = slice(c * 128, (c + 1) * 128)
            d_raw, gw = _norm_rope_bwd(dq_ref[:, cs] * _SCALE, q_ref[:, cs], qw_ref[...], cos, sin)
            out_ref[:, cs] = d_raw.astype(BF16)
            gq = gq + jnp.sum(gw, axis=0, keepdims=True)
        for c in range(KV_W // 128):
            cs = slice(c * 128, (c + 1) * 128)
            d_raw, gw = _norm_rope_bwd(dk_ref[:, cs], k_ref[:, cs], kw_ref[...], cos, sin)
            out_ref[:, ATTN_W + c * 128:ATTN_W + (c + 1) * 128] = d_raw.astype(BF16)
            gk = gk + jnp.sum(gw, axis=0, keepdims=True)
        out_ref[:, ATTN_W + KV_W:z0] = dv_ref[...].astype(BF16)
        out_ref[:, z0:z0 + ATTN_W] = dza_ref[...]
        out_ref[:, z0 + ATTN_W:z0 + ATTN_W + SSM_W] = du_ref[...].astype(BF16)
        out_ref[:, z0 + ATTN_W + SSM_W:] = dzs_ref[...]
        gq_ref[...] += gq
        gk_ref[...] += gk

    row = pl.BlockSpec((1, 128), lambda i: (0, 0))
    blk = lambda w, c: pl.BlockSpec((tm, w), lambda i: (i, c))
    return pl.pallas_call(
        body,
        name="qk_prep_bwd",
        grid=(L // tm,),
        in_specs=[blk(ATTN_W, 0), blk(KV_W, 4), blk(256, 0), row, row, blk(ATTN_W, 0), blk(KV_W, 0), blk(KV_W, 0),
                  blk(ATTN_W, 0), blk(SSM_W, 0), blk(SSM_W, 0)],
        out_specs=[blk(IN_W, 0), row, row],
        out_shape=[jax.ShapeDtypeStruct((L, IN_W), BF16), jax.ShapeDtypeStruct((1, 128), F32),
                   jax.ShapeDtypeStruct((1, 128), F32)],
        compiler_params=_cp(("arbitrary",)),
    )(proj, proj, tab, jnp.tile(qw, 2).reshape(1, 128), jnp.tile(kw, 2).reshape(1, 128), d_q, d_k, d_v,
      d_za, d_u, d_zs)


def _cmul(a, b):
    return a[0] * b[0] - a[1] * b[1], a[0] * b[1] + a[1] * b[0]


def _cmul_conj(a, b):
    return a[0] * b[0] + a[1] * b[1], a[1] * b[0] - a[0] * b[1]


def _cadd(a, b):
    return a[0] + b[0], a[1] + b[1]


def _dot3(a, b, dn):
    ah, bh = a.astype(BF16), b.astype(BF16)
    al, bl = (a - ah.astype(F32)).astype(BF16), (b - bh.astype(F32)).astype(BF16)
    d = lambda u, v: lax.dot_general(u, v, dn, preferred_element_type=F32)
    return d(ah, bh) + d(ah, bl) + d(al, bh)


def _s5_discretise(a_re, a_im, ls, cosx, sinx, bt):
    delta = jnp.exp(ls)
    er = jnp.exp(a_re * delta)
    lb = (er * cosx, er * sinx)
    den = a_re * a_re + a_im * a_im
    coef = _cmul_conj((lb[0] - 1.0, lb[1]), (a_re, a_im))
    coef = (coef[0] / den, coef[1] / den)
    return delta, lb, coef, den, _cmul(coef, bt)


def _powers(lb):
    pw = [(jnp.ones_like(lb[0]), jnp.zeros_like(lb[0]))]
    for _ in range(CHUNK):
        pw.append(_cmul(pw[-1], lb))
    return pw


def _block_rows(a, pw, idx):
    blocks = [_cmul(a, pw[i]) for i in idx]
    return (jnp.concatenate([b[0] for b in blocks], axis=-2), jnp.concatenate([b[1] for b in blocks], axis=-2))


def _block_rows_bwd(g, a, pw, idx, g_pw):
    g_a = (jnp.zeros_like(a[0]), jnp.zeros_like(a[0]))
    for j, i in enumerate(idx):
        gj = (g[0][..., j * SSM_H:(j + 1) * SSM_H, :], g[1][..., j * SSM_H:(j + 1) * SSM_H, :])
        g_a = _cadd(g_a, _cmul_conj(gj, pw[i]))
        gp = _cmul_conj(gj, a)
        g_pw[i] = _cadd(g_pw[i], (jnp.sum(gp[0], axis=-2, keepdims=True), jnp.sum(gp[1], axis=-2, keepdims=True)))
    return g_a


_IDX_S = [CHUNK - 1 - s for s in range(CHUNK)]
_IDX_O = [t + 1 for t in range(CHUNK)]
_IDX_K = list(range(CHUNK))
_PREP_IN = 9


def _prep_args(p):
    row = lambda t: t.reshape(SSM_G, 1, SSM_P)
    xi = p["a_im"] * jnp.exp(p["log_step"])[:, None]
    return (row(p["a_re"]), row(p["a_im"]), row(jnp.broadcast_to(p["log_step"][:, None], (SSM_G, SSM_P))),
            row(jnp.cos(xi)), row(jnp.sin(xi)), p["b_re"].transpose(0, 2, 1), p["b_im"].transpose(0, 2, 1),
            p["c_re"], p["c_im"])


PREP_GROUPS = 8


def _prep_specs():
    r1 = pl.BlockSpec((PREP_GROUPS, 1, SSM_P), lambda g: (g, 0, 0))
    r16 = pl.BlockSpec((PREP_GROUPS, SSM_H, SSM_P), lambda g: (g, 0, 0))
    return [r1] * 5 + [r16] * 4, r1, r16


def _ssm_prep(p):
    def one_group(q, are, aim, ls, cosx, sinx, btr, bti, cre, cim, mt_ref, s_ref, o_ref, a_ref):
        _, lb, _, _, bb = _s5_discretise(are[q], aim[q], ls[q], cosx[q], sinx[q], (btr[q], bti[q]))
        pw = _powers(lb)
        c = (cre[q], cim[q])
        sc = _block_rows(bb, pw, _IDX_S)
        ot = _block_rows(c, pw, _IDX_O)
        ok = _block_rows(c, pw, _IDX_K)
        s_ref[q] = jnp.concatenate([sc[0], sc[1]], axis=1).astype(BF16)
        o_ref[q] = jnp.concatenate([ot[0], -ot[1]], axis=1).astype(BF16)
        a_ref[q] = jnp.concatenate([pw[CHUNK][0], pw[CHUNK][1]], axis=1)
        kt = _dot3(jnp.concatenate([bb[0], -bb[1]], axis=1), jnp.concatenate([ok[0], ok[1]], axis=1), _NT)
        lane = lax.broadcasted_iota(jnp.int32, kt.shape, 1)
        for s in range(CHUNK):
            blk = kt if s == 0 else jnp.where(lane >= SSM_H * s, pltpu.roll(kt, SSM_H * s, 1), 0.0)
            mt_ref[q, s * SSM_H:(s + 1) * SSM_H, :] = blk.astype(BF16)

    def body(*refs):
        for q in range(PREP_GROUPS):
            one_group(q, *refs)

    in_specs, r1, _ = _prep_specs()
    g3 = lambda r, c: pl.BlockSpec((PREP_GROUPS, r, c), lambda g: (g, 0, 0))
    return pl.pallas_call(
        body,
        name="ssm_prep",
        grid=(SSM_G // PREP_GROUPS,),
        in_specs=in_specs,
        out_specs=[g3(CW, CW), g3(CW, 2 * SSM_P), g3(CW, 2 * SSM_P), g3(1, 2 * SSM_P)],
        out_shape=[jax.ShapeDtypeStruct((SSM_G, CW, CW), BF16), jax.ShapeDtypeStruct((SSM_G, CW, 2 * SSM_P), BF16),
                   jax.ShapeDtypeStruct((SSM_G, CW, 2 * SSM_P), BF16),
                   jax.ShapeDtypeStruct((SSM_G, 1, 2 * SSM_P), F32)],
        compiler_params=_cp(("parallel",)),
    )(*_prep_args(p))


def _ssm_prep_bwd(p, g_mt, g_scat, g_ocat, g_a16, ride):
    def body(are, aim, ls, cosx, sinx, btr, bti, cre, cim, gmt_ref, gs_ref, go_ref, ga_ref,
             g_are, g_aim, g_ls, g_btr, g_bti, g_cre, g_cim, ga1_scr, gb1_scr):
        lam = (are[...], aim[...])
        bt = (btr[...], bti[...])
        delta, lb, coef, den, bb = _s5_discretise(lam[0], lam[1], ls[...], cosx[...], sinx[...], bt)
        pw = _powers(lb)
        c = (cre[...], cim[...])
        ok = _block_rows(c, pw, _IDX_K)
        g_pw = [(jnp.zeros_like(lb[0]), jnp.zeros_like(lb[0])) for _ in range(CHUNK + 1)]
        lane = lax.broadcasted_iota(jnp.int32, (SSM_H, CW), 1)
        for q in range(PREP_GROUPS):
            g_kt = gmt_ref[q, :SSM_H, :]
            for s in range(1, CHUNK):
                blk = gmt_ref[q, s * SSM_H:(s + 1) * SSM_H, :]
                g_kt = g_kt + jnp.where(lane < CW - SSM_H * s, pltpu.roll(blk, CW - SSM_H * s, 1), 0.0)
            a1 = jnp.concatenate([bb[0][q], -bb[1][q]], axis=1)
            b1 = jnp.concatenate([ok[0][q], ok[1][q]], axis=1)
            ga1_scr[q] = _dot3(g_kt, b1, _NN)
            gb1_scr[q] = _dot3(g_kt, a1, _TN)
        g_a1, g_b1 = ga1_scr[...], gb1_scr[...]
        g_bb = (g_a1[..., :SSM_P], -g_a1[..., SSM_P:])
        g_c = _block_rows_bwd((g_b1[..., :SSM_P], g_b1[..., SSM_P:]), c, pw, _IDX_K, g_pw)
        gs = gs_ref[...]
        g_bb = _cadd(g_bb, _block_rows_bwd((gs[..., :SSM_P], gs[..., SSM_P:]), bb, pw, _IDX_S, g_pw))
        go = go_ref[...]
        g_c = _cadd(g_c, _block_rows_bwd((go[..., :SSM_P], -go[..., SSM_P:]), c, pw, _IDX_O, g_pw))
        ga = ga_ref[...]
        g_pw[CHUNK] = _cadd(g_pw[CHUNK], (ga[..., :SSM_P], ga[..., SSM_P:]))
        g_lb = (jnp.zeros_like(lb[0]), jnp.zeros_like(lb[0]))
        for l in range(CHUNK - 1, -1, -1):
            g_lb = _cadd(g_lb, _cmul_conj(g_pw[l + 1], pw[l]))
            g_pw[l] = _cadd(g_pw[l], _cmul_conj(g_pw[l + 1], lb))
        g_bt = _cmul_conj(g_bb, coef)
        gc = _cmul_conj(g_bb, bt)
        g_coef = (jnp.sum(gc[0], axis=-2, keepdims=True), jnp.sum(gc[1], axis=-2, keepdims=True))
        lam_den = (lam[0] / den, lam[1] / den)
        g_lb = _cadd(g_lb, _cmul(g_coef, lam_den))
        t = _cmul(_cmul_conj(g_coef, coef), lam_den)
        g_x = _cmul_conj(g_lb, lb)
        g_are[...] = g_x[0] * delta - t[0]
        g_aim[...] = g_x[1] * delta - t[1]
        g_ls[...] = (g_x[0] * lam[0] + g_x[1] * lam[1]) * delta
        g_btr[...] = g_bt[0]
        g_bti[...] = g_bt[1]
        g_cre[...] = g_c[0]
        g_cim[...] = g_c[1]

    in_specs, r1, r16 = _prep_specs()
    g3 = lambda r, c: pl.BlockSpec((PREP_GROUPS, r, c), lambda g: (g, 0, 0))
    rows = jax.ShapeDtypeStruct((SSM_G, 1, SSM_P), F32)
    mats = jax.ShapeDtypeStruct((SSM_G, SSM_H, SSM_P), F32)
    (g_are, g_aim, g_ls, g_btr, g_bti, g_cre, g_cim), landed = _call(
        body, "ssm_prep_bwd", (SSM_G // PREP_GROUPS,),
        in_specs + [g3(CW, CW), g3(CW, 2 * SSM_P), g3(CW, 2 * SSM_P), g3(1, 2 * SSM_P)],
        [r1] * 3 + [r16] * 4, [rows] * 3 + [mats] * 4, (*_prep_args(p), g_mt, g_scat, g_ocat, g_a16),
        [pltpu.VMEM((PREP_GROUPS, SSM_H, 2 * SSM_P), F32), pltpu.VMEM((PREP_GROUPS, CW, 2 * SSM_P), F32)], ride)
    grads = dict(a_re=g_are.reshape(SSM_G, SSM_P), a_im=g_aim.reshape(SSM_G, SSM_P),
                 log_step=jnp.sum(g_ls.reshape(SSM_G, SSM_P), axis=1),
                 b_re=g_btr.transpose(0, 2, 1), b_im=g_bti.transpose(0, 2, 1), c_re=g_cre, c_im=g_cim)
    return grads, landed


def _cmul_const(xv, ar, ai):
    return xv * ar + pltpu.roll(xv, SSM_P, 1) * ai


def _chunk_scan(inc, a_row, reverse):
    n = inc.shape[0]
    lane = lax.broadcasted_iota(jnp.int32, (1, 2 * SSM_P), 1)
    row = lax.broadcasted_iota(jnp.int32, inc.shape, 0)
    sign = jnp.where(lane < SSM_P, -1.0, 1.0)
    ar = jnp.where(lane < SSM_P, a_row, pltpu.roll(a_row, SSM_P, 1))
    ai = jnp.where(lane < SSM_P, pltpu.roll(a_row, SSM_P, 1), a_row)
    if reverse:
        ai = -ai
    xv = inc
    s = 1
    while s < n:
        if reverse:
            sh = jnp.where(row < n - s, pltpu.roll(xv, n - s, 0), 0.0)
        else:
            sh = jnp.where(row >= s, pltpu.roll(xv, s, 0), 0.0)
        xv = xv + _cmul_const(sh, ar, ai * sign)
        ar, ai = ar * ar - ai * ai, 2.0 * ar * ai
        s *= 2
    return xv


def _shift_rows(xv, reverse):
    n = xv.shape[0]
    row = lax.broadcasted_iota(jnp.int32, xv.shape, 0)
    if reverse:
        return jnp.where(row < n - 1, pltpu.roll(xv, n - 1, 0), 0.0)
    return jnp.where(row >= 1, pltpu.roll(xv, 1, 0), 0.0)


GB = 128 // SSM_H
U_COL0 = (ATTN_W + 2 * KV_W + ATTN_W) // 128


HALF = CHUNK // 2


def _chunk_perm():
    r = jnp.arange(HALF * 128)
    t, g8, h = r // 128, (r % 128) // SSM_H, r % SSM_H
    return ((g8 * 128 + t * SSM_H + h)[:, None] == jnp.arange(GB * 128)[None, :]).astype(BF16)


def _load_perm(p_hbm, p_scr, sem):
    @pl.when(pl.program_id(0) == 0)
    def _():
        cp = pltpu.make_async_copy(p_hbm, p_scr, sem)
        cp.start()
        cp.wait()


def _rows_to_chunks(pieces, perm):
    halves = [jnp.dot(jnp.concatenate(pieces[k * HALF:(k + 1) * HALF], axis=1).astype(BF16), perm,
                      preferred_element_type=F32).astype(BF16) for k in range(2)]
    return [jnp.concatenate([hv[:, g * 128:(g + 1) * 128] for hv in halves], axis=1) for g in range(GB)]


def _chunks_to_rows(groups, perm, two_pass):
    pieces = []
    for k in range(2):
        v = jnp.concatenate([gv[:, k * 128:(k + 1) * 128] for gv in groups], axis=1)
        hi = v.astype(BF16)
        out = lax.dot_general(hi, perm, _NT, preferred_element_type=F32)
        if two_pass:
            lo = (v - hi.astype(F32)).astype(BF16)
            out = out + lax.dot_general(lo, perm, _NT, preferred_element_type=F32)
        pieces += [out[:, t * 128:(t + 1) * 128] for t in range(HALF)]
    return pieces


def _ssm_fwd(proj, perm, mt, scat, ocat, a16, d_skip):
    L = proj.shape[0]
    nc = L // CHUNK

    def body(u_ref, p_hbm, mt_ref, s_ref, o_ref, a_ref, d_ref, y_ref, yg_ref, h_ref, p_scr, sem):
        _load_perm(p_hbm, p_scr, sem)
        perm = p_scr[...]
        rows = [pl.ds(t, nc, stride=CHUNK) for t in range(CHUNK)]
        ua = _rows_to_chunks([u_ref[r, :] for r in rows], perm)
        ys = []
        for g in range(GB):
            uv = ua[g]
            inc = jnp.dot(uv, s_ref[g], preferred_element_type=F32)
            hx = _shift_rows(_chunk_scan(inc, a_ref[g], False), False)
            h_ref[g] = hx
            ys.append(jnp.dot(uv, mt_ref[g], preferred_element_type=F32)
                      + lax.dot_general(hx.astype(BF16), o_ref[g], _NT, preferred_element_type=F32))
        yp = _chunks_to_rows(ys, perm, True)
        for t, r in enumerate(rows):
            y = yp[t] + d_ref[...] * u_ref[r, :]
            y_ref[r, :] = y
            yg_ref[r, :] = _gelu(y)

    g3 = lambda r, c: pl.BlockSpec((GB, r, c), lambda g: (g, 0, 0))
    col = pl.BlockSpec((L, 128), lambda g: (0, g))
    return pl.pallas_call(
        body,
        name="ssm_fwd",
        grid=(SSM_G // GB,),
        in_specs=[pl.BlockSpec((L, 128), lambda g: (0, U_COL0 + g)), _ANY,
                  g3(CW, CW), g3(CW, 2 * SSM_P), g3(CW, 2 * SSM_P), g3(1, 2 * SSM_P),
                  pl.BlockSpec((1, 128), lambda g: (0, g))],
        out_specs=[col, col, g3(nc, 2 * SSM_P)],
        out_shape=[jax.ShapeDtypeStruct((L, SSM_W), F32), jax.ShapeDtypeStruct((L, SSM_W), F32),
                   jax.ShapeDtypeStruct((SSM_G, nc, 2 * SSM_P), F32)],
        scratch_shapes=[pltpu.VMEM((HALF * 128, GB * 128), BF16), pltpu.SemaphoreType.DMA],
        compiler_params=_cp(("arbitrary",)),
    )(proj, perm, mt, scat, ocat, a16, d_skip.reshape(1, SSM_W))


def _ssm_bwd(d_yg, y, proj, hx, perm, mt, scat, ocat, a16, d_skip, ride):
    L = proj.shape[0]
    nc = L // CHUNK

    def body(dg_ref, y_ref, u_ref, h_ref, p_hbm, mt_ref, s_ref, o_ref, a_ref, d_ref,
             du_ref, gmt_ref, gs_ref, go_ref, ga_ref, gd_ref, p_scr, sem):
        _load_perm(p_hbm, p_scr, sem)
        perm = p_scr[...]
        rows = [pl.ds(t, nc, stride=CHUNK) for t in range(CHUNK)]
        us = [u_ref[r, :] for r in rows]
        dys = [dg_ref[r, :] * _dgelu(y_ref[r, :]) for r in rows]
        gd = jnp.zeros((1, 128), F32)
        for uv, dy in zip(us, dys):
            gd = gd + jnp.sum(dy * uv, axis=0, keepdims=True)
        gd_ref[...] = gd
        ua = _rows_to_chunks(us, perm)
        dya = _rows_to_chunks(dys, perm)
        lane = lax.broadcasted_iota(jnp.int32, (1, 2 * SSM_P), 1)
        dus = []
        for g in range(GB):
            uv, dy, hx_v = ua[g], dya[g], h_ref[g]
            dh = jnp.dot(dy, o_ref[g], preferred_element_type=F32)
            dinc = _shift_rows(_chunk_scan(dh, a_ref[g], True), True)
            dinc_b = dinc.astype(BF16)
            dus.append(lax.dot_general(dy, mt_ref[g], _NT, preferred_element_type=F32)
                       + lax.dot_general(dinc_b, s_ref[g], _NT, preferred_element_type=F32))
            gmt_ref[g] = lax.dot_general(uv, dy, _TN, preferred_element_type=F32)
            gs_ref[g] = lax.dot_general(uv, dinc_b, _TN, preferred_element_type=F32)
            go_ref[g] = lax.dot_general(dy, hx_v.astype(BF16), _TN, preferred_element_type=F32)
            p1 = dinc * hx_v
            p2 = pltpu.roll(dinc, SSM_P, 1) * hx_v
            t1 = jnp.sum(p1 + pltpu.roll(p1, SSM_P, 1), axis=0, keepdims=True)
            t2 = jnp.sum(p2 - pltpu.roll(p2, SSM_P, 1), axis=0, keepdims=True)
            ga_ref[g] = jnp.where(lane < SSM_P, t1, pltpu.roll(t2, SSM_P, 1))
        dup = _chunks_to_rows(dus, perm, False)
        for t, r in enumerate(rows):
            du_ref[r, :] = dup[t] + d_ref[...] * dys[t]

    g3 = lambda r, c: pl.BlockSpec((GB, r, c), lambda g: (g, 0, 0))
    col = pl.BlockSpec((L, 128), lambda g: (0, g))
    row = pl.BlockSpec((1, 128), lambda g: (0, g))
    return _call(
        body, "ssm_bwd", (SSM_G // GB,),
        [col, col, pl.BlockSpec((L, 128), lambda g: (0, U_COL0 + g)), g3(nc, 2 * SSM_P), _ANY,
         g3(CW, CW), g3(CW, 2 * SSM_P), g3(CW, 2 * SSM_P), g3(1, 2 * SSM_P), row],
        [col, g3(CW, CW), g3(CW, 2 * SSM_P), g3(CW, 2 * SSM_P), g3(1, 2 * SSM_P), row],
        [jax.ShapeDtypeStruct((L, SSM_W), F32), jax.ShapeDtypeStruct((SSM_G, CW, CW), F32),
         jax.ShapeDtypeStruct((SSM_G, CW, 2 * SSM_P), F32), jax.ShapeDtypeStruct((SSM_G, CW, 2 * SSM_P), F32),
         jax.ShapeDtypeStruct((SSM_G, 1, 2 * SSM_P), F32), jax.ShapeDtypeStruct((1, SSM_W), F32)],
        (d_yg, y, proj, hx, perm, mt, scat, ocat, a16, d_skip.reshape(1, SSM_W)),
        [pltpu.VMEM((HALF * 128, GB * 128), BF16), pltpu.SemaphoreType.DMA], ride)


def _merge(og, yg, gpre, proj, b_glu, wa, ws):
    L = og.shape[0]
    tm = _tile(L, 256)

    def body(og_ref, yg_ref, gp_ref, z0_ref, z1_ref, b_ref, wa_ref, ws_ref, m_ref):
        zs = jnp.concatenate([z0_ref[...], z1_ref[...]], axis=1)
        os_ = yg_ref[...] * _sigmoid(gp_ref[...] + b_ref[...]) * _silu(zs)
        ogv = og_ref[...]
        ra = lax.rsqrt(jnp.mean(ogv * ogv, axis=-1, keepdims=True) + NORM_EPS)
        rs = lax.rsqrt(jnp.mean(os_ * os_, axis=-1, keepdims=True) + NORM_EPS)
        m_ref[:, :ATTN_W] = (ogv * ra * wa_ref[...]).astype(BF16)
        m_ref[:, ATTN_W:] = (os_ * rs * ws_ref[...]).astype(BF16)

    row = lambda w: pl.BlockSpec((1, w), lambda i: (0, 0))
    return pl.pallas_call(
        body,
        name="merge",
        grid=(L // tm,),
        in_specs=[pl.BlockSpec((tm, ATTN_W), lambda i: (i, 0)), pl.BlockSpec((tm, SSM_W), lambda i: (i, 0)),
                  pl.BlockSpec((tm, SSM_W), lambda i: (i, 0)),
                  pl.BlockSpec((tm, 512), lambda i: (i, 7)), pl.BlockSpec((tm, 512), lambda i: (i, 8)),
                  row(SSM_W), row(ATTN_W), row(SSM_W)],
        out_specs=pl.BlockSpec((tm, D_MODEL), lambda i: (i, 0)),
        out_shape=jax.ShapeDtypeStruct((L, D_MODEL), BF16),
        compiler_params=_cp(("parallel",)),
    )(og, yg, gpre, proj, proj, b_glu.reshape(1, SSM_W), wa.reshape(1, ATTN_W), ws.reshape(1, SSM_W))


def _outproj_loss(merged, w_out, x, target):
    L = x.shape[0]
    tm, tn = _tile(L, 512), 1024
    ni, nj = L // tm, D_MODEL // tn

    def body(m_ref, w_ref, x_ref, t_ref, d_ref, db_ref, l_ref):
        out = x_ref[...] + jnp.dot(m_ref[...], w_ref[...], preferred_element_type=F32)
        diff = out - t_ref[...]
        d = diff * (1.0 / D_MODEL)
        d_ref[...] = d
        db_ref[...] = d.astype(BF16)
        l_ref[...] = jnp.full((1, 8, 128), jnp.sum(diff * diff), F32)

    return pl.pallas_call(
        body,
        name="outproj_loss",
        grid=(nj, ni),
        in_specs=[pl.BlockSpec((tm, D_MODEL), lambda j, i: (i, 0)),
                  pl.BlockSpec((D_MODEL, tn), lambda j, i: (0, j)),
                  pl.BlockSpec((tm, tn), lambda j, i: (i, j)),
                  pl.BlockSpec((tm, tn), lambda j, i: (i, j))],
        out_specs=[pl.BlockSpec((tm, tn), lambda j, i: (i, j)), pl.BlockSpec((tm, tn), lambda j, i: (i, j)),
                   pl.BlockSpec((1, 8, 128), lambda j, i: (i * nj + j, 0, 0))],
        out_shape=[jax.ShapeDtypeStruct((L, D_MODEL), F32), jax.ShapeDtypeStruct((L, D_MODEL), BF16),
                   jax.ShapeDtypeStruct((ni * nj, 8, 128), F32)],
        compiler_params=_cp(("parallel", "parallel")),
    )(merged, w_out, x, target)


def _merge_bwd(d_m, og, o, yg, gpre, proj, b_glu, wa, ws):
    L = og.shape[0]
    tm = _tile(L, 256)

    def body(dm_ref, og_ref, o_ref, yg_ref, gp_ref, za0_ref, za1_ref, zs0_ref, zs1_ref, b_ref, wa_ref, ws_ref,
             do_ref, dza_ref, dzs_ref, dg_ref, dyg_ref, gwa_ref, gws_ref, gb_ref):
        i = pl.program_id(0)

        @pl.when(i == 0)
        def _():
            gwa_ref[...] = jnp.zeros_like(gwa_ref)
            gws_ref[...] = jnp.zeros_like(gws_ref)
            gb_ref[...] = jnp.zeros_like(gb_ref)

        za = jnp.concatenate([za0_ref[...], za1_ref[...]], axis=1)
        zs = jnp.concatenate([zs0_ref[...], zs1_ref[...]], axis=1)
        ogv, dma = og_ref[...], dm_ref[:, :ATTN_W]
        ra = lax.rsqrt(jnp.mean(ogv * ogv, axis=-1, keepdims=True) + NORM_EPS)
        xh = ogv * ra
        gwa_ref[...] += jnp.sum(dma * xh, axis=0, keepdims=True)
        gx = dma * wa_ref[...]
        d_og = ra * (gx - xh * jnp.mean(gx * xh, axis=-1, keepdims=True))
        do_ref[...] = d_og * _silu(za)
        dza_ref[...] = (d_og * o_ref[...] * _dsilu(za)).astype(BF16)
        ygv = yg_ref[...]
        sg = _sigmoid(gp_ref[...] + b_ref[...])
        y2 = ygv * sg
        sz = _silu(zs)
        os_ = y2 * sz
        dms = dm_ref[:, ATTN_W:]
        rs = lax.rsqrt(jnp.mean(os_ * os_, axis=-1, keepdims=True) + NORM_EPS)
        xs = os_ * rs
        gws_ref[...] += jnp.sum(dms * xs, axis=0, keepdims=True)
        gxs = dms * ws_ref[...]
        d_os = rs * (gxs - xs * jnp.mean(gxs * xs, axis=-1, keepdims=True))
        dzs_ref[...] = (d_os * y2 * _dsilu(zs)).astype(BF16)
        d_y2 = d_os * sz
        d_g = d_y2 * ygv * sg * (1.0 - sg)
        dg_ref[...] = d_g.astype(BF16)
        gb_ref[...] += jnp.sum(d_g, axis=0, keepdims=True)
        dyg_ref[...] = d_y2 * sg

    row = lambda w: pl.BlockSpec((1, w), lambda i: (0, 0))
    full = lambda w: pl.BlockSpec((tm, w), lambda i: (i, 0))
    half = lambda c: pl.BlockSpec((tm, 512), lambda i: (i, c))
    return pl.pallas_call(
        body,
        name="merge_bwd",
        grid=(L // tm,),
        in_specs=[full(D_MODEL), full(ATTN_W), full(ATTN_W), full(SSM_W), full(SSM_W),
                  half(3), half(4), half(7), half(8), row(SSM_W), row(ATTN_W), row(SSM_W)],
        out_specs=[full(ATTN_W), full(ATTN_W), full(SSM_W), full(SSM_W), full(SSM_W),
                   row(ATTN_W), row(SSM_W), row(SSM_W)],
        out_shape=[jax.ShapeDtypeStruct((L, ATTN_W), F32), jax.ShapeDtypeStruct((L, ATTN_W), BF16),
                   jax.ShapeDtypeStruct((L, SSM_W), BF16), jax.ShapeDtypeStruct((L, SSM_W), BF16),
                   jax.ShapeDtypeStruct((L, SSM_W), F32),
                   jax.ShapeDtypeStruct((1, ATTN_W), F32), jax.ShapeDtypeStruct((1, SSM_W), F32),
                   jax.ShapeDtypeStruct((1, SSM_W), F32)],
        compiler_params=_cp(("arbitrary",)),
    )(d_m, og, o, yg, gpre, proj, proj, proj, proj, b_glu.reshape(1, SSM_W), wa.reshape(1, ATTN_W),
      ws.reshape(1, SSM_W))


def _rms_bwd_x(x, norm_w, d_hn, d_out, ride):
    L = x.shape[0]
    tm = _tile(L, 256)

    def body(x_ref, w_ref, dh_ref, do_ref, gx_ref, gw_ref):
        i = pl.program_id(0)

        @pl.when(i == 0)
        def _():
            gw_ref[...] = jnp.zeros_like(gw_ref)

        xv, dh = x_ref[...], dh_ref[...]
        r = lax.rsqrt(jnp.mean(xv * xv, axis=-1, keepdims=True) + NORM_EPS)
        xh = xv * r
        gw_ref[...] += jnp.sum(dh * xh, axis=0, keepdims=True)
        gx = dh * w_ref[...]
        gx_ref[...] = do_ref[...] + r * (gx - xh * jnp.mean(gx * xh, axis=-1, keepdims=True))

    blk = pl.BlockSpec((tm, D_MODEL), lambda i: (i, 0))
    row = pl.BlockSpec((1, D_MODEL), lambda i: (0, 0))
    return _call(body, "rms_bwd_x", (L // tm,), [blk, row, blk, blk], [blk, row],
                 [jax.ShapeDtypeStruct((L, D_MODEL), F32), jax.ShapeDtypeStruct((1, D_MODEL), F32)],
                 (x, norm_w.reshape(1, D_MODEL), d_hn, d_out), ride=ride)


def _rope_table(positions):
    inv_freq = ROPE_THETA ** (-jnp.arange(0, HEAD_DIM, 2, dtype=F32) / HEAD_DIM)
    ang = positions.astype(F32)[:, None] * inv_freq
    c, s = jnp.cos(ang), jnp.sin(ang)
    return jnp.concatenate([c, c, c, c, -s, s, -s, s], axis=1)


def _step(x, positions, target, w, core, chip):
    small = {n: w[n] for n in _SMALL}
    tab = _rope_table(positions)
    mt_b, scat_b, ocat_b, a16 = _ssm_prep(small)
    perm = _chunk_perm()
    blocks = lambda t: t.reshape(N_DEV, t.shape[0] // N_DEV, t.shape[1])

    (wt_in,) = _run_exchange(_gather_exchange([w["w_in"].T.astype(BF16)]), "gather_w_in")
    wt_in = wt_in.reshape(IN_W, D_MODEL)

    (proj, hn), (w_glu, w_out) = _rms_inproj(
        x, small["norm_w"], wt_in, _gather_exchange([w["w_glu"].astype(BF16), w["w_out"].astype(BF16)]))
    w_glu, w_out = w_glu.reshape(SSM_W, SSM_W), w_out.reshape(D_MODEL, D_MODEL)
    q_rot, k_rot = _qk_prep(proj, tab, small["q_norm_w"], small["k_norm_w"])
    og, o, lse = _attn_fwd(q_rot, k_rot, proj, small["sinks"])
    y, yg, hx = _ssm_fwd(proj, perm, mt_b, scat_b, ocat_b, a16, small["d_skip"])
    gpre = _mm(yg, w_glu, "nn", F32, "glu_fwd")
    merged = _merge(og, yg, gpre, proj, small["b_glu"], small["attn_out_norm_w"], small["ssm_out_norm_w"])
    d_out, d_out_b, loss_parts = _outproj_loss(merged, w_out, x, target)
    loss = 0.5 * jnp.sum(loss_parts[:, 0, 0]) / D_MODEL

    g_w_out = blocks(_mm(merged, d_out_b, "tn", F32, "grad_w_out"))
    d_m = _mm(d_out_b, w_out, "nt", F32, "d_merged")
    d_o, d_za, d_zs, d_g, d_yg1, g_wa, g_ws, g_bglu = _merge_bwd(
        d_m, og, o, yg, gpre, proj, small["b_glu"], small["attn_out_norm_w"], small["ssm_out_norm_w"])
    g_w_glu = blocks(_mm(yg, d_g, "tn", F32, "grad_w_glu"))
    d_yg = _mm(d_g, w_glu, "nt", F32, "d_yg", add=d_yg1)
    (d_u, g_mt, g_scat, g_ocat, g_a16, g_dskip), (ra_out, ra_glu) = _ssm_bwd(
        d_yg, y, proj, hx, perm, mt_b, scat_b, ocat_b, a16, small["d_skip"], _pair_exchange([g_w_out, g_w_glu]))
    p_out = _pair_sum(g_w_out, ra_out, core, BF16, "pair_sum_out")
    p_glu = _pair_sum(g_w_glu, ra_glu, core, BF16, "pair_sum_glu")
    (d_q, d_k, d_v, g_sinks), (rb_out, rb_glu) = _attn_bwd(
        q_rot, k_rot, proj, small["sinks"], d_o, o, lse, _chip_exchange([p_out, p_glu]))
    d_proj, g_qw, g_kw = _qk_prep_bwd(proj, tab, small["q_norm_w"], small["k_norm_w"], d_q, d_k, d_v,
                                      d_za, d_u, d_zs)
    g_qw = g_qw[0, :HEAD_DIM] + g_qw[0, HEAD_DIM:]
    g_kw = g_kw[0, :HEAD_DIM] + g_kw[0, HEAD_DIM:]
    g_in_a = blocks(_mm(d_proj, hn, "tn", F32, "grad_w_in_a", panel=0))
    g_in_b, (ra_a,) = _mm(d_proj, hn, "tn", F32, "grad_w_in_b", panel=1, ride=_pair_exchange([g_in_a]))
    g_in_b = blocks(g_in_b)
    p_a = _pair_sum(g_in_a, ra_a, core, BF16, "pair_sum_in_a")
    d_hn, (rb_a, ra_b) = _mm(d_proj, wt_in, "nn", F32, "d_hn",
                             ride=_both(_chip_exchange([p_a]), _pair_exchange([g_in_b])))
    p_b = _pair_sum(g_in_b, ra_b, core, BF16, "pair_sum_in_b")
    g_small, (rb_b,) = _ssm_prep_bwd(small, g_mt, g_scat, g_ocat, g_a16, _chip_exchange([p_b]))
    (grad_x, g_nw), _ = _rms_bwd_x(x, small["norm_w"], d_hn, d_out, None)
    g_wt_in = jnp.concatenate([_chip_sum(p_a, rb_a, chip, "chip_sum_in_a"),
                               _chip_sum(p_b, rb_b, chip, "chip_sum_in_b")], axis=1)

    g_small.update(norm_w=g_nw.reshape(-1), q_norm_w=g_qw.reshape(-1), k_norm_w=g_kw.reshape(-1),
                   sinks=g_sinks[0, :N_HEADS], d_skip=g_dskip.reshape(-1), b_glu=g_bglu.reshape(-1),
                   attn_out_norm_w=g_wa.reshape(-1), ssm_out_norm_w=g_ws.reshape(-1))
    slab = _pack(g_small, loss).reshape(N_DEV, _PACK_ROWS // N_DEV, 128)
    (ra_s,) = _run_exchange(_pair_exchange([slab]), "pair_exchange_small")
    p_s = _pair_sum(slab, ra_s, core, F32, "pair_sum_small")
    (rb_s,) = _run_exchange(_chip_exchange([p_s]), "chip_exchange_small")
    (g_packed,) = _run_exchange(_gather_exchange([_chip_sum(p_s, rb_s, chip, "chip_sum_small")]), "gather_small")

    g_packed = g_packed.reshape(_PACK_ROWS, 128)
    grads = _unpack(g_packed, w)
    grads.update(w_in=g_wt_in.T,
                 w_glu=_chip_sum(p_glu, rb_glu, chip, "chip_sum_glu"),
                 w_out=_chip_sum(p_out, rb_out, chip, "chip_sum_out"))
    return g_packed[_LOSS_ROW, 0], grad_x, grads


_ANY = pl.BlockSpec(memory_space=pl.ANY)


class _Exchange:
    def __init__(self, arrays, out_shape, sems, start, finish):
        self.arrays, self.out_shape, self.sems, self.start, self.finish = arrays, out_shape, sems, start, finish


def _gather_exchange(blocks):
    n = len(blocks)

    def parts(ins, outs, sems):
        send_sems, recv_sems, local_sems = sems
        x, y, c = lax.axis_index("x"), lax.axis_index("y"), lax.axis_index("c")
        me, sibling = (x, y, c), (x, y, 1 - c)
        chips = [(1 - x, y), (x, 1 - y), (1 - x, 1 - y)]

        def slot(k, dev):
            return outs[k].at[4 * dev[0] + 2 * dev[1] + dev[2]]

        def copy(k, q, block, to, src=None):
            return pltpu.make_async_remote_copy(
                src_ref=slot(k, block) if src is None else src, dst_ref=slot(k, block),
                send_sem=send_sems.at[k, q], recv_sem=recv_sems.at[k, q], device_id=to, device_id_type=MESH)

        mine = [pltpu.make_async_copy(ins[k], slot(k, me), local_sems.at[k]) for k in range(n)]
        first = []
        for k in range(n):
            first.append(copy(k, 0, me, sibling, src=ins[k]))
            first += [copy(k, 1 + j, me, (*chip, c), src=ins[k]) for j, chip in enumerate(chips)]
        return me, sibling, chips, c, copy, mine, first

    def start(ins, outs, sems):
        *_, mine, first = parts(ins, outs, sems)
        for cp in mine + first:
            cp.start()

    def finish(ins, outs, sems):
        me, sibling, chips, c, copy, mine, first = parts(ins, outs, sems)
        passed = []
        for j, chip in enumerate(chips):
            for k in range(n):
                copy(k, 1 + j, (*chip, c), me).wait_recv()
                fwd = copy(k, 4 + j, (*chip, c), sibling)
                fwd.start()
                passed.append(fwd)
        for k in range(n):
            copy(k, 0, sibling, me).wait_recv()
            for j, chip in enumerate(chips):
                copy(k, 4 + j, (*chip, 1 - c), me).wait_recv()
        for cp in first + passed:
            cp.wait_send()
        for cp in mine:
            cp.wait()

    return _Exchange(blocks, [jax.ShapeDtypeStruct((N_DEV,) + b.shape, b.dtype) for b in blocks],
                     [pltpu.SemaphoreType.DMA((n, 7)), pltpu.SemaphoreType.DMA((n, 7)), pltpu.SemaphoreType.DMA((n,))],
                     start, finish)


def _direct_exchange(arrays, out_lead, fan, route):
    n = len(arrays)

    def copies(ins, outs, sems):
        send_sems, recv_sems = sems
        legs = route(lax.axis_index("x"), lax.axis_index("y"), lax.axis_index("c"))
        return [pltpu.make_async_remote_copy(
            src_ref=ins[k].at[src], dst_ref=outs[k].at[q], send_sem=send_sems.at[k, q], recv_sem=recv_sems.at[k, q],
            device_id=to, device_id_type=MESH) for k in range(n) for src, q, to in legs]

    def start(ins, outs, sems):
        for cp in copies(ins, outs, sems):
            cp.start()

    def finish(ins, outs, sems):
        for cp in copies(ins, outs, sems):
            cp.wait()

    return _Exchange(arrays, [jax.ShapeDtypeStruct((out_lead,) + a.shape[1:], a.dtype) for a in arrays],
                     [pltpu.SemaphoreType.DMA((n, fan)), pltpu.SemaphoreType.DMA((n, fan))], start, finish)


def _pair_exchange(grads):
    return _direct_exchange(grads, 4, 4, lambda x, y, c: [(2 * chip + (1 - c), chip, (x, y, 1 - c))
                                                          for chip in range(4)])


def _chip_exchange(parts):
    def route(x, y, c):
        chips = [(1 - x, y), (x, 1 - y), (1 - x, 1 - y)]
        return [(2 * chip[0] + chip[1], q, (*chip, c)) for q, chip in enumerate(chips)]
    return _direct_exchange(parts, 3, 3, route)


def _both(ex1, ex2):
    n1, s1 = len(ex1.arrays), len(ex1.sems)

    def halves(ins, outs, sems):
        return (ins[:n1], outs[:n1], sems[:s1]), (ins[n1:], outs[n1:], sems[s1:])

    def start(ins, outs, sems):
        h1, h2 = halves(ins, outs, sems)
        ex1.start(*h1)
        ex2.start(*h2)

    def finish(ins, outs, sems):
        h1, h2 = halves(ins, outs, sems)
        ex1.finish(*h1)
        ex2.finish(*h2)

    return _Exchange(list(ex1.arrays) + list(ex2.arrays), list(ex1.out_shape) + list(ex2.out_shape),
                     list(ex1.sems) + list(ex2.sems), start, finish)


def _run_exchange(ex, name):
    n = len(ex.arrays)

    def body(*refs):
        ins, outs, sems = refs[:n], refs[n:2 * n], refs[2 * n:]
        ex.start(ins, outs, sems)
        ex.finish(ins, outs, sems)

    return list(pl.pallas_call(body, name=name, in_specs=[_ANY] * n, out_specs=[_ANY] * n, out_shape=ex.out_shape,
                               scratch_shapes=ex.sems)(*ex.arrays))


def _call(body, name, grid, in_specs, out_specs, out_shape, args, scratch_shapes=(), ride=None):
    if ride is None:
        sem = ("arbitrary",) * len(grid)
        return pl.pallas_call(body, name=name, grid=grid, in_specs=in_specs, out_specs=out_specs, out_shape=out_shape,
                              scratch_shapes=list(scratch_shapes), compiler_params=_cp(sem))(*args), None
    n_in, n_out, n_scr, n_x = len(in_specs), len(out_specs), len(scratch_shapes), len(ride.arrays)

    def wrapped(*refs):
        ins, refs = refs[:n_in], refs[n_in:]
        x_in, refs = refs[:n_x], refs[n_x:]
        outs, refs = refs[:n_out], refs[n_out:]
        x_out, refs = refs[:n_x], refs[n_x:]
        scr, sems = refs[:n_scr], refs[n_scr:]
        first = pl.program_id(0) == 0
        last = pl.program_id(0) == grid[0] - 1
        for a in range(1, len(grid)):
            first = jnp.logical_and(first, pl.program_id(a) == 0)
            last = jnp.logical_and(last, pl.program_id(a) == grid[a] - 1)

        @pl.when(first)
        def _():
            ride.start(x_in, x_out, sems)

        body(*ins, *outs, *scr)

        @pl.when(last)
        def _():
            ride.finish(x_in, x_out, sems)

    res = pl.pallas_call(
        wrapped, name=name, grid=grid, in_specs=list(in_specs) + [_ANY] * n_x,
        out_specs=list(out_specs) + [_ANY] * n_x, out_shape=list(out_shape) + list(ride.out_shape),
        scratch_shapes=list(scratch_shapes) + list(ride.sems),
        compiler_params=_cp(("arbitrary",) * len(grid)))(*args, *ride.arrays)
    return res[:n_out], list(res[n_out:])


def _pair_sum(g, ra, core, out_dtype, name):
    _, r, C = g.shape
    tr = _tile(r, 576)

    def body(c_ref, g_ref, ra_ref, p_ref):
        p_ref[...] = (g_ref[...] + ra_ref[...]).astype(p_ref.dtype)

    return pl.pallas_call(
        body,
        name=name,
        grid_spec=pltpu.PrefetchScalarGridSpec(
            num_scalar_prefetch=1,
            grid=(4, r // tr),
            in_specs=[pl.BlockSpec((1, tr, C), lambda j, t, c_ref: (2 * j + c_ref[0], t, 0)),
                      pl.BlockSpec((1, tr, C), lambda j, t, c_ref: (j, t, 0))],
            out_specs=pl.BlockSpec((1, tr, C), lambda j, t, c_ref: (j, t, 0)),
        ),
        out_shape=jax.ShapeDtypeStruct((4, r, C), out_dtype),
        compiler_params=_cp(("parallel", "parallel")),
    )(core, g, ra)


def _chip_sum(p, rb, chip, name):
    _, r, C = p.shape
    tr = _tile(r, 576)

    def body(c_ref, p_ref, rb_ref, o_ref):
        acc = p_ref[0].astype(F32) + rb_ref[0].astype(F32)
        acc = acc + rb_ref[1].astype(F32)
        o_ref[...] = acc + rb_ref[2].astype(F32)

    return pl.pallas_call(
        body,
        name=name,
        grid_spec=pltpu.PrefetchScalarGridSpec(
            num_scalar_prefetch=1,
            grid=(r // tr,),
            in_specs=[pl.BlockSpec((1, tr, C), lambda t, c_ref: (c_ref[0], t, 0)),
                      pl.BlockSpec((3, tr, C), lambda t, c_ref: (0, t, 0))],
            out_specs=pl.BlockSpec((tr, C), lambda t, c_ref: (t, 0)),
        ),
        out_shape=jax.ShapeDtypeStruct((r, C), F32),
        compiler_params=_cp(("parallel",)),
    )(chip, p, rb)


def _adamw(g, w, m, v, name):
    R, C = g.shape
    tr = _tile(R, 256)
    c1 = 1.0 - ADAM_B1 ** ADAM_STEP
    c2 = 1.0 - ADAM_B2 ** ADAM_STEP

    def body(g_ref, w_ref, m_ref, v_ref, d_ref, nm_ref, nv_ref):
        gv = g_ref[...]
        nm = ADAM_B1 * m_ref[...] + (1.0 - ADAM_B1) * gv
        nv = ADAM_B2 * v_ref[...] + (1.0 - ADAM_B2) * (gv * gv)
        nm_ref[...] = nm
        nv_ref[...] = nv
        d_ref[...] = -ADAM_LR * ((nm / c1) / (jnp.sqrt(nv / c2) + ADAM_EPS) + ADAM_WD * w_ref[...])

    blk = pl.BlockSpec((tr, C), lambda i: (i, 0))
    return pl.pallas_call(
        body, name=name, grid=(R // tr,), in_specs=[blk] * 4, out_specs=[blk] * 3,
        out_shape=[jax.ShapeDtypeStruct((R, C), F32)] * 3, compiler_params=_cp(("parallel",)),
    )(g, w, m, v)


_SMALL = ("norm_w", "q_norm_w", "k_norm_w", "sinks", "a_re", "a_im", "log_step", "b_re", "b_im", "c_re", "c_im",
          "d_skip", "b_glu", "attn_out_norm_w", "ssm_out_norm_w")
_WEIGHTS = ("norm_w", "w_in", "q_norm_w", "k_norm_w", "sinks", "a_re", "a_im", "log_step", "b_re", "b_im", "c_re",
            "c_im", "d_skip", "w_glu", "b_glu", "attn_out_norm_w", "ssm_out_norm_w", "w_out")
_SMALL_2D = dict(norm_w=(1, 2048), q_norm_w=(1, 64), k_norm_w=(1, 64), sinks=(1, 16), a_re=(64, 64), a_im=(64, 64),
                 log_step=(1, 64), b_re=(4096, 16), b_im=(4096, 16), c_re=(1024, 64), c_im=(1024, 64),
                 d_skip=(1, 1024), b_glu=(1, 1024), attn_out_norm_w=(1, 1024), ssm_out_norm_w=(1, 1024))


def _slab_rows(n):
    return -(-n // 1024) * 8


_PACK_ROWS = 2304


_LOSS_ROW = 2192


def _pack(d, loss):
    parts = []
    for n in _SMALL:
        flat = d[n].reshape(-1).astype(F32)
        rows = _slab_rows(flat.shape[0])
        parts.append(jnp.pad(flat, (0, rows * 128 - flat.shape[0])).reshape(rows, 128))
    assert sum(p.shape[0] for p in parts) == _LOSS_ROW
    parts.append(jnp.pad(loss.reshape(1, 1), ((0, _PACK_ROWS - _LOSS_ROW - 1), (0, 127))))
    return jnp.concatenate(parts, axis=0)


def _unpack(packed, like):
    out, off = {}, 0
    for n in _SMALL:
        size = math.prod(like[n].shape)
        rows = _slab_rows(size)
        out[n] = packed[off:off + rows].reshape(-1)[:size].reshape(like[n].shape)
        off += rows
    return out


def _adamw_small(g, w, m, v):
    c1 = 1.0 - ADAM_B1 ** ADAM_STEP
    c2 = 1.0 - ADAM_B2 ** ADAM_STEP
    k = len(_SMALL)

    def body(*refs):
        ins, outs = refs[:4 * k], refs[4 * k:]
        for j in range(k):
            gv, wv, mv, vv = (ins[q * k + j][...] for q in range(4))
            nm = ADAM_B1 * mv + (1.0 - ADAM_B1) * gv
            nv = ADAM_B2 * vv + (1.0 - ADAM_B2) * (gv * gv)
            outs[j][...] = -ADAM_LR * ((nm / c1) / (jnp.sqrt(nv / c2) + ADAM_EPS) + ADAM_WD * wv)
            outs[k + j][...] = nm
            outs[2 * k + j][...] = nv

    args = [d[n].reshape(_SMALL_2D[n]) for d in (g, w, m, v) for n in _SMALL]
    shapes = [jax.ShapeDtypeStruct(_SMALL_2D[n], F32) for _ in range(3) for n in _SMALL]
    outs = pl.pallas_call(body, name="adamw_small", out_shape=shapes, compiler_params=_cp())(*args)
    res = []
    for q in range(3):
        res.append({n: outs[q * k + j].reshape(w[n].shape) for j, n in enumerate(_SMALL)})
    return res


def kernel(x, positions, norm_w, w_in, q_norm_w, k_norm_w, sinks, a_re, a_im, log_step, b_re, b_im, c_re, c_im, d_skip, w_glu, b_glu, attn_out_norm_w, ssm_out_norm_w, w_out, loss_target, m_norm_w, m_w_in, m_q_norm_w, m_k_norm_w, m_sinks, m_a_re, m_a_im, m_log_step, m_b_re, m_b_im, m_c_re, m_c_im, m_d_skip, m_w_glu, m_b_glu, m_attn_out_norm_w, m_ssm_out_norm_w, m_w_out, v_norm_w, v_w_in, v_q_norm_w, v_k_norm_w, v_sinks, v_a_re, v_a_im, v_log_step, v_b_re, v_b_im, v_c_re, v_c_im, v_d_skip, v_w_glu, v_b_glu, v_attn_out_norm_w, v_ssm_out_norm_w, v_w_out):
    w = dict(norm_w=norm_w, w_in=w_in, q_norm_w=q_norm_w, k_norm_w=k_norm_w, sinks=sinks, a_re=a_re, a_im=a_im,
             log_step=log_step, b_re=b_re, b_im=b_im, c_re=c_re, c_im=c_im, d_skip=d_skip, w_glu=w_glu, b_glu=b_glu,
             attn_out_norm_w=attn_out_norm_w, ssm_out_norm_w=ssm_out_norm_w, w_out=w_out)
    m = dict(norm_w=m_norm_w, w_in=m_w_in, q_norm_w=m_q_norm_w, k_norm_w=m_k_norm_w, sinks=m_sinks, a_re=m_a_re,
             a_im=m_a_im, log_step=m_log_step, b_re=m_b_re, b_im=m_b_im, c_re=m_c_re, c_im=m_c_im, d_skip=m_d_skip,
             w_glu=m_w_glu, b_glu=m_b_glu, attn_out_norm_w=m_attn_out_norm_w, ssm_out_norm_w=m_ssm_out_norm_w,
             w_out=m_w_out)
    v = dict(norm_w=v_norm_w, w_in=v_w_in, q_norm_w=v_q_norm_w, k_norm_w=v_k_norm_w, sinks=v_sinks, a_re=v_a_re,
             a_im=v_a_im, log_step=v_log_step, b_re=v_b_re, b_im=v_b_im, c_re=v_c_re, c_im=v_c_im, d_skip=v_d_skip,
             w_glu=v_w_glu, b_glu=v_b_glu, attn_out_norm_w=v_attn_out_norm_w, ssm_out_norm_w=v_ssm_out_norm_w,
             w_out=v_w_out)
    core = lax.axis_index("c").astype(jnp.int32).reshape(1)
    chip = (2 * lax.axis_index("x") + lax.axis_index("y")).astype(jnp.int32).reshape(1)

    loss, grad_x, grads = _step(x[0], positions[0], loss_target[0], w, core, chip)
    delta, new_m, new_v = {}, {}, {}
    for n in ("w_in", "w_glu", "w_out"):
        delta[n], new_m[n], new_v[n] = _adamw(grads[n], w[n], m[n], v[n], f"adamw_{n}")
    d_s, m_s, v_s = _adamw_small(grads, w, m, v)
    delta.update(d_s)
    new_m.update(m_s)
    new_v.update(v_s)

    return (loss, grad_x[None], *[grads[n] for n in _WEIGHTS], *[delta[n] for n in _WEIGHTS],
            *[new_m[n] for n in _WEIGHTS], *[new_v[n] for n in _WEIGHTS])
```

```python
import functools
import math

import jax
import jax.numpy as jnp
from jax import lax
from jax.experimental import pallas as pl
from jax.experimental.pallas import tpu as pltpu

F32 = jnp.float32
BF16 = jnp.bfloat16

D_MODEL = 2048
ATTN_W = 1024
KV_W = 256
SSM_W = 1024
HEAD_DIM = 64
N_HEADS = 16
N_KV = 4
KV_REP = 4
IN_W = 4608
BLOCK = 128
ROPE_THETA = 10000.0
NORM_EPS = 1e-6
SSM_G = 64
SSM_P = 64
SSM_H = 16
CHUNK = 16
CW = CHUNK * SSM_H
N_DEV = 8

ADAM_LR = 0.001
ADAM_B1 = 0.9
ADAM_B2 = 0.999
ADAM_EPS = 1e-08
ADAM_WD = 0.01
ADAM_STEP = 10

VMEM_LIMIT = 56 * 1024 * 1024
MESH = pl.DeviceIdType.MESH


def _cp(sem=None):
    if sem is None:
        return pltpu.CompilerParams(vmem_limit_bytes=VMEM_LIMIT)
    return pltpu.CompilerParams(vmem_limit_bytes=VMEM_LIMIT, dimension_semantics=sem)


def _sigmoid(x):
    return 0.5 * jnp.tanh(0.5 * x) + 0.5


def _silu(x):
    return x * _sigmoid(x)


def _dsilu(x):
    s = _sigmoid(x)
    return s * (1.0 + x * (1.0 - s))


_GELU_C = math.sqrt(2.0 / math.pi)


def _gelu(y):
    t = jnp.tanh(_GELU_C * (y + 0.044715 * y * y * y))
    return 0.5 * y * (1.0 + t)


def _dgelu(y):
    t = jnp.tanh(_GELU_C * (y + 0.044715 * y * y * y))
    return 0.5 * (1.0 + t) + 0.5 * y * (1.0 - t * t) * _GELU_C * (1.0 + 3.0 * 0.044715 * y * y)


def _tile(n, want):
    if n <= want:
        return n
    for t in range(want - want % 16, 0, -16):
        if n % t == 0:
            return t
    raise ValueError((n, want))


def _mm(a, b, mode, out_dtype, name, tm=512, tn=1024, add=None, ride=None, panel=None):
    if mode == "nn":
        (M, K), (K2, N) = a.shape, b.shape
    elif mode == "nt":
        (M, K), (N, K2) = a.shape, b.shape
    else:
        (K, M), (K2, N) = a.shape, b.shape
    assert K == K2
    tm, tn = _tile(M, tm), _tile(N, tn)
    p0 = 0
    if panel is not None:
        assert mode != "nt" and add is None
        p0, N = panel, tn
    dn = {"nn": _NN, "nt": _NT, "tn": _TN}[mode]

    def body(a_ref, b_ref, *rest):
        o_ref = rest[-1]
        acc = lax.dot_general(a_ref[...].astype(BF16), b_ref[...].astype(BF16), dn, preferred_element_type=F32)
        if add is not None:
            acc = acc + rest[0][...]
        o_ref[...] = acc.astype(o_ref.dtype)

    a_spec = pl.BlockSpec((K, tm), lambda j, i: (0, i)) if mode == "tn" else pl.BlockSpec((tm, K), lambda j, i: (i, 0))
    b_spec = (pl.BlockSpec((tn, K), lambda j, i: (j, 0)) if mode == "nt"
              else pl.BlockSpec((K, tn), lambda j, i: (0, j + p0)))
    o_spec = pl.BlockSpec((tm, tn), lambda j, i: (i, j))
    extra = () if add is None else (add,)
    if ride is not None:
        (out,), landed = _call(body, name, (N // tn, M // tm), [a_spec, b_spec] + [o_spec] * len(extra), [o_spec],
                               [jax.ShapeDtypeStruct((M, N), out_dtype)], (a, b, *extra), ride=ride)
        return out, landed
    return pl.pallas_call(
        body,
        name=name,
        grid=(N // tn, M // tm),
        in_specs=[a_spec, b_spec] + [o_spec] * len(extra),
        out_specs=o_spec,
        out_shape=jax.ShapeDtypeStruct((M, N), out_dtype),
        compiler_params=_cp(("parallel", "parallel")),
    )(a, b, *extra)


_CHIP_ORDER = (0, 2, 1, 3)


def _rms_inproj_gather(x, norm_w, wt_shard, chip):
    L = x.shape[0]
    tm = _tile(L, 512)
    ni = L // tm
    r = IN_W // N_DEV
    tn = 2 * r

    def body(chip_ref, x_ref, nw_ref, shard, proj_ref, hn_ref, wt_hbm, hn_scr, w_scr, send_sems, recv_sems, loc_sems):
        jc, i = pl.program_id(0), pl.program_id(1)
        xx, yy, c = lax.axis_index("x"), lax.axis_index("y"), lax.axis_index("c")
        me, sibling = (xx, yy, c), (xx, yy, 1 - c)
        chips = [(1 - xx, yy), (xx, 1 - yy), (1 - xx, 1 - yy)]

        def slot(dev):
            return wt_hbm.at[4 * dev[0] + 2 * dev[1] + dev[2]]

        def copy(q, block, to, src=None):
            return pltpu.make_async_remote_copy(
                src_ref=slot(block) if src is None else src, dst_ref=slot(block),
                send_sem=send_sems.at[q], recv_sem=recv_sems.at[q], device_id=to, device_id_type=MESH)

        def rows_of(buf, core):
            return w_scr.at[buf, pl.ds(pl.multiple_of(core * r, 16), r)]

        mine = pltpu.make_async_copy(shard, slot(me), loc_sems.at[0])
        sends = [copy(0, me, sibling, src=shard)] + [copy(1 + j, me, (*ch, c), src=shard) for j, ch in enumerate(chips)]
        first = jnp.logical_and(jc == 0, i == 0)

        @pl.when(first)
        def _():
            mine.start()
            for cp in sends:
                cp.start()
            own = pltpu.make_async_copy(shard, rows_of(0, c), loc_sems.at[1])
            own.start()
            copy(0, sibling, me).wait_recv()
            sib = pltpu.make_async_copy(slot(sibling), rows_of(0, 1 - c), loc_sems.at[2])
            sib.start()
            own.wait()
            sib.wait()

        for j, ch in enumerate(chips):
            @pl.when(jnp.logical_and(jc == 1 + j, i == 0))
            def _(j=j, ch=ch):
                buf = (1 + j) % 2
                copy(1 + j, (*ch, c), me).wait_recv()
                copy(4 + j, (*ch, c), sibling).start()
                direct = pltpu.make_async_copy(slot((*ch, c)), rows_of(buf, c), loc_sems.at[1])
                direct.start()
                copy(4 + j, (*ch, 1 - c), me).wait_recv()
                passed = pltpu.make_async_copy(slot((*ch, 1 - c)), rows_of(buf, 1 - c), loc_sems.at[2])
                passed.start()
                direct.wait()
                passed.wait()

        rows = pl.ds(pl.multiple_of(i * tm, tm), tm)

        @pl.when(jc == 0)
        def _():
            xv = x_ref[...]
            rstd = lax.rsqrt(jnp.mean(xv * xv, axis=-1, keepdims=True) + NORM_EPS)
            hn = (xv * rstd * nw_ref[...]).astype(BF16)
            hn_scr[rows, :] = hn
            hn_ref[...] = hn

        for buf in range(2):
            @pl.when(jc % 2 == buf)
            def _(buf=buf):
                proj_ref[...] = lax.dot_general(hn_scr[rows, :], w_scr[buf], _NT, preferred_element_type=F32)

        @pl.when(jnp.logical_and(jc == 3, i == ni - 1))
        def _():
            for cp in sends:
                cp.wait_send()
            for j, ch in enumerate(chips):
                copy(4 + j, (*ch, c), sibling).wait_send()
            mine.wait()

    def tile_of(jc, chip_ref):
        mask = jnp.where(jc == 1, _CHIP_ORDER[1], jnp.where(jc == 2, _CHIP_ORDER[2], jnp.where(jc == 3, _CHIP_ORDER[3], 0)))
        return jnp.bitwise_xor(chip_ref[0], mask)

    held = lambda jc, i: jnp.where(jc == 0, i, ni - 1)
    return pl.pallas_call(
        body,
        name="rms_inproj_gather",
        grid_spec=pltpu.PrefetchScalarGridSpec(
            num_scalar_prefetch=1,
            grid=(4, ni),
            in_specs=[pl.BlockSpec((tm, D_MODEL), lambda jc, i, ch: (held(jc, i), 0)),
                      pl.BlockSpec((1, D_MODEL), lambda jc, i, ch: (0, 0)), _ANY],
            out_specs=[pl.BlockSpec((tm, tn), lambda jc, i, ch: (i, tile_of(jc, ch))),
                       pl.BlockSpec((tm, D_MODEL), lambda jc, i, ch: (held(jc, i), 0)), _ANY],
            scratch_shapes=[pltpu.VMEM((L, D_MODEL), BF16), pltpu.VMEM((2, tn, D_MODEL), BF16),
                            pltpu.SemaphoreType.DMA((7,)), pltpu.SemaphoreType.DMA((7,)), pltpu.SemaphoreType.DMA((3,))],
        ),
        out_shape=[jax.ShapeDtypeStruct((L, IN_W), F32), jax.ShapeDtypeStruct((L, D_MODEL), BF16),
                   jax.ShapeDtypeStruct((N_DEV, r, D_MODEL), BF16)],
        compiler_params=_cp(("arbitrary", "arbitrary")),
    )(chip, x, norm_w.reshape(1, D_MODEL), wt_shard)


def _seg_sum(v):
    a = lax.broadcasted_iota(jnp.int32, (128, 128), 0) // HEAD_DIM
    b = lax.broadcasted_iota(jnp.int32, (128, 128), 1) // HEAD_DIM
    ones = jnp.where(a == b, 1.0, 0.0).astype(BF16)
    hi = v.astype(BF16)
    lo = (v - hi.astype(F32)).astype(BF16)
    return jnp.dot(hi, ones, preferred_element_type=F32) + jnp.dot(lo, ones, preferred_element_type=F32)


def _rot_half(t):
    lane = lax.broadcasted_iota(jnp.int32, t.shape, 1)
    return jnp.where(lane % HEAD_DIM < HEAD_DIM // 2, pltpu.roll(t, 128 - HEAD_DIM // 2, 1),
                     pltpu.roll(t, HEAD_DIM // 2, 1))


def _norm_rope(raw, w, cos, sin):
    r = lax.rsqrt(_seg_sum(raw * raw) * (1.0 / HEAD_DIM) + NORM_EPS)
    tn = raw * r * w
    return r, tn * cos + _rot_half(tn) * sin


def _norm_rope_bwd(d_rot, raw, w, cos, sin):
    r = lax.rsqrt(_seg_sum(raw * raw) * (1.0 / HEAD_DIM) + NORM_EPS)
    d_tn = d_rot * cos + _rot_half(d_rot * sin)
    xh = raw * r
    gw = d_tn * w
    d_raw = r * (gw - xh * (_seg_sum(gw * xh) * (1.0 / HEAD_DIM)))
    return d_raw, d_tn * xh


def _band_mask2(has_prev):
    qi = lax.broadcasted_iota(jnp.int32, (2 * BLOCK, 2 * BLOCK), 0) % BLOCK + BLOCK
    kj = lax.broadcasted_iota(jnp.int32, (2 * BLOCK, 2 * BLOCK), 1)
    rel = qi - kj
    return (rel >= 0) & (rel < BLOCK) & ((kj >= BLOCK) | has_prev)


def _half_tiles(pair):
    lo = lax.broadcasted_iota(jnp.int32, pair.shape, 1) < HEAD_DIM
    sw = pltpu.roll(pair, HEAD_DIM, 1)
    z = jnp.zeros_like(pair)
    return (jnp.where(lo, pair, z).astype(BF16), jnp.where(lo, z, sw).astype(BF16),
            jnp.where(lo, sw, z).astype(BF16), jnp.where(lo, z, pair).astype(BF16))


def _two_rows(top, bottom):
    row = lax.broadcasted_iota(jnp.int32, (2 * BLOCK, 1), 0)
    return jnp.where(row < BLOCK, top, bottom)


def _lane_col(mat, h):
    lane = lax.broadcasted_iota(jnp.int32, mat.shape, 1)
    return jnp.sum(jnp.where(lane == h, mat, 0.0), axis=1, keepdims=True)


_SCALE = 1.0 / math.sqrt(HEAD_DIM)
_NT = (((1,), (1,)), ((), ()))
_NN = (((1,), (0,)), ((), ()))
_TN = (((0,), (0,)), ((), ()))


def _qk_prep(proj, tab, qw, kw):
    L = proj.shape[0]
    tm = _tile(L, 512)

    def body(q_ref, k_ref, t_ref, qw_ref, kw_ref, qo_ref, ko_ref):
        cos, sin = t_ref[:, :128], t_ref[:, 128:]
        for c in range(ATTN_W // 128):
            _, qr = _norm_rope(q_ref[:, c * 128:(c + 1) * 128], qw_ref[...], cos, sin)
            qo_ref[:, c * 128:(c + 1) * 128] = (qr * _SCALE).astype(BF16)
        for c in range(KV_W // 128):
            _, kr = _norm_rope(k_ref[:, c * 128:(c + 1) * 128], kw_ref[...], cos, sin)
            ko_ref[:, c * 128:(c + 1) * 128] = kr.astype(BF16)

    row = pl.BlockSpec((1, 128), lambda i: (0, 0))
    return pl.pallas_call(
        body,
        name="qk_prep",
        grid=(L // tm,),
        in_specs=[pl.BlockSpec((tm, ATTN_W), lambda i: (i, 0)), pl.BlockSpec((tm, KV_W), lambda i: (i, 4)),
                  pl.BlockSpec((tm, 256), lambda i: (i, 0)), row, row],
        out_specs=[pl.BlockSpec((tm, ATTN_W), lambda i: (i, 0)), pl.BlockSpec((tm, KV_W), lambda i: (i, 0))],
        out_shape=[jax.ShapeDtypeStruct((L, ATTN_W), BF16), jax.ShapeDtypeStruct((L, KV_W), BF16)],
        compiler_params=_cp(("parallel",)),
    )(proj, proj, tab, jnp.tile(qw, 2).reshape(1, 128), jnp.tile(kw, 2).reshape(1, 128))


def _group_tiles(g, kt, vt):
    a, b = divmod(g, 2)
    return kt[a][2 * b], kt[a][2 * b + 1], vt[a][2 * b], vt[a][2 * b + 1]


def _attn_fwd(q, k, proj, sinks, ride):
    L = proj.shape[0]
    nb = L // BLOCK

    def body(q_ref, kc_ref, kp_ref, vc_ref, vp_ref, z0_ref, z1_ref, sink_ref, og_ref, o_ref, lse_ref):
        i = pl.program_id(0)
        mask = _band_mask2(i > 0)
        z = jnp.concatenate([z0_ref[...], z1_ref[...]], axis=1)
        lane = lax.broadcasted_iota(jnp.int32, (BLOCK, 128), 1)
        kt = [_half_tiles(jnp.concatenate([kp_ref[:, a * 128:(a + 1) * 128], kc_ref[:, a * 128:(a + 1) * 128]],
                                          axis=0).astype(F32)) for a in range(2)]
        vt = [_half_tiles(jnp.concatenate([vp_ref[:, a * 128:(a + 1) * 128], vc_ref[:, a * 128:(a + 1) * 128]],
                                          axis=0)) for a in range(2)]
        lse_mat = jnp.zeros((BLOCK, 128), F32)
        outs = []
        for g in range(N_KV):
            k_lo, k_hi, v_lo, v_hi = _group_tiles(g, kt, vt)
            q2 = jnp.concatenate([q_ref[:, 2 * g * 128:(2 * g + 1) * 128],
                                  q_ref[:, (2 * g + 1) * 128:(2 * g + 2) * 128]], axis=0)
            acc = jnp.zeros((2 * BLOCK, 128), F32)
            for half, (kh, vh) in enumerate(((k_lo, v_lo), (k_hi, v_hi))):
                h_top, h_bot = 4 * g + half, 4 * g + 2 + half
                s = jnp.where(mask, lax.dot_general(q2, kh, _NT, preferred_element_type=F32), -1e30)
                sink = _two_rows(sink_ref[h_top], sink_ref[h_bot])
                m = jnp.maximum(jnp.max(s, axis=-1, keepdims=True), sink)
                e = jnp.exp(s - m)
                den = jnp.sum(e, axis=-1, keepdims=True) + jnp.exp(sink - m)
                p = e * (1.0 / den)
                acc = acc + jnp.dot(p.astype(BF16), vh, preferred_element_type=F32)
                lse = m + jnp.log(den)
                lse_mat = jnp.where(lane == h_top, lse[:BLOCK], lse_mat)
                lse_mat = jnp.where(lane == h_bot, lse[BLOCK:], lse_mat)
            outs += [acc[:BLOCK], acc[BLOCK:]]
        o = jnp.concatenate(outs, axis=1)
        o_ref[...] = o
        og_ref[...] = o * _silu(z)
        lse_ref[...] = lse_mat

    prev = lambda i: jnp.maximum(i - 1, 0)
    return _call(
        body, "attn_fwd", (nb,),
        [pl.BlockSpec((BLOCK, ATTN_W), lambda i: (i, 0)),
         pl.BlockSpec((BLOCK, KV_W), lambda i: (i, 0)),
         pl.BlockSpec((BLOCK, KV_W), lambda i: (prev(i), 0)),
         pl.BlockSpec((BLOCK, KV_W), lambda i: (i, 5)),
         pl.BlockSpec((BLOCK, KV_W), lambda i: (prev(i), 5)),
         pl.BlockSpec((BLOCK, 512), lambda i: (i, 3)),
         pl.BlockSpec((BLOCK, 512), lambda i: (i, 4)),
         pl.BlockSpec(memory_space=pltpu.SMEM)],
        [pl.BlockSpec((BLOCK, ATTN_W), lambda i: (i, 0)),
         pl.BlockSpec((BLOCK, ATTN_W), lambda i: (i, 0)),
         pl.BlockSpec((BLOCK, 128), lambda i: (i, 0))],
        [jax.ShapeDtypeStruct((L, ATTN_W), F32), jax.ShapeDtypeStruct((L, ATTN_W), F32),
         jax.ShapeDtypeStruct((L, 128), F32)],
        (q, k, k, proj, proj, proj, proj, sinks), ride=ride)


def _attn_bwd(q, k, proj, sinks, d_o, o, lse, ride):
    L = proj.shape[0]
    nb = L // BLOCK

    def body(q_ref, kc_ref, kp_ref, vc_ref, vp_ref, do_ref, o_ref, lse_ref, sink_ref,
             dq_ref, dk_ref, dv_ref, gs_ref, ck_scr, cv_scr):
        i = pl.program_id(0)

        @pl.when(i == 0)
        def _():
            gs_ref[...] = jnp.zeros_like(gs_ref)
            ck_scr[...] = jnp.zeros_like(ck_scr)
            cv_scr[...] = jnp.zeros_like(cv_scr)

        @pl.when(i == nb)
        def _():
            dk_ref[...] = ck_scr[...]
            dv_ref[...] = cv_scr[...]

        @pl.when(i < nb)
        def _():
            mask = _band_mask2(i > 0)
            lane = lax.broadcasted_iota(jnp.int32, (1, 128), 1)
            lo = lax.broadcasted_iota(jnp.int32, (2 * BLOCK, 128), 1) < HEAD_DIM
            lse_c = lse_ref[...]
            kt = [_half_tiles(jnp.concatenate([kp_ref[:, a * 128:(a + 1) * 128], kc_ref[:, a * 128:(a + 1) * 128]],
                                              axis=0).astype(F32)) for a in range(2)]
            vt = [_half_tiles(jnp.concatenate([vp_ref[:, a * 128:(a + 1) * 128], vc_ref[:, a * 128:(a + 1) * 128]],
                                              axis=0)) for a in range(2)]
            gs = jnp.zeros((1, 128), F32)
            dq_parts = []
            dk_acc = [jnp.zeros((2 * BLOCK, 128), F32) for _ in range(2)]
            dv_acc = [jnp.zeros((2 * BLOCK, 128), F32) for _ in range(2)]
            for g in range(N_KV):
                a, b = divmod(g, 2)
                k_lo, k_hi, v_lo, v_hi = _group_tiles(g, kt, vt)
                t0, t1 = slice(2 * g * 128, (2 * g + 1) * 128), slice((2 * g + 1) * 128, (2 * g + 2) * 128)
                q2 = jnp.concatenate([q_ref[:, t0], q_ref[:, t1]], axis=0)
                do2 = jnp.concatenate([do_ref[:, t0], do_ref[:, t1]], axis=0)
                prod = do2 * jnp.concatenate([o_ref[:, t0], o_ref[:, t1]], axis=0)
                do2_b = do2.astype(BF16)
                dq2 = jnp.zeros((2 * BLOCK, 128), F32)
                dk_h, dv_h = [], []
                for half, (kh, vh) in enumerate(((k_lo, v_lo), (k_hi, v_hi))):
                    h_top, h_bot = 4 * g + half, 4 * g + 2 + half
                    lse = jnp.concatenate([_lane_col(lse_c, h_top), _lane_col(lse_c, h_bot)], axis=0)
                    sink = _two_rows(sink_ref[h_top], sink_ref[h_bot])
                    delta = jnp.sum(jnp.where(lo == (half == 0), prod, 0.0), axis=1, keepdims=True)
                    s = jnp.where(mask, lax.dot_general(q2, kh, _NT, preferred_element_type=F32), -1e30)
                    p = jnp.exp(s - lse)
                    dp = lax.dot_general(do2_b, vh, _NT, preferred_element_type=F32)
                    ds_b = (p * (dp - delta)).astype(BF16)
                    p_b = p.astype(BF16)
                    dq2 = dq2 + jnp.dot(ds_b, kh, preferred_element_type=F32)
                    dk_h.append(lax.dot_general(ds_b, q2, _TN, preferred_element_type=F32))
                    dv_h.append(lax.dot_general(p_b, do2_b, _TN, preferred_element_type=F32))
                    gsink = -jnp.exp(sink - lse) * delta
                    row = lax.broadcasted_iota(jnp.int32, (2 * BLOCK, 1), 0)
                    gs = gs + jnp.where(lane == h_top, jnp.sum(jnp.where(row < BLOCK, gsink, 0.0)), 0.0)
                    gs = gs + jnp.where(lane == h_bot, jnp.sum(jnp.where(row >= BLOCK, gsink, 0.0)), 0.0)
                dq_parts += [dq2[:BLOCK], dq2[BLOCK:]]
                for acc, parts in ((dk_acc, dk_h), (dv_acc, dv_h)):
                    t = jnp.where(lo, parts[0], parts[1])
                    t = t + pltpu.roll(t, HEAD_DIM, 1)
                    acc[a] = acc[a] + jnp.where(lo == (b == 0), t, 0.0)
            dq_ref[...] = jnp.concatenate(dq_parts, axis=1)
            dk_full = jnp.concatenate(dk_acc, axis=1)
            dv_full = jnp.concatenate(dv_acc, axis=1)
            dk_ref[...] = ck_scr[...] + dk_full[:BLOCK]
            dv_ref[...] = cv_scr[...] + dv_full[:BLOCK]
            ck_scr[...] = dk_full[BLOCK:]
            cv_scr[...] = dv_full[BLOCK:]
            gs_ref[...] += gs

    cur = lambda i: jnp.minimum(i, nb - 1)
    prev = lambda i: jnp.maximum(jnp.minimum(i, nb - 1) - 1, 0)
    done = lambda i: jnp.maximum(i - 1, 0)
    bs = pl.BlockSpec
    return _call(
        body, "attn_bwd", (nb + 1,),
        [bs((BLOCK, ATTN_W), lambda i: (cur(i), 0)),
         bs((BLOCK, KV_W), lambda i: (cur(i), 0)), bs((BLOCK, KV_W), lambda i: (prev(i), 0)),
         bs((BLOCK, KV_W), lambda i: (cur(i), 5)), bs((BLOCK, KV_W), lambda i: (prev(i), 5)),
         bs((BLOCK, ATTN_W), lambda i: (cur(i), 0)), bs((BLOCK, ATTN_W), lambda i: (cur(i), 0)),
         bs((BLOCK, 128), lambda i: (cur(i), 0)), bs(memory_space=pltpu.SMEM)],
        [bs((BLOCK, ATTN_W), lambda i: (cur(i), 0)),
         bs((BLOCK, KV_W), lambda i: (done(i), 0)), bs((BLOCK, KV_W), lambda i: (done(i), 0)),
         bs((1, 128), lambda i: (0, 0))],
        [jax.ShapeDtypeStruct((L, ATTN_W), F32), jax.ShapeDtypeStruct((L, KV_W), F32),
         jax.ShapeDtypeStruct((L, KV_W), F32), jax.ShapeDtypeStruct((1, 128), F32)],
        (q, k, k, proj, proj, d_o, o, lse, sinks),
        [pltpu.VMEM((BLOCK, KV_W), F32), pltpu.VMEM((BLOCK, KV_W), F32)], ride)


def _qk_prep_bwd(proj, tab, qw, kw, d_q, d_k, d_v, d_za, d_u, d_zs):
    L = proj.shape[0]
    tm = _tile(L, 512)
    z0 = ATTN_W + 2 * KV_W

    def body(q_ref, k_ref, t_ref, qw_ref, kw_ref, dq_ref, dk_ref, dv_ref, dza_ref, du_ref, dzs_ref,
             out_ref, gq_ref, gk_ref):
        i = pl.program_id(0)

        @pl.when(i == 0)
        def _():
            gq_ref[...] = jnp.zeros_like(gq_ref)
            gk_ref[...] = jnp.zeros_like(gk_ref)

        cos, sin = t_ref[:, :128], t_ref[:, 128:]
        gq = jnp.zeros((1, 128), F32)
        gk = jnp.zeros((1, 128), F32)
        for c in range(ATTN_W // 128):
            cs = slice(c * 128, (c + 1) * 128)
            d_raw, gw = _norm_rope_bwd(dq_ref[:, cs] * _SCALE, q_ref[:, cs], qw_ref[...], cos, sin)
            out_ref[:, cs] = d_raw.astype(BF16)
            gq = gq + jnp.sum(gw, axis=0, keepdims=True)
        for c in range(KV_W // 128):
            cs = slice(c * 128, (c + 1) * 128)
            d_raw, gw = _norm_rope_bwd(dk_ref[:, cs], k_ref[:, cs], kw_ref[...], cos, sin)
            out_ref[:, ATTN_W + c * 128:ATTN_W + (c + 1) * 128] = d_raw.astype(BF16)
            gk = gk + jnp.sum(gw, axis=0, keepdims=True)
        out_ref[:, ATTN_W + KV_W:z0] = dv_ref[...].astype(BF16)
        out_ref[:, z0:z0 + ATTN_W] = dza_ref[...]
        out_ref[:, z0 + ATTN_W:z0 + ATTN_W + SSM_W] = du_ref[...].astype(BF16)
        out_ref[:, z0 + ATTN_W + SSM_W:] = dzs_ref[...]
        gq_ref[...] += gq
        gk_ref[...] += gk

    row = pl.BlockSpec((1, 128), lambda i: (0, 0))
    blk = lambda w, c: pl.BlockSpec((tm, w), lambda i: (i, c))
    return pl.pallas_call(
        body,
        name="qk_prep_bwd",
        grid=(L // tm,),
        in_specs=[blk(ATTN_W, 0), blk(KV_W, 4), blk(256, 0), row, row, blk(ATTN_W, 0), blk(KV_W, 0), blk(KV_W, 0),
                  blk(ATTN_W, 0), blk(SSM_W, 0), blk(SSM_W, 0)],
        out_specs=[blk(IN_W, 0), row, row],
        out_shape=[jax.ShapeDtypeStruct((L, IN_W), BF16), jax.ShapeDtypeStruct((1, 128), F32),
                   jax.ShapeDtypeStruct((1, 128), F32)],
        compiler_params=_cp(("arbitrary",)),
    )(proj, proj, tab, jnp.tile(qw, 2).reshape(1, 128), jnp.tile(kw, 2).reshape(1, 128), d_q, d_k, d_v,
      d_za, d_u, d_zs)


def _cmul(a, b):
    return a[0] * b[0] - a[1] * b[1], a[0] * b[1] + a[1] * b[0]


def _cmul_conj(a, b):
    return a[0] * b[0] + a[1] * b[1], a[1] * b[0] - a[0] * b[1]


def _cadd(a, b):
    return a[0] + b[0], a[1] + b[1]


def _dot3(a, b, dn):
    ah, bh = a.astype(BF16), b.astype(BF16)
    al, bl = (a - ah.astype(F32)).astype(BF16), (b - bh.astype(F32)).astype(BF16)
    d = lambda u, v: lax.dot_general(u, v, dn, preferred_element_type=F32)
    return d(ah, bh) + d(ah, bl) + d(al, bh)


def _s5_discretise(a_re, a_im, ls, cosx, sinx, bt):
    delta = jnp.exp(ls)
    er = jnp.exp(a_re * delta)
    lb = (er * cosx, er * sinx)
    den = a_re * a_re + a_im * a_im
    coef = _cmul_conj((lb[0] - 1.0, lb[1]), (a_re, a_im))
    coef = (coef[0] / den, coef[1] / den)
    return delta, lb, coef, den, _cmul(coef, bt)


def _powers(lb):
    pw = [(jnp.ones_like(lb[0]), jnp.zeros_like(lb[0]))]
    for _ in range(CHUNK):
        pw.append(_cmul(pw[-1], lb))
    return pw


def _block_rows(a, pw, idx):
    blocks = [_cmul(a, pw[i]) for i in idx]
    return (jnp.concatenate([b[0] for b in blocks], axis=-2), jnp.concatenate([b[1] for b in blocks], axis=-2))


def _block_rows_bwd(g, a, pw, idx, g_pw):
    g_a = (jnp.zeros_like(a[0]), jnp.zeros_like(a[0]))
    for j, i in enumerate(idx):
        gj = (g[0][..., j * SSM_H:(j + 1) * SSM_H, :], g[1][..., j * SSM_H:(j + 1) * SSM_H, :])
        g_a = _cadd(g_a, _cmul_conj(gj, pw[i]))
        gp = _cmul_conj(gj, a)
        g_pw[i] = _cadd(g_pw[i], (jnp.sum(gp[0], axis=-2, keepdims=True), jnp.sum(gp[1], axis=-2, keepdims=True)))
    return g_a


_IDX_S = [CHUNK - 1 - s for s in range(CHUNK)]
_IDX_O = [t + 1 for t in range(CHUNK)]
_IDX_K = list(range(CHUNK))
_PREP_IN = 9


def _prep_args(p):
    row = lambda t: t.reshape(SSM_G, 1, SSM_P)
    xi = p["a_im"] * jnp.exp(p["log_step"])[:, None]
    return (row(p["a_re"]), row(p["a_im"]), row(jnp.broadcast_to(p["log_step"][:, None], (SSM_G, SSM_P))),
            row(jnp.cos(xi)), row(jnp.sin(xi)), p["b_re"].transpose(0, 2, 1), p["b_im"].transpose(0, 2, 1),
            p["c_re"], p["c_im"])


PREP_GROUPS = 8


def _prep_specs():
    r1 = pl.BlockSpec((PREP_GROUPS, 1, SSM_P), lambda g: (g, 0, 0))
    r16 = pl.BlockSpec((PREP_GROUPS, SSM_H, SSM_P), lambda g: (g, 0, 0))
    return [r1] * 5 + [r16] * 4, r1, r16


def _ssm_prep(p):
    def one_group(q, are, aim, ls, cosx, sinx, btr, bti, cre, cim, mt_ref, s_ref, o_ref, a_ref):
        _, lb, _, _, bb = _s5_discretise(are[q], aim[q], ls[q], cosx[q], sinx[q], (btr[q], bti[q]))
        pw = _powers(lb)
        c = (cre[q], cim[q])
        sc = _block_rows(bb, pw, _IDX_S)
        ot = _block_rows(c, pw, _IDX_O)
        ok = _block_rows(c, pw, _IDX_K)
        s_ref[q] = jnp.concatenate([sc[0], sc[1]], axis=1).astype(BF16)
        o_ref[q] = jnp.concatenate([ot[0], -ot[1]], axis=1).astype(BF16)
        a_ref[q] = jnp.concatenate([pw[CHUNK][0], pw[CHUNK][1]], axis=1)
        kt = _dot3(jnp.concatenate([bb[0], -bb[1]], axis=1), jnp.concatenate([ok[0], ok[1]], axis=1), _NT)
        lane = lax.broadcasted_iota(jnp.int32, kt.shape, 1)
        for s in range(CHUNK):
            blk = kt if s == 0 else jnp.where(lane >= SSM_H * s, pltpu.roll(kt, SSM_H * s, 1), 0.0)
            mt_ref[q, s * SSM_H:(s + 1) * SSM_H, :] = blk.astype(BF16)

    def body(*refs):
        for q in range(PREP_GROUPS):
            one_group(q, *refs)

    in_specs, r1, _ = _prep_specs()
    g3 = lambda r, c: pl.BlockSpec((PREP_GROUPS, r, c), lambda g: (g, 0, 0))
    return pl.pallas_call(
        body,
        name="ssm_prep",
        grid=(SSM_G // PREP_GROUPS,),
        in_specs=in_specs,
        out_specs=[g3(CW, CW), g3(CW, 2 * SSM_P), g3(CW, 2 * SSM_P), g3(1, 2 * SSM_P)],
        out_shape=[jax.ShapeDtypeStruct((SSM_G, CW, CW), BF16), jax.ShapeDtypeStruct((SSM_G, CW, 2 * SSM_P), BF16),
                   jax.ShapeDtypeStruct((SSM_G, CW, 2 * SSM_P), BF16),
                   jax.ShapeDtypeStruct((SSM_G, 1, 2 * SSM_P), F32)],
        compiler_params=_cp(("parallel",)),
    )(*_prep_args(p))


def _ssm_prep_bwd(p, g_mt, g_scat, g_ocat, g_a16, ride):
    def body(are, aim, ls, cosx, sinx, btr, bti, cre, cim, gmt_ref, gs_ref, go_ref, ga_ref,
             g_are, g_aim, g_ls, g_btr, g_bti, g_cre, g_cim, ga1_scr, gb1_scr):
        lam = (are[...], aim[...])
        bt = (btr[...], bti[...])
        delta, lb, coef, den, bb = _s5_discretise(lam[0], lam[1], ls[...], cosx[...], sinx[...], bt)
        pw = _powers(lb)
        c = (cre[...], cim[...])
        ok = _block_rows(c, pw, _IDX_K)
        g_pw = [(jnp.zeros_like(lb[0]), jnp.zeros_like(lb[0])) for _ in range(CHUNK + 1)]
        lane = lax.broadcasted_iota(jnp.int32, (SSM_H, CW), 1)
        for q in range(PREP_GROUPS):
            g_kt = gmt_ref[q, :SSM_H, :]
            for s in range(1, CHUNK):
                blk = gmt_ref[q, s * SSM_H:(s + 1) * SSM_H, :]
                g_kt = g_kt + jnp.where(lane < CW - SSM_H * s, pltpu.roll(blk, CW - SSM_H * s, 1), 0.0)
            a1 = jnp.concatenate([bb[0][q], -bb[1][q]], axis=1)
            b1 = jnp.concatenate([ok[0][q], ok[1][q]], axis=1)
            ga1_scr[q] = _dot3(g_kt, b1, _NN)
            gb1_scr[q] = _dot3(g_kt, a1, _TN)
        g_a1, g_b1 = ga1_scr[...], gb1_scr[...]
        g_bb = (g_a1[..., :SSM_P], -g_a1[..., SSM_P:])
        g_c = _block_rows_bwd((g_b1[..., :SSM_P], g_b1[..., SSM_P:]), c, pw, _IDX_K, g_pw)
        gs = gs_ref[...]
        g_bb = _cadd(g_bb, _block_rows_bwd((gs[..., :SSM_P], gs[..., SSM_P:]), bb, pw, _IDX_S, g_pw))
        go = go_ref[...]
        g_c = _cadd(g_c, _block_rows_bwd((go[..., :SSM_P], -go[..., SSM_P:]), c, pw, _IDX_O, g_pw))
        ga = ga_ref[...]
        g_pw[CHUNK] = _cadd(g_pw[CHUNK], (ga[..., :SSM_P], ga[..., SSM_P:]))
        g_lb = (jnp.zeros_like(lb[0]), jnp.zeros_like(lb[0]))
        for l in range(CHUNK - 1, -1, -1):
            g_lb = _cadd(g_lb, _cmul_conj(g_pw[l + 1], pw[l]))
            g_pw[l] = _cadd(g_pw[l], _cmul_conj(g_pw[l + 1], lb))
        g_bt = _cmul_conj(g_bb, coef)
        gc = _cmul_conj(g_bb, bt)
        g_coef = (jnp.sum(gc[0], axis=-2, keepdims=True), jnp.sum(gc[1], axis=-2, keepdims=True))
        lam_den = (lam[0] / den, lam[1] / den)
        g_lb = _cadd(g_lb, _cmul(g_coef, lam_den))
        t = _cmul(_cmul_conj(g_coef, coef), lam_den)
        g_x = _cmul_conj(g_lb, lb)
        g_are[...] = g_x[0] * delta - t[0]
        g_aim[...] = g_x[1] * delta - t[1]
        g_ls[...] = (g_x[0] * lam[0] + g_x[1] * lam[1]) * delta
        g_btr[...] = g_bt[0]
        g_bti[...] = g_bt[1]
        g_cre[...] = g_c[0]
        g_cim[...] = g_c[1]

    in_specs, r1, r16 = _prep_specs()
    g3 = lambda r, c: pl.BlockSpec((PREP_GROUPS, r, c), lambda g: (g, 0, 0))
    rows = jax.ShapeDtypeStruct((SSM_G, 1, SSM_P), F32)
    mats = jax.ShapeDtypeStruct((SSM_G, SSM_H, SSM_P), F32)
    (g_are, g_aim, g_ls, g_btr, g_bti, g_cre, g_cim), landed = _call(
        body, "ssm_prep_bwd", (SSM_G // PREP_GROUPS,),
        in_specs + [g3(CW, CW), g3(CW, 2 * SSM_P), g3(CW, 2 * SSM_P), g3(1, 2 * SSM_P)],
        [r1] * 3 + [r16] * 4, [rows] * 3 + [mats] * 4, (*_prep_args(p), g_mt, g_scat, g_ocat, g_a16),
        [pltpu.VMEM((PREP_GROUPS, SSM_H, 2 * SSM_P), F32), pltpu.VMEM((PREP_GROUPS, CW, 2 * SSM_P), F32)], ride)
    grads = dict(a_re=g_are.reshape(SSM_G, SSM_P), a_im=g_aim.reshape(SSM_G, SSM_P),
                 log_step=jnp.sum(g_ls.reshape(SSM_G, SSM_P), axis=1),
                 b_re=g_btr.transpose(0, 2, 1), b_im=g_bti.transpose(0, 2, 1), c_re=g_cre, c_im=g_cim)
    return grads, landed


def _cmul_const(xv, ar, ai):
    return xv * ar + pltpu.roll(xv, SSM_P, 1) * ai


def _chunk_scan(inc, a_row, reverse):
    n = inc.shape[0]
    lane = lax.broadcasted_iota(jnp.int32, (1, 2 * SSM_P), 1)
    row = lax.broadcasted_iota(jnp.int32, inc.shape, 0)
    sign = jnp.where(lane < SSM_P, -1.0, 1.0)
    ar = jnp.where(lane < SSM_P, a_row, pltpu.roll(a_row, SSM_P, 1))
    ai = jnp.where(lane < SSM_P, pltpu.roll(a_row, SSM_P, 1), a_row)
    if reverse:
        ai = -ai
    xv = inc
    s = 1
    while s < n:
        if reverse:
            sh = jnp.where(row < n - s, pltpu.roll(xv, n - s, 0), 0.0)
        else:
            sh = jnp.where(row >= s, pltpu.roll(xv, s, 0), 0.0)
        xv = xv + _cmul_const(sh, ar, ai * sign)
        ar, ai = ar * ar - ai * ai, 2.0 * ar * ai
        s *= 2
    return xv


def _shift_rows(xv, reverse):
    n = xv.shape[0]
    row = lax.broadcasted_iota(jnp.int32, xv.shape, 0)
    if reverse:
        return jnp.where(row < n - 1, pltpu.roll(xv, n - 1, 0), 0.0)
    return jnp.where(row >= 1, pltpu.roll(xv, 1, 0), 0.0)


GB = 128 // SSM_H
U_COL0 = (ATTN_W + 2 * KV_W + ATTN_W) // 128


HALF = CHUNK // 2


def _chunk_perm():
    r = jnp.arange(HALF * 128)
    t, g8, h = r // 128, (r % 128) // SSM_H, r % SSM_H
    return ((g8 * 128 + t * SSM_H + h)[:, None] == jnp.arange(GB * 128)[None, :]).astype(BF16)


def _load_perm(p_hbm, p_scr, sem):
    @pl.when(pl.program_id(0) == 0)
    def _():
        cp = pltpu.make_async_copy(p_hbm, p_scr, sem)
        cp.start()
        cp.wait()


def _rows_to_chunks(pieces, perm):
    halves = [jnp.dot(jnp.concatenate(pieces[k * HALF:(k + 1) * HALF], axis=1).astype(BF16), perm,
                      preferred_element_type=F32).astype(BF16) for k in range(2)]
    return [jnp.concatenate([hv[:, g * 128:(g + 1) * 128] for hv in halves], axis=1) for g in range(GB)]


def _chunks_to_rows(groups, perm, two_pass):
    pieces = []
    for k in range(2):
        v = jnp.concatenate([gv[:, k * 128:(k + 1) * 128] for gv in groups], axis=1)
        hi = v.astype(BF16)
        out = lax.dot_general(hi, perm, _NT, preferred_element_type=F32)
        if two_pass:
            lo = (v - hi.astype(F32)).astype(BF16)
            out = out + lax.dot_general(lo, perm, _NT, preferred_element_type=F32)
        pieces += [out[:, t * 128:(t + 1) * 128] for t in range(HALF)]
    return pieces


def _ssm_fwd(proj, perm, mt, scat, ocat, a16, d_skip):
    L = proj.shape[0]
    nc = L // CHUNK

    def body(u_ref, p_hbm, mt_ref, s_ref, o_ref, a_ref, d_ref, y_ref, yg_ref, h_ref, p_scr, sem):
        _load_perm(p_hbm, p_scr, sem)
        perm = p_scr[...]
        rows = [pl.ds(t, nc, stride=CHUNK) for t in range(CHUNK)]
        ua = _rows_to_chunks([u_ref[r, :] for r in rows], perm)
        ys = []
        for g in range(GB):
            uv = ua[g]
            inc = jnp.dot(uv, s_ref[g], preferred_element_type=F32)
            hx = _shift_rows(_chunk_scan(inc, a_ref[g], False), False)
            h_ref[g] = hx
            ys.append(jnp.dot(uv, mt_ref[g], preferred_element_type=F32)
                      + lax.dot_general(hx.astype(BF16), o_ref[g], _NT, preferred_element_type=F32))
        yp = _chunks_to_rows(ys, perm, True)
        for t, r in enumerate(rows):
            y = yp[t] + d_ref[...] * u_ref[r, :]
            y_ref[r, :] = y
            yg_ref[r, :] = _gelu(y)

    g3 = lambda r, c: pl.BlockSpec((GB, r, c), lambda g: (g, 0, 0))
    col = pl.BlockSpec((L, 128), lambda g: (0, g))
    return pl.pallas_call(
        body,
        name="ssm_fwd",
        grid=(SSM_G // GB,),
        in_specs=[pl.BlockSpec((L, 128), lambda g: (0, U_COL0 + g)), _ANY,
                  g3(CW, CW), g3(CW, 2 * SSM_P), g3(CW, 2 * SSM_P), g3(1, 2 * SSM_P),
                  pl.BlockSpec((1, 128), lambda g: (0, g))],
        out_specs=[col, col, g3(nc, 2 * SSM_P)],
        out_shape=[jax.ShapeDtypeStruct((L, SSM_W), F32), jax.ShapeDtypeStruct((L, SSM_W), F32),
                   jax.ShapeDtypeStruct((SSM_G, nc, 2 * SSM_P), F32)],
        scratch_shapes=[pltpu.VMEM((HALF * 128, GB * 128), BF16), pltpu.SemaphoreType.DMA],
        compiler_params=_cp(("arbitrary",)),
    )(proj, perm, mt, scat, ocat, a16, d_skip.reshape(1, SSM_W))


def _ssm_bwd(d_yg, y, proj, hx, perm, mt, scat, ocat, a16, d_skip, ride):
    L = proj.shape[0]
    nc = L // CHUNK

    def body(dg_ref, y_ref, u_ref, h_ref, p_hbm, mt_ref, s_ref, o_ref, a_ref, d_ref,
             du_ref, gmt_ref, gs_ref, go_ref, ga_ref, gd_ref, p_scr, sem):
        _load_perm(p_hbm, p_scr, sem)
        perm = p_scr[...]
        rows = [pl.ds(t, nc, stride=CHUNK) for t in range(CHUNK)]
        us = [u_ref[r, :] for r in rows]
        dys = [dg_ref[r, :] * _dgelu(y_ref[r, :]) for r in rows]
        gd = jnp.zeros((1, 128), F32)
        for uv, dy in zip(us, dys):
            gd = gd + jnp.sum(dy * uv, axis=0, keepdims=True)
        gd_ref[...] = gd
        ua = _rows_to_chunks(us, perm)
        dya = _rows_to_chunks(dys, perm)
        lane = lax.broadcasted_iota(jnp.int32, (1, 2 * SSM_P), 1)
        dus = []
        for g in range(GB):
            uv, dy, hx_v = ua[g], dya[g], h_ref[g]
            dh = jnp.dot(dy, o_ref[g], preferred_element_type=F32)
            dinc = _shift_rows(_chunk_scan(dh, a_ref[g], True), True)
            dinc_b = dinc.astype(BF16)
            dus.append(lax.dot_general(dy, mt_ref[g], _NT, preferred_element_type=F32)
                       + lax.dot_general(dinc_b, s_ref[g], _NT, preferred_element_type=F32))
            gmt_ref[g] = lax.dot_general(uv, dy, _TN, preferred_element_type=F32)
            gs_ref[g] = lax.dot_general(uv, dinc_b, _TN, preferred_element_type=F32)
            go_ref[g] = lax.dot_general(dy, hx_v.astype(BF16), _TN, preferred_element_type=F32)
            p1 = dinc * hx_v
            p2 = pltpu.roll(dinc, SSM_P, 1) * hx_v
            t1 = jnp.sum(p1 + pltpu.roll(p1, SSM_P, 1), axis=0, keepdims=True)
            t2 = jnp.sum(p2 - pltpu.roll(p2, SSM_P, 1), axis=0, keepdims=True)
            ga_ref[g] = jnp.where(lane < SSM_P, t1, pltpu.roll(t2, SSM_P, 1))
        dup = _chunks_to_rows(dus, perm, False)
        for t, r in enumerate(rows):
            du_ref[r, :] = dup[t] + d_ref[...] * dys[t]

    g3 = lambda r, c: pl.BlockSpec((GB, r, c), lambda g: (g, 0, 0))
    col = pl.BlockSpec((L, 128), lambda g: (0, g))
    row = pl.BlockSpec((1, 128), lambda g: (0, g))
    return _call(
        body, "ssm_bwd", (SSM_G // GB,),
        [col, col, pl.BlockSpec((L, 128), lambda g: (0, U_COL0 + g)), g3(nc, 2 * SSM_P), _ANY,
         g3(CW, CW), g3(CW, 2 * SSM_P), g3(CW, 2 * SSM_P), g3(1, 2 * SSM_P), row],
        [col, g3(CW, CW), g3(CW, 2 * SSM_P), g3(CW, 2 * SSM_P), g3(1, 2 * SSM_P), row],
        [jax.ShapeDtypeStruct((L, SSM_W), F32), jax.ShapeDtypeStruct((SSM_G, CW, CW), F32),
         jax.ShapeDtypeStruct((SSM_G, CW, 2 * SSM_P), F32), jax.ShapeDtypeStruct((SSM_G, CW, 2 * SSM_P), F32),
         jax.ShapeDtypeStruct((SSM_G, 1, 2 * SSM_P), F32), jax.ShapeDtypeStruct((1, SSM_W), F32)],
        (d_yg, y, proj, hx, perm, mt, scat, ocat, a16, d_skip.reshape(1, SSM_W)),
        [pltpu.VMEM((HALF * 128, GB * 128), BF16), pltpu.SemaphoreType.DMA], ride)


def _merge(og, yg, gpre, proj, b_glu, wa, ws):
    L = og.shape[0]
    tm = _tile(L, 256)

    def body(og_ref, yg_ref, gp_ref, z0_ref, z1_ref, b_ref, wa_ref, ws_ref, m_ref):
        zs = jnp.concatenate([z0_ref[...], z1_ref[...]], axis=1)
        os_ = yg_ref[...] * _sigmoid(gp_ref[...] + b_ref[...]) * _silu(zs)
        ogv = og_ref[...]
        ra = lax.rsqrt(jnp.mean(ogv * ogv, axis=-1, keepdims=True) + NORM_EPS)
        rs = lax.rsqrt(jnp.mean(os_ * os_, axis=-1, keepdims=True) + NORM_EPS)
        m_ref[:, :ATTN_W] = (ogv * ra * wa_ref[...]).astype(BF16)
        m_ref[:, ATTN_W:] = (os_ * rs * ws_ref[...]).astype(BF16)

    row = lambda w: pl.BlockSpec((1, w), lambda i: (0, 0))
    return pl.pallas_call(
        body,
        name="merge",
        grid=(L // tm,),
        in_specs=[pl.BlockSpec((tm, ATTN_W), lambda i: (i, 0)), pl.BlockSpec((tm, SSM_W), lambda i: (i, 0)),
                  pl.BlockSpec((tm, SSM_W), lambda i: (i, 0)),
                  pl.BlockSpec((tm, 512), lambda i: (i, 7)), pl.BlockSpec((tm, 512), lambda i: (i, 8)),
                  row(SSM_W), row(ATTN_W), row(SSM_W)],
        out_specs=pl.BlockSpec((tm, D_MODEL), lambda i: (i, 0)),
        out_shape=jax.ShapeDtypeStruct((L, D_MODEL), BF16),
        compiler_params=_cp(("parallel",)),
    )(og, yg, gpre, proj, proj, b_glu.reshape(1, SSM_W), wa.reshape(1, ATTN_W), ws.reshape(1, SSM_W))


def _outproj_loss(merged, w_out, x, target):
    L = x.shape[0]
    tm, tn = _tile(L, 512), 1024
    ni, nj = L // tm, D_MODEL // tn

    def body(m_ref, w_ref, x_ref, t_ref, d_ref, db_ref, l_ref):
        out = x_ref[...] + jnp.dot(m_ref[...], w_ref[...], preferred_element_type=F32)
        diff = out - t_ref[...]
        d = diff * (1.0 / D_MODEL)
        d_ref[...] = d
        db_ref[...] = d.astype(BF16)
        l_ref[...] = jnp.full((1, 8, 128), jnp.sum(diff * diff), F32)

    return pl.pallas_call(
        body,
        name="outproj_loss",
        grid=(nj, ni),
        in_specs=[pl.BlockSpec((tm, D_MODEL), lambda j, i: (i, 0)),
                  pl.BlockSpec((D_MODEL, tn), lambda j, i: (0, j)),
                  pl.BlockSpec((tm, tn), lambda j, i: (i, j)),
                  pl.BlockSpec((tm, tn), lambda j, i: (i, j))],
        out_specs=[pl.BlockSpec((tm, tn), lambda j, i: (i, j)), pl.BlockSpec((tm, tn), lambda j, i: (i, j)),
                   pl.BlockSpec((1, 8, 128), lambda j, i: (i * nj + j, 0, 0))],
        out_shape=[jax.ShapeDtypeStruct((L, D_MODEL), F32), jax.ShapeDtypeStruct((L, D_MODEL), BF16),
                   jax.ShapeDtypeStruct((ni * nj, 8, 128), F32)],
        compiler_params=_cp(("parallel", "parallel")),
    )(merged, w_out, x, target)


def _merge_bwd(d_m, og, o, yg, gpre, proj, b_glu, wa, ws):
    L = og.shape[0]
    tm = _tile(L, 256)

    def body(dm_ref, og_ref, o_ref, yg_ref, gp_ref, za0_ref, za1_ref, zs0_ref, zs1_ref, b_ref, wa_ref, ws_ref,
             do_ref, dza_ref, dzs_ref, dg_ref, dyg_ref, gwa_ref, gws_ref, gb_ref):
        i = pl.program_id(0)

        @pl.when(i == 0)
        def _():
            gwa_ref[...] = jnp.zeros_like(gwa_ref)
            gws_ref[...] = jnp.zeros_like(gws_ref)
            gb_ref[...] = jnp.zeros_like(gb_ref)

        za = jnp.concatenate([za0_ref[...], za1_ref[...]], axis=1)
        zs = jnp.concatenate([zs0_ref[...], zs1_ref[...]], axis=1)
        ogv, dma = og_ref[...], dm_ref[:, :ATTN_W]
        ra = lax.rsqrt(jnp.mean(ogv * ogv, axis=-1, keepdims=True) + NORM_EPS)
        xh = ogv * ra
        gwa_ref[...] += jnp.sum(dma * xh, axis=0, keepdims=True)
        gx = dma * wa_ref[...]
        d_og = ra * (gx - xh * jnp.mean(gx * xh, axis=-1, keepdims=True))
        do_ref[...] = d_og * _silu(za)
        dza_ref[...] = (d_og * o_ref[...] * _dsilu(za)).astype(BF16)
        ygv = yg_ref[...]
        sg = _sigmoid(gp_ref[...] + b_ref[...])
        y2 = ygv * sg
        sz = _silu(zs)
        os_ = y2 * sz
        dms = dm_ref[:, ATTN_W:]
        rs = lax.rsqrt(jnp.mean(os_ * os_, axis=-1, keepdims=True) + NORM_EPS)
        xs = os_ * rs
        gws_ref[...] += jnp.sum(dms * xs, axis=0, keepdims=True)
        gxs = dms * ws_ref[...]
        d_os = rs * (gxs - xs * jnp.mean(gxs * xs, axis=-1, keepdims=True))
        dzs_ref[...] = (d_os * y2 * _dsilu(zs)).astype(BF16)
        d_y2 = d_os * sz
        d_g = d_y2 * ygv * sg * (1.0 - sg)
        dg_ref[...] = d_g.astype(BF16)
        gb_ref[...] += jnp.sum(d_g, axis=0, keepdims=True)
        dyg_ref[...] = d_y2 * sg

    row = lambda w: pl.BlockSpec((1, w), lambda i: (0, 0))
    full = lambda w: pl.BlockSpec((tm, w), lambda i: (i, 0))
    half = lambda c: pl.BlockSpec((tm, 512), lambda i: (i, c))
    return pl.pallas_call(
        body,
        name="merge_bwd",
        grid=(L // tm,),
        in_specs=[full(D_MODEL), full(ATTN_W), full(ATTN_W), full(SSM_W), full(SSM_W),
                  half(3), half(4), half(7), half(8), row(SSM_W), row(ATTN_W), row(SSM_W)],
        out_specs=[full(ATTN_W), full(ATTN_W), full(SSM_W), full(SSM_W), full(SSM_W),
                   row(ATTN_W), row(SSM_W), row(SSM_W)],
        out_shape=[jax.ShapeDtypeStruct((L, ATTN_W), F32), jax.ShapeDtypeStruct((L, ATTN_W), BF16),
                   jax.ShapeDtypeStruct((L, SSM_W), BF16), jax.ShapeDtypeStruct((L, SSM_W), BF16),
                   jax.ShapeDtypeStruct((L, SSM_W), F32),
                   jax.ShapeDtypeStruct((1, ATTN_W), F32), jax.ShapeDtypeStruct((1, SSM_W), F32),
                   jax.ShapeDtypeStruct((1, SSM_W), F32)],
        compiler_params=_cp(("arbitrary",)),
    )(d_m, og, o, yg, gpre, proj, proj, proj, proj, b_glu.reshape(1, SSM_W), wa.reshape(1, ATTN_W),
      ws.reshape(1, SSM_W))


def _rms_bwd_x(x, norm_w, d_hn, d_out, ride):
    L = x.shape[0]
    tm = _tile(L, 256)

    def body(x_ref, w_ref, dh_ref, do_ref, gx_ref, gw_ref):
        i = pl.program_id(0)

        @pl.when(i == 0)
        def _():
            gw_ref[...] = jnp.zeros_like(gw_ref)

        xv, dh = x_ref[...], dh_ref[...]
        r = lax.rsqrt(jnp.mean(xv * xv, axis=-1, keepdims=True) + NORM_EPS)
        xh = xv * r
        gw_ref[...] += jnp.sum(dh * xh, axis=0, keepdims=True)
        gx = dh * w_ref[...]
        gx_ref[...] = do_ref[...] + r * (gx - xh * jnp.mean(gx * xh, axis=-1, keepdims=True))

    blk = pl.BlockSpec((tm, D_MODEL), lambda i: (i, 0))
    row = pl.BlockSpec((1, D_MODEL), lambda i: (0, 0))
    return _call(body, "rms_bwd_x", (L // tm,), [blk, row, blk, blk], [blk, row],
                 [jax.ShapeDtypeStruct((L, D_MODEL), F32), jax.ShapeDtypeStruct((1, D_MODEL), F32)],
                 (x, norm_w.reshape(1, D_MODEL), d_hn, d_out), ride=ride)


def _rope_table(positions):
    inv_freq = ROPE_THETA ** (-jnp.arange(0, HEAD_DIM, 2, dtype=F32) / HEAD_DIM)
    ang = positions.astype(F32)[:, None] * inv_freq
    c, s = jnp.cos(ang), jnp.sin(ang)
    return jnp.concatenate([c, c, c, c, -s, s, -s, s], axis=1)


def _step(x, positions, target, w, core, chip):
    small = {n: w[n] for n in _SMALL}
    tab = _rope_table(positions)
    mt_b, scat_b, ocat_b, a16 = _ssm_prep(small)
    perm = _chunk_perm()
    blocks = lambda t: t.reshape(N_DEV, t.shape[0] // N_DEV, t.shape[1])

    proj, hn, wt_in = _rms_inproj_gather(x, small["norm_w"], w["w_in"].T.astype(BF16), chip)
    wt_in = wt_in.reshape(IN_W, D_MODEL)
    q_rot, k_rot = _qk_prep(proj, tab, small["q_norm_w"], small["k_norm_w"])
    (og, o, lse), (w_glu, w_out) = _attn_fwd(
        q_rot, k_rot, proj, small["sinks"], _gather_exchange([w["w_glu"].astype(BF16), w["w_out"].astype(BF16)]))
    w_glu, w_out = w_glu.reshape(SSM_W, SSM_W), w_out.reshape(D_MODEL, D_MODEL)
    y, yg, hx = _ssm_fwd(proj, perm, mt_b, scat_b, ocat_b, a16, small["d_skip"])
    gpre = _mm(yg, w_glu, "nn", F32, "glu_fwd")
    merged = _merge(og, yg, gpre, proj, small["b_glu"], small["attn_out_norm_w"], small["ssm_out_norm_w"])
    d_out, d_out_b, loss_parts = _outproj_loss(merged, w_out, x, target)
    loss = 0.5 * jnp.sum(loss_parts[:, 0, 0]) / D_MODEL

    g_w_out = blocks(_mm(merged, d_out_b, "tn", F32, "grad_w_out"))
    d_m = _mm(d_out_b, w_out, "nt", F32, "d_merged")
    d_o, d_za, d_zs, d_g, d_yg1, g_wa, g_ws, g_bglu = _merge_bwd(
        d_m, og, o, yg, gpre, proj, small["b_glu"], small["attn_out_norm_w"], small["ssm_out_norm_w"])
    g_w_glu = blocks(_mm(yg, d_g, "tn", F32, "grad_w_glu"))
    d_yg = _mm(d_g, w_glu, "nt", F32, "d_yg", add=d_yg1)
    (d_u, g_mt, g_scat, g_ocat, g_a16, g_dskip), (ra_out, ra_glu) = _ssm_bwd(
        d_yg, y, proj, hx, perm, mt_b, scat_b, ocat_b, a16, small["d_skip"], _pair_exchange([g_w_out, g_w_glu]))
    p_out = _pair_sum(g_w_out, ra_out, core, BF16, "pair_sum_out")
    p_glu = _pair_sum(g_w_glu, ra_glu, core, BF16, "pair_sum_glu")
    (d_q, d_k, d_v, g_sinks), (rb_out, rb_glu) = _attn_bwd(
        q_rot, k_rot, proj, small["sinks"], d_o, o, lse, _chip_exchange([p_out, p_glu]))
    d_proj, g_qw, g_kw = _qk_prep_bwd(proj, tab, small["q_norm_w"], small["k_norm_w"], d_q, d_k, d_v,
                                      d_za, d_u, d_zs)
    g_qw = g_qw[0, :HEAD_DIM] + g_qw[0, HEAD_DIM:]
    g_kw = g_kw[0, :HEAD_DIM] + g_kw[0, HEAD_DIM:]
    g_in_a = blocks(_mm(d_proj, hn, "tn", F32, "grad_w_in_a", panel=0))
    g_in_b, (ra_a,) = _mm(d_proj, hn, "tn", F32, "grad_w_in_b", panel=1, ride=_pair_exchange([g_in_a]))
    g_in_b = blocks(g_in_b)
    p_a = _pair_sum(g_in_a, ra_a, core, BF16, "pair_sum_in_a")
    d_hn, (rb_a, ra_b) = _mm(d_proj, wt_in, "nn", F32, "d_hn",
                             ride=_both(_chip_exchange([p_a]), _pair_exchange([g_in_b])))
    p_b = _pair_sum(g_in_b, ra_b, core, BF16, "pair_sum_in_b")
    g_small, (rb_b,) = _ssm_prep_bwd(small, g_mt, g_scat, g_ocat, g_a16, _chip_exchange([p_b]))
    (grad_x, g_nw), _ = _rms_bwd_x(x, small["norm_w"], d_hn, d_out, None)
    g_wt_in = jnp.concatenate([_chip_sum(p_a, rb_a, chip, "chip_sum_in_a"),
                               _chip_sum(p_b, rb_b, chip, "chip_sum_in_b")], axis=1)

    g_small.update(norm_w=g_nw.reshape(-1), q_norm_w=g_qw.reshape(-1), k_norm_w=g_kw.reshape(-1),
                   sinks=g_sinks[0, :N_HEADS], d_skip=g_dskip.reshape(-1), b_glu=g_bglu.reshape(-1),
                   attn_out_norm_w=g_wa.reshape(-1), ssm_out_norm_w=g_ws.reshape(-1))
    slab = _pack(g_small, loss).reshape(N_DEV, _PACK_ROWS // N_DEV, 128)
    (ra_s,) = _run_exchange(_pair_exchange([slab]), "pair_exchange_small")
    p_s = _pair_sum(slab, ra_s, core, F32, "pair_sum_small")
    (rb_s,) = _run_exchange(_chip_exchange([p_s]), "chip_exchange_small")
    (g_packed,) = _run_exchange(_gather_exchange([_chip_sum(p_s, rb_s, chip, "chip_sum_small")]), "gather_small")

    g_packed = g_packed.reshape(_PACK_ROWS, 128)
    grads = _unpack(g_packed, w)
    grads.update(w_in=g_wt_in.T,
                 w_glu=_chip_sum(p_glu, rb_glu, chip, "chip_sum_glu"),
                 w_out=_chip_sum(p_out, rb_out, chip, "chip_sum_out"))
    return g_packed[_LOSS_ROW, 0], grad_x, grads


_ANY = pl.BlockSpec(memory_space=pl.ANY)


class _Exchange:
    def __init__(self, arrays, out_shape, sems, start, finish):
        self.arrays, self.out_shape, self.sems, self.start, self.finish = arrays, out_shape, sems, start, finish


def _gather_exchange(blocks):
    n = len(blocks)

    def parts(ins, outs, sems):
        send_sems, recv_sems, local_sems = sems
        x, y, c = lax.axis_index("x"), lax.axis_index("y"), lax.axis_index("c")
        me, sibling = (x, y, c), (x, y, 1 - c)
        chips = [(1 - x, y), (x, 1 - y), (1 - x, 1 - y)]

        def slot(k, dev):
            return outs[k].at[4 * dev[0] + 2 * dev[1] + dev[2]]

        def copy(k, q, block, to, src=None):
            return pltpu.make_async_remote_copy(
                src_ref=slot(k, block) if src is None else src, dst_ref=slot(k, block),
                send_sem=send_sems.at[k, q], recv_sem=recv_sems.at[k, q], device_id=to, device_id_type=MESH)

        mine = [pltpu.make_async_copy(ins[k], slot(k, me), local_sems.at[k]) for k in range(n)]
        first = []
        for k in range(n):
            first.append(copy(k, 0, me, sibling, src=ins[k]))
            first += [copy(k, 1 + j, me, (*chip, c), src=ins[k]) for j, chip in enumerate(chips)]
        return me, sibling, chips, c, copy, mine, first

    def start(ins, outs, sems):
        *_, mine, first = parts(ins, outs, sems)
        for cp in mine + first:
            cp.start()

    def finish(ins, outs, sems):
        me, sibling, chips, c, copy, mine, first = parts(ins, outs, sems)
        passed = []
        for j, chip in enumerate(chips):
            for k in range(n):
                copy(k, 1 + j, (*chip, c), me).wait_recv()
                fwd = copy(k, 4 + j, (*chip, c), sibling)
                fwd.start()
                passed.append(fwd)
        for k in range(n):
            copy(k, 0, sibling, me).wait_recv()
            for j, chip in enumerate(chips):
                copy(k, 4 + j, (*chip, 1 - c), me).wait_recv()
        for cp in first + passed:
            cp.wait_send()
        for cp in mine:
            cp.wait()

    return _Exchange(blocks, [jax.ShapeDtypeStruct((N_DEV,) + b.shape, b.dtype) for b in blocks],
                     [pltpu.SemaphoreType.DMA((n, 7)), pltpu.SemaphoreType.DMA((n, 7)), pltpu.SemaphoreType.DMA((n,))],
                     start, finish)


def _direct_exchange(arrays, out_lead, fan, route):
    n = len(arrays)

    def copies(ins, outs, sems):
        send_sems, recv_sems = sems
        legs = route(lax.axis_index("x"), lax.axis_index("y"), lax.axis_index("c"))
        return [pltpu.make_async_remote_copy(
            src_ref=ins[k].at[src], dst_ref=outs[k].at[q], send_sem=send_sems.at[k, q], recv_sem=recv_sems.at[k, q],
            device_id=to, device_id_type=MESH) for k in range(n) for src, q, to in legs]

    def start(ins, outs, sems):
        for cp in copies(ins, outs, sems):
            cp.start()

    def finish(ins, outs, sems):
        for cp in copies(ins, outs, sems):
            cp.wait()

    return _Exchange(arrays, [jax.ShapeDtypeStruct((out_lead,) + a.shape[1:], a.dtype) for a in arrays],
                     [pltpu.SemaphoreType.DMA((n, fan)), pltpu.SemaphoreType.DMA((n, fan))], start, finish)


def _pair_exchange(grads):
    return _direct_exchange(grads, 4, 4, lambda x, y, c: [(2 * chip + (1 - c), chip, (x, y, 1 - c))
                                                          for chip in range(4)])


def _chip_exchange(parts):
    def route(x, y, c):
        chips = [(1 - x, y), (x, 1 - y), (1 - x, 1 - y)]
        return [(2 * chip[0] + chip[1], q, (*chip, c)) for q, chip in enumerate(chips)]
    return _direct_exchange(parts, 3, 3, route)


def _both(ex1, ex2):
    n1, s1 = len(ex1.arrays), len(ex1.sems)

    def halves(ins, outs, sems):
        return (ins[:n1], outs[:n1], sems[:s1]), (ins[n1:], outs[n1:], sems[s1:])

    def start(ins, outs, sems):
        h1, h2 = halves(ins, outs, sems)
        ex1.start(*h1)
        ex2.start(*h2)

    def finish(ins, outs, sems):
        h1, h2 = halves(ins, outs, sems)
        ex1.finish(*h1)
        ex2.finish(*h2)

    return _Exchange(list(ex1.arrays) + list(ex2.arrays), list(ex1.out_shape) + list(ex2.out_shape),
                     list(ex1.sems) + list(ex2.sems), start, finish)


def _run_exchange(ex, name):
    n = len(ex.arrays)

    def body(*refs):
        ins, outs, sems = refs[:n], refs[n:2 * n], refs[2 * n:]
        ex.start(ins, outs, sems)
        ex.finish(ins, outs, sems)

    return list(pl.pallas_call(body, name=name, in_specs=[_ANY] * n, out_specs=[_ANY] * n, out_shape=ex.out_shape,
                               scratch_shapes=ex.sems)(*ex.arrays))


def _call(body, name, grid, in_specs, out_specs, out_shape, args, scratch_shapes=(), ride=None):
    if ride is None:
        sem = ("arbitrary",) * len(grid)
        return pl.pallas_call(body, name=name, grid=grid, in_specs=in_specs, out_specs=out_specs, out_shape=out_shape,
                              scratch_shapes=list(scratch_shapes), compiler_params=_cp(sem))(*args), None
    n_in, n_out, n_scr, n_x = len(in_specs), len(out_specs), len(scratch_shapes), len(ride.arrays)

    def wrapped(*refs):
        ins, refs = refs[:n_in], refs[n_in:]
        x_in, refs = refs[:n_x], refs[n_x:]
        outs, refs = refs[:n_out], refs[n_out:]
        x_out, refs = refs[:n_x], refs[n_x:]
        scr, sems = refs[:n_scr], refs[n_scr:]
        first = pl.program_id(0) == 0
        last = pl.program_id(0) == grid[0] - 1
        for a in range(1, len(grid)):
            first = jnp.logical_and(first, pl.program_id(a) == 0)
            last = jnp.logical_and(last, pl.program_id(a) == grid[a] - 1)

        @pl.when(first)
        def _():
            ride.start(x_in, x_out, sems)

        body(*ins, *outs, *scr)

        @pl.when(last)
        def _():
            ride.finish(x_in, x_out, sems)

    res = pl.pallas_call(
        wrapped, name=name, grid=grid, in_specs=list(in_specs) + [_ANY] * n_x,
        out_specs=list(out_specs) + [_ANY] * n_x, out_shape=list(out_shape) + list(ride.out_shape),
        scratch_shapes=list(scratch_shapes) + list(ride.sems),
        compiler_params=_cp(("arbitrary",) * len(grid)))(*args, *ride.arrays)
    return res[:n_out], list(res[n_out:])


def _pair_sum(g, ra, core, out_dtype, name):
    _, r, C = g.shape
    tr = _tile(r, 576)

    def body(c_ref, g_ref, ra_ref, p_ref):
        p_ref[...] = (g_ref[...] + ra_ref[...]).astype(p_ref.dtype)

    return pl.pallas_call(
        body,
        name=name,
        grid_spec=pltpu.PrefetchScalarGridSpec(
            num_scalar_prefetch=1,
            grid=(4, r // tr),
            in_specs=[pl.BlockSpec((1, tr, C), lambda j, t, c_ref: (2 * j + c_ref[0], t, 0)),
                      pl.BlockSpec((1, tr, C), lambda j, t, c_ref: (j, t, 0))],
            out_specs=pl.BlockSpec((1, tr, C), lambda j, t, c_ref: (j, t, 0)),
        ),
        out_shape=jax.ShapeDtypeStruct((4, r, C), out_dtype),
        compiler_params=_cp(("parallel", "parallel")),
    )(core, g, ra)


def _chip_sum(p, rb, chip, name):
    _, r, C = p.shape
    tr = _tile(r, 576)

    def body(c_ref, p_ref, rb_ref, o_ref):
        acc = p_ref[0].astype(F32) + rb_ref[0].astype(F32)
        acc = acc + rb_ref[1].astype(F32)
        o_ref[...] = acc + rb_ref[2].astype(F32)

    return pl.pallas_call(
        body,
        name=name,
        grid_spec=pltpu.PrefetchScalarGridSpec(
            num_scalar_prefetch=1,
            grid=(r // tr,),
            in_specs=[pl.BlockSpec((1, tr, C), lambda t, c_ref: (c_ref[0], t, 0)),
                      pl.BlockSpec((3, tr, C), lambda t, c_ref: (0, t, 0))],
            out_specs=pl.BlockSpec((tr, C), lambda t, c_ref: (t, 0)),
        ),
        out_shape=jax.ShapeDtypeStruct((r, C), F32),
        compiler_params=_cp(("parallel",)),
    )(chip, p, rb)


def _adamw(g, w, m, v, name):
    R, C = g.shape
    tr = _tile(R, 256)
    c1 = 1.0 - ADAM_B1 ** ADAM_STEP
    c2 = 1.0 - ADAM_B2 ** ADAM_STEP

    def body(g_ref, w_ref, m_ref, v_ref, d_ref, nm_ref, nv_ref):
        gv = g_ref[...]
        nm = ADAM_B1 * m_ref[...] + (1.0 - ADAM_B1) * gv
        nv = ADAM_B2 * v_ref[...] + (1.0 - ADAM_B2) * (gv * gv)
        nm_ref[...] = nm
        nv_ref[...] = nv
        d_ref[...] = -ADAM_LR * ((nm / c1) / (jnp.sqrt(nv / c2) + ADAM_EPS) + ADAM_WD * w_ref[...])

    blk = pl.BlockSpec((tr, C), lambda i: (i, 0))
    return pl.pallas_call(
        body, name=name, grid=(R // tr,), in_specs=[blk] * 4, out_specs=[blk] * 3,
        out_shape=[jax.ShapeDtypeStruct((R, C), F32)] * 3, compiler_params=_cp(("parallel",)),
    )(g, w, m, v)


_SMALL = ("norm_w", "q_norm_w", "k_norm_w", "sinks", "a_re", "a_im", "log_step", "b_re", "b_im", "c_re", "c_im",
          "d_skip", "b_glu", "attn_out_norm_w", "ssm_out_norm_w")
_WEIGHTS = ("norm_w", "w_in", "q_norm_w", "k_norm_w", "sinks", "a_re", "a_im", "log_step", "b_re", "b_im", "c_re",
            "c_im", "d_skip", "w_glu", "b_glu", "attn_out_norm_w", "ssm_out_norm_w", "w_out")
_SMALL_2D = dict(norm_w=(1, 2048), q_norm_w=(1, 64), k_norm_w=(1, 64), sinks=(1, 16), a_re=(64, 64), a_im=(64, 64),
                 log_step=(1, 64), b_re=(4096, 16), b_im=(4096, 16), c_re=(1024, 64), c_im=(1024, 64),
                 d_skip=(1, 1024), b_glu=(1, 1024), attn_out_norm_w=(1, 1024), ssm_out_norm_w=(1, 1024))


def _slab_rows(n):
    return -(-n // 1024) * 8


_PACK_ROWS = 2304


_LOSS_ROW = 2192


def _pack(d, loss):
    parts = []
    for n in _SMALL:
        flat = d[n].reshape(-1).astype(F32)
        rows = _slab_rows(flat.shape[0])
        parts.append(jnp.pad(flat, (0, rows * 128 - flat.shape[0])).reshape(rows, 128))
    assert sum(p.shape[0] for p in parts) == _LOSS_ROW
    parts.append(jnp.pad(loss.reshape(1, 1), ((0, _PACK_ROWS - _LOSS_ROW - 1), (0, 127))))
    return jnp.concatenate(parts, axis=0)


def _unpack(packed, like):
    out, off = {}, 0
    for n in _SMALL:
        size = math.prod(like[n].shape)
        rows = _slab_rows(size)
        out[n] = packed[off:off + rows].reshape(-1)[:size].reshape(like[n].shape)
        off += rows
    return out


def _adamw_small(g, w, m, v):
    c1 = 1.0 - ADAM_B1 ** ADAM_STEP
    c2 = 1.0 - ADAM_B2 ** ADAM_STEP
    k = len(_SMALL)

    def body(*refs):
        ins, outs = refs[:4 * k], refs[4 * k:]
        for j in range(k):
            gv, wv, mv, vv = (ins[q * k + j][...] for q in range(4))
            nm = ADAM_B1 * mv + (1.0 - ADAM_B1) * gv
            nv = ADAM_B2 * vv + (1.0 - ADAM_B2) * (gv * gv)
            outs[j][...] = -ADAM_LR * ((nm / c1) / (jnp.sqrt(nv / c2) + ADAM_EPS) + ADAM_WD * wv)
            outs[k + j][...] = nm
            outs[2 * k + j][...] = nv

    args = [d[n].reshape(_SMALL_2D[n]) for d in (g, w, m, v) for n in _SMALL]
    shapes = [jax.ShapeDtypeStruct(_SMALL_2D[n], F32) for _ in range(3) for n in _SMALL]
    outs = pl.pallas_call(body, name="adamw_small", out_shape=shapes, compiler_params=_cp())(*args)
    res = []
    for q in range(3):
        res.append({n: outs[q * k + j].reshape(w[n].shape) for j, n in enumerate(_SMALL)})
    return res


def kernel(x, positions, norm_w, w_in, q_norm_w, k_norm_w, sinks, a_re, a_im, log_step, b_re, b_im, c_re, c_im, d_skip, w_glu, b_glu, attn_out_norm_w, ssm_out_norm_w, w_out, loss_target, m_norm_w, m_w_in, m_q_norm_w, m_k_norm_w, m_sinks, m_a_re, m_a_im, m_log_step, m_b_re, m_b_im, m_c_re, m_c_im, m_d_skip, m_w_glu, m_b_glu, m_attn_out_norm_w, m_ssm_out_norm_w, m_w_out, v_norm_w, v_w_in, v_q_norm_w, v_k_norm_w, v_sinks, v_a_re, v_a_im, v_log_step, v_b_re, v_b_im, v_c_re, v_c_im, v_d_skip, v_w_glu, v_b_glu, v_attn_out_norm_w, v_ssm_out_norm_w, v_w_out):
    w = dict(norm_w=norm_w, w_in=w_in, q_norm_w=q_norm_w, k_norm_w=k_norm_w, sinks=sinks, a_re=a_re, a_im=a_im,
             log_step=log_step, b_re=b_re, b_im=b_im, c_re=c_re, c_im=c_im, d_skip=d_skip, w_glu=w_glu, b_glu=b_glu,
             attn_out_norm_w=attn_out_norm_w, ssm_out_norm_w=ssm_out_norm_w, w_out=w_out)
    m = dict(norm_w=m_norm_w, w_in=m_w_in, q_norm_w=m_q_norm_w, k_norm_w=m_k_norm_w, sinks=m_sinks, a_re=m_a_re,
             a_im=m_a_im, log_step=m_log_step, b_re=m_b_re, b_im=m_b_im, c_re=m_c_re, c_im=m_c_im, d_skip=m_d_skip,
             w_glu=m_w_glu, b_glu=m_b_glu, attn_out_norm_w=m_attn_out_norm_w, ssm_out_norm_w=m_ssm_out_norm_w,
             w_out=m_w_out)
    v = dict(norm_w=v_norm_w, w_in=v_w_in, q_norm_w=v_q_norm_w, k_norm_w=v_k_norm_w, sinks=v_sinks, a_re=v_a_re,
             a_im=v_a_im, log_step=v_log_step, b_re=v_b_re, b_im=v_b_im, c_re=v_c_re, c_im=v_c_im, d_skip=v_d_skip,
             w_glu=v_w_glu, b_glu=v_b_glu, attn_out_norm_w=v_attn_out_norm_w, ssm_out_norm_w=v_ssm_out_norm_w,
             w_out=v_w_out)
    core = lax.axis_index("c").astype(jnp.int32).reshape(1)
    chip = (2 * lax.axis_index("x") + lax.axis_index("y")).astype(jnp.int32).reshape(1)

    loss, grad_x, grads = _step(x[0], positions[0], loss_target[0], w, core, chip)
    delta, new_m, new_v = {}, {}, {}
    for n in ("w_in", "w_glu", "w_out"):
        delta[n], new_m[n], new_v[n] = _adamw(grads[n], w[n], m[n], v[n], f"adamw_{n}")
    d_s, m_s, v_s = _adamw_small(grads, w, m, v)
    delta.update(d_s)
    new_m.update(m_s)
    new_v.update(v_s)

    return (loss, grad_x[None], *[grads[n] for n in _WEIGHTS], *[delta[n] for n in _WEIGHTS],
            *[new_m[n] for n in _WEIGHTS], *[new_v[n] for n in _WEIGHTS])
```

```python
import functools
import math

import jax
import jax.numpy as jnp
from jax import lax
from jax.experimental import pallas as pl
from jax.experimental.pallas import tpu as pltpu

F32 = jnp.float32
BF16 = jnp.bfloat16

D_MODEL = 2048
ATTN_W = 1024
KV_W = 256
SSM_W = 1024
HEAD_DIM = 64
N_HEADS = 16
N_KV = 4
KV_REP = 4
IN_W = 4608
BLOCK = 128
ROPE_THETA = 10000.0
NORM_EPS = 1e-6
SSM_G = 64
SSM_P = 64
SSM_H = 16
CHUNK = 16
CW = CHUNK * SSM_H
N_DEV = 8

ADAM_LR = 0.001
ADAM_B1 = 0.9
ADAM_B2 = 0.999
ADAM_EPS = 1e-08
ADAM_WD = 0.01
ADAM_STEP = 10

VMEM_LIMIT = 56 * 1024 * 1024
MESH = pl.DeviceIdType.MESH


def _cp(sem=None):
    if sem is None:
        return pltpu.CompilerParams(vmem_limit_bytes=VMEM_LIMIT)
    return pltpu.CompilerParams(vmem_limit_bytes=VMEM_LIMIT, dimension_semantics=sem)


def _sigmoid(x):
    return 0.5 * jnp.tanh(0.5 * x) + 0.5


def _silu(x):
    return x * _sigmoid(x)


def _dsilu(x):
    s = _sigmoid(x)
    return s * (1.0 + x * (1.0 - s))


_GELU_C = math.sqrt(2.0 / math.pi)


def _gelu(y):
    t = jnp.tanh(_GELU_C * (y + 0.044715 * y * y * y))
    return 0.5 * y * (1.0 + t)


def _dgelu(y):
    t = jnp.tanh(_GELU_C * (y + 0.044715 * y * y * y))
    return 0.5 * (1.0 + t) + 0.5 * y * (1.0 - t * t) * _GELU_C * (1.0 + 3.0 * 0.044715 * y * y)


def _tile(n, want):
    if n <= want:
        return n
    for t in range(want - want % 16, 0, -16):
        if n % t == 0:
            return t
    raise ValueError((n, want))


def _mm(a, b, mode, out_dtype, name, tm=512, tn=1024, add=None, ride=None, panel=None):
    if mode == "nn":
        (M, K), (K2, N) = a.shape, b.shape
    elif mode == "nt":
        (M, K), (N, K2) = a.shape, b.shape
    else:
        (K, M), (K2, N) = a.shape, b.shape
    assert K == K2
    tm, tn = _tile(M, tm), _tile(N, tn)
    p0 = 0
    if panel is not None:
        assert mode != "nt" and add is None
        p0, N = panel, tn
    dn = {"nn": _NN, "nt": _NT, "tn": _TN}[mode]

    def body(a_ref, b_ref, *rest):
        o_ref = rest[-1]
        acc = lax.dot_general(a_ref[...].astype(BF16), b_ref[...].astype(BF16), dn, preferred_element_type=F32)
        if add is not None:
            acc = acc + rest[0][...]
        o_ref[...] = acc.astype(o_ref.dtype)

    a_spec = pl.BlockSpec((K, tm), lambda j, i: (0, i)) if mode == "tn" else pl.BlockSpec((tm, K), lambda j, i: (i, 0))
    b_spec = (pl.BlockSpec((tn, K), lambda j, i: (j, 0)) if mode == "nt"
              else pl.BlockSpec((K, tn), lambda j, i: (0, j + p0)))
    o_spec = pl.BlockSpec((tm, tn), lambda j, i: (i, j))
    extra = () if add is None else (add,)
    if ride is not None:
        (out,), landed = _call(body, name, (N // tn, M // tm), [a_spec, b_spec] + [o_spec] * len(extra), [o_spec],
                               [jax.ShapeDtypeStruct((M, N), out_dtype)], (a, b, *extra), ride=ride)
        return out, landed
    return pl.pallas_call(
        body,
        name=name,
        grid=(N // tn, M // tm),
        in_specs=[a_spec, b_spec] + [o_spec] * len(extra),
        out_specs=o_spec,
        out_shape=jax.ShapeDtypeStruct((M, N), out_dtype),
        compiler_params=_cp(("parallel", "parallel")),
    )(a, b, *extra)


_CHIP_ORDER = (0, 2, 1, 3)


def _rms_inproj_gather(x, norm_w, wt_shard, chip):
    L = x.shape[0]
    tm = _tile(L, 512)
    ni = L // tm
    r = IN_W // N_DEV
    tn = 2 * r

    def body(chip_ref, x_ref, nw_ref, shard, proj_ref, hn_ref, wt_hbm, hn_scr, w_scr, send_sems, recv_sems, loc_sems):
        jc, i = pl.program_id(0), pl.program_id(1)
        xx, yy, c = lax.axis_index("x"), lax.axis_index("y"), lax.axis_index("c")
        me, sibling = (xx, yy, c), (xx, yy, 1 - c)
        chips = [(1 - xx, yy), (xx, 1 - yy), (1 - xx, 1 - yy)]

        def slot(dev):
            return wt_hbm.at[4 * dev[0] + 2 * dev[1] + dev[2]]

        def copy(q, block, to, src=None):
            return pltpu.make_async_remote_copy(
                src_ref=slot(block) if src is None else src, dst_ref=slot(block),
                send_sem=send_sems.at[q], recv_sem=recv_sems.at[q], device_id=to, device_id_type=MESH)

        def rows_of(buf, core):
            return w_scr.at[buf, pl.ds(pl.multiple_of(core * r, 16), r)]

        mine = pltpu.make_async_copy(shard, slot(me), loc_sems.at[0])
        sends = [copy(0, me, sibling, src=shard)] + [copy(1 + j, me, (*ch, c), src=shard) for j, ch in enumerate(chips)]
        first = jnp.logical_and(jc == 0, i == 0)

        @pl.when(first)
        def _():
            mine.start()
            for cp in sends[:3]:
                cp.start()
            own = pltpu.make_async_copy(shard, rows_of(0, c), loc_sems.at[1])
            own.start()
            copy(0, sibling, me).wait_recv()
            sib = pltpu.make_async_copy(slot(sibling), rows_of(0, 1 - c), loc_sems.at[2])
            sib.start()
            own.wait()
            sib.wait()

        for j, ch in enumerate(chips):
            @pl.when(jnp.logical_and(jc == 1 + j, i == 0))
            def _(j=j, ch=ch):
                buf = (1 + j) % 2
                copy(1 + j, (*ch, c), me).wait_recv()
                copy(4 + j, (*ch, c), sibling).start()
                if j == 0:
                    sends[1].wait_send()
                    sends[2].wait_send()
                    sends[3].start()
                direct = pltpu.make_async_copy(slot((*ch, c)), rows_of(buf, c), loc_sems.at[1])
                direct.start()
                copy(4 + j, (*ch, 1 - c), me).wait_recv()
                passed = pltpu.make_async_copy(slot((*ch, 1 - c)), rows_of(buf, 1 - c), loc_sems.at[2])
                passed.start()
                direct.wait()
                passed.wait()

        rows = pl.ds(pl.multiple_of(i * tm, tm), tm)

        @pl.when(jc == 0)
        def _():
            xv = x_ref[...]
            rstd = lax.rsqrt(jnp.mean(xv * xv, axis=-1, keepdims=True) + NORM_EPS)
            hn = (xv * rstd * nw_ref[...]).astype(BF16)
            hn_scr[rows, :] = hn
            hn_ref[...] = hn

        for buf in range(2):
            @pl.when(jc % 2 == buf)
            def _(buf=buf):
                proj_ref[...] = lax.dot_general(hn_scr[rows, :], w_scr[buf], _NT, preferred_element_type=F32)

        @pl.when(jnp.logical_and(jc == 3, i == ni - 1))
        def _():
            sends[0].wait_send()
            sends[3].wait_send()
            for j, ch in enumerate(chips):
                copy(4 + j, (*ch, c), sibling).wait_send()
            mine.wait()

    def tile_of(jc, chip_ref):
        mask = jnp.where(jc == 1, _CHIP_ORDER[1], jnp.where(jc == 2, _CHIP_ORDER[2], jnp.where(jc == 3, _CHIP_ORDER[3], 0)))
        return jnp.bitwise_xor(chip_ref[0], mask)

    held = lambda jc, i: jnp.where(jc == 0, i, ni - 1)
    return pl.pallas_call(
        body,
        name="rms_inproj_gather",
        grid_spec=pltpu.PrefetchScalarGridSpec(
            num_scalar_prefetch=1,
            grid=(4, ni),
            in_specs=[pl.BlockSpec((tm, D_MODEL), lambda jc, i, ch: (held(jc, i), 0)),
                      pl.BlockSpec((1, D_MODEL), lambda jc, i, ch: (0, 0)), _ANY],
            out_specs=[pl.BlockSpec((tm, tn), lambda jc, i, ch: (i, tile_of(jc, ch))),
                       pl.BlockSpec((tm, D_MODEL), lambda jc, i, ch: (held(jc, i), 0)), _ANY],
            scratch_shapes=[pltpu.VMEM((L, D_MODEL), BF16), pltpu.VMEM((2, tn, D_MODEL), BF16),
                            pltpu.SemaphoreType.DMA((7,)), pltpu.SemaphoreType.DMA((7,)), pltpu.SemaphoreType.DMA((3,))],
        ),
        out_shape=[jax.ShapeDtypeStruct((L, IN_W), F32), jax.ShapeDtypeStruct((L, D_MODEL), BF16),
                   jax.ShapeDtypeStruct((N_DEV, r, D_MODEL), BF16)],
        compiler_params=_cp(("arbitrary", "arbitrary")),
    )(chip, x, norm_w.reshape(1, D_MODEL), wt_shard)


def _seg_sum(v):
    a = lax.broadcasted_iota(jnp.int32, (128, 128), 0) // HEAD_DIM
    b = lax.broadcasted_iota(jnp.int32, (128, 128), 1) // HEAD_DIM
    ones = jnp.where(a == b, 1.0, 0.0).astype(BF16)
    hi = v.astype(BF16)
    lo = (v - hi.astype(F32)).astype(BF16)
    return jnp.dot(hi, ones, preferred_element_type=F32) + jnp.dot(lo, ones, preferred_element_type=F32)


def _rot_half(t):
    lane = lax.broadcasted_iota(jnp.int32, t.shape, 1)
    return jnp.where(lane % HEAD_DIM < HEAD_DIM // 2, pltpu.roll(t, 128 - HEAD_DIM // 2, 1),
                     pltpu.roll(t, HEAD_DIM // 2, 1))


def _norm_rope(raw, w, cos, sin):
    r = lax.rsqrt(_seg_sum(raw * raw) * (1.0 / HEAD_DIM) + NORM_EPS)
    tn = raw * r * w
    return r, tn * cos + _rot_half(tn) * sin


def _norm_rope_bwd(d_rot, raw, w, cos, sin):
    r = lax.rsqrt(_seg_sum(raw * raw) * (1.0 / HEAD_DIM) + NORM_EPS)
    d_tn = d_rot * cos + _rot_half(d_rot * sin)
    xh = raw * r
    gw = d_tn * w
    d_raw = r * (gw - xh * (_seg_sum(gw * xh) * (1.0 / HEAD_DIM)))
    return d_raw, d_tn * xh


def _band_mask2(has_prev):
    qi = lax.broadcasted_iota(jnp.int32, (2 * BLOCK, 2 * BLOCK), 0) % BLOCK + BLOCK
    kj = lax.broadcasted_iota(jnp.int32, (2 * BLOCK, 2 * BLOCK), 1)
    rel = qi - kj
    return (rel >= 0) & (rel < BLOCK) & ((kj >= BLOCK) | has_prev)


def _half_tiles(pair):
    lo = lax.broadcasted_iota(jnp.int32, pair.shape, 1) < HEAD_DIM
    sw = pltpu.roll(pair, HEAD_DIM, 1)
    z = jnp.zeros_like(pair)
    return (jnp.where(lo, pair, z).astype(BF16), jnp.where(lo, z, sw).astype(BF16),
            jnp.where(lo, sw, z).astype(BF16), jnp.where(lo, z, pair).astype(BF16))


def _two_rows(top, bottom):
    row = lax.broadcasted_iota(jnp.int32, (2 * BLOCK, 1), 0)
    return jnp.where(row < BLOCK, top, bottom)


def _lane_col(mat, h):
    lane = lax.broadcasted_iota(jnp.int32, mat.shape, 1)
    return jnp.sum(jnp.where(lane == h, mat, 0.0), axis=1, keepdims=True)


_SCALE = 1.0 / math.sqrt(HEAD_DIM)
_NT = (((1,), (1,)), ((), ()))
_NN = (((1,), (0,)), ((), ()))
_TN = (((0,), (0,)), ((), ()))


def _qk_prep(proj, tab, qw, kw):
    L = proj.shape[0]
    tm = _tile(L, 512)

    def body(q_ref, k_ref, t_ref, qw_ref, kw_ref, qo_ref, ko_ref):
        cos, sin = t_ref[:, :128], t_ref[:, 128:]
        for c in range(ATTN_W // 128):
            _, qr = _norm_rope(q_ref[:, c * 128:(c + 1) * 128], qw_ref[...], cos, sin)
            qo_ref[:, c * 128:(c + 1) * 128] = (qr * _SCALE).astype(BF16)
        for c in range(KV_W // 128):
            _, kr = _norm_rope(k_ref[:, c * 128:(c + 1) * 128], kw_ref[...], cos, sin)
            ko_ref[:, c * 128:(c + 1) * 128] = kr.astype(BF16)

    row = pl.BlockSpec((1, 128), lambda i: (0, 0))
    return pl.pallas_call(
        body,
        name="qk_prep",
        grid=(L // tm,),
        in_specs=[pl.BlockSpec((tm, ATTN_W), lambda i: (i, 0)), pl.BlockSpec((tm, KV_W), lambda i: (i, 4)),
                  pl.BlockSpec((tm, 256), lambda i: (i, 0)), row, row],
        out_specs=[pl.BlockSpec((tm, ATTN_W), lambda i: (i, 0)), pl.BlockSpec((tm, KV_W), lambda i: (i, 0))],
        out_shape=[jax.ShapeDtypeStruct((L, ATTN_W), BF16), jax.ShapeDtypeStruct((L, KV_W), BF16)],
        compiler_params=_cp(("parallel",)),
    )(proj, proj, tab, jnp.tile(qw, 2).reshape(1, 128), jnp.tile(kw, 2).reshape(1, 128))


def _group_tiles(g, kt, vt):
    a, b = divmod(g, 2)
    return kt[a][2 * b], kt[a][2 * b + 1], vt[a][2 * b], vt[a][2 * b + 1]


def _attn_fwd(q, k, proj, sinks, ride):
    L = proj.shape[0]
    nb = L // BLOCK

    def body(q_ref, kc_ref, kp_ref, vc_ref, vp_ref, z0_ref, z1_ref, sink_ref, og_ref, o_ref, lse_ref):
        i = pl.program_id(0)
        mask = _band_mask2(i > 0)
        z = jnp.concatenate([z0_ref[...], z1_ref[...]], axis=1)
        lane = lax.broadcasted_iota(jnp.int32, (BLOCK, 128), 1)
        kt = [_half_tiles(jnp.concatenate([kp_ref[:, a * 128:(a + 1) * 128], kc_ref[:, a * 128:(a + 1) * 128]],
                                          axis=0).astype(F32)) for a in range(2)]
        vt = [_half_tiles(jnp.concatenate([vp_ref[:, a * 128:(a + 1) * 128], vc_ref[:, a * 128:(a + 1) * 128]],
                                          axis=0)) for a in range(2)]
        lse_mat = jnp.zeros((BLOCK, 128), F32)
        outs = []
        for g in range(N_KV):
            k_lo, k_hi, v_lo, v_hi = _group_tiles(g, kt, vt)
            q2 = jnp.concatenate([q_ref[:, 2 * g * 128:(2 * g + 1) * 128],
                                  q_ref[:, (2 * g + 1) * 128:(2 * g + 2) * 128]], axis=0)
            acc = jnp.zeros((2 * BLOCK, 128), F32)
            for half, (kh, vh) in enumerate(((k_lo, v_lo), (k_hi, v_hi))):
                h_top, h_bot = 4 * g + half, 4 * g + 2 + half
                s = jnp.where(mask, lax.dot_general(q2, kh, _NT, preferred_element_type=F32), -1e30)
                sink = _two_rows(sink_ref[h_top], sink_ref[h_bot])
                m = jnp.maximum(jnp.max(s, axis=-1, keepdims=True), sink)
                e = jnp.exp(s - m)
                den = jnp.sum(e, axis=-1, keepdims=True) + jnp.exp(sink - m)
                p = e * (1.0 / den)
                acc = acc + jnp.dot(p.astype(BF16), vh, preferred_element_type=F32)
                lse = m + jnp.log(den)
                lse_mat = jnp.where(lane == h_top, lse[:BLOCK], lse_mat)
                lse_mat = jnp.where(lane == h_bot, lse[BLOCK:], lse_mat)
            outs += [acc[:BLOCK], acc[BLOCK:]]
        o = jnp.concatenate(outs, axis=1)
        o_ref[...] = o
        og_ref[...] = o * _silu(z)
        lse_ref[...] = lse_mat

    prev = lambda i: jnp.maximum(i - 1, 0)
    return _call(
        body, "attn_fwd", (nb,),
        [pl.BlockSpec((BLOCK, ATTN_W), lambda i: (i, 0)),
         pl.BlockSpec((BLOCK, KV_W), lambda i: (i, 0)),
         pl.BlockSpec((BLOCK, KV_W), lambda i: (prev(i), 0)),
         pl.BlockSpec((BLOCK, KV_W), lambda i: (i, 5)),
         pl.BlockSpec((BLOCK, KV_W), lambda i: (prev(i), 5)),
         pl.BlockSpec((BLOCK, 512), lambda i: (i, 3)),
         pl.BlockSpec((BLOCK, 512), lambda i: (i, 4)),
         pl.BlockSpec(memory_space=pltpu.SMEM)],
        [pl.BlockSpec((BLOCK, ATTN_W), lambda i: (i, 0)),
         pl.BlockSpec((BLOCK, ATTN_W), lambda i: (i, 0)),
         pl.BlockSpec((BLOCK, 128), lambda i: (i, 0))],
        [jax.ShapeDtypeStruct((L, ATTN_W), F32), jax.ShapeDtypeStruct((L, ATTN_W), F32),
         jax.ShapeDtypeStruct((L, 128), F32)],
        (q, k, k, proj, proj, proj, proj, sinks), ride=ride)


def _attn_bwd(q, k, proj, sinks, d_o, o, lse, ride):
    L = proj.shape[0]
    nb = L // BLOCK

    def body(q_ref, kc_ref, kp_ref, vc_ref, vp_ref, do_ref, o_ref, lse_ref, sink_ref,
             dq_ref, dk_ref, dv_ref, gs_ref, ck_scr, cv_scr):
        i = pl.program_id(0)

        @pl.when(i == 0)
        def _():
            gs_ref[...] = jnp.zeros_like(gs_ref)
            ck_scr[...] = jnp.zeros_like(ck_scr)
            cv_scr[...] = jnp.zeros_like(cv_scr)

        @pl.when(i == nb)
        def _():
            dk_ref[...] = ck_scr[...]
            dv_ref[...] = cv_scr[...]

        @pl.when(i < nb)
        def _():
            mask = _band_mask2(i > 0)
            lane = lax.broadcasted_iota(jnp.int32, (1, 128), 1)
            lo = lax.broadcasted_iota(jnp.int32, (2 * BLOCK, 128), 1) < HEAD_DIM
            lse_c = lse_ref[...]
            kt = [_half_tiles(jnp.concatenate([kp_ref[:, a * 128:(a + 1) * 128], kc_ref[:, a * 128:(a + 1) * 128]],
                                              axis=0).astype(F32)) for a in range(2)]
            vt = [_half_tiles(jnp.concatenate([vp_ref[:, a * 128:(a + 1) * 128], vc_ref[:, a * 128:(a + 1) * 128]],
                                              axis=0)) for a in range(2)]
            gs = jnp.zeros((1, 128), F32)
            dq_parts = []
            dk_acc = [jnp.zeros((2 * BLOCK, 128), F32) for _ in range(2)]
            dv_acc = [jnp.zeros((2 * BLOCK, 128), F32) for _ in range(2)]
            for g in range(N_KV):
                a, b = divmod(g, 2)
                k_lo, k_hi, v_lo, v_hi = _group_tiles(g, kt, vt)
                t0, t1 = slice(2 * g * 128, (2 * g + 1) * 128), slice((2 * g + 1) * 128, (2 * g + 2) * 128)
                q2 = jnp.concatenate([q_ref[:, t0], q_ref[:, t1]], axis=0)
                do2 = jnp.concatenate([do_ref[:, t0], do_ref[:, t1]], axis=0)
                prod = do2 * jnp.concatenate([o_ref[:, t0], o_ref[:, t1]], axis=0)
                do2_b = do2.astype(BF16)
                dq2 = jnp.zeros((2 * BLOCK, 128), F32)
                dk_h, dv_h = [], []
                for half, (kh, vh) in enumerate(((k_lo, v_lo), (k_hi, v_hi))):
                    h_top, h_bot = 4 * g + half, 4 * g + 2 + half
                    lse = jnp.concatenate([_lane_col(lse_c, h_top), _lane_col(lse_c, h_bot)], axis=0)
                    sink = _two_rows(sink_ref[h_top], sink_ref[h_bot])
                    delta = jnp.sum(jnp.where(lo == (half == 0), prod, 0.0), axis=1, keepdims=True)
                    s = jnp.where(mask, lax.dot_general(q2, kh, _NT, preferred_element_type=F32), -1e30)
                    p = jnp.exp(s - lse)
                    dp = lax.dot_general(do2_b, vh, _NT, preferred_element_type=F32)
                    ds_b = (p * (dp - delta)).astype(BF16)
                    p_b = p.astype(BF16)
                    dq2 = dq2 + jnp.dot(ds_b, kh, preferred_element_type=F32)
                    dk_h.append(lax.dot_general(ds_b, q2, _TN, preferred_element_type=F32))
                    dv_h.append(lax.dot_general(p_b, do2_b, _TN, preferred_element_type=F32))
                    gsink = -jnp.exp(sink - lse) * delta
                    row = lax.broadcasted_iota(jnp.int32, (2 * BLOCK, 1), 0)
                    gs = gs + jnp.where(lane == h_top, jnp.sum(jnp.where(row < BLOCK, gsink, 0.0)), 0.0)
                    gs = gs + jnp.where(lane == h_bot, jnp.sum(jnp.where(row >= BLOCK, gsink, 0.0)), 0.0)
                dq_parts += [dq2[:BLOCK], dq2[BLOCK:]]
                for acc, parts in ((dk_acc, dk_h), (dv_acc, dv_h)):
                    t = jnp.where(lo, parts[0], parts[1])
                    t = t + pltpu.roll(t, HEAD_DIM, 1)
                    acc[a] = acc[a] + jnp.where(lo == (b == 0), t, 0.0)
            dq_ref[...] = jnp.concatenate(dq_parts, axis=1)
            dk_full = jnp.concatenate(dk_acc, axis=1)
            dv_full = jnp.concatenate(dv_acc, axis=1)
            dk_ref[...] = ck_scr[...] + dk_full[:BLOCK]
            dv_ref[...] = cv_scr[...] + dv_full[:BLOCK]
            ck_scr[...] = dk_full[BLOCK:]
            cv_scr[...] = dv_full[BLOCK:]
            gs_ref[...] += gs

    cur = lambda i: jnp.minimum(i, nb - 1)
    prev = lambda i: jnp.maximum(jnp.minimum(i, nb - 1) - 1, 0)
    done = lambda i: jnp.maximum(i - 1, 0)
    bs = pl.BlockSpec
    return _call(
        body, "attn_bwd", (nb + 1,),
        [bs((BLOCK, ATTN_W), lambda i: (cur(i), 0)),
         bs((BLOCK, KV_W), lambda i: (cur(i), 0)), bs((BLOCK, KV_W), lambda i: (prev(i), 0)),
         bs((BLOCK, KV_W), lambda i: (cur(i), 5)), bs((BLOCK, KV_W), lambda i: (prev(i), 5)),
         bs((BLOCK, ATTN_W), lambda i: (cur(i), 0)), bs((BLOCK, ATTN_W), lambda i: (cur(i), 0)),
         bs((BLOCK, 128), lambda i: (cur(i), 0)), bs(memory_space=pltpu.SMEM)],
        [bs((BLOCK, ATTN_W), lambda i: (cur(i), 0)),
         bs((BLOCK, KV_W), lambda i: (done(i), 0)), bs((BLOCK, KV_W), lambda i: (done(i), 0)),
         bs((1, 128), lambda i: (0, 0))],
        [jax.ShapeDtypeStruct((L, ATTN_W), F32), jax.ShapeDtypeStruct((L, KV_W), F32),
         jax.ShapeDtypeStruct((L, KV_W), F32), jax.ShapeDtypeStruct((1, 128), F32)],
        (q, k, k, proj, proj, d_o, o, lse, sinks),
        [pltpu.VMEM((BLOCK, KV_W), F32), pltpu.VMEM((BLOCK, KV_W), F32)], ride)


def _qk_prep_bwd(proj, tab, qw, kw, d_q, d_k, d_v, d_za, d_u, d_zs):
    L = proj.shape[0]
    tm = _tile(L, 512)
    z0 = ATTN_W + 2 * KV_W

    def body(q_ref, k_ref, t_ref, qw_ref, kw_ref, dq_ref, dk_ref, dv_ref, dza_ref, du_ref, dzs_ref,
             out_ref, gq_ref, gk_ref):
        i = pl.program_id(0)

        @pl.when(i == 0)
        def _():
            gq_ref[...] = jnp.zeros_like(gq_ref)
            gk_ref[...] = jnp.zeros_like(gk_ref)

        cos, sin = t_ref[:, :128], t_ref[:, 128:]
        gq = jnp.zeros((1, 128), F32)
        gk = jnp.zeros((1, 128), F32)
        for c in range(ATTN_W // 128):
            cs = slice(c * 128, (c + 1) * 128)
            d_raw, gw = _norm_rope_bwd(dq_ref[:, cs] * _SCALE, q_ref[:, cs], qw_ref[...], cos, sin)
            out_ref[:, cs] = d_raw.astype(BF16)
            gq = gq + jnp.sum(gw, axis=0, keepdims=True)
        for c in range(KV_W // 128):
            cs = slice(c * 128, (c + 1) * 128)
            d_raw, gw = _norm_rope_bwd(dk_ref[:, cs], k_ref[:, cs], kw_ref[...], cos, sin)
            out_ref[:, ATTN_W + c * 128:ATTN_W + (c + 1) * 128] = d_raw.astype(BF16)
            gk = gk + jnp.sum(gw, axis=0, keepdims=True)
        out_ref[:, ATTN_W + KV_W:z0] = dv_ref[...].astype(BF16)
        out_ref[:, z0:z0 + ATTN_W] = dza_ref[...]
        out_ref[:, z0 + ATTN_W:z0 + ATTN_W + SSM_W] = du_ref[...].astype(BF16)
        out_ref[:, z0 + ATTN_W + SSM_W:] = dzs_ref[...]
        gq_ref[...] += gq
        gk_ref[...] += gk

    row = pl.BlockSpec((1, 128), lambda i: (0, 0))
    blk = lambda w, c: pl.BlockSpec((tm, w), lambda i: (i, c))
    return pl.pallas_call(
        body,
        name="qk_prep_bwd",
        grid=(L // tm,),
        in_specs=[blk(ATTN_W, 0), blk(KV_W, 4), blk(256, 0), row, row, blk(ATTN_W, 0), blk(KV_W, 0), blk(KV_W, 0),
                  blk(ATTN_W, 0), blk(SSM_W, 0), blk(SSM_W, 0)],
        out_specs=[blk(IN_W, 0), row, row],
        out_shape=[jax.ShapeDtypeStruct((L, IN_W), BF16), jax.ShapeDtypeStruct((1, 128), F32),
                   jax.ShapeDtypeStruct((1, 128), F32)],
        compiler_params=_cp(("arbitrary",)),
    )(proj, proj, tab, jnp.tile(qw, 2).reshape(1, 128), jnp.tile(kw, 2).reshape(1, 128), d_q, d_k, d_v,
      d_za, d_u, d_zs)


def _cmul(a, b):
    return a[0] * b[0] - a[1] * b[1], a[0] * b[1] + a[1] * b[0]


def _cmul_conj(a, b):
    return a[0] * b[0] + a[1] * b[1], a[1] * b[0] - a[0] * b[1]


def _cadd(a, b):
    return a[0] + b[0], a[1] + b[1]


def _dot3(a, b, dn):
    ah, bh = a.astype(BF16), b.astype(BF16)
    al, bl = (a - ah.astype(F32)).astype(BF16), (b - bh.astype(F32)).astype(BF16)
    d = lambda u, v: lax.dot_general(u, v, dn, preferred_element_type=F32)
    return d(ah, bh) + d(ah, bl) + d(al, bh)


def _s5_discretise(a_re, a_im, ls, cosx, sinx, bt):
    delta = jnp.exp(ls)
    er = jnp.exp(a_re * delta)
    lb = (er * cosx, er * sinx)
    den = a_re * a_re + a_im * a_im
    coef = _cmul_conj((lb[0] - 1.0, lb[1]), (a_re, a_im))
    coef = (coef[0] / den, coef[1] / den)
    return delta, lb, coef, den, _cmul(coef, bt)


def _powers(lb):
    pw = [(jnp.ones_like(lb[0]), jnp.zeros_like(lb[0]))]
    for _ in range(CHUNK):
        pw.append(_cmul(pw[-1], lb))
    return pw


def _block_rows(a, pw, idx):
    blocks = [_cmul(a, pw[i]) for i in idx]
    return (jnp.concatenate([b[0] for b in blocks], axis=-2), jnp.concatenate([b[1] for b in blocks], axis=-2))


def _block_rows_bwd(g, a, pw, idx, g_pw):
    g_a = (jnp.zeros_like(a[0]), jnp.zeros_like(a[0]))
    for j, i in enumerate(idx):
        gj = (g[0][..., j * SSM_H:(j + 1) * SSM_H, :], g[1][..., j * SSM_H:(j + 1) * SSM_H, :])
        g_a = _cadd(g_a, _cmul_conj(gj, pw[i]))
        gp = _cmul_conj(gj, a)
        g_pw[i] = _cadd(g_pw[i], (jnp.sum(gp[0], axis=-2, keepdims=True), jnp.sum(gp[1], axis=-2, keepdims=True)))
    return g_a


_IDX_S = [CHUNK - 1 - s for s in range(CHUNK)]
_IDX_O = [t + 1 for t in range(CHUNK)]
_IDX_K = list(range(CHUNK))
_PREP_IN = 9


def _prep_args(p):
    row = lambda t: t.reshape(SSM_G, 1, SSM_P)
    xi = p["a_im"] * jnp.exp(p["log_step"])[:, None]
    return (row(p["a_re"]), row(p["a_im"]), row(jnp.broadcast_to(p["log_step"][:, None], (SSM_G, SSM_P))),
            row(jnp.cos(xi)), row(jnp.sin(xi)), p["b_re"].transpose(0, 2, 1), p["b_im"].transpose(0, 2, 1),
            p["c_re"], p["c_im"])


PREP_GROUPS = 8


def _prep_specs():
    r1 = pl.BlockSpec((PREP_GROUPS, 1, SSM_P), lambda g: (g, 0, 0))
    r16 = pl.BlockSpec((PREP_GROUPS, SSM_H, SSM_P), lambda g: (g, 0, 0))
    return [r1] * 5 + [r16] * 4, r1, r16


def _ssm_prep(p):
    def one_group(q, are, aim, ls, cosx, sinx, btr, bti, cre, cim, mt_ref, s_ref, o_ref, a_ref):
        _, lb, _, _, bb = _s5_discretise(are[q], aim[q], ls[q], cosx[q], sinx[q], (btr[q], bti[q]))
        pw = _powers(lb)
        c = (cre[q], cim[q])
        sc = _block_rows(bb, pw, _IDX_S)
        ot = _block_rows(c, pw, _IDX_O)
        ok = _block_rows(c, pw, _IDX_K)
        s_ref[q] = jnp.concatenate([sc[0], sc[1]], axis=1).astype(BF16)
        o_ref[q] = jnp.concatenate([ot[0], -ot[1]], axis=1).astype(BF16)
        a_ref[q] = jnp.concatenate([pw[CHUNK][0], pw[CHUNK][1]], axis=1)
        kt = _dot3(jnp.concatenate([bb[0], -bb[1]], axis=1), jnp.concatenate([ok[0], ok[1]], axis=1), _NT)
        lane = lax.broadcasted_iota(jnp.int32, kt.shape, 1)
        for s in range(CHUNK):
            blk = kt if s == 0 else jnp.where(lane >= SSM_H * s, pltpu.roll(kt, SSM_H * s, 1), 0.0)
            mt_ref[q, s * SSM_H:(s + 1) * SSM_H, :] = blk.astype(BF16)

    def body(*refs):
        for q in range(PREP_GROUPS):
            one_group(q, *refs)

    in_specs, r1, _ = _prep_specs()
    g3 = lambda r, c: pl.BlockSpec((PREP_GROUPS, r, c), lambda g: (g, 0, 0))
    return pl.pallas_call(
        body,
        name="ssm_prep",
        grid=(SSM_G // PREP_GROUPS,),
        in_specs=in_specs,
        out_specs=[g3(CW, CW), g3(CW, 2 * SSM_P), g3(CW, 2 * SSM_P), g3(1, 2 * SSM_P)],
        out_shape=[jax.ShapeDtypeStruct((SSM_G, CW, CW), BF16), jax.ShapeDtypeStruct((SSM_G, CW, 2 * SSM_P), BF16),
                   jax.ShapeDtypeStruct((SSM_G, CW, 2 * SSM_P), BF16),
                   jax.ShapeDtypeStruct((SSM_G, 1, 2 * SSM_P), F32)],
        compiler_params=_cp(("parallel",)),
    )(*_prep_args(p))


def _ssm_prep_bwd(p, g_mt, g_scat, g_ocat, g_a16, ride):
    def body(are, aim, ls, cosx, sinx, btr, bti, cre, cim, gmt_ref, gs_ref, go_ref, ga_ref,
             g_are, g_aim, g_ls, g_btr, g_bti, g_cre, g_cim, ga1_scr, gb1_scr):
        lam = (are[...], aim[...])
        bt = (btr[...], bti[...])
        delta, lb, coef, den, bb = _s5_discretise(lam[0], lam[1], ls[...], cosx[...], sinx[...], bt)
        pw = _powers(lb)
        c = (cre[...], cim[...])
        ok = _block_rows(c, pw, _IDX_K)
        g_pw = [(jnp.zeros_like(lb[0]), jnp.zeros_like(lb[0])) for _ in range(CHUNK + 1)]
        lane = lax.broadcasted_iota(jnp.int32, (SSM_H, CW), 1)
        for q in range(PREP_GROUPS):
            g_kt = gmt_ref[q, :SSM_H, :]
            for s in range(1, CHUNK):
                blk = gmt_ref[q, s * SSM_H:(s + 1) * SSM_H, :]
                g_kt = g_kt + jnp.where(lane < CW - SSM_H * s, pltpu.roll(blk, CW - SSM_H * s, 1), 0.0)
            a1 = jnp.concatenate([bb[0][q], -bb[1][q]], axis=1)
            b1 = jnp.concatenate([ok[0][q], ok[1][q]], axis=1)
            ga1_scr[q] = _dot3(g_kt, b1, _NN)
            gb1_scr[q] = _dot3(g_kt, a1, _TN)
        g_a1, g_b1 = ga1_scr[...], gb1_scr[...]
        g_bb = (g_a1[..., :SSM_P], -g_a1[..., SSM_P:])
        g_c = _block_rows_bwd((g_b1[..., :SSM_P], g_b1[..., SSM_P:]), c, pw, _IDX_K, g_pw)
        gs = gs_ref[...]
        g_bb = _cadd(g_bb, _block_rows_bwd((gs[..., :SSM_P], gs[..., SSM_P:]), bb, pw, _IDX_S, g_pw))
        go = go_ref[...]
        g_c = _cadd(g_c, _block_rows_bwd((go[..., :SSM_P], -go[..., SSM_P:]), c, pw, _IDX_O, g_pw))
        ga = ga_ref[...]
        g_pw[CHUNK] = _cadd(g_pw[CHUNK], (ga[..., :SSM_P], ga[..., SSM_P:]))
        g_lb = (jnp.zeros_like(lb[0]), jnp.zeros_like(lb[0]))
        for l in range(CHUNK - 1, -1, -1):
            g_lb = _cadd(g_lb, _cmul_conj(g_pw[l + 1], pw[l]))
            g_pw[l] = _cadd(g_pw[l], _cmul_conj(g_pw[l + 1], lb))
        g_bt = _cmul_conj(g_bb, coef)
        gc = _cmul_conj(g_bb, bt)
        g_coef = (jnp.sum(gc[0], axis=-2, keepdims=True), jnp.sum(gc[1], axis=-2, keepdims=True))
        lam_den = (lam[0] / den, lam[1] / den)
        g_lb = _cadd(g_lb, _cmul(g_coef, lam_den))
        t = _cmul(_cmul_conj(g_coef, coef), lam_den)
        g_x = _cmul_conj(g_lb, lb)
        g_are[...] = g_x[0] * delta - t[0]
        g_aim[...] = g_x[1] * delta - t[1]
        g_ls[...] = (g_x[0] * lam[0] + g_x[1] * lam[1]) * delta
        g_btr[...] = g_bt[0]
        g_bti[...] = g_bt[1]
        g_cre[...] = g_c[0]
        g_cim[...] = g_c[1]

    in_specs, r1, r16 = _prep_specs()
    g3 = lambda r, c: pl.BlockSpec((PREP_GROUPS, r, c), lambda g: (g, 0, 0))
    rows = jax.ShapeDtypeStruct((SSM_G, 1, SSM_P), F32)
    mats = jax.ShapeDtypeStruct((SSM_G, SSM_H, SSM_P), F32)
    (g_are, g_aim, g_ls, g_btr, g_bti, g_cre, g_cim), landed = _call(
        body, "ssm_prep_bwd", (SSM_G // PREP_GROUPS,),
        in_specs + [g3(CW, CW), g3(CW, 2 * SSM_P), g3(CW, 2 * SSM_P), g3(1, 2 * SSM_P)],
        [r1] * 3 + [r16] * 4, [rows] * 3 + [mats] * 4, (*_prep_args(p), g_mt, g_scat, g_ocat, g_a16),
        [pltpu.VMEM((PREP_GROUPS, SSM_H, 2 * SSM_P), F32), pltpu.VMEM((PREP_GROUPS, CW, 2 * SSM_P), F32)], ride)
    grads = dict(a_re=g_are.reshape(SSM_G, SSM_P), a_im=g_aim.reshape(SSM_G, SSM_P),
                 log_step=jnp.sum(g_ls.reshape(SSM_G, SSM_P), axis=1),
                 b_re=g_btr.transpose(0, 2, 1), b_im=g_bti.transpose(0, 2, 1), c_re=g_cre, c_im=g_cim)
    return grads, landed


def _cmul_const(xv, ar, ai):
    return xv * ar + pltpu.roll(xv, SSM_P, 1) * ai


def _chunk_scan(inc, a_row, reverse):
    n = inc.shape[0]
    lane = lax.broadcasted_iota(jnp.int32, (1, 2 * SSM_P), 1)
    row = lax.broadcasted_iota(jnp.int32, inc.shape, 0)
    sign = jnp.where(lane < SSM_P, -1.0, 1.0)
    ar = jnp.where(lane < SSM_P, a_row, pltpu.roll(a_row, SSM_P, 1))
    ai = jnp.where(lane < SSM_P, pltpu.roll(a_row, SSM_P, 1), a_row)
    if reverse:
        ai = -ai
    xv = inc
    s = 1
    while s < n:
        if reverse:
            sh = jnp.where(row < n - s, pltpu.roll(xv, n - s, 0), 0.0)
        else:
            sh = jnp.where(row >= s, pltpu.roll(xv, s, 0), 0.0)
        xv = xv + _cmul_const(sh, ar, ai * sign)
        ar, ai = ar * ar - ai * ai, 2.0 * ar * ai
        s *= 2
    return xv


def _shift_rows(xv, reverse):
    n = xv.shape[0]
    row = lax.broadcasted_iota(jnp.int32, xv.shape, 0)
    if reverse:
        return jnp.where(row < n - 1, pltpu.roll(xv, n - 1, 0), 0.0)
    return jnp.where(row >= 1, pltpu.roll(xv, 1, 0), 0.0)


GB = 128 // SSM_H
U_COL0 = (ATTN_W + 2 * KV_W + ATTN_W) // 128


HALF = CHUNK // 2


def _chunk_perm():
    r = jnp.arange(HALF * 128)
    t, g8, h = r // 128, (r % 128) // SSM_H, r % SSM_H
    return ((g8 * 128 + t * SSM_H + h)[:, None] == jnp.arange(GB * 128)[None, :]).astype(BF16)


def _load_perm(p_hbm, p_scr, sem):
    @pl.when(pl.program_id(0) == 0)
    def _():
        cp = pltpu.make_async_copy(p_hbm, p_scr, sem)
        cp.start()
        cp.wait()


def _rows_to_chunks(pieces, perm):
    halves = [jnp.dot(jnp.concatenate(pieces[k * HALF:(k + 1) * HALF], axis=1).astype(BF16), perm,
                      preferred_element_type=F32).astype(BF16) for k in range(2)]
    return [jnp.concatenate([hv[:, g * 128:(g + 1) * 128] for hv in halves], axis=1) for g in range(GB)]


def _chunks_to_rows(groups, perm, two_pass):
    pieces = []
    for k in range(2):
        v = jnp.concatenate([gv[:, k * 128:(k + 1) * 128] for gv in groups], axis=1)
        hi = v.astype(BF16)
        out = lax.dot_general(hi, perm, _NT, preferred_element_type=F32)
        if two_pass:
            lo = (v - hi.astype(F32)).astype(BF16)
            out = out + lax.dot_general(lo, perm, _NT, preferred_element_type=F32)
        pieces += [out[:, t * 128:(t + 1) * 128] for t in range(HALF)]
    return pieces


def _ssm_fwd(proj, perm, mt, scat, ocat, a16, d_skip, ride):
    L = proj.shape[0]
    nc = L // CHUNK

    def body(u_ref, p_hbm, mt_ref, s_ref, o_ref, a_ref, d_ref, y_ref, yg_ref, h_ref, p_scr, sem):
        _load_perm(p_hbm, p_scr, sem)
        perm = p_scr[...]
        rows = [pl.ds(t, nc, stride=CHUNK) for t in range(CHUNK)]
        ua = _rows_to_chunks([u_ref[r, :] for r in rows], perm)
        ys = []
        for g in range(GB):
            uv = ua[g]
            inc = jnp.dot(uv, s_ref[g], preferred_element_type=F32)
            hx = _shift_rows(_chunk_scan(inc, a_ref[g], False), False)
            h_ref[g] = hx
            ys.append(jnp.dot(uv, mt_ref[g], preferred_element_type=F32)
                      + lax.dot_general(hx.astype(BF16), o_ref[g], _NT, preferred_element_type=F32))
        yp = _chunks_to_rows(ys, perm, True)
        for t, r in enumerate(rows):
            y = yp[t] + d_ref[...] * u_ref[r, :]
            y_ref[r, :] = y
            yg_ref[r, :] = _gelu(y)

    g3 = lambda r, c: pl.BlockSpec((GB, r, c), lambda g: (g, 0, 0))
    col = pl.BlockSpec((L, 128), lambda g: (0, g))
    return _call(
        body, "ssm_fwd", (SSM_G // GB,),
        [pl.BlockSpec((L, 128), lambda g: (0, U_COL0 + g)), _ANY,
         g3(CW, CW), g3(CW, 2 * SSM_P), g3(CW, 2 * SSM_P), g3(1, 2 * SSM_P),
         pl.BlockSpec((1, 128), lambda g: (0, g))],
        [col, col, g3(nc, 2 * SSM_P)],
        [jax.ShapeDtypeStruct((L, SSM_W), F32), jax.ShapeDtypeStruct((L, SSM_W), F32),
         jax.ShapeDtypeStruct((SSM_G, nc, 2 * SSM_P), F32)],
        (proj, perm, mt, scat, ocat, a16, d_skip.reshape(1, SSM_W)),
        [pltpu.VMEM((HALF * 128, GB * 128), BF16), pltpu.SemaphoreType.DMA], ride)


def _ssm_bwd(d_yg, y, proj, hx, perm, mt, scat, ocat, a16, d_skip, ride):
    L = proj.shape[0]
    nc = L // CHUNK

    def body(dg_ref, y_ref, u_ref, h_ref, p_hbm, mt_ref, s_ref, o_ref, a_ref, d_ref,
             du_ref, gmt_ref, gs_ref, go_ref, ga_ref, gd_ref, p_scr, sem):
        _load_perm(p_hbm, p_scr, sem)
        perm = p_scr[...]
        rows = [pl.ds(t, nc, stride=CHUNK) for t in range(CHUNK)]
        us = [u_ref[r, :] for r in rows]
        dys = [dg_ref[r, :] * _dgelu(y_ref[r, :]) for r in rows]
        gd = jnp.zeros((1, 128), F32)
        for uv, dy in zip(us, dys):
            gd = gd + jnp.sum(dy * uv, axis=0, keepdims=True)
        gd_ref[...] = gd
        ua = _rows_to_chunks(us, perm)
        dya = _rows_to_chunks(dys, perm)
        lane = lax.broadcasted_iota(jnp.int32, (1, 2 * SSM_P), 1)
        dus = []
        for g in range(GB):
            uv, dy, hx_v = ua[g], dya[g], h_ref[g]
            dh = jnp.dot(dy, o_ref[g], preferred_element_type=F32)
            dinc = _shift_rows(_chunk_scan(dh, a_ref[g], True), True)
            dinc_b = dinc.astype(BF16)
            dus.append(lax.dot_general(dy, mt_ref[g], _NT, preferred_element_type=F32)
                       + lax.dot_general(dinc_b, s_ref[g], _NT, preferred_element_type=F32))
            gmt_ref[g] = lax.dot_general(uv, dy, _TN, preferred_element_type=F32)
            gs_ref[g] = lax.dot_general(uv, dinc_b, _TN, preferred_element_type=F32)
            go_ref[g] = lax.dot_general(dy, hx_v.astype(BF16), _TN, preferred_element_type=F32)
            p1 = dinc * hx_v
            p2 = pltpu.roll(dinc, SSM_P, 1) * hx_v
            t1 = jnp.sum(p1 + pltpu.roll(p1, SSM_P, 1), axis=0, keepdims=True)
            t2 = jnp.sum(p2 - pltpu.roll(p2, SSM_P, 1), axis=0, keepdims=True)
            ga_ref[g] = jnp.where(lane < SSM_P, t1, pltpu.roll(t2, SSM_P, 1))
        dup = _chunks_to_rows(dus, perm, False)
        for t, r in enumerate(rows):
            du_ref[r, :] = dup[t] + d_ref[...] * dys[t]

    g3 = lambda r, c: pl.BlockSpec((GB, r, c), lambda g: (g, 0, 0))
    col = pl.BlockSpec((L, 128), lambda g: (0, g))
    row = pl.BlockSpec((1, 128), lambda g: (0, g))
    return _call(
        body, "ssm_bwd", (SSM_G // GB,),
        [col, col, pl.BlockSpec((L, 128), lambda g: (0, U_COL0 + g)), g3(nc, 2 * SSM_P), _ANY,
         g3(CW, CW), g3(CW, 2 * SSM_P), g3(CW, 2 * SSM_P), g3(1, 2 * SSM_P), row],
        [col, g3(CW, CW), g3(CW, 2 * SSM_P), g3(CW, 2 * SSM_P), g3(1, 2 * SSM_P), row],
        [jax.ShapeDtypeStruct((L, SSM_W), F32), jax.ShapeDtypeStruct((SSM_G, CW, CW), F32),
         jax.ShapeDtypeStruct((SSM_G, CW, 2 * SSM_P), F32), jax.ShapeDtypeStruct((SSM_G, CW, 2 * SSM_P), F32),
         jax.ShapeDtypeStruct((SSM_G, 1, 2 * SSM_P), F32), jax.ShapeDtypeStruct((1, SSM_W), F32)],
        (d_yg, y, proj, hx, perm, mt, scat, ocat, a16, d_skip.reshape(1, SSM_W)),
        [pltpu.VMEM((HALF * 128, GB * 128), BF16), pltpu.SemaphoreType.DMA], ride)


def _merge(og, yg, gpre, proj, b_glu, wa, ws):
    L = og.shape[0]
    tm = _tile(L, 256)

    def body(og_ref, yg_ref, gp_ref, z0_ref, z1_ref, b_ref, wa_ref, ws_ref, m_ref):
        zs = jnp.concatenate([z0_ref[...], z1_ref[...]], axis=1)
        os_ = yg_ref[...] * _sigmoid(gp_ref[...] + b_ref[...]) * _silu(zs)
        ogv = og_ref[...]
        ra = lax.rsqrt(jnp.mean(ogv * ogv, axis=-1, keepdims=True) + NORM_EPS)
        rs = lax.rsqrt(jnp.mean(os_ * os_, axis=-1, keepdims=True) + NORM_EPS)
        m_ref[:, :ATTN_W] = (ogv * ra * wa_ref[...]).astype(BF16)
        m_ref[:, ATTN_W:] = (os_ * rs * ws_ref[...]).astype(BF16)

    row = lambda w: pl.BlockSpec((1, w), lambda i: (0, 0))
    return pl.pallas_call(
        body,
        name="merge",
        grid=(L // tm,),
        in_specs=[pl.BlockSpec((tm, ATTN_W), lambda i: (i, 0)), pl.BlockSpec((tm, SSM_W), lambda i: (i, 0)),
                  pl.BlockSpec((tm, SSM_W), lambda i: (i, 0)),
                  pl.BlockSpec((tm, 512), lambda i: (i, 7)), pl.BlockSpec((tm, 512), lambda i: (i, 8)),
                  row(SSM_W), row(ATTN_W), row(SSM_W)],
        out_specs=pl.BlockSpec((tm, D_MODEL), lambda i: (i, 0)),
        out_shape=jax.ShapeDtypeStruct((L, D_MODEL), BF16),
        compiler_params=_cp(("parallel",)),
    )(og, yg, gpre, proj, proj, b_glu.reshape(1, SSM_W), wa.reshape(1, ATTN_W), ws.reshape(1, SSM_W))


def _outproj_loss(merged, w_out, x, target):
    L = x.shape[0]
    tm, tn = _tile(L, 512), 1024
    ni, nj = L // tm, D_MODEL // tn

    def body(m_ref, w_ref, x_ref, t_ref, d_ref, db_ref, l_ref):
        out = x_ref[...] + jnp.dot(m_ref[...], w_ref[...], preferred_element_type=F32)
        diff = out - t_ref[...]
        d = diff * (1.0 / D_MODEL)
        d_ref[...] = d
        db_ref[...] = d.astype(BF16)
        l_ref[...] = jnp.full((1, 8, 128), jnp.sum(diff * diff), F32)

    return pl.pallas_call(
        body,
        name="outproj_loss",
        grid=(nj, ni),
        in_specs=[pl.BlockSpec((tm, D_MODEL), lambda j, i: (i, 0)),
                  pl.BlockSpec((D_MODEL, tn), lambda j, i: (0, j)),
                  pl.BlockSpec((tm, tn), lambda j, i: (i, j)),
                  pl.BlockSpec((tm, tn), lambda j, i: (i, j))],
        out_specs=[pl.BlockSpec((tm, tn), lambda j, i: (i, j)), pl.BlockSpec((tm, tn), lambda j, i: (i, j)),
                   pl.BlockSpec((1, 8, 128), lambda j, i: (i * nj + j, 0, 0))],
        out_shape=[jax.ShapeDtypeStruct((L, D_MODEL), F32), jax.ShapeDtypeStruct((L, D_MODEL), BF16),
                   jax.ShapeDtypeStruct((ni * nj, 8, 128), F32)],
        compiler_params=_cp(("parallel", "parallel")),
    )(merged, w_out, x, target)


def _merge_bwd(d_m, og, o, yg, gpre, proj, b_glu, wa, ws):
    L = og.shape[0]
    tm = _tile(L, 256)

    def body(dm_ref, og_ref, o_ref, yg_ref, gp_ref, za0_ref, za1_ref, zs0_ref, zs1_ref, b_ref, wa_ref, ws_ref,
             do_ref, dza_ref, dzs_ref, dg_ref, dyg_ref, gwa_ref, gws_ref, gb_ref):
        i = pl.program_id(0)

        @pl.when(i == 0)
        def _():
            gwa_ref[...] = jnp.zeros_like(gwa_ref)
            gws_ref[...] = jnp.zeros_like(gws_ref)
            gb_ref[...] = jnp.zeros_like(gb_ref)

        za = jnp.concatenate([za0_ref[...], za1_ref[...]], axis=1)
        zs = jnp.concatenate([zs0_ref[...], zs1_ref[...]], axis=1)
        ogv, dma = og_ref[...], dm_ref[:, :ATTN_W]
        ra = lax.rsqrt(jnp.mean(ogv * ogv, axis=-1, keepdims=True) + NORM_EPS)
        xh = ogv * ra
        gwa_ref[...] += jnp.sum(dma * xh, axis=0, keepdims=True)
        gx = dma * wa_ref[...]
        d_og = ra * (gx - xh * jnp.mean(gx * xh, axis=-1, keepdims=True))
        do_ref[...] = d_og * _silu(za)
        dza_ref[...] = (d_og * o_ref[...] * _dsilu(za)).astype(BF16)
        ygv = yg_ref[...]
        sg = _sigmoid(gp_ref[...] + b_ref[...])
        y2 = ygv * sg
        sz = _silu(zs)
        os_ = y2 * sz
        dms = dm_ref[:, ATTN_W:]
        rs = lax.rsqrt(jnp.mean(os_ * os_, axis=-1, keepdims=True) + NORM_EPS)
        xs = os_ * rs
        gws_ref[...] += jnp.sum(dms * xs, axis=0, keepdims=True)
        gxs = dms * ws_ref[...]
        d_os = rs * (gxs - xs * jnp.mean(gxs * xs, axis=-1, keepdims=True))
        dzs_ref[...] = (d_os * y2 * _dsilu(zs)).astype(BF16)
        d_y2 = d_os * sz
        d_g = d_y2 * ygv * sg * (1.0 - sg)
        dg_ref[...] = d_g.astype(BF16)
        gb_ref[...] += jnp.sum(d_g, axis=0, keepdims=True)
        dyg_ref[...] = d_y2 * sg

    row = lambda w: pl.BlockSpec((1, w), lambda i: (0, 0))
    full = lambda w: pl.BlockSpec((tm, w), lambda i: (i, 0))
    half = lambda c: pl.BlockSpec((tm, 512), lambda i: (i, c))
    return pl.pallas_call(
        body,
        name="merge_bwd",
        grid=(L // tm,),
        in_specs=[full(D_MODEL), full(ATTN_W), full(ATTN_W), full(SSM_W), full(SSM_W),
                  half(3), half(4), half(7), half(8), row(SSM_W), row(ATTN_W), row(SSM_W)],
        out_specs=[full(ATTN_W), full(ATTN_W), full(SSM_W), full(SSM_W), full(SSM_W),
                   row(ATTN_W), row(SSM_W), row(SSM_W)],
        out_shape=[jax.ShapeDtypeStruct((L, ATTN_W), F32), jax.ShapeDtypeStruct((L, ATTN_W), BF16),
                   jax.ShapeDtypeStruct((L, SSM_W), BF16), jax.ShapeDtypeStruct((L, SSM_W), BF16),
                   jax.ShapeDtypeStruct((L, SSM_W), F32),
                   jax.ShapeDtypeStruct((1, ATTN_W), F32), jax.ShapeDtypeStruct((1, SSM_W), F32),
                   jax.ShapeDtypeStruct((1, SSM_W), F32)],
        compiler_params=_cp(("arbitrary",)),
    )(d_m, og, o, yg, gpre, proj, proj, proj, proj, b_glu.reshape(1, SSM_W), wa.reshape(1, ATTN_W),
      ws.reshape(1, SSM_W))


def _rms_bwd_x(x, norm_w, d_hn, d_out, ride):
    L = x.shape[0]
    tm = _tile(L, 256)

    def body(x_ref, w_ref, dh_ref, do_ref, gx_ref, gw_ref):
        i = pl.program_id(0)

        @pl.when(i == 0)
        def _():
            gw_ref[...] = jnp.zeros_like(gw_ref)

        xv, dh = x_ref[...], dh_ref[...]
        r = lax.rsqrt(jnp.mean(xv * xv, axis=-1, keepdims=True) + NORM_EPS)
        xh = xv * r
        gw_ref[...] += jnp.sum(dh * xh, axis=0, keepdims=True)
        gx = dh * w_ref[...]
        gx_ref[...] = do_ref[...] + r * (gx - xh * jnp.mean(gx * xh, axis=-1, keepdims=True))

    blk = pl.BlockSpec((tm, D_MODEL), lambda i: (i, 0))
    row = pl.BlockSpec((1, D_MODEL), lambda i: (0, 0))
    return _call(body, "rms_bwd_x", (L // tm,), [blk, row, blk, blk], [blk, row],
                 [jax.ShapeDtypeStruct((L, D_MODEL), F32), jax.ShapeDtypeStruct((1, D_MODEL), F32)],
                 (x, norm_w.reshape(1, D_MODEL), d_hn, d_out), ride=ride)


def _rope_table(positions):
    inv_freq = ROPE_THETA ** (-jnp.arange(0, HEAD_DIM, 2, dtype=F32) / HEAD_DIM)
    ang = positions.astype(F32)[:, None] * inv_freq
    c, s = jnp.cos(ang), jnp.sin(ang)
    return jnp.concatenate([c, c, c, c, -s, s, -s, s], axis=1)


def _step(x, positions, target, w, core, chip):
    small = {n: w[n] for n in _SMALL}
    tab = _rope_table(positions)
    mt_b, scat_b, ocat_b, a16 = _ssm_prep(small)
    perm = _chunk_perm()
    blocks = lambda t: t.reshape(N_DEV, t.shape[0] // N_DEV, t.shape[1])

    proj, hn, wt_in = _rms_inproj_gather(x, small["norm_w"], w["w_in"].T.astype(BF16), chip)
    wt_in = wt_in.reshape(IN_W, D_MODEL)
    q_rot, k_rot = _qk_prep(proj, tab, small["q_norm_w"], small["k_norm_w"])
    (og, o, lse), (w_glu,) = _attn_fwd(q_rot, k_rot, proj, small["sinks"],
                                       _gather_exchange([w["w_glu"].astype(BF16)]))
    (y, yg, hx), (w_out,) = _ssm_fwd(proj, perm, mt_b, scat_b, ocat_b, a16, small["d_skip"],
                                     _gather_exchange([w["w_out"].astype(BF16)]))
    w_glu, w_out = w_glu.reshape(SSM_W, SSM_W), w_out.reshape(D_MODEL, D_MODEL)
    gpre = _mm(yg, w_glu, "nn", F32, "glu_fwd")
    merged = _merge(og, yg, gpre, proj, small["b_glu"], small["attn_out_norm_w"], small["ssm_out_norm_w"])
    d_out, d_out_b, loss_parts = _outproj_loss(merged, w_out, x, target)
    loss = 0.5 * jnp.sum(loss_parts[:, 0, 0]) / D_MODEL

    g_w_out = blocks(_mm(merged, d_out_b, "tn", F32, "grad_w_out"))
    d_m = _mm(d_out_b, w_out, "nt", F32, "d_merged")
    d_o, d_za, d_zs, d_g, d_yg1, g_wa, g_ws, g_bglu = _merge_bwd(
        d_m, og, o, yg, gpre, proj, small["b_glu"], small["attn_out_norm_w"], small["ssm_out_norm_w"])
    g_w_glu = blocks(_mm(yg, d_g, "tn", F32, "grad_w_glu"))
    d_yg = _mm(d_g, w_glu, "nt", F32, "d_yg", add=d_yg1)
    (d_u, g_mt, g_scat, g_ocat, g_a16, g_dskip), (ra_out, ra_glu) = _ssm_bwd(
        d_yg, y, proj, hx, perm, mt_b, scat_b, ocat_b, a16, small["d_skip"], _pair_exchange([g_w_out, g_w_glu]))
    p_out = _pair_sum(g_w_out, ra_out, core, BF16, "pair_sum_out")
    p_glu = _pair_sum(g_w_glu, ra_glu, core, BF16, "pair_sum_glu")
    (d_q, d_k, d_v, g_sinks), (rb_out, rb_glu) = _attn_bwd(
        q_rot, k_rot, proj, small["sinks"], d_o, o, lse, _chip_exchange([p_out, p_glu]))
    d_proj, g_qw, g_kw = _qk_prep_bwd(proj, tab, small["q_norm_w"], small["k_norm_w"], d_q, d_k, d_v,
                                      d_za, d_u, d_zs)
    g_qw = g_qw[0, :HEAD_DIM] + g_qw[0, HEAD_DIM:]
    g_kw = g_kw[0, :HEAD_DIM] + g_kw[0, HEAD_DIM:]
    g_in_a = blocks(_mm(d_proj, hn, "tn", F32, "grad_w_in_a", panel=0))
    g_in_b, (ra_a,) = _mm(d_proj, hn, "tn", F32, "grad_w_in_b", panel=1, ride=_pair_exchange([g_in_a]))
    g_in_b = blocks(g_in_b)
    p_a = _pair_sum(g_in_a, ra_a, core, BF16, "pair_sum_in_a")
    d_hn, (rb_a, ra_b) = _mm(d_proj, wt_in, "nn", F32, "d_hn",
                             ride=_both(_chip_exchange([p_a]), _pair_exchange([g_in_b])))
    p_b = _pair_sum(g_in_b, ra_b, core, BF16, "pair_sum_in_b")
    g_small, (rb_b,) = _ssm_prep_bwd(small, g_mt, g_scat, g_ocat, g_a16, _chip_exchange([p_b]))
    (grad_x, g_nw), _ = _rms_bwd_x(x, small["norm_w"], d_hn, d_out, None)
    g_wt_in = jnp.concatenate([_chip_sum(p_a, rb_a, chip, "chip_sum_in_a"),
                               _chip_sum(p_b, rb_b, chip, "chip_sum_in_b")], axis=1)

    g_small.update(norm_w=g_nw.reshape(-1), q_norm_w=g_qw.reshape(-1), k_norm_w=g_kw.reshape(-1),
                   sinks=g_sinks[0, :N_HEADS], d_skip=g_dskip.reshape(-1), b_glu=g_bglu.reshape(-1),
                   attn_out_norm_w=g_wa.reshape(-1), ssm_out_norm_w=g_ws.reshape(-1))
    slab = _pack(g_small, loss).reshape(N_DEV, _PACK_ROWS // N_DEV, 128)
    (ra_s,) = _run_exchange(_pair_exchange([slab]), "pair_exchange_small")
    p_s = _pair_sum(slab, ra_s, core, F32, "pair_sum_small")
    (rb_s,) = _run_exchange(_chip_exchange([p_s]), "chip_exchange_small")
    (g_packed,) = _run_exchange(_gather_exchange([_chip_sum(p_s, rb_s, chip, "chip_sum_small")]), "gather_small")

    g_packed = g_packed.reshape(_PACK_ROWS, 128)
    grads = _unpack(g_packed, w)
    grads.update(w_in=g_wt_in.T,
                 w_glu=_chip_sum(p_glu, rb_glu, chip, "chip_sum_glu"),
                 w_out=_chip_sum(p_out, rb_out, chip, "chip_sum_out"))
    return g_packed[_LOSS_ROW, 0], grad_x, grads


_ANY = pl.BlockSpec(memory_space=pl.ANY)


class _Exchange:
    def __init__(self, arrays, out_shape, sems, start, finish):
        self.arrays, self.out_shape, self.sems, self.start, self.finish = arrays, out_shape, sems, start, finish


def _gather_exchange(blocks):
    n = len(blocks)

    def parts(ins, outs, sems):
        send_sems, recv_sems, local_sems = sems
        x, y, c = lax.axis_index("x"), lax.axis_index("y"), lax.axis_index("c")
        me, sibling = (x, y, c), (x, y, 1 - c)
        chips = [(1 - x, y), (x, 1 - y), (1 - x, 1 - y)]

        def slot(k, dev):
            return outs[k].at[4 * dev[0] + 2 * dev[1] + dev[2]]

        def copy(k, q, block, to, src=None):
            return pltpu.make_async_remote_copy(
                src_ref=slot(k, block) if src is None else src, dst_ref=slot(k, block),
                send_sem=send_sems.at[k, q], recv_sem=recv_sems.at[k, q], device_id=to, device_id_type=MESH)

        mine = [pltpu.make_async_copy(ins[k], slot(k, me), local_sems.at[k]) for k in range(n)]
        first = []
        for k in range(n):
            first.append(copy(k, 0, me, sibling, src=ins[k]))
            first += [copy(k, 1 + j, me, (*chip, c), src=ins[k]) for j, chip in enumerate(chips)]
        return me, sibling, chips, c, copy, mine, first

    def start(ins, outs, sems):
        *_, mine, first = parts(ins, outs, sems)
        for cp in mine + first:
            cp.start()

    def finish(ins, outs, sems):
        me, sibling, chips, c, copy, mine, first = parts(ins, outs, sems)
        passed = []
        for j, chip in enumerate(chips):
            for k in range(n):
                copy(k, 1 + j, (*chip, c), me).wait_recv()
                fwd = copy(k, 4 + j, (*chip, c), sibling)
                fwd.start()
                passed.append(fwd)
        for k in range(n):
            copy(k, 0, sibling, me).wait_recv()
            for j, chip in enumerate(chips):
                copy(k, 4 + j, (*chip, 1 - c), me).wait_recv()
        for cp in first + passed:
            cp.wait_send()
        for cp in mine:
            cp.wait()

    return _Exchange(blocks, [jax.ShapeDtypeStruct((N_DEV,) + b.shape, b.dtype) for b in blocks],
                     [pltpu.SemaphoreType.DMA((n, 7)), pltpu.SemaphoreType.DMA((n, 7)), pltpu.SemaphoreType.DMA((n,))],
                     start, finish)


def _direct_exchange(arrays, out_lead, fan, route):
    n = len(arrays)

    def copies(ins, outs, sems):
        send_sems, recv_sems = sems
        legs = route(lax.axis_index("x"), lax.axis_index("y"), lax.axis_index("c"))
        return [pltpu.make_async_remote_copy(
            src_ref=ins[k].at[src], dst_ref=outs[k].at[q], send_sem=send_sems.at[k, q], recv_sem=recv_sems.at[k, q],
            device_id=to, device_id_type=MESH) for k in range(n) for src, q, to in legs]

    def start(ins, outs, sems):
        for cp in copies(ins, outs, sems):
            cp.start()

    def finish(ins, outs, sems):
        for cp in copies(ins, outs, sems):
            cp.wait()

    return _Exchange(arrays, [jax.ShapeDtypeStruct((out_lead,) + a.shape[1:], a.dtype) for a in arrays],
                     [pltpu.SemaphoreType.DMA((n, fan)), pltpu.SemaphoreType.DMA((n, fan))], start, finish)


def _pair_exchange(grads):
    return _direct_exchange(grads, 4, 4, lambda x, y, c: [(2 * chip + (1 - c), chip, (x, y, 1 - c))
                                                          for chip in range(4)])


def _chip_exchange(parts):
    def route(x, y, c):
        chips = [(1 - x, y), (x, 1 - y), (1 - x, 1 - y)]
        return [(2 * chip[0] + chip[1], q, (*chip, c)) for q, chip in enumerate(chips)]
    return _direct_exchange(parts, 3, 3, route)


def _both(ex1, ex2):
    n1, s1 = len(ex1.arrays), len(ex1.sems)

    def halves(ins, outs, sems):
        return (ins[:n1], outs[:n1], sems[:s1]), (ins[n1:], outs[n1:], sems[s1:])

    def start(ins, outs, sems):
        h1, h2 = halves(ins, outs, sems)
        ex1.start(*h1)
        ex2.start(*h2)

    def finish(ins, outs, sems):
        h1, h2 = halves(ins, outs, sems)
        ex1.finish(*h1)
        ex2.finish(*h2)

    return _Exchange(list(ex1.arrays) + list(ex2.arrays), list(ex1.out_shape) + list(ex2.out_shape),
                     list(ex1.sems) + list(ex2.sems), start, finish)


def _run_exchange(ex, name):
    n = len(ex.arrays)

    def body(*refs):
        ins, outs, sems = refs[:n], refs[n:2 * n], refs[2 * n:]
        ex.start(ins, outs, sems)
        ex.finish(ins, outs, sems)

    return list(pl.pallas_call(body, name=name, in_specs=[_ANY] * n, out_specs=[_ANY] * n, out_shape=ex.out_shape,
                               scratch_shapes=ex.sems)(*ex.arrays))


def _call(body, name, grid, in_specs, out_specs, out_shape, args, scratch_shapes=(), ride=None):
    if ride is None:
        sem = ("arbitrary",) * len(grid)
        return pl.pallas_call(body, name=name, grid=grid, in_specs=in_specs, out_specs=out_specs, out_shape=out_shape,
                              scratch_shapes=list(scratch_shapes), compiler_params=_cp(sem))(*args), None
    n_in, n_out, n_scr, n_x = len(in_specs), len(out_specs), len(scratch_shapes), len(ride.arrays)

    def wrapped(*refs):
        ins, refs = refs[:n_in], refs[n_in:]
        x_in, refs = refs[:n_x], refs[n_x:]
        outs, refs = refs[:n_out], refs[n_out:]
        x_out, refs = refs[:n_x], refs[n_x:]
        scr, sems = refs[:n_scr], refs[n_scr:]
        first = pl.program_id(0) == 0
        last = pl.program_id(0) == grid[0] - 1
        for a in range(1, len(grid)):
            first = jnp.logical_and(first, pl.program_id(a) == 0)
            last = jnp.logical_and(last, pl.program_id(a) == grid[a] - 1)

        @pl.when(first)
        def _():
            ride.start(x_in, x_out, sems)

        body(*ins, *outs, *scr)

        @pl.when(last)
        def _():
            ride.finish(x_in, x_out, sems)

    res = pl.pallas_call(
        wrapped, name=name, grid=grid, in_specs=list(in_specs) + [_ANY] * n_x,
        out_specs=list(out_specs) + [_ANY] * n_x, out_shape=list(out_shape) + list(ride.out_shape),
        scratch_shapes=list(scratch_shapes) + list(ride.sems),
        compiler_params=_cp(("arbitrary",) * len(grid)))(*args, *ride.arrays)
    return res[:n_out], list(res[n_out:])


def _pair_sum(g, ra, core, out_dtype, name):
    _, r, C = g.shape
    tr = _tile(r, 576)

    def body(c_ref, g_ref, ra_ref, p_ref):
        p_ref[...] = (g_ref[...] + ra_ref[...]).astype(p_ref.dtype)

    return pl.pallas_call(
        body,
        name=name,
        grid_spec=pltpu.PrefetchScalarGridSpec(
            num_scalar_prefetch=1,
            grid=(4, r // tr),
            in_specs=[pl.BlockSpec((1, tr, C), lambda j, t, c_ref: (2 * j + c_ref[0], t, 0)),
                      pl.BlockSpec((1, tr, C), lambda j, t, c_ref: (j, t, 0))],
            out_specs=pl.BlockSpec((1, tr, C), lambda j, t, c_ref: (j, t, 0)),
        ),
        out_shape=jax.ShapeDtypeStruct((4, r, C), out_dtype),
        compiler_params=_cp(("parallel", "parallel")),
    )(core, g, ra)


def _chip_sum(p, rb, chip, name):
    _, r, C = p.shape
    tr = _tile(r, 576)

    def body(c_ref, p_ref, rb_ref, o_ref):
        acc = p_ref[0].astype(F32) + rb_ref[0].astype(F32)
        acc = acc + rb_ref[1].astype(F32)
        o_ref[...] = acc + rb_ref[2].astype(F32)

    return pl.pallas_call(
        body,
        name=name,
        grid_spec=pltpu.PrefetchScalarGridSpec(
            num_scalar_prefetch=1,
            grid=(r // tr,),
            in_specs=[pl.BlockSpec((1, tr, C), lambda t, c_ref: (c_ref[0], t, 0)),
                      pl.BlockSpec((3, tr, C), lambda t, c_ref: (0, t, 0))],
            out_specs=pl.BlockSpec((tr, C), lambda t, c_ref: (t, 0)),
        ),
        out_shape=jax.ShapeDtypeStruct((r, C), F32),
        compiler_params=_cp(("parallel",)),
    )(chip, p, rb)


def _adamw(g, w, m, v, name):
    R, C = g.shape
    tr = _tile(R, 256)
    c1 = 1.0 - ADAM_B1 ** ADAM_STEP
    c2 = 1.0 - ADAM_B2 ** ADAM_STEP

    def body(g_ref, w_ref, m_ref, v_ref, d_ref, nm_ref, nv_ref):
        gv = g_ref[...]
        nm = ADAM_B1 * m_ref[...] + (1.0 - ADAM_B1) * gv
        nv = ADAM_B2 * v_ref[...] + (1.0 - ADAM_B2) * (gv * gv)
        nm_ref[...] = nm
        nv_ref[...] = nv
        d_ref[...] = -ADAM_LR * ((nm / c1) / (jnp.sqrt(nv / c2) + ADAM_EPS) + ADAM_WD * w_ref[...])

    blk = pl.BlockSpec((tr, C), lambda i: (i, 0))
    return pl.pallas_call(
        body, name=name, grid=(R // tr,), in_specs=[blk] * 4, out_specs=[blk] * 3,
        out_shape=[jax.ShapeDtypeStruct((R, C), F32)] * 3, compiler_params=_cp(("parallel",)),
    )(g, w, m, v)


_SMALL = ("norm_w", "q_norm_w", "k_norm_w", "sinks", "a_re", "a_im", "log_step", "b_re", "b_im", "c_re", "c_im",
          "d_skip", "b_glu", "attn_out_norm_w", "ssm_out_norm_w")
_WEIGHTS = ("norm_w", "w_in", "q_norm_w", "k_norm_w", "sinks", "a_re", "a_im", "log_step", "b_re", "b_im", "c_re",
            "c_im", "d_skip", "w_glu", "b_glu", "attn_out_norm_w", "ssm_out_norm_w", "w_out")
_SMALL_2D = dict(norm_w=(1, 2048), q_norm_w=(1, 64), k_norm_w=(1, 64), sinks=(1, 16), a_re=(64, 64), a_im=(64, 64),
                 log_step=(1, 64), b_re=(4096, 16), b_im=(4096, 16), c_re=(1024, 64), c_im=(1024, 64),
                 d_skip=(1, 1024), b_glu=(1, 1024), attn_out_norm_w=(1, 1024), ssm_out_norm_w=(1, 1024))


def _slab_rows(n):
    return -(-n // 1024) * 8


_PACK_ROWS = 2304


_LOSS_ROW = 2192


def _pack(d, loss):
    parts = []
    for n in _SMALL:
        flat = d[n].reshape(-1).astype(F32)
        rows = _slab_rows(flat.shape[0])
        parts.append(jnp.pad(flat, (0, rows * 128 - flat.shape[0])).reshape(rows, 128))
    assert sum(p.shape[0] for p in parts) == _LOSS_ROW
    parts.append(jnp.pad(loss.reshape(1, 1), ((0, _PACK_ROWS - _LOSS_ROW - 1), (0, 127))))
    return jnp.concatenate(parts, axis=0)


def _unpack(packed, like):
    out, off = {}, 0
    for n in _SMALL:
        size = math.prod(like[n].shape)
        rows = _slab_rows(size)
        out[n] = packed[off:off + rows].reshape(-1)[:size].reshape(like[n].shape)
        off += rows
    return out


def _adamw_small(g, w, m, v):
    c1 = 1.0 - ADAM_B1 ** ADAM_STEP
    c2 = 1.0 - ADAM_B2 ** ADAM_STEP
    k = len(_SMALL)

    def body(*refs):
        ins, outs = refs[:4 * k], refs[4 * k:]
        for j in range(k):
            gv, wv, mv, vv = (ins[q * k + j][...] for q in range(4))
            nm = ADAM_B1 * mv + (1.0 - ADAM_B1) * gv
            nv = ADAM_B2 * vv + (1.0 - ADAM_B2) * (gv * gv)
            outs[j][...] = -ADAM_LR * ((nm / c1) / (jnp.sqrt(nv / c2) + ADAM_EPS) + ADAM_WD * wv)
            outs[k + j][...] = nm
            outs[2 * k + j][...] = nv

    args = [d[n].reshape(_SMALL_2D[n]) for d in (g, w, m, v) for n in _SMALL]
    shapes = [jax.ShapeDtypeStruct(_SMALL_2D[n], F32) for _ in range(3) for n in _SMALL]
    outs = pl.pallas_call(body, name="adamw_small", out_shape=shapes, compiler_params=_cp())(*args)
    res = []
    for q in range(3):
        res.append({n: outs[q * k + j].reshape(w[n].shape) for j, n in enumerate(_SMALL)})
    return res


def kernel(x, positions, norm_w, w_in, q_norm_w, k_norm_w, sinks, a_re, a_im, log_step, b_re, b_im, c_re, c_im, d_skip, w_glu, b_glu, attn_out_norm_w, ssm_out_norm_w, w_out, loss_target, m_norm_w, m_w_in, m_q_norm_w, m_k_norm_w, m_sinks, m_a_re, m_a_im, m_log_step, m_b_re, m_b_im, m_c_re, m_c_im, m_d_skip, m_w_glu, m_b_glu, m_attn_out_norm_w, m_ssm_out_norm_w, m_w_out, v_norm_w, v_w_in, v_q_norm_w, v_k_norm_w, v_sinks, v_a_re, v_a_im, v_log_step, v_b_re, v_b_im, v_c_re, v_c_im, v_d_skip, v_w_glu, v_b_glu, v_attn_out_norm_w, v_ssm_out_norm_w, v_w_out):
    w = dict(norm_w=norm_w, w_in=w_in, q_norm_w=q_norm_w, k_norm_w=k_norm_w, sinks=sinks, a_re=a_re, a_im=a_im,
             log_step=log_step, b_re=b_re, b_im=b_im, c_re=c_re, c_im=c_im, d_skip=d_skip, w_glu=w_glu, b_glu=b_glu,
             attn_out_norm_w=attn_out_norm_w, ssm_out_norm_w=ssm_out_norm_w, w_out=w_out)
    m = dict(norm_w=m_norm_w, w_in=m_w_in, q_norm_w=m_q_norm_w, k_norm_w=m_k_norm_w, sinks=m_sinks, a_re=m_a_re,
             a_im=m_a_im, log_step=m_log_step, b_re=m_b_re, b_im=m_b_im, c_re=m_c_re, c_im=m_c_im, d_skip=m_d_skip,
             w_glu=m_w_glu, b_glu=m_b_glu, attn_out_norm_w=m_attn_out_norm_w, ssm_out_norm_w=m_ssm_out_norm_w,
             w_out=m_w_out)
    v = dict(norm_w=v_norm_w, w_in=v_w_in, q_norm_w=v_q_norm_w, k_norm_w=v_k_norm_w, sinks=v_sinks, a_re=v_a_re,
             a_im=v_a_im, log_step=v_log_step, b_re=v_b_re, b_im=v_b_im, c_re=v_c_re, c_im=v_c_im, d_skip=v_d_skip,
             w_glu=v_w_glu, b_glu=v_b_glu, attn_out_norm_w=v_attn_out_norm_w, ssm_out_norm_w=v_ssm_out_norm_w,
             w_out=v_w_out)
    core = lax.axis_index("c").astype(jnp.int32).reshape(1)
    chip = (2 * lax.axis_index("x") + lax.axis_index("y")).astype(jnp.int32).reshape(1)

    loss, grad_x, grads = _step(x[0], positions[0], loss_target[0], w, core, chip)
    delta, new_m, new_v = {}, {}, {}
    for n in ("w_in", "w_glu", "w_out"):
        delta[n], new_m[n], new_v[n] = _adamw(grads[n], w[n], m[n], v[n], f"adamw_{n}")
    d_s, m_s, v_s = _adamw_small(grads, w, m, v)
    delta.update(d_s)
    new_m.update(m_s)
    new_v.update(v_s)

    return (loss, grad_x[None], *[grads[n] for n in _WEIGHTS], *[delta[n] for n in _WEIGHTS],
            *[new_m[n] for n in _WEIGHTS], *[new_v[n] for n in _WEIGHTS])
```

```python
import functools
import math

import jax
import jax.numpy as jnp
from jax import lax
from jax.experimental import pallas as pl
from jax.experimental.pallas import tpu as pltpu

F32 = jnp.float32
BF16 = jnp.bfloat16

D_MODEL = 2048
ATTN_W = 1024
KV_W = 256
SSM_W = 1024
HEAD_DIM = 64
N_HEADS = 16
N_KV = 4
KV_REP = 4
IN_W = 4608
BLOCK = 128
ROPE_THETA = 10000.0
NORM_EPS = 1e-6
SSM_G = 64
SSM_P = 64
SSM_H = 16
CHUNK = 16
CW = CHUNK * SSM_H
N_DEV = 8

ADAM_LR = 0.001
ADAM_B1 = 0.9
ADAM_B2 = 0.999
ADAM_EPS = 1e-08
ADAM_WD = 0.01
ADAM_STEP = 10

VMEM_LIMIT = 56 * 1024 * 1024
MESH = pl.DeviceIdType.MESH


def _cp(sem=None):
    if sem is None:
        return pltpu.CompilerParams(vmem_limit_bytes=VMEM_LIMIT)
    return pltpu.CompilerParams(vmem_limit_bytes=VMEM_LIMIT, dimension_semantics=sem)


def _sigmoid(x):
    return 0.5 * jnp.tanh(0.5 * x) + 0.5


def _silu(x):
    return x * _sigmoid(x)


def _dsilu(x):
    s = _sigmoid(x)
    return s * (1.0 + x * (1.0 - s))


_GELU_C = math.sqrt(2.0 / math.pi)


def _gelu(y):
    t = jnp.tanh(_GELU_C * (y + 0.044715 * y * y * y))
    return 0.5 * y * (1.0 + t)


def _dgelu(y):
    t = jnp.tanh(_GELU_C * (y + 0.044715 * y * y * y))
    return 0.5 * (1.0 + t) + 0.5 * y * (1.0 - t * t) * _GELU_C * (1.0 + 3.0 * 0.044715 * y * y)


def _tile(n, want):
    if n <= want:
        return n
    for t in range(want - want % 16, 0, -16):
        if n % t == 0:
            return t
    raise ValueError((n, want))


def _mm(a, b, mode, out_dtype, name, tm=512, tn=1024, add=None, ride=None, panel=None):
    if mode == "nn":
        (M, K), (K2, N) = a.shape, b.shape
    elif mode == "nt":
        (M, K), (N, K2) = a.shape, b.shape
    else:
        (K, M), (K2, N) = a.shape, b.shape
    assert K == K2
    tm, tn = _tile(M, tm), _tile(N, tn)
    p0 = 0
    if panel is not None:
        assert mode != "nt" and add is None
        p0, N = panel, tn
    dn = {"nn": _NN, "nt": _NT, "tn": _TN}[mode]

    def body(a_ref, b_ref, *rest):
        o_ref = rest[-1]
        acc = lax.dot_general(a_ref[...].astype(BF16), b_ref[...].astype(BF16), dn, preferred_element_type=F32)
        if add is not None:
            acc = acc + rest[0][...]
        o_ref[...] = acc.astype(o_ref.dtype)

    a_spec = pl.BlockSpec((K, tm), lambda j, i: (0, i)) if mode == "tn" else pl.BlockSpec((tm, K), lambda j, i: (i, 0))
    b_spec = (pl.BlockSpec((tn, K), lambda j, i: (j, 0)) if mode == "nt"
              else pl.BlockSpec((K, tn), lambda j, i: (0, j + p0)))
    o_spec = pl.BlockSpec((tm, tn), lambda j, i: (i, j))
    extra = () if add is None else (add,)
    if ride is not None:
        (out,), landed = _call(body, name, (N // tn, M // tm), [a_spec, b_spec] + [o_spec] * len(extra), [o_spec],
                               [jax.ShapeDtypeStruct((M, N), out_dtype)], (a, b, *extra), ride=ride)
        return out, landed
    return pl.pallas_call(
        body,
        name=name,
        grid=(N // tn, M // tm),
        in_specs=[a_spec, b_spec] + [o_spec] * len(extra),
        out_specs=o_spec,
        out_shape=jax.ShapeDtypeStruct((M, N), out_dtype),
        compiler_params=_cp(("parallel", "parallel")),
    )(a, b, *extra)


_CHIP_ORDER = (0, 2, 1, 3)


def _rms_inproj_gather(x, norm_w, wt_shard, chip):
    L = x.shape[0]
    tm = _tile(L, 512)
    ni = L // tm
    r = IN_W // N_DEV
    tn = 2 * r

    def body(chip_ref, x_ref, nw_ref, shard, proj_ref, hn_ref, wt_hbm, hn_scr, w_scr, send_sems, recv_sems, loc_sems):
        jc, i = pl.program_id(0), pl.program_id(1)
        xx, yy, c = lax.axis_index("x"), lax.axis_index("y"), lax.axis_index("c")
        me, sibling = (xx, yy, c), (xx, yy, 1 - c)
        chips = [(1 - xx, yy), (xx, 1 - yy), (1 - xx, 1 - yy)]

        def slot(dev):
            return wt_hbm.at[4 * dev[0] + 2 * dev[1] + dev[2]]

        def copy(q, block, to, src=None):
            return pltpu.make_async_remote_copy(
                src_ref=slot(block) if src is None else src, dst_ref=slot(block),
                send_sem=send_sems.at[q], recv_sem=recv_sems.at[q], device_id=to, device_id_type=MESH)

        def rows_of(buf, core):
            return w_scr.at[buf, pl.ds(pl.multiple_of(core * r, 16), r)]

        mine = pltpu.make_async_copy(shard, slot(me), loc_sems.at[0])
        sends = [copy(0, me, sibling, src=shard)] + [copy(1 + j, me, (*ch, c), src=shard) for j, ch in enumerate(chips)]
        first = jnp.logical_and(jc == 0, i == 0)

        @pl.when(first)
        def _():
            mine.start()
            for cp in sends[:3]:
                cp.start()
            own = pltpu.make_async_copy(shard, rows_of(0, c), loc_sems.at[1])
            own.start()
            copy(0, sibling, me).wait_recv()
            sib = pltpu.make_async_copy(slot(sibling), rows_of(0, 1 - c), loc_sems.at[2])
            sib.start()
            own.wait()
            sib.wait()

        for j, ch in enumerate(chips):
            @pl.when(jnp.logical_and(jc == 1 + j, i == 0))
            def _(j=j, ch=ch):
                buf = (1 + j) % 2
                copy(1 + j, (*ch, c), me).wait_recv()
                copy(4 + j, (*ch, c), sibling).start()
                if j == 0:
                    sends[1].wait_send()
                    sends[2].wait_send()
                    sends[3].start()
                direct = pltpu.make_async_copy(slot((*ch, c)), rows_of(buf, c), loc_sems.at[1])
                direct.start()
                copy(4 + j, (*ch, 1 - c), me).wait_recv()
                passed = pltpu.make_async_copy(slot((*ch, 1 - c)), rows_of(buf, 1 - c), loc_sems.at[2])
                passed.start()
                direct.wait()
                passed.wait()

        rows = pl.ds(pl.multiple_of(i * tm, tm), tm)

        @pl.when(jc == 0)
        def _():
            xv = x_ref[...]
            rstd = lax.rsqrt(jnp.mean(xv * xv, axis=-1, keepdims=True) + NORM_EPS)
            hn = (xv * rstd * nw_ref[...]).astype(BF16)
            hn_scr[rows, :] = hn
            hn_ref[...] = hn

        for buf in range(2):
            @pl.when(jc % 2 == buf)
            def _(buf=buf):
                proj_ref[...] = lax.dot_general(hn_scr[rows, :], w_scr[buf], _NT, preferred_element_type=F32)

        @pl.when(jnp.logical_and(jc == 3, i == ni - 1))
        def _():
            sends[0].wait_send()
            sends[3].wait_send()
            for j, ch in enumerate(chips):
                copy(4 + j, (*ch, c), sibling).wait_send()
            mine.wait()

    def tile_of(jc, chip_ref):
        mask = jnp.where(jc == 1, _CHIP_ORDER[1], jnp.where(jc == 2, _CHIP_ORDER[2], jnp.where(jc == 3, _CHIP_ORDER[3], 0)))
        return jnp.bitwise_xor(chip_ref[0], mask)

    held = lambda jc, i: jnp.where(jc == 0, i, ni - 1)
    return pl.pallas_call(
        body,
        name="rms_inproj_gather",
        grid_spec=pltpu.PrefetchScalarGridSpec(
            num_scalar_prefetch=1,
            grid=(4, ni),
            in_specs=[pl.BlockSpec((tm, D_MODEL), lambda jc, i, ch: (held(jc, i), 0)),
                      pl.BlockSpec((1, D_MODEL), lambda jc, i, ch: (0, 0)), _ANY],
            out_specs=[pl.BlockSpec((tm, tn), lambda jc, i, ch: (i, tile_of(jc, ch))),
                       pl.BlockSpec((tm, D_MODEL), lambda jc, i, ch: (held(jc, i), 0)), _ANY],
            scratch_shapes=[pltpu.VMEM((L, D_MODEL), BF16), pltpu.VMEM((2, tn, D_MODEL), BF16),
                            pltpu.SemaphoreType.DMA((7,)), pltpu.SemaphoreType.DMA((7,)), pltpu.SemaphoreType.DMA((3,))],
        ),
        out_shape=[jax.ShapeDtypeStruct((L, IN_W), F32), jax.ShapeDtypeStruct((L, D_MODEL), BF16),
                   jax.ShapeDtypeStruct((N_DEV, r, D_MODEL), BF16)],
        compiler_params=_cp(("arbitrary", "arbitrary")),
    )(chip, x, norm_w.reshape(1, D_MODEL), wt_shard)


def _seg_sum(v):
    a = lax.broadcasted_iota(jnp.int32, (128, 128), 0) // HEAD_DIM
    b = lax.broadcasted_iota(jnp.int32, (128, 128), 1) // HEAD_DIM
    ones = jnp.where(a == b, 1.0, 0.0).astype(BF16)
    hi = v.astype(BF16)
    lo = (v - hi.astype(F32)).astype(BF16)
    return jnp.dot(hi, ones, preferred_element_type=F32) + jnp.dot(lo, ones, preferred_element_type=F32)


def _rot_half(t):
    lane = lax.broadcasted_iota(jnp.int32, t.shape, 1)
    return jnp.where(lane % HEAD_DIM < HEAD_DIM // 2, pltpu.roll(t, 128 - HEAD_DIM // 2, 1),
                     pltpu.roll(t, HEAD_DIM // 2, 1))


def _norm_rope(raw, w, cos, sin):
    r = lax.rsqrt(_seg_sum(raw * raw) * (1.0 / HEAD_DIM) + NORM_EPS)
    tn = raw * r * w
    return r, tn * cos + _rot_half(tn) * sin


def _norm_rope_bwd(d_rot, raw, w, cos, sin):
    r = lax.rsqrt(_seg_sum(raw * raw) * (1.0 / HEAD_DIM) + NORM_EPS)
    d_tn = d_rot * cos + _rot_half(d_rot * sin)
    xh = raw * r
    gw = d_tn * w
    d_raw = r * (gw - xh * (_seg_sum(gw * xh) * (1.0 / HEAD_DIM)))
    return d_raw, d_tn * xh


def _band_mask2(has_prev):
    qi = lax.broadcasted_iota(jnp.int32, (2 * BLOCK, 2 * BLOCK), 0) % BLOCK + BLOCK
    kj = lax.broadcasted_iota(jnp.int32, (2 * BLOCK, 2 * BLOCK), 1)
    rel = qi - kj
    return (rel >= 0) & (rel < BLOCK) & ((kj >= BLOCK) | has_prev)


def _half_tiles(pair):
    lo = lax.broadcasted_iota(jnp.int32, pair.shape, 1) < HEAD_DIM
    sw = pltpu.roll(pair, HEAD_DIM, 1)
    z = jnp.zeros_like(pair)
    return (jnp.where(lo, pair, z).astype(BF16), jnp.where(lo, z, sw).astype(BF16),
            jnp.where(lo, sw, z).astype(BF16), jnp.where(lo, z, pair).astype(BF16))


def _two_rows(top, bottom):
    row = lax.broadcasted_iota(jnp.int32, (2 * BLOCK, 1), 0)
    return jnp.where(row < BLOCK, top, bottom)


def _lane_col(mat, h):
    lane = lax.broadcasted_iota(jnp.int32, mat.shape, 1)
    return jnp.sum(jnp.where(lane == h, mat, 0.0), axis=1, keepdims=True)


_SCALE = 1.0 / math.sqrt(HEAD_DIM)
_NT = (((1,), (1,)), ((), ()))
_NN = (((1,), (0,)), ((), ()))
_TN = (((0,), (0,)), ((), ()))


def _qk_prep(proj, tab, qw, kw):
    L = proj.shape[0]
    tm = _tile(L, 512)

    def body(q_ref, k_ref, t_ref, qw_ref, kw_ref, qo_ref, ko_ref):
        cos, sin = t_ref[:, :128], t_ref[:, 128:]
        for c in range(ATTN_W // 128):
            _, qr = _norm_rope(q_ref[:, c * 128:(c + 1) * 128], qw_ref[...], cos, sin)
            qo_ref[:, c * 128:(c + 1) * 128] = (qr * _SCALE).astype(BF16)
        for c in range(KV_W // 128):
            _, kr = _norm_rope(k_ref[:, c * 128:(c + 1) * 128], kw_ref[...], cos, sin)
            ko_ref[:, c * 128:(c + 1) * 128] = kr.astype(BF16)

    row = pl.BlockSpec((1, 128), lambda i: (0, 0))
    return pl.pallas_call(
        body,
        name="qk_prep",
        grid=(L // tm,),
        in_specs=[pl.BlockSpec((tm, ATTN_W), lambda i: (i, 0)), pl.BlockSpec((tm, KV_W), lambda i: (i, 4)),
                  pl.BlockSpec((tm, 256), lambda i: (i, 0)), row, row],
        out_specs=[pl.BlockSpec((tm, ATTN_W), lambda i: (i, 0)), pl.BlockSpec((tm, KV_W), lambda i: (i, 0))],
        out_shape=[jax.ShapeDtypeStruct((L, ATTN_W), BF16), jax.ShapeDtypeStruct((L, KV_W), BF16)],
        compiler_params=_cp(("parallel",)),
    )(proj, proj, tab, jnp.tile(qw, 2).reshape(1, 128), jnp.tile(kw, 2).reshape(1, 128))


def _group_tiles(g, kt, vt):
    a, b = divmod(g, 2)
    return kt[a][2 * b], kt[a][2 * b + 1], vt[a][2 * b], vt[a][2 * b + 1]


def _attn_fwd(q, k, proj, sinks, ride):
    L = proj.shape[0]
    nb = L // BLOCK

    def body(q_ref, kc_ref, kp_ref, vc_ref, vp_ref, z0_ref, z1_ref, sink_ref, og_ref, o_ref, lse_ref):
        i = pl.program_id(0)
        mask = _band_mask2(i > 0)
        z = jnp.concatenate([z0_ref[...], z1_ref[...]], axis=1)
        lane = lax.broadcasted_iota(jnp.int32, (BLOCK, 128), 1)
        kt = [_half_tiles(jnp.concatenate([kp_ref[:, a * 128:(a + 1) * 128], kc_ref[:, a * 128:(a + 1) * 128]],
                                          axis=0).astype(F32)) for a in range(2)]
        vt = [_half_tiles(jnp.concatenate([vp_ref[:, a * 128:(a + 1) * 128], vc_ref[:, a * 128:(a + 1) * 128]],
                                          axis=0)) for a in range(2)]
        lse_mat = jnp.zeros((BLOCK, 128), F32)
        outs = []
        for g in range(N_KV):
            k_lo, k_hi, v_lo, v_hi = _group_tiles(g, kt, vt)
            q2 = jnp.concatenate([q_ref[:, 2 * g * 128:(2 * g + 1) * 128],
                                  q_ref[:, (2 * g + 1) * 128:(2 * g + 2) * 128]], axis=0)
            acc = jnp.zeros((2 * BLOCK, 128), F32)
            for half, (kh, vh) in enumerate(((k_lo, v_lo), (k_hi, v_hi))):
                h_top, h_bot = 4 * g + half, 4 * g + 2 + half
                s = jnp.where(mask, lax.dot_general(q2, kh, _NT, preferred_element_type=F32), -1e30)
                sink = _two_rows(sink_ref[h_top], sink_ref[h_bot])
                m = jnp.maximum(jnp.max(s, axis=-1, keepdims=True), sink)
                e = jnp.exp(s - m)
                den = jnp.sum(e, axis=-1, keepdims=True) + jnp.exp(sink - m)
                p = e * (1.0 / den)
                acc = acc + jnp.dot(p.astype(BF16), vh, preferred_element_type=F32)
                lse = m + jnp.log(den)
                lse_mat = jnp.where(lane == h_top, lse[:BLOCK], lse_mat)
                lse_mat = jnp.where(lane == h_bot, lse[BLOCK:], lse_mat)
            outs += [acc[:BLOCK], acc[BLOCK:]]
        o = jnp.concatenate(outs, axis=1)
        o_ref[...] = o
        og_ref[...] = o * _silu(z)
        lse_ref[...] = lse_mat

    prev = lambda i: jnp.maximum(i - 1, 0)
    return _call(
        body, "attn_fwd", (nb,),
        [pl.BlockSpec((BLOCK, ATTN_W), lambda i: (i, 0)),
         pl.BlockSpec((BLOCK, KV_W), lambda i: (i, 0)),
         pl.BlockSpec((BLOCK, KV_W), lambda i: (prev(i), 0)),
         pl.BlockSpec((BLOCK, KV_W), lambda i: (i, 5)),
         pl.BlockSpec((BLOCK, KV_W), lambda i: (prev(i), 5)),
         pl.BlockSpec((BLOCK, 512), lambda i: (i, 3)),
         pl.BlockSpec((BLOCK, 512), lambda i: (i, 4)),
         pl.BlockSpec(memory_space=pltpu.SMEM)],
        [pl.BlockSpec((BLOCK, ATTN_W), lambda i: (i, 0)),
         pl.BlockSpec((BLOCK, ATTN_W), lambda i: (i, 0)),
         pl.BlockSpec((BLOCK, 128), lambda i: (i, 0))],
        [jax.ShapeDtypeStruct((L, ATTN_W), F32), jax.ShapeDtypeStruct((L, ATTN_W), F32),
         jax.ShapeDtypeStruct((L, 128), F32)],
        (q, k, k, proj, proj, proj, proj, sinks), ride=ride)


def _attn_bwd(q, k, proj, sinks, d_o, o, lse, ride):
    L = proj.shape[0]
    nb = L // BLOCK

    def body(q_ref, kc_ref, kp_ref, vc_ref, vp_ref, do_ref, o_ref, lse_ref, sink_ref,
             dq_ref, dk_ref, dv_ref, gs_ref, ck_scr, cv_scr):
        i = pl.program_id(0)

        @pl.when(i == 0)
        def _():
            gs_ref[...] = jnp.zeros_like(gs_ref)
            ck_scr[...] = jnp.zeros_like(ck_scr)
            cv_scr[...] = jnp.zeros_like(cv_scr)

        @pl.when(i == nb)
        def _():
            dk_ref[...] = ck_scr[...]
            dv_ref[...] = cv_scr[...]

        @pl.when(i < nb)
        def _():
            mask = _band_mask2(i > 0)
            lane = lax.broadcasted_iota(jnp.int32, (1, 128), 1)
            lo = lax.broadcasted_iota(jnp.int32, (2 * BLOCK, 128), 1) < HEAD_DIM
            lse_c = lse_ref[...]
            kt = [_half_tiles(jnp.concatenate([kp_ref[:, a * 128:(a + 1) * 128], kc_ref[:, a * 128:(a + 1) * 128]],
                                              axis=0).astype(F32)) for a in range(2)]
            vt = [_half_tiles(jnp.concatenate([vp_ref[:, a * 128:(a + 1) * 128], vc_ref[:, a * 128:(a + 1) * 128]],
                                              axis=0)) for a in range(2)]
            gs = jnp.zeros((1, 128), F32)
            dq_parts = []
            dk_acc = [jnp.zeros((2 * BLOCK, 128), F32) for _ in range(2)]
            dv_acc = [jnp.zeros((2 * BLOCK, 128), F32) for _ in range(2)]
            for g in range(N_KV):
                a, b = divmod(g, 2)
                k_lo, k_hi, v_lo, v_hi = _group_tiles(g, kt, vt)
                t0, t1 = slice(2 * g * 128, (2 * g + 1) * 128), slice((2 * g + 1) * 128, (2 * g + 2) * 128)
                q2 = jnp.concatenate([q_ref[:, t0], q_ref[:, t1]], axis=0)
                do2 = jnp.concatenate([do_ref[:, t0], do_ref[:, t1]], axis=0)
                prod = do2 * jnp.concatenate([o_ref[:, t0], o_ref[:, t1]], axis=0)
                do2_b = do2.astype(BF16)
                dq2 = jnp.zeros((2 * BLOCK, 128), F32)
                dk_h, dv_h = [], []
                for half, (kh, vh) in enumerate(((k_lo, v_lo), (k_hi, v_hi))):
                    h_top, h_bot = 4 * g + half, 4 * g + 2 + half
                    lse = jnp.concatenate([_lane_col(lse_c, h_top), _lane_col(lse_c, h_bot)], axis=0)
                    sink = _two_rows(sink_ref[h_top], sink_ref[h_bot])
                    delta = jnp.sum(jnp.where(lo == (half == 0), prod, 0.0), axis=1, keepdims=True)
                    s = jnp.where(mask, lax.dot_general(q2, kh, _NT, preferred_element_type=F32), -1e30)
                    p = jnp.exp(s - lse)
                    dp = lax.dot_general(do2_b, vh, _NT, preferred_element_type=F32)
                    ds_b = (p * (dp - delta)).astype(BF16)
                    p_b = p.astype(BF16)
                    dq2 = dq2 + jnp.dot(ds_b, kh, preferred_element_type=F32)
                    dk_h.append(lax.dot_general(ds_b, q2, _TN, preferred_element_type=F32))
                    dv_h.append(lax.dot_general(p_b, do2_b, _TN, preferred_element_type=F32))
                    gsink = -jnp.exp(sink - lse) * delta
                    row = lax.broadcasted_iota(jnp.int32, (2 * BLOCK, 1), 0)
                    gs = gs + jnp.where(lane == h_top, jnp.sum(jnp.where(row < BLOCK, gsink, 0.0)), 0.0)
                    gs = gs + jnp.where(lane == h_bot, jnp.sum(jnp.where(row >= BLOCK, gsink, 0.0)), 0.0)
                dq_parts += [dq2[:BLOCK], dq2[BLOCK:]]
                for acc, parts in ((dk_acc, dk_h), (dv_acc, dv_h)):
                    t = jnp.where(lo, parts[0], parts[1])
                    t = t + pltpu.roll(t, HEAD_DIM, 1)
                    acc[a] = acc[a] + jnp.where(lo == (b == 0), t, 0.0)
            dq_ref[...] = jnp.concatenate(dq_parts, axis=1)
            dk_full = jnp.concatenate(dk_acc, axis=1)
            dv_full = jnp.concatenate(dv_acc, axis=1)
            dk_ref[...] = ck_scr[...] + dk_full[:BLOCK]
            dv_ref[...] = cv_scr[...] + dv_full[:BLOCK]
            ck_scr[...] = dk_full[BLOCK:]
            cv_scr[...] = dv_full[BLOCK:]
            gs_ref[...] += gs

    cur = lambda i: jnp.minimum(i, nb - 1)
    prev = lambda i: jnp.maximum(jnp.minimum(i, nb - 1) - 1, 0)
    done = lambda i: jnp.maximum(i - 1, 0)
    bs = pl.BlockSpec
    return _call(
        body, "attn_bwd", (nb + 1,),
        [bs((BLOCK, ATTN_W), lambda i: (cur(i), 0)),
         bs((BLOCK, KV_W), lambda i: (cur(i), 0)), bs((BLOCK, KV_W), lambda i: (prev(i), 0)),
         bs((BLOCK, KV_W), lambda i: (cur(i), 5)), bs((BLOCK, KV_W), lambda i: (prev(i), 5)),
         bs((BLOCK, ATTN_W), lambda i: (cur(i), 0)), bs((BLOCK, ATTN_W), lambda i: (cur(i), 0)),
         bs((BLOCK, 128), lambda i: (cur(i), 0)), bs(memory_space=pltpu.SMEM)],
        [bs((BLOCK, ATTN_W), lambda i: (cur(i), 0)),
         bs((BLOCK, KV_W), lambda i: (done(i), 0)), bs((BLOCK, KV_W), lambda i: (done(i), 0)),
         bs((1, 128), lambda i: (0, 0))],
        [jax.ShapeDtypeStruct((L, ATTN_W), F32), jax.ShapeDtypeStruct((L, KV_W), F32),
         jax.ShapeDtypeStruct((L, KV_W), F32), jax.ShapeDtypeStruct((1, 128), F32)],
        (q, k, k, proj, proj, d_o, o, lse, sinks),
        [pltpu.VMEM((BLOCK, KV_W), F32), pltpu.VMEM((BLOCK, KV_W), F32)], ride)


def _qk_prep_bwd(proj, tab, qw, kw, d_q, d_k, d_v, d_za, d_u, d_zs):
    L = proj.shape[0]
    tm = _tile(L, 512)
    z0 = ATTN_W + 2 * KV_W

    def body(q_ref, k_ref, t_ref, qw_ref, kw_ref, dq_ref, dk_ref, dv_ref, dza_ref, du_ref, dzs_ref,
             out_ref, gq_ref, gk_ref):
        i = pl.program_id(0)

        @pl.when(i == 0)
        def _():
            gq_ref[...] = jnp.zeros_like(gq_ref)
            gk_ref[...] = jnp.zeros_like(gk_ref)

        cos, sin = t_ref[:, :128], t_ref[:, 128:]
        gq = jnp.zeros((1, 128), F32)
        gk = jnp.zeros((1, 128), F32)
        for c in range(ATTN_W // 128):
            cs = slice(c * 128, (c + 1) * 128)
            d_raw, gw = _norm_rope_bwd(dq_ref[:, cs] * _SCALE, q_ref[:, cs], qw_ref[...], cos, sin)
            out_ref[:, cs] = d_raw.astype(BF16)
            gq = gq + jnp.sum(gw, axis=0, keepdims=True)
        for c in range(KV_W // 128):
            cs = slice(c * 128, (c + 1) * 128)
            d_raw, gw = _norm_rope_bwd(dk_ref[:, cs], k_ref[:, cs], kw_ref[...], cos, sin)
            out_ref[:, ATTN_W + c * 128:ATTN_W + (c + 1) * 128] = d_raw.astype(BF16)
            gk = gk + jnp.sum(gw, axis=0, keepdims=True)
        out_ref[:, ATTN_W + KV_W:z0] = dv_ref[...].astype(BF16)
        out_ref[:, z0:z0 + ATTN_W] = dza_ref[...]
        out_ref[:, z0 + ATTN_W:z0 + ATTN_W + SSM_W] = du_ref[...].astype(BF16)
        out_ref[:, z0 + ATTN_W + SSM_W:] = dzs_ref[...]
        gq_ref[...] += gq
        gk_ref[...] += gk

    row = pl.BlockSpec((1, 128), lambda i: (0, 0))
    blk = lambda w, c: pl.BlockSpec((tm, w), lambda i: (i, c))
    return pl.pallas_call(
        body,
        name="qk_prep_bwd",
        grid=(L // tm,),
        in_specs=[blk(ATTN_W, 0), blk(KV_W, 4), blk(256, 0), row, row, blk(ATTN_W, 0), blk(KV_W, 0), blk(KV_W, 0),
                  blk(ATTN_W, 0), blk(SSM_W, 0), blk(SSM_W, 0)],
        out_specs=[blk(IN_W, 0), row, row],
        out_shape=[jax.ShapeDtypeStruct((L, IN_W), BF16), jax.ShapeDtypeStruct((1, 128), F32),
                   jax.ShapeDtypeStruct((1, 128), F32)],
        compiler_params=_cp(("arbitrary",)),
    )(proj, proj, tab, jnp.tile(qw, 2).reshape(1, 128), jnp.tile(kw, 2).reshape(1, 128), d_q, d_k, d_v,
      d_za, d_u, d_zs)


def _cmul(a, b):
    return a[0] * b[0] - a[1] * b[1], a[0] * b[1] + a[1] * b[0]


def _cmul_conj(a, b):
    return a[0] * b[0] + a[1] * b[1], a[1] * b[0] - a[0] * b[1]


def _cadd(a, b):
    return a[0] + b[0], a[1] + b[1]


def _dot3(a, b, dn):
    ah, bh = a.astype(BF16), b.astype(BF16)
    al, bl = (a - ah.astype(F32)).astype(BF16), (b - bh.astype(F32)).astype(BF16)
    d = lambda u, v: lax.dot_general(u, v, dn, preferred_element_type=F32)
    return d(ah, bh) + d(ah, bl) + d(al, bh)


def _s5_discretise(a_re, a_im, ls, cosx, sinx, bt):
    delta = jnp.exp(ls)
    er = jnp.exp(a_re * delta)
    lb = (er * cosx, er * sinx)
    den = a_re * a_re + a_im * a_im
    coef = _cmul_conj((lb[0] - 1.0, lb[1]), (a_re, a_im))
    coef = (coef[0] / den, coef[1] / den)
    return delta, lb, coef, den, _cmul(coef, bt)


def _powers(lb):
    pw = [(jnp.ones_like(lb[0]), jnp.zeros_like(lb[0]))]
    for _ in range(CHUNK):
        pw.append(_cmul(pw[-1], lb))
    return pw


def _block_rows(a, pw, idx):
    blocks = [_cmul(a, pw[i]) for i in idx]
    return (jnp.concatenate([b[0] for b in blocks], axis=-2), jnp.concatenate([b[1] for b in blocks], axis=-2))


def _block_rows_bwd(g, a, pw, idx, g_pw):
    g_a = (jnp.zeros_like(a[0]), jnp.zeros_like(a[0]))
    for j, i in enumerate(idx):
        gj = (g[0][..., j * SSM_H:(j + 1) * SSM_H, :], g[1][..., j * SSM_H:(j + 1) * SSM_H, :])
        g_a = _cadd(g_a, _cmul_conj(gj, pw[i]))
        gp = _cmul_conj(gj, a)
        g_pw[i] = _cadd(g_pw[i], (jnp.sum(gp[0], axis=-2, keepdims=True), jnp.sum(gp[1], axis=-2, keepdims=True)))
    return g_a


_IDX_S = [CHUNK - 1 - s for s in range(CHUNK)]
_IDX_O = [t + 1 for t in range(CHUNK)]
_IDX_K = list(range(CHUNK))
_PREP_IN = 9


def _prep_args(p):
    row = lambda t: t.reshape(SSM_G, 1, SSM_P)
    xi = p["a_im"] * jnp.exp(p["log_step"])[:, None]
    return (row(p["a_re"]), row(p["a_im"]), row(jnp.broadcast_to(p["log_step"][:, None], (SSM_G, SSM_P))),
            row(jnp.cos(xi)), row(jnp.sin(xi)), p["b_re"].transpose(0, 2, 1), p["b_im"].transpose(0, 2, 1),
            p["c_re"], p["c_im"])


PREP_GROUPS = 8


def _prep_specs():
    r1 = pl.BlockSpec((PREP_GROUPS, 1, SSM_P), lambda g: (g, 0, 0))
    r16 = pl.BlockSpec((PREP_GROUPS, SSM_H, SSM_P), lambda g: (g, 0, 0))
    return [r1] * 5 + [r16] * 4, r1, r16


def _ssm_prep(p):
    def one_group(q, are, aim, ls, cosx, sinx, btr, bti, cre, cim, mt_ref, s_ref, o_ref, a_ref):
        _, lb, _, _, bb = _s5_discretise(are[q], aim[q], ls[q], cosx[q], sinx[q], (btr[q], bti[q]))
        pw = _powers(lb)
        c = (cre[q], cim[q])
        sc = _block_rows(bb, pw, _IDX_S)
        ot = _block_rows(c, pw, _IDX_O)
        ok = _block_rows(c, pw, _IDX_K)
        s_ref[q] = jnp.concatenate([sc[0], sc[1]], axis=1).astype(BF16)
        o_ref[q] = jnp.concatenate([ot[0], -ot[1]], axis=1).astype(BF16)
        a_ref[q] = jnp.concatenate([pw[CHUNK][0], pw[CHUNK][1]], axis=1)
        kt = _dot3(jnp.concatenate([bb[0], -bb[1]], axis=1), jnp.concatenate([ok[0], ok[1]], axis=1), _NT)
        lane = lax.broadcasted_iota(jnp.int32, kt.shape, 1)
        for s in range(CHUNK):
            blk = kt if s == 0 else jnp.where(lane >= SSM_H * s, pltpu.roll(kt, SSM_H * s, 1), 0.0)
            mt_ref[q, s * SSM_H:(s + 1) * SSM_H, :] = blk.astype(BF16)

    def body(*refs):
        for q in range(PREP_GROUPS):
            one_group(q, *refs)

    in_specs, r1, _ = _prep_specs()
    g3 = lambda r, c: pl.BlockSpec((PREP_GROUPS, r, c), lambda g: (g, 0, 0))
    return pl.pallas_call(
        body,
        name="ssm_prep",
        grid=(SSM_G // PREP_GROUPS,),
        in_specs=in_specs,
        out_specs=[g3(CW, CW), g3(CW, 2 * SSM_P), g3(CW, 2 * SSM_P), g3(1, 2 * SSM_P)],
        out_shape=[jax.ShapeDtypeStruct((SSM_G, CW, CW), BF16), jax.ShapeDtypeStruct((SSM_G, CW, 2 * SSM_P), BF16),
                   jax.ShapeDtypeStruct((SSM_G, CW, 2 * SSM_P), BF16),
                   jax.ShapeDtypeStruct((SSM_G, 1, 2 * SSM_P), F32)],
        compiler_params=_cp(("parallel",)),
    )(*_prep_args(p))


def _ssm_prep_bwd(p, g_mt, g_scat, g_ocat, g_a16, ride):
    def body(are, aim, ls, cosx, sinx, btr, bti, cre, cim, gmt_ref, gs_ref, go_ref, ga_ref,
             g_are, g_aim, g_ls, g_btr, g_bti, g_cre, g_cim, ga1_scr, gb1_scr):
        lam = (are[...], aim[...])
        bt = (btr[...], bti[...])
        delta, lb, coef, den, bb = _s5_discretise(lam[0], lam[1], ls[...], cosx[...], sinx[...], bt)
        pw = _powers(lb)
        c = (cre[...], cim[...])
        ok = _block_rows(c, pw, _IDX_K)
        g_pw = [(jnp.zeros_like(lb[0]), jnp.zeros_like(lb[0])) for _ in range(CHUNK + 1)]
        lane = lax.broadcasted_iota(jnp.int32, (SSM_H, CW), 1)
        for q in range(PREP_GROUPS):
            g_kt = gmt_ref[q, :SSM_H, :]
            for s in range(1, CHUNK):
                blk = gmt_ref[q, s * SSM_H:(s + 1) * SSM_H, :]
                g_kt = g_kt + jnp.where(lane < CW - SSM_H * s, pltpu.roll(blk, CW - SSM_H * s, 1), 0.0)
            a1 = jnp.concatenate([bb[0][q], -bb[1][q]], axis=1)
            b1 = jnp.concatenate([ok[0][q], ok[1][q]], axis=1)
            ga1_scr[q] = _dot3(g_kt, b1, _NN)
            gb1_scr[q] = _dot3(g_kt, a1, _TN)
        g_a1, g_b1 = ga1_scr[...], gb1_scr[...]
        g_bb = (g_a1[..., :SSM_P], -g_a1[..., SSM_P:])
        g_c = _block_rows_bwd((g_b1[..., :SSM_P], g_b1[..., SSM_P:]), c, pw, _IDX_K, g_pw)
        gs = gs_ref[...]
        g_bb = _cadd(g_bb, _block_rows_bwd((gs[..., :SSM_P], gs[..., SSM_P:]), bb, pw, _IDX_S, g_pw))
        go = go_ref[...]
        g_c = _cadd(g_c, _block_rows_bwd((go[..., :SSM_P], -go[..., SSM_P:]), c, pw, _IDX_O, g_pw))
        ga = ga_ref[...]
        g_pw[CHUNK] = _cadd(g_pw[CHUNK], (ga[..., :SSM_P], ga[..., SSM_P:]))
        g_lb = (jnp.zeros_like(lb[0]), jnp.zeros_like(lb[0]))
        for l in range(CHUNK - 1, -1, -1):
            g_lb = _cadd(g_lb, _cmul_conj(g_pw[l + 1], pw[l]))
            g_pw[l] = _cadd(g_pw[l], _cmul_conj(g_pw[l + 1], lb))
        g_bt = _cmul_conj(g_bb, coef)
        gc = _cmul_conj(g_bb, bt)
        g_coef = (jnp.sum(gc[0], axis=-2, keepdims=True), jnp.sum(gc[1], axis=-2, keepdims=True))
        lam_den = (lam[0] / den, lam[1] / den)
        g_lb = _cadd(g_lb, _cmul(g_coef, lam_den))
        t = _cmul(_cmul_conj(g_coef, coef), lam_den)
        g_x = _cmul_conj(g_lb, lb)
        g_are[...] = g_x[0] * delta - t[0]
        g_aim[...] = g_x[1] * delta - t[1]
        g_ls[...] = (g_x[0] * lam[0] + g_x[1] * lam[1]) * delta
        g_btr[...] = g_bt[0]
        g_bti[...] = g_bt[1]
        g_cre[...] = g_c[0]
        g_cim[...] = g_c[1]

    in_specs, r1, r16 = _prep_specs()
    g3 = lambda r, c: pl.BlockSpec((PREP_GROUPS, r, c), lambda g: (g, 0, 0))
    rows = jax.ShapeDtypeStruct((SSM_G, 1, SSM_P), F32)
    mats = jax.ShapeDtypeStruct((SSM_G, SSM_H, SSM_P), F32)
    (g_are, g_aim, g_ls, g_btr, g_bti, g_cre, g_cim), landed = _call(
        body, "ssm_prep_bwd", (SSM_G // PREP_GROUPS,),
        in_specs + [g3(CW, CW), g3(CW, 2 * SSM_P), g3(CW, 2 * SSM_P), g3(1, 2 * SSM_P)],
        [r1] * 3 + [r16] * 4, [rows] * 3 + [mats] * 4, (*_prep_args(p), g_mt, g_scat, g_ocat, g_a16),
        [pltpu.VMEM((PREP_GROUPS, SSM_H, 2 * SSM_P), F32), pltpu.VMEM((PREP_GROUPS, CW, 2 * SSM_P), F32)], ride)
    grads = dict(a_re=g_are.reshape(SSM_G, SSM_P), a_im=g_aim.reshape(SSM_G, SSM_P),
                 log_step=jnp.sum(g_ls.reshape(SSM_G, SSM_P), axis=1),
                 b_re=g_btr.transpose(0, 2, 1), b_im=g_bti.transpose(0, 2, 1), c_re=g_cre, c_im=g_cim)
    return grads, landed


def _cmul_const(xv, ar, ai):
    return xv * ar + pltpu.roll(xv, SSM_P, 1) * ai


def _chunk_scan(inc, a_row, reverse):
    n = inc.shape[0]
    lane = lax.broadcasted_iota(jnp.int32, (1, 2 * SSM_P), 1)
    row = lax.broadcasted_iota(jnp.int32, inc.shape, 0)
    sign = jnp.where(lane < SSM_P, -1.0, 1.0)
    ar = jnp.where(lane < SSM_P, a_row, pltpu.roll(a_row, SSM_P, 1))
    ai = jnp.where(lane < SSM_P, pltpu.roll(a_row, SSM_P, 1), a_row)
    if reverse:
        ai = -ai
    xv = inc
    s = 1
    while s < n:
        if reverse:
            sh = jnp.where(row < n - s, pltpu.roll(xv, n - s, 0), 0.0)
        else:
            sh = jnp.where(row >= s, pltpu.roll(xv, s, 0), 0.0)
        xv = xv + _cmul_const(sh, ar, ai * sign)
        ar, ai = ar * ar - ai * ai, 2.0 * ar * ai
        s *= 2
    return xv


def _shift_rows(xv, reverse):
    n = xv.shape[0]
    row = lax.broadcasted_iota(jnp.int32, xv.shape, 0)
    if reverse:
        return jnp.where(row < n - 1, pltpu.roll(xv, n - 1, 0), 0.0)
    return jnp.where(row >= 1, pltpu.roll(xv, 1, 0), 0.0)


GB = 128 // SSM_H
U_COL0 = (ATTN_W + 2 * KV_W + ATTN_W) // 128


HALF = CHUNK // 2


def _chunk_perm():
    r = jnp.arange(HALF * 128)
    t, g8, h = r // 128, (r % 128) // SSM_H, r % SSM_H
    return ((g8 * 128 + t * SSM_H + h)[:, None] == jnp.arange(GB * 128)[None, :]).astype(BF16)


def _load_perm(p_hbm, p_scr, sem):
    @pl.when(pl.program_id(0) == 0)
    def _():
        cp = pltpu.make_async_copy(p_hbm, p_scr, sem)
        cp.start()
        cp.wait()


def _rows_to_chunks(pieces, perm):
    halves = [jnp.dot(jnp.concatenate(pieces[k * HALF:(k + 1) * HALF], axis=1).astype(BF16), perm,
                      preferred_element_type=F32).astype(BF16) for k in range(2)]
    return [jnp.concatenate([hv[:, g * 128:(g + 1) * 128] for hv in halves], axis=1) for g in range(GB)]


def _chunks_to_rows(groups, perm, two_pass):
    pieces = []
    for k in range(2):
        v = jnp.concatenate([gv[:, k * 128:(k + 1) * 128] for gv in groups], axis=1)
        hi = v.astype(BF16)
        out = lax.dot_general(hi, perm, _NT, preferred_element_type=F32)
        if two_pass:
            lo = (v - hi.astype(F32)).astype(BF16)
            out = out + lax.dot_general(lo, perm, _NT, preferred_element_type=F32)
        pieces += [out[:, t * 128:(t + 1) * 128] for t in range(HALF)]
    return pieces


def _ssm_fwd(proj, perm, mt, scat, ocat, a16, d_skip, ride):
    L = proj.shape[0]
    nc = L // CHUNK

    def body(u_ref, p_hbm, mt_ref, s_ref, o_ref, a_ref, d_ref, y_ref, yg_ref, h_ref, p_scr, sem):
        _load_perm(p_hbm, p_scr, sem)
        perm = p_scr[...]
        rows = [pl.ds(t, nc, stride=CHUNK) for t in range(CHUNK)]
        us = [u_ref[r, :] for r in rows]
        ua = _rows_to_chunks(us, perm)
        ys = []
        for g in range(GB):
            uv = ua[g]
            inc = jnp.dot(uv, s_ref[g], preferred_element_type=F32)
            hx = _shift_rows(_chunk_scan(inc, a_ref[g], False), False)
            h_ref[g] = hx
            ys.append(jnp.dot(uv, mt_ref[g], preferred_element_type=F32)
                      + lax.dot_general(hx.astype(BF16), o_ref[g], _NT, preferred_element_type=F32))
        yp = _chunks_to_rows(ys, perm, True)
        for t, r in enumerate(rows):
            y = yp[t] + d_ref[...] * us[t]
            y_ref[r, :] = y
            yg_ref[r, :] = _gelu(y)

    g3 = lambda r, c: pl.BlockSpec((GB, r, c), lambda g: (g, 0, 0))
    col = pl.BlockSpec((L, 128), lambda g: (0, g))
    return _call(
        body, "ssm_fwd", (SSM_G // GB,),
        [pl.BlockSpec((L, 128), lambda g: (0, U_COL0 + g)), _ANY,
         g3(CW, CW), g3(CW, 2 * SSM_P), g3(CW, 2 * SSM_P), g3(1, 2 * SSM_P),
         pl.BlockSpec((1, 128), lambda g: (0, g))],
        [col, col, g3(nc, 2 * SSM_P)],
        [jax.ShapeDtypeStruct((L, SSM_W), F32), jax.ShapeDtypeStruct((L, SSM_W), F32),
         jax.ShapeDtypeStruct((SSM_G, nc, 2 * SSM_P), F32)],
        (proj, perm, mt, scat, ocat, a16, d_skip.reshape(1, SSM_W)),
        [pltpu.VMEM((HALF * 128, GB * 128), BF16), pltpu.SemaphoreType.DMA], ride)


def _ssm_bwd(d_yg, y, proj, hx, perm, mt, scat, ocat, a16, d_skip, ride):
    L = proj.shape[0]
    nc = L // CHUNK

    def body(dg_ref, y_ref, u_ref, h_ref, p_hbm, mt_ref, s_ref, o_ref, a_ref, d_ref,
             du_ref, gmt_ref, gs_ref, go_ref, ga_ref, gd_ref, p_scr, sem):
        _load_perm(p_hbm, p_scr, sem)
        perm = p_scr[...]
        rows = [pl.ds(t, nc, stride=CHUNK) for t in range(CHUNK)]
        us = [u_ref[r, :] for r in rows]
        dys = [dg_ref[r, :] * _dgelu(y_ref[r, :]) for r in rows]
        gd = jnp.zeros((1, 128), F32)
        for uv, dy in zip(us, dys):
            gd = gd + jnp.sum(dy * uv, axis=0, keepdims=True)
        gd_ref[...] = gd
        ua = _rows_to_chunks(us, perm)
        dya = _rows_to_chunks(dys, perm)
        lane = lax.broadcasted_iota(jnp.int32, (1, 2 * SSM_P), 1)
        dus = []
        for g in range(GB):
            uv, dy, hx_v = ua[g], dya[g], h_ref[g]
            dh = jnp.dot(dy, o_ref[g], preferred_element_type=F32)
            dinc = _shift_rows(_chunk_scan(dh, a_ref[g], True), True)
            dinc_b = dinc.astype(BF16)
            dus.append(lax.dot_general(dy, mt_ref[g], _NT, preferred_element_type=F32)
                       + lax.dot_general(dinc_b, s_ref[g], _NT, preferred_element_type=F32))
            gmt_ref[g] = lax.dot_general(uv, dy, _TN, preferred_element_type=F32)
            gs_ref[g] = lax.dot_general(uv, dinc_b, _TN, preferred_element_type=F32)
            go_ref[g] = lax.dot_general(dy, hx_v.astype(BF16), _TN, preferred_element_type=F32)
            p1 = dinc * hx_v
            p2 = pltpu.roll(dinc, SSM_P, 1) * hx_v
            t1 = jnp.sum(p1 + pltpu.roll(p1, SSM_P, 1), axis=0, keepdims=True)
            t2 = jnp.sum(p2 - pltpu.roll(p2, SSM_P, 1), axis=0, keepdims=True)
            ga_ref[g] = jnp.where(lane < SSM_P, t1, pltpu.roll(t2, SSM_P, 1))
        dup = _chunks_to_rows(dus, perm, False)
        for t, r in enumerate(rows):
            du_ref[r, :] = dup[t] + d_ref[...] * dys[t]

    g3 = lambda r, c: pl.BlockSpec((GB, r, c), lambda g: (g, 0, 0))
    col = pl.BlockSpec((L, 128), lambda g: (0, g))
    row = pl.BlockSpec((1, 128), lambda g: (0, g))
    return _call(
        body, "ssm_bwd", (SSM_G // GB,),
        [col, col, pl.BlockSpec((L, 128), lambda g: (0, U_COL0 + g)), g3(nc, 2 * SSM_P), _ANY,
         g3(CW, CW), g3(CW, 2 * SSM_P), g3(CW, 2 * SSM_P), g3(1, 2 * SSM_P), row],
        [col, g3(CW, CW), g3(CW, 2 * SSM_P), g3(CW, 2 * SSM_P), g3(1, 2 * SSM_P), row],
        [jax.ShapeDtypeStruct((L, SSM_W), F32), jax.ShapeDtypeStruct((SSM_G, CW, CW), F32),
         jax.ShapeDtypeStruct((SSM_G, CW, 2 * SSM_P), F32), jax.ShapeDtypeStruct((SSM_G, CW, 2 * SSM_P), F32),
         jax.ShapeDtypeStruct((SSM_G, 1, 2 * SSM_P), F32), jax.ShapeDtypeStruct((1, SSM_W), F32)],
        (d_yg, y, proj, hx, perm, mt, scat, ocat, a16, d_skip.reshape(1, SSM_W)),
        [pltpu.VMEM((HALF * 128, GB * 128), BF16), pltpu.SemaphoreType.DMA], ride)


def _merge(og, yg, gpre, proj, b_glu, wa, ws):
    L = og.shape[0]
    tm = _tile(L, 256)

    def body(og_ref, yg_ref, gp_ref, z0_ref, z1_ref, b_ref, wa_ref, ws_ref, m_ref):
        zs = jnp.concatenate([z0_ref[...], z1_ref[...]], axis=1)
        os_ = yg_ref[...] * _sigmoid(gp_ref[...] + b_ref[...]) * _silu(zs)
        ogv = og_ref[...]
        ra = lax.rsqrt(jnp.mean(ogv * ogv, axis=-1, keepdims=True) + NORM_EPS)
        rs = lax.rsqrt(jnp.mean(os_ * os_, axis=-1, keepdims=True) + NORM_EPS)
        m_ref[:, :ATTN_W] = (ogv * ra * wa_ref[...]).astype(BF16)
        m_ref[:, ATTN_W:] = (os_ * rs * ws_ref[...]).astype(BF16)

    row = lambda w: pl.BlockSpec((1, w), lambda i: (0, 0))
    return pl.pallas_call(
        body,
        name="merge",
        grid=(L // tm,),
        in_specs=[pl.BlockSpec((tm, ATTN_W), lambda i: (i, 0)), pl.BlockSpec((tm, SSM_W), lambda i: (i, 0)),
                  pl.BlockSpec((tm, SSM_W), lambda i: (i, 0)),
                  pl.BlockSpec((tm, 512), lambda i: (i, 7)), pl.BlockSpec((tm, 512), lambda i: (i, 8)),
                  row(SSM_W), row(ATTN_W), row(SSM_W)],
        out_specs=pl.BlockSpec((tm, D_MODEL), lambda i: (i, 0)),
        out_shape=jax.ShapeDtypeStruct((L, D_MODEL), BF16),
        compiler_params=_cp(("parallel",)),
    )(og, yg, gpre, proj, proj, b_glu.reshape(1, SSM_W), wa.reshape(1, ATTN_W), ws.reshape(1, SSM_W))


def _outproj_loss(merged, w_out, x, target):
    L = x.shape[0]
    tm, tn = _tile(L, 512), 1024
    ni, nj = L // tm, D_MODEL // tn

    def body(m_ref, w_ref, x_ref, t_ref, d_ref, db_ref, l_ref):
        out = x_ref[...] + jnp.dot(m_ref[...], w_ref[...], preferred_element_type=F32)
        diff = out - t_ref[...]
        d = diff * (1.0 / D_MODEL)
        d_ref[...] = d
        db_ref[...] = d.astype(BF16)
        l_ref[...] = jnp.full((1, 8, 128), jnp.sum(diff * diff), F32)

    return pl.pallas_call(
        body,
        name="outproj_loss",
        grid=(nj, ni),
        in_specs=[pl.BlockSpec((tm, D_MODEL), lambda j, i: (i, 0)),
                  pl.BlockSpec((D_MODEL, tn), lambda j, i: (0, j)),
                  pl.BlockSpec((tm, tn), lambda j, i: (i, j)),
                  pl.BlockSpec((tm, tn), lambda j, i: (i, j))],
        out_specs=[pl.BlockSpec((tm, tn), lambda j, i: (i, j)), pl.BlockSpec((tm, tn), lambda j, i: (i, j)),
                   pl.BlockSpec((1, 8, 128), lambda j, i: (i * nj + j, 0, 0))],
        out_shape=[jax.ShapeDtypeStruct((L, D_MODEL), F32), jax.ShapeDtypeStruct((L, D_MODEL), BF16),
                   jax.ShapeDtypeStruct((ni * nj, 8, 128), F32)],
        compiler_params=_cp(("parallel", "parallel")),
    )(merged, w_out, x, target)


def _merge_bwd(d_out_b, w_out, og, o, yg, gpre, proj, b_glu, wa, ws):
    L = og.shape[0]
    tm = _tile(L, 256)

    def body(dout_ref, wo_ref, og_ref, o_ref, yg_ref, gp_ref, za0_ref, za1_ref, zs0_ref, zs1_ref, b_ref, wa_ref,
             ws_ref, do_ref, dza_ref, dzs_ref, dg_ref, dyg_ref, gwa_ref, gws_ref, gb_ref):
        i = pl.program_id(0)

        @pl.when(i == 0)
        def _():
            gwa_ref[...] = jnp.zeros_like(gwa_ref)
            gws_ref[...] = jnp.zeros_like(gws_ref)
            gb_ref[...] = jnp.zeros_like(gb_ref)

        dm = lax.dot_general(dout_ref[...], wo_ref[...], _NT, preferred_element_type=F32)
        za = jnp.concatenate([za0_ref[...], za1_ref[...]], axis=1)
        zs = jnp.concatenate([zs0_ref[...], zs1_ref[...]], axis=1)
        ogv, dma = og_ref[...], dm[:, :ATTN_W]
        ra = lax.rsqrt(jnp.mean(ogv * ogv, axis=-1, keepdims=True) + NORM_EPS)
        xh = ogv * ra
        gwa_ref[...] += jnp.sum(dma * xh, axis=0, keepdims=True)
        gx = dma * wa_ref[...]
        d_og = ra * (gx - xh * jnp.mean(gx * xh, axis=-1, keepdims=True))
        do_ref[...] = d_og * _silu(za)
        dza_ref[...] = (d_og * o_ref[...] * _dsilu(za)).astype(BF16)
        ygv = yg_ref[...]
        sg = _sigmoid(gp_ref[...] + b_ref[...])
        y2 = ygv * sg
        sz = _silu(zs)
        os_ = y2 * sz
        dms = dm[:, ATTN_W:]
        rs = lax.rsqrt(jnp.mean(os_ * os_, axis=-1, keepdims=True) + NORM_EPS)
        xs = os_ * rs
        gws_ref[...] += jnp.sum(dms * xs, axis=0, keepdims=True)
        gxs = dms * ws_ref[...]
        d_os = rs * (gxs - xs * jnp.mean(gxs * xs, axis=-1, keepdims=True))
        dzs_ref[...] = (d_os * y2 * _dsilu(zs)).astype(BF16)
        d_y2 = d_os * sz
        d_g = d_y2 * ygv * sg * (1.0 - sg)
        dg_ref[...] = d_g.astype(BF16)
        gb_ref[...] += jnp.sum(d_g, axis=0, keepdims=True)
        dyg_ref[...] = d_y2 * sg

    row = lambda w: pl.BlockSpec((1, w), lambda i: (0, 0))
    full = lambda w: pl.BlockSpec((tm, w), lambda i: (i, 0))
    half = lambda c: pl.BlockSpec((tm, 512), lambda i: (i, c))
    return pl.pallas_call(
        body,
        name="merge_bwd",
        grid=(L // tm,),
        in_specs=[full(D_MODEL), pl.BlockSpec((D_MODEL, D_MODEL), lambda i: (0, 0)),
                  full(ATTN_W), full(ATTN_W), full(SSM_W), full(SSM_W),
                  half(3), half(4), half(7), half(8), row(SSM_W), row(ATTN_W), row(SSM_W)],
        out_specs=[full(ATTN_W), full(ATTN_W), full(SSM_W), full(SSM_W), full(SSM_W),
                   row(ATTN_W), row(SSM_W), row(SSM_W)],
        out_shape=[jax.ShapeDtypeStruct((L, ATTN_W), F32), jax.ShapeDtypeStruct((L, ATTN_W), BF16),
                   jax.ShapeDtypeStruct((L, SSM_W), BF16), jax.ShapeDtypeStruct((L, SSM_W), BF16),
                   jax.ShapeDtypeStruct((L, SSM_W), F32),
                   jax.ShapeDtypeStruct((1, ATTN_W), F32), jax.ShapeDtypeStruct((1, SSM_W), F32),
                   jax.ShapeDtypeStruct((1, SSM_W), F32)],
        compiler_params=_cp(("arbitrary",)),
    )(d_out_b, w_out, og, o, yg, gpre, proj, proj, proj, proj, b_glu.reshape(1, SSM_W), wa.reshape(1, ATTN_W),
      ws.reshape(1, SSM_W))


def _rms_bwd_x(x, norm_w, d_hn, d_out, ride):
    L = x.shape[0]
    tm = _tile(L, 256)

    def body(x_ref, w_ref, dh_ref, do_ref, gx_ref, gw_ref):
        i = pl.program_id(0)

        @pl.when(i == 0)
        def _():
            gw_ref[...] = jnp.zeros_like(gw_ref)

        xv, dh = x_ref[...], dh_ref[...]
        r = lax.rsqrt(jnp.mean(xv * xv, axis=-1, keepdims=True) + NORM_EPS)
        xh = xv * r
        gw_ref[...] += jnp.sum(dh * xh, axis=0, keepdims=True)
        gx = dh * w_ref[...]
        gx_ref[...] = do_ref[...] + r * (gx - xh * jnp.mean(gx * xh, axis=-1, keepdims=True))

    blk = pl.BlockSpec((tm, D_MODEL), lambda i: (i, 0))
    row = pl.BlockSpec((1, D_MODEL), lambda i: (0, 0))
    return _call(body, "rms_bwd_x", (L // tm,), [blk, row, blk, blk], [blk, row],
                 [jax.ShapeDtypeStruct((L, D_MODEL), F32), jax.ShapeDtypeStruct((1, D_MODEL), F32)],
                 (x, norm_w.reshape(1, D_MODEL), d_hn, d_out), ride=ride)


def _rope_table(positions):
    inv_freq = ROPE_THETA ** (-jnp.arange(0, HEAD_DIM, 2, dtype=F32) / HEAD_DIM)
    ang = positions.astype(F32)[:, None] * inv_freq
    c, s = jnp.cos(ang), jnp.sin(ang)
    return jnp.concatenate([c, c, c, c, -s, s, -s, s], axis=1)


def _step(x, positions, target, w, core, chip):
    small = {n: w[n] for n in _SMALL}
    tab = _rope_table(positions)
    mt_b, scat_b, ocat_b, a16 = _ssm_prep(small)
    perm = _chunk_perm()
    blocks = lambda t: t.reshape(N_DEV, t.shape[0] // N_DEV, t.shape[1])

    proj, hn, wt_in = _rms_inproj_gather(x, small["norm_w"], w["w_in"].T.astype(BF16), chip)
    wt_in = wt_in.reshape(IN_W, D_MODEL)
    q_rot, k_rot = _qk_prep(proj, tab, small["q_norm_w"], small["k_norm_w"])
    (og, o, lse), (w_glu,) = _attn_fwd(q_rot, k_rot, proj, small["sinks"],
                                       _gather_exchange([w["w_glu"].astype(BF16)]))
    (y, yg, hx), (w_out,) = _ssm_fwd(proj, perm, mt_b, scat_b, ocat_b, a16, small["d_skip"],
                                     _gather_exchange([w["w_out"].astype(BF16)]))
    w_glu, w_out = w_glu.reshape(SSM_W, SSM_W), w_out.reshape(D_MODEL, D_MODEL)
    gpre = _mm(yg, w_glu, "nn", F32, "glu_fwd")
    merged = _merge(og, yg, gpre, proj, small["b_glu"], small["attn_out_norm_w"], small["ssm_out_norm_w"])
    d_out, d_out_b, loss_parts = _outproj_loss(merged, w_out, x, target)
    loss = 0.5 * jnp.sum(loss_parts[:, 0, 0]) / D_MODEL

    g_w_out = blocks(_mm(merged, d_out_b, "tn", F32, "grad_w_out"))
    d_o, d_za, d_zs, d_g, d_yg1, g_wa, g_ws, g_bglu = _merge_bwd(
        d_out_b, w_out, og, o, yg, gpre, proj, small["b_glu"], small["attn_out_norm_w"], small["ssm_out_norm_w"])
    g_w_glu = blocks(_mm(yg, d_g, "tn", F32, "grad_w_glu"))
    d_yg = _mm(d_g, w_glu, "nt", F32, "d_yg", add=d_yg1)
    (d_u, g_mt, g_scat, g_ocat, g_a16, g_dskip), (ra_out, ra_glu) = _ssm_bwd(
        d_yg, y, proj, hx, perm, mt_b, scat_b, ocat_b, a16, small["d_skip"], _pair_exchange([g_w_out, g_w_glu]))
    p_out = _pair_sum(g_w_out, ra_out, core, BF16, "pair_sum_out")
    p_glu = _pair_sum(g_w_glu, ra_glu, core, BF16, "pair_sum_glu")
    (d_q, d_k, d_v, g_sinks), (rb_out, rb_glu) = _attn_bwd(
        q_rot, k_rot, proj, small["sinks"], d_o, o, lse, _chip_exchange([p_out, p_glu]))
    d_proj, g_qw, g_kw = _qk_prep_bwd(proj, tab, small["q_norm_w"], small["k_norm_w"], d_q, d_k, d_v,
                                      d_za, d_u, d_zs)
    g_qw = g_qw[0, :HEAD_DIM] + g_qw[0, HEAD_DIM:]
    g_kw = g_kw[0, :HEAD_DIM] + g_kw[0, HEAD_DIM:]
    g_in_a = blocks(_mm(d_proj, hn, "tn", F32, "grad_w_in_a", panel=0))
    g_in_b, (ra_a,) = _mm(d_proj, hn, "tn", F32, "grad_w_in_b", panel=1, ride=_pair_exchange([g_in_a]))
    g_in_b = blocks(g_in_b)
    p_a = _pair_sum(g_in_a, ra_a, core, BF16, "pair_sum_in_a")
    d_hn, (rb_a, ra_b) = _mm(d_proj, wt_in, "nn", F32, "d_hn",
                             ride=_both(_chip_exchange([p_a]), _pair_exchange([g_in_b])))
    p_b = _pair_sum(g_in_b, ra_b, core, BF16, "pair_sum_in_b")
    g_small, (rb_b,) = _ssm_prep_bwd(small, g_mt, g_scat, g_ocat, g_a16, _chip_exchange([p_b]))
    (grad_x, g_nw), _ = _rms_bwd_x(x, small["norm_w"], d_hn, d_out, None)
    g_wt_in = jnp.concatenate([_chip_sum(p_a, rb_a, chip, "chip_sum_in_a"),
                               _chip_sum(p_b, rb_b, chip, "chip_sum_in_b")], axis=1)

    g_small.update(norm_w=g_nw.reshape(-1), q_norm_w=g_qw.reshape(-1), k_norm_w=g_kw.reshape(-1),
                   sinks=g_sinks[0, :N_HEADS], d_skip=g_dskip.reshape(-1), b_glu=g_bglu.reshape(-1),
                   attn_out_norm_w=g_wa.reshape(-1), ssm_out_norm_w=g_ws.reshape(-1))
    slab = _pack(g_small, loss).reshape(N_DEV, _PACK_ROWS // N_DEV, 128)
    (ra_s,) = _run_exchange(_pair_exchange([slab]), "pair_exchange_small")
    p_s = _pair_sum(slab, ra_s, core, F32, "pair_sum_small")
    (rb_s,) = _run_exchange(_chip_exchange([p_s]), "chip_exchange_small")
    (g_packed,) = _run_exchange(_gather_exchange([_chip_sum(p_s, rb_s, chip, "chip_sum_small")]), "gather_small")

    g_packed = g_packed.reshape(_PACK_ROWS, 128)
    grads = _unpack(g_packed, w)
    grads.update(w_in=g_wt_in.T,
                 w_glu=_chip_sum(p_glu, rb_glu, chip, "chip_sum_glu"),
                 w_out=_chip_sum(p_out, rb_out, chip, "chip_sum_out"))
    return g_packed[_LOSS_ROW, 0], grad_x, grads


_ANY = pl.BlockSpec(memory_space=pl.ANY)


class _Exchange:
    def __init__(self, arrays, out_shape, sems, start, finish, relay=None):
        self.arrays, self.out_shape, self.sems, self.start, self.finish = arrays, out_shape, sems, start, finish
        self.relay = relay if relay is not None else (lambda ins, outs, sems: None)


def _gather_exchange(blocks):
    n = len(blocks)

    def parts(ins, outs, sems):
        send_sems, recv_sems, local_sems = sems
        x, y, c = lax.axis_index("x"), lax.axis_index("y"), lax.axis_index("c")
        me, sibling = (x, y, c), (x, y, 1 - c)
        chips = [(1 - x, y), (x, 1 - y), (1 - x, 1 - y)]

        def slot(k, dev):
            return outs[k].at[4 * dev[0] + 2 * dev[1] + dev[2]]

        def copy(k, q, block, to, src=None):
            return pltpu.make_async_remote_copy(
                src_ref=slot(k, block) if src is None else src, dst_ref=slot(k, block),
                send_sem=send_sems.at[k, q], recv_sem=recv_sems.at[k, q], device_id=to, device_id_type=MESH)

        mine = [pltpu.make_async_copy(ins[k], slot(k, me), local_sems.at[k]) for k in range(n)]
        first = []
        for k in range(n):
            first.append(copy(k, 0, me, sibling, src=ins[k]))
            first += [copy(k, 1 + j, me, (*chip, c), src=ins[k]) for j, chip in enumerate(chips)]
        return me, sibling, chips, c, copy, mine, first

    def start(ins, outs, sems):
        *_, mine, first = parts(ins, outs, sems)
        for cp in mine + first:
            cp.start()

    def relay(ins, outs, sems):
        me, sibling, chips, c, copy, _, _ = parts(ins, outs, sems)
        for j, chip in enumerate(chips):
            for k in range(n):
                copy(k, 1 + j, (*chip, c), me).wait_recv()
                copy(k, 4 + j, (*chip, c), sibling).start()

    def finish(ins, outs, sems):
        me, sibling, chips, c, copy, mine, first = parts(ins, outs, sems)
        for k in range(n):
            copy(k, 0, sibling, me).wait_recv()
            for j, chip in enumerate(chips):
                copy(k, 4 + j, (*chip, 1 - c), me).wait_recv()
        for cp in first + [copy(k, 4 + j, (*chip, c), sibling) for k in range(n) for j, chip in enumerate(chips)]:
            cp.wait_send()
        for cp in mine:
            cp.wait()

    return _Exchange(blocks, [jax.ShapeDtypeStruct((N_DEV,) + b.shape, b.dtype) for b in blocks],
                     [pltpu.SemaphoreType.DMA((n, 7)), pltpu.SemaphoreType.DMA((n, 7)), pltpu.SemaphoreType.DMA((n,))],
                     start, finish, relay)


def _direct_exchange(arrays, out_lead, fan, route):
    n = len(arrays)

    def copies(ins, outs, sems):
        send_sems, recv_sems = sems
        legs = route(lax.axis_index("x"), lax.axis_index("y"), lax.axis_index("c"))
        return [pltpu.make_async_remote_copy(
            src_ref=ins[k].at[src], dst_ref=outs[k].at[q], send_sem=send_sems.at[k, q], recv_sem=recv_sems.at[k, q],
            device_id=to, device_id_type=MESH) for k in range(n) for src, q, to in legs]

    def start(ins, outs, sems):
        for cp in copies(ins, outs, sems):
            cp.start()

    def finish(ins, outs, sems):
        for cp in copies(ins, outs, sems):
            cp.wait()

    return _Exchange(arrays, [jax.ShapeDtypeStruct((out_lead,) + a.shape[1:], a.dtype) for a in arrays],
                     [pltpu.SemaphoreType.DMA((n, fan)), pltpu.SemaphoreType.DMA((n, fan))], start, finish)


def _pair_exchange(grads):
    return _direct_exchange(grads, 4, 4, lambda x, y, c: [(2 * chip + (1 - c), chip, (x, y, 1 - c))
                                                          for chip in range(4)])


def _chip_exchange(parts):
    def route(x, y, c):
        chips = [(1 - x, y), (x, 1 - y), (1 - x, 1 - y)]
        return [(2 * chip[0] + chip[1], q, (*chip, c)) for q, chip in enumerate(chips)]
    return _direct_exchange(parts, 3, 3, route)


def _both(ex1, ex2):
    n1, s1 = len(ex1.arrays), len(ex1.sems)

    def halves(ins, outs, sems):
        return (ins[:n1], outs[:n1], sems[:s1]), (ins[n1:], outs[n1:], sems[s1:])

    def start(ins, outs, sems):
        h1, h2 = halves(ins, outs, sems)
        ex1.start(*h1)
        ex2.start(*h2)

    def relay(ins, outs, sems):
        h1, h2 = halves(ins, outs, sems)
        ex1.relay(*h1)
        ex2.relay(*h2)

    def finish(ins, outs, sems):
        h1, h2 = halves(ins, outs, sems)
        ex1.finish(*h1)
        ex2.finish(*h2)

    return _Exchange(list(ex1.arrays) + list(ex2.arrays), list(ex1.out_shape) + list(ex2.out_shape),
                     list(ex1.sems) + list(ex2.sems), start, finish, relay)


def _run_exchange(ex, name):
    n = len(ex.arrays)

    def body(*refs):
        ins, outs, sems = refs[:n], refs[n:2 * n], refs[2 * n:]
        ex.start(ins, outs, sems)
        ex.relay(ins, outs, sems)
        ex.finish(ins, outs, sems)

    return list(pl.pallas_call(body, name=name, in_specs=[_ANY] * n, out_specs=[_ANY] * n, out_shape=ex.out_shape,
                               scratch_shapes=ex.sems)(*ex.arrays))


def _call(body, name, grid, in_specs, out_specs, out_shape, args, scratch_shapes=(), ride=None):
    if ride is None:
        sem = ("arbitrary",) * len(grid)
        return pl.pallas_call(body, name=name, grid=grid, in_specs=in_specs, out_specs=out_specs, out_shape=out_shape,
                              scratch_shapes=list(scratch_shapes), compiler_params=_cp(sem))(*args), None
    n_in, n_out, n_scr, n_x = len(in_specs), len(out_specs), len(scratch_shapes), len(ride.arrays)

    def wrapped(*refs):
        ins, refs = refs[:n_in], refs[n_in:]
        x_in, refs = refs[:n_x], refs[n_x:]
        outs, refs = refs[:n_out], refs[n_out:]
        x_out, refs = refs[:n_x], refs[n_x:]
        scr, sems = refs[:n_scr], refs[n_scr:]
        step, total = pl.program_id(0), grid[0]
        for a in range(1, len(grid)):
            step, total = step * grid[a] + pl.program_id(a), total * grid[a]
        @pl.when(step == 0)
        def _():
            ride.start(x_in, x_out, sems)

        @pl.when(step == max(total - 2, 0))
        def _():
            ride.relay(x_in, x_out, sems)

        body(*ins, *outs, *scr)

        @pl.when(step == total - 1)
        def _():
            ride.finish(x_in, x_out, sems)

    res = pl.pallas_call(
        wrapped, name=name, grid=grid, in_specs=list(in_specs) + [_ANY] * n_x,
        out_specs=list(out_specs) + [_ANY] * n_x, out_shape=list(out_shape) + list(ride.out_shape),
        scratch_shapes=list(scratch_shapes) + list(ride.sems),
        compiler_params=_cp(("arbitrary",) * len(grid)))(*args, *ride.arrays)
    return res[:n_out], list(res[n_out:])


def _pair_sum(g, ra, core, out_dtype, name):
    _, r, C = g.shape
    tr = _tile(r, 576)

    def body(c_ref, g_ref, ra_ref, p_ref):
        p_ref[...] = (g_ref[...] + ra_ref[...]).astype(p_ref.dtype)

    return pl.pallas_call(
        body,
        name=name,
        grid_spec=pltpu.PrefetchScalarGridSpec(
            num_scalar_prefetch=1,
            grid=(4, r // tr),
            in_specs=[pl.BlockSpec((1, tr, C), lambda j, t, c_ref: (2 * j + c_ref[0], t, 0)),
                      pl.BlockSpec((1, tr, C), lambda j, t, c_ref: (j, t, 0))],
            out_specs=pl.BlockSpec((1, tr, C), lambda j, t, c_ref: (j, t, 0)),
        ),
        out_shape=jax.ShapeDtypeStruct((4, r, C), out_dtype),
        compiler_params=_cp(("parallel", "parallel")),
    )(core, g, ra)


def _chip_sum(p, rb, chip, name):
    _, r, C = p.shape
    tr = _tile(r, 576)

    def body(c_ref, p_ref, rb_ref, o_ref):
        acc = p_ref[0].astype(F32) + rb_ref[0].astype(F32)
        acc = acc + rb_ref[1].astype(F32)
        o_ref[...] = acc + rb_ref[2].astype(F32)

    return pl.pallas_call(
        body,
        name=name,
        grid_spec=pltpu.PrefetchScalarGridSpec(
            num_scalar_prefetch=1,
            grid=(r // tr,),
            in_specs=[pl.BlockSpec((1, tr, C), lambda t, c_ref: (c_ref[0], t, 0)),
                      pl.BlockSpec((3, tr, C), lambda t, c_ref: (0, t, 0))],
            out_specs=pl.BlockSpec((tr, C), lambda t, c_ref: (t, 0)),
        ),
        out_shape=jax.ShapeDtypeStruct((r, C), F32),
        compiler_params=_cp(("parallel",)),
    )(chip, p, rb)


def _adamw(g, w, m, v, name):
    R, C = g.shape
    tr = _tile(R, 256)
    c1 = 1.0 - ADAM_B1 ** ADAM_STEP
    c2 = 1.0 - ADAM_B2 ** ADAM_STEP

    def body(g_ref, w_ref, m_ref, v_ref, d_ref, nm_ref, nv_ref):
        gv = g_ref[...]
        nm = ADAM_B1 * m_ref[...] + (1.0 - ADAM_B1) * gv
        nv = ADAM_B2 * v_ref[...] + (1.0 - ADAM_B2) * (gv * gv)
        nm_ref[...] = nm
        nv_ref[...] = nv
        d_ref[...] = -ADAM_LR * ((nm / c1) / (jnp.sqrt(nv / c2) + ADAM_EPS) + ADAM_WD * w_ref[...])

    blk = pl.BlockSpec((tr, C), lambda i: (i, 0))
    return pl.pallas_call(
        body, name=name, grid=(R // tr,), in_specs=[blk] * 4, out_specs=[blk] * 3,
        out_shape=[jax.ShapeDtypeStruct((R, C), F32)] * 3, compiler_params=_cp(("parallel",)),
    )(g, w, m, v)


_SMALL = ("norm_w", "q_norm_w", "k_norm_w", "sinks", "a_re", "a_im", "log_step", "b_re", "b_im", "c_re", "c_im",
          "d_skip", "b_glu", "attn_out_norm_w", "ssm_out_norm_w")
_WEIGHTS = ("norm_w", "w_in", "q_norm_w", "k_norm_w", "sinks", "a_re", "a_im", "log_step", "b_re", "b_im", "c_re",
            "c_im", "d_skip", "w_glu", "b_glu", "attn_out_norm_w", "ssm_out_norm_w", "w_out")
_SMALL_2D = dict(norm_w=(1, 2048), q_norm_w=(1, 64), k_norm_w=(1, 64), sinks=(1, 16), a_re=(64, 64), a_im=(64, 64),
                 log_step=(1, 64), b_re=(4096, 16), b_im=(4096, 16), c_re=(1024, 64), c_im=(1024, 64),
                 d_skip=(1, 1024), b_glu=(1, 1024), attn_out_norm_w=(1, 1024), ssm_out_norm_w=(1, 1024))


def _slab_rows(n):
    return -(-n // 1024) * 8


_PACK_ROWS = 2304


_LOSS_ROW = 2192


def _pack(d, loss):
    parts = []
    for n in _SMALL:
        flat = d[n].reshape(-1).astype(F32)
        rows = _slab_rows(flat.shape[0])
        parts.append(jnp.pad(flat, (0, rows * 128 - flat.shape[0])).reshape(rows, 128))
    assert sum(p.shape[0] for p in parts) == _LOSS_ROW
    parts.append(jnp.pad(loss.reshape(1, 1), ((0, _PACK_ROWS - _LOSS_ROW - 1), (0, 127))))
    return jnp.concatenate(parts, axis=0)


def _unpack(packed, like):
    out, off = {}, 0
    for n in _SMALL:
        size = math.prod(like[n].shape)
        rows = _slab_rows(size)
        out[n] = packed[off:off + rows].reshape(-1)[:size].reshape(like[n].shape)
        off += rows
    return out


def _adamw_small(g, w, m, v):
    c1 = 1.0 - ADAM_B1 ** ADAM_STEP
    c2 = 1.0 - ADAM_B2 ** ADAM_STEP
    k = len(_SMALL)

    def body(*refs):
        ins, outs = refs[:4 * k], refs[4 * k:]
        for j in range(k):
            gv, wv, mv, vv = (ins[q * k + j][...] for q in range(4))
            nm = ADAM_B1 * mv + (1.0 - ADAM_B1) * gv
            nv = ADAM_B2 * vv + (1.0 - ADAM_B2) * (gv * gv)
            outs[j][...] = -ADAM_LR * ((nm / c1) / (jnp.sqrt(nv / c2) + ADAM_EPS) + ADAM_WD * wv)
            outs[k + j][...] = nm
            outs[2 * k + j][...] = nv

    args = [d[n].reshape(_SMALL_2D[n]) for d in (g, w, m, v) for n in _SMALL]
    shapes = [jax.ShapeDtypeStruct(_SMALL_2D[n], F32) for _ in range(3) for n in _SMALL]
    outs = pl.pallas_call(body, name="adamw_small", out_shape=shapes, compiler_params=_cp())(*args)
    res = []
    for q in range(3):
        res.append({n: outs[q * k + j].reshape(w[n].shape) for j, n in enumerate(_SMALL)})
    return res


def kernel(x, positions, norm_w, w_in, q_norm_w, k_norm_w, sinks, a_re, a_im, log_step, b_re, b_im, c_re, c_im, d_skip, w_glu, b_glu, attn_out_norm_w, ssm_out_norm_w, w_out, loss_target, m_norm_w, m_w_in, m_q_norm_w, m_k_norm_w, m_sinks, m_a_re, m_a_im, m_log_step, m_b_re, m_b_im, m_c_re, m_c_im, m_d_skip, m_w_glu, m_b_glu, m_attn_out_norm_w, m_ssm_out_norm_w, m_w_out, v_norm_w, v_w_in, v_q_norm_w, v_k_norm_w, v_sinks, v_a_re, v_a_im, v_log_step, v_b_re, v_b_im, v_c_re, v_c_im, v_d_skip, v_w_glu, v_b_glu, v_attn_out_norm_w, v_ssm_out_norm_w, v_w_out):
    w = dict(norm_w=norm_w, w_in=w_in, q_norm_w=q_norm_w, k_norm_w=k_norm_w, sinks=sinks, a_re=a_re, a_im=a_im,
             log_step=log_step, b_re=b_re, b_im=b_im, c_re=c_re, c_im=c_im, d_skip=d_skip, w_glu=w_glu, b_glu=b_glu,
             attn_out_norm_w=attn_out_norm_w, ssm_out_norm_w=ssm_out_norm_w, w_out=w_out)
    m = dict(norm_w=m_norm_w, w_in=m_w_in, q_norm_w=m_q_norm_w, k_norm_w=m_k_norm_w, sinks=m_sinks, a_re=m_a_re,
             a_im=m_a_im, log_step=m_log_step, b_re=m_b_re, b_im=m_b_im, c_re=m_c_re, c_im=m_c_im, d_skip=m_d_skip,
             w_glu=m_w_glu, b_glu=m_b_glu, attn_out_norm_w=m_attn_out_norm_w, ssm_out_norm_w=m_ssm_out_norm_w,
             w_out=m_w_out)
    v = dict(norm_w=v_norm_w, w_in=v_w_in, q_norm_w=v_q_norm_w, k_norm_w=v_k_norm_w, sinks=v_sinks, a_re=v_a_re,
             a_im=v_a_im, log_step=v_log_step, b_re=v_b_re, b_im=v_b_im, c_re=v_c_re, c_im=v_c_im, d_skip=v_d_skip,
             w_glu=v_w_glu, b_glu=v_b_glu, attn_out_norm_w=v_attn_out_norm_w, ssm_out_norm_w=v_ssm_out_norm_w,
             w_out=v_w_out)
    core = lax.axis_index("c").astype(jnp.int32).reshape(1)
    chip = (2 * lax.axis_index("x") + lax.axis_index("y")).astype(jnp.int32).reshape(1)

    loss, grad_x, grads = _step(x[0], positions[0], loss_target[0], w, core, chip)
    delta, new_m, new_v = {}, {}, {}
    for n in ("w_in", "w_glu", "w_out"):
        delta[n], new_m[n], new_v[n] = _adamw(grads[n], w[n], m[n], v[n], f"adamw_{n}")
    d_s, m_s, v_s = _adamw_small(grads, w, m, v)
    delta.update(d_s)
    new_m.update(m_s)
    new_v.update(v_s)

    return (loss, grad_x[None], *[grads[n] for n in _WEIGHTS], *[delta[n] for n in _WEIGHTS],
            *[new_m[n] for n in _WEIGHTS], *[new_v[n] for n in _WEIGHTS])
```

```python
import functools
import math

import jax
import jax.numpy as jnp
from jax import lax
from jax.experimental import pallas as pl
from jax.experimental.pallas import tpu as pltpu

F32 = jnp.float32
BF16 = jnp.bfloat16

D_MODEL = 2048
ATTN_W = 1024
KV_W = 256
SSM_W = 1024
HEAD_DIM = 64
N_HEADS = 16
N_KV = 4
KV_REP = 4
IN_W = 4608
BLOCK = 128
ROPE_THETA = 10000.0
NORM_EPS = 1e-6
SSM_G = 64
SSM_P = 64
SSM_H = 16
CHUNK = 16
CW = CHUNK * SSM_H
N_DEV = 8

ADAM_LR = 0.001
ADAM_B1 = 0.9
ADAM_B2 = 0.999
ADAM_EPS = 1e-08
ADAM_WD = 0.01
ADAM_STEP = 10

VMEM_LIMIT = 56 * 1024 * 1024
MESH = pl.DeviceIdType.MESH


def _cp(sem=None):
    if sem is None:
        return pltpu.CompilerParams(vmem_limit_bytes=VMEM_LIMIT)
    return pltpu.CompilerParams(vmem_limit_bytes=VMEM_LIMIT, dimension_semantics=sem)


def _sigmoid(x):
    return 0.5 * jnp.tanh(0.5 * x) + 0.5


def _silu(x):
    return x * _sigmoid(x)


def _dsilu(x):
    s = _sigmoid(x)
    return s * (1.0 + x * (1.0 - s))


_GELU_C = math.sqrt(2.0 / math.pi)


def _gelu(y):
    t = jnp.tanh(_GELU_C * (y + 0.044715 * y * y * y))
    return 0.5 * y * (1.0 + t)


def _dgelu(y):
    t = jnp.tanh(_GELU_C * (y + 0.044715 * y * y * y))
    return 0.5 * (1.0 + t) + 0.5 * y * (1.0 - t * t) * _GELU_C * (1.0 + 3.0 * 0.044715 * y * y)


def _tile(n, want):
    if n <= want:
        return n
    for t in range(want - want % 16, 0, -16):
        if n % t == 0:
            return t
    raise ValueError((n, want))


def _mm(a, b, mode, out_dtype, name, tm=512, tn=1024, add=None, ride=None, panel=None):
    if mode == "nn":
        (M, K), (K2, N) = a.shape, b.shape
    elif mode == "nt":
        (M, K), (N, K2) = a.shape, b.shape
    else:
        (K, M), (K2, N) = a.shape, b.shape
    assert K == K2
    tm, tn = _tile(M, tm), _tile(N, tn)
    p0 = 0
    if panel is not None:
        assert mode != "nt" and add is None
        p0, N = panel, tn
    dn = {"nn": _NN, "nt": _NT, "tn": _TN}[mode]

    def body(a_ref, b_ref, *rest):
        o_ref = rest[-1]
        acc = lax.dot_general(a_ref[...].astype(BF16), b_ref[...].astype(BF16), dn, preferred_element_type=F32)
        if add is not None:
            acc = acc + rest[0][...]
        o_ref[...] = acc.astype(o_ref.dtype)

    a_spec = pl.BlockSpec((K, tm), lambda j, i: (0, i)) if mode == "tn" else pl.BlockSpec((tm, K), lambda j, i: (i, 0))
    b_spec = (pl.BlockSpec((tn, K), lambda j, i: (j, 0)) if mode == "nt"
              else pl.BlockSpec((K, tn), lambda j, i: (0, j + p0)))
    o_spec = pl.BlockSpec((tm, tn), lambda j, i: (i, j))
    extra = () if add is None else (add,)
    if ride is not None:
        (out,), landed = _call(body, name, (N // tn, M // tm), [a_spec, b_spec] + [o_spec] * len(extra), [o_spec],
                               [jax.ShapeDtypeStruct((M, N), out_dtype)], (a, b, *extra), ride=ride)
        return out, landed
    return pl.pallas_call(
        body,
        name=name,
        grid=(N // tn, M // tm),
        in_specs=[a_spec, b_spec] + [o_spec] * len(extra),
        out_specs=o_spec,
        out_shape=jax.ShapeDtypeStruct((M, N), out_dtype),
        compiler_params=_cp(("parallel", "parallel")),
    )(a, b, *extra)


_CHIP_ORDER = (0, 2, 1, 3)


def _rms_inproj_gather(x, norm_w, wt_shard, chip):
    L = x.shape[0]
    tm = _tile(L, 512)
    ni = L // tm
    r = IN_W // N_DEV
    tn = 2 * r

    def body(chip_ref, x_ref, nw_ref, shard, proj_ref, hn_ref, wt_hbm, hn_scr, w_scr, send_sems, recv_sems, loc_sems):
        jc, i = pl.program_id(0), pl.program_id(1)
        xx, yy, c = lax.axis_index("x"), lax.axis_index("y"), lax.axis_index("c")
        me, sibling = (xx, yy, c), (xx, yy, 1 - c)
        chips = [(1 - xx, yy), (xx, 1 - yy), (1 - xx, 1 - yy)]

        def slot(dev):
            return wt_hbm.at[4 * dev[0] + 2 * dev[1] + dev[2]]

        def copy(q, block, to, src=None):
            return pltpu.make_async_remote_copy(
                src_ref=slot(block) if src is None else src, dst_ref=slot(block),
                send_sem=send_sems.at[q], recv_sem=recv_sems.at[q], device_id=to, device_id_type=MESH)

        def rows_of(buf, core):
            return w_scr.at[buf, pl.ds(pl.multiple_of(core * r, 16), r)]

        mine = pltpu.make_async_copy(shard, slot(me), loc_sems.at[0])
        sends = [copy(0, me, sibling, src=shard)] + [copy(1 + j, me, (*ch, c), src=shard) for j, ch in enumerate(chips)]
        first = jnp.logical_and(jc == 0, i == 0)

        @pl.when(first)
        def _():
            mine.start()
            for cp in sends[:3]:
                cp.start()
            own = pltpu.make_async_copy(shard, rows_of(0, c), loc_sems.at[1])
            own.start()
            copy(0, sibling, me).wait_recv()
            sib = pltpu.make_async_copy(slot(sibling), rows_of(0, 1 - c), loc_sems.at[2])
            sib.start()
            own.wait()
            sib.wait()

        def take_direct(j, ch):
            copy(1 + j, (*ch, c), me).wait_recv()
            copy(4 + j, (*ch, c), sibling).start()
            if j == 0:
                sends[1].wait_send()
                sends[2].wait_send()
                sends[3].start()
            pltpu.make_async_copy(slot((*ch, c)), rows_of((1 + j) % 2, c), loc_sems.at[1]).start()

        for j, ch in enumerate(chips):
            early = jnp.logical_and(jc == j, i == ni // 2) if j > 0 else jnp.logical_and(jc == 1, i == 0)

            @pl.when(early)
            def _(j=j, ch=ch):
                take_direct(j, ch)

            @pl.when(jnp.logical_and(jc == 1 + j, i == 0))
            def _(j=j, ch=ch):
                buf = (1 + j) % 2
                copy(4 + j, (*ch, 1 - c), me).wait_recv()
                passed = pltpu.make_async_copy(slot((*ch, 1 - c)), rows_of(buf, 1 - c), loc_sems.at[2])
                passed.start()
                pltpu.make_async_copy(slot((*ch, c)), rows_of(buf, c), loc_sems.at[1]).wait()
                passed.wait()

        rows = pl.ds(pl.multiple_of(i * tm, tm), tm)

        @pl.when(jc == 0)
        def _():
            xv = x_ref[...]
            rstd = lax.rsqrt(jnp.mean(xv * xv, axis=-1, keepdims=True) + NORM_EPS)
            hn = (xv * rstd * nw_ref[...]).astype(BF16)
            hn_scr[rows, :] = hn
            hn_ref[...] = hn

        for buf in range(2):
            @pl.when(jc % 2 == buf)
            def _(buf=buf):
                proj_ref[...] = lax.dot_general(hn_scr[rows, :], w_scr[buf], _NT, preferred_element_type=F32)

        @pl.when(jnp.logical_and(jc == 3, i == ni - 1))
        def _():
            sends[0].wait_send()
            sends[3].wait_send()
            for j, ch in enumerate(chips):
                copy(4 + j, (*ch, c), sibling).wait_send()
            mine.wait()

    def tile_of(jc, chip_ref):
        mask = jnp.where(jc == 1, _CHIP_ORDER[1], jnp.where(jc == 2, _CHIP_ORDER[2], jnp.where(jc == 3, _CHIP_ORDER[3], 0)))
        return jnp.bitwise_xor(chip_ref[0], mask)

    held = lambda jc, i: jnp.where(jc == 0, i, ni - 1)
    return pl.pallas_call(
        body,
        name="rms_inproj_gather",
        grid_spec=pltpu.PrefetchScalarGridSpec(
            num_scalar_prefetch=1,
            grid=(4, ni),
            in_specs=[pl.BlockSpec((tm, D_MODEL), lambda jc, i, ch: (held(jc, i), 0)),
                      pl.BlockSpec((1, D_MODEL), lambda jc, i, ch: (0, 0)), _ANY],
            out_specs=[pl.BlockSpec((tm, tn), lambda jc, i, ch: (i, tile_of(jc, ch))),
                       pl.BlockSpec((tm, D_MODEL), lambda jc, i, ch: (held(jc, i), 0)), _ANY],
            scratch_shapes=[pltpu.VMEM((L, D_MODEL), BF16), pltpu.VMEM((2, tn, D_MODEL), BF16),
                            pltpu.SemaphoreType.DMA((7,)), pltpu.SemaphoreType.DMA((7,)), pltpu.SemaphoreType.DMA((3,))],
        ),
        out_shape=[jax.ShapeDtypeStruct((L, IN_W), F32), jax.ShapeDtypeStruct((L, D_MODEL), BF16),
                   jax.ShapeDtypeStruct((N_DEV, r, D_MODEL), BF16)],
        compiler_params=_cp(("arbitrary", "arbitrary")),
    )(chip, x, norm_w.reshape(1, D_MODEL), wt_shard)


def _seg_sum(v):
    a = lax.broadcasted_iota(jnp.int32, (128, 128), 0) // HEAD_DIM
    b = lax.broadcasted_iota(jnp.int32, (128, 128), 1) // HEAD_DIM
    ones = jnp.where(a == b, 1.0, 0.0).astype(BF16)
    hi = v.astype(BF16)
    lo = (v - hi.astype(F32)).astype(BF16)
    return jnp.dot(hi, ones, preferred_element_type=F32) + jnp.dot(lo, ones, preferred_element_type=F32)


def _rot_half(t):
    lane = lax.broadcasted_iota(jnp.int32, t.shape, 1)
    return jnp.where(lane % HEAD_DIM < HEAD_DIM // 2, pltpu.roll(t, 128 - HEAD_DIM // 2, 1),
                     pltpu.roll(t, HEAD_DIM // 2, 1))


def _norm_rope(raw, w, cos, sin):
    r = lax.rsqrt(_seg_sum(raw * raw) * (1.0 / HEAD_DIM) + NORM_EPS)
    tn = raw * r * w
    return r, tn * cos + _rot_half(tn) * sin


def _norm_rope_bwd(d_rot, raw, w, cos, sin):
    r = lax.rsqrt(_seg_sum(raw * raw) * (1.0 / HEAD_DIM) + NORM_EPS)
    d_tn = d_rot * cos + _rot_half(d_rot * sin)
    xh = raw * r
    gw = d_tn * w
    d_raw = r * (gw - xh * (_seg_sum(gw * xh) * (1.0 / HEAD_DIM)))
    return d_raw, d_tn * xh


def _band_mask2(has_prev):
    qi = lax.broadcasted_iota(jnp.int32, (2 * BLOCK, 2 * BLOCK), 0) % BLOCK + BLOCK
    kj = lax.broadcasted_iota(jnp.int32, (2 * BLOCK, 2 * BLOCK), 1)
    rel = qi - kj
    return (rel >= 0) & (rel < BLOCK) & ((kj >= BLOCK) | has_prev)


def _half_tiles(pair):
    lo = lax.broadcasted_iota(jnp.int32, pair.shape, 1) < HEAD_DIM
    sw = pltpu.roll(pair, HEAD_DIM, 1)
    z = jnp.zeros_like(pair)
    return (jnp.where(lo, pair, z).astype(BF16), jnp.where(lo, z, sw).astype(BF16),
            jnp.where(lo, sw, z).astype(BF16), jnp.where(lo, z, pair).astype(BF16))


def _two_rows(top, bottom):
    row = lax.broadcasted_iota(jnp.int32, (2 * BLOCK, 1), 0)
    return jnp.where(row < BLOCK, top, bottom)


def _lane_col(mat, h):
    lane = lax.broadcasted_iota(jnp.int32, mat.shape, 1)
    return jnp.sum(jnp.where(lane == h, mat, 0.0), axis=1, keepdims=True)


_SCALE = 1.0 / math.sqrt(HEAD_DIM)
_NT = (((1,), (1,)), ((), ()))
_NN = (((1,), (0,)), ((), ()))
_TN = (((0,), (0,)), ((), ()))


def _qk_prep(proj, tab, qw, kw):
    L = proj.shape[0]
    tm = _tile(L, 512)

    def body(q_ref, k_ref, t_ref, qw_ref, kw_ref, qo_ref, ko_ref):
        cos, sin = t_ref[:, :128], t_ref[:, 128:]
        for c in range(ATTN_W // 128):
            _, qr = _norm_rope(q_ref[:, c * 128:(c + 1) * 128], qw_ref[...], cos, sin)
            qo_ref[:, c * 128:(c + 1) * 128] = (qr * _SCALE).astype(BF16)
        for c in range(KV_W // 128):
            _, kr = _norm_rope(k_ref[:, c * 128:(c + 1) * 128], kw_ref[...], cos, sin)
            ko_ref[:, c * 128:(c + 1) * 128] = kr.astype(BF16)

    row = pl.BlockSpec((1, 128), lambda i: (0, 0))
    return pl.pallas_call(
        body,
        name="qk_prep",
        grid=(L // tm,),
        in_specs=[pl.BlockSpec((tm, ATTN_W), lambda i: (i, 0)), pl.BlockSpec((tm, KV_W), lambda i: (i, 4)),
                  pl.BlockSpec((tm, 256), lambda i: (i, 0)), row, row],
        out_specs=[pl.BlockSpec((tm, ATTN_W), lambda i: (i, 0)), pl.BlockSpec((tm, KV_W), lambda i: (i, 0))],
        out_shape=[jax.ShapeDtypeStruct((L, ATTN_W), BF16), jax.ShapeDtypeStruct((L, KV_W), BF16)],
        compiler_params=_cp(("parallel",)),
    )(proj, proj, tab, jnp.tile(qw, 2).reshape(1, 128), jnp.tile(kw, 2).reshape(1, 128))


def _group_tiles(g, kt, vt):
    a, b = divmod(g, 2)
    return kt[a][2 * b], kt[a][2 * b + 1], vt[a][2 * b], vt[a][2 * b + 1]


def _attn_fwd(q, k, proj, sinks, ride):
    L = proj.shape[0]
    nb = L // BLOCK

    def body(q_ref, kc_ref, kp_ref, vc_ref, vp_ref, z0_ref, z1_ref, sink_ref, og_ref, o_ref, lse_ref):
        i = pl.program_id(0)
        mask = _band_mask2(i > 0)
        z = jnp.concatenate([z0_ref[...], z1_ref[...]], axis=1)
        lane = lax.broadcasted_iota(jnp.int32, (BLOCK, 128), 1)
        kt = [_half_tiles(jnp.concatenate([kp_ref[:, a * 128:(a + 1) * 128], kc_ref[:, a * 128:(a + 1) * 128]],
                                          axis=0).astype(F32)) for a in range(2)]
        vt = [_half_tiles(jnp.concatenate([vp_ref[:, a * 128:(a + 1) * 128], vc_ref[:, a * 128:(a + 1) * 128]],
                                          axis=0)) for a in range(2)]
        lse_mat = jnp.zeros((BLOCK, 128), F32)
        outs = []
        for g in range(N_KV):
            k_lo, k_hi, v_lo, v_hi = _group_tiles(g, kt, vt)
            q2 = jnp.concatenate([q_ref[:, 2 * g * 128:(2 * g + 1) * 128],
                                  q_ref[:, (2 * g + 1) * 128:(2 * g + 2) * 128]], axis=0)
            acc = jnp.zeros((2 * BLOCK, 128), F32)
            for half, (kh, vh) in enumerate(((k_lo, v_lo), (k_hi, v_hi))):
                h_top, h_bot = 4 * g + half, 4 * g + 2 + half
                s = jnp.where(mask, lax.dot_general(q2, kh, _NT, preferred_element_type=F32), -1e30)
                sink = _two_rows(sink_ref[h_top], sink_ref[h_bot])
                m = jnp.maximum(jnp.max(s, axis=-1, keepdims=True), sink)
                e = jnp.exp(s - m)
                den = jnp.sum(e, axis=-1, keepdims=True) + jnp.exp(sink - m)
                p = e * (1.0 / den)
                acc = acc + jnp.dot(p.astype(BF16), vh, preferred_element_type=F32)
                lse = m + jnp.log(den)
                lse_mat = jnp.where(lane == h_top, lse[:BLOCK], lse_mat)
                lse_mat = jnp.where(lane == h_bot, lse[BLOCK:], lse_mat)
            outs += [acc[:BLOCK], acc[BLOCK:]]
        o = jnp.concatenate(outs, axis=1)
        o_ref[...] = o
        og_ref[...] = o * _silu(z)
        lse_ref[...] = lse_mat

    prev = lambda i: jnp.maximum(i - 1, 0)
    return _call(
        body, "attn_fwd", (nb,),
        [pl.BlockSpec((BLOCK, ATTN_W), lambda i: (i, 0)),
         pl.BlockSpec((BLOCK, KV_W), lambda i: (i, 0)),
         pl.BlockSpec((BLOCK, KV_W), lambda i: (prev(i), 0)),
         pl.BlockSpec((BLOCK, KV_W), lambda i: (i, 5)),
         pl.BlockSpec((BLOCK, KV_W), lambda i: (prev(i), 5)),
         pl.BlockSpec((BLOCK, 512), lambda i: (i, 3)),
         pl.BlockSpec((BLOCK, 512), lambda i: (i, 4)),
         pl.BlockSpec(memory_space=pltpu.SMEM)],
        [pl.BlockSpec((BLOCK, ATTN_W), lambda i: (i, 0)),
         pl.BlockSpec((BLOCK, ATTN_W), lambda i: (i, 0)),
         pl.BlockSpec((BLOCK, 128), lambda i: (i, 0))],
        [jax.ShapeDtypeStruct((L, ATTN_W), F32), jax.ShapeDtypeStruct((L, ATTN_W), F32),
         jax.ShapeDtypeStruct((L, 128), F32)],
        (q, k, k, proj, proj, proj, proj, sinks), ride=ride)


def _attn_bwd(q, k, proj, sinks, d_o, o, lse, ride):
    L = proj.shape[0]
    nb = L // BLOCK

    def body(q_ref, kc_ref, kp_ref, vc_ref, vp_ref, do_ref, o_ref, lse_ref, sink_ref,
             dq_ref, dk_ref, dv_ref, gs_ref, ck_scr, cv_scr):
        i = pl.program_id(0)

        @pl.when(i == 0)
        def _():
            gs_ref[...] = jnp.zeros_like(gs_ref)
            ck_scr[...] = jnp.zeros_like(ck_scr)
            cv_scr[...] = jnp.zeros_like(cv_scr)

        @pl.when(i == nb)
        def _():
            dk_ref[...] = ck_scr[...]
            dv_ref[...] = cv_scr[...]

        @pl.when(i < nb)
        def _():
            mask = _band_mask2(i > 0)
            lane = lax.broadcasted_iota(jnp.int32, (1, 128), 1)
            lo = lax.broadcasted_iota(jnp.int32, (2 * BLOCK, 128), 1) < HEAD_DIM
            lse_c = lse_ref[...]
            kt = [_half_tiles(jnp.concatenate([kp_ref[:, a * 128:(a + 1) * 128], kc_ref[:, a * 128:(a + 1) * 128]],
                                              axis=0).astype(F32)) for a in range(2)]
            vt = [_half_tiles(jnp.concatenate([vp_ref[:, a * 128:(a + 1) * 128], vc_ref[:, a * 128:(a + 1) * 128]],
                                              axis=0)) for a in range(2)]
            gs = jnp.zeros((1, 128), F32)
            dq_parts = []
            dk_acc = [jnp.zeros((2 * BLOCK, 128), F32) for _ in range(2)]
            dv_acc = [jnp.zeros((2 * BLOCK, 128), F32) for _ in range(2)]
            for g in range(N_KV):
                a, b = divmod(g, 2)
                k_lo, k_hi, v_lo, v_hi = _group_tiles(g, kt, vt)
                t0, t1 = slice(2 * g * 128, (2 * g + 1) * 128), slice((2 * g + 1) * 128, (2 * g + 2) * 128)
                q2 = jnp.concatenate([q_ref[:, t0], q_ref[:, t1]], axis=0)
                do2 = jnp.concatenate([do_ref[:, t0], do_ref[:, t1]], axis=0)
                prod = do2 * jnp.concatenate([o_ref[:, t0], o_ref[:, t1]], axis=0)
                do2_b = do2.astype(BF16)
                dq2 = jnp.zeros((2 * BLOCK, 128), F32)
                dk_h, dv_h = [], []
                for half, (kh, vh) in enumerate(((k_lo, v_lo), (k_hi, v_hi))):
                    h_top, h_bot = 4 * g + half, 4 * g + 2 + half
                    lse = jnp.concatenate([_lane_col(lse_c, h_top), _lane_col(lse_c, h_bot)], axis=0)
                    sink = _two_rows(sink_ref[h_top], sink_ref[h_bot])
                    delta = jnp.sum(jnp.where(lo == (half == 0), prod, 0.0), axis=1, keepdims=True)
                    s = jnp.where(mask, lax.dot_general(q2, kh, _NT, preferred_element_type=F32), -1e30)
                    p = jnp.exp(s - lse)
                    dp = lax.dot_general(do2_b, vh, _NT, preferred_element_type=F32)
                    ds_b = (p * (dp - delta)).astype(BF16)
                    p_b = p.astype(BF16)
                    dq2 = dq2 + jnp.dot(ds_b, kh, preferred_element_type=F32)
                    dk_h.append(lax.dot_general(ds_b, q2, _TN, preferred_element_type=F32))
                    dv_h.append(lax.dot_general(p_b, do2_b, _TN, preferred_element_type=F32))
                    gsink = -jnp.exp(sink - lse) * delta
                    row = lax.broadcasted_iota(jnp.int32, (2 * BLOCK, 1), 0)
                    gs = gs + jnp.where(lane == h_top, jnp.sum(jnp.where(row < BLOCK, gsink, 0.0)), 0.0)
                    gs = gs + jnp.where(lane == h_bot, jnp.sum(jnp.where(row >= BLOCK, gsink, 0.0)), 0.0)
                dq_parts += [dq2[:BLOCK], dq2[BLOCK:]]
                for acc, parts in ((dk_acc, dk_h), (dv_acc, dv_h)):
                    t = jnp.where(lo, parts[0], parts[1])
                    t = t + pltpu.roll(t, HEAD_DIM, 1)
                    acc[a] = acc[a] + jnp.where(lo == (b == 0), t, 0.0)
            dq_ref[...] = jnp.concatenate(dq_parts, axis=1)
            dk_full = jnp.concatenate(dk_acc, axis=1)
            dv_full = jnp.concatenate(dv_acc, axis=1)
            dk_ref[...] = ck_scr[...] + dk_full[:BLOCK]
            dv_ref[...] = cv_scr[...] + dv_full[:BLOCK]
            ck_scr[...] = dk_full[BLOCK:]
            cv_scr[...] = dv_full[BLOCK:]
            gs_ref[...] += gs

    cur = lambda i: jnp.minimum(i, nb - 1)
    prev = lambda i: jnp.maximum(jnp.minimum(i, nb - 1) - 1, 0)
    done = lambda i: jnp.maximum(i - 1, 0)
    bs = pl.BlockSpec
    return _call(
        body, "attn_bwd", (nb + 1,),
        [bs((BLOCK, ATTN_W), lambda i: (cur(i), 0)),
         bs((BLOCK, KV_W), lambda i: (cur(i), 0)), bs((BLOCK, KV_W), lambda i: (prev(i), 0)),
         bs((BLOCK, KV_W), lambda i: (cur(i), 5)), bs((BLOCK, KV_W), lambda i: (prev(i), 5)),
         bs((BLOCK, ATTN_W), lambda i: (cur(i), 0)), bs((BLOCK, ATTN_W), lambda i: (cur(i), 0)),
         bs((BLOCK, 128), lambda i: (cur(i), 0)), bs(memory_space=pltpu.SMEM)],
        [bs((BLOCK, ATTN_W), lambda i: (cur(i), 0)),
         bs((BLOCK, KV_W), lambda i: (done(i), 0)), bs((BLOCK, KV_W), lambda i: (done(i), 0)),
         bs((1, 128), lambda i: (0, 0))],
        [jax.ShapeDtypeStruct((L, ATTN_W), F32), jax.ShapeDtypeStruct((L, KV_W), F32),
         jax.ShapeDtypeStruct((L, KV_W), F32), jax.ShapeDtypeStruct((1, 128), F32)],
        (q, k, k, proj, proj, d_o, o, lse, sinks),
        [pltpu.VMEM((BLOCK, KV_W), F32), pltpu.VMEM((BLOCK, KV_W), F32)], ride)


def _qk_prep_bwd(proj, tab, qw, kw, d_q, d_k, d_v, d_za, d_u, d_zs):
    L = proj.shape[0]
    tm = _tile(L, 512)
    z0 = ATTN_W + 2 * KV_W

    def body(q_ref, k_ref, t_ref, qw_ref, kw_ref, dq_ref, dk_ref, dv_ref, dza_ref, du_ref, dzs_ref,
             out_ref, gq_ref, gk_ref):
        i = pl.program_id(0)

        @pl.when(i == 0)
        def _():
            gq_ref[...] = jnp.zeros_like(gq_ref)
            gk_ref[...] = jnp.zeros_like(gk_ref)

        cos, sin = t_ref[:, :128], t_ref[:, 128:]
        gq = jnp.zeros((1, 128), F32)
        gk = jnp.zeros((1, 128), F32)
        for c in range(ATTN_W // 128):
            cs = slice(c * 128, (c + 1) * 128)
            d_raw, gw = _norm_rope_bwd(dq_ref[:, cs] * _SCALE, q_ref[:, cs], qw_ref[...], cos, sin)
            out_ref[:, cs] = d_raw.astype(BF16)
            gq = gq + jnp.sum(gw, axis=0, keepdims=True)
        for c in range(KV_W // 128):
            cs = slice(c * 128, (c + 1) * 128)
            d_raw, gw = _norm_rope_bwd(dk_ref[:, cs], k_ref[:, cs], kw_ref[...], cos, sin)
            out_ref[:, ATTN_W + c * 128:ATTN_W + (c + 1) * 128] = d_raw.astype(BF16)
            gk = gk + jnp.sum(gw, axis=0, keepdims=True)
        out_ref[:, ATTN_W + KV_W:z0] = dv_ref[...].astype(BF16)
        out_ref[:, z0:z0 + ATTN_W] = dza_ref[...]
        out_ref[:, z0 + ATTN_W:z0 + ATTN_W + SSM_W] = du_ref[...].astype(BF16)
        out_ref[:, z0 + ATTN_W + SSM_W:] = dzs_ref[...]
        gq_ref[...] += gq
        gk_ref[...] += gk

    row = pl.BlockSpec((1, 128), lambda i: (0, 0))
    blk = lambda w, c: pl.BlockSpec((tm, w), lambda i: (i, c))
    return pl.pallas_call(
        body,
        name="qk_prep_bwd",
        grid=(L // tm,),
        in_specs=[blk(ATTN_W, 0), blk(KV_W, 4), blk(256, 0), row, row, blk(ATTN_W, 0), blk(KV_W, 0), blk(KV_W, 0),
                  blk(ATTN_W, 0), blk(SSM_W, 0), blk(SSM_W, 0)],
        out_specs=[blk(IN_W, 0), row, row],
        out_shape=[jax.ShapeDtypeStruct((L, IN_W), BF16), jax.ShapeDtypeStruct((1, 128), F32),
                   jax.ShapeDtypeStruct((1, 128), F32)],
        compiler_params=_cp(("arbitrary",)),
    )(proj, proj, tab, jnp.tile(qw, 2).reshape(1, 128), jnp.tile(kw, 2).reshape(1, 128), d_q, d_k, d_v,
      d_za, d_u, d_zs)


def _cmul(a, b):
    return a[0] * b[0] - a[1] * b[1], a[0] * b[1] + a[1] * b[0]


def _cmul_conj(a, b):
    return a[0] * b[0] + a[1] * b[1], a[1] * b[0] - a[0] * b[1]


def _cadd(a, b):
    return a[0] + b[0], a[1] + b[1]


def _dot3(a, b, dn):
    ah, bh = a.astype(BF16), b.astype(BF16)
    al, bl = (a - ah.astype(F32)).astype(BF16), (b - bh.astype(F32)).astype(BF16)
    d = lambda u, v: lax.dot_general(u, v, dn, preferred_element_type=F32)
    return d(ah, bh) + d(ah, bl) + d(al, bh)


def _s5_discretise(a_re, a_im, ls, cosx, sinx, bt):
    delta = jnp.exp(ls)
    er = jnp.exp(a_re * delta)
    lb = (er * cosx, er * sinx)
    den = a_re * a_re + a_im * a_im
    coef = _cmul_conj((lb[0] - 1.0, lb[1]), (a_re, a_im))
    coef = (coef[0] / den, coef[1] / den)
    return delta, lb, coef, den, _cmul(coef, bt)


def _powers(lb):
    pw = [(jnp.ones_like(lb[0]), jnp.zeros_like(lb[0]))]
    for _ in range(CHUNK):
        pw.append(_cmul(pw[-1], lb))
    return pw


def _block_rows(a, pw, idx):
    blocks = [_cmul(a, pw[i]) for i in idx]
    return (jnp.concatenate([b[0] for b in blocks], axis=-2), jnp.concatenate([b[1] for b in blocks], axis=-2))


def _block_rows_bwd(g, a, pw, idx, g_pw):
    g_a = (jnp.zeros_like(a[0]), jnp.zeros_like(a[0]))
    for j, i in enumerate(idx):
        gj = (g[0][..., j * SSM_H:(j + 1) * SSM_H, :], g[1][..., j * SSM_H:(j + 1) * SSM_H, :])
        g_a = _cadd(g_a, _cmul_conj(gj, pw[i]))
        gp = _cmul_conj(gj, a)
        g_pw[i] = _cadd(g_pw[i], (jnp.sum(gp[0], axis=-2, keepdims=True), jnp.sum(gp[1], axis=-2, keepdims=True)))
    return g_a


_IDX_S = [CHUNK - 1 - s for s in range(CHUNK)]
_IDX_O = [t + 1 for t in range(CHUNK)]
_IDX_K = list(range(CHUNK))
_PREP_IN = 9


def _prep_args(p):
    row = lambda t: t.reshape(SSM_G, 1, SSM_P)
    xi = p["a_im"] * jnp.exp(p["log_step"])[:, None]
    return (row(p["a_re"]), row(p["a_im"]), row(jnp.broadcast_to(p["log_step"][:, None], (SSM_G, SSM_P))),
            row(jnp.cos(xi)), row(jnp.sin(xi)), p["b_re"].transpose(0, 2, 1), p["b_im"].transpose(0, 2, 1),
            p["c_re"], p["c_im"])


PREP_GROUPS = 8


def _prep_specs():
    r1 = pl.BlockSpec((PREP_GROUPS, 1, SSM_P), lambda g: (g, 0, 0))
    r16 = pl.BlockSpec((PREP_GROUPS, SSM_H, SSM_P), lambda g: (g, 0, 0))
    return [r1] * 5 + [r16] * 4, r1, r16


def _ssm_prep(p):
    def one_group(q, are, aim, ls, cosx, sinx, btr, bti, cre, cim, mt_ref, s_ref, o_ref, a_ref):
        _, lb, _, _, bb = _s5_discretise(are[q], aim[q], ls[q], cosx[q], sinx[q], (btr[q], bti[q]))
        pw = _powers(lb)
        c = (cre[q], cim[q])
        sc = _block_rows(bb, pw, _IDX_S)
        ot = _block_rows(c, pw, _IDX_O)
        ok = _block_rows(c, pw, _IDX_K)
        s_ref[q] = jnp.concatenate([sc[0], sc[1]], axis=1).astype(BF16)
        o_ref[q] = jnp.concatenate([ot[0], -ot[1]], axis=1).astype(BF16)
        a_ref[q] = jnp.concatenate([pw[CHUNK][0], pw[CHUNK][1]], axis=1)
        kt = _dot3(jnp.concatenate([bb[0], -bb[1]], axis=1), jnp.concatenate([ok[0], ok[1]], axis=1), _NT)
        lane = lax.broadcasted_iota(jnp.int32, kt.shape, 1)
        for s in range(CHUNK):
            blk = kt if s == 0 else jnp.where(lane >= SSM_H * s, pltpu.roll(kt, SSM_H * s, 1), 0.0)
            mt_ref[q, s * SSM_H:(s + 1) * SSM_H, :] = blk.astype(BF16)

    def body(*refs):
        for q in range(PREP_GROUPS):
            one_group(q, *refs)

    in_specs, r1, _ = _prep_specs()
    g3 = lambda r, c: pl.BlockSpec((PREP_GROUPS, r, c), lambda g: (g, 0, 0))
    return pl.pallas_call(
        body,
        name="ssm_prep",
        grid=(SSM_G // PREP_GROUPS,),
        in_specs=in_specs,
        out_specs=[g3(CW, CW), g3(CW, 2 * SSM_P), g3(CW, 2 * SSM_P), g3(1, 2 * SSM_P)],
        out_shape=[jax.ShapeDtypeStruct((SSM_G, CW, CW), BF16), jax.ShapeDtypeStruct((SSM_G, CW, 2 * SSM_P), BF16),
                   jax.ShapeDtypeStruct((SSM_G, CW, 2 * SSM_P), BF16),
                   jax.ShapeDtypeStruct((SSM_G, 1, 2 * SSM_P), F32)],
        compiler_params=_cp(("parallel",)),
    )(*_prep_args(p))


def _ssm_prep_bwd(p, g_mt, g_scat, g_ocat, g_a16, ride):
    def body(are, aim, ls, cosx, sinx, btr, bti, cre, cim, gmt_ref, gs_ref, go_ref, ga_ref,
             g_are, g_aim, g_ls, g_btr, g_bti, g_cre, g_cim, ga1_scr, gb1_scr):
        lam = (are[...], aim[...])
        bt = (btr[...], bti[...])
        delta, lb, coef, den, bb = _s5_discretise(lam[0], lam[1], ls[...], cosx[...], sinx[...], bt)
        pw = _powers(lb)
        c = (cre[...], cim[...])
        ok = _block_rows(c, pw, _IDX_K)
        g_pw = [(jnp.zeros_like(lb[0]), jnp.zeros_like(lb[0])) for _ in range(CHUNK + 1)]
        lane = lax.broadcasted_iota(jnp.int32, (SSM_H, CW), 1)
        for q in range(PREP_GROUPS):
            g_kt = gmt_ref[q, :SSM_H, :]
            for s in range(1, CHUNK):
                blk = gmt_ref[q, s * SSM_H:(s + 1) * SSM_H, :]
                g_kt = g_kt + jnp.where(lane < CW - SSM_H * s, pltpu.roll(blk, CW - SSM_H * s, 1), 0.0)
            a1 = jnp.concatenate([bb[0][q], -bb[1][q]], axis=1)
            b1 = jnp.concatenate([ok[0][q], ok[1][q]], axis=1)
            ga1_scr[q] = _dot3(g_kt, b1, _NN)
            gb1_scr[q] = _dot3(g_kt, a1, _TN)
        g_a1, g_b1 = ga1_scr[...], gb1_scr[...]
        g_bb = (g_a1[..., :SSM_P], -g_a1[..., SSM_P:])
        g_c = _block_rows_bwd((g_b1[..., :SSM_P], g_b1[..., SSM_P:]), c, pw, _IDX_K, g_pw)
        gs = gs_ref[...]
        g_bb = _cadd(g_bb, _block_rows_bwd((gs[..., :SSM_P], gs[..., SSM_P:]), bb, pw, _IDX_S, g_pw))
        go = go_ref[...]
        g_c = _cadd(g_c, _block_rows_bwd((go[..., :SSM_P], -go[..., SSM_P:]), c, pw, _IDX_O, g_pw))
        ga = ga_ref[...]
        g_pw[CHUNK] = _cadd(g_pw[CHUNK], (ga[..., :SSM_P], ga[..., SSM_P:]))
        g_lb = (jnp.zeros_like(lb[0]), jnp.zeros_like(lb[0]))
        for l in range(CHUNK - 1, -1, -1):
            g_lb = _cadd(g_lb, _cmul_conj(g_pw[l + 1], pw[l]))
            g_pw[l] = _cadd(g_pw[l], _cmul_conj(g_pw[l + 1], lb))
        g_bt = _cmul_conj(g_bb, coef)
        gc = _cmul_conj(g_bb, bt)
        g_coef = (jnp.sum(gc[0], axis=-2, keepdims=True), jnp.sum(gc[1], axis=-2, keepdims=True))
        lam_den = (lam[0] / den, lam[1] / den)
        g_lb = _cadd(g_lb, _cmul(g_coef, lam_den))
        t = _cmul(_cmul_conj(g_coef, coef), lam_den)
        g_x = _cmul_conj(g_lb, lb)
        g_are[...] = g_x[0] * delta - t[0]
        g_aim[...] = g_x[1] * delta - t[1]
        g_ls[...] = (g_x[0] * lam[0] + g_x[1] * lam[1]) * delta
        g_btr[...] = g_bt[0]
        g_bti[...] = g_bt[1]
        g_cre[...] = g_c[0]
        g_cim[...] = g_c[1]

    in_specs, r1, r16 = _prep_specs()
    g3 = lambda r, c: pl.BlockSpec((PREP_GROUPS, r, c), lambda g: (g, 0, 0))
    rows = jax.ShapeDtypeStruct((SSM_G, 1, SSM_P), F32)
    mats = jax.ShapeDtypeStruct((SSM_G, SSM_H, SSM_P), F32)
    (g_are, g_aim, g_ls, g_btr, g_bti, g_cre, g_cim), landed = _call(
        body, "ssm_prep_bwd", (SSM_G // PREP_GROUPS,),
        in_specs + [g3(CW, CW), g3(CW, 2 * SSM_P), g3(CW, 2 * SSM_P), g3(1, 2 * SSM_P)],
        [r1] * 3 + [r16] * 4, [rows] * 3 + [mats] * 4, (*_prep_args(p), g_mt, g_scat, g_ocat, g_a16),
        [pltpu.VMEM((PREP_GROUPS, SSM_H, 2 * SSM_P), F32), pltpu.VMEM((PREP_GROUPS, CW, 2 * SSM_P), F32)], ride)
    grads = dict(a_re=g_are.reshape(SSM_G, SSM_P), a_im=g_aim.reshape(SSM_G, SSM_P),
                 log_step=jnp.sum(g_ls.reshape(SSM_G, SSM_P), axis=1),
                 b_re=g_btr.transpose(0, 2, 1), b_im=g_bti.transpose(0, 2, 1), c_re=g_cre, c_im=g_cim)
    return grads, landed


def _cmul_const(xv, ar, ai):
    return xv * ar + pltpu.roll(xv, SSM_P, 1) * ai


def _chunk_scan(inc, a_row, reverse):
    n = inc.shape[0]
    lane = lax.broadcasted_iota(jnp.int32, (1, 2 * SSM_P), 1)
    row = lax.broadcasted_iota(jnp.int32, inc.shape, 0)
    sign = jnp.where(lane < SSM_P, -1.0, 1.0)
    ar = jnp.where(lane < SSM_P, a_row, pltpu.roll(a_row, SSM_P, 1))
    ai = jnp.where(lane < SSM_P, pltpu.roll(a_row, SSM_P, 1), a_row)
    if reverse:
        ai = -ai
    xv = inc
    s = 1
    while s < n:
        if reverse:
            sh = jnp.where(row < n - s, pltpu.roll(xv, n - s, 0), 0.0)
        else:
            sh = jnp.where(row >= s, pltpu.roll(xv, s, 0), 0.0)
        xv = xv + _cmul_const(sh, ar, ai * sign)
        ar, ai = ar * ar - ai * ai, 2.0 * ar * ai
        s *= 2
    return xv


def _shift_rows(xv, reverse):
    n = xv.shape[0]
    row = lax.broadcasted_iota(jnp.int32, xv.shape, 0)
    if reverse:
        return jnp.where(row < n - 1, pltpu.roll(xv, n - 1, 0), 0.0)
    return jnp.where(row >= 1, pltpu.roll(xv, 1, 0), 0.0)


GB = 128 // SSM_H
U_COL0 = (ATTN_W + 2 * KV_W + ATTN_W) // 128


HALF = CHUNK // 2


def _chunk_perm():
    r = jnp.arange(HALF * 128)
    t, g8, h = r // 128, (r % 128) // SSM_H, r % SSM_H
    return ((g8 * 128 + t * SSM_H + h)[:, None] == jnp.arange(GB * 128)[None, :]).astype(BF16)


def _load_perm(p_hbm, p_scr, sem):
    @pl.when(pl.program_id(0) == 0)
    def _():
        cp = pltpu.make_async_copy(p_hbm, p_scr, sem)
        cp.start()
        cp.wait()


def _rows_to_chunks(pieces, perm):
    halves = [jnp.dot(jnp.concatenate(pieces[k * HALF:(k + 1) * HALF], axis=1).astype(BF16), perm,
                      preferred_element_type=F32).astype(BF16) for k in range(2)]
    return [jnp.concatenate([hv[:, g * 128:(g + 1) * 128] for hv in halves], axis=1) for g in range(GB)]


def _chunks_to_rows(groups, perm, two_pass):
    pieces = []
    for k in range(2):
        v = jnp.concatenate([gv[:, k * 128:(k + 1) * 128] for gv in groups], axis=1)
        hi = v.astype(BF16)
        out = lax.dot_general(hi, perm, _NT, preferred_element_type=F32)
        if two_pass:
            lo = (v - hi.astype(F32)).astype(BF16)
            out = out + lax.dot_general(lo, perm, _NT, preferred_element_type=F32)
        pieces += [out[:, t * 128:(t + 1) * 128] for t in range(HALF)]
    return pieces


def _ssm_fwd(proj, perm, mt, scat, ocat, a16, d_skip, ride):
    L = proj.shape[0]
    nc = L // CHUNK

    def body(u_ref, p_hbm, mt_ref, s_ref, o_ref, a_ref, d_ref, y_ref, yg_ref, h_ref, p_scr, sem):
        _load_perm(p_hbm, p_scr, sem)
        perm = p_scr[...]
        rows = [pl.ds(t, nc, stride=CHUNK) for t in range(CHUNK)]
        us = [u_ref[r, :] for r in rows]
        ua = _rows_to_chunks(us, perm)
        ys = []
        for g in range(GB):
            uv = ua[g]
            inc = jnp.dot(uv, s_ref[g], preferred_element_type=F32)
            hx = _shift_rows(_chunk_scan(inc, a_ref[g], False), False)
            h_ref[g] = hx
            ys.append(jnp.dot(uv, mt_ref[g], preferred_element_type=F32)
                      + lax.dot_general(hx.astype(BF16), o_ref[g], _NT, preferred_element_type=F32))
        yp = _chunks_to_rows(ys, perm, True)
        for t, r in enumerate(rows):
            y = yp[t] + d_ref[...] * us[t]
            y_ref[r, :] = y
            yg_ref[r, :] = _gelu(y)

    g3 = lambda r, c: pl.BlockSpec((GB, r, c), lambda g: (g, 0, 0))
    col = pl.BlockSpec((L, 128), lambda g: (0, g))
    return _call(
        body, "ssm_fwd", (SSM_G // GB,),
        [pl.BlockSpec((L, 128), lambda g: (0, U_COL0 + g)), _ANY,
         g3(CW, CW), g3(CW, 2 * SSM_P), g3(CW, 2 * SSM_P), g3(1, 2 * SSM_P),
         pl.BlockSpec((1, 128), lambda g: (0, g))],
        [col, col, g3(nc, 2 * SSM_P)],
        [jax.ShapeDtypeStruct((L, SSM_W), F32), jax.ShapeDtypeStruct((L, SSM_W), F32),
         jax.ShapeDtypeStruct((SSM_G, nc, 2 * SSM_P), F32)],
        (proj, perm, mt, scat, ocat, a16, d_skip.reshape(1, SSM_W)),
        [pltpu.VMEM((HALF * 128, GB * 128), BF16), pltpu.SemaphoreType.DMA], ride)


def _ssm_bwd(d_yg, y, proj, hx, perm, mt, scat, ocat, a16, d_skip, ride):
    L = proj.shape[0]
    nc = L // CHUNK

    def body(dg_ref, y_ref, u_ref, h_ref, p_hbm, mt_ref, s_ref, o_ref, a_ref, d_ref,
             du_ref, gmt_ref, gs_ref, go_ref, ga_ref, gd_ref, p_scr, sem):
        _load_perm(p_hbm, p_scr, sem)
        perm = p_scr[...]
        rows = [pl.ds(t, nc, stride=CHUNK) for t in range(CHUNK)]
        us = [u_ref[r, :] for r in rows]
        dys = [dg_ref[r, :] * _dgelu(y_ref[r, :]) for r in rows]
        gd = jnp.zeros((1, 128), F32)
        for uv, dy in zip(us, dys):
            gd = gd + jnp.sum(dy * uv, axis=0, keepdims=True)
        gd_ref[...] = gd
        ua = _rows_to_chunks(us, perm)
        dya = _rows_to_chunks(dys, perm)
        lane = lax.broadcasted_iota(jnp.int32, (1, 2 * SSM_P), 1)
        dus = []
        for g in range(GB):
            uv, dy, hx_v = ua[g], dya[g], h_ref[g]
            dh = jnp.dot(dy, o_ref[g], preferred_element_type=F32)
            dinc = _shift_rows(_chunk_scan(dh, a_ref[g], True), True)
            dinc_b = dinc.astype(BF16)
            dus.append(lax.dot_general(dy, mt_ref[g], _NT, preferred_element_type=F32)
                       + lax.dot_general(dinc_b, s_ref[g], _NT, preferred_element_type=F32))
            gmt_ref[g] = lax.dot_general(uv, dy, _TN, preferred_element_type=F32)
            gs_ref[g] = lax.dot_general(uv, dinc_b, _TN, preferred_element_type=F32)
            go_ref[g] = lax.dot_general(dy, hx_v.astype(BF16), _TN, preferred_element_type=F32)
            p1 = dinc * hx_v
            p2 = pltpu.roll(dinc, SSM_P, 1) * hx_v
            t1 = jnp.sum(p1 + pltpu.roll(p1, SSM_P, 1), axis=0, keepdims=True)
            t2 = jnp.sum(p2 - pltpu.roll(p2, SSM_P, 1), axis=0, keepdims=True)
            ga_ref[g] = jnp.where(lane < SSM_P, t1, pltpu.roll(t2, SSM_P, 1))
        dup = _chunks_to_rows(dus, perm, False)
        for t, r in enumerate(rows):
            du_ref[r, :] = dup[t] + d_ref[...] * dys[t]

    g3 = lambda r, c: pl.BlockSpec((GB, r, c), lambda g: (g, 0, 0))
    col = pl.BlockSpec((L, 128), lambda g: (0, g))
    row = pl.BlockSpec((1, 128), lambda g: (0, g))
    return _call(
        body, "ssm_bwd", (SSM_G // GB,),
        [col, col, pl.BlockSpec((L, 128), lambda g: (0, U_COL0 + g)), g3(nc, 2 * SSM_P), _ANY,
         g3(CW, CW), g3(CW, 2 * SSM_P), g3(CW, 2 * SSM_P), g3(1, 2 * SSM_P), row],
        [col, g3(CW, CW), g3(CW, 2 * SSM_P), g3(CW, 2 * SSM_P), g3(1, 2 * SSM_P), row],
        [jax.ShapeDtypeStruct((L, SSM_W), F32), jax.ShapeDtypeStruct((SSM_G, CW, CW), F32),
         jax.ShapeDtypeStruct((SSM_G, CW, 2 * SSM_P), F32), jax.ShapeDtypeStruct((SSM_G, CW, 2 * SSM_P), F32),
         jax.ShapeDtypeStruct((SSM_G, 1, 2 * SSM_P), F32), jax.ShapeDtypeStruct((1, SSM_W), F32)],
        (d_yg, y, proj, hx, perm, mt, scat, ocat, a16, d_skip.reshape(1, SSM_W)),
        [pltpu.VMEM((HALF * 128, GB * 128), BF16), pltpu.SemaphoreType.DMA], ride)


def _merge(og, yg, gpre, proj, b_glu, wa, ws):
    L = og.shape[0]
    tm = _tile(L, 256)

    def body(og_ref, yg_ref, gp_ref, z0_ref, z1_ref, b_ref, wa_ref, ws_ref, m_ref):
        zs = jnp.concatenate([z0_ref[...], z1_ref[...]], axis=1)
        os_ = yg_ref[...] * _sigmoid(gp_ref[...] + b_ref[...]) * _silu(zs)
        ogv = og_ref[...]
        ra = lax.rsqrt(jnp.mean(ogv * ogv, axis=-1, keepdims=True) + NORM_EPS)
        rs = lax.rsqrt(jnp.mean(os_ * os_, axis=-1, keepdims=True) + NORM_EPS)
        m_ref[:, :ATTN_W] = (ogv * ra * wa_ref[...]).astype(BF16)
        m_ref[:, ATTN_W:] = (os_ * rs * ws_ref[...]).astype(BF16)

    row = lambda w: pl.BlockSpec((1, w), lambda i: (0, 0))
    return pl.pallas_call(
        body,
        name="merge",
        grid=(L // tm,),
        in_specs=[pl.BlockSpec((tm, ATTN_W), lambda i: (i, 0)), pl.BlockSpec((tm, SSM_W), lambda i: (i, 0)),
                  pl.BlockSpec((tm, SSM_W), lambda i: (i, 0)),
                  pl.BlockSpec((tm, 512), lambda i: (i, 7)), pl.BlockSpec((tm, 512), lambda i: (i, 8)),
                  row(SSM_W), row(ATTN_W), row(SSM_W)],
        out_specs=pl.BlockSpec((tm, D_MODEL), lambda i: (i, 0)),
        out_shape=jax.ShapeDtypeStruct((L, D_MODEL), BF16),
        compiler_params=_cp(("parallel",)),
    )(og, yg, gpre, proj, proj, b_glu.reshape(1, SSM_W), wa.reshape(1, ATTN_W), ws.reshape(1, SSM_W))


def _outproj_loss(merged, w_out, x, target):
    L = x.shape[0]
    tm, tn = _tile(L, 512), 1024
    ni, nj = L // tm, D_MODEL // tn

    def body(m_ref, w_ref, x_ref, t_ref, d_ref, db_ref, l_ref):
        out = x_ref[...] + jnp.dot(m_ref[...], w_ref[...], preferred_element_type=F32)
        diff = out - t_ref[...]
        d = diff * (1.0 / D_MODEL)
        d_ref[...] = d
        db_ref[...] = d.astype(BF16)
        l_ref[...] = jnp.full((1, 8, 128), jnp.sum(diff * diff), F32)

    return pl.pallas_call(
        body,
        name="outproj_loss",
        grid=(nj, ni),
        in_specs=[pl.BlockSpec((tm, D_MODEL), lambda j, i: (i, 0)),
                  pl.BlockSpec((D_MODEL, tn), lambda j, i: (0, j)),
                  pl.BlockSpec((tm, tn), lambda j, i: (i, j)),
                  pl.BlockSpec((tm, tn), lambda j, i: (i, j))],
        out_specs=[pl.BlockSpec((tm, tn), lambda j, i: (i, j)), pl.BlockSpec((tm, tn), lambda j, i: (i, j)),
                   pl.BlockSpec((1, 8, 128), lambda j, i: (i * nj + j, 0, 0))],
        out_shape=[jax.ShapeDtypeStruct((L, D_MODEL), F32), jax.ShapeDtypeStruct((L, D_MODEL), BF16),
                   jax.ShapeDtypeStruct((ni * nj, 8, 128), F32)],
        compiler_params=_cp(("parallel", "parallel")),
    )(merged, w_out, x, target)


def _merge_bwd(d_out_b, w_out, og, o, yg, gpre, proj, b_glu, wa, ws):
    L = og.shape[0]
    tm = _tile(L, 256)

    def body(dout_ref, wo_ref, og_ref, o_ref, yg_ref, gp_ref, za0_ref, za1_ref, zs0_ref, zs1_ref, b_ref, wa_ref,
             ws_ref, do_ref, dza_ref, dzs_ref, dg_ref, dyg_ref, gwa_ref, gws_ref, gb_ref):
        i = pl.program_id(0)

        @pl.when(i == 0)
        def _():
            gwa_ref[...] = jnp.zeros_like(gwa_ref)
            gws_ref[...] = jnp.zeros_like(gws_ref)
            gb_ref[...] = jnp.zeros_like(gb_ref)

        dm = lax.dot_general(dout_ref[...], wo_ref[...], _NT, preferred_element_type=F32)
        za = jnp.concatenate([za0_ref[...], za1_ref[...]], axis=1)
        zs = jnp.concatenate([zs0_ref[...], zs1_ref[...]], axis=1)
        ogv, dma = og_ref[...], dm[:, :ATTN_W]
        ra = lax.rsqrt(jnp.mean(ogv * ogv, axis=-1, keepdims=True) + NORM_EPS)
        xh = ogv * ra
        gwa_ref[...] += jnp.sum(dma * xh, axis=0, keepdims=True)
        gx = dma * wa_ref[...]
        d_og = ra * (gx - xh * jnp.mean(gx * xh, axis=-1, keepdims=True))
        do_ref[...] = d_og * _silu(za)
        dza_ref[...] = (d_og * o_ref[...] * _dsilu(za)).astype(BF16)
        ygv = yg_ref[...]
        sg = _sigmoid(gp_ref[...] + b_ref[...])
        y2 = ygv * sg
        sz = _silu(zs)
        os_ = y2 * sz
        dms = dm[:, ATTN_W:]
        rs = lax.rsqrt(jnp.mean(os_ * os_, axis=-1, keepdims=True) + NORM_EPS)
        xs = os_ * rs
        gws_ref[...] += jnp.sum(dms * xs, axis=0, keepdims=True)
        gxs = dms * ws_ref[...]
        d_os = rs * (gxs - xs * jnp.mean(gxs * xs, axis=-1, keepdims=True))
        dzs_ref[...] = (d_os * y2 * _dsilu(zs)).astype(BF16)
        d_y2 = d_os * sz
        d_g = d_y2 * ygv * sg * (1.0 - sg)
        dg_ref[...] = d_g.astype(BF16)
        gb_ref[...] += jnp.sum(d_g, axis=0, keepdims=True)
        dyg_ref[...] = d_y2 * sg

    row = lambda w: pl.BlockSpec((1, w), lambda i: (0, 0))
    full = lambda w: pl.BlockSpec((tm, w), lambda i: (i, 0))
    half = lambda c: pl.BlockSpec((tm, 512), lambda i: (i, c))
    return pl.pallas_call(
        body,
        name="merge_bwd",
        grid=(L // tm,),
        in_specs=[full(D_MODEL), pl.BlockSpec((D_MODEL, D_MODEL), lambda i: (0, 0)),
                  full(ATTN_W), full(ATTN_W), full(SSM_W), full(SSM_W),
                  half(3), half(4), half(7), half(8), row(SSM_W), row(ATTN_W), row(SSM_W)],
        out_specs=[full(ATTN_W), full(ATTN_W), full(SSM_W), full(SSM_W), full(SSM_W),
                   row(ATTN_W), row(SSM_W), row(SSM_W)],
        out_shape=[jax.ShapeDtypeStruct((L, ATTN_W), F32), jax.ShapeDtypeStruct((L, ATTN_W), BF16),
                   jax.ShapeDtypeStruct((L, SSM_W), BF16), jax.ShapeDtypeStruct((L, SSM_W), BF16),
                   jax.ShapeDtypeStruct((L, SSM_W), F32),
                   jax.ShapeDtypeStruct((1, ATTN_W), F32), jax.ShapeDtypeStruct((1, SSM_W), F32),
                   jax.ShapeDtypeStruct((1, SSM_W), F32)],
        compiler_params=_cp(("arbitrary",)),
    )(d_out_b, w_out, og, o, yg, gpre, proj, proj, proj, proj, b_glu.reshape(1, SSM_W), wa.reshape(1, ATTN_W),
      ws.reshape(1, SSM_W))


def _rms_bwd_x(x, norm_w, d_hn, d_out, ride):
    L = x.shape[0]
    tm = _tile(L, 256)

    def body(x_ref, w_ref, dh_ref, do_ref, gx_ref, gw_ref):
        i = pl.program_id(0)

        @pl.when(i == 0)
        def _():
            gw_ref[...] = jnp.zeros_like(gw_ref)

        xv, dh = x_ref[...], dh_ref[...]
        r = lax.rsqrt(jnp.mean(xv * xv, axis=-1, keepdims=True) + NORM_EPS)
        xh = xv * r
        gw_ref[...] += jnp.sum(dh * xh, axis=0, keepdims=True)
        gx = dh * w_ref[...]
        gx_ref[...] = do_ref[...] + r * (gx - xh * jnp.mean(gx * xh, axis=-1, keepdims=True))

    blk = pl.BlockSpec((tm, D_MODEL), lambda i: (i, 0))
    row = pl.BlockSpec((1, D_MODEL), lambda i: (0, 0))
    return _call(body, "rms_bwd_x", (L // tm,), [blk, row, blk, blk], [blk, row],
                 [jax.ShapeDtypeStruct((L, D_MODEL), F32), jax.ShapeDtypeStruct((1, D_MODEL), F32)],
                 (x, norm_w.reshape(1, D_MODEL), d_hn, d_out), ride=ride)


def _rope_table(positions):
    inv_freq = ROPE_THETA ** (-jnp.arange(0, HEAD_DIM, 2, dtype=F32) / HEAD_DIM)
    ang = positions.astype(F32)[:, None] * inv_freq
    c, s = jnp.cos(ang), jnp.sin(ang)
    return jnp.concatenate([c, c, c, c, -s, s, -s, s], axis=1)


def _step(x, positions, target, w, core, chip):
    small = {n: w[n] for n in _SMALL}
    tab = _rope_table(positions)
    mt_b, scat_b, ocat_b, a16 = _ssm_prep(small)
    perm = _chunk_perm()
    blocks = lambda t: t.reshape(N_DEV, t.shape[0] // N_DEV, t.shape[1])

    proj, hn, wt_in = _rms_inproj_gather(x, small["norm_w"], w["w_in"].T.astype(BF16), chip)
    wt_in = wt_in.reshape(IN_W, D_MODEL)
    q_rot, k_rot = _qk_prep(proj, tab, small["q_norm_w"], small["k_norm_w"])
    (og, o, lse), (w_glu,) = _attn_fwd(q_rot, k_rot, proj, small["sinks"],
                                       _gather_exchange([w["w_glu"].astype(BF16)]))
    (y, yg, hx), (w_out,) = _ssm_fwd(proj, perm, mt_b, scat_b, ocat_b, a16, small["d_skip"],
                                     _gather_exchange([w["w_out"].astype(BF16)]))
    w_glu, w_out = w_glu.reshape(SSM_W, SSM_W), w_out.reshape(D_MODEL, D_MODEL)
    gpre = _mm(yg, w_glu, "nn", F32, "glu_fwd")
    merged = _merge(og, yg, gpre, proj, small["b_glu"], small["attn_out_norm_w"], small["ssm_out_norm_w"])
    d_out, d_out_b, loss_parts = _outproj_loss(merged, w_out, x, target)
    loss = 0.5 * jnp.sum(loss_parts[:, 0, 0]) / D_MODEL

    g_w_out = blocks(_mm(merged, d_out_b, "tn", F32, "grad_w_out"))
    d_o, d_za, d_zs, d_g, d_yg1, g_wa, g_ws, g_bglu = _merge_bwd(
        d_out_b, w_out, og, o, yg, gpre, proj, small["b_glu"], small["attn_out_norm_w"], small["ssm_out_norm_w"])
    g_w_glu = blocks(_mm(yg, d_g, "tn", F32, "grad_w_glu"))
    d_yg = _mm(d_g, w_glu, "nt", F32, "d_yg", add=d_yg1)
    (d_u, g_mt, g_scat, g_ocat, g_a16, g_dskip), (ra_out, ra_glu) = _ssm_bwd(
        d_yg, y, proj, hx, perm, mt_b, scat_b, ocat_b, a16, small["d_skip"], _pair_exchange([g_w_out, g_w_glu]))
    p_out = _pair_sum(g_w_out, ra_out, core, BF16, "pair_sum_out")
    p_glu = _pair_sum(g_w_glu, ra_glu, core, BF16, "pair_sum_glu")
    (d_q, d_k, d_v, g_sinks), (rb_out, rb_glu) = _attn_bwd(
        q_rot, k_rot, proj, small["sinks"], d_o, o, lse, _chip_exchange([p_out, p_glu]))
    d_proj, g_qw, g_kw = _qk_prep_bwd(proj, tab, small["q_norm_w"], small["k_norm_w"], d_q, d_k, d_v,
                                      d_za, d_u, d_zs)
    g_qw = g_qw[0, :HEAD_DIM] + g_qw[0, HEAD_DIM:]
    g_kw = g_kw[0, :HEAD_DIM] + g_kw[0, HEAD_DIM:]
    g_in_a = blocks(_mm(d_proj, hn, "tn", F32, "grad_w_in_a", panel=0))
    g_in_b, (ra_a,) = _mm(d_proj, hn, "tn", F32, "grad_w_in_b", panel=1, ride=_pair_exchange([g_in_a]))
    g_in_b = blocks(g_in_b)
    p_a = _pair_sum(g_in_a, ra_a, core, BF16, "pair_sum_in_a")
    d_hn, (rb_a, ra_b) = _mm(d_proj, wt_in, "nn", F32, "d_hn",
                             ride=_both(_chip_exchange([p_a]), _pair_exchange([g_in_b])))
    p_b = _pair_sum(g_in_b, ra_b, core, BF16, "pair_sum_in_b")
    g_small, (rb_b,) = _ssm_prep_bwd(small, g_mt, g_scat, g_ocat, g_a16, _chip_exchange([p_b]))
    (grad_x, g_nw), _ = _rms_bwd_x(x, small["norm_w"], d_hn, d_out, None)
    g_wt_in = jnp.concatenate([_chip_sum(p_a, rb_a, chip, "chip_sum_in_a"),
                               _chip_sum(p_b, rb_b, chip, "chip_sum_in_b")], axis=1)

    g_small.update(norm_w=g_nw.reshape(-1), q_norm_w=g_qw.reshape(-1), k_norm_w=g_kw.reshape(-1),
                   sinks=g_sinks[0, :N_HEADS], d_skip=g_dskip.reshape(-1), b_glu=g_bglu.reshape(-1),
                   attn_out_norm_w=g_wa.reshape(-1), ssm_out_norm_w=g_ws.reshape(-1))
    slab = _pack(g_small, loss).reshape(N_DEV, _PACK_ROWS // N_DEV, 128)
    (ra_s,) = _run_exchange(_pair_exchange([slab]), "pair_exchange_small")
    p_s = _pair_sum(slab, ra_s, core, F32, "pair_sum_small")
    (rb_s,) = _run_exchange(_chip_exchange([p_s]), "chip_exchange_small")
    (g_packed,) = _run_exchange(_gather_exchange([_chip_sum(p_s, rb_s, chip, "chip_sum_small")]), "gather_small")

    g_packed = g_packed.reshape(_PACK_ROWS, 128)
    grads = _unpack(g_packed, w)
    grads.update(w_in=g_wt_in.T,
                 w_glu=_chip_sum(p_glu, rb_glu, chip, "chip_sum_glu"),
                 w_out=_chip_sum(p_out, rb_out, chip, "chip_sum_out"))
    return g_packed[_LOSS_ROW, 0], grad_x, grads


_ANY = pl.BlockSpec(memory_space=pl.ANY)


class _Exchange:
    def __init__(self, arrays, out_shape, sems, start, finish, relay=None):
        self.arrays, self.out_shape, self.sems, self.start, self.finish = arrays, out_shape, sems, start, finish
        self.relay = relay if relay is not None else (lambda ins, outs, sems: None)


def _gather_exchange(blocks):
    n = len(blocks)

    def parts(ins, outs, sems):
        send_sems, recv_sems, local_sems = sems
        x, y, c = lax.axis_index("x"), lax.axis_index("y"), lax.axis_index("c")
        me, sibling = (x, y, c), (x, y, 1 - c)
        chips = [(1 - x, y), (x, 1 - y), (1 - x, 1 - y)]

        def slot(k, dev):
            return outs[k].at[4 * dev[0] + 2 * dev[1] + dev[2]]

        def copy(k, q, block, to, src=None):
            return pltpu.make_async_remote_copy(
                src_ref=slot(k, block) if src is None else src, dst_ref=slot(k, block),
                send_sem=send_sems.at[k, q], recv_sem=recv_sems.at[k, q], device_id=to, device_id_type=MESH)

        mine = [pltpu.make_async_copy(ins[k], slot(k, me), local_sems.at[k]) for k in range(n)]
        first = []
        for k in range(n):
            first.append(copy(k, 0, me, sibling, src=ins[k]))
            first += [copy(k, 1 + j, me, (*chip, c), src=ins[k]) for j, chip in enumerate(chips)]
        return me, sibling, chips, c, copy, mine, first

    def start(ins, outs, sems):
        *_, mine, first = parts(ins, outs, sems)
        for cp in mine + first:
            cp.start()

    def relay(ins, outs, sems):
        me, sibling, chips, c, copy, _, _ = parts(ins, outs, sems)
        for j, chip in enumerate(chips):
            for k in range(n):
                copy(k, 1 + j, (*chip, c), me).wait_recv()
                copy(k, 4 + j, (*chip, c), sibling).start()

    def finish(ins, outs, sems):
        me, sibling, chips, c, copy, mine, first = parts(ins, outs, sems)
        for k in range(n):
            copy(k, 0, sibling, me).wait_recv()
            for j, chip in enumerate(chips):
                copy(k, 4 + j, (*chip, 1 - c), me).wait_recv()
        for cp in first + [copy(k, 4 + j, (*chip, c), sibling) for k in range(n) for j, chip in enumerate(chips)]:
            cp.wait_send()
        for cp in mine:
            cp.wait()

    return _Exchange(blocks, [jax.ShapeDtypeStruct((N_DEV,) + b.shape, b.dtype) for b in blocks],
                     [pltpu.SemaphoreType.DMA((n, 7)), pltpu.SemaphoreType.DMA((n, 7)), pltpu.SemaphoreType.DMA((n,))],
                     start, finish, relay)


def _direct_exchange(arrays, out_lead, fan, route):
    n = len(arrays)

    def copies(ins, outs, sems):
        send_sems, recv_sems = sems
        legs = route(lax.axis_index("x"), lax.axis_index("y"), lax.axis_index("c"))
        return [pltpu.make_async_remote_copy(
            src_ref=ins[k].at[src], dst_ref=outs[k].at[q], send_sem=send_sems.at[k, q], recv_sem=recv_sems.at[k, q],
            device_id=to, device_id_type=MESH) for k in range(n) for src, q, to in legs]

    def start(ins, outs, sems):
        for cp in copies(ins, outs, sems):
            cp.start()

    def finish(ins, outs, sems):
        for cp in copies(ins, outs, sems):
            cp.wait()

    return _Exchange(arrays, [jax.ShapeDtypeStruct((out_lead,) + a.shape[1:], a.dtype) for a in arrays],
                     [pltpu.SemaphoreType.DMA((n, fan)), pltpu.SemaphoreType.DMA((n, fan))], start, finish)


def _pair_exchange(grads):
    return _direct_exchange(grads, 4, 4, lambda x, y, c: [(2 * chip + (1 - c), chip, (x, y, 1 - c))
                                                          for chip in range(4)])


def _chip_exchange(parts):
    def route(x, y, c):
        chips = [(1 - x, y), (x, 1 - y), (1 - x, 1 - y)]
        return [(2 * chip[0] + chip[1], q, (*chip, c)) for q, chip in enumerate(chips)]
    return _direct_exchange(parts, 3, 3, route)


def _both(ex1, ex2):
    n1, s1 = len(ex1.arrays), len(ex1.sems)

    def halves(ins, outs, sems):
        return (ins[:n1], outs[:n1], sems[:s1]), (ins[n1:], outs[n1:], sems[s1:])

    def start(ins, outs, sems):
        h1, h2 = halves(ins, outs, sems)
        ex1.start(*h1)
        ex2.start(*h2)

    def relay(ins, outs, sems):
        h1, h2 = halves(ins, outs, sems)
        ex1.relay(*h1)
        ex2.relay(*h2)

    def finish(ins, outs, sems):
        h1, h2 = halves(ins, outs, sems)
        ex1.finish(*h1)
        ex2.finish(*h2)

    return _Exchange(list(ex1.arrays) + list(ex2.arrays), list(ex1.out_shape) + list(ex2.out_shape),
                     list(ex1.sems) + list(ex2.sems), start, finish, relay)


def _run_exchange(ex, name):
    n = len(ex.arrays)

    def body(*refs):
        ins, outs, sems = refs[:n], refs[n:2 * n], refs[2 * n:]
        ex.start(ins, outs, sems)
        ex.relay(ins, outs, sems)
        ex.finish(ins, outs, sems)

    return list(pl.pallas_call(body, name=name, in_specs=[_ANY] * n, out_specs=[_ANY] * n, out_shape=ex.out_shape,
                               scratch_shapes=ex.sems)(*ex.arrays))


def _call(body, name, grid, in_specs, out_specs, out_shape, args, scratch_shapes=(), ride=None):
    if ride is None:
        sem = ("arbitrary",) * len(grid)
        return pl.pallas_call(body, name=name, grid=grid, in_specs=in_specs, out_specs=out_specs, out_shape=out_shape,
                              scratch_shapes=list(scratch_shapes), compiler_params=_cp(sem))(*args), None
    n_in, n_out, n_scr, n_x = len(in_specs), len(out_specs), len(scratch_shapes), len(ride.arrays)

    def wrapped(*refs):
        ins, refs = refs[:n_in], refs[n_in:]
        x_in, refs = refs[:n_x], refs[n_x:]
        outs, refs = refs[:n_out], refs[n_out:]
        x_out, refs = refs[:n_x], refs[n_x:]
        scr, sems = refs[:n_scr], refs[n_scr:]
        step, total = pl.program_id(0), grid[0]
        for a in range(1, len(grid)):
            step, total = step * grid[a] + pl.program_id(a), total * grid[a]
        @pl.when(step == 0)
        def _():
            ride.start(x_in, x_out, sems)

        @pl.when(step == max(total - 2, 0))
        def _():
            ride.relay(x_in, x_out, sems)

        body(*ins, *outs, *scr)

        @pl.when(step == total - 1)
        def _():
            ride.finish(x_in, x_out, sems)

    res = pl.pallas_call(
        wrapped, name=name, grid=grid, in_specs=list(in_specs) + [_ANY] * n_x,
        out_specs=list(out_specs) + [_ANY] * n_x, out_shape=list(out_shape) + list(ride.out_shape),
        scratch_shapes=list(scratch_shapes) + list(ride.sems),
        compiler_params=_cp(("arbitrary",) * len(grid)))(*args, *ride.arrays)
    return res[:n_out], list(res[n_out:])


def _pair_sum(g, ra, core, out_dtype, name):
    _, r, C = g.shape
    tr = _tile(r, 576)

    def body(c_ref, g_ref, ra_ref, p_ref):
        p_ref[...] = (g_ref[...] + ra_ref[...]).astype(p_ref.dtype)

    return pl.pallas_call(
        body,
        name=name,
        grid_spec=pltpu.PrefetchScalarGridSpec(
            num_scalar_prefetch=1,
            grid=(4, r // tr),
            in_specs=[pl.BlockSpec((1, tr, C), lambda j, t, c_ref: (2 * j + c_ref[0], t, 0)),
                      pl.BlockSpec((1, tr, C), lambda j, t, c_ref: (j, t, 0))],
            out_specs=pl.BlockSpec((1, tr, C), lambda j, t, c_ref: (j, t, 0)),
        ),
        out_shape=jax.ShapeDtypeStruct((4, r, C), out_dtype),
        compiler_params=_cp(("parallel", "parallel")),
    )(core, g, ra)


def _chip_sum(p, rb, chip, name):
    _, r, C = p.shape
    tr = _tile(r, 576)

    def body(c_ref, p_ref, rb_ref, o_ref):
        acc = p_ref[0].astype(F32) + rb_ref[0].astype(F32)
        acc = acc + rb_ref[1].astype(F32)
        o_ref[...] = acc + rb_ref[2].astype(F32)

    return pl.pallas_call(
        body,
        name=name,
        grid_spec=pltpu.PrefetchScalarGridSpec(
            num_scalar_prefetch=1,
            grid=(r // tr,),
            in_specs=[pl.BlockSpec((1, tr, C), lambda t, c_ref: (c_ref[0], t, 0)),
                      pl.BlockSpec((3, tr, C), lambda t, c_ref: (0, t, 0))],
            out_specs=pl.BlockSpec((tr, C), lambda t, c_ref: (t, 0)),
        ),
        out_shape=jax.ShapeDtypeStruct((r, C), F32),
        compiler_params=_cp(("parallel",)),
    )(chip, p, rb)


def _adamw(g, w, m, v, name):
    R, C = g.shape
    tr = _tile(R, 256)
    c1 = 1.0 - ADAM_B1 ** ADAM_STEP
    c2 = 1.0 - ADAM_B2 ** ADAM_STEP

    def body(g_ref, w_ref, m_ref, v_ref, d_ref, nm_ref, nv_ref):
        gv = g_ref[...]
        nm = ADAM_B1 * m_ref[...] + (1.0 - ADAM_B1) * gv
        nv = ADAM_B2 * v_ref[...] + (1.0 - ADAM_B2) * (gv * gv)
        nm_ref[...] = nm
        nv_ref[...] = nv
        d_ref[...] = -ADAM_LR * ((nm / c1) / (jnp.sqrt(nv / c2) + ADAM_EPS) + ADAM_WD * w_ref[...])

    blk = pl.BlockSpec((tr, C), lambda i: (i, 0))
    return pl.pallas_call(
        body, name=name, grid=(R // tr,), in_specs=[blk] * 4, out_specs=[blk] * 3,
        out_shape=[jax.ShapeDtypeStruct((R, C), F32)] * 3, compiler_params=_cp(("parallel",)),
    )(g, w, m, v)


_SMALL = ("norm_w", "q_norm_w", "k_norm_w", "sinks", "a_re", "a_im", "log_step", "b_re", "b_im", "c_re", "c_im",
          "d_skip", "b_glu", "attn_out_norm_w", "ssm_out_norm_w")
_WEIGHTS = ("norm_w", "w_in", "q_norm_w", "k_norm_w", "sinks", "a_re", "a_im", "log_step", "b_re", "b_im", "c_re",
            "c_im", "d_skip", "w_glu", "b_glu", "attn_out_norm_w", "ssm_out_norm_w", "w_out")
_SMALL_2D = dict(norm_w=(1, 2048), q_norm_w=(1, 64), k_norm_w=(1, 64), sinks=(1, 16), a_re=(64, 64), a_im=(64, 64),
                 log_step=(1, 64), b_re=(1024, 64), b_im=(1024, 64), c_re=(1024, 64), c_im=(1024, 64),
                 d_skip=(1, 1024), b_glu=(1, 1024), attn_out_norm_w=(1, 1024), ssm_out_norm_w=(1, 1024))
_P_MINOR = ("b_re", "b_im")


def _flat_form(n, t):
    return t.transpose(0, 2, 1) if n in _P_MINOR else t


def _own_form(n, t, shape):
    if n in _P_MINOR:
        return t.reshape(shape[0], shape[2], shape[1]).transpose(0, 2, 1)
    return t.reshape(shape)


def _slab_rows(n):
    return -(-n // 1024) * 8


_PACK_ROWS = 2304


_LOSS_ROW = 2192


def _pack(d, loss):
    parts = []
    for n in _SMALL:
        flat = _flat_form(n, d[n]).reshape(-1).astype(F32)
        rows = _slab_rows(flat.shape[0])
        parts.append(jnp.pad(flat, (0, rows * 128 - flat.shape[0])).reshape(rows, 128))
    assert sum(p.shape[0] for p in parts) == _LOSS_ROW
    parts.append(jnp.pad(loss.reshape(1, 1), ((0, _PACK_ROWS - _LOSS_ROW - 1), (0, 127))))
    return jnp.concatenate(parts, axis=0)


def _unpack(packed, like):
    out, off = {}, 0
    for n in _SMALL:
        size = math.prod(like[n].shape)
        rows = _slab_rows(size)
        out[n] = _own_form(n, packed[off:off + rows].reshape(-1)[:size], like[n].shape)
        off += rows
    return out


def _adamw_small(g, w, m, v):
    c1 = 1.0 - ADAM_B1 ** ADAM_STEP
    c2 = 1.0 - ADAM_B2 ** ADAM_STEP
    k = len(_SMALL)

    def body(*refs):
        ins, outs = refs[:4 * k], refs[4 * k:]
        for j in range(k):
            gv, wv, mv, vv = (ins[q * k + j][...] for q in range(4))
            nm = ADAM_B1 * mv + (1.0 - ADAM_B1) * gv
            nv = ADAM_B2 * vv + (1.0 - ADAM_B2) * (gv * gv)
            outs[j][...] = -ADAM_LR * ((nm / c1) / (jnp.sqrt(nv / c2) + ADAM_EPS) + ADAM_WD * wv)
            outs[k + j][...] = nm
            outs[2 * k + j][...] = nv

    args = [_flat_form(n, d[n]).reshape(_SMALL_2D[n]) for d in (g, w, m, v) for n in _SMALL]
    shapes = [jax.ShapeDtypeStruct(_SMALL_2D[n], F32) for _ in range(3) for n in _SMALL]
    outs = pl.pallas_call(body, name="adamw_small", out_shape=shapes, compiler_params=_cp())(*args)
    res = []
    for q in range(3):
        res.append({n: _own_form(n, outs[q * k + j], w[n].shape) for j, n in enumerate(_SMALL)})
    return res


def kernel(x, positions, norm_w, w_in, q_norm_w, k_norm_w, sinks, a_re, a_im, log_step, b_re, b_im, c_re, c_im, d_skip, w_glu, b_glu, attn_out_norm_w, ssm_out_norm_w, w_out, loss_target, m_norm_w, m_w_in, m_q_norm_w, m_k_norm_w, m_sinks, m_a_re, m_a_im, m_log_step, m_b_re, m_b_im, m_c_re, m_c_im, m_d_skip, m_w_glu, m_b_glu, m_attn_out_norm_w, m_ssm_out_norm_w, m_w_out, v_norm_w, v_w_in, v_q_norm_w, v_k_norm_w, v_sinks, v_a_re, v_a_im, v_log_step, v_b_re, v_b_im, v_c_re, v_c_im, v_d_skip, v_w_glu, v_b_glu, v_attn_out_norm_w, v_ssm_out_norm_w, v_w_out):
    w = dict(norm_w=norm_w, w_in=w_in, q_norm_w=q_norm_w, k_norm_w=k_norm_w, sinks=sinks, a_re=a_re, a_im=a_im,
             log_step=log_step, b_re=b_re, b_im=b_im, c_re=c_re, c_im=c_im, d_skip=d_skip, w_glu=w_glu, b_glu=b_glu,
             attn_out_norm_w=attn_out_norm_w, ssm_out_norm_w=ssm_out_norm_w, w_out=w_out)
    m = dict(norm_w=m_norm_w, w_in=m_w_in, q_norm_w=m_q_norm_w, k_norm_w=m_k_norm_w, sinks=m_sinks, a_re=m_a_re,
             a_im=m_a_im, log_step=m_log_step, b_re=m_b_re, b_im=m_b_im, c_re=m_c_re, c_im=m_c_im, d_skip=m_d_skip,
             w_glu=m_w_glu, b_glu=m_b_glu, attn_out_norm_w=m_attn_out_norm_w, ssm_out_norm_w=m_ssm_out_norm_w,
             w_out=m_w_out)
    v = dict(norm_w=v_norm_w, w_in=v_w_in, q_norm_w=v_q_norm_w, k_norm_w=v_k_norm_w, sinks=v_sinks, a_re=v_a_re,
             a_im=v_a_im, log_step=v_log_step, b_re=v_b_re, b_im=v_b_im, c_re=v_c_re, c_im=v_c_im, d_skip=v_d_skip,
             w_glu=v_w_glu, b_glu=v_b_glu, attn_out_norm_w=v_attn_out_norm_w, ssm_out_norm_w=v_ssm_out_norm_w,
             w_out=v_w_out)
    core = lax.axis_index("c").astype(jnp.int32).reshape(1)
    chip = (2 * lax.axis_index("x") + lax.axis_index("y")).astype(jnp.int32).reshape(1)

    loss, grad_x, grads = _step(x[0], positions[0], loss_target[0], w, core, chip)
    delta, new_m, new_v = {}, {}, {}
    for n in ("w_glu", "w_out"):
        delta[n], new_m[n], new_v[n] = _adamw(grads[n], w[n], m[n], v[n], f"adamw_{n}")
    d_t, m_t, v_t = _adamw(grads["w_in"].T, w["w_in"].T, m["w_in"].T, v["w_in"].T, "adamw_w_in")
    delta["w_in"], new_m["w_in"], new_v["w_in"] = d_t.T, m_t.T, v_t.T
    d_s, m_s, v_s = _adamw_small(grads, w, m, v)
    delta.update(d_s)
    new_m.update(m_s)
    new_v.update(v_s)

    return (loss, grad_x[None], *[grads[n] for n in _WEIGHTS], *[delta[n] for n in _WEIGHTS],
            *[new_m[n] for n in _WEIGHTS], *[new_v[n] for n in _WEIGHTS])
```

```python
import functools
import math

import jax
import jax.numpy as jnp
from jax import lax
from jax.experimental import pallas as pl
from jax.experimental.pallas import tpu as pltpu

F32 = jnp.float32
BF16 = jnp.bfloat16

D_MODEL = 2048
ATTN_W = 1024
KV_W = 256
SSM_W = 1024
HEAD_DIM = 64
N_HEADS = 16
N_KV = 4
KV_REP = 4
IN_W = 4608
BLOCK = 128
ROPE_THETA = 10000.0
NORM_EPS = 1e-6
SSM_G = 64
SSM_P = 64
SSM_H = 16
CHUNK = 16
CW = CHUNK * SSM_H
N_DEV = 8

ADAM_LR = 0.001
ADAM_B1 = 0.9
ADAM_B2 = 0.999
ADAM_EPS = 1e-08
ADAM_WD = 0.01
ADAM_STEP = 10

VMEM_LIMIT = 56 * 1024 * 1024
MESH = pl.DeviceIdType.MESH


def _cp(sem=None):
    if sem is None:
        return pltpu.CompilerParams(vmem_limit_bytes=VMEM_LIMIT)
    return pltpu.CompilerParams(vmem_limit_bytes=VMEM_LIMIT, dimension_semantics=sem)


def _sigmoid(x):
    return 0.5 * jnp.tanh(0.5 * x) + 0.5


def _silu(x):
    return x * _sigmoid(x)


def _dsilu(x):
    s = _sigmoid(x)
    return s * (1.0 + x * (1.0 - s))


_GELU_C = math.sqrt(2.0 / math.pi)


def _gelu(y):
    t = jnp.tanh(_GELU_C * (y + 0.044715 * y * y * y))
    return 0.5 * y * (1.0 + t)


def _dgelu(y):
    t = jnp.tanh(_GELU_C * (y + 0.044715 * y * y * y))
    return 0.5 * (1.0 + t) + 0.5 * y * (1.0 - t * t) * _GELU_C * (1.0 + 3.0 * 0.044715 * y * y)


def _tile(n, want):
    if n <= want:
        return n
    for t in range(want - want % 16, 0, -16):
        if n % t == 0:
            return t
    raise ValueError((n, want))


def _mm(a, b, mode, out_dtype, name, tm=512, tn=1024, add=None, ride=None, panel=None):
    if mode == "nn":
        (M, K), (K2, N) = a.shape, b.shape
    elif mode == "nt":
        (M, K), (N, K2) = a.shape, b.shape
    else:
        (K, M), (K2, N) = a.shape, b.shape
    assert K == K2
    tm, tn = _tile(M, tm), _tile(N, tn)
    p0 = 0
    if panel is not None:
        assert mode != "nt" and add is None
        p0, N = panel, tn
    dn = {"nn": _NN, "nt": _NT, "tn": _TN}[mode]

    def body(a_ref, b_ref, *rest):
        o_ref = rest[-1]
        acc = lax.dot_general(a_ref[...].astype(BF16), b_ref[...].astype(BF16), dn, preferred_element_type=F32)
        if add is not None:
            acc = acc + rest[0][...]
        o_ref[...] = acc.astype(o_ref.dtype)

    a_spec = pl.BlockSpec((K, tm), lambda j, i: (0, i)) if mode == "tn" else pl.BlockSpec((tm, K), lambda j, i: (i, 0))
    b_spec = (pl.BlockSpec((tn, K), lambda j, i: (j, 0)) if mode == "nt"
              else pl.BlockSpec((K, tn), lambda j, i: (0, j + p0)))
    o_spec = pl.BlockSpec((tm, tn), lambda j, i: (i, j))
    extra = () if add is None else (add,)
    if ride is not None:
        (out,), landed = _call(body, name, (N // tn, M // tm), [a_spec, b_spec] + [o_spec] * len(extra), [o_spec],
                               [jax.ShapeDtypeStruct((M, N), out_dtype)], (a, b, *extra), ride=ride)
        return out, landed
    return pl.pallas_call(
        body,
        name=name,
        grid=(N // tn, M // tm),
        in_specs=[a_spec, b_spec] + [o_spec] * len(extra),
        out_specs=o_spec,
        out_shape=jax.ShapeDtypeStruct((M, N), out_dtype),
        compiler_params=_cp(("parallel", "parallel")),
    )(a, b, *extra)


_CHIP_ORDER = (0, 2, 1, 3)


def _rms_inproj_gather(x, norm_w, wt_shard, chip):
    L = x.shape[0]
    tm = _tile(L, 512)
    ni = L // tm
    r = IN_W // N_DEV
    tn = 2 * r

    def body(chip_ref, x_ref, nw_ref, shard, proj_ref, hn_ref, wt_hbm, hn_scr, w_scr, send_sems, recv_sems, loc_sems):
        jc, i = pl.program_id(0), pl.program_id(1)
        xx, yy, c = lax.axis_index("x"), lax.axis_index("y"), lax.axis_index("c")
        me, sibling = (xx, yy, c), (xx, yy, 1 - c)
        chips = [(1 - xx, yy), (xx, 1 - yy), (1 - xx, 1 - yy)]

        def slot(dev):
            return wt_hbm.at[4 * dev[0] + 2 * dev[1] + dev[2]]

        def copy(q, block, to, src=None):
            return pltpu.make_async_remote_copy(
                src_ref=slot(block) if src is None else src, dst_ref=slot(block),
                send_sem=send_sems.at[q], recv_sem=recv_sems.at[q], device_id=to, device_id_type=MESH)

        def rows_of(buf, core):
            return w_scr.at[buf, pl.ds(pl.multiple_of(core * r, 16), r)]

        mine = pltpu.make_async_copy(shard, slot(me), loc_sems.at[0])
        sends = [copy(0, me, sibling, src=shard)] + [copy(1 + j, me, (*ch, c), src=shard) for j, ch in enumerate(chips)]
        first = jnp.logical_and(jc == 0, i == 0)

        @pl.when(first)
        def _():
            mine.start()
            for cp in sends[:3]:
                cp.start()
            own = pltpu.make_async_copy(shard, rows_of(0, c), loc_sems.at[1])
            own.start()
            copy(0, sibling, me).wait_recv()
            sib = pltpu.make_async_copy(slot(sibling), rows_of(0, 1 - c), loc_sems.at[2])
            sib.start()
            own.wait()
            sib.wait()

        def take_direct(j, ch):
            copy(1 + j, (*ch, c), me).wait_recv()
            copy(4 + j, (*ch, c), sibling).start()
            if j == 0:
                sends[1].wait_send()
                sends[2].wait_send()
                sends[3].start()
            pltpu.make_async_copy(slot((*ch, c)), rows_of((1 + j) % 2, c), loc_sems.at[1]).start()

        for j, ch in enumerate(chips):
            early = jnp.logical_and(jc == j, i == ni // 2) if j > 0 else jnp.logical_and(jc == 1, i == 0)

            @pl.when(early)
            def _(j=j, ch=ch):
                take_direct(j, ch)

            @pl.when(jnp.logical_and(jc == 1 + j, i == 0))
            def _(j=j, ch=ch):
                buf = (1 + j) % 2
                copy(4 + j, (*ch, 1 - c), me).wait_recv()
                passed = pltpu.make_async_copy(slot((*ch, 1 - c)), rows_of(buf, 1 - c), loc_sems.at[2])
                passed.start()
                pltpu.make_async_copy(slot((*ch, c)), rows_of(buf, c), loc_sems.at[1]).wait()
                passed.wait()

        rows = pl.ds(pl.multiple_of(i * tm, tm), tm)

        @pl.when(jc == 0)
        def _():
            xv = x_ref[...]
            rstd = lax.rsqrt(jnp.mean(xv * xv, axis=-1, keepdims=True) + NORM_EPS)
            hn = (xv * rstd * nw_ref[...]).astype(BF16)
            hn_scr[rows, :] = hn
            hn_ref[...] = hn

        for buf in range(2):
            @pl.when(jc % 2 == buf)
            def _(buf=buf):
                proj_ref[...] = lax.dot_general(hn_scr[rows, :], w_scr[buf], _NT, preferred_element_type=F32)

        @pl.when(jnp.logical_and(jc == 3, i == ni - 1))
        def _():
            sends[0].wait_send()
            sends[3].wait_send()
            for j, ch in enumerate(chips):
                copy(4 + j, (*ch, c), sibling).wait_send()
            mine.wait()

    def tile_of(jc, chip_ref):
        mask = jnp.where(jc == 1, _CHIP_ORDER[1], jnp.where(jc == 2, _CHIP_ORDER[2], jnp.where(jc == 3, _CHIP_ORDER[3], 0)))
        return jnp.bitwise_xor(chip_ref[0], mask)

    held = lambda jc, i: jnp.where(jc == 0, i, ni - 1)
    return pl.pallas_call(
        body,
        name="rms_inproj_gather",
        grid_spec=pltpu.PrefetchScalarGridSpec(
            num_scalar_prefetch=1,
            grid=(4, ni),
            in_specs=[pl.BlockSpec((tm, D_MODEL), lambda jc, i, ch: (held(jc, i), 0)),
                      pl.BlockSpec((1, D_MODEL), lambda jc, i, ch: (0, 0)), _ANY],
            out_specs=[pl.BlockSpec((tm, tn), lambda jc, i, ch: (i, tile_of(jc, ch))),
                       pl.BlockSpec((tm, D_MODEL), lambda jc, i, ch: (held(jc, i), 0)), _ANY],
            scratch_shapes=[pltpu.VMEM((L, D_MODEL), BF16), pltpu.VMEM((2, tn, D_MODEL), BF16),
                            pltpu.SemaphoreType.DMA((7,)), pltpu.SemaphoreType.DMA((7,)), pltpu.SemaphoreType.DMA((3,))],
        ),
        out_shape=[jax.ShapeDtypeStruct((L, IN_W), F32), jax.ShapeDtypeStruct((L, D_MODEL), BF16),
                   jax.ShapeDtypeStruct((N_DEV, r, D_MODEL), BF16)],
        compiler_params=_cp(("arbitrary", "arbitrary")),
    )(chip, x, norm_w.reshape(1, D_MODEL), wt_shard)


def _seg_sum(v):
    a = lax.broadcasted_iota(jnp.int32, (128, 128), 0) // HEAD_DIM
    b = lax.broadcasted_iota(jnp.int32, (128, 128), 1) // HEAD_DIM
    ones = jnp.where(a == b, 1.0, 0.0).astype(BF16)
    hi = v.astype(BF16)
    lo = (v - hi.astype(F32)).astype(BF16)
    return jnp.dot(hi, ones, preferred_element_type=F32) + jnp.dot(lo, ones, preferred_element_type=F32)


def _rot_half(t):
    lane = lax.broadcasted_iota(jnp.int32, t.shape, 1)
    return jnp.where(lane % HEAD_DIM < HEAD_DIM // 2, pltpu.roll(t, 128 - HEAD_DIM // 2, 1),
                     pltpu.roll(t, HEAD_DIM // 2, 1))


def _norm_rope(raw, w, cos, sin):
    r = lax.rsqrt(_seg_sum(raw * raw) * (1.0 / HEAD_DIM) + NORM_EPS)
    tn = raw * r * w
    return r, tn * cos + _rot_half(tn) * sin


def _norm_rope_bwd(d_rot, raw, w, cos, sin):
    r = lax.rsqrt(_seg_sum(raw * raw) * (1.0 / HEAD_DIM) + NORM_EPS)
    d_tn = d_rot * cos + _rot_half(d_rot * sin)
    xh = raw * r
    gw = d_tn * w
    d_raw = r * (gw - xh * (_seg_sum(gw * xh) * (1.0 / HEAD_DIM)))
    return d_raw, d_tn * xh


def _band_mask2(has_prev):
    qi = lax.broadcasted_iota(jnp.int32, (2 * BLOCK, 2 * BLOCK), 0) % BLOCK + BLOCK
    kj = lax.broadcasted_iota(jnp.int32, (2 * BLOCK, 2 * BLOCK), 1)
    rel = qi - kj
    return (rel >= 0) & (rel < BLOCK) & ((kj >= BLOCK) | has_prev)


def _half_tiles(pair):
    lo = lax.broadcasted_iota(jnp.int32, pair.shape, 1) < HEAD_DIM
    sw = pltpu.roll(pair, HEAD_DIM, 1)
    z = jnp.zeros_like(pair)
    return (jnp.where(lo, pair, z).astype(BF16), jnp.where(lo, z, sw).astype(BF16),
            jnp.where(lo, sw, z).astype(BF16), jnp.where(lo, z, pair).astype(BF16))


def _two_rows(top, bottom):
    row = lax.broadcasted_iota(jnp.int32, (2 * BLOCK, 1), 0)
    return jnp.where(row < BLOCK, top, bottom)


def _lane_col(mat, h):
    lane = lax.broadcasted_iota(jnp.int32, mat.shape, 1)
    return jnp.sum(jnp.where(lane == h, mat, 0.0), axis=1, keepdims=True)


_SCALE = 1.0 / math.sqrt(HEAD_DIM)
_NT = (((1,), (1,)), ((), ()))
_NN = (((1,), (0,)), ((), ()))
_TN = (((0,), (0,)), ((), ()))


def _qk_prep(proj, tab, qw, kw):
    L = proj.shape[0]
    tm = _tile(L, 512)

    def body(q_ref, k_ref, t_ref, qw_ref, kw_ref, qo_ref, ko_ref):
        cos, sin = t_ref[:, :128], t_ref[:, 128:]
        for c in range(ATTN_W // 128):
            _, qr = _norm_rope(q_ref[:, c * 128:(c + 1) * 128], qw_ref[...], cos, sin)
            qo_ref[:, c * 128:(c + 1) * 128] = (qr * _SCALE).astype(BF16)
        for c in range(KV_W // 128):
            _, kr = _norm_rope(k_ref[:, c * 128:(c + 1) * 128], kw_ref[...], cos, sin)
            ko_ref[:, c * 128:(c + 1) * 128] = kr.astype(BF16)

    row = pl.BlockSpec((1, 128), lambda i: (0, 0))
    return pl.pallas_call(
        body,
        name="qk_prep",
        grid=(L // tm,),
        in_specs=[pl.BlockSpec((tm, ATTN_W), lambda i: (i, 0)), pl.BlockSpec((tm, KV_W), lambda i: (i, 4)),
                  pl.BlockSpec((tm, 256), lambda i: (i, 0)), row, row],
        out_specs=[pl.BlockSpec((tm, ATTN_W), lambda i: (i, 0)), pl.BlockSpec((tm, KV_W), lambda i: (i, 0))],
        out_shape=[jax.ShapeDtypeStruct((L, ATTN_W), BF16), jax.ShapeDtypeStruct((L, KV_W), BF16)],
        compiler_params=_cp(("parallel",)),
    )(proj, proj, tab, jnp.tile(qw, 2).reshape(1, 128), jnp.tile(kw, 2).reshape(1, 128))


def _group_tiles(g, kt, vt):
    a, b = divmod(g, 2)
    return kt[a][2 * b], kt[a][2 * b + 1], vt[a][2 * b], vt[a][2 * b + 1]


def _attn_fwd(q, k, proj, sinks, ride):
    L = proj.shape[0]
    nb = L // BLOCK

    def body(q_ref, kc_ref, kp_ref, vc_ref, vp_ref, z0_ref, z1_ref, sink_ref, og_ref, o_ref, lse_ref):
        i = pl.program_id(0)
        mask = _band_mask2(i > 0)
        z = jnp.concatenate([z0_ref[...], z1_ref[...]], axis=1)
        lane = lax.broadcasted_iota(jnp.int32, (BLOCK, 128), 1)
        kt = [_half_tiles(jnp.concatenate([kp_ref[:, a * 128:(a + 1) * 128], kc_ref[:, a * 128:(a + 1) * 128]],
                                          axis=0).astype(F32)) for a in range(2)]
        vt = [_half_tiles(jnp.concatenate([vp_ref[:, a * 128:(a + 1) * 128], vc_ref[:, a * 128:(a + 1) * 128]],
                                          axis=0)) for a in range(2)]
        lse_mat = jnp.zeros((BLOCK, 128), F32)
        outs = []
        for g in range(N_KV):
            k_lo, k_hi, v_lo, v_hi = _group_tiles(g, kt, vt)
            q2 = jnp.concatenate([q_ref[:, 2 * g * 128:(2 * g + 1) * 128],
                                  q_ref[:, (2 * g + 1) * 128:(2 * g + 2) * 128]], axis=0)
            acc = jnp.zeros((2 * BLOCK, 128), F32)
            for half, (kh, vh) in enumerate(((k_lo, v_lo), (k_hi, v_hi))):
                h_top, h_bot = 4 * g + half, 4 * g + 2 + half
                s = jnp.where(mask, lax.dot_general(q2, kh, _NT, preferred_element_type=F32), -1e30)
                sink = _two_rows(sink_ref[h_top], sink_ref[h_bot])
                m = jnp.maximum(jnp.max(s, axis=-1, keepdims=True), sink)
                e = jnp.exp(s - m)
                den = jnp.sum(e, axis=-1, keepdims=True) + jnp.exp(sink - m)
                p = e * (1.0 / den)
                acc = acc + jnp.dot(p.astype(BF16), vh, preferred_element_type=F32)
                lse = m + jnp.log(den)
                lse_mat = jnp.where(lane == h_top, lse[:BLOCK], lse_mat)
                lse_mat = jnp.where(lane == h_bot, lse[BLOCK:], lse_mat)
            outs += [acc[:BLOCK], acc[BLOCK:]]
        o = jnp.concatenate(outs, axis=1)
        o_ref[...] = o
        og_ref[...] = o * _silu(z)
        lse_ref[...] = lse_mat

    prev = lambda i: jnp.maximum(i - 1, 0)
    return _call(
        body, "attn_fwd", (nb,),
        [pl.BlockSpec((BLOCK, ATTN_W), lambda i: (i, 0)),
         pl.BlockSpec((BLOCK, KV_W), lambda i: (i, 0)),
         pl.BlockSpec((BLOCK, KV_W), lambda i: (prev(i), 0)),
         pl.BlockSpec((BLOCK, KV_W), lambda i: (i, 5)),
         pl.BlockSpec((BLOCK, KV_W), lambda i: (prev(i), 5)),
         pl.BlockSpec((BLOCK, 512), lambda i: (i, 3)),
         pl.BlockSpec((BLOCK, 512), lambda i: (i, 4)),
         pl.BlockSpec(memory_space=pltpu.SMEM)],
        [pl.BlockSpec((BLOCK, ATTN_W), lambda i: (i, 0)),
         pl.BlockSpec((BLOCK, ATTN_W), lambda i: (i, 0)),
         pl.BlockSpec((BLOCK, 128), lambda i: (i, 0))],
        [jax.ShapeDtypeStruct((L, ATTN_W), F32), jax.ShapeDtypeStruct((L, ATTN_W), F32),
         jax.ShapeDtypeStruct((L, 128), F32)],
        (q, k, k, proj, proj, proj, proj, sinks), ride=ride)


def _attn_bwd(q, k, proj, sinks, d_o, o, lse, ride):
    L = proj.shape[0]
    nb = L // BLOCK

    def body(q_ref, kc_ref, kp_ref, vc_ref, vp_ref, do_ref, o_ref, lse_ref, sink_ref,
             dq_ref, dk_ref, dv_ref, gs_ref, ck_scr, cv_scr):
        i = pl.program_id(0)

        @pl.when(i == 0)
        def _():
            gs_ref[...] = jnp.zeros_like(gs_ref)
            ck_scr[...] = jnp.zeros_like(ck_scr)
            cv_scr[...] = jnp.zeros_like(cv_scr)

        @pl.when(i == nb)
        def _():
            dk_ref[...] = ck_scr[...]
            dv_ref[...] = cv_scr[...]

        @pl.when(i < nb)
        def _():
            mask = _band_mask2(i > 0)
            lane = lax.broadcasted_iota(jnp.int32, (1, 128), 1)
            lo = lax.broadcasted_iota(jnp.int32, (2 * BLOCK, 128), 1) < HEAD_DIM
            lse_c = lse_ref[...]
            kt = [_half_tiles(jnp.concatenate([kp_ref[:, a * 128:(a + 1) * 128], kc_ref[:, a * 128:(a + 1) * 128]],
                                              axis=0).astype(F32)) for a in range(2)]
            vt = [_half_tiles(jnp.concatenate([vp_ref[:, a * 128:(a + 1) * 128], vc_ref[:, a * 128:(a + 1) * 128]],
                                              axis=0)) for a in range(2)]
            gs = jnp.zeros((1, 128), F32)
            dq_parts = []
            dk_acc = [jnp.zeros((2 * BLOCK, 128), F32) for _ in range(2)]
            dv_acc = [jnp.zeros((2 * BLOCK, 128), F32) for _ in range(2)]
            for g in range(N_KV):
                a, b = divmod(g, 2)
                k_lo, k_hi, v_lo, v_hi = _group_tiles(g, kt, vt)
                t0, t1 = slice(2 * g * 128, (2 * g + 1) * 128), slice((2 * g + 1) * 128, (2 * g + 2) * 128)
                q2 = jnp.concatenate([q_ref[:, t0], q_ref[:, t1]], axis=0)
                do2 = jnp.concatenate([do_ref[:, t0], do_ref[:, t1]], axis=0)
                prod = do2 * jnp.concatenate([o_ref[:, t0], o_ref[:, t1]], axis=0)
                do2_b = do2.astype(BF16)
                dq2 = jnp.zeros((2 * BLOCK, 128), F32)
                dk_h, dv_h = [], []
                for half, (kh, vh) in enumerate(((k_lo, v_lo), (k_hi, v_hi))):
                    h_top, h_bot = 4 * g + half, 4 * g + 2 + half
                    lse = jnp.concatenate([_lane_col(lse_c, h_top), _lane_col(lse_c, h_bot)], axis=0)
                    sink = _two_rows(sink_ref[h_top], sink_ref[h_bot])
                    delta = jnp.sum(jnp.where(lo == (half == 0), prod, 0.0), axis=1, keepdims=True)
                    s = jnp.where(mask, lax.dot_general(q2, kh, _NT, preferred_element_type=F32), -1e30)
                    p = jnp.exp(s - lse)
                    dp = lax.dot_general(do2_b, vh, _NT, preferred_element_type=F32)
                    ds_b = (p * (dp - delta)).astype(BF16)
                    p_b = p.astype(BF16)
                    dq2 = dq2 + jnp.dot(ds_b, kh, preferred_element_type=F32)
                    dk_h.append(lax.dot_general(ds_b, q2, _TN, preferred_element_type=F32))
                    dv_h.append(lax.dot_general(p_b, do2_b, _TN, preferred_element_type=F32))
                    gsink = -jnp.exp(sink - lse) * delta
                    row = lax.broadcasted_iota(jnp.int32, (2 * BLOCK, 1), 0)
                    gs = gs + jnp.where(lane == h_top, jnp.sum(jnp.where(row < BLOCK, gsink, 0.0)), 0.0)
                    gs = gs + jnp.where(lane == h_bot, jnp.sum(jnp.where(row >= BLOCK, gsink, 0.0)), 0.0)
                dq_parts += [dq2[:BLOCK], dq2[BLOCK:]]
                for acc, parts in ((dk_acc, dk_h), (dv_acc, dv_h)):
                    t = jnp.where(lo, parts[0], parts[1])
                    t = t + pltpu.roll(t, HEAD_DIM, 1)
                    acc[a] = acc[a] + jnp.where(lo == (b == 0), t, 0.0)
            dq_ref[...] = jnp.concatenate(dq_parts, axis=1)
            dk_full = jnp.concatenate(dk_acc, axis=1)
            dv_full = jnp.concatenate(dv_acc, axis=1)
            dk_ref[...] = ck_scr[...] + dk_full[:BLOCK]
            dv_ref[...] = cv_scr[...] + dv_full[:BLOCK]
            ck_scr[...] = dk_full[BLOCK:]
            cv_scr[...] = dv_full[BLOCK:]
            gs_ref[...] += gs

    cur = lambda i: jnp.minimum(i, nb - 1)
    prev = lambda i: jnp.maximum(jnp.minimum(i, nb - 1) - 1, 0)
    done = lambda i: jnp.maximum(i - 1, 0)
    bs = pl.BlockSpec
    return _call(
        body, "attn_bwd", (nb + 1,),
        [bs((BLOCK, ATTN_W), lambda i: (cur(i), 0)),
         bs((BLOCK, KV_W), lambda i: (cur(i), 0)), bs((BLOCK, KV_W), lambda i: (prev(i), 0)),
         bs((BLOCK, KV_W), lambda i: (cur(i), 5)), bs((BLOCK, KV_W), lambda i: (prev(i), 5)),
         bs((BLOCK, ATTN_W), lambda i: (cur(i), 0)), bs((BLOCK, ATTN_W), lambda i: (cur(i), 0)),
         bs((BLOCK, 128), lambda i: (cur(i), 0)), bs(memory_space=pltpu.SMEM)],
        [bs((BLOCK, ATTN_W), lambda i: (cur(i), 0)),
         bs((BLOCK, KV_W), lambda i: (done(i), 0)), bs((BLOCK, KV_W), lambda i: (done(i), 0)),
         bs((1, 128), lambda i: (0, 0))],
        [jax.ShapeDtypeStruct((L, ATTN_W), F32), jax.ShapeDtypeStruct((L, KV_W), F32),
         jax.ShapeDtypeStruct((L, KV_W), F32), jax.ShapeDtypeStruct((1, 128), F32)],
        (q, k, k, proj, proj, d_o, o, lse, sinks),
        [pltpu.VMEM((BLOCK, KV_W), F32), pltpu.VMEM((BLOCK, KV_W), F32)], ride)


def _qk_prep_bwd(proj, tab, qw, kw, d_q, d_k, d_v, d_za, d_u, d_zs):
    L = proj.shape[0]
    tm = _tile(L, 512)
    z0 = ATTN_W + 2 * KV_W

    def body(q_ref, k_ref, t_ref, qw_ref, kw_ref, dq_ref, dk_ref, dv_ref, dza_ref, du_ref, dzs_ref,
             out_ref, gq_ref, gk_ref):
        i = pl.program_id(0)

        @pl.when(i == 0)
        def _():
            gq_ref[...] = jnp.zeros_like(gq_ref)
            gk_ref[...] = jnp.zeros_like(gk_ref)

        cos, sin = t_ref[:, :128], t_ref[:, 128:]
        gq = jnp.zeros((1, 128), F32)
        gk = jnp.zeros((1, 128), F32)
        for c in range(ATTN_W // 128):
            cs = slice(c * 128, (c + 1) * 128)
            d_raw, gw = _norm_rope_bwd(dq_ref[:, cs] * _SCALE, q_ref[:, cs], qw_ref[...], cos, sin)
            out_ref[:, cs] = d_raw.astype(BF16)
            gq = gq + jnp.sum(gw, axis=0, keepdims=True)
        for c in range(KV_W // 128):
            cs = slice(c * 128, (c + 1) * 128)
            d_raw, gw = _norm_rope_bwd(dk_ref[:, cs], k_ref[:, cs], kw_ref[...], cos, sin)
            out_ref[:, ATTN_W + c * 128:ATTN_W + (c + 1) * 128] = d_raw.astype(BF16)
            gk = gk + jnp.sum(gw, axis=0, keepdims=True)
        out_ref[:, ATTN_W + KV_W:z0] = dv_ref[...].astype(BF16)
        out_ref[:, z0:z0 + ATTN_W] = dza_ref[...]
        out_ref[:, z0 + ATTN_W:z0 + ATTN_W + SSM_W] = du_ref[...].astype(BF16)
        out_ref[:, z0 + ATTN_W + SSM_W:] = dzs_ref[...]
        gq_ref[...] += gq
        gk_ref[...] += gk

    row = pl.BlockSpec((1, 128), lambda i: (0, 0))
    blk = lambda w, c: pl.BlockSpec((tm, w), lambda i: (i, c))
    return pl.pallas_call(
        body,
        name="qk_prep_bwd",
        grid=(L // tm,),
        in_specs=[blk(ATTN_W, 0), blk(KV_W, 4), blk(256, 0), row, row, blk(ATTN_W, 0), blk(KV_W, 0), blk(KV_W, 0),
                  blk(ATTN_W, 0), blk(SSM_W, 0), blk(SSM_W, 0)],
        out_specs=[blk(IN_W, 0), row, row],
        out_shape=[jax.ShapeDtypeStruct((L, IN_W), BF16), jax.ShapeDtypeStruct((1, 128), F32),
                   jax.ShapeDtypeStruct((1, 128), F32)],
        compiler_params=_cp(("arbitrary",)),
    )(proj, proj, tab, jnp.tile(qw, 2).reshape(1, 128), jnp.tile(kw, 2).reshape(1, 128), d_q, d_k, d_v,
      d_za, d_u, d_zs)


def _cmul(a, b):
    return a[0] * b[0] - a[1] * b[1], a[0] * b[1] + a[1] * b[0]


def _cmul_conj(a, b):
    return a[0] * b[0] + a[1] * b[1], a[1] * b[0] - a[0] * b[1]


def _cadd(a, b):
    return a[0] + b[0], a[1] + b[1]


def _dot3(a, b, dn):
    ah, bh = a.astype(BF16), b.astype(BF16)
    al, bl = (a - ah.astype(F32)).astype(BF16), (b - bh.astype(F32)).astype(BF16)
    d = lambda u, v: lax.dot_general(u, v, dn, preferred_element_type=F32)
    return d(ah, bh) + d(ah, bl) + d(al, bh)


def _s5_discretise(a_re, a_im, ls, cosx, sinx, bt):
    delta = jnp.exp(ls)
    er = jnp.exp(a_re * delta)
    lb = (er * cosx, er * sinx)
    den = a_re * a_re + a_im * a_im
    coef = _cmul_conj((lb[0] - 1.0, lb[1]), (a_re, a_im))
    coef = (coef[0] / den, coef[1] / den)
    return delta, lb, coef, den, _cmul(coef, bt)


def _powers(lb):
    pw = [(jnp.ones_like(lb[0]), jnp.zeros_like(lb[0]))]
    for _ in range(CHUNK):
        pw.append(_cmul(pw[-1], lb))
    return pw


def _block_rows(a, pw, idx):
    blocks = [_cmul(a, pw[i]) for i in idx]
    return (jnp.concatenate([b[0] for b in blocks], axis=-2), jnp.concatenate([b[1] for b in blocks], axis=-2))


def _block_rows_bwd(g, a, pw, idx, g_pw):
    g_a = (jnp.zeros_like(a[0]), jnp.zeros_like(a[0]))
    for j, i in enumerate(idx):
        gj = (g[0][..., j * SSM_H:(j + 1) * SSM_H, :], g[1][..., j * SSM_H:(j + 1) * SSM_H, :])
        g_a = _cadd(g_a, _cmul_conj(gj, pw[i]))
        gp = _cmul_conj(gj, a)
        g_pw[i] = _cadd(g_pw[i], (jnp.sum(gp[0], axis=-2, keepdims=True), jnp.sum(gp[1], axis=-2, keepdims=True)))
    return g_a


_IDX_S = [CHUNK - 1 - s for s in range(CHUNK)]
_IDX_O = [t + 1 for t in range(CHUNK)]
_IDX_K = list(range(CHUNK))
_PREP_IN = 9


def _prep_args(p):
    row = lambda t: t.reshape(SSM_G, 1, SSM_P)
    xi = p["a_im"] * jnp.exp(p["log_step"])[:, None]
    return (row(p["a_re"]), row(p["a_im"]), row(jnp.broadcast_to(p["log_step"][:, None], (SSM_G, SSM_P))),
            row(jnp.cos(xi)), row(jnp.sin(xi)), p["b_re"].transpose(0, 2, 1), p["b_im"].transpose(0, 2, 1),
            p["c_re"], p["c_im"])


PREP_GROUPS = 8


def _prep_specs():
    r1 = pl.BlockSpec((PREP_GROUPS, 1, SSM_P), lambda g: (g, 0, 0))
    r16 = pl.BlockSpec((PREP_GROUPS, SSM_H, SSM_P), lambda g: (g, 0, 0))
    return [r1] * 5 + [r16] * 4, r1, r16


def _ssm_prep(p):
    def one_group(q, are, aim, ls, cosx, sinx, btr, bti, cre, cim, mt_ref, s_ref, o_ref, a_ref):
        _, lb, _, _, bb = _s5_discretise(are[q], aim[q], ls[q], cosx[q], sinx[q], (btr[q], bti[q]))
        pw = _powers(lb)
        c = (cre[q], cim[q])
        sc = _block_rows(bb, pw, _IDX_S)
        ot = _block_rows(c, pw, _IDX_O)
        ok = _block_rows(c, pw, _IDX_K)
        s_ref[q] = jnp.concatenate([sc[0], sc[1]], axis=1).astype(BF16)
        o_ref[q] = jnp.concatenate([ot[0], -ot[1]], axis=1).astype(BF16)
        a_ref[q] = jnp.concatenate([pw[CHUNK][0], pw[CHUNK][1]], axis=1)
        kt = _dot3(jnp.concatenate([bb[0], -bb[1]], axis=1), jnp.concatenate([ok[0], ok[1]], axis=1), _NT)
        lane = lax.broadcasted_iota(jnp.int32, kt.shape, 1)
        for s in range(CHUNK):
            blk = kt if s == 0 else jnp.where(lane >= SSM_H * s, pltpu.roll(kt, SSM_H * s, 1), 0.0)
            mt_ref[q, s * SSM_H:(s + 1) * SSM_H, :] = blk.astype(BF16)

    def body(*refs):
        for q in range(PREP_GROUPS):
            one_group(q, *refs)

    in_specs, r1, _ = _prep_specs()
    g3 = lambda r, c: pl.BlockSpec((PREP_GROUPS, r, c), lambda g: (g, 0, 0))
    return pl.pallas_call(
        body,
        name="ssm_prep",
        grid=(SSM_G // PREP_GROUPS,),
        in_specs=in_specs,
        out_specs=[g3(CW, CW), g3(CW, 2 * SSM_P), g3(CW, 2 * SSM_P), g3(1, 2 * SSM_P)],
        out_shape=[jax.ShapeDtypeStruct((SSM_G, CW, CW), BF16), jax.ShapeDtypeStruct((SSM_G, CW, 2 * SSM_P), BF16),
                   jax.ShapeDtypeStruct((SSM_G, CW, 2 * SSM_P), BF16),
                   jax.ShapeDtypeStruct((SSM_G, 1, 2 * SSM_P), F32)],
        compiler_params=_cp(("parallel",)),
    )(*_prep_args(p))


def _ssm_prep_bwd(p, g_mt, g_scat, g_ocat, g_a16, ride):
    def body(are, aim, ls, cosx, sinx, btr, bti, cre, cim, gmt_ref, gs_ref, go_ref, ga_ref,
             g_are, g_aim, g_ls, g_btr, g_bti, g_cre, g_cim, ga1_scr, gb1_scr):
        lam = (are[...], aim[...])
        bt = (btr[...], bti[...])
        delta, lb, coef, den, bb = _s5_discretise(lam[0], lam[1], ls[...], cosx[...], sinx[...], bt)
        pw = _powers(lb)
        c = (cre[...], cim[...])
        ok = _block_rows(c, pw, _IDX_K)
        g_pw = [(jnp.zeros_like(lb[0]), jnp.zeros_like(lb[0])) for _ in range(CHUNK + 1)]
        lane = lax.broadcasted_iota(jnp.int32, (SSM_H, CW), 1)
        for q in range(PREP_GROUPS):
            g_kt = gmt_ref[q, :SSM_H, :]
            for s in range(1, CHUNK):
                blk = gmt_ref[q, s * SSM_H:(s + 1) * SSM_H, :]
                g_kt = g_kt + jnp.where(lane < CW - SSM_H * s, pltpu.roll(blk, CW - SSM_H * s, 1), 0.0)
            a1 = jnp.concatenate([bb[0][q], -bb[1][q]], axis=1)
            b1 = jnp.concatenate([ok[0][q], ok[1][q]], axis=1)
            ga1_scr[q] = _dot3(g_kt, b1, _NN)
            gb1_scr[q] = _dot3(g_kt, a1, _TN)
        g_a1, g_b1 = ga1_scr[...], gb1_scr[...]
        g_bb = (g_a1[..., :SSM_P], -g_a1[..., SSM_P:])
        g_c = _block_rows_bwd((g_b1[..., :SSM_P], g_b1[..., SSM_P:]), c, pw, _IDX_K, g_pw)
        gs = gs_ref[...]
        g_bb = _cadd(g_bb, _block_rows_bwd((gs[..., :SSM_P], gs[..., SSM_P:]), bb, pw, _IDX_S, g_pw))
        go = go_ref[...]
        g_c = _cadd(g_c, _block_rows_bwd((go[..., :SSM_P], -go[..., SSM_P:]), c, pw, _IDX_O, g_pw))
        ga = ga_ref[...]
        g_pw[CHUNK] = _cadd(g_pw[CHUNK], (ga[..., :SSM_P], ga[..., SSM_P:]))
        g_lb = (jnp.zeros_like(lb[0]), jnp.zeros_like(lb[0]))
        for l in range(CHUNK - 1, -1, -1):
            g_lb = _cadd(g_lb, _cmul_conj(g_pw[l + 1], pw[l]))
            g_pw[l] = _cadd(g_pw[l], _cmul_conj(g_pw[l + 1], lb))
        g_bt = _cmul_conj(g_bb, coef)
        gc = _cmul_conj(g_bb, bt)
        g_coef = (jnp.sum(gc[0], axis=-2, keepdims=True), jnp.sum(gc[1], axis=-2, keepdims=True))
        lam_den = (lam[0] / den, lam[1] / den)
        g_lb = _cadd(g_lb, _cmul(g_coef, lam_den))
        t = _cmul(_cmul_conj(g_coef, coef), lam_den)
        g_x = _cmul_conj(g_lb, lb)
        g_are[...] = g_x[0] * delta - t[0]
        g_aim[...] = g_x[1] * delta - t[1]
        g_ls[...] = (g_x[0] * lam[0] + g_x[1] * lam[1]) * delta
        g_btr[...] = g_bt[0]
        g_bti[...] = g_bt[1]
        g_cre[...] = g_c[0]
        g_cim[...] = g_c[1]

    in_specs, r1, r16 = _prep_specs()
    g3 = lambda r, c: pl.BlockSpec((PREP_GROUPS, r, c), lambda g: (g, 0, 0))
    rows = jax.ShapeDtypeStruct((SSM_G, 1, SSM_P), F32)
    mats = jax.ShapeDtypeStruct((SSM_G, SSM_H, SSM_P), F32)
    (g_are, g_aim, g_ls, g_btr, g_bti, g_cre, g_cim), landed = _call(
        body, "ssm_prep_bwd", (SSM_G // PREP_GROUPS,),
        in_specs + [g3(CW, CW), g3(CW, 2 * SSM_P), g3(CW, 2 * SSM_P), g3(1, 2 * SSM_P)],
        [r1] * 3 + [r16] * 4, [rows] * 3 + [mats] * 4, (*_prep_args(p), g_mt, g_scat, g_ocat, g_a16),
        [pltpu.VMEM((PREP_GROUPS, SSM_H, 2 * SSM_P), F32), pltpu.VMEM((PREP_GROUPS, CW, 2 * SSM_P), F32)], ride)
    grads = dict(a_re=g_are.reshape(SSM_G, SSM_P), a_im=g_aim.reshape(SSM_G, SSM_P),
                 log_step=jnp.sum(g_ls.reshape(SSM_G, SSM_P), axis=1),
                 b_re=g_btr.transpose(0, 2, 1), b_im=g_bti.transpose(0, 2, 1), c_re=g_cre, c_im=g_cim)
    return grads, landed


def _cmul_const(xv, ar, ai):
    return xv * ar + pltpu.roll(xv, SSM_P, 1) * ai


def _chunk_scan(inc, a_row, reverse):
    n = inc.shape[0]
    lane = lax.broadcasted_iota(jnp.int32, (1, 2 * SSM_P), 1)
    row = lax.broadcasted_iota(jnp.int32, inc.shape, 0)
    sign = jnp.where(lane < SSM_P, -1.0, 1.0)
    ar = jnp.where(lane < SSM_P, a_row, pltpu.roll(a_row, SSM_P, 1))
    ai = jnp.where(lane < SSM_P, pltpu.roll(a_row, SSM_P, 1), a_row)
    if reverse:
        ai = -ai
    xv = inc
    s = 1
    while s < n:
        if reverse:
            sh = jnp.where(row < n - s, pltpu.roll(xv, n - s, 0), 0.0)
        else:
            sh = jnp.where(row >= s, pltpu.roll(xv, s, 0), 0.0)
        xv = xv + _cmul_const(sh, ar, ai * sign)
        ar, ai = ar * ar - ai * ai, 2.0 * ar * ai
        s *= 2
    return xv


def _shift_rows(xv, reverse):
    n = xv.shape[0]
    row = lax.broadcasted_iota(jnp.int32, xv.shape, 0)
    if reverse:
        return jnp.where(row < n - 1, pltpu.roll(xv, n - 1, 0), 0.0)
    return jnp.where(row >= 1, pltpu.roll(xv, 1, 0), 0.0)


GB = 128 // SSM_H
U_COL0 = (ATTN_W + 2 * KV_W + ATTN_W) // 128


HALF = CHUNK // 2


def _chunk_perm():
    r = jnp.arange(HALF * 128)
    t, g8, h = r // 128, (r % 128) // SSM_H, r % SSM_H
    return ((g8 * 128 + t * SSM_H + h)[:, None] == jnp.arange(GB * 128)[None, :]).astype(BF16)


def _load_perm(p_hbm, p_scr, sem):
    @pl.when(pl.program_id(0) == 0)
    def _():
        cp = pltpu.make_async_copy(p_hbm, p_scr, sem)
        cp.start()
        cp.wait()


def _rows_to_chunks(pieces, perm):
    halves = [jnp.dot(jnp.concatenate(pieces[k * HALF:(k + 1) * HALF], axis=1).astype(BF16), perm,
                      preferred_element_type=F32).astype(BF16) for k in range(2)]
    return [jnp.concatenate([hv[:, g * 128:(g + 1) * 128] for hv in halves], axis=1) for g in range(GB)]


def _chunks_to_rows(groups, perm, two_pass):
    pieces = []
    for k in range(2):
        v = jnp.concatenate([gv[:, k * 128:(k + 1) * 128] for gv in groups], axis=1)
        hi = v.astype(BF16)
        out = lax.dot_general(hi, perm, _NT, preferred_element_type=F32)
        if two_pass:
            lo = (v - hi.astype(F32)).astype(BF16)
            out = out + lax.dot_general(lo, perm, _NT, preferred_element_type=F32)
        pieces += [out[:, t * 128:(t + 1) * 128] for t in range(HALF)]
    return pieces


def _ssm_fwd(proj, perm, mt, scat, ocat, a16, d_skip, ride):
    L = proj.shape[0]
    nc = L // CHUNK

    def body(u_ref, p_hbm, mt_ref, s_ref, o_ref, a_ref, d_ref, y_ref, yg_ref, h_ref, p_scr, sem):
        _load_perm(p_hbm, p_scr, sem)
        perm = p_scr[...]
        rows = [pl.ds(t, nc, stride=CHUNK) for t in range(CHUNK)]
        us = [u_ref[r, :] for r in rows]
        ua = _rows_to_chunks(us, perm)
        ys = []
        for g in range(GB):
            uv = ua[g]
            inc = jnp.dot(uv, s_ref[g], preferred_element_type=F32)
            hx = _shift_rows(_chunk_scan(inc, a_ref[g], False), False)
            h_ref[g] = hx
            ys.append(jnp.dot(uv, mt_ref[g], preferred_element_type=F32)
                      + lax.dot_general(hx.astype(BF16), o_ref[g], _NT, preferred_element_type=F32))
        yp = _chunks_to_rows(ys, perm, True)
        for t, r in enumerate(rows):
            y = yp[t] + d_ref[...] * us[t]
            y_ref[r, :] = y
            yg_ref[r, :] = _gelu(y)

    g3 = lambda r, c: pl.BlockSpec((GB, r, c), lambda g: (g, 0, 0))
    col = pl.BlockSpec((L, 128), lambda g: (0, g))
    return _call(
        body, "ssm_fwd", (SSM_G // GB,),
        [pl.BlockSpec((L, 128), lambda g: (0, U_COL0 + g)), _ANY,
         g3(CW, CW), g3(CW, 2 * SSM_P), g3(CW, 2 * SSM_P), g3(1, 2 * SSM_P),
         pl.BlockSpec((1, 128), lambda g: (0, g))],
        [col, col, g3(nc, 2 * SSM_P)],
        [jax.ShapeDtypeStruct((L, SSM_W), F32), jax.ShapeDtypeStruct((L, SSM_W), F32),
         jax.ShapeDtypeStruct((SSM_G, nc, 2 * SSM_P), F32)],
        (proj, perm, mt, scat, ocat, a16, d_skip.reshape(1, SSM_W)),
        [pltpu.VMEM((HALF * 128, GB * 128), BF16), pltpu.SemaphoreType.DMA], ride)


def _ssm_bwd(d_yg, y, proj, hx, perm, mt, scat, ocat, a16, d_skip, ride):
    L = proj.shape[0]
    nc = L // CHUNK

    def body(dg_ref, y_ref, u_ref, h_ref, p_hbm, mt_ref, s_ref, o_ref, a_ref, d_ref,
             du_ref, gmt_ref, gs_ref, go_ref, ga_ref, gd_ref, p_scr, sem):
        _load_perm(p_hbm, p_scr, sem)
        perm = p_scr[...]
        rows = [pl.ds(t, nc, stride=CHUNK) for t in range(CHUNK)]
        us = [u_ref[r, :] for r in rows]
        dys = [dg_ref[r, :] * _dgelu(y_ref[r, :]) for r in rows]
        gd = jnp.zeros((1, 128), F32)
        for uv, dy in zip(us, dys):
            gd = gd + jnp.sum(dy * uv, axis=0, keepdims=True)
        gd_ref[...] = gd
        ua = _rows_to_chunks(us, perm)
        dya = _rows_to_chunks(dys, perm)
        lane = lax.broadcasted_iota(jnp.int32, (1, 2 * SSM_P), 1)
        dus = []
        for g in range(GB):
            uv, dy, hx_v = ua[g], dya[g], h_ref[g]
            dh = jnp.dot(dy, o_ref[g], preferred_element_type=F32)
            dinc = _shift_rows(_chunk_scan(dh, a_ref[g], True), True)
            dinc_b = dinc.astype(BF16)
            dus.append(lax.dot_general(dy, mt_ref[g], _NT, preferred_element_type=F32)
                       + lax.dot_general(dinc_b, s_ref[g], _NT, preferred_element_type=F32))
            gmt_ref[g] = lax.dot_general(uv, dy, _TN, preferred_element_type=F32)
            gs_ref[g] = lax.dot_general(uv, dinc_b, _TN, preferred_element_type=F32)
            go_ref[g] = lax.dot_general(dy, hx_v.astype(BF16), _TN, preferred_element_type=F32)
            p1 = dinc * hx_v
            p2 = pltpu.roll(dinc, SSM_P, 1) * hx_v
            t1 = jnp.sum(p1 + pltpu.roll(p1, SSM_P, 1), axis=0, keepdims=True)
            t2 = jnp.sum(p2 - pltpu.roll(p2, SSM_P, 1), axis=0, keepdims=True)
            ga_ref[g] = jnp.where(lane < SSM_P, t1, pltpu.roll(t2, SSM_P, 1))
        dup = _chunks_to_rows(dus, perm, False)
        for t, r in enumerate(rows):
            du_ref[r, :] = dup[t] + d_ref[...] * dys[t]

    g3 = lambda r, c: pl.BlockSpec((GB, r, c), lambda g: (g, 0, 0))
    col = pl.BlockSpec((L, 128), lambda g: (0, g))
    row = pl.BlockSpec((1, 128), lambda g: (0, g))
    return _call(
        body, "ssm_bwd", (SSM_G // GB,),
        [col, col, pl.BlockSpec((L, 128), lambda g: (0, U_COL0 + g)), g3(nc, 2 * SSM_P), _ANY,
         g3(CW, CW), g3(CW, 2 * SSM_P), g3(CW, 2 * SSM_P), g3(1, 2 * SSM_P), row],
        [col, g3(CW, CW), g3(CW, 2 * SSM_P), g3(CW, 2 * SSM_P), g3(1, 2 * SSM_P), row],
        [jax.ShapeDtypeStruct((L, SSM_W), F32), jax.ShapeDtypeStruct((SSM_G, CW, CW), F32),
         jax.ShapeDtypeStruct((SSM_G, CW, 2 * SSM_P), F32), jax.ShapeDtypeStruct((SSM_G, CW, 2 * SSM_P), F32),
         jax.ShapeDtypeStruct((SSM_G, 1, 2 * SSM_P), F32), jax.ShapeDtypeStruct((1, SSM_W), F32)],
        (d_yg, y, proj, hx, perm, mt, scat, ocat, a16, d_skip.reshape(1, SSM_W)),
        [pltpu.VMEM((HALF * 128, GB * 128), BF16), pltpu.SemaphoreType.DMA], ride)


def _merge(og, yg, w_glu, proj, b_glu, wa, ws):
    L = og.shape[0]
    tm = _tile(L, 256)

    def body(og_ref, yg_ref, wg_ref, z0_ref, z1_ref, b_ref, wa_ref, ws_ref, m_ref, gp_ref):
        zs = jnp.concatenate([z0_ref[...], z1_ref[...]], axis=1)
        ygv = yg_ref[...]
        gpre = jnp.dot(ygv.astype(BF16), wg_ref[...], preferred_element_type=F32)
        gp_ref[...] = gpre
        os_ = ygv * _sigmoid(gpre + b_ref[...]) * _silu(zs)
        ogv = og_ref[...]
        ra = lax.rsqrt(jnp.mean(ogv * ogv, axis=-1, keepdims=True) + NORM_EPS)
        rs = lax.rsqrt(jnp.mean(os_ * os_, axis=-1, keepdims=True) + NORM_EPS)
        m_ref[:, :ATTN_W] = (ogv * ra * wa_ref[...]).astype(BF16)
        m_ref[:, ATTN_W:] = (os_ * rs * ws_ref[...]).astype(BF16)

    row = lambda w: pl.BlockSpec((1, w), lambda i: (0, 0))
    return pl.pallas_call(
        body,
        name="merge",
        grid=(L // tm,),
        in_specs=[pl.BlockSpec((tm, ATTN_W), lambda i: (i, 0)), pl.BlockSpec((tm, SSM_W), lambda i: (i, 0)),
                  pl.BlockSpec((SSM_W, SSM_W), lambda i: (0, 0)),
                  pl.BlockSpec((tm, 512), lambda i: (i, 7)), pl.BlockSpec((tm, 512), lambda i: (i, 8)),
                  row(SSM_W), row(ATTN_W), row(SSM_W)],
        out_specs=[pl.BlockSpec((tm, D_MODEL), lambda i: (i, 0)), pl.BlockSpec((tm, SSM_W), lambda i: (i, 0))],
        out_shape=[jax.ShapeDtypeStruct((L, D_MODEL), BF16), jax.ShapeDtypeStruct((L, SSM_W), F32)],
        compiler_params=_cp(("parallel",)),
    )(og, yg, w_glu, proj, proj, b_glu.reshape(1, SSM_W), wa.reshape(1, ATTN_W), ws.reshape(1, SSM_W))


def _outproj_loss(merged, w_out, x, target):
    L = x.shape[0]
    tm, tn = _tile(L, 512), 1024
    ni, nj = L // tm, D_MODEL // tn

    def body(m_ref, w_ref, x_ref, t_ref, d_ref, db_ref, l_ref):
        out = x_ref[...] + jnp.dot(m_ref[...], w_ref[...], preferred_element_type=F32)
        diff = out - t_ref[...]
        d = diff * (1.0 / D_MODEL)
        d_ref[...] = d
        db_ref[...] = d.astype(BF16)
        l_ref[...] = jnp.full((1, 8, 128), jnp.sum(diff * diff), F32)

    return pl.pallas_call(
        body,
        name="outproj_loss",
        grid=(nj, ni),
        in_specs=[pl.BlockSpec((tm, D_MODEL), lambda j, i: (i, 0)),
                  pl.BlockSpec((D_MODEL, tn), lambda j, i: (0, j)),
                  pl.BlockSpec((tm, tn), lambda j, i: (i, j)),
                  pl.BlockSpec((tm, tn), lambda j, i: (i, j))],
        out_specs=[pl.BlockSpec((tm, tn), lambda j, i: (i, j)), pl.BlockSpec((tm, tn), lambda j, i: (i, j)),
                   pl.BlockSpec((1, 8, 128), lambda j, i: (i * nj + j, 0, 0))],
        out_shape=[jax.ShapeDtypeStruct((L, D_MODEL), F32), jax.ShapeDtypeStruct((L, D_MODEL), BF16),
                   jax.ShapeDtypeStruct((ni * nj, 8, 128), F32)],
        compiler_params=_cp(("parallel", "parallel")),
    )(merged, w_out, x, target)


def _merge_bwd(d_out_b, w_out, og, o, yg, gpre, proj, b_glu, wa, ws):
    L = og.shape[0]
    tm = _tile(L, 256)

    def body(dout_ref, wo_ref, og_ref, o_ref, yg_ref, gp_ref, za0_ref, za1_ref, zs0_ref, zs1_ref, b_ref, wa_ref,
             ws_ref, do_ref, dza_ref, dzs_ref, dg_ref, dyg_ref, gwa_ref, gws_ref, gb_ref):
        i = pl.program_id(0)

        @pl.when(i == 0)
        def _():
            gwa_ref[...] = jnp.zeros_like(gwa_ref)
            gws_ref[...] = jnp.zeros_like(gws_ref)
            gb_ref[...] = jnp.zeros_like(gb_ref)

        dm = lax.dot_general(dout_ref[...], wo_ref[...], _NT, preferred_element_type=F32)
        za = jnp.concatenate([za0_ref[...], za1_ref[...]], axis=1)
        zs = jnp.concatenate([zs0_ref[...], zs1_ref[...]], axis=1)
        ogv, dma = og_ref[...], dm[:, :ATTN_W]
        ra = lax.rsqrt(jnp.mean(ogv * ogv, axis=-1, keepdims=True) + NORM_EPS)
        xh = ogv * ra
        gwa_ref[...] += jnp.sum(dma * xh, axis=0, keepdims=True)
        gx = dma * wa_ref[...]
        d_og = ra * (gx - xh * jnp.mean(gx * xh, axis=-1, keepdims=True))
        do_ref[...] = d_og * _silu(za)
        dza_ref[...] = (d_og * o_ref[...] * _dsilu(za)).astype(BF16)
        ygv = yg_ref[...]
        sg = _sigmoid(gp_ref[...] + b_ref[...])
        y2 = ygv * sg
        sz = _silu(zs)
        os_ = y2 * sz
        dms = dm[:, ATTN_W:]
        rs = lax.rsqrt(jnp.mean(os_ * os_, axis=-1, keepdims=True) + NORM_EPS)
        xs = os_ * rs
        gws_ref[...] += jnp.sum(dms * xs, axis=0, keepdims=True)
        gxs = dms * ws_ref[...]
        d_os = rs * (gxs - xs * jnp.mean(gxs * xs, axis=-1, keepdims=True))
        dzs_ref[...] = (d_os * y2 * _dsilu(zs)).astype(BF16)
        d_y2 = d_os * sz
        d_g = d_y2 * ygv * sg * (1.0 - sg)
        dg_ref[...] = d_g.astype(BF16)
        gb_ref[...] += jnp.sum(d_g, axis=0, keepdims=True)
        dyg_ref[...] = d_y2 * sg

    row = lambda w: pl.BlockSpec((1, w), lambda i: (0, 0))
    full = lambda w: pl.BlockSpec((tm, w), lambda i: (i, 0))
    half = lambda c: pl.BlockSpec((tm, 512), lambda i: (i, c))
    return pl.pallas_call(
        body,
        name="merge_bwd",
        grid=(L // tm,),
        in_specs=[full(D_MODEL), pl.BlockSpec((D_MODEL, D_MODEL), lambda i: (0, 0)),
                  full(ATTN_W), full(ATTN_W), full(SSM_W), full(SSM_W),
                  half(3), half(4), half(7), half(8), row(SSM_W), row(ATTN_W), row(SSM_W)],
        out_specs=[full(ATTN_W), full(ATTN_W), full(SSM_W), full(SSM_W), full(SSM_W),
                   row(ATTN_W), row(SSM_W), row(SSM_W)],
        out_shape=[jax.ShapeDtypeStruct((L, ATTN_W), F32), jax.ShapeDtypeStruct((L, ATTN_W), BF16),
                   jax.ShapeDtypeStruct((L, SSM_W), BF16), jax.ShapeDtypeStruct((L, SSM_W), BF16),
                   jax.ShapeDtypeStruct((L, SSM_W), F32),
                   jax.ShapeDtypeStruct((1, ATTN_W), F32), jax.ShapeDtypeStruct((1, SSM_W), F32),
                   jax.ShapeDtypeStruct((1, SSM_W), F32)],
        compiler_params=_cp(("arbitrary",)),
    )(d_out_b, w_out, og, o, yg, gpre, proj, proj, proj, proj, b_glu.reshape(1, SSM_W), wa.reshape(1, ATTN_W),
      ws.reshape(1, SSM_W))


def _rms_bwd_x(x, norm_w, d_hn, d_out, ride):
    L = x.shape[0]
    tm = _tile(L, 256)

    def body(x_ref, w_ref, dh_ref, do_ref, gx_ref, gw_ref):
        i = pl.program_id(0)

        @pl.when(i == 0)
        def _():
            gw_ref[...] = jnp.zeros_like(gw_ref)

        xv, dh = x_ref[...], dh_ref[...]
        r = lax.rsqrt(jnp.mean(xv * xv, axis=-1, keepdims=True) + NORM_EPS)
        xh = xv * r
        gw_ref[...] += jnp.sum(dh * xh, axis=0, keepdims=True)
        gx = dh * w_ref[...]
        gx_ref[...] = do_ref[...] + r * (gx - xh * jnp.mean(gx * xh, axis=-1, keepdims=True))

    blk = pl.BlockSpec((tm, D_MODEL), lambda i: (i, 0))
    row = pl.BlockSpec((1, D_MODEL), lambda i: (0, 0))
    return _call(body, "rms_bwd_x", (L // tm,), [blk, row, blk, blk], [blk, row],
                 [jax.ShapeDtypeStruct((L, D_MODEL), F32), jax.ShapeDtypeStruct((1, D_MODEL), F32)],
                 (x, norm_w.reshape(1, D_MODEL), d_hn, d_out), ride=ride)


def _rope_table(positions):
    lane = jnp.arange(256)
    inv_freq = ROPE_THETA ** (-(2 * (lane % (HEAD_DIM // 2))).astype(F32) / HEAD_DIM)
    ang = positions.astype(F32)[:, None] * inv_freq[None, :]
    sign = jnp.where(lane % HEAD_DIM < HEAD_DIM // 2, -1.0, 1.0)
    return jnp.where(lane < 128, jnp.cos(ang), sign * jnp.sin(ang))


def _step(x, positions, target, w, core, chip):
    small = {n: w[n] for n in _SMALL}
    tab = _rope_table(positions)
    mt_b, scat_b, ocat_b, a16 = _ssm_prep(small)
    perm = _chunk_perm()
    blocks = lambda t: t.reshape(N_DEV, t.shape[0] // N_DEV, t.shape[1])

    proj, hn, wt_in = _rms_inproj_gather(x, small["norm_w"], w["w_in"].T.astype(BF16), chip)
    wt_in = wt_in.reshape(IN_W, D_MODEL)
    q_rot, k_rot = _qk_prep(proj, tab, small["q_norm_w"], small["k_norm_w"])
    (og, o, lse), (w_glu,) = _attn_fwd(q_rot, k_rot, proj, small["sinks"],
                                       _gather_exchange([w["w_glu"].astype(BF16)]))
    (y, yg, hx), (w_out,) = _ssm_fwd(proj, perm, mt_b, scat_b, ocat_b, a16, small["d_skip"],
                                     _gather_exchange([w["w_out"].astype(BF16)]))
    w_glu, w_out = w_glu.reshape(SSM_W, SSM_W), w_out.reshape(D_MODEL, D_MODEL)
    merged, gpre = _merge(og, yg, w_glu, proj, small["b_glu"], small["attn_out_norm_w"], small["ssm_out_norm_w"])
    d_out, d_out_b, loss_parts = _outproj_loss(merged, w_out, x, target)
    loss = 0.5 * jnp.sum(loss_parts[:, 0, 0]) / D_MODEL

    g_w_out = blocks(_mm(merged, d_out_b, "tn", F32, "grad_w_out"))
    d_o, d_za, d_zs, d_g, d_yg1, g_wa, g_ws, g_bglu = _merge_bwd(
        d_out_b, w_out, og, o, yg, gpre, proj, small["b_glu"], small["attn_out_norm_w"], small["ssm_out_norm_w"])
    g_w_glu = blocks(_mm(yg, d_g, "tn", F32, "grad_w_glu"))
    d_yg = _mm(d_g, w_glu, "nt", F32, "d_yg", add=d_yg1)
    (d_u, g_mt, g_scat, g_ocat, g_a16, g_dskip), (ra_out, ra_glu) = _ssm_bwd(
        d_yg, y, proj, hx, perm, mt_b, scat_b, ocat_b, a16, small["d_skip"], _pair_exchange([g_w_out, g_w_glu]))
    p_out = _pair_sum(g_w_out, ra_out, core, BF16, "pair_sum_out")
    p_glu = _pair_sum(g_w_glu, ra_glu, core, BF16, "pair_sum_glu")
    (d_q, d_k, d_v, g_sinks), (rb_out, rb_glu) = _attn_bwd(
        q_rot, k_rot, proj, small["sinks"], d_o, o, lse, _chip_exchange([p_out, p_glu]))
    d_proj, g_qw, g_kw = _qk_prep_bwd(proj, tab, small["q_norm_w"], small["k_norm_w"], d_q, d_k, d_v,
                                      d_za, d_u, d_zs)
    g_qw = g_qw[0, :HEAD_DIM] + g_qw[0, HEAD_DIM:]
    g_kw = g_kw[0, :HEAD_DIM] + g_kw[0, HEAD_DIM:]
    g_in_a = blocks(_mm(d_proj, hn, "tn", F32, "grad_w_in_a", panel=0))
    g_in_b, (ra_a,) = _mm(d_proj, hn, "tn", F32, "grad_w_in_b", panel=1, ride=_pair_exchange([g_in_a]))
    g_in_b = blocks(g_in_b)
    p_a = _pair_sum(g_in_a, ra_a, core, BF16, "pair_sum_in_a")
    d_hn, (rb_a, ra_b) = _mm(d_proj, wt_in, "nn", F32, "d_hn",
                             ride=_both(_chip_exchange([p_a]), _pair_exchange([g_in_b])))
    p_b = _pair_sum(g_in_b, ra_b, core, BF16, "pair_sum_in_b")
    g_small, (rb_b,) = _ssm_prep_bwd(small, g_mt, g_scat, g_ocat, g_a16, _chip_exchange([p_b]))
    (grad_x, g_nw), _ = _rms_bwd_x(x, small["norm_w"], d_hn, d_out, None)
    g_wt_in = jnp.concatenate([_chip_sum(p_a, rb_a, chip, "chip_sum_in_a"),
                               _chip_sum(p_b, rb_b, chip, "chip_sum_in_b")], axis=1)

    g_small.update(norm_w=g_nw.reshape(-1), q_norm_w=g_qw.reshape(-1), k_norm_w=g_kw.reshape(-1),
                   sinks=g_sinks[0, :N_HEADS], d_skip=g_dskip.reshape(-1), b_glu=g_bglu.reshape(-1),
                   attn_out_norm_w=g_wa.reshape(-1), ssm_out_norm_w=g_ws.reshape(-1))
    slab = _pack(g_small, loss).reshape(N_DEV, _PACK_ROWS // N_DEV, 128)
    (ra_s,) = _run_exchange(_pair_exchange([slab]), "pair_exchange_small")
    p_s = _pair_sum(slab, ra_s, core, F32, "pair_sum_small")
    (rb_s,) = _run_exchange(_chip_exchange([p_s]), "chip_exchange_small")
    (g_packed,) = _run_exchange(_gather_exchange([_chip_sum(p_s, rb_s, chip, "chip_sum_small")]), "gather_small")

    g_packed = g_packed.reshape(_PACK_ROWS, 128)
    grads = _unpack(g_packed, w)
    grads.update(w_in=g_wt_in.T,
                 w_glu=_chip_sum(p_glu, rb_glu, chip, "chip_sum_glu"),
                 w_out=_chip_sum(p_out, rb_out, chip, "chip_sum_out"))
    return g_packed[_LOSS_ROW, 0], grad_x, grads


_ANY = pl.BlockSpec(memory_space=pl.ANY)


class _Exchange:
    def __init__(self, arrays, out_shape, sems, start, finish, relay=None):
        self.arrays, self.out_shape, self.sems, self.start, self.finish = arrays, out_shape, sems, start, finish
        self.relay = relay if relay is not None else (lambda ins, outs, sems: None)


def _gather_exchange(blocks):
    n = len(blocks)

    def parts(ins, outs, sems):
        send_sems, recv_sems, local_sems = sems
        x, y, c = lax.axis_index("x"), lax.axis_index("y"), lax.axis_index("c")
        me, sibling = (x, y, c), (x, y, 1 - c)
        chips = [(1 - x, y), (x, 1 - y), (1 - x, 1 - y)]

        def slot(k, dev):
            return outs[k].at[4 * dev[0] + 2 * dev[1] + dev[2]]

        def copy(k, q, block, to, src=None):
            return pltpu.make_async_remote_copy(
                src_ref=slot(k, block) if src is None else src, dst_ref=slot(k, block),
                send_sem=send_sems.at[k, q], recv_sem=recv_sems.at[k, q], device_id=to, device_id_type=MESH)

        mine = [pltpu.make_async_copy(ins[k], slot(k, me), local_sems.at[k]) for k in range(n)]
        first = []
        for k in range(n):
            first.append(copy(k, 0, me, sibling, src=ins[k]))
            first += [copy(k, 1 + j, me, (*chip, c), src=ins[k]) for j, chip in enumerate(chips)]
        return me, sibling, chips, c, copy, mine, first

    def start(ins, outs, sems):
        *_, mine, first = parts(ins, outs, sems)
        for cp in mine + first:
            cp.start()

    def relay(ins, outs, sems):
        me, sibling, chips, c, copy, _, _ = parts(ins, outs, sems)
        for j, chip in enumerate(chips):
            for k in range(n):
                copy(k, 1 + j, (*chip, c), me).wait_recv()
                copy(k, 4 + j, (*chip, c), sibling).start()

    def finish(ins, outs, sems):
        me, sibling, chips, c, copy, mine, first = parts(ins, outs, sems)
        for k in range(n):
            copy(k, 0, sibling, me).wait_recv()
            for j, chip in enumerate(chips):
                copy(k, 4 + j, (*chip, 1 - c), me).wait_recv()
        for cp in first + [copy(k, 4 + j, (*chip, c), sibling) for k in range(n) for j, chip in enumerate(chips)]:
            cp.wait_send()
        for cp in mine:
            cp.wait()

    return _Exchange(blocks, [jax.ShapeDtypeStruct((N_DEV,) + b.shape, b.dtype) for b in blocks],
                     [pltpu.SemaphoreType.DMA((n, 7)), pltpu.SemaphoreType.DMA((n, 7)), pltpu.SemaphoreType.DMA((n,))],
                     start, finish, relay)


def _direct_exchange(arrays, out_lead, fan, route):
    n = len(arrays)

    def copies(ins, outs, sems):
        send_sems, recv_sems = sems
        legs = route(lax.axis_index("x"), lax.axis_index("y"), lax.axis_index("c"))
        return [pltpu.make_async_remote_copy(
            src_ref=ins[k].at[src], dst_ref=outs[k].at[q], send_sem=send_sems.at[k, q], recv_sem=recv_sems.at[k, q],
            device_id=to, device_id_type=MESH) for k in range(n) for src, q, to in legs]

    def start(ins, outs, sems):
        for cp in copies(ins, outs, sems):
            cp.start()

    def finish(ins, outs, sems):
        for cp in copies(ins, outs, sems):
            cp.wait()

    return _Exchange(arrays, [jax.ShapeDtypeStruct((out_lead,) + a.shape[1:], a.dtype) for a in arrays],
                     [pltpu.SemaphoreType.DMA((n, fan)), pltpu.SemaphoreType.DMA((n, fan))], start, finish)


def _pair_exchange(grads):
    return _direct_exchange(grads, 4, 4, lambda x, y, c: [(2 * chip + (1 - c), chip, (x, y, 1 - c))
                                                          for chip in range(4)])


def _chip_exchange(parts):
    def route(x, y, c):
        chips = [(1 - x, y), (x, 1 - y), (1 - x, 1 - y)]
        return [(2 * chip[0] + chip[1], q, (*chip, c)) for q, chip in enumerate(chips)]
    return _direct_exchange(parts, 3, 3, route)


def _both(ex1, ex2):
    n1, s1 = len(ex1.arrays), len(ex1.sems)

    def halves(ins, outs, sems):
        return (ins[:n1], outs[:n1], sems[:s1]), (ins[n1:], outs[n1:], sems[s1:])

    def start(ins, outs, sems):
        h1, h2 = halves(ins, outs, sems)
        ex1.start(*h1)
        ex2.start(*h2)

    def relay(ins, outs, sems):
        h1, h2 = halves(ins, outs, sems)
        ex1.relay(*h1)
        ex2.relay(*h2)

    def finish(ins, outs, sems):
        h1, h2 = halves(ins, outs, sems)
        ex1.finish(*h1)
        ex2.finish(*h2)

    return _Exchange(list(ex1.arrays) + list(ex2.arrays), list(ex1.out_shape) + list(ex2.out_shape),
                     list(ex1.sems) + list(ex2.sems), start, finish, relay)


def _run_exchange(ex, name):
    n = len(ex.arrays)

    def body(*refs):
        ins, outs, sems = refs[:n], refs[n:2 * n], refs[2 * n:]
        ex.start(ins, outs, sems)
        ex.relay(ins, outs, sems)
        ex.finish(ins, outs, sems)

    return list(pl.pallas_call(body, name=name, in_specs=[_ANY] * n, out_specs=[_ANY] * n, out_shape=ex.out_shape,
                               scratch_shapes=ex.sems)(*ex.arrays))


def _call(body, name, grid, in_specs, out_specs, out_shape, args, scratch_shapes=(), ride=None):
    if ride is None:
        sem = ("arbitrary",) * len(grid)
        return pl.pallas_call(body, name=name, grid=grid, in_specs=in_specs, out_specs=out_specs, out_shape=out_shape,
                              scratch_shapes=list(scratch_shapes), compiler_params=_cp(sem))(*args), None
    n_in, n_out, n_scr, n_x = len(in_specs), len(out_specs), len(scratch_shapes), len(ride.arrays)

    def wrapped(*refs):
        ins, refs = refs[:n_in], refs[n_in:]
        x_in, refs = refs[:n_x], refs[n_x:]
        outs, refs = refs[:n_out], refs[n_out:]
        x_out, refs = refs[:n_x], refs[n_x:]
        scr, sems = refs[:n_scr], refs[n_scr:]
        step, total = pl.program_id(0), grid[0]
        for a in range(1, len(grid)):
            step, total = step * grid[a] + pl.program_id(a), total * grid[a]
        @pl.when(step == 0)
        def _():
            ride.start(x_in, x_out, sems)

        @pl.when(step == max(total - 2, 0))
        def _():
            ride.relay(x_in, x_out, sems)

        body(*ins, *outs, *scr)

        @pl.when(step == total - 1)
        def _():
            ride.finish(x_in, x_out, sems)

    res = pl.pallas_call(
        wrapped, name=name, grid=grid, in_specs=list(in_specs) + [_ANY] * n_x,
        out_specs=list(out_specs) + [_ANY] * n_x, out_shape=list(out_shape) + list(ride.out_shape),
        scratch_shapes=list(scratch_shapes) + list(ride.sems),
        compiler_params=_cp(("arbitrary",) * len(grid)))(*args, *ride.arrays)
    return res[:n_out], list(res[n_out:])


def _pair_sum(g, ra, core, out_dtype, name):
    _, r, C = g.shape
    tr = _tile(r, 576)

    def body(c_ref, g_ref, ra_ref, p_ref):
        p_ref[...] = (g_ref[...] + ra_ref[...]).astype(p_ref.dtype)

    return pl.pallas_call(
        body,
        name=name,
        grid_spec=pltpu.PrefetchScalarGridSpec(
            num_scalar_prefetch=1,
            grid=(4, r // tr),
            in_specs=[pl.BlockSpec((1, tr, C), lambda j, t, c_ref: (2 * j + c_ref[0], t, 0)),
                      pl.BlockSpec((1, tr, C), lambda j, t, c_ref: (j, t, 0))],
            out_specs=pl.BlockSpec((1, tr, C), lambda j, t, c_ref: (j, t, 0)),
        ),
        out_shape=jax.ShapeDtypeStruct((4, r, C), out_dtype),
        compiler_params=_cp(("parallel", "parallel")),
    )(core, g, ra)


def _chip_sum(p, rb, chip, name):
    _, r, C = p.shape
    tr = _tile(r, 576)

    def body(c_ref, p_ref, rb_ref, o_ref):
        acc = p_ref[0].astype(F32) + rb_ref[0].astype(F32)
        acc = acc + rb_ref[1].astype(F32)
        o_ref[...] = acc + rb_ref[2].astype(F32)

    return pl.pallas_call(
        body,
        name=name,
        grid_spec=pltpu.PrefetchScalarGridSpec(
            num_scalar_prefetch=1,
            grid=(r // tr,),
            in_specs=[pl.BlockSpec((1, tr, C), lambda t, c_ref: (c_ref[0], t, 0)),
                      pl.BlockSpec((3, tr, C), lambda t, c_ref: (0, t, 0))],
            out_specs=pl.BlockSpec((tr, C), lambda t, c_ref: (t, 0)),
        ),
        out_shape=jax.ShapeDtypeStruct((r, C), F32),
        compiler_params=_cp(("parallel",)),
    )(chip, p, rb)


def _adamw(g, w, m, v, name):
    R, C = g.shape
    tr = _tile(R, 256)
    c1 = 1.0 - ADAM_B1 ** ADAM_STEP
    c2 = 1.0 - ADAM_B2 ** ADAM_STEP

    def body(g_ref, w_ref, m_ref, v_ref, d_ref, nm_ref, nv_ref):
        gv = g_ref[...]
        nm = ADAM_B1 * m_ref[...] + (1.0 - ADAM_B1) * gv
        nv = ADAM_B2 * v_ref[...] + (1.0 - ADAM_B2) * (gv * gv)
        nm_ref[...] = nm
        nv_ref[...] = nv
        d_ref[...] = -ADAM_LR * ((nm / c1) / (jnp.sqrt(nv / c2) + ADAM_EPS) + ADAM_WD * w_ref[...])

    blk = pl.BlockSpec((tr, C), lambda i: (i, 0))
    return pl.pallas_call(
        body, name=name, grid=(R // tr,), in_specs=[blk] * 4, out_specs=[blk] * 3,
        out_shape=[jax.ShapeDtypeStruct((R, C), F32)] * 3, compiler_params=_cp(("parallel",)),
    )(g, w, m, v)


_SMALL = ("norm_w", "q_norm_w", "k_norm_w", "sinks", "a_re", "a_im", "log_step", "b_re", "b_im", "c_re", "c_im",
          "d_skip", "b_glu", "attn_out_norm_w", "ssm_out_norm_w")
_WEIGHTS = ("norm_w", "w_in", "q_norm_w", "k_norm_w", "sinks", "a_re", "a_im", "log_step", "b_re", "b_im", "c_re",
            "c_im", "d_skip", "w_glu", "b_glu", "attn_out_norm_w", "ssm_out_norm_w", "w_out")
_SMALL_2D = dict(norm_w=(1, 2048), q_norm_w=(1, 64), k_norm_w=(1, 64), sinks=(1, 16), a_re=(64, 64), a_im=(64, 64),
                 log_step=(1, 64), b_re=(1024, 64), b_im=(1024, 64), c_re=(1024, 64), c_im=(1024, 64),
                 d_skip=(1, 1024), b_glu=(1, 1024), attn_out_norm_w=(1, 1024), ssm_out_norm_w=(1, 1024))
_P_MINOR = ("b_re", "b_im")


def _flat_form(n, t):
    return t.transpose(0, 2, 1) if n in _P_MINOR else t


def _own_form(n, t, shape):
    if n in _P_MINOR:
        return t.reshape(shape[0], shape[2], shape[1]).transpose(0, 2, 1)
    return t.reshape(shape)


def _slab_rows(n):
    return -(-n // 1024) * 8


_PACK_ROWS = 2304


_LOSS_ROW = 2192


def _pack(d, loss):
    parts = []
    for n in _SMALL:
        flat = _flat_form(n, d[n]).reshape(-1).astype(F32)
        rows = _slab_rows(flat.shape[0])
        parts.append(jnp.pad(flat, (0, rows * 128 - flat.shape[0])).reshape(rows, 128))
    assert sum(p.shape[0] for p in parts) == _LOSS_ROW
    parts.append(jnp.pad(loss.reshape(1, 1), ((0, _PACK_ROWS - _LOSS_ROW - 1), (0, 127))))
    return jnp.concatenate(parts, axis=0)


def _unpack(packed, like):
    out, off = {}, 0
    for n in _SMALL:
        size = math.prod(like[n].shape)
        rows = _slab_rows(size)
        out[n] = _own_form(n, packed[off:off + rows].reshape(-1)[:size], like[n].shape)
        off += rows
    return out


def _adamw_small(g, w, m, v):
    c1 = 1.0 - ADAM_B1 ** ADAM_STEP
    c2 = 1.0 - ADAM_B2 ** ADAM_STEP
    k = len(_SMALL)

    def body(*refs):
        ins, outs = refs[:4 * k], refs[4 * k:]
        for j in range(k):
            gv, wv, mv, vv = (ins[q * k + j][...] for q in range(4))
            nm = ADAM_B1 * mv + (1.0 - ADAM_B1) * gv
            nv = ADAM_B2 * vv + (1.0 - ADAM_B2) * (gv * gv)
            outs[j][...] = -ADAM_LR * ((nm / c1) / (jnp.sqrt(nv / c2) + ADAM_EPS) + ADAM_WD * wv)
            outs[k + j][...] = nm
            outs[2 * k + j][...] = nv

    args = [_flat_form(n, d[n]).reshape(_SMALL_2D[n]) for d in (g, w, m, v) for n in _SMALL]
    shapes = [jax.ShapeDtypeStruct(_SMALL_2D[n], F32) for _ in range(3) for n in _SMALL]
    outs = pl.pallas_call(body, name="adamw_small", out_shape=shapes, compiler_params=_cp())(*args)
    res = []
    for q in range(3):
        res.append({n: _own_form(n, outs[q * k + j], w[n].shape) for j, n in enumerate(_SMALL)})
    return res


def kernel(x, positions, norm_w, w_in, q_norm_w, k_norm_w, sinks, a_re, a_im, log_step, b_re, b_im, c_re, c_im, d_skip, w_glu, b_glu, attn_out_norm_w, ssm_out_norm_w, w_out, loss_target, m_norm_w, m_w_in, m_q_norm_w, m_k_norm_w, m_sinks, m_a_re, m_a_im, m_log_step, m_b_re, m_b_im, m_c_re, m_c_im, m_d_skip, m_w_glu, m_b_glu, m_attn_out_norm_w, m_ssm_out_norm_w, m_w_out, v_norm_w, v_w_in, v_q_norm_w, v_k_norm_w, v_sinks, v_a_re, v_a_im, v_log_step, v_b_re, v_b_im, v_c_re, v_c_im, v_d_skip, v_w_glu, v_b_glu, v_attn_out_norm_w, v_ssm_out_norm_w, v_w_out):
    w = dict(norm_w=norm_w, w_in=w_in, q_norm_w=q_norm_w, k_norm_w=k_norm_w, sinks=sinks, a_re=a_re, a_im=a_im,
             log_step=log_step, b_re=b_re, b_im=b_im, c_re=c_re, c_im=c_im, d_skip=d_skip, w_glu=w_glu, b_glu=b_glu,
             attn_out_norm_w=attn_out_norm_w, ssm_out_norm_w=ssm_out_norm_w, w_out=w_out)
    m = dict(norm_w=m_norm_w, w_in=m_w_in, q_norm_w=m_q_norm_w, k_norm_w=m_k_norm_w, sinks=m_sinks, a_re=m_a_re,
             a_im=m_a_im, log_step=m_log_step, b_re=m_b_re, b_im=m_b_im, c_re=m_c_re, c_im=m_c_im, d_skip=m_d_skip,
             w_glu=m_w_glu, b_glu=m_b_glu, attn_out_norm_w=m_attn_out_norm_w, ssm_out_norm_w=m_ssm_out_norm_w,
             w_out=m_w_out)
    v = dict(norm_w=v_norm_w, w_in=v_w_in, q_norm_w=v_q_norm_w, k_norm_w=v_k_norm_w, sinks=v_sinks, a_re=v_a_re,
             a_im=v_a_im, log_step=v_log_step, b_re=v_b_re, b_im=v_b_im, c_re=v_c_re, c_im=v_c_im, d_skip=v_d_skip,
             w_glu=v_w_glu, b_glu=v_b_glu, attn_out_norm_w=v_attn_out_norm_w, ssm_out_norm_w=v_ssm_out_norm_w,
             w_out=v_w_out)
    core = lax.axis_index("c").astype(jnp.int32).reshape(1)
    chip = (2 * lax.axis_index("x") + lax.axis_index("y")).astype(jnp.int32).reshape(1)

    loss, grad_x, grads = _step(x[0], positions[0], loss_target[0], w, core, chip)
    delta, new_m, new_v = {}, {}, {}
    for n in ("w_glu", "w_out"):
        delta[n], new_m[n], new_v[n] = _adamw(grads[n], w[n], m[n], v[n], f"adamw_{n}")
    d_t, m_t, v_t = _adamw(grads["w_in"].T, w["w_in"].T, m["w_in"].T, v["w_in"].T, "adamw_w_in")
    delta["w_in"], new_m["w_in"], new_v["w_in"] = d_t.T, m_t.T, v_t.T
    d_s, m_s, v_s = _adamw_small(grads, w, m, v)
    delta.update(d_s)
    new_m.update(m_s)
    new_v.update(v_s)

    return (loss, grad_x[None], *[grads[n] for n in _WEIGHTS], *[delta[n] for n in _WEIGHTS],
            *[new_m[n] for n in _WEIGHTS], *[new_v[n] for n in _WEIGHTS])
```

```python
import functools
import math

import jax
import jax.numpy as jnp
from jax import lax
from jax.experimental import pallas as pl
from jax.experimental.pallas import tpu as pltpu

F32 = jnp.float32
BF16 = jnp.bfloat16

D_MODEL = 2048
ATTN_W = 1024
KV_W = 256
SSM_W = 1024
HEAD_DIM = 64
N_HEADS = 16
N_KV = 4
KV_REP = 4
IN_W = 4608
BLOCK = 128
ROPE_THETA = 10000.0
NORM_EPS = 1e-6
SSM_G = 64
SSM_P = 64
SSM_H = 16
CHUNK = 16
CW = CHUNK * SSM_H
N_DEV = 8

ADAM_LR = 0.001
ADAM_B1 = 0.9
ADAM_B2 = 0.999
ADAM_EPS = 1e-08
ADAM_WD = 0.01
ADAM_STEP = 10

VMEM_LIMIT = 56 * 1024 * 1024
MESH = pl.DeviceIdType.MESH


def _cp(sem=None):
    if sem is None:
        return pltpu.CompilerParams(vmem_limit_bytes=VMEM_LIMIT)
    return pltpu.CompilerParams(vmem_limit_bytes=VMEM_LIMIT, dimension_semantics=sem)


def _sigmoid(x):
    return 0.5 * jnp.tanh(0.5 * x) + 0.5


def _silu(x):
    return x * _sigmoid(x)


def _dsilu(x):
    s = _sigmoid(x)
    return s * (1.0 + x * (1.0 - s))


_GELU_C = math.sqrt(2.0 / math.pi)


def _gelu(y):
    t = jnp.tanh(_GELU_C * (y + 0.044715 * y * y * y))
    return 0.5 * y * (1.0 + t)


def _dgelu(y):
    t = jnp.tanh(_GELU_C * (y + 0.044715 * y * y * y))
    return 0.5 * (1.0 + t) + 0.5 * y * (1.0 - t * t) * _GELU_C * (1.0 + 3.0 * 0.044715 * y * y)


def _tile(n, want):
    if n <= want:
        return n
    for t in range(want - want % 16, 0, -16):
        if n % t == 0:
            return t
    raise ValueError((n, want))


def _mm(a, b, mode, out_dtype, name, tm=512, tn=1024, add=None, ride=None, panel=None):
    if mode == "nn":
        (M, K), (K2, N) = a.shape, b.shape
    elif mode == "nt":
        (M, K), (N, K2) = a.shape, b.shape
    else:
        (K, M), (K2, N) = a.shape, b.shape
    assert K == K2
    tm, tn = _tile(M, tm), _tile(N, tn)
    p0 = 0
    if panel is not None:
        assert mode != "nt" and add is None
        p0, N = panel, tn
    dn = {"nn": _NN, "nt": _NT, "tn": _TN}[mode]

    def body(a_ref, b_ref, *rest):
        o_ref = rest[-1]
        acc = lax.dot_general(a_ref[...].astype(BF16), b_ref[...].astype(BF16), dn, preferred_element_type=F32)
        if add is not None:
            acc = acc + rest[0][...]
        o_ref[...] = acc.astype(o_ref.dtype)

    a_spec = pl.BlockSpec((K, tm), lambda j, i: (0, i)) if mode == "tn" else pl.BlockSpec((tm, K), lambda j, i: (i, 0))
    b_spec = (pl.BlockSpec((tn, K), lambda j, i: (j, 0)) if mode == "nt"
              else pl.BlockSpec((K, tn), lambda j, i: (0, j + p0)))
    o_spec = pl.BlockSpec((tm, tn), lambda j, i: (i, j))
    extra = () if add is None else (add,)
    if ride is not None:
        (out,), landed = _call(body, name, (N // tn, M // tm), [a_spec, b_spec] + [o_spec] * len(extra), [o_spec],
                               [jax.ShapeDtypeStruct((M, N), out_dtype)], (a, b, *extra), ride=ride)
        return out, landed
    return pl.pallas_call(
        body,
        name=name,
        grid=(N // tn, M // tm),
        in_specs=[a_spec, b_spec] + [o_spec] * len(extra),
        out_specs=o_spec,
        out_shape=jax.ShapeDtypeStruct((M, N), out_dtype),
        compiler_params=_cp(("parallel", "parallel")),
    )(a, b, *extra)


_CHIP_ORDER = (0, 2, 1, 3)


def _rms_inproj_gather(x, norm_w, wt_shard, chip):
    L = x.shape[0]
    tm = _tile(L, 512)
    ni = L // tm
    r = IN_W // N_DEV
    tn = 2 * r

    def body(chip_ref, x_ref, nw_ref, shard, proj_ref, hn_ref, wt_hbm, hn_scr, w_scr, send_sems, recv_sems, loc_sems):
        jc, i = pl.program_id(0), pl.program_id(1)
        xx, yy, c = lax.axis_index("x"), lax.axis_index("y"), lax.axis_index("c")
        me, sibling = (xx, yy, c), (xx, yy, 1 - c)
        chips = [(1 - xx, yy), (xx, 1 - yy), (1 - xx, 1 - yy)]

        def slot(dev):
            return wt_hbm.at[4 * dev[0] + 2 * dev[1] + dev[2]]

        def copy(q, block, to, src=None):
            return pltpu.make_async_remote_copy(
                src_ref=slot(block) if src is None else src, dst_ref=slot(block),
                send_sem=send_sems.at[q], recv_sem=recv_sems.at[q], device_id=to, device_id_type=MESH)

        def rows_of(buf, core):
            return w_scr.at[buf, pl.ds(pl.multiple_of(core * r, 16), r)]

        mine = pltpu.make_async_copy(shard, slot(me), loc_sems.at[0])
        sends = [copy(0, me, sibling, src=shard)] + [copy(1 + j, me, (*ch, c), src=shard) for j, ch in enumerate(chips)]
        first = jnp.logical_and(jc == 0, i == 0)

        @pl.when(first)
        def _():
            mine.start()
            for cp in sends[:3]:
                cp.start()
            own = pltpu.make_async_copy(shard, rows_of(0, c), loc_sems.at[1])
            own.start()
            copy(0, sibling, me).wait_recv()
            sib = pltpu.make_async_copy(slot(sibling), rows_of(0, 1 - c), loc_sems.at[2])
            sib.start()
            own.wait()
            sib.wait()

        def take_direct(j, ch):
            copy(1 + j, (*ch, c), me).wait_recv()
            copy(4 + j, (*ch, c), sibling).start()
            if j == 0:
                sends[1].wait_send()
                sends[2].wait_send()
                sends[3].start()
            pltpu.make_async_copy(slot((*ch, c)), rows_of((1 + j) % 2, c), loc_sems.at[1]).start()

        for j, ch in enumerate(chips):
            early = jnp.logical_and(jc == j, i == ni // 2) if j > 0 else jnp.logical_and(jc == 1, i == 0)

            @pl.when(early)
            def _(j=j, ch=ch):
                take_direct(j, ch)

            @pl.when(jnp.logical_and(jc == 1 + j, i == 0))
            def _(j=j, ch=ch):
                buf = (1 + j) % 2
                copy(4 + j, (*ch, 1 - c), me).wait_recv()
                passed = pltpu.make_async_copy(slot((*ch, 1 - c)), rows_of(buf, 1 - c), loc_sems.at[2])
                passed.start()
                pltpu.make_async_copy(slot((*ch, c)), rows_of(buf, c), loc_sems.at[1]).wait()
                passed.wait()

        rows = pl.ds(pl.multiple_of(i * tm, tm), tm)

        @pl.when(jc == 0)
        def _():
            xv = x_ref[...]
            rstd = lax.rsqrt(jnp.mean(xv * xv, axis=-1, keepdims=True) + NORM_EPS)
            hn = (xv * rstd * nw_ref[...]).astype(BF16)
            hn_scr[rows, :] = hn
            hn_ref[...] = hn

        for buf in range(2):
            @pl.when(jc % 2 == buf)
            def _(buf=buf):
                proj_ref[...] = lax.dot_general(hn_scr[rows, :], w_scr[buf], _NT, preferred_element_type=F32)

        @pl.when(jnp.logical_and(jc == 3, i == ni - 1))
        def _():
            sends[0].wait_send()
            sends[3].wait_send()
            for j, ch in enumerate(chips):
                copy(4 + j, (*ch, c), sibling).wait_send()
            mine.wait()

    def tile_of(jc, chip_ref):
        mask = jnp.where(jc == 1, _CHIP_ORDER[1], jnp.where(jc == 2, _CHIP_ORDER[2], jnp.where(jc == 3, _CHIP_ORDER[3], 0)))
        return jnp.bitwise_xor(chip_ref[0], mask)

    held = lambda jc, i: jnp.where(jc == 0, i, ni - 1)
    return pl.pallas_call(
        body,
        name="rms_inproj_gather",
        grid_spec=pltpu.PrefetchScalarGridSpec(
            num_scalar_prefetch=1,
            grid=(4, ni),
            in_specs=[pl.BlockSpec((tm, D_MODEL), lambda jc, i, ch: (held(jc, i), 0)),
                      pl.BlockSpec((1, D_MODEL), lambda jc, i, ch: (0, 0)), _ANY],
            out_specs=[pl.BlockSpec((tm, tn), lambda jc, i, ch: (i, tile_of(jc, ch))),
                       pl.BlockSpec((tm, D_MODEL), lambda jc, i, ch: (held(jc, i), 0)), _ANY],
            scratch_shapes=[pltpu.VMEM((L, D_MODEL), BF16), pltpu.VMEM((2, tn, D_MODEL), BF16),
                            pltpu.SemaphoreType.DMA((7,)), pltpu.SemaphoreType.DMA((7,)), pltpu.SemaphoreType.DMA((3,))],
        ),
        out_shape=[jax.ShapeDtypeStruct((L, IN_W), F32), jax.ShapeDtypeStruct((L, D_MODEL), BF16),
                   jax.ShapeDtypeStruct((N_DEV, r, D_MODEL), BF16)],
        compiler_params=_cp(("arbitrary", "arbitrary")),
    )(chip, x, norm_w.reshape(1, D_MODEL), wt_shard)


def _seg_sum(v):
    a = lax.broadcasted_iota(jnp.int32, (128, 128), 0) // HEAD_DIM
    b = lax.broadcasted_iota(jnp.int32, (128, 128), 1) // HEAD_DIM
    ones = jnp.where(a == b, 1.0, 0.0).astype(BF16)
    hi = v.astype(BF16)
    lo = (v - hi.astype(F32)).astype(BF16)
    return jnp.dot(hi, ones, preferred_element_type=F32) + jnp.dot(lo, ones, preferred_element_type=F32)


def _rot_half(t):
    lane = lax.broadcasted_iota(jnp.int32, t.shape, 1)
    return jnp.where(lane % HEAD_DIM < HEAD_DIM // 2, pltpu.roll(t, 128 - HEAD_DIM // 2, 1),
                     pltpu.roll(t, HEAD_DIM // 2, 1))


def _norm_rope(raw, w, cos, sin):
    r = lax.rsqrt(_seg_sum(raw * raw) * (1.0 / HEAD_DIM) + NORM_EPS)
    tn = raw * r * w
    return r, tn * cos + _rot_half(tn) * sin


def _norm_rope_bwd(d_rot, raw, w, cos, sin):
    r = lax.rsqrt(_seg_sum(raw * raw) * (1.0 / HEAD_DIM) + NORM_EPS)
    d_tn = d_rot * cos + _rot_half(d_rot * sin)
    xh = raw * r
    gw = d_tn * w
    d_raw = r * (gw - xh * (_seg_sum(gw * xh) * (1.0 / HEAD_DIM)))
    return d_raw, d_tn * xh


def _band_mask2(has_prev):
    qi = lax.broadcasted_iota(jnp.int32, (2 * BLOCK, 2 * BLOCK), 0) % BLOCK + BLOCK
    kj = lax.broadcasted_iota(jnp.int32, (2 * BLOCK, 2 * BLOCK), 1)
    rel = qi - kj
    return (rel >= 0) & (rel < BLOCK) & ((kj >= BLOCK) | has_prev)


def _half_tiles(pair):
    lo = lax.broadcasted_iota(jnp.int32, pair.shape, 1) < HEAD_DIM
    sw = pltpu.roll(pair, HEAD_DIM, 1)
    z = jnp.zeros_like(pair)
    return (jnp.where(lo, pair, z).astype(BF16), jnp.where(lo, z, sw).astype(BF16),
            jnp.where(lo, sw, z).astype(BF16), jnp.where(lo, z, pair).astype(BF16))


def _two_rows(top, bottom):
    row = lax.broadcasted_iota(jnp.int32, (2 * BLOCK, 1), 0)
    return jnp.where(row < BLOCK, top, bottom)


def _lane_col(mat, h):
    lane = lax.broadcasted_iota(jnp.int32, mat.shape, 1)
    return jnp.sum(jnp.where(lane == h, mat, 0.0), axis=1, keepdims=True)


_SCALE = 1.0 / math.sqrt(HEAD_DIM)
_NT = (((1,), (1,)), ((), ()))
_NN = (((1,), (0,)), ((), ()))
_TN = (((0,), (0,)), ((), ()))


def _qk_prep(proj, tab, qw, kw):
    L = proj.shape[0]
    tm = _tile(L, 512)

    def body(q_ref, k_ref, t_ref, qw_ref, kw_ref, qo_ref, ko_ref):
        cos, sin = t_ref[:, :128], t_ref[:, 128:]
        for c in range(ATTN_W // 128):
            _, qr = _norm_rope(q_ref[:, c * 128:(c + 1) * 128], qw_ref[...], cos, sin)
            qo_ref[:, c * 128:(c + 1) * 128] = (qr * _SCALE).astype(BF16)
        for c in range(KV_W // 128):
            _, kr = _norm_rope(k_ref[:, c * 128:(c + 1) * 128], kw_ref[...], cos, sin)
            ko_ref[:, c * 128:(c + 1) * 128] = kr.astype(BF16)

    row = pl.BlockSpec((1, 128), lambda i: (0, 0))
    return pl.pallas_call(
        body,
        name="qk_prep",
        grid=(L // tm,),
        in_specs=[pl.BlockSpec((tm, ATTN_W), lambda i: (i, 0)), pl.BlockSpec((tm, KV_W), lambda i: (i, 4)),
                  pl.BlockSpec((tm, 256), lambda i: (i, 0)), row, row],
        out_specs=[pl.BlockSpec((tm, ATTN_W), lambda i: (i, 0)), pl.BlockSpec((tm, KV_W), lambda i: (i, 0))],
        out_shape=[jax.ShapeDtypeStruct((L, ATTN_W), BF16), jax.ShapeDtypeStruct((L, KV_W), BF16)],
        compiler_params=_cp(("parallel",)),
    )(proj, proj, tab, jnp.tile(qw, 2).reshape(1, 128), jnp.tile(kw, 2).reshape(1, 128))


def _group_tiles(g, kt, vt):
    a, b = divmod(g, 2)
    return kt[a][2 * b], kt[a][2 * b + 1], vt[a][2 * b], vt[a][2 * b + 1]


def _attn_fwd(q, k, proj, sinks, ride):
    L = proj.shape[0]
    nb = L // BLOCK

    def body(q_ref, kc_ref, kp_ref, vc_ref, vp_ref, z0_ref, z1_ref, sink_ref, og_ref, o_ref, lse_ref):
        i = pl.program_id(0)
        mask = _band_mask2(i > 0)
        z = jnp.concatenate([z0_ref[...], z1_ref[...]], axis=1)
        lane = lax.broadcasted_iota(jnp.int32, (BLOCK, 128), 1)
        kt = [_half_tiles(jnp.concatenate([kp_ref[:, a * 128:(a + 1) * 128], kc_ref[:, a * 128:(a + 1) * 128]],
                                          axis=0).astype(F32)) for a in range(2)]
        vt = [_half_tiles(jnp.concatenate([vp_ref[:, a * 128:(a + 1) * 128], vc_ref[:, a * 128:(a + 1) * 128]],
                                          axis=0)) for a in range(2)]
        lse_mat = jnp.zeros((BLOCK, 128), F32)
        outs = []
        for g in range(N_KV):
            k_lo, k_hi, v_lo, v_hi = _group_tiles(g, kt, vt)
            q2 = jnp.concatenate([q_ref[:, 2 * g * 128:(2 * g + 1) * 128],
                                  q_ref[:, (2 * g + 1) * 128:(2 * g + 2) * 128]], axis=0)
            acc = jnp.zeros((2 * BLOCK, 128), F32)
            for half, (kh, vh) in enumerate(((k_lo, v_lo), (k_hi, v_hi))):
                h_top, h_bot = 4 * g + half, 4 * g + 2 + half
                s = jnp.where(mask, lax.dot_general(q2, kh, _NT, preferred_element_type=F32), -1e30)
                sink = _two_rows(sink_ref[h_top], sink_ref[h_bot])
                m = jnp.maximum(jnp.max(s, axis=-1, keepdims=True), sink)
                e = jnp.exp(s - m)
                den = jnp.sum(e, axis=-1, keepdims=True) + jnp.exp(sink - m)
                p = e * (1.0 / den)
                acc = acc + jnp.dot(p.astype(BF16), vh, preferred_element_type=F32)
                lse = m + jnp.log(den)
                lse_mat = jnp.where(lane == h_top, lse[:BLOCK], lse_mat)
                lse_mat = jnp.where(lane == h_bot, lse[BLOCK:], lse_mat)
            outs += [acc[:BLOCK], acc[BLOCK:]]
        o = jnp.concatenate(outs, axis=1)
        o_ref[...] = o
        og_ref[...] = o * _silu(z)
        lse_ref[...] = lse_mat

    prev = lambda i: jnp.maximum(i - 1, 0)
    return _call(
        body, "attn_fwd", (nb,),
        [pl.BlockSpec((BLOCK, ATTN_W), lambda i: (i, 0)),
         pl.BlockSpec((BLOCK, KV_W), lambda i: (i, 0)),
         pl.BlockSpec((BLOCK, KV_W), lambda i: (prev(i), 0)),
         pl.BlockSpec((BLOCK, KV_W), lambda i: (i, 5)),
         pl.BlockSpec((BLOCK, KV_W), lambda i: (prev(i), 5)),
         pl.BlockSpec((BLOCK, 512), lambda i: (i, 3)),
         pl.BlockSpec((BLOCK, 512), lambda i: (i, 4)),
         pl.BlockSpec(memory_space=pltpu.SMEM)],
        [pl.BlockSpec((BLOCK, ATTN_W), lambda i: (i, 0)),
         pl.BlockSpec((BLOCK, ATTN_W), lambda i: (i, 0)),
         pl.BlockSpec((BLOCK, 128), lambda i: (i, 0))],
        [jax.ShapeDtypeStruct((L, ATTN_W), F32), jax.ShapeDtypeStruct((L, ATTN_W), F32),
         jax.ShapeDtypeStruct((L, 128), F32)],
        (q, k, k, proj, proj, proj, proj, sinks), ride=ride)


def _attn_bwd(q, k, proj, sinks, d_o, o, lse, ride):
    L = proj.shape[0]
    nb = L // BLOCK

    def body(q_ref, kc_ref, kp_ref, vc_ref, vp_ref, do_ref, o_ref, lse_ref, sink_ref,
             dq_ref, dk_ref, dv_ref, gs_ref, ck_scr, cv_scr):
        i = pl.program_id(0)

        @pl.when(i == 0)
        def _():
            gs_ref[...] = jnp.zeros_like(gs_ref)
            ck_scr[...] = jnp.zeros_like(ck_scr)
            cv_scr[...] = jnp.zeros_like(cv_scr)

        @pl.when(i == nb)
        def _():
            dk_ref[...] = ck_scr[...]
            dv_ref[...] = cv_scr[...]

        @pl.when(i < nb)
        def _():
            mask = _band_mask2(i > 0)
            lane = lax.broadcasted_iota(jnp.int32, (1, 128), 1)
            lo = lax.broadcasted_iota(jnp.int32, (2 * BLOCK, 128), 1) < HEAD_DIM
            lse_c = lse_ref[...]
            kt = [_half_tiles(jnp.concatenate([kp_ref[:, a * 128:(a + 1) * 128], kc_ref[:, a * 128:(a + 1) * 128]],
                                              axis=0).astype(F32)) for a in range(2)]
            vt = [_half_tiles(jnp.concatenate([vp_ref[:, a * 128:(a + 1) * 128], vc_ref[:, a * 128:(a + 1) * 128]],
                                              axis=0)) for a in range(2)]
            gs = jnp.zeros((1, 128), F32)
            dq_parts = []
            dk_acc = [jnp.zeros((2 * BLOCK, 128), F32) for _ in range(2)]
            dv_acc = [jnp.zeros((2 * BLOCK, 128), F32) for _ in range(2)]
            for g in range(N_KV):
                a, b = divmod(g, 2)
                k_lo, k_hi, v_lo, v_hi = _group_tiles(g, kt, vt)
                t0, t1 = slice(2 * g * 128, (2 * g + 1) * 128), slice((2 * g + 1) * 128, (2 * g + 2) * 128)
                q2 = jnp.concatenate([q_ref[:, t0], q_ref[:, t1]], axis=0)
                do2 = jnp.concatenate([do_ref[:, t0], do_ref[:, t1]], axis=0)
                prod = do2 * jnp.concatenate([o_ref[:, t0], o_ref[:, t1]], axis=0)
                do2_b = do2.astype(BF16)
                dq2 = jnp.zeros((2 * BLOCK, 128), F32)
                dk_h, dv_h = [], []
                for half, (kh, vh) in enumerate(((k_lo, v_lo), (k_hi, v_hi))):
                    h_top, h_bot = 4 * g + half, 4 * g + 2 + half
                    lse = jnp.concatenate([_lane_col(lse_c, h_top), _lane_col(lse_c, h_bot)], axis=0)
                    sink = _two_rows(sink_ref[h_top], sink_ref[h_bot])
                    delta = jnp.sum(jnp.where(lo == (half == 0), prod, 0.0), axis=1, keepdims=True)
                    s = jnp.where(mask, lax.dot_general(q2, kh, _NT, preferred_element_type=F32), -1e30)
                    p = jnp.exp(s - lse)
                    dp = lax.dot_general(do2_b, vh, _NT, preferred_element_type=F32)
                    ds_b = (p * (dp - delta)).astype(BF16)
                    p_b = p.astype(BF16)
                    dq2 = dq2 + jnp.dot(ds_b, kh, preferred_element_type=F32)
                    dk_h.append(lax.dot_general(ds_b, q2, _TN, preferred_element_type=F32))
                    dv_h.append(lax.dot_general(p_b, do2_b, _TN, preferred_element_type=F32))
                    gsink = -jnp.exp(sink - lse) * delta
                    row = lax.broadcasted_iota(jnp.int32, (2 * BLOCK, 1), 0)
                    gs = gs + jnp.where(lane == h_top, jnp.sum(jnp.where(row < BLOCK, gsink, 0.0)), 0.0)
                    gs = gs + jnp.where(lane == h_bot, jnp.sum(jnp.where(row >= BLOCK, gsink, 0.0)), 0.0)
                dq_parts += [dq2[:BLOCK], dq2[BLOCK:]]
                for acc, parts in ((dk_acc, dk_h), (dv_acc, dv_h)):
                    t = jnp.where(lo, parts[0], parts[1])
                    t = t + pltpu.roll(t, HEAD_DIM, 1)
                    acc[a] = acc[a] + jnp.where(lo == (b == 0), t, 0.0)
            dq_ref[...] = jnp.concatenate(dq_parts, axis=1)
            dk_full = jnp.concatenate(dk_acc, axis=1)
            dv_full = jnp.concatenate(dv_acc, axis=1)
            dk_ref[...] = ck_scr[...] + dk_full[:BLOCK]
            dv_ref[...] = cv_scr[...] + dv_full[:BLOCK]
            ck_scr[...] = dk_full[BLOCK:]
            cv_scr[...] = dv_full[BLOCK:]
            gs_ref[...] += gs

    cur = lambda i: jnp.minimum(i, nb - 1)
    prev = lambda i: jnp.maximum(jnp.minimum(i, nb - 1) - 1, 0)
    done = lambda i: jnp.maximum(i - 1, 0)
    bs = pl.BlockSpec
    return _call(
        body, "attn_bwd", (nb + 1,),
        [bs((BLOCK, ATTN_W), lambda i: (cur(i), 0)),
         bs((BLOCK, KV_W), lambda i: (cur(i), 0)), bs((BLOCK, KV_W), lambda i: (prev(i), 0)),
         bs((BLOCK, KV_W), lambda i: (cur(i), 5)), bs((BLOCK, KV_W), lambda i: (prev(i), 5)),
         bs((BLOCK, ATTN_W), lambda i: (cur(i), 0)), bs((BLOCK, ATTN_W), lambda i: (cur(i), 0)),
         bs((BLOCK, 128), lambda i: (cur(i), 0)), bs(memory_space=pltpu.SMEM)],
        [bs((BLOCK, ATTN_W), lambda i: (cur(i), 0)),
         bs((BLOCK, KV_W), lambda i: (done(i), 0)), bs((BLOCK, KV_W), lambda i: (done(i), 0)),
         bs((1, 128), lambda i: (0, 0))],
        [jax.ShapeDtypeStruct((L, ATTN_W), F32), jax.ShapeDtypeStruct((L, KV_W), F32),
         jax.ShapeDtypeStruct((L, KV_W), F32), jax.ShapeDtypeStruct((1, 128), F32)],
        (q, k, k, proj, proj, d_o, o, lse, sinks),
        [pltpu.VMEM((BLOCK, KV_W), F32), pltpu.VMEM((BLOCK, KV_W), F32)], ride)


def _qk_prep_bwd(proj, tab, qw, kw, d_q, d_k, d_v, d_za, d_u, d_zs):
    L = proj.shape[0]
    tm = _tile(L, 512)
    z0 = ATTN_W + 2 * KV_W

    def body(q_ref, k_ref, t_ref, qw_ref, kw_ref, dq_ref, dk_ref, dv_ref, dza_ref, du_ref, dzs_ref,
             out_ref, gq_ref, gk_ref):
        i = pl.program_id(0)

        @pl.when(i == 0)
        def _():
            gq_ref[...] = jnp.zeros_like(gq_ref)
            gk_ref[...] = jnp.zeros_like(gk_ref)

        cos, sin = t_ref[:, :128], t_ref[:, 128:]
        gq = jnp.zeros((1, 128), F32)
        gk = jnp.zeros((1, 128), F32)
        for c in range(ATTN_W // 128):
            cs = slice(c * 128, (c + 1) * 128)
            d_raw, gw = _norm_rope_bwd(dq_ref[:, cs] * _SCALE, q_ref[:, cs], qw_ref[...], cos, sin)
            out_ref[:, cs] = d_raw.astype(BF16)
            gq = gq + jnp.sum(gw, axis=0, keepdims=True)
        for c in range(KV_W // 128):
            cs = slice(c * 128, (c + 1) * 128)
            d_raw, gw = _norm_rope_bwd(dk_ref[:, cs], k_ref[:, cs], kw_ref[...], cos, sin)
            out_ref[:, ATTN_W + c * 128:ATTN_W + (c + 1) * 128] = d_raw.astype(BF16)
            gk = gk + jnp.sum(gw, axis=0, keepdims=True)
        out_ref[:, ATTN_W + KV_W:z0] = dv_ref[...].astype(BF16)
        out_ref[:, z0:z0 + ATTN_W] = dza_ref[...]
        out_ref[:, z0 + ATTN_W:z0 + ATTN_W + SSM_W] = du_ref[...].astype(BF16)
        out_ref[:, z0 + ATTN_W + SSM_W:] = dzs_ref[...]
        gq_ref[...] += gq
        gk_ref[...] += gk

    row = pl.BlockSpec((1, 128), lambda i: (0, 0))
    blk = lambda w, c: pl.BlockSpec((tm, w), lambda i: (i, c))
    return pl.pallas_call(
        body,
        name="qk_prep_bwd",
        grid=(L // tm,),
        in_specs=[blk(ATTN_W, 0), blk(KV_W, 4), blk(256, 0), row, row, blk(ATTN_W, 0), blk(KV_W, 0), blk(KV_W, 0),
                  blk(ATTN_W, 0), blk(SSM_W, 0), blk(SSM_W, 0)],
        out_specs=[blk(IN_W, 0), row, row],
        out_shape=[jax.ShapeDtypeStruct((L, IN_W), BF16), jax.ShapeDtypeStruct((1, 128), F32),
                   jax.ShapeDtypeStruct((1, 128), F32)],
        compiler_params=_cp(("arbitrary",)),
    )(proj, proj, tab, jnp.tile(qw, 2).reshape(1, 128), jnp.tile(kw, 2).reshape(1, 128), d_q, d_k, d_v,
      d_za, d_u, d_zs)


def _cmul(a, b):
    return a[0] * b[0] - a[1] * b[1], a[0] * b[1] + a[1] * b[0]


def _cmul_conj(a, b):
    return a[0] * b[0] + a[1] * b[1], a[1] * b[0] - a[0] * b[1]


def _cadd(a, b):
    return a[0] + b[0], a[1] + b[1]


def _dot3(a, b, dn):
    ah, bh = a.astype(BF16), b.astype(BF16)
    al, bl = (a - ah.astype(F32)).astype(BF16), (b - bh.astype(F32)).astype(BF16)
    d = lambda u, v: lax.dot_general(u, v, dn, preferred_element_type=F32)
    return d(ah, bh) + d(ah, bl) + d(al, bh)


def _s5_discretise(a_re, a_im, ls, cosx, sinx, bt):
    delta = jnp.exp(ls)
    er = jnp.exp(a_re * delta)
    lb = (er * cosx, er * sinx)
    den = a_re * a_re + a_im * a_im
    coef = _cmul_conj((lb[0] - 1.0, lb[1]), (a_re, a_im))
    coef = (coef[0] / den, coef[1] / den)
    return delta, lb, coef, den, _cmul(coef, bt)


def _powers(lb):
    pw = [(jnp.ones_like(lb[0]), jnp.zeros_like(lb[0]))]
    for _ in range(CHUNK):
        pw.append(_cmul(pw[-1], lb))
    return pw


def _block_rows(a, pw, idx):
    blocks = [_cmul(a, pw[i]) for i in idx]
    return (jnp.concatenate([b[0] for b in blocks], axis=-2), jnp.concatenate([b[1] for b in blocks], axis=-2))


def _block_rows_bwd(g, a, pw, idx, g_pw):
    g_a = (jnp.zeros_like(a[0]), jnp.zeros_like(a[0]))
    for j, i in enumerate(idx):
        gj = (g[0][..., j * SSM_H:(j + 1) * SSM_H, :], g[1][..., j * SSM_H:(j + 1) * SSM_H, :])
        g_a = _cadd(g_a, _cmul_conj(gj, pw[i]))
        gp = _cmul_conj(gj, a)
        g_pw[i] = _cadd(g_pw[i], (jnp.sum(gp[0], axis=-2, keepdims=True), jnp.sum(gp[1], axis=-2, keepdims=True)))
    return g_a


_IDX_S = [CHUNK - 1 - s for s in range(CHUNK)]
_IDX_O = [t + 1 for t in range(CHUNK)]
_IDX_K = list(range(CHUNK))
_PREP_IN = 9


def _prep_args(p):
    row = lambda t: t.reshape(SSM_G, 1, SSM_P)
    xi = p["a_im"] * jnp.exp(p["log_step"])[:, None]
    return (row(p["a_re"]), row(p["a_im"]), row(jnp.broadcast_to(p["log_step"][:, None], (SSM_G, SSM_P))),
            row(jnp.cos(xi)), row(jnp.sin(xi)), p["b_re"].transpose(0, 2, 1), p["b_im"].transpose(0, 2, 1),
            p["c_re"], p["c_im"])


PREP_GROUPS = 8


def _prep_specs():
    r1 = pl.BlockSpec((PREP_GROUPS, 1, SSM_P), lambda g: (g, 0, 0))
    r16 = pl.BlockSpec((PREP_GROUPS, SSM_H, SSM_P), lambda g: (g, 0, 0))
    return [r1] * 5 + [r16] * 4, r1, r16


def _ssm_prep(p):
    def one_group(q, are, aim, ls, cosx, sinx, btr, bti, cre, cim, mt_ref, s_ref, o_ref, a_ref):
        _, lb, _, _, bb = _s5_discretise(are[q], aim[q], ls[q], cosx[q], sinx[q], (btr[q], bti[q]))
        pw = _powers(lb)
        c = (cre[q], cim[q])
        sc = _block_rows(bb, pw, _IDX_S)
        ot = _block_rows(c, pw, _IDX_O)
        ok = _block_rows(c, pw, _IDX_K)
        s_ref[q] = jnp.concatenate([sc[0], sc[1]], axis=1).astype(BF16)
        o_ref[q] = jnp.concatenate([ot[0], -ot[1]], axis=1).astype(BF16)
        a_ref[q] = jnp.concatenate([pw[CHUNK][0], pw[CHUNK][1]], axis=1)
        kt = _dot3(jnp.concatenate([bb[0], -bb[1]], axis=1), jnp.concatenate([ok[0], ok[1]], axis=1), _NT)
        lane = lax.broadcasted_iota(jnp.int32, kt.shape, 1)
        for s in range(CHUNK):
            blk = kt if s == 0 else jnp.where(lane >= SSM_H * s, pltpu.roll(kt, SSM_H * s, 1), 0.0)
            mt_ref[q, s * SSM_H:(s + 1) * SSM_H, :] = blk.astype(BF16)

    def body(*refs):
        for q in range(PREP_GROUPS):
            one_group(q, *refs)

    in_specs, r1, _ = _prep_specs()
    g3 = lambda r, c: pl.BlockSpec((PREP_GROUPS, r, c), lambda g: (g, 0, 0))
    return pl.pallas_call(
        body,
        name="ssm_prep",
        grid=(SSM_G // PREP_GROUPS,),
        in_specs=in_specs,
        out_specs=[g3(CW, CW), g3(CW, 2 * SSM_P), g3(CW, 2 * SSM_P), g3(1, 2 * SSM_P)],
        out_shape=[jax.ShapeDtypeStruct((SSM_G, CW, CW), BF16), jax.ShapeDtypeStruct((SSM_G, CW, 2 * SSM_P), BF16),
                   jax.ShapeDtypeStruct((SSM_G, CW, 2 * SSM_P), BF16),
                   jax.ShapeDtypeStruct((SSM_G, 1, 2 * SSM_P), F32)],
        compiler_params=_cp(("parallel",)),
    )(*_prep_args(p))


def _ssm_prep_bwd(p, g_mt, g_scat, g_ocat, g_a16, ride):
    def body(are, aim, ls, cosx, sinx, btr, bti, cre, cim, gmt_ref, gs_ref, go_ref, ga_ref,
             g_are, g_aim, g_ls, g_btr, g_bti, g_cre, g_cim, ga1_scr, gb1_scr):
        lam = (are[...], aim[...])
        bt = (btr[...], bti[...])
        delta, lb, coef, den, bb = _s5_discretise(lam[0], lam[1], ls[...], cosx[...], sinx[...], bt)
        pw = _powers(lb)
        c = (cre[...], cim[...])
        ok = _block_rows(c, pw, _IDX_K)
        g_pw = [(jnp.zeros_like(lb[0]), jnp.zeros_like(lb[0])) for _ in range(CHUNK + 1)]
        lane = lax.broadcasted_iota(jnp.int32, (SSM_H, CW), 1)
        for q in range(PREP_GROUPS):
            g_kt = gmt_ref[q, :SSM_H, :]
            for s in range(1, CHUNK):
                blk = gmt_ref[q, s * SSM_H:(s + 1) * SSM_H, :]
                g_kt = g_kt + jnp.where(lane < CW - SSM_H * s, pltpu.roll(blk, CW - SSM_H * s, 1), 0.0)
            a1 = jnp.concatenate([bb[0][q], -bb[1][q]], axis=1)
            b1 = jnp.concatenate([ok[0][q], ok[1][q]], axis=1)
            ga1_scr[q] = _dot3(g_kt, b1, _NN)
            gb1_scr[q] = _dot3(g_kt, a1, _TN)
        g_a1, g_b1 = ga1_scr[...], gb1_scr[...]
        g_bb = (g_a1[..., :SSM_P], -g_a1[..., SSM_P:])
        g_c = _block_rows_bwd((g_b1[..., :SSM_P], g_b1[..., SSM_P:]), c, pw, _IDX_K, g_pw)
        gs = gs_ref[...]
        g_bb = _cadd(g_bb, _block_rows_bwd((gs[..., :SSM_P], gs[..., SSM_P:]), bb, pw, _IDX_S, g_pw))
        go = go_ref[...]
        g_c = _cadd(g_c, _block_rows_bwd((go[..., :SSM_P], -go[..., SSM_P:]), c, pw, _IDX_O, g_pw))
        ga = ga_ref[...]
        g_pw[CHUNK] = _cadd(g_pw[CHUNK], (ga[..., :SSM_P], ga[..., SSM_P:]))
        g_lb = (jnp.zeros_like(lb[0]), jnp.zeros_like(lb[0]))
        for l in range(CHUNK - 1, -1, -1):
            g_lb = _cadd(g_lb, _cmul_conj(g_pw[l + 1], pw[l]))
            g_pw[l] = _cadd(g_pw[l], _cmul_conj(g_pw[l + 1], lb))
        g_bt = _cmul_conj(g_bb, coef)
        gc = _cmul_conj(g_bb, bt)
        g_coef = (jnp.sum(gc[0], axis=-2, keepdims=True), jnp.sum(gc[1], axis=-2, keepdims=True))
        lam_den = (lam[0] / den, lam[1] / den)
        g_lb = _cadd(g_lb, _cmul(g_coef, lam_den))
        t = _cmul(_cmul_conj(g_coef, coef), lam_den)
        g_x = _cmul_conj(g_lb, lb)
        g_are[...] = g_x[0] * delta - t[0]
        g_aim[...] = g_x[1] * delta - t[1]
        g_ls[...] = (g_x[0] * lam[0] + g_x[1] * lam[1]) * delta
        g_btr[...] = g_bt[0]
        g_bti[...] = g_bt[1]
        g_cre[...] = g_c[0]
        g_cim[...] = g_c[1]

    in_specs, r1, r16 = _prep_specs()
    g3 = lambda r, c: pl.BlockSpec((PREP_GROUPS, r, c), lambda g: (g, 0, 0))
    rows = jax.ShapeDtypeStruct((SSM_G, 1, SSM_P), F32)
    mats = jax.ShapeDtypeStruct((SSM_G, SSM_H, SSM_P), F32)
    (g_are, g_aim, g_ls, g_btr, g_bti, g_cre, g_cim), landed = _call(
        body, "ssm_prep_bwd", (SSM_G // PREP_GROUPS,),
        in_specs + [g3(CW, CW), g3(CW, 2 * SSM_P), g3(CW, 2 * SSM_P), g3(1, 2 * SSM_P)],
        [r1] * 3 + [r16] * 4, [rows] * 3 + [mats] * 4, (*_prep_args(p), g_mt, g_scat, g_ocat, g_a16),
        [pltpu.VMEM((PREP_GROUPS, SSM_H, 2 * SSM_P), F32), pltpu.VMEM((PREP_GROUPS, CW, 2 * SSM_P), F32)], ride)
    grads = dict(a_re=g_are.reshape(SSM_G, SSM_P), a_im=g_aim.reshape(SSM_G, SSM_P),
                 log_step=jnp.sum(g_ls.reshape(SSM_G, SSM_P), axis=1),
                 b_re=g_btr.transpose(0, 2, 1), b_im=g_bti.transpose(0, 2, 1), c_re=g_cre, c_im=g_cim)
    return grads, landed


def _cmul_const(xv, ar, ai):
    return xv * ar + pltpu.roll(xv, SSM_P, 1) * ai


def _chunk_scan(inc, a_row, reverse):
    n = inc.shape[0]
    lane = lax.broadcasted_iota(jnp.int32, (1, 2 * SSM_P), 1)
    row = lax.broadcasted_iota(jnp.int32, inc.shape, 0)
    sign = jnp.where(lane < SSM_P, -1.0, 1.0)
    ar = jnp.where(lane < SSM_P, a_row, pltpu.roll(a_row, SSM_P, 1))
    ai = jnp.where(lane < SSM_P, pltpu.roll(a_row, SSM_P, 1), a_row)
    if reverse:
        ai = -ai
    xv = inc
    s = 1
    while s < n:
        if reverse:
            sh = jnp.where(row < n - s, pltpu.roll(xv, n - s, 0), 0.0)
        else:
            sh = jnp.where(row >= s, pltpu.roll(xv, s, 0), 0.0)
        xv = xv + _cmul_const(sh, ar, ai * sign)
        ar, ai = ar * ar - ai * ai, 2.0 * ar * ai
        s *= 2
    return xv


def _shift_rows(xv, reverse):
    n = xv.shape[0]
    row = lax.broadcasted_iota(jnp.int32, xv.shape, 0)
    if reverse:
        return jnp.where(row < n - 1, pltpu.roll(xv, n - 1, 0), 0.0)
    return jnp.where(row >= 1, pltpu.roll(xv, 1, 0), 0.0)


GB = 128 // SSM_H
U_COL0 = (ATTN_W + 2 * KV_W + ATTN_W) // 128


HALF = CHUNK // 2


def _chunk_perm():
    r = jnp.arange(HALF * 128)
    t, g8, h = r // 128, (r % 128) // SSM_H, r % SSM_H
    return ((g8 * 128 + t * SSM_H + h)[:, None] == jnp.arange(GB * 128)[None, :]).astype(BF16)


def _load_perm(p_hbm, p_scr, sem):
    @pl.when(pl.program_id(0) == 0)
    def _():
        cp = pltpu.make_async_copy(p_hbm, p_scr, sem)
        cp.start()
        cp.wait()


def _rows_to_chunks(pieces, perm):
    halves = [jnp.dot(jnp.concatenate(pieces[k * HALF:(k + 1) * HALF], axis=1).astype(BF16), perm,
                      preferred_element_type=F32).astype(BF16) for k in range(2)]
    return [jnp.concatenate([hv[:, g * 128:(g + 1) * 128] for hv in halves], axis=1) for g in range(GB)]


def _chunks_to_rows(groups, perm, two_pass):
    pieces = []
    for k in range(2):
        v = jnp.concatenate([gv[:, k * 128:(k + 1) * 128] for gv in groups], axis=1)
        hi = v.astype(BF16)
        out = lax.dot_general(hi, perm, _NT, preferred_element_type=F32)
        if two_pass:
            lo = (v - hi.astype(F32)).astype(BF16)
            out = out + lax.dot_general(lo, perm, _NT, preferred_element_type=F32)
        pieces += [out[:, t * 128:(t + 1) * 128] for t in range(HALF)]
    return pieces


def _ssm_fwd(proj, perm, mt, scat, ocat, a16, d_skip, ride):
    L = proj.shape[0]
    nc = L // CHUNK

    def body(u_ref, p_hbm, mt_ref, s_ref, o_ref, a_ref, d_ref, y_ref, yg_ref, h_ref, p_scr, sem):
        _load_perm(p_hbm, p_scr, sem)
        perm = p_scr[...]
        rows = [pl.ds(t, nc, stride=CHUNK) for t in range(CHUNK)]
        us = [u_ref[r, :] for r in rows]
        ua = _rows_to_chunks(us, perm)
        ys = []
        for g in range(GB):
            uv = ua[g]
            inc = jnp.dot(uv, s_ref[g], preferred_element_type=F32)
            hx = _shift_rows(_chunk_scan(inc, a_ref[g], False), False)
            h_ref[g] = hx
            ys.append(jnp.dot(uv, mt_ref[g], preferred_element_type=F32)
                      + lax.dot_general(hx.astype(BF16), o_ref[g], _NT, preferred_element_type=F32))
        yp = _chunks_to_rows(ys, perm, True)
        for t, r in enumerate(rows):
            y = yp[t] + d_ref[...] * us[t]
            y_ref[r, :] = y
            yg_ref[r, :] = _gelu(y)

    g3 = lambda r, c: pl.BlockSpec((GB, r, c), lambda g: (g, 0, 0))
    col = pl.BlockSpec((L, 128), lambda g: (0, g))
    return _call(
        body, "ssm_fwd", (SSM_G // GB,),
        [pl.BlockSpec((L, 128), lambda g: (0, U_COL0 + g)), _ANY,
         g3(CW, CW), g3(CW, 2 * SSM_P), g3(CW, 2 * SSM_P), g3(1, 2 * SSM_P),
         pl.BlockSpec((1, 128), lambda g: (0, g))],
        [col, col, g3(nc, 2 * SSM_P)],
        [jax.ShapeDtypeStruct((L, SSM_W), F32), jax.ShapeDtypeStruct((L, SSM_W), F32),
         jax.ShapeDtypeStruct((SSM_G, nc, 2 * SSM_P), F32)],
        (proj, perm, mt, scat, ocat, a16, d_skip.reshape(1, SSM_W)),
        [pltpu.VMEM((HALF * 128, GB * 128), BF16), pltpu.SemaphoreType.DMA], ride)


def _ssm_bwd(d_yg, y, proj, hx, perm, mt, scat, ocat, a16, d_skip, ride):
    L = proj.shape[0]
    nc = L // CHUNK

    def body(dg_ref, y_ref, u_ref, h_ref, p_hbm, mt_ref, s_ref, o_ref, a_ref, d_ref,
             du_ref, gmt_ref, gs_ref, go_ref, ga_ref, gd_ref, p_scr, sem):
        _load_perm(p_hbm, p_scr, sem)
        perm = p_scr[...]
        rows = [pl.ds(t, nc, stride=CHUNK) for t in range(CHUNK)]
        us = [u_ref[r, :] for r in rows]
        dys = [dg_ref[r, :] * _dgelu(y_ref[r, :]) for r in rows]
        gd = jnp.zeros((1, 128), F32)
        for uv, dy in zip(us, dys):
            gd = gd + jnp.sum(dy * uv, axis=0, keepdims=True)
        gd_ref[...] = gd
        ua = _rows_to_chunks(us, perm)
        dya = _rows_to_chunks(dys, perm)
        lane = lax.broadcasted_iota(jnp.int32, (1, 2 * SSM_P), 1)
        dus = []
        for g in range(GB):
            uv, dy, hx_v = ua[g], dya[g], h_ref[g]
            dh = jnp.dot(dy, o_ref[g], preferred_element_type=F32)
            dinc = _shift_rows(_chunk_scan(dh, a_ref[g], True), True)
            dinc_b = dinc.astype(BF16)
            dus.append(lax.dot_general(dy, mt_ref[g], _NT, preferred_element_type=F32)
                       + lax.dot_general(dinc_b, s_ref[g], _NT, preferred_element_type=F32))
            gmt_ref[g] = lax.dot_general(uv, dy, _TN, preferred_element_type=F32)
            gs_ref[g] = lax.dot_general(uv, dinc_b, _TN, preferred_element_type=F32)
            go_ref[g] = lax.dot_general(dy, hx_v.astype(BF16), _TN, preferred_element_type=F32)
            p1 = dinc * hx_v
            p2 = pltpu.roll(dinc, SSM_P, 1) * hx_v
            t1 = jnp.sum(p1 + pltpu.roll(p1, SSM_P, 1), axis=0, keepdims=True)
            t2 = jnp.sum(p2 - pltpu.roll(p2, SSM_P, 1), axis=0, keepdims=True)
            ga_ref[g] = jnp.where(lane < SSM_P, t1, pltpu.roll(t2, SSM_P, 1))
        dup = _chunks_to_rows(dus, perm, False)
        for t, r in enumerate(rows):
            du_ref[r, :] = dup[t] + d_ref[...] * dys[t]

    g3 = lambda r, c: pl.BlockSpec((GB, r, c), lambda g: (g, 0, 0))
    col = pl.BlockSpec((L, 128), lambda g: (0, g))
    row = pl.BlockSpec((1, 128), lambda g: (0, g))
    return _call(
        body, "ssm_bwd", (SSM_G // GB,),
        [col, col, pl.BlockSpec((L, 128), lambda g: (0, U_COL0 + g)), g3(nc, 2 * SSM_P), _ANY,
         g3(CW, CW), g3(CW, 2 * SSM_P), g3(CW, 2 * SSM_P), g3(1, 2 * SSM_P), row],
        [col, g3(CW, CW), g3(CW, 2 * SSM_P), g3(CW, 2 * SSM_P), g3(1, 2 * SSM_P), row],
        [jax.ShapeDtypeStruct((L, SSM_W), F32), jax.ShapeDtypeStruct((SSM_G, CW, CW), F32),
         jax.ShapeDtypeStruct((SSM_G, CW, 2 * SSM_P), F32), jax.ShapeDtypeStruct((SSM_G, CW, 2 * SSM_P), F32),
         jax.ShapeDtypeStruct((SSM_G, 1, 2 * SSM_P), F32), jax.ShapeDtypeStruct((1, SSM_W), F32)],
        (d_yg, y, proj, hx, perm, mt, scat, ocat, a16, d_skip.reshape(1, SSM_W)),
        [pltpu.VMEM((HALF * 128, GB * 128), BF16), pltpu.SemaphoreType.DMA], ride)


def _merge(og, yg, w_glu, proj, b_glu, wa, ws):
    L = og.shape[0]
    tm = _tile(L, 256)

    def body(og_ref, yg_ref, wg_ref, z0_ref, z1_ref, b_ref, wa_ref, ws_ref, m_ref, gp_ref):
        zs = jnp.concatenate([z0_ref[...], z1_ref[...]], axis=1)
        ygv = yg_ref[...]
        gpre = jnp.dot(ygv.astype(BF16), wg_ref[...], preferred_element_type=F32)
        gp_ref[...] = gpre
        os_ = ygv * _sigmoid(gpre + b_ref[...]) * _silu(zs)
        ogv = og_ref[...]
        ra = lax.rsqrt(jnp.mean(ogv * ogv, axis=-1, keepdims=True) + NORM_EPS)
        rs = lax.rsqrt(jnp.mean(os_ * os_, axis=-1, keepdims=True) + NORM_EPS)
        m_ref[:, :ATTN_W] = (ogv * ra * wa_ref[...]).astype(BF16)
        m_ref[:, ATTN_W:] = (os_ * rs * ws_ref[...]).astype(BF16)

    row = lambda w: pl.BlockSpec((1, w), lambda i: (0, 0))
    return pl.pallas_call(
        body,
        name="merge",
        grid=(L // tm,),
        in_specs=[pl.BlockSpec((tm, ATTN_W), lambda i: (i, 0)), pl.BlockSpec((tm, SSM_W), lambda i: (i, 0)),
                  pl.BlockSpec((SSM_W, SSM_W), lambda i: (0, 0)),
                  pl.BlockSpec((tm, 512), lambda i: (i, 7)), pl.BlockSpec((tm, 512), lambda i: (i, 8)),
                  row(SSM_W), row(ATTN_W), row(SSM_W)],
        out_specs=[pl.BlockSpec((tm, D_MODEL), lambda i: (i, 0)), pl.BlockSpec((tm, SSM_W), lambda i: (i, 0))],
        out_shape=[jax.ShapeDtypeStruct((L, D_MODEL), BF16), jax.ShapeDtypeStruct((L, SSM_W), F32)],
        compiler_params=_cp(("parallel",)),
    )(og, yg, w_glu, proj, proj, b_glu.reshape(1, SSM_W), wa.reshape(1, ATTN_W), ws.reshape(1, SSM_W))


def _outproj_loss(merged, w_out, x, target):
    L = x.shape[0]
    tm, tn = _tile(L, 512), 1024
    ni, nj = L // tm, D_MODEL // tn

    def body(m_ref, w_ref, x_ref, t_ref, d_ref, db_ref, l_ref):
        out = x_ref[...] + jnp.dot(m_ref[...], w_ref[...], preferred_element_type=F32)
        diff = out - t_ref[...]
        d = diff * (1.0 / D_MODEL)
        d_ref[...] = d
        db_ref[...] = d.astype(BF16)
        l_ref[...] = jnp.full((1, 8, 128), jnp.sum(diff * diff), F32)

    return pl.pallas_call(
        body,
        name="outproj_loss",
        grid=(nj, ni),
        in_specs=[pl.BlockSpec((tm, D_MODEL), lambda j, i: (i, 0)),
                  pl.BlockSpec((D_MODEL, tn), lambda j, i: (0, j)),
                  pl.BlockSpec((tm, tn), lambda j, i: (i, j)),
                  pl.BlockSpec((tm, tn), lambda j, i: (i, j))],
        out_specs=[pl.BlockSpec((tm, tn), lambda j, i: (i, j)), pl.BlockSpec((tm, tn), lambda j, i: (i, j)),
                   pl.BlockSpec((1, 8, 128), lambda j, i: (i * nj + j, 0, 0))],
        out_shape=[jax.ShapeDtypeStruct((L, D_MODEL), F32), jax.ShapeDtypeStruct((L, D_MODEL), BF16),
                   jax.ShapeDtypeStruct((ni * nj, 8, 128), F32)],
        compiler_params=_cp(("parallel", "parallel")),
    )(merged, w_out, x, target)


def _merge_bwd(d_out_b, w_out, og, o, yg, gpre, proj, b_glu, wa, ws):
    L = og.shape[0]
    tm = _tile(L, 256)

    def body(dout_ref, wo_ref, og_ref, o_ref, yg_ref, gp_ref, za0_ref, za1_ref, zs0_ref, zs1_ref, b_ref, wa_ref,
             ws_ref, do_ref, dza_ref, dzs_ref, dg_ref, dyg_ref, gwa_ref, gws_ref, gb_ref):
        i = pl.program_id(0)

        @pl.when(i == 0)
        def _():
            gwa_ref[...] = jnp.zeros_like(gwa_ref)
            gws_ref[...] = jnp.zeros_like(gws_ref)
            gb_ref[...] = jnp.zeros_like(gb_ref)

        dm = lax.dot_general(dout_ref[...], wo_ref[...], _NT, preferred_element_type=F32)
        za = jnp.concatenate([za0_ref[...], za1_ref[...]], axis=1)
        zs = jnp.concatenate([zs0_ref[...], zs1_ref[...]], axis=1)
        ogv, dma = og_ref[...], dm[:, :ATTN_W]
        ra = lax.rsqrt(jnp.mean(ogv * ogv, axis=-1, keepdims=True) + NORM_EPS)
        xh = ogv * ra
        gwa_ref[...] += jnp.sum(dma * xh, axis=0, keepdims=True)
        gx = dma * wa_ref[...]
        d_og = ra * (gx - xh * jnp.mean(gx * xh, axis=-1, keepdims=True))
        do_ref[...] = d_og * _silu(za)
        dza_ref[...] = (d_og * o_ref[...] * _dsilu(za)).astype(BF16)
        ygv = yg_ref[...]
        sg = _sigmoid(gp_ref[...] + b_ref[...])
        y2 = ygv * sg
        sz = _silu(zs)
        os_ = y2 * sz
        dms = dm[:, ATTN_W:]
        rs = lax.rsqrt(jnp.mean(os_ * os_, axis=-1, keepdims=True) + NORM_EPS)
        xs = os_ * rs
        gws_ref[...] += jnp.sum(dms * xs, axis=0, keepdims=True)
        gxs = dms * ws_ref[...]
        d_os = rs * (gxs - xs * jnp.mean(gxs * xs, axis=-1, keepdims=True))
        dzs_ref[...] = (d_os * y2 * _dsilu(zs)).astype(BF16)
        d_y2 = d_os * sz
        d_g = d_y2 * ygv * sg * (1.0 - sg)
        dg_ref[...] = d_g.astype(BF16)
        gb_ref[...] += jnp.sum(d_g, axis=0, keepdims=True)
        dyg_ref[...] = d_y2 * sg

    row = lambda w: pl.BlockSpec((1, w), lambda i: (0, 0))
    full = lambda w: pl.BlockSpec((tm, w), lambda i: (i, 0))
    half = lambda c: pl.BlockSpec((tm, 512), lambda i: (i, c))
    return pl.pallas_call(
        body,
        name="merge_bwd",
        grid=(L // tm,),
        in_specs=[full(D_MODEL), pl.BlockSpec((D_MODEL, D_MODEL), lambda i: (0, 0)),
                  full(ATTN_W), full(ATTN_W), full(SSM_W), full(SSM_W),
                  half(3), half(4), half(7), half(8), row(SSM_W), row(ATTN_W), row(SSM_W)],
        out_specs=[full(ATTN_W), full(ATTN_W), full(SSM_W), full(SSM_W), full(SSM_W),
                   row(ATTN_W), row(SSM_W), row(SSM_W)],
        out_shape=[jax.ShapeDtypeStruct((L, ATTN_W), F32), jax.ShapeDtypeStruct((L, ATTN_W), BF16),
                   jax.ShapeDtypeStruct((L, SSM_W), BF16), jax.ShapeDtypeStruct((L, SSM_W), BF16),
                   jax.ShapeDtypeStruct((L, SSM_W), F32),
                   jax.ShapeDtypeStruct((1, ATTN_W), F32), jax.ShapeDtypeStruct((1, SSM_W), F32),
                   jax.ShapeDtypeStruct((1, SSM_W), F32)],
        compiler_params=_cp(("arbitrary",)),
    )(d_out_b, w_out, og, o, yg, gpre, proj, proj, proj, proj, b_glu.reshape(1, SSM_W), wa.reshape(1, ATTN_W),
      ws.reshape(1, SSM_W))


def _rms_bwd_x(x, norm_w, d_hn, d_out, ride):
    L = x.shape[0]
    tm = _tile(L, 256)

    def body(x_ref, w_ref, dh_ref, do_ref, gx_ref, gw_ref):
        i = pl.program_id(0)

        @pl.when(i == 0)
        def _():
            gw_ref[...] = jnp.zeros_like(gw_ref)

        xv, dh = x_ref[...], dh_ref[...]
        r = lax.rsqrt(jnp.mean(xv * xv, axis=-1, keepdims=True) + NORM_EPS)
        xh = xv * r
        gw_ref[...] += jnp.sum(dh * xh, axis=0, keepdims=True)
        gx = dh * w_ref[...]
        gx_ref[...] = do_ref[...] + r * (gx - xh * jnp.mean(gx * xh, axis=-1, keepdims=True))

    blk = pl.BlockSpec((tm, D_MODEL), lambda i: (i, 0))
    row = pl.BlockSpec((1, D_MODEL), lambda i: (0, 0))
    return _call(body, "rms_bwd_x", (L // tm,), [blk, row, blk, blk], [blk, row],
                 [jax.ShapeDtypeStruct((L, D_MODEL), F32), jax.ShapeDtypeStruct((1, D_MODEL), F32)],
                 (x, norm_w.reshape(1, D_MODEL), d_hn, d_out), ride=ride)


def _rope_table(positions):
    lane = jnp.arange(256)
    inv_freq = ROPE_THETA ** (-(2 * (lane % (HEAD_DIM // 2))).astype(F32) / HEAD_DIM)
    ang = positions.astype(F32)[:, None] * inv_freq[None, :]
    sign = jnp.where(lane % HEAD_DIM < HEAD_DIM // 2, -1.0, 1.0)
    return jnp.where(lane < 128, jnp.cos(ang), sign * jnp.sin(ang))


def _step(x, positions, target, w, core, chip):
    small = {n: w[n] for n in _SMALL}
    tab = _rope_table(positions)
    mt_b, scat_b, ocat_b, a16 = _ssm_prep(small)
    perm = _chunk_perm()
    blocks = lambda t: t.reshape(N_DEV, t.shape[0] // N_DEV, t.shape[1])

    proj, hn, wt_in = _rms_inproj_gather(x, small["norm_w"], w["w_in"].T.astype(BF16), chip)
    wt_in = wt_in.reshape(IN_W, D_MODEL)
    q_rot, k_rot = _qk_prep(proj, tab, small["q_norm_w"], small["k_norm_w"])
    (og, o, lse), (w_glu,) = _attn_fwd(q_rot, k_rot, proj, small["sinks"],
                                       _gather_exchange([w["w_glu"].astype(BF16)]))
    (y, yg, hx), (w_out,) = _ssm_fwd(proj, perm, mt_b, scat_b, ocat_b, a16, small["d_skip"],
                                     _gather_exchange([w["w_out"].astype(BF16)]))
    w_glu, w_out = w_glu.reshape(SSM_W, SSM_W), w_out.reshape(D_MODEL, D_MODEL)
    merged, gpre = _merge(og, yg, w_glu, proj, small["b_glu"], small["attn_out_norm_w"], small["ssm_out_norm_w"])
    d_out, d_out_b, loss_parts = _outproj_loss(merged, w_out, x, target)
    loss = 0.5 * jnp.sum(loss_parts[:, 0, 0]) / D_MODEL

    g_w_out = blocks(_mm(merged, d_out_b, "tn", F32, "grad_w_out"))
    d_o, d_za, d_zs, d_g, d_yg1, g_wa, g_ws, g_bglu = _merge_bwd(
        d_out_b, w_out, og, o, yg, gpre, proj, small["b_glu"], small["attn_out_norm_w"], small["ssm_out_norm_w"])
    g_w_glu = blocks(_mm(yg, d_g, "tn", F32, "grad_w_glu"))
    d_yg = _mm(d_g, w_glu, "nt", F32, "d_yg", add=d_yg1)
    (d_u, g_mt, g_scat, g_ocat, g_a16, g_dskip), (ra_out, ra_glu) = _ssm_bwd(
        d_yg, y, proj, hx, perm, mt_b, scat_b, ocat_b, a16, small["d_skip"], _pair_exchange([g_w_out, g_w_glu]))
    p_out = _pair_sum(g_w_out, ra_out, core, BF16, "pair_sum_out")
    p_glu = _pair_sum(g_w_glu, ra_glu, core, BF16, "pair_sum_glu")
    (d_q, d_k, d_v, g_sinks), (rb_out, rb_glu) = _attn_bwd(
        q_rot, k_rot, proj, small["sinks"], d_o, o, lse, _chip_exchange([p_out, p_glu]))
    d_proj, g_qw, g_kw = _qk_prep_bwd(proj, tab, small["q_norm_w"], small["k_norm_w"], d_q, d_k, d_v,
                                      d_za, d_u, d_zs)
    g_qw = g_qw[0, :HEAD_DIM] + g_qw[0, HEAD_DIM:]
    g_kw = g_kw[0, :HEAD_DIM] + g_kw[0, HEAD_DIM:]
    g_in_a = blocks(_mm(d_proj, hn, "tn", F32, "grad_w_in_a", panel=0))
    g_in_b, (ra_a,) = _mm(d_proj, hn, "tn", F32, "grad_w_in_b", panel=1, ride=_pair_exchange([g_in_a]))
    g_in_b = blocks(g_in_b)
    p_a = _pair_sum(g_in_a, ra_a, core, BF16, "pair_sum_in_a")
    d_hn, (rb_a, ra_b) = _mm(d_proj, wt_in, "nn", F32, "d_hn",
                             ride=_both(_chip_exchange([p_a]), _pair_exchange([g_in_b])))
    p_b = _pair_sum(g_in_b, ra_b, core, BF16, "pair_sum_in_b")
    g_small, (rb_b,) = _ssm_prep_bwd(small, g_mt, g_scat, g_ocat, g_a16, _chip_exchange([p_b]))
    (grad_x, g_nw), _ = _rms_bwd_x(x, small["norm_w"], d_hn, d_out, None)

    g_small.update(norm_w=g_nw.reshape(-1), q_norm_w=g_qw.reshape(-1), k_norm_w=g_kw.reshape(-1),
                   sinks=g_sinks[0, :N_HEADS], d_skip=g_dskip.reshape(-1), b_glu=g_bglu.reshape(-1),
                   attn_out_norm_w=g_wa.reshape(-1), ssm_out_norm_w=g_ws.reshape(-1))
    slab = _pack(g_small, loss).reshape(N_DEV, _PACK_ROWS // N_DEV, 128)
    (ra_s,) = _run_exchange(_pair_exchange([slab]), "pair_exchange_small")
    p_s = _pair_sum(slab, ra_s, core, F32, "pair_sum_small")
    (rb_s,) = _run_exchange(_chip_exchange([p_s]), "chip_exchange_small")
    (g_packed,) = _run_exchange(_gather_exchange([_chip_sum(p_s, rb_s, chip, "chip_sum_small")]), "gather_small")

    g_packed = g_packed.reshape(_PACK_ROWS, 128)
    grads = _unpack(g_packed, w)
    parts = dict(w_in=([p_a, p_b], [rb_a, rb_b]), w_glu=([p_glu], [rb_glu]), w_out=([p_out], [rb_out]))
    return g_packed[_LOSS_ROW, 0], grad_x, grads, parts


_ANY = pl.BlockSpec(memory_space=pl.ANY)


class _Exchange:
    def __init__(self, arrays, out_shape, sems, start, finish, relay=None):
        self.arrays, self.out_shape, self.sems, self.start, self.finish = arrays, out_shape, sems, start, finish
        self.relay = relay if relay is not None else (lambda ins, outs, sems: None)


def _gather_exchange(blocks):
    n = len(blocks)

    def parts(ins, outs, sems):
        send_sems, recv_sems, local_sems = sems
        x, y, c = lax.axis_index("x"), lax.axis_index("y"), lax.axis_index("c")
        me, sibling = (x, y, c), (x, y, 1 - c)
        chips = [(1 - x, y), (x, 1 - y), (1 - x, 1 - y)]

        def slot(k, dev):
            return outs[k].at[4 * dev[0] + 2 * dev[1] + dev[2]]

        def copy(k, q, block, to, src=None):
            return pltpu.make_async_remote_copy(
                src_ref=slot(k, block) if src is None else src, dst_ref=slot(k, block),
                send_sem=send_sems.at[k, q], recv_sem=recv_sems.at[k, q], device_id=to, device_id_type=MESH)

        mine = [pltpu.make_async_copy(ins[k], slot(k, me), local_sems.at[k]) for k in range(n)]
        first = []
        for k in range(n):
            first.append(copy(k, 0, me, sibling, src=ins[k]))
            first += [copy(k, 1 + j, me, (*chip, c), src=ins[k]) for j, chip in enumerate(chips)]
        return me, sibling, chips, c, copy, mine, first

    def start(ins, outs, sems):
        *_, mine, first = parts(ins, outs, sems)
        for cp in mine + first:
            cp.start()

    def relay(ins, outs, sems):
        me, sibling, chips, c, copy, _, _ = parts(ins, outs, sems)
        for j, chip in enumerate(chips):
            for k in range(n):
                copy(k, 1 + j, (*chip, c), me).wait_recv()
                copy(k, 4 + j, (*chip, c), sibling).start()

    def finish(ins, outs, sems):
        me, sibling, chips, c, copy, mine, first = parts(ins, outs, sems)
        for k in range(n):
            copy(k, 0, sibling, me).wait_recv()
            for j, chip in enumerate(chips):
                copy(k, 4 + j, (*chip, 1 - c), me).wait_recv()
        for cp in first + [copy(k, 4 + j, (*chip, c), sibling) for k in range(n) for j, chip in enumerate(chips)]:
            cp.wait_send()
        for cp in mine:
            cp.wait()

    return _Exchange(blocks, [jax.ShapeDtypeStruct((N_DEV,) + b.shape, b.dtype) for b in blocks],
                     [pltpu.SemaphoreType.DMA((n, 7)), pltpu.SemaphoreType.DMA((n, 7)), pltpu.SemaphoreType.DMA((n,))],
                     start, finish, relay)


def _direct_exchange(arrays, out_lead, fan, route):
    n = len(arrays)

    def copies(ins, outs, sems):
        send_sems, recv_sems = sems
        legs = route(lax.axis_index("x"), lax.axis_index("y"), lax.axis_index("c"))
        return [pltpu.make_async_remote_copy(
            src_ref=ins[k].at[src], dst_ref=outs[k].at[q], send_sem=send_sems.at[k, q], recv_sem=recv_sems.at[k, q],
            device_id=to, device_id_type=MESH) for k in range(n) for src, q, to in legs]

    def start(ins, outs, sems):
        for cp in copies(ins, outs, sems):
            cp.start()

    def finish(ins, outs, sems):
        for cp in copies(ins, outs, sems):
            cp.wait()

    return _Exchange(arrays, [jax.ShapeDtypeStruct((out_lead,) + a.shape[1:], a.dtype) for a in arrays],
                     [pltpu.SemaphoreType.DMA((n, fan)), pltpu.SemaphoreType.DMA((n, fan))], start, finish)


def _pair_exchange(grads):
    return _direct_exchange(grads, 4, 4, lambda x, y, c: [(2 * chip + (1 - c), chip, (x, y, 1 - c))
                                                          for chip in range(4)])


def _chip_exchange(parts):
    def route(x, y, c):
        chips = [(1 - x, y), (x, 1 - y), (1 - x, 1 - y)]
        return [(2 * chip[0] + chip[1], q, (*chip, c)) for q, chip in enumerate(chips)]
    return _direct_exchange(parts, 3, 3, route)


def _both(ex1, ex2):
    n1, s1 = len(ex1.arrays), len(ex1.sems)

    def halves(ins, outs, sems):
        return (ins[:n1], outs[:n1], sems[:s1]), (ins[n1:], outs[n1:], sems[s1:])

    def start(ins, outs, sems):
        h1, h2 = halves(ins, outs, sems)
        ex1.start(*h1)
        ex2.start(*h2)

    def relay(ins, outs, sems):
        h1, h2 = halves(ins, outs, sems)
        ex1.relay(*h1)
        ex2.relay(*h2)

    def finish(ins, outs, sems):
        h1, h2 = halves(ins, outs, sems)
        ex1.finish(*h1)
        ex2.finish(*h2)

    return _Exchange(list(ex1.arrays) + list(ex2.arrays), list(ex1.out_shape) + list(ex2.out_shape),
                     list(ex1.sems) + list(ex2.sems), start, finish, relay)


def _run_exchange(ex, name):
    n = len(ex.arrays)

    def body(*refs):
        ins, outs, sems = refs[:n], refs[n:2 * n], refs[2 * n:]
        ex.start(ins, outs, sems)
        ex.relay(ins, outs, sems)
        ex.finish(ins, outs, sems)

    return list(pl.pallas_call(body, name=name, in_specs=[_ANY] * n, out_specs=[_ANY] * n, out_shape=ex.out_shape,
                               scratch_shapes=ex.sems)(*ex.arrays))


def _call(body, name, grid, in_specs, out_specs, out_shape, args, scratch_shapes=(), ride=None):
    if ride is None:
        sem = ("arbitrary",) * len(grid)
        return pl.pallas_call(body, name=name, grid=grid, in_specs=in_specs, out_specs=out_specs, out_shape=out_shape,
                              scratch_shapes=list(scratch_shapes), compiler_params=_cp(sem))(*args), None
    n_in, n_out, n_scr, n_x = len(in_specs), len(out_specs), len(scratch_shapes), len(ride.arrays)

    def wrapped(*refs):
        ins, refs = refs[:n_in], refs[n_in:]
        x_in, refs = refs[:n_x], refs[n_x:]
        outs, refs = refs[:n_out], refs[n_out:]
        x_out, refs = refs[:n_x], refs[n_x:]
        scr, sems = refs[:n_scr], refs[n_scr:]
        step, total = pl.program_id(0), grid[0]
        for a in range(1, len(grid)):
            step, total = step * grid[a] + pl.program_id(a), total * grid[a]
        @pl.when(step == 0)
        def _():
            ride.start(x_in, x_out, sems)

        @pl.when(step == max(total - 2, 0))
        def _():
            ride.relay(x_in, x_out, sems)

        body(*ins, *outs, *scr)

        @pl.when(step == total - 1)
        def _():
            ride.finish(x_in, x_out, sems)

    res = pl.pallas_call(
        wrapped, name=name, grid=grid, in_specs=list(in_specs) + [_ANY] * n_x,
        out_specs=list(out_specs) + [_ANY] * n_x, out_shape=list(out_shape) + list(ride.out_shape),
        scratch_shapes=list(scratch_shapes) + list(ride.sems),
        compiler_params=_cp(("arbitrary",) * len(grid)))(*args, *ride.arrays)
    return res[:n_out], list(res[n_out:])


def _pair_sum(g, ra, core, out_dtype, name):
    _, r, C = g.shape
    tr = _tile(r, 576)

    def body(c_ref, g_ref, ra_ref, p_ref):
        p_ref[...] = (g_ref[...] + ra_ref[...]).astype(p_ref.dtype)

    return pl.pallas_call(
        body,
        name=name,
        grid_spec=pltpu.PrefetchScalarGridSpec(
            num_scalar_prefetch=1,
            grid=(4, r // tr),
            in_specs=[pl.BlockSpec((1, tr, C), lambda j, t, c_ref: (2 * j + c_ref[0], t, 0)),
                      pl.BlockSpec((1, tr, C), lambda j, t, c_ref: (j, t, 0))],
            out_specs=pl.BlockSpec((1, tr, C), lambda j, t, c_ref: (j, t, 0)),
        ),
        out_shape=jax.ShapeDtypeStruct((4, r, C), out_dtype),
        compiler_params=_cp(("parallel", "parallel")),
    )(core, g, ra)


def _chip_sum(p, rb, chip, name):
    _, r, C = p.shape
    tr = _tile(r, 576)

    def body(c_ref, p_ref, rb_ref, o_ref):
        acc = p_ref[0].astype(F32) + rb_ref[0].astype(F32)
        acc = acc + rb_ref[1].astype(F32)
        o_ref[...] = acc + rb_ref[2].astype(F32)

    return pl.pallas_call(
        body,
        name=name,
        grid_spec=pltpu.PrefetchScalarGridSpec(
            num_scalar_prefetch=1,
            grid=(r // tr,),
            in_specs=[pl.BlockSpec((1, tr, C), lambda t, c_ref: (c_ref[0], t, 0)),
                      pl.BlockSpec((3, tr, C), lambda t, c_ref: (0, t, 0))],
            out_specs=pl.BlockSpec((tr, C), lambda t, c_ref: (t, 0)),
        ),
        out_shape=jax.ShapeDtypeStruct((r, C), F32),
        compiler_params=_cp(("parallel",)),
    )(chip, p, rb)


def _adamw_reduced(ps, rbs, chip, w, m, v, name):
    nh = len(ps)
    R, C = w.shape
    ch = C // nh
    tr = _tile(R, 288)
    nt = R // tr
    c1 = 1.0 - ADAM_B1 ** ADAM_STEP
    c2 = 1.0 - ADAM_B2 ** ADAM_STEP

    def body(c_ref, *refs):
        p_refs, rb_refs = refs[:nh], refs[nh:2 * nh]
        w_ref, m_ref, v_ref, g_ref, d_ref, nm_ref, nv_ref = refs[2 * nh:]
        for h in range(nh):
            @pl.when(pl.program_id(0) == h)
            def _(h=h):
                rb = rb_refs[h]
                gv = p_refs[h][0].astype(F32) + rb[0].astype(F32)
                gv = gv + rb[1].astype(F32)
                gv = gv + rb[2].astype(F32)
                nm = ADAM_B1 * m_ref[...] + (1.0 - ADAM_B1) * gv
                nv = ADAM_B2 * v_ref[...] + (1.0 - ADAM_B2) * (gv * gv)
                g_ref[...] = gv
                nm_ref[...] = nm
                nv_ref[...] = nv
                d_ref[...] = -ADAM_LR * ((nm / c1) / (jnp.sqrt(nv / c2) + ADAM_EPS) + ADAM_WD * w_ref[...])

    def held(h):
        return lambda hh, tt: jnp.where(hh == h, tt, jnp.where(hh < h, 0, nt - 1))

    p_specs = [pl.BlockSpec((1, tr, ch), lambda hh, tt, c_ref, f=held(h): (c_ref[0], f(hh, tt), 0))
               for h in range(nh)]
    rb_specs = [pl.BlockSpec((3, tr, ch), lambda hh, tt, c_ref, f=held(h): (0, f(hh, tt), 0)) for h in range(nh)]
    blk = pl.BlockSpec((tr, ch), lambda hh, tt, c_ref: (tt, hh))
    return pl.pallas_call(
        body,
        name=name,
        grid_spec=pltpu.PrefetchScalarGridSpec(
            num_scalar_prefetch=1, grid=(nh, nt), in_specs=p_specs + rb_specs + [blk] * 3, out_specs=[blk] * 4),
        out_shape=[jax.ShapeDtypeStruct((R, C), F32)] * 4,
        compiler_params=_cp(("arbitrary", "arbitrary")),
    )(chip, *ps, *rbs, w, m, v)


_SMALL = ("norm_w", "q_norm_w", "k_norm_w", "sinks", "a_re", "a_im", "log_step", "b_re", "b_im", "c_re", "c_im",
          "d_skip", "b_glu", "attn_out_norm_w", "ssm_out_norm_w")
_WEIGHTS = ("norm_w", "w_in", "q_norm_w", "k_norm_w", "sinks", "a_re", "a_im", "log_step", "b_re", "b_im", "c_re",
            "c_im", "d_skip", "w_glu", "b_glu", "attn_out_norm_w", "ssm_out_norm_w", "w_out")
_SMALL_2D = dict(norm_w=(1, 2048), q_norm_w=(1, 64), k_norm_w=(1, 64), sinks=(1, 16), a_re=(64, 64), a_im=(64, 64),
                 log_step=(1, 64), b_re=(1024, 64), b_im=(1024, 64), c_re=(1024, 64), c_im=(1024, 64),
                 d_skip=(1, 1024), b_glu=(1, 1024), attn_out_norm_w=(1, 1024), ssm_out_norm_w=(1, 1024))
_P_MINOR = ("b_re", "b_im")


def _flat_form(n, t):
    return t.transpose(0, 2, 1) if n in _P_MINOR else t


def _own_form(n, t, shape):
    if n in _P_MINOR:
        return t.reshape(shape[0], shape[2], shape[1]).transpose(0, 2, 1)
    return t.reshape(shape)


def _slab_rows(n):
    return -(-n // 1024) * 8


_PACK_ROWS = 2304


_LOSS_ROW = 2192


def _pack(d, loss):
    parts = []
    for n in _SMALL:
        flat = _flat_form(n, d[n]).reshape(-1).astype(F32)
        rows = _slab_rows(flat.shape[0])
        parts.append(jnp.pad(flat, (0, rows * 128 - flat.shape[0])).reshape(rows, 128))
    assert sum(p.shape[0] for p in parts) == _LOSS_ROW
    parts.append(jnp.pad(loss.reshape(1, 1), ((0, _PACK_ROWS - _LOSS_ROW - 1), (0, 127))))
    return jnp.concatenate(parts, axis=0)


def _unpack(packed, like):
    out, off = {}, 0
    for n in _SMALL:
        size = math.prod(like[n].shape)
        rows = _slab_rows(size)
        out[n] = _own_form(n, packed[off:off + rows].reshape(-1)[:size], like[n].shape)
        off += rows
    return out


def _adamw_small(g, w, m, v):
    c1 = 1.0 - ADAM_B1 ** ADAM_STEP
    c2 = 1.0 - ADAM_B2 ** ADAM_STEP
    k = len(_SMALL)

    def body(*refs):
        ins, outs = refs[:4 * k], refs[4 * k:]
        for j in range(k):
            gv, wv, mv, vv = (ins[q * k + j][...] for q in range(4))
            nm = ADAM_B1 * mv + (1.0 - ADAM_B1) * gv
            nv = ADAM_B2 * vv + (1.0 - ADAM_B2) * (gv * gv)
            outs[j][...] = -ADAM_LR * ((nm / c1) / (jnp.sqrt(nv / c2) + ADAM_EPS) + ADAM_WD * wv)
            outs[k + j][...] = nm
            outs[2 * k + j][...] = nv

    args = [_flat_form(n, d[n]).reshape(_SMALL_2D[n]) for d in (g, w, m, v) for n in _SMALL]
    shapes = [jax.ShapeDtypeStruct(_SMALL_2D[n], F32) for _ in range(3) for n in _SMALL]
    outs = pl.pallas_call(body, name="adamw_small", out_shape=shapes, compiler_params=_cp())(*args)
    res = []
    for q in range(3):
        res.append({n: _own_form(n, outs[q * k + j], w[n].shape) for j, n in enumerate(_SMALL)})
    return res


def kernel(x, positions, norm_w, w_in, q_norm_w, k_norm_w, sinks, a_re, a_im, log_step, b_re, b_im, c_re, c_im, d_skip, w_glu, b_glu, attn_out_norm_w, ssm_out_norm_w, w_out, loss_target, m_norm_w, m_w_in, m_q_norm_w, m_k_norm_w, m_sinks, m_a_re, m_a_im, m_log_step, m_b_re, m_b_im, m_c_re, m_c_im, m_d_skip, m_w_glu, m_b_glu, m_attn_out_norm_w, m_ssm_out_norm_w, m_w_out, v_norm_w, v_w_in, v_q_norm_w, v_k_norm_w, v_sinks, v_a_re, v_a_im, v_log_step, v_b_re, v_b_im, v_c_re, v_c_im, v_d_skip, v_w_glu, v_b_glu, v_attn_out_norm_w, v_ssm_out_norm_w, v_w_out):
    w = dict(norm_w=norm_w, w_in=w_in, q_norm_w=q_norm_w, k_norm_w=k_norm_w, sinks=sinks, a_re=a_re, a_im=a_im,
             log_step=log_step, b_re=b_re, b_im=b_im, c_re=c_re, c_im=c_im, d_skip=d_skip, w_glu=w_glu, b_glu=b_glu,
             attn_out_norm_w=attn_out_norm_w, ssm_out_norm_w=ssm_out_norm_w, w_out=w_out)
    m = dict(norm_w=m_norm_w, w_in=m_w_in, q_norm_w=m_q_norm_w, k_norm_w=m_k_norm_w, sinks=m_sinks, a_re=m_a_re,
             a_im=m_a_im, log_step=m_log_step, b_re=m_b_re, b_im=m_b_im, c_re=m_c_re, c_im=m_c_im, d_skip=m_d_skip,
             w_glu=m_w_glu, b_glu=m_b_glu, attn_out_norm_w=m_attn_out_norm_w, ssm_out_norm_w=m_ssm_out_norm_w,
             w_out=m_w_out)
    v = dict(norm_w=v_norm_w, w_in=v_w_in, q_norm_w=v_q_norm_w, k_norm_w=v_k_norm_w, sinks=v_sinks, a_re=v_a_re,
             a_im=v_a_im, log_step=v_log_step, b_re=v_b_re, b_im=v_b_im, c_re=v_c_re, c_im=v_c_im, d_skip=v_d_skip,
             w_glu=v_w_glu, b_glu=v_b_glu, attn_out_norm_w=v_attn_out_norm_w, ssm_out_norm_w=v_ssm_out_norm_w,
             w_out=v_w_out)
    core = lax.axis_index("c").astype(jnp.int32).reshape(1)
    chip = (2 * lax.axis_index("x") + lax.axis_index("y")).astype(jnp.int32).reshape(1)

    loss, grad_x, grads, parts = _step(x[0], positions[0], loss_target[0], w, core, chip)
    delta, new_m, new_v = {}, {}, {}
    for n in ("w_glu", "w_out"):
        grads[n], delta[n], new_m[n], new_v[n] = _adamw_reduced(*parts[n], chip, w[n], m[n], v[n], f"adamw_{n}")
    g_t, d_t, m_t, v_t = _adamw_reduced(*parts["w_in"], chip, w["w_in"].T, m["w_in"].T, v["w_in"].T, "adamw_w_in")
    grads["w_in"], delta["w_in"], new_m["w_in"], new_v["w_in"] = g_t.T, d_t.T, m_t.T, v_t.T
    d_s, m_s, v_s = _adamw_small(grads, w, m, v)
    delta.update(d_s)
    new_m.update(m_s)
    new_v.update(v_s)

    return (loss, grad_x[None], *[grads[n] for n in _WEIGHTS], *[delta[n] for n in _WEIGHTS],
            *[new_m[n] for n in _WEIGHTS], *[new_v[n] for n in _WEIGHTS])
```

```python
import math

import jax
import jax.numpy as jnp
from jax import lax
from jax.experimental import pallas as pl
from jax.experimental.pallas import tpu as pltpu

F32 = jnp.float32
BF16 = jnp.bfloat16

D_MODEL = 2048
ATTN_W = 1024
KV_W = 256
SSM_W = 1024
HEAD_DIM = 64
N_HEADS = 16
N_KV = 4
IN_W = 4608
BLOCK = 128
ROPE_THETA = 10000.0
NORM_EPS = 1e-6
SSM_G = 64
SSM_P = 64
SSM_H = 16
CHUNK = 16
CW = CHUNK * SSM_H
N_DEV = 8

ADAM_LR = 0.001
ADAM_B1 = 0.9
ADAM_B2 = 0.999
ADAM_EPS = 1e-08
ADAM_WD = 0.01
ADAM_STEP = 10

VMEM_LIMIT = 56 * 1024 * 1024
MESH = pl.DeviceIdType.MESH


def _cp(sem=None):
    if sem is None:
        return pltpu.CompilerParams(vmem_limit_bytes=VMEM_LIMIT)
    return pltpu.CompilerParams(vmem_limit_bytes=VMEM_LIMIT, dimension_semantics=sem)


def _sigmoid(x):
    return 0.5 * jnp.tanh(0.5 * x) + 0.5


def _silu(x):
    return x * _sigmoid(x)


def _dsilu(x):
    s = _sigmoid(x)
    return s * (1.0 + x * (1.0 - s))


_GELU_C = math.sqrt(2.0 / math.pi)


def _gelu(y):
    t = jnp.tanh(_GELU_C * (y + 0.044715 * y * y * y))
    return 0.5 * y * (1.0 + t)


def _dgelu(y):
    t = jnp.tanh(_GELU_C * (y + 0.044715 * y * y * y))
    return 0.5 * (1.0 + t) + 0.5 * y * (1.0 - t * t) * _GELU_C * (1.0 + 3.0 * 0.044715 * y * y)


def _tile(n, want):
    if n <= want:
        return n
    for t in range(want - want % 16, 0, -16):
        if n % t == 0:
            return t
    raise ValueError((n, want))


def _mm(a, b, mode, out_dtype, name, tm=512, tn=1024, add=None, ride=None, panel=None):
    if mode == "nn":
        (M, K), (K2, N) = a.shape, b.shape
    elif mode == "nt":
        (M, K), (N, K2) = a.shape, b.shape
    else:
        (K, M), (K2, N) = a.shape, b.shape
    assert K == K2
    tm, tn = _tile(M, tm), _tile(N, tn)
    p0 = 0
    if panel is not None:
        assert mode != "nt" and add is None
        p0, N = panel, tn
    dn = {"nn": _NN, "nt": _NT, "tn": _TN}[mode]

    def body(a_ref, b_ref, *rest):
        o_ref = rest[-1]
        acc = lax.dot_general(a_ref[...].astype(BF16), b_ref[...].astype(BF16), dn, preferred_element_type=F32)
        if add is not None:
            acc = acc + rest[0][...]
        o_ref[...] = acc.astype(o_ref.dtype)

    a_spec = pl.BlockSpec((K, tm), lambda j, i: (0, i)) if mode == "tn" else pl.BlockSpec((tm, K), lambda j, i: (i, 0))
    b_spec = (pl.BlockSpec((tn, K), lambda j, i: (j, 0)) if mode == "nt"
              else pl.BlockSpec((K, tn), lambda j, i: (0, j + p0)))
    o_spec = pl.BlockSpec((tm, tn), lambda j, i: (i, j))
    extra = () if add is None else (add,)
    if ride is not None:
        (out,), landed = _call(body, name, (N // tn, M // tm), [a_spec, b_spec] + [o_spec] * len(extra), [o_spec],
                               [jax.ShapeDtypeStruct((M, N), out_dtype)], (a, b, *extra), ride=ride)
        return out, landed
    return pl.pallas_call(
        body,
        name=name,
        grid=(N // tn, M // tm),
        in_specs=[a_spec, b_spec] + [o_spec] * len(extra),
        out_specs=o_spec,
        out_shape=jax.ShapeDtypeStruct((M, N), out_dtype),
        compiler_params=_cp(("parallel", "parallel")),
    )(a, b, *extra)


_CHIP_ORDER = (0, 2, 1, 3)


def _rms_inproj_gather(x, norm_w, wt_shard, chip):
    L = x.shape[0]
    tm = _tile(L, 512)
    ni = L // tm
    r = IN_W // N_DEV
    tn = 2 * r

    def body(chip_ref, x_ref, nw_ref, shard, proj_ref, hn_ref, wt_hbm, hn_scr, w_scr, send_sems, recv_sems, loc_sems):
        jc, i = pl.program_id(0), pl.program_id(1)
        xx, yy, c = lax.axis_index("x"), lax.axis_index("y"), lax.axis_index("c")
        me, sibling = (xx, yy, c), (xx, yy, 1 - c)
        chips = [(1 - xx, yy), (xx, 1 - yy), (1 - xx, 1 - yy)]

        def slot(dev):
            return wt_hbm.at[4 * dev[0] + 2 * dev[1] + dev[2]]

        def copy(q, block, to, src=None):
            return pltpu.make_async_remote_copy(
                src_ref=slot(block) if src is None else src, dst_ref=slot(block),
                send_sem=send_sems.at[q], recv_sem=recv_sems.at[q], device_id=to, device_id_type=MESH)

        def rows_of(buf, core):
            return w_scr.at[buf, pl.ds(pl.multiple_of(core * r, 16), r)]

        mine = pltpu.make_async_copy(shard, slot(me), loc_sems.at[0])
        sends = [copy(0, me, sibling, src=shard)] + [copy(1 + j, me, (*ch, c), src=shard) for j, ch in enumerate(chips)]
        first = jnp.logical_and(jc == 0, i == 0)

        @pl.when(first)
        def _():
            mine.start()
            for cp in sends[:3]:
                cp.start()
            own = pltpu.make_async_copy(shard, rows_of(0, c), loc_sems.at[1])
            own.start()
            copy(0, sibling, me).wait_recv()
            sib = pltpu.make_async_copy(slot(sibling), rows_of(0, 1 - c), loc_sems.at[2])
            sib.start()
            own.wait()
            sib.wait()

        def take_direct(j, ch):
            copy(1 + j, (*ch, c), me).wait_recv()
            copy(4 + j, (*ch, c), sibling).start()
            if j == 0:
                sends[1].wait_send()
                sends[2].wait_send()
                sends[3].start()
            pltpu.make_async_copy(slot((*ch, c)), rows_of((1 + j) % 2, c), loc_sems.at[1]).start()

        for j, ch in enumerate(chips):
            early = jnp.logical_and(jc == j, i == ni // 2) if j > 0 else jnp.logical_and(jc == 1, i == 0)

            @pl.when(early)
            def _(j=j, ch=ch):
                take_direct(j, ch)

            @pl.when(jnp.logical_and(jc == 1 + j, i == 0))
            def _(j=j, ch=ch):
                buf = (1 + j) % 2
                copy(4 + j, (*ch, 1 - c), me).wait_recv()
                passed = pltpu.make_async_copy(slot((*ch, 1 - c)), rows_of(buf, 1 - c), loc_sems.at[2])
                passed.start()
                pltpu.make_async_copy(slot((*ch, c)), rows_of(buf, c), loc_sems.at[1]).wait()
                passed.wait()

        rows = pl.ds(pl.multiple_of(i * tm, tm), tm)

        @pl.when(jc == 0)
        def _():
            xv = x_ref[...]
            rstd = lax.rsqrt(jnp.mean(xv * xv, axis=-1, keepdims=True) + NORM_EPS)
            hn = (xv * rstd * nw_ref[...]).astype(BF16)
            hn_scr[rows, :] = hn
            hn_ref[...] = hn

        for buf in range(2):
            @pl.when(jc % 2 == buf)
            def _(buf=buf):
                proj_ref[...] = lax.dot_general(hn_scr[rows, :], w_scr[buf], _NT, preferred_element_type=F32)

        @pl.when(jnp.logical_and(jc == 3, i == ni - 1))
        def _():
            sends[0].wait_send()
            sends[3].wait_send()
            for j, ch in enumerate(chips):
                copy(4 + j, (*ch, c), sibling).wait_send()
            mine.wait()

    def tile_of(jc, chip_ref):
        mask = jnp.where(jc == 1, _CHIP_ORDER[1], jnp.where(jc == 2, _CHIP_ORDER[2], jnp.where(jc == 3, _CHIP_ORDER[3], 0)))
        return jnp.bitwise_xor(chip_ref[0], mask)

    held = lambda jc, i: jnp.where(jc == 0, i, ni - 1)
    return pl.pallas_call(
        body,
        name="rms_inproj_gather",
        grid_spec=pltpu.PrefetchScalarGridSpec(
            num_scalar_prefetch=1,
            grid=(4, ni),
            in_specs=[pl.BlockSpec((tm, D_MODEL), lambda jc, i, ch: (held(jc, i), 0)),
                      pl.BlockSpec((1, D_MODEL), lambda jc, i, ch: (0, 0)), _ANY],
            out_specs=[pl.BlockSpec((tm, tn), lambda jc, i, ch: (i, tile_of(jc, ch))),
                       pl.BlockSpec((tm, D_MODEL), lambda jc, i, ch: (held(jc, i), 0)), _ANY],
            scratch_shapes=[pltpu.VMEM((L, D_MODEL), BF16), pltpu.VMEM((2, tn, D_MODEL), BF16),
                            pltpu.SemaphoreType.DMA((7,)), pltpu.SemaphoreType.DMA((7,)), pltpu.SemaphoreType.DMA((3,))],
        ),
        out_shape=[jax.ShapeDtypeStruct((L, IN_W), F32), jax.ShapeDtypeStruct((L, D_MODEL), BF16),
                   jax.ShapeDtypeStruct((N_DEV, r, D_MODEL), BF16)],
        compiler_params=_cp(("arbitrary", "arbitrary")),
    )(chip, x, norm_w.reshape(1, D_MODEL), wt_shard)


def _seg_sum(v):
    a = lax.broadcasted_iota(jnp.int32, (128, 128), 0) // HEAD_DIM
    b = lax.broadcasted_iota(jnp.int32, (128, 128), 1) // HEAD_DIM
    ones = jnp.where(a == b, 1.0, 0.0).astype(BF16)
    hi = v.astype(BF16)
    lo = (v - hi.astype(F32)).astype(BF16)
    return jnp.dot(hi, ones, preferred_element_type=F32) + jnp.dot(lo, ones, preferred_element_type=F32)


def _rot_half(t):
    lane = lax.broadcasted_iota(jnp.int32, t.shape, 1)
    return jnp.where(lane % HEAD_DIM < HEAD_DIM // 2, pltpu.roll(t, 128 - HEAD_DIM // 2, 1),
                     pltpu.roll(t, HEAD_DIM // 2, 1))


def _norm_rope(raw, w, cos, sin):
    r = lax.rsqrt(_seg_sum(raw * raw) * (1.0 / HEAD_DIM) + NORM_EPS)
    tn = raw * r * w
    return r, tn * cos + _rot_half(tn) * sin


def _norm_rope_bwd(d_rot, raw, w, cos, sin):
    r = lax.rsqrt(_seg_sum(raw * raw) * (1.0 / HEAD_DIM) + NORM_EPS)
    d_tn = d_rot * cos + _rot_half(d_rot * sin)
    xh = raw * r
    gw = d_tn * w
    d_raw = r * (gw - xh * (_seg_sum(gw * xh) * (1.0 / HEAD_DIM)))
    return d_raw, d_tn * xh


def _band_mask2(has_prev):
    qi = lax.broadcasted_iota(jnp.int32, (2 * BLOCK, 2 * BLOCK), 0) % BLOCK + BLOCK
    kj = lax.broadcasted_iota(jnp.int32, (2 * BLOCK, 2 * BLOCK), 1)
    rel = qi - kj
    return (rel >= 0) & (rel < BLOCK) & ((kj >= BLOCK) | has_prev)


def _half_tiles(pair):
    lo = lax.broadcasted_iota(jnp.int32, pair.shape, 1) < HEAD_DIM
    sw = pltpu.roll(pair, HEAD_DIM, 1)
    z = jnp.zeros_like(pair)
    return (jnp.where(lo, pair, z).astype(BF16), jnp.where(lo, z, sw).astype(BF16),
            jnp.where(lo, sw, z).astype(BF16), jnp.where(lo, z, pair).astype(BF16))


def _two_rows(top, bottom):
    row = lax.broadcasted_iota(jnp.int32, (2 * BLOCK, 1), 0)
    return jnp.where(row < BLOCK, top, bottom)


def _lane_col(mat, h):
    lane = lax.broadcasted_iota(jnp.int32, mat.shape, 1)
    return jnp.sum(jnp.where(lane == h, mat, 0.0), axis=1, keepdims=True)


_SCALE = 1.0 / math.sqrt(HEAD_DIM)
_NT = (((1,), (1,)), ((), ()))
_NN = (((1,), (0,)), ((), ()))
_TN = (((0,), (0,)), ((), ()))


def _qk_prep(proj, tab, qw, kw):
    L = proj.shape[0]
    tm = _tile(L, 512)

    def body(q_ref, k_ref, t_ref, qw_ref, kw_ref, qo_ref, ko_ref):
        cos, sin = t_ref[:, :128], t_ref[:, 128:]
        for c in range(ATTN_W // 128):
            _, qr = _norm_rope(q_ref[:, c * 128:(c + 1) * 128], qw_ref[...], cos, sin)
            qo_ref[:, c * 128:(c + 1) * 128] = (qr * _SCALE).astype(BF16)
        for c in range(KV_W // 128):
            _, kr = _norm_rope(k_ref[:, c * 128:(c + 1) * 128], kw_ref[...], cos, sin)
            ko_ref[:, c * 128:(c + 1) * 128] = kr.astype(BF16)

    row = pl.BlockSpec((1, 128), lambda i: (0, 0))
    return pl.pallas_call(
        body,
        name="qk_prep",
        grid=(L // tm,),
        in_specs=[pl.BlockSpec((tm, ATTN_W), lambda i: (i, 0)), pl.BlockSpec((tm, KV_W), lambda i: (i, 4)),
                  pl.BlockSpec((tm, 256), lambda i: (i, 0)), row, row],
        out_specs=[pl.BlockSpec((tm, ATTN_W), lambda i: (i, 0)), pl.BlockSpec((tm, KV_W), lambda i: (i, 0))],
        out_shape=[jax.ShapeDtypeStruct((L, ATTN_W), BF16), jax.ShapeDtypeStruct((L, KV_W), BF16)],
        compiler_params=_cp(("parallel",)),
    )(proj, proj, tab, jnp.tile(qw, 2).reshape(1, 128), jnp.tile(kw, 2).reshape(1, 128))


def _group_tiles(g, kt, vt):
    a, b = divmod(g, 2)
    return kt[a][2 * b], kt[a][2 * b + 1], vt[a][2 * b], vt[a][2 * b + 1]


def _attn_fwd(q, k, proj, sinks, ride):
    L = proj.shape[0]
    nb = L // BLOCK

    def body(q_ref, kc_ref, kp_ref, vc_ref, vp_ref, z0_ref, z1_ref, sink_ref, og_ref, o_ref, lse_ref):
        i = pl.program_id(0)
        mask = _band_mask2(i > 0)
        z = jnp.concatenate([z0_ref[...], z1_ref[...]], axis=1)
        lane = lax.broadcasted_iota(jnp.int32, (BLOCK, 128), 1)
        kt = [_half_tiles(jnp.concatenate([kp_ref[:, a * 128:(a + 1) * 128], kc_ref[:, a * 128:(a + 1) * 128]],
                                          axis=0).astype(F32)) for a in range(2)]
        vt = [_half_tiles(jnp.concatenate([vp_ref[:, a * 128:(a + 1) * 128], vc_ref[:, a * 128:(a + 1) * 128]],
                                          axis=0)) for a in range(2)]
        lse_mat = jnp.zeros((BLOCK, 128), F32)
        outs = []
        for g in range(N_KV):
            k_lo, k_hi, v_lo, v_hi = _group_tiles(g, kt, vt)
            q2 = jnp.concatenate([q_ref[:, 2 * g * 128:(2 * g + 1) * 128],
                                  q_ref[:, (2 * g + 1) * 128:(2 * g + 2) * 128]], axis=0)
            acc = jnp.zeros((2 * BLOCK, 128), F32)
            for half, (kh, vh) in enumerate(((k_lo, v_lo), (k_hi, v_hi))):
                h_top, h_bot = 4 * g + half, 4 * g + 2 + half
                s = jnp.where(mask, lax.dot_general(q2, kh, _NT, preferred_element_type=F32), -1e30)
                sink = _two_rows(sink_ref[h_top], sink_ref[h_bot])
                m = jnp.maximum(jnp.max(s, axis=-1, keepdims=True), sink)
                e = jnp.exp(s - m)
                den = jnp.sum(e, axis=-1, keepdims=True) + jnp.exp(sink - m)
                p = e * (1.0 / den)
                acc = acc + jnp.dot(p.astype(BF16), vh, preferred_element_type=F32)
                lse = m + jnp.log(den)
                lse_mat = jnp.where(lane == h_top, lse[:BLOCK], lse_mat)
                lse_mat = jnp.where(lane == h_bot, lse[BLOCK:], lse_mat)
            outs += [acc[:BLOCK], acc[BLOCK:]]
        o = jnp.concatenate(outs, axis=1)
        o_ref[...] = o
        og_ref[...] = o * _silu(z)
        lse_ref[...] = lse_mat

    prev = lambda i: jnp.maximum(i - 1, 0)
    return _call(
        body, "attn_fwd", (nb,),
        [pl.BlockSpec((BLOCK, ATTN_W), lambda i: (i, 0)),
         pl.BlockSpec((BLOCK, KV_W), lambda i: (i, 0)),
         pl.BlockSpec((BLOCK, KV_W), lambda i: (prev(i), 0)),
         pl.BlockSpec((BLOCK, KV_W), lambda i: (i, 5)),
         pl.BlockSpec((BLOCK, KV_W), lambda i: (prev(i), 5)),
         pl.BlockSpec((BLOCK, 512), lambda i: (i, 3)),
         pl.BlockSpec((BLOCK, 512), lambda i: (i, 4)),
         pl.BlockSpec(memory_space=pltpu.SMEM)],
        [pl.BlockSpec((BLOCK, ATTN_W), lambda i: (i, 0)),
         pl.BlockSpec((BLOCK, ATTN_W), lambda i: (i, 0)),
         pl.BlockSpec((BLOCK, 128), lambda i: (i, 0))],
        [jax.ShapeDtypeStruct((L, ATTN_W), F32), jax.ShapeDtypeStruct((L, ATTN_W), F32),
         jax.ShapeDtypeStruct((L, 128), F32)],
        (q, k, k, proj, proj, proj, proj, sinks), ride=ride)


def _attn_bwd(q, k, proj, sinks, d_o, o, lse, ride):
    L = proj.shape[0]
    nb = L // BLOCK

    def body(q_ref, kc_ref, kp_ref, vc_ref, vp_ref, do_ref, o_ref, lse_ref, sink_ref,
             dq_ref, dk_ref, dv_ref, gs_ref, ck_scr, cv_scr):
        i = pl.program_id(0)

        @pl.when(i == 0)
        def _():
            gs_ref[...] = jnp.zeros_like(gs_ref)
            ck_scr[...] = jnp.zeros_like(ck_scr)
            cv_scr[...] = jnp.zeros_like(cv_scr)

        @pl.when(i == nb)
        def _():
            dk_ref[...] = ck_scr[...]
            dv_ref[...] = cv_scr[...]

        @pl.when(i < nb)
        def _():
            mask = _band_mask2(i > 0)
            lane = lax.broadcasted_iota(jnp.int32, (1, 128), 1)
            lo = lax.broadcasted_iota(jnp.int32, (2 * BLOCK, 128), 1) < HEAD_DIM
            lse_c = lse_ref[...]
            kt = [_half_tiles(jnp.concatenate([kp_ref[:, a * 128:(a + 1) * 128], kc_ref[:, a * 128:(a + 1) * 128]],
                                              axis=0).astype(F32)) for a in range(2)]
            vt = [_half_tiles(jnp.concatenate([vp_ref[:, a * 128:(a + 1) * 128], vc_ref[:, a * 128:(a + 1) * 128]],
                                              axis=0)) for a in range(2)]
            gs = jnp.zeros((1, 128), F32)
            dq_parts = []
            dk_acc = [jnp.zeros((2 * BLOCK, 128), F32) for _ in range(2)]
            dv_acc = [jnp.zeros((2 * BLOCK, 128), F32) for _ in range(2)]
            for g in range(N_KV):
                a, b = divmod(g, 2)
                k_lo, k_hi, v_lo, v_hi = _group_tiles(g, kt, vt)
                t0, t1 = slice(2 * g * 128, (2 * g + 1) * 128), slice((2 * g + 1) * 128, (2 * g + 2) * 128)
                q2 = jnp.concatenate([q_ref[:, t0], q_ref[:, t1]], axis=0)
                do2 = jnp.concatenate([do_ref[:, t0], do_ref[:, t1]], axis=0)
                prod = do2 * jnp.concatenate([o_ref[:, t0], o_ref[:, t1]], axis=0)
                do2_b = do2.astype(BF16)
                dq2 = jnp.zeros((2 * BLOCK, 128), F32)
                dk_h, dv_h = [], []
                for half, (kh, vh) in enumerate(((k_lo, v_lo), (k_hi, v_hi))):
                    h_top, h_bot = 4 * g + half, 4 * g + 2 + half
                    lse = jnp.concatenate([_lane_col(lse_c, h_top), _lane_col(lse_c, h_bot)], axis=0)
                    sink = _two_rows(sink_ref[h_top], sink_ref[h_bot])
                    delta = jnp.sum(jnp.where(lo == (half == 0), prod, 0.0), axis=1, keepdims=True)
                    s = jnp.where(mask, lax.dot_general(q2, kh, _NT, preferred_element_type=F32), -1e30)
                    p = jnp.exp(s - lse)
                    dp = lax.dot_general(do2_b, vh, _NT, preferred_element_type=F32)
                    ds_b = (p * (dp - delta)).astype(BF16)
                    p_b = p.astype(BF16)
                    dq2 = dq2 + jnp.dot(ds_b, kh, preferred_element_type=F32)
                    dk_h.append(lax.dot_general(ds_b, q2, _TN, preferred_element_type=F32))
                    dv_h.append(lax.dot_general(p_b, do2_b, _TN, preferred_element_type=F32))
                    gsink = -jnp.exp(sink - lse) * delta
                    row = lax.broadcasted_iota(jnp.int32, (2 * BLOCK, 1), 0)
                    gs = gs + jnp.where(lane == h_top, jnp.sum(jnp.where(row < BLOCK, gsink, 0.0)), 0.0)
                    gs = gs + jnp.where(lane == h_bot, jnp.sum(jnp.where(row >= BLOCK, gsink, 0.0)), 0.0)
                dq_parts += [dq2[:BLOCK], dq2[BLOCK:]]
                for acc, parts in ((dk_acc, dk_h), (dv_acc, dv_h)):
                    t = jnp.where(lo, parts[0], parts[1])
                    t = t + pltpu.roll(t, HEAD_DIM, 1)
                    acc[a] = acc[a] + jnp.where(lo == (b == 0), t, 0.0)
            dq_ref[...] = jnp.concatenate(dq_parts, axis=1)
            dk_full = jnp.concatenate(dk_acc, axis=1)
            dv_full = jnp.concatenate(dv_acc, axis=1)
            dk_ref[...] = ck_scr[...] + dk_full[:BLOCK]
            dv_ref[...] = cv_scr[...] + dv_full[:BLOCK]
            ck_scr[...] = dk_full[BLOCK:]
            cv_scr[...] = dv_full[BLOCK:]
            gs_ref[...] += gs

    cur = lambda i: jnp.minimum(i, nb - 1)
    prev = lambda i: jnp.maximum(jnp.minimum(i, nb - 1) - 1, 0)
    done = lambda i: jnp.maximum(i - 1, 0)
    bs = pl.BlockSpec
    return _call(
        body, "attn_bwd", (nb + 1,),
        [bs((BLOCK, ATTN_W), lambda i: (cur(i), 0)),
         bs((BLOCK, KV_W), lambda i: (cur(i), 0)), bs((BLOCK, KV_W), lambda i: (prev(i), 0)),
         bs((BLOCK, KV_W), lambda i: (cur(i), 5)), bs((BLOCK, KV_W), lambda i: (prev(i), 5)),
         bs((BLOCK, ATTN_W), lambda i: (cur(i), 0)), bs((BLOCK, ATTN_W), lambda i: (cur(i), 0)),
         bs((BLOCK, 128), lambda i: (cur(i), 0)), bs(memory_space=pltpu.SMEM)],
        [bs((BLOCK, ATTN_W), lambda i: (cur(i), 0)),
         bs((BLOCK, KV_W), lambda i: (done(i), 0)), bs((BLOCK, KV_W), lambda i: (done(i), 0)),
         bs((1, 128), lambda i: (0, 0))],
        [jax.ShapeDtypeStruct((L, ATTN_W), F32), jax.ShapeDtypeStruct((L, KV_W), F32),
         jax.ShapeDtypeStruct((L, KV_W), F32), jax.ShapeDtypeStruct((1, 128), F32)],
        (q, k, k, proj, proj, d_o, o, lse, sinks),
        [pltpu.VMEM((BLOCK, KV_W), F32), pltpu.VMEM((BLOCK, KV_W), F32)], ride)


def _qk_prep_bwd(proj, tab, qw, kw, d_q, d_k, d_v, d_za, d_u, d_zs):
    L = proj.shape[0]
    tm = _tile(L, 512)
    z0 = ATTN_W + 2 * KV_W

    def body(q_ref, k_ref, t_ref, qw_ref, kw_ref, dq_ref, dk_ref, dv_ref, dza_ref, du_ref, dzs_ref,
             out_ref, gq_ref, gk_ref):
        i = pl.program_id(0)

        @pl.when(i == 0)
        def _():
            gq_ref[...] = jnp.zeros_like(gq_ref)
            gk_ref[...] = jnp.zeros_like(gk_ref)

        cos, sin = t_ref[:, :128], t_ref[:, 128:]
        gq = jnp.zeros((1, 128), F32)
        gk = jnp.zeros((1, 128), F32)
        for c in range(ATTN_W // 128):
            cs = slice(c * 128, (c + 1) * 128)
            d_raw, gw = _norm_rope_bwd(dq_ref[:, cs] * _SCALE, q_ref[:, cs], qw_ref[...], cos, sin)
            out_ref[:, cs] = d_raw.astype(BF16)
            gq = gq + jnp.sum(gw, axis=0, keepdims=True)
        for c in range(KV_W // 128):
            cs = slice(c * 128, (c + 1) * 128)
            d_raw, gw = _norm_rope_bwd(dk_ref[:, cs], k_ref[:, cs], kw_ref[...], cos, sin)
            out_ref[:, ATTN_W + c * 128:ATTN_W + (c + 1) * 128] = d_raw.astype(BF16)
            gk = gk + jnp.sum(gw, axis=0, keepdims=True)
        out_ref[:, ATTN_W + KV_W:z0] = dv_ref[...].astype(BF16)
        out_ref[:, z0:z0 + ATTN_W] = dza_ref[...]
        out_ref[:, z0 + ATTN_W:z0 + ATTN_W + SSM_W] = du_ref[...].astype(BF16)
        out_ref[:, z0 + ATTN_W + SSM_W:] = dzs_ref[...]
        gq_ref[...] += gq
        gk_ref[...] += gk

    row = pl.BlockSpec((1, 128), lambda i: (0, 0))
    blk = lambda w, c: pl.BlockSpec((tm, w), lambda i: (i, c))
    return pl.pallas_call(
        body,
        name="qk_prep_bwd",
        grid=(L // tm,),
        in_specs=[blk(ATTN_W, 0), blk(KV_W, 4), blk(256, 0), row, row, blk(ATTN_W, 0), blk(KV_W, 0), blk(KV_W, 0),
                  blk(ATTN_W, 0), blk(SSM_W, 0), blk(SSM_W, 0)],
        out_specs=[blk(IN_W, 0), row, row],
        out_shape=[jax.ShapeDtypeStruct((L, IN_W), BF16), jax.ShapeDtypeStruct((1, 128), F32),
                   jax.ShapeDtypeStruct((1, 128), F32)],
        compiler_params=_cp(("arbitrary",)),
    )(proj, proj, tab, jnp.tile(qw, 2).reshape(1, 128), jnp.tile(kw, 2).reshape(1, 128), d_q, d_k, d_v,
      d_za, d_u, d_zs)


def _cmul(a, b):
    return a[0] * b[0] - a[1] * b[1], a[0] * b[1] + a[1] * b[0]


def _cmul_conj(a, b):
    return a[0] * b[0] + a[1] * b[1], a[1] * b[0] - a[0] * b[1]


def _cadd(a, b):
    return a[0] + b[0], a[1] + b[1]


def _dot3(a, b, dn):
    ah, bh = a.astype(BF16), b.astype(BF16)
    al, bl = (a - ah.astype(F32)).astype(BF16), (b - bh.astype(F32)).astype(BF16)
    d = lambda u, v: lax.dot_general(u, v, dn, preferred_element_type=F32)
    return d(ah, bh) + d(ah, bl) + d(al, bh)


def _s5_discretise(a_re, a_im, ls, cosx, sinx, bt):
    delta = jnp.exp(ls)
    er = jnp.exp(a_re * delta)
    lb = (er * cosx, er * sinx)
    den = a_re * a_re + a_im * a_im
    coef = _cmul_conj((lb[0] - 1.0, lb[1]), (a_re, a_im))
    coef = (coef[0] / den, coef[1] / den)
    return delta, lb, coef, den, _cmul(coef, bt)


def _powers(lb):
    pw = [(jnp.ones_like(lb[0]), jnp.zeros_like(lb[0]))]
    for _ in range(CHUNK):
        pw.append(_cmul(pw[-1], lb))
    return pw


def _block_rows(a, pw, idx):
    blocks = [_cmul(a, pw[i]) for i in idx]
    return (jnp.concatenate([b[0] for b in blocks], axis=-2), jnp.concatenate([b[1] for b in blocks], axis=-2))


def _block_rows_bwd(g, a, pw, idx, g_pw):
    g_a = (jnp.zeros_like(a[0]), jnp.zeros_like(a[0]))
    for j, i in enumerate(idx):
        gj = (g[0][..., j * SSM_H:(j + 1) * SSM_H, :], g[1][..., j * SSM_H:(j + 1) * SSM_H, :])
        g_a = _cadd(g_a, _cmul_conj(gj, pw[i]))
        gp = _cmul_conj(gj, a)
        g_pw[i] = _cadd(g_pw[i], (jnp.sum(gp[0], axis=-2, keepdims=True), jnp.sum(gp[1], axis=-2, keepdims=True)))
    return g_a


_IDX_S = [CHUNK - 1 - s for s in range(CHUNK)]
_IDX_O = [t + 1 for t in range(CHUNK)]
_IDX_K = list(range(CHUNK))


def _prep_args(p):
    row = lambda t: t.reshape(SSM_G, 1, SSM_P)
    xi = p["a_im"] * jnp.exp(p["log_step"])[:, None]
    return (row(p["a_re"]), row(p["a_im"]), row(jnp.broadcast_to(p["log_step"][:, None], (SSM_G, SSM_P))),
            row(jnp.cos(xi)), row(jnp.sin(xi)), p["b_re"].transpose(0, 2, 1), p["b_im"].transpose(0, 2, 1),
            p["c_re"], p["c_im"])


PREP_GROUPS = 8


def _prep_specs():
    r1 = pl.BlockSpec((PREP_GROUPS, 1, SSM_P), lambda g: (g, 0, 0))
    r16 = pl.BlockSpec((PREP_GROUPS, SSM_H, SSM_P), lambda g: (g, 0, 0))
    return [r1] * 5 + [r16] * 4, r1, r16


def _ssm_prep(p):
    def one_group(q, are, aim, ls, cosx, sinx, btr, bti, cre, cim, mt_ref, s_ref, o_ref, a_ref):
        _, lb, _, _, bb = _s5_discretise(are[q], aim[q], ls[q], cosx[q], sinx[q], (btr[q], bti[q]))
        pw = _powers(lb)
        c = (cre[q], cim[q])
        sc = _block_rows(bb, pw, _IDX_S)
        ot = _block_rows(c, pw, _IDX_O)
        ok = _block_rows(c, pw, _IDX_K)
        s_ref[q] = jnp.concatenate([sc[0], sc[1]], axis=1).astype(BF16)
        o_ref[q] = jnp.concatenate([ot[0], -ot[1]], axis=1).astype(BF16)
        a_ref[q] = jnp.concatenate([pw[CHUNK][0], pw[CHUNK][1]], axis=1)
        kt = _dot3(jnp.concatenate([bb[0], -bb[1]], axis=1), jnp.concatenate([ok[0], ok[1]], axis=1), _NT)
        lane = lax.broadcasted_iota(jnp.int32, kt.shape, 1)
        for s in range(CHUNK):
            blk = kt if s == 0 else jnp.where(lane >= SSM_H * s, pltpu.roll(kt, SSM_H * s, 1), 0.0)
            mt_ref[q, s * SSM_H:(s + 1) * SSM_H, :] = blk.astype(BF16)

    def body(*refs):
        for q in range(PREP_GROUPS):
            one_group(q, *refs)

    in_specs, r1, _ = _prep_specs()
    g3 = lambda r, c: pl.BlockSpec((PREP_GROUPS, r, c), lambda g: (g, 0, 0))
    return pl.pallas_call(
        body,
        name="ssm_prep",
        grid=(SSM_G // PREP_GROUPS,),
        in_specs=in_specs,
        out_specs=[g3(CW, CW), g3(CW, 2 * SSM_P), g3(CW, 2 * SSM_P), g3(1, 2 * SSM_P)],
        out_shape=[jax.ShapeDtypeStruct((SSM_G, CW, CW), BF16), jax.ShapeDtypeStruct((SSM_G, CW, 2 * SSM_P), BF16),
                   jax.ShapeDtypeStruct((SSM_G, CW, 2 * SSM_P), BF16),
                   jax.ShapeDtypeStruct((SSM_G, 1, 2 * SSM_P), F32)],
        compiler_params=_cp(("parallel",)),
    )(*_prep_args(p))


def _ssm_prep_bwd(p, g_mt, g_scat, g_ocat, g_a16, ride):
    def body(are, aim, ls, cosx, sinx, btr, bti, cre, cim, gmt_ref, gs_ref, go_ref, ga_ref,
             g_are, g_aim, g_ls, g_btr, g_bti, g_cre, g_cim, ga1_scr, gb1_scr):
        lam = (are[...], aim[...])
        bt = (btr[...], bti[...])
        delta, lb, coef, den, bb = _s5_discretise(lam[0], lam[1], ls[...], cosx[...], sinx[...], bt)
        pw = _powers(lb)
        c = (cre[...], cim[...])
        ok = _block_rows(c, pw, _IDX_K)
        g_pw = [(jnp.zeros_like(lb[0]), jnp.zeros_like(lb[0])) for _ in range(CHUNK + 1)]
        lane = lax.broadcasted_iota(jnp.int32, (SSM_H, CW), 1)
        for q in range(PREP_GROUPS):
            g_kt = gmt_ref[q, :SSM_H, :]
            for s in range(1, CHUNK):
                blk = gmt_ref[q, s * SSM_H:(s + 1) * SSM_H, :]
                g_kt = g_kt + jnp.where(lane < CW - SSM_H * s, pltpu.roll(blk, CW - SSM_H * s, 1), 0.0)
            a1 = jnp.concatenate([bb[0][q], -bb[1][q]], axis=1)
            b1 = jnp.concatenate([ok[0][q], ok[1][q]], axis=1)
            ga1_scr[q] = _dot3(g_kt, b1, _NN)
            gb1_scr[q] = _dot3(g_kt, a1, _TN)
        g_a1, g_b1 = ga1_scr[...], gb1_scr[...]
        g_bb = (g_a1[..., :SSM_P], -g_a1[..., SSM_P:])
        g_c = _block_rows_bwd((g_b1[..., :SSM_P], g_b1[..., SSM_P:]), c, pw, _IDX_K, g_pw)
        gs = gs_ref[...]
        g_bb = _cadd(g_bb, _block_rows_bwd((gs[..., :SSM_P], gs[..., SSM_P:]), bb, pw, _IDX_S, g_pw))
        go = go_ref[...]
        g_c = _cadd(g_c, _block_rows_bwd((go[..., :SSM_P], -go[..., SSM_P:]), c, pw, _IDX_O, g_pw))
        ga = ga_ref[...]
        g_pw[CHUNK] = _cadd(g_pw[CHUNK], (ga[..., :SSM_P], ga[..., SSM_P:]))
        g_lb = (jnp.zeros_like(lb[0]), jnp.zeros_like(lb[0]))
        for l in range(CHUNK - 1, -1, -1):
            g_lb = _cadd(g_lb, _cmul_conj(g_pw[l + 1], pw[l]))
            g_pw[l] = _cadd(g_pw[l], _cmul_conj(g_pw[l + 1], lb))
        g_bt = _cmul_conj(g_bb, coef)
        gc = _cmul_conj(g_bb, bt)
        g_coef = (jnp.sum(gc[0], axis=-2, keepdims=True), jnp.sum(gc[1], axis=-2, keepdims=True))
        lam_den = (lam[0] / den, lam[1] / den)
        g_lb = _cadd(g_lb, _cmul(g_coef, lam_den))
        t = _cmul(_cmul_conj(g_coef, coef), lam_den)
        g_x = _cmul_conj(g_lb, lb)
        g_are[...] = g_x[0] * delta - t[0]
        g_aim[...] = g_x[1] * delta - t[1]
        g_ls[...] = (g_x[0] * lam[0] + g_x[1] * lam[1]) * delta
        g_btr[...] = g_bt[0]
        g_bti[...] = g_bt[1]
        g_cre[...] = g_c[0]
        g_cim[...] = g_c[1]

    in_specs, r1, r16 = _prep_specs()
    g3 = lambda r, c: pl.BlockSpec((PREP_GROUPS, r, c), lambda g: (g, 0, 0))
    rows = jax.ShapeDtypeStruct((SSM_G, 1, SSM_P), F32)
    mats = jax.ShapeDtypeStruct((SSM_G, SSM_H, SSM_P), F32)
    (g_are, g_aim, g_ls, g_btr, g_bti, g_cre, g_cim), landed = _call(
        body, "ssm_prep_bwd", (SSM_G // PREP_GROUPS,),
        in_specs + [g3(CW, CW), g3(CW, 2 * SSM_P), g3(CW, 2 * SSM_P), g3(1, 2 * SSM_P)],
        [r1] * 3 + [r16] * 4, [rows] * 3 + [mats] * 4, (*_prep_args(p), g_mt, g_scat, g_ocat, g_a16),
        [pltpu.VMEM((PREP_GROUPS, SSM_H, 2 * SSM_P), F32), pltpu.VMEM((PREP_GROUPS, CW, 2 * SSM_P), F32)], ride)
    grads = dict(a_re=g_are.reshape(SSM_G, SSM_P), a_im=g_aim.reshape(SSM_G, SSM_P),
                 log_step=jnp.sum(g_ls.reshape(SSM_G, SSM_P), axis=1),
                 b_re=g_btr.transpose(0, 2, 1), b_im=g_bti.transpose(0, 2, 1), c_re=g_cre, c_im=g_cim)
    return grads, landed


def _cmul_const(xv, ar, ai):
    return xv * ar + pltpu.roll(xv, SSM_P, 1) * ai


def _chunk_scan(inc, a_row, reverse):
    n = inc.shape[0]
    lane = lax.broadcasted_iota(jnp.int32, (1, 2 * SSM_P), 1)
    row = lax.broadcasted_iota(jnp.int32, inc.shape, 0)
    sign = jnp.where(lane < SSM_P, -1.0, 1.0)
    ar = jnp.where(lane < SSM_P, a_row, pltpu.roll(a_row, SSM_P, 1))
    ai = jnp.where(lane < SSM_P, pltpu.roll(a_row, SSM_P, 1), a_row)
    if reverse:
        ai = -ai
    xv = inc
    s = 1
    while s < n:
        if reverse:
            sh = jnp.where(row < n - s, pltpu.roll(xv, n - s, 0), 0.0)
        else:
            sh = jnp.where(row >= s, pltpu.roll(xv, s, 0), 0.0)
        xv = xv + _cmul_const(sh, ar, ai * sign)
        ar, ai = ar * ar - ai * ai, 2.0 * ar * ai
        s *= 2
    return xv


def _shift_rows(xv, reverse):
    n = xv.shape[0]
    row = lax.broadcasted_iota(jnp.int32, xv.shape, 0)
    if reverse:
        return jnp.where(row < n - 1, pltpu.roll(xv, n - 1, 0), 0.0)
    return jnp.where(row >= 1, pltpu.roll(xv, 1, 0), 0.0)


GB = 128 // SSM_H
U_COL0 = (ATTN_W + 2 * KV_W + ATTN_W) // 128


HALF = CHUNK // 2


def _chunk_perm():
    r = jnp.arange(HALF * 128)
    t, g8, h = r // 128, (r % 128) // SSM_H, r % SSM_H
    return ((g8 * 128 + t * SSM_H + h)[:, None] == jnp.arange(GB * 128)[None, :]).astype(BF16)


def _load_perm(p_hbm, p_scr, sem):
    @pl.when(pl.program_id(0) == 0)
    def _():
        cp = pltpu.make_async_copy(p_hbm, p_scr, sem)
        cp.start()
        cp.wait()


def _rows_to_chunks(pieces, perm):
    halves = [jnp.dot(jnp.concatenate(pieces[k * HALF:(k + 1) * HALF], axis=1).astype(BF16), perm,
                      preferred_element_type=F32).astype(BF16) for k in range(2)]
    return [jnp.concatenate([hv[:, g * 128:(g + 1) * 128] for hv in halves], axis=1) for g in range(GB)]


def _chunks_to_rows(groups, perm, two_pass):
    pieces = []
    for k in range(2):
        v = jnp.concatenate([gv[:, k * 128:(k + 1) * 128] for gv in groups], axis=1)
        hi = v.astype(BF16)
        out = lax.dot_general(hi, perm, _NT, preferred_element_type=F32)
        if two_pass:
            lo = (v - hi.astype(F32)).astype(BF16)
            out = out + lax.dot_general(lo, perm, _NT, preferred_element_type=F32)
        pieces += [out[:, t * 128:(t + 1) * 128] for t in range(HALF)]
    return pieces


def _ssm_fwd(proj, perm, mt, scat, ocat, a16, d_skip, ride):
    L = proj.shape[0]
    nc = L // CHUNK

    def body(u_ref, p_hbm, mt_ref, s_ref, o_ref, a_ref, d_ref, y_ref, yg_ref, h_ref, p_scr, sem):
        _load_perm(p_hbm, p_scr, sem)
        perm = p_scr[...]
        rows = [pl.ds(t, nc, stride=CHUNK) for t in range(CHUNK)]
        us = [u_ref[r, :] for r in rows]
        ua = _rows_to_chunks(us, perm)
        ys = []
        for g in range(GB):
            uv = ua[g]
            inc = jnp.dot(uv, s_ref[g], preferred_element_type=F32)
            hx = _shift_rows(_chunk_scan(inc, a_ref[g], False), False)
            h_ref[g] = hx
            ys.append(jnp.dot(uv, mt_ref[g], preferred_element_type=F32)
                      + lax.dot_general(hx.astype(BF16), o_ref[g], _NT, preferred_element_type=F32))
        yp = _chunks_to_rows(ys, perm, True)
        for t, r in enumerate(rows):
            y = yp[t] + d_ref[...] * us[t]
            y_ref[r, :] = y
            yg_ref[r, :] = _gelu(y)

    g3 = lambda r, c: pl.BlockSpec((GB, r, c), lambda g: (g, 0, 0))
    col = pl.BlockSpec((L, 128), lambda g: (0, g))
    return _call(
        body, "ssm_fwd", (SSM_G // GB,),
        [pl.BlockSpec((L, 128), lambda g: (0, U_COL0 + g)), _ANY,
         g3(CW, CW), g3(CW, 2 * SSM_P), g3(CW, 2 * SSM_P), g3(1, 2 * SSM_P),
         pl.BlockSpec((1, 128), lambda g: (0, g))],
        [col, col, g3(nc, 2 * SSM_P)],
        [jax.ShapeDtypeStruct((L, SSM_W), F32), jax.ShapeDtypeStruct((L, SSM_W), F32),
         jax.ShapeDtypeStruct((SSM_G, nc, 2 * SSM_P), F32)],
        (proj, perm, mt, scat, ocat, a16, d_skip.reshape(1, SSM_W)),
        [pltpu.VMEM((HALF * 128, GB * 128), BF16), pltpu.SemaphoreType.DMA], ride)


def _ssm_bwd(d_yg, y, proj, hx, perm, mt, scat, ocat, a16, d_skip, ride):
    L = proj.shape[0]
    nc = L // CHUNK

    def body(dg_ref, y_ref, u_ref, h_ref, p_hbm, mt_ref, s_ref, o_ref, a_ref, d_ref,
             du_ref, gmt_ref, gs_ref, go_ref, ga_ref, gd_ref, p_scr, sem):
        _load_perm(p_hbm, p_scr, sem)
        perm = p_scr[...]
        rows = [pl.ds(t, nc, stride=CHUNK) for t in range(CHUNK)]
        us = [u_ref[r, :] for r in rows]
        dys = [dg_ref[r, :] * _dgelu(y_ref[r, :]) for r in rows]
        gd = jnp.zeros((1, 128), F32)
        for uv, dy in zip(us, dys):
            gd = gd + jnp.sum(dy * uv, axis=0, keepdims=True)
        gd_ref[...] = gd
        ua = _rows_to_chunks(us, perm)
        dya = _rows_to_chunks(dys, perm)
        lane = lax.broadcasted_iota(jnp.int32, (1, 2 * SSM_P), 1)
        dus = []
        for g in range(GB):
            uv, dy, hx_v = ua[g], dya[g], h_ref[g]
            dh = jnp.dot(dy, o_ref[g], preferred_element_type=F32)
            dinc = _shift_rows(_chunk_scan(dh, a_ref[g], True), True)
            dinc_b = dinc.astype(BF16)
            dus.append(lax.dot_general(dy, mt_ref[g], _NT, preferred_element_type=F32)
                       + lax.dot_general(dinc_b, s_ref[g], _NT, preferred_element_type=F32))
            gmt_ref[g] = lax.dot_general(uv, dy, _TN, preferred_element_type=F32)
            gs_ref[g] = lax.dot_general(uv, dinc_b, _TN, preferred_element_type=F32)
            go_ref[g] = lax.dot_general(dy, hx_v.astype(BF16), _TN, preferred_element_type=F32)
            p1 = dinc * hx_v
            p2 = pltpu.roll(dinc, SSM_P, 1) * hx_v
            t1 = jnp.sum(p1 + pltpu.roll(p1, SSM_P, 1), axis=0, keepdims=True)
            t2 = jnp.sum(p2 - pltpu.roll(p2, SSM_P, 1), axis=0, keepdims=True)
            ga_ref[g] = jnp.where(lane < SSM_P, t1, pltpu.roll(t2, SSM_P, 1))
        dup = _chunks_to_rows(dus, perm, False)
        for t, r in enumerate(rows):
            du_ref[r, :] = dup[t] + d_ref[...] * dys[t]

    g3 = lambda r, c: pl.BlockSpec((GB, r, c), lambda g: (g, 0, 0))
    col = pl.BlockSpec((L, 128), lambda g: (0, g))
    row = pl.BlockSpec((1, 128), lambda g: (0, g))
    return _call(
        body, "ssm_bwd", (SSM_G // GB,),
        [col, col, pl.BlockSpec((L, 128), lambda g: (0, U_COL0 + g)), g3(nc, 2 * SSM_P), _ANY,
         g3(CW, CW), g3(CW, 2 * SSM_P), g3(CW, 2 * SSM_P), g3(1, 2 * SSM_P), row],
        [col, g3(CW, CW), g3(CW, 2 * SSM_P), g3(CW, 2 * SSM_P), g3(1, 2 * SSM_P), row],
        [jax.ShapeDtypeStruct((L, SSM_W), F32), jax.ShapeDtypeStruct((SSM_G, CW, CW), F32),
         jax.ShapeDtypeStruct((SSM_G, CW, 2 * SSM_P), F32), jax.ShapeDtypeStruct((SSM_G, CW, 2 * SSM_P), F32),
         jax.ShapeDtypeStruct((SSM_G, 1, 2 * SSM_P), F32), jax.ShapeDtypeStruct((1, SSM_W), F32)],
        (d_yg, y, proj, hx, perm, mt, scat, ocat, a16, d_skip.reshape(1, SSM_W)),
        [pltpu.VMEM((HALF * 128, GB * 128), BF16), pltpu.SemaphoreType.DMA], ride)


def _merge(og, yg, w_glu, proj, b_glu, wa, ws):
    L = og.shape[0]
    tm = _tile(L, 256)

    def body(og_ref, yg_ref, wg_ref, z0_ref, z1_ref, b_ref, wa_ref, ws_ref, m_ref, gp_ref):
        zs = jnp.concatenate([z0_ref[...], z1_ref[...]], axis=1)
        ygv = yg_ref[...]
        gpre = jnp.dot(ygv.astype(BF16), wg_ref[...], preferred_element_type=F32)
        gp_ref[...] = gpre
        os_ = ygv * _sigmoid(gpre + b_ref[...]) * _silu(zs)
        ogv = og_ref[...]
        ra = lax.rsqrt(jnp.mean(ogv * ogv, axis=-1, keepdims=True) + NORM_EPS)
        rs = lax.rsqrt(jnp.mean(os_ * os_, axis=-1, keepdims=True) + NORM_EPS)
        m_ref[:, :ATTN_W] = (ogv * ra * wa_ref[...]).astype(BF16)
        m_ref[:, ATTN_W:] = (os_ * rs * ws_ref[...]).astype(BF16)

    row = lambda w: pl.BlockSpec((1, w), lambda i: (0, 0))
    return pl.pallas_call(
        body,
        name="merge",
        grid=(L // tm,),
        in_specs=[pl.BlockSpec((tm, ATTN_W), lambda i: (i, 0)), pl.BlockSpec((tm, SSM_W), lambda i: (i, 0)),
                  pl.BlockSpec((SSM_W, SSM_W), lambda i: (0, 0)),
                  pl.BlockSpec((tm, 512), lambda i: (i, 7)), pl.BlockSpec((tm, 512), lambda i: (i, 8)),
                  row(SSM_W), row(ATTN_W), row(SSM_W)],
        out_specs=[pl.BlockSpec((tm, D_MODEL), lambda i: (i, 0)), pl.BlockSpec((tm, SSM_W), lambda i: (i, 0))],
        out_shape=[jax.ShapeDtypeStruct((L, D_MODEL), BF16), jax.ShapeDtypeStruct((L, SSM_W), F32)],
        compiler_params=_cp(("parallel",)),
    )(og, yg, w_glu, proj, proj, b_glu.reshape(1, SSM_W), wa.reshape(1, ATTN_W), ws.reshape(1, SSM_W))


def _outproj_loss(merged, w_out, x, target):
    L = x.shape[0]
    tm, tn = _tile(L, 512), 1024
    ni, nj = L // tm, D_MODEL // tn

    def body(m_ref, w_ref, x_ref, t_ref, d_ref, db_ref, l_ref):
        out = x_ref[...] + jnp.dot(m_ref[...], w_ref[...], preferred_element_type=F32)
        diff = out - t_ref[...]
        d = diff * (1.0 / D_MODEL)
        d_ref[...] = d
        db_ref[...] = d.astype(BF16)
        l_ref[...] = jnp.full((1, 8, 128), jnp.sum(diff * diff), F32)

    return pl.pallas_call(
        body,
        name="outproj_loss",
        grid=(nj, ni),
        in_specs=[pl.BlockSpec((tm, D_MODEL), lambda j, i: (i, 0)),
                  pl.BlockSpec((D_MODEL, tn), lambda j, i: (0, j)),
                  pl.BlockSpec((tm, tn), lambda j, i: (i, j)),
                  pl.BlockSpec((tm, tn), lambda j, i: (i, j))],
        out_specs=[pl.BlockSpec((tm, tn), lambda j, i: (i, j)), pl.BlockSpec((tm, tn), lambda j, i: (i, j)),
                   pl.BlockSpec((1, 8, 128), lambda j, i: (i * nj + j, 0, 0))],
        out_shape=[jax.ShapeDtypeStruct((L, D_MODEL), F32), jax.ShapeDtypeStruct((L, D_MODEL), BF16),
                   jax.ShapeDtypeStruct((ni * nj, 8, 128), F32)],
        compiler_params=_cp(("parallel", "parallel")),
    )(merged, w_out, x, target)


def _merge_bwd(d_out_b, w_out, og, o, yg, gpre, proj, b_glu, wa, ws):
    L = og.shape[0]
    tm = _tile(L, 256)

    def body(dout_ref, wo_ref, og_ref, o_ref, yg_ref, gp_ref, za0_ref, za1_ref, zs0_ref, zs1_ref, b_ref, wa_ref,
             ws_ref, do_ref, dza_ref, dzs_ref, dg_ref, dyg_ref, gwa_ref, gws_ref, gb_ref):
        i = pl.program_id(0)

        @pl.when(i == 0)
        def _():
            gwa_ref[...] = jnp.zeros_like(gwa_ref)
            gws_ref[...] = jnp.zeros_like(gws_ref)
            gb_ref[...] = jnp.zeros_like(gb_ref)

        dm = lax.dot_general(dout_ref[...], wo_ref[...], _NT, preferred_element_type=F32)
        za = jnp.concatenate([za0_ref[...], za1_ref[...]], axis=1)
        zs = jnp.concatenate([zs0_ref[...], zs1_ref[...]], axis=1)
        ogv, dma = og_ref[...], dm[:, :ATTN_W]
        ra = lax.rsqrt(jnp.mean(ogv * ogv, axis=-1, keepdims=True) + NORM_EPS)
        xh = ogv * ra
        gwa_ref[...] += jnp.sum(dma * xh, axis=0, keepdims=True)
        gx = dma * wa_ref[...]
        d_og = ra * (gx - xh * jnp.mean(gx * xh, axis=-1, keepdims=True))
        do_ref[...] = d_og * _silu(za)
        dza_ref[...] = (d_og * o_ref[...] * _dsilu(za)).astype(BF16)
        ygv = yg_ref[...]
        sg = _sigmoid(gp_ref[...] + b_ref[...])
        y2 = ygv * sg
        sz = _silu(zs)
        os_ = y2 * sz
        dms = dm[:, ATTN_W:]
        rs = lax.rsqrt(jnp.mean(os_ * os_, axis=-1, keepdims=True) + NORM_EPS)
        xs = os_ * rs
        gws_ref[...] += jnp.sum(dms * xs, axis=0, keepdims=True)
        gxs = dms * ws_ref[...]
        d_os = rs * (gxs - xs * jnp.mean(gxs * xs, axis=-1, keepdims=True))
        dzs_ref[...] = (d_os * y2 * _dsilu(zs)).astype(BF16)
        d_y2 = d_os * sz
        d_g = d_y2 * ygv * sg * (1.0 - sg)
        dg_ref[...] = d_g.astype(BF16)
        gb_ref[...] += jnp.sum(d_g, axis=0, keepdims=True)
        dyg_ref[...] = d_y2 * sg

    row = lambda w: pl.BlockSpec((1, w), lambda i: (0, 0))
    full = lambda w: pl.BlockSpec((tm, w), lambda i: (i, 0))
    half = lambda c: pl.BlockSpec((tm, 512), lambda i: (i, c))
    return pl.pallas_call(
        body,
        name="merge_bwd",
        grid=(L // tm,),
        in_specs=[full(D_MODEL), pl.BlockSpec((D_MODEL, D_MODEL), lambda i: (0, 0)),
                  full(ATTN_W), full(ATTN_W), full(SSM_W), full(SSM_W),
                  half(3), half(4), half(7), half(8), row(SSM_W), row(ATTN_W), row(SSM_W)],
        out_specs=[full(ATTN_W), full(ATTN_W), full(SSM_W), full(SSM_W), full(SSM_W),
                   row(ATTN_W), row(SSM_W), row(SSM_W)],
        out_shape=[jax.ShapeDtypeStruct((L, ATTN_W), F32), jax.ShapeDtypeStruct((L, ATTN_W), BF16),
                   jax.ShapeDtypeStruct((L, SSM_W), BF16), jax.ShapeDtypeStruct((L, SSM_W), BF16),
                   jax.ShapeDtypeStruct((L, SSM_W), F32),
                   jax.ShapeDtypeStruct((1, ATTN_W), F32), jax.ShapeDtypeStruct((1, SSM_W), F32),
                   jax.ShapeDtypeStruct((1, SSM_W), F32)],
        compiler_params=_cp(("arbitrary",)),
    )(d_out_b, w_out, og, o, yg, gpre, proj, proj, proj, proj, b_glu.reshape(1, SSM_W), wa.reshape(1, ATTN_W),
      ws.reshape(1, SSM_W))


def _rms_bwd_x(x, norm_w, d_hn, d_out, ride):
    L = x.shape[0]
    tm = _tile(L, 256)

    def body(x_ref, w_ref, dh_ref, do_ref, gx_ref, gw_ref):
        i = pl.program_id(0)

        @pl.when(i == 0)
        def _():
            gw_ref[...] = jnp.zeros_like(gw_ref)

        xv, dh = x_ref[...], dh_ref[...]
        r = lax.rsqrt(jnp.mean(xv * xv, axis=-1, keepdims=True) + NORM_EPS)
        xh = xv * r
        gw_ref[...] += jnp.sum(dh * xh, axis=0, keepdims=True)
        gx = dh * w_ref[...]
        gx_ref[...] = do_ref[...] + r * (gx - xh * jnp.mean(gx * xh, axis=-1, keepdims=True))

    blk = pl.BlockSpec((tm, D_MODEL), lambda i: (i, 0))
    row = pl.BlockSpec((1, D_MODEL), lambda i: (0, 0))
    return _call(body, "rms_bwd_x", (L // tm,), [blk, row, blk, blk], [blk, row],
                 [jax.ShapeDtypeStruct((L, D_MODEL), F32), jax.ShapeDtypeStruct((1, D_MODEL), F32)],
                 (x, norm_w.reshape(1, D_MODEL), d_hn, d_out), ride=ride)


def _rope_table(positions):
    lane = jnp.arange(256)
    inv_freq = ROPE_THETA ** (-(2 * (lane % (HEAD_DIM // 2))).astype(F32) / HEAD_DIM)
    ang = positions.astype(F32)[:, None] * inv_freq[None, :]
    sign = jnp.where(lane % HEAD_DIM < HEAD_DIM // 2, -1.0, 1.0)
    return jnp.where(lane < 128, jnp.cos(ang), sign * jnp.sin(ang))


def _step(x, positions, target, w, core, chip):
    small = {n: w[n] for n in _SMALL}
    tab = _rope_table(positions)
    mt_b, scat_b, ocat_b, a16 = _ssm_prep(small)
    perm = _chunk_perm()
    blocks = lambda t: t.reshape(N_DEV, t.shape[0] // N_DEV, t.shape[1])

    proj, hn, wt_in = _rms_inproj_gather(x, small["norm_w"], w["w_in"].T.astype(BF16), chip)
    wt_in = wt_in.reshape(IN_W, D_MODEL)
    q_rot, k_rot = _qk_prep(proj, tab, small["q_norm_w"], small["k_norm_w"])
    (og, o, lse), (w_glu,) = _attn_fwd(q_rot, k_rot, proj, small["sinks"],
                                       _gather_exchange([w["w_glu"].astype(BF16)]))
    (y, yg, hx), (w_out,) = _ssm_fwd(proj, perm, mt_b, scat_b, ocat_b, a16, small["d_skip"],
                                     _gather_exchange([w["w_out"].astype(BF16)]))
    w_glu, w_out = w_glu.reshape(SSM_W, SSM_W), w_out.reshape(D_MODEL, D_MODEL)
    merged, gpre = _merge(og, yg, w_glu, proj, small["b_glu"], small["attn_out_norm_w"], small["ssm_out_norm_w"])
    d_out, d_out_b, loss_parts = _outproj_loss(merged, w_out, x, target)
    loss = 0.5 * jnp.sum(loss_parts[:, 0, 0]) / D_MODEL

    g_w_out = blocks(_mm(merged, d_out_b, "tn", F32, "grad_w_out"))
    d_o, d_za, d_zs, d_g, d_yg1, g_wa, g_ws, g_bglu = _merge_bwd(
        d_out_b, w_out, og, o, yg, gpre, proj, small["b_glu"], small["attn_out_norm_w"], small["ssm_out_norm_w"])
    g_w_glu = blocks(_mm(yg, d_g, "tn", F32, "grad_w_glu"))
    d_yg = _mm(d_g, w_glu, "nt", F32, "d_yg", add=d_yg1)
    (d_u, g_mt, g_scat, g_ocat, g_a16, g_dskip), (ra_out, ra_glu) = _ssm_bwd(
        d_yg, y, proj, hx, perm, mt_b, scat_b, ocat_b, a16, small["d_skip"], _pair_exchange([g_w_out, g_w_glu]))
    p_out = _pair_sum(g_w_out, ra_out, core, BF16, "pair_sum_out")
    p_glu = _pair_sum(g_w_glu, ra_glu, core, BF16, "pair_sum_glu")
    (d_q, d_k, d_v, g_sinks), (rb_out, rb_glu) = _attn_bwd(
        q_rot, k_rot, proj, small["sinks"], d_o, o, lse, _chip_exchange([p_out, p_glu]))
    d_proj, g_qw, g_kw = _qk_prep_bwd(proj, tab, small["q_norm_w"], small["k_norm_w"], d_q, d_k, d_v,
                                      d_za, d_u, d_zs)
    g_qw = g_qw[0, :HEAD_DIM] + g_qw[0, HEAD_DIM:]
    g_kw = g_kw[0, :HEAD_DIM] + g_kw[0, HEAD_DIM:]
    g_in_a = blocks(_mm(d_proj, hn, "tn", F32, "grad_w_in_a", panel=0))
    g_in_b, (ra_a,) = _mm(d_proj, hn, "tn", F32, "grad_w_in_b", panel=1, ride=_pair_exchange([g_in_a]))
    g_in_b = blocks(g_in_b)
    p_a = _pair_sum(g_in_a, ra_a, core, BF16, "pair_sum_in_a")
    d_hn, (rb_a, ra_b) = _mm(d_proj, wt_in, "nn", F32, "d_hn",
                             ride=_both(_chip_exchange([p_a]), _pair_exchange([g_in_b])))
    p_b = _pair_sum(g_in_b, ra_b, core, BF16, "pair_sum_in_b")
    g_small, (rb_b,) = _ssm_prep_bwd(small, g_mt, g_scat, g_ocat, g_a16, _chip_exchange([p_b]))
    (grad_x, g_nw), _ = _rms_bwd_x(x, small["norm_w"], d_hn, d_out, None)

    g_small.update(norm_w=g_nw.reshape(-1), q_norm_w=g_qw.reshape(-1), k_norm_w=g_kw.reshape(-1),
                   sinks=g_sinks[0, :N_HEADS], d_skip=g_dskip.reshape(-1), b_glu=g_bglu.reshape(-1),
                   attn_out_norm_w=g_wa.reshape(-1), ssm_out_norm_w=g_ws.reshape(-1))
    g_packed = _slab_all_reduce(_pack(g_small, loss).reshape(N_DEV, _PACK_ROWS // N_DEV, 128))
    g_packed = g_packed.reshape(_PACK_ROWS, 128)
    grads = _unpack(g_packed, w)
    parts = dict(w_in=([p_a, p_b], [rb_a, rb_b]), w_glu=([p_glu], [rb_glu]), w_out=([p_out], [rb_out]))
    return g_packed[_LOSS_ROW, 0], grad_x, grads, parts


_ANY = pl.BlockSpec(memory_space=pl.ANY)


class _Exchange:
    def __init__(self, arrays, out_shape, sems, start, finish, relay=None):
        self.arrays, self.out_shape, self.sems, self.start, self.finish = arrays, out_shape, sems, start, finish
        self.relay = relay if relay is not None else (lambda ins, outs, sems: None)


def _gather_exchange(blocks):
    n = len(blocks)

    def parts(ins, outs, sems):
        send_sems, recv_sems, local_sems = sems
        x, y, c = lax.axis_index("x"), lax.axis_index("y"), lax.axis_index("c")
        me, sibling = (x, y, c), (x, y, 1 - c)
        chips = [(1 - x, y), (x, 1 - y), (1 - x, 1 - y)]

        def slot(k, dev):
            return outs[k].at[4 * dev[0] + 2 * dev[1] + dev[2]]

        def copy(k, q, block, to, src=None):
            return pltpu.make_async_remote_copy(
                src_ref=slot(k, block) if src is None else src, dst_ref=slot(k, block),
                send_sem=send_sems.at[k, q], recv_sem=recv_sems.at[k, q], device_id=to, device_id_type=MESH)

        mine = [pltpu.make_async_copy(ins[k], slot(k, me), local_sems.at[k]) for k in range(n)]
        first = []
        for k in range(n):
            first.append(copy(k, 0, me, sibling, src=ins[k]))
            first += [copy(k, 1 + j, me, (*chip, c), src=ins[k]) for j, chip in enumerate(chips)]
        return me, sibling, chips, c, copy, mine, first

    def start(ins, outs, sems):
        *_, mine, first = parts(ins, outs, sems)
        for cp in mine + first:
            cp.start()

    def relay(ins, outs, sems):
        me, sibling, chips, c, copy, _, _ = parts(ins, outs, sems)
        for j, chip in enumerate(chips):
            for k in range(n):
                copy(k, 1 + j, (*chip, c), me).wait_recv()
                copy(k, 4 + j, (*chip, c), sibling).start()

    def finish(ins, outs, sems):
        me, sibling, chips, c, copy, mine, first = parts(ins, outs, sems)
        for k in range(n):
            copy(k, 0, sibling, me).wait_recv()
            for j, chip in enumerate(chips):
                copy(k, 4 + j, (*chip, 1 - c), me).wait_recv()
        for cp in first + [copy(k, 4 + j, (*chip, c), sibling) for k in range(n) for j, chip in enumerate(chips)]:
            cp.wait_send()
        for cp in mine:
            cp.wait()

    return _Exchange(blocks, [jax.ShapeDtypeStruct((N_DEV,) + b.shape, b.dtype) for b in blocks],
                     [pltpu.SemaphoreType.DMA((n, 7)), pltpu.SemaphoreType.DMA((n, 7)), pltpu.SemaphoreType.DMA((n,))],
                     start, finish, relay)


def _direct_exchange(arrays, out_lead, fan, route):
    n = len(arrays)

    def copies(ins, outs, sems):
        send_sems, recv_sems = sems
        legs = route(lax.axis_index("x"), lax.axis_index("y"), lax.axis_index("c"))
        return [pltpu.make_async_remote_copy(
            src_ref=ins[k].at[src], dst_ref=outs[k].at[q], send_sem=send_sems.at[k, q], recv_sem=recv_sems.at[k, q],
            device_id=to, device_id_type=MESH) for k in range(n) for src, q, to in legs]

    def start(ins, outs, sems):
        for cp in copies(ins, outs, sems):
            cp.start()

    def finish(ins, outs, sems):
        for cp in copies(ins, outs, sems):
            cp.wait()

    return _Exchange(arrays, [jax.ShapeDtypeStruct((out_lead,) + a.shape[1:], a.dtype) for a in arrays],
                     [pltpu.SemaphoreType.DMA((n, fan)), pltpu.SemaphoreType.DMA((n, fan))], start, finish)


def _pair_exchange(grads):
    return _direct_exchange(grads, 4, 4, lambda x, y, c: [(2 * chip + (1 - c), chip, (x, y, 1 - c))
                                                          for chip in range(4)])


def _chip_exchange(parts):
    def route(x, y, c):
        chips = [(1 - x, y), (x, 1 - y), (1 - x, 1 - y)]
        return [(2 * chip[0] + chip[1], q, (*chip, c)) for q, chip in enumerate(chips)]
    return _direct_exchange(parts, 3, 3, route)


def _both(ex1, ex2):
    n1, s1 = len(ex1.arrays), len(ex1.sems)

    def halves(ins, outs, sems):
        return (ins[:n1], outs[:n1], sems[:s1]), (ins[n1:], outs[n1:], sems[s1:])

    def start(ins, outs, sems):
        h1, h2 = halves(ins, outs, sems)
        ex1.start(*h1)
        ex2.start(*h2)

    def relay(ins, outs, sems):
        h1, h2 = halves(ins, outs, sems)
        ex1.relay(*h1)
        ex2.relay(*h2)

    def finish(ins, outs, sems):
        h1, h2 = halves(ins, outs, sems)
        ex1.finish(*h1)
        ex2.finish(*h2)

    return _Exchange(list(ex1.arrays) + list(ex2.arrays), list(ex1.out_shape) + list(ex2.out_shape),
                     list(ex1.sems) + list(ex2.sems), start, finish, relay)


def _call(body, name, grid, in_specs, out_specs, out_shape, args, scratch_shapes=(), ride=None):
    if ride is None:
        sem = ("arbitrary",) * len(grid)
        return pl.pallas_call(body, name=name, grid=grid, in_specs=in_specs, out_specs=out_specs, out_shape=out_shape,
                              scratch_shapes=list(scratch_shapes), compiler_params=_cp(sem))(*args), None
    n_in, n_out, n_scr, n_x = len(in_specs), len(out_specs), len(scratch_shapes), len(ride.arrays)

    def wrapped(*refs):
        ins, refs = refs[:n_in], refs[n_in:]
        x_in, refs = refs[:n_x], refs[n_x:]
        outs, refs = refs[:n_out], refs[n_out:]
        x_out, refs = refs[:n_x], refs[n_x:]
        scr, sems = refs[:n_scr], refs[n_scr:]
        step, total = pl.program_id(0), grid[0]
        for a in range(1, len(grid)):
            step, total = step * grid[a] + pl.program_id(a), total * grid[a]
        @pl.when(step == 0)
        def _():
            ride.start(x_in, x_out, sems)

        @pl.when(step == max(total - 2, 0))
        def _():
            ride.relay(x_in, x_out, sems)

        body(*ins, *outs, *scr)

        @pl.when(step == total - 1)
        def _():
            ride.finish(x_in, x_out, sems)

    res = pl.pallas_call(
        wrapped, name=name, grid=grid, in_specs=list(in_specs) + [_ANY] * n_x,
        out_specs=list(out_specs) + [_ANY] * n_x, out_shape=list(out_shape) + list(ride.out_shape),
        scratch_shapes=list(scratch_shapes) + list(ride.sems),
        compiler_params=_cp(("arbitrary",) * len(grid)))(*args, *ride.arrays)
    return res[:n_out], list(res[n_out:])


def _pair_sum(g, ra, core, out_dtype, name):
    _, r, C = g.shape
    tr = _tile(r, 576)

    def body(c_ref, g_ref, ra_ref, p_ref):
        p_ref[...] = (g_ref[...] + ra_ref[...]).astype(p_ref.dtype)

    return pl.pallas_call(
        body,
        name=name,
        grid_spec=pltpu.PrefetchScalarGridSpec(
            num_scalar_prefetch=1,
            grid=(4, r // tr),
            in_specs=[pl.BlockSpec((1, tr, C), lambda j, t, c_ref: (2 * j + c_ref[0], t, 0)),
                      pl.BlockSpec((1, tr, C), lambda j, t, c_ref: (j, t, 0))],
            out_specs=pl.BlockSpec((1, tr, C), lambda j, t, c_ref: (j, t, 0)),
        ),
        out_shape=jax.ShapeDtypeStruct((4, r, C), out_dtype),
        compiler_params=_cp(("parallel", "parallel")),
    )(core, g, ra)


def _slab_all_reduce(slab):
    _, r, lanes = slab.shape

    def body(s_ref, o_ref, ra, rb, ps, sems_a, sems_b, sems_c):
        x, y, c = lax.axis_index("x"), lax.axis_index("y"), lax.axis_index("c")
        chips = [(1 - x, y), (x, 1 - y), (1 - x, 1 - y)]
        pair = [pltpu.make_async_remote_copy(
            src_ref=s_ref.at[2 * k + (1 - c)], dst_ref=ra.at[k], send_sem=sems_a.at[0, k], recv_sem=sems_a.at[1, k],
            device_id=(x, y, 1 - c), device_id_type=MESH) for k in range(4)]
        for cp in pair:
            cp.start()
        for cp in pair:
            cp.wait()
        for k in range(4):
            ps[k] = s_ref[2 * k + c] + ra[k]
        cross = [pltpu.make_async_remote_copy(
            src_ref=ps.at[2 * ch[0] + ch[1]], dst_ref=rb.at[q], send_sem=sems_b.at[0, q], recv_sem=sems_b.at[1, q],
            device_id=(*ch, c), device_id_type=MESH) for q, ch in enumerate(chips)]
        for cp in cross:
            cp.start()
        for cp in cross:
            cp.wait()
        me = 4 * x + 2 * y + c
        o_ref[me] = ((ps[2 * x + y] + rb[0]) + rb[1]) + rb[2]
        flips = [(dx, dy, dc) for dx in (0, 1) for dy in (0, 1) for dc in (0, 1) if dx + dy + dc]
        spread = [pltpu.make_async_remote_copy(
            src_ref=o_ref.at[me], dst_ref=o_ref.at[me], send_sem=sems_c.at[0, q], recv_sem=sems_c.at[1, q],
            device_id=(x + dx - 2 * x * dx, y + dy - 2 * y * dy, c + dc - 2 * c * dc), device_id_type=MESH)
            for q, (dx, dy, dc) in enumerate(flips)]
        for cp in spread:
            cp.start()
        for q, (dx, dy, dc) in enumerate(flips):
            peer = 4 * (x + dx - 2 * x * dx) + 2 * (y + dy - 2 * y * dy) + (c + dc - 2 * c * dc)
            pltpu.make_async_remote_copy(
                src_ref=o_ref.at[peer], dst_ref=o_ref.at[peer], send_sem=sems_c.at[0, q], recv_sem=sems_c.at[1, q],
                device_id=(x, y, c), device_id_type=MESH).wait_recv()
        for cp in spread:
            cp.wait_send()

    whole = pl.BlockSpec(memory_space=pltpu.VMEM)
    return pl.pallas_call(
        body, name="slab_all_reduce", in_specs=[whole], out_specs=whole,
        out_shape=jax.ShapeDtypeStruct(slab.shape, F32),
        scratch_shapes=[pltpu.VMEM((4, r, lanes), F32), pltpu.VMEM((3, r, lanes), F32), pltpu.VMEM((4, r, lanes), F32),
                        pltpu.SemaphoreType.DMA((2, 4)), pltpu.SemaphoreType.DMA((2, 3)),
                        pltpu.SemaphoreType.DMA((2, 7))],
        compiler_params=_cp(),
    )(slab)


def _adamw_reduced(ps, rbs, chip, w, m, v, name):
    nh = len(ps)
    R, C = w.shape
    ch = C // nh
    tr = _tile(R, 288)
    nt = R // tr
    c1 = 1.0 - ADAM_B1 ** ADAM_STEP
    c2 = 1.0 - ADAM_B2 ** ADAM_STEP

    def body(c_ref, *refs):
        p_refs, rb_refs = refs[:nh], refs[nh:2 * nh]
        w_ref, m_ref, v_ref, g_ref, d_ref, nm_ref, nv_ref = refs[2 * nh:]
        for h in range(nh):
            @pl.when(pl.program_id(0) == h)
            def _(h=h):
                rb = rb_refs[h]
                gv = p_refs[h][0].astype(F32) + rb[0].astype(F32)
                gv = gv + rb[1].astype(F32)
                gv = gv + rb[2].astype(F32)
                nm = ADAM_B1 * m_ref[...] + (1.0 - ADAM_B1) * gv
                nv = ADAM_B2 * v_ref[...] + (1.0 - ADAM_B2) * (gv * gv)
                g_ref[...] = gv
                nm_ref[...] = nm
                nv_ref[...] = nv
                d_ref[...] = -ADAM_LR * ((nm / c1) / (jnp.sqrt(nv / c2) + ADAM_EPS) + ADAM_WD * w_ref[...])

    def held(h):
        return lambda hh, tt: jnp.where(hh == h, tt, jnp.where(hh < h, 0, nt - 1))

    p_specs = [pl.BlockSpec((1, tr, ch), lambda hh, tt, c_ref, f=held(h): (c_ref[0], f(hh, tt), 0))
               for h in range(nh)]
    rb_specs = [pl.BlockSpec((3, tr, ch), lambda hh, tt, c_ref, f=held(h): (0, f(hh, tt), 0)) for h in range(nh)]
    blk = pl.BlockSpec((tr, ch), lambda hh, tt, c_ref: (tt, hh))
    return pl.pallas_call(
        body,
        name=name,
        grid_spec=pltpu.PrefetchScalarGridSpec(
            num_scalar_prefetch=1, grid=(nh, nt), in_specs=p_specs + rb_specs + [blk] * 3, out_specs=[blk] * 4),
        out_shape=[jax.ShapeDtypeStruct((R, C), F32)] * 4,
        compiler_params=_cp(("arbitrary", "arbitrary")),
    )(chip, *ps, *rbs, w, m, v)


_SMALL = ("norm_w", "q_norm_w", "k_norm_w", "sinks", "a_re", "a_im", "log_step", "b_re", "b_im", "c_re", "c_im",
          "d_skip", "b_glu", "attn_out_norm_w", "ssm_out_norm_w")
_WEIGHTS = ("norm_w", "w_in", "q_norm_w", "k_norm_w", "sinks", "a_re", "a_im", "log_step", "b_re", "b_im", "c_re",
            "c_im", "d_skip", "w_glu", "b_glu", "attn_out_norm_w", "ssm_out_norm_w", "w_out")
_SMALL_2D = dict(norm_w=(1, 2048), q_norm_w=(1, 64), k_norm_w=(1, 64), sinks=(1, 16), a_re=(64, 64), a_im=(64, 64),
                 log_step=(1, 64), b_re=(1024, 64), b_im=(1024, 64), c_re=(1024, 64), c_im=(1024, 64),
                 d_skip=(1, 1024), b_glu=(1, 1024), attn_out_norm_w=(1, 1024), ssm_out_norm_w=(1, 1024))
_P_MINOR = ("b_re", "b_im")


def _flat_form(n, t):
    return t.transpose(0, 2, 1) if n in _P_MINOR else t


def _own_form(n, t, shape):
    if n in _P_MINOR:
        return t.reshape(shape[0], shape[2], shape[1]).transpose(0, 2, 1)
    return t.reshape(shape)


def _slab_rows(n):
    return -(-n // 1024) * 8


_PACK_ROWS = 2304


_LOSS_ROW = 2192


def _pack(d, loss):
    parts = []
    for n in _SMALL:
        flat = _flat_form(n, d[n]).reshape(-1).astype(F32)
        rows = _slab_rows(flat.shape[0])
        parts.append(jnp.pad(flat, (0, rows * 128 - flat.shape[0])).reshape(rows, 128))
    assert sum(p.shape[0] for p in parts) == _LOSS_ROW
    parts.append(jnp.pad(loss.reshape(1, 1), ((0, _PACK_ROWS - _LOSS_ROW - 1), (0, 127))))
    return jnp.concatenate(parts, axis=0)


def _unpack(packed, like):
    out, off = {}, 0
    for n in _SMALL:
        size = math.prod(like[n].shape)
        rows = _slab_rows(size)
        out[n] = _own_form(n, packed[off:off + rows].reshape(-1)[:size], like[n].shape)
        off += rows
    return out


def _adamw_small(g, w, m, v):
    c1 = 1.0 - ADAM_B1 ** ADAM_STEP
    c2 = 1.0 - ADAM_B2 ** ADAM_STEP
    k = len(_SMALL)

    def body(*refs):
        ins, outs = refs[:4 * k], refs[4 * k:]
        for j in range(k):
            gv, wv, mv, vv = (ins[q * k + j][...] for q in range(4))
            nm = ADAM_B1 * mv + (1.0 - ADAM_B1) * gv
            nv = ADAM_B2 * vv + (1.0 - ADAM_B2) * (gv * gv)
            outs[j][...] = -ADAM_LR * ((nm / c1) / (jnp.sqrt(nv / c2) + ADAM_EPS) + ADAM_WD * wv)
            outs[k + j][...] = nm
            outs[2 * k + j][...] = nv

    args = [_flat_form(n, d[n]).reshape(_SMALL_2D[n]) for d in (g, w, m, v) for n in _SMALL]
    shapes = [jax.ShapeDtypeStruct(_SMALL_2D[n], F32) for _ in range(3) for n in _SMALL]
    outs = pl.pallas_call(body, name="adamw_small", out_shape=shapes, compiler_params=_cp())(*args)
    res = []
    for q in range(3):
        res.append({n: _own_form(n, outs[q * k + j], w[n].shape) for j, n in enumerate(_SMALL)})
    return res


def kernel(x, positions, norm_w, w_in, q_norm_w, k_norm_w, sinks, a_re, a_im, log_step, b_re, b_im, c_re, c_im, d_skip, w_glu, b_glu, attn_out_norm_w, ssm_out_norm_w, w_out, loss_target, m_norm_w, m_w_in, m_q_norm_w, m_k_norm_w, m_sinks, m_a_re, m_a_im, m_log_step, m_b_re, m_b_im, m_c_re, m_c_im, m_d_skip, m_w_glu, m_b_glu, m_attn_out_norm_w, m_ssm_out_norm_w, m_w_out, v_norm_w, v_w_in, v_q_norm_w, v_k_norm_w, v_sinks, v_a_re, v_a_im, v_log_step, v_b_re, v_b_im, v_c_re, v_c_im, v_d_skip, v_w_glu, v_b_glu, v_attn_out_norm_w, v_ssm_out_norm_w, v_w_out):
    w = dict(norm_w=norm_w, w_in=w_in, q_norm_w=q_norm_w, k_norm_w=k_norm_w, sinks=sinks, a_re=a_re, a_im=a_im,
             log_step=log_step, b_re=b_re, b_im=b_im, c_re=c_re, c_im=c_im, d_skip=d_skip, w_glu=w_glu, b_glu=b_glu,
             attn_out_norm_w=attn_out_norm_w, ssm_out_norm_w=ssm_out_norm_w, w_out=w_out)
    m = dict(norm_w=m_norm_w, w_in=m_w_in, q_norm_w=m_q_norm_w, k_norm_w=m_k_norm_w, sinks=m_sinks, a_re=m_a_re,
             a_im=m_a_im, log_step=m_log_step, b_re=m_b_re, b_im=m_b_im, c_re=m_c_re, c_im=m_c_im, d_skip=m_d_skip,
             w_glu=m_w_glu, b_glu=m_b_glu, attn_out_norm_w=m_attn_out_norm_w, ssm_out_norm_w=m_ssm_out_norm_w,
             w_out=m_w_out)
    v = dict(norm_w=v_norm_w, w_in=v_w_in, q_norm_w=v_q_norm_w, k_norm_w=v_k_norm_w, sinks=v_sinks, a_re=v_a_re,
             a_im=v_a_im, log_step=v_log_step, b_re=v_b_re, b_im=v_b_im, c_re=v_c_re, c_im=v_c_im, d_skip=v_d_skip,
             w_glu=v_w_glu, b_glu=v_b_glu, attn_out_norm_w=v_attn_out_norm_w, ssm_out_norm_w=v_ssm_out_norm_w,
             w_out=v_w_out)
    core = lax.axis_index("c").astype(jnp.int32).reshape(1)
    chip = (2 * lax.axis_index("x") + lax.axis_index("y")).astype(jnp.int32).reshape(1)

    loss, grad_x, grads, parts = _step(x[0], positions[0], loss_target[0], w, core, chip)
    delta, new_m, new_v = {}, {}, {}
    for n in ("w_glu", "w_out"):
        grads[n], delta[n], new_m[n], new_v[n] = _adamw_reduced(*parts[n], chip, w[n], m[n], v[n], f"adamw_{n}")
    g_t, d_t, m_t, v_t = _adamw_reduced(*parts["w_in"], chip, w["w_in"].T, m["w_in"].T, v["w_in"].T, "adamw_w_in")
    grads["w_in"], delta["w_in"], new_m["w_in"], new_v["w_in"] = g_t.T, d_t.T, m_t.T, v_t.T
    d_s, m_s, v_s = _adamw_small(grads, w, m, v)
    delta.update(d_s)
    new_m.update(m_s)
    new_v.update(v_s)

    return (loss, grad_x[None], *[grads[n] for n in _WEIGHTS], *[delta[n] for n in _WEIGHTS],
            *[new_m[n] for n in _WEIGHTS], *[new_v[n] for n in _WEIGHTS])
```

```python
import math

import jax
import jax.numpy as jnp
from jax import lax
from jax.experimental import pallas as pl
from jax.experimental.pallas import tpu as pltpu

F32 = jnp.float32
BF16 = jnp.bfloat16

D_MODEL = 2048
ATTN_W = 1024
KV_W = 256
SSM_W = 1024
HEAD_DIM = 64
N_HEADS = 16
N_KV = 4
IN_W = 4608
BLOCK = 128
ROPE_THETA = 10000.0
NORM_EPS = 1e-6
SSM_G = 64
SSM_P = 64
SSM_H = 16
CHUNK = 16
CW = CHUNK * SSM_H
N_DEV = 8

ADAM_LR = 0.001
ADAM_B1 = 0.9
ADAM_B2 = 0.999
ADAM_EPS = 1e-08
ADAM_WD = 0.01
ADAM_STEP = 10

VMEM_LIMIT = 56 * 1024 * 1024
MESH = pl.DeviceIdType.MESH


def _cp(sem=None):
    if sem is None:
        return pltpu.CompilerParams(vmem_limit_bytes=VMEM_LIMIT)
    return pltpu.CompilerParams(vmem_limit_bytes=VMEM_LIMIT, dimension_semantics=sem)


def _sigmoid(x):
    return 0.5 * jnp.tanh(0.5 * x) + 0.5


def _silu(x):
    return x * _sigmoid(x)


def _dsilu(x):
    s = _sigmoid(x)
    return s * (1.0 + x * (1.0 - s))


_GELU_C = math.sqrt(2.0 / math.pi)


def _gelu(y):
    t = jnp.tanh(_GELU_C * (y + 0.044715 * y * y * y))
    return 0.5 * y * (1.0 + t)


def _dgelu(y):
    t = jnp.tanh(_GELU_C * (y + 0.044715 * y * y * y))
    return 0.5 * (1.0 + t) + 0.5 * y * (1.0 - t * t) * _GELU_C * (1.0 + 3.0 * 0.044715 * y * y)


def _tile(n, want):
    if n <= want:
        return n
    for t in range(want - want % 16, 0, -16):
        if n % t == 0:
            return t
    raise ValueError((n, want))


def _mm(a, b, mode, out_dtype, name, tm=512, tn=1024, add=None, ride=None, panel=None):
    if mode == "nn":
        (M, K), (K2, N) = a.shape, b.shape
    elif mode == "nt":
        (M, K), (N, K2) = a.shape, b.shape
    else:
        (K, M), (K2, N) = a.shape, b.shape
    assert K == K2
    tm, tn = _tile(M, tm), _tile(N, tn)
    p0 = 0
    if panel is not None:
        assert mode != "nt" and add is None
        p0, N = panel, tn
    dn = {"nn": _NN, "nt": _NT, "tn": _TN}[mode]

    def body(a_ref, b_ref, *rest):
        o_ref = rest[-1]
        acc = lax.dot_general(a_ref[...].astype(BF16), b_ref[...].astype(BF16), dn, preferred_element_type=F32)
        if add is not None:
            acc = acc + rest[0][...]
        o_ref[...] = acc.astype(o_ref.dtype)

    a_spec = pl.BlockSpec((K, tm), lambda j, i: (0, i)) if mode == "tn" else pl.BlockSpec((tm, K), lambda j, i: (i, 0))
    b_spec = (pl.BlockSpec((tn, K), lambda j, i: (j, 0)) if mode == "nt"
              else pl.BlockSpec((K, tn), lambda j, i: (0, j + p0)))
    o_spec = pl.BlockSpec((tm, tn), lambda j, i: (i, j))
    extra = () if add is None else (add,)
    if ride is not None:
        (out,), landed = _call(body, name, (N // tn, M // tm), [a_spec, b_spec] + [o_spec] * len(extra), [o_spec],
                               [jax.ShapeDtypeStruct((M, N), out_dtype)], (a, b, *extra), ride=ride)
        return out, landed
    return pl.pallas_call(
        body,
        name=name,
        grid=(N // tn, M // tm),
        in_specs=[a_spec, b_spec] + [o_spec] * len(extra),
        out_specs=o_spec,
        out_shape=jax.ShapeDtypeStruct((M, N), out_dtype),
        compiler_params=_cp(("parallel", "parallel")),
    )(a, b, *extra)


_CHIP_ORDER = (0, 2, 1, 3)


def _rms_inproj_gather(x, norm_w, wt_shard, chip):
    L = x.shape[0]
    tm = _tile(L, 512)
    ni = L // tm
    r = IN_W // N_DEV
    tn = 2 * r

    def body(chip_ref, x_ref, nw_ref, shard, proj_ref, hn_ref, wt_hbm, hn_scr, w_scr, send_sems, recv_sems, loc_sems):
        jc, i = pl.program_id(0), pl.program_id(1)
        xx, yy, c = lax.axis_index("x"), lax.axis_index("y"), lax.axis_index("c")
        me, sibling = (xx, yy, c), (xx, yy, 1 - c)
        chips = [(1 - xx, yy), (xx, 1 - yy), (1 - xx, 1 - yy)]

        def slot(dev):
            return wt_hbm.at[4 * dev[0] + 2 * dev[1] + dev[2]]

        def copy(q, block, to, src=None):
            return pltpu.make_async_remote_copy(
                src_ref=slot(block) if src is None else src, dst_ref=slot(block),
                send_sem=send_sems.at[q], recv_sem=recv_sems.at[q], device_id=to, device_id_type=MESH)

        def rows_of(buf, core):
            return w_scr.at[buf, pl.ds(pl.multiple_of(core * r, 16), r)]

        mine = pltpu.make_async_copy(shard, slot(me), loc_sems.at[0])
        sends = [copy(0, me, sibling, src=shard)] + [copy(1 + j, me, (*ch, c), src=shard) for j, ch in enumerate(chips)]
        first = jnp.logical_and(jc == 0, i == 0)

        @pl.when(first)
        def _():
            mine.start()
            for cp in sends[:3]:
                cp.start()
            own = pltpu.make_async_copy(shard, rows_of(0, c), loc_sems.at[1])
            own.start()
            copy(0, sibling, me).wait_recv()
            sib = pltpu.make_async_copy(slot(sibling), rows_of(0, 1 - c), loc_sems.at[2])
            sib.start()
            own.wait()
            sib.wait()

        def take_direct(j, ch):
            copy(1 + j, (*ch, c), me).wait_recv()
            copy(4 + j, (*ch, c), sibling).start()
            if j == 0:
                sends[1].wait_send()
                sends[2].wait_send()
                sends[3].start()
            pltpu.make_async_copy(slot((*ch, c)), rows_of((1 + j) % 2, c), loc_sems.at[1]).start()

        for j, ch in enumerate(chips):
            early = jnp.logical_and(jc == j, i == ni // 2) if j > 0 else jnp.logical_and(jc == 1, i == 0)

            @pl.when(early)
            def _(j=j, ch=ch):
                take_direct(j, ch)

            @pl.when(jnp.logical_and(jc == 1 + j, i == 0))
            def _(j=j, ch=ch):
                buf = (1 + j) % 2
                copy(4 + j, (*ch, 1 - c), me).wait_recv()
                passed = pltpu.make_async_copy(slot((*ch, 1 - c)), rows_of(buf, 1 - c), loc_sems.at[2])
                passed.start()
                pltpu.make_async_copy(slot((*ch, c)), rows_of(buf, c), loc_sems.at[1]).wait()
                passed.wait()

        rows = pl.ds(pl.multiple_of(i * tm, tm), tm)

        @pl.when(jc == 0)
        def _():
            xv = x_ref[...]
            rstd = lax.rsqrt(jnp.mean(xv * xv, axis=-1, keepdims=True) + NORM_EPS)
            hn = (xv * rstd * nw_ref[...]).astype(BF16)
            hn_scr[rows, :] = hn
            hn_ref[...] = hn

        for buf in range(2):
            @pl.when(jc % 2 == buf)
            def _(buf=buf):
                proj_ref[...] = lax.dot_general(hn_scr[rows, :], w_scr[buf], _NT, preferred_element_type=F32)

        @pl.when(jnp.logical_and(jc == 3, i == ni - 1))
        def _():
            sends[0].wait_send()
            sends[3].wait_send()
            for j, ch in enumerate(chips):
                copy(4 + j, (*ch, c), sibling).wait_send()
            mine.wait()

    def tile_of(jc, chip_ref):
        mask = jnp.where(jc == 1, _CHIP_ORDER[1], jnp.where(jc == 2, _CHIP_ORDER[2], jnp.where(jc == 3, _CHIP_ORDER[3], 0)))
        return jnp.bitwise_xor(chip_ref[0], mask)

    held = lambda jc, i: jnp.where(jc == 0, i, ni - 1)
    return pl.pallas_call(
        body,
        name="rms_inproj_gather",
        grid_spec=pltpu.PrefetchScalarGridSpec(
            num_scalar_prefetch=1,
            grid=(4, ni),
            in_specs=[pl.BlockSpec((tm, D_MODEL), lambda jc, i, ch: (held(jc, i), 0)),
                      pl.BlockSpec((1, D_MODEL), lambda jc, i, ch: (0, 0)), _ANY],
            out_specs=[pl.BlockSpec((tm, tn), lambda jc, i, ch: (i, tile_of(jc, ch))),
                       pl.BlockSpec((tm, D_MODEL), lambda jc, i, ch: (held(jc, i), 0)), _ANY],
            scratch_shapes=[pltpu.VMEM((L, D_MODEL), BF16), pltpu.VMEM((2, tn, D_MODEL), BF16),
                            pltpu.SemaphoreType.DMA((7,)), pltpu.SemaphoreType.DMA((7,)), pltpu.SemaphoreType.DMA((3,))],
        ),
        out_shape=[jax.ShapeDtypeStruct((L, IN_W), F32), jax.ShapeDtypeStruct((L, D_MODEL), BF16),
                   jax.ShapeDtypeStruct((N_DEV, r, D_MODEL), BF16)],
        compiler_params=_cp(("arbitrary", "arbitrary")),
    )(chip, x, norm_w.reshape(1, D_MODEL), wt_shard)


def _seg_sum(v):
    a = lax.broadcasted_iota(jnp.int32, (128, 128), 0) // HEAD_DIM
    b = lax.broadcasted_iota(jnp.int32, (128, 128), 1) // HEAD_DIM
    ones = jnp.where(a == b, 1.0, 0.0).astype(BF16)
    hi = v.astype(BF16)
    lo = (v - hi.astype(F32)).astype(BF16)
    return jnp.dot(hi, ones, preferred_element_type=F32) + jnp.dot(lo, ones, preferred_element_type=F32)


def _rot_half(t):
    lane = lax.broadcasted_iota(jnp.int32, t.shape, 1)
    return jnp.where(lane % HEAD_DIM < HEAD_DIM // 2, pltpu.roll(t, 128 - HEAD_DIM // 2, 1),
                     pltpu.roll(t, HEAD_DIM // 2, 1))


def _norm_rope(raw, w, cos, sin):
    r = lax.rsqrt(_seg_sum(raw * raw) * (1.0 / HEAD_DIM) + NORM_EPS)
    tn = raw * r * w
    return r, tn * cos + _rot_half(tn) * sin


def _norm_rope_bwd(d_rot, raw, w, cos, sin):
    r = lax.rsqrt(_seg_sum(raw * raw) * (1.0 / HEAD_DIM) + NORM_EPS)
    d_tn = d_rot * cos + _rot_half(d_rot * sin)
    xh = raw * r
    gw = d_tn * w
    d_raw = r * (gw - xh * (_seg_sum(gw * xh) * (1.0 / HEAD_DIM)))
    return d_raw, d_tn * xh


def _band_mask2(has_prev):
    qi = lax.broadcasted_iota(jnp.int32, (2 * BLOCK, 2 * BLOCK), 0) % BLOCK + BLOCK
    kj = lax.broadcasted_iota(jnp.int32, (2 * BLOCK, 2 * BLOCK), 1)
    rel = qi - kj
    return (rel >= 0) & (rel < BLOCK) & ((kj >= BLOCK) | has_prev)


def _half_tiles(pair):
    lo = lax.broadcasted_iota(jnp.int32, pair.shape, 1) < HEAD_DIM
    sw = pltpu.roll(pair, HEAD_DIM, 1)
    z = jnp.zeros_like(pair)
    return (jnp.where(lo, pair, z).astype(BF16), jnp.where(lo, z, sw).astype(BF16),
            jnp.where(lo, sw, z).astype(BF16), jnp.where(lo, z, pair).astype(BF16))


def _two_rows(top, bottom):
    row = lax.broadcasted_iota(jnp.int32, (2 * BLOCK, 1), 0)
    return jnp.where(row < BLOCK, top, bottom)


def _lane_col(mat, h):
    lane = lax.broadcasted_iota(jnp.int32, mat.shape, 1)
    return jnp.sum(jnp.where(lane == h, mat, 0.0), axis=1, keepdims=True)


_SCALE = 1.0 / math.sqrt(HEAD_DIM)
_NT = (((1,), (1,)), ((), ()))
_NN = (((1,), (0,)), ((), ()))
_TN = (((0,), (0,)), ((), ()))


def _qk_prep(proj, tab, qw, kw):
    L = proj.shape[0]
    tm = _tile(L, 512)

    def body(q_ref, k_ref, t_ref, qw_ref, kw_ref, qo_ref, ko_ref):
        cos, sin = t_ref[:, :128], t_ref[:, 128:]
        for c in range(ATTN_W // 128):
            _, qr = _norm_rope(q_ref[:, c * 128:(c + 1) * 128], qw_ref[...], cos, sin)
            qo_ref[:, c * 128:(c + 1) * 128] = (qr * _SCALE).astype(BF16)
        for c in range(KV_W // 128):
            _, kr = _norm_rope(k_ref[:, c * 128:(c + 1) * 128], kw_ref[...], cos, sin)
            ko_ref[:, c * 128:(c + 1) * 128] = kr.astype(BF16)

    row = pl.BlockSpec((1, 128), lambda i: (0, 0))
    return pl.pallas_call(
        body,
        name="qk_prep",
        grid=(L // tm,),
        in_specs=[pl.BlockSpec((tm, ATTN_W), lambda i: (i, 0)), pl.BlockSpec((tm, KV_W), lambda i: (i, 4)),
                  pl.BlockSpec((tm, 256), lambda i: (i, 0)), row, row],
        out_specs=[pl.BlockSpec((tm, ATTN_W), lambda i: (i, 0)), pl.BlockSpec((tm, KV_W), lambda i: (i, 0))],
        out_shape=[jax.ShapeDtypeStruct((L, ATTN_W), BF16), jax.ShapeDtypeStruct((L, KV_W), BF16)],
        compiler_params=_cp(("parallel",)),
    )(proj, proj, tab, jnp.tile(qw, 2).reshape(1, 128), jnp.tile(kw, 2).reshape(1, 128))


def _group_tiles(g, kt, vt):
    a, b = divmod(g, 2)
    return kt[a][2 * b], kt[a][2 * b + 1], vt[a][2 * b], vt[a][2 * b + 1]


def _attn_fwd(q, k, proj, sinks, ride):
    L = proj.shape[0]
    nb = L // BLOCK

    def body(q_ref, kc_ref, kp_ref, vc_ref, vp_ref, z0_ref, z1_ref, sink_ref, og_ref, o_ref, lse_ref):
        i = pl.program_id(0)
        mask = _band_mask2(i > 0)
        z = jnp.concatenate([z0_ref[...], z1_ref[...]], axis=1)
        lane = lax.broadcasted_iota(jnp.int32, (BLOCK, 128), 1)
        kt = [_half_tiles(jnp.concatenate([kp_ref[:, a * 128:(a + 1) * 128], kc_ref[:, a * 128:(a + 1) * 128]],
                                          axis=0).astype(F32)) for a in range(2)]
        vt = [_half_tiles(jnp.concatenate([vp_ref[:, a * 128:(a + 1) * 128], vc_ref[:, a * 128:(a + 1) * 128]],
                                          axis=0)) for a in range(2)]
        lse_mat = jnp.zeros((BLOCK, 128), F32)
        outs = []
        for g in range(N_KV):
            k_lo, k_hi, v_lo, v_hi = _group_tiles(g, kt, vt)
            q2 = jnp.concatenate([q_ref[:, 2 * g * 128:(2 * g + 1) * 128],
                                  q_ref[:, (2 * g + 1) * 128:(2 * g + 2) * 128]], axis=0)
            acc = jnp.zeros((2 * BLOCK, 128), F32)
            for half, (kh, vh) in enumerate(((k_lo, v_lo), (k_hi, v_hi))):
                h_top, h_bot = 4 * g + half, 4 * g + 2 + half
                s = jnp.where(mask, lax.dot_general(q2, kh, _NT, preferred_element_type=F32), -1e30)
                sink = _two_rows(sink_ref[h_top], sink_ref[h_bot])
                m = jnp.maximum(jnp.max(s, axis=-1, keepdims=True), sink)
                e = jnp.exp(s - m)
                den = jnp.sum(e, axis=-1, keepdims=True) + jnp.exp(sink - m)
                p = e * (1.0 / den)
                acc = acc + jnp.dot(p.astype(BF16), vh, preferred_element_type=F32)
                lse = m + jnp.log(den)
                lse_mat = jnp.where(lane == h_top, lse[:BLOCK], lse_mat)
                lse_mat = jnp.where(lane == h_bot, lse[BLOCK:], lse_mat)
            outs += [acc[:BLOCK], acc[BLOCK:]]
        o = jnp.concatenate(outs, axis=1)
        o_ref[...] = o
        og_ref[...] = o * _silu(z)
        lse_ref[...] = lse_mat

    prev = lambda i: jnp.maximum(i - 1, 0)
    return _call(
        body, "attn_fwd", (nb,),
        [pl.BlockSpec((BLOCK, ATTN_W), lambda i: (i, 0)),
         pl.BlockSpec((BLOCK, KV_W), lambda i: (i, 0)),
         pl.BlockSpec((BLOCK, KV_W), lambda i: (prev(i), 0)),
         pl.BlockSpec((BLOCK, KV_W), lambda i: (i, 5)),
         pl.BlockSpec((BLOCK, KV_W), lambda i: (prev(i), 5)),
         pl.BlockSpec((BLOCK, 512), lambda i: (i, 3)),
         pl.BlockSpec((BLOCK, 512), lambda i: (i, 4)),
         pl.BlockSpec(memory_space=pltpu.SMEM)],
        [pl.BlockSpec((BLOCK, ATTN_W), lambda i: (i, 0)),
         pl.BlockSpec((BLOCK, ATTN_W), lambda i: (i, 0)),
         pl.BlockSpec((BLOCK, 128), lambda i: (i, 0))],
        [jax.ShapeDtypeStruct((L, ATTN_W), F32), jax.ShapeDtypeStruct((L, ATTN_W), F32),
         jax.ShapeDtypeStruct((L, 128), F32)],
        (q, k, k, proj, proj, proj, proj, sinks), ride=ride)


def _attn_bwd(q, k, proj, sinks, d_o, o, lse, ride):
    L = proj.shape[0]
    nb = L // BLOCK

    def body(q_ref, kc_ref, kp_ref, vc_ref, vp_ref, do_ref, o_ref, lse_ref, sink_ref,
             dq_ref, dk_ref, dv_ref, gs_ref, ck_scr, cv_scr):
        i = pl.program_id(0)

        @pl.when(i == 0)
        def _():
            gs_ref[...] = jnp.zeros_like(gs_ref)
            ck_scr[...] = jnp.zeros_like(ck_scr)
            cv_scr[...] = jnp.zeros_like(cv_scr)

        @pl.when(i == nb)
        def _():
            dk_ref[...] = ck_scr[...]
            dv_ref[...] = cv_scr[...]

        @pl.when(i < nb)
        def _():
            mask = _band_mask2(i > 0)
            lane = lax.broadcasted_iota(jnp.int32, (1, 128), 1)
            lo = lax.broadcasted_iota(jnp.int32, (2 * BLOCK, 128), 1) < HEAD_DIM
            lse_c = lse_ref[...]
            kt = [_half_tiles(jnp.concatenate([kp_ref[:, a * 128:(a + 1) * 128], kc_ref[:, a * 128:(a + 1) * 128]],
                                              axis=0).astype(F32)) for a in range(2)]
            vt = [_half_tiles(jnp.concatenate([vp_ref[:, a * 128:(a + 1) * 128], vc_ref[:, a * 128:(a + 1) * 128]],
                                              axis=0)) for a in range(2)]
            gs = jnp.zeros((1, 128), F32)
            dq_parts = []
            dk_acc = [jnp.zeros((2 * BLOCK, 128), F32) for _ in range(2)]
            dv_acc = [jnp.zeros((2 * BLOCK, 128), F32) for _ in range(2)]
            for g in range(N_KV):
                a, b = divmod(g, 2)
                k_lo, k_hi, v_lo, v_hi = _group_tiles(g, kt, vt)
                t0, t1 = slice(2 * g * 128, (2 * g + 1) * 128), slice((2 * g + 1) * 128, (2 * g + 2) * 128)
                q2 = jnp.concatenate([q_ref[:, t0], q_ref[:, t1]], axis=0)
                do2 = jnp.concatenate([do_ref[:, t0], do_ref[:, t1]], axis=0)
                prod = do2 * jnp.concatenate([o_ref[:, t0], o_ref[:, t1]], axis=0)
                do2_b = do2.astype(BF16)
                dq2 = jnp.zeros((2 * BLOCK, 128), F32)
                dk_h, dv_h = [], []
                for half, (kh, vh) in enumerate(((k_lo, v_lo), (k_hi, v_hi))):
                    h_top, h_bot = 4 * g + half, 4 * g + 2 + half
                    lse = jnp.concatenate([_lane_col(lse_c, h_top), _lane_col(lse_c, h_bot)], axis=0)
                    sink = _two_rows(sink_ref[h_top], sink_ref[h_bot])
                    delta = jnp.sum(jnp.where(lo == (half == 0), prod, 0.0), axis=1, keepdims=True)
                    s = jnp.where(mask, lax.dot_general(q2, kh, _NT, preferred_element_type=F32), -1e30)
                    p = jnp.exp(s - lse)
                    dp = lax.dot_general(do2_b, vh, _NT, preferred_element_type=F32)
                    ds_b = (p * (dp - delta)).astype(BF16)
                    p_b = p.astype(BF16)
                    dq2 = dq2 + jnp.dot(ds_b, kh, preferred_element_type=F32)
                    dk_h.append(lax.dot_general(ds_b, q2, _TN, preferred_element_type=F32))
                    dv_h.append(lax.dot_general(p_b, do2_b, _TN, preferred_element_type=F32))
                    gsink = -jnp.exp(sink - lse) * delta
                    row = lax.broadcasted_iota(jnp.int32, (2 * BLOCK, 1), 0)
                    gs = gs + jnp.where(lane == h_top, jnp.sum(jnp.where(row < BLOCK, gsink, 0.0)), 0.0)
                    gs = gs + jnp.where(lane == h_bot, jnp.sum(jnp.where(row >= BLOCK, gsink, 0.0)), 0.0)
                dq_parts += [dq2[:BLOCK], dq2[BLOCK:]]
                for acc, parts in ((dk_acc, dk_h), (dv_acc, dv_h)):
                    t = jnp.where(lo, parts[0], parts[1])
                    t = t + pltpu.roll(t, HEAD_DIM, 1)
                    acc[a] = acc[a] + jnp.where(lo == (b == 0), t, 0.0)
            dq_ref[...] = jnp.concatenate(dq_parts, axis=1)
            dk_full = jnp.concatenate(dk_acc, axis=1)
            dv_full = jnp.concatenate(dv_acc, axis=1)
            dk_ref[...] = ck_scr[...] + dk_full[:BLOCK]
            dv_ref[...] = cv_scr[...] + dv_full[:BLOCK]
            ck_scr[...] = dk_full[BLOCK:]
            cv_scr[...] = dv_full[BLOCK:]
            gs_ref[...] += gs

    cur = lambda i: jnp.minimum(i, nb - 1)
    prev = lambda i: jnp.maximum(jnp.minimum(i, nb - 1) - 1, 0)
    done = lambda i: jnp.maximum(i - 1, 0)
    bs = pl.BlockSpec
    return _call(
        body, "attn_bwd", (nb + 1,),
        [bs((BLOCK, ATTN_W), lambda i: (cur(i), 0)),
         bs((BLOCK, KV_W), lambda i: (cur(i), 0)), bs((BLOCK, KV_W), lambda i: (prev(i), 0)),
         bs((BLOCK, KV_W), lambda i: (cur(i), 5)), bs((BLOCK, KV_W), lambda i: (prev(i), 5)),
         bs((BLOCK, ATTN_W), lambda i: (cur(i), 0)), bs((BLOCK, ATTN_W), lambda i: (cur(i), 0)),
         bs((BLOCK, 128), lambda i: (cur(i), 0)), bs(memory_space=pltpu.SMEM)],
        [bs((BLOCK, ATTN_W), lambda i: (cur(i), 0)),
         bs((BLOCK, KV_W), lambda i: (done(i), 0)), bs((BLOCK, KV_W), lambda i: (done(i), 0)),
         bs((1, 128), lambda i: (0, 0))],
        [jax.ShapeDtypeStruct((L, ATTN_W), F32), jax.ShapeDtypeStruct((L, KV_W), F32),
         jax.ShapeDtypeStruct((L, KV_W), F32), jax.ShapeDtypeStruct((1, 128), F32)],
        (q, k, k, proj, proj, d_o, o, lse, sinks),
        [pltpu.VMEM((BLOCK, KV_W), F32), pltpu.VMEM((BLOCK, KV_W), F32)], ride)


def _qk_prep_bwd(proj, tab, qw, kw, d_q, d_k, d_v, d_za, d_u, d_zs):
    L = proj.shape[0]
    tm = _tile(L, 512)
    z0 = ATTN_W + 2 * KV_W

    def body(q_ref, k_ref, t_ref, qw_ref, kw_ref, dq_ref, dk_ref, dv_ref, dza_ref, du_ref, dzs_ref,
             out_ref, gq_ref, gk_ref):
        i = pl.program_id(0)

        @pl.when(i == 0)
        def _():
            gq_ref[...] = jnp.zeros_like(gq_ref)
            gk_ref[...] = jnp.zeros_like(gk_ref)

        cos, sin = t_ref[:, :128], t_ref[:, 128:]
        gq = jnp.zeros((1, 128), F32)
        gk = jnp.zeros((1, 128), F32)
        for c in range(ATTN_W // 128):
            cs = slice(c * 128, (c + 1) * 128)
            d_raw, gw = _norm_rope_bwd(dq_ref[:, cs] * _SCALE, q_ref[:, cs], qw_ref[...], cos, sin)
            out_ref[:, cs] = d_raw.astype(BF16)
            gq = gq + jnp.sum(gw, axis=0, keepdims=True)
        for c in range(KV_W // 128):
            cs = slice(c * 128, (c + 1) * 128)
            d_raw, gw = _norm_rope_bwd(dk_ref[:, cs], k_ref[:, cs], kw_ref[...], cos, sin)
            out_ref[:, ATTN_W + c * 128:ATTN_W + (c + 1) * 128] = d_raw.astype(BF16)
            gk = gk + jnp.sum(gw, axis=0, keepdims=True)
        out_ref[:, ATTN_W + KV_W:z0] = dv_ref[...].astype(BF16)
        out_ref[:, z0:z0 + ATTN_W] = dza_ref[...]
        out_ref[:, z0 + ATTN_W:z0 + ATTN_W + SSM_W] = du_ref[...].astype(BF16)
        out_ref[:, z0 + ATTN_W + SSM_W:] = dzs_ref[...]
        gq_ref[...] += gq
        gk_ref[...] += gk

    row = pl.BlockSpec((1, 128), lambda i: (0, 0))
    blk = lambda w, c: pl.BlockSpec((tm, w), lambda i: (i, c))
    return pl.pallas_call(
        body,
        name="qk_prep_bwd",
        grid=(L // tm,),
        in_specs=[blk(ATTN_W, 0), blk(KV_W, 4), blk(256, 0), row, row, blk(ATTN_W, 0), blk(KV_W, 0), blk(KV_W, 0),
                  blk(ATTN_W, 0), blk(SSM_W, 0), blk(SSM_W, 0)],
        out_specs=[blk(IN_W, 0), row, row],
        out_shape=[jax.ShapeDtypeStruct((L, IN_W), BF16), jax.ShapeDtypeStruct((1, 128), F32),
                   jax.ShapeDtypeStruct((1, 128), F32)],
        compiler_params=_cp(("arbitrary",)),
    )(proj, proj, tab, jnp.tile(qw, 2).reshape(1, 128), jnp.tile(kw, 2).reshape(1, 128), d_q, d_k, d_v,
      d_za, d_u, d_zs)


def _cmul(a, b):
    return a[0] * b[0] - a[1] * b[1], a[0] * b[1] + a[1] * b[0]


def _cmul_conj(a, b):
    return a[0] * b[0] + a[1] * b[1], a[1] * b[0] - a[0] * b[1]


def _cadd(a, b):
    return a[0] + b[0], a[1] + b[1]


def _dot3(a, b, dn):
    ah, bh = a.astype(BF16), b.astype(BF16)
    al, bl = (a - ah.astype(F32)).astype(BF16), (b - bh.astype(F32)).astype(BF16)
    d = lambda u, v: lax.dot_general(u, v, dn, preferred_element_type=F32)
    return d(ah, bh) + d(ah, bl) + d(al, bh)


def _s5_discretise(a_re, a_im, ls, cosx, sinx, bt):
    delta = jnp.exp(ls)
    er = jnp.exp(a_re * delta)
    lb = (er * cosx, er * sinx)
    den = a_re * a_re + a_im * a_im
    coef = _cmul_conj((lb[0] - 1.0, lb[1]), (a_re, a_im))
    coef = (coef[0] / den, coef[1] / den)
    return delta, lb, coef, den, _cmul(coef, bt)


def _powers(lb):
    pw = [(jnp.ones_like(lb[0]), jnp.zeros_like(lb[0]))]
    for _ in range(CHUNK):
        pw.append(_cmul(pw[-1], lb))
    return pw


def _block_rows(a, pw, idx):
    blocks = [_cmul(a, pw[i]) for i in idx]
    return (jnp.concatenate([b[0] for b in blocks], axis=-2), jnp.concatenate([b[1] for b in blocks], axis=-2))


def _block_rows_bwd(g, a, pw, idx, g_pw):
    g_a = (jnp.zeros_like(a[0]), jnp.zeros_like(a[0]))
    for j, i in enumerate(idx):
        gj = (g[0][..., j * SSM_H:(j + 1) * SSM_H, :], g[1][..., j * SSM_H:(j + 1) * SSM_H, :])
        g_a = _cadd(g_a, _cmul_conj(gj, pw[i]))
        gp = _cmul_conj(gj, a)
        g_pw[i] = _cadd(g_pw[i], (jnp.sum(gp[0], axis=-2, keepdims=True), jnp.sum(gp[1], axis=-2, keepdims=True)))
    return g_a


_IDX_S = [CHUNK - 1 - s for s in range(CHUNK)]
_IDX_O = [t + 1 for t in range(CHUNK)]
_IDX_K = list(range(CHUNK))


def _prep_args(p):
    row = lambda t: t.reshape(SSM_G, 1, SSM_P)
    xi = p["a_im"] * jnp.exp(p["log_step"])[:, None]
    return (row(p["a_re"]), row(p["a_im"]), row(jnp.broadcast_to(p["log_step"][:, None], (SSM_G, SSM_P))),
            row(jnp.cos(xi)), row(jnp.sin(xi)), p["b_re"].transpose(0, 2, 1), p["b_im"].transpose(0, 2, 1),
            p["c_re"], p["c_im"])


PREP_GROUPS = 8


def _prep_specs():
    r1 = pl.BlockSpec((PREP_GROUPS, 1, SSM_P), lambda g: (g, 0, 0))
    r16 = pl.BlockSpec((PREP_GROUPS, SSM_H, SSM_P), lambda g: (g, 0, 0))
    return [r1] * 5 + [r16] * 4, r1, r16


def _ssm_prep(p):
    def one_group(q, are, aim, ls, cosx, sinx, btr, bti, cre, cim, mt_ref, s_ref, o_ref, a_ref):
        _, lb, _, _, bb = _s5_discretise(are[q], aim[q], ls[q], cosx[q], sinx[q], (btr[q], bti[q]))
        pw = _powers(lb)
        c = (cre[q], cim[q])
        sc = _block_rows(bb, pw, _IDX_S)
        ot = _block_rows(c, pw, _IDX_O)
        ok = _block_rows(c, pw, _IDX_K)
        s_ref[q] = jnp.concatenate([sc[0], sc[1]], axis=1).astype(BF16)
        o_ref[q] = jnp.concatenate([ot[0], -ot[1]], axis=1).astype(BF16)
        a_ref[q] = jnp.concatenate([pw[CHUNK][0], pw[CHUNK][1]], axis=1)
        kt = _dot3(jnp.concatenate([bb[0], -bb[1]], axis=1), jnp.concatenate([ok[0], ok[1]], axis=1), _NT)
        lane = lax.broadcasted_iota(jnp.int32, kt.shape, 1)
        for s in range(CHUNK):
            blk = kt if s == 0 else jnp.where(lane >= SSM_H * s, pltpu.roll(kt, SSM_H * s, 1), 0.0)
            mt_ref[q, s * SSM_H:(s + 1) * SSM_H, :] = blk.astype(BF16)

    def body(*refs):
        for q in range(PREP_GROUPS):
            one_group(q, *refs)

    in_specs, r1, _ = _prep_specs()
    g3 = lambda r, c: pl.BlockSpec((PREP_GROUPS, r, c), lambda g: (g, 0, 0))
    return pl.pallas_call(
        body,
        name="ssm_prep",
        grid=(SSM_G // PREP_GROUPS,),
        in_specs=in_specs,
        out_specs=[g3(CW, CW), g3(CW, 2 * SSM_P), g3(CW, 2 * SSM_P), g3(1, 2 * SSM_P)],
        out_shape=[jax.ShapeDtypeStruct((SSM_G, CW, CW), BF16), jax.ShapeDtypeStruct((SSM_G, CW, 2 * SSM_P), BF16),
                   jax.ShapeDtypeStruct((SSM_G, CW, 2 * SSM_P), BF16),
                   jax.ShapeDtypeStruct((SSM_G, 1, 2 * SSM_P), F32)],
        compiler_params=_cp(("parallel",)),
    )(*_prep_args(p))


def _ssm_prep_bwd(p, g_mt, g_scat, g_ocat, g_a16, ride):
    def body(are, aim, ls, cosx, sinx, btr, bti, cre, cim, gmt_ref, gs_ref, go_ref, ga_ref,
             g_are, g_aim, g_ls, g_btr, g_bti, g_cre, g_cim, ga1_scr, gb1_scr):
        lam = (are[...], aim[...])
        bt = (btr[...], bti[...])
        delta, lb, coef, den, bb = _s5_discretise(lam[0], lam[1], ls[...], cosx[...], sinx[...], bt)
        pw = _powers(lb)
        c = (cre[...], cim[...])
        ok = _block_rows(c, pw, _IDX_K)
        g_pw = [(jnp.zeros_like(lb[0]), jnp.zeros_like(lb[0])) for _ in range(CHUNK + 1)]
        lane = lax.broadcasted_iota(jnp.int32, (SSM_H, CW), 1)
        for q in range(PREP_GROUPS):
            g_kt = gmt_ref[q, :SSM_H, :]
            for s in range(1, CHUNK):
                blk = gmt_ref[q, s * SSM_H:(s + 1) * SSM_H, :]
                g_kt = g_kt + jnp.where(lane < CW - SSM_H * s, pltpu.roll(blk, CW - SSM_H * s, 1), 0.0)
            a1 = jnp.concatenate([bb[0][q], -bb[1][q]], axis=1)
            b1 = jnp.concatenate([ok[0][q], ok[1][q]], axis=1)
            ga1_scr[q] = _dot3(g_kt, b1, _NN)
            gb1_scr[q] = _dot3(g_kt, a1, _TN)
        g_a1, g_b1 = ga1_scr[...], gb1_scr[...]
        g_bb = (g_a1[..., :SSM_P], -g_a1[..., SSM_P:])
        g_c = _block_rows_bwd((g_b1[..., :SSM_P], g_b1[..., SSM_P:]), c, pw, _IDX_K, g_pw)
        gs = gs_ref[...]
        g_bb = _cadd(g_bb, _block_rows_bwd((gs[..., :SSM_P], gs[..., SSM_P:]), bb, pw, _IDX_S, g_pw))
        go = go_ref[...]
        g_c = _cadd(g_c, _block_rows_bwd((go[..., :SSM_P], -go[..., SSM_P:]), c, pw, _IDX_O, g_pw))
        ga = ga_ref[...]
        g_pw[CHUNK] = _cadd(g_pw[CHUNK], (ga[..., :SSM_P], ga[..., SSM_P:]))
        g_lb = (jnp.zeros_like(lb[0]), jnp.zeros_like(lb[0]))
        for l in range(CHUNK - 1, -1, -1):
            g_lb = _cadd(g_lb, _cmul_conj(g_pw[l + 1], pw[l]))
            g_pw[l] = _cadd(g_pw[l], _cmul_conj(g_pw[l + 1], lb))
        g_bt = _cmul_conj(g_bb, coef)
        gc = _cmul_conj(g_bb, bt)
        g_coef = (jnp.sum(gc[0], axis=-2, keepdims=True), jnp.sum(gc[1], axis=-2, keepdims=True))
        lam_den = (lam[0] / den, lam[1] / den)
        g_lb = _cadd(g_lb, _cmul(g_coef, lam_den))
        t = _cmul(_cmul_conj(g_coef, coef), lam_den)
        g_x = _cmul_conj(g_lb, lb)
        g_are[...] = g_x[0] * delta - t[0]
        g_aim[...] = g_x[1] * delta - t[1]
        g_ls[...] = (g_x[0] * lam[0] + g_x[1] * lam[1]) * delta
        g_btr[...] = g_bt[0]
        g_bti[...] = g_bt[1]
        g_cre[...] = g_c[0]
        g_cim[...] = g_c[1]

    in_specs, r1, r16 = _prep_specs()
    g3 = lambda r, c: pl.BlockSpec((PREP_GROUPS, r, c), lambda g: (g, 0, 0))
    rows = jax.ShapeDtypeStruct((SSM_G, 1, SSM_P), F32)
    mats = jax.ShapeDtypeStruct((SSM_G, SSM_H, SSM_P), F32)
    (g_are, g_aim, g_ls, g_btr, g_bti, g_cre, g_cim), landed = _call(
        body, "ssm_prep_bwd", (SSM_G // PREP_GROUPS,),
        in_specs + [g3(CW, CW), g3(CW, 2 * SSM_P), g3(CW, 2 * SSM_P), g3(1, 2 * SSM_P)],
        [r1] * 3 + [r16] * 4, [rows] * 3 + [mats] * 4, (*_prep_args(p), g_mt, g_scat, g_ocat, g_a16),
        [pltpu.VMEM((PREP_GROUPS, SSM_H, 2 * SSM_P), F32), pltpu.VMEM((PREP_GROUPS, CW, 2 * SSM_P), F32)], ride)
    grads = dict(a_re=g_are.reshape(SSM_G, SSM_P), a_im=g_aim.reshape(SSM_G, SSM_P),
                 log_step=jnp.sum(g_ls.reshape(SSM_G, SSM_P), axis=1),
                 b_re=g_btr.transpose(0, 2, 1), b_im=g_bti.transpose(0, 2, 1), c_re=g_cre, c_im=g_cim)
    return grads, landed


def _cmul_const(xv, ar, ai):
    return xv * ar + pltpu.roll(xv, SSM_P, 1) * ai


def _chunk_scan(inc, a_row, reverse):
    n = inc.shape[0]
    lane = lax.broadcasted_iota(jnp.int32, (1, 2 * SSM_P), 1)
    row = lax.broadcasted_iota(jnp.int32, inc.shape, 0)
    sign = jnp.where(lane < SSM_P, -1.0, 1.0)
    ar = jnp.where(lane < SSM_P, a_row, pltpu.roll(a_row, SSM_P, 1))
    ai = jnp.where(lane < SSM_P, pltpu.roll(a_row, SSM_P, 1), a_row)
    if reverse:
        ai = -ai
    xv = inc
    s = 1
    while s < n:
        if reverse:
            sh = jnp.where(row < n - s, pltpu.roll(xv, n - s, 0), 0.0)
        else:
            sh = jnp.where(row >= s, pltpu.roll(xv, s, 0), 0.0)
        xv = xv + _cmul_const(sh, ar, ai * sign)
        ar, ai = ar * ar - ai * ai, 2.0 * ar * ai
        s *= 2
    return xv


def _shift_rows(xv, reverse):
    n = xv.shape[0]
    row = lax.broadcasted_iota(jnp.int32, xv.shape, 0)
    if reverse:
        return jnp.where(row < n - 1, pltpu.roll(xv, n - 1, 0), 0.0)
    return jnp.where(row >= 1, pltpu.roll(xv, 1, 0), 0.0)


GB = 128 // SSM_H
U_COL0 = (ATTN_W + 2 * KV_W + ATTN_W) // 128


HALF = CHUNK // 2


def _chunk_perm():
    r = jnp.arange(HALF * 128)
    t, g8, h = r // 128, (r % 128) // SSM_H, r % SSM_H
    return ((g8 * 128 + t * SSM_H + h)[:, None] == jnp.arange(GB * 128)[None, :]).astype(BF16)


def _load_perm(p_hbm, p_scr, sem):
    @pl.when(pl.program_id(0) == 0)
    def _():
        cp = pltpu.make_async_copy(p_hbm, p_scr, sem)
        cp.start()
        cp.wait()


def _rows_to_chunks(pieces, perm):
    halves = [jnp.dot(jnp.concatenate(pieces[k * HALF:(k + 1) * HALF], axis=1).astype(BF16), perm,
                      preferred_element_type=F32).astype(BF16) for k in range(2)]
    return [jnp.concatenate([hv[:, g * 128:(g + 1) * 128] for hv in halves], axis=1) for g in range(GB)]


def _chunks_to_rows(groups, perm, two_pass):
    pieces = []
    for k in range(2):
        v = jnp.concatenate([gv[:, k * 128:(k + 1) * 128] for gv in groups], axis=1)
        hi = v.astype(BF16)
        out = lax.dot_general(hi, perm, _NT, preferred_element_type=F32)
        if two_pass:
            lo = (v - hi.astype(F32)).astype(BF16)
            out = out + lax.dot_general(lo, perm, _NT, preferred_element_type=F32)
        pieces += [out[:, t * 128:(t + 1) * 128] for t in range(HALF)]
    return pieces


def _ssm_fwd(proj, perm, mt, scat, ocat, a16, d_skip, ride):
    L = proj.shape[0]
    nc = L // CHUNK

    def body(u_ref, p_hbm, mt_ref, s_ref, o_ref, a_ref, d_ref, y_ref, yg_ref, h_ref, p_scr, sem):
        _load_perm(p_hbm, p_scr, sem)
        perm = p_scr[...]
        rows = [pl.ds(t, nc, stride=CHUNK) for t in range(CHUNK)]
        us = [u_ref[r, :] for r in rows]
        ua = _rows_to_chunks(us, perm)
        ys = []
        for g in range(GB):
            uv = ua[g]
            inc = jnp.dot(uv, s_ref[g], preferred_element_type=F32)
            hx = _shift_rows(_chunk_scan(inc, a_ref[g], False), False)
            h_ref[g] = hx
            ys.append(jnp.dot(uv, mt_ref[g], preferred_element_type=F32)
                      + lax.dot_general(hx.astype(BF16), o_ref[g], _NT, preferred_element_type=F32))
        yp = _chunks_to_rows(ys, perm, True)
        for t, r in enumerate(rows):
            y = yp[t] + d_ref[...] * us[t]
            y_ref[r, :] = y
            yg_ref[r, :] = _gelu(y)

    g3 = lambda r, c: pl.BlockSpec((GB, r, c), lambda g: (g, 0, 0))
    col = pl.BlockSpec((L, 128), lambda g: (0, g))
    return _call(
        body, "ssm_fwd", (SSM_G // GB,),
        [pl.BlockSpec((L, 128), lambda g: (0, U_COL0 + g)), _ANY,
         g3(CW, CW), g3(CW, 2 * SSM_P), g3(CW, 2 * SSM_P), g3(1, 2 * SSM_P),
         pl.BlockSpec((1, 128), lambda g: (0, g))],
        [col, col, g3(nc, 2 * SSM_P)],
        [jax.ShapeDtypeStruct((L, SSM_W), F32), jax.ShapeDtypeStruct((L, SSM_W), F32),
         jax.ShapeDtypeStruct((SSM_G, nc, 2 * SSM_P), F32)],
        (proj, perm, mt, scat, ocat, a16, d_skip.reshape(1, SSM_W)),
        [pltpu.VMEM((HALF * 128, GB * 128), BF16), pltpu.SemaphoreType.DMA], ride)


def _ssm_bwd(d_yg, y, proj, hx, perm, mt, scat, ocat, a16, d_skip, ride):
    L = proj.shape[0]
    nc = L // CHUNK

    def body(dg_ref, y_ref, u_ref, h_ref, p_hbm, mt_ref, s_ref, o_ref, a_ref, d_ref,
             du_ref, gmt_ref, gs_ref, go_ref, ga_ref, gd_ref, p_scr, sem):
        _load_perm(p_hbm, p_scr, sem)
        perm = p_scr[...]
        rows = [pl.ds(t, nc, stride=CHUNK) for t in range(CHUNK)]
        us = [u_ref[r, :] for r in rows]
        dys = [dg_ref[r, :] * _dgelu(y_ref[r, :]) for r in rows]
        gd = jnp.zeros((1, 128), F32)
        for uv, dy in zip(us, dys):
            gd = gd + jnp.sum(dy * uv, axis=0, keepdims=True)
        gd_ref[...] = gd
        ua = _rows_to_chunks(us, perm)
        dya = _rows_to_chunks(dys, perm)
        lane = lax.broadcasted_iota(jnp.int32, (1, 2 * SSM_P), 1)
        dus = []
        for g in range(GB):
            uv, dy, hx_v = ua[g], dya[g], h_ref[g]
            dh = jnp.dot(dy, o_ref[g], preferred_element_type=F32)
            dinc = _shift_rows(_chunk_scan(dh, a_ref[g], True), True)
            dinc_b = dinc.astype(BF16)
            dus.append(lax.dot_general(dy, mt_ref[g], _NT, preferred_element_type=F32)
                       + lax.dot_general(dinc_b, s_ref[g], _NT, preferred_element_type=F32))
            gmt_ref[g] = lax.dot_general(uv, dy, _TN, preferred_element_type=F32)
            gs_ref[g] = lax.dot_general(uv, dinc_b, _TN, preferred_element_type=F32)
            go_ref[g] = lax.dot_general(dy, hx_v.astype(BF16), _TN, preferred_element_type=F32)
            p1 = dinc * hx_v
            p2 = pltpu.roll(dinc, SSM_P, 1) * hx_v
            t1 = jnp.sum(p1 + pltpu.roll(p1, SSM_P, 1), axis=0, keepdims=True)
            t2 = jnp.sum(p2 - pltpu.roll(p2, SSM_P, 1), axis=0, keepdims=True)
            ga_ref[g] = jnp.where(lane < SSM_P, t1, pltpu.roll(t2, SSM_P, 1))
        dup = _chunks_to_rows(dus, perm, False)
        for t, r in enumerate(rows):
            du_ref[r, :] = dup[t] + d_ref[...] * dys[t]

    g3 = lambda r, c: pl.BlockSpec((GB, r, c), lambda g: (g, 0, 0))
    col = pl.BlockSpec((L, 128), lambda g: (0, g))
    row = pl.BlockSpec((1, 128), lambda g: (0, g))
    return _call(
        body, "ssm_bwd", (SSM_G // GB,),
        [col, col, pl.BlockSpec((L, 128), lambda g: (0, U_COL0 + g)), g3(nc, 2 * SSM_P), _ANY,
         g3(CW, CW), g3(CW, 2 * SSM_P), g3(CW, 2 * SSM_P), g3(1, 2 * SSM_P), row],
        [col, g3(CW, CW), g3(CW, 2 * SSM_P), g3(CW, 2 * SSM_P), g3(1, 2 * SSM_P), row],
        [jax.ShapeDtypeStruct((L, SSM_W), F32), jax.ShapeDtypeStruct((SSM_G, CW, CW), F32),
         jax.ShapeDtypeStruct((SSM_G, CW, 2 * SSM_P), F32), jax.ShapeDtypeStruct((SSM_G, CW, 2 * SSM_P), F32),
         jax.ShapeDtypeStruct((SSM_G, 1, 2 * SSM_P), F32), jax.ShapeDtypeStruct((1, SSM_W), F32)],
        (d_yg, y, proj, hx, perm, mt, scat, ocat, a16, d_skip.reshape(1, SSM_W)),
        [pltpu.VMEM((HALF * 128, GB * 128), BF16), pltpu.SemaphoreType.DMA], ride)


def _merge(og, yg, w_glu, proj, b_glu, wa, ws):
    L = og.shape[0]
    tm = _tile(L, 256)

    def body(og_ref, yg_ref, wg_ref, z0_ref, z1_ref, b_ref, wa_ref, ws_ref, m_ref, gp_ref):
        zs = jnp.concatenate([z0_ref[...], z1_ref[...]], axis=1)
        ygv = yg_ref[...]
        gpre = jnp.dot(ygv.astype(BF16), wg_ref[...], preferred_element_type=F32)
        gp_ref[...] = gpre
        os_ = ygv * _sigmoid(gpre + b_ref[...]) * _silu(zs)
        ogv = og_ref[...]
        ra = lax.rsqrt(jnp.mean(ogv * ogv, axis=-1, keepdims=True) + NORM_EPS)
        rs = lax.rsqrt(jnp.mean(os_ * os_, axis=-1, keepdims=True) + NORM_EPS)
        m_ref[:, :ATTN_W] = (ogv * ra * wa_ref[...]).astype(BF16)
        m_ref[:, ATTN_W:] = (os_ * rs * ws_ref[...]).astype(BF16)

    row = lambda w: pl.BlockSpec((1, w), lambda i: (0, 0))
    return pl.pallas_call(
        body,
        name="merge",
        grid=(L // tm,),
        in_specs=[pl.BlockSpec((tm, ATTN_W), lambda i: (i, 0)), pl.BlockSpec((tm, SSM_W), lambda i: (i, 0)),
                  pl.BlockSpec((SSM_W, SSM_W), lambda i: (0, 0)),
                  pl.BlockSpec((tm, 512), lambda i: (i, 7)), pl.BlockSpec((tm, 512), lambda i: (i, 8)),
                  row(SSM_W), row(ATTN_W), row(SSM_W)],
        out_specs=[pl.BlockSpec((tm, D_MODEL), lambda i: (i, 0)), pl.BlockSpec((tm, SSM_W), lambda i: (i, 0))],
        out_shape=[jax.ShapeDtypeStruct((L, D_MODEL), BF16), jax.ShapeDtypeStruct((L, SSM_W), F32)],
        compiler_params=_cp(("parallel",)),
    )(og, yg, w_glu, proj, proj, b_glu.reshape(1, SSM_W), wa.reshape(1, ATTN_W), ws.reshape(1, SSM_W))


def _outproj_loss(merged, w_out, x, target):
    L = x.shape[0]
    tm, tn = _tile(L, 512), 1024
    ni, nj = L // tm, D_MODEL // tn

    def body(m_ref, w_ref, x_ref, t_ref, d_ref, db_ref, l_ref):
        out = x_ref[...] + jnp.dot(m_ref[...], w_ref[...], preferred_element_type=F32)
        diff = out - t_ref[...]
        d = diff * (1.0 / D_MODEL)
        d_ref[...] = d
        db_ref[...] = d.astype(BF16)
        l_ref[...] = jnp.full((1, 8, 128), jnp.sum(diff * diff), F32)

    return pl.pallas_call(
        body,
        name="outproj_loss",
        grid=(nj, ni),
        in_specs=[pl.BlockSpec((tm, D_MODEL), lambda j, i: (i, 0)),
                  pl.BlockSpec((D_MODEL, tn), lambda j, i: (0, j)),
                  pl.BlockSpec((tm, tn), lambda j, i: (i, j)),
                  pl.BlockSpec((tm, tn), lambda j, i: (i, j))],
        out_specs=[pl.BlockSpec((tm, tn), lambda j, i: (i, j)), pl.BlockSpec((tm, tn), lambda j, i: (i, j)),
                   pl.BlockSpec((1, 8, 128), lambda j, i: (i * nj + j, 0, 0))],
        out_shape=[jax.ShapeDtypeStruct((L, D_MODEL), F32), jax.ShapeDtypeStruct((L, D_MODEL), BF16),
                   jax.ShapeDtypeStruct((ni * nj, 8, 128), F32)],
        compiler_params=_cp(("parallel", "parallel")),
    )(merged, w_out, x, target)


def _merge_bwd(d_out_b, w_out, og, o, yg, gpre, proj, b_glu, wa, ws):
    L = og.shape[0]
    tm = _tile(L, 256)

    def body(dout_ref, wo_ref, og_ref, o_ref, yg_ref, gp_ref, za0_ref, za1_ref, zs0_ref, zs1_ref, b_ref, wa_ref,
             ws_ref, do_ref, dza_ref, dzs_ref, dg_ref, dyg_ref, gwa_ref, gws_ref, gb_ref):
        i = pl.program_id(0)

        @pl.when(i == 0)
        def _():
            gwa_ref[...] = jnp.zeros_like(gwa_ref)
            gws_ref[...] = jnp.zeros_like(gws_ref)
            gb_ref[...] = jnp.zeros_like(gb_ref)

        dm = lax.dot_general(dout_ref[...], wo_ref[...], _NT, preferred_element_type=F32)
        za = jnp.concatenate([za0_ref[...], za1_ref[...]], axis=1)
        zs = jnp.concatenate([zs0_ref[...], zs1_ref[...]], axis=1)
        ogv, dma = og_ref[...], dm[:, :ATTN_W]
        ra = lax.rsqrt(jnp.mean(ogv * ogv, axis=-1, keepdims=True) + NORM_EPS)
        xh = ogv * ra
        gwa_ref[...] += jnp.sum(dma * xh, axis=0, keepdims=True)
        gx = dma * wa_ref[...]
        d_og = ra * (gx - xh * jnp.mean(gx * xh, axis=-1, keepdims=True))
        do_ref[...] = d_og * _silu(za)
        dza_ref[...] = (d_og * o_ref[...] * _dsilu(za)).astype(BF16)
        ygv = yg_ref[...]
        sg = _sigmoid(gp_ref[...] + b_ref[...])
        y2 = ygv * sg
        sz = _silu(zs)
        os_ = y2 * sz
        dms = dm[:, ATTN_W:]
        rs = lax.rsqrt(jnp.mean(os_ * os_, axis=-1, keepdims=True) + NORM_EPS)
        xs = os_ * rs
        gws_ref[...] += jnp.sum(dms * xs, axis=0, keepdims=True)
        gxs = dms * ws_ref[...]
        d_os = rs * (gxs - xs * jnp.mean(gxs * xs, axis=-1, keepdims=True))
        dzs_ref[...] = (d_os * y2 * _dsilu(zs)).astype(BF16)
        d_y2 = d_os * sz
        d_g = d_y2 * ygv * sg * (1.0 - sg)
        dg_ref[...] = d_g.astype(BF16)
        gb_ref[...] += jnp.sum(d_g, axis=0, keepdims=True)
        dyg_ref[...] = d_y2 * sg

    row = lambda w: pl.BlockSpec((1, w), lambda i: (0, 0))
    full = lambda w: pl.BlockSpec((tm, w), lambda i: (i, 0))
    half = lambda c: pl.BlockSpec((tm, 512), lambda i: (i, c))
    return pl.pallas_call(
        body,
        name="merge_bwd",
        grid=(L // tm,),
        in_specs=[full(D_MODEL), pl.BlockSpec((D_MODEL, D_MODEL), lambda i: (0, 0)),
                  full(ATTN_W), full(ATTN_W), full(SSM_W), full(SSM_W),
                  half(3), half(4), half(7), half(8), row(SSM_W), row(ATTN_W), row(SSM_W)],
        out_specs=[full(ATTN_W), full(ATTN_W), full(SSM_W), full(SSM_W), full(SSM_W),
                   row(ATTN_W), row(SSM_W), row(SSM_W)],
        out_shape=[jax.ShapeDtypeStruct((L, ATTN_W), F32), jax.ShapeDtypeStruct((L, ATTN_W), BF16),
                   jax.ShapeDtypeStruct((L, SSM_W), BF16), jax.ShapeDtypeStruct((L, SSM_W), BF16),
                   jax.ShapeDtypeStruct((L, SSM_W), F32),
                   jax.ShapeDtypeStruct((1, ATTN_W), F32), jax.ShapeDtypeStruct((1, SSM_W), F32),
                   jax.ShapeDtypeStruct((1, SSM_W), F32)],
        compiler_params=_cp(("arbitrary",)),
    )(d_out_b, w_out, og, o, yg, gpre, proj, proj, proj, proj, b_glu.reshape(1, SSM_W), wa.reshape(1, ATTN_W),
      ws.reshape(1, SSM_W))


def _rms_bwd_x(x, norm_w, d_hn, d_out, ride):
    L = x.shape[0]
    tm = _tile(L, 256)

    def body(x_ref, w_ref, dh_ref, do_ref, gx_ref, gw_ref):
        i = pl.program_id(0)

        @pl.when(i == 0)
        def _():
            gw_ref[...] = jnp.zeros_like(gw_ref)

        xv, dh = x_ref[...], dh_ref[...]
        r = lax.rsqrt(jnp.mean(xv * xv, axis=-1, keepdims=True) + NORM_EPS)
        xh = xv * r
        gw_ref[...] += jnp.sum(dh * xh, axis=0, keepdims=True)
        gx = dh * w_ref[...]
        gx_ref[...] = do_ref[...] + r * (gx - xh * jnp.mean(gx * xh, axis=-1, keepdims=True))

    blk = pl.BlockSpec((tm, D_MODEL), lambda i: (i, 0))
    row = pl.BlockSpec((1, D_MODEL), lambda i: (0, 0))
    return _call(body, "rms_bwd_x", (L // tm,), [blk, row, blk, blk], [blk, row],
                 [jax.ShapeDtypeStruct((L, D_MODEL), F32), jax.ShapeDtypeStruct((1, D_MODEL), F32)],
                 (x, norm_w.reshape(1, D_MODEL), d_hn, d_out), ride=ride)


def _rope_table(positions):
    lane = jnp.arange(256)
    inv_freq = ROPE_THETA ** (-(2 * (lane % (HEAD_DIM // 2))).astype(F32) / HEAD_DIM)
    ang = positions.astype(F32)[:, None] * inv_freq[None, :]
    sign = jnp.where(lane % HEAD_DIM < HEAD_DIM // 2, -1.0, 1.0)
    return jnp.where(lane < 128, jnp.cos(ang), sign * jnp.sin(ang))


def _step(x, positions, target, w, core, chip):
    small = {n: w[n] for n in _SMALL}
    tab = _rope_table(positions)
    mt_b, scat_b, ocat_b, a16 = _ssm_prep(small)
    perm = _chunk_perm()
    blocks = lambda t: t.reshape(N_DEV, t.shape[0] // N_DEV, t.shape[1])

    proj, hn, wt_in = _rms_inproj_gather(x, small["norm_w"], w["w_in"].T.astype(BF16), chip)
    wt_in = wt_in.reshape(IN_W, D_MODEL)
    q_rot, k_rot = _qk_prep(proj, tab, small["q_norm_w"], small["k_norm_w"])
    (og, o, lse), (w_glu,) = _attn_fwd(q_rot, k_rot, proj, small["sinks"],
                                       _gather_exchange([w["w_glu"].astype(BF16)]))
    (y, yg, hx), (w_out,) = _ssm_fwd(proj, perm, mt_b, scat_b, ocat_b, a16, small["d_skip"],
                                     _gather_exchange([w["w_out"].astype(BF16)]))
    w_glu, w_out = w_glu.reshape(SSM_W, SSM_W), w_out.reshape(D_MODEL, D_MODEL)
    merged, gpre = _merge(og, yg, w_glu, proj, small["b_glu"], small["attn_out_norm_w"], small["ssm_out_norm_w"])
    d_out, d_out_b, loss_parts = _outproj_loss(merged, w_out, x, target)
    loss = 0.5 * jnp.sum(loss_parts[:, 0, 0]) / D_MODEL

    g_w_out = blocks(_mm(merged, d_out_b, "tn", F32, "grad_w_out", tm=1024))
    d_o, d_za, d_zs, d_g, d_yg1, g_wa, g_ws, g_bglu = _merge_bwd(
        d_out_b, w_out, og, o, yg, gpre, proj, small["b_glu"], small["attn_out_norm_w"], small["ssm_out_norm_w"])
    g_w_glu = blocks(_mm(yg, d_g, "tn", F32, "grad_w_glu"))
    d_yg = _mm(d_g, w_glu, "nt", F32, "d_yg", add=d_yg1)
    (d_u, g_mt, g_scat, g_ocat, g_a16, g_dskip), (ra_out, ra_glu) = _ssm_bwd(
        d_yg, y, proj, hx, perm, mt_b, scat_b, ocat_b, a16, small["d_skip"], _pair_exchange([g_w_out, g_w_glu]))
    p_out = _pair_sum(g_w_out, ra_out, core, BF16, "pair_sum_out")
    p_glu = _pair_sum(g_w_glu, ra_glu, core, BF16, "pair_sum_glu")
    (d_q, d_k, d_v, g_sinks), (rb_out, rb_glu) = _attn_bwd(
        q_rot, k_rot, proj, small["sinks"], d_o, o, lse, _chip_exchange([p_out, p_glu]))
    d_proj, g_qw, g_kw = _qk_prep_bwd(proj, tab, small["q_norm_w"], small["k_norm_w"], d_q, d_k, d_v,
                                      d_za, d_u, d_zs)
    g_qw = g_qw[0, :HEAD_DIM] + g_qw[0, HEAD_DIM:]
    g_kw = g_kw[0, :HEAD_DIM] + g_kw[0, HEAD_DIM:]
    g_in_a = blocks(_mm(d_proj, hn, "tn", F32, "grad_w_in_a", tm=1152, panel=0))
    g_in_b, (ra_a,) = _mm(d_proj, hn, "tn", F32, "grad_w_in_b", tm=1152, panel=1, ride=_pair_exchange([g_in_a]))
    g_in_b = blocks(g_in_b)
    p_a = _pair_sum(g_in_a, ra_a, core, BF16, "pair_sum_in_a")
    d_hn, (rb_a, ra_b) = _mm(d_proj, wt_in, "nn", F32, "d_hn", tm=1024,
                             ride=_both(_chip_exchange([p_a]), _pair_exchange([g_in_b])))
    p_b = _pair_sum(g_in_b, ra_b, core, BF16, "pair_sum_in_b")
    g_small, (rb_b,) = _ssm_prep_bwd(small, g_mt, g_scat, g_ocat, g_a16, _chip_exchange([p_b]))
    (grad_x, g_nw), _ = _rms_bwd_x(x, small["norm_w"], d_hn, d_out, None)

    g_small.update(norm_w=g_nw.reshape(-1), q_norm_w=g_qw.reshape(-1), k_norm_w=g_kw.reshape(-1),
                   sinks=g_sinks[0, :N_HEADS], d_skip=g_dskip.reshape(-1), b_glu=g_bglu.reshape(-1),
                   attn_out_norm_w=g_wa.reshape(-1), ssm_out_norm_w=g_ws.reshape(-1))
    g_packed = _slab_all_reduce(_pack(g_small, loss).reshape(N_DEV, _PACK_ROWS // N_DEV, 128))
    g_packed = g_packed.reshape(_PACK_ROWS, 128)
    grads = _unpack(g_packed, w)
    parts = dict(w_in=([p_a, p_b], [rb_a, rb_b]), w_glu=([p_glu], [rb_glu]), w_out=([p_out], [rb_out]))
    return g_packed[_LOSS_ROW, 0], grad_x, grads, parts


_ANY = pl.BlockSpec(memory_space=pl.ANY)


class _Exchange:
    def __init__(self, arrays, out_shape, sems, start, finish, relay=None):
        self.arrays, self.out_shape, self.sems, self.start, self.finish = arrays, out_shape, sems, start, finish
        self.relay = relay if relay is not None else (lambda ins, outs, sems: None)


def _gather_exchange(blocks):
    n = len(blocks)

    def parts(ins, outs, sems):
        send_sems, recv_sems, local_sems = sems
        x, y, c = lax.axis_index("x"), lax.axis_index("y"), lax.axis_index("c")
        me, sibling = (x, y, c), (x, y, 1 - c)
        chips = [(1 - x, y), (x, 1 - y), (1 - x, 1 - y)]

        def slot(k, dev):
            return outs[k].at[4 * dev[0] + 2 * dev[1] + dev[2]]

        def copy(k, q, block, to, src=None):
            return pltpu.make_async_remote_copy(
                src_ref=slot(k, block) if src is None else src, dst_ref=slot(k, block),
                send_sem=send_sems.at[k, q], recv_sem=recv_sems.at[k, q], device_id=to, device_id_type=MESH)

        mine = [pltpu.make_async_copy(ins[k], slot(k, me), local_sems.at[k]) for k in range(n)]
        first = []
        for k in range(n):
            first.append(copy(k, 0, me, sibling, src=ins[k]))
            first += [copy(k, 1 + j, me, (*chip, c), src=ins[k]) for j, chip in enumerate(chips)]
        return me, sibling, chips, c, copy, mine, first

    def start(ins, outs, sems):
        *_, mine, first = parts(ins, outs, sems)
        for cp in mine + first:
            cp.start()

    def relay(ins, outs, sems):
        me, sibling, chips, c, copy, _, _ = parts(ins, outs, sems)
        for j, chip in enumerate(chips):
            for k in range(n):
                copy(k, 1 + j, (*chip, c), me).wait_recv()
                copy(k, 4 + j, (*chip, c), sibling).start()

    def finish(ins, outs, sems):
        me, sibling, chips, c, copy, mine, first = parts(ins, outs, sems)
        for k in range(n):
            copy(k, 0, sibling, me).wait_recv()
            for j, chip in enumerate(chips):
                copy(k, 4 + j, (*chip, 1 - c), me).wait_recv()
        for cp in first + [copy(k, 4 + j, (*chip, c), sibling) for k in range(n) for j, chip in enumerate(chips)]:
            cp.wait_send()
        for cp in mine:
            cp.wait()

    return _Exchange(blocks, [jax.ShapeDtypeStruct((N_DEV,) + b.shape, b.dtype) for b in blocks],
                     [pltpu.SemaphoreType.DMA((n, 7)), pltpu.SemaphoreType.DMA((n, 7)), pltpu.SemaphoreType.DMA((n,))],
                     start, finish, relay)


def _direct_exchange(arrays, out_lead, fan, route):
    n = len(arrays)

    def copies(ins, outs, sems):
        send_sems, recv_sems = sems
        legs = route(lax.axis_index("x"), lax.axis_index("y"), lax.axis_index("c"))
        return [pltpu.make_async_remote_copy(
            src_ref=ins[k].at[src], dst_ref=outs[k].at[q], send_sem=send_sems.at[k, q], recv_sem=recv_sems.at[k, q],
            device_id=to, device_id_type=MESH) for k in range(n) for src, q, to in legs]

    def start(ins, outs, sems):
        for cp in copies(ins, outs, sems):
            cp.start()

    def finish(ins, outs, sems):
        for cp in copies(ins, outs, sems):
            cp.wait()

    return _Exchange(arrays, [jax.ShapeDtypeStruct((out_lead,) + a.shape[1:], a.dtype) for a in arrays],
                     [pltpu.SemaphoreType.DMA((n, fan)), pltpu.SemaphoreType.DMA((n, fan))], start, finish)


def _pair_exchange(grads):
    return _direct_exchange(grads, 4, 4, lambda x, y, c: [(2 * chip + (1 - c), chip, (x, y, 1 - c))
                                                          for chip in range(4)])


def _chip_exchange(parts):
    def route(x, y, c):
        chips = [(1 - x, y), (x, 1 - y), (1 - x, 1 - y)]
        return [(2 * chip[0] + chip[1], q, (*chip, c)) for q, chip in enumerate(chips)]
    return _direct_exchange(parts, 3, 3, route)


def _both(ex1, ex2):
    n1, s1 = len(ex1.arrays), len(ex1.sems)

    def halves(ins, outs, sems):
        return (ins[:n1], outs[:n1], sems[:s1]), (ins[n1:], outs[n1:], sems[s1:])

    def start(ins, outs, sems):
        h1, h2 = halves(ins, outs, sems)
        ex1.start(*h1)
        ex2.start(*h2)

    def relay(ins, outs, sems):
        h1, h2 = halves(ins, outs, sems)
        ex1.relay(*h1)
        ex2.relay(*h2)

    def finish(ins, outs, sems):
        h1, h2 = halves(ins, outs, sems)
        ex1.finish(*h1)
        ex2.finish(*h2)

    return _Exchange(list(ex1.arrays) + list(ex2.arrays), list(ex1.out_shape) + list(ex2.out_shape),
                     list(ex1.sems) + list(ex2.sems), start, finish, relay)


def _call(body, name, grid, in_specs, out_specs, out_shape, args, scratch_shapes=(), ride=None):
    if ride is None:
        sem = ("arbitrary",) * len(grid)
        return pl.pallas_call(body, name=name, grid=grid, in_specs=in_specs, out_specs=out_specs, out_shape=out_shape,
                              scratch_shapes=list(scratch_shapes), compiler_params=_cp(sem))(*args), None
    n_in, n_out, n_scr, n_x = len(in_specs), len(out_specs), len(scratch_shapes), len(ride.arrays)

    def wrapped(*refs):
        ins, refs = refs[:n_in], refs[n_in:]
        x_in, refs = refs[:n_x], refs[n_x:]
        outs, refs = refs[:n_out], refs[n_out:]
        x_out, refs = refs[:n_x], refs[n_x:]
        scr, sems = refs[:n_scr], refs[n_scr:]
        step, total = pl.program_id(0), grid[0]
        for a in range(1, len(grid)):
            step, total = step * grid[a] + pl.program_id(a), total * grid[a]
        @pl.when(step == 0)
        def _():
            ride.start(x_in, x_out, sems)

        @pl.when(step == max(total - 2, 0))
        def _():
            ride.relay(x_in, x_out, sems)

        body(*ins, *outs, *scr)

        @pl.when(step == total - 1)
        def _():
            ride.finish(x_in, x_out, sems)

    res = pl.pallas_call(
        wrapped, name=name, grid=grid, in_specs=list(in_specs) + [_ANY] * n_x,
        out_specs=list(out_specs) + [_ANY] * n_x, out_shape=list(out_shape) + list(ride.out_shape),
        scratch_shapes=list(scratch_shapes) + list(ride.sems),
        compiler_params=_cp(("arbitrary",) * len(grid)))(*args, *ride.arrays)
    return res[:n_out], list(res[n_out:])


def _pair_sum(g, ra, core, out_dtype, name):
    _, r, C = g.shape
    tr = _tile(r, 576)

    def body(c_ref, g_ref, ra_ref, p_ref):
        p_ref[...] = (g_ref[...] + ra_ref[...]).astype(p_ref.dtype)

    return pl.pallas_call(
        body,
        name=name,
        grid_spec=pltpu.PrefetchScalarGridSpec(
            num_scalar_prefetch=1,
            grid=(4, r // tr),
            in_specs=[pl.BlockSpec((1, tr, C), lambda j, t, c_ref: (2 * j + c_ref[0], t, 0)),
                      pl.BlockSpec((1, tr, C), lambda j, t, c_ref: (j, t, 0))],
            out_specs=pl.BlockSpec((1, tr, C), lambda j, t, c_ref: (j, t, 0)),
        ),
        out_shape=jax.ShapeDtypeStruct((4, r, C), out_dtype),
        compiler_params=_cp(("parallel", "parallel")),
    )(core, g, ra)


def _slab_all_reduce(slab):
    _, r, lanes = slab.shape

    def body(s_ref, o_ref, ra, rb, ps, sems_a, sems_b, sems_c):
        x, y, c = lax.axis_index("x"), lax.axis_index("y"), lax.axis_index("c")
        chips = [(1 - x, y), (x, 1 - y), (1 - x, 1 - y)]
        pair = [pltpu.make_async_remote_copy(
            src_ref=s_ref.at[2 * k + (1 - c)], dst_ref=ra.at[k], send_sem=sems_a.at[0, k], recv_sem=sems_a.at[1, k],
            device_id=(x, y, 1 - c), device_id_type=MESH) for k in range(4)]
        for cp in pair:
            cp.start()
        for cp in pair:
            cp.wait()
        for k in range(4):
            ps[k] = s_ref[2 * k + c] + ra[k]
        cross = [pltpu.make_async_remote_copy(
            src_ref=ps.at[2 * ch[0] + ch[1]], dst_ref=rb.at[q], send_sem=sems_b.at[0, q], recv_sem=sems_b.at[1, q],
            device_id=(*ch, c), device_id_type=MESH) for q, ch in enumerate(chips)]
        for cp in cross:
            cp.start()
        for cp in cross:
            cp.wait()
        me = 4 * x + 2 * y + c
        o_ref[me] = ((ps[2 * x + y] + rb[0]) + rb[1]) + rb[2]
        flips = [(dx, dy, dc) for dx in (0, 1) for dy in (0, 1) for dc in (0, 1) if dx + dy + dc]
        spread = [pltpu.make_async_remote_copy(
            src_ref=o_ref.at[me], dst_ref=o_ref.at[me], send_sem=sems_c.at[0, q], recv_sem=sems_c.at[1, q],
            device_id=(x + dx - 2 * x * dx, y + dy - 2 * y * dy, c + dc - 2 * c * dc), device_id_type=MESH)
            for q, (dx, dy, dc) in enumerate(flips)]
        for cp in spread:
            cp.start()
        for q, (dx, dy, dc) in enumerate(flips):
            peer = 4 * (x + dx - 2 * x * dx) + 2 * (y + dy - 2 * y * dy) + (c + dc - 2 * c * dc)
            pltpu.make_async_remote_copy(
                src_ref=o_ref.at[peer], dst_ref=o_ref.at[peer], send_sem=sems_c.at[0, q], recv_sem=sems_c.at[1, q],
                device_id=(x, y, c), device_id_type=MESH).wait_recv()
        for cp in spread:
            cp.wait_send()

    whole = pl.BlockSpec(memory_space=pltpu.VMEM)
    return pl.pallas_call(
        body, name="slab_all_reduce", in_specs=[whole], out_specs=whole,
        out_shape=jax.ShapeDtypeStruct(slab.shape, F32),
        scratch_shapes=[pltpu.VMEM((4, r, lanes), F32), pltpu.VMEM((3, r, lanes), F32), pltpu.VMEM((4, r, lanes), F32),
                        pltpu.SemaphoreType.DMA((2, 4)), pltpu.SemaphoreType.DMA((2, 3)),
                        pltpu.SemaphoreType.DMA((2, 7))],
        compiler_params=_cp(),
    )(slab)


def _adamw_reduced(ps, rbs, chip, w, m, v, name):
    nh = len(ps)
    R, C = w.shape
    ch = C // nh
    tr = _tile(R, 288)
    nt = R // tr
    c1 = 1.0 - ADAM_B1 ** ADAM_STEP
    c2 = 1.0 - ADAM_B2 ** ADAM_STEP

    def body(c_ref, *refs):
        p_refs, rb_refs = refs[:nh], refs[nh:2 * nh]
        w_ref, m_ref, v_ref, g_ref, d_ref, nm_ref, nv_ref = refs[2 * nh:]
        for h in range(nh):
            @pl.when(pl.program_id(0) == h)
            def _(h=h):
                rb = rb_refs[h]
                gv = p_refs[h][0].astype(F32) + rb[0].astype(F32)
                gv = gv + rb[1].astype(F32)
                gv = gv + rb[2].astype(F32)
                nm = ADAM_B1 * m_ref[...] + (1.0 - ADAM_B1) * gv
                nv = ADAM_B2 * v_ref[...] + (1.0 - ADAM_B2) * (gv * gv)
                g_ref[...] = gv
                nm_ref[...] = nm
                nv_ref[...] = nv
                d_ref[...] = -ADAM_LR * ((nm / c1) / (jnp.sqrt(nv / c2) + ADAM_EPS) + ADAM_WD * w_ref[...])

    def held(h):
        return lambda hh, tt: jnp.where(hh == h, tt, jnp.where(hh < h, 0, nt - 1))

    p_specs = [pl.BlockSpec((1, tr, ch), lambda hh, tt, c_ref, f=held(h): (c_ref[0], f(hh, tt), 0))
               for h in range(nh)]
    rb_specs = [pl.BlockSpec((3, tr, ch), lambda hh, tt, c_ref, f=held(h): (0, f(hh, tt), 0)) for h in range(nh)]
    blk = pl.BlockSpec((tr, ch), lambda hh, tt, c_ref: (tt, hh))
    return pl.pallas_call(
        body,
        name=name,
        grid_spec=pltpu.PrefetchScalarGridSpec(
            num_scalar_prefetch=1, grid=(nh, nt), in_specs=p_specs + rb_specs + [blk] * 3, out_specs=[blk] * 4),
        out_shape=[jax.ShapeDtypeStruct((R, C), F32)] * 4,
        compiler_params=_cp(("arbitrary", "arbitrary")),
    )(chip, *ps, *rbs, w, m, v)


_SMALL = ("norm_w", "q_norm_w", "k_norm_w", "sinks", "a_re", "a_im", "log_step", "b_re", "b_im", "c_re", "c_im",
          "d_skip", "b_glu", "attn_out_norm_w", "ssm_out_norm_w")
_WEIGHTS = ("norm_w", "w_in", "q_norm_w", "k_norm_w", "sinks", "a_re", "a_im", "log_step", "b_re", "b_im", "c_re",
            "c_im", "d_skip", "w_glu", "b_glu", "attn_out_norm_w", "ssm_out_norm_w", "w_out")
_SMALL_2D = dict(norm_w=(1, 2048), q_norm_w=(1, 64), k_norm_w=(1, 64), sinks=(1, 16), a_re=(64, 64), a_im=(64, 64),
                 log_step=(1, 64), b_re=(1024, 64), b_im=(1024, 64), c_re=(1024, 64), c_im=(1024, 64),
                 d_skip=(1, 1024), b_glu=(1, 1024), attn_out_norm_w=(1, 1024), ssm_out_norm_w=(1, 1024))
_P_MINOR = ("b_re", "b_im")


def _flat_form(n, t):
    return t.transpose(0, 2, 1) if n in _P_MINOR else t


def _own_form(n, t, shape):
    if n in _P_MINOR:
        return t.reshape(shape[0], shape[2], shape[1]).transpose(0, 2, 1)
    return t.reshape(shape)


def _slab_rows(n):
    return -(-n // 1024) * 8


_PACK_ROWS = 2304


_LOSS_ROW = 2192


def _pack(d, loss):
    parts = []
    for n in _SMALL:
        flat = _flat_form(n, d[n]).reshape(-1).astype(F32)
        rows = _slab_rows(flat.shape[0])
        parts.append(jnp.pad(flat, (0, rows * 128 - flat.shape[0])).reshape(rows, 128))
    assert sum(p.shape[0] for p in parts) == _LOSS_ROW
    parts.append(jnp.pad(loss.reshape(1, 1), ((0, _PACK_ROWS - _LOSS_ROW - 1), (0, 127))))
    return jnp.concatenate(parts, axis=0)


def _unpack(packed, like):
    out, off = {}, 0
    for n in _SMALL:
        size = math.prod(like[n].shape)
        rows = _slab_rows(size)
        out[n] = _own_form(n, packed[off:off + rows].reshape(-1)[:size], like[n].shape)
        off += rows
    return out


def _adamw_small(g, w, m, v):
    c1 = 1.0 - ADAM_B1 ** ADAM_STEP
    c2 = 1.0 - ADAM_B2 ** ADAM_STEP
    k = len(_SMALL)

    def body(*refs):
        ins, outs = refs[:4 * k], refs[4 * k:]
        for j in range(k):
            gv, wv, mv, vv = (ins[q * k + j][...] for q in range(4))
            nm = ADAM_B1 * mv + (1.0 - ADAM_B1) * gv
            nv = ADAM_B2 * vv + (1.0 - ADAM_B2) * (gv * gv)
            outs[j][...] = -ADAM_LR * ((nm / c1) / (jnp.sqrt(nv / c2) + ADAM_EPS) + ADAM_WD * wv)
            outs[k + j][...] = nm
            outs[2 * k + j][...] = nv

    args = [_flat_form(n, d[n]).reshape(_SMALL_2D[n]) for d in (g, w, m, v) for n in _SMALL]
    shapes = [jax.ShapeDtypeStruct(_SMALL_2D[n], F32) for _ in range(3) for n in _SMALL]
    outs = pl.pallas_call(body, name="adamw_small", out_shape=shapes, compiler_params=_cp())(*args)
    res = []
    for q in range(3):
        res.append({n: _own_form(n, outs[q * k + j], w[n].shape) for j, n in enumerate(_SMALL)})
    return res


def kernel(x, positions, norm_w, w_in, q_norm_w, k_norm_w, sinks, a_re, a_im, log_step, b_re, b_im, c_re, c_im, d_skip, w_glu, b_glu, attn_out_norm_w, ssm_out_norm_w, w_out, loss_target, m_norm_w, m_w_in, m_q_norm_w, m_k_norm_w, m_sinks, m_a_re, m_a_im, m_log_step, m_b_re, m_b_im, m_c_re, m_c_im, m_d_skip, m_w_glu, m_b_glu, m_attn_out_norm_w, m_ssm_out_norm_w, m_w_out, v_norm_w, v_w_in, v_q_norm_w, v_k_norm_w, v_sinks, v_a_re, v_a_im, v_log_step, v_b_re, v_b_im, v_c_re, v_c_im, v_d_skip, v_w_glu, v_b_glu, v_attn_out_norm_w, v_ssm_out_norm_w, v_w_out):
    w = dict(norm_w=norm_w, w_in=w_in, q_norm_w=q_norm_w, k_norm_w=k_norm_w, sinks=sinks, a_re=a_re, a_im=a_im,
             log_step=log_step, b_re=b_re, b_im=b_im, c_re=c_re, c_im=c_im, d_skip=d_skip, w_glu=w_glu, b_glu=b_glu,
             attn_out_norm_w=attn_out_norm_w, ssm_out_norm_w=ssm_out_norm_w, w_out=w_out)
    m = dict(norm_w=m_norm_w, w_in=m_w_in, q_norm_w=m_q_norm_w, k_norm_w=m_k_norm_w, sinks=m_sinks, a_re=m_a_re,
             a_im=m_a_im, log_step=m_log_step, b_re=m_b_re, b_im=m_b_im, c_re=m_c_re, c_im=m_c_im, d_skip=m_d_skip,
             w_glu=m_w_glu, b_glu=m_b_glu, attn_out_norm_w=m_attn_out_norm_w, ssm_out_norm_w=m_ssm_out_norm_w,
             w_out=m_w_out)
    v = dict(norm_w=v_norm_w, w_in=v_w_in, q_norm_w=v_q_norm_w, k_norm_w=v_k_norm_w, sinks=v_sinks, a_re=v_a_re,
             a_im=v_a_im, log_step=v_log_step, b_re=v_b_re, b_im=v_b_im, c_re=v_c_re, c_im=v_c_im, d_skip=v_d_skip,
             w_glu=v_w_glu, b_glu=v_b_glu, attn_out_norm_w=v_attn_out_norm_w, ssm_out_norm_w=v_ssm_out_norm_w,
             w_out=v_w_out)
    core = lax.axis_index("c").astype(jnp.int32).reshape(1)
    chip = (2 * lax.axis_index("x") + lax.axis_index("y")).astype(jnp.int32).reshape(1)

    loss, grad_x, grads, parts = _step(x[0], positions[0], loss_target[0], w, core, chip)
    delta, new_m, new_v = {}, {}, {}
    for n in ("w_glu", "w_out"):
        grads[n], delta[n], new_m[n], new_v[n] = _adamw_reduced(*parts[n], chip, w[n], m[n], v[n], f"adamw_{n}")
    g_t, d_t, m_t, v_t = _adamw_reduced(*parts["w_in"], chip, w["w_in"].T, m["w_in"].T, v["w_in"].T, "adamw_w_in")
    grads["w_in"], delta["w_in"], new_m["w_in"], new_v["w_in"] = g_t.T, d_t.T, m_t.T, v_t.T
    d_s, m_s, v_s = _adamw_small(grads, w, m, v)
    delta.update(d_s)
    new_m.update(m_s)
    new_v.update(v_s)

    return (loss, grad_x[None], *[grads[n] for n in _WEIGHTS], *[delta[n] for n in _WEIGHTS],
            *[new_m[n] for n in _WEIGHTS], *[new_v[n] for n in _WEIGHTS])
```

```python
import math

import jax
import jax.numpy as jnp
from jax import lax
from jax.experimental import pallas as pl
from jax.experimental.pallas import tpu as pltpu

F32 = jnp.float32
BF16 = jnp.bfloat16

D_MODEL = 2048
ATTN_W = 1024
KV_W = 256
SSM_W = 1024
HEAD_DIM = 64
N_HEADS = 16
N_KV = 4
IN_W = 4608
BLOCK = 128
ROPE_THETA = 10000.0
NORM_EPS = 1e-6
SSM_G = 64
SSM_P = 64
SSM_H = 16
CHUNK = 16
CW = CHUNK * SSM_H
N_DEV = 8

ADAM_LR = 0.001
ADAM_B1 = 0.9
ADAM_B2 = 0.999
ADAM_EPS = 1e-08
ADAM_WD = 0.01
ADAM_STEP = 10

VMEM_LIMIT = 56 * 1024 * 1024
MESH = pl.DeviceIdType.MESH


def _cp(sem=None):
    if sem is None:
        return pltpu.CompilerParams(vmem_limit_bytes=VMEM_LIMIT)
    return pltpu.CompilerParams(vmem_limit_bytes=VMEM_LIMIT, dimension_semantics=sem)


def _sigmoid(x):
    return 0.5 * jnp.tanh(0.5 * x) + 0.5


def _silu(x):
    return x * _sigmoid(x)


def _dsilu(x):
    s = _sigmoid(x)
    return s * (1.0 + x * (1.0 - s))


_GELU_C = math.sqrt(2.0 / math.pi)


def _gelu(y):
    t = jnp.tanh(_GELU_C * (y + 0.044715 * y * y * y))
    return 0.5 * y * (1.0 + t)


def _dgelu(y):
    t = jnp.tanh(_GELU_C * (y + 0.044715 * y * y * y))
    return 0.5 * (1.0 + t) + 0.5 * y * (1.0 - t * t) * _GELU_C * (1.0 + 3.0 * 0.044715 * y * y)


def _tile(n, want):
    if n <= want:
        return n
    for t in range(want - want % 16, 0, -16):
        if n % t == 0:
            return t
    raise ValueError((n, want))


def _mm(a, b, mode, out_dtype, name, tm=512, tn=1024, add=None, ride=None, panel=None):
    if mode == "nn":
        (M, K), (K2, N) = a.shape, b.shape
    elif mode == "nt":
        (M, K), (N, K2) = a.shape, b.shape
    else:
        (K, M), (K2, N) = a.shape, b.shape
    assert K == K2
    tm, tn = _tile(M, tm), _tile(N, tn)
    p0 = 0
    if panel is not None:
        assert mode != "nt" and add is None
        p0, N = panel, tn
    dn = {"nn": _NN, "nt": _NT, "tn": _TN}[mode]

    def body(a_ref, b_ref, *rest):
        o_ref = rest[-1]
        acc = lax.dot_general(a_ref[...].astype(BF16), b_ref[...].astype(BF16), dn, preferred_element_type=F32)
        if add is not None:
            acc = acc + rest[0][...]
        o_ref[...] = acc.astype(o_ref.dtype)

    a_spec = pl.BlockSpec((K, tm), lambda j, i: (0, i)) if mode == "tn" else pl.BlockSpec((tm, K), lambda j, i: (i, 0))
    b_spec = (pl.BlockSpec((tn, K), lambda j, i: (j, 0)) if mode == "nt"
              else pl.BlockSpec((K, tn), lambda j, i: (0, j + p0)))
    o_spec = pl.BlockSpec((tm, tn), lambda j, i: (i, j))
    extra = () if add is None else (add,)
    if ride is not None:
        (out,), landed = _call(body, name, (N // tn, M // tm), [a_spec, b_spec] + [o_spec] * len(extra), [o_spec],
                               [jax.ShapeDtypeStruct((M, N), out_dtype)], (a, b, *extra), ride=ride)
        return out, landed
    return pl.pallas_call(
        body,
        name=name,
        grid=(N // tn, M // tm),
        in_specs=[a_spec, b_spec] + [o_spec] * len(extra),
        out_specs=o_spec,
        out_shape=jax.ShapeDtypeStruct((M, N), out_dtype),
        compiler_params=_cp(("parallel", "parallel")),
    )(a, b, *extra)


_CHIP_ORDER = (0, 2, 1, 3)


def _rms_inproj_gather(x, norm_w, wt_shard, chip):
    L = x.shape[0]
    tm = _tile(L, 512)
    ni = L // tm
    r = IN_W // N_DEV
    tn = 2 * r

    def body(chip_ref, x_ref, nw_ref, shard, proj_ref, hn_ref, wt_hbm, hn_scr, w_scr, send_sems, recv_sems, loc_sems):
        jc, i = pl.program_id(0), pl.program_id(1)
        xx, yy, c = lax.axis_index("x"), lax.axis_index("y"), lax.axis_index("c")
        me, sibling = (xx, yy, c), (xx, yy, 1 - c)
        chips = [(1 - xx, yy), (xx, 1 - yy), (1 - xx, 1 - yy)]

        def slot(dev):
            return wt_hbm.at[4 * dev[0] + 2 * dev[1] + dev[2]]

        def copy(q, block, to, src=None):
            return pltpu.make_async_remote_copy(
                src_ref=slot(block) if src is None else src, dst_ref=slot(block),
                send_sem=send_sems.at[q], recv_sem=recv_sems.at[q], device_id=to, device_id_type=MESH)

        def rows_of(buf, core):
            return w_scr.at[buf, pl.ds(pl.multiple_of(core * r, 16), r)]

        mine = pltpu.make_async_copy(shard, slot(me), loc_sems.at[0])
        sends = [copy(0, me, sibling, src=shard)] + [copy(1 + j, me, (*ch, c), src=shard) for j, ch in enumerate(chips)]
        first = jnp.logical_and(jc == 0, i == 0)

        @pl.when(first)
        def _():
            mine.start()
            for cp in sends[:3]:
                cp.start()
            own = pltpu.make_async_copy(shard, rows_of(0, c), loc_sems.at[1])
            own.start()
            copy(0, sibling, me).wait_recv()
            sib = pltpu.make_async_copy(slot(sibling), rows_of(0, 1 - c), loc_sems.at[2])
            sib.start()
            own.wait()
            sib.wait()

        def take_direct(j, ch):
            copy(1 + j, (*ch, c), me).wait_recv()
            copy(4 + j, (*ch, c), sibling).start()
            if j == 0:
                sends[1].wait_send()
                sends[2].wait_send()
                sends[3].start()
            pltpu.make_async_copy(slot((*ch, c)), rows_of((1 + j) % 2, c), loc_sems.at[1]).start()

        for j, ch in enumerate(chips):
            early = jnp.logical_and(jc == j, i == ni // 2) if j > 0 else jnp.logical_and(jc == 1, i == 0)

            @pl.when(early)
            def _(j=j, ch=ch):
                take_direct(j, ch)

            @pl.when(jnp.logical_and(jc == 1 + j, i == 0))
            def _(j=j, ch=ch):
                buf = (1 + j) % 2
                copy(4 + j, (*ch, 1 - c), me).wait_recv()
                passed = pltpu.make_async_copy(slot((*ch, 1 - c)), rows_of(buf, 1 - c), loc_sems.at[2])
                passed.start()
                pltpu.make_async_copy(slot((*ch, c)), rows_of(buf, c), loc_sems.at[1]).wait()
                passed.wait()

        rows = pl.ds(pl.multiple_of(i * tm, tm), tm)

        @pl.when(jc == 0)
        def _():
            xv = x_ref[...]
            rstd = lax.rsqrt(jnp.mean(xv * xv, axis=-1, keepdims=True) + NORM_EPS)
            hn = (xv * rstd * nw_ref[...]).astype(BF16)
            hn_scr[rows, :] = hn
            hn_ref[...] = hn

        for buf in range(2):
            @pl.when(jc % 2 == buf)
            def _(buf=buf):
                proj_ref[...] = lax.dot_general(hn_scr[rows, :], w_scr[buf], _NT, preferred_element_type=F32)

        @pl.when(jnp.logical_and(jc == 3, i == ni - 1))
        def _():
            sends[0].wait_send()
            sends[3].wait_send()
            for j, ch in enumerate(chips):
                copy(4 + j, (*ch, c), sibling).wait_send()
            mine.wait()

    def tile_of(jc, chip_ref):
        mask = jnp.where(jc == 1, _CHIP_ORDER[1], jnp.where(jc == 2, _CHIP_ORDER[2], jnp.where(jc == 3, _CHIP_ORDER[3], 0)))
        return jnp.bitwise_xor(chip_ref[0], mask)

    held = lambda jc, i: jnp.where(jc == 0, i, ni - 1)
    return pl.pallas_call(
        body,
        name="rms_inproj_gather",
        grid_spec=pltpu.PrefetchScalarGridSpec(
            num_scalar_prefetch=1,
            grid=(4, ni),
            in_specs=[pl.BlockSpec((tm, D_MODEL), lambda jc, i, ch: (held(jc, i), 0)),
                      pl.BlockSpec((1, D_MODEL), lambda jc, i, ch: (0, 0)), _ANY],
            out_specs=[pl.BlockSpec((tm, tn), lambda jc, i, ch: (i, tile_of(jc, ch))),
                       pl.BlockSpec((tm, D_MODEL), lambda jc, i, ch: (held(jc, i), 0)), _ANY],
            scratch_shapes=[pltpu.VMEM((L, D_MODEL), BF16), pltpu.VMEM((2, tn, D_MODEL), BF16),
                            pltpu.SemaphoreType.DMA((7,)), pltpu.SemaphoreType.DMA((7,)), pltpu.SemaphoreType.DMA((3,))],
        ),
        out_shape=[jax.ShapeDtypeStruct((L, IN_W), F32), jax.ShapeDtypeStruct((L, D_MODEL), BF16),
                   jax.ShapeDtypeStruct((N_DEV, r, D_MODEL), BF16)],
        compiler_params=_cp(("arbitrary", "arbitrary")),
    )(chip, x, norm_w.reshape(1, D_MODEL), wt_shard)


def _seg_sum(v):
    a = lax.broadcasted_iota(jnp.int32, (128, 128), 0) // HEAD_DIM
    b = lax.broadcasted_iota(jnp.int32, (128, 128), 1) // HEAD_DIM
    ones = jnp.where(a == b, 1.0, 0.0).astype(BF16)
    hi = v.astype(BF16)
    lo = (v - hi.astype(F32)).astype(BF16)
    return jnp.dot(hi, ones, preferred_element_type=F32) + jnp.dot(lo, ones, preferred_element_type=F32)


def _rot_half(t):
    lane = lax.broadcasted_iota(jnp.int32, t.shape, 1)
    return jnp.where(lane % HEAD_DIM < HEAD_DIM // 2, pltpu.roll(t, 128 - HEAD_DIM // 2, 1),
                     pltpu.roll(t, HEAD_DIM // 2, 1))


def _norm_rope(raw, w, cos, sin):
    r = lax.rsqrt(_seg_sum(raw * raw) * (1.0 / HEAD_DIM) + NORM_EPS)
    tn = raw * r * w
    return r, tn * cos + _rot_half(tn) * sin


def _norm_rope_bwd(d_rot, raw, w, cos, sin):
    r = lax.rsqrt(_seg_sum(raw * raw) * (1.0 / HEAD_DIM) + NORM_EPS)
    d_tn = d_rot * cos + _rot_half(d_rot * sin)
    xh = raw * r
    gw = d_tn * w
    d_raw = r * (gw - xh * (_seg_sum(gw * xh) * (1.0 / HEAD_DIM)))
    return d_raw, d_tn * xh


def _band_mask2(has_prev):
    qi = lax.broadcasted_iota(jnp.int32, (2 * BLOCK, 2 * BLOCK), 0) % BLOCK + BLOCK
    kj = lax.broadcasted_iota(jnp.int32, (2 * BLOCK, 2 * BLOCK), 1)
    rel = qi - kj
    return (rel >= 0) & (rel < BLOCK) & ((kj >= BLOCK) | has_prev)


def _half_tiles(pair):
    lo = lax.broadcasted_iota(jnp.int32, pair.shape, 1) < HEAD_DIM
    sw = pltpu.roll(pair, HEAD_DIM, 1)
    z = jnp.zeros_like(pair)
    return (jnp.where(lo, pair, z).astype(BF16), jnp.where(lo, z, sw).astype(BF16),
            jnp.where(lo, sw, z).astype(BF16), jnp.where(lo, z, pair).astype(BF16))


def _two_rows(top, bottom):
    row = lax.broadcasted_iota(jnp.int32, (2 * BLOCK, 1), 0)
    return jnp.where(row < BLOCK, top, bottom)


def _lane_col(mat, h):
    lane = lax.broadcasted_iota(jnp.int32, mat.shape, 1)
    return jnp.sum(jnp.where(lane == h, mat, 0.0), axis=1, keepdims=True)


_SCALE = 1.0 / math.sqrt(HEAD_DIM)
_NT = (((1,), (1,)), ((), ()))
_NN = (((1,), (0,)), ((), ()))
_TN = (((0,), (0,)), ((), ()))


def _qk_prep(proj, tab, qw, kw):
    L = proj.shape[0]
    tm = _tile(L, 512)

    def body(q_ref, k_ref, t_ref, qw_ref, kw_ref, qo_ref, ko_ref):
        cos, sin = t_ref[:, :128], t_ref[:, 128:]
        for c in range(ATTN_W // 128):
            _, qr = _norm_rope(q_ref[:, c * 128:(c + 1) * 128], qw_ref[...], cos, sin)
            qo_ref[:, c * 128:(c + 1) * 128] = (qr * _SCALE).astype(BF16)
        for c in range(KV_W // 128):
            _, kr = _norm_rope(k_ref[:, c * 128:(c + 1) * 128], kw_ref[...], cos, sin)
            ko_ref[:, c * 128:(c + 1) * 128] = kr.astype(BF16)

    row = pl.BlockSpec((1, 128), lambda i: (0, 0))
    return pl.pallas_call(
        body,
        name="qk_prep",
        grid=(L // tm,),
        in_specs=[pl.BlockSpec((tm, ATTN_W), lambda i: (i, 0)), pl.BlockSpec((tm, KV_W), lambda i: (i, 4)),
                  pl.BlockSpec((tm, 256), lambda i: (i, 0)), row, row],
        out_specs=[pl.BlockSpec((tm, ATTN_W), lambda i: (i, 0)), pl.BlockSpec((tm, KV_W), lambda i: (i, 0))],
        out_shape=[jax.ShapeDtypeStruct((L, ATTN_W), BF16), jax.ShapeDtypeStruct((L, KV_W), BF16)],
        compiler_params=_cp(("parallel",)),
    )(proj, proj, tab, jnp.tile(qw, 2).reshape(1, 128), jnp.tile(kw, 2).reshape(1, 128))


def _group_tiles(g, kt, vt):
    a, b = divmod(g, 2)
    return kt[a][2 * b], kt[a][2 * b + 1], vt[a][2 * b], vt[a][2 * b + 1]


def _attn_fwd(q, k, proj, sinks, ride):
    L = proj.shape[0]
    nb = L // BLOCK

    def body(q_ref, kc_ref, kp_ref, vc_ref, vp_ref, z0_ref, z1_ref, sink_ref, og_ref, o_ref, lse_ref):
        i = pl.program_id(0)
        mask = _band_mask2(i > 0)
        z = jnp.concatenate([z0_ref[...], z1_ref[...]], axis=1)
        lane = lax.broadcasted_iota(jnp.int32, (BLOCK, 128), 1)
        kt = [_half_tiles(jnp.concatenate([kp_ref[:, a * 128:(a + 1) * 128], kc_ref[:, a * 128:(a + 1) * 128]],
                                          axis=0).astype(F32)) for a in range(2)]
        vt = [_half_tiles(jnp.concatenate([vp_ref[:, a * 128:(a + 1) * 128], vc_ref[:, a * 128:(a + 1) * 128]],
                                          axis=0)) for a in range(2)]
        lse_mat = jnp.zeros((BLOCK, 128), F32)
        outs = []
        for g in range(N_KV):
            k_lo, k_hi, v_lo, v_hi = _group_tiles(g, kt, vt)
            q2 = jnp.concatenate([q_ref[:, 2 * g * 128:(2 * g + 1) * 128],
                                  q_ref[:, (2 * g + 1) * 128:(2 * g + 2) * 128]], axis=0)
            acc = jnp.zeros((2 * BLOCK, 128), F32)
            for half, (kh, vh) in enumerate(((k_lo, v_lo), (k_hi, v_hi))):
                h_top, h_bot = 4 * g + half, 4 * g + 2 + half
                s = jnp.where(mask, lax.dot_general(q2, kh, _NT, preferred_element_type=F32), -1e30)
                sink = _two_rows(sink_ref[h_top], sink_ref[h_bot])
                m = jnp.maximum(jnp.max(s, axis=-1, keepdims=True), sink)
                e = jnp.exp(s - m)
                den = jnp.sum(e, axis=-1, keepdims=True) + jnp.exp(sink - m)
                p = e * (1.0 / den)
                acc = acc + jnp.dot(p.astype(BF16), vh, preferred_element_type=F32)
                lse = m + jnp.log(den)
                lse_mat = jnp.where(lane == h_top, lse[:BLOCK], lse_mat)
                lse_mat = jnp.where(lane == h_bot, lse[BLOCK:], lse_mat)
            outs += [acc[:BLOCK], acc[BLOCK:]]
        o = jnp.concatenate(outs, axis=1)
        o_ref[...] = o
        og_ref[...] = o * _silu(z)
        lse_ref[...] = lse_mat

    prev = lambda i: jnp.maximum(i - 1, 0)
    return _call(
        body, "attn_fwd", (nb,),
        [pl.BlockSpec((BLOCK, ATTN_W), lambda i: (i, 0)),
         pl.BlockSpec((BLOCK, KV_W), lambda i: (i, 0)),
         pl.BlockSpec((BLOCK, KV_W), lambda i: (prev(i), 0)),
         pl.BlockSpec((BLOCK, KV_W), lambda i: (i, 5)),
         pl.BlockSpec((BLOCK, KV_W), lambda i: (prev(i), 5)),
         pl.BlockSpec((BLOCK, 512), lambda i: (i, 3)),
         pl.BlockSpec((BLOCK, 512), lambda i: (i, 4)),
         pl.BlockSpec(memory_space=pltpu.SMEM)],
        [pl.BlockSpec((BLOCK, ATTN_W), lambda i: (i, 0)),
         pl.BlockSpec((BLOCK, ATTN_W), lambda i: (i, 0)),
         pl.BlockSpec((BLOCK, 128), lambda i: (i, 0))],
        [jax.ShapeDtypeStruct((L, ATTN_W), F32), jax.ShapeDtypeStruct((L, ATTN_W), F32),
         jax.ShapeDtypeStruct((L, 128), F32)],
        (q, k, k, proj, proj, proj, proj, sinks), ride=ride)


def _attn_bwd(q, k, proj, sinks, d_o, o, lse, ride):
    L = proj.shape[0]
    nb = L // BLOCK

    def body(q_ref, kc_ref, kp_ref, vc_ref, vp_ref, do_ref, o_ref, lse_ref, sink_ref,
             dq_ref, dk_ref, dv_ref, gs_ref, ck_scr, cv_scr):
        i = pl.program_id(0)

        @pl.when(i == 0)
        def _():
            gs_ref[...] = jnp.zeros_like(gs_ref)
            ck_scr[...] = jnp.zeros_like(ck_scr)
            cv_scr[...] = jnp.zeros_like(cv_scr)

        @pl.when(i == nb)
        def _():
            dk_ref[...] = ck_scr[...]
            dv_ref[...] = cv_scr[...]

        @pl.when(i < nb)
        def _():
            mask = _band_mask2(i > 0)
            lane = lax.broadcasted_iota(jnp.int32, (1, 128), 1)
            lo = lax.broadcasted_iota(jnp.int32, (2 * BLOCK, 128), 1) < HEAD_DIM
            lse_c = lse_ref[...]
            kt = [_half_tiles(jnp.concatenate([kp_ref[:, a * 128:(a + 1) * 128], kc_ref[:, a * 128:(a + 1) * 128]],
                                              axis=0).astype(F32)) for a in range(2)]
            vt = [_half_tiles(jnp.concatenate([vp_ref[:, a * 128:(a + 1) * 128], vc_ref[:, a * 128:(a + 1) * 128]],
                                              axis=0)) for a in range(2)]
            gs = jnp.zeros((1, 128), F32)
            dq_parts = []
            dk_acc = [jnp.zeros((2 * BLOCK, 128), F32) for _ in range(2)]
            dv_acc = [jnp.zeros((2 * BLOCK, 128), F32) for _ in range(2)]
            for g in range(N_KV):
                a, b = divmod(g, 2)
                k_lo, k_hi, v_lo, v_hi = _group_tiles(g, kt, vt)
                t0, t1 = slice(2 * g * 128, (2 * g + 1) * 128), slice((2 * g + 1) * 128, (2 * g + 2) * 128)
                q2 = jnp.concatenate([q_ref[:, t0], q_ref[:, t1]], axis=0)
                do2 = jnp.concatenate([do_ref[:, t0], do_ref[:, t1]], axis=0)
                prod = do2 * jnp.concatenate([o_ref[:, t0], o_ref[:, t1]], axis=0)
                do2_b = do2.astype(BF16)
                dq2 = jnp.zeros((2 * BLOCK, 128), F32)
                dk_h, dv_h = [], []
                for half, (kh, vh) in enumerate(((k_lo, v_lo), (k_hi, v_hi))):
                    h_top, h_bot = 4 * g + half, 4 * g + 2 + half
                    lse = jnp.concatenate([_lane_col(lse_c, h_top), _lane_col(lse_c, h_bot)], axis=0)
                    sink = _two_rows(sink_ref[h_top], sink_ref[h_bot])
                    delta = jnp.sum(jnp.where(lo == (half == 0), prod, 0.0), axis=1, keepdims=True)
                    s = jnp.where(mask, lax.dot_general(q2, kh, _NT, preferred_element_type=F32), -1e30)
                    p = jnp.exp(s - lse)
                    dp = lax.dot_general(do2_b, vh, _NT, preferred_element_type=F32)
                    ds_b = (p * (dp - delta)).astype(BF16)
                    p_b = p.astype(BF16)
                    dq2 = dq2 + jnp.dot(ds_b, kh, preferred_element_type=F32)
                    dk_h.append(lax.dot_general(ds_b, q2, _TN, preferred_element_type=F32))
                    dv_h.append(lax.dot_general(p_b, do2_b, _TN, preferred_element_type=F32))
                    gsink = -jnp.exp(sink - lse) * delta
                    row = lax.broadcasted_iota(jnp.int32, (2 * BLOCK, 1), 0)
                    gs = gs + jnp.where(lane == h_top, jnp.sum(jnp.where(row < BLOCK, gsink, 0.0)), 0.0)
                    gs = gs + jnp.where(lane == h_bot, jnp.sum(jnp.where(row >= BLOCK, gsink, 0.0)), 0.0)
                dq_parts += [dq2[:BLOCK], dq2[BLOCK:]]
                for acc, parts in ((dk_acc, dk_h), (dv_acc, dv_h)):
                    t = jnp.where(lo, parts[0], parts[1])
                    t = t + pltpu.roll(t, HEAD_DIM, 1)
                    acc[a] = acc[a] + jnp.where(lo == (b == 0), t, 0.0)
            dq_ref[...] = jnp.concatenate(dq_parts, axis=1)
            dk_full = jnp.concatenate(dk_acc, axis=1)
            dv_full = jnp.concatenate(dv_acc, axis=1)
            dk_ref[...] = ck_scr[...] + dk_full[:BLOCK]
            dv_ref[...] = cv_scr[...] + dv_full[:BLOCK]
            ck_scr[...] = dk_full[BLOCK:]
            cv_scr[...] = dv_full[BLOCK:]
            gs_ref[...] += gs

    cur = lambda i: jnp.minimum(i, nb - 1)
    prev = lambda i: jnp.maximum(jnp.minimum(i, nb - 1) - 1, 0)
    done = lambda i: jnp.maximum(i - 1, 0)
    bs = pl.BlockSpec
    return _call(
        body, "attn_bwd", (nb + 1,),
        [bs((BLOCK, ATTN_W), lambda i: (cur(i), 0)),
         bs((BLOCK, KV_W), lambda i: (cur(i), 0)), bs((BLOCK, KV_W), lambda i: (prev(i), 0)),
         bs((BLOCK, KV_W), lambda i: (cur(i), 5)), bs((BLOCK, KV_W), lambda i: (prev(i), 5)),
         bs((BLOCK, ATTN_W), lambda i: (cur(i), 0)), bs((BLOCK, ATTN_W), lambda i: (cur(i), 0)),
         bs((BLOCK, 128), lambda i: (cur(i), 0)), bs(memory_space=pltpu.SMEM)],
        [bs((BLOCK, ATTN_W), lambda i: (cur(i), 0)),
         bs((BLOCK, KV_W), lambda i: (done(i), 0)), bs((BLOCK, KV_W), lambda i: (done(i), 0)),
         bs((1, 128), lambda i: (0, 0))],
        [jax.ShapeDtypeStruct((L, ATTN_W), F32), jax.ShapeDtypeStruct((L, KV_W), F32),
         jax.ShapeDtypeStruct((L, KV_W), F32), jax.ShapeDtypeStruct((1, 128), F32)],
        (q, k, k, proj, proj, d_o, o, lse, sinks),
        [pltpu.VMEM((BLOCK, KV_W), F32), pltpu.VMEM((BLOCK, KV_W), F32)], ride)


def _qk_prep_bwd(proj, tab, qw, kw, d_q, d_k, d_v, d_za, d_u, d_zs):
    L = proj.shape[0]
    tm = _tile(L, 512)
    z0 = ATTN_W + 2 * KV_W

    def body(q_ref, k_ref, t_ref, qw_ref, kw_ref, dq_ref, dk_ref, dv_ref, dza_ref, du_ref, dzs_ref,
             out_ref, gq_ref, gk_ref):
        i = pl.program_id(0)

        @pl.when(i == 0)
        def _():
            gq_ref[...] = jnp.zeros_like(gq_ref)
            gk_ref[...] = jnp.zeros_like(gk_ref)

        cos, sin = t_ref[:, :128], t_ref[:, 128:]
        gq = jnp.zeros((1, 128), F32)
        gk = jnp.zeros((1, 128), F32)
        for c in range(ATTN_W // 128):
            cs = slice(c * 128, (c + 1) * 128)
            d_raw, gw = _norm_rope_bwd(dq_ref[:, cs] * _SCALE, q_ref[:, cs], qw_ref[...], cos, sin)
            out_ref[:, cs] = d_raw.astype(BF16)
            gq = gq + jnp.sum(gw, axis=0, keepdims=True)
        for c in range(KV_W // 128):
            cs = slice(c * 128, (c + 1) * 128)
            d_raw, gw = _norm_rope_bwd(dk_ref[:, cs], k_ref[:, cs], kw_ref[...], cos, sin)
            out_ref[:, ATTN_W + c * 128:ATTN_W + (c + 1) * 128] = d_raw.astype(BF16)
            gk = gk + jnp.sum(gw, axis=0, keepdims=True)
        out_ref[:, ATTN_W + KV_W:z0] = dv_ref[...].astype(BF16)
        out_ref[:, z0:z0 + ATTN_W] = dza_ref[...]
        out_ref[:, z0 + ATTN_W:z0 + ATTN_W + SSM_W] = du_ref[...].astype(BF16)
        out_ref[:, z0 + ATTN_W + SSM_W:] = dzs_ref[...]
        gq_ref[...] += gq
        gk_ref[...] += gk

    row = pl.BlockSpec((1, 128), lambda i: (0, 0))
    blk = lambda w, c: pl.BlockSpec((tm, w), lambda i: (i, c))
    return pl.pallas_call(
        body,
        name="qk_prep_bwd",
        grid=(L // tm,),
        in_specs=[blk(ATTN_W, 0), blk(KV_W, 4), blk(256, 0), row, row, blk(ATTN_W, 0), blk(KV_W, 0), blk(KV_W, 0),
                  blk(ATTN_W, 0), blk(SSM_W, 0), blk(SSM_W, 0)],
        out_specs=[blk(IN_W, 0), row, row],
        out_shape=[jax.ShapeDtypeStruct((L, IN_W), BF16), jax.ShapeDtypeStruct((1, 128), F32),
                   jax.ShapeDtypeStruct((1, 128), F32)],
        compiler_params=_cp(("arbitrary",)),
    )(proj, proj, tab, jnp.tile(qw, 2).reshape(1, 128), jnp.tile(kw, 2).reshape(1, 128), d_q, d_k, d_v,
      d_za, d_u, d_zs)


def _cmul(a, b):
    return a[0] * b[0] - a[1] * b[1], a[0] * b[1] + a[1] * b[0]


def _cmul_conj(a, b):
    return a[0] * b[0] + a[1] * b[1], a[1] * b[0] - a[0] * b[1]


def _cadd(a, b):
    return a[0] + b[0], a[1] + b[1]


def _dot3(a, b, dn):
    ah, bh = a.astype(BF16), b.astype(BF16)
    al, bl = (a - ah.astype(F32)).astype(BF16), (b - bh.astype(F32)).astype(BF16)
    d = lambda u, v: lax.dot_general(u, v, dn, preferred_element_type=F32)
    return d(ah, bh) + d(ah, bl) + d(al, bh)


def _s5_discretise(a_re, a_im, ls, cosx, sinx, bt):
    delta = jnp.exp(ls)
    er = jnp.exp(a_re * delta)
    lb = (er * cosx, er * sinx)
    den = a_re * a_re + a_im * a_im
    coef = _cmul_conj((lb[0] - 1.0, lb[1]), (a_re, a_im))
    coef = (coef[0] / den, coef[1] / den)
    return delta, lb, coef, den, _cmul(coef, bt)


def _powers(lb):
    pw = [(jnp.ones_like(lb[0]), jnp.zeros_like(lb[0]))]
    for _ in range(CHUNK):
        pw.append(_cmul(pw[-1], lb))
    return pw


def _block_rows(a, pw, idx):
    blocks = [_cmul(a, pw[i]) for i in idx]
    return (jnp.concatenate([b[0] for b in blocks], axis=-2), jnp.concatenate([b[1] for b in blocks], axis=-2))


def _block_rows_bwd(g, a, pw, idx, g_pw):
    g_a = (jnp.zeros_like(a[0]), jnp.zeros_like(a[0]))
    for j, i in enumerate(idx):
        gj = (g[0][..., j * SSM_H:(j + 1) * SSM_H, :], g[1][..., j * SSM_H:(j + 1) * SSM_H, :])
        g_a = _cadd(g_a, _cmul_conj(gj, pw[i]))
        gp = _cmul_conj(gj, a)
        g_pw[i] = _cadd(g_pw[i], (jnp.sum(gp[0], axis=-2, keepdims=True), jnp.sum(gp[1], axis=-2, keepdims=True)))
    return g_a


_IDX_S = [CHUNK - 1 - s for s in range(CHUNK)]
_IDX_C = list(range(CHUNK + 1))


def _prep_args(p):
    row = lambda t: t.reshape(SSM_G, 1, SSM_P)
    xi = p["a_im"] * jnp.exp(p["log_step"])[:, None]
    return (row(p["a_re"]), row(p["a_im"]), row(jnp.broadcast_to(p["log_step"][:, None], (SSM_G, SSM_P))),
            row(jnp.cos(xi)), row(jnp.sin(xi)), p["b_re"].transpose(0, 2, 1), p["b_im"].transpose(0, 2, 1),
            p["c_re"], p["c_im"])


PREP_GROUPS = 8


def _prep_specs():
    r1 = pl.BlockSpec((PREP_GROUPS, 1, SSM_P), lambda g: (g, 0, 0))
    r16 = pl.BlockSpec((PREP_GROUPS, SSM_H, SSM_P), lambda g: (g, 0, 0))
    return [r1] * 5 + [r16] * 4, r1, r16


def _ssm_prep(p):
    def one_group(q, are, aim, ls, cosx, sinx, btr, bti, cre, cim, mt_ref, s_ref, o_ref, a_ref):
        _, lb, _, _, bb = _s5_discretise(are[q], aim[q], ls[q], cosx[q], sinx[q], (btr[q], bti[q]))
        pw = _powers(lb)
        c = (cre[q], cim[q])
        sc = _block_rows(bb, pw, _IDX_S)
        cl = _block_rows(c, pw, _IDX_C)
        ok = (cl[0][:CW], cl[1][:CW])
        ot = (cl[0][SSM_H:], cl[1][SSM_H:])
        s_ref[q] = jnp.concatenate([sc[0], sc[1]], axis=1).astype(BF16)
        o_ref[q] = jnp.concatenate([ot[0], -ot[1]], axis=1).astype(BF16)
        a_ref[q] = jnp.concatenate([pw[CHUNK][0], pw[CHUNK][1]], axis=1)
        kt = _dot3(jnp.concatenate([bb[0], -bb[1]], axis=1), jnp.concatenate([ok[0], ok[1]], axis=1), _NT)
        lane = lax.broadcasted_iota(jnp.int32, kt.shape, 1)
        for s in range(CHUNK):
            blk = kt if s == 0 else jnp.where(lane >= SSM_H * s, pltpu.roll(kt, SSM_H * s, 1), 0.0)
            mt_ref[q, s * SSM_H:(s + 1) * SSM_H, :] = blk.astype(BF16)

    def body(*refs):
        for q in range(PREP_GROUPS):
            one_group(q, *refs)

    in_specs, r1, _ = _prep_specs()
    g3 = lambda r, c: pl.BlockSpec((PREP_GROUPS, r, c), lambda g: (g, 0, 0))
    return pl.pallas_call(
        body,
        name="ssm_prep",
        grid=(SSM_G // PREP_GROUPS,),
        in_specs=in_specs,
        out_specs=[g3(CW, CW), g3(CW, 2 * SSM_P), g3(CW, 2 * SSM_P), g3(1, 2 * SSM_P)],
        out_shape=[jax.ShapeDtypeStruct((SSM_G, CW, CW), BF16), jax.ShapeDtypeStruct((SSM_G, CW, 2 * SSM_P), BF16),
                   jax.ShapeDtypeStruct((SSM_G, CW, 2 * SSM_P), BF16),
                   jax.ShapeDtypeStruct((SSM_G, 1, 2 * SSM_P), F32)],
        compiler_params=_cp(("parallel",)),
    )(*_prep_args(p))


def _ssm_prep_bwd(p, g_mt, g_scat, g_ocat, g_a16, ride):
    def body(are, aim, ls, cosx, sinx, btr, bti, cre, cim, gmt_ref, gs_ref, go_ref, ga_ref,
             g_are, g_aim, g_ls, g_btr, g_bti, g_cre, g_cim, ga1_scr, gb1_scr):
        lam = (are[...], aim[...])
        bt = (btr[...], bti[...])
        delta, lb, coef, den, bb = _s5_discretise(lam[0], lam[1], ls[...], cosx[...], sinx[...], bt)
        pw = _powers(lb)
        c = (cre[...], cim[...])
        ok = _block_rows(c, pw, _IDX_C[:CHUNK])
        g_pw =[(jnp.zeros_like(lb[0]), jnp.zeros_like(lb[0])) for _ in range(CHUNK + 1)]
        lane = lax.broadcasted_iota(jnp.int32, (SSM_H, CW), 1)
        for q in range(PREP_GROUPS):
            g_kt = gmt_ref[q, :SSM_H, :]
            for s in range(1, CHUNK):
                blk = gmt_ref[q, s * SSM_H:(s + 1) * SSM_H, :]
                g_kt = g_kt + jnp.where(lane < CW - SSM_H * s, pltpu.roll(blk, CW - SSM_H * s, 1), 0.0)
            a1 = jnp.concatenate([bb[0][q], -bb[1][q]], axis=1)
            b1 = jnp.concatenate([ok[0][q], ok[1][q]], axis=1)
            ga1_scr[q] = _dot3(g_kt, b1, _NN)
            gb1_scr[q] = _dot3(g_kt, a1, _TN)
        g_a1, g_b1 = ga1_scr[...], gb1_scr[...]
        g_bb = (g_a1[..., :SSM_P], -g_a1[..., SSM_P:])
        gs = gs_ref[...]
        g_bb = _cadd(g_bb, _block_rows_bwd((gs[..., :SSM_P], gs[..., SSM_P:]), bb, pw, _IDX_S, g_pw))
        go = go_ref[...]
        pad = jnp.zeros_like(go[..., :SSM_H, :SSM_P])
        g_cl = (jnp.concatenate([g_b1[..., :SSM_P], pad], axis=-2) + jnp.concatenate([pad, go[..., :SSM_P]], axis=-2),
                jnp.concatenate([g_b1[..., SSM_P:], pad], axis=-2) - jnp.concatenate([pad, go[..., SSM_P:]], axis=-2))
        g_c = _block_rows_bwd(g_cl, c, pw, _IDX_C, g_pw)
        ga = ga_ref[...]
        g_pw[CHUNK] = _cadd(g_pw[CHUNK], (ga[..., :SSM_P], ga[..., SSM_P:]))
        g_lb = (jnp.zeros_like(lb[0]), jnp.zeros_like(lb[0]))
        for l in range(CHUNK - 1, -1, -1):
            g_lb = _cadd(g_lb, _cmul_conj(g_pw[l + 1], pw[l]))
            g_pw[l] = _cadd(g_pw[l], _cmul_conj(g_pw[l + 1], lb))
        g_bt = _cmul_conj(g_bb, coef)
        gc = _cmul_conj(g_bb, bt)
        g_coef = (jnp.sum(gc[0], axis=-2, keepdims=True), jnp.sum(gc[1], axis=-2, keepdims=True))
        lam_den = (lam[0] / den, lam[1] / den)
        g_lb = _cadd(g_lb, _cmul(g_coef, lam_den))
        t = _cmul(_cmul_conj(g_coef, coef), lam_den)
        g_x = _cmul_conj(g_lb, lb)
        g_are[...] = g_x[0] * delta - t[0]
        g_aim[...] = g_x[1] * delta - t[1]
        g_ls[...] = (g_x[0] * lam[0] + g_x[1] * lam[1]) * delta
        g_btr[...] = g_bt[0]
        g_bti[...] = g_bt[1]
        g_cre[...] = g_c[0]
        g_cim[...] = g_c[1]

    in_specs, r1, r16 = _prep_specs()
    g3 = lambda r, c: pl.BlockSpec((PREP_GROUPS, r, c), lambda g: (g, 0, 0))
    rows = jax.ShapeDtypeStruct((SSM_G, 1, SSM_P), F32)
    mats = jax.ShapeDtypeStruct((SSM_G, SSM_H, SSM_P), F32)
    (g_are, g_aim, g_ls, g_btr, g_bti, g_cre, g_cim), landed = _call(
        body, "ssm_prep_bwd", (SSM_G // PREP_GROUPS,),
        in_specs + [g3(CW, CW), g3(CW, 2 * SSM_P), g3(CW, 2 * SSM_P), g3(1, 2 * SSM_P)],
        [r1] * 3 + [r16] * 4, [rows] * 3 + [mats] * 4, (*_prep_args(p), g_mt, g_scat, g_ocat, g_a16),
        [pltpu.VMEM((PREP_GROUPS, SSM_H, 2 * SSM_P), F32), pltpu.VMEM((PREP_GROUPS, CW, 2 * SSM_P), F32)], ride)
    grads = dict(a_re=g_are.reshape(SSM_G, SSM_P), a_im=g_aim.reshape(SSM_G, SSM_P),
                 log_step=jnp.sum(g_ls.reshape(SSM_G, SSM_P), axis=1),
                 b_re=g_btr.transpose(0, 2, 1), b_im=g_bti.transpose(0, 2, 1), c_re=g_cre, c_im=g_cim)
    return grads, landed


def _cmul_const(xv, ar, ai):
    return xv * ar + pltpu.roll(xv, SSM_P, 1) * ai


def _chunk_scan(inc, a_row, reverse):
    n = inc.shape[0]
    lane = lax.broadcasted_iota(jnp.int32, (1, 2 * SSM_P), 1)
    row = lax.broadcasted_iota(jnp.int32, inc.shape, 0)
    sign = jnp.where(lane < SSM_P, -1.0, 1.0)
    ar = jnp.where(lane < SSM_P, a_row, pltpu.roll(a_row, SSM_P, 1))
    ai = jnp.where(lane < SSM_P, pltpu.roll(a_row, SSM_P, 1), a_row)
    if reverse:
        ai = -ai
    xv = inc
    s = 1
    while s < n:
        if reverse:
            sh = jnp.where(row < n - s, pltpu.roll(xv, n - s, 0), 0.0)
        else:
            sh = jnp.where(row >= s, pltpu.roll(xv, s, 0), 0.0)
        xv = xv + _cmul_const(sh, ar, ai * sign)
        ar, ai = ar * ar - ai * ai, 2.0 * ar * ai
        s *= 2
    return xv


def _shift_rows(xv, reverse):
    n = xv.shape[0]
    row = lax.broadcasted_iota(jnp.int32, xv.shape, 0)
    if reverse:
        return jnp.where(row < n - 1, pltpu.roll(xv, n - 1, 0), 0.0)
    return jnp.where(row >= 1, pltpu.roll(xv, 1, 0), 0.0)


GB = 128 // SSM_H
U_COL0 = (ATTN_W + 2 * KV_W + ATTN_W) // 128


HALF = CHUNK // 2


def _chunk_perm():
    r = jnp.arange(HALF * 128)
    t, g8, h = r // 128, (r % 128) // SSM_H, r % SSM_H
    return ((g8 * 128 + t * SSM_H + h)[:, None] == jnp.arange(GB * 128)[None, :]).astype(BF16)


def _load_perm(p_hbm, p_scr, sem):
    @pl.when(pl.program_id(0) == 0)
    def _():
        cp = pltpu.make_async_copy(p_hbm, p_scr, sem)
        cp.start()
        cp.wait()


def _rows_to_chunks(pieces, perm):
    halves = [jnp.dot(jnp.concatenate(pieces[k * HALF:(k + 1) * HALF], axis=1).astype(BF16), perm,
                      preferred_element_type=F32).astype(BF16) for k in range(2)]
    return [jnp.concatenate([hv[:, g * 128:(g + 1) * 128] for hv in halves], axis=1) for g in range(GB)]


def _chunks_to_rows(groups, perm, two_pass):
    pieces = []
    for k in range(2):
        v = jnp.concatenate([gv[:, k * 128:(k + 1) * 128] for gv in groups], axis=1)
        hi = v.astype(BF16)
        out = lax.dot_general(hi, perm, _NT, preferred_element_type=F32)
        if two_pass:
            lo = (v - hi.astype(F32)).astype(BF16)
            out = out + lax.dot_general(lo, perm, _NT, preferred_element_type=F32)
        pieces += [out[:, t * 128:(t + 1) * 128] for t in range(HALF)]
    return pieces


def _ssm_fwd(proj, perm, mt, scat, ocat, a16, d_skip, ride):
    L = proj.shape[0]
    nc = L // CHUNK

    def body(u_ref, p_hbm, mt_ref, s_ref, o_ref, a_ref, d_ref, y_ref, yg_ref, h_ref, p_scr, sem):
        _load_perm(p_hbm, p_scr, sem)
        perm = p_scr[...]
        rows = [pl.ds(t, nc, stride=CHUNK) for t in range(CHUNK)]
        us = [u_ref[r, :] for r in rows]
        ua = _rows_to_chunks(us, perm)
        ys = []
        for g in range(GB):
            uv = ua[g]
            inc = jnp.dot(uv, s_ref[g], preferred_element_type=F32)
            hx = _shift_rows(_chunk_scan(inc, a_ref[g], False), False)
            h_ref[g] = hx
            ys.append(jnp.dot(uv, mt_ref[g], preferred_element_type=F32)
                      + lax.dot_general(hx.astype(BF16), o_ref[g], _NT, preferred_element_type=F32))
        yp = _chunks_to_rows(ys, perm, True)
        for t, r in enumerate(rows):
            y = yp[t] + d_ref[...] * us[t]
            y_ref[r, :] = y
            yg_ref[r, :] = _gelu(y)

    g3 = lambda r, c: pl.BlockSpec((GB, r, c), lambda g: (g, 0, 0))
    col = pl.BlockSpec((L, 128), lambda g: (0, g))
    return _call(
        body, "ssm_fwd", (SSM_G // GB,),
        [pl.BlockSpec((L, 128), lambda g: (0, U_COL0 + g)), _ANY,
         g3(CW, CW), g3(CW, 2 * SSM_P), g3(CW, 2 * SSM_P), g3(1, 2 * SSM_P),
         pl.BlockSpec((1, 128), lambda g: (0, g))],
        [col, col, g3(nc, 2 * SSM_P)],
        [jax.ShapeDtypeStruct((L, SSM_W), F32), jax.ShapeDtypeStruct((L, SSM_W), F32),
         jax.ShapeDtypeStruct((SSM_G, nc, 2 * SSM_P), F32)],
        (proj, perm, mt, scat, ocat, a16, d_skip.reshape(1, SSM_W)),
        [pltpu.VMEM((HALF * 128, GB * 128), BF16), pltpu.SemaphoreType.DMA], ride)


def _ssm_bwd(d_yg, y, proj, hx, perm, mt, scat, ocat, a16, d_skip, ride):
    L = proj.shape[0]
    nc = L // CHUNK

    def body(dg_ref, y_ref, u_ref, h_ref, p_hbm, mt_ref, s_ref, o_ref, a_ref, d_ref,
             du_ref, gmt_ref, gs_ref, go_ref, ga_ref, gd_ref, p_scr, sem):
        _load_perm(p_hbm, p_scr, sem)
        perm = p_scr[...]
        rows = [pl.ds(t, nc, stride=CHUNK) for t in range(CHUNK)]
        us = [u_ref[r, :] for r in rows]
        dys = [dg_ref[r, :] * _dgelu(y_ref[r, :]) for r in rows]
        gd = jnp.zeros((1, 128), F32)
        for uv, dy in zip(us, dys):
            gd = gd + jnp.sum(dy * uv, axis=0, keepdims=True)
        gd_ref[...] = gd
        ua = _rows_to_chunks(us, perm)
        dya = _rows_to_chunks(dys, perm)
        lane = lax.broadcasted_iota(jnp.int32, (1, 2 * SSM_P), 1)
        dus = []
        for g in range(GB):
            uv, dy, hx_v = ua[g], dya[g], h_ref[g]
            dh = jnp.dot(dy, o_ref[g], preferred_element_type=F32)
            dinc = _shift_rows(_chunk_scan(dh, a_ref[g], True), True)
            dinc_b = dinc.astype(BF16)
            dus.append(lax.dot_general(dy, mt_ref[g], _NT, preferred_element_type=F32)
                       + lax.dot_general(dinc_b, s_ref[g], _NT, preferred_element_type=F32))
            gmt_ref[g] = lax.dot_general(uv, dy, _TN, preferred_element_type=F32)
            gs_ref[g] = lax.dot_general(uv, dinc_b, _TN, preferred_element_type=F32)
            go_ref[g] = lax.dot_general(dy, hx_v.astype(BF16), _TN, preferred_element_type=F32)
            p1 = dinc * hx_v
            p2 = pltpu.roll(dinc, SSM_P, 1) * hx_v
            t1 = jnp.sum(p1 + pltpu.roll(p1, SSM_P, 1), axis=0, keepdims=True)
            t2 = jnp.sum(p2 - pltpu.roll(p2, SSM_P, 1), axis=0, keepdims=True)
            ga_ref[g] = jnp.where(lane < SSM_P, t1, pltpu.roll(t2, SSM_P, 1))
        dup = _chunks_to_rows(dus, perm, False)
        for t, r in enumerate(rows):
            du_ref[r, :] = dup[t] + d_ref[...] * dys[t]

    g3 = lambda r, c: pl.BlockSpec((GB, r, c), lambda g: (g, 0, 0))
    col = pl.BlockSpec((L, 128), lambda g: (0, g))
    row = pl.BlockSpec((1, 128), lambda g: (0, g))
    return _call(
        body, "ssm_bwd", (SSM_G // GB,),
        [col, col, pl.BlockSpec((L, 128), lambda g: (0, U_COL0 + g)), g3(nc, 2 * SSM_P), _ANY,
         g3(CW, CW), g3(CW, 2 * SSM_P), g3(CW, 2 * SSM_P), g3(1, 2 * SSM_P), row],
        [col, g3(CW, CW), g3(CW, 2 * SSM_P), g3(CW, 2 * SSM_P), g3(1, 2 * SSM_P), row],
        [jax.ShapeDtypeStruct((L, SSM_W), F32), jax.ShapeDtypeStruct((SSM_G, CW, CW), F32),
         jax.ShapeDtypeStruct((SSM_G, CW, 2 * SSM_P), F32), jax.ShapeDtypeStruct((SSM_G, CW, 2 * SSM_P), F32),
         jax.ShapeDtypeStruct((SSM_G, 1, 2 * SSM_P), F32), jax.ShapeDtypeStruct((1, SSM_W), F32)],
        (d_yg, y, proj, hx, perm, mt, scat, ocat, a16, d_skip.reshape(1, SSM_W)),
        [pltpu.VMEM((HALF * 128, GB * 128), BF16), pltpu.SemaphoreType.DMA], ride)


def _merge(og, yg, w_glu, proj, b_glu, wa, ws):
    L = og.shape[0]
    tm = _tile(L, 256)

    def body(og_ref, yg_ref, wg_ref, z0_ref, z1_ref, b_ref, wa_ref, ws_ref, m_ref, gp_ref):
        zs = jnp.concatenate([z0_ref[...], z1_ref[...]], axis=1)
        ygv = yg_ref[...]
        gpre = jnp.dot(ygv.astype(BF16), wg_ref[...], preferred_element_type=F32)
        gp_ref[...] = gpre
        os_ = ygv * _sigmoid(gpre + b_ref[...]) * _silu(zs)
        ogv = og_ref[...]
        ra = lax.rsqrt(jnp.mean(ogv * ogv, axis=-1, keepdims=True) + NORM_EPS)
        rs = lax.rsqrt(jnp.mean(os_ * os_, axis=-1, keepdims=True) + NORM_EPS)
        m_ref[:, :ATTN_W] = (ogv * ra * wa_ref[...]).astype(BF16)
        m_ref[:, ATTN_W:] = (os_ * rs * ws_ref[...]).astype(BF16)

    row = lambda w: pl.BlockSpec((1, w), lambda i: (0, 0))
    return pl.pallas_call(
        body,
        name="merge",
        grid=(L // tm,),
        in_specs=[pl.BlockSpec((tm, ATTN_W), lambda i: (i, 0)), pl.BlockSpec((tm, SSM_W), lambda i: (i, 0)),
                  pl.BlockSpec((SSM_W, SSM_W), lambda i: (0, 0)),
                  pl.BlockSpec((tm, 512), lambda i: (i, 7)), pl.BlockSpec((tm, 512), lambda i: (i, 8)),
                  row(SSM_W), row(ATTN_W), row(SSM_W)],
        out_specs=[pl.BlockSpec((tm, D_MODEL), lambda i: (i, 0)), pl.BlockSpec((tm, SSM_W), lambda i: (i, 0))],
        out_shape=[jax.ShapeDtypeStruct((L, D_MODEL), BF16), jax.ShapeDtypeStruct((L, SSM_W), F32)],
        compiler_params=_cp(("parallel",)),
    )(og, yg, w_glu, proj, proj, b_glu.reshape(1, SSM_W), wa.reshape(1, ATTN_W), ws.reshape(1, SSM_W))


def _outproj_loss(merged, w_out, x, target):
    L = x.shape[0]
    tm, tn = _tile(L, 512), 1024
    ni, nj = L // tm, D_MODEL // tn

    def body(m_ref, w_ref, x_ref, t_ref, d_ref, db_ref, l_ref):
        out = x_ref[...] + jnp.dot(m_ref[...], w_ref[...], preferred_element_type=F32)
        diff = out - t_ref[...]
        d = diff * (1.0 / D_MODEL)
        d_ref[...] = d
        db_ref[...] = d.astype(BF16)
        l_ref[...] = jnp.full((1, 8, 128), jnp.sum(diff * diff), F32)

    return pl.pallas_call(
        body,
        name="outproj_loss",
        grid=(nj, ni),
        in_specs=[pl.BlockSpec((tm, D_MODEL), lambda j, i: (i, 0)),
                  pl.BlockSpec((D_MODEL, tn), lambda j, i: (0, j)),
                  pl.BlockSpec((tm, tn), lambda j, i: (i, j)),
                  pl.BlockSpec((tm, tn), lambda j, i: (i, j))],
        out_specs=[pl.BlockSpec((tm, tn), lambda j, i: (i, j)), pl.BlockSpec((tm, tn), lambda j, i: (i, j)),
                   pl.BlockSpec((1, 8, 128), lambda j, i: (i * nj + j, 0, 0))],
        out_shape=[jax.ShapeDtypeStruct((L, D_MODEL), F32), jax.ShapeDtypeStruct((L, D_MODEL), BF16),
                   jax.ShapeDtypeStruct((ni * nj, 8, 128), F32)],
        compiler_params=_cp(("parallel", "parallel")),
    )(merged, w_out, x, target)


def _merge_bwd(d_out_b, w_out, og, o, yg, gpre, proj, b_glu, wa, ws):
    L = og.shape[0]
    tm = _tile(L, 256)

    def body(dout_ref, wo_ref, og_ref, o_ref, yg_ref, gp_ref, za0_ref, za1_ref, zs0_ref, zs1_ref, b_ref, wa_ref,
             ws_ref, do_ref, dza_ref, dzs_ref, dg_ref, dyg_ref, gwa_ref, gws_ref, gb_ref):
        i = pl.program_id(0)

        @pl.when(i == 0)
        def _():
            gwa_ref[...] = jnp.zeros_like(gwa_ref)
            gws_ref[...] = jnp.zeros_like(gws_ref)
            gb_ref[...] = jnp.zeros_like(gb_ref)

        dm = lax.dot_general(dout_ref[...], wo_ref[...], _NT, preferred_element_type=F32)
        za = jnp.concatenate([za0_ref[...], za1_ref[...]], axis=1)
        zs = jnp.concatenate([zs0_ref[...], zs1_ref[...]], axis=1)
        ogv, dma = og_ref[...], dm[:, :ATTN_W]
        ra = lax.rsqrt(jnp.mean(ogv * ogv, axis=-1, keepdims=True) + NORM_EPS)
        xh = ogv * ra
        gwa_ref[...] += jnp.sum(dma * xh, axis=0, keepdims=True)
        gx = dma * wa_ref[...]
        d_og = ra * (gx - xh * jnp.mean(gx * xh, axis=-1, keepdims=True))
        do_ref[...] = d_og * _silu(za)
        dza_ref[...] = (d_og * o_ref[...] * _dsilu(za)).astype(BF16)
        ygv = yg_ref[...]
        sg = _sigmoid(gp_ref[...] + b_ref[...])
        y2 = ygv * sg
        sz = _silu(zs)
        os_ = y2 * sz
        dms = dm[:, ATTN_W:]
        rs = lax.rsqrt(jnp.mean(os_ * os_, axis=-1, keepdims=True) + NORM_EPS)
        xs = os_ * rs
        gws_ref[...] += jnp.sum(dms * xs, axis=0, keepdims=True)
        gxs = dms * ws_ref[...]
        d_os = rs * (gxs - xs * jnp.mean(gxs * xs, axis=-1, keepdims=True))
        dzs_ref[...] = (d_os * y2 * _dsilu(zs)).astype(BF16)
        d_y2 = d_os * sz
        d_g = d_y2 * ygv * sg * (1.0 - sg)
        dg_ref[...] = d_g.astype(BF16)
        gb_ref[...] += jnp.sum(d_g, axis=0, keepdims=True)
        dyg_ref[...] = d_y2 * sg

    row = lambda w: pl.BlockSpec((1, w), lambda i: (0, 0))
    full = lambda w: pl.BlockSpec((tm, w), lambda i: (i, 0))
    half = lambda c: pl.BlockSpec((tm, 512), lambda i: (i, c))
    return pl.pallas_call(
        body,
        name="merge_bwd",
        grid=(L // tm,),
        in_specs=[full(D_MODEL), pl.BlockSpec((D_MODEL, D_MODEL), lambda i: (0, 0)),
                  full(ATTN_W), full(ATTN_W), full(SSM_W), full(SSM_W),
                  half(3), half(4), half(7), half(8), row(SSM_W), row(ATTN_W), row(SSM_W)],
        out_specs=[full(ATTN_W), full(ATTN_W), full(SSM_W), full(SSM_W), full(SSM_W),
                   row(ATTN_W), row(SSM_W), row(SSM_W)],
        out_shape=[jax.ShapeDtypeStruct((L, ATTN_W), F32), jax.ShapeDtypeStruct((L, ATTN_W), BF16),
                   jax.ShapeDtypeStruct((L, SSM_W), BF16), jax.ShapeDtypeStruct((L, SSM_W), BF16),
                   jax.ShapeDtypeStruct((L, SSM_W), F32),
                   jax.ShapeDtypeStruct((1, ATTN_W), F32), jax.ShapeDtypeStruct((1, SSM_W), F32),
                   jax.ShapeDtypeStruct((1, SSM_W), F32)],
        compiler_params=_cp(("arbitrary",)),
    )(d_out_b, w_out, og, o, yg, gpre, proj, proj, proj, proj, b_glu.reshape(1, SSM_W), wa.reshape(1, ATTN_W),
      ws.reshape(1, SSM_W))


def _rms_bwd_x(x, norm_w, d_hn, d_out, ride):
    L = x.shape[0]
    tm = _tile(L, 256)

    def body(x_ref, w_ref, dh_ref, do_ref, gx_ref, gw_ref):
        i = pl.program_id(0)

        @pl.when(i == 0)
        def _():
            gw_ref[...] = jnp.zeros_like(gw_ref)

        xv, dh = x_ref[...], dh_ref[...]
        r = lax.rsqrt(jnp.mean(xv * xv, axis=-1, keepdims=True) + NORM_EPS)
        xh = xv * r
        gw_ref[...] += jnp.sum(dh * xh, axis=0, keepdims=True)
        gx = dh * w_ref[...]
        gx_ref[...] = do_ref[...] + r * (gx - xh * jnp.mean(gx * xh, axis=-1, keepdims=True))

    blk = pl.BlockSpec((tm, D_MODEL), lambda i: (i, 0))
    row = pl.BlockSpec((1, D_MODEL), lambda i: (0, 0))
    return _call(body, "rms_bwd_x", (L // tm,), [blk, row, blk, blk], [blk, row],
                 [jax.ShapeDtypeStruct((L, D_MODEL), F32), jax.ShapeDtypeStruct((1, D_MODEL), F32)],
                 (x, norm_w.reshape(1, D_MODEL), d_hn, d_out), ride=ride)


def _rope_table(positions):
    lane = jnp.arange(256)
    inv_freq = ROPE_THETA ** (-(2 * (lane % (HEAD_DIM // 2))).astype(F32) / HEAD_DIM)
    ang = positions.astype(F32)[:, None] * inv_freq[None, :]
    sign = jnp.where(lane % HEAD_DIM < HEAD_DIM // 2, -1.0, 1.0)
    return jnp.where(lane < 128, jnp.cos(ang), sign * jnp.sin(ang))


def _step(x, positions, target, w, core, chip):
    small = {n: w[n] for n in _SMALL}
    tab = _rope_table(positions)
    mt_b, scat_b, ocat_b, a16 = _ssm_prep(small)
    perm = _chunk_perm()
    blocks = lambda t: t.reshape(N_DEV, t.shape[0] // N_DEV, t.shape[1])

    proj, hn, wt_in = _rms_inproj_gather(x, small["norm_w"], w["w_in"].T.astype(BF16), chip)
    wt_in = wt_in.reshape(IN_W, D_MODEL)
    q_rot, k_rot = _qk_prep(proj, tab, small["q_norm_w"], small["k_norm_w"])
    (og, o, lse), (w_glu,) = _attn_fwd(q_rot, k_rot, proj, small["sinks"],
                                       _gather_exchange([w["w_glu"].astype(BF16)]))
    (y, yg, hx), (w_out,) = _ssm_fwd(proj, perm, mt_b, scat_b, ocat_b, a16, small["d_skip"],
                                     _gather_exchange([w["w_out"].astype(BF16)]))
    w_glu, w_out = w_glu.reshape(SSM_W, SSM_W), w_out.reshape(D_MODEL, D_MODEL)
    merged, gpre = _merge(og, yg, w_glu, proj, small["b_glu"], small["attn_out_norm_w"], small["ssm_out_norm_w"])
    d_out, d_out_b, loss_parts = _outproj_loss(merged, w_out, x, target)
    loss = 0.5 * jnp.sum(loss_parts[:, 0, 0]) / D_MODEL

    g_w_out = blocks(_mm(merged, d_out_b, "tn", F32, "grad_w_out", tm=1024))
    d_o, d_za, d_zs, d_g, d_yg1, g_wa, g_ws, g_bglu = _merge_bwd(
        d_out_b, w_out, og, o, yg, gpre, proj, small["b_glu"], small["attn_out_norm_w"], small["ssm_out_norm_w"])
    g_w_glu = blocks(_mm(yg, d_g, "tn", F32, "grad_w_glu"))
    d_yg = _mm(d_g, w_glu, "nt", F32, "d_yg", add=d_yg1)
    (d_u, g_mt, g_scat, g_ocat, g_a16, g_dskip), (ra_out, ra_glu) = _ssm_bwd(
        d_yg, y, proj, hx, perm, mt_b, scat_b, ocat_b, a16, small["d_skip"], _pair_exchange([g_w_out, g_w_glu]))
    p_out = _pair_sum(g_w_out, ra_out, core, BF16, "pair_sum_out")
    p_glu = _pair_sum(g_w_glu, ra_glu, core, BF16, "pair_sum_glu")
    (d_q, d_k, d_v, g_sinks), (rb_out, rb_glu) = _attn_bwd(
        q_rot, k_rot, proj, small["sinks"], d_o, o, lse, _chip_exchange([p_out, p_glu]))
    d_proj, g_qw, g_kw = _qk_prep_bwd(proj, tab, small["q_norm_w"], small["k_norm_w"], d_q, d_k, d_v,
                                      d_za, d_u, d_zs)
    g_qw = g_qw[0, :HEAD_DIM] + g_qw[0, HEAD_DIM:]
    g_kw = g_kw[0, :HEAD_DIM] + g_kw[0, HEAD_DIM:]
    g_in_a = blocks(_mm(d_proj, hn, "tn", F32, "grad_w_in_a", tm=1152, panel=0))
    g_in_b, (ra_a,) = _mm(d_proj, hn, "tn", F32, "grad_w_in_b", tm=1152, panel=1, ride=_pair_exchange([g_in_a]))
    g_in_b = blocks(g_in_b)
    p_a = _pair_sum(g_in_a, ra_a, core, BF16, "pair_sum_in_a")
    d_hn, (rb_a, ra_b) = _mm(d_proj, wt_in, "nn", F32, "d_hn", tm=1024,
                             ride=_both(_chip_exchange([p_a]), _pair_exchange([g_in_b])))
    p_b = _pair_sum(g_in_b, ra_b, core, BF16, "pair_sum_in_b")
    g_small, (rb_b,) = _ssm_prep_bwd(small, g_mt, g_scat, g_ocat, g_a16, _chip_exchange([p_b]))
    (grad_x, g_nw), _ = _rms_bwd_x(x, small["norm_w"], d_hn, d_out, None)

    g_small.update(norm_w=g_nw.reshape(-1), q_norm_w=g_qw.reshape(-1), k_norm_w=g_kw.reshape(-1),
                   sinks=g_sinks[0, :N_HEADS], d_skip=g_dskip.reshape(-1), b_glu=g_bglu.reshape(-1),
                   attn_out_norm_w=g_wa.reshape(-1), ssm_out_norm_w=g_ws.reshape(-1))
    g_packed = _slab_all_reduce(_pack(g_small, loss).reshape(N_DEV, _PACK_ROWS // N_DEV, 128))
    g_packed = g_packed.reshape(_PACK_ROWS, 128)
    grads = _unpack(g_packed, w)
    parts = dict(w_in=([p_a, p_b], [rb_a, rb_b]), w_glu=([p_glu], [rb_glu]), w_out=([p_out], [rb_out]))
    return g_packed[_LOSS_ROW, 0], grad_x, grads, parts


_ANY = pl.BlockSpec(memory_space=pl.ANY)


class _Exchange:
    def __init__(self, arrays, out_shape, sems, start, finish, relay=None):
        self.arrays, self.out_shape, self.sems, self.start, self.finish = arrays, out_shape, sems, start, finish
        self.relay = relay if relay is not None else (lambda ins, outs, sems: None)


def _gather_exchange(blocks):
    n = len(blocks)

    def parts(ins, outs, sems):
        send_sems, recv_sems, local_sems = sems
        x, y, c = lax.axis_index("x"), lax.axis_index("y"), lax.axis_index("c")
        me, sibling = (x, y, c), (x, y, 1 - c)
        chips = [(1 - x, y), (x, 1 - y), (1 - x, 1 - y)]

        def slot(k, dev):
            return outs[k].at[4 * dev[0] + 2 * dev[1] + dev[2]]

        def copy(k, q, block, to, src=None):
            return pltpu.make_async_remote_copy(
                src_ref=slot(k, block) if src is None else src, dst_ref=slot(k, block),
                send_sem=send_sems.at[k, q], recv_sem=recv_sems.at[k, q], device_id=to, device_id_type=MESH)

        mine = [pltpu.make_async_copy(ins[k], slot(k, me), local_sems.at[k]) for k in range(n)]
        first = []
        for k in range(n):
            first.append(copy(k, 0, me, sibling, src=ins[k]))
            first += [copy(k, 1 + j, me, (*chip, c), src=ins[k]) for j, chip in enumerate(chips)]
        return me, sibling, chips, c, copy, mine, first

    def start(ins, outs, sems):
        *_, mine, first = parts(ins, outs, sems)
        for cp in mine + first:
            cp.start()

    def relay(ins, outs, sems):
        me, sibling, chips, c, copy, _, _ = parts(ins, outs, sems)
        for j, chip in enumerate(chips):
            for k in range(n):
                copy(k, 1 + j, (*chip, c), me).wait_recv()
                copy(k, 4 + j, (*chip, c), sibling).start()

    def finish(ins, outs, sems):
        me, sibling, chips, c, copy, mine, first = parts(ins, outs, sems)
        for k in range(n):
            copy(k, 0, sibling, me).wait_recv()
            for j, chip in enumerate(chips):
                copy(k, 4 + j, (*chip, 1 - c), me).wait_recv()
        for cp in first + [copy(k, 4 + j, (*chip, c), sibling) for k in range(n) for j, chip in enumerate(chips)]:
            cp.wait_send()
        for cp in mine:
            cp.wait()

    return _Exchange(blocks, [jax.ShapeDtypeStruct((N_DEV,) + b.shape, b.dtype) for b in blocks],
                     [pltpu.SemaphoreType.DMA((n, 7)), pltpu.SemaphoreType.DMA((n, 7)), pltpu.SemaphoreType.DMA((n,))],
                     start, finish, relay)


def _direct_exchange(arrays, out_lead, fan, route):
    n = len(arrays)

    def copies(ins, outs, sems):
        send_sems, recv_sems = sems
        legs = route(lax.axis_index("x"), lax.axis_index("y"), lax.axis_index("c"))
        return [pltpu.make_async_remote_copy(
            src_ref=ins[k].at[src], dst_ref=outs[k].at[q], send_sem=send_sems.at[k, q], recv_sem=recv_sems.at[k, q],
            device_id=to, device_id_type=MESH) for k in range(n) for src, q, to in legs]

    def start(ins, outs, sems):
        for cp in copies(ins, outs, sems):
            cp.start()

    def finish(ins, outs, sems):
        for cp in copies(ins, outs, sems):
            cp.wait()

    return _Exchange(arrays, [jax.ShapeDtypeStruct((out_lead,) + a.shape[1:], a.dtype) for a in arrays],
                     [pltpu.SemaphoreType.DMA((n, fan)), pltpu.SemaphoreType.DMA((n, fan))], start, finish)


def _pair_exchange(grads):
    return _direct_exchange(grads, 4, 4, lambda x, y, c: [(2 * chip + (1 - c), chip, (x, y, 1 - c))
                                                          for chip in range(4)])


def _chip_exchange(parts):
    def route(x, y, c):
        chips = [(1 - x, y), (x, 1 - y), (1 - x, 1 - y)]
        return [(2 * chip[0] + chip[1], q, (*chip, c)) for q, chip in enumerate(chips)]
    return _direct_exchange(parts, 3, 3, route)


def _both(ex1, ex2):
    n1, s1 = len(ex1.arrays), len(ex1.sems)

    def halves(ins, outs, sems):
        return (ins[:n1], outs[:n1], sems[:s1]), (ins[n1:], outs[n1:], sems[s1:])

    def start(ins, outs, sems):
        h1, h2 = halves(ins, outs, sems)
        ex1.start(*h1)
        ex2.start(*h2)

    def relay(ins, outs, sems):
        h1, h2 = halves(ins, outs, sems)
        ex1.relay(*h1)
        ex2.relay(*h2)

    def finish(ins, outs, sems):
        h1, h2 = halves(ins, outs, sems)
        ex1.finish(*h1)
        ex2.finish(*h2)

    return _Exchange(list(ex1.arrays) + list(ex2.arrays), list(ex1.out_shape) + list(ex2.out_shape),
                     list(ex1.sems) + list(ex2.sems), start, finish, relay)


def _call(body, name, grid, in_specs, out_specs, out_shape, args, scratch_shapes=(), ride=None):
    if ride is None:
        sem = ("arbitrary",) * len(grid)
        return pl.pallas_call(body, name=name, grid=grid, in_specs=in_specs, out_specs=out_specs, out_shape=out_shape,
                              scratch_shapes=list(scratch_shapes), compiler_params=_cp(sem))(*args), None
    n_in, n_out, n_scr, n_x = len(in_specs), len(out_specs), len(scratch_shapes), len(ride.arrays)

    def wrapped(*refs):
        ins, refs = refs[:n_in], refs[n_in:]
        x_in, refs = refs[:n_x], refs[n_x:]
        outs, refs = refs[:n_out], refs[n_out:]
        x_out, refs = refs[:n_x], refs[n_x:]
        scr, sems = refs[:n_scr], refs[n_scr:]
        step, total = pl.program_id(0), grid[0]
        for a in range(1, len(grid)):
            step, total = step * grid[a] + pl.program_id(a), total * grid[a]
        @pl.when(step == 0)
        def _():
            ride.start(x_in, x_out, sems)

        @pl.when(step == max(total - 2, 0))
        def _():
            ride.relay(x_in, x_out, sems)

        body(*ins, *outs, *scr)

        @pl.when(step == total - 1)
        def _():
            ride.finish(x_in, x_out, sems)

    res = pl.pallas_call(
        wrapped, name=name, grid=grid, in_specs=list(in_specs) + [_ANY] * n_x,
        out_specs=list(out_specs) + [_ANY] * n_x, out_shape=list(out_shape) + list(ride.out_shape),
        scratch_shapes=list(scratch_shapes) + list(ride.sems),
        compiler_params=_cp(("arbitrary",) * len(grid)))(*args, *ride.arrays)
    return res[:n_out], list(res[n_out:])


def _pair_sum(g, ra, core, out_dtype, name):
    _, r, C = g.shape
    tr = _tile(r, 576)

    def body(c_ref, g_ref, ra_ref, p_ref):
        p_ref[...] = (g_ref[...] + ra_ref[...]).astype(p_ref.dtype)

    return pl.pallas_call(
        body,
        name=name,
        grid_spec=pltpu.PrefetchScalarGridSpec(
            num_scalar_prefetch=1,
            grid=(4, r // tr),
            in_specs=[pl.BlockSpec((1, tr, C), lambda j, t, c_ref: (2 * j + c_ref[0], t, 0)),
                      pl.BlockSpec((1, tr, C), lambda j, t, c_ref: (j, t, 0))],
            out_specs=pl.BlockSpec((1, tr, C), lambda j, t, c_ref: (j, t, 0)),
        ),
        out_shape=jax.ShapeDtypeStruct((4, r, C), out_dtype),
        compiler_params=_cp(("parallel", "parallel")),
    )(core, g, ra)


def _slab_all_reduce(slab):
    _, r, lanes = slab.shape

    def body(s_ref, o_ref, ra, rb, ps, sems_a, sems_b, sems_c):
        x, y, c = lax.axis_index("x"), lax.axis_index("y"), lax.axis_index("c")
        chips = [(1 - x, y), (x, 1 - y), (1 - x, 1 - y)]
        pair = [pltpu.make_async_remote_copy(
            src_ref=s_ref.at[2 * k + (1 - c)], dst_ref=ra.at[k], send_sem=sems_a.at[0, k], recv_sem=sems_a.at[1, k],
            device_id=(x, y, 1 - c), device_id_type=MESH) for k in range(4)]
        for cp in pair:
            cp.start()
        for cp in pair:
            cp.wait()
        for k in range(4):
            ps[k] = s_ref[2 * k + c] + ra[k]
        cross = [pltpu.make_async_remote_copy(
            src_ref=ps.at[2 * ch[0] + ch[1]], dst_ref=rb.at[q], send_sem=sems_b.at[0, q], recv_sem=sems_b.at[1, q],
            device_id=(*ch, c), device_id_type=MESH) for q, ch in enumerate(chips)]
        for cp in cross:
            cp.start()
        for cp in cross:
            cp.wait()
        me = 4 * x + 2 * y + c
        o_ref[me] = ((ps[2 * x + y] + rb[0]) + rb[1]) + rb[2]
        flips = [(dx, dy, dc) for dx in (0, 1) for dy in (0, 1) for dc in (0, 1) if dx + dy + dc]
        spread = [pltpu.make_async_remote_copy(
            src_ref=o_ref.at[me], dst_ref=o_ref.at[me], send_sem=sems_c.at[0, q], recv_sem=sems_c.at[1, q],
            device_id=(x + dx - 2 * x * dx, y + dy - 2 * y * dy, c + dc - 2 * c * dc), device_id_type=MESH)
            for q, (dx, dy, dc) in enumerate(flips)]
        for cp in spread:
            cp.start()
        for q, (dx, dy, dc) in enumerate(flips):
            peer = 4 * (x + dx - 2 * x * dx) + 2 * (y + dy - 2 * y * dy) + (c + dc - 2 * c * dc)
            pltpu.make_async_remote_copy(
                src_ref=o_ref.at[peer], dst_ref=o_ref.at[peer], send_sem=sems_c.at[0, q], recv_sem=sems_c.at[1, q],
                device_id=(x, y, c), device_id_type=MESH).wait_recv()
        for cp in spread:
            cp.wait_send()

    whole = pl.BlockSpec(memory_space=pltpu.VMEM)
    return pl.pallas_call(
        body, name="slab_all_reduce", in_specs=[whole], out_specs=whole,
        out_shape=jax.ShapeDtypeStruct(slab.shape, F32),
        scratch_shapes=[pltpu.VMEM((4, r, lanes), F32), pltpu.VMEM((3, r, lanes), F32), pltpu.VMEM((4, r, lanes), F32),
                        pltpu.SemaphoreType.DMA((2, 4)), pltpu.SemaphoreType.DMA((2, 3)),
                        pltpu.SemaphoreType.DMA((2, 7))],
        compiler_params=_cp(),
    )(slab)


def _adamw_reduced(ps, rbs, chip, w, m, v, name):
    nh = len(ps)
    R, C = w.shape
    ch = C // nh
    tr = _tile(R, 288)
    nt = R // tr
    c1 = 1.0 - ADAM_B1 ** ADAM_STEP
    c2 = 1.0 - ADAM_B2 ** ADAM_STEP

    def body(c_ref, *refs):
        p_refs, rb_refs = refs[:nh], refs[nh:2 * nh]
        w_ref, m_ref, v_ref, g_ref, d_ref, nm_ref, nv_ref = refs[2 * nh:]
        for h in range(nh):
            @pl.when(pl.program_id(0) == h)
            def _(h=h):
                rb = rb_refs[h]
                gv = p_refs[h][0].astype(F32) + rb[0].astype(F32)
                gv = gv + rb[1].astype(F32)
                gv = gv + rb[2].astype(F32)
                nm = ADAM_B1 * m_ref[...] + (1.0 - ADAM_B1) * gv
                nv = ADAM_B2 * v_ref[...] + (1.0 - ADAM_B2) * (gv * gv)
                g_ref[...] = gv
                nm_ref[...] = nm
                nv_ref[...] = nv
                d_ref[...] = -ADAM_LR * ((nm / c1) / (jnp.sqrt(nv / c2) + ADAM_EPS) + ADAM_WD * w_ref[...])

    def held(h):
        return lambda hh, tt: jnp.where(hh == h, tt, jnp.where(hh < h, 0, nt - 1))

    p_specs = [pl.BlockSpec((1, tr, ch), lambda hh, tt, c_ref, f=held(h): (c_ref[0], f(hh, tt), 0))
               for h in range(nh)]
    rb_specs = [pl.BlockSpec((3, tr, ch), lambda hh, tt, c_ref, f=held(h): (0, f(hh, tt), 0)) for h in range(nh)]
    blk = pl.BlockSpec((tr, ch), lambda hh, tt, c_ref: (tt, hh))
    return pl.pallas_call(
        body,
        name=name,
        grid_spec=pltpu.PrefetchScalarGridSpec(
            num_scalar_prefetch=1, grid=(nh, nt), in_specs=p_specs + rb_specs + [blk] * 3, out_specs=[blk] * 4),
        out_shape=[jax.ShapeDtypeStruct((R, C), F32)] * 4,
        compiler_params=_cp(("arbitrary", "arbitrary")),
    )(chip, *ps, *rbs, w, m, v)


_SMALL = ("norm_w", "q_norm_w", "k_norm_w", "sinks", "a_re", "a_im", "log_step", "b_re", "b_im", "c_re", "c_im",
          "d_skip", "b_glu", "attn_out_norm_w", "ssm_out_norm_w")
_WEIGHTS = ("norm_w", "w_in", "q_norm_w", "k_norm_w", "sinks", "a_re", "a_im", "log_step", "b_re", "b_im", "c_re",
            "c_im", "d_skip", "w_glu", "b_glu", "attn_out_norm_w", "ssm_out_norm_w", "w_out")
_SMALL_2D = dict(norm_w=(1, 2048), q_norm_w=(1, 64), k_norm_w=(1, 64), sinks=(1, 16), a_re=(64, 64), a_im=(64, 64),
                 log_step=(1, 64), b_re=(1024, 64), b_im=(1024, 64), c_re=(1024, 64), c_im=(1024, 64),
                 d_skip=(1, 1024), b_glu=(1, 1024), attn_out_norm_w=(1, 1024), ssm_out_norm_w=(1, 1024))
_P_MINOR = ("b_re", "b_im")


def _flat_form(n, t):
    return t.transpose(0, 2, 1) if n in _P_MINOR else t


def _own_form(n, t, shape):
    if n in _P_MINOR:
        return t.reshape(shape[0], shape[2], shape[1]).transpose(0, 2, 1)
    return t.reshape(shape)


def _slab_rows(n):
    return -(-n // 1024) * 8


_PACK_ROWS = 2304


_LOSS_ROW = 2192


def _pack(d, loss):
    parts = []
    for n in _SMALL:
        flat = _flat_form(n, d[n]).reshape(-1).astype(F32)
        rows = _slab_rows(flat.shape[0])
        parts.append(jnp.pad(flat, (0, rows * 128 - flat.shape[0])).reshape(rows, 128))
    assert sum(p.shape[0] for p in parts) == _LOSS_ROW
    parts.append(jnp.pad(loss.reshape(1, 1), ((0, _PACK_ROWS - _LOSS_ROW - 1), (0, 127))))
    return jnp.concatenate(parts, axis=0)


def _unpack(packed, like):
    out, off = {}, 0
    for n in _SMALL:
        size = math.prod(like[n].shape)
        rows = _slab_rows(size)
        out[n] = _own_form(n, packed[off:off + rows].reshape(-1)[:size], like[n].shape)
        off += rows
    return out


def _adamw_small(g, w, m, v):
    c1 = 1.0 - ADAM_B1 ** ADAM_STEP
    c2 = 1.0 - ADAM_B2 ** ADAM_STEP
    k = len(_SMALL)

    def body(*refs):
        ins, outs = refs[:4 * k], refs[4 * k:]
        for j in range(k):
            gv, wv, mv, vv = (ins[q * k + j][...] for q in range(4))
            nm = ADAM_B1 * mv + (1.0 - ADAM_B1) * gv
            nv = ADAM_B2 * vv + (1.0 - ADAM_B2) * (gv * gv)
            outs[j][...] = -ADAM_LR * ((nm / c1) / (jnp.sqrt(nv / c2) + ADAM_EPS) + ADAM_WD * wv)
            outs[k + j][...] = nm
            outs[2 * k + j][...] = nv

    args = [_flat_form(n, d[n]).reshape(_SMALL_2D[n]) for d in (g, w, m, v) for n in _SMALL]
    shapes = [jax.ShapeDtypeStruct(_SMALL_2D[n], F32) for _ in range(3) for n in _SMALL]
    outs = pl.pallas_call(body, name="adamw_small", out_shape=shapes, compiler_params=_cp())(*args)
    res = []
    for q in range(3):
        res.append({n: _own_form(n, outs[q * k + j], w[n].shape) for j, n in enumerate(_SMALL)})
    return res


def kernel(x, positions, norm_w, w_in, q_norm_w, k_norm_w, sinks, a_re, a_im, log_step, b_re, b_im, c_re, c_im, d_skip, w_glu, b_glu, attn_out_norm_w, ssm_out_norm_w, w_out, loss_target, m_norm_w, m_w_in, m_q_norm_w, m_k_norm_w, m_sinks, m_a_re, m_a_im, m_log_step, m_b_re, m_b_im, m_c_re, m_c_im, m_d_skip, m_w_glu, m_b_glu, m_attn_out_norm_w, m_ssm_out_norm_w, m_w_out, v_norm_w, v_w_in, v_q_norm_w, v_k_norm_w, v_sinks, v_a_re, v_a_im, v_log_step, v_b_re, v_b_im, v_c_re, v_c_im, v_d_skip, v_w_glu, v_b_glu, v_attn_out_norm_w, v_ssm_out_norm_w, v_w_out):
    w = dict(norm_w=norm_w, w_in=w_in, q_norm_w=q_norm_w, k_norm_w=k_norm_w, sinks=sinks, a_re=a_re, a_im=a_im,
             log_step=log_step, b_re=b_re, b_im=b_im, c_re=c_re, c_im=c_im, d_skip=d_skip, w_glu=w_glu, b_glu=b_glu,
             attn_out_norm_w=attn_out_norm_w, ssm_out_norm_w=ssm_out_norm_w, w_out=w_out)
    m = dict(norm_w=m_norm_w, w_in=m_w_in, q_norm_w=m_q_norm_w, k_norm_w=m_k_norm_w, sinks=m_sinks, a_re=m_a_re,
             a_im=m_a_im, log_step=m_log_step, b_re=m_b_re, b_im=m_b_im, c_re=m_c_re, c_im=m_c_im, d_skip=m_d_skip,
             w_glu=m_w_glu, b_glu=m_b_glu, attn_out_norm_w=m_attn_out_norm_w, ssm_out_norm_w=m_ssm_out_norm_w,
             w_out=m_w_out)
    v = dict(norm_w=v_norm_w, w_in=v_w_in, q_norm_w=v_q_norm_w, k_norm_w=v_k_norm_w, sinks=v_sinks, a_re=v_a_re,
             a_im=v_a_im, log_step=v_log_step, b_re=v_b_re, b_im=v_b_im, c_re=v_c_re, c_im=v_c_im, d_skip=v_d_skip,
             w_glu=v_w_glu, b_glu=v_b_glu, attn_out_norm_w=v_attn_out_norm_w, ssm_out_norm_w=v_ssm_out_norm_w,
             w_out=v_w_out)
    core = lax.axis_index("c").astype(jnp.int32).reshape(1)
    chip = (2 * lax.axis_index("x") + lax.axis_index("y")).astype(jnp.int32).reshape(1)

    loss, grad_x, grads, parts = _step(x[0], positions[0], loss_target[0], w, core, chip)
    delta, new_m, new_v = {}, {}, {}
    for n in ("w_glu", "w_out"):
        grads[n], delta[n], new_m[n], new_v[n] = _adamw_reduced(*parts[n], chip, w[n], m[n], v[n], f"adamw_{n}")
    g_t, d_t, m_t, v_t = _adamw_reduced(*parts["w_in"], chip, w["w_in"].T, m["w_in"].T, v["w_in"].T, "adamw_w_in")
    grads["w_in"], delta["w_in"], new_m["w_in"], new_v["w_in"] = g_t.T, d_t.T, m_t.T, v_t.T
    d_s, m_s, v_s = _adamw_small(grads, w, m, v)
    delta.update(d_s)
    new_m.update(m_s)
    new_v.update(v_s)

    return (loss, grad_x[None], *[grads[n] for n in _WEIGHTS], *[delta[n] for n in _WEIGHTS],
            *[new_m[n] for n in _WEIGHTS], *[new_v[n] for n in _WEIGHTS])
```

```python
import math

import jax
import jax.numpy as jnp
from jax import lax
from jax.experimental import pallas as pl
from jax.experimental.pallas import tpu as pltpu

F32 = jnp.float32
BF16 = jnp.bfloat16

D_MODEL = 2048
ATTN_W = 1024
KV_W = 256
SSM_W = 1024
HEAD_DIM = 64
N_HEADS = 16
N_KV = 4
IN_W = 4608
BLOCK = 128
ROPE_THETA = 10000.0
NORM_EPS = 1e-6
SSM_G = 64
SSM_P = 64
SSM_H = 16
CHUNK = 16
CW = CHUNK * SSM_H
N_DEV = 8

ADAM_LR = 0.001
ADAM_B1 = 0.9
ADAM_B2 = 0.999
ADAM_EPS = 1e-08
ADAM_WD = 0.01
ADAM_STEP = 10

VMEM_LIMIT = 56 * 1024 * 1024
MESH = pl.DeviceIdType.MESH


def _cp(sem=None):
    if sem is None:
        return pltpu.CompilerParams(vmem_limit_bytes=VMEM_LIMIT)
    return pltpu.CompilerParams(vmem_limit_bytes=VMEM_LIMIT, dimension_semantics=sem)


def _sigmoid(x):
    return 0.5 * jnp.tanh(0.5 * x) + 0.5


def _silu(x):
    return x * _sigmoid(x)


def _dsilu(x):
    s = _sigmoid(x)
    return s * (1.0 + x * (1.0 - s))


_GELU_C = math.sqrt(2.0 / math.pi)


def _gelu(y):
    t = jnp.tanh(_GELU_C * (y + 0.044715 * y * y * y))
    return 0.5 * y * (1.0 + t)


def _dgelu(y):
    t = jnp.tanh(_GELU_C * (y + 0.044715 * y * y * y))
    return 0.5 * (1.0 + t) + 0.5 * y * (1.0 - t * t) * _GELU_C * (1.0 + 3.0 * 0.044715 * y * y)


def _tile(n, want):
    if n <= want:
        return n
    for t in range(want - want % 16, 0, -16):
        if n % t == 0:
            return t
    raise ValueError((n, want))


def _mm(a, b, mode, out_dtype, name, tm=512, tn=1024, add=None, ride=None, panel=None):
    if mode == "nn":
        (M, K), (K2, N) = a.shape, b.shape
    elif mode == "nt":
        (M, K), (N, K2) = a.shape, b.shape
    else:
        (K, M), (K2, N) = a.shape, b.shape
    assert K == K2
    tm, tn = _tile(M, tm), _tile(N, tn)
    p0 = 0
    if panel is not None:
        assert mode != "nt" and add is None
        p0, N = panel, tn
    dn = {"nn": _NN, "nt": _NT, "tn": _TN}[mode]

    def body(a_ref, b_ref, *rest):
        o_ref = rest[-1]
        acc = lax.dot_general(a_ref[...].astype(BF16), b_ref[...].astype(BF16), dn, preferred_element_type=F32)
        if add is not None:
            acc = acc + rest[0][...]
        o_ref[...] = acc.astype(o_ref.dtype)

    a_spec = pl.BlockSpec((K, tm), lambda j, i: (0, i)) if mode == "tn" else pl.BlockSpec((tm, K), lambda j, i: (i, 0))
    b_spec = (pl.BlockSpec((tn, K), lambda j, i: (j, 0)) if mode == "nt"
              else pl.BlockSpec((K, tn), lambda j, i: (0, j + p0)))
    o_spec = pl.BlockSpec((tm, tn), lambda j, i: (i, j))
    extra = () if add is None else (add,)
    if ride is not None:
        (out,), landed = _call(body, name, (N // tn, M // tm), [a_spec, b_spec] + [o_spec] * len(extra), [o_spec],
                               [jax.ShapeDtypeStruct((M, N), out_dtype)], (a, b, *extra), ride=ride)
        return out, landed
    return pl.pallas_call(
        body,
        name=name,
        grid=(N // tn, M // tm),
        in_specs=[a_spec, b_spec] + [o_spec] * len(extra),
        out_specs=o_spec,
        out_shape=jax.ShapeDtypeStruct((M, N), out_dtype),
        compiler_params=_cp(("parallel", "parallel")),
    )(a, b, *extra)


_CHIP_ORDER = (0, 2, 1, 3)


def _rms_inproj_gather(x, norm_w, wt_shard, chip):
    L = x.shape[0]
    tm = _tile(L, 512)
    ni = L // tm
    r = IN_W // N_DEV
    tn = 2 * r

    def body(chip_ref, x_ref, nw_ref, shard, proj_ref, hn_ref, wt_hbm, hn_scr, w_scr, send_sems, recv_sems, loc_sems):
        jc, i = pl.program_id(0), pl.program_id(1)
        xx, yy, c = lax.axis_index("x"), lax.axis_index("y"), lax.axis_index("c")
        me, sibling = (xx, yy, c), (xx, yy, 1 - c)
        chips = [(1 - xx, yy), (xx, 1 - yy), (1 - xx, 1 - yy)]

        def slot(dev):
            return wt_hbm.at[4 * dev[0] + 2 * dev[1] + dev[2]]

        def copy(q, block, to, src=None):
            return pltpu.make_async_remote_copy(
                src_ref=slot(block) if src is None else src, dst_ref=slot(block),
                send_sem=send_sems.at[q], recv_sem=recv_sems.at[q], device_id=to, device_id_type=MESH)

        def rows_of(buf, core):
            return w_scr.at[buf, pl.ds(pl.multiple_of(core * r, 16), r)]

        mine = pltpu.make_async_copy(shard, slot(me), loc_sems.at[0])
        sends = [copy(0, me, sibling, src=shard)] + [copy(1 + j, me, (*ch, c), src=shard) for j, ch in enumerate(chips)]
        first = jnp.logical_and(jc == 0, i == 0)

        @pl.when(first)
        def _():
            mine.start()
            for cp in sends[:3]:
                cp.start()
            own = pltpu.make_async_copy(shard, rows_of(0, c), loc_sems.at[1])
            own.start()
            copy(0, sibling, me).wait_recv()
            sib = pltpu.make_async_copy(slot(sibling), rows_of(0, 1 - c), loc_sems.at[2])
            sib.start()
            own.wait()
            sib.wait()

        def take_direct(j, ch):
            copy(1 + j, (*ch, c), me).wait_recv()
            copy(4 + j, (*ch, c), sibling).start()
            if j == 0:
                sends[1].wait_send()
                sends[2].wait_send()
                sends[3].start()
            pltpu.make_async_copy(slot((*ch, c)), rows_of((1 + j) % 2, c), loc_sems.at[1]).start()

        for j, ch in enumerate(chips):
            early = jnp.logical_and(jc == j, i == ni // 2) if j > 0 else jnp.logical_and(jc == 1, i == 0)

            @pl.when(early)
            def _(j=j, ch=ch):
                take_direct(j, ch)

            @pl.when(jnp.logical_and(jc == 1 + j, i == 0))
            def _(j=j, ch=ch):
                buf = (1 + j) % 2
                copy(4 + j, (*ch, 1 - c), me).wait_recv()
                passed = pltpu.make_async_copy(slot((*ch, 1 - c)), rows_of(buf, 1 - c), loc_sems.at[2])
                passed.start()
                pltpu.make_async_copy(slot((*ch, c)), rows_of(buf, c), loc_sems.at[1]).wait()
                passed.wait()

        rows = pl.ds(pl.multiple_of(i * tm, tm), tm)

        @pl.when(jc == 0)
        def _():
            xv = x_ref[...]
            rstd = lax.rsqrt(jnp.mean(xv * xv, axis=-1, keepdims=True) + NORM_EPS)
            hn = (xv * rstd * nw_ref[...]).astype(BF16)
            hn_scr[rows, :] = hn
            hn_ref[...] = hn

        for buf in range(2):
            @pl.when(jc % 2 == buf)
            def _(buf=buf):
                proj_ref[...] = lax.dot_general(hn_scr[rows, :], w_scr[buf], _NT, preferred_element_type=F32)

        @pl.when(jnp.logical_and(jc == 3, i == ni - 1))
        def _():
            sends[0].wait_send()
            sends[3].wait_send()
            for j, ch in enumerate(chips):
                copy(4 + j, (*ch, c), sibling).wait_send()
            mine.wait()

    def tile_of(jc, chip_ref):
        mask = jnp.where(jc == 1, _CHIP_ORDER[1], jnp.where(jc == 2, _CHIP_ORDER[2], jnp.where(jc == 3, _CHIP_ORDER[3], 0)))
        return jnp.bitwise_xor(chip_ref[0], mask)

    held = lambda jc, i: jnp.where(jc == 0, i, ni - 1)
    return pl.pallas_call(
        body,
        name="rms_inproj_gather",
        grid_spec=pltpu.PrefetchScalarGridSpec(
            num_scalar_prefetch=1,
            grid=(4, ni),
            in_specs=[pl.BlockSpec((tm, D_MODEL), lambda jc, i, ch: (held(jc, i), 0)),
                      pl.BlockSpec((1, D_MODEL), lambda jc, i, ch: (0, 0)), _ANY],
            out_specs=[pl.BlockSpec((tm, tn), lambda jc, i, ch: (i, tile_of(jc, ch))),
                       pl.BlockSpec((tm, D_MODEL), lambda jc, i, ch: (held(jc, i), 0)), _ANY],
            scratch_shapes=[pltpu.VMEM((L, D_MODEL), BF16), pltpu.VMEM((2, tn, D_MODEL), BF16),
                            pltpu.SemaphoreType.DMA((7,)), pltpu.SemaphoreType.DMA((7,)), pltpu.SemaphoreType.DMA((3,))],
        ),
        out_shape=[jax.ShapeDtypeStruct((L, IN_W), F32), jax.ShapeDtypeStruct((L, D_MODEL), BF16),
                   jax.ShapeDtypeStruct((N_DEV, r, D_MODEL), BF16)],
        compiler_params=_cp(("arbitrary", "arbitrary")),
    )(chip, x, norm_w.reshape(1, D_MODEL), wt_shard)


def _seg_sum(v):
    a = lax.broadcasted_iota(jnp.int32, (128, 128), 0) // HEAD_DIM
    b = lax.broadcasted_iota(jnp.int32, (128, 128), 1) // HEAD_DIM
    ones = jnp.where(a == b, 1.0, 0.0).astype(BF16)
    hi = v.astype(BF16)
    lo = (v - hi.astype(F32)).astype(BF16)
    return jnp.dot(hi, ones, preferred_element_type=F32) + jnp.dot(lo, ones, preferred_element_type=F32)


def _rot_half(t):
    lane = lax.broadcasted_iota(jnp.int32, t.shape, 1)
    return jnp.where(lane % HEAD_DIM < HEAD_DIM // 2, pltpu.roll(t, 128 - HEAD_DIM // 2, 1),
                     pltpu.roll(t, HEAD_DIM // 2, 1))


def _norm_rope(raw, w, cos, sin):
    r = lax.rsqrt(_seg_sum(raw * raw) * (1.0 / HEAD_DIM) + NORM_EPS)
    tn = raw * r * w
    return r, tn * cos + _rot_half(tn) * sin


def _norm_rope_bwd(d_rot, raw, w, cos, sin):
    r = lax.rsqrt(_seg_sum(raw * raw) * (1.0 / HEAD_DIM) + NORM_EPS)
    d_tn = d_rot * cos + _rot_half(d_rot * sin)
    xh = raw * r
    gw = d_tn * w
    d_raw = r * (gw - xh * (_seg_sum(gw * xh) * (1.0 / HEAD_DIM)))
    return d_raw, d_tn * xh


def _band_mask2(has_prev):
    qi = lax.broadcasted_iota(jnp.int32, (2 * BLOCK, 2 * BLOCK), 0) % BLOCK + BLOCK
    kj = lax.broadcasted_iota(jnp.int32, (2 * BLOCK, 2 * BLOCK), 1)
    rel = qi - kj
    return (rel >= 0) & (rel < BLOCK) & ((kj >= BLOCK) | has_prev)


def _half_tiles(pair):
    lo = lax.broadcasted_iota(jnp.int32, pair.shape, 1) < HEAD_DIM
    sw = pltpu.roll(pair, HEAD_DIM, 1)
    z = jnp.zeros_like(pair)
    return (jnp.where(lo, pair, z).astype(BF16), jnp.where(lo, z, sw).astype(BF16),
            jnp.where(lo, sw, z).astype(BF16), jnp.where(lo, z, pair).astype(BF16))


def _two_rows(top, bottom):
    row = lax.broadcasted_iota(jnp.int32, (2 * BLOCK, 1), 0)
    return jnp.where(row < BLOCK, top, bottom)


def _lane_col(mat, h):
    lane = lax.broadcasted_iota(jnp.int32, mat.shape, 1)
    return jnp.sum(jnp.where(lane == h, mat, 0.0), axis=1, keepdims=True)


_SCALE = 1.0 / math.sqrt(HEAD_DIM)
_NT = (((1,), (1,)), ((), ()))
_NN = (((1,), (0,)), ((), ()))
_TN = (((0,), (0,)), ((), ()))


def _qk_prep(proj, tab, qw, kw):
    L = proj.shape[0]
    tm = _tile(L, 512)

    def body(q_ref, k_ref, t_ref, qw_ref, kw_ref, qo_ref, ko_ref):
        cos, sin = t_ref[:, :128], t_ref[:, 128:]
        for c in range(ATTN_W // 128):
            _, qr = _norm_rope(q_ref[:, c * 128:(c + 1) * 128], qw_ref[...], cos, sin)
            qo_ref[:, c * 128:(c + 1) * 128] = (qr * _SCALE).astype(BF16)
        for c in range(KV_W // 128):
            _, kr = _norm_rope(k_ref[:, c * 128:(c + 1) * 128], kw_ref[...], cos, sin)
            ko_ref[:, c * 128:(c + 1) * 128] = kr.astype(BF16)

    row = pl.BlockSpec((1, 128), lambda i: (0, 0))
    return pl.pallas_call(
        body,
        name="qk_prep",
        grid=(L // tm,),
        in_specs=[pl.BlockSpec((tm, ATTN_W), lambda i: (i, 0)), pl.BlockSpec((tm, KV_W), lambda i: (i, 4)),
                  pl.BlockSpec((tm, 256), lambda i: (i, 0)), row, row],
        out_specs=[pl.BlockSpec((tm, ATTN_W), lambda i: (i, 0)), pl.BlockSpec((tm, KV_W), lambda i: (i, 0))],
        out_shape=[jax.ShapeDtypeStruct((L, ATTN_W), BF16), jax.ShapeDtypeStruct((L, KV_W), BF16)],
        compiler_params=_cp(("parallel",)),
    )(proj, proj, tab, jnp.tile(qw, 2).reshape(1, 128), jnp.tile(kw, 2).reshape(1, 128))


def _group_tiles(g, kt, vt):
    a, b = divmod(g, 2)
    return kt[a][2 * b], kt[a][2 * b + 1], vt[a][2 * b], vt[a][2 * b + 1]


def _attn_fwd(q, k, proj, sinks, ride):
    L = proj.shape[0]
    nb = L // BLOCK

    def body(q_ref, kc_ref, kp_ref, vc_ref, vp_ref, z0_ref, z1_ref, sink_ref, og_ref, o_ref, lse_ref):
        i = pl.program_id(0)
        mask = _band_mask2(i > 0)
        z = jnp.concatenate([z0_ref[...], z1_ref[...]], axis=1)
        lane = lax.broadcasted_iota(jnp.int32, (BLOCK, 128), 1)
        kt = [_half_tiles(jnp.concatenate([kp_ref[:, a * 128:(a + 1) * 128], kc_ref[:, a * 128:(a + 1) * 128]],
                                          axis=0).astype(F32)) for a in range(2)]
        vt = [_half_tiles(jnp.concatenate([vp_ref[:, a * 128:(a + 1) * 128], vc_ref[:, a * 128:(a + 1) * 128]],
                                          axis=0)) for a in range(2)]
        lse_mat = jnp.zeros((BLOCK, 128), F32)
        outs = []
        for g in range(N_KV):
            k_lo, k_hi, v_lo, v_hi = _group_tiles(g, kt, vt)
            q2 = jnp.concatenate([q_ref[:, 2 * g * 128:(2 * g + 1) * 128],
                                  q_ref[:, (2 * g + 1) * 128:(2 * g + 2) * 128]], axis=0)
            acc = jnp.zeros((2 * BLOCK, 128), F32)
            for half, (kh, vh) in enumerate(((k_lo, v_lo), (k_hi, v_hi))):
                h_top, h_bot = 4 * g + half, 4 * g + 2 + half
                s = jnp.where(mask, lax.dot_general(q2, kh, _NT, preferred_element_type=F32), -1e30)
                sink = _two_rows(sink_ref[h_top], sink_ref[h_bot])
                m = jnp.maximum(jnp.max(s, axis=-1, keepdims=True), sink)
                e = jnp.exp(s - m)
                den = jnp.sum(e, axis=-1, keepdims=True) + jnp.exp(sink - m)
                p = e * (1.0 / den)
                acc = acc + jnp.dot(p.astype(BF16), vh, preferred_element_type=F32)
                lse = m + jnp.log(den)
                lse_mat = jnp.where(lane == h_top, lse[:BLOCK], lse_mat)
                lse_mat = jnp.where(lane == h_bot, lse[BLOCK:], lse_mat)
            outs += [acc[:BLOCK], acc[BLOCK:]]
        o = jnp.concatenate(outs, axis=1)
        o_ref[...] = o
        og_ref[...] = o * _silu(z)
        lse_ref[...] = lse_mat

    prev = lambda i: jnp.maximum(i - 1, 0)
    return _call(
        body, "attn_fwd", (nb,),
        [pl.BlockSpec((BLOCK, ATTN_W), lambda i: (i, 0)),
         pl.BlockSpec((BLOCK, KV_W), lambda i: (i, 0)),
         pl.BlockSpec((BLOCK, KV_W), lambda i: (prev(i), 0)),
         pl.BlockSpec((BLOCK, KV_W), lambda i: (i, 5)),
         pl.BlockSpec((BLOCK, KV_W), lambda i: (prev(i), 5)),
         pl.BlockSpec((BLOCK, 512), lambda i: (i, 3)),
         pl.BlockSpec((BLOCK, 512), lambda i: (i, 4)),
         pl.BlockSpec(memory_space=pltpu.SMEM)],
        [pl.BlockSpec((BLOCK, ATTN_W), lambda i: (i, 0)),
         pl.BlockSpec((BLOCK, ATTN_W), lambda i: (i, 0)),
         pl.BlockSpec((BLOCK, 128), lambda i: (i, 0))],
        [jax.ShapeDtypeStruct((L, ATTN_W), F32), jax.ShapeDtypeStruct((L, ATTN_W), F32),
         jax.ShapeDtypeStruct((L, 128), F32)],
        (q, k, k, proj, proj, proj, proj, sinks), ride=ride)


def _attn_bwd(q, k, proj, sinks, d_o, o, lse, ride):
    L = proj.shape[0]
    nb = L // BLOCK

    def body(q_ref, kc_ref, kp_ref, vc_ref, vp_ref, do_ref, o_ref, lse_ref, sink_ref,
             dq_ref, dk_ref, dv_ref, gs_ref, ck_scr, cv_scr):
        i = pl.program_id(0)

        @pl.when(i == 0)
        def _():
            gs_ref[...] = jnp.zeros_like(gs_ref)
            ck_scr[...] = jnp.zeros_like(ck_scr)
            cv_scr[...] = jnp.zeros_like(cv_scr)

        @pl.when(i == nb)
        def _():
            dk_ref[...] = ck_scr[...]
            dv_ref[...] = cv_scr[...]

        @pl.when(i < nb)
        def _():
            mask = _band_mask2(i > 0)
            lane = lax.broadcasted_iota(jnp.int32, (1, 128), 1)
            lo = lax.broadcasted_iota(jnp.int32, (2 * BLOCK, 128), 1) < HEAD_DIM
            lse_c = lse_ref[...]
            kt = [_half_tiles(jnp.concatenate([kp_ref[:, a * 128:(a + 1) * 128], kc_ref[:, a * 128:(a + 1) * 128]],
                                              axis=0).astype(F32)) for a in range(2)]
            vt = [_half_tiles(jnp.concatenate([vp_ref[:, a * 128:(a + 1) * 128], vc_ref[:, a * 128:(a + 1) * 128]],
                                              axis=0)) for a in range(2)]
            gs = jnp.zeros((1, 128), F32)
            dq_parts = []
            dk_acc = [jnp.zeros((2 * BLOCK, 128), F32) for _ in range(2)]
            dv_acc = [jnp.zeros((2 * BLOCK, 128), F32) for _ in range(2)]
            for g in range(N_KV):
                a, b = divmod(g, 2)
                k_lo, k_hi, v_lo, v_hi = _group_tiles(g, kt, vt)
                t0, t1 = slice(2 * g * 128, (2 * g + 1) * 128), slice((2 * g + 1) * 128, (2 * g + 2) * 128)
                q2 = jnp.concatenate([q_ref[:, t0], q_ref[:, t1]], axis=0)
                do2 = jnp.concatenate([do_ref[:, t0], do_ref[:, t1]], axis=0)
                prod = do2 * jnp.concatenate([o_ref[:, t0], o_ref[:, t1]], axis=0)
                do2_b = do2.astype(BF16)
                dq2 = jnp.zeros((2 * BLOCK, 128), F32)
                dk_h, dv_h = [], []
                for half, (kh, vh) in enumerate(((k_lo, v_lo), (k_hi, v_hi))):
                    h_top, h_bot = 4 * g + half, 4 * g + 2 + half
                    lse = jnp.concatenate([_lane_col(lse_c, h_top), _lane_col(lse_c, h_bot)], axis=0)
                    sink = _two_rows(sink_ref[h_top], sink_ref[h_bot])
                    delta = jnp.sum(jnp.where(lo == (half == 0), prod, 0.0), axis=1, keepdims=True)
                    s = jnp.where(mask, lax.dot_general(q2, kh, _NT, preferred_element_type=F32), -1e30)
                    p = jnp.exp(s - lse)
                    dp = lax.dot_general(do2_b, vh, _NT, preferred_element_type=F32)
                    ds_b = (p * (dp - delta)).astype(BF16)
                    p_b = p.astype(BF16)
                    dq2 = dq2 + jnp.dot(ds_b, kh, preferred_element_type=F32)
                    dk_h.append(lax.dot_general(ds_b, q2, _TN, preferred_element_type=F32))
                    dv_h.append(lax.dot_general(p_b, do2_b, _TN, preferred_element_type=F32))
                    gsink = -jnp.exp(sink - lse) * delta
                    row = lax.broadcasted_iota(jnp.int32, (2 * BLOCK, 1), 0)
                    gs = gs + jnp.where(lane == h_top, jnp.sum(jnp.where(row < BLOCK, gsink, 0.0)), 0.0)
                    gs = gs + jnp.where(lane == h_bot, jnp.sum(jnp.where(row >= BLOCK, gsink, 0.0)), 0.0)
                dq_parts += [dq2[:BLOCK], dq2[BLOCK:]]
                for acc, parts in ((dk_acc, dk_h), (dv_acc, dv_h)):
                    t = jnp.where(lo, parts[0], parts[1])
                    t = t + pltpu.roll(t, HEAD_DIM, 1)
                    acc[a] = acc[a] + jnp.where(lo == (b == 0), t, 0.0)
            dq_ref[...] = jnp.concatenate(dq_parts, axis=1)
            dk_full = jnp.concatenate(dk_acc, axis=1)
            dv_full = jnp.concatenate(dv_acc, axis=1)
            dk_ref[...] = ck_scr[...] + dk_full[:BLOCK]
            dv_ref[...] = cv_scr[...] + dv_full[:BLOCK]
            ck_scr[...] = dk_full[BLOCK:]
            cv_scr[...] = dv_full[BLOCK:]
            gs_ref[...] += gs

    cur = lambda i: jnp.minimum(i, nb - 1)
    prev = lambda i: jnp.maximum(jnp.minimum(i, nb - 1) - 1, 0)
    done = lambda i: jnp.maximum(i - 1, 0)
    bs = pl.BlockSpec
    return _call(
        body, "attn_bwd", (nb + 1,),
        [bs((BLOCK, ATTN_W), lambda i: (cur(i), 0)),
         bs((BLOCK, KV_W), lambda i: (cur(i), 0)), bs((BLOCK, KV_W), lambda i: (prev(i), 0)),
         bs((BLOCK, KV_W), lambda i: (cur(i), 5)), bs((BLOCK, KV_W), lambda i: (prev(i), 5)),
         bs((BLOCK, ATTN_W), lambda i: (cur(i), 0)), bs((BLOCK, ATTN_W), lambda i: (cur(i), 0)),
         bs((BLOCK, 128), lambda i: (cur(i), 0)), bs(memory_space=pltpu.SMEM)],
        [bs((BLOCK, ATTN_W), lambda i: (cur(i), 0)),
         bs((BLOCK, KV_W), lambda i: (done(i), 0)), bs((BLOCK, KV_W), lambda i: (done(i), 0)),
         bs((1, 128), lambda i: (0, 0))],
        [jax.ShapeDtypeStruct((L, ATTN_W), F32), jax.ShapeDtypeStruct((L, KV_W), F32),
         jax.ShapeDtypeStruct((L, KV_W), F32), jax.ShapeDtypeStruct((1, 128), F32)],
        (q, k, k, proj, proj, d_o, o, lse, sinks),
        [pltpu.VMEM((BLOCK, KV_W), F32), pltpu.VMEM((BLOCK, KV_W), F32)], ride)


def _qk_prep_bwd(proj, tab, qw, kw, d_q, d_k, d_v, d_za, d_u, d_zs):
    L = proj.shape[0]
    tm = _tile(L, 512)
    z0 = ATTN_W + 2 * KV_W

    def body(q_ref, k_ref, t_ref, qw_ref, kw_ref, dq_ref, dk_ref, dv_ref, dza_ref, du_ref, dzs_ref,
             out_ref, gq_ref, gk_ref):
        i = pl.program_id(0)

        @pl.when(i == 0)
        def _():
            gq_ref[...] = jnp.zeros_like(gq_ref)
            gk_ref[...] = jnp.zeros_like(gk_ref)

        cos, sin = t_ref[:, :128], t_ref[:, 128:]
        gq = jnp.zeros((1, 128), F32)
        gk = jnp.zeros((1, 128), F32)
        for c in range(ATTN_W // 128):
            cs = slice(c * 128, (c + 1) * 128)
            d_raw, gw = _norm_rope_bwd(dq_ref[:, cs] * _SCALE, q_ref[:, cs], qw_ref[...], cos, sin)
            out_ref[:, cs] = d_raw.astype(BF16)
            gq = gq + jnp.sum(gw, axis=0, keepdims=True)
        for c in range(KV_W // 128):
            cs = slice(c * 128, (c + 1) * 128)
            d_raw, gw = _norm_rope_bwd(dk_ref[:, cs], k_ref[:, cs], kw_ref[...], cos, sin)
            out_ref[:, ATTN_W + c * 128:ATTN_W + (c + 1) * 128] = d_raw.astype(BF16)
            gk = gk + jnp.sum(gw, axis=0, keepdims=True)
        out_ref[:, ATTN_W + KV_W:z0] = dv_ref[...].astype(BF16)
        out_ref[:, z0:z0 + ATTN_W] = dza_ref[...]
        out_ref[:, z0 + ATTN_W:z0 + ATTN_W + SSM_W] = du_ref[...].astype(BF16)
        out_ref[:, z0 + ATTN_W + SSM_W:] = dzs_ref[...]
        gq_ref[...] += gq
        gk_ref[...] += gk

    row = pl.BlockSpec((1, 128), lambda i: (0, 0))
    blk = lambda w, c: pl.BlockSpec((tm, w), lambda i: (i, c))
    return pl.pallas_call(
        body,
        name="qk_prep_bwd",
        grid=(L // tm,),
        in_specs=[blk(ATTN_W, 0), blk(KV_W, 4), blk(256, 0), row, row, blk(ATTN_W, 0), blk(KV_W, 0), blk(KV_W, 0),
                  blk(ATTN_W, 0), blk(SSM_W, 0), blk(SSM_W, 0)],
        out_specs=[blk(IN_W, 0), row, row],
        out_shape=[jax.ShapeDtypeStruct((L, IN_W), BF16), jax.ShapeDtypeStruct((1, 128), F32),
                   jax.ShapeDtypeStruct((1, 128), F32)],
        compiler_params=_cp(("arbitrary",)),
    )(proj, proj, tab, jnp.tile(qw, 2).reshape(1, 128), jnp.tile(kw, 2).reshape(1, 128), d_q, d_k, d_v,
      d_za, d_u, d_zs)


def _cmul(a, b):
    return a[0] * b[0] - a[1] * b[1], a[0] * b[1] + a[1] * b[0]


def _cmul_conj(a, b):
    return a[0] * b[0] + a[1] * b[1], a[1] * b[0] - a[0] * b[1]


def _cadd(a, b):
    return a[0] + b[0], a[1] + b[1]


def _dot3(a, b, dn):
    ah, bh = a.astype(BF16), b.astype(BF16)
    al, bl = (a - ah.astype(F32)).astype(BF16), (b - bh.astype(F32)).astype(BF16)
    d = lambda u, v: lax.dot_general(u, v, dn, preferred_element_type=F32)
    return d(ah, bh) + d(ah, bl) + d(al, bh)


def _s5_discretise(a_re, a_im, ls, cosx, sinx, bt):
    delta = jnp.exp(ls)
    er = jnp.exp(a_re * delta)
    lb = (er * cosx, er * sinx)
    den = a_re * a_re + a_im * a_im
    coef = _cmul_conj((lb[0] - 1.0, lb[1]), (a_re, a_im))
    coef = (coef[0] / den, coef[1] / den)
    return delta, lb, coef, den, _cmul(coef, bt)


def _powers(lb):
    pw = [(jnp.ones_like(lb[0]), jnp.zeros_like(lb[0]))]
    for _ in range(CHUNK):
        pw.append(_cmul(pw[-1], lb))
    return pw


def _block_rows(a, pw, idx):
    blocks = [_cmul(a, pw[i]) for i in idx]
    return (jnp.concatenate([b[0] for b in blocks], axis=-2), jnp.concatenate([b[1] for b in blocks], axis=-2))


def _block_rows_bwd(g, a, pw, idx, g_pw):
    g_a = (jnp.zeros_like(a[0]), jnp.zeros_like(a[0]))
    for j, i in enumerate(idx):
        gj = (g[0][..., j * SSM_H:(j + 1) * SSM_H, :], g[1][..., j * SSM_H:(j + 1) * SSM_H, :])
        g_a = _cadd(g_a, _cmul_conj(gj, pw[i]))
        gp = _cmul_conj(gj, a)
        g_pw[i] = _cadd(g_pw[i], (jnp.sum(gp[0], axis=-2, keepdims=True), jnp.sum(gp[1], axis=-2, keepdims=True)))
    return g_a


_IDX_S = [CHUNK - 1 - s for s in range(CHUNK)]
_IDX_C = list(range(CHUNK + 1))


def _prep_args(p):
    row = lambda t: t.reshape(SSM_G, 1, SSM_P)
    xi = p["a_im"] * jnp.exp(p["log_step"])[:, None]
    return (row(p["a_re"]), row(p["a_im"]), row(jnp.broadcast_to(p["log_step"][:, None], (SSM_G, SSM_P))),
            row(jnp.cos(xi)), row(jnp.sin(xi)), p["b_re"].transpose(0, 2, 1), p["b_im"].transpose(0, 2, 1),
            p["c_re"], p["c_im"])


PREP_GROUPS = 8


def _prep_specs():
    r1 = pl.BlockSpec((PREP_GROUPS, 1, SSM_P), lambda g: (g, 0, 0))
    r16 = pl.BlockSpec((PREP_GROUPS, SSM_H, SSM_P), lambda g: (g, 0, 0))
    return [r1] * 5 + [r16] * 4, r1, r16


def _ssm_prep(p):
    def one_group(q, are, aim, ls, cosx, sinx, btr, bti, cre, cim, mt_ref, s_ref, o_ref, a_ref):
        _, lb, _, _, bb = _s5_discretise(are[q], aim[q], ls[q], cosx[q], sinx[q], (btr[q], bti[q]))
        pw = _powers(lb)
        c = (cre[q], cim[q])
        sc = _block_rows(bb, pw, _IDX_S)
        cl = _block_rows(c, pw, _IDX_C)
        ok = (cl[0][:CW], cl[1][:CW])
        ot = (cl[0][SSM_H:], cl[1][SSM_H:])
        s_ref[q] = jnp.concatenate([sc[0], sc[1]], axis=1).astype(BF16)
        o_ref[q] = jnp.concatenate([ot[0], -ot[1]], axis=1).astype(BF16)
        a_ref[q] = jnp.concatenate([pw[CHUNK][0], pw[CHUNK][1]], axis=1)
        kt = _dot3(jnp.concatenate([bb[0], -bb[1]], axis=1), jnp.concatenate([ok[0], ok[1]], axis=1), _NT)
        lane = lax.broadcasted_iota(jnp.int32, kt.shape, 1)
        for s in range(CHUNK):
            blk = kt if s == 0 else jnp.where(lane >= SSM_H * s, pltpu.roll(kt, SSM_H * s, 1), 0.0)
            mt_ref[q, s * SSM_H:(s + 1) * SSM_H, :] = blk.astype(BF16)

    def body(*refs):
        for q in range(PREP_GROUPS):
            one_group(q, *refs)

    in_specs, r1, _ = _prep_specs()
    g3 = lambda r, c: pl.BlockSpec((PREP_GROUPS, r, c), lambda g: (g, 0, 0))
    return pl.pallas_call(
        body,
        name="ssm_prep",
        grid=(SSM_G // PREP_GROUPS,),
        in_specs=in_specs,
        out_specs=[g3(CW, CW), g3(CW, 2 * SSM_P), g3(CW, 2 * SSM_P), g3(1, 2 * SSM_P)],
        out_shape=[jax.ShapeDtypeStruct((SSM_G, CW, CW), BF16), jax.ShapeDtypeStruct((SSM_G, CW, 2 * SSM_P), BF16),
                   jax.ShapeDtypeStruct((SSM_G, CW, 2 * SSM_P), BF16),
                   jax.ShapeDtypeStruct((SSM_G, 1, 2 * SSM_P), F32)],
        compiler_params=_cp(("parallel",)),
    )(*_prep_args(p))


def _ssm_prep_bwd(p, g_mt, g_scat, g_ocat, g_a16, ride):
    def body(are, aim, ls, cosx, sinx, btr, bti, cre, cim, gmt_ref, gs_ref, go_ref, ga_ref,
             g_are, g_aim, g_ls, g_btr, g_bti, g_cre, g_cim, ga1_scr, gb1_scr):
        lam = (are[...], aim[...])
        bt = (btr[...], bti[...])
        delta, lb, coef, den, bb = _s5_discretise(lam[0], lam[1], ls[...], cosx[...], sinx[...], bt)
        pw = _powers(lb)
        c = (cre[...], cim[...])
        ok = _block_rows(c, pw, _IDX_C[:CHUNK])
        g_pw =[(jnp.zeros_like(lb[0]), jnp.zeros_like(lb[0])) for _ in range(CHUNK + 1)]
        lane = lax.broadcasted_iota(jnp.int32, (SSM_H, CW), 1)
        for q in range(PREP_GROUPS):
            g_kt = gmt_ref[q, :SSM_H, :]
            for s in range(1, CHUNK):
                blk = gmt_ref[q, s * SSM_H:(s + 1) * SSM_H, :]
                g_kt = g_kt + jnp.where(lane < CW - SSM_H * s, pltpu.roll(blk, CW - SSM_H * s, 1), 0.0)
            a1 = jnp.concatenate([bb[0][q], -bb[1][q]], axis=1)
            b1 = jnp.concatenate([ok[0][q], ok[1][q]], axis=1)
            ga1_scr[q] = _dot3(g_kt, b1, _NN)
            gb1_scr[q] = _dot3(g_kt, a1, _TN)
        g_a1, g_b1 = ga1_scr[...], gb1_scr[...]
        g_bb = (g_a1[..., :SSM_P], -g_a1[..., SSM_P:])
        gs = gs_ref[...]
        g_bb = _cadd(g_bb, _block_rows_bwd((gs[..., :SSM_P], gs[..., SSM_P:]), bb, pw, _IDX_S, g_pw))
        go = go_ref[...]
        pad = jnp.zeros_like(go[..., :SSM_H, :SSM_P])
        g_cl = (jnp.concatenate([g_b1[..., :SSM_P], pad], axis=-2) + jnp.concatenate([pad, go[..., :SSM_P]], axis=-2),
                jnp.concatenate([g_b1[..., SSM_P:], pad], axis=-2) - jnp.concatenate([pad, go[..., SSM_P:]], axis=-2))
        g_c = _block_rows_bwd(g_cl, c, pw, _IDX_C, g_pw)
        ga = ga_ref[...]
        g_pw[CHUNK] = _cadd(g_pw[CHUNK], (ga[..., :SSM_P], ga[..., SSM_P:]))
        g_lb = (jnp.zeros_like(lb[0]), jnp.zeros_like(lb[0]))
        for l in range(CHUNK - 1, -1, -1):
            g_lb = _cadd(g_lb, _cmul_conj(g_pw[l + 1], pw[l]))
            g_pw[l] = _cadd(g_pw[l], _cmul_conj(g_pw[l + 1], lb))
        g_bt = _cmul_conj(g_bb, coef)
        gc = _cmul_conj(g_bb, bt)
        g_coef = (jnp.sum(gc[0], axis=-2, keepdims=True), jnp.sum(gc[1], axis=-2, keepdims=True))
        lam_den = (lam[0] / den, lam[1] / den)
        g_lb = _cadd(g_lb, _cmul(g_coef, lam_den))
        t = _cmul(_cmul_conj(g_coef, coef), lam_den)
        g_x = _cmul_conj(g_lb, lb)
        g_are[...] = g_x[0] * delta - t[0]
        g_aim[...] = g_x[1] * delta - t[1]
        g_ls[...] = (g_x[0] * lam[0] + g_x[1] * lam[1]) * delta
        g_btr[...] = g_bt[0]
        g_bti[...] = g_bt[1]
        g_cre[...] = g_c[0]
        g_cim[...] = g_c[1]

    in_specs, r1, r16 = _prep_specs()
    g3 = lambda r, c: pl.BlockSpec((PREP_GROUPS, r, c), lambda g: (g, 0, 0))
    rows = jax.ShapeDtypeStruct((SSM_G, 1, SSM_P), F32)
    mats = jax.ShapeDtypeStruct((SSM_G, SSM_H, SSM_P), F32)
    (g_are, g_aim, g_ls, g_btr, g_bti, g_cre, g_cim), landed = _call(
        body, "ssm_prep_bwd", (SSM_G // PREP_GROUPS,),
        in_specs + [g3(CW, CW), g3(CW, 2 * SSM_P), g3(CW, 2 * SSM_P), g3(1, 2 * SSM_P)],
        [r1] * 3 + [r16] * 4, [rows] * 3 + [mats] * 4, (*_prep_args(p), g_mt, g_scat, g_ocat, g_a16),
        [pltpu.VMEM((PREP_GROUPS, SSM_H, 2 * SSM_P), F32), pltpu.VMEM((PREP_GROUPS, CW, 2 * SSM_P), F32)], ride)
    grads = dict(a_re=g_are.reshape(SSM_G, SSM_P), a_im=g_aim.reshape(SSM_G, SSM_P),
                 log_step=jnp.sum(g_ls.reshape(SSM_G, SSM_P), axis=1),
                 b_re=g_btr.transpose(0, 2, 1), b_im=g_bti.transpose(0, 2, 1), c_re=g_cre, c_im=g_cim)
    return grads, landed


def _cmul_const(xv, ar, ai):
    return xv * ar + pltpu.roll(xv, SSM_P, 1) * ai


def _chunk_scan(inc, a_row, reverse):
    n = inc.shape[0]
    lane = lax.broadcasted_iota(jnp.int32, (1, 2 * SSM_P), 1)
    row = lax.broadcasted_iota(jnp.int32, inc.shape, 0)
    sign = jnp.where(lane < SSM_P, -1.0, 1.0)
    ar = jnp.where(lane < SSM_P, a_row, pltpu.roll(a_row, SSM_P, 1))
    ai = jnp.where(lane < SSM_P, pltpu.roll(a_row, SSM_P, 1), a_row)
    if reverse:
        ai = -ai
    xv = inc
    s = 1
    while s < n:
        if reverse:
            sh = jnp.where(row < n - s, pltpu.roll(xv, n - s, 0), 0.0)
        else:
            sh = jnp.where(row >= s, pltpu.roll(xv, s, 0), 0.0)
        xv = xv + _cmul_const(sh, ar, ai * sign)
        ar, ai = ar * ar - ai * ai, 2.0 * ar * ai
        s *= 2
    return xv


def _shift_rows(xv, reverse):
    n = xv.shape[0]
    row = lax.broadcasted_iota(jnp.int32, xv.shape, 0)
    if reverse:
        return jnp.where(row < n - 1, pltpu.roll(xv, n - 1, 0), 0.0)
    return jnp.where(row >= 1, pltpu.roll(xv, 1, 0), 0.0)


GB = 128 // SSM_H
U_COL0 = (ATTN_W + 2 * KV_W + ATTN_W) // 128


HALF = CHUNK // 2


def _chunk_perm():
    r = jnp.arange(HALF * 128)
    t, g8, h = r // 128, (r % 128) // SSM_H, r % SSM_H
    return ((g8 * 128 + t * SSM_H + h)[:, None] == jnp.arange(GB * 128)[None, :]).astype(BF16)


def _load_perm(p_hbm, p_scr, sem):
    @pl.when(pl.program_id(0) == 0)
    def _():
        cp = pltpu.make_async_copy(p_hbm, p_scr, sem)
        cp.start()
        cp.wait()


def _rows_to_chunks(pieces, perm):
    halves = [jnp.dot(jnp.concatenate(pieces[k * HALF:(k + 1) * HALF], axis=1).astype(BF16), perm,
                      preferred_element_type=F32).astype(BF16) for k in range(2)]
    return [jnp.concatenate([hv[:, g * 128:(g + 1) * 128] for hv in halves], axis=1) for g in range(GB)]


def _chunks_to_rows(groups, perm, two_pass):
    pieces = []
    for k in range(2):
        v = jnp.concatenate([gv[:, k * 128:(k + 1) * 128] for gv in groups], axis=1)
        hi = v.astype(BF16)
        out = lax.dot_general(hi, perm, _NT, preferred_element_type=F32)
        if two_pass:
            lo = (v - hi.astype(F32)).astype(BF16)
            out = out + lax.dot_general(lo, perm, _NT, preferred_element_type=F32)
        pieces += [out[:, t * 128:(t + 1) * 128] for t in range(HALF)]
    return pieces


def _ssm_fwd(proj, perm, mt, scat, ocat, a16, d_skip, ride):
    L = proj.shape[0]
    nc = L // CHUNK

    def body(u_ref, p_hbm, mt_ref, s_ref, o_ref, a_ref, d_ref, y_ref, yg_ref, h_ref, p_scr, sem):
        _load_perm(p_hbm, p_scr, sem)
        perm = p_scr[...]
        rows = [pl.ds(t, nc, stride=CHUNK) for t in range(CHUNK)]
        us = [u_ref[r, :] for r in rows]
        ua = _rows_to_chunks(us, perm)
        ys = []
        for g in range(GB):
            uv = ua[g]
            inc = jnp.dot(uv, s_ref[g], preferred_element_type=F32)
            hx = _shift_rows(_chunk_scan(inc, a_ref[g], False), False)
            h_ref[g] = hx
            ys.append(jnp.dot(uv, mt_ref[g], preferred_element_type=F32)
                      + lax.dot_general(hx.astype(BF16), o_ref[g], _NT, preferred_element_type=F32))
        yp = _chunks_to_rows(ys, perm, True)
        for t, r in enumerate(rows):
            y = yp[t] + d_ref[...] * us[t]
            y_ref[r, :] = y
            yg_ref[r, :] = _gelu(y)

    g3 = lambda r, c: pl.BlockSpec((GB, r, c), lambda g: (g, 0, 0))
    col = pl.BlockSpec((L, 128), lambda g: (0, g))
    return _call(
        body, "ssm_fwd", (SSM_G // GB,),
        [pl.BlockSpec((L, 128), lambda g: (0, U_COL0 + g)), _ANY,
         g3(CW, CW), g3(CW, 2 * SSM_P), g3(CW, 2 * SSM_P), g3(1, 2 * SSM_P),
         pl.BlockSpec((1, 128), lambda g: (0, g))],
        [col, col, g3(nc, 2 * SSM_P)],
        [jax.ShapeDtypeStruct((L, SSM_W), F32), jax.ShapeDtypeStruct((L, SSM_W), F32),
         jax.ShapeDtypeStruct((SSM_G, nc, 2 * SSM_P), F32)],
        (proj, perm, mt, scat, ocat, a16, d_skip.reshape(1, SSM_W)),
        [pltpu.VMEM((HALF * 128, GB * 128), BF16), pltpu.SemaphoreType.DMA], ride)


def _ssm_bwd(d_yg, y, proj, hx, perm, mt, scat, ocat, a16, d_skip, ride):
    L = proj.shape[0]
    nc = L // CHUNK

    def body(dg_ref, y_ref, u_ref, h_ref, p_hbm, mt_ref, s_ref, o_ref, a_ref, d_ref,
             du_ref, gmt_ref, gs_ref, go_ref, ga_ref, gd_ref, p_scr, sem):
        _load_perm(p_hbm, p_scr, sem)
        perm = p_scr[...]
        rows = [pl.ds(t, nc, stride=CHUNK) for t in range(CHUNK)]
        us = [u_ref[r, :] for r in rows]
        dys = [dg_ref[r, :] * _dgelu(y_ref[r, :]) for r in rows]
        gd = jnp.zeros((1, 128), F32)
        for uv, dy in zip(us, dys):
            gd = gd + jnp.sum(dy * uv, axis=0, keepdims=True)
        gd_ref[...] = gd
        ua = _rows_to_chunks(us, perm)
        dya = _rows_to_chunks(dys, perm)
        lane = lax.broadcasted_iota(jnp.int32, (1, 2 * SSM_P), 1)
        dus = []
        for g in range(GB):
            uv, dy, hx_v = ua[g], dya[g], h_ref[g]
            dh = jnp.dot(dy, o_ref[g], preferred_element_type=F32)
            dinc = _shift_rows(_chunk_scan(dh, a_ref[g], True), True)
            dinc_b = dinc.astype(BF16)
            dus.append(lax.dot_general(dy, mt_ref[g], _NT, preferred_element_type=F32)
                       + lax.dot_general(dinc_b, s_ref[g], _NT, preferred_element_type=F32))
            gmt_ref[g] = lax.dot_general(uv, dy, _TN, preferred_element_type=F32)
            gs_ref[g] = lax.dot_general(uv, dinc_b, _TN, preferred_element_type=F32)
            go_ref[g] = lax.dot_general(dy, hx_v.astype(BF16), _TN, preferred_element_type=F32)
            p1 = dinc * hx_v
            p2 = pltpu.roll(dinc, SSM_P, 1) * hx_v
            t1 = jnp.sum(p1 + pltpu.roll(p1, SSM_P, 1), axis=0, keepdims=True)
            t2 = jnp.sum(p2 - pltpu.roll(p2, SSM_P, 1), axis=0, keepdims=True)
            ga_ref[g] = jnp.where(lane < SSM_P, t1, pltpu.roll(t2, SSM_P, 1))
        dup = _chunks_to_rows(dus, perm, False)
        for t, r in enumerate(rows):
            du_ref[r, :] = dup[t] + d_ref[...] * dys[t]

    g3 = lambda r, c: pl.BlockSpec((GB, r, c), lambda g: (g, 0, 0))
    col = pl.BlockSpec((L, 128), lambda g: (0, g))
    row = pl.BlockSpec((1, 128), lambda g: (0, g))
    return _call(
        body, "ssm_bwd", (SSM_G // GB,),
        [col, col, pl.BlockSpec((L, 128), lambda g: (0, U_COL0 + g)), g3(nc, 2 * SSM_P), _ANY,
         g3(CW, CW), g3(CW, 2 * SSM_P), g3(CW, 2 * SSM_P), g3(1, 2 * SSM_P), row],
        [col, g3(CW, CW), g3(CW, 2 * SSM_P), g3(CW, 2 * SSM_P), g3(1, 2 * SSM_P), row],
        [jax.ShapeDtypeStruct((L, SSM_W), F32), jax.ShapeDtypeStruct((SSM_G, CW, CW), F32),
         jax.ShapeDtypeStruct((SSM_G, CW, 2 * SSM_P), F32), jax.ShapeDtypeStruct((SSM_G, CW, 2 * SSM_P), F32),
         jax.ShapeDtypeStruct((SSM_G, 1, 2 * SSM_P), F32), jax.ShapeDtypeStruct((1, SSM_W), F32)],
        (d_yg, y, proj, hx, perm, mt, scat, ocat, a16, d_skip.reshape(1, SSM_W)),
        [pltpu.VMEM((HALF * 128, GB * 128), BF16), pltpu.SemaphoreType.DMA], ride)


def _merge(og, yg, w_glu, proj, b_glu, wa, ws):
    L = og.shape[0]
    tm = _tile(L, 256)

    def body(og_ref, yg_ref, wg_ref, z0_ref, z1_ref, b_ref, wa_ref, ws_ref, m_ref, gp_ref):
        zs = jnp.concatenate([z0_ref[...], z1_ref[...]], axis=1)
        ygv = yg_ref[...]
        gpre = jnp.dot(ygv.astype(BF16), wg_ref[...], preferred_element_type=F32)
        gp_ref[...] = gpre
        os_ = ygv * _sigmoid(gpre + b_ref[...]) * _silu(zs)
        ogv = og_ref[...]
        ra = lax.rsqrt(jnp.mean(ogv * ogv, axis=-1, keepdims=True) + NORM_EPS)
        rs = lax.rsqrt(jnp.mean(os_ * os_, axis=-1, keepdims=True) + NORM_EPS)
        m_ref[:, :ATTN_W] = (ogv * ra * wa_ref[...]).astype(BF16)
        m_ref[:, ATTN_W:] = (os_ * rs * ws_ref[...]).astype(BF16)

    row = lambda w: pl.BlockSpec((1, w), lambda i: (0, 0))
    return pl.pallas_call(
        body,
        name="merge",
        grid=(L // tm,),
        in_specs=[pl.BlockSpec((tm, ATTN_W), lambda i: (i, 0)), pl.BlockSpec((tm, SSM_W), lambda i: (i, 0)),
                  pl.BlockSpec((SSM_W, SSM_W), lambda i: (0, 0)),
                  pl.BlockSpec((tm, 512), lambda i: (i, 7)), pl.BlockSpec((tm, 512), lambda i: (i, 8)),
                  row(SSM_W), row(ATTN_W), row(SSM_W)],
        out_specs=[pl.BlockSpec((tm, D_MODEL), lambda i: (i, 0)), pl.BlockSpec((tm, SSM_W), lambda i: (i, 0))],
        out_shape=[jax.ShapeDtypeStruct((L, D_MODEL), BF16), jax.ShapeDtypeStruct((L, SSM_W), F32)],
        compiler_params=_cp(("parallel",)),
    )(og, yg, w_glu, proj, proj, b_glu.reshape(1, SSM_W), wa.reshape(1, ATTN_W), ws.reshape(1, SSM_W))


def _outproj_loss(merged, w_out, x, target):
    L = x.shape[0]
    tm, tn = _tile(L, 512), 1024
    ni, nj = L // tm, D_MODEL // tn

    def body(m_ref, w_ref, x_ref, t_ref, d_ref, db_ref, l_ref):
        out = x_ref[...] + jnp.dot(m_ref[...], w_ref[...], preferred_element_type=F32)
        diff = out - t_ref[...]
        d = diff * (1.0 / D_MODEL)
        d_ref[...] = d
        db_ref[...] = d.astype(BF16)
        l_ref[...] = jnp.full((1, 8, 128), jnp.sum(diff * diff), F32)

    return pl.pallas_call(
        body,
        name="outproj_loss",
        grid=(nj, ni),
        in_specs=[pl.BlockSpec((tm, D_MODEL), lambda j, i: (i, 0)),
                  pl.BlockSpec((D_MODEL, tn), lambda j, i: (0, j)),
                  pl.BlockSpec((tm, tn), lambda j, i: (i, j)),
                  pl.BlockSpec((tm, tn), lambda j, i: (i, j))],
        out_specs=[pl.BlockSpec((tm, tn), lambda j, i: (i, j)), pl.BlockSpec((tm, tn), lambda j, i: (i, j)),
                   pl.BlockSpec((1, 8, 128), lambda j, i: (i * nj + j, 0, 0))],
        out_shape=[jax.ShapeDtypeStruct((L, D_MODEL), F32), jax.ShapeDtypeStruct((L, D_MODEL), BF16),
                   jax.ShapeDtypeStruct((ni * nj, 8, 128), F32)],
        compiler_params=_cp(("parallel", "parallel")),
    )(merged, w_out, x, target)


def _merge_bwd(d_out_b, w_out, og, o, yg, gpre, proj, b_glu, wa, ws):
    L = og.shape[0]
    tm = _tile(L, 256)

    def body(dout_ref, wo_ref, og_ref, o_ref, yg_ref, gp_ref, za0_ref, za1_ref, zs0_ref, zs1_ref, b_ref, wa_ref,
             ws_ref, do_ref, dza_ref, dzs_ref, dg_ref, dyg_ref, gwa_ref, gws_ref, gb_ref):
        i = pl.program_id(0)

        @pl.when(i == 0)
        def _():
            gwa_ref[...] = jnp.zeros_like(gwa_ref)
            gws_ref[...] = jnp.zeros_like(gws_ref)
            gb_ref[...] = jnp.zeros_like(gb_ref)

        dm = lax.dot_general(dout_ref[...], wo_ref[...], _NT, preferred_element_type=F32)
        za = jnp.concatenate([za0_ref[...], za1_ref[...]], axis=1)
        zs = jnp.concatenate([zs0_ref[...], zs1_ref[...]], axis=1)
        ogv, dma = og_ref[...], dm[:, :ATTN_W]
        ra = lax.rsqrt(jnp.mean(ogv * ogv, axis=-1, keepdims=True) + NORM_EPS)
        xh = ogv * ra
        gwa_ref[...] += jnp.sum(dma * xh, axis=0, keepdims=True)
        gx = dma * wa_ref[...]
        d_og = ra * (gx - xh * jnp.mean(gx * xh, axis=-1, keepdims=True))
        do_ref[...] = d_og * _silu(za)
        dza_ref[...] = (d_og * o_ref[...] * _dsilu(za)).astype(BF16)
        ygv = yg_ref[...]
        sg = _sigmoid(gp_ref[...] + b_ref[...])
        y2 = ygv * sg
        sz = _silu(zs)
        os_ = y2 * sz
        dms = dm[:, ATTN_W:]
        rs = lax.rsqrt(jnp.mean(os_ * os_, axis=-1, keepdims=True) + NORM_EPS)
        xs = os_ * rs
        gws_ref[...] += jnp.sum(dms * xs, axis=0, keepdims=True)
        gxs = dms * ws_ref[...]
        d_os = rs * (gxs - xs * jnp.mean(gxs * xs, axis=-1, keepdims=True))
        dzs_ref[...] = (d_os * y2 * _dsilu(zs)).astype(BF16)
        d_y2 = d_os * sz
        d_g = d_y2 * ygv * sg * (1.0 - sg)
        dg_ref[...] = d_g.astype(BF16)
        gb_ref[...] += jnp.sum(d_g, axis=0, keepdims=True)
        dyg_ref[...] = d_y2 * sg

    row = lambda w: pl.BlockSpec((1, w), lambda i: (0, 0))
    full = lambda w: pl.BlockSpec((tm, w), lambda i: (i, 0))
    half = lambda c: pl.BlockSpec((tm, 512), lambda i: (i, c))
    return pl.pallas_call(
        body,
        name="merge_bwd",
        grid=(L // tm,),
        in_specs=[full(D_MODEL), pl.BlockSpec((D_MODEL, D_MODEL), lambda i: (0, 0)),
                  full(ATTN_W), full(ATTN_W), full(SSM_W), full(SSM_W),
                  half(3), half(4), half(7), half(8), row(SSM_W), row(ATTN_W), row(SSM_W)],
        out_specs=[full(ATTN_W), full(ATTN_W), full(SSM_W), full(SSM_W), full(SSM_W),
                   row(ATTN_W), row(SSM_W), row(SSM_W)],
        out_shape=[jax.ShapeDtypeStruct((L, ATTN_W), F32), jax.ShapeDtypeStruct((L, ATTN_W), BF16),
                   jax.ShapeDtypeStruct((L, SSM_W), BF16), jax.ShapeDtypeStruct((L, SSM_W), BF16),
                   jax.ShapeDtypeStruct((L, SSM_W), F32),
                   jax.ShapeDtypeStruct((1, ATTN_W), F32), jax.ShapeDtypeStruct((1, SSM_W), F32),
                   jax.ShapeDtypeStruct((1, SSM_W), F32)],
        compiler_params=_cp(("arbitrary",)),
    )(d_out_b, w_out, og, o, yg, gpre, proj, proj, proj, proj, b_glu.reshape(1, SSM_W), wa.reshape(1, ATTN_W),
      ws.reshape(1, SSM_W))


def _rms_bwd_x(x, norm_w, d_hn, d_out, ride):
    L = x.shape[0]
    tm = _tile(L, 256)

    def body(x_ref, w_ref, dh_ref, do_ref, gx_ref, gw_ref):
        i = pl.program_id(0)

        @pl.when(i == 0)
        def _():
            gw_ref[...] = jnp.zeros_like(gw_ref)

        xv, dh = x_ref[...], dh_ref[...]
        r = lax.rsqrt(jnp.mean(xv * xv, axis=-1, keepdims=True) + NORM_EPS)
        xh = xv * r
        gw_ref[...] += jnp.sum(dh * xh, axis=0, keepdims=True)
        gx = dh * w_ref[...]
        gx_ref[...] = do_ref[...] + r * (gx - xh * jnp.mean(gx * xh, axis=-1, keepdims=True))

    blk = pl.BlockSpec((tm, D_MODEL), lambda i: (i, 0))
    row = pl.BlockSpec((1, D_MODEL), lambda i: (0, 0))
    return _call(body, "rms_bwd_x", (L // tm,), [blk, row, blk, blk], [blk, row],
                 [jax.ShapeDtypeStruct((L, D_MODEL), F32), jax.ShapeDtypeStruct((1, D_MODEL), F32)],
                 (x, norm_w.reshape(1, D_MODEL), d_hn, d_out), ride=ride)


def _rope_table(positions):
    inv_freq = ROPE_THETA ** (-jnp.arange(0, HEAD_DIM, 2, dtype=F32) / HEAD_DIM)
    ang = positions.astype(F32)[:, None] * inv_freq
    sign = jnp.where(jnp.arange(128) % HEAD_DIM < HEAD_DIM // 2, -1.0, 1.0)
    return jnp.concatenate([jnp.tile(jnp.cos(ang), (1, 4)), jnp.tile(jnp.sin(ang), (1, 4)) * sign], axis=1)


def _step(x, positions, target, w, core, chip):
    small = {n: w[n] for n in _SMALL}
    tab = _rope_table(positions)
    mt_b, scat_b, ocat_b, a16 = _ssm_prep(small)
    perm = _chunk_perm()
    blocks = lambda t: t.reshape(N_DEV, t.shape[0] // N_DEV, t.shape[1])

    proj, hn, wt_in = _rms_inproj_gather(x, small["norm_w"], w["w_in"].T.astype(BF16), chip)
    wt_in = wt_in.reshape(IN_W, D_MODEL)
    q_rot, k_rot = _qk_prep(proj, tab, small["q_norm_w"], small["k_norm_w"])
    (og, o, lse), (w_glu,) = _attn_fwd(q_rot, k_rot, proj, small["sinks"],
                                       _gather_exchange([w["w_glu"].astype(BF16)]))
    (y, yg, hx), (w_out,) = _ssm_fwd(proj, perm, mt_b, scat_b, ocat_b, a16, small["d_skip"],
                                     _gather_exchange([w["w_out"].astype(BF16)]))
    w_glu, w_out = w_glu.reshape(SSM_W, SSM_W), w_out.reshape(D_MODEL, D_MODEL)
    merged, gpre = _merge(og, yg, w_glu, proj, small["b_glu"], small["attn_out_norm_w"], small["ssm_out_norm_w"])
    d_out, d_out_b, loss_parts = _outproj_loss(merged, w_out, x, target)
    loss = 0.5 * jnp.sum(loss_parts[:, 0, 0]) / D_MODEL

    g_w_out = blocks(_mm(merged, d_out_b, "tn", F32, "grad_w_out", tm=1024))
    d_o, d_za, d_zs, d_g, d_yg1, g_wa, g_ws, g_bglu = _merge_bwd(
        d_out_b, w_out, og, o, yg, gpre, proj, small["b_glu"], small["attn_out_norm_w"], small["ssm_out_norm_w"])
    g_w_glu = blocks(_mm(yg, d_g, "tn", F32, "grad_w_glu"))
    d_yg = _mm(d_g, w_glu, "nt", F32, "d_yg", add=d_yg1)
    (d_u, g_mt, g_scat, g_ocat, g_a16, g_dskip), (ra_out, ra_glu) = _ssm_bwd(
        d_yg, y, proj, hx, perm, mt_b, scat_b, ocat_b, a16, small["d_skip"], _pair_exchange([g_w_out, g_w_glu]))
    p_out = _pair_sum(g_w_out, ra_out, core, BF16, "pair_sum_out")
    p_glu = _pair_sum(g_w_glu, ra_glu, core, BF16, "pair_sum_glu")
    (d_q, d_k, d_v, g_sinks), (rb_out, rb_glu) = _attn_bwd(
        q_rot, k_rot, proj, small["sinks"], d_o, o, lse, _chip_exchange([p_out, p_glu]))
    d_proj, g_qw, g_kw = _qk_prep_bwd(proj, tab, small["q_norm_w"], small["k_norm_w"], d_q, d_k, d_v,
                                      d_za, d_u, d_zs)
    g_qw = g_qw[0, :HEAD_DIM] + g_qw[0, HEAD_DIM:]
    g_kw = g_kw[0, :HEAD_DIM] + g_kw[0, HEAD_DIM:]
    g_in_a = blocks(_mm(d_proj, hn, "tn", F32, "grad_w_in_a", tm=1152, panel=0))
    g_in_b, (ra_a,) = _mm(d_proj, hn, "tn", F32, "grad_w_in_b", tm=1152, panel=1, ride=_pair_exchange([g_in_a]))
    g_in_b = blocks(g_in_b)
    p_a = _pair_sum(g_in_a, ra_a, core, BF16, "pair_sum_in_a")
    d_hn, (rb_a, ra_b) = _mm(d_proj, wt_in, "nn", F32, "d_hn", tm=1024,
                             ride=_both(_chip_exchange([p_a]), _pair_exchange([g_in_b])))
    p_b = _pair_sum(g_in_b, ra_b, core, BF16, "pair_sum_in_b")
    g_small, (rb_b,) = _ssm_prep_bwd(small, g_mt, g_scat, g_ocat, g_a16, _chip_exchange([p_b]))
    (grad_x, g_nw), _ = _rms_bwd_x(x, small["norm_w"], d_hn, d_out, None)

    g_small.update(norm_w=g_nw.reshape(-1), q_norm_w=g_qw.reshape(-1), k_norm_w=g_kw.reshape(-1),
                   sinks=g_sinks[0, :N_HEADS], d_skip=g_dskip.reshape(-1), b_glu=g_bglu.reshape(-1),
                   attn_out_norm_w=g_wa.reshape(-1), ssm_out_norm_w=g_ws.reshape(-1))
    g_packed = _slab_all_reduce(_pack(g_small, loss).reshape(N_DEV, _PACK_ROWS // N_DEV, 128))
    g_packed = g_packed.reshape(_PACK_ROWS, 128)
    grads = _unpack(g_packed, w)
    parts = dict(w_in=([p_a, p_b], [rb_a, rb_b]), w_glu=([p_glu], [rb_glu]), w_out=([p_out], [rb_out]))
    return g_packed[_LOSS_ROW, 0], grad_x, grads, parts


_ANY = pl.BlockSpec(memory_space=pl.ANY)


class _Exchange:
    def __init__(self, arrays, out_shape, sems, start, finish, relay=None):
        self.arrays, self.out_shape, self.sems, self.start, self.finish = arrays, out_shape, sems, start, finish
        self.relay = relay if relay is not None else (lambda ins, outs, sems: None)


def _gather_exchange(blocks):
    n = len(blocks)

    def parts(ins, outs, sems):
        send_sems, recv_sems, local_sems = sems
        x, y, c = lax.axis_index("x"), lax.axis_index("y"), lax.axis_index("c")
        me, sibling = (x, y, c), (x, y, 1 - c)
        chips = [(1 - x, y), (x, 1 - y), (1 - x, 1 - y)]

        def slot(k, dev):
            return outs[k].at[4 * dev[0] + 2 * dev[1] + dev[2]]

        def copy(k, q, block, to, src=None):
            return pltpu.make_async_remote_copy(
                src_ref=slot(k, block) if src is None else src, dst_ref=slot(k, block),
                send_sem=send_sems.at[k, q], recv_sem=recv_sems.at[k, q], device_id=to, device_id_type=MESH)

        mine = [pltpu.make_async_copy(ins[k], slot(k, me), local_sems.at[k]) for k in range(n)]
        first = []
        for k in range(n):
            first.append(copy(k, 0, me, sibling, src=ins[k]))
            first += [copy(k, 1 + j, me, (*chip, c), src=ins[k]) for j, chip in enumerate(chips)]
        return me, sibling, chips, c, copy, mine, first

    def start(ins, outs, sems):
        *_, mine, first = parts(ins, outs, sems)
        for cp in mine + first:
            cp.start()

    def relay(ins, outs, sems):
        me, sibling, chips, c, copy, _, _ = parts(ins, outs, sems)
        for j, chip in enumerate(chips):
            for k in range(n):
                copy(k, 1 + j, (*chip, c), me).wait_recv()
                copy(k, 4 + j, (*chip, c), sibling).start()

    def finish(ins, outs, sems):
        me, sibling, chips, c, copy, mine, first = parts(ins, outs, sems)
        for k in range(n):
            copy(k, 0, sibling, me).wait_recv()
            for j, chip in enumerate(chips):
                copy(k, 4 + j, (*chip, 1 - c), me).wait_recv()
        for cp in first + [copy(k, 4 + j, (*chip, c), sibling) for k in range(n) for j, chip in enumerate(chips)]:
            cp.wait_send()
        for cp in mine:
            cp.wait()

    return _Exchange(blocks, [jax.ShapeDtypeStruct((N_DEV,) + b.shape, b.dtype) for b in blocks],
                     [pltpu.SemaphoreType.DMA((n, 7)), pltpu.SemaphoreType.DMA((n, 7)), pltpu.SemaphoreType.DMA((n,))],
                     start, finish, relay)


def _direct_exchange(arrays, out_lead, fan, route):
    n = len(arrays)

    def copies(ins, outs, sems):
        send_sems, recv_sems = sems
        legs = route(lax.axis_index("x"), lax.axis_index("y"), lax.axis_index("c"))
        return [pltpu.make_async_remote_copy(
            src_ref=ins[k].at[src], dst_ref=outs[k].at[q], send_sem=send_sems.at[k, q], recv_sem=recv_sems.at[k, q],
            device_id=to, device_id_type=MESH) for k in range(n) for src, q, to in legs]

    def start(ins, outs, sems):
        for cp in copies(ins, outs, sems):
            cp.start()

    def finish(ins, outs, sems):
        for cp in copies(ins, outs, sems):
            cp.wait()

    return _Exchange(arrays, [jax.ShapeDtypeStruct((out_lead,) + a.shape[1:], a.dtype) for a in arrays],
                     [pltpu.SemaphoreType.DMA((n, fan)), pltpu.SemaphoreType.DMA((n, fan))], start, finish)


def _pair_exchange(grads):
    return _direct_exchange(grads, 4, 4, lambda x, y, c: [(2 * chip + (1 - c), chip, (x, y, 1 - c))
                                                          for chip in range(4)])


def _chip_exchange(parts):
    def route(x, y, c):
        chips = [(1 - x, y), (x, 1 - y), (1 - x, 1 - y)]
        return [(2 * chip[0] + chip[1], q, (*chip, c)) for q, chip in enumerate(chips)]
    return _direct_exchange(parts, 3, 3, route)


def _both(ex1, ex2):
    n1, s1 = len(ex1.arrays), len(ex1.sems)

    def halves(ins, outs, sems):
        return (ins[:n1], outs[:n1], sems[:s1]), (ins[n1:], outs[n1:], sems[s1:])

    def start(ins, outs, sems):
        h1, h2 = halves(ins, outs, sems)
        ex1.start(*h1)
        ex2.start(*h2)

    def relay(ins, outs, sems):
        h1, h2 = halves(ins, outs, sems)
        ex1.relay(*h1)
        ex2.relay(*h2)

    def finish(ins, outs, sems):
        h1, h2 = halves(ins, outs, sems)
        ex1.finish(*h1)
        ex2.finish(*h2)

    return _Exchange(list(ex1.arrays) + list(ex2.arrays), list(ex1.out_shape) + list(ex2.out_shape),
                     list(ex1.sems) + list(ex2.sems), start, finish, relay)


def _call(body, name, grid, in_specs, out_specs, out_shape, args, scratch_shapes=(), ride=None):
    if ride is None:
        sem = ("arbitrary",) * len(grid)
        return pl.pallas_call(body, name=name, grid=grid, in_specs=in_specs, out_specs=out_specs, out_shape=out_shape,
                              scratch_shapes=list(scratch_shapes), compiler_params=_cp(sem))(*args), None
    n_in, n_out, n_scr, n_x = len(in_specs), len(out_specs), len(scratch_shapes), len(ride.arrays)

    def wrapped(*refs):
        ins, refs = refs[:n_in], refs[n_in:]
        x_in, refs = refs[:n_x], refs[n_x:]
        outs, refs = refs[:n_out], refs[n_out:]
        x_out, refs = refs[:n_x], refs[n_x:]
        scr, sems = refs[:n_scr], refs[n_scr:]
        step, total = pl.program_id(0), grid[0]
        for a in range(1, len(grid)):
            step, total = step * grid[a] + pl.program_id(a), total * grid[a]
        @pl.when(step == 0)
        def _():
            ride.start(x_in, x_out, sems)

        @pl.when(step == max(total - 2, 0))
        def _():
            ride.relay(x_in, x_out, sems)

        body(*ins, *outs, *scr)

        @pl.when(step == total - 1)
        def _():
            ride.finish(x_in, x_out, sems)

    res = pl.pallas_call(
        wrapped, name=name, grid=grid, in_specs=list(in_specs) + [_ANY] * n_x,
        out_specs=list(out_specs) + [_ANY] * n_x, out_shape=list(out_shape) + list(ride.out_shape),
        scratch_shapes=list(scratch_shapes) + list(ride.sems),
        compiler_params=_cp(("arbitrary",) * len(grid)))(*args, *ride.arrays)
    return res[:n_out], list(res[n_out:])


def _pair_sum(g, ra, core, out_dtype, name):
    _, r, C = g.shape
    tr = _tile(r, 576)

    def body(c_ref, g_ref, ra_ref, p_ref):
        p_ref[...] = (g_ref[...] + ra_ref[...]).astype(p_ref.dtype)

    return pl.pallas_call(
        body,
        name=name,
        grid_spec=pltpu.PrefetchScalarGridSpec(
            num_scalar_prefetch=1,
            grid=(4, r // tr),
            in_specs=[pl.BlockSpec((1, tr, C), lambda j, t, c_ref: (2 * j + c_ref[0], t, 0)),
                      pl.BlockSpec((1, tr, C), lambda j, t, c_ref: (j, t, 0))],
            out_specs=pl.BlockSpec((1, tr, C), lambda j, t, c_ref: (j, t, 0)),
        ),
        out_shape=jax.ShapeDtypeStruct((4, r, C), out_dtype),
        compiler_params=_cp(("parallel", "parallel")),
    )(core, g, ra)


def _slab_all_reduce(slab):
    _, r, lanes = slab.shape

    def body(s_ref, o_ref, ra, rb, ps, sems_a, sems_b, sems_c):
        x, y, c = lax.axis_index("x"), lax.axis_index("y"), lax.axis_index("c")
        chips = [(1 - x, y), (x, 1 - y), (1 - x, 1 - y)]
        pair = [pltpu.make_async_remote_copy(
            src_ref=s_ref.at[2 * k + (1 - c)], dst_ref=ra.at[k], send_sem=sems_a.at[0, k], recv_sem=sems_a.at[1, k],
            device_id=(x, y, 1 - c), device_id_type=MESH) for k in range(4)]
        for cp in pair:
            cp.start()
        for cp in pair:
            cp.wait()
        for k in range(4):
            ps[k] = s_ref[2 * k + c] + ra[k]
        cross = [pltpu.make_async_remote_copy(
            src_ref=ps.at[2 * ch[0] + ch[1]], dst_ref=rb.at[q], send_sem=sems_b.at[0, q], recv_sem=sems_b.at[1, q],
            device_id=(*ch, c), device_id_type=MESH) for q, ch in enumerate(chips)]
        for cp in cross:
            cp.start()
        for cp in cross:
            cp.wait()
        me = 4 * x + 2 * y + c
        o_ref[me] = ((ps[2 * x + y] + rb[0]) + rb[1]) + rb[2]
        flips = [(dx, dy, dc) for dx in (0, 1) for dy in (0, 1) for dc in (0, 1) if dx + dy + dc]
        spread = [pltpu.make_async_remote_copy(
            src_ref=o_ref.at[me], dst_ref=o_ref.at[me], send_sem=sems_c.at[0, q], recv_sem=sems_c.at[1, q],
            device_id=(x + dx - 2 * x * dx, y + dy - 2 * y * dy, c + dc - 2 * c * dc), device_id_type=MESH)
            for q, (dx, dy, dc) in enumerate(flips)]
        for cp in spread:
            cp.start()
        for q, (dx, dy, dc) in enumerate(flips):
            peer = 4 * (x + dx - 2 * x * dx) + 2 * (y + dy - 2 * y * dy) + (c + dc - 2 * c * dc)
            pltpu.make_async_remote_copy(
                src_ref=o_ref.at[peer], dst_ref=o_ref.at[peer], send_sem=sems_c.at[0, q], recv_sem=sems_c.at[1, q],
                device_id=(x, y, c), device_id_type=MESH).wait_recv()
        for cp in spread:
            cp.wait_send()

    whole = pl.BlockSpec(memory_space=pltpu.VMEM)
    return pl.pallas_call(
        body, name="slab_all_reduce", in_specs=[whole], out_specs=whole,
        out_shape=jax.ShapeDtypeStruct(slab.shape, F32),
        scratch_shapes=[pltpu.VMEM((4, r, lanes), F32), pltpu.VMEM((3, r, lanes), F32), pltpu.VMEM((4, r, lanes), F32),
                        pltpu.SemaphoreType.DMA((2, 4)), pltpu.SemaphoreType.DMA((2, 3)),
                        pltpu.SemaphoreType.DMA((2, 7))],
        compiler_params=_cp(),
    )(slab)


def _adamw_reduced(ps, rbs, chip, w, m, v, name):
    nh = len(ps)
    R, C = w.shape
    ch = C // nh
    tr = _tile(R, 288)
    nt = R // tr
    c1 = 1.0 - ADAM_B1 ** ADAM_STEP
    c2 = 1.0 - ADAM_B2 ** ADAM_STEP

    def body(c_ref, *refs):
        p_refs, rb_refs = refs[:nh], refs[nh:2 * nh]
        w_ref, m_ref, v_ref, g_ref, d_ref, nm_ref, nv_ref = refs[2 * nh:]
        for h in range(nh):
            @pl.when(pl.program_id(0) == h)
            def _(h=h):
                rb = rb_refs[h]
                gv = p_refs[h][0].astype(F32) + rb[0].astype(F32)
                gv = gv + rb[1].astype(F32)
                gv = gv + rb[2].astype(F32)
                nm = ADAM_B1 * m_ref[...] + (1.0 - ADAM_B1) * gv
                nv = ADAM_B2 * v_ref[...] + (1.0 - ADAM_B2) * (gv * gv)
                g_ref[...] = gv
                nm_ref[...] = nm
                nv_ref[...] = nv
                d_ref[...] = -ADAM_LR * ((nm / c1) / (jnp.sqrt(nv / c2) + ADAM_EPS) + ADAM_WD * w_ref[...])

    def held(h):
        return lambda hh, tt: jnp.where(hh == h, tt, jnp.where(hh < h, 0, nt - 1))

    p_specs = [pl.BlockSpec((1, tr, ch), lambda hh, tt, c_ref, f=held(h): (c_ref[0], f(hh, tt), 0))
               for h in range(nh)]
    rb_specs = [pl.BlockSpec((3, tr, ch), lambda hh, tt, c_ref, f=held(h): (0, f(hh, tt), 0)) for h in range(nh)]
    blk = pl.BlockSpec((tr, ch), lambda hh, tt, c_ref: (tt, hh))
    return pl.pallas_call(
        body,
        name=name,
        grid_spec=pltpu.PrefetchScalarGridSpec(
            num_scalar_prefetch=1, grid=(nh, nt), in_specs=p_specs + rb_specs + [blk] * 3, out_specs=[blk] * 4),
        out_shape=[jax.ShapeDtypeStruct((R, C), F32)] * 4,
        compiler_params=_cp(("arbitrary", "arbitrary")),
    )(chip, *ps, *rbs, w, m, v)


_SMALL = ("norm_w", "q_norm_w", "k_norm_w", "sinks", "a_re", "a_im", "log_step", "b_re", "b_im", "c_re", "c_im",
          "d_skip", "b_glu", "attn_out_norm_w", "ssm_out_norm_w")
_WEIGHTS = ("norm_w", "w_in", "q_norm_w", "k_norm_w", "sinks", "a_re", "a_im", "log_step", "b_re", "b_im", "c_re",
            "c_im", "d_skip", "w_glu", "b_glu", "attn_out_norm_w", "ssm_out_norm_w", "w_out")
_SMALL_2D = dict(norm_w=(1, 2048), q_norm_w=(1, 64), k_norm_w=(1, 64), sinks=(1, 16), a_re=(64, 64), a_im=(64, 64),
                 log_step=(1, 64), b_re=(1024, 64), b_im=(1024, 64), c_re=(1024, 64), c_im=(1024, 64),
                 d_skip=(1, 1024), b_glu=(1, 1024), attn_out_norm_w=(1, 1024), ssm_out_norm_w=(1, 1024))
_P_MINOR = ("b_re", "b_im")


def _flat_form(n, t):
    return t.transpose(0, 2, 1) if n in _P_MINOR else t


def _own_form(n, t, shape):
    if n in _P_MINOR:
        return t.reshape(shape[0], shape[2], shape[1]).transpose(0, 2, 1)
    return t.reshape(shape)


def _slab_rows(n):
    return -(-n // 1024) * 8


_PACK_ROWS = 2304


_LOSS_ROW = 2192


def _pack(d, loss):
    parts = []
    for n in _SMALL:
        flat = _flat_form(n, d[n]).reshape(-1).astype(F32)
        rows = _slab_rows(flat.shape[0])
        parts.append(jnp.pad(flat, (0, rows * 128 - flat.shape[0])).reshape(rows, 128))
    assert sum(p.shape[0] for p in parts) == _LOSS_ROW
    parts.append(jnp.pad(loss.reshape(1, 1), ((0, _PACK_ROWS - _LOSS_ROW - 1), (0, 127))))
    return jnp.concatenate(parts, axis=0)


def _unpack(packed, like):
    out, off = {}, 0
    for n in _SMALL:
        size = math.prod(like[n].shape)
        rows = _slab_rows(size)
        out[n] = _own_form(n, packed[off:off + rows].reshape(-1)[:size], like[n].shape)
        off += rows
    return out


def _adamw_small(g, w, m, v):
    c1 = 1.0 - ADAM_B1 ** ADAM_STEP
    c2 = 1.0 - ADAM_B2 ** ADAM_STEP
    k = len(_SMALL)

    def body(*refs):
        ins, outs = refs[:4 * k], refs[4 * k:]
        for j in range(k):
            gv, wv, mv, vv = (ins[q * k + j][...] for q in range(4))
            nm = ADAM_B1 * mv + (1.0 - ADAM_B1) * gv
            nv = ADAM_B2 * vv + (1.0 - ADAM_B2) * (gv * gv)
            outs[j][...] = -ADAM_LR * ((nm / c1) / (jnp.sqrt(nv / c2) + ADAM_EPS) + ADAM_WD * wv)
            outs[k + j][...] = nm
            outs[2 * k + j][...] = nv

    args = [_flat_form(n, d[n]).reshape(_SMALL_2D[n]) for d in (g, w, m, v) for n in _SMALL]
    shapes = [jax.ShapeDtypeStruct(_SMALL_2D[n], F32) for _ in range(3) for n in _SMALL]
    outs = pl.pallas_call(body, name="adamw_small", out_shape=shapes, compiler_params=_cp())(*args)
    res = []
    for q in range(3):
        res.append({n: _own_form(n, outs[q * k + j], w[n].shape) for j, n in enumerate(_SMALL)})
    return res


def kernel(x, positions, norm_w, w_in, q_norm_w, k_norm_w, sinks, a_re, a_im, log_step, b_re, b_im, c_re, c_im, d_skip, w_glu, b_glu, attn_out_norm_w, ssm_out_norm_w, w_out, loss_target, m_norm_w, m_w_in, m_q_norm_w, m_k_norm_w, m_sinks, m_a_re, m_a_im, m_log_step, m_b_re, m_b_im, m_c_re, m_c_im, m_d_skip, m_w_glu, m_b_glu, m_attn_out_norm_w, m_ssm_out_norm_w, m_w_out, v_norm_w, v_w_in, v_q_norm_w, v_k_norm_w, v_sinks, v_a_re, v_a_im, v_log_step, v_b_re, v_b_im, v_c_re, v_c_im, v_d_skip, v_w_glu, v_b_glu, v_attn_out_norm_w, v_ssm_out_norm_w, v_w_out):
    w = dict(norm_w=norm_w, w_in=w_in, q_norm_w=q_norm_w, k_norm_w=k_norm_w, sinks=sinks, a_re=a_re, a_im=a_im,
             log_step=log_step, b_re=b_re, b_im=b_im, c_re=c_re, c_im=c_im, d_skip=d_skip, w_glu=w_glu, b_glu=b_glu,
             attn_out_norm_w=attn_out_norm_w, ssm_out_norm_w=ssm_out_norm_w, w_out=w_out)
    m = dict(norm_w=m_norm_w, w_in=m_w_in, q_norm_w=m_q_norm_w, k_norm_w=m_k_norm_w, sinks=m_sinks, a_re=m_a_re,
             a_im=m_a_im, log_step=m_log_step, b_re=m_b_re, b_im=m_b_im, c_re=m_c_re, c_im=m_c_im, d_skip=m_d_skip,
             w_glu=m_w_glu, b_glu=m_b_glu, attn_out_norm_w=m_attn_out_norm_w, ssm_out_norm_w=m_ssm_out_norm_w,
             w_out=m_w_out)
    v = dict(norm_w=v_norm_w, w_in=v_w_in, q_norm_w=v_q_norm_w, k_norm_w=v_k_norm_w, sinks=v_sinks, a_re=v_a_re,
             a_im=v_a_im, log_step=v_log_step, b_re=v_b_re, b_im=v_b_im, c_re=v_c_re, c_im=v_c_im, d_skip=v_d_skip,
             w_glu=v_w_glu, b_glu=v_b_glu, attn_out_norm_w=v_attn_out_norm_w, ssm_out_norm_w=v_ssm_out_norm_w,
             w_out=v_w_out)
    core = lax.axis_index("c").astype(jnp.int32).reshape(1)
    chip = (2 * lax.axis_index("x") + lax.axis_index("y")).astype(jnp.int32).reshape(1)

    loss, grad_x, grads, parts = _step(x[0], positions[0], loss_target[0], w, core, chip)
    delta, new_m, new_v = {}, {}, {}
    for n in ("w_glu", "w_out"):
        grads[n], delta[n], new_m[n], new_v[n] = _adamw_reduced(*parts[n], chip, w[n], m[n], v[n], f"adamw_{n}")
    g_t, d_t, m_t, v_t = _adamw_reduced(*parts["w_in"], chip, w["w_in"].T, m["w_in"].T, v["w_in"].T, "adamw_w_in")
    grads["w_in"], delta["w_in"], new_m["w_in"], new_v["w_in"] = g_t.T, d_t.T, m_t.T, v_t.T
    d_s, m_s, v_s = _adamw_small(grads, w, m, v)
    delta.update(d_s)
    new_m.update(m_s)
    new_v.update(v_s)

    return (loss, grad_x[None], *[grads[n] for n in _WEIGHTS], *[delta[n] for n in _WEIGHTS],
            *[new_m[n] for n in _WEIGHTS], *[new_v[n] for n in _WEIGHTS])
```

```python
import math

import jax
import jax.numpy as jnp
from jax import lax
from jax.experimental import pallas as pl
from jax.experimental.pallas import tpu as pltpu

F32 = jnp.float32
BF16 = jnp.bfloat16

D_MODEL = 2048
ATTN_W = 1024
KV_W = 256
SSM_W = 1024
HEAD_DIM = 64
N_HEADS = 16
N_KV = 4
IN_W = 4608
BLOCK = 128
ROPE_THETA = 10000.0
NORM_EPS = 1e-6
SSM_G = 64
SSM_P = 64
SSM_H = 16
CHUNK = 16
CW = CHUNK * SSM_H
N_DEV = 8

ADAM_LR = 0.001
ADAM_B1 = 0.9
ADAM_B2 = 0.999
ADAM_EPS = 1e-08
ADAM_WD = 0.01
ADAM_STEP = 10

VMEM_LIMIT = 56 * 1024 * 1024
MESH = pl.DeviceIdType.MESH


def _cp(sem=None):
    if sem is None:
        return pltpu.CompilerParams(vmem_limit_bytes=VMEM_LIMIT)
    return pltpu.CompilerParams(vmem_limit_bytes=VMEM_LIMIT, dimension_semantics=sem)


def _sigmoid(x):
    return 0.5 * jnp.tanh(0.5 * x) + 0.5


def _silu(x):
    return x * _sigmoid(x)


def _dsilu(x):
    s = _sigmoid(x)
    return s * (1.0 + x * (1.0 - s))


_GELU_C = math.sqrt(2.0 / math.pi)


def _gelu(y):
    t = jnp.tanh(_GELU_C * (y + 0.044715 * y * y * y))
    return 0.5 * y * (1.0 + t)


def _dgelu(y):
    t = jnp.tanh(_GELU_C * (y + 0.044715 * y * y * y))
    return 0.5 * (1.0 + t) + 0.5 * y * (1.0 - t * t) * _GELU_C * (1.0 + 3.0 * 0.044715 * y * y)


def _tile(n, want):
    if n <= want:
        return n
    for t in range(want - want % 16, 0, -16):
        if n % t == 0:
            return t
    raise ValueError((n, want))


def _mm(a, b, mode, out_dtype, name, tm=512, tn=1024, add=None, ride=None, panel=None):
    if mode == "nn":
        (M, K), (K2, N) = a.shape, b.shape
    elif mode == "nt":
        (M, K), (N, K2) = a.shape, b.shape
    else:
        (K, M), (K2, N) = a.shape, b.shape
    assert K == K2
    tm, tn = _tile(M, tm), _tile(N, tn)
    p0 = 0
    if panel is not None:
        assert mode != "nt" and add is None
        p0, N = panel, tn
    dn = {"nn": _NN, "nt": _NT, "tn": _TN}[mode]

    def body(a_ref, b_ref, *rest):
        o_ref = rest[-1]
        acc = lax.dot_general(a_ref[...].astype(BF16), b_ref[...].astype(BF16), dn, preferred_element_type=F32)
        if add is not None:
            acc = acc + rest[0][...]
        o_ref[...] = acc.astype(o_ref.dtype)

    a_spec = pl.BlockSpec((K, tm), lambda j, i: (0, i)) if mode == "tn" else pl.BlockSpec((tm, K), lambda j, i: (i, 0))
    b_spec = (pl.BlockSpec((tn, K), lambda j, i: (j, 0)) if mode == "nt"
              else pl.BlockSpec((K, tn), lambda j, i: (0, j + p0)))
    o_spec = pl.BlockSpec((tm, tn), lambda j, i: (i, j))
    extra = () if add is None else (add,)
    if ride is not None:
        (out,), landed = _call(body, name, (N // tn, M // tm), [a_spec, b_spec] + [o_spec] * len(extra), [o_spec],
                               [jax.ShapeDtypeStruct((M, N), out_dtype)], (a, b, *extra), ride=ride)
        return out, landed
    return pl.pallas_call(
        body,
        name=name,
        grid=(N // tn, M // tm),
        in_specs=[a_spec, b_spec] + [o_spec] * len(extra),
        out_specs=o_spec,
        out_shape=jax.ShapeDtypeStruct((M, N), out_dtype),
        compiler_params=_cp(("parallel", "parallel")),
    )(a, b, *extra)


_CHIP_ORDER = (0, 2, 1, 3)


def _rms_inproj_gather(x, norm_w, wt_shard, chip):
    L = x.shape[0]
    tm = _tile(L, 1024)
    ni = L // tm
    r = IN_W // N_DEV
    tn = 2 * r

    def body(chip_ref, x_ref, nw_ref, shard, proj_ref, hn_hbm, wt_hbm, hn_scr, w_scr, send_sems, recv_sems, loc_sems):
        jc, i = pl.program_id(0), pl.program_id(1)
        xx, yy, c = lax.axis_index("x"), lax.axis_index("y"), lax.axis_index("c")
        me, sibling = (xx, yy, c), (xx, yy, 1 - c)
        chips = [(1 - xx, yy), (xx, 1 - yy), (1 - xx, 1 - yy)]

        def slot(dev):
            return wt_hbm.at[4 * dev[0] + 2 * dev[1] + dev[2]]

        def copy(q, block, to, src=None):
            return pltpu.make_async_remote_copy(
                src_ref=slot(block) if src is None else src, dst_ref=slot(block),
                send_sem=send_sems.at[q], recv_sem=recv_sems.at[q], device_id=to, device_id_type=MESH)

        def rows_of(buf, core):
            return w_scr.at[buf, pl.ds(pl.multiple_of(core * r, 16), r)]

        mine = pltpu.make_async_copy(shard, slot(me), loc_sems.at[0])
        sends = [copy(0, me, sibling, src=shard)] + [copy(1 + j, me, (*ch, c), src=shard) for j, ch in enumerate(chips)]
        first = jnp.logical_and(jc == 0, i == 0)

        @pl.when(first)
        def _():
            mine.start()
            for cp in sends[:3]:
                cp.start()
            own = pltpu.make_async_copy(shard, rows_of(0, c), loc_sems.at[1])
            own.start()
            copy(0, sibling, me).wait_recv()
            sib = pltpu.make_async_copy(slot(sibling), rows_of(0, 1 - c), loc_sems.at[2])
            sib.start()
            own.wait()
            sib.wait()

        def take_direct(j, ch):
            copy(1 + j, (*ch, c), me).wait_recv()
            copy(4 + j, (*ch, c), sibling).start()
            if j == 0:
                sends[1].wait_send()
                sends[2].wait_send()
                sends[3].start()
            pltpu.make_async_copy(slot((*ch, c)), rows_of((1 + j) % 2, c), loc_sems.at[1]).start()

        for j, ch in enumerate(chips):
            early = jnp.logical_and(jc == j, i == ni // 2) if j > 0 else jnp.logical_and(jc == 1, i == 0)

            @pl.when(early)
            def _(j=j, ch=ch):
                take_direct(j, ch)

            @pl.when(jnp.logical_and(jc == 1 + j, i == 0))
            def _(j=j, ch=ch):
                buf = (1 + j) % 2
                copy(4 + j, (*ch, 1 - c), me).wait_recv()
                passed = pltpu.make_async_copy(slot((*ch, 1 - c)), rows_of(buf, 1 - c), loc_sems.at[2])
                passed.start()
                pltpu.make_async_copy(slot((*ch, c)), rows_of(buf, c), loc_sems.at[1]).wait()
                passed.wait()

        rows = pl.ds(pl.multiple_of(i * tm, tm), tm)

        @pl.when(jc == 0)
        def _():
            xv = x_ref[...]
            rstd = lax.rsqrt(jnp.mean(xv * xv, axis=-1, keepdims=True) + NORM_EPS)
            hn_scr[rows, :] = (xv * rstd * nw_ref[...]).astype(BF16)

        keep_hn = pltpu.make_async_copy(hn_scr, hn_hbm, loc_sems.at[3])

        @pl.when(jnp.logical_and(jc == 1, i == 0))
        def _():
            keep_hn.start()

        for buf in range(2):
            @pl.when(jc % 2 == buf)
            def _(buf=buf):
                proj_ref[...] = lax.dot_general(hn_scr[rows, :], w_scr[buf], _NT, preferred_element_type=F32)

        @pl.when(jnp.logical_and(jc == 3, i == ni - 1))
        def _():
            sends[0].wait_send()
            sends[3].wait_send()
            for j, ch in enumerate(chips):
                copy(4 + j, (*ch, c), sibling).wait_send()
            mine.wait()
            keep_hn.wait()

    def tile_of(jc, chip_ref):
        mask = jnp.where(jc == 1, _CHIP_ORDER[1], jnp.where(jc == 2, _CHIP_ORDER[2], jnp.where(jc == 3, _CHIP_ORDER[3], 0)))
        return jnp.bitwise_xor(chip_ref[0], mask)

    held = lambda jc, i: jnp.where(jc == 0, i, ni - 1)
    return pl.pallas_call(
        body,
        name="rms_inproj_gather",
        grid_spec=pltpu.PrefetchScalarGridSpec(
            num_scalar_prefetch=1,
            grid=(4, ni),
            in_specs=[pl.BlockSpec((tm, D_MODEL), lambda jc, i, ch: (held(jc, i), 0)),
                      pl.BlockSpec((1, D_MODEL), lambda jc, i, ch: (0, 0)), _ANY],
            out_specs=[pl.BlockSpec((tm, tn), lambda jc, i, ch: (i, tile_of(jc, ch))), _ANY, _ANY],
            scratch_shapes=[pltpu.VMEM((L, D_MODEL), BF16), pltpu.VMEM((2, tn, D_MODEL), BF16),
                            pltpu.SemaphoreType.DMA((7,)), pltpu.SemaphoreType.DMA((7,)), pltpu.SemaphoreType.DMA((4,))],
        ),
        out_shape=[jax.ShapeDtypeStruct((L, IN_W), F32), jax.ShapeDtypeStruct((L, D_MODEL), BF16),
                   jax.ShapeDtypeStruct((N_DEV, r, D_MODEL), BF16)],
        compiler_params=_cp(("arbitrary", "arbitrary")),
    )(chip, x, norm_w.reshape(1, D_MODEL), wt_shard)


def _seg_sum(v):
    a = lax.broadcasted_iota(jnp.int32, (128, 128), 0) // HEAD_DIM
    b = lax.broadcasted_iota(jnp.int32, (128, 128), 1) // HEAD_DIM
    ones = jnp.where(a == b, 1.0, 0.0).astype(BF16)
    hi = v.astype(BF16)
    lo = (v - hi.astype(F32)).astype(BF16)
    return jnp.dot(hi, ones, preferred_element_type=F32) + jnp.dot(lo, ones, preferred_element_type=F32)


def _rot_half(t):
    lane = lax.broadcasted_iota(jnp.int32, t.shape, 1)
    return jnp.where(lane % HEAD_DIM < HEAD_DIM // 2, pltpu.roll(t, 128 - HEAD_DIM // 2, 1),
                     pltpu.roll(t, HEAD_DIM // 2, 1))


def _norm_rope(raw, w, cos, sin):
    r = lax.rsqrt(_seg_sum(raw * raw) * (1.0 / HEAD_DIM) + NORM_EPS)
    tn = raw * r * w
    return r, tn * cos + _rot_half(tn) * sin


def _norm_rope_bwd(d_rot, raw, w, cos, sin):
    r = lax.rsqrt(_seg_sum(raw * raw) * (1.0 / HEAD_DIM) + NORM_EPS)
    d_tn = d_rot * cos + _rot_half(d_rot * sin)
    xh = raw * r
    gw = d_tn * w
    d_raw = r * (gw - xh * (_seg_sum(gw * xh) * (1.0 / HEAD_DIM)))
    return d_raw, d_tn * xh


def _band_mask2(has_prev):
    qi = lax.broadcasted_iota(jnp.int32, (2 * BLOCK, 2 * BLOCK), 0) % BLOCK + BLOCK
    kj = lax.broadcasted_iota(jnp.int32, (2 * BLOCK, 2 * BLOCK), 1)
    rel = qi - kj
    return (rel >= 0) & (rel < BLOCK) & ((kj >= BLOCK) | has_prev)


def _half_tiles(pair):
    lo = lax.broadcasted_iota(jnp.int32, pair.shape, 1) < HEAD_DIM
    sw = pltpu.roll(pair, HEAD_DIM, 1)
    z = jnp.zeros_like(pair)
    return (jnp.where(lo, pair, z).astype(BF16), jnp.where(lo, z, sw).astype(BF16),
            jnp.where(lo, sw, z).astype(BF16), jnp.where(lo, z, pair).astype(BF16))


def _two_rows(top, bottom):
    row = lax.broadcasted_iota(jnp.int32, (2 * BLOCK, 1), 0)
    return jnp.where(row < BLOCK, top, bottom)


def _lane_col(mat, h):
    lane = lax.broadcasted_iota(jnp.int32, mat.shape, 1)
    return jnp.sum(jnp.where(lane == h, mat, 0.0), axis=1, keepdims=True)


_SCALE = 1.0 / math.sqrt(HEAD_DIM)
_NT = (((1,), (1,)), ((), ()))
_NN = (((1,), (0,)), ((), ()))
_TN = (((0,), (0,)), ((), ()))


def _qk_prep(proj, tab, qw, kw):
    L = proj.shape[0]
    tm = _tile(L, 512)

    def body(q_ref, k_ref, t_ref, qw_ref, kw_ref, qo_ref, ko_ref):
        cos, sin = t_ref[:, :128], t_ref[:, 128:]
        for c in range(ATTN_W // 128):
            _, qr = _norm_rope(q_ref[:, c * 128:(c + 1) * 128], qw_ref[...], cos, sin)
            qo_ref[:, c * 128:(c + 1) * 128] = (qr * _SCALE).astype(BF16)
        for c in range(KV_W // 128):
            _, kr = _norm_rope(k_ref[:, c * 128:(c + 1) * 128], kw_ref[...], cos, sin)
            ko_ref[:, c * 128:(c + 1) * 128] = kr.astype(BF16)

    row = pl.BlockSpec((1, 128), lambda i: (0, 0))
    return pl.pallas_call(
        body,
        name="qk_prep",
        grid=(L // tm,),
        in_specs=[pl.BlockSpec((tm, ATTN_W), lambda i: (i, 0)), pl.BlockSpec((tm, KV_W), lambda i: (i, 4)),
                  pl.BlockSpec((tm, 256), lambda i: (i, 0)), row, row],
        out_specs=[pl.BlockSpec((tm, ATTN_W), lambda i: (i, 0)), pl.BlockSpec((tm, KV_W), lambda i: (i, 0))],
        out_shape=[jax.ShapeDtypeStruct((L, ATTN_W), BF16), jax.ShapeDtypeStruct((L, KV_W), BF16)],
        compiler_params=_cp(("parallel",)),
    )(proj, proj, tab, jnp.tile(qw, 2).reshape(1, 128), jnp.tile(kw, 2).reshape(1, 128))


def _group_tiles(g, kt, vt):
    a, b = divmod(g, 2)
    return kt[a][2 * b], kt[a][2 * b + 1], vt[a][2 * b], vt[a][2 * b + 1]


def _attn_fwd(q, k, proj, sinks, ride):
    L = proj.shape[0]
    nb = L // BLOCK

    def body(q_ref, kc_ref, kp_ref, vc_ref, vp_ref, z0_ref, z1_ref, sink_ref, og_ref, o_ref, lse_ref):
        i = pl.program_id(0)
        mask = _band_mask2(i > 0)
        z = jnp.concatenate([z0_ref[...], z1_ref[...]], axis=1)
        lane = lax.broadcasted_iota(jnp.int32, (BLOCK, 128), 1)
        kt = [_half_tiles(jnp.concatenate([kp_ref[:, a * 128:(a + 1) * 128], kc_ref[:, a * 128:(a + 1) * 128]],
                                          axis=0).astype(F32)) for a in range(2)]
        vt = [_half_tiles(jnp.concatenate([vp_ref[:, a * 128:(a + 1) * 128], vc_ref[:, a * 128:(a + 1) * 128]],
                                          axis=0)) for a in range(2)]
        lse_mat = jnp.zeros((BLOCK, 128), F32)
        outs = []
        for g in range(N_KV):
            k_lo, k_hi, v_lo, v_hi = _group_tiles(g, kt, vt)
            q2 = jnp.concatenate([q_ref[:, 2 * g * 128:(2 * g + 1) * 128],
                                  q_ref[:, (2 * g + 1) * 128:(2 * g + 2) * 128]], axis=0)
            acc = jnp.zeros((2 * BLOCK, 128), F32)
            for half, (kh, vh) in enumerate(((k_lo, v_lo), (k_hi, v_hi))):
                h_top, h_bot = 4 * g + half, 4 * g + 2 + half
                s = jnp.where(mask, lax.dot_general(q2, kh, _NT, preferred_element_type=F32), -1e30)
                sink = _two_rows(sink_ref[h_top], sink_ref[h_bot])
                m = jnp.maximum(jnp.max(s, axis=-1, keepdims=True), sink)
                e = jnp.exp(s - m)
                den = jnp.sum(e, axis=-1, keepdims=True) + jnp.exp(sink - m)
                p = e * (1.0 / den)
                acc = acc + jnp.dot(p.astype(BF16), vh, preferred_element_type=F32)
                lse = m + jnp.log(den)
                lse_mat = jnp.where(lane == h_top, lse[:BLOCK], lse_mat)
                lse_mat = jnp.where(lane == h_bot, lse[BLOCK:], lse_mat)
            outs += [acc[:BLOCK], acc[BLOCK:]]
        o = jnp.concatenate(outs, axis=1)
        o_ref[...] = o
        og_ref[...] = o * _silu(z)
        lse_ref[...] = lse_mat

    prev = lambda i: jnp.maximum(i - 1, 0)
    return _call(
        body, "attn_fwd", (nb,),
        [pl.BlockSpec((BLOCK, ATTN_W), lambda i: (i, 0)),
         pl.BlockSpec((BLOCK, KV_W), lambda i: (i, 0)),
         pl.BlockSpec((BLOCK, KV_W), lambda i: (prev(i), 0)),
         pl.BlockSpec((BLOCK, KV_W), lambda i: (i, 5)),
         pl.BlockSpec((BLOCK, KV_W), lambda i: (prev(i), 5)),
         pl.BlockSpec((BLOCK, 512), lambda i: (i, 3)),
         pl.BlockSpec((BLOCK, 512), lambda i: (i, 4)),
         pl.BlockSpec(memory_space=pltpu.SMEM)],
        [pl.BlockSpec((BLOCK, ATTN_W), lambda i: (i, 0)),
         pl.BlockSpec((BLOCK, ATTN_W), lambda i: (i, 0)),
         pl.BlockSpec((BLOCK, 128), lambda i: (i, 0))],
        [jax.ShapeDtypeStruct((L, ATTN_W), F32), jax.ShapeDtypeStruct((L, ATTN_W), F32),
         jax.ShapeDtypeStruct((L, 128), F32)],
        (q, k, k, proj, proj, proj, proj, sinks), ride=ride)


def _attn_bwd(q, k, proj, sinks, d_o, o, lse, ride):
    L = proj.shape[0]
    nb = L // BLOCK

    def body(q_ref, kc_ref, kp_ref, vc_ref, vp_ref, do_ref, o_ref, lse_ref, sink_ref,
             dq_ref, dk_ref, dv_ref, gs_ref, ck_scr, cv_scr):
        i = pl.program_id(0)

        @pl.when(i == 0)
        def _():
            gs_ref[...] = jnp.zeros_like(gs_ref)
            ck_scr[...] = jnp.zeros_like(ck_scr)
            cv_scr[...] = jnp.zeros_like(cv_scr)

        @pl.when(i == nb)
        def _():
            dk_ref[...] = ck_scr[...]
            dv_ref[...] = cv_scr[...]

        @pl.when(i < nb)
        def _():
            mask = _band_mask2(i > 0)
            lane = lax.broadcasted_iota(jnp.int32, (1, 128), 1)
            lo = lax.broadcasted_iota(jnp.int32, (2 * BLOCK, 128), 1) < HEAD_DIM
            lse_c = lse_ref[...]
            kt = [_half_tiles(jnp.concatenate([kp_ref[:, a * 128:(a + 1) * 128], kc_ref[:, a * 128:(a + 1) * 128]],
                                              axis=0).astype(F32)) for a in range(2)]
            vt = [_half_tiles(jnp.concatenate([vp_ref[:, a * 128:(a + 1) * 128], vc_ref[:, a * 128:(a + 1) * 128]],
                                              axis=0)) for a in range(2)]
            gs = jnp.zeros((1, 128), F32)
            dq_parts = []
            dk_acc = [jnp.zeros((2 * BLOCK, 128), F32) for _ in range(2)]
            dv_acc = [jnp.zeros((2 * BLOCK, 128), F32) for _ in range(2)]
            for g in range(N_KV):
                a, b = divmod(g, 2)
                k_lo, k_hi, v_lo, v_hi = _group_tiles(g, kt, vt)
                t0, t1 = slice(2 * g * 128, (2 * g + 1) * 128), slice((2 * g + 1) * 128, (2 * g + 2) * 128)
                q2 = jnp.concatenate([q_ref[:, t0], q_ref[:, t1]], axis=0)
                do2 = jnp.concatenate([do_ref[:, t0], do_ref[:, t1]], axis=0)
                prod = do2 * jnp.concatenate([o_ref[:, t0], o_ref[:, t1]], axis=0)
                do2_b = do2.astype(BF16)
                dq2 = jnp.zeros((2 * BLOCK, 128), F32)
                dk_h, dv_h = [], []
                for half, (kh, vh) in enumerate(((k_lo, v_lo), (k_hi, v_hi))):
                    h_top, h_bot = 4 * g + half, 4 * g + 2 + half
                    lse = jnp.concatenate([_lane_col(lse_c, h_top), _lane_col(lse_c, h_bot)], axis=0)
                    sink = _two_rows(sink_ref[h_top], sink_ref[h_bot])
                    delta = jnp.sum(jnp.where(lo == (half == 0), prod, 0.0), axis=1, keepdims=True)
                    s = jnp.where(mask, lax.dot_general(q2, kh, _NT, preferred_element_type=F32), -1e30)
                    p = jnp.exp(s - lse)
                    dp = lax.dot_general(do2_b, vh, _NT, preferred_element_type=F32)
                    ds_b = (p * (dp - delta)).astype(BF16)
                    p_b = p.astype(BF16)
                    dq2 = dq2 + jnp.dot(ds_b, kh, preferred_element_type=F32)
                    dk_h.append(lax.dot_general(ds_b, q2, _TN, preferred_element_type=F32))
                    dv_h.append(lax.dot_general(p_b, do2_b, _TN, preferred_element_type=F32))
                    gsink = -jnp.exp(sink - lse) * delta
                    row = lax.broadcasted_iota(jnp.int32, (2 * BLOCK, 1), 0)
                    gs = gs + jnp.where(lane == h_top, jnp.sum(jnp.where(row < BLOCK, gsink, 0.0)), 0.0)
                    gs = gs + jnp.where(lane == h_bot, jnp.sum(jnp.where(row >= BLOCK, gsink, 0.0)), 0.0)
                dq_parts += [dq2[:BLOCK], dq2[BLOCK:]]
                for acc, parts in ((dk_acc, dk_h), (dv_acc, dv_h)):
                    t = jnp.where(lo, parts[0], parts[1])
                    t = t + pltpu.roll(t, HEAD_DIM, 1)
                    acc[a] = acc[a] + jnp.where(lo == (b == 0), t, 0.0)
            dq_ref[...] = jnp.concatenate(dq_parts, axis=1)
            dk_full = jnp.concatenate(dk_acc, axis=1)
            dv_full = jnp.concatenate(dv_acc, axis=1)
            dk_ref[...] = ck_scr[...] + dk_full[:BLOCK]
            dv_ref[...] = cv_scr[...] + dv_full[:BLOCK]
            ck_scr[...] = dk_full[BLOCK:]
            cv_scr[...] = dv_full[BLOCK:]
            gs_ref[...] += gs

    cur = lambda i: jnp.minimum(i, nb - 1)
    prev = lambda i: jnp.maximum(jnp.minimum(i, nb - 1) - 1, 0)
    done = lambda i: jnp.maximum(i - 1, 0)
    bs = pl.BlockSpec
    return _call(
        body, "attn_bwd", (nb + 1,),
        [bs((BLOCK, ATTN_W), lambda i: (cur(i), 0)),
         bs((BLOCK, KV_W), lambda i: (cur(i), 0)), bs((BLOCK, KV_W), lambda i: (prev(i), 0)),
         bs((BLOCK, KV_W), lambda i: (cur(i), 5)), bs((BLOCK, KV_W), lambda i: (prev(i), 5)),
         bs((BLOCK, ATTN_W), lambda i: (cur(i), 0)), bs((BLOCK, ATTN_W), lambda i: (cur(i), 0)),
         bs((BLOCK, 128), lambda i: (cur(i), 0)), bs(memory_space=pltpu.SMEM)],
        [bs((BLOCK, ATTN_W), lambda i: (cur(i), 0)),
         bs((BLOCK, KV_W), lambda i: (done(i), 0)), bs((BLOCK, KV_W), lambda i: (done(i), 0)),
         bs((1, 128), lambda i: (0, 0))],
        [jax.ShapeDtypeStruct((L, ATTN_W), F32), jax.ShapeDtypeStruct((L, KV_W), F32),
         jax.ShapeDtypeStruct((L, KV_W), F32), jax.ShapeDtypeStruct((1, 128), F32)],
        (q, k, k, proj, proj, d_o, o, lse, sinks),
        [pltpu.VMEM((BLOCK, KV_W), F32), pltpu.VMEM((BLOCK, KV_W), F32)], ride)


def _qk_prep_bwd(proj, tab, qw, kw, d_q, d_k, d_v, d_za, d_u, d_zs):
    L = proj.shape[0]
    tm = _tile(L, 512)
    z0 = ATTN_W + 2 * KV_W

    def body(q_ref, k_ref, t_ref, qw_ref, kw_ref, dq_ref, dk_ref, dv_ref, dza_ref, du_ref, dzs_ref,
             out_ref, gq_ref, gk_ref):
        i = pl.program_id(0)

        @pl.when(i == 0)
        def _():
            gq_ref[...] = jnp.zeros_like(gq_ref)
            gk_ref[...] = jnp.zeros_like(gk_ref)

        cos, sin = t_ref[:, :128], t_ref[:, 128:]
        gq = jnp.zeros((1, 128), F32)
        gk = jnp.zeros((1, 128), F32)
        for c in range(ATTN_W // 128):
            cs = slice(c * 128, (c + 1) * 128)
            d_raw, gw = _norm_rope_bwd(dq_ref[:, cs] * _SCALE, q_ref[:, cs], qw_ref[...], cos, sin)
            out_ref[:, cs] = d_raw.astype(BF16)
            gq = gq + jnp.sum(gw, axis=0, keepdims=True)
        for c in range(KV_W // 128):
            cs = slice(c * 128, (c + 1) * 128)
            d_raw, gw = _norm_rope_bwd(dk_ref[:, cs], k_ref[:, cs], kw_ref[...], cos, sin)
            out_ref[:, ATTN_W + c * 128:ATTN_W + (c + 1) * 128] = d_raw.astype(BF16)
            gk = gk + jnp.sum(gw, axis=0, keepdims=True)
        out_ref[:, ATTN_W + KV_W:z0] = dv_ref[...].astype(BF16)
        out_ref[:, z0:z0 + ATTN_W] = dza_ref[...]
        out_ref[:, z0 + ATTN_W:z0 + ATTN_W + SSM_W] = du_ref[...].astype(BF16)
        out_ref[:, z0 + ATTN_W + SSM_W:] = dzs_ref[...]
        gq_ref[...] += gq
        gk_ref[...] += gk

    row = pl.BlockSpec((1, 128), lambda i: (0, 0))
    blk = lambda w, c: pl.BlockSpec((tm, w), lambda i: (i, c))
    return pl.pallas_call(
        body,
        name="qk_prep_bwd",
        grid=(L // tm,),
        in_specs=[blk(ATTN_W, 0), blk(KV_W, 4), blk(256, 0), row, row, blk(ATTN_W, 0), blk(KV_W, 0), blk(KV_W, 0),
                  blk(ATTN_W, 0), blk(SSM_W, 0), blk(SSM_W, 0)],
        out_specs=[blk(IN_W, 0), row, row],
        out_shape=[jax.ShapeDtypeStruct((L, IN_W), BF16), jax.ShapeDtypeStruct((1, 128), F32),
                   jax.ShapeDtypeStruct((1, 128), F32)],
        compiler_params=_cp(("arbitrary",)),
    )(proj, proj, tab, jnp.tile(qw, 2).reshape(1, 128), jnp.tile(kw, 2).reshape(1, 128), d_q, d_k, d_v,
      d_za, d_u, d_zs)


def _cmul(a, b):
    return a[0] * b[0] - a[1] * b[1], a[0] * b[1] + a[1] * b[0]


def _cmul_conj(a, b):
    return a[0] * b[0] + a[1] * b[1], a[1] * b[0] - a[0] * b[1]


def _cadd(a, b):
    return a[0] + b[0], a[1] + b[1]


def _dot3(a, b, dn):
    ah, bh = a.astype(BF16), b.astype(BF16)
    al, bl = (a - ah.astype(F32)).astype(BF16), (b - bh.astype(F32)).astype(BF16)
    d = lambda u, v: lax.dot_general(u, v, dn, preferred_element_type=F32)
    return d(ah, bh) + d(ah, bl) + d(al, bh)


def _s5_discretise(a_re, a_im, ls, cosx, sinx, bt):
    delta = jnp.exp(ls)
    er = jnp.exp(a_re * delta)
    lb = (er * cosx, er * sinx)
    den = a_re * a_re + a_im * a_im
    coef = _cmul_conj((lb[0] - 1.0, lb[1]), (a_re, a_im))
    coef = (coef[0] / den, coef[1] / den)
    return delta, lb, coef, den, _cmul(coef, bt)


def _powers(lb):
    pw = [(jnp.ones_like(lb[0]), jnp.zeros_like(lb[0]))]
    for _ in range(CHUNK):
        pw.append(_cmul(pw[-1], lb))
    return pw


def _block_rows(a, pw, idx):
    blocks = [_cmul(a, pw[i]) for i in idx]
    return (jnp.concatenate([b[0] for b in blocks], axis=-2), jnp.concatenate([b[1] for b in blocks], axis=-2))


def _block_rows_bwd(g, a, pw, idx, g_pw):
    g_a = (jnp.zeros_like(a[0]), jnp.zeros_like(a[0]))
    for j, i in enumerate(idx):
        gj = (g[0][..., j * SSM_H:(j + 1) * SSM_H, :], g[1][..., j * SSM_H:(j + 1) * SSM_H, :])
        g_a = _cadd(g_a, _cmul_conj(gj, pw[i]))
        gp = _cmul_conj(gj, a)
        g_pw[i] = _cadd(g_pw[i], (jnp.sum(gp[0], axis=-2, keepdims=True), jnp.sum(gp[1], axis=-2, keepdims=True)))
    return g_a


_IDX_S = [CHUNK - 1 - s for s in range(CHUNK)]
_IDX_C = list(range(CHUNK + 1))


def _prep_args(p):
    row = lambda t: t.reshape(SSM_G, 1, SSM_P)
    xi = p["a_im"] * jnp.exp(p["log_step"])[:, None]
    return (row(p["a_re"]), row(p["a_im"]), row(jnp.broadcast_to(p["log_step"][:, None], (SSM_G, SSM_P))),
            row(jnp.cos(xi)), row(jnp.sin(xi)), p["b_re"].transpose(0, 2, 1), p["b_im"].transpose(0, 2, 1),
            p["c_re"], p["c_im"])


PREP_GROUPS = 8


def _prep_specs():
    r1 = pl.BlockSpec((PREP_GROUPS, 1, SSM_P), lambda g: (g, 0, 0))
    r16 = pl.BlockSpec((PREP_GROUPS, SSM_H, SSM_P), lambda g: (g, 0, 0))
    return [r1] * 5 + [r16] * 4, r1, r16


def _ssm_prep(p):
    def one_group(q, are, aim, ls, cosx, sinx, btr, bti, cre, cim, mt_ref, s_ref, o_ref, a_ref):
        _, lb, _, _, bb = _s5_discretise(are[q], aim[q], ls[q], cosx[q], sinx[q], (btr[q], bti[q]))
        pw = _powers(lb)
        c = (cre[q], cim[q])
        sc = _block_rows(bb, pw, _IDX_S)
        cl = _block_rows(c, pw, _IDX_C)
        ok = (cl[0][:CW], cl[1][:CW])
        ot = (cl[0][SSM_H:], cl[1][SSM_H:])
        s_ref[q] = jnp.concatenate([sc[0], sc[1]], axis=1).astype(BF16)
        o_ref[q] = jnp.concatenate([ot[0], -ot[1]], axis=1).astype(BF16)
        a_ref[q] = jnp.concatenate([pw[CHUNK][0], pw[CHUNK][1]], axis=1)
        kt = _dot3(jnp.concatenate([bb[0], -bb[1]], axis=1), jnp.concatenate([ok[0], ok[1]], axis=1), _NT)
        lane = lax.broadcasted_iota(jnp.int32, kt.shape, 1)
        for s in range(CHUNK):
            blk = kt if s == 0 else jnp.where(lane >= SSM_H * s, pltpu.roll(kt, SSM_H * s, 1), 0.0)
            mt_ref[q, s * SSM_H:(s + 1) * SSM_H, :] = blk.astype(BF16)

    def body(*refs):
        for q in range(PREP_GROUPS):
            one_group(q, *refs)

    in_specs, r1, _ = _prep_specs()
    g3 = lambda r, c: pl.BlockSpec((PREP_GROUPS, r, c), lambda g: (g, 0, 0))
    return pl.pallas_call(
        body,
        name="ssm_prep",
        grid=(SSM_G // PREP_GROUPS,),
        in_specs=in_specs,
        out_specs=[g3(CW, CW), g3(CW, 2 * SSM_P), g3(CW, 2 * SSM_P), g3(1, 2 * SSM_P)],
        out_shape=[jax.ShapeDtypeStruct((SSM_G, CW, CW), BF16), jax.ShapeDtypeStruct((SSM_G, CW, 2 * SSM_P), BF16),
                   jax.ShapeDtypeStruct((SSM_G, CW, 2 * SSM_P), BF16),
                   jax.ShapeDtypeStruct((SSM_G, 1, 2 * SSM_P), F32)],
        compiler_params=_cp(("parallel",)),
    )(*_prep_args(p))


def _ssm_prep_bwd(p, g_mt, g_scat, g_ocat, g_a16, ride):
    def body(are, aim, ls, cosx, sinx, btr, bti, cre, cim, gmt_ref, gs_ref, go_ref, ga_ref,
             g_are, g_aim, g_ls, g_btr, g_bti, g_cre, g_cim, ga1_scr, gb1_scr):
        lam = (are[...], aim[...])
        bt = (btr[...], bti[...])
        delta, lb, coef, den, bb = _s5_discretise(lam[0], lam[1], ls[...], cosx[...], sinx[...], bt)
        pw = _powers(lb)
        c = (cre[...], cim[...])
        ok = _block_rows(c, pw, _IDX_C[:CHUNK])
        g_pw =[(jnp.zeros_like(lb[0]), jnp.zeros_like(lb[0])) for _ in range(CHUNK + 1)]
        lane = lax.broadcasted_iota(jnp.int32, (SSM_H, CW), 1)
        for q in range(PREP_GROUPS):
            g_kt = gmt_ref[q, :SSM_H, :]
            for s in range(1, CHUNK):
                blk = gmt_ref[q, s * SSM_H:(s + 1) * SSM_H, :]
                g_kt = g_kt + jnp.where(lane < CW - SSM_H * s, pltpu.roll(blk, CW - SSM_H * s, 1), 0.0)
            a1 = jnp.concatenate([bb[0][q], -bb[1][q]], axis=1)
            b1 = jnp.concatenate([ok[0][q], ok[1][q]], axis=1)
            ga1_scr[q] = _dot3(g_kt, b1, _NN)
            gb1_scr[q] = _dot3(g_kt, a1, _TN)
        g_a1, g_b1 = ga1_scr[...], gb1_scr[...]
        g_bb = (g_a1[..., :SSM_P], -g_a1[..., SSM_P:])
        gs = gs_ref[...]
        g_bb = _cadd(g_bb, _block_rows_bwd((gs[..., :SSM_P], gs[..., SSM_P:]), bb, pw, _IDX_S, g_pw))
        go = go_ref[...]
        pad = jnp.zeros_like(go[..., :SSM_H, :SSM_P])
        g_cl = (jnp.concatenate([g_b1[..., :SSM_P], pad], axis=-2) + jnp.concatenate([pad, go[..., :SSM_P]], axis=-2),
                jnp.concatenate([g_b1[..., SSM_P:], pad], axis=-2) - jnp.concatenate([pad, go[..., SSM_P:]], axis=-2))
        g_c = _block_rows_bwd(g_cl, c, pw, _IDX_C, g_pw)
        ga = ga_ref[...]
        g_pw[CHUNK] = _cadd(g_pw[CHUNK], (ga[..., :SSM_P], ga[..., SSM_P:]))
        g_lb = (jnp.zeros_like(lb[0]), jnp.zeros_like(lb[0]))
        for l in range(CHUNK - 1, -1, -1):
            g_lb = _cadd(g_lb, _cmul_conj(g_pw[l + 1], pw[l]))
            g_pw[l] = _cadd(g_pw[l], _cmul_conj(g_pw[l + 1], lb))
        g_bt = _cmul_conj(g_bb, coef)
        gc = _cmul_conj(g_bb, bt)
        g_coef = (jnp.sum(gc[0], axis=-2, keepdims=True), jnp.sum(gc[1], axis=-2, keepdims=True))
        lam_den = (lam[0] / den, lam[1] / den)
        g_lb = _cadd(g_lb, _cmul(g_coef, lam_den))
        t = _cmul(_cmul_conj(g_coef, coef), lam_den)
        g_x = _cmul_conj(g_lb, lb)
        g_are[...] = g_x[0] * delta - t[0]
        g_aim[...] = g_x[1] * delta - t[1]
        g_ls[...] = (g_x[0] * lam[0] + g_x[1] * lam[1]) * delta
        g_btr[...] = g_bt[0]
        g_bti[...] = g_bt[1]
        g_cre[...] = g_c[0]
        g_cim[...] = g_c[1]

    in_specs, r1, r16 = _prep_specs()
    g3 = lambda r, c: pl.BlockSpec((PREP_GROUPS, r, c), lambda g: (g, 0, 0))
    rows = jax.ShapeDtypeStruct((SSM_G, 1, SSM_P), F32)
    mats = jax.ShapeDtypeStruct((SSM_G, SSM_H, SSM_P), F32)
    (g_are, g_aim, g_ls, g_btr, g_bti, g_cre, g_cim), landed = _call(
        body, "ssm_prep_bwd", (SSM_G // PREP_GROUPS,),
        in_specs + [g3(CW, CW), g3(CW, 2 * SSM_P), g3(CW, 2 * SSM_P), g3(1, 2 * SSM_P)],
        [r1] * 3 + [r16] * 4, [rows] * 3 + [mats] * 4, (*_prep_args(p), g_mt, g_scat, g_ocat, g_a16),
        [pltpu.VMEM((PREP_GROUPS, SSM_H, 2 * SSM_P), F32), pltpu.VMEM((PREP_GROUPS, CW, 2 * SSM_P), F32)], ride)
    grads = dict(a_re=g_are.reshape(SSM_G, SSM_P), a_im=g_aim.reshape(SSM_G, SSM_P),
                 log_step=jnp.sum(g_ls.reshape(SSM_G, SSM_P), axis=1),
                 b_re=g_btr.transpose(0, 2, 1), b_im=g_bti.transpose(0, 2, 1), c_re=g_cre, c_im=g_cim)
    return grads, landed


def _cmul_const(xv, ar, ai):
    return xv * ar + pltpu.roll(xv, SSM_P, 1) * ai


def _chunk_scan(inc, a_row, reverse):
    n = inc.shape[0]
    lane = lax.broadcasted_iota(jnp.int32, (1, 2 * SSM_P), 1)
    row = lax.broadcasted_iota(jnp.int32, inc.shape, 0)
    sign = jnp.where(lane < SSM_P, -1.0, 1.0)
    ar = jnp.where(lane < SSM_P, a_row, pltpu.roll(a_row, SSM_P, 1))
    ai = jnp.where(lane < SSM_P, pltpu.roll(a_row, SSM_P, 1), a_row)
    if reverse:
        ai = -ai
    xv = inc
    s = 1
    while s < n:
        if reverse:
            sh = jnp.where(row < n - s, pltpu.roll(xv, n - s, 0), 0.0)
        else:
            sh = jnp.where(row >= s, pltpu.roll(xv, s, 0), 0.0)
        xv = xv + _cmul_const(sh, ar, ai * sign)
        ar, ai = ar * ar - ai * ai, 2.0 * ar * ai
        s *= 2
    return xv


def _shift_rows(xv, reverse):
    n = xv.shape[0]
    row = lax.broadcasted_iota(jnp.int32, xv.shape, 0)
    if reverse:
        return jnp.where(row < n - 1, pltpu.roll(xv, n - 1, 0), 0.0)
    return jnp.where(row >= 1, pltpu.roll(xv, 1, 0), 0.0)


GB = 128 // SSM_H
U_COL0 = (ATTN_W + 2 * KV_W + ATTN_W) // 128


HALF = CHUNK // 2


def _chunk_perm():
    r = jnp.arange(HALF * 128)
    t, g8, h = r // 128, (r % 128) // SSM_H, r % SSM_H
    return ((g8 * 128 + t * SSM_H + h)[:, None] == jnp.arange(GB * 128)[None, :]).astype(BF16)


def _load_perm(p_hbm, p_scr, sem):
    @pl.when(pl.program_id(0) == 0)
    def _():
        cp = pltpu.make_async_copy(p_hbm, p_scr, sem)
        cp.start()
        cp.wait()


def _rows_to_chunks(pieces, perm):
    halves = [jnp.dot(jnp.concatenate(pieces[k * HALF:(k + 1) * HALF], axis=1).astype(BF16), perm,
                      preferred_element_type=F32).astype(BF16) for k in range(2)]
    return [jnp.concatenate([hv[:, g * 128:(g + 1) * 128] for hv in halves], axis=1) for g in range(GB)]


def _chunks_to_rows(groups, perm, two_pass):
    pieces = []
    for k in range(2):
        v = jnp.concatenate([gv[:, k * 128:(k + 1) * 128] for gv in groups], axis=1)
        hi = v.astype(BF16)
        out = lax.dot_general(hi, perm, _NT, preferred_element_type=F32)
        if two_pass:
            lo = (v - hi.astype(F32)).astype(BF16)
            out = out + lax.dot_general(lo, perm, _NT, preferred_element_type=F32)
        pieces += [out[:, t * 128:(t + 1) * 128] for t in range(HALF)]
    return pieces


def _ssm_fwd(proj, perm, mt, scat, ocat, a16, d_skip, ride):
    L = proj.shape[0]
    nc = L // CHUNK

    def body(u_ref, p_hbm, mt_ref, s_ref, o_ref, a_ref, d_ref, y_ref, yg_ref, h_ref, p_scr, sem):
        _load_perm(p_hbm, p_scr, sem)
        perm = p_scr[...]
        rows = [pl.ds(t, nc, stride=CHUNK) for t in range(CHUNK)]
        us = [u_ref[r, :] for r in rows]
        ua = _rows_to_chunks(us, perm)
        ys = []
        for g in range(GB):
            uv = ua[g]
            inc = jnp.dot(uv, s_ref[g], preferred_element_type=F32)
            hx = _shift_rows(_chunk_scan(inc, a_ref[g], False), False)
            h_ref[g] = hx
            ys.append(jnp.dot(uv, mt_ref[g], preferred_element_type=F32)
                      + lax.dot_general(hx.astype(BF16), o_ref[g], _NT, preferred_element_type=F32))
        yp = _chunks_to_rows(ys, perm, True)
        for t, r in enumerate(rows):
            y = yp[t] + d_ref[...] * us[t]
            y_ref[r, :] = y
            yg_ref[r, :] = _gelu(y)

    g3 = lambda r, c: pl.BlockSpec((GB, r, c), lambda g: (g, 0, 0))
    col = pl.BlockSpec((L, 128), lambda g: (0, g))
    return _call(
        body, "ssm_fwd", (SSM_G // GB,),
        [pl.BlockSpec((L, 128), lambda g: (0, U_COL0 + g)), _ANY,
         g3(CW, CW), g3(CW, 2 * SSM_P), g3(CW, 2 * SSM_P), g3(1, 2 * SSM_P),
         pl.BlockSpec((1, 128), lambda g: (0, g))],
        [col, col, g3(nc, 2 * SSM_P)],
        [jax.ShapeDtypeStruct((L, SSM_W), F32), jax.ShapeDtypeStruct((L, SSM_W), F32),
         jax.ShapeDtypeStruct((SSM_G, nc, 2 * SSM_P), F32)],
        (proj, perm, mt, scat, ocat, a16, d_skip.reshape(1, SSM_W)),
        [pltpu.VMEM((HALF * 128, GB * 128), BF16), pltpu.SemaphoreType.DMA], ride)


def _ssm_bwd(d_yg, y, proj, hx, perm, mt, scat, ocat, a16, d_skip, ride):
    L = proj.shape[0]
    nc = L // CHUNK

    def body(dg_ref, y_ref, u_ref, h_ref, p_hbm, mt_ref, s_ref, o_ref, a_ref, d_ref,
             du_ref, gmt_ref, gs_ref, go_ref, ga_ref, gd_ref, p_scr, sem):
        _load_perm(p_hbm, p_scr, sem)
        perm = p_scr[...]
        rows = [pl.ds(t, nc, stride=CHUNK) for t in range(CHUNK)]
        us = [u_ref[r, :] for r in rows]
        dys = [dg_ref[r, :] * _dgelu(y_ref[r, :]) for r in rows]
        gd = jnp.zeros((1, 128), F32)
        for uv, dy in zip(us, dys):
            gd = gd + jnp.sum(dy * uv, axis=0, keepdims=True)
        gd_ref[...] = gd
        ua = _rows_to_chunks(us, perm)
        dya = _rows_to_chunks(dys, perm)
        lane = lax.broadcasted_iota(jnp.int32, (1, 2 * SSM_P), 1)
        dus = []
        for g in range(GB):
            uv, dy, hx_v = ua[g], dya[g], h_ref[g]
            dh = jnp.dot(dy, o_ref[g], preferred_element_type=F32)
            dinc = _shift_rows(_chunk_scan(dh, a_ref[g], True), True)
            dinc_b = dinc.astype(BF16)
            dus.append(lax.dot_general(dy, mt_ref[g], _NT, preferred_element_type=F32)
                       + lax.dot_general(dinc_b, s_ref[g], _NT, preferred_element_type=F32))
            gmt_ref[g] = lax.dot_general(uv, dy, _TN, preferred_element_type=F32)
            gs_ref[g] = lax.dot_general(uv, dinc_b, _TN, preferred_element_type=F32)
            go_ref[g] = lax.dot_general(dy, hx_v.astype(BF16), _TN, preferred_element_type=F32)
            p1 = dinc * hx_v
            p2 = pltpu.roll(dinc, SSM_P, 1) * hx_v
            t1 = jnp.sum(p1 + pltpu.roll(p1, SSM_P, 1), axis=0, keepdims=True)
            t2 = jnp.sum(p2 - pltpu.roll(p2, SSM_P, 1), axis=0, keepdims=True)
            ga_ref[g] = jnp.where(lane < SSM_P, t1, pltpu.roll(t2, SSM_P, 1))
        dup = _chunks_to_rows(dus, perm, False)
        for t, r in enumerate(rows):
            du_ref[r, :] = dup[t] + d_ref[...] * dys[t]

    g3 = lambda r, c: pl.BlockSpec((GB, r, c), lambda g: (g, 0, 0))
    col = pl.BlockSpec((L, 128), lambda g: (0, g))
    row = pl.BlockSpec((1, 128), lambda g: (0, g))
    return _call(
        body, "ssm_bwd", (SSM_G // GB,),
        [col, col, pl.BlockSpec((L, 128), lambda g: (0, U_COL0 + g)), g3(nc, 2 * SSM_P), _ANY,
         g3(CW, CW), g3(CW, 2 * SSM_P), g3(CW, 2 * SSM_P), g3(1, 2 * SSM_P), row],
        [col, g3(CW, CW), g3(CW, 2 * SSM_P), g3(CW, 2 * SSM_P), g3(1, 2 * SSM_P), row],
        [jax.ShapeDtypeStruct((L, SSM_W), F32), jax.ShapeDtypeStruct((SSM_G, CW, CW), F32),
         jax.ShapeDtypeStruct((SSM_G, CW, 2 * SSM_P), F32), jax.ShapeDtypeStruct((SSM_G, CW, 2 * SSM_P), F32),
         jax.ShapeDtypeStruct((SSM_G, 1, 2 * SSM_P), F32), jax.ShapeDtypeStruct((1, SSM_W), F32)],
        (d_yg, y, proj, hx, perm, mt, scat, ocat, a16, d_skip.reshape(1, SSM_W)),
        [pltpu.VMEM((HALF * 128, GB * 128), BF16), pltpu.SemaphoreType.DMA], ride)


def _merge(og, yg, w_glu, proj, b_glu, wa, ws):
    L = og.shape[0]
    tm = _tile(L, 256)

    def body(og_ref, yg_ref, wg_ref, z0_ref, z1_ref, b_ref, wa_ref, ws_ref, m_ref, gp_ref):
        zs = jnp.concatenate([z0_ref[...], z1_ref[...]], axis=1)
        ygv = yg_ref[...]
        gpre = jnp.dot(ygv.astype(BF16), wg_ref[...], preferred_element_type=F32)
        gp_ref[...] = gpre
        os_ = ygv * _sigmoid(gpre + b_ref[...]) * _silu(zs)
        ogv = og_ref[...]
        ra = lax.rsqrt(jnp.mean(ogv * ogv, axis=-1, keepdims=True) + NORM_EPS)
        rs = lax.rsqrt(jnp.mean(os_ * os_, axis=-1, keepdims=True) + NORM_EPS)
        m_ref[:, :ATTN_W] = (ogv * ra * wa_ref[...]).astype(BF16)
        m_ref[:, ATTN_W:] = (os_ * rs * ws_ref[...]).astype(BF16)

    row = lambda w: pl.BlockSpec((1, w), lambda i: (0, 0))
    return pl.pallas_call(
        body,
        name="merge",
        grid=(L // tm,),
        in_specs=[pl.BlockSpec((tm, ATTN_W), lambda i: (i, 0)), pl.BlockSpec((tm, SSM_W), lambda i: (i, 0)),
                  pl.BlockSpec((SSM_W, SSM_W), lambda i: (0, 0)),
                  pl.BlockSpec((tm, 512), lambda i: (i, 7)), pl.BlockSpec((tm, 512), lambda i: (i, 8)),
                  row(SSM_W), row(ATTN_W), row(SSM_W)],
        out_specs=[pl.BlockSpec((tm, D_MODEL), lambda i: (i, 0)), pl.BlockSpec((tm, SSM_W), lambda i: (i, 0))],
        out_shape=[jax.ShapeDtypeStruct((L, D_MODEL), BF16), jax.ShapeDtypeStruct((L, SSM_W), F32)],
        compiler_params=_cp(("parallel",)),
    )(og, yg, w_glu, proj, proj, b_glu.reshape(1, SSM_W), wa.reshape(1, ATTN_W), ws.reshape(1, SSM_W))


def _outproj_loss(merged, w_out, x, target):
    L = x.shape[0]
    tm, tn = _tile(L, 512), 1024
    ni, nj = L // tm, D_MODEL // tn

    def body(m_ref, w_ref, x_ref, t_ref, d_ref, db_ref, l_ref):
        out = x_ref[...] + jnp.dot(m_ref[...], w_ref[...], preferred_element_type=F32)
        diff = out - t_ref[...]
        d = diff * (1.0 / D_MODEL)
        d_ref[...] = d
        db_ref[...] = d.astype(BF16)
        l_ref[...] = jnp.full((1, 8, 128), jnp.sum(diff * diff), F32)

    return pl.pallas_call(
        body,
        name="outproj_loss",
        grid=(nj, ni),
        in_specs=[pl.BlockSpec((tm, D_MODEL), lambda j, i: (i, 0)),
                  pl.BlockSpec((D_MODEL, tn), lambda j, i: (0, j)),
                  pl.BlockSpec((tm, tn), lambda j, i: (i, j)),
                  pl.BlockSpec((tm, tn), lambda j, i: (i, j))],
        out_specs=[pl.BlockSpec((tm, tn), lambda j, i: (i, j)), pl.BlockSpec((tm, tn), lambda j, i: (i, j)),
                   pl.BlockSpec((1, 8, 128), lambda j, i: (i * nj + j, 0, 0))],
        out_shape=[jax.ShapeDtypeStruct((L, D_MODEL), F32), jax.ShapeDtypeStruct((L, D_MODEL), BF16),
                   jax.ShapeDtypeStruct((ni * nj, 8, 128), F32)],
        compiler_params=_cp(("parallel", "parallel")),
    )(merged, w_out, x, target)


def _merge_bwd(d_out_b, w_out, og, o, yg, gpre, proj, b_glu, wa, ws):
    L = og.shape[0]
    tm = _tile(L, 256)

    def body(dout_ref, wo_ref, og_ref, o_ref, yg_ref, gp_ref, za0_ref, za1_ref, zs0_ref, zs1_ref, b_ref, wa_ref,
             ws_ref, do_ref, dza_ref, dzs_ref, dg_ref, dyg_ref, gwa_ref, gws_ref, gb_ref):
        i = pl.program_id(0)

        @pl.when(i == 0)
        def _():
            gwa_ref[...] = jnp.zeros_like(gwa_ref)
            gws_ref[...] = jnp.zeros_like(gws_ref)
            gb_ref[...] = jnp.zeros_like(gb_ref)

        dm = lax.dot_general(dout_ref[...], wo_ref[...], _NT, preferred_element_type=F32)
        za = jnp.concatenate([za0_ref[...], za1_ref[...]], axis=1)
        zs = jnp.concatenate([zs0_ref[...], zs1_ref[...]], axis=1)
        ogv, dma = og_ref[...], dm[:, :ATTN_W]
        ra = lax.rsqrt(jnp.mean(ogv * ogv, axis=-1, keepdims=True) + NORM_EPS)
        xh = ogv * ra
        gwa_ref[...] += jnp.sum(dma * xh, axis=0, keepdims=True)
        gx = dma * wa_ref[...]
        d_og = ra * (gx - xh * jnp.mean(gx * xh, axis=-1, keepdims=True))
        do_ref[...] = d_og * _silu(za)
        dza_ref[...] = (d_og * o_ref[...] * _dsilu(za)).astype(BF16)
        ygv = yg_ref[...]
        sg = _sigmoid(gp_ref[...] + b_ref[...])
        y2 = ygv * sg
        sz = _silu(zs)
        os_ = y2 * sz
        dms = dm[:, ATTN_W:]
        rs = lax.rsqrt(jnp.mean(os_ * os_, axis=-1, keepdims=True) + NORM_EPS)
        xs = os_ * rs
        gws_ref[...] += jnp.sum(dms * xs, axis=0, keepdims=True)
        gxs = dms * ws_ref[...]
        d_os = rs * (gxs - xs * jnp.mean(gxs * xs, axis=-1, keepdims=True))
        dzs_ref[...] = (d_os * y2 * _dsilu(zs)).astype(BF16)
        d_y2 = d_os * sz
        d_g = d_y2 * ygv * sg * (1.0 - sg)
        dg_ref[...] = d_g.astype(BF16)
        gb_ref[...] += jnp.sum(d_g, axis=0, keepdims=True)
        dyg_ref[...] = d_y2 * sg

    row = lambda w: pl.BlockSpec((1, w), lambda i: (0, 0))
    full = lambda w: pl.BlockSpec((tm, w), lambda i: (i, 0))
    half = lambda c: pl.BlockSpec((tm, 512), lambda i: (i, c))
    return pl.pallas_call(
        body,
        name="merge_bwd",
        grid=(L // tm,),
        in_specs=[full(D_MODEL), pl.BlockSpec((D_MODEL, D_MODEL), lambda i: (0, 0)),
                  full(ATTN_W), full(ATTN_W), full(SSM_W), full(SSM_W),
                  half(3), half(4), half(7), half(8), row(SSM_W), row(ATTN_W), row(SSM_W)],
        out_specs=[full(ATTN_W), full(ATTN_W), full(SSM_W), full(SSM_W), full(SSM_W),
                   row(ATTN_W), row(SSM_W), row(SSM_W)],
        out_shape=[jax.ShapeDtypeStruct((L, ATTN_W), F32), jax.ShapeDtypeStruct((L, ATTN_W), BF16),
                   jax.ShapeDtypeStruct((L, SSM_W), BF16), jax.ShapeDtypeStruct((L, SSM_W), BF16),
                   jax.ShapeDtypeStruct((L, SSM_W), F32),
                   jax.ShapeDtypeStruct((1, ATTN_W), F32), jax.ShapeDtypeStruct((1, SSM_W), F32),
                   jax.ShapeDtypeStruct((1, SSM_W), F32)],
        compiler_params=_cp(("arbitrary",)),
    )(d_out_b, w_out, og, o, yg, gpre, proj, proj, proj, proj, b_glu.reshape(1, SSM_W), wa.reshape(1, ATTN_W),
      ws.reshape(1, SSM_W))


def _rms_bwd_x(x, norm_w, d_hn, d_out, ride):
    L = x.shape[0]
    tm = _tile(L, 256)

    def body(x_ref, w_ref, dh_ref, do_ref, gx_ref, gw_ref):
        i = pl.program_id(0)

        @pl.when(i == 0)
        def _():
            gw_ref[...] = jnp.zeros_like(gw_ref)

        xv, dh = x_ref[...], dh_ref[...]
        r = lax.rsqrt(jnp.mean(xv * xv, axis=-1, keepdims=True) + NORM_EPS)
        xh = xv * r
        gw_ref[...] += jnp.sum(dh * xh, axis=0, keepdims=True)
        gx = dh * w_ref[...]
        gx_ref[...] = do_ref[...] + r * (gx - xh * jnp.mean(gx * xh, axis=-1, keepdims=True))

    blk = pl.BlockSpec((tm, D_MODEL), lambda i: (i, 0))
    row = pl.BlockSpec((1, D_MODEL), lambda i: (0, 0))
    return _call(body, "rms_bwd_x", (L // tm,), [blk, row, blk, blk], [blk, row],
                 [jax.ShapeDtypeStruct((L, D_MODEL), F32), jax.ShapeDtypeStruct((1, D_MODEL), F32)],
                 (x, norm_w.reshape(1, D_MODEL), d_hn, d_out), ride=ride)


def _rope_table(positions):
    inv_freq = ROPE_THETA ** (-jnp.arange(0, HEAD_DIM, 2, dtype=F32) / HEAD_DIM)
    ang = positions.astype(F32)[:, None] * inv_freq
    sign = jnp.where(jnp.arange(128) % HEAD_DIM < HEAD_DIM // 2, -1.0, 1.0)
    return jnp.concatenate([jnp.tile(jnp.cos(ang), (1, 4)), jnp.tile(jnp.sin(ang), (1, 4)) * sign], axis=1)


def _step(x, positions, target, w, core, chip):
    small = {n: w[n] for n in _SMALL}
    tab = _rope_table(positions)
    mt_b, scat_b, ocat_b, a16 = _ssm_prep(small)
    perm = _chunk_perm()
    blocks = lambda t: t.reshape(N_DEV, t.shape[0] // N_DEV, t.shape[1])

    proj, hn, wt_in = _rms_inproj_gather(x, small["norm_w"], w["w_in"].T.astype(BF16), chip)
    wt_in = wt_in.reshape(IN_W, D_MODEL)
    q_rot, k_rot = _qk_prep(proj, tab, small["q_norm_w"], small["k_norm_w"])
    (og, o, lse), (w_glu,) = _attn_fwd(q_rot, k_rot, proj, small["sinks"],
                                       _gather_exchange([w["w_glu"].astype(BF16)]))
    (y, yg, hx), (w_out,) = _ssm_fwd(proj, perm, mt_b, scat_b, ocat_b, a16, small["d_skip"],
                                     _gather_exchange([w["w_out"].astype(BF16)]))
    w_glu, w_out = w_glu.reshape(SSM_W, SSM_W), w_out.reshape(D_MODEL, D_MODEL)
    merged, gpre = _merge(og, yg, w_glu, proj, small["b_glu"], small["attn_out_norm_w"], small["ssm_out_norm_w"])
    d_out, d_out_b, loss_parts = _outproj_loss(merged, w_out, x, target)
    loss = 0.5 * jnp.sum(loss_parts[:, 0, 0]) / D_MODEL

    g_w_out = blocks(_mm(merged, d_out_b, "tn", F32, "grad_w_out", tm=1024))
    d_o, d_za, d_zs, d_g, d_yg1, g_wa, g_ws, g_bglu = _merge_bwd(
        d_out_b, w_out, og, o, yg, gpre, proj, small["b_glu"], small["attn_out_norm_w"], small["ssm_out_norm_w"])
    g_w_glu = blocks(_mm(yg, d_g, "tn", F32, "grad_w_glu"))
    d_yg = _mm(d_g, w_glu, "nt", F32, "d_yg", add=d_yg1)
    (d_u, g_mt, g_scat, g_ocat, g_a16, g_dskip), (ra_out, ra_glu) = _ssm_bwd(
        d_yg, y, proj, hx, perm, mt_b, scat_b, ocat_b, a16, small["d_skip"], _pair_exchange([g_w_out, g_w_glu]))
    p_out = _pair_sum(g_w_out, ra_out, core, BF16, "pair_sum_out")
    p_glu = _pair_sum(g_w_glu, ra_glu, core, BF16, "pair_sum_glu")
    (d_q, d_k, d_v, g_sinks), (rb_out, rb_glu) = _attn_bwd(
        q_rot, k_rot, proj, small["sinks"], d_o, o, lse, _chip_exchange([p_out, p_glu]))
    d_proj, g_qw, g_kw = _qk_prep_bwd(proj, tab, small["q_norm_w"], small["k_norm_w"], d_q, d_k, d_v,
                                      d_za, d_u, d_zs)
    g_qw = g_qw[0, :HEAD_DIM] + g_qw[0, HEAD_DIM:]
    g_kw = g_kw[0, :HEAD_DIM] + g_kw[0, HEAD_DIM:]
    g_in_a = blocks(_mm(d_proj, hn, "tn", F32, "grad_w_in_a", tm=1152, panel=0))
    g_in_b, (ra_a,) = _mm(d_proj, hn, "tn", F32, "grad_w_in_b", tm=1152, panel=1, ride=_pair_exchange([g_in_a]))
    g_in_b = blocks(g_in_b)
    p_a = _pair_sum(g_in_a, ra_a, core, BF16, "pair_sum_in_a")
    d_hn, (rb_a, ra_b) = _mm(d_proj, wt_in, "nn", F32, "d_hn", tm=1024,
                             ride=_both(_chip_exchange([p_a]), _pair_exchange([g_in_b])))
    p_b = _pair_sum(g_in_b, ra_b, core, BF16, "pair_sum_in_b")
    g_small, (rb_b,) = _ssm_prep_bwd(small, g_mt, g_scat, g_ocat, g_a16, _chip_exchange([p_b]))
    (grad_x, g_nw), _ = _rms_bwd_x(x, small["norm_w"], d_hn, d_out, None)

    g_small.update(norm_w=g_nw.reshape(-1), q_norm_w=g_qw.reshape(-1), k_norm_w=g_kw.reshape(-1),
                   sinks=g_sinks[0, :N_HEADS], d_skip=g_dskip.reshape(-1), b_glu=g_bglu.reshape(-1),
                   attn_out_norm_w=g_wa.reshape(-1), ssm_out_norm_w=g_ws.reshape(-1))
    g_packed = _slab_all_reduce(_pack(g_small, loss).reshape(N_DEV, _PACK_ROWS // N_DEV, 128))
    g_packed = g_packed.reshape(_PACK_ROWS, 128)
    grads = _unpack(g_packed, w)
    parts = dict(w_in=([p_a, p_b], [rb_a, rb_b]), w_glu=([p_glu], [rb_glu]), w_out=([p_out], [rb_out]))
    return g_packed[_LOSS_ROW, 0], grad_x, grads, parts


_ANY = pl.BlockSpec(memory_space=pl.ANY)


class _Exchange:
    def __init__(self, arrays, out_shape, sems, start, finish, relay=None):
        self.arrays, self.out_shape, self.sems, self.start, self.finish = arrays, out_shape, sems, start, finish
        self.relay = relay if relay is not None else (lambda ins, outs, sems: None)


def _gather_exchange(blocks):
    n = len(blocks)

    def parts(ins, outs, sems):
        send_sems, recv_sems, local_sems = sems
        x, y, c = lax.axis_index("x"), lax.axis_index("y"), lax.axis_index("c")
        me, sibling = (x, y, c), (x, y, 1 - c)
        chips = [(1 - x, y), (x, 1 - y), (1 - x, 1 - y)]

        def slot(k, dev):
            return outs[k].at[4 * dev[0] + 2 * dev[1] + dev[2]]

        def copy(k, q, block, to, src=None):
            return pltpu.make_async_remote_copy(
                src_ref=slot(k, block) if src is None else src, dst_ref=slot(k, block),
                send_sem=send_sems.at[k, q], recv_sem=recv_sems.at[k, q], device_id=to, device_id_type=MESH)

        mine = [pltpu.make_async_copy(ins[k], slot(k, me), local_sems.at[k]) for k in range(n)]
        first = []
        for k in range(n):
            first.append(copy(k, 0, me, sibling, src=ins[k]))
            first += [copy(k, 1 + j, me, (*chip, c), src=ins[k]) for j, chip in enumerate(chips)]
        return me, sibling, chips, c, copy, mine, first

    def start(ins, outs, sems):
        *_, mine, first = parts(ins, outs, sems)
        for cp in mine + first:
            cp.start()

    def relay(ins, outs, sems):
        me, sibling, chips, c, copy, _, _ = parts(ins, outs, sems)
        for j, chip in enumerate(chips):
            for k in range(n):
                copy(k, 1 + j, (*chip, c), me).wait_recv()
                copy(k, 4 + j, (*chip, c), sibling).start()

    def finish(ins, outs, sems):
        me, sibling, chips, c, copy, mine, first = parts(ins, outs, sems)
        for k in range(n):
            copy(k, 0, sibling, me).wait_recv()
            for j, chip in enumerate(chips):
                copy(k, 4 + j, (*chip, 1 - c), me).wait_recv()
        for cp in first + [copy(k, 4 + j, (*chip, c), sibling) for k in range(n) for j, chip in enumerate(chips)]:
            cp.wait_send()
        for cp in mine:
            cp.wait()

    return _Exchange(blocks, [jax.ShapeDtypeStruct((N_DEV,) + b.shape, b.dtype) for b in blocks],
                     [pltpu.SemaphoreType.DMA((n, 7)), pltpu.SemaphoreType.DMA((n, 7)), pltpu.SemaphoreType.DMA((n,))],
                     start, finish, relay)


def _direct_exchange(arrays, out_lead, fan, route):
    n = len(arrays)

    def copies(ins, outs, sems):
        send_sems, recv_sems = sems
        legs = route(lax.axis_index("x"), lax.axis_index("y"), lax.axis_index("c"))
        return [pltpu.make_async_remote_copy(
            src_ref=ins[k].at[src], dst_ref=outs[k].at[q], send_sem=send_sems.at[k, q], recv_sem=recv_sems.at[k, q],
            device_id=to, device_id_type=MESH) for k in range(n) for src, q, to in legs]

    def start(ins, outs, sems):
        for cp in copies(ins, outs, sems):
            cp.start()

    def finish(ins, outs, sems):
        for cp in copies(ins, outs, sems):
            cp.wait()

    return _Exchange(arrays, [jax.ShapeDtypeStruct((out_lead,) + a.shape[1:], a.dtype) for a in arrays],
                     [pltpu.SemaphoreType.DMA((n, fan)), pltpu.SemaphoreType.DMA((n, fan))], start, finish)


def _pair_exchange(grads):
    return _direct_exchange(grads, 4, 4, lambda x, y, c: [(2 * chip + (1 - c), chip, (x, y, 1 - c))
                                                          for chip in range(4)])


def _chip_exchange(parts):
    def route(x, y, c):
        chips = [(1 - x, y), (x, 1 - y), (1 - x, 1 - y)]
        return [(2 * chip[0] + chip[1], q, (*chip, c)) for q, chip in enumerate(chips)]
    return _direct_exchange(parts, 3, 3, route)


def _both(ex1, ex2):
    n1, s1 = len(ex1.arrays), len(ex1.sems)

    def halves(ins, outs, sems):
        return (ins[:n1], outs[:n1], sems[:s1]), (ins[n1:], outs[n1:], sems[s1:])

    def start(ins, outs, sems):
        h1, h2 = halves(ins, outs, sems)
        ex1.start(*h1)
        ex2.start(*h2)

    def relay(ins, outs, sems):
        h1, h2 = halves(ins, outs, sems)
        ex1.relay(*h1)
        ex2.relay(*h2)

    def finish(ins, outs, sems):
        h1, h2 = halves(ins, outs, sems)
        ex1.finish(*h1)
        ex2.finish(*h2)

    return _Exchange(list(ex1.arrays) + list(ex2.arrays), list(ex1.out_shape) + list(ex2.out_shape),
                     list(ex1.sems) + list(ex2.sems), start, finish, relay)


def _call(body, name, grid, in_specs, out_specs, out_shape, args, scratch_shapes=(), ride=None):
    if ride is None:
        sem = ("arbitrary",) * len(grid)
        return pl.pallas_call(body, name=name, grid=grid, in_specs=in_specs, out_specs=out_specs, out_shape=out_shape,
                              scratch_shapes=list(scratch_shapes), compiler_params=_cp(sem))(*args), None
    n_in, n_out, n_scr, n_x = len(in_specs), len(out_specs), len(scratch_shapes), len(ride.arrays)

    def wrapped(*refs):
        ins, refs = refs[:n_in], refs[n_in:]
        x_in, refs = refs[:n_x], refs[n_x:]
        outs, refs = refs[:n_out], refs[n_out:]
        x_out, refs = refs[:n_x], refs[n_x:]
        scr, sems = refs[:n_scr], refs[n_scr:]
        step, total = pl.program_id(0), grid[0]
        for a in range(1, len(grid)):
            step, total = step * grid[a] + pl.program_id(a), total * grid[a]
        @pl.when(step == 0)
        def _():
            ride.start(x_in, x_out, sems)

        @pl.when(step == max(total - 2, 0))
        def _():
            ride.relay(x_in, x_out, sems)

        body(*ins, *outs, *scr)

        @pl.when(step == total - 1)
        def _():
            ride.finish(x_in, x_out, sems)

    res = pl.pallas_call(
        wrapped, name=name, grid=grid, in_specs=list(in_specs) + [_ANY] * n_x,
        out_specs=list(out_specs) + [_ANY] * n_x, out_shape=list(out_shape) + list(ride.out_shape),
        scratch_shapes=list(scratch_shapes) + list(ride.sems),
        compiler_params=_cp(("arbitrary",) * len(grid)))(*args, *ride.arrays)
    return res[:n_out], list(res[n_out:])


def _pair_sum(g, ra, core, out_dtype, name):
    _, r, C = g.shape
    tr = _tile(r, 576)

    def body(c_ref, g_ref, ra_ref, p_ref):
        p_ref[...] = (g_ref[...] + ra_ref[...]).astype(p_ref.dtype)

    return pl.pallas_call(
        body,
        name=name,
        grid_spec=pltpu.PrefetchScalarGridSpec(
            num_scalar_prefetch=1,
            grid=(4, r // tr),
            in_specs=[pl.BlockSpec((1, tr, C), lambda j, t, c_ref: (2 * j + c_ref[0], t, 0)),
                      pl.BlockSpec((1, tr, C), lambda j, t, c_ref: (j, t, 0))],
            out_specs=pl.BlockSpec((1, tr, C), lambda j, t, c_ref: (j, t, 0)),
        ),
        out_shape=jax.ShapeDtypeStruct((4, r, C), out_dtype),
        compiler_params=_cp(("parallel", "parallel")),
    )(core, g, ra)


def _slab_all_reduce(slab):
    _, r, lanes = slab.shape

    def body(s_ref, o_ref, ra, rb, ps, sems_a, sems_b, sems_c):
        x, y, c = lax.axis_index("x"), lax.axis_index("y"), lax.axis_index("c")
        chips = [(1 - x, y), (x, 1 - y), (1 - x, 1 - y)]
        pair = [pltpu.make_async_remote_copy(
            src_ref=s_ref.at[2 * k + (1 - c)], dst_ref=ra.at[k], send_sem=sems_a.at[0, k], recv_sem=sems_a.at[1, k],
            device_id=(x, y, 1 - c), device_id_type=MESH) for k in range(4)]
        for cp in pair:
            cp.start()
        for cp in pair:
            cp.wait()
        for k in range(4):
            ps[k] = s_ref[2 * k + c] + ra[k]
        cross = [pltpu.make_async_remote_copy(
            src_ref=ps.at[2 * ch[0] + ch[1]], dst_ref=rb.at[q], send_sem=sems_b.at[0, q], recv_sem=sems_b.at[1, q],
            device_id=(*ch, c), device_id_type=MESH) for q, ch in enumerate(chips)]
        for cp in cross:
            cp.start()
        for cp in cross:
            cp.wait()
        me = 4 * x + 2 * y + c
        o_ref[me] = ((ps[2 * x + y] + rb[0]) + rb[1]) + rb[2]
        flips = [(dx, dy, dc) for dx in (0, 1) for dy in (0, 1) for dc in (0, 1) if dx + dy + dc]
        spread = [pltpu.make_async_remote_copy(
            src_ref=o_ref.at[me], dst_ref=o_ref.at[me], send_sem=sems_c.at[0, q], recv_sem=sems_c.at[1, q],
            device_id=(x + dx - 2 * x * dx, y + dy - 2 * y * dy, c + dc - 2 * c * dc), device_id_type=MESH)
            for q, (dx, dy, dc) in enumerate(flips)]
        for cp in spread:
            cp.start()
        for q, (dx, dy, dc) in enumerate(flips):
            peer = 4 * (x + dx - 2 * x * dx) + 2 * (y + dy - 2 * y * dy) + (c + dc - 2 * c * dc)
            pltpu.make_async_remote_copy(
                src_ref=o_ref.at[peer], dst_ref=o_ref.at[peer], send_sem=sems_c.at[0, q], recv_sem=sems_c.at[1, q],
                device_id=(x, y, c), device_id_type=MESH).wait_recv()
        for cp in spread:
            cp.wait_send()

    whole = pl.BlockSpec(memory_space=pltpu.VMEM)
    return pl.pallas_call(
        body, name="slab_all_reduce", in_specs=[whole], out_specs=whole,
        out_shape=jax.ShapeDtypeStruct(slab.shape, F32),
        scratch_shapes=[pltpu.VMEM((4, r, lanes), F32), pltpu.VMEM((3, r, lanes), F32), pltpu.VMEM((4, r, lanes), F32),
                        pltpu.SemaphoreType.DMA((2, 4)), pltpu.SemaphoreType.DMA((2, 3)),
                        pltpu.SemaphoreType.DMA((2, 7))],
        compiler_params=_cp(),
    )(slab)


def _adamw_reduced(ps, rbs, chip, w, m, v, name):
    nh = len(ps)
    R, C = w.shape
    ch = C // nh
    tr = _tile(R, 288)
    nt = R // tr
    c1 = 1.0 - ADAM_B1 ** ADAM_STEP
    c2 = 1.0 - ADAM_B2 ** ADAM_STEP

    def body(c_ref, *refs):
        p_refs, rb_refs = refs[:nh], refs[nh:2 * nh]
        w_ref, m_ref, v_ref, g_ref, d_ref, nm_ref, nv_ref = refs[2 * nh:]
        for h in range(nh):
            @pl.when(pl.program_id(0) == h)
            def _(h=h):
                rb = rb_refs[h]
                gv = p_refs[h][0].astype(F32) + rb[0].astype(F32)
                gv = gv + rb[1].astype(F32)
                gv = gv + rb[2].astype(F32)
                nm = ADAM_B1 * m_ref[...] + (1.0 - ADAM_B1) * gv
                nv = ADAM_B2 * v_ref[...] + (1.0 - ADAM_B2) * (gv * gv)
                g_ref[...] = gv
                nm_ref[...] = nm
                nv_ref[...] = nv
                d_ref[...] = -ADAM_LR * ((nm / c1) / (jnp.sqrt(nv / c2) + ADAM_EPS) + ADAM_WD * w_ref[...])

    def held(h):
        return lambda hh, tt: jnp.where(hh == h, tt, jnp.where(hh < h, 0, nt - 1))

    p_specs = [pl.BlockSpec((1, tr, ch), lambda hh, tt, c_ref, f=held(h): (c_ref[0], f(hh, tt), 0))
               for h in range(nh)]
    rb_specs = [pl.BlockSpec((3, tr, ch), lambda hh, tt, c_ref, f=held(h): (0, f(hh, tt), 0)) for h in range(nh)]
    blk = pl.BlockSpec((tr, ch), lambda hh, tt, c_ref: (tt, hh))
    return pl.pallas_call(
        body,
        name=name,
        grid_spec=pltpu.PrefetchScalarGridSpec(
            num_scalar_prefetch=1, grid=(nh, nt), in_specs=p_specs + rb_specs + [blk] * 3, out_specs=[blk] * 4),
        out_shape=[jax.ShapeDtypeStruct((R, C), F32)] * 4,
        compiler_params=_cp(("arbitrary", "arbitrary")),
    )(chip, *ps, *rbs, w, m, v)


_SMALL = ("norm_w", "q_norm_w", "k_norm_w", "sinks", "a_re", "a_im", "log_step", "b_re", "b_im", "c_re", "c_im",
          "d_skip", "b_glu", "attn_out_norm_w", "ssm_out_norm_w")
_WEIGHTS = ("norm_w", "w_in", "q_norm_w", "k_norm_w", "sinks", "a_re", "a_im", "log_step", "b_re", "b_im", "c_re",
            "c_im", "d_skip", "w_glu", "b_glu", "attn_out_norm_w", "ssm_out_norm_w", "w_out")
_SMALL_2D = dict(norm_w=(1, 2048), q_norm_w=(1, 64), k_norm_w=(1, 64), sinks=(1, 16), a_re=(64, 64), a_im=(64, 64),
                 log_step=(1, 64), b_re=(1024, 64), b_im=(1024, 64), c_re=(1024, 64), c_im=(1024, 64),
                 d_skip=(1, 1024), b_glu=(1, 1024), attn_out_norm_w=(1, 1024), ssm_out_norm_w=(1, 1024))
_P_MINOR = ("b_re", "b_im")


def _flat_form(n, t):
    return t.transpose(0, 2, 1) if n in _P_MINOR else t


def _own_form(n, t, shape):
    if n in _P_MINOR:
        return t.reshape(shape[0], shape[2], shape[1]).transpose(0, 2, 1)
    return t.reshape(shape)


def _slab_rows(n):
    return -(-n // 1024) * 8


_PACK_ROWS = 2304


_LOSS_ROW = 2192


def _pack(d, loss):
    parts = []
    for n in _SMALL:
        flat = _flat_form(n, d[n]).reshape(-1).astype(F32)
        rows = _slab_rows(flat.shape[0])
        parts.append(jnp.pad(flat, (0, rows * 128 - flat.shape[0])).reshape(rows, 128))
    assert sum(p.shape[0] for p in parts) == _LOSS_ROW
    parts.append(jnp.pad(loss.reshape(1, 1), ((0, _PACK_ROWS - _LOSS_ROW - 1), (0, 127))))
    return jnp.concatenate(parts, axis=0)


def _unpack(packed, like):
    out, off = {}, 0
    for n in _SMALL:
        size = math.prod(like[n].shape)
        rows = _slab_rows(size)
        out[n] = _own_form(n, packed[off:off + rows].reshape(-1)[:size], like[n].shape)
        off += rows
    return out


def _adamw_small(g, w, m, v):
    c1 = 1.0 - ADAM_B1 ** ADAM_STEP
    c2 = 1.0 - ADAM_B2 ** ADAM_STEP
    k = len(_SMALL)

    def body(*refs):
        ins, outs = refs[:4 * k], refs[4 * k:]
        for j in range(k):
            gv, wv, mv, vv = (ins[q * k + j][...] for q in range(4))
            nm = ADAM_B1 * mv + (1.0 - ADAM_B1) * gv
            nv = ADAM_B2 * vv + (1.0 - ADAM_B2) * (gv * gv)
            outs[j][...] = -ADAM_LR * ((nm / c1) / (jnp.sqrt(nv / c2) + ADAM_EPS) + ADAM_WD * wv)
            outs[k + j][...] = nm
            outs[2 * k + j][...] = nv

    args = [_flat_form(n, d[n]).reshape(_SMALL_2D[n]) for d in (g, w, m, v) for n in _SMALL]
    shapes = [jax.ShapeDtypeStruct(_SMALL_2D[n], F32) for _ in range(3) for n in _SMALL]
    outs = pl.pallas_call(body, name="adamw_small", out_shape=shapes, compiler_params=_cp())(*args)
    res = []
    for q in range(3):
        res.append({n: _own_form(n, outs[q * k + j], w[n].shape) for j, n in enumerate(_SMALL)})
    return res


def kernel(x, positions, norm_w, w_in, q_norm_w, k_norm_w, sinks, a_re, a_im, log_step, b_re, b_im, c_re, c_im, d_skip, w_glu, b_glu, attn_out_norm_w, ssm_out_norm_w, w_out, loss_target, m_norm_w, m_w_in, m_q_norm_w, m_k_norm_w, m_sinks, m_a_re, m_a_im, m_log_step, m_b_re, m_b_im, m_c_re, m_c_im, m_d_skip, m_w_glu, m_b_glu, m_attn_out_norm_w, m_ssm_out_norm_w, m_w_out, v_norm_w, v_w_in, v_q_norm_w, v_k_norm_w, v_sinks, v_a_re, v_a_im, v_log_step, v_b_re, v_b_im, v_c_re, v_c_im, v_d_skip, v_w_glu, v_b_glu, v_attn_out_norm_w, v_ssm_out_norm_w, v_w_out):
    w = dict(norm_w=norm_w, w_in=w_in, q_norm_w=q_norm_w, k_norm_w=k_norm_w, sinks=sinks, a_re=a_re, a_im=a_im,
             log_step=log_step, b_re=b_re, b_im=b_im, c_re=c_re, c_im=c_im, d_skip=d_skip, w_glu=w_glu, b_glu=b_glu,
             attn_out_norm_w=attn_out_norm_w, ssm_out_norm_w=ssm_out_norm_w, w_out=w_out)
    m = dict(norm_w=m_norm_w, w_in=m_w_in, q_norm_w=m_q_norm_w, k_norm_w=m_k_norm_w, sinks=m_sinks, a_re=m_a_re,
             a_im=m_a_im, log_step=m_log_step, b_re=m_b_re, b_im=m_b_im, c_re=m_c_re, c_im=m_c_im, d_skip=m_d_skip,
             w_glu=m_w_glu, b_glu=m_b_glu, attn_out_norm_w=m_attn_out_norm_w, ssm_out_norm_w=m_ssm_out_norm_w,
             w_out=m_w_out)
    v = dict(norm_w=v_norm_w, w_in=v_w_in, q_norm_w=v_q_norm_w, k_norm_w=v_k_norm_w, sinks=v_sinks, a_re=v_a_re,
             a_im=v_a_im, log_step=v_log_step, b_re=v_b_re, b_im=v_b_im, c_re=v_c_re, c_im=v_c_im, d_skip=v_d_skip,
             w_glu=v_w_glu, b_glu=v_b_glu, attn_out_norm_w=v_attn_out_norm_w, ssm_out_norm_w=v_ssm_out_norm_w,
             w_out=v_w_out)
    core = lax.axis_index("c").astype(jnp.int32).reshape(1)
    chip = (2 * lax.axis_index("x") + lax.axis_index("y")).astype(jnp.int32).reshape(1)

    loss, grad_x, grads, parts = _step(x[0], positions[0], loss_target[0], w, core, chip)
    delta, new_m, new_v = {}, {}, {}
    for n in ("w_glu", "w_out"):
        grads[n], delta[n], new_m[n], new_v[n] = _adamw_reduced(*parts[n], chip, w[n], m[n], v[n], f"adamw_{n}")
    g_t, d_t, m_t, v_t = _adamw_reduced(*parts["w_in"], chip, w["w_in"].T, m["w_in"].T, v["w_in"].T, "adamw_w_in")
    grads["w_in"], delta["w_in"], new_m["w_in"], new_v["w_in"] = g_t.T, d_t.T, m_t.T, v_t.T
    d_s, m_s, v_s = _adamw_small(grads, w, m, v)
    delta.update(d_s)
    new_m.update(m_s)
    new_v.update(v_s)

    return (loss, grad_x[None], *[grads[n] for n in _WEIGHTS], *[delta[n] for n in _WEIGHTS],
            *[new_m[n] for n in _WEIGHTS], *[new_v[n] for n in _WEIGHTS])
```

```python
import math

import jax
import jax.numpy as jnp
from jax import lax
from jax.experimental import pallas as pl
from jax.experimental.pallas import tpu as pltpu

F32 = jnp.float32
BF16 = jnp.bfloat16

D_MODEL = 2048
ATTN_W = 1024
KV_W = 256
SSM_W = 1024
HEAD_DIM = 64
N_HEADS = 16
N_KV = 4
IN_W = 4608
BLOCK = 128
ROPE_THETA = 10000.0
NORM_EPS = 1e-6
SSM_G = 64
SSM_P = 64
SSM_H = 16
CHUNK = 16
CW = CHUNK * SSM_H
N_DEV = 8

ADAM_LR = 0.001
ADAM_B1 = 0.9
ADAM_B2 = 0.999
ADAM_EPS = 1e-08
ADAM_WD = 0.01
ADAM_STEP = 10

VMEM_LIMIT = 56 * 1024 * 1024
MESH = pl.DeviceIdType.MESH


def _cp(sem=None):
    if sem is None:
        return pltpu.CompilerParams(vmem_limit_bytes=VMEM_LIMIT)
    return pltpu.CompilerParams(vmem_limit_bytes=VMEM_LIMIT, dimension_semantics=sem)


def _sigmoid(x):
    return 0.5 * jnp.tanh(0.5 * x) + 0.5


def _silu(x):
    return x * _sigmoid(x)


def _dsilu(x):
    s = _sigmoid(x)
    return s * (1.0 + x * (1.0 - s))


_GELU_C = math.sqrt(2.0 / math.pi)


def _gelu(y):
    t = jnp.tanh(_GELU_C * (y + 0.044715 * y * y * y))
    return 0.5 * y * (1.0 + t)


def _dgelu(y):
    t = jnp.tanh(_GELU_C * (y + 0.044715 * y * y * y))
    return 0.5 * (1.0 + t) + 0.5 * y * (1.0 - t * t) * _GELU_C * (1.0 + 3.0 * 0.044715 * y * y)


def _tile(n, want):
    if n <= want:
        return n
    for t in range(want - want % 16, 0, -16):
        if n % t == 0:
            return t
    raise ValueError((n, want))


def _mm(a, b, mode, out_dtype, name, tm=512, tn=1024, add=None, ride=None, panel=None):
    if mode == "nn":
        (M, K), (K2, N) = a.shape, b.shape
    elif mode == "nt":
        (M, K), (N, K2) = a.shape, b.shape
    else:
        (K, M), (K2, N) = a.shape, b.shape
    assert K == K2
    tm, tn = _tile(M, tm), _tile(N, tn)
    p0 = 0
    if panel is not None:
        assert mode != "nt" and add is None
        p0, N = panel, tn
    dn = {"nn": _NN, "nt": _NT, "tn": _TN}[mode]

    def body(a_ref, b_ref, *rest):
        o_ref = rest[-1]
        acc = lax.dot_general(a_ref[...].astype(BF16), b_ref[...].astype(BF16), dn, preferred_element_type=F32)
        if add is not None:
            acc = acc + rest[0][...]
        o_ref[...] = acc.astype(o_ref.dtype)

    a_spec = pl.BlockSpec((K, tm), lambda j, i: (0, i)) if mode == "tn" else pl.BlockSpec((tm, K), lambda j, i: (i, 0))
    b_spec = (pl.BlockSpec((tn, K), lambda j, i: (j, 0)) if mode == "nt"
              else pl.BlockSpec((K, tn), lambda j, i: (0, j + p0)))
    o_spec = pl.BlockSpec((tm, tn), lambda j, i: (i, j))
    extra = () if add is None else (add,)
    if ride is not None:
        (out,), landed = _call(body, name, (N // tn, M // tm), [a_spec, b_spec] + [o_spec] * len(extra), [o_spec],
                               [jax.ShapeDtypeStruct((M, N), out_dtype)], (a, b, *extra), ride=ride)
        return out, landed
    return pl.pallas_call(
        body,
        name=name,
        grid=(N // tn, M // tm),
        in_specs=[a_spec, b_spec] + [o_spec] * len(extra),
        out_specs=o_spec,
        out_shape=jax.ShapeDtypeStruct((M, N), out_dtype),
        compiler_params=_cp(("parallel", "parallel")),
    )(a, b, *extra)


_CHIP_ORDER = (0, 2, 1, 3)


def _rms_inproj_gather(x, norm_w, wt_shard, chip):
    L = x.shape[0]
    tm = _tile(L, 1024)
    ni = L // tm
    r = IN_W // N_DEV
    tn = 2 * r

    def body(chip_ref, x_ref, nw_ref, shard, proj_ref, hn_hbm, wt_hbm, hn_scr, w_scr, send_sems, recv_sems, loc_sems):
        jc, i = pl.program_id(0), pl.program_id(1)
        xx, yy, c = lax.axis_index("x"), lax.axis_index("y"), lax.axis_index("c")
        me, sibling = (xx, yy, c), (xx, yy, 1 - c)
        chips = [(1 - xx, yy), (xx, 1 - yy), (1 - xx, 1 - yy)]

        def slot(dev):
            return wt_hbm.at[4 * dev[0] + 2 * dev[1] + dev[2]]

        def copy(q, block, to, src=None):
            return pltpu.make_async_remote_copy(
                src_ref=slot(block) if src is None else src, dst_ref=slot(block),
                send_sem=send_sems.at[q], recv_sem=recv_sems.at[q], device_id=to, device_id_type=MESH)

        def rows_of(buf, core):
            return w_scr.at[buf, pl.ds(pl.multiple_of(core * r, 16), r)]

        mine = pltpu.make_async_copy(shard, slot(me), loc_sems.at[0])
        sends = [copy(0, me, sibling, src=shard)] + [copy(1 + j, me, (*ch, c), src=shard) for j, ch in enumerate(chips[:2])]
        relay_block = (xx + (1 - c) * (1 - 2 * xx), yy + c * (1 - 2 * yy), c)
        relay = copy(3, relay_block, (xx + c * (1 - 2 * xx), yy + (1 - c) * (1 - 2 * yy), c))
        first = jnp.logical_and(jc == 0, i == 0)

        @pl.when(first)
        def _():
            mine.start()
            for cp in sends:
                cp.start()
            own = pltpu.make_async_copy(shard, rows_of(0, c), loc_sems.at[1])
            own.start()
            copy(0, sibling, me).wait_recv()
            sib = pltpu.make_async_copy(slot(sibling), rows_of(0, 1 - c), loc_sems.at[2])
            sib.start()
            own.wait()
            sib.wait()

        def to_vmem(j, ch):
            pltpu.make_async_copy(slot((*ch, c)), rows_of((1 + j) % 2, c), loc_sems.at[1 + j]).start()

        @pl.when(jnp.logical_and(jc == 1, i == 0))
        def _():
            for j in range(2):
                copy(1 + j, (*chips[j], c), me).wait_recv()
                copy(4 + j, (*chips[j], c), sibling).start()
            relay.start()
            to_vmem(0, chips[0])

        @pl.when(jnp.logical_and(jc == 1, i == ni // 2))
        def _():
            to_vmem(1, chips[1])

        @pl.when(jnp.logical_and(jc == 2, i == ni // 2))
        def _():
            copy(3, (*chips[2], c), me).wait_recv()
            copy(6, (*chips[2], c), sibling).start()
            to_vmem(2, chips[2])

        for j, ch in enumerate(chips):
            @pl.when(jnp.logical_and(jc == 1 + j, i == 0))
            def _(j=j, ch=ch):
                buf = (1 + j) % 2
                copy(4 + j, (*ch, 1 - c), me).wait_recv()
                passed = pltpu.make_async_copy(slot((*ch, 1 - c)), rows_of(buf, 1 - c), loc_sems.at[4 + j])
                passed.start()
                pltpu.make_async_copy(slot((*ch, c)), rows_of(buf, c), loc_sems.at[1 + j]).wait()
                passed.wait()

        rows = pl.ds(pl.multiple_of(i * tm, tm), tm)

        @pl.when(jc == 0)
        def _():
            xv = x_ref[...]
            rstd = lax.rsqrt(jnp.mean(xv * xv, axis=-1, keepdims=True) + NORM_EPS)
            hn_scr[rows, :] = (xv * rstd * nw_ref[...]).astype(BF16)

        keep_hn = pltpu.make_async_copy(hn_scr, hn_hbm, loc_sems.at[7])

        @pl.when(jnp.logical_and(jc == 1, i == 0))
        def _():
            keep_hn.start()

        for buf in range(2):
            @pl.when(jc % 2 == buf)
            def _(buf=buf):
                proj_ref[...] = lax.dot_general(hn_scr[rows, :], w_scr[buf], _NT, preferred_element_type=F32)

        @pl.when(jnp.logical_and(jc == 3, i == ni - 1))
        def _():
            for cp in sends + [relay]:
                cp.wait_send()
            for j, ch in enumerate(chips):
                copy(4 + j, (*ch, c), sibling).wait_send()
            mine.wait()
            keep_hn.wait()

    def tile_of(jc, chip_ref):
        mask = jnp.where(jc == 1, _CHIP_ORDER[1], jnp.where(jc == 2, _CHIP_ORDER[2], jnp.where(jc == 3, _CHIP_ORDER[3], 0)))
        return jnp.bitwise_xor(chip_ref[0], mask)

    held = lambda jc, i: jnp.where(jc == 0, i, ni - 1)
    return pl.pallas_call(
        body,
        name="rms_inproj_gather",
        grid_spec=pltpu.PrefetchScalarGridSpec(
            num_scalar_prefetch=1,
            grid=(4, ni),
            in_specs=[pl.BlockSpec((tm, D_MODEL), lambda jc, i, ch: (held(jc, i), 0)),
                      pl.BlockSpec((1, D_MODEL), lambda jc, i, ch: (0, 0)), _ANY],
            out_specs=[pl.BlockSpec((tm, tn), lambda jc, i, ch: (i, tile_of(jc, ch))), _ANY, _ANY],
            scratch_shapes=[pltpu.VMEM((L, D_MODEL), BF16), pltpu.VMEM((2, tn, D_MODEL), BF16),
                            pltpu.SemaphoreType.DMA((7,)), pltpu.SemaphoreType.DMA((7,)), pltpu.SemaphoreType.DMA((8,))],
        ),
        out_shape=[jax.ShapeDtypeStruct((L, IN_W), F32), jax.ShapeDtypeStruct((L, D_MODEL), BF16),
                   jax.ShapeDtypeStruct((N_DEV, r, D_MODEL), BF16)],
        compiler_params=_cp(("arbitrary", "arbitrary")),
    )(chip, x, norm_w.reshape(1, D_MODEL), wt_shard)


def _seg_sum(v):
    a = lax.broadcasted_iota(jnp.int32, (128, 128), 0) // HEAD_DIM
    b = lax.broadcasted_iota(jnp.int32, (128, 128), 1) // HEAD_DIM
    ones = jnp.where(a == b, 1.0, 0.0).astype(BF16)
    hi = v.astype(BF16)
    lo = (v - hi.astype(F32)).astype(BF16)
    return jnp.dot(hi, ones, preferred_element_type=F32) + jnp.dot(lo, ones, preferred_element_type=F32)


def _rot_half(t):
    lane = lax.broadcasted_iota(jnp.int32, t.shape, 1)
    return jnp.where(lane % HEAD_DIM < HEAD_DIM // 2, pltpu.roll(t, 128 - HEAD_DIM // 2, 1),
                     pltpu.roll(t, HEAD_DIM // 2, 1))


def _norm_rope(raw, w, cos, sin):
    r = lax.rsqrt(_seg_sum(raw * raw) * (1.0 / HEAD_DIM) + NORM_EPS)
    tn = raw * r * w
    return r, tn * cos + _rot_half(tn) * sin


def _norm_rope_bwd(d_rot, raw, w, cos, sin):
    r = lax.rsqrt(_seg_sum(raw * raw) * (1.0 / HEAD_DIM) + NORM_EPS)
    d_tn = d_rot * cos + _rot_half(d_rot * sin)
    xh = raw * r
    gw = d_tn * w
    d_raw = r * (gw - xh * (_seg_sum(gw * xh) * (1.0 / HEAD_DIM)))
    return d_raw, d_tn * xh


def _band_mask2(has_prev):
    qi = lax.broadcasted_iota(jnp.int32, (2 * BLOCK, 2 * BLOCK), 0) % BLOCK + BLOCK
    kj = lax.broadcasted_iota(jnp.int32, (2 * BLOCK, 2 * BLOCK), 1)
    rel = qi - kj
    return (rel >= 0) & (rel < BLOCK) & ((kj >= BLOCK) | has_prev)


def _half_tiles(pair):
    lo = lax.broadcasted_iota(jnp.int32, pair.shape, 1) < HEAD_DIM
    sw = pltpu.roll(pair, HEAD_DIM, 1)
    z = jnp.zeros_like(pair)
    return (jnp.where(lo, pair, z).astype(BF16), jnp.where(lo, z, sw).astype(BF16),
            jnp.where(lo, sw, z).astype(BF16), jnp.where(lo, z, pair).astype(BF16))


def _two_rows(top, bottom):
    row = lax.broadcasted_iota(jnp.int32, (2 * BLOCK, 1), 0)
    return jnp.where(row < BLOCK, top, bottom)


def _lane_col(mat, h):
    lane = lax.broadcasted_iota(jnp.int32, mat.shape, 1)
    return jnp.sum(jnp.where(lane == h, mat, 0.0), axis=1, keepdims=True)


_SCALE = 1.0 / math.sqrt(HEAD_DIM)
_NT = (((1,), (1,)), ((), ()))
_NN = (((1,), (0,)), ((), ()))
_TN = (((0,), (0,)), ((), ()))


def _qk_prep(proj, tab, qw, kw):
    L = proj.shape[0]
    tm = _tile(L, 512)

    def body(q_ref, k_ref, t_ref, qw_ref, kw_ref, qo_ref, ko_ref):
        cos, sin = t_ref[:, :128], t_ref[:, 128:]
        for c in range(ATTN_W // 128):
            _, qr = _norm_rope(q_ref[:, c * 128:(c + 1) * 128], qw_ref[...], cos, sin)
            qo_ref[:, c * 128:(c + 1) * 128] = (qr * _SCALE).astype(BF16)
        for c in range(KV_W // 128):
            _, kr = _norm_rope(k_ref[:, c * 128:(c + 1) * 128], kw_ref[...], cos, sin)
            ko_ref[:, c * 128:(c + 1) * 128] = kr.astype(BF16)

    row = pl.BlockSpec((1, 128), lambda i: (0, 0))
    return pl.pallas_call(
        body,
        name="qk_prep",
        grid=(L // tm,),
        in_specs=[pl.BlockSpec((tm, ATTN_W), lambda i: (i, 0)), pl.BlockSpec((tm, KV_W), lambda i: (i, 4)),
                  pl.BlockSpec((tm, 256), lambda i: (i, 0)), row, row],
        out_specs=[pl.BlockSpec((tm, ATTN_W), lambda i: (i, 0)), pl.BlockSpec((tm, KV_W), lambda i: (i, 0))],
        out_shape=[jax.ShapeDtypeStruct((L, ATTN_W), BF16), jax.ShapeDtypeStruct((L, KV_W), BF16)],
        compiler_params=_cp(("parallel",)),
    )(proj, proj, tab, jnp.tile(qw, 2).reshape(1, 128), jnp.tile(kw, 2).reshape(1, 128))


def _group_tiles(g, kt, vt):
    a, b = divmod(g, 2)
    return kt[a][2 * b], kt[a][2 * b + 1], vt[a][2 * b], vt[a][2 * b + 1]


def _attn_fwd(q, k, proj, sinks, ride):
    L = proj.shape[0]
    nb = L // BLOCK

    def body(q_ref, kc_ref, kp_ref, vc_ref, vp_ref, z0_ref, z1_ref, sink_ref, og_ref, o_ref, lse_ref):
        i = pl.program_id(0)
        mask = _band_mask2(i > 0)
        z = jnp.concatenate([z0_ref[...], z1_ref[...]], axis=1)
        lane = lax.broadcasted_iota(jnp.int32, (BLOCK, 128), 1)
        kt = [_half_tiles(jnp.concatenate([kp_ref[:, a * 128:(a + 1) * 128], kc_ref[:, a * 128:(a + 1) * 128]],
                                          axis=0).astype(F32)) for a in range(2)]
        vt = [_half_tiles(jnp.concatenate([vp_ref[:, a * 128:(a + 1) * 128], vc_ref[:, a * 128:(a + 1) * 128]],
                                          axis=0)) for a in range(2)]
        lse_mat = jnp.zeros((BLOCK, 128), F32)
        outs = []
        for g in range(N_KV):
            k_lo, k_hi, v_lo, v_hi = _group_tiles(g, kt, vt)
            q2 = jnp.concatenate([q_ref[:, 2 * g * 128:(2 * g + 1) * 128],
                                  q_ref[:, (2 * g + 1) * 128:(2 * g + 2) * 128]], axis=0)
            acc = jnp.zeros((2 * BLOCK, 128), F32)
            for half, (kh, vh) in enumerate(((k_lo, v_lo), (k_hi, v_hi))):
                h_top, h_bot = 4 * g + half, 4 * g + 2 + half
                s = jnp.where(mask, lax.dot_general(q2, kh, _NT, preferred_element_type=F32), -1e30)
                sink = _two_rows(sink_ref[h_top], sink_ref[h_bot])
                m = jnp.maximum(jnp.max(s, axis=-1, keepdims=True), sink)
                e = jnp.exp(s - m)
                den = jnp.sum(e, axis=-1, keepdims=True) + jnp.exp(sink - m)
                p = e * (1.0 / den)
                acc = acc + jnp.dot(p.astype(BF16), vh, preferred_element_type=F32)
                lse = m + jnp.log(den)
                lse_mat = jnp.where(lane == h_top, lse[:BLOCK], lse_mat)
                lse_mat = jnp.where(lane == h_bot, lse[BLOCK:], lse_mat)
            outs += [acc[:BLOCK], acc[BLOCK:]]
        o = jnp.concatenate(outs, axis=1)
        o_ref[...] = o
        og_ref[...] = o * _silu(z)
        lse_ref[...] = lse_mat

    prev = lambda i: jnp.maximum(i - 1, 0)
    return _call(
        body, "attn_fwd", (nb,),
        [pl.BlockSpec((BLOCK, ATTN_W), lambda i: (i, 0)),
         pl.BlockSpec((BLOCK, KV_W), lambda i: (i, 0)),
         pl.BlockSpec((BLOCK, KV_W), lambda i: (prev(i), 0)),
         pl.BlockSpec((BLOCK, KV_W), lambda i: (i, 5)),
         pl.BlockSpec((BLOCK, KV_W), lambda i: (prev(i), 5)),
         pl.BlockSpec((BLOCK, 512), lambda i: (i, 3)),
         pl.BlockSpec((BLOCK, 512), lambda i: (i, 4)),
         pl.BlockSpec(memory_space=pltpu.SMEM)],
        [pl.BlockSpec((BLOCK, ATTN_W), lambda i: (i, 0)),
         pl.BlockSpec((BLOCK, ATTN_W), lambda i: (i, 0)),
         pl.BlockSpec((BLOCK, 128), lambda i: (i, 0))],
        [jax.ShapeDtypeStruct((L, ATTN_W), F32), jax.ShapeDtypeStruct((L, ATTN_W), F32),
         jax.ShapeDtypeStruct((L, 128), F32)],
        (q, k, k, proj, proj, proj, proj, sinks), ride=ride)


def _attn_bwd(q, k, proj, sinks, d_o, o, lse, ride):
    L = proj.shape[0]
    nb = L // BLOCK

    def body(q_ref, kc_ref, kp_ref, vc_ref, vp_ref, do_ref, o_ref, lse_ref, sink_ref,
             dq_ref, dk_ref, dv_ref, gs_ref, ck_scr, cv_scr):
        i = pl.program_id(0)

        @pl.when(i == 0)
        def _():
            gs_ref[...] = jnp.zeros_like(gs_ref)
            ck_scr[...] = jnp.zeros_like(ck_scr)
            cv_scr[...] = jnp.zeros_like(cv_scr)

        @pl.when(i == nb)
        def _():
            dk_ref[...] = ck_scr[...]
            dv_ref[...] = cv_scr[...]

        @pl.when(i < nb)
        def _():
            mask = _band_mask2(i > 0)
            lane = lax.broadcasted_iota(jnp.int32, (1, 128), 1)
            lo = lax.broadcasted_iota(jnp.int32, (2 * BLOCK, 128), 1) < HEAD_DIM
            lse_c = lse_ref[...]
            kt = [_half_tiles(jnp.concatenate([kp_ref[:, a * 128:(a + 1) * 128], kc_ref[:, a * 128:(a + 1) * 128]],
                                              axis=0).astype(F32)) for a in range(2)]
            vt = [_half_tiles(jnp.concatenate([vp_ref[:, a * 128:(a + 1) * 128], vc_ref[:, a * 128:(a + 1) * 128]],
                                              axis=0)) for a in range(2)]
            gs = jnp.zeros((1, 128), F32)
            dq_parts = []
            dk_acc = [jnp.zeros((2 * BLOCK, 128), F32) for _ in range(2)]
            dv_acc = [jnp.zeros((2 * BLOCK, 128), F32) for _ in range(2)]
            for g in range(N_KV):
                a, b = divmod(g, 2)
                k_lo, k_hi, v_lo, v_hi = _group_tiles(g, kt, vt)
                t0, t1 = slice(2 * g * 128, (2 * g + 1) * 128), slice((2 * g + 1) * 128, (2 * g + 2) * 128)
                q2 = jnp.concatenate([q_ref[:, t0], q_ref[:, t1]], axis=0)
                do2 = jnp.concatenate([do_ref[:, t0], do_ref[:, t1]], axis=0)
                prod = do2 * jnp.concatenate([o_ref[:, t0], o_ref[:, t1]], axis=0)
                do2_b = do2.astype(BF16)
                dq2 = jnp.zeros((2 * BLOCK, 128), F32)
                dk_h, dv_h = [], []
                for half, (kh, vh) in enumerate(((k_lo, v_lo), (k_hi, v_hi))):
                    h_top, h_bot = 4 * g + half, 4 * g + 2 + half
                    lse = jnp.concatenate([_lane_col(lse_c, h_top), _lane_col(lse_c, h_bot)], axis=0)
                    sink = _two_rows(sink_ref[h_top], sink_ref[h_bot])
                    delta = jnp.sum(jnp.where(lo == (half == 0), prod, 0.0), axis=1, keepdims=True)
                    s = jnp.where(mask, lax.dot_general(q2, kh, _NT, preferred_element_type=F32), -1e30)
                    p = jnp.exp(s - lse)
                    dp = lax.dot_general(do2_b, vh, _NT, preferred_element_type=F32)
                    ds_b = (p * (dp - delta)).astype(BF16)
                    p_b = p.astype(BF16)
                    dq2 = dq2 + jnp.dot(ds_b, kh, preferred_element_type=F32)
                    dk_h.append(lax.dot_general(ds_b, q2, _TN, preferred_element_type=F32))
                    dv_h.append(lax.dot_general(p_b, do2_b, _TN, preferred_element_type=F32))
                    gsink = -jnp.exp(sink - lse) * delta
                    row = lax.broadcasted_iota(jnp.int32, (2 * BLOCK, 1), 0)
                    gs = gs + jnp.where(lane == h_top, jnp.sum(jnp.where(row < BLOCK, gsink, 0.0)), 0.0)
                    gs = gs + jnp.where(lane == h_bot, jnp.sum(jnp.where(row >= BLOCK, gsink, 0.0)), 0.0)
                dq_parts += [dq2[:BLOCK], dq2[BLOCK:]]
                for acc, parts in ((dk_acc, dk_h), (dv_acc, dv_h)):
                    t = jnp.where(lo, parts[0], parts[1])
                    t = t + pltpu.roll(t, HEAD_DIM, 1)
                    acc[a] = acc[a] + jnp.where(lo == (b == 0), t, 0.0)
            dq_ref[...] = jnp.concatenate(dq_parts, axis=1)
            dk_full = jnp.concatenate(dk_acc, axis=1)
            dv_full = jnp.concatenate(dv_acc, axis=1)
            dk_ref[...] = ck_scr[...] + dk_full[:BLOCK]
            dv_ref[...] = cv_scr[...] + dv_full[:BLOCK]
            ck_scr[...] = dk_full[BLOCK:]
            cv_scr[...] = dv_full[BLOCK:]
            gs_ref[...] += gs

    cur = lambda i: jnp.minimum(i, nb - 1)
    prev = lambda i: jnp.maximum(jnp.minimum(i, nb - 1) - 1, 0)
    done = lambda i: jnp.maximum(i - 1, 0)
    bs = pl.BlockSpec
    return _call(
        body, "attn_bwd", (nb + 1,),
        [bs((BLOCK, ATTN_W), lambda i: (cur(i), 0)),
         bs((BLOCK, KV_W), lambda i: (cur(i), 0)), bs((BLOCK, KV_W), lambda i: (prev(i), 0)),
         bs((BLOCK, KV_W), lambda i: (cur(i), 5)), bs((BLOCK, KV_W), lambda i: (prev(i), 5)),
         bs((BLOCK, ATTN_W), lambda i: (cur(i), 0)), bs((BLOCK, ATTN_W), lambda i: (cur(i), 0)),
         bs((BLOCK, 128), lambda i: (cur(i), 0)), bs(memory_space=pltpu.SMEM)],
        [bs((BLOCK, ATTN_W), lambda i: (cur(i), 0)),
         bs((BLOCK, KV_W), lambda i: (done(i), 0)), bs((BLOCK, KV_W), lambda i: (done(i), 0)),
         bs((1, 128), lambda i: (0, 0))],
        [jax.ShapeDtypeStruct((L, ATTN_W), F32), jax.ShapeDtypeStruct((L, KV_W), F32),
         jax.ShapeDtypeStruct((L, KV_W), F32), jax.ShapeDtypeStruct((1, 128), F32)],
        (q, k, k, proj, proj, d_o, o, lse, sinks),
        [pltpu.VMEM((BLOCK, KV_W), F32), pltpu.VMEM((BLOCK, KV_W), F32)], ride)


def _qk_prep_bwd(proj, tab, qw, kw, d_q, d_k, d_v, d_za, d_u, d_zs):
    L = proj.shape[0]
    tm = _tile(L, 512)
    z0 = ATTN_W + 2 * KV_W

    def body(q_ref, k_ref, t_ref, qw_ref, kw_ref, dq_ref, dk_ref, dv_ref, dza_ref, du_ref, dzs_ref,
             out_ref, gq_ref, gk_ref):
        i = pl.program_id(0)

        @pl.when(i == 0)
        def _():
            gq_ref[...] = jnp.zeros_like(gq_ref)
            gk_ref[...] = jnp.zeros_like(gk_ref)

        cos, sin = t_ref[:, :128], t_ref[:, 128:]
        gq = jnp.zeros((1, 128), F32)
        gk = jnp.zeros((1, 128), F32)
        for c in range(ATTN_W // 128):
            cs = slice(c * 128, (c + 1) * 128)
            d_raw, gw = _norm_rope_bwd(dq_ref[:, cs] * _SCALE, q_ref[:, cs], qw_ref[...], cos, sin)
            out_ref[:, cs] = d_raw.astype(BF16)
            gq = gq + jnp.sum(gw, axis=0, keepdims=True)
        for c in range(KV_W // 128):
            cs = slice(c * 128, (c + 1) * 128)
            d_raw, gw = _norm_rope_bwd(dk_ref[:, cs], k_ref[:, cs], kw_ref[...], cos, sin)
            out_ref[:, ATTN_W + c * 128:ATTN_W + (c + 1) * 128] = d_raw.astype(BF16)
            gk = gk + jnp.sum(gw, axis=0, keepdims=True)
        out_ref[:, ATTN_W + KV_W:z0] = dv_ref[...].astype(BF16)
        out_ref[:, z0:z0 + ATTN_W] = dza_ref[...]
        out_ref[:, z0 + ATTN_W:z0 + ATTN_W + SSM_W] = du_ref[...].astype(BF16)
        out_ref[:, z0 + ATTN_W + SSM_W:] = dzs_ref[...]
        gq_ref[...] += gq
        gk_ref[...] += gk

    row = pl.BlockSpec((1, 128), lambda i: (0, 0))
    blk = lambda w, c: pl.BlockSpec((tm, w), lambda i: (i, c))
    return pl.pallas_call(
        body,
        name="qk_prep_bwd",
        grid=(L // tm,),
        in_specs=[blk(ATTN_W, 0), blk(KV_W, 4), blk(256, 0), row, row, blk(ATTN_W, 0), blk(KV_W, 0), blk(KV_W, 0),
                  blk(ATTN_W, 0), blk(SSM_W, 0), blk(SSM_W, 0)],
        out_specs=[blk(IN_W, 0), row, row],
        out_shape=[jax.ShapeDtypeStruct((L, IN_W), BF16), jax.ShapeDtypeStruct((1, 128), F32),
                   jax.ShapeDtypeStruct((1, 128), F32)],
        compiler_params=_cp(("arbitrary",)),
    )(proj, proj, tab, jnp.tile(qw, 2).reshape(1, 128), jnp.tile(kw, 2).reshape(1, 128), d_q, d_k, d_v,
      d_za, d_u, d_zs)


def _cmul(a, b):
    return a[0] * b[0] - a[1] * b[1], a[0] * b[1] + a[1] * b[0]


def _cmul_conj(a, b):
    return a[0] * b[0] + a[1] * b[1], a[1] * b[0] - a[0] * b[1]


def _cadd(a, b):
    return a[0] + b[0], a[1] + b[1]


def _dot3(a, b, dn):
    ah, bh = a.astype(BF16), b.astype(BF16)
    al, bl = (a - ah.astype(F32)).astype(BF16), (b - bh.astype(F32)).astype(BF16)
    d = lambda u, v: lax.dot_general(u, v, dn, preferred_element_type=F32)
    return d(ah, bh) + d(ah, bl) + d(al, bh)


def _s5_discretise(a_re, a_im, ls, cosx, sinx, bt):
    delta = jnp.exp(ls)
    er = jnp.exp(a_re * delta)
    lb = (er * cosx, er * sinx)
    den = a_re * a_re + a_im * a_im
    coef = _cmul_conj((lb[0] - 1.0, lb[1]), (a_re, a_im))
    coef = (coef[0] / den, coef[1] / den)
    return delta, lb, coef, den, _cmul(coef, bt)


def _powers(lb):
    pw = [(jnp.ones_like(lb[0]), jnp.zeros_like(lb[0]))]
    for _ in range(CHUNK):
        pw.append(_cmul(pw[-1], lb))
    return pw


def _block_rows(a, pw, idx):
    blocks = [_cmul(a, pw[i]) for i in idx]
    return (jnp.concatenate([b[0] for b in blocks], axis=-2), jnp.concatenate([b[1] for b in blocks], axis=-2))


def _block_rows_bwd(g, a, pw, idx, g_pw):
    g_a = (jnp.zeros_like(a[0]), jnp.zeros_like(a[0]))
    for j, i in enumerate(idx):
        gj = (g[0][..., j * SSM_H:(j + 1) * SSM_H, :], g[1][..., j * SSM_H:(j + 1) * SSM_H, :])
        g_a = _cadd(g_a, _cmul_conj(gj, pw[i]))
        gp = _cmul_conj(gj, a)
        g_pw[i] = _cadd(g_pw[i], (jnp.sum(gp[0], axis=-2, keepdims=True), jnp.sum(gp[1], axis=-2, keepdims=True)))
    return g_a


_IDX_S = [CHUNK - 1 - s for s in range(CHUNK)]
_IDX_C = list(range(CHUNK + 1))


def _prep_args(p):
    row = lambda t: t.reshape(SSM_G, 1, SSM_P)
    xi = p["a_im"] * jnp.exp(p["log_step"])[:, None]
    return (row(p["a_re"]), row(p["a_im"]), row(jnp.broadcast_to(p["log_step"][:, None], (SSM_G, SSM_P))),
            row(jnp.cos(xi)), row(jnp.sin(xi)), p["b_re"].transpose(0, 2, 1), p["b_im"].transpose(0, 2, 1),
            p["c_re"], p["c_im"])


PREP_GROUPS = 8


def _prep_specs():
    r1 = pl.BlockSpec((PREP_GROUPS, 1, SSM_P), lambda g: (g, 0, 0))
    r16 = pl.BlockSpec((PREP_GROUPS, SSM_H, SSM_P), lambda g: (g, 0, 0))
    return [r1] * 5 + [r16] * 4, r1, r16


def _ssm_prep(p):
    def one_group(q, are, aim, ls, cosx, sinx, btr, bti, cre, cim, mt_ref, s_ref, o_ref, a_ref):
        _, lb, _, _, bb = _s5_discretise(are[q], aim[q], ls[q], cosx[q], sinx[q], (btr[q], bti[q]))
        pw = _powers(lb)
        c = (cre[q], cim[q])
        sc = _block_rows(bb, pw, _IDX_S)
        cl = _block_rows(c, pw, _IDX_C)
        ok = (cl[0][:CW], cl[1][:CW])
        ot = (cl[0][SSM_H:], cl[1][SSM_H:])
        s_ref[q] = jnp.concatenate([sc[0], sc[1]], axis=1).astype(BF16)
        o_ref[q] = jnp.concatenate([ot[0], -ot[1]], axis=1).astype(BF16)
        a_ref[q] = jnp.concatenate([pw[CHUNK][0], pw[CHUNK][1]], axis=1)
        kt = _dot3(jnp.concatenate([bb[0], -bb[1]], axis=1), jnp.concatenate([ok[0], ok[1]], axis=1), _NT)
        lane = lax.broadcasted_iota(jnp.int32, kt.shape, 1)
        for s in range(CHUNK):
            blk = kt if s == 0 else jnp.where(lane >= SSM_H * s, pltpu.roll(kt, SSM_H * s, 1), 0.0)
            mt_ref[q, s * SSM_H:(s + 1) * SSM_H, :] = blk.astype(BF16)

    def body(*refs):
        for q in range(PREP_GROUPS):
            one_group(q, *refs)

    in_specs, r1, _ = _prep_specs()
    g3 = lambda r, c: pl.BlockSpec((PREP_GROUPS, r, c), lambda g: (g, 0, 0))
    return pl.pallas_call(
        body,
        name="ssm_prep",
        grid=(SSM_G // PREP_GROUPS,),
        in_specs=in_specs,
        out_specs=[g3(CW, CW), g3(CW, 2 * SSM_P), g3(CW, 2 * SSM_P), g3(1, 2 * SSM_P)],
        out_shape=[jax.ShapeDtypeStruct((SSM_G, CW, CW), BF16), jax.ShapeDtypeStruct((SSM_G, CW, 2 * SSM_P), BF16),
                   jax.ShapeDtypeStruct((SSM_G, CW, 2 * SSM_P), BF16),
                   jax.ShapeDtypeStruct((SSM_G, 1, 2 * SSM_P), F32)],
        compiler_params=_cp(("parallel",)),
    )(*_prep_args(p))


def _ssm_prep_bwd(p, g_mt, g_scat, g_ocat, g_a16, ride):
    def body(are, aim, ls, cosx, sinx, btr, bti, cre, cim, gmt_ref, gs_ref, go_ref, ga_ref,
             g_are, g_aim, g_ls, g_btr, g_bti, g_cre, g_cim, ga1_scr, gb1_scr):
        lam = (are[...], aim[...])
        bt = (btr[...], bti[...])
        delta, lb, coef, den, bb = _s5_discretise(lam[0], lam[1], ls[...], cosx[...], sinx[...], bt)
        pw = _powers(lb)
        c = (cre[...], cim[...])
        ok = _block_rows(c, pw, _IDX_C[:CHUNK])
        g_pw =[(jnp.zeros_like(lb[0]), jnp.zeros_like(lb[0])) for _ in range(CHUNK + 1)]
        lane = lax.broadcasted_iota(jnp.int32, (SSM_H, CW), 1)
        for q in range(PREP_GROUPS):
            g_kt = gmt_ref[q, :SSM_H, :]
            for s in range(1, CHUNK):
                blk = gmt_ref[q, s * SSM_H:(s + 1) * SSM_H, :]
                g_kt = g_kt + jnp.where(lane < CW - SSM_H * s, pltpu.roll(blk, CW - SSM_H * s, 1), 0.0)
            a1 = jnp.concatenate([bb[0][q], -bb[1][q]], axis=1)
            b1 = jnp.concatenate([ok[0][q], ok[1][q]], axis=1)
            ga1_scr[q] = _dot3(g_kt, b1, _NN)
            gb1_scr[q] = _dot3(g_kt, a1, _TN)
        g_a1, g_b1 = ga1_scr[...], gb1_scr[...]
        g_bb = (g_a1[..., :SSM_P], -g_a1[..., SSM_P:])
        gs = gs_ref[...]
        g_bb = _cadd(g_bb, _block_rows_bwd((gs[..., :SSM_P], gs[..., SSM_P:]), bb, pw, _IDX_S, g_pw))
        go = go_ref[...]
        pad = jnp.zeros_like(go[..., :SSM_H, :SSM_P])
        g_cl = (jnp.concatenate([g_b1[..., :SSM_P], pad], axis=-2) + jnp.concatenate([pad, go[..., :SSM_P]], axis=-2),
                jnp.concatenate([g_b1[..., SSM_P:], pad], axis=-2) - jnp.concatenate([pad, go[..., SSM_P:]], axis=-2))
        g_c = _block_rows_bwd(g_cl, c, pw, _IDX_C, g_pw)
        ga = ga_ref[...]
        g_pw[CHUNK] = _cadd(g_pw[CHUNK], (ga[..., :SSM_P], ga[..., SSM_P:]))
        g_lb = (jnp.zeros_like(lb[0]), jnp.zeros_like(lb[0]))
        for l in range(CHUNK - 1, -1, -1):
            g_lb = _cadd(g_lb, _cmul_conj(g_pw[l + 1], pw[l]))
            g_pw[l] = _cadd(g_pw[l], _cmul_conj(g_pw[l + 1], lb))
        g_bt = _cmul_conj(g_bb, coef)
        gc = _cmul_conj(g_bb, bt)
        g_coef = (jnp.sum(gc[0], axis=-2, keepdims=True), jnp.sum(gc[1], axis=-2, keepdims=True))
        lam_den = (lam[0] / den, lam[1] / den)
        g_lb = _cadd(g_lb, _cmul(g_coef, lam_den))
        t = _cmul(_cmul_conj(g_coef, coef), lam_den)
        g_x = _cmul_conj(g_lb, lb)
        g_are[...] = g_x[0] * delta - t[0]
        g_aim[...] = g_x[1] * delta - t[1]
        g_ls[...] = (g_x[0] * lam[0] + g_x[1] * lam[1]) * delta
        g_btr[...] = g_bt[0]
        g_bti[...] = g_bt[1]
        g_cre[...] = g_c[0]
        g_cim[...] = g_c[1]

    in_specs, r1, r16 = _prep_specs()
    g3 = lambda r, c: pl.BlockSpec((PREP_GROUPS, r, c), lambda g: (g, 0, 0))
    rows = jax.ShapeDtypeStruct((SSM_G, 1, SSM_P), F32)
    mats = jax.ShapeDtypeStruct((SSM_G, SSM_H, SSM_P), F32)
    (g_are, g_aim, g_ls, g_btr, g_bti, g_cre, g_cim), landed = _call(
        body, "ssm_prep_bwd", (SSM_G // PREP_GROUPS,),
        in_specs + [g3(CW, CW), g3(CW, 2 * SSM_P), g3(CW, 2 * SSM_P), g3(1, 2 * SSM_P)],
        [r1] * 3 + [r16] * 4, [rows] * 3 + [mats] * 4, (*_prep_args(p), g_mt, g_scat, g_ocat, g_a16),
        [pltpu.VMEM((PREP_GROUPS, SSM_H, 2 * SSM_P), F32), pltpu.VMEM((PREP_GROUPS, CW, 2 * SSM_P), F32)], ride)
    grads = dict(a_re=g_are.reshape(SSM_G, SSM_P), a_im=g_aim.reshape(SSM_G, SSM_P),
                 log_step=jnp.sum(g_ls.reshape(SSM_G, SSM_P), axis=1),
                 b_re=g_btr.transpose(0, 2, 1), b_im=g_bti.transpose(0, 2, 1), c_re=g_cre, c_im=g_cim)
    return grads, landed


def _cmul_const(xv, ar, ai):
    return xv * ar + pltpu.roll(xv, SSM_P, 1) * ai


def _chunk_scan(inc, a_row, reverse):
    n = inc.shape[0]
    lane = lax.broadcasted_iota(jnp.int32, (1, 2 * SSM_P), 1)
    row = lax.broadcasted_iota(jnp.int32, inc.shape, 0)
    sign = jnp.where(lane < SSM_P, -1.0, 1.0)
    ar = jnp.where(lane < SSM_P, a_row, pltpu.roll(a_row, SSM_P, 1))
    ai = jnp.where(lane < SSM_P, pltpu.roll(a_row, SSM_P, 1), a_row)
    if reverse:
        ai = -ai
    xv = inc
    s = 1
    while s < n:
        if reverse:
            sh = jnp.where(row < n - s, pltpu.roll(xv, n - s, 0), 0.0)
        else:
            sh = jnp.where(row >= s, pltpu.roll(xv, s, 0), 0.0)
        xv = xv + _cmul_const(sh, ar, ai * sign)
        ar, ai = ar * ar - ai * ai, 2.0 * ar * ai
        s *= 2
    return xv


def _shift_rows(xv, reverse):
    n = xv.shape[0]
    row = lax.broadcasted_iota(jnp.int32, xv.shape, 0)
    if reverse:
        return jnp.where(row < n - 1, pltpu.roll(xv, n - 1, 0), 0.0)
    return jnp.where(row >= 1, pltpu.roll(xv, 1, 0), 0.0)


GB = 128 // SSM_H
U_COL0 = (ATTN_W + 2 * KV_W + ATTN_W) // 128


HALF = CHUNK // 2


def _chunk_perm():
    r = jnp.arange(HALF * 128)
    t, g8, h = r // 128, (r % 128) // SSM_H, r % SSM_H
    return ((g8 * 128 + t * SSM_H + h)[:, None] == jnp.arange(GB * 128)[None, :]).astype(BF16)


def _load_perm(p_hbm, p_scr, sem):
    @pl.when(pl.program_id(0) == 0)
    def _():
        cp = pltpu.make_async_copy(p_hbm, p_scr, sem)
        cp.start()
        cp.wait()


def _rows_to_chunks(pieces, perm):
    halves = [jnp.dot(jnp.concatenate(pieces[k * HALF:(k + 1) * HALF], axis=1).astype(BF16), perm,
                      preferred_element_type=F32).astype(BF16) for k in range(2)]
    return [jnp.concatenate([hv[:, g * 128:(g + 1) * 128] for hv in halves], axis=1) for g in range(GB)]


def _chunks_to_rows(groups, perm, two_pass):
    pieces = []
    for k in range(2):
        v = jnp.concatenate([gv[:, k * 128:(k + 1) * 128] for gv in groups], axis=1)
        hi = v.astype(BF16)
        out = lax.dot_general(hi, perm, _NT, preferred_element_type=F32)
        if two_pass:
            lo = (v - hi.astype(F32)).astype(BF16)
            out = out + lax.dot_general(lo, perm, _NT, preferred_element_type=F32)
        pieces += [out[:, t * 128:(t + 1) * 128] for t in range(HALF)]
    return pieces


def _ssm_fwd(proj, perm, mt, scat, ocat, a16, d_skip, ride):
    L = proj.shape[0]
    nc = L // CHUNK

    def body(u_ref, p_hbm, mt_ref, s_ref, o_ref, a_ref, d_ref, y_ref, yg_ref, h_ref, p_scr, sem):
        _load_perm(p_hbm, p_scr, sem)
        perm = p_scr[...]
        rows = [pl.ds(t, nc, stride=CHUNK) for t in range(CHUNK)]
        us = [u_ref[r, :] for r in rows]
        ua = _rows_to_chunks(us, perm)
        ys = []
        for g in range(GB):
            uv = ua[g]
            inc = jnp.dot(uv, s_ref[g], preferred_element_type=F32)
            hx = _shift_rows(_chunk_scan(inc, a_ref[g], False), False)
            h_ref[g] = hx
            ys.append(jnp.dot(uv, mt_ref[g], preferred_element_type=F32)
                      + lax.dot_general(hx.astype(BF16), o_ref[g], _NT, preferred_element_type=F32))
        yp = _chunks_to_rows(ys, perm, True)
        for t, r in enumerate(rows):
            y = yp[t] + d_ref[...] * us[t]
            y_ref[r, :] = y
            yg_ref[r, :] = _gelu(y)

    g3 = lambda r, c: pl.BlockSpec((GB, r, c), lambda g: (g, 0, 0))
    col = pl.BlockSpec((L, 128), lambda g: (0, g))
    return _call(
        body, "ssm_fwd", (SSM_G // GB,),
        [pl.BlockSpec((L, 128), lambda g: (0, U_COL0 + g)), _ANY,
         g3(CW, CW), g3(CW, 2 * SSM_P), g3(CW, 2 * SSM_P), g3(1, 2 * SSM_P),
         pl.BlockSpec((1, 128), lambda g: (0, g))],
        [col, col, g3(nc, 2 * SSM_P)],
        [jax.ShapeDtypeStruct((L, SSM_W), F32), jax.ShapeDtypeStruct((L, SSM_W), F32),
         jax.ShapeDtypeStruct((SSM_G, nc, 2 * SSM_P), F32)],
        (proj, perm, mt, scat, ocat, a16, d_skip.reshape(1, SSM_W)),
        [pltpu.VMEM((HALF * 128, GB * 128), BF16), pltpu.SemaphoreType.DMA], ride)


def _ssm_bwd(d_yg, y, proj, hx, perm, mt, scat, ocat, a16, d_skip, ride):
    L = proj.shape[0]
    nc = L // CHUNK

    def body(dg_ref, y_ref, u_ref, h_ref, p_hbm, mt_ref, s_ref, o_ref, a_ref, d_ref,
             du_ref, gmt_ref, gs_ref, go_ref, ga_ref, gd_ref, p_scr, sem):
        _load_perm(p_hbm, p_scr, sem)
        perm = p_scr[...]
        rows = [pl.ds(t, nc, stride=CHUNK) for t in range(CHUNK)]
        us = [u_ref[r, :] for r in rows]
        dys = [dg_ref[r, :] * _dgelu(y_ref[r, :]) for r in rows]
        gd = jnp.zeros((1, 128), F32)
        for uv, dy in zip(us, dys):
            gd = gd + jnp.sum(dy * uv, axis=0, keepdims=True)
        gd_ref[...] = gd
        ua = _rows_to_chunks(us, perm)
        dya = _rows_to_chunks(dys, perm)
        lane = lax.broadcasted_iota(jnp.int32, (1, 2 * SSM_P), 1)
        dus = []
        for g in range(GB):
            uv, dy, hx_v = ua[g], dya[g], h_ref[g]
            dh = jnp.dot(dy, o_ref[g], preferred_element_type=F32)
            dinc = _shift_rows(_chunk_scan(dh, a_ref[g], True), True)
            dinc_b = dinc.astype(BF16)
            dus.append(lax.dot_general(dy, mt_ref[g], _NT, preferred_element_type=F32)
                       + lax.dot_general(dinc_b, s_ref[g], _NT, preferred_element_type=F32))
            gmt_ref[g] = lax.dot_general(uv, dy, _TN, preferred_element_type=F32)
            gs_ref[g] = lax.dot_general(uv, dinc_b, _TN, preferred_element_type=F32)
            go_ref[g] = lax.dot_general(dy, hx_v.astype(BF16), _TN, preferred_element_type=F32)
            p1 = dinc * hx_v
            p2 = pltpu.roll(dinc, SSM_P, 1) * hx_v
            t1 = jnp.sum(p1 + pltpu.roll(p1, SSM_P, 1), axis=0, keepdims=True)
            t2 = jnp.sum(p2 - pltpu.roll(p2, SSM_P, 1), axis=0, keepdims=True)
            ga_ref[g] = jnp.where(lane < SSM_P, t1, pltpu.roll(t2, SSM_P, 1))
        dup = _chunks_to_rows(dus, perm, False)
        for t, r in enumerate(rows):
            du_ref[r, :] = dup[t] + d_ref[...] * dys[t]

    g3 = lambda r, c: pl.BlockSpec((GB, r, c), lambda g: (g, 0, 0))
    col = pl.BlockSpec((L, 128), lambda g: (0, g))
    row = pl.BlockSpec((1, 128), lambda g: (0, g))
    return _call(
        body, "ssm_bwd", (SSM_G // GB,),
        [col, col, pl.BlockSpec((L, 128), lambda g: (0, U_COL0 + g)), g3(nc, 2 * SSM_P), _ANY,
         g3(CW, CW), g3(CW, 2 * SSM_P), g3(CW, 2 * SSM_P), g3(1, 2 * SSM_P), row],
        [col, g3(CW, CW), g3(CW, 2 * SSM_P), g3(CW, 2 * SSM_P), g3(1, 2 * SSM_P), row],
        [jax.ShapeDtypeStruct((L, SSM_W), F32), jax.ShapeDtypeStruct((SSM_G, CW, CW), F32),
         jax.ShapeDtypeStruct((SSM_G, CW, 2 * SSM_P), F32), jax.ShapeDtypeStruct((SSM_G, CW, 2 * SSM_P), F32),
         jax.ShapeDtypeStruct((SSM_G, 1, 2 * SSM_P), F32), jax.ShapeDtypeStruct((1, SSM_W), F32)],
        (d_yg, y, proj, hx, perm, mt, scat, ocat, a16, d_skip.reshape(1, SSM_W)),
        [pltpu.VMEM((HALF * 128, GB * 128), BF16), pltpu.SemaphoreType.DMA], ride)


def _merge(og, yg, w_glu, proj, b_glu, wa, ws):
    L = og.shape[0]
    tm = _tile(L, 256)

    def body(og_ref, yg_ref, wg_ref, z0_ref, z1_ref, b_ref, wa_ref, ws_ref, m_ref, gp_ref):
        zs = jnp.concatenate([z0_ref[...], z1_ref[...]], axis=1)
        ygv = yg_ref[...]
        gpre = jnp.dot(ygv.astype(BF16), wg_ref[...], preferred_element_type=F32)
        gp_ref[...] = gpre
        os_ = ygv * _sigmoid(gpre + b_ref[...]) * _silu(zs)
        ogv = og_ref[...]
        ra = lax.rsqrt(jnp.mean(ogv * ogv, axis=-1, keepdims=True) + NORM_EPS)
        rs = lax.rsqrt(jnp.mean(os_ * os_, axis=-1, keepdims=True) + NORM_EPS)
        m_ref[:, :ATTN_W] = (ogv * ra * wa_ref[...]).astype(BF16)
        m_ref[:, ATTN_W:] = (os_ * rs * ws_ref[...]).astype(BF16)

    row = lambda w: pl.BlockSpec((1, w), lambda i: (0, 0))
    return pl.pallas_call(
        body,
        name="merge",
        grid=(L // tm,),
        in_specs=[pl.BlockSpec((tm, ATTN_W), lambda i: (i, 0)), pl.BlockSpec((tm, SSM_W), lambda i: (i, 0)),
                  pl.BlockSpec((SSM_W, SSM_W), lambda i: (0, 0)),
                  pl.BlockSpec((tm, 512), lambda i: (i, 7)), pl.BlockSpec((tm, 512), lambda i: (i, 8)),
                  row(SSM_W), row(ATTN_W), row(SSM_W)],
        out_specs=[pl.BlockSpec((tm, D_MODEL), lambda i: (i, 0)), pl.BlockSpec((tm, SSM_W), lambda i: (i, 0))],
        out_shape=[jax.ShapeDtypeStruct((L, D_MODEL), BF16), jax.ShapeDtypeStruct((L, SSM_W), F32)],
        compiler_params=_cp(("parallel",)),
    )(og, yg, w_glu, proj, proj, b_glu.reshape(1, SSM_W), wa.reshape(1, ATTN_W), ws.reshape(1, SSM_W))


def _outproj_loss(merged, w_out, x, target):
    L = x.shape[0]
    tm, tn = _tile(L, 512), 1024
    ni, nj = L // tm, D_MODEL // tn

    def body(m_ref, w_ref, x_ref, t_ref, d_ref, db_ref, l_ref):
        out = x_ref[...] + jnp.dot(m_ref[...], w_ref[...], preferred_element_type=F32)
        diff = out - t_ref[...]
        d = diff * (1.0 / D_MODEL)
        d_ref[...] = d
        db_ref[...] = d.astype(BF16)
        l_ref[...] = jnp.full((1, 8, 128), jnp.sum(diff * diff), F32)

    return pl.pallas_call(
        body,
        name="outproj_loss",
        grid=(nj, ni),
        in_specs=[pl.BlockSpec((tm, D_MODEL), lambda j, i: (i, 0)),
                  pl.BlockSpec((D_MODEL, tn), lambda j, i: (0, j)),
                  pl.BlockSpec((tm, tn), lambda j, i: (i, j)),
                  pl.BlockSpec((tm, tn), lambda j, i: (i, j))],
        out_specs=[pl.BlockSpec((tm, tn), lambda j, i: (i, j)), pl.BlockSpec((tm, tn), lambda j, i: (i, j)),
                   pl.BlockSpec((1, 8, 128), lambda j, i: (i * nj + j, 0, 0))],
        out_shape=[jax.ShapeDtypeStruct((L, D_MODEL), F32), jax.ShapeDtypeStruct((L, D_MODEL), BF16),
                   jax.ShapeDtypeStruct((ni * nj, 8, 128), F32)],
        compiler_params=_cp(("parallel", "parallel")),
    )(merged, w_out, x, target)


def _merge_bwd(d_out_b, w_out, og, o, yg, gpre, proj, b_glu, wa, ws):
    L = og.shape[0]
    tm = _tile(L, 256)

    def body(dout_ref, wo_ref, og_ref, o_ref, yg_ref, gp_ref, za0_ref, za1_ref, zs0_ref, zs1_ref, b_ref, wa_ref,
             ws_ref, do_ref, dza_ref, dzs_ref, dg_ref, dyg_ref, gwa_ref, gws_ref, gb_ref):
        i = pl.program_id(0)

        @pl.when(i == 0)
        def _():
            gwa_ref[...] = jnp.zeros_like(gwa_ref)
            gws_ref[...] = jnp.zeros_like(gws_ref)
            gb_ref[...] = jnp.zeros_like(gb_ref)

        dm = lax.dot_general(dout_ref[...], wo_ref[...], _NT, preferred_element_type=F32)
        za = jnp.concatenate([za0_ref[...], za1_ref[...]], axis=1)
        zs = jnp.concatenate([zs0_ref[...], zs1_ref[...]], axis=1)
        ogv, dma = og_ref[...], dm[:, :ATTN_W]
        ra = lax.rsqrt(jnp.mean(ogv * ogv, axis=-1, keepdims=True) + NORM_EPS)
        xh = ogv * ra
        gwa_ref[...] += jnp.sum(dma * xh, axis=0, keepdims=True)
        gx = dma * wa_ref[...]
        d_og = ra * (gx - xh * jnp.mean(gx * xh, axis=-1, keepdims=True))
        do_ref[...] = d_og * _silu(za)
        dza_ref[...] = (d_og * o_ref[...] * _dsilu(za)).astype(BF16)
        ygv = yg_ref[...]
        sg = _sigmoid(gp_ref[...] + b_ref[...])
        y2 = ygv * sg
        sz = _silu(zs)
        os_ = y2 * sz
        dms = dm[:, ATTN_W:]
        rs = lax.rsqrt(jnp.mean(os_ * os_, axis=-1, keepdims=True) + NORM_EPS)
        xs = os_ * rs
        gws_ref[...] += jnp.sum(dms * xs, axis=0, keepdims=True)
        gxs = dms * ws_ref[...]
        d_os = rs * (gxs - xs * jnp.mean(gxs * xs, axis=-1, keepdims=True))
        dzs_ref[...] = (d_os * y2 * _dsilu(zs)).astype(BF16)
        d_y2 = d_os * sz
        d_g = d_y2 * ygv * sg * (1.0 - sg)
        dg_ref[...] = d_g.astype(BF16)
        gb_ref[...] += jnp.sum(d_g, axis=0, keepdims=True)
        dyg_ref[...] = d_y2 * sg

    row = lambda w: pl.BlockSpec((1, w), lambda i: (0, 0))
    full = lambda w: pl.BlockSpec((tm, w), lambda i: (i, 0))
    half = lambda c: pl.BlockSpec((tm, 512), lambda i: (i, c))
    return pl.pallas_call(
        body,
        name="merge_bwd",
        grid=(L // tm,),
        in_specs=[full(D_MODEL), pl.BlockSpec((D_MODEL, D_MODEL), lambda i: (0, 0)),
                  full(ATTN_W), full(ATTN_W), full(SSM_W), full(SSM_W),
                  half(3), half(4), half(7), half(8), row(SSM_W), row(ATTN_W), row(SSM_W)],
        out_specs=[full(ATTN_W), full(ATTN_W), full(SSM_W), full(SSM_W), full(SSM_W),
                   row(ATTN_W), row(SSM_W), row(SSM_W)],
        out_shape=[jax.ShapeDtypeStruct((L, ATTN_W), F32), jax.ShapeDtypeStruct((L, ATTN_W), BF16),
                   jax.ShapeDtypeStruct((L, SSM_W), BF16), jax.ShapeDtypeStruct((L, SSM_W), BF16),
                   jax.ShapeDtypeStruct((L, SSM_W), F32),
                   jax.ShapeDtypeStruct((1, ATTN_W), F32), jax.ShapeDtypeStruct((1, SSM_W), F32),
                   jax.ShapeDtypeStruct((1, SSM_W), F32)],
        compiler_params=_cp(("arbitrary",)),
    )(d_out_b, w_out, og, o, yg, gpre, proj, proj, proj, proj, b_glu.reshape(1, SSM_W), wa.reshape(1, ATTN_W),
      ws.reshape(1, SSM_W))


def _rms_bwd_x(x, norm_w, d_hn, d_out, ride):
    L = x.shape[0]
    tm = _tile(L, 256)

    def body(x_ref, w_ref, dh_ref, do_ref, gx_ref, gw_ref):
        i = pl.program_id(0)

        @pl.when(i == 0)
        def _():
            gw_ref[...] = jnp.zeros_like(gw_ref)

        xv, dh = x_ref[...], dh_ref[...]
        r = lax.rsqrt(jnp.mean(xv * xv, axis=-1, keepdims=True) + NORM_EPS)
        xh = xv * r
        gw_ref[...] += jnp.sum(dh * xh, axis=0, keepdims=True)
        gx = dh * w_ref[...]
        gx_ref[...] = do_ref[...] + r * (gx - xh * jnp.mean(gx * xh, axis=-1, keepdims=True))

    blk = pl.BlockSpec((tm, D_MODEL), lambda i: (i, 0))
    row = pl.BlockSpec((1, D_MODEL), lambda i: (0, 0))
    return _call(body, "rms_bwd_x", (L // tm,), [blk, row, blk, blk], [blk, row],
                 [jax.ShapeDtypeStruct((L, D_MODEL), F32), jax.ShapeDtypeStruct((1, D_MODEL), F32)],
                 (x, norm_w.reshape(1, D_MODEL), d_hn, d_out), ride=ride)


def _rope_table(positions):
    inv_freq = ROPE_THETA ** (-jnp.arange(0, HEAD_DIM, 2, dtype=F32) / HEAD_DIM)
    ang = positions.astype(F32)[:, None] * inv_freq
    sign = jnp.where(jnp.arange(128) % HEAD_DIM < HEAD_DIM // 2, -1.0, 1.0)
    return jnp.concatenate([jnp.tile(jnp.cos(ang), (1, 4)), jnp.tile(jnp.sin(ang), (1, 4)) * sign], axis=1)


def _step(x, positions, target, w, core, chip):
    small = {n: w[n] for n in _SMALL}
    tab = _rope_table(positions)
    mt_b, scat_b, ocat_b, a16 = _ssm_prep(small)
    perm = _chunk_perm()
    blocks = lambda t: t.reshape(N_DEV, t.shape[0] // N_DEV, t.shape[1])

    proj, hn, wt_in = _rms_inproj_gather(x, small["norm_w"], w["w_in"].T.astype(BF16), chip)
    wt_in = wt_in.reshape(IN_W, D_MODEL)
    q_rot, k_rot = _qk_prep(proj, tab, small["q_norm_w"], small["k_norm_w"])
    (og, o, lse), (w_glu,) = _attn_fwd(q_rot, k_rot, proj, small["sinks"],
                                       _gather_exchange([w["w_glu"].astype(BF16)]))
    (y, yg, hx), (w_out,) = _ssm_fwd(proj, perm, mt_b, scat_b, ocat_b, a16, small["d_skip"],
                                     _gather_exchange([w["w_out"].astype(BF16)]))
    w_glu, w_out = w_glu.reshape(SSM_W, SSM_W), w_out.reshape(D_MODEL, D_MODEL)
    merged, gpre = _merge(og, yg, w_glu, proj, small["b_glu"], small["attn_out_norm_w"], small["ssm_out_norm_w"])
    d_out, d_out_b, loss_parts = _outproj_loss(merged, w_out, x, target)
    loss = 0.5 * jnp.sum(loss_parts[:, 0, 0]) / D_MODEL

    g_w_out = blocks(_mm(merged, d_out_b, "tn", F32, "grad_w_out", tm=1024))
    d_o, d_za, d_zs, d_g, d_yg1, g_wa, g_ws, g_bglu = _merge_bwd(
        d_out_b, w_out, og, o, yg, gpre, proj, small["b_glu"], small["attn_out_norm_w"], small["ssm_out_norm_w"])
    g_w_glu = blocks(_mm(yg, d_g, "tn", F32, "grad_w_glu"))
    d_yg = _mm(d_g, w_glu, "nt", F32, "d_yg", add=d_yg1)
    (d_u, g_mt, g_scat, g_ocat, g_a16, g_dskip), (ra_out, ra_glu) = _ssm_bwd(
        d_yg, y, proj, hx, perm, mt_b, scat_b, ocat_b, a16, small["d_skip"], _pair_exchange([g_w_out, g_w_glu]))
    p_out = _pair_sum(g_w_out, ra_out, core, BF16, "pair_sum_out")
    p_glu = _pair_sum(g_w_glu, ra_glu, core, BF16, "pair_sum_glu")
    (d_q, d_k, d_v, g_sinks), (rb_out, rb_glu) = _attn_bwd(
        q_rot, k_rot, proj, small["sinks"], d_o, o, lse, _chip_exchange([p_out, p_glu]))
    d_proj, g_qw, g_kw = _qk_prep_bwd(proj, tab, small["q_norm_w"], small["k_norm_w"], d_q, d_k, d_v,
                                      d_za, d_u, d_zs)
    g_qw = g_qw[0, :HEAD_DIM] + g_qw[0, HEAD_DIM:]
    g_kw = g_kw[0, :HEAD_DIM] + g_kw[0, HEAD_DIM:]
    g_in_a = blocks(_mm(d_proj, hn, "tn", F32, "grad_w_in_a", tm=1152, panel=0))
    g_in_b, (ra_a,) = _mm(d_proj, hn, "tn", F32, "grad_w_in_b", tm=1152, panel=1, ride=_pair_exchange([g_in_a]))
    g_in_b = blocks(g_in_b)
    p_a = _pair_sum(g_in_a, ra_a, core, BF16, "pair_sum_in_a")
    d_hn, (rb_a, ra_b) = _mm(d_proj, wt_in, "nn", F32, "d_hn", tm=1024,
                             ride=_both(_chip_exchange([p_a]), _pair_exchange([g_in_b])))
    p_b = _pair_sum(g_in_b, ra_b, core, BF16, "pair_sum_in_b")
    g_small, (rb_b,) = _ssm_prep_bwd(small, g_mt, g_scat, g_ocat, g_a16, _chip_exchange([p_b]))
    (grad_x, g_nw), _ = _rms_bwd_x(x, small["norm_w"], d_hn, d_out, None)

    g_small.update(norm_w=g_nw.reshape(-1), q_norm_w=g_qw.reshape(-1), k_norm_w=g_kw.reshape(-1),
                   sinks=g_sinks[0, :N_HEADS], d_skip=g_dskip.reshape(-1), b_glu=g_bglu.reshape(-1),
                   attn_out_norm_w=g_wa.reshape(-1), ssm_out_norm_w=g_ws.reshape(-1))
    g_packed = _slab_all_reduce(_pack(g_small, loss).reshape(N_DEV, _PACK_ROWS // N_DEV, 128))
    g_packed = g_packed.reshape(_PACK_ROWS, 128)
    grads = _unpack(g_packed, w)
    parts = dict(w_in=([p_a, p_b], [rb_a, rb_b]), w_glu=([p_glu], [rb_glu]), w_out=([p_out], [rb_out]))
    return g_packed[_LOSS_ROW, 0], grad_x, grads, parts


_ANY = pl.BlockSpec(memory_space=pl.ANY)


class _Exchange:
    def __init__(self, arrays, out_shape, sems, start, finish, relay=None):
        self.arrays, self.out_shape, self.sems, self.start, self.finish = arrays, out_shape, sems, start, finish
        self.relay = relay if relay is not None else (lambda ins, outs, sems: None)


def _gather_exchange(blocks):
    n = len(blocks)

    def parts(ins, outs, sems):
        send_sems, recv_sems, local_sems = sems
        x, y, c = lax.axis_index("x"), lax.axis_index("y"), lax.axis_index("c")
        me, sibling = (x, y, c), (x, y, 1 - c)
        chips = [(1 - x, y), (x, 1 - y), (1 - x, 1 - y)]

        def slot(k, dev):
            return outs[k].at[4 * dev[0] + 2 * dev[1] + dev[2]]

        def copy(k, q, block, to, src=None):
            return pltpu.make_async_remote_copy(
                src_ref=slot(k, block) if src is None else src, dst_ref=slot(k, block),
                send_sem=send_sems.at[k, q], recv_sem=recv_sems.at[k, q], device_id=to, device_id_type=MESH)

        mine = [pltpu.make_async_copy(ins[k], slot(k, me), local_sems.at[k]) for k in range(n)]
        first = []
        for k in range(n):
            first.append(copy(k, 0, me, sibling, src=ins[k]))
            first += [copy(k, 1 + j, me, (*chip, c), src=ins[k]) for j, chip in enumerate(chips)]
        return me, sibling, chips, c, copy, mine, first

    def start(ins, outs, sems):
        *_, mine, first = parts(ins, outs, sems)
        for cp in mine + first:
            cp.start()

    def relay(ins, outs, sems):
        me, sibling, chips, c, copy, _, _ = parts(ins, outs, sems)
        for j, chip in enumerate(chips):
            for k in range(n):
                copy(k, 1 + j, (*chip, c), me).wait_recv()
                copy(k, 4 + j, (*chip, c), sibling).start()

    def finish(ins, outs, sems):
        me, sibling, chips, c, copy, mine, first = parts(ins, outs, sems)
        for k in range(n):
            copy(k, 0, sibling, me).wait_recv()
            for j, chip in enumerate(chips):
                copy(k, 4 + j, (*chip, 1 - c), me).wait_recv()
        for cp in first + [copy(k, 4 + j, (*chip, c), sibling) for k in range(n) for j, chip in enumerate(chips)]:
            cp.wait_send()
        for cp in mine:
            cp.wait()

    return _Exchange(blocks, [jax.ShapeDtypeStruct((N_DEV,) + b.shape, b.dtype) for b in blocks],
                     [pltpu.SemaphoreType.DMA((n, 7)), pltpu.SemaphoreType.DMA((n, 7)), pltpu.SemaphoreType.DMA((n,))],
                     start, finish, relay)


def _direct_exchange(arrays, out_lead, fan, route):
    n = len(arrays)

    def copies(ins, outs, sems):
        send_sems, recv_sems = sems
        legs = route(lax.axis_index("x"), lax.axis_index("y"), lax.axis_index("c"))
        return [pltpu.make_async_remote_copy(
            src_ref=ins[k].at[src], dst_ref=outs[k].at[q], send_sem=send_sems.at[k, q], recv_sem=recv_sems.at[k, q],
            device_id=to, device_id_type=MESH) for k in range(n) for src, q, to in legs]

    def start(ins, outs, sems):
        for cp in copies(ins, outs, sems):
            cp.start()

    def finish(ins, outs, sems):
        for cp in copies(ins, outs, sems):
            cp.wait()

    return _Exchange(arrays, [jax.ShapeDtypeStruct((out_lead,) + a.shape[1:], a.dtype) for a in arrays],
                     [pltpu.SemaphoreType.DMA((n, fan)), pltpu.SemaphoreType.DMA((n, fan))], start, finish)


def _pair_exchange(grads):
    return _direct_exchange(grads, 4, 4, lambda x, y, c: [(2 * chip + (1 - c), chip, (x, y, 1 - c))
                                                          for chip in range(4)])


def _chip_exchange(parts):
    def route(x, y, c):
        chips = [(1 - x, y), (x, 1 - y), (1 - x, 1 - y)]
        return [(2 * chip[0] + chip[1], q, (*chip, c)) for q, chip in enumerate(chips)]
    return _direct_exchange(parts, 3, 3, route)


def _both(ex1, ex2):
    n1, s1 = len(ex1.arrays), len(ex1.sems)

    def halves(ins, outs, sems):
        return (ins[:n1], outs[:n1], sems[:s1]), (ins[n1:], outs[n1:], sems[s1:])

    def start(ins, outs, sems):
        h1, h2 = halves(ins, outs, sems)
        ex1.start(*h1)
        ex2.start(*h2)

    def relay(ins, outs, sems):
        h1, h2 = halves(ins, outs, sems)
        ex1.relay(*h1)
        ex2.relay(*h2)

    def finish(ins, outs, sems):
        h1, h2 = halves(ins, outs, sems)
        ex1.finish(*h1)
        ex2.finish(*h2)

    return _Exchange(list(ex1.arrays) + list(ex2.arrays), list(ex1.out_shape) + list(ex2.out_shape),
                     list(ex1.sems) + list(ex2.sems), start, finish, relay)


def _call(body, name, grid, in_specs, out_specs, out_shape, args, scratch_shapes=(), ride=None):
    if ride is None:
        sem = ("arbitrary",) * len(grid)
        return pl.pallas_call(body, name=name, grid=grid, in_specs=in_specs, out_specs=out_specs, out_shape=out_shape,
                              scratch_shapes=list(scratch_shapes), compiler_params=_cp(sem))(*args), None
    n_in, n_out, n_scr, n_x = len(in_specs), len(out_specs), len(scratch_shapes), len(ride.arrays)

    def wrapped(*refs):
        ins, refs = refs[:n_in], refs[n_in:]
        x_in, refs = refs[:n_x], refs[n_x:]
        outs, refs = refs[:n_out], refs[n_out:]
        x_out, refs = refs[:n_x], refs[n_x:]
        scr, sems = refs[:n_scr], refs[n_scr:]
        step, total = pl.program_id(0), grid[0]
        for a in range(1, len(grid)):
            step, total = step * grid[a] + pl.program_id(a), total * grid[a]
        @pl.when(step == 0)
        def _():
            ride.start(x_in, x_out, sems)

        @pl.when(step == max(total - 2, 0))
        def _():
            ride.relay(x_in, x_out, sems)

        body(*ins, *outs, *scr)

        @pl.when(step == total - 1)
        def _():
            ride.finish(x_in, x_out, sems)

    res = pl.pallas_call(
        wrapped, name=name, grid=grid, in_specs=list(in_specs) + [_ANY] * n_x,
        out_specs=list(out_specs) + [_ANY] * n_x, out_shape=list(out_shape) + list(ride.out_shape),
        scratch_shapes=list(scratch_shapes) + list(ride.sems),
        compiler_params=_cp(("arbitrary",) * len(grid)))(*args, *ride.arrays)
    return res[:n_out], list(res[n_out:])


def _pair_sum(g, ra, core, out_dtype, name):
    _, r, C = g.shape
    tr = _tile(r, 576)

    def body(c_ref, g_ref, ra_ref, p_ref):
        p_ref[...] = (g_ref[...] + ra_ref[...]).astype(p_ref.dtype)

    return pl.pallas_call(
        body,
        name=name,
        grid_spec=pltpu.PrefetchScalarGridSpec(
            num_scalar_prefetch=1,
            grid=(4, r // tr),
            in_specs=[pl.BlockSpec((1, tr, C), lambda j, t, c_ref: (2 * j + c_ref[0], t, 0)),
                      pl.BlockSpec((1, tr, C), lambda j, t, c_ref: (j, t, 0))],
            out_specs=pl.BlockSpec((1, tr, C), lambda j, t, c_ref: (j, t, 0)),
        ),
        out_shape=jax.ShapeDtypeStruct((4, r, C), out_dtype),
        compiler_params=_cp(("parallel", "parallel")),
    )(core, g, ra)


def _slab_all_reduce(slab):
    _, r, lanes = slab.shape

    def body(s_ref, o_ref, ra, rb, ps, sems_a, sems_b, sems_c):
        x, y, c = lax.axis_index("x"), lax.axis_index("y"), lax.axis_index("c")
        chips = [(1 - x, y), (x, 1 - y), (1 - x, 1 - y)]
        pair = [pltpu.make_async_remote_copy(
            src_ref=s_ref.at[2 * k + (1 - c)], dst_ref=ra.at[k], send_sem=sems_a.at[0, k], recv_sem=sems_a.at[1, k],
            device_id=(x, y, 1 - c), device_id_type=MESH) for k in range(4)]
        for cp in pair:
            cp.start()
        for cp in pair:
            cp.wait()
        for k in range(4):
            ps[k] = s_ref[2 * k + c] + ra[k]
        cross = [pltpu.make_async_remote_copy(
            src_ref=ps.at[2 * ch[0] + ch[1]], dst_ref=rb.at[q], send_sem=sems_b.at[0, q], recv_sem=sems_b.at[1, q],
            device_id=(*ch, c), device_id_type=MESH) for q, ch in enumerate(chips)]
        for cp in cross:
            cp.start()
        for cp in cross:
            cp.wait()
        me = 4 * x + 2 * y + c
        o_ref[me] = ((ps[2 * x + y] + rb[0]) + rb[1]) + rb[2]
        flips = [(dx, dy, dc) for dx in (0, 1) for dy in (0, 1) for dc in (0, 1) if dx + dy + dc]
        spread = [pltpu.make_async_remote_copy(
            src_ref=o_ref.at[me], dst_ref=o_ref.at[me], send_sem=sems_c.at[0, q], recv_sem=sems_c.at[1, q],
            device_id=(x + dx - 2 * x * dx, y + dy - 2 * y * dy, c + dc - 2 * c * dc), device_id_type=MESH)
            for q, (dx, dy, dc) in enumerate(flips)]
        for cp in spread:
            cp.start()
        for q, (dx, dy, dc) in enumerate(flips):
            peer = 4 * (x + dx - 2 * x * dx) + 2 * (y + dy - 2 * y * dy) + (c + dc - 2 * c * dc)
            pltpu.make_async_remote_copy(
                src_ref=o_ref.at[peer], dst_ref=o_ref.at[peer], send_sem=sems_c.at[0, q], recv_sem=sems_c.at[1, q],
                device_id=(x, y, c), device_id_type=MESH).wait_recv()
        for cp in spread:
            cp.wait_send()

    whole = pl.BlockSpec(memory_space=pltpu.VMEM)
    return pl.pallas_call(
        body, name="slab_all_reduce", in_specs=[whole], out_specs=whole,
        out_shape=jax.ShapeDtypeStruct(slab.shape, F32),
        scratch_shapes=[pltpu.VMEM((4, r, lanes), F32), pltpu.VMEM((3, r, lanes), F32), pltpu.VMEM((4, r, lanes), F32),
                        pltpu.SemaphoreType.DMA((2, 4)), pltpu.SemaphoreType.DMA((2, 3)),
                        pltpu.SemaphoreType.DMA((2, 7))],
        compiler_params=_cp(),
    )(slab)


def _adamw_reduced(ps, rbs, chip, w, m, v, name):
    nh = len(ps)
    R, C = w.shape
    ch = C // nh
    tr = _tile(R, 288)
    nt = R // tr
    c1 = 1.0 - ADAM_B1 ** ADAM_STEP
    c2 = 1.0 - ADAM_B2 ** ADAM_STEP

    def body(c_ref, *refs):
        p_refs, rb_refs = refs[:nh], refs[nh:2 * nh]
        w_ref, m_ref, v_ref, g_ref, d_ref, nm_ref, nv_ref = refs[2 * nh:]
        for h in range(nh):
            @pl.when(pl.program_id(0) == h)
            def _(h=h):
                rb = rb_refs[h]
                gv = p_refs[h][0].astype(F32) + rb[0].astype(F32)
                gv = gv + rb[1].astype(F32)
                gv = gv + rb[2].astype(F32)
                nm = ADAM_B1 * m_ref[...] + (1.0 - ADAM_B1) * gv
                nv = ADAM_B2 * v_ref[...] + (1.0 - ADAM_B2) * (gv * gv)
                g_ref[...] = gv
                nm_ref[...] = nm
                nv_ref[...] = nv
                d_ref[...] = -ADAM_LR * ((nm / c1) / (jnp.sqrt(nv / c2) + ADAM_EPS) + ADAM_WD * w_ref[...])

    def held(h):
        return lambda hh, tt: jnp.where(hh == h, tt, jnp.where(hh < h, 0, nt - 1))

    p_specs = [pl.BlockSpec((1, tr, ch), lambda hh, tt, c_ref, f=held(h): (c_ref[0], f(hh, tt), 0))
               for h in range(nh)]
    rb_specs = [pl.BlockSpec((3, tr, ch), lambda hh, tt, c_ref, f=held(h): (0, f(hh, tt), 0)) for h in range(nh)]
    blk = pl.BlockSpec((tr, ch), lambda hh, tt, c_ref: (tt, hh))
    return pl.pallas_call(
        body,
        name=name,
        grid_spec=pltpu.PrefetchScalarGridSpec(
            num_scalar_prefetch=1, grid=(nh, nt), in_specs=p_specs + rb_specs + [blk] * 3, out_specs=[blk] * 4),
        out_shape=[jax.ShapeDtypeStruct((R, C), F32)] * 4,
        compiler_params=_cp(("arbitrary", "arbitrary")),
    )(chip, *ps, *rbs, w, m, v)


_SMALL = ("norm_w", "q_norm_w", "k_norm_w", "sinks", "a_re", "a_im", "log_step", "b_re", "b_im", "c_re", "c_im",
          "d_skip", "b_glu", "attn_out_norm_w", "ssm_out_norm_w")
_WEIGHTS = ("norm_w", "w_in", "q_norm_w", "k_norm_w", "sinks", "a_re", "a_im", "log_step", "b_re", "b_im", "c_re",
            "c_im", "d_skip", "w_glu", "b_glu", "attn_out_norm_w", "ssm_out_norm_w", "w_out")
_SMALL_2D = dict(norm_w=(1, 2048), q_norm_w=(1, 64), k_norm_w=(1, 64), sinks=(1, 16), a_re=(64, 64), a_im=(64, 64),
                 log_step=(1, 64), b_re=(1024, 64), b_im=(1024, 64), c_re=(1024, 64), c_im=(1024, 64),
                 d_skip=(1, 1024), b_glu=(1, 1024), attn_out_norm_w=(1, 1024), ssm_out_norm_w=(1, 1024))
_P_MINOR = ("b_re", "b_im")


def _flat_form(n, t):
    return t.transpose(0, 2, 1) if n in _P_MINOR else t


def _own_form(n, t, shape):
    if n in _P_MINOR:
        return t.reshape(shape[0], shape[2], shape[1]).transpose(0, 2, 1)
    return t.reshape(shape)


def _slab_rows(n):
    return -(-n // 1024) * 8


_PACK_ROWS = 2304


_LOSS_ROW = 2192


def _pack(d, loss):
    parts = []
    for n in _SMALL:
        flat = _flat_form(n, d[n]).reshape(-1).astype(F32)
        rows = _slab_rows(flat.shape[0])
        parts.append(jnp.pad(flat, (0, rows * 128 - flat.shape[0])).reshape(rows, 128))
    assert sum(p.shape[0] for p in parts) == _LOSS_ROW
    parts.append(jnp.pad(loss.reshape(1, 1), ((0, _PACK_ROWS - _LOSS_ROW - 1), (0, 127))))
    return jnp.concatenate(parts, axis=0)


def _unpack(packed, like):
    out, off = {}, 0
    for n in _SMALL:
        size = math.prod(like[n].shape)
        rows = _slab_rows(size)
        out[n] = _own_form(n, packed[off:off + rows].reshape(-1)[:size], like[n].shape)
        off += rows
    return out


def _adamw_small(g, w, m, v):
    c1 = 1.0 - ADAM_B1 ** ADAM_STEP
    c2 = 1.0 - ADAM_B2 ** ADAM_STEP
    k = len(_SMALL)

    def body(*refs):
        ins, outs = refs[:4 * k], refs[4 * k:]
        for j in range(k):
            gv, wv, mv, vv = (ins[q * k + j][...] for q in range(4))
            nm = ADAM_B1 * mv + (1.0 - ADAM_B1) * gv
            nv = ADAM_B2 * vv + (1.0 - ADAM_B2) * (gv * gv)
            outs[j][...] = -ADAM_LR * ((nm / c1) / (jnp.sqrt(nv / c2) + ADAM_EPS) + ADAM_WD * wv)
            outs[k + j][...] = nm
            outs[2 * k + j][...] = nv

    args = [_flat_form(n, d[n]).reshape(_SMALL_2D[n]) for d in (g, w, m, v) for n in _SMALL]
    shapes = [jax.ShapeDtypeStruct(_SMALL_2D[n], F32) for _ in range(3) for n in _SMALL]
    outs = pl.pallas_call(body, name="adamw_small", out_shape=shapes, compiler_params=_cp())(*args)
    res = []
    for q in range(3):
        res.append({n: _own_form(n, outs[q * k + j], w[n].shape) for j, n in enumerate(_SMALL)})
    return res


def kernel(x, positions, norm_w, w_in, q_norm_w, k_norm_w, sinks, a_re, a_im, log_step, b_re, b_im, c_re, c_im, d_skip, w_glu, b_glu, attn_out_norm_w, ssm_out_norm_w, w_out, loss_target, m_norm_w, m_w_in, m_q_norm_w, m_k_norm_w, m_sinks, m_a_re, m_a_im, m_log_step, m_b_re, m_b_im, m_c_re, m_c_im, m_d_skip, m_w_glu, m_b_glu, m_attn_out_norm_w, m_ssm_out_norm_w, m_w_out, v_norm_w, v_w_in, v_q_norm_w, v_k_norm_w, v_sinks, v_a_re, v_a_im, v_log_step, v_b_re, v_b_im, v_c_re, v_c_im, v_d_skip, v_w_glu, v_b_glu, v_attn_out_norm_w, v_ssm_out_norm_w, v_w_out):
    w = dict(norm_w=norm_w, w_in=w_in, q_norm_w=q_norm_w, k_norm_w=k_norm_w, sinks=sinks, a_re=a_re, a_im=a_im,
             log_step=log_step, b_re=b_re, b_im=b_im, c_re=c_re, c_im=c_im, d_skip=d_skip, w_glu=w_glu, b_glu=b_glu,
             attn_out_norm_w=attn_out_norm_w, ssm_out_norm_w=ssm_out_norm_w, w_out=w_out)
    m = dict(norm_w=m_norm_w, w_in=m_w_in, q_norm_w=m_q_norm_w, k_norm_w=m_k_norm_w, sinks=m_sinks, a_re=m_a_re,
             a_im=m_a_im, log_step=m_log_step, b_re=m_b_re, b_im=m_b_im, c_re=m_c_re, c_im=m_c_im, d_skip=m_d_skip,
             w_glu=m_w_glu, b_glu=m_b_glu, attn_out_norm_w=m_attn_out_norm_w, ssm_out_norm_w=m_ssm_out_norm_w,
             w_out=m_w_out)
    v = dict(norm_w=v_norm_w, w_in=v_w_in, q_norm_w=v_q_norm_w, k_norm_w=v_k_norm_w, sinks=v_sinks, a_re=v_a_re,
             a_im=v_a_im, log_step=v_log_step, b_re=v_b_re, b_im=v_b_im, c_re=v_c_re, c_im=v_c_im, d_skip=v_d_skip,
             w_glu=v_w_glu, b_glu=v_b_glu, attn_out_norm_w=v_attn_out_norm_w, ssm_out_norm_w=v_ssm_out_norm_w,
             w_out=v_w_out)
    core = lax.axis_index("c").astype(jnp.int32).reshape(1)
    chip = (2 * lax.axis_index("x") + lax.axis_index("y")).astype(jnp.int32).reshape(1)

    loss, grad_x, grads, parts = _step(x[0], positions[0], loss_target[0], w, core, chip)
    delta, new_m, new_v = {}, {}, {}
    for n in ("w_glu", "w_out"):
        grads[n], delta[n], new_m[n], new_v[n] = _adamw_reduced(*parts[n], chip, w[n], m[n], v[n], f"adamw_{n}")
    g_t, d_t, m_t, v_t = _adamw_reduced(*parts["w_in"], chip, w["w_in"].T, m["w_in"].T, v["w_in"].T, "adamw_w_in")
    grads["w_in"], delta["w_in"], new_m["w_in"], new_v["w_in"] = g_t.T, d_t.T, m_t.T, v_t.T
    d_s, m_s, v_s = _adamw_small(grads, w, m, v)
    delta.update(d_s)
    new_m.update(m_s)
    new_v.update(v_s)

    return (loss, grad_x[None], *[grads[n] for n in _WEIGHTS], *[delta[n] for n in _WEIGHTS],
            *[new_m[n] for n in _WEIGHTS], *[new_v[n] for n in _WEIGHTS])
```

```python
import math

import jax
import jax.numpy as jnp
from jax import lax
from jax.experimental import pallas as pl
from jax.experimental.pallas import tpu as pltpu

F32 = jnp.float32
BF16 = jnp.bfloat16

D_MODEL = 2048
ATTN_W = 1024
KV_W = 256
SSM_W = 1024
HEAD_DIM = 64
N_HEADS = 16
N_KV = 4
IN_W = 4608
BLOCK = 128
ROPE_THETA = 10000.0
NORM_EPS = 1e-6
SSM_G = 64
SSM_P = 64
SSM_H = 16
CHUNK = 16
CW = CHUNK * SSM_H
N_DEV = 8

ADAM_LR = 0.001
ADAM_B1 = 0.9
ADAM_B2 = 0.999
ADAM_EPS = 1e-08
ADAM_WD = 0.01
ADAM_STEP = 10

VMEM_LIMIT = 56 * 1024 * 1024
MESH = pl.DeviceIdType.MESH


def _cp(sem=None):
    if sem is None:
        return pltpu.CompilerParams(vmem_limit_bytes=VMEM_LIMIT)
    return pltpu.CompilerParams(vmem_limit_bytes=VMEM_LIMIT, dimension_semantics=sem)


def _sigmoid(x):
    return 0.5 * jnp.tanh(0.5 * x) + 0.5


def _silu(x):
    return x * _sigmoid(x)


def _dsilu(x):
    s = _sigmoid(x)
    return s * (1.0 + x * (1.0 - s))


_GELU_C = math.sqrt(2.0 / math.pi)


def _gelu(y):
    t = jnp.tanh(_GELU_C * (y + 0.044715 * y * y * y))
    return 0.5 * y * (1.0 + t)


def _dgelu(y):
    t = jnp.tanh(_GELU_C * (y + 0.044715 * y * y * y))
    return 0.5 * (1.0 + t) + 0.5 * y * (1.0 - t * t) * _GELU_C * (1.0 + 3.0 * 0.044715 * y * y)


def _tile(n, want):
    if n <= want:
        return n
    for t in range(want - want % 16, 0, -16):
        if n % t == 0:
            return t
    raise ValueError((n, want))


def _mm(a, b, mode, out_dtype, name, tm=512, tn=1024, add=None, ride=None, panel=None):
    if mode == "nn":
        (M, K), (K2, N) = a.shape, b.shape
    elif mode == "nt":
        (M, K), (N, K2) = a.shape, b.shape
    else:
        (K, M), (K2, N) = a.shape, b.shape
    assert K == K2
    tm, tn = _tile(M, tm), _tile(N, tn)
    p0 = 0
    if panel is not None:
        assert mode != "nt" and add is None
        p0, N = panel, tn
    dn = {"nn": _NN, "nt": _NT, "tn": _TN}[mode]

    def body(a_ref, b_ref, *rest):
        o_ref = rest[-1]
        acc = lax.dot_general(a_ref[...].astype(BF16), b_ref[...].astype(BF16), dn, preferred_element_type=F32)
        if add is not None:
            acc = acc + rest[0][...]
        o_ref[...] = acc.astype(o_ref.dtype)

    a_spec = pl.BlockSpec((K, tm), lambda j, i: (0, i)) if mode == "tn" else pl.BlockSpec((tm, K), lambda j, i: (i, 0))
    b_spec = (pl.BlockSpec((tn, K), lambda j, i: (j, 0)) if mode == "nt"
              else pl.BlockSpec((K, tn), lambda j, i: (0, j + p0)))
    o_spec = pl.BlockSpec((tm, tn), lambda j, i: (i, j))
    extra = () if add is None else (add,)
    if ride is not None:
        (out,), landed = _call(body, name, (N // tn, M // tm), [a_spec, b_spec] + [o_spec] * len(extra), [o_spec],
                               [jax.ShapeDtypeStruct((M, N), out_dtype)], (a, b, *extra), ride=ride)
        return out, landed
    return pl.pallas_call(
        body,
        name=name,
        grid=(N // tn, M // tm),
        in_specs=[a_spec, b_spec] + [o_spec] * len(extra),
        out_specs=o_spec,
        out_shape=jax.ShapeDtypeStruct((M, N), out_dtype),
        compiler_params=_cp(("parallel", "parallel")),
    )(a, b, *extra)


_CHIP_ORDER = (0, 2, 1, 3)


def _rms_inproj_gather(x, norm_w, wt_shard, chip):
    L = x.shape[0]
    tm = _tile(L, 1024)
    ni = L // tm
    r = IN_W // N_DEV
    tn = 2 * r

    def body(chip_ref, x_ref, nw_ref, shard, proj_ref, hn_hbm, wt_hbm, hn_scr, w_scr, send_sems, recv_sems, loc_sems):
        jc, i = pl.program_id(0), pl.program_id(1)
        xx, yy, c = lax.axis_index("x"), lax.axis_index("y"), lax.axis_index("c")
        me, sibling = (xx, yy, c), (xx, yy, 1 - c)
        chips = [(1 - xx, yy), (xx, 1 - yy), (1 - xx, 1 - yy)]

        def slot(dev):
            return wt_hbm.at[4 * dev[0] + 2 * dev[1] + dev[2]]

        def copy(q, block, to, src=None):
            return pltpu.make_async_remote_copy(
                src_ref=slot(block) if src is None else src, dst_ref=slot(block),
                send_sem=send_sems.at[q], recv_sem=recv_sems.at[q], device_id=to, device_id_type=MESH)

        def rows_of(buf, core):
            return w_scr.at[buf, pl.ds(pl.multiple_of(core * r, 16), r)]

        mine = pltpu.make_async_copy(shard, slot(me), loc_sems.at[0])
        sends = [copy(0, me, sibling, src=shard)] + [copy(1 + j, me, (*ch, c), src=shard) for j, ch in enumerate(chips[:2])]
        relay_block = (xx + (1 - c) * (1 - 2 * xx), yy + c * (1 - 2 * yy), c)
        relay = copy(3, relay_block, (xx + c * (1 - 2 * xx), yy + (1 - c) * (1 - 2 * yy), c))
        first = jnp.logical_and(jc == 0, i == 0)

        @pl.when(first)
        def _():
            mine.start()
            for cp in sends:
                cp.start()
            own = pltpu.make_async_copy(shard, rows_of(0, c), loc_sems.at[1])
            own.start()
            copy(0, sibling, me).wait_recv()
            sib = pltpu.make_async_copy(slot(sibling), rows_of(0, 1 - c), loc_sems.at[2])
            sib.start()
            own.wait()
            sib.wait()

        def to_vmem(j, ch):
            pltpu.make_async_copy(slot((*ch, c)), rows_of((1 + j) % 2, c), loc_sems.at[1 + j]).start()

        @pl.when(jnp.logical_and(jc == 1, i == 0))
        def _():
            for j in range(2):
                copy(1 + j, (*chips[j], c), me).wait_recv()
                copy(4 + j, (*chips[j], c), sibling).start()
            relay.start()
            to_vmem(0, chips[0])

        @pl.when(jnp.logical_and(jc == 1, i == ni // 2))
        def _():
            to_vmem(1, chips[1])

        @pl.when(jnp.logical_and(jc == 2, i == ni // 2))
        def _():
            copy(3, (*chips[2], c), me).wait_recv()
            copy(6, (*chips[2], c), sibling).start()
            to_vmem(2, chips[2])

        for j, ch in enumerate(chips):
            @pl.when(jnp.logical_and(jc == 1 + j, i == 0))
            def _(j=j, ch=ch):
                buf = (1 + j) % 2
                copy(4 + j, (*ch, 1 - c), me).wait_recv()
                passed = pltpu.make_async_copy(slot((*ch, 1 - c)), rows_of(buf, 1 - c), loc_sems.at[4 + j])
                passed.start()
                pltpu.make_async_copy(slot((*ch, c)), rows_of(buf, c), loc_sems.at[1 + j]).wait()
                passed.wait()

        rows = pl.ds(pl.multiple_of(i * tm, tm), tm)

        @pl.when(jc == 0)
        def _():
            xv = x_ref[...]
            rstd = lax.rsqrt(jnp.mean(xv * xv, axis=-1, keepdims=True) + NORM_EPS)
            hn_scr[rows, :] = (xv * rstd * nw_ref[...]).astype(BF16)

        keep_hn = pltpu.make_async_copy(hn_scr, hn_hbm, loc_sems.at[7])

        @pl.when(jnp.logical_and(jc == 1, i == 0))
        def _():
            keep_hn.start()

        for buf in range(2):
            @pl.when(jc % 2 == buf)
            def _(buf=buf):
                proj_ref[...] = lax.dot_general(hn_scr[rows, :], w_scr[buf], _NT, preferred_element_type=F32)

        @pl.when(jnp.logical_and(jc == 3, i == ni - 1))
        def _():
            for cp in sends + [relay]:
                cp.wait_send()
            for j, ch in enumerate(chips):
                copy(4 + j, (*ch, c), sibling).wait_send()
            mine.wait()
            keep_hn.wait()

    def tile_of(jc, chip_ref):
        mask = jnp.where(jc == 1, _CHIP_ORDER[1], jnp.where(jc == 2, _CHIP_ORDER[2], jnp.where(jc == 3, _CHIP_ORDER[3], 0)))
        return jnp.bitwise_xor(chip_ref[0], mask)

    held = lambda jc, i: jnp.where(jc == 0, i, ni - 1)
    return pl.pallas_call(
        body,
        name="rms_inproj_gather",
        grid_spec=pltpu.PrefetchScalarGridSpec(
            num_scalar_prefetch=1,
            grid=(4, ni),
            in_specs=[pl.BlockSpec((tm, D_MODEL), lambda jc, i, ch: (held(jc, i), 0)),
                      pl.BlockSpec((1, D_MODEL), lambda jc, i, ch: (0, 0)), _ANY],
            out_specs=[pl.BlockSpec((tm, tn), lambda jc, i, ch: (i, tile_of(jc, ch))), _ANY, _ANY],
            scratch_shapes=[pltpu.VMEM((L, D_MODEL), BF16), pltpu.VMEM((2, tn, D_MODEL), BF16),
                            pltpu.SemaphoreType.DMA((7,)), pltpu.SemaphoreType.DMA((7,)), pltpu.SemaphoreType.DMA((8,))],
        ),
        out_shape=[jax.ShapeDtypeStruct((L, IN_W), F32), jax.ShapeDtypeStruct((L, D_MODEL), BF16),
                   jax.ShapeDtypeStruct((N_DEV, r, D_MODEL), BF16)],
        compiler_params=_cp(("arbitrary", "arbitrary")),
    )(chip, x, norm_w.reshape(1, D_MODEL), wt_shard)


def _seg_sum(v):
    a = lax.broadcasted_iota(jnp.int32, (128, 128), 0) // HEAD_DIM
    b = lax.broadcasted_iota(jnp.int32, (128, 128), 1) // HEAD_DIM
    ones = jnp.where(a == b, 1.0, 0.0).astype(BF16)
    hi = v.astype(BF16)
    lo = (v - hi.astype(F32)).astype(BF16)
    return jnp.dot(hi, ones, preferred_element_type=F32) + jnp.dot(lo, ones, preferred_element_type=F32)


def _rot_half(t):
    lane = lax.broadcasted_iota(jnp.int32, t.shape, 1)
    return jnp.where(lane % HEAD_DIM < HEAD_DIM // 2, pltpu.roll(t, 128 - HEAD_DIM // 2, 1),
                     pltpu.roll(t, HEAD_DIM // 2, 1))


def _norm_rope(raw, w, cos, sin):
    r = lax.rsqrt(_seg_sum(raw * raw) * (1.0 / HEAD_DIM) + NORM_EPS)
    tn = raw * r * w
    return r, tn * cos + _rot_half(tn) * sin


def _norm_rope_bwd(d_rot, raw, w, cos, sin):
    r = lax.rsqrt(_seg_sum(raw * raw) * (1.0 / HEAD_DIM) + NORM_EPS)
    d_tn = d_rot * cos + _rot_half(d_rot * sin)
    xh = raw * r
    gw = d_tn * w
    d_raw = r * (gw - xh * (_seg_sum(gw * xh) * (1.0 / HEAD_DIM)))
    return d_raw, d_tn * xh


def _band_mask2(has_prev):
    qi = lax.broadcasted_iota(jnp.int32, (2 * BLOCK, 2 * BLOCK), 0) % BLOCK + BLOCK
    kj = lax.broadcasted_iota(jnp.int32, (2 * BLOCK, 2 * BLOCK), 1)
    rel = qi - kj
    return (rel >= 0) & (rel < BLOCK) & ((kj >= BLOCK) | has_prev)


def _half_tiles(pair):
    lo = lax.broadcasted_iota(jnp.int32, pair.shape, 1) < HEAD_DIM
    sw = pltpu.roll(pair, HEAD_DIM, 1)
    z = jnp.zeros_like(pair)
    return (jnp.where(lo, pair, z).astype(BF16), jnp.where(lo, z, sw).astype(BF16),
            jnp.where(lo, sw, z).astype(BF16), jnp.where(lo, z, pair).astype(BF16))


def _two_rows(top, bottom):
    row = lax.broadcasted_iota(jnp.int32, (2 * BLOCK, 1), 0)
    return jnp.where(row < BLOCK, top, bottom)


def _lane_col(mat, h):
    lane = lax.broadcasted_iota(jnp.int32, mat.shape, 1)
    return jnp.sum(jnp.where(lane == h, mat, 0.0), axis=1, keepdims=True)


_SCALE = 1.0 / math.sqrt(HEAD_DIM)
_NT = (((1,), (1,)), ((), ()))
_NN = (((1,), (0,)), ((), ()))
_TN = (((0,), (0,)), ((), ()))


def _qk_prep(proj, tab, qw, kw):
    L = proj.shape[0]
    tm = _tile(L, 512)

    def body(q_ref, k_ref, t_ref, qw_ref, kw_ref, qo_ref, ko_ref):
        cos, sin = t_ref[:, :128], t_ref[:, 128:]
        for c in range(ATTN_W // 128):
            _, qr = _norm_rope(q_ref[:, c * 128:(c + 1) * 128], qw_ref[...], cos, sin)
            qo_ref[:, c * 128:(c + 1) * 128] = (qr * _SCALE).astype(BF16)
        for c in range(KV_W // 128):
            _, kr = _norm_rope(k_ref[:, c * 128:(c + 1) * 128], kw_ref[...], cos, sin)
            ko_ref[:, c * 128:(c + 1) * 128] = kr.astype(BF16)

    row = pl.BlockSpec((1, 128), lambda i: (0, 0))
    return pl.pallas_call(
        body,
        name="qk_prep",
        grid=(L // tm,),
        in_specs=[pl.BlockSpec((tm, ATTN_W), lambda i: (i, 0)), pl.BlockSpec((tm, KV_W), lambda i: (i, 4)),
                  pl.BlockSpec((tm, 256), lambda i: (i, 0)), row, row],
        out_specs=[pl.BlockSpec((tm, ATTN_W), lambda i: (i, 0)), pl.BlockSpec((tm, KV_W), lambda i: (i, 0))],
        out_shape=[jax.ShapeDtypeStruct((L, ATTN_W), BF16), jax.ShapeDtypeStruct((L, KV_W), BF16)],
        compiler_params=_cp(("parallel",)),
    )(proj, proj, tab, jnp.tile(qw, 2).reshape(1, 128), jnp.tile(kw, 2).reshape(1, 128))


def _group_tiles(g, kt, vt):
    a, b = divmod(g, 2)
    return kt[a][2 * b], kt[a][2 * b + 1], vt[a][2 * b], vt[a][2 * b + 1]


def _attn_fwd(q, k, proj, sinks, ride):
    L = proj.shape[0]
    nb = L // BLOCK

    def body(q_ref, kc_ref, kp_ref, vc_ref, vp_ref, z0_ref, z1_ref, sink_ref, og_ref, o_ref, lse_ref):
        i = pl.program_id(0)
        mask = _band_mask2(i > 0)
        z = jnp.concatenate([z0_ref[...], z1_ref[...]], axis=1)
        lane = lax.broadcasted_iota(jnp.int32, (BLOCK, 128), 1)
        kt = [_half_tiles(jnp.concatenate([kp_ref[:, a * 128:(a + 1) * 128], kc_ref[:, a * 128:(a + 1) * 128]],
                                          axis=0).astype(F32)) for a in range(2)]
        vt = [_half_tiles(jnp.concatenate([vp_ref[:, a * 128:(a + 1) * 128], vc_ref[:, a * 128:(a + 1) * 128]],
                                          axis=0)) for a in range(2)]
        lse_mat = jnp.zeros((BLOCK, 128), F32)
        outs = []
        for g in range(N_KV):
            k_lo, k_hi, v_lo, v_hi = _group_tiles(g, kt, vt)
            q2 = jnp.concatenate([q_ref[:, 2 * g * 128:(2 * g + 1) * 128],
                                  q_ref[:, (2 * g + 1) * 128:(2 * g + 2) * 128]], axis=0)
            acc = jnp.zeros((2 * BLOCK, 128), F32)
            for half, (kh, vh) in enumerate(((k_lo, v_lo), (k_hi, v_hi))):
                h_top, h_bot = 4 * g + half, 4 * g + 2 + half
                s = jnp.where(mask, lax.dot_general(q2, kh, _NT, preferred_element_type=F32), -1e30)
                sink = _two_rows(sink_ref[h_top], sink_ref[h_bot])
                m = jnp.maximum(jnp.max(s, axis=-1, keepdims=True), sink)
                e = jnp.exp(s - m)
                den = jnp.sum(e, axis=-1, keepdims=True) + jnp.exp(sink - m)
                p = e * (1.0 / den)
                acc = acc + jnp.dot(p.astype(BF16), vh, preferred_element_type=F32)
                lse = m + jnp.log(den)
                lse_mat = jnp.where(lane == h_top, lse[:BLOCK], lse_mat)
                lse_mat = jnp.where(lane == h_bot, lse[BLOCK:], lse_mat)
            outs += [acc[:BLOCK], acc[BLOCK:]]
        o = jnp.concatenate(outs, axis=1)
        o_ref[...] = o.astype(BF16)
        og_ref[...] = (o * _silu(z)).astype(BF16)
        lse_ref[...] = lse_mat

    prev = lambda i: jnp.maximum(i - 1, 0)
    return _call(
        body, "attn_fwd", (nb,),
        [pl.BlockSpec((BLOCK, ATTN_W), lambda i: (i, 0)),
         pl.BlockSpec((BLOCK, KV_W), lambda i: (i, 0)),
         pl.BlockSpec((BLOCK, KV_W), lambda i: (prev(i), 0)),
         pl.BlockSpec((BLOCK, KV_W), lambda i: (i, 5)),
         pl.BlockSpec((BLOCK, KV_W), lambda i: (prev(i), 5)),
         pl.BlockSpec((BLOCK, 512), lambda i: (i, 3)),
         pl.BlockSpec((BLOCK, 512), lambda i: (i, 4)),
         pl.BlockSpec(memory_space=pltpu.SMEM)],
        [pl.BlockSpec((BLOCK, ATTN_W), lambda i: (i, 0)),
         pl.BlockSpec((BLOCK, ATTN_W), lambda i: (i, 0)),
         pl.BlockSpec((BLOCK, 128), lambda i: (i, 0))],
        [jax.ShapeDtypeStruct((L, ATTN_W), BF16), jax.ShapeDtypeStruct((L, ATTN_W), BF16),
         jax.ShapeDtypeStruct((L, 128), F32)],
        (q, k, k, proj, proj, proj, proj, sinks), ride=ride)


def _attn_bwd(q, k, proj, sinks, d_o, o, lse, ride):
    L = proj.shape[0]
    nb = L // BLOCK

    def body(q_ref, kc_ref, kp_ref, vc_ref, vp_ref, do_ref, o_ref, lse_ref, sink_ref,
             dq_ref, dk_ref, dv_ref, gs_ref, ck_scr, cv_scr):
        i = pl.program_id(0)

        @pl.when(i == 0)
        def _():
            gs_ref[...] = jnp.zeros_like(gs_ref)
            ck_scr[...] = jnp.zeros_like(ck_scr)
            cv_scr[...] = jnp.zeros_like(cv_scr)

        @pl.when(i == nb)
        def _():
            dk_ref[...] = ck_scr[...]
            dv_ref[...] = cv_scr[...]

        @pl.when(i < nb)
        def _():
            mask = _band_mask2(i > 0)
            lane = lax.broadcasted_iota(jnp.int32, (1, 128), 1)
            lo = lax.broadcasted_iota(jnp.int32, (2 * BLOCK, 128), 1) < HEAD_DIM
            lse_c = lse_ref[...]
            kt = [_half_tiles(jnp.concatenate([kp_ref[:, a * 128:(a + 1) * 128], kc_ref[:, a * 128:(a + 1) * 128]],
                                              axis=0).astype(F32)) for a in range(2)]
            vt = [_half_tiles(jnp.concatenate([vp_ref[:, a * 128:(a + 1) * 128], vc_ref[:, a * 128:(a + 1) * 128]],
                                              axis=0)) for a in range(2)]
            gs = jnp.zeros((1, 128), F32)
            dq_parts = []
            dk_acc = [jnp.zeros((2 * BLOCK, 128), F32) for _ in range(2)]
            dv_acc = [jnp.zeros((2 * BLOCK, 128), F32) for _ in range(2)]
            for g in range(N_KV):
                a, b = divmod(g, 2)
                k_lo, k_hi, v_lo, v_hi = _group_tiles(g, kt, vt)
                t0, t1 = slice(2 * g * 128, (2 * g + 1) * 128), slice((2 * g + 1) * 128, (2 * g + 2) * 128)
                q2 = jnp.concatenate([q_ref[:, t0], q_ref[:, t1]], axis=0)
                do2 = jnp.concatenate([do_ref[:, t0], do_ref[:, t1]], axis=0)
                prod = do2.astype(F32) * jnp.concatenate([o_ref[:, t0], o_ref[:, t1]], axis=0).astype(F32)
                do2_b = do2.astype(BF16)
                dq2 = jnp.zeros((2 * BLOCK, 128), F32)
                dk_h, dv_h = [], []
                for half, (kh, vh) in enumerate(((k_lo, v_lo), (k_hi, v_hi))):
                    h_top, h_bot = 4 * g + half, 4 * g + 2 + half
                    lse = jnp.concatenate([_lane_col(lse_c, h_top), _lane_col(lse_c, h_bot)], axis=0)
                    sink = _two_rows(sink_ref[h_top], sink_ref[h_bot])
                    delta = jnp.sum(jnp.where(lo == (half == 0), prod, 0.0), axis=1, keepdims=True)
                    s = jnp.where(mask, lax.dot_general(q2, kh, _NT, preferred_element_type=F32), -1e30)
                    p = jnp.exp(s - lse)
                    dp = lax.dot_general(do2_b, vh, _NT, preferred_element_type=F32)
                    ds_b = (p * (dp - delta)).astype(BF16)
                    p_b = p.astype(BF16)
                    dq2 = dq2 + jnp.dot(ds_b, kh, preferred_element_type=F32)
                    dk_h.append(lax.dot_general(ds_b, q2, _TN, preferred_element_type=F32))
                    dv_h.append(lax.dot_general(p_b, do2_b, _TN, preferred_element_type=F32))
                    gsink = -jnp.exp(sink - lse) * delta
                    row = lax.broadcasted_iota(jnp.int32, (2 * BLOCK, 1), 0)
                    gs = gs + jnp.where(lane == h_top, jnp.sum(jnp.where(row < BLOCK, gsink, 0.0)), 0.0)
                    gs = gs + jnp.where(lane == h_bot, jnp.sum(jnp.where(row >= BLOCK, gsink, 0.0)), 0.0)
                dq_parts += [dq2[:BLOCK], dq2[BLOCK:]]
                for acc, parts in ((dk_acc, dk_h), (dv_acc, dv_h)):
                    t = jnp.where(lo, parts[0], parts[1])
                    t = t + pltpu.roll(t, HEAD_DIM, 1)
                    acc[a] = acc[a] + jnp.where(lo == (b == 0), t, 0.0)
            dq_ref[...] = jnp.concatenate(dq_parts, axis=1)
            dk_full = jnp.concatenate(dk_acc, axis=1)
            dv_full = jnp.concatenate(dv_acc, axis=1)
            dk_ref[...] = ck_scr[...] + dk_full[:BLOCK]
            dv_ref[...] = cv_scr[...] + dv_full[:BLOCK]
            ck_scr[...] = dk_full[BLOCK:]
            cv_scr[...] = dv_full[BLOCK:]
            gs_ref[...] += gs

    cur = lambda i: jnp.minimum(i, nb - 1)
    prev = lambda i: jnp.maximum(jnp.minimum(i, nb - 1) - 1, 0)
    done = lambda i: jnp.maximum(i - 1, 0)
    bs = pl.BlockSpec
    return _call(
        body, "attn_bwd", (nb + 1,),
        [bs((BLOCK, ATTN_W), lambda i: (cur(i), 0)),
         bs((BLOCK, KV_W), lambda i: (cur(i), 0)), bs((BLOCK, KV_W), lambda i: (prev(i), 0)),
         bs((BLOCK, KV_W), lambda i: (cur(i), 5)), bs((BLOCK, KV_W), lambda i: (prev(i), 5)),
         bs((BLOCK, ATTN_W), lambda i: (cur(i), 0)), bs((BLOCK, ATTN_W), lambda i: (cur(i), 0)),
         bs((BLOCK, 128), lambda i: (cur(i), 0)), bs(memory_space=pltpu.SMEM)],
        [bs((BLOCK, ATTN_W), lambda i: (cur(i), 0)),
         bs((BLOCK, KV_W), lambda i: (done(i), 0)), bs((BLOCK, KV_W), lambda i: (done(i), 0)),
         bs((1, 128), lambda i: (0, 0))],
        [jax.ShapeDtypeStruct((L, ATTN_W), F32), jax.ShapeDtypeStruct((L, KV_W), F32),
         jax.ShapeDtypeStruct((L, KV_W), F32), jax.ShapeDtypeStruct((1, 128), F32)],
        (q, k, k, proj, proj, d_o, o, lse, sinks),
        [pltpu.VMEM((BLOCK, KV_W), F32), pltpu.VMEM((BLOCK, KV_W), F32)], ride)


def _qk_prep_bwd(proj, tab, qw, kw, d_q, d_k, d_v, d_za, d_u, d_zs):
    L = proj.shape[0]
    tm = _tile(L, 512)
    z0 = ATTN_W + 2 * KV_W

    def body(q_ref, k_ref, t_ref, qw_ref, kw_ref, dq_ref, dk_ref, dv_ref, dza_ref, du_ref, dzs_ref,
             out_ref, gq_ref, gk_ref):
        i = pl.program_id(0)

        @pl.when(i == 0)
        def _():
            gq_ref[...] = jnp.zeros_like(gq_ref)
            gk_ref[...] = jnp.zeros_like(gk_ref)

        cos, sin = t_ref[:, :128], t_ref[:, 128:]
        gq = jnp.zeros((1, 128), F32)
        gk = jnp.zeros((1, 128), F32)
        for c in range(ATTN_W // 128):
            cs = slice(c * 128, (c + 1) * 128)
            d_raw, gw = _norm_rope_bwd(dq_ref[:, cs] * _SCALE, q_ref[:, cs], qw_ref[...], cos, sin)
            out_ref[:, cs] = d_raw.astype(BF16)
            gq = gq + jnp.sum(gw, axis=0, keepdims=True)
        for c in range(KV_W // 128):
            cs = slice(c * 128, (c + 1) * 128)
            d_raw, gw = _norm_rope_bwd(dk_ref[:, cs], k_ref[:, cs], kw_ref[...], cos, sin)
            out_ref[:, ATTN_W + c * 128:ATTN_W + (c + 1) * 128] = d_raw.astype(BF16)
            gk = gk + jnp.sum(gw, axis=0, keepdims=True)
        out_ref[:, ATTN_W + KV_W:z0] = dv_ref[...].astype(BF16)
        out_ref[:, z0:z0 + ATTN_W] = dza_ref[...]
        out_ref[:, z0 + ATTN_W:z0 + ATTN_W + SSM_W] = du_ref[...].astype(BF16)
        out_ref[:, z0 + ATTN_W + SSM_W:] = dzs_ref[...]
        gq_ref[...] += gq
        gk_ref[...] += gk

    row = pl.BlockSpec((1, 128), lambda i: (0, 0))
    blk = lambda w, c: pl.BlockSpec((tm, w), lambda i: (i, c))
    return pl.pallas_call(
        body,
        name="qk_prep_bwd",
        grid=(L // tm,),
        in_specs=[blk(ATTN_W, 0), blk(KV_W, 4), blk(256, 0), row, row, blk(ATTN_W, 0), blk(KV_W, 0), blk(KV_W, 0),
                  blk(ATTN_W, 0), blk(SSM_W, 0), blk(SSM_W, 0)],
        out_specs=[blk(IN_W, 0), row, row],
        out_shape=[jax.ShapeDtypeStruct((L, IN_W), BF16), jax.ShapeDtypeStruct((1, 128), F32),
                   jax.ShapeDtypeStruct((1, 128), F32)],
        compiler_params=_cp(("arbitrary",)),
    )(proj, proj, tab, jnp.tile(qw, 2).reshape(1, 128), jnp.tile(kw, 2).reshape(1, 128), d_q, d_k, d_v,
      d_za, d_u, d_zs)


def _cmul(a, b):
    return a[0] * b[0] - a[1] * b[1], a[0] * b[1] + a[1] * b[0]


def _cmul_conj(a, b):
    return a[0] * b[0] + a[1] * b[1], a[1] * b[0] - a[0] * b[1]


def _cadd(a, b):
    return a[0] + b[0], a[1] + b[1]


def _dot3(a, b, dn):
    ah, bh = a.astype(BF16), b.astype(BF16)
    al, bl = (a - ah.astype(F32)).astype(BF16), (b - bh.astype(F32)).astype(BF16)
    d = lambda u, v: lax.dot_general(u, v, dn, preferred_element_type=F32)
    return d(ah, bh) + d(ah, bl) + d(al, bh)


def _s5_discretise(a_re, a_im, ls, cosx, sinx, bt):
    delta = jnp.exp(ls)
    er = jnp.exp(a_re * delta)
    lb = (er * cosx, er * sinx)
    den = a_re * a_re + a_im * a_im
    coef = _cmul_conj((lb[0] - 1.0, lb[1]), (a_re, a_im))
    coef = (coef[0] / den, coef[1] / den)
    return delta, lb, coef, den, _cmul(coef, bt)


def _powers(lb):
    pw = [(jnp.ones_like(lb[0]), jnp.zeros_like(lb[0]))]
    for _ in range(CHUNK):
        pw.append(_cmul(pw[-1], lb))
    return pw


def _block_rows(a, pw, idx):
    blocks = [_cmul(a, pw[i]) for i in idx]
    return (jnp.concatenate([b[0] for b in blocks], axis=-2), jnp.concatenate([b[1] for b in blocks], axis=-2))


def _block_rows_bwd(g, a, pw, idx, g_pw):
    g_a = (jnp.zeros_like(a[0]), jnp.zeros_like(a[0]))
    for j, i in enumerate(idx):
        gj = (g[0][..., j * SSM_H:(j + 1) * SSM_H, :], g[1][..., j * SSM_H:(j + 1) * SSM_H, :])
        g_a = _cadd(g_a, _cmul_conj(gj, pw[i]))
        gp = _cmul_conj(gj, a)
        g_pw[i] = _cadd(g_pw[i], (jnp.sum(gp[0], axis=-2, keepdims=True), jnp.sum(gp[1], axis=-2, keepdims=True)))
    return g_a


_IDX_S = [CHUNK - 1 - s for s in range(CHUNK)]
_IDX_C = list(range(CHUNK + 1))


def _prep_args(p):
    row = lambda t: t.reshape(SSM_G, 1, SSM_P)
    xi = p["a_im"] * jnp.exp(p["log_step"])[:, None]
    return (row(p["a_re"]), row(p["a_im"]), row(jnp.broadcast_to(p["log_step"][:, None], (SSM_G, SSM_P))),
            row(jnp.cos(xi)), row(jnp.sin(xi)), p["b_re"].transpose(0, 2, 1), p["b_im"].transpose(0, 2, 1),
            p["c_re"], p["c_im"])


PREP_GROUPS = 8


def _prep_specs():
    r1 = pl.BlockSpec((PREP_GROUPS, 1, SSM_P), lambda g: (g, 0, 0))
    r16 = pl.BlockSpec((PREP_GROUPS, SSM_H, SSM_P), lambda g: (g, 0, 0))
    return [r1] * 5 + [r16] * 4, r1, r16


def _ssm_prep(p):
    def one_group(q, are, aim, ls, cosx, sinx, btr, bti, cre, cim, mt_ref, s_ref, o_ref, a_ref):
        _, lb, _, _, bb = _s5_discretise(are[q], aim[q], ls[q], cosx[q], sinx[q], (btr[q], bti[q]))
        pw = _powers(lb)
        c = (cre[q], cim[q])
        sc = _block_rows(bb, pw, _IDX_S)
        cl = _block_rows(c, pw, _IDX_C)
        ok = (cl[0][:CW], cl[1][:CW])
        ot = (cl[0][SSM_H:], cl[1][SSM_H:])
        s_ref[q] = jnp.concatenate([sc[0], sc[1]], axis=1).astype(BF16)
        o_ref[q] = jnp.concatenate([ot[0], -ot[1]], axis=1).astype(BF16)
        a_ref[q] = jnp.concatenate([pw[CHUNK][0], pw[CHUNK][1]], axis=1)
        kt = _dot3(jnp.concatenate([bb[0], -bb[1]], axis=1), jnp.concatenate([ok[0], ok[1]], axis=1), _NT)
        lane = lax.broadcasted_iota(jnp.int32, kt.shape, 1)
        for s in range(CHUNK):
            blk = kt if s == 0 else jnp.where(lane >= SSM_H * s, pltpu.roll(kt, SSM_H * s, 1), 0.0)
            mt_ref[q, s * SSM_H:(s + 1) * SSM_H, :] = blk.astype(BF16)

    def body(*refs):
        for q in range(PREP_GROUPS):
            one_group(q, *refs)

    in_specs, r1, _ = _prep_specs()
    g3 = lambda r, c: pl.BlockSpec((PREP_GROUPS, r, c), lambda g: (g, 0, 0))
    return pl.pallas_call(
        body,
        name="ssm_prep",
        grid=(SSM_G // PREP_GROUPS,),
        in_specs=in_specs,
        out_specs=[g3(CW, CW), g3(CW, 2 * SSM_P), g3(CW, 2 * SSM_P), g3(1, 2 * SSM_P)],
        out_shape=[jax.ShapeDtypeStruct((SSM_G, CW, CW), BF16), jax.ShapeDtypeStruct((SSM_G, CW, 2 * SSM_P), BF16),
                   jax.ShapeDtypeStruct((SSM_G, CW, 2 * SSM_P), BF16),
                   jax.ShapeDtypeStruct((SSM_G, 1, 2 * SSM_P), F32)],
        compiler_params=_cp(("parallel",)),
    )(*_prep_args(p))


def _ssm_prep_bwd(p, g_mt, g_scat, g_ocat, g_a16, ride):
    def body(are, aim, ls, cosx, sinx, btr, bti, cre, cim, gmt_ref, gs_ref, go_ref, ga_ref,
             g_are, g_aim, g_ls, g_btr, g_bti, g_cre, g_cim, ga1_scr, gb1_scr):
        lam = (are[...], aim[...])
        bt = (btr[...], bti[...])
        delta, lb, coef, den, bb = _s5_discretise(lam[0], lam[1], ls[...], cosx[...], sinx[...], bt)
        pw = _powers(lb)
        c = (cre[...], cim[...])
        ok = _block_rows(c, pw, _IDX_C[:CHUNK])
        g_pw =[(jnp.zeros_like(lb[0]), jnp.zeros_like(lb[0])) for _ in range(CHUNK + 1)]
        lane = lax.broadcasted_iota(jnp.int32, (SSM_H, CW), 1)
        for q in range(PREP_GROUPS):
            g_kt = gmt_ref[q, :SSM_H, :]
            for s in range(1, CHUNK):
                blk = gmt_ref[q, s * SSM_H:(s + 1) * SSM_H, :]
                g_kt = g_kt + jnp.where(lane < CW - SSM_H * s, pltpu.roll(blk, CW - SSM_H * s, 1), 0.0)
            a1 = jnp.concatenate([bb[0][q], -bb[1][q]], axis=1)
            b1 = jnp.concatenate([ok[0][q], ok[1][q]], axis=1)
            ga1_scr[q] = _dot3(g_kt, b1, _NN)
            gb1_scr[q] = _dot3(g_kt, a1, _TN)
        g_a1, g_b1 = ga1_scr[...], gb1_scr[...]
        g_bb = (g_a1[..., :SSM_P], -g_a1[..., SSM_P:])
        gs = gs_ref[...]
        g_bb = _cadd(g_bb, _block_rows_bwd((gs[..., :SSM_P], gs[..., SSM_P:]), bb, pw, _IDX_S, g_pw))
        go = go_ref[...]
        pad = jnp.zeros_like(go[..., :SSM_H, :SSM_P])
        g_cl = (jnp.concatenate([g_b1[..., :SSM_P], pad], axis=-2) + jnp.concatenate([pad, go[..., :SSM_P]], axis=-2),
                jnp.concatenate([g_b1[..., SSM_P:], pad], axis=-2) - jnp.concatenate([pad, go[..., SSM_P:]], axis=-2))
        g_c = _block_rows_bwd(g_cl, c, pw, _IDX_C, g_pw)
        ga = ga_ref[...]
        g_pw[CHUNK] = _cadd(g_pw[CHUNK], (ga[..., :SSM_P], ga[..., SSM_P:]))
        g_lb = (jnp.zeros_like(lb[0]), jnp.zeros_like(lb[0]))
        for l in range(CHUNK - 1, -1, -1):
            g_lb = _cadd(g_lb, _cmul_conj(g_pw[l + 1], pw[l]))
            g_pw[l] = _cadd(g_pw[l], _cmul_conj(g_pw[l + 1], lb))
        g_bt = _cmul_conj(g_bb, coef)
        gc = _cmul_conj(g_bb, bt)
        g_coef = (jnp.sum(gc[0], axis=-2, keepdims=True), jnp.sum(gc[1], axis=-2, keepdims=True))
        lam_den = (lam[0] / den, lam[1] / den)
        g_lb = _cadd(g_lb, _cmul(g_coef, lam_den))
        t = _cmul(_cmul_conj(g_coef, coef), lam_den)
        g_x = _cmul_conj(g_lb, lb)
        g_are[...] = g_x[0] * delta - t[0]
        g_aim[...] = g_x[1] * delta - t[1]
        g_ls[...] = (g_x[0] * lam[0] + g_x[1] * lam[1]) * delta
        g_btr[...] = g_bt[0]
        g_bti[...] = g_bt[1]
        g_cre[...] = g_c[0]
        g_cim[...] = g_c[1]

    in_specs, r1, r16 = _prep_specs()
    g3 = lambda r, c: pl.BlockSpec((PREP_GROUPS, r, c), lambda g: (g, 0, 0))
    rows = jax.ShapeDtypeStruct((SSM_G, 1, SSM_P), F32)
    mats = jax.ShapeDtypeStruct((SSM_G, SSM_H, SSM_P), F32)
    (g_are, g_aim, g_ls, g_btr, g_bti, g_cre, g_cim), landed = _call(
        body, "ssm_prep_bwd", (SSM_G // PREP_GROUPS,),
        in_specs + [g3(CW, CW), g3(CW, 2 * SSM_P), g3(CW, 2 * SSM_P), g3(1, 2 * SSM_P)],
        [r1] * 3 + [r16] * 4, [rows] * 3 + [mats] * 4, (*_prep_args(p), g_mt, g_scat, g_ocat, g_a16),
        [pltpu.VMEM((PREP_GROUPS, SSM_H, 2 * SSM_P), F32), pltpu.VMEM((PREP_GROUPS, CW, 2 * SSM_P), F32)], ride)
    grads = dict(a_re=g_are.reshape(SSM_G, SSM_P), a_im=g_aim.reshape(SSM_G, SSM_P),
                 log_step=jnp.sum(g_ls.reshape(SSM_G, SSM_P), axis=1),
                 b_re=g_btr.transpose(0, 2, 1), b_im=g_bti.transpose(0, 2, 1), c_re=g_cre, c_im=g_cim)
    return grads, landed


def _cmul_const(xv, ar, ai):
    return xv * ar + pltpu.roll(xv, SSM_P, 1) * ai


def _chunk_scan(inc, a_row, reverse):
    n = inc.shape[0]
    lane = lax.broadcasted_iota(jnp.int32, (1, 2 * SSM_P), 1)
    row = lax.broadcasted_iota(jnp.int32, inc.shape, 0)
    sign = jnp.where(lane < SSM_P, -1.0, 1.0)
    ar = jnp.where(lane < SSM_P, a_row, pltpu.roll(a_row, SSM_P, 1))
    ai = jnp.where(lane < SSM_P, pltpu.roll(a_row, SSM_P, 1), a_row)
    if reverse:
        ai = -ai
    xv = inc
    s = 1
    while s < n:
        if reverse:
            sh = jnp.where(row < n - s, pltpu.roll(xv, n - s, 0), 0.0)
        else:
            sh = jnp.where(row >= s, pltpu.roll(xv, s, 0), 0.0)
        xv = xv + _cmul_const(sh, ar, ai * sign)
        ar, ai = ar * ar - ai * ai, 2.0 * ar * ai
        s *= 2
    return xv


def _shift_rows(xv, reverse):
    n = xv.shape[0]
    row = lax.broadcasted_iota(jnp.int32, xv.shape, 0)
    if reverse:
        return jnp.where(row < n - 1, pltpu.roll(xv, n - 1, 0), 0.0)
    return jnp.where(row >= 1, pltpu.roll(xv, 1, 0), 0.0)


GB = 128 // SSM_H
U_COL0 = (ATTN_W + 2 * KV_W + ATTN_W) // 128


HALF = CHUNK // 2


def _chunk_perm():
    r = jnp.arange(HALF * 128)
    t, g8, h = r // 128, (r % 128) // SSM_H, r % SSM_H
    return ((g8 * 128 + t * SSM_H + h)[:, None] == jnp.arange(GB * 128)[None, :]).astype(BF16)


def _load_perm(p_hbm, p_scr, sem):
    @pl.when(pl.program_id(0) == 0)
    def _():
        cp = pltpu.make_async_copy(p_hbm, p_scr, sem)
        cp.start()
        cp.wait()


def _rows_to_chunks(pieces, perm):
    halves = [jnp.dot(jnp.concatenate(pieces[k * HALF:(k + 1) * HALF], axis=1).astype(BF16), perm,
                      preferred_element_type=F32).astype(BF16) for k in range(2)]
    return [jnp.concatenate([hv[:, g * 128:(g + 1) * 128] for hv in halves], axis=1) for g in range(GB)]


def _chunks_to_rows(groups, perm, two_pass):
    pieces = []
    for k in range(2):
        v = jnp.concatenate([gv[:, k * 128:(k + 1) * 128] for gv in groups], axis=1)
        hi = v.astype(BF16)
        out = lax.dot_general(hi, perm, _NT, preferred_element_type=F32)
        if two_pass:
            lo = (v - hi.astype(F32)).astype(BF16)
            out = out + lax.dot_general(lo, perm, _NT, preferred_element_type=F32)
        pieces += [out[:, t * 128:(t + 1) * 128] for t in range(HALF)]
    return pieces


def _ssm_fwd(proj, perm, mt, scat, ocat, a16, d_skip, ride):
    L = proj.shape[0]
    nc = L // CHUNK

    def body(u_ref, p_hbm, mt_ref, s_ref, o_ref, a_ref, d_ref, y_ref, yg_ref, h_ref, p_scr, sem):
        _load_perm(p_hbm, p_scr, sem)
        perm = p_scr[...]
        rows = [pl.ds(t, nc, stride=CHUNK) for t in range(CHUNK)]
        us = [u_ref[r, :] for r in rows]
        ua = _rows_to_chunks(us, perm)
        ys = []
        for g in range(GB):
            uv = ua[g]
            inc = jnp.dot(uv, s_ref[g], preferred_element_type=F32)
            hx = _shift_rows(_chunk_scan(inc, a_ref[g], False), False)
            h_ref[g] = hx
            ys.append(jnp.dot(uv, mt_ref[g], preferred_element_type=F32)
                      + lax.dot_general(hx.astype(BF16), o_ref[g], _NT, preferred_element_type=F32))
        yp = _chunks_to_rows(ys, perm, True)
        for t, r in enumerate(rows):
            y = yp[t] + d_ref[...] * us[t]
            y_ref[r, :] = y
            yg_ref[r, :] = _gelu(y)

    g3 = lambda r, c: pl.BlockSpec((GB, r, c), lambda g: (g, 0, 0))
    col = pl.BlockSpec((L, 128), lambda g: (0, g))
    return _call(
        body, "ssm_fwd", (SSM_G // GB,),
        [pl.BlockSpec((L, 128), lambda g: (0, U_COL0 + g)), _ANY,
         g3(CW, CW), g3(CW, 2 * SSM_P), g3(CW, 2 * SSM_P), g3(1, 2 * SSM_P),
         pl.BlockSpec((1, 128), lambda g: (0, g))],
        [col, col, g3(nc, 2 * SSM_P)],
        [jax.ShapeDtypeStruct((L, SSM_W), F32), jax.ShapeDtypeStruct((L, SSM_W), F32),
         jax.ShapeDtypeStruct((SSM_G, nc, 2 * SSM_P), F32)],
        (proj, perm, mt, scat, ocat, a16, d_skip.reshape(1, SSM_W)),
        [pltpu.VMEM((HALF * 128, GB * 128), BF16), pltpu.SemaphoreType.DMA], ride)


def _ssm_bwd(d_yg, y, proj, hx, perm, mt, scat, ocat, a16, d_skip, ride):
    L = proj.shape[0]
    nc = L // CHUNK

    def body(dg_ref, y_ref, u_ref, h_ref, p_hbm, mt_ref, s_ref, o_ref, a_ref, d_ref,
             du_ref, gmt_ref, gs_ref, go_ref, ga_ref, gd_ref, p_scr, sem):
        _load_perm(p_hbm, p_scr, sem)
        perm = p_scr[...]
        rows = [pl.ds(t, nc, stride=CHUNK) for t in range(CHUNK)]
        us = [u_ref[r, :] for r in rows]
        dys = [dg_ref[r, :] * _dgelu(y_ref[r, :]) for r in rows]
        gd = jnp.zeros((1, 128), F32)
        for uv, dy in zip(us, dys):
            gd = gd + jnp.sum(dy * uv, axis=0, keepdims=True)
        gd_ref[...] = gd
        ua = _rows_to_chunks(us, perm)
        dya = _rows_to_chunks(dys, perm)
        lane = lax.broadcasted_iota(jnp.int32, (1, 2 * SSM_P), 1)
        dus = []
        for g in range(GB):
            uv, dy, hx_v = ua[g], dya[g], h_ref[g]
            dh = jnp.dot(dy, o_ref[g], preferred_element_type=F32)
            dinc = _shift_rows(_chunk_scan(dh, a_ref[g], True), True)
            dinc_b = dinc.astype(BF16)
            dus.append(lax.dot_general(dy, mt_ref[g], _NT, preferred_element_type=F32)
                       + lax.dot_general(dinc_b, s_ref[g], _NT, preferred_element_type=F32))
            gmt_ref[g] = lax.dot_general(uv, dy, _TN, preferred_element_type=F32)
            gs_ref[g] = lax.dot_general(uv, dinc_b, _TN, preferred_element_type=F32)
            go_ref[g] = lax.dot_general(dy, hx_v.astype(BF16), _TN, preferred_element_type=F32)
            p1 = dinc * hx_v
            p2 = pltpu.roll(dinc, SSM_P, 1) * hx_v
            t1 = jnp.sum(p1 + pltpu.roll(p1, SSM_P, 1), axis=0, keepdims=True)
            t2 = jnp.sum(p2 - pltpu.roll(p2, SSM_P, 1), axis=0, keepdims=True)
            ga_ref[g] = jnp.where(lane < SSM_P, t1, pltpu.roll(t2, SSM_P, 1))
        dup = _chunks_to_rows(dus, perm, False)
        for t, r in enumerate(rows):
            du_ref[r, :] = dup[t] + d_ref[...] * dys[t]

    g3 = lambda r, c: pl.BlockSpec((GB, r, c), lambda g: (g, 0, 0))
    col = pl.BlockSpec((L, 128), lambda g: (0, g))
    row = pl.BlockSpec((1, 128), lambda g: (0, g))
    return _call(
        body, "ssm_bwd", (SSM_G // GB,),
        [col, col, pl.BlockSpec((L, 128), lambda g: (0, U_COL0 + g)), g3(nc, 2 * SSM_P), _ANY,
         g3(CW, CW), g3(CW, 2 * SSM_P), g3(CW, 2 * SSM_P), g3(1, 2 * SSM_P), row],
        [col, g3(CW, CW), g3(CW, 2 * SSM_P), g3(CW, 2 * SSM_P), g3(1, 2 * SSM_P), row],
        [jax.ShapeDtypeStruct((L, SSM_W), F32), jax.ShapeDtypeStruct((SSM_G, CW, CW), F32),
         jax.ShapeDtypeStruct((SSM_G, CW, 2 * SSM_P), F32), jax.ShapeDtypeStruct((SSM_G, CW, 2 * SSM_P), F32),
         jax.ShapeDtypeStruct((SSM_G, 1, 2 * SSM_P), F32), jax.ShapeDtypeStruct((1, SSM_W), F32)],
        (d_yg, y, proj, hx, perm, mt, scat, ocat, a16, d_skip.reshape(1, SSM_W)),
        [pltpu.VMEM((HALF * 128, GB * 128), BF16), pltpu.SemaphoreType.DMA], ride)


def _merge(og, yg, w_glu, proj, b_glu, wa, ws):
    L = og.shape[0]
    tm = _tile(L, 256)

    def body(og_ref, yg_ref, wg_ref, z0_ref, z1_ref, b_ref, wa_ref, ws_ref, m_ref, gp_ref):
        zs = jnp.concatenate([z0_ref[...], z1_ref[...]], axis=1)
        ygv = yg_ref[...]
        gpre = jnp.dot(ygv.astype(BF16), wg_ref[...], preferred_element_type=F32)
        gp_ref[...] = gpre
        os_ = ygv * _sigmoid(gpre + b_ref[...]) * _silu(zs)
        ogv = og_ref[...].astype(F32)
        ra = lax.rsqrt(jnp.mean(ogv * ogv, axis=-1, keepdims=True) + NORM_EPS)
        rs = lax.rsqrt(jnp.mean(os_ * os_, axis=-1, keepdims=True) + NORM_EPS)
        m_ref[:, :ATTN_W] = (ogv * ra * wa_ref[...]).astype(BF16)
        m_ref[:, ATTN_W:] = (os_ * rs * ws_ref[...]).astype(BF16)

    row = lambda w: pl.BlockSpec((1, w), lambda i: (0, 0))
    return pl.pallas_call(
        body,
        name="merge",
        grid=(L // tm,),
        in_specs=[pl.BlockSpec((tm, ATTN_W), lambda i: (i, 0)), pl.BlockSpec((tm, SSM_W), lambda i: (i, 0)),
                  pl.BlockSpec((SSM_W, SSM_W), lambda i: (0, 0)),
                  pl.BlockSpec((tm, 512), lambda i: (i, 7)), pl.BlockSpec((tm, 512), lambda i: (i, 8)),
                  row(SSM_W), row(ATTN_W), row(SSM_W)],
        out_specs=[pl.BlockSpec((tm, D_MODEL), lambda i: (i, 0)), pl.BlockSpec((tm, SSM_W), lambda i: (i, 0))],
        out_shape=[jax.ShapeDtypeStruct((L, D_MODEL), BF16), jax.ShapeDtypeStruct((L, SSM_W), F32)],
        compiler_params=_cp(("parallel",)),
    )(og, yg, w_glu, proj, proj, b_glu.reshape(1, SSM_W), wa.reshape(1, ATTN_W), ws.reshape(1, SSM_W))


def _outproj_loss(merged, w_out, x, target):
    L = x.shape[0]
    tm, tn = _tile(L, 512), 1024
    ni, nj = L // tm, D_MODEL // tn

    def body(m_ref, w_ref, x_ref, t_ref, d_ref, db_ref, l_ref):
        out = x_ref[...] + jnp.dot(m_ref[...], w_ref[...], preferred_element_type=F32)
        diff = out - t_ref[...]
        d = diff * (1.0 / D_MODEL)
        d_ref[...] = d
        db_ref[...] = d.astype(BF16)
        l_ref[...] = jnp.full((1, 8, 128), jnp.sum(diff * diff), F32)

    return pl.pallas_call(
        body,
        name="outproj_loss",
        grid=(nj, ni),
        in_specs=[pl.BlockSpec((tm, D_MODEL), lambda j, i: (i, 0)),
                  pl.BlockSpec((D_MODEL, tn), lambda j, i: (0, j)),
                  pl.BlockSpec((tm, tn), lambda j, i: (i, j)),
                  pl.BlockSpec((tm, tn), lambda j, i: (i, j))],
        out_specs=[pl.BlockSpec((tm, tn), lambda j, i: (i, j)), pl.BlockSpec((tm, tn), lambda j, i: (i, j)),
                   pl.BlockSpec((1, 8, 128), lambda j, i: (i * nj + j, 0, 0))],
        out_shape=[jax.ShapeDtypeStruct((L, D_MODEL), F32), jax.ShapeDtypeStruct((L, D_MODEL), BF16),
                   jax.ShapeDtypeStruct((ni * nj, 8, 128), F32)],
        compiler_params=_cp(("parallel", "parallel")),
    )(merged, w_out, x, target)


def _merge_bwd(d_out_b, w_out, og, o, yg, gpre, proj, b_glu, wa, ws):
    L = og.shape[0]
    tm = _tile(L, 256)

    def body(dout_ref, wo_ref, og_ref, o_ref, yg_ref, gp_ref, za0_ref, za1_ref, zs0_ref, zs1_ref, b_ref, wa_ref,
             ws_ref, do_ref, dza_ref, dzs_ref, dg_ref, dyg_ref, gwa_ref, gws_ref, gb_ref):
        i = pl.program_id(0)

        @pl.when(i == 0)
        def _():
            gwa_ref[...] = jnp.zeros_like(gwa_ref)
            gws_ref[...] = jnp.zeros_like(gws_ref)
            gb_ref[...] = jnp.zeros_like(gb_ref)

        dm = lax.dot_general(dout_ref[...], wo_ref[...], _NT, preferred_element_type=F32)
        za = jnp.concatenate([za0_ref[...], za1_ref[...]], axis=1)
        zs = jnp.concatenate([zs0_ref[...], zs1_ref[...]], axis=1)
        ogv, dma = og_ref[...].astype(F32), dm[:, :ATTN_W]
        ra = lax.rsqrt(jnp.mean(ogv * ogv, axis=-1, keepdims=True) + NORM_EPS)
        xh = ogv * ra
        gwa_ref[...] += jnp.sum(dma * xh, axis=0, keepdims=True)
        gx = dma * wa_ref[...]
        d_og = ra * (gx - xh * jnp.mean(gx * xh, axis=-1, keepdims=True))
        do_ref[...] = (d_og * _silu(za)).astype(BF16)
        dza_ref[...] = (d_og * o_ref[...].astype(F32) * _dsilu(za)).astype(BF16)
        ygv = yg_ref[...]
        sg = _sigmoid(gp_ref[...] + b_ref[...])
        y2 = ygv * sg
        sz = _silu(zs)
        os_ = y2 * sz
        dms = dm[:, ATTN_W:]
        rs = lax.rsqrt(jnp.mean(os_ * os_, axis=-1, keepdims=True) + NORM_EPS)
        xs = os_ * rs
        gws_ref[...] += jnp.sum(dms * xs, axis=0, keepdims=True)
        gxs = dms * ws_ref[...]
        d_os = rs * (gxs - xs * jnp.mean(gxs * xs, axis=-1, keepdims=True))
        dzs_ref[...] = (d_os * y2 * _dsilu(zs)).astype(BF16)
        d_y2 = d_os * sz
        d_g = d_y2 * ygv * sg * (1.0 - sg)
        dg_ref[...] = d_g.astype(BF16)
        gb_ref[...] += jnp.sum(d_g, axis=0, keepdims=True)
        dyg_ref[...] = d_y2 * sg

    row = lambda w: pl.BlockSpec((1, w), lambda i: (0, 0))
    full = lambda w: pl.BlockSpec((tm, w), lambda i: (i, 0))
    half = lambda c: pl.BlockSpec((tm, 512), lambda i: (i, c))
    return pl.pallas_call(
        body,
        name="merge_bwd",
        grid=(L // tm,),
        in_specs=[full(D_MODEL), pl.BlockSpec((D_MODEL, D_MODEL), lambda i: (0, 0)),
                  full(ATTN_W), full(ATTN_W), full(SSM_W), full(SSM_W),
                  half(3), half(4), half(7), half(8), row(SSM_W), row(ATTN_W), row(SSM_W)],
        out_specs=[full(ATTN_W), full(ATTN_W), full(SSM_W), full(SSM_W), full(SSM_W),
                   row(ATTN_W), row(SSM_W), row(SSM_W)],
        out_shape=[jax.ShapeDtypeStruct((L, ATTN_W), BF16), jax.ShapeDtypeStruct((L, ATTN_W), BF16),
                   jax.ShapeDtypeStruct((L, SSM_W), BF16), jax.ShapeDtypeStruct((L, SSM_W), BF16),
                   jax.ShapeDtypeStruct((L, SSM_W), F32),
                   jax.ShapeDtypeStruct((1, ATTN_W), F32), jax.ShapeDtypeStruct((1, SSM_W), F32),
                   jax.ShapeDtypeStruct((1, SSM_W), F32)],
        compiler_params=_cp(("arbitrary",)),
    )(d_out_b, w_out, og, o, yg, gpre, proj, proj, proj, proj, b_glu.reshape(1, SSM_W), wa.reshape(1, ATTN_W),
      ws.reshape(1, SSM_W))


def _rms_bwd_x(x, norm_w, d_hn, d_out, ride):
    L = x.shape[0]
    tm = _tile(L, 256)

    def body(x_ref, w_ref, dh_ref, do_ref, gx_ref, gw_ref):
        i = pl.program_id(0)

        @pl.when(i == 0)
        def _():
            gw_ref[...] = jnp.zeros_like(gw_ref)

        xv, dh = x_ref[...], dh_ref[...]
        r = lax.rsqrt(jnp.mean(xv * xv, axis=-1, keepdims=True) + NORM_EPS)
        xh = xv * r
        gw_ref[...] += jnp.sum(dh * xh, axis=0, keepdims=True)
        gx = dh * w_ref[...]
        gx_ref[...] = do_ref[...] + r * (gx - xh * jnp.mean(gx * xh, axis=-1, keepdims=True))

    blk = pl.BlockSpec((tm, D_MODEL), lambda i: (i, 0))
    row = pl.BlockSpec((1, D_MODEL), lambda i: (0, 0))
    return _call(body, "rms_bwd_x", (L // tm,), [blk, row, blk, blk], [blk, row],
                 [jax.ShapeDtypeStruct((L, D_MODEL), F32), jax.ShapeDtypeStruct((1, D_MODEL), F32)],
                 (x, norm_w.reshape(1, D_MODEL), d_hn, d_out), ride=ride)


def _rope_table(positions):
    inv_freq = ROPE_THETA ** (-jnp.arange(0, HEAD_DIM, 2, dtype=F32) / HEAD_DIM)
    ang = positions.astype(F32)[:, None] * inv_freq
    sign = jnp.where(jnp.arange(128) % HEAD_DIM < HEAD_DIM // 2, -1.0, 1.0)
    return jnp.concatenate([jnp.tile(jnp.cos(ang), (1, 4)), jnp.tile(jnp.sin(ang), (1, 4)) * sign], axis=1)


def _step(x, positions, target, w, core, chip):
    small = {n: w[n] for n in _SMALL}
    tab = _rope_table(positions)
    mt_b, scat_b, ocat_b, a16 = _ssm_prep(small)
    perm = _chunk_perm()
    blocks = lambda t: t.reshape(N_DEV, t.shape[0] // N_DEV, t.shape[1])

    proj, hn, wt_in = _rms_inproj_gather(x, small["norm_w"], w["w_in"].T.astype(BF16), chip)
    wt_in = wt_in.reshape(IN_W, D_MODEL)
    q_rot, k_rot = _qk_prep(proj, tab, small["q_norm_w"], small["k_norm_w"])
    (og, o, lse), (w_glu,) = _attn_fwd(q_rot, k_rot, proj, small["sinks"],
                                       _gather_exchange([w["w_glu"].astype(BF16)]))
    (y, yg, hx), (w_out,) = _ssm_fwd(proj, perm, mt_b, scat_b, ocat_b, a16, small["d_skip"],
                                     _gather_exchange([w["w_out"].astype(BF16)]))
    w_glu, w_out = w_glu.reshape(SSM_W, SSM_W), w_out.reshape(D_MODEL, D_MODEL)
    merged, gpre = _merge(og, yg, w_glu, proj, small["b_glu"], small["attn_out_norm_w"], small["ssm_out_norm_w"])
    d_out, d_out_b, loss_parts = _outproj_loss(merged, w_out, x, target)
    loss = 0.5 * jnp.sum(loss_parts[:, 0, 0]) / D_MODEL

    g_w_out = blocks(_mm(merged, d_out_b, "tn", F32, "grad_w_out", tm=1024))
    d_o, d_za, d_zs, d_g, d_yg1, g_wa, g_ws, g_bglu = _merge_bwd(
        d_out_b, w_out, og, o, yg, gpre, proj, small["b_glu"], small["attn_out_norm_w"], small["ssm_out_norm_w"])
    g_w_glu = blocks(_mm(yg, d_g, "tn", F32, "grad_w_glu"))
    d_yg = _mm(d_g, w_glu, "nt", F32, "d_yg", add=d_yg1)
    (d_u, g_mt, g_scat, g_ocat, g_a16, g_dskip), (ra_out, ra_glu) = _ssm_bwd(
        d_yg, y, proj, hx, perm, mt_b, scat_b, ocat_b, a16, small["d_skip"], _pair_exchange([g_w_out, g_w_glu]))
    p_out = _pair_sum(g_w_out, ra_out, core, BF16, "pair_sum_out")
    p_glu = _pair_sum(g_w_glu, ra_glu, core, BF16, "pair_sum_glu")
    (d_q, d_k, d_v, g_sinks), (rb_out, rb_glu) = _attn_bwd(
        q_rot, k_rot, proj, small["sinks"], d_o, o, lse, _chip_exchange([p_out, p_glu]))
    d_proj, g_qw, g_kw = _qk_prep_bwd(proj, tab, small["q_norm_w"], small["k_norm_w"], d_q, d_k, d_v,
                                      d_za, d_u, d_zs)
    g_qw = g_qw[0, :HEAD_DIM] + g_qw[0, HEAD_DIM:]
    g_kw = g_kw[0, :HEAD_DIM] + g_kw[0, HEAD_DIM:]
    g_in_a = blocks(_mm(d_proj, hn, "tn", F32, "grad_w_in_a", tm=1152, panel=0))
    g_in_b, (ra_a,) = _mm(d_proj, hn, "tn", F32, "grad_w_in_b", tm=1152, panel=1, ride=_pair_exchange([g_in_a]))
    g_in_b = blocks(g_in_b)
    p_a = _pair_sum(g_in_a, ra_a, core, BF16, "pair_sum_in_a")
    d_hn, (rb_a, ra_b) = _mm(d_proj, wt_in, "nn", F32, "d_hn", tm=1024,
                             ride=_both(_chip_exchange([p_a]), _pair_exchange([g_in_b])))
    p_b = _pair_sum(g_in_b, ra_b, core, BF16, "pair_sum_in_b")
    g_small, (rb_b,) = _ssm_prep_bwd(small, g_mt, g_scat, g_ocat, g_a16, _chip_exchange([p_b]))
    (grad_x, g_nw), _ = _rms_bwd_x(x, small["norm_w"], d_hn, d_out, None)

    g_small.update(norm_w=g_nw.reshape(-1), q_norm_w=g_qw.reshape(-1), k_norm_w=g_kw.reshape(-1),
                   sinks=g_sinks[0, :N_HEADS], d_skip=g_dskip.reshape(-1), b_glu=g_bglu.reshape(-1),
                   attn_out_norm_w=g_wa.reshape(-1), ssm_out_norm_w=g_ws.reshape(-1))
    g_packed = _slab_all_reduce(_pack(g_small, loss).reshape(N_DEV, _PACK_ROWS // N_DEV, 128))
    g_packed = g_packed.reshape(_PACK_ROWS, 128)
    grads = _unpack(g_packed, w)
    parts = dict(w_in=([p_a, p_b], [rb_a, rb_b]), w_glu=([p_glu], [rb_glu]), w_out=([p_out], [rb_out]))
    return g_packed[_LOSS_ROW, 0], grad_x, grads, parts


_ANY = pl.BlockSpec(memory_space=pl.ANY)


class _Exchange:
    def __init__(self, arrays, out_shape, sems, start, finish, relay=None):
        self.arrays, self.out_shape, self.sems, self.start, self.finish = arrays, out_shape, sems, start, finish
        self.relay = relay if relay is not None else (lambda ins, outs, sems: None)


def _gather_exchange(blocks):
    n = len(blocks)

    def parts(ins, outs, sems):
        send_sems, recv_sems, local_sems = sems
        x, y, c = lax.axis_index("x"), lax.axis_index("y"), lax.axis_index("c")
        me, sibling = (x, y, c), (x, y, 1 - c)
        chips = [(1 - x, y), (x, 1 - y), (1 - x, 1 - y)]

        def slot(k, dev):
            return outs[k].at[4 * dev[0] + 2 * dev[1] + dev[2]]

        def copy(k, q, block, to, src=None):
            return pltpu.make_async_remote_copy(
                src_ref=slot(k, block) if src is None else src, dst_ref=slot(k, block),
                send_sem=send_sems.at[k, q], recv_sem=recv_sems.at[k, q], device_id=to, device_id_type=MESH)

        mine = [pltpu.make_async_copy(ins[k], slot(k, me), local_sems.at[k]) for k in range(n)]
        first = []
        for k in range(n):
            first.append(copy(k, 0, me, sibling, src=ins[k]))
            first += [copy(k, 1 + j, me, (*chip, c), src=ins[k]) for j, chip in enumerate(chips)]
        return me, sibling, chips, c, copy, mine, first

    def start(ins, outs, sems):
        *_, mine, first = parts(ins, outs, sems)
        for cp in mine + first:
            cp.start()

    def relay(ins, outs, sems):
        me, sibling, chips, c, copy, _, _ = parts(ins, outs, sems)
        for j, chip in enumerate(chips):
            for k in range(n):
                copy(k, 1 + j, (*chip, c), me).wait_recv()
                copy(k, 4 + j, (*chip, c), sibling).start()

    def finish(ins, outs, sems):
        me, sibling, chips, c, copy, mine, first = parts(ins, outs, sems)
        for k in range(n):
            copy(k, 0, sibling, me).wait_recv()
            for j, chip in enumerate(chips):
                copy(k, 4 + j, (*chip, 1 - c), me).wait_recv()
        for cp in first + [copy(k, 4 + j, (*chip, c), sibling) for k in range(n) for j, chip in enumerate(chips)]:
            cp.wait_send()
        for cp in mine:
            cp.wait()

    return _Exchange(blocks, [jax.ShapeDtypeStruct((N_DEV,) + b.shape, b.dtype) for b in blocks],
                     [pltpu.SemaphoreType.DMA((n, 7)), pltpu.SemaphoreType.DMA((n, 7)), pltpu.SemaphoreType.DMA((n,))],
                     start, finish, relay)


def _direct_exchange(arrays, out_lead, fan, route):
    n = len(arrays)

    def copies(ins, outs, sems):
        send_sems, recv_sems = sems
        legs = route(lax.axis_index("x"), lax.axis_index("y"), lax.axis_index("c"))
        return [pltpu.make_async_remote_copy(
            src_ref=ins[k].at[src], dst_ref=outs[k].at[q], send_sem=send_sems.at[k, q], recv_sem=recv_sems.at[k, q],
            device_id=to, device_id_type=MESH) for k in range(n) for src, q, to in legs]

    def start(ins, outs, sems):
        for cp in copies(ins, outs, sems):
            cp.start()

    def finish(ins, outs, sems):
        for cp in copies(ins, outs, sems):
            cp.wait()

    return _Exchange(arrays, [jax.ShapeDtypeStruct((out_lead,) + a.shape[1:], a.dtype) for a in arrays],
                     [pltpu.SemaphoreType.DMA((n, fan)), pltpu.SemaphoreType.DMA((n, fan))], start, finish)


def _pair_exchange(grads):
    return _direct_exchange(grads, 4, 4, lambda x, y, c: [(2 * chip + (1 - c), chip, (x, y, 1 - c))
                                                          for chip in range(4)])


def _chip_exchange(parts):
    def route(x, y, c):
        chips = [(1 - x, y), (x, 1 - y), (1 - x, 1 - y)]
        return [(2 * chip[0] + chip[1], q, (*chip, c)) for q, chip in enumerate(chips)]
    return _direct_exchange(parts, 3, 3, route)


def _both(ex1, ex2):
    n1, s1 = len(ex1.arrays), len(ex1.sems)

    def halves(ins, outs, sems):
        return (ins[:n1], outs[:n1], sems[:s1]), (ins[n1:], outs[n1:], sems[s1:])

    def start(ins, outs, sems):
        h1, h2 = halves(ins, outs, sems)
        ex1.start(*h1)
        ex2.start(*h2)

    def relay(ins, outs, sems):
        h1, h2 = halves(ins, outs, sems)
        ex1.relay(*h1)
        ex2.relay(*h2)

    def finish(ins, outs, sems):
        h1, h2 = halves(ins, outs, sems)
        ex1.finish(*h1)
        ex2.finish(*h2)

    return _Exchange(list(ex1.arrays) + list(ex2.arrays), list(ex1.out_shape) + list(ex2.out_shape),
                     list(ex1.sems) + list(ex2.sems), start, finish, relay)


def _call(body, name, grid, in_specs, out_specs, out_shape, args, scratch_shapes=(), ride=None):
    if ride is None:
        sem = ("arbitrary",) * len(grid)
        return pl.pallas_call(body, name=name, grid=grid, in_specs=in_specs, out_specs=out_specs, out_shape=out_shape,
                              scratch_shapes=list(scratch_shapes), compiler_params=_cp(sem))(*args), None
    n_in, n_out, n_scr, n_x = len(in_specs), len(out_specs), len(scratch_shapes), len(ride.arrays)

    def wrapped(*refs):
        ins, refs = refs[:n_in], refs[n_in:]
        x_in, refs = refs[:n_x], refs[n_x:]
        outs, refs = refs[:n_out], refs[n_out:]
        x_out, refs = refs[:n_x], refs[n_x:]
        scr, sems = refs[:n_scr], refs[n_scr:]
        step, total = pl.program_id(0), grid[0]
        for a in range(1, len(grid)):
            step, total = step * grid[a] + pl.program_id(a), total * grid[a]
        @pl.when(step == 0)
        def _():
            ride.start(x_in, x_out, sems)

        @pl.when(step == max(total - 2, 0))
        def _():
            ride.relay(x_in, x_out, sems)

        body(*ins, *outs, *scr)

        @pl.when(step == total - 1)
        def _():
            ride.finish(x_in, x_out, sems)

    res = pl.pallas_call(
        wrapped, name=name, grid=grid, in_specs=list(in_specs) + [_ANY] * n_x,
        out_specs=list(out_specs) + [_ANY] * n_x, out_shape=list(out_shape) + list(ride.out_shape),
        scratch_shapes=list(scratch_shapes) + list(ride.sems),
        compiler_params=_cp(("arbitrary",) * len(grid)))(*args, *ride.arrays)
    return res[:n_out], list(res[n_out:])


def _pair_sum(g, ra, core, out_dtype, name):
    _, r, C = g.shape
    tr = _tile(r, 576)

    def body(c_ref, g_ref, ra_ref, p_ref):
        p_ref[...] = (g_ref[...] + ra_ref[...]).astype(p_ref.dtype)

    return pl.pallas_call(
        body,
        name=name,
        grid_spec=pltpu.PrefetchScalarGridSpec(
            num_scalar_prefetch=1,
            grid=(4, r // tr),
            in_specs=[pl.BlockSpec((1, tr, C), lambda j, t, c_ref: (2 * j + c_ref[0], t, 0)),
                      pl.BlockSpec((1, tr, C), lambda j, t, c_ref: (j, t, 0))],
            out_specs=pl.BlockSpec((1, tr, C), lambda j, t, c_ref: (j, t, 0)),
        ),
        out_shape=jax.ShapeDtypeStruct((4, r, C), out_dtype),
        compiler_params=_cp(("parallel", "parallel")),
    )(core, g, ra)


def _slab_all_reduce(slab):
    _, r, lanes = slab.shape

    def body(s_ref, o_ref, ra, rb, ps, sems_a, sems_b, sems_c):
        x, y, c = lax.axis_index("x"), lax.axis_index("y"), lax.axis_index("c")
        chips = [(1 - x, y), (x, 1 - y), (1 - x, 1 - y)]
        pair = [pltpu.make_async_remote_copy(
            src_ref=s_ref.at[2 * k + (1 - c)], dst_ref=ra.at[k], send_sem=sems_a.at[0, k], recv_sem=sems_a.at[1, k],
            device_id=(x, y, 1 - c), device_id_type=MESH) for k in range(4)]
        for cp in pair:
            cp.start()
        for cp in pair:
            cp.wait()
        for k in range(4):
            ps[k] = s_ref[2 * k + c] + ra[k]
        cross = [pltpu.make_async_remote_copy(
            src_ref=ps.at[2 * ch[0] + ch[1]], dst_ref=rb.at[q], send_sem=sems_b.at[0, q], recv_sem=sems_b.at[1, q],
            device_id=(*ch, c), device_id_type=MESH) for q, ch in enumerate(chips)]
        for cp in cross:
            cp.start()
        for cp in cross:
            cp.wait()
        me = 4 * x + 2 * y + c
        o_ref[me] = ((ps[2 * x + y] + rb[0]) + rb[1]) + rb[2]
        flips = [(dx, dy, dc) for dx in (0, 1) for dy in (0, 1) for dc in (0, 1) if dx + dy + dc]
        spread = [pltpu.make_async_remote_copy(
            src_ref=o_ref.at[me], dst_ref=o_ref.at[me], send_sem=sems_c.at[0, q], recv_sem=sems_c.at[1, q],
            device_id=(x + dx - 2 * x * dx, y + dy - 2 * y * dy, c + dc - 2 * c * dc), device_id_type=MESH)
            for q, (dx, dy, dc) in enumerate(flips)]
        for cp in spread:
            cp.start()
        for q, (dx, dy, dc) in enumerate(flips):
            peer = 4 * (x + dx - 2 * x * dx) + 2 * (y + dy - 2 * y * dy) + (c + dc - 2 * c * dc)
            pltpu.make_async_remote_copy(
                src_ref=o_ref.at[peer], dst_ref=o_ref.at[peer], send_sem=sems_c.at[0, q], recv_sem=sems_c.at[1, q],
                device_id=(x, y, c), device_id_type=MESH).wait_recv()
        for cp in spread:
            cp.wait_send()

    whole = pl.BlockSpec(memory_space=pltpu.VMEM)
    return pl.pallas_call(
        body, name="slab_all_reduce", in_specs=[whole], out_specs=whole,
        out_shape=jax.ShapeDtypeStruct(slab.shape, F32),
        scratch_shapes=[pltpu.VMEM((4, r, lanes), F32), pltpu.VMEM((3, r, lanes), F32), pltpu.VMEM((4, r, lanes), F32),
                        pltpu.SemaphoreType.DMA((2, 4)), pltpu.SemaphoreType.DMA((2, 3)),
                        pltpu.SemaphoreType.DMA((2, 7))],
        compiler_params=_cp(),
    )(slab)


def _adamw_reduced(ps, rbs, chip, w, m, v, name):
    nh = len(ps)
    R, C = w.shape
    ch = C // nh
    tr = _tile(R, 288)
    nt = R // tr
    c1 = 1.0 - ADAM_B1 ** ADAM_STEP
    c2 = 1.0 - ADAM_B2 ** ADAM_STEP

    def body(c_ref, *refs):
        p_refs, rb_refs = refs[:nh], refs[nh:2 * nh]
        w_ref, m_ref, v_ref, g_ref, d_ref, nm_ref, nv_ref = refs[2 * nh:]
        for h in range(nh):
            @pl.when(pl.program_id(0) == h)
            def _(h=h):
                rb = rb_refs[h]
                gv = p_refs[h][0].astype(F32) + rb[0].astype(F32)
                gv = gv + rb[1].astype(F32)
                gv = gv + rb[2].astype(F32)
                nm = ADAM_B1 * m_ref[...] + (1.0 - ADAM_B1) * gv
                nv = ADAM_B2 * v_ref[...] + (1.0 - ADAM_B2) * (gv * gv)
                g_ref[...] = gv
                nm_ref[...] = nm
                nv_ref[...] = nv
                d_ref[...] = -ADAM_LR * ((nm / c1) / (jnp.sqrt(nv / c2) + ADAM_EPS) + ADAM_WD * w_ref[...])

    def held(h):
        return lambda hh, tt: jnp.where(hh == h, tt, jnp.where(hh < h, 0, nt - 1))

    p_specs = [pl.BlockSpec((1, tr, ch), lambda hh, tt, c_ref, f=held(h): (c_ref[0], f(hh, tt), 0))
               for h in range(nh)]
    rb_specs = [pl.BlockSpec((3, tr, ch), lambda hh, tt, c_ref, f=held(h): (0, f(hh, tt), 0)) for h in range(nh)]
    blk = pl.BlockSpec((tr, ch), lambda hh, tt, c_ref: (tt, hh))
    return pl.pallas_call(
        body,
        name=name,
        grid_spec=pltpu.PrefetchScalarGridSpec(
            num_scalar_prefetch=1, grid=(nh, nt), in_specs=p_specs + rb_specs + [blk] * 3, out_specs=[blk] * 4),
        out_shape=[jax.ShapeDtypeStruct((R, C), F32)] * 4,
        compiler_params=_cp(("arbitrary", "arbitrary")),
    )(chip, *ps, *rbs, w, m, v)


_SMALL = ("norm_w", "q_norm_w", "k_norm_w", "sinks", "a_re", "a_im", "log_step", "b_re", "b_im", "c_re", "c_im",
          "d_skip", "b_glu", "attn_out_norm_w", "ssm_out_norm_w")
_WEIGHTS = ("norm_w", "w_in", "q_norm_w", "k_norm_w", "sinks", "a_re", "a_im", "log_step", "b_re", "b_im", "c_re",
            "c_im", "d_skip", "w_glu", "b_glu", "attn_out_norm_w", "ssm_out_norm_w", "w_out")
_SMALL_2D = dict(norm_w=(1, 2048), q_norm_w=(1, 64), k_norm_w=(1, 64), sinks=(1, 16), a_re=(64, 64), a_im=(64, 64),
                 log_step=(1, 64), b_re=(1024, 64), b_im=(1024, 64), c_re=(1024, 64), c_im=(1024, 64),
                 d_skip=(1, 1024), b_glu=(1, 1024), attn_out_norm_w=(1, 1024), ssm_out_norm_w=(1, 1024))
_P_MINOR = ("b_re", "b_im")


def _flat_form(n, t):
    return t.transpose(0, 2, 1) if n in _P_MINOR else t


def _own_form(n, t, shape):
    if n in _P_MINOR:
        return t.reshape(shape[0], shape[2], shape[1]).transpose(0, 2, 1)
    return t.reshape(shape)


def _slab_rows(n):
    return -(-n // 1024) * 8


_PACK_ROWS = 2304


_LOSS_ROW = 2192


def _pack(d, loss):
    parts = []
    for n in _SMALL:
        flat = _flat_form(n, d[n]).reshape(-1).astype(F32)
        rows = _slab_rows(flat.shape[0])
        parts.append(jnp.pad(flat, (0, rows * 128 - flat.shape[0])).reshape(rows, 128))
    assert sum(p.shape[0] for p in parts) == _LOSS_ROW
    parts.append(jnp.pad(loss.reshape(1, 1), ((0, _PACK_ROWS - _LOSS_ROW - 1), (0, 127))))
    return jnp.concatenate(parts, axis=0)


def _unpack(packed, like):
    out, off = {}, 0
    for n in _SMALL:
        size = math.prod(like[n].shape)
        rows = _slab_rows(size)
        out[n] = _own_form(n, packed[off:off + rows].reshape(-1)[:size], like[n].shape)
        off += rows
    return out


def _adamw_small(g, w, m, v):
    c1 = 1.0 - ADAM_B1 ** ADAM_STEP
    c2 = 1.0 - ADAM_B2 ** ADAM_STEP
    k = len(_SMALL)

    def body(*refs):
        ins, outs = refs[:4 * k], refs[4 * k:]
        for j in range(k):
            gv, wv, mv, vv = (ins[q * k + j][...] for q in range(4))
            nm = ADAM_B1 * mv + (1.0 - ADAM_B1) * gv
            nv = ADAM_B2 * vv + (1.0 - ADAM_B2) * (gv * gv)
            outs[j][...] = -ADAM_LR * ((nm / c1) / (jnp.sqrt(nv / c2) + ADAM_EPS) + ADAM_WD * wv)
            outs[k + j][...] = nm
            outs[2 * k + j][...] = nv

    args = [_flat_form(n, d[n]).reshape(_SMALL_2D[n]) for d in (g, w, m, v) for n in _SMALL]
    shapes = [jax.ShapeDtypeStruct(_SMALL_2D[n], F32) for _ in range(3) for n in _SMALL]
    outs = pl.pallas_call(body, name="adamw_small", out_shape=shapes, compiler_params=_cp())(*args)
    res = []
    for q in range(3):
        res.append({n: _own_form(n, outs[q * k + j], w[n].shape) for j, n in enumerate(_SMALL)})
    return res


def kernel(x, positions, norm_w, w_in, q_norm_w, k_norm_w, sinks, a_re, a_im, log_step, b_re, b_im, c_re, c_im, d_skip, w_glu, b_glu, attn_out_norm_w, ssm_out_norm_w, w_out, loss_target, m_norm_w, m_w_in, m_q_norm_w, m_k_norm_w, m_sinks, m_a_re, m_a_im, m_log_step, m_b_re, m_b_im, m_c_re, m_c_im, m_d_skip, m_w_glu, m_b_glu, m_attn_out_norm_w, m_ssm_out_norm_w, m_w_out, v_norm_w, v_w_in, v_q_norm_w, v_k_norm_w, v_sinks, v_a_re, v_a_im, v_log_step, v_b_re, v_b_im, v_c_re, v_c_im, v_d_skip, v_w_glu, v_b_glu, v_attn_out_norm_w, v_ssm_out_norm_w, v_w_out):
    w = dict(norm_w=norm_w, w_in=w_in, q_norm_w=q_norm_w, k_norm_w=k_norm_w, sinks=sinks, a_re=a_re, a_im=a_im,
             log_step=log_step, b_re=b_re, b_im=b_im, c_re=c_re, c_im=c_im, d_skip=d_skip, w_glu=w_glu, b_glu=b_glu,
             attn_out_norm_w=attn_out_norm_w, ssm_out_norm_w=ssm_out_norm_w, w_out=w_out)
    m = dict(norm_w=m_norm_w, w_in=m_w_in, q_norm_w=m_q_norm_w, k_norm_w=m_k_norm_w, sinks=m_sinks, a_re=m_a_re,
             a_im=m_a_im, log_step=m_log_step, b_re=m_b_re, b_im=m_b_im, c_re=m_c_re, c_im=m_c_im, d_skip=m_d_skip,
             w_glu=m_w_glu, b_glu=m_b_glu, attn_out_norm_w=m_attn_out_norm_w, ssm_out_norm_w=m_ssm_out_norm_w,
             w_out=m_w_out)
    v = dict(norm_w=v_norm_w, w_in=v_w_in, q_norm_w=v_q_norm_w, k_norm_w=v_k_norm_w, sinks=v_sinks, a_re=v_a_re,
             a_im=v_a_im, log_step=v_log_step, b_re=v_b_re, b_im=v_b_im, c_re=v_c_re, c_im=v_c_im, d_skip=v_d_skip,
             w_glu=v_w_glu, b_glu=v_b_glu, attn_out_norm_w=v_attn_out_norm_w, ssm_out_norm_w=v_ssm_out_norm_w,
             w_out=v_w_out)
    core = lax.axis_index("c").astype(jnp.int32).reshape(1)
    chip = (2 * lax.axis_index("x") + lax.axis_index("y")).astype(jnp.int32).reshape(1)

    loss, grad_x, grads, parts = _step(x[0], positions[0], loss_target[0], w, core, chip)
    delta, new_m, new_v = {}, {}, {}
    for n in ("w_glu", "w_out"):
        grads[n], delta[n], new_m[n], new_v[n] = _adamw_reduced(*parts[n], chip, w[n], m[n], v[n], f"adamw_{n}")
    g_t, d_t, m_t, v_t = _adamw_reduced(*parts["w_in"], chip, w["w_in"].T, m["w_in"].T, v["w_in"].T, "adamw_w_in")
    grads["w_in"], delta["w_in"], new_m["w_in"], new_v["w_in"] = g_t.T, d_t.T, m_t.T, v_t.T
    d_s, m_s, v_s = _adamw_small(grads, w, m, v)
    delta.update(d_s)
    new_m.update(m_s)
    new_v.update(v_s)

    return (loss, grad_x[None], *[grads[n] for n in _WEIGHTS], *[delta[n] for n in _WEIGHTS],
            *[new_m[n] for n in _WEIGHTS], *[new_v[n] for n in _WEIGHTS])
```

```python
import math

import jax
import jax.numpy as jnp
from jax import lax
from jax.experimental import pallas as pl
from jax.experimental.pallas import tpu as pltpu

F32 = jnp.float32
BF16 = jnp.bfloat16

D_MODEL = 2048
ATTN_W = 1024
KV_W = 256
SSM_W = 1024
HEAD_DIM = 64
N_HEADS = 16
N_KV = 4
IN_W = 4608
BLOCK = 128
ROPE_THETA = 10000.0
NORM_EPS = 1e-6
SSM_G = 64
SSM_P = 64
SSM_H = 16
CHUNK = 16
CW = CHUNK * SSM_H
N_DEV = 8

ADAM_LR = 0.001
ADAM_B1 = 0.9
ADAM_B2 = 0.999
ADAM_EPS = 1e-08
ADAM_WD = 0.01
ADAM_STEP = 10

VMEM_LIMIT = 56 * 1024 * 1024
MESH = pl.DeviceIdType.MESH


def _cp(sem=None):
    if sem is None:
        return pltpu.CompilerParams(vmem_limit_bytes=VMEM_LIMIT)
    return pltpu.CompilerParams(vmem_limit_bytes=VMEM_LIMIT, dimension_semantics=sem)


def _sigmoid(x):
    return 0.5 * jnp.tanh(0.5 * x) + 0.5


def _silu(x):
    return x * _sigmoid(x)


def _dsilu(x):
    s = _sigmoid(x)
    return s * (1.0 + x * (1.0 - s))


_GELU_C = math.sqrt(2.0 / math.pi)


def _gelu(y):
    t = jnp.tanh(_GELU_C * (y + 0.044715 * y * y * y))
    return 0.5 * y * (1.0 + t)


def _dgelu(y):
    t = jnp.tanh(_GELU_C * (y + 0.044715 * y * y * y))
    return 0.5 * (1.0 + t) + 0.5 * y * (1.0 - t * t) * _GELU_C * (1.0 + 3.0 * 0.044715 * y * y)


def _tile(n, want):
    if n <= want:
        return n
    for t in range(want - want % 16, 0, -16):
        if n % t == 0:
            return t
    raise ValueError((n, want))


def _mm(a, b, mode, out_dtype, name, tm=512, tn=1024, add=None, ride=None, panel=None):
    if mode == "nn":
        (M, K), (K2, N) = a.shape, b.shape
    elif mode == "nt":
        (M, K), (N, K2) = a.shape, b.shape
    else:
        (K, M), (K2, N) = a.shape, b.shape
    assert K == K2
    tm, tn = _tile(M, tm), _tile(N, tn)
    p0 = 0
    if panel is not None:
        assert mode != "nt" and add is None
        p0, N = panel, tn
    dn = {"nn": _NN, "nt": _NT, "tn": _TN}[mode]

    def body(a_ref, b_ref, *rest):
        o_ref = rest[-1]
        acc = lax.dot_general(a_ref[...].astype(BF16), b_ref[...].astype(BF16), dn, preferred_element_type=F32)
        if add is not None:
            acc = acc + rest[0][...]
        o_ref[...] = acc.astype(o_ref.dtype)

    a_spec = pl.BlockSpec((K, tm), lambda j, i: (0, i)) if mode == "tn" else pl.BlockSpec((tm, K), lambda j, i: (i, 0))
    b_spec = (pl.BlockSpec((tn, K), lambda j, i: (j, 0)) if mode == "nt"
              else pl.BlockSpec((K, tn), lambda j, i: (0, j + p0)))
    o_spec = pl.BlockSpec((tm, tn), lambda j, i: (i, j))
    extra = () if add is None else (add,)
    if ride is not None:
        (out,), landed = _call(body, name, (N // tn, M // tm), [a_spec, b_spec] + [o_spec] * len(extra), [o_spec],
                               [jax.ShapeDtypeStruct((M, N), out_dtype)], (a, b, *extra), ride=ride)
        return out, landed
    return pl.pallas_call(
        body,
        name=name,
        grid=(N // tn, M // tm),
        in_specs=[a_spec, b_spec] + [o_spec] * len(extra),
        out_specs=o_spec,
        out_shape=jax.ShapeDtypeStruct((M, N), out_dtype),
        compiler_params=_cp(("parallel", "parallel")),
    )(a, b, *extra)


_CHIP_ORDER = (0, 2, 1, 3)


def _rms_inproj_gather(x, norm_w, wt_shard, chip):
    L = x.shape[0]
    tm = _tile(L, 1024)
    ni = L // tm
    r = IN_W // N_DEV
    tn = 2 * r

    def body(chip_ref, x_ref, nw_ref, shard, proj_ref, hn_hbm, wt_hbm, hn_scr, w_scr, send_sems, recv_sems, loc_sems):
        jc, i = pl.program_id(0), pl.program_id(1)
        xx, yy, c = lax.axis_index("x"), lax.axis_index("y"), lax.axis_index("c")
        me, sibling = (xx, yy, c), (xx, yy, 1 - c)
        chips = [(1 - xx, yy), (xx, 1 - yy), (1 - xx, 1 - yy)]

        def slot(dev):
            return wt_hbm.at[4 * dev[0] + 2 * dev[1] + dev[2]]

        def copy(q, block, to, src=None):
            return pltpu.make_async_remote_copy(
                src_ref=slot(block) if src is None else src, dst_ref=slot(block),
                send_sem=send_sems.at[q], recv_sem=recv_sems.at[q], device_id=to, device_id_type=MESH)

        def rows_of(buf, core):
            return w_scr.at[buf, pl.ds(pl.multiple_of(core * r, 16), r)]

        mine = pltpu.make_async_copy(shard, slot(me), loc_sems.at[0])
        sends = [copy(0, me, sibling, src=shard)] + [copy(1 + j, me, (*ch, c), src=shard) for j, ch in enumerate(chips[:2])]
        relay_block = (xx + (1 - c) * (1 - 2 * xx), yy + c * (1 - 2 * yy), c)
        relay = copy(3, relay_block, (xx + c * (1 - 2 * xx), yy + (1 - c) * (1 - 2 * yy), c))
        first = jnp.logical_and(jc == 0, i == 0)

        @pl.when(first)
        def _():
            mine.start()
            for cp in sends:
                cp.start()
            own = pltpu.make_async_copy(shard, rows_of(0, c), loc_sems.at[1])
            own.start()
            copy(0, sibling, me).wait_recv()
            sib = pltpu.make_async_copy(slot(sibling), rows_of(0, 1 - c), loc_sems.at[2])
            sib.start()
            own.wait()
            sib.wait()

        def to_vmem(j, ch):
            pltpu.make_async_copy(slot((*ch, c)), rows_of((1 + j) % 2, c), loc_sems.at[1 + j]).start()

        @pl.when(jnp.logical_and(jc == 1, i == 0))
        def _():
            for j in range(2):
                copy(1 + j, (*chips[j], c), me).wait_recv()
                copy(4 + j, (*chips[j], c), sibling).start()
            relay.start()
            to_vmem(0, chips[0])

        @pl.when(jnp.logical_and(jc == 1, i == ni // 2))
        def _():
            to_vmem(1, chips[1])

        @pl.when(jnp.logical_and(jc == 2, i == ni // 2))
        def _():
            copy(3, (*chips[2], c), me).wait_recv()
            copy(6, (*chips[2], c), sibling).start()
            to_vmem(2, chips[2])

        for j, ch in enumerate(chips):
            @pl.when(jnp.logical_and(jc == 1 + j, i == 0))
            def _(j=j, ch=ch):
                buf = (1 + j) % 2
                copy(4 + j, (*ch, 1 - c), me).wait_recv()
                passed = pltpu.make_async_copy(slot((*ch, 1 - c)), rows_of(buf, 1 - c), loc_sems.at[4 + j])
                passed.start()
                pltpu.make_async_copy(slot((*ch, c)), rows_of(buf, c), loc_sems.at[1 + j]).wait()
                passed.wait()

        rows = pl.ds(pl.multiple_of(i * tm, tm), tm)

        @pl.when(jc == 0)
        def _():
            xv = x_ref[...]
            rstd = lax.rsqrt(jnp.mean(xv * xv, axis=-1, keepdims=True) + NORM_EPS)
            hn_scr[rows, :] = (xv * rstd * nw_ref[...]).astype(BF16)

        keep_hn = pltpu.make_async_copy(hn_scr, hn_hbm, loc_sems.at[7])

        @pl.when(jnp.logical_and(jc == 1, i == 0))
        def _():
            keep_hn.start()

        for buf in range(2):
            @pl.when(jc % 2 == buf)
            def _(buf=buf):
                proj_ref[...] = lax.dot_general(hn_scr[rows, :], w_scr[buf], _NT, preferred_element_type=F32)

        @pl.when(jnp.logical_and(jc == 3, i == ni - 1))
        def _():
            for cp in sends + [relay]:
                cp.wait_send()
            for j, ch in enumerate(chips):
                copy(4 + j, (*ch, c), sibling).wait_send()
            mine.wait()
            keep_hn.wait()

    def tile_of(jc, chip_ref):
        mask = jnp.where(jc == 1, _CHIP_ORDER[1], jnp.where(jc == 2, _CHIP_ORDER[2], jnp.where(jc == 3, _CHIP_ORDER[3], 0)))
        return jnp.bitwise_xor(chip_ref[0], mask)

    held = lambda jc, i: jnp.where(jc == 0, i, ni - 1)
    return pl.pallas_call(
        body,
        name="rms_inproj_gather",
        grid_spec=pltpu.PrefetchScalarGridSpec(
            num_scalar_prefetch=1,
            grid=(4, ni),
            in_specs=[pl.BlockSpec((tm, D_MODEL), lambda jc, i, ch: (held(jc, i), 0)),
                      pl.BlockSpec((1, D_MODEL), lambda jc, i, ch: (0, 0)), _ANY],
            out_specs=[pl.BlockSpec((tm, tn), lambda jc, i, ch: (i, tile_of(jc, ch))), _ANY, _ANY],
            scratch_shapes=[pltpu.VMEM((L, D_MODEL), BF16), pltpu.VMEM((2, tn, D_MODEL), BF16),
                            pltpu.SemaphoreType.DMA((7,)), pltpu.SemaphoreType.DMA((7,)), pltpu.SemaphoreType.DMA((8,))],
        ),
        out_shape=[jax.ShapeDtypeStruct((L, IN_W), F32), jax.ShapeDtypeStruct((L, D_MODEL), BF16),
                   jax.ShapeDtypeStruct((N_DEV, r, D_MODEL), BF16)],
        compiler_params=_cp(("arbitrary", "arbitrary")),
    )(chip, x, norm_w.reshape(1, D_MODEL), wt_shard)


def _seg_sum(v):
    a = lax.broadcasted_iota(jnp.int32, (128, 128), 0) // HEAD_DIM
    b = lax.broadcasted_iota(jnp.int32, (128, 128), 1) // HEAD_DIM
    ones = jnp.where(a == b, 1.0, 0.0).astype(BF16)
    hi = v.astype(BF16)
    lo = (v - hi.astype(F32)).astype(BF16)
    return jnp.dot(hi, ones, preferred_element_type=F32) + jnp.dot(lo, ones, preferred_element_type=F32)


def _rot_half(t):
    lane = lax.broadcasted_iota(jnp.int32, t.shape, 1)
    return jnp.where(lane % HEAD_DIM < HEAD_DIM // 2, pltpu.roll(t, 128 - HEAD_DIM // 2, 1),
                     pltpu.roll(t, HEAD_DIM // 2, 1))


def _norm_rope(raw, w, cos, sin):
    r = lax.rsqrt(_seg_sum(raw * raw) * (1.0 / HEAD_DIM) + NORM_EPS)
    tn = raw * r * w
    return r, tn * cos + _rot_half(tn) * sin


def _norm_rope_bwd(d_rot, raw, w, cos, sin):
    r = lax.rsqrt(_seg_sum(raw * raw) * (1.0 / HEAD_DIM) + NORM_EPS)
    d_tn = d_rot * cos + _rot_half(d_rot * sin)
    xh = raw * r
    gw = d_tn * w
    d_raw = r * (gw - xh * (_seg_sum(gw * xh) * (1.0 / HEAD_DIM)))
    return d_raw, d_tn * xh


def _band_mask2(has_prev, keys_on_rows=False):
    qd, kd = (1, 0) if keys_on_rows else (0, 1)
    qi = lax.broadcasted_iota(jnp.int32, (2 * BLOCK, 2 * BLOCK), qd) % BLOCK + BLOCK
    kj = lax.broadcasted_iota(jnp.int32, (2 * BLOCK, 2 * BLOCK), kd)
    rel = qi - kj
    return (rel >= 0) & (rel < BLOCK) & ((kj >= BLOCK) | has_prev)


def _half_tiles(pair):
    lo = lax.broadcasted_iota(jnp.int32, pair.shape, 1) < HEAD_DIM
    sw = pltpu.roll(pair, HEAD_DIM, 1)
    z = jnp.zeros_like(pair)
    return (jnp.where(lo, pair, z).astype(BF16), jnp.where(lo, z, sw).astype(BF16),
            jnp.where(lo, sw, z).astype(BF16), jnp.where(lo, z, pair).astype(BF16))


def _two_rows(top, bottom):
    row = lax.broadcasted_iota(jnp.int32, (2 * BLOCK, 1), 0)
    return jnp.where(row < BLOCK, top, bottom)


def _lane_col(mat, h):
    lane = lax.broadcasted_iota(jnp.int32, mat.shape, 1)
    return jnp.sum(jnp.where(lane == h, mat, 0.0), axis=1, keepdims=True)


_SCALE = 1.0 / math.sqrt(HEAD_DIM)
_NT = (((1,), (1,)), ((), ()))
_NN = (((1,), (0,)), ((), ()))
_TN = (((0,), (0,)), ((), ()))


def _qk_prep(proj, tab, qw, kw):
    L = proj.shape[0]
    tm = _tile(L, 512)

    def body(q_ref, k_ref, t_ref, qw_ref, kw_ref, qo_ref, ko_ref):
        cos, sin = t_ref[:, :128], t_ref[:, 128:]
        for c in range(ATTN_W // 128):
            _, qr = _norm_rope(q_ref[:, c * 128:(c + 1) * 128], qw_ref[...], cos, sin)
            qo_ref[:, c * 128:(c + 1) * 128] = (qr * _SCALE).astype(BF16)
        for c in range(KV_W // 128):
            _, kr = _norm_rope(k_ref[:, c * 128:(c + 1) * 128], kw_ref[...], cos, sin)
            ko_ref[:, c * 128:(c + 1) * 128] = kr.astype(BF16)

    row = pl.BlockSpec((1, 128), lambda i: (0, 0))
    return pl.pallas_call(
        body,
        name="qk_prep",
        grid=(L // tm,),
        in_specs=[pl.BlockSpec((tm, ATTN_W), lambda i: (i, 0)), pl.BlockSpec((tm, KV_W), lambda i: (i, 4)),
                  pl.BlockSpec((tm, 256), lambda i: (i, 0)), row, row],
        out_specs=[pl.BlockSpec((tm, ATTN_W), lambda i: (i, 0)), pl.BlockSpec((tm, KV_W), lambda i: (i, 0))],
        out_shape=[jax.ShapeDtypeStruct((L, ATTN_W), BF16), jax.ShapeDtypeStruct((L, KV_W), BF16)],
        compiler_params=_cp(("parallel",)),
    )(proj, proj, tab, jnp.tile(qw, 2).reshape(1, 128), jnp.tile(kw, 2).reshape(1, 128))


def _group_tiles(g, kt, vt):
    a, b = divmod(g, 2)
    return kt[a][2 * b], kt[a][2 * b + 1], vt[a][2 * b], vt[a][2 * b + 1]


def _attn_fwd(q, k, proj, sinks, ride):
    L = proj.shape[0]
    nb = L // BLOCK

    def body(q_ref, kc_ref, kp_ref, vc_ref, vp_ref, z0_ref, z1_ref, sink_ref, og_ref, o_ref, lse_ref):
        i = pl.program_id(0)
        mask = _band_mask2(i > 0)
        z = jnp.concatenate([z0_ref[...], z1_ref[...]], axis=1)
        lane = lax.broadcasted_iota(jnp.int32, (BLOCK, 128), 1)
        kt = [_half_tiles(jnp.concatenate([kp_ref[:, a * 128:(a + 1) * 128], kc_ref[:, a * 128:(a + 1) * 128]],
                                          axis=0).astype(F32)) for a in range(2)]
        vt = [_half_tiles(jnp.concatenate([vp_ref[:, a * 128:(a + 1) * 128], vc_ref[:, a * 128:(a + 1) * 128]],
                                          axis=0)) for a in range(2)]
        lse_mat = jnp.zeros((BLOCK, 128), F32)
        pairs = []
        for g in range(N_KV):
            k_lo, k_hi, v_lo, v_hi = _group_tiles(g, kt, vt)
            q2 = jnp.concatenate([q_ref[:, 2 * g * 128:(2 * g + 1) * 128],
                                  q_ref[:, (2 * g + 1) * 128:(2 * g + 2) * 128]], axis=0)
            for half, (kh, vh) in enumerate(((k_lo, v_lo), (k_hi, v_hi))):
                pairs.append(dict(g=g, half=half, vh=vh, s=lax.dot_general(q2, kh, _NT, preferred_element_type=F32)))
        for pr in pairs:
            h_top, h_bot = 4 * pr["g"] + pr["half"], 4 * pr["g"] + 2 + pr["half"]
            s = jnp.where(mask, pr["s"], -1e30)
            sink = _two_rows(sink_ref[h_top], sink_ref[h_bot])
            m = jnp.maximum(jnp.max(s, axis=-1, keepdims=True), sink)
            e = jnp.exp(s - m)
            den = jnp.sum(e, axis=-1, keepdims=True) + jnp.exp(sink - m)
            pr["p_b"] = (e * (1.0 / den)).astype(BF16)
            lse = m + jnp.log(den)
            lse_mat = jnp.where(lane == h_top, lse[:BLOCK], lse_mat)
            lse_mat = jnp.where(lane == h_bot, lse[BLOCK:], lse_mat)
        outs = []
        for g in range(N_KV):
            acc = (jnp.dot(pairs[2 * g]["p_b"], pairs[2 * g]["vh"], preferred_element_type=F32)
                   + jnp.dot(pairs[2 * g + 1]["p_b"], pairs[2 * g + 1]["vh"], preferred_element_type=F32))
            outs += [acc[:BLOCK], acc[BLOCK:]]
        o = jnp.concatenate(outs, axis=1)
        o_ref[...] = o.astype(BF16)
        og_ref[...] = (o * _silu(z)).astype(BF16)
        lse_ref[...] = lse_mat

    prev = lambda i: jnp.maximum(i - 1, 0)
    return _call(
        body, "attn_fwd", (nb,),
        [pl.BlockSpec((BLOCK, ATTN_W), lambda i: (i, 0)),
         pl.BlockSpec((BLOCK, KV_W), lambda i: (i, 0)),
         pl.BlockSpec((BLOCK, KV_W), lambda i: (prev(i), 0)),
         pl.BlockSpec((BLOCK, KV_W), lambda i: (i, 5)),
         pl.BlockSpec((BLOCK, KV_W), lambda i: (prev(i), 5)),
         pl.BlockSpec((BLOCK, 512), lambda i: (i, 3)),
         pl.BlockSpec((BLOCK, 512), lambda i: (i, 4)),
         pl.BlockSpec(memory_space=pltpu.SMEM)],
        [pl.BlockSpec((BLOCK, ATTN_W), lambda i: (i, 0)),
         pl.BlockSpec((BLOCK, ATTN_W), lambda i: (i, 0)),
         pl.BlockSpec((BLOCK, 128), lambda i: (i, 0))],
        [jax.ShapeDtypeStruct((L, ATTN_W), BF16), jax.ShapeDtypeStruct((L, ATTN_W), BF16),
         jax.ShapeDtypeStruct((L, 128), F32)],
        (q, k, k, proj, proj, proj, proj, sinks), ride=ride)


def _attn_bwd(q, k, proj, sinks, d_o, o, lse, ride):
    L = proj.shape[0]
    nb = L // BLOCK

    def body(q_ref, kc_ref, kp_ref, vc_ref, vp_ref, do_ref, o_ref, lse_ref, sink_ref,
             dq_ref, dk_ref, dv_ref, gs_ref, ck_scr, cv_scr):
        i = pl.program_id(0)

        @pl.when(i == 0)
        def _():
            gs_ref[...] = jnp.zeros_like(gs_ref)
            ck_scr[...] = jnp.zeros_like(ck_scr)
            cv_scr[...] = jnp.zeros_like(cv_scr)

        @pl.when(i == nb)
        def _():
            dk_ref[...] = ck_scr[...]
            dv_ref[...] = cv_scr[...]

        @pl.when(i < nb)
        def _():
            mask = _band_mask2(i > 0, keys_on_rows=True)
            lane = lax.broadcasted_iota(jnp.int32, (1, 128), 1)
            lane2 = lax.broadcasted_iota(jnp.int32, (1, 2 * BLOCK), 1)
            lo = lax.broadcasted_iota(jnp.int32, (2 * BLOCK, 128), 1) < HEAD_DIM
            lse_t = lse_ref[...].T
            prod_all = do_ref[...].astype(F32) * o_ref[...].astype(F32)
            seg = (lax.broadcasted_iota(jnp.int32, (N_HEADS, ATTN_W), 1) // HEAD_DIM
                   == lax.broadcasted_iota(jnp.int32, (N_HEADS, ATTN_W), 0)).astype(BF16)
            prod_hi = prod_all.astype(BF16)
            prod_lo = (prod_all - prod_hi.astype(F32)).astype(BF16)
            delta_t = (lax.dot_general(seg, prod_hi, _NT, preferred_element_type=F32)
                       + lax.dot_general(seg, prod_lo, _NT, preferred_element_type=F32))
            kt = [_half_tiles(jnp.concatenate([kp_ref[:, a * 128:(a + 1) * 128], kc_ref[:, a * 128:(a + 1) * 128]],
                                              axis=0).astype(F32)) for a in range(2)]
            vt = [_half_tiles(jnp.concatenate([vp_ref[:, a * 128:(a + 1) * 128], vc_ref[:, a * 128:(a + 1) * 128]],
                                              axis=0)) for a in range(2)]
            gs = jnp.zeros((1, 128), F32)
            dq_parts = []
            dk_acc = [jnp.zeros((2 * BLOCK, 128), F32) for _ in range(2)]
            dv_acc = [jnp.zeros((2 * BLOCK, 128), F32) for _ in range(2)]
            pairs = []
            for g in range(N_KV):
                k_lo, k_hi, v_lo, v_hi = _group_tiles(g, kt, vt)
                t0, t1 = slice(2 * g * 128, (2 * g + 1) * 128), slice((2 * g + 1) * 128, (2 * g + 2) * 128)
                q2 = jnp.concatenate([q_ref[:, t0], q_ref[:, t1]], axis=0)
                do2_b = jnp.concatenate([do_ref[:, t0], do_ref[:, t1]], axis=0).astype(BF16)
                for half, (kh, vh) in enumerate(((k_lo, v_lo), (k_hi, v_hi))):
                    pairs.append(dict(g=g, half=half, kh=kh, q2=q2, do2_b=do2_b,
                                      s=lax.dot_general(kh, q2, _NT, preferred_element_type=F32),
                                      dp=lax.dot_general(vh, do2_b, _NT, preferred_element_type=F32)))
            for pr in pairs:
                h_top, h_bot = 4 * pr["g"] + pr["half"], 4 * pr["g"] + 2 + pr["half"]
                pick = lambda t: jnp.concatenate([t[h_top:h_top + 1, :], t[h_bot:h_bot + 1, :]], axis=1)
                lse, delta = pick(lse_t), pick(delta_t)
                sink = jnp.where(lane2 < BLOCK, sink_ref[h_top], sink_ref[h_bot])
                p = jnp.exp(jnp.where(mask, pr["s"], -1e30) - lse)
                pr["ds_b"] = (p * (pr["dp"] - delta)).astype(BF16)
                pr["p_b"] = p.astype(BF16)
                gsink = -jnp.exp(sink - lse) * delta
                gs = gs + jnp.where(lane == h_top, jnp.sum(jnp.where(lane2 < BLOCK, gsink, 0.0)), 0.0)
                gs = gs + jnp.where(lane == h_bot, jnp.sum(jnp.where(lane2 >= BLOCK, gsink, 0.0)), 0.0)
            for g in range(N_KV):
                a, b = divmod(g, 2)
                dq2 = jnp.zeros((2 * BLOCK, 128), F32)
                dk_h, dv_h = [], []
                for pr in pairs[2 * g:2 * g + 2]:
                    dq2 = dq2 + lax.dot_general(pr["ds_b"], pr["kh"], _TN, preferred_element_type=F32)
                    dk_h.append(jnp.dot(pr["ds_b"], pr["q2"], preferred_element_type=F32))
                    dv_h.append(jnp.dot(pr["p_b"], pr["do2_b"], preferred_element_type=F32))
                dq_parts += [dq2[:BLOCK], dq2[BLOCK:]]
                for acc, parts in ((dk_acc, dk_h), (dv_acc, dv_h)):
                    t = jnp.where(lo, parts[0], parts[1])
                    t = t + pltpu.roll(t, HEAD_DIM, 1)
                    acc[a] = acc[a] + jnp.where(lo == (b == 0), t, 0.0)
            dq_ref[...] = jnp.concatenate(dq_parts, axis=1)
            dk_full = jnp.concatenate(dk_acc, axis=1)
            dv_full = jnp.concatenate(dv_acc, axis=1)
            dk_ref[...] = ck_scr[...] + dk_full[:BLOCK]
            dv_ref[...] = cv_scr[...] + dv_full[:BLOCK]
            ck_scr[...] = dk_full[BLOCK:]
            cv_scr[...] = dv_full[BLOCK:]
            gs_ref[...] += gs

    cur = lambda i: jnp.minimum(i, nb - 1)
    prev = lambda i: jnp.maximum(jnp.minimum(i, nb - 1) - 1, 0)
    done = lambda i: jnp.maximum(i - 1, 0)
    bs = pl.BlockSpec
    return _call(
        body, "attn_bwd", (nb + 1,),
        [bs((BLOCK, ATTN_W), lambda i: (cur(i), 0)),
         bs((BLOCK, KV_W), lambda i: (cur(i), 0)), bs((BLOCK, KV_W), lambda i: (prev(i), 0)),
         bs((BLOCK, KV_W), lambda i: (cur(i), 5)), bs((BLOCK, KV_W), lambda i: (prev(i), 5)),
         bs((BLOCK, ATTN_W), lambda i: (cur(i), 0)), bs((BLOCK, ATTN_W), lambda i: (cur(i), 0)),
         bs((BLOCK, 128), lambda i: (cur(i), 0)), bs(memory_space=pltpu.SMEM)],
        [bs((BLOCK, ATTN_W), lambda i: (cur(i), 0)),
         bs((BLOCK, KV_W), lambda i: (done(i), 0)), bs((BLOCK, KV_W), lambda i: (done(i), 0)),
         bs((1, 128), lambda i: (0, 0))],
        [jax.ShapeDtypeStruct((L, ATTN_W), F32), jax.ShapeDtypeStruct((L, KV_W), F32),
         jax.ShapeDtypeStruct((L, KV_W), F32), jax.ShapeDtypeStruct((1, 128), F32)],
        (q, k, k, proj, proj, d_o, o, lse, sinks),
        [pltpu.VMEM((BLOCK, KV_W), F32), pltpu.VMEM((BLOCK, KV_W), F32)], ride)


def _qk_prep_bwd(proj, tab, qw, kw, d_q, d_k, d_v, d_za, d_u, d_zs):
    L = proj.shape[0]
    tm = _tile(L, 512)
    z0 = ATTN_W + 2 * KV_W

    def body(q_ref, k_ref, t_ref, qw_ref, kw_ref, dq_ref, dk_ref, dv_ref, dza_ref, du_ref, dzs_ref,
             out_ref, gq_ref, gk_ref):
        i = pl.program_id(0)

        @pl.when(i == 0)
        def _():
            gq_ref[...] = jnp.zeros_like(gq_ref)
            gk_ref[...] = jnp.zeros_like(gk_ref)

        cos, sin = t_ref[:, :128], t_ref[:, 128:]
        gq = jnp.zeros((1, 128), F32)
        gk = jnp.zeros((1, 128), F32)
        for c in range(ATTN_W // 128):
            cs = slice(c * 128, (c + 1) * 128)
            d_raw, gw = _norm_rope_bwd(dq_ref[:, cs] * _SCALE, q_ref[:, cs], qw_ref[...], cos, sin)
            out_ref[:, cs] = d_raw.astype(BF16)
            gq = gq + jnp.sum(gw, axis=0, keepdims=True)
        for c in range(KV_W // 128):
            cs = slice(c * 128, (c + 1) * 128)
            d_raw, gw = _norm_rope_bwd(dk_ref[:, cs], k_ref[:, cs], kw_ref[...], cos, sin)
            out_ref[:, ATTN_W + c * 128:ATTN_W + (c + 1) * 128] = d_raw.astype(BF16)
            gk = gk + jnp.sum(gw, axis=0, keepdims=True)
        out_ref[:, ATTN_W + KV_W:z0] = dv_ref[...].astype(BF16)
        out_ref[:, z0:z0 + ATTN_W] = dza_ref[...]
        out_ref[:, z0 + ATTN_W:z0 + ATTN_W + SSM_W] = du_ref[...].astype(BF16)
        out_ref[:, z0 + ATTN_W + SSM_W:] = dzs_ref[...]
        gq_ref[...] += gq
        gk_ref[...] += gk

    row = pl.BlockSpec((1, 128), lambda i: (0, 0))
    blk = lambda w, c: pl.BlockSpec((tm, w), lambda i: (i, c))
    return pl.pallas_call(
        body,
        name="qk_prep_bwd",
        grid=(L // tm,),
        in_specs=[blk(ATTN_W, 0), blk(KV_W, 4), blk(256, 0), row, row, blk(ATTN_W, 0), blk(KV_W, 0), blk(KV_W, 0),
                  blk(ATTN_W, 0), blk(SSM_W, 0), blk(SSM_W, 0)],
        out_specs=[blk(IN_W, 0), row, row],
        out_shape=[jax.ShapeDtypeStruct((L, IN_W), BF16), jax.ShapeDtypeStruct((1, 128), F32),
                   jax.ShapeDtypeStruct((1, 128), F32)],
        compiler_params=_cp(("arbitrary",)),
    )(proj, proj, tab, jnp.tile(qw, 2).reshape(1, 128), jnp.tile(kw, 2).reshape(1, 128), d_q, d_k, d_v,
      d_za, d_u, d_zs)


def _cmul(a, b):
    return a[0] * b[0] - a[1] * b[1], a[0] * b[1] + a[1] * b[0]


def _cmul_conj(a, b):
    return a[0] * b[0] + a[1] * b[1], a[1] * b[0] - a[0] * b[1]


def _cadd(a, b):
    return a[0] + b[0], a[1] + b[1]


def _dot3(a, b, dn):
    ah, bh = a.astype(BF16), b.astype(BF16)
    al, bl = (a - ah.astype(F32)).astype(BF16), (b - bh.astype(F32)).astype(BF16)
    d = lambda u, v: lax.dot_general(u, v, dn, preferred_element_type=F32)
    return d(ah, bh) + d(ah, bl) + d(al, bh)


def _s5_discretise(a_re, a_im, ls, cosx, sinx, bt):
    delta = jnp.exp(ls)
    er = jnp.exp(a_re * delta)
    lb = (er * cosx, er * sinx)
    den = a_re * a_re + a_im * a_im
    coef = _cmul_conj((lb[0] - 1.0, lb[1]), (a_re, a_im))
    coef = (coef[0] / den, coef[1] / den)
    return delta, lb, coef, den, _cmul(coef, bt)


def _powers(lb):
    pw = [(jnp.ones_like(lb[0]), jnp.zeros_like(lb[0]))]
    for _ in range(CHUNK):
        pw.append(_cmul(pw[-1], lb))
    return pw


def _block_rows(a, pw, idx):
    blocks = [_cmul(a, pw[i]) for i in idx]
    return (jnp.concatenate([b[0] for b in blocks], axis=-2), jnp.concatenate([b[1] for b in blocks], axis=-2))


def _block_rows_bwd(g, a, pw, idx, g_pw):
    g_a = (jnp.zeros_like(a[0]), jnp.zeros_like(a[0]))
    for j, i in enumerate(idx):
        gj = (g[0][..., j * SSM_H:(j + 1) * SSM_H, :], g[1][..., j * SSM_H:(j + 1) * SSM_H, :])
        g_a = _cadd(g_a, _cmul_conj(gj, pw[i]))
        gp = _cmul_conj(gj, a)
        g_pw[i] = _cadd(g_pw[i], (jnp.sum(gp[0], axis=-2, keepdims=True), jnp.sum(gp[1], axis=-2, keepdims=True)))
    return g_a


_IDX_S = [CHUNK - 1 - s for s in range(CHUNK)]
_IDX_C = list(range(CHUNK + 1))


def _prep_args(p):
    row = lambda t: t.reshape(SSM_G, 1, SSM_P)
    xi = p["a_im"] * jnp.exp(p["log_step"])[:, None]
    return (row(p["a_re"]), row(p["a_im"]), row(jnp.broadcast_to(p["log_step"][:, None], (SSM_G, SSM_P))),
            row(jnp.cos(xi)), row(jnp.sin(xi)), p["b_re"].transpose(0, 2, 1), p["b_im"].transpose(0, 2, 1),
            p["c_re"], p["c_im"])


PREP_GROUPS = 8


def _prep_specs():
    r1 = pl.BlockSpec((PREP_GROUPS, 1, SSM_P), lambda g: (g, 0, 0))
    r16 = pl.BlockSpec((PREP_GROUPS, SSM_H, SSM_P), lambda g: (g, 0, 0))
    return [r1] * 5 + [r16] * 4, r1, r16


def _ssm_prep(p):
    def one_group(q, are, aim, ls, cosx, sinx, btr, bti, cre, cim, mt_ref, s_ref, o_ref, a_ref):
        _, lb, _, _, bb = _s5_discretise(are[q], aim[q], ls[q], cosx[q], sinx[q], (btr[q], bti[q]))
        pw = _powers(lb)
        c = (cre[q], cim[q])
        sc = _block_rows(bb, pw, _IDX_S)
        cl = _block_rows(c, pw, _IDX_C)
        ok = (cl[0][:CW], cl[1][:CW])
        ot = (cl[0][SSM_H:], cl[1][SSM_H:])
        s_ref[q] = jnp.concatenate([sc[0], sc[1]], axis=1).astype(BF16)
        o_ref[q] = jnp.concatenate([ot[0], -ot[1]], axis=1).astype(BF16)
        a_ref[q] = jnp.concatenate([pw[CHUNK][0], pw[CHUNK][1]], axis=1)
        kt = _dot3(jnp.concatenate([bb[0], -bb[1]], axis=1), jnp.concatenate([ok[0], ok[1]], axis=1), _NT)
        lane = lax.broadcasted_iota(jnp.int32, kt.shape, 1)
        for s in range(CHUNK):
            blk = kt if s == 0 else jnp.where(lane >= SSM_H * s, pltpu.roll(kt, SSM_H * s, 1), 0.0)
            mt_ref[q, s * SSM_H:(s + 1) * SSM_H, :] = blk.astype(BF16)

    def body(*refs):
        for q in range(PREP_GROUPS):
            one_group(q, *refs)

    in_specs, r1, _ = _prep_specs()
    g3 = lambda r, c: pl.BlockSpec((PREP_GROUPS, r, c), lambda g: (g, 0, 0))
    return pl.pallas_call(
        body,
        name="ssm_prep",
        grid=(SSM_G // PREP_GROUPS,),
        in_specs=in_specs,
        out_specs=[g3(CW, CW), g3(CW, 2 * SSM_P), g3(CW, 2 * SSM_P), g3(1, 2 * SSM_P)],
        out_shape=[jax.ShapeDtypeStruct((SSM_G, CW, CW), BF16), jax.ShapeDtypeStruct((SSM_G, CW, 2 * SSM_P), BF16),
                   jax.ShapeDtypeStruct((SSM_G, CW, 2 * SSM_P), BF16),
                   jax.ShapeDtypeStruct((SSM_G, 1, 2 * SSM_P), F32)],
        compiler_params=_cp(("parallel",)),
    )(*_prep_args(p))


def _ssm_prep_bwd(p, g_mt, g_scat, g_ocat, g_a16, ride):
    def body(are, aim, ls, cosx, sinx, btr, bti, cre, cim, gmt_ref, gs_ref, go_ref, ga_ref,
             g_are, g_aim, g_ls, g_btr, g_bti, g_cre, g_cim, ga1_scr, gb1_scr):
        lam = (are[...], aim[...])
        bt = (btr[...], bti[...])
        delta, lb, coef, den, bb = _s5_discretise(lam[0], lam[1], ls[...], cosx[...], sinx[...], bt)
        pw = _powers(lb)
        c = (cre[...], cim[...])
        ok = _block_rows(c, pw, _IDX_C[:CHUNK])
        g_pw =[(jnp.zeros_like(lb[0]), jnp.zeros_like(lb[0])) for _ in range(CHUNK + 1)]
        lane = lax.broadcasted_iota(jnp.int32, (SSM_H, CW), 1)
        for q in range(PREP_GROUPS):
            g_kt = gmt_ref[q, :SSM_H, :]
            for s in range(1, CHUNK):
                blk = gmt_ref[q, s * SSM_H:(s + 1) * SSM_H, :]
                g_kt = g_kt + jnp.where(lane < CW - SSM_H * s, pltpu.roll(blk, CW - SSM_H * s, 1), 0.0)
            a1 = jnp.concatenate([bb[0][q], -bb[1][q]], axis=1)
            b1 = jnp.concatenate([ok[0][q], ok[1][q]], axis=1)
            ga1_scr[q] = _dot3(g_kt, b1, _NN)
            gb1_scr[q] = _dot3(g_kt, a1, _TN)
        g_a1, g_b1 = ga1_scr[...], gb1_scr[...]
        g_bb = (g_a1[..., :SSM_P], -g_a1[..., SSM_P:])
        gs = gs_ref[...]
        g_bb = _cadd(g_bb, _block_rows_bwd((gs[..., :SSM_P], gs[..., SSM_P:]), bb, pw, _IDX_S, g_pw))
        go = go_ref[...]
        pad = jnp.zeros_like(go[..., :SSM_H, :SSM_P])
        g_cl = (jnp.concatenate([g_b1[..., :SSM_P], pad], axis=-2) + jnp.concatenate([pad, go[..., :SSM_P]], axis=-2),
                jnp.concatenate([g_b1[..., SSM_P:], pad], axis=-2) - jnp.concatenate([pad, go[..., SSM_P:]], axis=-2))
        g_c = _block_rows_bwd(g_cl, c, pw, _IDX_C, g_pw)
        ga = ga_ref[...]
        g_pw[CHUNK] = _cadd(g_pw[CHUNK], (ga[..., :SSM_P], ga[..., SSM_P:]))
        g_lb = (jnp.zeros_like(lb[0]), jnp.zeros_like(lb[0]))
        for l in range(CHUNK - 1, -1, -1):
            g_lb = _cadd(g_lb, _cmul_conj(g_pw[l + 1], pw[l]))
            g_pw[l] = _cadd(g_pw[l], _cmul_conj(g_pw[l + 1], lb))
        g_bt = _cmul_conj(g_bb, coef)
        gc = _cmul_conj(g_bb, bt)
        g_coef = (jnp.sum(gc[0], axis=-2, keepdims=True), jnp.sum(gc[1], axis=-2, keepdims=True))
        lam_den = (lam[0] / den, lam[1] / den)
        g_lb = _cadd(g_lb, _cmul(g_coef, lam_den))
        t = _cmul(_cmul_conj(g_coef, coef), lam_den)
        g_x = _cmul_conj(g_lb, lb)
        g_are[...] = g_x[0] * delta - t[0]
        g_aim[...] = g_x[1] * delta - t[1]
        g_ls[...] = (g_x[0] * lam[0] + g_x[1] * lam[1]) * delta
        g_btr[...] = g_bt[0]
        g_bti[...] = g_bt[1]
        g_cre[...] = g_c[0]
        g_cim[...] = g_c[1]

    in_specs, r1, r16 = _prep_specs()
    g3 = lambda r, c: pl.BlockSpec((PREP_GROUPS, r, c), lambda g: (g, 0, 0))
    rows = jax.ShapeDtypeStruct((SSM_G, 1, SSM_P), F32)
    mats = jax.ShapeDtypeStruct((SSM_G, SSM_H, SSM_P), F32)
    (g_are, g_aim, g_ls, g_btr, g_bti, g_cre, g_cim), landed = _call(
        body, "ssm_prep_bwd", (SSM_G // PREP_GROUPS,),
        in_specs + [g3(CW, CW), g3(CW, 2 * SSM_P), g3(CW, 2 * SSM_P), g3(1, 2 * SSM_P)],
        [r1] * 3 + [r16] * 4, [rows] * 3 + [mats] * 4, (*_prep_args(p), g_mt, g_scat, g_ocat, g_a16),
        [pltpu.VMEM((PREP_GROUPS, SSM_H, 2 * SSM_P), F32), pltpu.VMEM((PREP_GROUPS, CW, 2 * SSM_P), F32)], ride)
    grads = dict(a_re=g_are.reshape(SSM_G, SSM_P), a_im=g_aim.reshape(SSM_G, SSM_P),
                 log_step=jnp.sum(g_ls.reshape(SSM_G, SSM_P), axis=1),
                 b_re=g_btr.transpose(0, 2, 1), b_im=g_bti.transpose(0, 2, 1), c_re=g_cre, c_im=g_cim)
    return grads, landed


def _cmul_const(xv, ar, ai):
    return xv * ar + pltpu.roll(xv, SSM_P, 1) * ai


def _chunk_scan(inc, a_row, reverse):
    n = inc.shape[0]
    lane = lax.broadcasted_iota(jnp.int32, (1, 2 * SSM_P), 1)
    row = lax.broadcasted_iota(jnp.int32, inc.shape, 0)
    sign = jnp.where(lane < SSM_P, -1.0, 1.0)
    ar = jnp.where(lane < SSM_P, a_row, pltpu.roll(a_row, SSM_P, 1))
    ai = jnp.where(lane < SSM_P, pltpu.roll(a_row, SSM_P, 1), a_row)
    if reverse:
        ai = -ai
    xv = inc
    s = 1
    while s < n:
        if reverse:
            sh = jnp.where(row < n - s, pltpu.roll(xv, n - s, 0), 0.0)
        else:
            sh = jnp.where(row >= s, pltpu.roll(xv, s, 0), 0.0)
        xv = xv + _cmul_const(sh, ar, ai * sign)
        ar, ai = ar * ar - ai * ai, 2.0 * ar * ai
        s *= 2
    return xv


def _shift_rows(xv, reverse):
    n = xv.shape[0]
    row = lax.broadcasted_iota(jnp.int32, xv.shape, 0)
    if reverse:
        return jnp.where(row < n - 1, pltpu.roll(xv, n - 1, 0), 0.0)
    return jnp.where(row >= 1, pltpu.roll(xv, 1, 0), 0.0)


GB = 128 // SSM_H
U_COL0 = (ATTN_W + 2 * KV_W + ATTN_W) // 128


HALF = CHUNK // 2


def _chunk_perm():
    r = jnp.arange(HALF * 128)
    t, g8, h = r // 128, (r % 128) // SSM_H, r % SSM_H
    return ((g8 * 128 + t * SSM_H + h)[:, None] == jnp.arange(GB * 128)[None, :]).astype(BF16)


def _load_perm(p_hbm, p_scr, sem):
    @pl.when(pl.program_id(0) == 0)
    def _():
        cp = pltpu.make_async_copy(p_hbm, p_scr, sem)
        cp.start()
        cp.wait()


def _rows_to_chunks(pieces, perm):
    halves = [jnp.dot(jnp.concatenate(pieces[k * HALF:(k + 1) * HALF], axis=1).astype(BF16), perm,
                      preferred_element_type=F32).astype(BF16) for k in range(2)]
    return [jnp.concatenate([hv[:, g * 128:(g + 1) * 128] for hv in halves], axis=1) for g in range(GB)]


def _chunks_to_rows(groups, perm, two_pass):
    pieces = []
    for k in range(2):
        v = jnp.concatenate([gv[:, k * 128:(k + 1) * 128] for gv in groups], axis=1)
        hi = v.astype(BF16)
        out = lax.dot_general(hi, perm, _NT, preferred_element_type=F32)
        if two_pass:
            lo = (v - hi.astype(F32)).astype(BF16)
            out = out + lax.dot_general(lo, perm, _NT, preferred_element_type=F32)
        pieces += [out[:, t * 128:(t + 1) * 128] for t in range(HALF)]
    return pieces


def _ssm_fwd(proj, perm, mt, scat, ocat, a16, d_skip, ride):
    L = proj.shape[0]
    nc = L // CHUNK

    def body(u_ref, p_hbm, mt_ref, s_ref, o_ref, a_ref, d_ref, y_ref, yg_ref, h_ref, p_scr, sem):
        _load_perm(p_hbm, p_scr, sem)
        perm = p_scr[...]
        rows = [pl.ds(t, nc, stride=CHUNK) for t in range(CHUNK)]
        us = [u_ref[r, :] for r in rows]
        ua = _rows_to_chunks(us, perm)
        incs = [jnp.dot(ua[g], s_ref[g], preferred_element_type=F32) for g in range(GB)]
        intra = [jnp.dot(ua[g], mt_ref[g], preferred_element_type=F32) for g in range(GB)]
        hxs = [_shift_rows(_chunk_scan(incs[g], a_ref[g], False), False) for g in range(GB)]
        ys = []
        for g in range(GB):
            h_ref[g] = hxs[g]
            ys.append(intra[g] + lax.dot_general(hxs[g].astype(BF16), o_ref[g], _NT, preferred_element_type=F32))
        yp = _chunks_to_rows(ys, perm, True)
        for t, r in enumerate(rows):
            y = yp[t] + d_ref[...] * us[t]
            y_ref[r, :] = y
            yg_ref[r, :] = _gelu(y)

    g3 = lambda r, c: pl.BlockSpec((GB, r, c), lambda g: (g, 0, 0))
    col = pl.BlockSpec((L, 128), lambda g: (0, g))
    return _call(
        body, "ssm_fwd", (SSM_G // GB,),
        [pl.BlockSpec((L, 128), lambda g: (0, U_COL0 + g)), _ANY,
         g3(CW, CW), g3(CW, 2 * SSM_P), g3(CW, 2 * SSM_P), g3(1, 2 * SSM_P),
         pl.BlockSpec((1, 128), lambda g: (0, g))],
        [col, col, g3(nc, 2 * SSM_P)],
        [jax.ShapeDtypeStruct((L, SSM_W), F32), jax.ShapeDtypeStruct((L, SSM_W), F32),
         jax.ShapeDtypeStruct((SSM_G, nc, 2 * SSM_P), F32)],
        (proj, perm, mt, scat, ocat, a16, d_skip.reshape(1, SSM_W)),
        [pltpu.VMEM((HALF * 128, GB * 128), BF16), pltpu.SemaphoreType.DMA], ride)


def _ssm_bwd(d_yg, y, proj, hx, perm, mt, scat, ocat, a16, d_skip, ride):
    L = proj.shape[0]
    nc = L // CHUNK

    def body(dg_ref, y_ref, u_ref, h_ref, p_hbm, mt_ref, s_ref, o_ref, a_ref, d_ref,
             du_ref, gmt_ref, gs_ref, go_ref, ga_ref, gd_ref, p_scr, sem):
        _load_perm(p_hbm, p_scr, sem)
        perm = p_scr[...]
        rows = [pl.ds(t, nc, stride=CHUNK) for t in range(CHUNK)]
        us = [u_ref[r, :] for r in rows]
        dys = [dg_ref[r, :] * _dgelu(y_ref[r, :]) for r in rows]
        gd = jnp.zeros((1, 128), F32)
        for uv, dy in zip(us, dys):
            gd = gd + jnp.sum(dy * uv, axis=0, keepdims=True)
        gd_ref[...] = gd
        ua = _rows_to_chunks(us, perm)
        dya = _rows_to_chunks(dys, perm)
        lane = lax.broadcasted_iota(jnp.int32, (1, 2 * SSM_P), 1)
        dhs = [jnp.dot(dya[g], o_ref[g], preferred_element_type=F32) for g in range(GB)]
        intra = [lax.dot_general(dya[g], mt_ref[g], _NT, preferred_element_type=F32) for g in range(GB)]
        for g in range(GB):
            gmt_ref[g] = lax.dot_general(ua[g], dya[g], _TN, preferred_element_type=F32)
            go_ref[g] = lax.dot_general(dya[g], h_ref[g].astype(BF16), _TN, preferred_element_type=F32)
        dincs = [_shift_rows(_chunk_scan(dhs[g], a_ref[g], True), True) for g in range(GB)]
        dus = []
        for g in range(GB):
            dinc, hx_v = dincs[g], h_ref[g]
            dinc_b = dinc.astype(BF16)
            dus.append(intra[g] + lax.dot_general(dinc_b, s_ref[g], _NT, preferred_element_type=F32))
            gs_ref[g] = lax.dot_general(ua[g], dinc_b, _TN, preferred_element_type=F32)
            p1 = dinc * hx_v
            p2 = pltpu.roll(dinc, SSM_P, 1) * hx_v
            t1 = jnp.sum(p1 + pltpu.roll(p1, SSM_P, 1), axis=0, keepdims=True)
            t2 = jnp.sum(p2 - pltpu.roll(p2, SSM_P, 1), axis=0, keepdims=True)
            ga_ref[g] = jnp.where(lane < SSM_P, t1, pltpu.roll(t2, SSM_P, 1))
        dup = _chunks_to_rows(dus, perm, False)
        for t, r in enumerate(rows):
            du_ref[r, :] = dup[t] + d_ref[...] * dys[t]

    g3 = lambda r, c: pl.BlockSpec((GB, r, c), lambda g: (g, 0, 0))
    col = pl.BlockSpec((L, 128), lambda g: (0, g))
    row = pl.BlockSpec((1, 128), lambda g: (0, g))
    return _call(
        body, "ssm_bwd", (SSM_G // GB,),
        [col, col, pl.BlockSpec((L, 128), lambda g: (0, U_COL0 + g)), g3(nc, 2 * SSM_P), _ANY,
         g3(CW, CW), g3(CW, 2 * SSM_P), g3(CW, 2 * SSM_P), g3(1, 2 * SSM_P), row],
        [col, g3(CW, CW), g3(CW, 2 * SSM_P), g3(CW, 2 * SSM_P), g3(1, 2 * SSM_P), row],
        [jax.ShapeDtypeStruct((L, SSM_W), F32), jax.ShapeDtypeStruct((SSM_G, CW, CW), F32),
         jax.ShapeDtypeStruct((SSM_G, CW, 2 * SSM_P), F32), jax.ShapeDtypeStruct((SSM_G, CW, 2 * SSM_P), F32),
         jax.ShapeDtypeStruct((SSM_G, 1, 2 * SSM_P), F32), jax.ShapeDtypeStruct((1, SSM_W), F32)],
        (d_yg, y, proj, hx, perm, mt, scat, ocat, a16, d_skip.reshape(1, SSM_W)),
        [pltpu.VMEM((HALF * 128, GB * 128), BF16), pltpu.SemaphoreType.DMA], ride)


def _merge(og, yg, w_glu, proj, b_glu, wa, ws):
    L = og.shape[0]
    tm = _tile(L, 256)

    def body(og_ref, yg_ref, wg_ref, z0_ref, z1_ref, b_ref, wa_ref, ws_ref, m_ref, gp_ref):
        zs = jnp.concatenate([z0_ref[...], z1_ref[...]], axis=1)
        ygv = yg_ref[...]
        gpre = jnp.dot(ygv.astype(BF16), wg_ref[...], preferred_element_type=F32)
        gp_ref[...] = gpre
        os_ = ygv * _sigmoid(gpre + b_ref[...]) * _silu(zs)
        ogv = og_ref[...].astype(F32)
        ra = lax.rsqrt(jnp.mean(ogv * ogv, axis=-1, keepdims=True) + NORM_EPS)
        rs = lax.rsqrt(jnp.mean(os_ * os_, axis=-1, keepdims=True) + NORM_EPS)
        m_ref[:, :ATTN_W] = (ogv * ra * wa_ref[...]).astype(BF16)
        m_ref[:, ATTN_W:] = (os_ * rs * ws_ref[...]).astype(BF16)

    row = lambda w: pl.BlockSpec((1, w), lambda i: (0, 0))
    return pl.pallas_call(
        body,
        name="merge",
        grid=(L // tm,),
        in_specs=[pl.BlockSpec((tm, ATTN_W), lambda i: (i, 0)), pl.BlockSpec((tm, SSM_W), lambda i: (i, 0)),
                  pl.BlockSpec((SSM_W, SSM_W), lambda i: (0, 0)),
                  pl.BlockSpec((tm, 512), lambda i: (i, 7)), pl.BlockSpec((tm, 512), lambda i: (i, 8)),
                  row(SSM_W), row(ATTN_W), row(SSM_W)],
        out_specs=[pl.BlockSpec((tm, D_MODEL), lambda i: (i, 0)), pl.BlockSpec((tm, SSM_W), lambda i: (i, 0))],
        out_shape=[jax.ShapeDtypeStruct((L, D_MODEL), BF16), jax.ShapeDtypeStruct((L, SSM_W), F32)],
        compiler_params=_cp(("parallel",)),
    )(og, yg, w_glu, proj, proj, b_glu.reshape(1, SSM_W), wa.reshape(1, ATTN_W), ws.reshape(1, SSM_W))


def _outproj_loss(merged, w_out, x, target):
    L = x.shape[0]
    tm, tn = _tile(L, 512), 1024
    ni, nj = L // tm, D_MODEL // tn

    def body(m_ref, w_ref, x_ref, t_ref, d_ref, db_ref, l_ref):
        out = x_ref[...] + jnp.dot(m_ref[...], w_ref[...], preferred_element_type=F32)
        diff = out - t_ref[...]
        d = diff * (1.0 / D_MODEL)
        d_ref[...] = d
        db_ref[...] = d.astype(BF16)
        l_ref[...] = jnp.full((1, 8, 128), jnp.sum(diff * diff), F32)

    return pl.pallas_call(
        body,
        name="outproj_loss",
        grid=(nj, ni),
        in_specs=[pl.BlockSpec((tm, D_MODEL), lambda j, i: (i, 0)),
                  pl.BlockSpec((D_MODEL, tn), lambda j, i: (0, j)),
                  pl.BlockSpec((tm, tn), lambda j, i: (i, j)),
                  pl.BlockSpec((tm, tn), lambda j, i: (i, j))],
        out_specs=[pl.BlockSpec((tm, tn), lambda j, i: (i, j)), pl.BlockSpec((tm, tn), lambda j, i: (i, j)),
                   pl.BlockSpec((1, 8, 128), lambda j, i: (i * nj + j, 0, 0))],
        out_shape=[jax.ShapeDtypeStruct((L, D_MODEL), F32), jax.ShapeDtypeStruct((L, D_MODEL), BF16),
                   jax.ShapeDtypeStruct((ni * nj, 8, 128), F32)],
        compiler_params=_cp(("parallel", "parallel")),
    )(merged, w_out, x, target)


def _merge_bwd(d_out_b, w_out, og, o, yg, gpre, proj, b_glu, wa, ws):
    L = og.shape[0]
    tm = _tile(L, 256)

    def body(dout_ref, wo_ref, og_ref, o_ref, yg_ref, gp_ref, za0_ref, za1_ref, zs0_ref, zs1_ref, b_ref, wa_ref,
             ws_ref, do_ref, dza_ref, dzs_ref, dg_ref, dyg_ref, gwa_ref, gws_ref, gb_ref):
        i = pl.program_id(0)

        @pl.when(i == 0)
        def _():
            gwa_ref[...] = jnp.zeros_like(gwa_ref)
            gws_ref[...] = jnp.zeros_like(gws_ref)
            gb_ref[...] = jnp.zeros_like(gb_ref)

        dm = lax.dot_general(dout_ref[...], wo_ref[...], _NT, preferred_element_type=F32)
        za = jnp.concatenate([za0_ref[...], za1_ref[...]], axis=1)
        zs = jnp.concatenate([zs0_ref[...], zs1_ref[...]], axis=1)
        ogv, dma = og_ref[...].astype(F32), dm[:, :ATTN_W]
        ra = lax.rsqrt(jnp.mean(ogv * ogv, axis=-1, keepdims=True) + NORM_EPS)
        xh = ogv * ra
        gwa_ref[...] += jnp.sum(dma * xh, axis=0, keepdims=True)
        gx = dma * wa_ref[...]
        d_og = ra * (gx - xh * jnp.mean(gx * xh, axis=-1, keepdims=True))
        do_ref[...] = (d_og * _silu(za)).astype(BF16)
        dza_ref[...] = (d_og * o_ref[...].astype(F32) * _dsilu(za)).astype(BF16)
        ygv = yg_ref[...]
        sg = _sigmoid(gp_ref[...] + b_ref[...])
        y2 = ygv * sg
        sz = _silu(zs)
        os_ = y2 * sz
        dms = dm[:, ATTN_W:]
        rs = lax.rsqrt(jnp.mean(os_ * os_, axis=-1, keepdims=True) + NORM_EPS)
        xs = os_ * rs
        gws_ref[...] += jnp.sum(dms * xs, axis=0, keepdims=True)
        gxs = dms * ws_ref[...]
        d_os = rs * (gxs - xs * jnp.mean(gxs * xs, axis=-1, keepdims=True))
        dzs_ref[...] = (d_os * y2 * _dsilu(zs)).astype(BF16)
        d_y2 = d_os * sz
        d_g = d_y2 * ygv * sg * (1.0 - sg)
        dg_ref[...] = d_g.astype(BF16)
        gb_ref[...] += jnp.sum(d_g, axis=0, keepdims=True)
        dyg_ref[...] = d_y2 * sg

    row = lambda w: pl.BlockSpec((1, w), lambda i: (0, 0))
    full = lambda w: pl.BlockSpec((tm, w), lambda i: (i, 0))
    half = lambda c: pl.BlockSpec((tm, 512), lambda i: (i, c))
    return pl.pallas_call(
        body,
        name="merge_bwd",
        grid=(L // tm,),
        in_specs=[full(D_MODEL), pl.BlockSpec((D_MODEL, D_MODEL), lambda i: (0, 0)),
                  full(ATTN_W), full(ATTN_W), full(SSM_W), full(SSM_W),
                  half(3), half(4), half(7), half(8), row(SSM_W), row(ATTN_W), row(SSM_W)],
        out_specs=[full(ATTN_W), full(ATTN_W), full(SSM_W), full(SSM_W), full(SSM_W),
                   row(ATTN_W), row(SSM_W), row(SSM_W)],
        out_shape=[jax.ShapeDtypeStruct((L, ATTN_W), BF16), jax.ShapeDtypeStruct((L, ATTN_W), BF16),
                   jax.ShapeDtypeStruct((L, SSM_W), BF16), jax.ShapeDtypeStruct((L, SSM_W), BF16),
                   jax.ShapeDtypeStruct((L, SSM_W), F32),
                   jax.ShapeDtypeStruct((1, ATTN_W), F32), jax.ShapeDtypeStruct((1, SSM_W), F32),
                   jax.ShapeDtypeStruct((1, SSM_W), F32)],
        compiler_params=_cp(("arbitrary",)),
    )(d_out_b, w_out, og, o, yg, gpre, proj, proj, proj, proj, b_glu.reshape(1, SSM_W), wa.reshape(1, ATTN_W),
      ws.reshape(1, SSM_W))


def _rms_bwd_x(x, norm_w, d_hn, d_out, ride):
    L = x.shape[0]
    tm = _tile(L, 256)

    def body(x_ref, w_ref, dh_ref, do_ref, gx_ref, gw_ref):
        i = pl.program_id(0)

        @pl.when(i == 0)
        def _():
            gw_ref[...] = jnp.zeros_like(gw_ref)

        xv, dh = x_ref[...], dh_ref[...]
        r = lax.rsqrt(jnp.mean(xv * xv, axis=-1, keepdims=True) + NORM_EPS)
        xh = xv * r
        gw_ref[...] += jnp.sum(dh * xh, axis=0, keepdims=True)
        gx = dh * w_ref[...]
        gx_ref[...] = do_ref[...] + r * (gx - xh * jnp.mean(gx * xh, axis=-1, keepdims=True))

    blk = pl.BlockSpec((tm, D_MODEL), lambda i: (i, 0))
    row = pl.BlockSpec((1, D_MODEL), lambda i: (0, 0))
    return _call(body, "rms_bwd_x", (L // tm,), [blk, row, blk, blk], [blk, row],
                 [jax.ShapeDtypeStruct((L, D_MODEL), F32), jax.ShapeDtypeStruct((1, D_MODEL), F32)],
                 (x, norm_w.reshape(1, D_MODEL), d_hn, d_out), ride=ride)


def _rope_table(positions):
    inv_freq = ROPE_THETA ** (-jnp.arange(0, HEAD_DIM, 2, dtype=F32) / HEAD_DIM)
    ang = positions.astype(F32)[:, None] * inv_freq
    sign = jnp.where(jnp.arange(128) % HEAD_DIM < HEAD_DIM // 2, -1.0, 1.0)
    return jnp.concatenate([jnp.tile(jnp.cos(ang), (1, 4)), jnp.tile(jnp.sin(ang), (1, 4)) * sign], axis=1)


def _step(x, positions, target, w, core, chip):
    small = {n: w[n] for n in _SMALL}
    tab = _rope_table(positions)
    mt_b, scat_b, ocat_b, a16 = _ssm_prep(small)
    perm = _chunk_perm()
    blocks = lambda t: t.reshape(N_DEV, t.shape[0] // N_DEV, t.shape[1])

    proj, hn, wt_in = _rms_inproj_gather(x, small["norm_w"], w["w_in"].T.astype(BF16), chip)
    wt_in = wt_in.reshape(IN_W, D_MODEL)
    q_rot, k_rot = _qk_prep(proj, tab, small["q_norm_w"], small["k_norm_w"])
    (og, o, lse), (w_glu,) = _attn_fwd(q_rot, k_rot, proj, small["sinks"],
                                       _gather_exchange([w["w_glu"].astype(BF16)]))
    (y, yg, hx), (w_out,) = _ssm_fwd(proj, perm, mt_b, scat_b, ocat_b, a16, small["d_skip"],
                                     _gather_exchange([w["w_out"].astype(BF16)]))
    w_glu, w_out = w_glu.reshape(SSM_W, SSM_W), w_out.reshape(D_MODEL, D_MODEL)
    merged, gpre = _merge(og, yg, w_glu, proj, small["b_glu"], small["attn_out_norm_w"], small["ssm_out_norm_w"])
    d_out, d_out_b, loss_parts = _outproj_loss(merged, w_out, x, target)
    loss = 0.5 * jnp.sum(loss_parts[:, 0, 0]) / D_MODEL

    g_w_out = blocks(_mm(merged, d_out_b, "tn", F32, "grad_w_out", tm=1024))
    d_o, d_za, d_zs, d_g, d_yg1, g_wa, g_ws, g_bglu = _merge_bwd(
        d_out_b, w_out, og, o, yg, gpre, proj, small["b_glu"], small["attn_out_norm_w"], small["ssm_out_norm_w"])
    g_w_glu = blocks(_mm(yg, d_g, "tn", F32, "grad_w_glu"))
    d_yg = _mm(d_g, w_glu, "nt", F32, "d_yg", add=d_yg1)
    (d_u, g_mt, g_scat, g_ocat, g_a16, g_dskip), (ra_out, ra_glu) = _ssm_bwd(
        d_yg, y, proj, hx, perm, mt_b, scat_b, ocat_b, a16, small["d_skip"], _pair_exchange([g_w_out, g_w_glu]))
    p_out = _pair_sum(g_w_out, ra_out, core, BF16, "pair_sum_out")
    p_glu = _pair_sum(g_w_glu, ra_glu, core, BF16, "pair_sum_glu")
    (d_q, d_k, d_v, g_sinks), (rb_out, rb_glu) = _attn_bwd(
        q_rot, k_rot, proj, small["sinks"], d_o, o, lse, _chip_exchange([p_out, p_glu]))
    d_proj, g_qw, g_kw = _qk_prep_bwd(proj, tab, small["q_norm_w"], small["k_norm_w"], d_q, d_k, d_v,
                                      d_za, d_u, d_zs)
    g_qw = g_qw[0, :HEAD_DIM] + g_qw[0, HEAD_DIM:]
    g_kw = g_kw[0, :HEAD_DIM] + g_kw[0, HEAD_DIM:]
    g_in_a = blocks(_mm(d_proj, hn, "tn", F32, "grad_w_in_a", tm=1152, panel=0))
    g_in_b, (ra_a,) = _mm(d_proj, hn, "tn", F32, "grad_w_in_b", tm=1152, panel=1, ride=_pair_exchange([g_in_a]))
    g_in_b = blocks(g_in_b)
    p_a = _pair_sum(g_in_a, ra_a, core, BF16, "pair_sum_in_a")
    d_hn, (rb_a, ra_b) = _mm(d_proj, wt_in, "nn", F32, "d_hn", tm=1024,
                             ride=_both(_chip_exchange([p_a]), _pair_exchange([g_in_b])))
    p_b = _pair_sum(g_in_b, ra_b, core, BF16, "pair_sum_in_b")
    g_small, (rb_b,) = _ssm_prep_bwd(small, g_mt, g_scat, g_ocat, g_a16, _chip_exchange([p_b]))
    (grad_x, g_nw), _ = _rms_bwd_x(x, small["norm_w"], d_hn, d_out, None)

    g_small.update(norm_w=g_nw.reshape(-1), q_norm_w=g_qw.reshape(-1), k_norm_w=g_kw.reshape(-1),
                   sinks=g_sinks[0, :N_HEADS], d_skip=g_dskip.reshape(-1), b_glu=g_bglu.reshape(-1),
                   attn_out_norm_w=g_wa.reshape(-1), ssm_out_norm_w=g_ws.reshape(-1))
    g_packed = _slab_all_reduce(_pack(g_small, loss).reshape(N_DEV, _PACK_ROWS // N_DEV, 128))
    g_packed = g_packed.reshape(_PACK_ROWS, 128)
    grads = _unpack(g_packed, w)
    parts = dict(w_in=([p_a, p_b], [rb_a, rb_b]), w_glu=([p_glu], [rb_glu]), w_out=([p_out], [rb_out]))
    return g_packed[_LOSS_ROW, 0], grad_x, grads, parts


_ANY = pl.BlockSpec(memory_space=pl.ANY)


class _Exchange:
    def __init__(self, arrays, out_shape, sems, start, finish, relay=None):
        self.arrays, self.out_shape, self.sems, self.start, self.finish = arrays, out_shape, sems, start, finish
        self.relay = relay if relay is not None else (lambda ins, outs, sems: None)


def _gather_exchange(blocks):
    n = len(blocks)

    def parts(ins, outs, sems):
        send_sems, recv_sems, local_sems = sems
        x, y, c = lax.axis_index("x"), lax.axis_index("y"), lax.axis_index("c")
        me, sibling = (x, y, c), (x, y, 1 - c)
        chips = [(1 - x, y), (x, 1 - y), (1 - x, 1 - y)]

        def slot(k, dev):
            return outs[k].at[4 * dev[0] + 2 * dev[1] + dev[2]]

        def copy(k, q, block, to, src=None):
            return pltpu.make_async_remote_copy(
                src_ref=slot(k, block) if src is None else src, dst_ref=slot(k, block),
                send_sem=send_sems.at[k, q], recv_sem=recv_sems.at[k, q], device_id=to, device_id_type=MESH)

        mine = [pltpu.make_async_copy(ins[k], slot(k, me), local_sems.at[k]) for k in range(n)]
        first = []
        for k in range(n):
            first.append(copy(k, 0, me, sibling, src=ins[k]))
            first += [copy(k, 1 + j, me, (*chip, c), src=ins[k]) for j, chip in enumerate(chips)]
        return me, sibling, chips, c, copy, mine, first

    def start(ins, outs, sems):
        *_, mine, first = parts(ins, outs, sems)
        for cp in mine + first:
            cp.start()

    def relay(ins, outs, sems):
        me, sibling, chips, c, copy, _, _ = parts(ins, outs, sems)
        for j, chip in enumerate(chips):
            for k in range(n):
                copy(k, 1 + j, (*chip, c), me).wait_recv()
                copy(k, 4 + j, (*chip, c), sibling).start()

    def finish(ins, outs, sems):
        me, sibling, chips, c, copy, mine, first = parts(ins, outs, sems)
        for k in range(n):
            copy(k, 0, sibling, me).wait_recv()
            for j, chip in enumerate(chips):
                copy(k, 4 + j, (*chip, 1 - c), me).wait_recv()
        for cp in first + [copy(k, 4 + j, (*chip, c), sibling) for k in range(n) for j, chip in enumerate(chips)]:
            cp.wait_send()
        for cp in mine:
            cp.wait()

    return _Exchange(blocks, [jax.ShapeDtypeStruct((N_DEV,) + b.shape, b.dtype) for b in blocks],
                     [pltpu.SemaphoreType.DMA((n, 7)), pltpu.SemaphoreType.DMA((n, 7)), pltpu.SemaphoreType.DMA((n,))],
                     start, finish, relay)


def _direct_exchange(arrays, out_lead, fan, route):
    n = len(arrays)

    def copies(ins, outs, sems):
        send_sems, recv_sems = sems
        legs = route(lax.axis_index("x"), lax.axis_index("y"), lax.axis_index("c"))
        return [pltpu.make_async_remote_copy(
            src_ref=ins[k].at[src], dst_ref=outs[k].at[q], send_sem=send_sems.at[k, q], recv_sem=recv_sems.at[k, q],
            device_id=to, device_id_type=MESH) for k in range(n) for src, q, to in legs]

    def start(ins, outs, sems):
        for cp in copies(ins, outs, sems):
            cp.start()

    def finish(ins, outs, sems):
        for cp in copies(ins, outs, sems):
            cp.wait()

    return _Exchange(arrays, [jax.ShapeDtypeStruct((out_lead,) + a.shape[1:], a.dtype) for a in arrays],
                     [pltpu.SemaphoreType.DMA((n, fan)), pltpu.SemaphoreType.DMA((n, fan))], start, finish)


def _pair_exchange(grads):
    return _direct_exchange(grads, 4, 4, lambda x, y, c: [(2 * chip + (1 - c), chip, (x, y, 1 - c))
                                                          for chip in range(4)])


def _chip_exchange(parts):
    def route(x, y, c):
        chips = [(1 - x, y), (x, 1 - y), (1 - x, 1 - y)]
        return [(2 * chip[0] + chip[1], q, (*chip, c)) for q, chip in enumerate(chips)]
    return _direct_exchange(parts, 3, 3, route)


def _both(ex1, ex2):
    n1, s1 = len(ex1.arrays), len(ex1.sems)

    def halves(ins, outs, sems):
        return (ins[:n1], outs[:n1], sems[:s1]), (ins[n1:], outs[n1:], sems[s1:])

    def start(ins, outs, sems):
        h1, h2 = halves(ins, outs, sems)
        ex1.start(*h1)
        ex2.start(*h2)

    def relay(ins, outs, sems):
        h1, h2 = halves(ins, outs, sems)
        ex1.relay(*h1)
        ex2.relay(*h2)

    def finish(ins, outs, sems):
        h1, h2 = halves(ins, outs, sems)
        ex1.finish(*h1)
        ex2.finish(*h2)

    return _Exchange(list(ex1.arrays) + list(ex2.arrays), list(ex1.out_shape) + list(ex2.out_shape),
                     list(ex1.sems) + list(ex2.sems), start, finish, relay)


def _call(body, name, grid, in_specs, out_specs, out_shape, args, scratch_shapes=(), ride=None):
    if ride is None:
        sem = ("arbitrary",) * len(grid)
        return pl.pallas_call(body, name=name, grid=grid, in_specs=in_specs, out_specs=out_specs, out_shape=out_shape,
                              scratch_shapes=list(scratch_shapes), compiler_params=_cp(sem))(*args), None
    n_in, n_out, n_scr, n_x = len(in_specs), len(out_specs), len(scratch_shapes), len(ride.arrays)

    def wrapped(*refs):
        ins, refs = refs[:n_in], refs[n_in:]
        x_in, refs = refs[:n_x], refs[n_x:]
        outs, refs = refs[:n_out], refs[n_out:]
        x_out, refs = refs[:n_x], refs[n_x:]
        scr, sems = refs[:n_scr], refs[n_scr:]
        step, total = pl.program_id(0), grid[0]
        for a in range(1, len(grid)):
            step, total = step * grid[a] + pl.program_id(a), total * grid[a]
        @pl.when(step == 0)
        def _():
            ride.start(x_in, x_out, sems)

        @pl.when(step == max(total - 2, 0))
        def _():
            ride.relay(x_in, x_out, sems)

        body(*ins, *outs, *scr)

        @pl.when(step == total - 1)
        def _():
            ride.finish(x_in, x_out, sems)

    res = pl.pallas_call(
        wrapped, name=name, grid=grid, in_specs=list(in_specs) + [_ANY] * n_x,
        out_specs=list(out_specs) + [_ANY] * n_x, out_shape=list(out_shape) + list(ride.out_shape),
        scratch_shapes=list(scratch_shapes) + list(ride.sems),
        compiler_params=_cp(("arbitrary",) * len(grid)))(*args, *ride.arrays)
    return res[:n_out], list(res[n_out:])


def _pair_sum(g, ra, core, out_dtype, name):
    _, r, C = g.shape
    tr = _tile(r, 576)

    def body(c_ref, g_ref, ra_ref, p_ref):
        p_ref[...] = (g_ref[...] + ra_ref[...]).astype(p_ref.dtype)

    return pl.pallas_call(
        body,
        name=name,
        grid_spec=pltpu.PrefetchScalarGridSpec(
            num_scalar_prefetch=1,
            grid=(4, r // tr),
            in_specs=[pl.BlockSpec((1, tr, C), lambda j, t, c_ref: (2 * j + c_ref[0], t, 0)),
                      pl.BlockSpec((1, tr, C), lambda j, t, c_ref: (j, t, 0))],
            out_specs=pl.BlockSpec((1, tr, C), lambda j, t, c_ref: (j, t, 0)),
        ),
        out_shape=jax.ShapeDtypeStruct((4, r, C), out_dtype),
        compiler_params=_cp(("parallel", "parallel")),
    )(core, g, ra)


def _slab_all_reduce(slab):
    _, r, lanes = slab.shape

    def body(s_ref, o_ref, ra, rb, ps, sems_a, sems_b, sems_c):
        x, y, c = lax.axis_index("x"), lax.axis_index("y"), lax.axis_index("c")
        chips = [(1 - x, y), (x, 1 - y), (1 - x, 1 - y)]
        pair = [pltpu.make_async_remote_copy(
            src_ref=s_ref.at[2 * k + (1 - c)], dst_ref=ra.at[k], send_sem=sems_a.at[0, k], recv_sem=sems_a.at[1, k],
            device_id=(x, y, 1 - c), device_id_type=MESH) for k in range(4)]
        for cp in pair:
            cp.start()
        for cp in pair:
            cp.wait()
        for k in range(4):
            ps[k] = s_ref[2 * k + c] + ra[k]
        cross = [pltpu.make_async_remote_copy(
            src_ref=ps.at[2 * ch[0] + ch[1]], dst_ref=rb.at[q], send_sem=sems_b.at[0, q], recv_sem=sems_b.at[1, q],
            device_id=(*ch, c), device_id_type=MESH) for q, ch in enumerate(chips)]
        for cp in cross:
            cp.start()
        for cp in cross:
            cp.wait()
        me = 4 * x + 2 * y + c
        o_ref[me] = ((ps[2 * x + y] + rb[0]) + rb[1]) + rb[2]
        flips = [(dx, dy, dc) for dx in (0, 1) for dy in (0, 1) for dc in (0, 1) if dx + dy + dc]
        spread = [pltpu.make_async_remote_copy(
            src_ref=o_ref.at[me], dst_ref=o_ref.at[me], send_sem=sems_c.at[0, q], recv_sem=sems_c.at[1, q],
            device_id=(x + dx - 2 * x * dx, y + dy - 2 * y * dy, c + dc - 2 * c * dc), device_id_type=MESH)
            for q, (dx, dy, dc) in enumerate(flips)]
        for cp in spread:
            cp.start()
        for q, (dx, dy, dc) in enumerate(flips):
            peer = 4 * (x + dx - 2 * x * dx) + 2 * (y + dy - 2 * y * dy) + (c + dc - 2 * c * dc)
            pltpu.make_async_remote_copy(
                src_ref=o_ref.at[peer], dst_ref=o_ref.at[peer], send_sem=sems_c.at[0, q], recv_sem=sems_c.at[1, q],
                device_id=(x, y, c), device_id_type=MESH).wait_recv()
        for cp in spread:
            cp.wait_send()

    whole = pl.BlockSpec(memory_space=pltpu.VMEM)
    return pl.pallas_call(
        body, name="slab_all_reduce", in_specs=[whole], out_specs=whole,
        out_shape=jax.ShapeDtypeStruct(slab.shape, F32),
        scratch_shapes=[pltpu.VMEM((4, r, lanes), F32), pltpu.VMEM((3, r, lanes), F32), pltpu.VMEM((4, r, lanes), F32),
                        pltpu.SemaphoreType.DMA((2, 4)), pltpu.SemaphoreType.DMA((2, 3)),
                        pltpu.SemaphoreType.DMA((2, 7))],
        compiler_params=_cp(),
    )(slab)


def _adamw_reduced(ps, rbs, chip, w, m, v, name):
    nh = len(ps)
    R, C = w.shape
    ch = C // nh
    tr = _tile(R, 288)
    nt = R // tr
    c1 = 1.0 - ADAM_B1 ** ADAM_STEP
    c2 = 1.0 - ADAM_B2 ** ADAM_STEP

    def body(c_ref, *refs):
        p_refs, rb_refs = refs[:nh], refs[nh:2 * nh]
        w_ref, m_ref, v_ref, g_ref, d_ref, nm_ref, nv_ref = refs[2 * nh:]
        for h in range(nh):
            @pl.when(pl.program_id(0) == h)
            def _(h=h):
                rb = rb_refs[h]
                gv = p_refs[h][0].astype(F32) + rb[0].astype(F32)
                gv = gv + rb[1].astype(F32)
                gv = gv + rb[2].astype(F32)
                nm = ADAM_B1 * m_ref[...] + (1.0 - ADAM_B1) * gv
                nv = ADAM_B2 * v_ref[...] + (1.0 - ADAM_B2) * (gv * gv)
                g_ref[...] = gv
                nm_ref[...] = nm
                nv_ref[...] = nv
                d_ref[...] = -ADAM_LR * ((nm / c1) / (jnp.sqrt(nv / c2) + ADAM_EPS) + ADAM_WD * w_ref[...])

    def held(h):
        return lambda hh, tt: jnp.where(hh == h, tt, jnp.where(hh < h, 0, nt - 1))

    p_specs = [pl.BlockSpec((1, tr, ch), lambda hh, tt, c_ref, f=held(h): (c_ref[0], f(hh, tt), 0))
               for h in range(nh)]
    rb_specs = [pl.BlockSpec((3, tr, ch), lambda hh, tt, c_ref, f=held(h): (0, f(hh, tt), 0)) for h in range(nh)]
    blk = pl.BlockSpec((tr, ch), lambda hh, tt, c_ref: (tt, hh))
    return pl.pallas_call(
        body,
        name=name,
        grid_spec=pltpu.PrefetchScalarGridSpec(
            num_scalar_prefetch=1, grid=(nh, nt), in_specs=p_specs + rb_specs + [blk] * 3, out_specs=[blk] * 4),
        out_shape=[jax.ShapeDtypeStruct((R, C), F32)] * 4,
        compiler_params=_cp(("arbitrary", "arbitrary")),
    )(chip, *ps, *rbs, w, m, v)


_SMALL = ("norm_w", "q_norm_w", "k_norm_w", "sinks", "a_re", "a_im", "log_step", "b_re", "b_im", "c_re", "c_im",
          "d_skip", "b_glu", "attn_out_norm_w", "ssm_out_norm_w")
_WEIGHTS = ("norm_w", "w_in", "q_norm_w", "k_norm_w", "sinks", "a_re", "a_im", "log_step", "b_re", "b_im", "c_re",
            "c_im", "d_skip", "w_glu", "b_glu", "attn_out_norm_w", "ssm_out_norm_w", "w_out")
_SMALL_2D = dict(norm_w=(1, 2048), q_norm_w=(1, 64), k_norm_w=(1, 64), sinks=(1, 16), a_re=(64, 64), a_im=(64, 64),
                 log_step=(1, 64), b_re=(1024, 64), b_im=(1024, 64), c_re=(1024, 64), c_im=(1024, 64),
                 d_skip=(1, 1024), b_glu=(1, 1024), attn_out_norm_w=(1, 1024), ssm_out_norm_w=(1, 1024))
_P_MINOR = ("b_re", "b_im")


def _flat_form(n, t):
    return t.transpose(0, 2, 1) if n in _P_MINOR else t


def _own_form(n, t, shape):
    if n in _P_MINOR:
        return t.reshape(shape[0], shape[2], shape[1]).transpose(0, 2, 1)
    return t.reshape(shape)


def _slab_rows(n):
    return -(-n // 1024) * 8


_PACK_ROWS = 2304


_LOSS_ROW = 2192


def _pack(d, loss):
    parts = []
    for n in _SMALL:
        flat = _flat_form(n, d[n]).reshape(-1).astype(F32)
        rows = _slab_rows(flat.shape[0])
        parts.append(jnp.pad(flat, (0, rows * 128 - flat.shape[0])).reshape(rows, 128))
    assert sum(p.shape[0] for p in parts) == _LOSS_ROW
    parts.append(jnp.pad(loss.reshape(1, 1), ((0, _PACK_ROWS - _LOSS_ROW - 1), (0, 127))))
    return jnp.concatenate(parts, axis=0)


def _unpack(packed, like):
    out, off = {}, 0
    for n in _SMALL:
        size = math.prod(like[n].shape)
        rows = _slab_rows(size)
        out[n] = _own_form(n, packed[off:off + rows].reshape(-1)[:size], like[n].shape)
        off += rows
    return out


def _adamw_small(g, w, m, v):
    c1 = 1.0 - ADAM_B1 ** ADAM_STEP
    c2 = 1.0 - ADAM_B2 ** ADAM_STEP
    k = len(_SMALL)

    def body(*refs):
        ins, outs = refs[:4 * k], refs[4 * k:]
        for j in range(k):
            gv, wv, mv, vv = (ins[q * k + j][...] for q in range(4))
            nm = ADAM_B1 * mv + (1.0 - ADAM_B1) * gv
            nv = ADAM_B2 * vv + (1.0 - ADAM_B2) * (gv * gv)
            outs[j][...] = -ADAM_LR * ((nm / c1) / (jnp.sqrt(nv / c2) + ADAM_EPS) + ADAM_WD * wv)
            outs[k + j][...] = nm
            outs[2 * k + j][...] = nv

    args = [_flat_form(n, d[n]).reshape(_SMALL_2D[n]) for d in (g, w, m, v) for n in _SMALL]
    shapes = [jax.ShapeDtypeStruct(_SMALL_2D[n], F32) for _ in range(3) for n in _SMALL]
    outs = pl.pallas_call(body, name="adamw_small", out_shape=shapes, compiler_params=_cp())(*args)
    res = []
    for q in range(3):
        res.append({n: _own_form(n, outs[q * k + j], w[n].shape) for j, n in enumerate(_SMALL)})
    return res


def kernel(x, positions, norm_w, w_in, q_norm_w, k_norm_w, sinks, a_re, a_im, log_step, b_re, b_im, c_re, c_im, d_skip, w_glu, b_glu, attn_out_norm_w, ssm_out_norm_w, w_out, loss_target, m_norm_w, m_w_in, m_q_norm_w, m_k_norm_w, m_sinks, m_a_re, m_a_im, m_log_step, m_b_re, m_b_im, m_c_re, m_c_im, m_d_skip, m_w_glu, m_b_glu, m_attn_out_norm_w, m_ssm_out_norm_w, m_w_out, v_norm_w, v_w_in, v_q_norm_w, v_k_norm_w, v_sinks, v_a_re, v_a_im, v_log_step, v_b_re, v_b_im, v_c_re, v_c_im, v_d_skip, v_w_glu, v_b_glu, v_attn_out_norm_w, v_ssm_out_norm_w, v_w_out):
    w = dict(norm_w=norm_w, w_in=w_in, q_norm_w=q_norm_w, k_norm_w=k_norm_w, sinks=sinks, a_re=a_re, a_im=a_im,
             log_step=log_step, b_re=b_re, b_im=b_im, c_re=c_re, c_im=c_im, d_skip=d_skip, w_glu=w_glu, b_glu=b_glu,
             attn_out_norm_w=attn_out_norm_w, ssm_out_norm_w=ssm_out_norm_w, w_out=w_out)
    m = dict(norm_w=m_norm_w, w_in=m_w_in, q_norm_w=m_q_norm_w, k_norm_w=m_k_norm_w, sinks=m_sinks, a_re=m_a_re,
             a_im=m_a_im, log_step=m_log_step, b_re=m_b_re, b_im=m_b_im, c_re=m_c_re, c_im=m_c_im, d_skip=m_d_skip,
             w_glu=m_w_glu, b_glu=m_b_glu, attn_out_norm_w=m_attn_out_norm_w, ssm_out_norm_w=m_ssm_out_norm_w,
             w_out=m_w_out)
    v = dict(norm_w=v_norm_w, w_in=v_w_in, q_norm_w=v_q_norm_w, k_norm_w=v_k_norm_w, sinks=v_sinks, a_re=v_a_re,
             a_im=v_a_im, log_step=v_log_step, b_re=v_b_re, b_im=v_b_im, c_re=v_c_re, c_im=v_c_im, d_skip=v_d_skip,
             w_glu=v_w_glu, b_glu=v_b_glu, attn_out_norm_w=v_attn_out_norm_w, ssm_out_norm_w=v_ssm_out_norm_w,
             w_out=v_w_out)
    core = lax.axis_index("c").astype(jnp.int32).reshape(1)
    chip = (2 * lax.axis_index("x") + lax.axis_index("y")).astype(jnp.int32).reshape(1)

    loss, grad_x, grads, parts = _step(x[0], positions[0], loss_target[0], w, core, chip)
    delta, new_m, new_v = {}, {}, {}
    for n in ("w_glu", "w_out"):
        grads[n], delta[n], new_m[n], new_v[n] = _adamw_reduced(*parts[n], chip, w[n], m[n], v[n], f"adamw_{n}")
    g_t, d_t, m_t, v_t = _adamw_reduced(*parts["w_in"], chip, w["w_in"].T, m["w_in"].T, v["w_in"].T, "adamw_w_in")
    grads["w_in"], delta["w_in"], new_m["w_in"], new_v["w_in"] = g_t.T, d_t.T, m_t.T, v_t.T
    d_s, m_s, v_s = _adamw_small(grads, w, m, v)
    delta.update(d_s)
    new_m.update(m_s)
    new_v.update(v_s)

    return (loss, grad_x[None], *[grads[n] for n in _WEIGHTS], *[delta[n] for n in _WEIGHTS],
            *[new_m[n] for n in _WEIGHTS], *[new_v[n] for n in _WEIGHTS])
```

```python
import math

import jax
import jax.numpy as jnp
from jax import lax
from jax.experimental import pallas as pl
from jax.experimental.pallas import tpu as pltpu

F32 = jnp.float32
BF16 = jnp.bfloat16

D_MODEL = 2048
ATTN_W = 1024
KV_W = 256
SSM_W = 1024
HEAD_DIM = 64
N_HEADS = 16
N_KV = 4
IN_W = 4608
BLOCK = 128
ROPE_THETA = 10000.0
NORM_EPS = 1e-6
SSM_G = 64
SSM_P = 64
SSM_H = 16
CHUNK = 16
CW = CHUNK * SSM_H
N_DEV = 8

ADAM_LR = 0.001
ADAM_B1 = 0.9
ADAM_B2 = 0.999
ADAM_EPS = 1e-08
ADAM_WD = 0.01
ADAM_STEP = 10

VMEM_LIMIT = 56 * 1024 * 1024
MESH = pl.DeviceIdType.MESH


def _cp(sem=None):
    if sem is None:
        return pltpu.CompilerParams(vmem_limit_bytes=VMEM_LIMIT)
    return pltpu.CompilerParams(vmem_limit_bytes=VMEM_LIMIT, dimension_semantics=sem)


def _sigmoid(x):
    return 0.5 * jnp.tanh(0.5 * x) + 0.5


def _silu(x):
    return x * _sigmoid(x)


def _dsilu(x):
    s = _sigmoid(x)
    return s * (1.0 + x * (1.0 - s))


_GELU_C = math.sqrt(2.0 / math.pi)


def _gelu(y):
    t = jnp.tanh(_GELU_C * (y + 0.044715 * y * y * y))
    return 0.5 * y * (1.0 + t)


def _dgelu(y):
    t = jnp.tanh(_GELU_C * (y + 0.044715 * y * y * y))
    return 0.5 * (1.0 + t) + 0.5 * y * (1.0 - t * t) * _GELU_C * (1.0 + 3.0 * 0.044715 * y * y)


def _tile(n, want):
    if n <= want:
        return n
    for t in range(want - want % 16, 0, -16):
        if n % t == 0:
            return t
    raise ValueError((n, want))


def _mm(a, b, mode, out_dtype, name, tm=512, tn=1024, add=None, ride=None, panel=None):
    if mode == "nn":
        (M, K), (K2, N) = a.shape, b.shape
    elif mode == "nt":
        (M, K), (N, K2) = a.shape, b.shape
    else:
        (K, M), (K2, N) = a.shape, b.shape
    assert K == K2
    tm, tn = _tile(M, tm), _tile(N, tn)
    p0 = 0
    if panel is not None:
        assert mode != "nt" and add is None
        p0, N = panel, tn
    dn = {"nn": _NN, "nt": _NT, "tn": _TN}[mode]

    def body(a_ref, b_ref, *rest):
        o_ref = rest[-1]
        acc = lax.dot_general(a_ref[...].astype(BF16), b_ref[...].astype(BF16), dn, preferred_element_type=F32)
        if add is not None:
            acc = acc + rest[0][...]
        o_ref[...] = acc.astype(o_ref.dtype)

    a_spec = pl.BlockSpec((K, tm), lambda j, i: (0, i)) if mode == "tn" else pl.BlockSpec((tm, K), lambda j, i: (i, 0))
    b_spec = (pl.BlockSpec((tn, K), lambda j, i: (j, 0)) if mode == "nt"
              else pl.BlockSpec((K, tn), lambda j, i: (0, j + p0)))
    o_spec = pl.BlockSpec((tm, tn), lambda j, i: (i, j))
    extra = () if add is None else (add,)
    if ride is not None:
        (out,), landed = _call(body, name, (N // tn, M // tm), [a_spec, b_spec] + [o_spec] * len(extra), [o_spec],
                               [jax.ShapeDtypeStruct((M, N), out_dtype)], (a, b, *extra), ride=ride)
        return out, landed
    return pl.pallas_call(
        body,
        name=name,
        grid=(N // tn, M // tm),
        in_specs=[a_spec, b_spec] + [o_spec] * len(extra),
        out_specs=o_spec,
        out_shape=jax.ShapeDtypeStruct((M, N), out_dtype),
        compiler_params=_cp(("parallel", "parallel")),
    )(a, b, *extra)


_CHIP_ORDER = (0, 2, 1, 3)


def _rms_inproj_gather(x, norm_w, wt_shard, chip):
    L = x.shape[0]
    tm = _tile(L, 1024)
    ni = L // tm
    r = IN_W // N_DEV
    tn = 2 * r

    def body(chip_ref, x_ref, nw_ref, shard, proj_ref, hn_hbm, wt_hbm, hn_scr, w_scr, send_sems, recv_sems, loc_sems):
        jc, i = pl.program_id(0), pl.program_id(1)
        xx, yy, c = lax.axis_index("x"), lax.axis_index("y"), lax.axis_index("c")
        me, sibling = (xx, yy, c), (xx, yy, 1 - c)
        chips = [(1 - xx, yy), (xx, 1 - yy), (1 - xx, 1 - yy)]

        def slot(dev):
            return wt_hbm.at[4 * dev[0] + 2 * dev[1] + dev[2]]

        def copy(q, block, to, src=None):
            return pltpu.make_async_remote_copy(
                src_ref=slot(block) if src is None else src, dst_ref=slot(block),
                send_sem=send_sems.at[q], recv_sem=recv_sems.at[q], device_id=to, device_id_type=MESH)

        def rows_of(buf, core):
            return w_scr.at[buf, pl.ds(pl.multiple_of(core * r, 16), r)]

        mine = pltpu.make_async_copy(shard, slot(me), loc_sems.at[0])
        sends = [copy(0, me, sibling, src=shard)] + [copy(1 + j, me, (*ch, c), src=shard) for j, ch in enumerate(chips[:2])]
        relay_block = (xx + (1 - c) * (1 - 2 * xx), yy + c * (1 - 2 * yy), c)
        relay = copy(3, relay_block, (xx + c * (1 - 2 * xx), yy + (1 - c) * (1 - 2 * yy), c))
        first = jnp.logical_and(jc == 0, i == 0)

        @pl.when(first)
        def _():
            mine.start()
            for cp in sends:
                cp.start()
            own = pltpu.make_async_copy(shard, rows_of(0, c), loc_sems.at[1])
            own.start()
            copy(0, sibling, me).wait_recv()
            sib = pltpu.make_async_copy(slot(sibling), rows_of(0, 1 - c), loc_sems.at[2])
            sib.start()
            own.wait()
            sib.wait()

        def to_vmem(j, ch):
            pltpu.make_async_copy(slot((*ch, c)), rows_of((1 + j) % 2, c), loc_sems.at[1 + j]).start()

        @pl.when(jnp.logical_and(jc == 1, i == 0))
        def _():
            for j in range(2):
                copy(1 + j, (*chips[j], c), me).wait_recv()
                copy(4 + j, (*chips[j], c), sibling).start()
            relay.start()
            to_vmem(0, chips[0])

        @pl.when(jnp.logical_and(jc == 1, i == ni // 2))
        def _():
            to_vmem(1, chips[1])

        @pl.when(jnp.logical_and(jc == 2, i == ni // 2))
        def _():
            copy(3, (*chips[2], c), me).wait_recv()
            copy(6, (*chips[2], c), sibling).start()
            to_vmem(2, chips[2])

        for j, ch in enumerate(chips):
            @pl.when(jnp.logical_and(jc == 1 + j, i == 0))
            def _(j=j, ch=ch):
                buf = (1 + j) % 2
                copy(4 + j, (*ch, 1 - c), me).wait_recv()
                passed = pltpu.make_async_copy(slot((*ch, 1 - c)), rows_of(buf, 1 - c), loc_sems.at[4 + j])
                passed.start()
                pltpu.make_async_copy(slot((*ch, c)), rows_of(buf, c), loc_sems.at[1 + j]).wait()
                passed.wait()

        rows = pl.ds(pl.multiple_of(i * tm, tm), tm)

        @pl.when(jc == 0)
        def _():
            xv = x_ref[...]
            rstd = lax.rsqrt(jnp.mean(xv * xv, axis=-1, keepdims=True) + NORM_EPS)
            hn_scr[rows, :] = (xv * rstd * nw_ref[...]).astype(BF16)

        keep_hn = pltpu.make_async_copy(hn_scr, hn_hbm, loc_sems.at[7])

        @pl.when(jnp.logical_and(jc == 1, i == 0))
        def _():
            keep_hn.start()

        for buf in range(2):
            @pl.when(jc % 2 == buf)
            def _(buf=buf):
                proj_ref[...] = lax.dot_general(hn_scr[rows, :], w_scr[buf], _NT, preferred_element_type=F32)

        @pl.when(jnp.logical_and(jc == 3, i == ni - 1))
        def _():
            for cp in sends + [relay]:
                cp.wait_send()
            for j, ch in enumerate(chips):
                copy(4 + j, (*ch, c), sibling).wait_send()
            mine.wait()
            keep_hn.wait()

    def tile_of(jc, chip_ref):
        mask = jnp.where(jc == 1, _CHIP_ORDER[1], jnp.where(jc == 2, _CHIP_ORDER[2], jnp.where(jc == 3, _CHIP_ORDER[3], 0)))
        return jnp.bitwise_xor(chip_ref[0], mask)

    held = lambda jc, i: jnp.where(jc == 0, i, ni - 1)
    return pl.pallas_call(
        body,
        name="rms_inproj_gather",
        grid_spec=pltpu.PrefetchScalarGridSpec(
            num_scalar_prefetch=1,
            grid=(4, ni),
            in_specs=[pl.BlockSpec((tm, D_MODEL), lambda jc, i, ch: (held(jc, i), 0)),
                      pl.BlockSpec((1, D_MODEL), lambda jc, i, ch: (0, 0)), _ANY],
            out_specs=[pl.BlockSpec((tm, tn), lambda jc, i, ch: (i, tile_of(jc, ch))), _ANY, _ANY],
            scratch_shapes=[pltpu.VMEM((L, D_MODEL), BF16), pltpu.VMEM((2, tn, D_MODEL), BF16),
                            pltpu.SemaphoreType.DMA((7,)), pltpu.SemaphoreType.DMA((7,)), pltpu.SemaphoreType.DMA((8,))],
        ),
        out_shape=[jax.ShapeDtypeStruct((L, IN_W), F32), jax.ShapeDtypeStruct((L, D_MODEL), BF16),
                   jax.ShapeDtypeStruct((N_DEV, r, D_MODEL), BF16)],
        compiler_params=_cp(("arbitrary", "arbitrary")),
    )(chip, x, norm_w.reshape(1, D_MODEL), wt_shard)


def _seg_sum(v):
    a = lax.broadcasted_iota(jnp.int32, (128, 128), 0) // HEAD_DIM
    b = lax.broadcasted_iota(jnp.int32, (128, 128), 1) // HEAD_DIM
    ones = jnp.where(a == b, 1.0, 0.0).astype(BF16)
    hi = v.astype(BF16)
    lo = (v - hi.astype(F32)).astype(BF16)
    return jnp.dot(hi, ones, preferred_element_type=F32) + jnp.dot(lo, ones, preferred_element_type=F32)


def _rot_half(t):
    lane = lax.broadcasted_iota(jnp.int32, t.shape, 1)
    return jnp.where(lane % HEAD_DIM < HEAD_DIM // 2, pltpu.roll(t, 128 - HEAD_DIM // 2, 1),
                     pltpu.roll(t, HEAD_DIM // 2, 1))


def _norm_rope(raw, w, cos, sin):
    r = lax.rsqrt(_seg_sum(raw * raw) * (1.0 / HEAD_DIM) + NORM_EPS)
    tn = raw * r * w
    return r, tn * cos + _rot_half(tn) * sin


def _norm_rope_bwd(d_rot, raw, w, cos, sin):
    r = lax.rsqrt(_seg_sum(raw * raw) * (1.0 / HEAD_DIM) + NORM_EPS)
    d_tn = d_rot * cos + _rot_half(d_rot * sin)
    xh = raw * r
    gw = d_tn * w
    d_raw = r * (gw - xh * (_seg_sum(gw * xh) * (1.0 / HEAD_DIM)))
    return d_raw, d_tn * xh


def _band_mask2(has_prev, keys_on_rows=False):
    qd, kd = (1, 0) if keys_on_rows else (0, 1)
    qi = lax.broadcasted_iota(jnp.int32, (2 * BLOCK, 2 * BLOCK), qd) % BLOCK + BLOCK
    kj = lax.broadcasted_iota(jnp.int32, (2 * BLOCK, 2 * BLOCK), kd)
    rel = qi - kj
    return (rel >= 0) & (rel < BLOCK) & ((kj >= BLOCK) | has_prev)


def _half_tiles(pair):
    lo = lax.broadcasted_iota(jnp.int32, pair.shape, 1) < HEAD_DIM
    sw = pltpu.roll(pair, HEAD_DIM, 1)
    z = jnp.zeros_like(pair)
    return (jnp.where(lo, pair, z).astype(BF16), jnp.where(lo, z, sw).astype(BF16),
            jnp.where(lo, sw, z).astype(BF16), jnp.where(lo, z, pair).astype(BF16))


def _two_rows(top, bottom):
    row = lax.broadcasted_iota(jnp.int32, (2 * BLOCK, 1), 0)
    return jnp.where(row < BLOCK, top, bottom)


def _lane_col(mat, h):
    lane = lax.broadcasted_iota(jnp.int32, mat.shape, 1)
    return jnp.sum(jnp.where(lane == h, mat, 0.0), axis=1, keepdims=True)


_SCALE = 1.0 / math.sqrt(HEAD_DIM)
_NT = (((1,), (1,)), ((), ()))
_NN = (((1,), (0,)), ((), ()))
_TN = (((0,), (0,)), ((), ()))


def _qk_prep(proj, tab, qw, kw):
    L = proj.shape[0]
    tm = _tile(L, 512)

    def body(q_ref, k_ref, t_ref, qw_ref, kw_ref, qo_ref, ko_ref):
        cos, sin = t_ref[:, :128], t_ref[:, 128:]
        for c in range(ATTN_W // 128):
            _, qr = _norm_rope(q_ref[:, c * 128:(c + 1) * 128], qw_ref[...], cos, sin)
            qo_ref[:, c * 128:(c + 1) * 128] = (qr * _SCALE).astype(BF16)
        for c in range(KV_W // 128):
            _, kr = _norm_rope(k_ref[:, c * 128:(c + 1) * 128], kw_ref[...], cos, sin)
            ko_ref[:, c * 128:(c + 1) * 128] = kr.astype(BF16)

    row = pl.BlockSpec((1, 128), lambda i: (0, 0))
    return pl.pallas_call(
        body,
        name="qk_prep",
        grid=(L // tm,),
        in_specs=[pl.BlockSpec((tm, ATTN_W), lambda i: (i, 0)), pl.BlockSpec((tm, KV_W), lambda i: (i, 4)),
                  pl.BlockSpec((tm, 256), lambda i: (i, 0)), row, row],
        out_specs=[pl.BlockSpec((tm, ATTN_W), lambda i: (i, 0)), pl.BlockSpec((tm, KV_W), lambda i: (i, 0))],
        out_shape=[jax.ShapeDtypeStruct((L, ATTN_W), BF16), jax.ShapeDtypeStruct((L, KV_W), BF16)],
        compiler_params=_cp(("parallel",)),
    )(proj, proj, tab, jnp.tile(qw, 2).reshape(1, 128), jnp.tile(kw, 2).reshape(1, 128))


def _group_tiles(g, kt, vt):
    a, b = divmod(g, 2)
    return kt[a][2 * b], kt[a][2 * b + 1], vt[a][2 * b], vt[a][2 * b + 1]


def _attn_fwd(q, k, proj, sinks, ride):
    L = proj.shape[0]
    nb = L // BLOCK

    def body(q_ref, kc_ref, kp_ref, vc_ref, vp_ref, z0_ref, z1_ref, sink_ref, og_ref, o_ref, lse_ref):
        i = pl.program_id(0)
        mask = _band_mask2(i > 0)
        z = jnp.concatenate([z0_ref[...], z1_ref[...]], axis=1)
        lane = lax.broadcasted_iota(jnp.int32, (BLOCK, 128), 1)
        kt = [_half_tiles(jnp.concatenate([kp_ref[:, a * 128:(a + 1) * 128], kc_ref[:, a * 128:(a + 1) * 128]],
                                          axis=0).astype(F32)) for a in range(2)]
        vt = [_half_tiles(jnp.concatenate([vp_ref[:, a * 128:(a + 1) * 128], vc_ref[:, a * 128:(a + 1) * 128]],
                                          axis=0)) for a in range(2)]
        lse_mat = jnp.zeros((BLOCK, 128), F32)
        pairs = []
        for g in range(N_KV):
            k_lo, k_hi, v_lo, v_hi = _group_tiles(g, kt, vt)
            q2 = jnp.concatenate([q_ref[:, 2 * g * 128:(2 * g + 1) * 128],
                                  q_ref[:, (2 * g + 1) * 128:(2 * g + 2) * 128]], axis=0)
            for half, (kh, vh) in enumerate(((k_lo, v_lo), (k_hi, v_hi))):
                pairs.append(dict(g=g, half=half, vh=vh, s=lax.dot_general(q2, kh, _NT, preferred_element_type=F32)))
        for pr in pairs:
            h_top, h_bot = 4 * pr["g"] + pr["half"], 4 * pr["g"] + 2 + pr["half"]
            s = jnp.where(mask, pr["s"], -1e30)
            sink = _two_rows(sink_ref[h_top], sink_ref[h_bot])
            m = jnp.maximum(jnp.max(s, axis=-1, keepdims=True), sink)
            e = jnp.exp(s - m)
            den = jnp.sum(e, axis=-1, keepdims=True) + jnp.exp(sink - m)
            pr["p_b"] = (e * (1.0 / den)).astype(BF16)
            lse = m + jnp.log(den)
            lse_mat = jnp.where(lane == h_top, lse[:BLOCK], lse_mat)
            lse_mat = jnp.where(lane == h_bot, lse[BLOCK:], lse_mat)
        outs = []
        for g in range(N_KV):
            acc = (jnp.dot(pairs[2 * g]["p_b"], pairs[2 * g]["vh"], preferred_element_type=F32)
                   + jnp.dot(pairs[2 * g + 1]["p_b"], pairs[2 * g + 1]["vh"], preferred_element_type=F32))
            outs += [acc[:BLOCK], acc[BLOCK:]]
        o = jnp.concatenate(outs, axis=1)
        o_ref[...] = o.astype(BF16)
        og_ref[...] = (o * _silu(z)).astype(BF16)
        lse_ref[...] = lse_mat

    prev = lambda i: jnp.maximum(i - 1, 0)
    return _call(
        body, "attn_fwd", (nb,),
        [pl.BlockSpec((BLOCK, ATTN_W), lambda i: (i, 0)),
         pl.BlockSpec((BLOCK, KV_W), lambda i: (i, 0)),
         pl.BlockSpec((BLOCK, KV_W), lambda i: (prev(i), 0)),
         pl.BlockSpec((BLOCK, KV_W), lambda i: (i, 5)),
         pl.BlockSpec((BLOCK, KV_W), lambda i: (prev(i), 5)),
         pl.BlockSpec((BLOCK, 512), lambda i: (i, 3)),
         pl.BlockSpec((BLOCK, 512), lambda i: (i, 4)),
         pl.BlockSpec(memory_space=pltpu.SMEM)],
        [pl.BlockSpec((BLOCK, ATTN_W), lambda i: (i, 0)),
         pl.BlockSpec((BLOCK, ATTN_W), lambda i: (i, 0)),
         pl.BlockSpec((BLOCK, 128), lambda i: (i, 0))],
        [jax.ShapeDtypeStruct((L, ATTN_W), BF16), jax.ShapeDtypeStruct((L, ATTN_W), BF16),
         jax.ShapeDtypeStruct((L, 128), F32)],
        (q, k, k, proj, proj, proj, proj, sinks), ride=ride)


def _attn_bwd(q, k, proj, sinks, d_o, o, lse, ride):
    L = proj.shape[0]
    nb = L // BLOCK

    def body(q_ref, kc_ref, kp_ref, vc_ref, vp_ref, do_ref, o_ref, lse_ref, sink_ref,
             dq_ref, dk_ref, dv_ref, gs_ref, ck_scr, cv_scr):
        i = pl.program_id(0)

        @pl.when(i == 0)
        def _():
            gs_ref[...] = jnp.zeros_like(gs_ref)
            ck_scr[...] = jnp.zeros_like(ck_scr)
            cv_scr[...] = jnp.zeros_like(cv_scr)

        @pl.when(i == nb)
        def _():
            dk_ref[...] = ck_scr[...]
            dv_ref[...] = cv_scr[...]

        @pl.when(i < nb)
        def _():
            mask = _band_mask2(i > 0, keys_on_rows=True)
            lane = lax.broadcasted_iota(jnp.int32, (1, 128), 1)
            lane2 = lax.broadcasted_iota(jnp.int32, (1, 2 * BLOCK), 1)
            lo = lax.broadcasted_iota(jnp.int32, (2 * BLOCK, 128), 1) < HEAD_DIM
            lse_t = lse_ref[...].T
            prod_all = do_ref[...].astype(F32) * o_ref[...].astype(F32)
            seg = (lax.broadcasted_iota(jnp.int32, (N_HEADS, ATTN_W), 1) // HEAD_DIM
                   == lax.broadcasted_iota(jnp.int32, (N_HEADS, ATTN_W), 0)).astype(BF16)
            prod_hi = prod_all.astype(BF16)
            prod_lo = (prod_all - prod_hi.astype(F32)).astype(BF16)
            delta_t = (lax.dot_general(seg, prod_hi, _NT, preferred_element_type=F32)
                       + lax.dot_general(seg, prod_lo, _NT, preferred_element_type=F32))
            kt = [_half_tiles(jnp.concatenate([kp_ref[:, a * 128:(a + 1) * 128], kc_ref[:, a * 128:(a + 1) * 128]],
                                              axis=0).astype(F32)) for a in range(2)]
            vt = [_half_tiles(jnp.concatenate([vp_ref[:, a * 128:(a + 1) * 128], vc_ref[:, a * 128:(a + 1) * 128]],
                                              axis=0)) for a in range(2)]
            gs = jnp.zeros((1, 128), F32)
            dq_parts = []
            dk_acc = [jnp.zeros((2 * BLOCK, 128), F32) for _ in range(2)]
            dv_acc = [jnp.zeros((2 * BLOCK, 128), F32) for _ in range(2)]
            pairs = []
            for g in range(N_KV):
                k_lo, k_hi, v_lo, v_hi = _group_tiles(g, kt, vt)
                t0, t1 = slice(2 * g * 128, (2 * g + 1) * 128), slice((2 * g + 1) * 128, (2 * g + 2) * 128)
                q2 = jnp.concatenate([q_ref[:, t0], q_ref[:, t1]], axis=0)
                do2_b = jnp.concatenate([do_ref[:, t0], do_ref[:, t1]], axis=0).astype(BF16)
                for half, (kh, vh) in enumerate(((k_lo, v_lo), (k_hi, v_hi))):
                    pairs.append(dict(g=g, half=half, kh=kh, q2=q2, do2_b=do2_b,
                                      s=lax.dot_general(kh, q2, _NT, preferred_element_type=F32),
                                      dp=lax.dot_general(vh, do2_b, _NT, preferred_element_type=F32)))
            for pr in pairs:
                h_top, h_bot = 4 * pr["g"] + pr["half"], 4 * pr["g"] + 2 + pr["half"]
                pick = lambda t: jnp.concatenate([t[h_top:h_top + 1, :], t[h_bot:h_bot + 1, :]], axis=1)
                lse, delta = pick(lse_t), pick(delta_t)
                sink = jnp.where(lane2 < BLOCK, sink_ref[h_top], sink_ref[h_bot])
                p = jnp.exp(jnp.where(mask, pr["s"], -1e30) - lse)
                pr["ds_b"] = (p * (pr["dp"] - delta)).astype(BF16)
                pr["p_b"] = p.astype(BF16)
                gsink = -jnp.exp(sink - lse) * delta
                gs = gs + jnp.where(lane == h_top, jnp.sum(jnp.where(lane2 < BLOCK, gsink, 0.0)), 0.0)
                gs = gs + jnp.where(lane == h_bot, jnp.sum(jnp.where(lane2 >= BLOCK, gsink, 0.0)), 0.0)
            for g in range(N_KV):
                a, b = divmod(g, 2)
                dq2 = jnp.zeros((2 * BLOCK, 128), F32)
                dk_h, dv_h = [], []
                for pr in pairs[2 * g:2 * g + 2]:
                    dq2 = dq2 + lax.dot_general(pr["ds_b"], pr["kh"], _TN, preferred_element_type=F32)
                    dk_h.append(jnp.dot(pr["ds_b"], pr["q2"], preferred_element_type=F32))
                    dv_h.append(jnp.dot(pr["p_b"], pr["do2_b"], preferred_element_type=F32))
                dq_parts += [dq2[:BLOCK], dq2[BLOCK:]]
                for acc, parts in ((dk_acc, dk_h), (dv_acc, dv_h)):
                    t = jnp.where(lo, parts[0], parts[1])
                    t = t + pltpu.roll(t, HEAD_DIM, 1)
                    acc[a] = acc[a] + jnp.where(lo == (b == 0), t, 0.0)
            dq_ref[...] = jnp.concatenate(dq_parts, axis=1)
            dk_full = jnp.concatenate(dk_acc, axis=1)
            dv_full = jnp.concatenate(dv_acc, axis=1)
            dk_ref[...] = ck_scr[...] + dk_full[:BLOCK]
            dv_ref[...] = cv_scr[...] + dv_full[:BLOCK]
            ck_scr[...] = dk_full[BLOCK:]
            cv_scr[...] = dv_full[BLOCK:]
            gs_ref[...] += gs

    cur = lambda i: jnp.minimum(i, nb - 1)
    prev = lambda i: jnp.maximum(jnp.minimum(i, nb - 1) - 1, 0)
    done = lambda i: jnp.maximum(i - 1, 0)
    bs = pl.BlockSpec
    return _call(
        body, "attn_bwd", (nb + 1,),
        [bs((BLOCK, ATTN_W), lambda i: (cur(i), 0)),
         bs((BLOCK, KV_W), lambda i: (cur(i), 0)), bs((BLOCK, KV_W), lambda i: (prev(i), 0)),
         bs((BLOCK, KV_W), lambda i: (cur(i), 5)), bs((BLOCK, KV_W), lambda i: (prev(i), 5)),
         bs((BLOCK, ATTN_W), lambda i: (cur(i), 0)), bs((BLOCK, ATTN_W), lambda i: (cur(i), 0)),
         bs((BLOCK, 128), lambda i: (cur(i), 0)), bs(memory_space=pltpu.SMEM)],
        [bs((BLOCK, ATTN_W), lambda i: (cur(i), 0)),
         bs((BLOCK, KV_W), lambda i: (done(i), 0)), bs((BLOCK, KV_W), lambda i: (done(i), 0)),
         bs((1, 128), lambda i: (0, 0))],
        [jax.ShapeDtypeStruct((L, ATTN_W), F32), jax.ShapeDtypeStruct((L, KV_W), F32),
         jax.ShapeDtypeStruct((L, KV_W), F32), jax.ShapeDtypeStruct((1, 128), F32)],
        (q, k, k, proj, proj, d_o, o, lse, sinks),
        [pltpu.VMEM((BLOCK, KV_W), F32), pltpu.VMEM((BLOCK, KV_W), F32)], ride)


def _qk_prep_bwd(proj, tab, qw, kw, d_q, d_k, d_v, d_za, d_u, d_zs):
    L = proj.shape[0]
    tm = _tile(L, 512)
    z0 = ATTN_W + 2 * KV_W

    def body(q_ref, k_ref, t_ref, qw_ref, kw_ref, dq_ref, dk_ref, dv_ref, dza_ref, du_ref, dzs_ref,
             out_ref, gq_ref, gk_ref):
        i = pl.program_id(0)

        @pl.when(i == 0)
        def _():
            gq_ref[...] = jnp.zeros_like(gq_ref)
            gk_ref[...] = jnp.zeros_like(gk_ref)

        cos, sin = t_ref[:, :128], t_ref[:, 128:]
        gq = jnp.zeros((1, 128), F32)
        gk = jnp.zeros((1, 128), F32)
        for c in range(ATTN_W // 128):
            cs = slice(c * 128, (c + 1) * 128)
            d_raw, gw = _norm_rope_bwd(dq_ref[:, cs] * _SCALE, q_ref[:, cs], qw_ref[...], cos, sin)
            out_ref[:, cs] = d_raw.astype(BF16)
            gq = gq + jnp.sum(gw, axis=0, keepdims=True)
        for c in range(KV_W // 128):
            cs = slice(c * 128, (c + 1) * 128)
            d_raw, gw = _norm_rope_bwd(dk_ref[:, cs], k_ref[:, cs], kw_ref[...], cos, sin)
            out_ref[:, ATTN_W + c * 128:ATTN_W + (c + 1) * 128] = d_raw.astype(BF16)
            gk = gk + jnp.sum(gw, axis=0, keepdims=True)
        out_ref[:, ATTN_W + KV_W:z0] = dv_ref[...].astype(BF16)
        out_ref[:, z0:z0 + ATTN_W] = dza_ref[...]
        out_ref[:, z0 + ATTN_W:z0 + ATTN_W + SSM_W] = du_ref[...].astype(BF16)
        out_ref[:, z0 + ATTN_W + SSM_W:] = dzs_ref[...]
        gq_ref[...] += gq
        gk_ref[...] += gk

    row = pl.BlockSpec((1, 128), lambda i: (0, 0))
    blk = lambda w, c: pl.BlockSpec((tm, w), lambda i: (i, c))
    return pl.pallas_call(
        body,
        name="qk_prep_bwd",
        grid=(L // tm,),
        in_specs=[blk(ATTN_W, 0), blk(KV_W, 4), blk(256, 0), row, row, blk(ATTN_W, 0), blk(KV_W, 0), blk(KV_W, 0),
                  blk(ATTN_W, 0), blk(SSM_W, 0), blk(SSM_W, 0)],
        out_specs=[blk(IN_W, 0), row, row],
        out_shape=[jax.ShapeDtypeStruct((L, IN_W), BF16), jax.ShapeDtypeStruct((1, 128), F32),
                   jax.ShapeDtypeStruct((1, 128), F32)],
        compiler_params=_cp(("arbitrary",)),
    )(proj, proj, tab, jnp.tile(qw, 2).reshape(1, 128), jnp.tile(kw, 2).reshape(1, 128), d_q, d_k, d_v,
      d_za, d_u, d_zs)


def _cmul(a, b):
    return a[0] * b[0] - a[1] * b[1], a[0] * b[1] + a[1] * b[0]


def _cmul_conj(a, b):
    return a[0] * b[0] + a[1] * b[1], a[1] * b[0] - a[0] * b[1]


def _cadd(a, b):
    return a[0] + b[0], a[1] + b[1]


def _dot3(a, b, dn):
    ah, bh = a.astype(BF16), b.astype(BF16)
    al, bl = (a - ah.astype(F32)).astype(BF16), (b - bh.astype(F32)).astype(BF16)
    d = lambda u, v: lax.dot_general(u, v, dn, preferred_element_type=F32)
    return d(ah, bh) + d(ah, bl) + d(al, bh)


def _s5_discretise(a_re, a_im, ls, cosx, sinx, bt):
    delta = jnp.exp(ls)
    er = jnp.exp(a_re * delta)
    lb = (er * cosx, er * sinx)
    den = a_re * a_re + a_im * a_im
    coef = _cmul_conj((lb[0] - 1.0, lb[1]), (a_re, a_im))
    coef = (coef[0] / den, coef[1] / den)
    return delta, lb, coef, den, _cmul(coef, bt)


def _powers(lb):
    pw = [(jnp.ones_like(lb[0]), jnp.zeros_like(lb[0]))]
    for _ in range(CHUNK):
        pw.append(_cmul(pw[-1], lb))
    return pw


def _block_rows(a, pw, idx):
    blocks = [_cmul(a, pw[i]) for i in idx]
    return (jnp.concatenate([b[0] for b in blocks], axis=-2), jnp.concatenate([b[1] for b in blocks], axis=-2))


def _block_rows_bwd(g, a, pw, idx, g_pw):
    g_a = (jnp.zeros_like(a[0]), jnp.zeros_like(a[0]))
    for j, i in enumerate(idx):
        gj = (g[0][..., j * SSM_H:(j + 1) * SSM_H, :], g[1][..., j * SSM_H:(j + 1) * SSM_H, :])
        g_a = _cadd(g_a, _cmul_conj(gj, pw[i]))
        gp = _cmul_conj(gj, a)
        g_pw[i] = _cadd(g_pw[i], (jnp.sum(gp[0], axis=-2, keepdims=True), jnp.sum(gp[1], axis=-2, keepdims=True)))
    return g_a


_IDX_S = [CHUNK - 1 - s for s in range(CHUNK)]
_IDX_C = list(range(CHUNK + 1))


def _prep_args(p):
    row = lambda t: t.reshape(SSM_G, 1, SSM_P)
    xi = p["a_im"] * jnp.exp(p["log_step"])[:, None]
    return (row(p["a_re"]), row(p["a_im"]), row(jnp.broadcast_to(p["log_step"][:, None], (SSM_G, SSM_P))),
            row(jnp.cos(xi)), row(jnp.sin(xi)), p["b_re"].transpose(0, 2, 1), p["b_im"].transpose(0, 2, 1),
            p["c_re"], p["c_im"])


PREP_GROUPS = 8


def _prep_specs():
    r1 = pl.BlockSpec((PREP_GROUPS, 1, SSM_P), lambda g: (g, 0, 0))
    r16 = pl.BlockSpec((PREP_GROUPS, SSM_H, SSM_P), lambda g: (g, 0, 0))
    return [r1] * 5 + [r16] * 4, r1, r16


def _ssm_prep(p):
    def one_group(q, are, aim, ls, cosx, sinx, btr, bti, cre, cim, mt_ref, s_ref, o_ref, a_ref):
        _, lb, _, _, bb = _s5_discretise(are[q], aim[q], ls[q], cosx[q], sinx[q], (btr[q], bti[q]))
        pw = _powers(lb)
        c = (cre[q], cim[q])
        sc = _block_rows(bb, pw, _IDX_S)
        cl = _block_rows(c, pw, _IDX_C)
        ok = (cl[0][:CW], cl[1][:CW])
        ot = (cl[0][SSM_H:], cl[1][SSM_H:])
        s_ref[q] = jnp.concatenate([sc[0], sc[1]], axis=1).astype(BF16)
        o_ref[q] = jnp.concatenate([ot[0], -ot[1]], axis=1).astype(BF16)
        a_ref[q] = jnp.concatenate([pw[CHUNK][0], pw[CHUNK][1]], axis=1)
        kt = _dot3(jnp.concatenate([bb[0], -bb[1]], axis=1), jnp.concatenate([ok[0], ok[1]], axis=1), _NT)
        lane = lax.broadcasted_iota(jnp.int32, kt.shape, 1)
        for s in range(CHUNK):
            blk = kt if s == 0 else jnp.where(lane >= SSM_H * s, pltpu.roll(kt, SSM_H * s, 1), 0.0)
            mt_ref[q, s * SSM_H:(s + 1) * SSM_H, :] = blk.astype(BF16)

    def body(*refs):
        for q in range(PREP_GROUPS):
            one_group(q, *refs)

    in_specs, r1, _ = _prep_specs()
    g3 = lambda r, c: pl.BlockSpec((PREP_GROUPS, r, c), lambda g: (g, 0, 0))
    return pl.pallas_call(
        body,
        name="ssm_prep",
        grid=(SSM_G // PREP_GROUPS,),
        in_specs=in_specs,
        out_specs=[g3(CW, CW), g3(CW, 2 * SSM_P), g3(CW, 2 * SSM_P), g3(1, 2 * SSM_P)],
        out_shape=[jax.ShapeDtypeStruct((SSM_G, CW, CW), BF16), jax.ShapeDtypeStruct((SSM_G, CW, 2 * SSM_P), BF16),
                   jax.ShapeDtypeStruct((SSM_G, CW, 2 * SSM_P), BF16),
                   jax.ShapeDtypeStruct((SSM_G, 1, 2 * SSM_P), F32)],
        compiler_params=_cp(("parallel",)),
    )(*_prep_args(p))


def _ssm_prep_bwd(p, g_mt, g_scat, g_ocat, g_a16, ride):
    def body(are, aim, ls, cosx, sinx, btr, bti, cre, cim, gmt_ref, gs_ref, go_ref, ga_ref,
             g_are, g_aim, g_ls, g_btr, g_bti, g_cre, g_cim, ga1_scr, gb1_scr):
        lam = (are[...], aim[...])
        bt = (btr[...], bti[...])
        delta, lb, coef, den, bb = _s5_discretise(lam[0], lam[1], ls[...], cosx[...], sinx[...], bt)
        pw = _powers(lb)
        c = (cre[...], cim[...])
        ok = _block_rows(c, pw, _IDX_C[:CHUNK])
        g_pw =[(jnp.zeros_like(lb[0]), jnp.zeros_like(lb[0])) for _ in range(CHUNK + 1)]
        lane = lax.broadcasted_iota(jnp.int32, (SSM_H, CW), 1)
        for q in range(PREP_GROUPS):
            g_kt = gmt_ref[q, :SSM_H, :]
            for s in range(1, CHUNK):
                blk = gmt_ref[q, s * SSM_H:(s + 1) * SSM_H, :]
                g_kt = g_kt + jnp.where(lane < CW - SSM_H * s, pltpu.roll(blk, CW - SSM_H * s, 1), 0.0)
            a1 = jnp.concatenate([bb[0][q], -bb[1][q]], axis=1)
            b1 = jnp.concatenate([ok[0][q], ok[1][q]], axis=1)
            ga1_scr[q] = _dot3(g_kt, b1, _NN)
            gb1_scr[q] = _dot3(g_kt, a1, _TN)
        g_a1, g_b1 = ga1_scr[...], gb1_scr[...]
        g_bb = (g_a1[..., :SSM_P], -g_a1[..., SSM_P:])
        gs = gs_ref[...]
        g_bb = _cadd(g_bb, _block_rows_bwd((gs[..., :SSM_P], gs[..., SSM_P:]), bb, pw, _IDX_S, g_pw))
        go = go_ref[...]
        pad = jnp.zeros_like(go[..., :SSM_H, :SSM_P])
        g_cl = (jnp.concatenate([g_b1[..., :SSM_P], pad], axis=-2) + jnp.concatenate([pad, go[..., :SSM_P]], axis=-2),
                jnp.concatenate([g_b1[..., SSM_P:], pad], axis=-2) - jnp.concatenate([pad, go[..., SSM_P:]], axis=-2))
        g_c = _block_rows_bwd(g_cl, c, pw, _IDX_C, g_pw)
        ga = ga_ref[...]
        g_pw[CHUNK] = _cadd(g_pw[CHUNK], (ga[..., :SSM_P], ga[..., SSM_P:]))
        g_lb = (jnp.zeros_like(lb[0]), jnp.zeros_like(lb[0]))
        for l in range(CHUNK - 1, -1, -1):
            g_lb = _cadd(g_lb, _cmul_conj(g_pw[l + 1], pw[l]))
            g_pw[l] = _cadd(g_pw[l], _cmul_conj(g_pw[l + 1], lb))
        g_bt = _cmul_conj(g_bb, coef)
        gc = _cmul_conj(g_bb, bt)
        g_coef = (jnp.sum(gc[0], axis=-2, keepdims=True), jnp.sum(gc[1], axis=-2, keepdims=True))
        lam_den = (lam[0] / den, lam[1] / den)
        g_lb = _cadd(g_lb, _cmul(g_coef, lam_den))
        t = _cmul(_cmul_conj(g_coef, coef), lam_den)
        g_x = _cmul_conj(g_lb, lb)
        g_are[...] = g_x[0] * delta - t[0]
        g_aim[...] = g_x[1] * delta - t[1]
        g_ls[...] = (g_x[0] * lam[0] + g_x[1] * lam[1]) * delta
        g_btr[...] = g_bt[0]
        g_bti[...] = g_bt[1]
        g_cre[...] = g_c[0]
        g_cim[...] = g_c[1]

    in_specs, r1, r16 = _prep_specs()
    g3 = lambda r, c: pl.BlockSpec((PREP_GROUPS, r, c), lambda g: (g, 0, 0))
    rows = jax.ShapeDtypeStruct((SSM_G, 1, SSM_P), F32)
    mats = jax.ShapeDtypeStruct((SSM_G, SSM_H, SSM_P), F32)
    (g_are, g_aim, g_ls, g_btr, g_bti, g_cre, g_cim), landed = _call(
        body, "ssm_prep_bwd", (SSM_G // PREP_GROUPS,),
        in_specs + [g3(CW, CW), g3(CW, 2 * SSM_P), g3(CW, 2 * SSM_P), g3(1, 2 * SSM_P)],
        [r1] * 3 + [r16] * 4, [rows] * 3 + [mats] * 4, (*_prep_args(p), g_mt, g_scat, g_ocat, g_a16),
        [pltpu.VMEM((PREP_GROUPS, SSM_H, 2 * SSM_P), F32), pltpu.VMEM((PREP_GROUPS, CW, 2 * SSM_P), F32)], ride)
    grads = dict(a_re=g_are.reshape(SSM_G, SSM_P), a_im=g_aim.reshape(SSM_G, SSM_P),
                 log_step=jnp.sum(g_ls.reshape(SSM_G, SSM_P), axis=1),
                 b_re=g_btr.transpose(0, 2, 1), b_im=g_bti.transpose(0, 2, 1), c_re=g_cre, c_im=g_cim)
    return grads, landed


def _cmul_const(xv, ar, ai):
    return xv * ar + pltpu.roll(xv, SSM_P, 1) * ai


def _chunk_scan(inc, a_row, reverse):
    n = inc.shape[0]
    lane = lax.broadcasted_iota(jnp.int32, (1, 2 * SSM_P), 1)
    row = lax.broadcasted_iota(jnp.int32, inc.shape, 0)
    sign = jnp.where(lane < SSM_P, -1.0, 1.0)
    ar = jnp.where(lane < SSM_P, a_row, pltpu.roll(a_row, SSM_P, 1))
    ai = jnp.where(lane < SSM_P, pltpu.roll(a_row, SSM_P, 1), a_row)
    if reverse:
        ai = -ai
    xv = inc
    s = 1
    while s < n:
        if reverse:
            sh = jnp.where(row < n - s, pltpu.roll(xv, n - s, 0), 0.0)
        else:
            sh = jnp.where(row >= s, pltpu.roll(xv, s, 0), 0.0)
        xv = xv + _cmul_const(sh, ar, ai * sign)
        ar, ai = ar * ar - ai * ai, 2.0 * ar * ai
        s *= 2
    return xv


def _shift_rows(xv, reverse):
    n = xv.shape[0]
    row = lax.broadcasted_iota(jnp.int32, xv.shape, 0)
    if reverse:
        return jnp.where(row < n - 1, pltpu.roll(xv, n - 1, 0), 0.0)
    return jnp.where(row >= 1, pltpu.roll(xv, 1, 0), 0.0)


GB = 128 // SSM_H
U_COL0 = (ATTN_W + 2 * KV_W + ATTN_W) // 128


HALF = CHUNK // 2


def _chunk_perm():
    r = jnp.arange(HALF * 128)
    t, g8, h = r // 128, (r % 128) // SSM_H, r % SSM_H
    return ((g8 * 128 + t * SSM_H + h)[:, None] == jnp.arange(GB * 128)[None, :]).astype(BF16)


def _load_perm(p_hbm, p_scr, sem):
    @pl.when(pl.program_id(0) == 0)
    def _():
        cp = pltpu.make_async_copy(p_hbm, p_scr, sem)
        cp.start()
        cp.wait()


def _rows_to_chunks(pieces, perm):
    halves = [jnp.dot(jnp.concatenate(pieces[k * HALF:(k + 1) * HALF], axis=1).astype(BF16), perm,
                      preferred_element_type=F32).astype(BF16) for k in range(2)]
    return [jnp.concatenate([hv[:, g * 128:(g + 1) * 128] for hv in halves], axis=1) for g in range(GB)]


def _chunks_to_rows(groups, perm, two_pass):
    pieces = []
    for k in range(2):
        v = jnp.concatenate([gv[:, k * 128:(k + 1) * 128] for gv in groups], axis=1)
        hi = v.astype(BF16)
        out = lax.dot_general(hi, perm, _NT, preferred_element_type=F32)
        if two_pass:
            lo = (v - hi.astype(F32)).astype(BF16)
            out = out + lax.dot_general(lo, perm, _NT, preferred_element_type=F32)
        pieces += [out[:, t * 128:(t + 1) * 128] for t in range(HALF)]
    return pieces


def _ssm_fwd(proj, perm, mt, scat, ocat, a16, d_skip, ride):
    L = proj.shape[0]
    nc = L // CHUNK

    def body(u_ref, p_hbm, mt_ref, s_ref, o_ref, a_ref, d_ref, y_ref, yg_ref, h_ref, p_scr, sem):
        _load_perm(p_hbm, p_scr, sem)
        perm = p_scr[...]
        rows = [pl.ds(t, nc, stride=CHUNK) for t in range(CHUNK)]
        us = [u_ref[r, :] for r in rows]
        ua = _rows_to_chunks(us, perm)
        incs = [jnp.dot(ua[g], s_ref[g], preferred_element_type=F32) for g in range(GB)]
        intra = [jnp.dot(ua[g], mt_ref[g], preferred_element_type=F32) for g in range(GB)]
        hxs = [_shift_rows(_chunk_scan(incs[g], a_ref[g], False), False) for g in range(GB)]
        ys = []
        for g in range(GB):
            h_ref[g] = hxs[g]
            ys.append(intra[g] + lax.dot_general(hxs[g].astype(BF16), o_ref[g], _NT, preferred_element_type=F32))
        yp = _chunks_to_rows(ys, perm, True)
        for t, r in enumerate(rows):
            y = yp[t] + d_ref[...] * us[t]
            y_ref[r, :] = y
            yg_ref[r, :] = _gelu(y)

    g3 = lambda r, c: pl.BlockSpec((GB, r, c), lambda g: (g, 0, 0))
    col = pl.BlockSpec((L, 128), lambda g: (0, g))
    return _call(
        body, "ssm_fwd", (SSM_G // GB,),
        [pl.BlockSpec((L, 128), lambda g: (0, U_COL0 + g)), _ANY,
         g3(CW, CW), g3(CW, 2 * SSM_P), g3(CW, 2 * SSM_P), g3(1, 2 * SSM_P),
         pl.BlockSpec((1, 128), lambda g: (0, g))],
        [col, col, g3(nc, 2 * SSM_P)],
        [jax.ShapeDtypeStruct((L, SSM_W), F32), jax.ShapeDtypeStruct((L, SSM_W), F32),
         jax.ShapeDtypeStruct((SSM_G, nc, 2 * SSM_P), F32)],
        (proj, perm, mt, scat, ocat, a16, d_skip.reshape(1, SSM_W)),
        [pltpu.VMEM((HALF * 128, GB * 128), BF16), pltpu.SemaphoreType.DMA], ride)


def _ssm_bwd(d_yg, y, proj, hx, perm, mt, scat, ocat, a16, d_skip, ride):
    L = proj.shape[0]
    nc = L // CHUNK

    def body(dg_ref, y_ref, u_ref, h_ref, p_hbm, mt_ref, s_ref, o_ref, a_ref, d_ref,
             du_ref, gmt_ref, gs_ref, go_ref, ga_ref, gd_ref, p_scr, sem):
        _load_perm(p_hbm, p_scr, sem)
        perm = p_scr[...]
        rows = [pl.ds(t, nc, stride=CHUNK) for t in range(CHUNK)]
        us = [u_ref[r, :] for r in rows]
        dys = [dg_ref[r, :] * _dgelu(y_ref[r, :]) for r in rows]
        gd = jnp.zeros((1, 128), F32)
        for uv, dy in zip(us, dys):
            gd = gd + jnp.sum(dy * uv, axis=0, keepdims=True)
        gd_ref[...] = gd
        ua = _rows_to_chunks(us, perm)
        dya = _rows_to_chunks(dys, perm)
        lane = lax.broadcasted_iota(jnp.int32, (1, 2 * SSM_P), 1)
        dhs = [jnp.dot(dya[g], o_ref[g], preferred_element_type=F32) for g in range(GB)]
        intra = [lax.dot_general(dya[g], mt_ref[g], _NT, preferred_element_type=F32) for g in range(GB)]
        for g in range(GB):
            gmt_ref[g] = lax.dot_general(ua[g], dya[g], _TN, preferred_element_type=F32)
            go_ref[g] = lax.dot_general(dya[g], h_ref[g].astype(BF16), _TN, preferred_element_type=F32)
        dincs = [_shift_rows(_chunk_scan(dhs[g], a_ref[g], True), True) for g in range(GB)]
        dus = []
        for g in range(GB):
            dinc, hx_v = dincs[g], h_ref[g]
            dinc_b = dinc.astype(BF16)
            dus.append(intra[g] + lax.dot_general(dinc_b, s_ref[g], _NT, preferred_element_type=F32))
            gs_ref[g] = lax.dot_general(ua[g], dinc_b, _TN, preferred_element_type=F32)
            p1 = dinc * hx_v
            p2 = pltpu.roll(dinc, SSM_P, 1) * hx_v
            t1 = jnp.sum(p1 + pltpu.roll(p1, SSM_P, 1), axis=0, keepdims=True)
            t2 = jnp.sum(p2 - pltpu.roll(p2, SSM_P, 1), axis=0, keepdims=True)
            ga_ref[g] = jnp.where(lane < SSM_P, t1, pltpu.roll(t2, SSM_P, 1))
        dup = _chunks_to_rows(dus, perm, False)
        for t, r in enumerate(rows):
            du_ref[r, :] = dup[t] + d_ref[...] * dys[t]

    g3 = lambda r, c: pl.BlockSpec((GB, r, c), lambda g: (g, 0, 0))
    col = pl.BlockSpec((L, 128), lambda g: (0, g))
    row = pl.BlockSpec((1, 128), lambda g: (0, g))
    return _call(
        body, "ssm_bwd", (SSM_G // GB,),
        [col, col, pl.BlockSpec((L, 128), lambda g: (0, U_COL0 + g)), g3(nc, 2 * SSM_P), _ANY,
         g3(CW, CW), g3(CW, 2 * SSM_P), g3(CW, 2 * SSM_P), g3(1, 2 * SSM_P), row],
        [col, g3(CW, CW), g3(CW, 2 * SSM_P), g3(CW, 2 * SSM_P), g3(1, 2 * SSM_P), row],
        [jax.ShapeDtypeStruct((L, SSM_W), F32), jax.ShapeDtypeStruct((SSM_G, CW, CW), F32),
         jax.ShapeDtypeStruct((SSM_G, CW, 2 * SSM_P), F32), jax.ShapeDtypeStruct((SSM_G, CW, 2 * SSM_P), F32),
         jax.ShapeDtypeStruct((SSM_G, 1, 2 * SSM_P), F32), jax.ShapeDtypeStruct((1, SSM_W), F32)],
        (d_yg, y, proj, hx, perm, mt, scat, ocat, a16, d_skip.reshape(1, SSM_W)),
        [pltpu.VMEM((HALF * 128, GB * 128), BF16), pltpu.SemaphoreType.DMA], ride)


def _merge(og, yg, w_glu, proj, b_glu, wa, ws):
    L = og.shape[0]
    tm = _tile(L, 256)

    def body(og_ref, yg_ref, wg_ref, z0_ref, z1_ref, b_ref, wa_ref, ws_ref, m_ref, gp_ref):
        zs = jnp.concatenate([z0_ref[...], z1_ref[...]], axis=1)
        ygv = yg_ref[...]
        gpre = jnp.dot(ygv.astype(BF16), wg_ref[...], preferred_element_type=F32)
        gp_ref[...] = gpre
        os_ = ygv * _sigmoid(gpre + b_ref[...]) * _silu(zs)
        ogv = og_ref[...].astype(F32)
        ra = lax.rsqrt(jnp.mean(ogv * ogv, axis=-1, keepdims=True) + NORM_EPS)
        rs = lax.rsqrt(jnp.mean(os_ * os_, axis=-1, keepdims=True) + NORM_EPS)
        m_ref[:, :ATTN_W] = (ogv * ra * wa_ref[...]).astype(BF16)
        m_ref[:, ATTN_W:] = (os_ * rs * ws_ref[...]).astype(BF16)

    row = lambda w: pl.BlockSpec((1, w), lambda i: (0, 0))
    return pl.pallas_call(
        body,
        name="merge",
        grid=(L // tm,),
        in_specs=[pl.BlockSpec((tm, ATTN_W), lambda i: (i, 0)), pl.BlockSpec((tm, SSM_W), lambda i: (i, 0)),
                  pl.BlockSpec((SSM_W, SSM_W), lambda i: (0, 0)),
                  pl.BlockSpec((tm, 512), lambda i: (i, 7)), pl.BlockSpec((tm, 512), lambda i: (i, 8)),
                  row(SSM_W), row(ATTN_W), row(SSM_W)],
        out_specs=[pl.BlockSpec((tm, D_MODEL), lambda i: (i, 0)), pl.BlockSpec((tm, SSM_W), lambda i: (i, 0))],
        out_shape=[jax.ShapeDtypeStruct((L, D_MODEL), BF16), jax.ShapeDtypeStruct((L, SSM_W), F32)],
        compiler_params=_cp(("parallel",)),
    )(og, yg, w_glu, proj, proj, b_glu.reshape(1, SSM_W), wa.reshape(1, ATTN_W), ws.reshape(1, SSM_W))


def _outproj_loss(merged, w_out, x, target):
    L = x.shape[0]
    tm, tn = _tile(L, 256), D_MODEL
    ni, nj = L // tm, D_MODEL // tn

    def body(m_ref, w_ref, x_ref, t_ref, d_ref, db_ref, l_ref):
        out = x_ref[...] + jnp.dot(m_ref[...], w_ref[...], preferred_element_type=F32)
        diff = out - t_ref[...]
        d = diff * (1.0 / D_MODEL)
        d_ref[...] = d
        db_ref[...] = d.astype(BF16)
        l_ref[...] = jnp.full((1, 8, 128), jnp.sum(diff * diff), F32)

    return pl.pallas_call(
        body,
        name="outproj_loss",
        grid=(nj, ni),
        in_specs=[pl.BlockSpec((tm, D_MODEL), lambda j, i: (i, 0)),
                  pl.BlockSpec((D_MODEL, tn), lambda j, i: (0, j)),
                  pl.BlockSpec((tm, tn), lambda j, i: (i, j)),
                  pl.BlockSpec((tm, tn), lambda j, i: (i, j))],
        out_specs=[pl.BlockSpec((tm, tn), lambda j, i: (i, j)), pl.BlockSpec((tm, tn), lambda j, i: (i, j)),
                   pl.BlockSpec((1, 8, 128), lambda j, i: (i * nj + j, 0, 0))],
        out_shape=[jax.ShapeDtypeStruct((L, D_MODEL), F32), jax.ShapeDtypeStruct((L, D_MODEL), BF16),
                   jax.ShapeDtypeStruct((ni * nj, 8, 128), F32)],
        compiler_params=_cp(("parallel", "parallel")),
    )(merged, w_out, x, target)


def _merge_bwd(d_out_b, w_out, og, o, yg, gpre, proj, b_glu, wa, ws):
    L = og.shape[0]
    tm = _tile(L, 256)

    def body(dout_ref, wo_ref, og_ref, o_ref, yg_ref, gp_ref, za0_ref, za1_ref, zs0_ref, zs1_ref, b_ref, wa_ref,
             ws_ref, do_ref, dza_ref, dzs_ref, dg_ref, dyg_ref, gwa_ref, gws_ref, gb_ref):
        i = pl.program_id(0)

        @pl.when(i == 0)
        def _():
            gwa_ref[...] = jnp.zeros_like(gwa_ref)
            gws_ref[...] = jnp.zeros_like(gws_ref)
            gb_ref[...] = jnp.zeros_like(gb_ref)

        dm = lax.dot_general(dout_ref[...], wo_ref[...], _NT, preferred_element_type=F32)
        za = jnp.concatenate([za0_ref[...], za1_ref[...]], axis=1)
        zs = jnp.concatenate([zs0_ref[...], zs1_ref[...]], axis=1)
        ogv, dma = og_ref[...].astype(F32), dm[:, :ATTN_W]
        ra = lax.rsqrt(jnp.mean(ogv * ogv, axis=-1, keepdims=True) + NORM_EPS)
        xh = ogv * ra
        gwa_ref[...] += jnp.sum(dma * xh, axis=0, keepdims=True)
        gx = dma * wa_ref[...]
        d_og = ra * (gx - xh * jnp.mean(gx * xh, axis=-1, keepdims=True))
        do_ref[...] = (d_og * _silu(za)).astype(BF16)
        dza_ref[...] = (d_og * o_ref[...].astype(F32) * _dsilu(za)).astype(BF16)
        ygv = yg_ref[...]
        sg = _sigmoid(gp_ref[...] + b_ref[...])
        y2 = ygv * sg
        sz = _silu(zs)
        os_ = y2 * sz
        dms = dm[:, ATTN_W:]
        rs = lax.rsqrt(jnp.mean(os_ * os_, axis=-1, keepdims=True) + NORM_EPS)
        xs = os_ * rs
        gws_ref[...] += jnp.sum(dms * xs, axis=0, keepdims=True)
        gxs = dms * ws_ref[...]
        d_os = rs * (gxs - xs * jnp.mean(gxs * xs, axis=-1, keepdims=True))
        dzs_ref[...] = (d_os * y2 * _dsilu(zs)).astype(BF16)
        d_y2 = d_os * sz
        d_g = d_y2 * ygv * sg * (1.0 - sg)
        dg_ref[...] = d_g.astype(BF16)
        gb_ref[...] += jnp.sum(d_g, axis=0, keepdims=True)
        dyg_ref[...] = d_y2 * sg

    row = lambda w: pl.BlockSpec((1, w), lambda i: (0, 0))
    full = lambda w: pl.BlockSpec((tm, w), lambda i: (i, 0))
    half = lambda c: pl.BlockSpec((tm, 512), lambda i: (i, c))
    return pl.pallas_call(
        body,
        name="merge_bwd",
        grid=(L // tm,),
        in_specs=[full(D_MODEL), pl.BlockSpec((D_MODEL, D_MODEL), lambda i: (0, 0)),
                  full(ATTN_W), full(ATTN_W), full(SSM_W), full(SSM_W),
                  half(3), half(4), half(7), half(8), row(SSM_W), row(ATTN_W), row(SSM_W)],
        out_specs=[full(ATTN_W), full(ATTN_W), full(SSM_W), full(SSM_W), full(SSM_W),
                   row(ATTN_W), row(SSM_W), row(SSM_W)],
        out_shape=[jax.ShapeDtypeStruct((L, ATTN_W), BF16), jax.ShapeDtypeStruct((L, ATTN_W), BF16),
                   jax.ShapeDtypeStruct((L, SSM_W), BF16), jax.ShapeDtypeStruct((L, SSM_W), BF16),
                   jax.ShapeDtypeStruct((L, SSM_W), F32),
                   jax.ShapeDtypeStruct((1, ATTN_W), F32), jax.ShapeDtypeStruct((1, SSM_W), F32),
                   jax.ShapeDtypeStruct((1, SSM_W), F32)],
        compiler_params=_cp(("arbitrary",)),
    )(d_out_b, w_out, og, o, yg, gpre, proj, proj, proj, proj, b_glu.reshape(1, SSM_W), wa.reshape(1, ATTN_W),
      ws.reshape(1, SSM_W))


def _rms_bwd_x(x, norm_w, d_hn, d_out, ride):
    L = x.shape[0]
    tm = _tile(L, 256)

    def body(x_ref, w_ref, dh_ref, do_ref, gx_ref, gw_ref):
        i = pl.program_id(0)

        @pl.when(i == 0)
        def _():
            gw_ref[...] = jnp.zeros_like(gw_ref)

        xv, dh = x_ref[...], dh_ref[...]
        r = lax.rsqrt(jnp.mean(xv * xv, axis=-1, keepdims=True) + NORM_EPS)
        xh = xv * r
        gw_ref[...] += jnp.sum(dh * xh, axis=0, keepdims=True)
        gx = dh * w_ref[...]
        gx_ref[...] = do_ref[...] + r * (gx - xh * jnp.mean(gx * xh, axis=-1, keepdims=True))

    blk = pl.BlockSpec((tm, D_MODEL), lambda i: (i, 0))
    row = pl.BlockSpec((1, D_MODEL), lambda i: (0, 0))
    return _call(body, "rms_bwd_x", (L // tm,), [blk, row, blk, blk], [blk, row],
                 [jax.ShapeDtypeStruct((L, D_MODEL), F32), jax.ShapeDtypeStruct((1, D_MODEL), F32)],
                 (x, norm_w.reshape(1, D_MODEL), d_hn, d_out), ride=ride)


def _rope_table(positions):
    inv_freq = ROPE_THETA ** (-jnp.arange(0, HEAD_DIM, 2, dtype=F32) / HEAD_DIM)
    ang = positions.astype(F32)[:, None] * inv_freq
    sign = jnp.where(jnp.arange(128) % HEAD_DIM < HEAD_DIM // 2, -1.0, 1.0)
    return jnp.concatenate([jnp.tile(jnp.cos(ang), (1, 4)), jnp.tile(jnp.sin(ang), (1, 4)) * sign], axis=1)


def _step(x, positions, target, w, core, chip):
    small = {n: w[n] for n in _SMALL}
    tab = _rope_table(positions)
    mt_b, scat_b, ocat_b, a16 = _ssm_prep(small)
    perm = _chunk_perm()
    blocks = lambda t: t.reshape(N_DEV, t.shape[0] // N_DEV, t.shape[1])

    proj, hn, wt_in = _rms_inproj_gather(x, small["norm_w"], w["w_in"].T.astype(BF16), chip)
    wt_in = wt_in.reshape(IN_W, D_MODEL)
    q_rot, k_rot = _qk_prep(proj, tab, small["q_norm_w"], small["k_norm_w"])
    (og, o, lse), (w_glu,) = _attn_fwd(q_rot, k_rot, proj, small["sinks"],
                                       _gather_exchange([w["w_glu"].astype(BF16)]))
    (y, yg, hx), (w_out,) = _ssm_fwd(proj, perm, mt_b, scat_b, ocat_b, a16, small["d_skip"],
                                     _gather_exchange([w["w_out"].astype(BF16)]))
    w_glu, w_out = w_glu.reshape(SSM_W, SSM_W), w_out.reshape(D_MODEL, D_MODEL)
    merged, gpre = _merge(og, yg, w_glu, proj, small["b_glu"], small["attn_out_norm_w"], small["ssm_out_norm_w"])
    d_out, d_out_b, loss_parts = _outproj_loss(merged, w_out, x, target)
    loss = 0.5 * jnp.sum(loss_parts[:, 0, 0]) / D_MODEL

    g_w_out = blocks(_mm(merged, d_out_b, "tn", F32, "grad_w_out", tm=1024))
    d_o, d_za, d_zs, d_g, d_yg1, g_wa, g_ws, g_bglu = _merge_bwd(
        d_out_b, w_out, og, o, yg, gpre, proj, small["b_glu"], small["attn_out_norm_w"], small["ssm_out_norm_w"])
    g_w_glu = blocks(_mm(yg, d_g, "tn", F32, "grad_w_glu"))
    d_yg = _mm(d_g, w_glu, "nt", F32, "d_yg", add=d_yg1)
    (d_u, g_mt, g_scat, g_ocat, g_a16, g_dskip), (ra_out, ra_glu) = _ssm_bwd(
        d_yg, y, proj, hx, perm, mt_b, scat_b, ocat_b, a16, small["d_skip"], _pair_exchange([g_w_out, g_w_glu]))
    p_out = _pair_sum(g_w_out, ra_out, core, BF16, "pair_sum_out")
    p_glu = _pair_sum(g_w_glu, ra_glu, core, BF16, "pair_sum_glu")
    (d_q, d_k, d_v, g_sinks), (rb_out, rb_glu) = _attn_bwd(
        q_rot, k_rot, proj, small["sinks"], d_o, o, lse, _chip_exchange([p_out, p_glu]))
    d_proj, g_qw, g_kw = _qk_prep_bwd(proj, tab, small["q_norm_w"], small["k_norm_w"], d_q, d_k, d_v,
                                      d_za, d_u, d_zs)
    g_qw = g_qw[0, :HEAD_DIM] + g_qw[0, HEAD_DIM:]
    g_kw = g_kw[0, :HEAD_DIM] + g_kw[0, HEAD_DIM:]
    g_in_a = blocks(_mm(d_proj, hn, "tn", F32, "grad_w_in_a", tm=1152, panel=0))
    g_in_b, (ra_a,) = _mm(d_proj, hn, "tn", F32, "grad_w_in_b", tm=1152, panel=1, ride=_pair_exchange([g_in_a]))
    g_in_b = blocks(g_in_b)
    p_a = _pair_sum(g_in_a, ra_a, core, BF16, "pair_sum_in_a")
    d_hn, (rb_a, ra_b) = _mm(d_proj, wt_in, "nn", F32, "d_hn", tm=1024,
                             ride=_both(_chip_exchange([p_a]), _pair_exchange([g_in_b])))
    p_b = _pair_sum(g_in_b, ra_b, core, BF16, "pair_sum_in_b")
    g_small, (rb_b,) = _ssm_prep_bwd(small, g_mt, g_scat, g_ocat, g_a16, _chip_exchange([p_b]))
    (grad_x, g_nw), _ = _rms_bwd_x(x, small["norm_w"], d_hn, d_out, None)

    g_small.update(norm_w=g_nw.reshape(-1), q_norm_w=g_qw.reshape(-1), k_norm_w=g_kw.reshape(-1),
                   sinks=g_sinks[0, :N_HEADS], d_skip=g_dskip.reshape(-1), b_glu=g_bglu.reshape(-1),
                   attn_out_norm_w=g_wa.reshape(-1), ssm_out_norm_w=g_ws.reshape(-1))
    g_packed = _slab_all_reduce(_pack(g_small, loss).reshape(N_DEV, _PACK_ROWS // N_DEV, 128))
    g_packed = g_packed.reshape(_PACK_ROWS, 128)
    grads = _unpack(g_packed, w)
    parts = dict(w_in=([p_a, p_b], [rb_a, rb_b]), w_glu=([p_glu], [rb_glu]), w_out=([p_out], [rb_out]))
    return g_packed[_LOSS_ROW, 0], grad_x, grads, parts


_ANY = pl.BlockSpec(memory_space=pl.ANY)


class _Exchange:
    def __init__(self, arrays, out_shape, sems, start, finish, relay=None):
        self.arrays, self.out_shape, self.sems, self.start, self.finish = arrays, out_shape, sems, start, finish
        self.relay = relay if relay is not None else (lambda ins, outs, sems: None)


def _gather_exchange(blocks):
    n = len(blocks)

    def parts(ins, outs, sems):
        send_sems, recv_sems, local_sems = sems
        x, y, c = lax.axis_index("x"), lax.axis_index("y"), lax.axis_index("c")
        me, sibling = (x, y, c), (x, y, 1 - c)
        chips = [(1 - x, y), (x, 1 - y), (1 - x, 1 - y)]

        def slot(k, dev):
            return outs[k].at[4 * dev[0] + 2 * dev[1] + dev[2]]

        def copy(k, q, block, to, src=None):
            return pltpu.make_async_remote_copy(
                src_ref=slot(k, block) if src is None else src, dst_ref=slot(k, block),
                send_sem=send_sems.at[k, q], recv_sem=recv_sems.at[k, q], device_id=to, device_id_type=MESH)

        mine = [pltpu.make_async_copy(ins[k], slot(k, me), local_sems.at[k]) for k in range(n)]
        first = []
        for k in range(n):
            first.append(copy(k, 0, me, sibling, src=ins[k]))
            first += [copy(k, 1 + j, me, (*chip, c), src=ins[k]) for j, chip in enumerate(chips)]
        return me, sibling, chips, c, copy, mine, first

    def start(ins, outs, sems):
        *_, mine, first = parts(ins, outs, sems)
        for cp in mine + first:
            cp.start()

    def relay(ins, outs, sems):
        me, sibling, chips, c, copy, _, _ = parts(ins, outs, sems)
        for j, chip in enumerate(chips):
            for k in range(n):
                copy(k, 1 + j, (*chip, c), me).wait_recv()
                copy(k, 4 + j, (*chip, c), sibling).start()

    def finish(ins, outs, sems):
        me, sibling, chips, c, copy, mine, first = parts(ins, outs, sems)
        for k in range(n):
            copy(k, 0, sibling, me).wait_recv()
            for j, chip in enumerate(chips):
                copy(k, 4 + j, (*chip, 1 - c), me).wait_recv()
        for cp in first + [copy(k, 4 + j, (*chip, c), sibling) for k in range(n) for j, chip in enumerate(chips)]:
            cp.wait_send()
        for cp in mine:
            cp.wait()

    return _Exchange(blocks, [jax.ShapeDtypeStruct((N_DEV,) + b.shape, b.dtype) for b in blocks],
                     [pltpu.SemaphoreType.DMA((n, 7)), pltpu.SemaphoreType.DMA((n, 7)), pltpu.SemaphoreType.DMA((n,))],
                     start, finish, relay)


def _direct_exchange(arrays, out_lead, fan, route):
    n = len(arrays)

    def copies(ins, outs, sems):
        send_sems, recv_sems = sems
        legs = route(lax.axis_index("x"), lax.axis_index("y"), lax.axis_index("c"))
        return [pltpu.make_async_remote_copy(
            src_ref=ins[k].at[src], dst_ref=outs[k].at[q], send_sem=send_sems.at[k, q], recv_sem=recv_sems.at[k, q],
            device_id=to, device_id_type=MESH) for k in range(n) for src, q, to in legs]

    def start(ins, outs, sems):
        for cp in copies(ins, outs, sems):
            cp.start()

    def finish(ins, outs, sems):
        for cp in copies(ins, outs, sems):
            cp.wait()

    return _Exchange(arrays, [jax.ShapeDtypeStruct((out_lead,) + a.shape[1:], a.dtype) for a in arrays],
                     [pltpu.SemaphoreType.DMA((n, fan)), pltpu.SemaphoreType.DMA((n, fan))], start, finish)


def _pair_exchange(grads):
    return _direct_exchange(grads, 4, 4, lambda x, y, c: [(2 * chip + (1 - c), chip, (x, y, 1 - c))
                                                          for chip in range(4)])


def _chip_exchange(parts):
    def route(x, y, c):
        chips = [(1 - x, y), (x, 1 - y), (1 - x, 1 - y)]
        return [(2 * chip[0] + chip[1], q, (*chip, c)) for q, chip in enumerate(chips)]
    return _direct_exchange(parts, 3, 3, route)


def _both(ex1, ex2):
    n1, s1 = len(ex1.arrays), len(ex1.sems)

    def halves(ins, outs, sems):
        return (ins[:n1], outs[:n1], sems[:s1]), (ins[n1:], outs[n1:], sems[s1:])

    def start(ins, outs, sems):
        h1, h2 = halves(ins, outs, sems)
        ex1.start(*h1)
        ex2.start(*h2)

    def relay(ins, outs, sems):
        h1, h2 = halves(ins, outs, sems)
        ex1.relay(*h1)
        ex2.relay(*h2)

    def finish(ins, outs, sems):
        h1, h2 = halves(ins, outs, sems)
        ex1.finish(*h1)
        ex2.finish(*h2)

    return _Exchange(list(ex1.arrays) + list(ex2.arrays), list(ex1.out_shape) + list(ex2.out_shape),
                     list(ex1.sems) + list(ex2.sems), start, finish, relay)


def _call(body, name, grid, in_specs, out_specs, out_shape, args, scratch_shapes=(), ride=None):
    if ride is None:
        sem = ("arbitrary",) * len(grid)
        return pl.pallas_call(body, name=name, grid=grid, in_specs=in_specs, out_specs=out_specs, out_shape=out_shape,
                              scratch_shapes=list(scratch_shapes), compiler_params=_cp(sem))(*args), None
    n_in, n_out, n_scr, n_x = len(in_specs), len(out_specs), len(scratch_shapes), len(ride.arrays)

    def wrapped(*refs):
        ins, refs = refs[:n_in], refs[n_in:]
        x_in, refs = refs[:n_x], refs[n_x:]
        outs, refs = refs[:n_out], refs[n_out:]
        x_out, refs = refs[:n_x], refs[n_x:]
        scr, sems = refs[:n_scr], refs[n_scr:]
        step, total = pl.program_id(0), grid[0]
        for a in range(1, len(grid)):
            step, total = step * grid[a] + pl.program_id(a), total * grid[a]
        @pl.when(step == 0)
        def _():
            ride.start(x_in, x_out, sems)

        @pl.when(step == max(total - 2, 0))
        def _():
            ride.relay(x_in, x_out, sems)

        body(*ins, *outs, *scr)

        @pl.when(step == total - 1)
        def _():
            ride.finish(x_in, x_out, sems)

    res = pl.pallas_call(
        wrapped, name=name, grid=grid, in_specs=list(in_specs) + [_ANY] * n_x,
        out_specs=list(out_specs) + [_ANY] * n_x, out_shape=list(out_shape) + list(ride.out_shape),
        scratch_shapes=list(scratch_shapes) + list(ride.sems),
        compiler_params=_cp(("arbitrary",) * len(grid)))(*args, *ride.arrays)
    return res[:n_out], list(res[n_out:])


def _pair_sum(g, ra, core, out_dtype, name):
    _, r, C = g.shape
    tr = _tile(r, 576)

    def body(c_ref, g_ref, ra_ref, p_ref):
        p_ref[...] = (g_ref[...] + ra_ref[...]).astype(p_ref.dtype)

    return pl.pallas_call(
        body,
        name=name,
        grid_spec=pltpu.PrefetchScalarGridSpec(
            num_scalar_prefetch=1,
            grid=(4, r // tr),
            in_specs=[pl.BlockSpec((1, tr, C), lambda j, t, c_ref: (2 * j + c_ref[0], t, 0)),
                      pl.BlockSpec((1, tr, C), lambda j, t, c_ref: (j, t, 0))],
            out_specs=pl.BlockSpec((1, tr, C), lambda j, t, c_ref: (j, t, 0)),
        ),
        out_shape=jax.ShapeDtypeStruct((4, r, C), out_dtype),
        compiler_params=_cp(("parallel", "parallel")),
    )(core, g, ra)


def _slab_all_reduce(slab):
    _, r, lanes = slab.shape

    def body(s_ref, o_ref, ra, rb, ps, sems_a, sems_b, sems_c):
        x, y, c = lax.axis_index("x"), lax.axis_index("y"), lax.axis_index("c")
        chips = [(1 - x, y), (x, 1 - y), (1 - x, 1 - y)]
        pair = [pltpu.make_async_remote_copy(
            src_ref=s_ref.at[2 * k + (1 - c)], dst_ref=ra.at[k], send_sem=sems_a.at[0, k], recv_sem=sems_a.at[1, k],
            device_id=(x, y, 1 - c), device_id_type=MESH) for k in range(4)]
        for cp in pair:
            cp.start()
        for cp in pair:
            cp.wait()
        for k in range(4):
            ps[k] = s_ref[2 * k + c] + ra[k]
        cross = [pltpu.make_async_remote_copy(
            src_ref=ps.at[2 * ch[0] + ch[1]], dst_ref=rb.at[q], send_sem=sems_b.at[0, q], recv_sem=sems_b.at[1, q],
            device_id=(*ch, c), device_id_type=MESH) for q, ch in enumerate(chips)]
        for cp in cross:
            cp.start()
        for cp in cross:
            cp.wait()
        me = 4 * x + 2 * y + c
        o_ref[me] = ((ps[2 * x + y] + rb[0]) + rb[1]) + rb[2]
        flips = [(dx, dy, dc) for dx in (0, 1) for dy in (0, 1) for dc in (0, 1) if dx + dy + dc]
        spread = [pltpu.make_async_remote_copy(
            src_ref=o_ref.at[me], dst_ref=o_ref.at[me], send_sem=sems_c.at[0, q], recv_sem=sems_c.at[1, q],
            device_id=(x + dx - 2 * x * dx, y + dy - 2 * y * dy, c + dc - 2 * c * dc), device_id_type=MESH)
            for q, (dx, dy, dc) in enumerate(flips)]
        for cp in spread:
            cp.start()
        for q, (dx, dy, dc) in enumerate(flips):
            peer = 4 * (x + dx - 2 * x * dx) + 2 * (y + dy - 2 * y * dy) + (c + dc - 2 * c * dc)
            pltpu.make_async_remote_copy(
                src_ref=o_ref.at[peer], dst_ref=o_ref.at[peer], send_sem=sems_c.at[0, q], recv_sem=sems_c.at[1, q],
                device_id=(x, y, c), device_id_type=MESH).wait_recv()
        for cp in spread:
            cp.wait_send()

    whole = pl.BlockSpec(memory_space=pltpu.VMEM)
    return pl.pallas_call(
        body, name="slab_all_reduce", in_specs=[whole], out_specs=whole,
        out_shape=jax.ShapeDtypeStruct(slab.shape, F32),
        scratch_shapes=[pltpu.VMEM((4, r, lanes), F32), pltpu.VMEM((3, r, lanes), F32), pltpu.VMEM((4, r, lanes), F32),
                        pltpu.SemaphoreType.DMA((2, 4)), pltpu.SemaphoreType.DMA((2, 3)),
                        pltpu.SemaphoreType.DMA((2, 7))],
        compiler_params=_cp(),
    )(slab)


def _adamw_reduced(ps, rbs, chip, w, m, v, name):
    nh = len(ps)
    R, C = w.shape
    ch = C // nh
    tr = _tile(R, 288)
    nt = R // tr
    c1 = 1.0 - ADAM_B1 ** ADAM_STEP
    c2 = 1.0 - ADAM_B2 ** ADAM_STEP

    def body(c_ref, *refs):
        p_refs, rb_refs = refs[:nh], refs[nh:2 * nh]
        w_ref, m_ref, v_ref, g_ref, d_ref, nm_ref, nv_ref = refs[2 * nh:]
        for h in range(nh):
            @pl.when(pl.program_id(0) == h)
            def _(h=h):
                rb = rb_refs[h]
                gv = p_refs[h][0].astype(F32) + rb[0].astype(F32)
                gv = gv + rb[1].astype(F32)
                gv = gv + rb[2].astype(F32)
                nm = ADAM_B1 * m_ref[...] + (1.0 - ADAM_B1) * gv
                nv = ADAM_B2 * v_ref[...] + (1.0 - ADAM_B2) * (gv * gv)
                g_ref[...] = gv
                nm_ref[...] = nm
                nv_ref[...] = nv
                d_ref[...] = -ADAM_LR * ((nm / c1) / (jnp.sqrt(nv / c2) + ADAM_EPS) + ADAM_WD * w_ref[...])

    def held(h):
        return lambda hh, tt: jnp.where(hh == h, tt, jnp.where(hh < h, 0, nt - 1))

    p_specs = [pl.BlockSpec((1, tr, ch), lambda hh, tt, c_ref, f=held(h): (c_ref[0], f(hh, tt), 0))
               for h in range(nh)]
    rb_specs = [pl.BlockSpec((3, tr, ch), lambda hh, tt, c_ref, f=held(h): (0, f(hh, tt), 0)) for h in range(nh)]
    blk = pl.BlockSpec((tr, ch), lambda hh, tt, c_ref: (tt, hh))
    return pl.pallas_call(
        body,
        name=name,
        grid_spec=pltpu.PrefetchScalarGridSpec(
            num_scalar_prefetch=1, grid=(nh, nt), in_specs=p_specs + rb_specs + [blk] * 3, out_specs=[blk] * 4),
        out_shape=[jax.ShapeDtypeStruct((R, C), F32)] * 4,
        compiler_params=_cp(("arbitrary", "arbitrary")),
    )(chip, *ps, *rbs, w, m, v)


_SMALL = ("norm_w", "q_norm_w", "k_norm_w", "sinks", "a_re", "a_im", "log_step", "b_re", "b_im", "c_re", "c_im",
          "d_skip", "b_glu", "attn_out_norm_w", "ssm_out_norm_w")
_WEIGHTS = ("norm_w", "w_in", "q_norm_w", "k_norm_w", "sinks", "a_re", "a_im", "log_step", "b_re", "b_im", "c_re",
            "c_im", "d_skip", "w_glu", "b_glu", "attn_out_norm_w", "ssm_out_norm_w", "w_out")
_SMALL_2D = dict(norm_w=(1, 2048), q_norm_w=(1, 64), k_norm_w=(1, 64), sinks=(1, 16), a_re=(64, 64), a_im=(64, 64),
                 log_step=(1, 64), b_re=(1024, 64), b_im=(1024, 64), c_re=(1024, 64), c_im=(1024, 64),
                 d_skip=(1, 1024), b_glu=(1, 1024), attn_out_norm_w=(1, 1024), ssm_out_norm_w=(1, 1024))
_P_MINOR = ("b_re", "b_im")


def _flat_form(n, t):
    return t.transpose(0, 2, 1) if n in _P_MINOR else t


def _own_form(n, t, shape):
    if n in _P_MINOR:
        return t.reshape(shape[0], shape[2], shape[1]).transpose(0, 2, 1)
    return t.reshape(shape)


def _slab_rows(n):
    return -(-n // 1024) * 8


_PACK_ROWS = 2304


_LOSS_ROW = 2192


def _pack(d, loss):
    parts = []
    for n in _SMALL:
        flat = _flat_form(n, d[n]).reshape(-1).astype(F32)
        rows = _slab_rows(flat.shape[0])
        parts.append(jnp.pad(flat, (0, rows * 128 - flat.shape[0])).reshape(rows, 128))
    assert sum(p.shape[0] for p in parts) == _LOSS_ROW
    parts.append(jnp.pad(loss.reshape(1, 1), ((0, _PACK_ROWS - _LOSS_ROW - 1), (0, 127))))
    return jnp.concatenate(parts, axis=0)


def _unpack(packed, like):
    out, off = {}, 0
    for n in _SMALL:
        size = math.prod(like[n].shape)
        rows = _slab_rows(size)
        out[n] = _own_form(n, packed[off:off + rows].reshape(-1)[:size], like[n].shape)
        off += rows
    return out


def _adamw_small(g, w, m, v):
    c1 = 1.0 - ADAM_B1 ** ADAM_STEP
    c2 = 1.0 - ADAM_B2 ** ADAM_STEP
    k = len(_SMALL)

    def body(*refs):
        ins, outs = refs[:4 * k], refs[4 * k:]
        for j in range(k):
            gv, wv, mv, vv = (ins[q * k + j][...] for q in range(4))
            nm = ADAM_B1 * mv + (1.0 - ADAM_B1) * gv
            nv = ADAM_B2 * vv + (1.0 - ADAM_B2) * (gv * gv)
            outs[j][...] = -ADAM_LR * ((nm / c1) / (jnp.sqrt(nv / c2) + ADAM_EPS) + ADAM_WD * wv)
            outs[k + j][...] = nm
            outs[2 * k + j][...] = nv

    args = [_flat_form(n, d[n]).reshape(_SMALL_2D[n]) for d in (g, w, m, v) for n in _SMALL]
    shapes = [jax.ShapeDtypeStruct(_SMALL_2D[n], F32) for _ in range(3) for n in _SMALL]
    outs = pl.pallas_call(body, name="adamw_small", out_shape=shapes, compiler_params=_cp())(*args)
    res = []
    for q in range(3):
        res.append({n: _own_form(n, outs[q * k + j], w[n].shape) for j, n in enumerate(_SMALL)})
    return res


def kernel(x, positions, norm_w, w_in, q_norm_w, k_norm_w, sinks, a_re, a_im, log_step, b_re, b_im, c_re, c_im, d_skip, w_glu, b_glu, attn_out_norm_w, ssm_out_norm_w, w_out, loss_target, m_norm_w, m_w_in, m_q_norm_w, m_k_norm_w, m_sinks, m_a_re, m_a_im, m_log_step, m_b_re, m_b_im, m_c_re, m_c_im, m_d_skip, m_w_glu, m_b_glu, m_attn_out_norm_w, m_ssm_out_norm_w, m_w_out, v_norm_w, v_w_in, v_q_norm_w, v_k_norm_w, v_sinks, v_a_re, v_a_im, v_log_step, v_b_re, v_b_im, v_c_re, v_c_im, v_d_skip, v_w_glu, v_b_glu, v_attn_out_norm_w, v_ssm_out_norm_w, v_w_out):
    w = dict(norm_w=norm_w, w_in=w_in, q_norm_w=q_norm_w, k_norm_w=k_norm_w, sinks=sinks, a_re=a_re, a_im=a_im,
             log_step=log_step, b_re=b_re, b_im=b_im, c_re=c_re, c_im=c_im, d_skip=d_skip, w_glu=w_glu, b_glu=b_glu,
             attn_out_norm_w=attn_out_norm_w, ssm_out_norm_w=ssm_out_norm_w, w_out=w_out)
    m = dict(norm_w=m_norm_w, w_in=m_w_in, q_norm_w=m_q_norm_w, k_norm_w=m_k_norm_w, sinks=m_sinks, a_re=m_a_re,
             a_im=m_a_im, log_step=m_log_step, b_re=m_b_re, b_im=m_b_im, c_re=m_c_re, c_im=m_c_im, d_skip=m_d_skip,
             w_glu=m_w_glu, b_glu=m_b_glu, attn_out_norm_w=m_attn_out_norm_w, ssm_out_norm_w=m_ssm_out_norm_w,
             w_out=m_w_out)
    v = dict(norm_w=v_norm_w, w_in=v_w_in, q_norm_w=v_q_norm_w, k_norm_w=v_k_norm_w, sinks=v_sinks, a_re=v_a_re,
             a_im=v_a_im, log_step=v_log_step, b_re=v_b_re, b_im=v_b_im, c_re=v_c_re, c_im=v_c_im, d_skip=v_d_skip,
             w_glu=v_w_glu, b_glu=v_b_glu, attn_out_norm_w=v_attn_out_norm_w, ssm_out_norm_w=v_ssm_out_norm_w,
             w_out=v_w_out)
    core = lax.axis_index("c").astype(jnp.int32).reshape(1)
    chip = (2 * lax.axis_index("x") + lax.axis_index("y")).astype(jnp.int32).reshape(1)

    loss, grad_x, grads, parts = _step(x[0], positions[0], loss_target[0], w, core, chip)
    delta, new_m, new_v = {}, {}, {}
    for n in ("w_glu", "w_out"):
        grads[n], delta[n], new_m[n], new_v[n] = _adamw_reduced(*parts[n], chip, w[n], m[n], v[n], f"adamw_{n}")
    g_t, d_t, m_t, v_t = _adamw_reduced(*parts["w_in"], chip, w["w_in"].T, m["w_in"].T, v["w_in"].T, "adamw_w_in")
    grads["w_in"], delta["w_in"], new_m["w_in"], new_v["w_in"] = g_t.T, d_t.T, m_t.T, v_t.T
    d_s, m_s, v_s = _adamw_small(grads, w, m, v)
    delta.update(d_s)
    new_m.update(m_s)
    new_v.update(v_s)

    return (loss, grad_x[None], *[grads[n] for n in _WEIGHTS], *[delta[n] for n in _WEIGHTS],
            *[new_m[n] for n in _WEIGHTS], *[new_v[n] for n in _WEIGHTS])
```

```python
import math

import jax
import jax.numpy as jnp
from jax import lax
from jax.experimental import pallas as pl
from jax.experimental.pallas import tpu as pltpu

F32 = jnp.float32
BF16 = jnp.bfloat16

D_MODEL = 2048
ATTN_W = 1024
KV_W = 256
SSM_W = 1024
HEAD_DIM = 64
N_HEADS = 16
N_KV = 4
IN_W = 4608
BLOCK = 128
ROPE_THETA = 10000.0
NORM_EPS = 1e-6
SSM_G = 64
SSM_P = 64
SSM_H = 16
CHUNK = 16
CW = CHUNK * SSM_H
N_DEV = 8

ADAM_LR = 0.001
ADAM_B1 = 0.9
ADAM_B2 = 0.999
ADAM_EPS = 1e-08
ADAM_WD = 0.01
ADAM_STEP = 10

VMEM_LIMIT = 56 * 1024 * 1024
MESH = pl.DeviceIdType.MESH


def _cp(sem=None):
    if sem is None:
        return pltpu.CompilerParams(vmem_limit_bytes=VMEM_LIMIT)
    return pltpu.CompilerParams(vmem_limit_bytes=VMEM_LIMIT, dimension_semantics=sem)


def _sigmoid(x):
    return 0.5 * jnp.tanh(0.5 * x) + 0.5


def _silu(x):
    return x * _sigmoid(x)


def _dsilu(x):
    s = _sigmoid(x)
    return s * (1.0 + x * (1.0 - s))


_GELU_C = math.sqrt(2.0 / math.pi)


def _gelu(y):
    t = jnp.tanh(_GELU_C * (y + 0.044715 * y * y * y))
    return 0.5 * y * (1.0 + t)


def _dgelu(y):
    t = jnp.tanh(_GELU_C * (y + 0.044715 * y * y * y))
    return 0.5 * (1.0 + t) + 0.5 * y * (1.0 - t * t) * _GELU_C * (1.0 + 3.0 * 0.044715 * y * y)


def _tile(n, want):
    if n <= want:
        return n
    for t in range(want - want % 16, 0, -16):
        if n % t == 0:
            return t
    raise ValueError((n, want))


def _mm(a, b, mode, out_dtype, name, tm=512, tn=1024, ride=None, panel=None):
    if mode == "nn":
        (M, K), (K2, N) = a.shape, b.shape
    else:
        (K, M), (K2, N) = a.shape, b.shape
    assert K == K2
    tm, tn = _tile(M, tm), _tile(N, tn)
    p0 = 0
    if panel is not None:
        p0, N = panel, tn
    dn = {"nn": _NN, "tn": _TN}[mode]

    def body(a_ref, b_ref, o_ref):
        acc = lax.dot_general(a_ref[...].astype(BF16), b_ref[...].astype(BF16), dn, preferred_element_type=F32)
        o_ref[...] = acc.astype(o_ref.dtype)

    a_spec = pl.BlockSpec((K, tm), lambda j, i: (0, i)) if mode == "tn" else pl.BlockSpec((tm, K), lambda j, i: (i, 0))
    b_spec = pl.BlockSpec((K, tn), lambda j, i: (0, j + p0))
    o_spec = pl.BlockSpec((tm, tn), lambda j, i: (i, j))
    if ride is not None:
        (out,), landed = _call(body, name, (N // tn, M // tm), [a_spec, b_spec], [o_spec],
                               [jax.ShapeDtypeStruct((M, N), out_dtype)], (a, b), ride=ride)
        return out, landed
    return pl.pallas_call(
        body,
        name=name,
        grid=(N // tn, M // tm),
        in_specs=[a_spec, b_spec],
        out_specs=o_spec,
        out_shape=jax.ShapeDtypeStruct((M, N), out_dtype),
        compiler_params=_cp(("parallel", "parallel")),
    )(a, b)


_CHIP_ORDER = (0, 2, 1, 3)


def _rms_inproj_gather(x, norm_w, wt_shard, chip):
    L = x.shape[0]
    tm = _tile(L, 1024)
    ni = L // tm
    r = IN_W // N_DEV
    tn = 2 * r

    def body(chip_ref, x_ref, nw_ref, shard, proj_ref, hn_hbm, wt_hbm, hn_scr, w_scr, send_sems, recv_sems, loc_sems):
        jc, i = pl.program_id(0), pl.program_id(1)
        xx, yy, c = lax.axis_index("x"), lax.axis_index("y"), lax.axis_index("c")
        me, sibling = (xx, yy, c), (xx, yy, 1 - c)
        chips = [(1 - xx, yy), (xx, 1 - yy), (1 - xx, 1 - yy)]

        def slot(dev):
            return wt_hbm.at[4 * dev[0] + 2 * dev[1] + dev[2]]

        def copy(q, block, to, src=None):
            return pltpu.make_async_remote_copy(
                src_ref=slot(block) if src is None else src, dst_ref=slot(block),
                send_sem=send_sems.at[q], recv_sem=recv_sems.at[q], device_id=to, device_id_type=MESH)

        def rows_of(buf, core):
            return w_scr.at[buf, pl.ds(pl.multiple_of(core * r, 16), r)]

        mine = pltpu.make_async_copy(shard, slot(me), loc_sems.at[0])
        sends = [copy(0, me, sibling, src=shard)] + [copy(1 + j, me, (*ch, c), src=shard) for j, ch in enumerate(chips[:2])]
        relay_block = (xx + (1 - c) * (1 - 2 * xx), yy + c * (1 - 2 * yy), c)
        relay = copy(3, relay_block, (xx + c * (1 - 2 * xx), yy + (1 - c) * (1 - 2 * yy), c))
        first = jnp.logical_and(jc == 0, i == 0)

        @pl.when(first)
        def _():
            mine.start()
            for cp in sends:
                cp.start()
            own = pltpu.make_async_copy(shard, rows_of(0, c), loc_sems.at[1])
            own.start()
            copy(0, sibling, me).wait_recv()
            sib = pltpu.make_async_copy(slot(sibling), rows_of(0, 1 - c), loc_sems.at[2])
            sib.start()
            own.wait()
            sib.wait()

        def to_vmem(j, ch):
            pltpu.make_async_copy(slot((*ch, c)), rows_of((1 + j) % 2, c), loc_sems.at[1 + j]).start()

        @pl.when(jnp.logical_and(jc == 1, i == 0))
        def _():
            for j in range(2):
                copy(1 + j, (*chips[j], c), me).wait_recv()
                copy(4 + j, (*chips[j], c), sibling).start()
            relay.start()
            to_vmem(0, chips[0])

        @pl.when(jnp.logical_and(jc == 1, i == ni // 2))
        def _():
            to_vmem(1, chips[1])

        @pl.when(jnp.logical_and(jc == 2, i == ni // 2))
        def _():
            copy(3, (*chips[2], c), me).wait_recv()
            copy(6, (*chips[2], c), sibling).start()
            to_vmem(2, chips[2])

        for j, ch in enumerate(chips):
            @pl.when(jnp.logical_and(jc == 1 + j, i == 0))
            def _(j=j, ch=ch):
                buf = (1 + j) % 2
                copy(4 + j, (*ch, 1 - c), me).wait_recv()
                passed = pltpu.make_async_copy(slot((*ch, 1 - c)), rows_of(buf, 1 - c), loc_sems.at[4 + j])
                passed.start()
                pltpu.make_async_copy(slot((*ch, c)), rows_of(buf, c), loc_sems.at[1 + j]).wait()
                passed.wait()

        rows = pl.ds(pl.multiple_of(i * tm, tm), tm)

        @pl.when(jc == 0)
        def _():
            xv = x_ref[...]
            rstd = lax.rsqrt(jnp.mean(xv * xv, axis=-1, keepdims=True) + NORM_EPS)
            hn_scr[rows, :] = (xv * rstd * nw_ref[...]).astype(BF16)

        keep_hn = pltpu.make_async_copy(hn_scr, hn_hbm, loc_sems.at[7])

        @pl.when(jnp.logical_and(jc == 1, i == 0))
        def _():
            keep_hn.start()

        for buf in range(2):
            @pl.when(jc % 2 == buf)
            def _(buf=buf):
                proj_ref[...] = lax.dot_general(hn_scr[rows, :], w_scr[buf], _NT, preferred_element_type=F32)

        @pl.when(jnp.logical_and(jc == 3, i == ni - 1))
        def _():
            for cp in sends + [relay]:
                cp.wait_send()
            for j, ch in enumerate(chips):
                copy(4 + j, (*ch, c), sibling).wait_send()
            mine.wait()
            keep_hn.wait()

    def tile_of(jc, chip_ref):
        mask = jnp.where(jc == 1, _CHIP_ORDER[1], jnp.where(jc == 2, _CHIP_ORDER[2], jnp.where(jc == 3, _CHIP_ORDER[3], 0)))
        return jnp.bitwise_xor(chip_ref[0], mask)

    held = lambda jc, i: jnp.where(jc == 0, i, ni - 1)
    return pl.pallas_call(
        body,
        name="rms_inproj_gather",
        grid_spec=pltpu.PrefetchScalarGridSpec(
            num_scalar_prefetch=1,
            grid=(4, ni),
            in_specs=[pl.BlockSpec((tm, D_MODEL), lambda jc, i, ch: (held(jc, i), 0)),
                      pl.BlockSpec((1, D_MODEL), lambda jc, i, ch: (0, 0)), _ANY],
            out_specs=[pl.BlockSpec((tm, tn), lambda jc, i, ch: (i, tile_of(jc, ch))), _ANY, _ANY],
            scratch_shapes=[pltpu.VMEM((L, D_MODEL), BF16), pltpu.VMEM((2, tn, D_MODEL), BF16),
                            pltpu.SemaphoreType.DMA((7,)), pltpu.SemaphoreType.DMA((7,)), pltpu.SemaphoreType.DMA((8,))],
        ),
        out_shape=[jax.ShapeDtypeStruct((L, IN_W), F32), jax.ShapeDtypeStruct((L, D_MODEL), BF16),
                   jax.ShapeDtypeStruct((N_DEV, r, D_MODEL), BF16)],
        compiler_params=_cp(("arbitrary", "arbitrary")),
    )(chip, x, norm_w.reshape(1, D_MODEL), wt_shard)


def _seg_sum(v):
    a = lax.broadcasted_iota(jnp.int32, (128, 128), 0) // HEAD_DIM
    b = lax.broadcasted_iota(jnp.int32, (128, 128), 1) // HEAD_DIM
    ones = jnp.where(a == b, 1.0, 0.0).astype(BF16)
    hi = v.astype(BF16)
    lo = (v - hi.astype(F32)).astype(BF16)
    return jnp.dot(hi, ones, preferred_element_type=F32) + jnp.dot(lo, ones, preferred_element_type=F32)


def _rot_half(t):
    lane = lax.broadcasted_iota(jnp.int32, t.shape, 1)
    return jnp.where(lane % HEAD_DIM < HEAD_DIM // 2, pltpu.roll(t, 128 - HEAD_DIM // 2, 1),
                     pltpu.roll(t, HEAD_DIM // 2, 1))


def _norm_rope(raw, w, cos, sin):
    r = lax.rsqrt(_seg_sum(raw * raw) * (1.0 / HEAD_DIM) + NORM_EPS)
    tn = raw * r * w
    return r, tn * cos + _rot_half(tn) * sin


def _norm_rope_bwd(d_rot, raw, w, cos, sin):
    r = lax.rsqrt(_seg_sum(raw * raw) * (1.0 / HEAD_DIM) + NORM_EPS)
    d_tn = d_rot * cos + _rot_half(d_rot * sin)
    xh = raw * r
    gw = d_tn * w
    d_raw = r * (gw - xh * (_seg_sum(gw * xh) * (1.0 / HEAD_DIM)))
    return d_raw, d_tn * xh


def _band_mask2(has_prev, keys_on_rows=False):
    qd, kd = (1, 0) if keys_on_rows else (0, 1)
    qi = lax.broadcasted_iota(jnp.int32, (2 * BLOCK, 2 * BLOCK), qd) % BLOCK + BLOCK
    kj = lax.broadcasted_iota(jnp.int32, (2 * BLOCK, 2 * BLOCK), kd)
    rel = qi - kj
    return (rel >= 0) & (rel < BLOCK) & ((kj >= BLOCK) | has_prev)


def _half_tiles(pair):
    lo = lax.broadcasted_iota(jnp.int32, pair.shape, 1) < HEAD_DIM
    sw = pltpu.roll(pair, HEAD_DIM, 1)
    z = jnp.zeros_like(pair)
    return (jnp.where(lo, pair, z).astype(BF16), jnp.where(lo, z, sw).astype(BF16),
            jnp.where(lo, sw, z).astype(BF16), jnp.where(lo, z, pair).astype(BF16))


def _two_rows(top, bottom):
    row = lax.broadcasted_iota(jnp.int32, (2 * BLOCK, 1), 0)
    return jnp.where(row < BLOCK, top, bottom)


_SCALE = 1.0 / math.sqrt(HEAD_DIM)
_NT = (((1,), (1,)), ((), ()))
_NN = (((1,), (0,)), ((), ()))
_TN = (((0,), (0,)), ((), ()))


def _qk_prep(proj, tab, qw, kw):
    L = proj.shape[0]
    tm = _tile(L, 512)

    def body(q_ref, k_ref, t_ref, qw_ref, kw_ref, qo_ref, ko_ref):
        cos, sin = t_ref[:, :128], t_ref[:, 128:]
        for c in range(ATTN_W // 128):
            _, qr = _norm_rope(q_ref[:, c * 128:(c + 1) * 128], qw_ref[...], cos, sin)
            qo_ref[:, c * 128:(c + 1) * 128] = (qr * _SCALE).astype(BF16)
        for c in range(KV_W // 128):
            _, kr = _norm_rope(k_ref[:, c * 128:(c + 1) * 128], kw_ref[...], cos, sin)
            ko_ref[:, c * 128:(c + 1) * 128] = kr.astype(BF16)

    row = pl.BlockSpec((1, 128), lambda i: (0, 0))
    return pl.pallas_call(
        body,
        name="qk_prep",
        grid=(L // tm,),
        in_specs=[pl.BlockSpec((tm, ATTN_W), lambda i: (i, 0)), pl.BlockSpec((tm, KV_W), lambda i: (i, 4)),
                  pl.BlockSpec((tm, 256), lambda i: (i, 0)), row, row],
        out_specs=[pl.BlockSpec((tm, ATTN_W), lambda i: (i, 0)), pl.BlockSpec((tm, KV_W), lambda i: (i, 0))],
        out_shape=[jax.ShapeDtypeStruct((L, ATTN_W), BF16), jax.ShapeDtypeStruct((L, KV_W), BF16)],
        compiler_params=_cp(("parallel",)),
    )(proj, proj, tab, jnp.tile(qw, 2).reshape(1, 128), jnp.tile(kw, 2).reshape(1, 128))


def _group_tiles(g, kt, vt):
    a, b = divmod(g, 2)
    return kt[a][2 * b], kt[a][2 * b + 1], vt[a][2 * b], vt[a][2 * b + 1]


def _attn_fwd(q, k, proj, sinks, ride):
    L = proj.shape[0]
    nb = L // BLOCK

    def body(q_ref, kc_ref, kp_ref, vc_ref, vp_ref, z0_ref, z1_ref, sink_ref, og_ref, o_ref, lse_ref):
        i = pl.program_id(0)
        mask = _band_mask2(i > 0)
        z = jnp.concatenate([z0_ref[...], z1_ref[...]], axis=1)
        lane = lax.broadcasted_iota(jnp.int32, (BLOCK, 128), 1)
        kt = [_half_tiles(jnp.concatenate([kp_ref[:, a * 128:(a + 1) * 128], kc_ref[:, a * 128:(a + 1) * 128]],
                                          axis=0).astype(F32)) for a in range(2)]
        vt = [_half_tiles(jnp.concatenate([vp_ref[:, a * 128:(a + 1) * 128], vc_ref[:, a * 128:(a + 1) * 128]],
                                          axis=0)) for a in range(2)]
        lse_mat = jnp.zeros((BLOCK, 128), F32)
        pairs = []
        for g in range(N_KV):
            k_lo, k_hi, v_lo, v_hi = _group_tiles(g, kt, vt)
            q2 = jnp.concatenate([q_ref[:, 2 * g * 128:(2 * g + 1) * 128],
                                  q_ref[:, (2 * g + 1) * 128:(2 * g + 2) * 128]], axis=0)
            for half, (kh, vh) in enumerate(((k_lo, v_lo), (k_hi, v_hi))):
                pairs.append(dict(g=g, half=half, vh=vh, s=lax.dot_general(q2, kh, _NT, preferred_element_type=F32)))
        for pr in pairs:
            h_top, h_bot = 4 * pr["g"] + pr["half"], 4 * pr["g"] + 2 + pr["half"]
            s = jnp.where(mask, pr["s"], -1e30)
            sink = _two_rows(sink_ref[h_top], sink_ref[h_bot])
            m = jnp.maximum(jnp.max(s, axis=-1, keepdims=True), sink)
            e = jnp.exp(s - m)
            den = jnp.sum(e, axis=-1, keepdims=True) + jnp.exp(sink - m)
            pr["p_b"] = (e * (1.0 / den)).astype(BF16)
            lse = m + jnp.log(den)
            lse_mat = jnp.where(lane == h_top, lse[:BLOCK], lse_mat)
            lse_mat = jnp.where(lane == h_bot, lse[BLOCK:], lse_mat)
        outs = []
        for g in range(N_KV):
            acc = (jnp.dot(pairs[2 * g]["p_b"], pairs[2 * g]["vh"], preferred_element_type=F32)
                   + jnp.dot(pairs[2 * g + 1]["p_b"], pairs[2 * g + 1]["vh"], preferred_element_type=F32))
            outs += [acc[:BLOCK], acc[BLOCK:]]
        o = jnp.concatenate(outs, axis=1)
        o_ref[...] = o.astype(BF16)
        og_ref[...] = (o * _silu(z)).astype(BF16)
        lse_ref[...] = lse_mat

    prev = lambda i: jnp.maximum(i - 1, 0)
    return _call(
        body, "attn_fwd", (nb,),
        [pl.BlockSpec((BLOCK, ATTN_W), lambda i: (i, 0)),
         pl.BlockSpec((BLOCK, KV_W), lambda i: (i, 0)),
         pl.BlockSpec((BLOCK, KV_W), lambda i: (prev(i), 0)),
         pl.BlockSpec((BLOCK, KV_W), lambda i: (i, 5)),
         pl.BlockSpec((BLOCK, KV_W), lambda i: (prev(i), 5)),
         pl.BlockSpec((BLOCK, 512), lambda i: (i, 3)),
         pl.BlockSpec((BLOCK, 512), lambda i: (i, 4)),
         pl.BlockSpec(memory_space=pltpu.SMEM)],
        [pl.BlockSpec((BLOCK, ATTN_W), lambda i: (i, 0)),
         pl.BlockSpec((BLOCK, ATTN_W), lambda i: (i, 0)),
         pl.BlockSpec((BLOCK, 128), lambda i: (i, 0))],
        [jax.ShapeDtypeStruct((L, ATTN_W), BF16), jax.ShapeDtypeStruct((L, ATTN_W), BF16),
         jax.ShapeDtypeStruct((L, 128), F32)],
        (q, k, k, proj, proj, proj, proj, sinks), ride=ride)


def _attn_bwd(q, k, proj, sinks, d_o, o, lse, ride):
    L = proj.shape[0]
    nb = L // BLOCK

    def body(q_ref, kc_ref, kp_ref, vc_ref, vp_ref, do_ref, o_ref, lse_ref, sink_ref,
             dq_ref, dk_ref, dv_ref, gs_ref, ck_scr, cv_scr):
        i = pl.program_id(0)

        @pl.when(i == 0)
        def _():
            gs_ref[...] = jnp.zeros_like(gs_ref)
            ck_scr[...] = jnp.zeros_like(ck_scr)
            cv_scr[...] = jnp.zeros_like(cv_scr)

        @pl.when(i == nb)
        def _():
            dk_ref[...] = ck_scr[...]
            dv_ref[...] = cv_scr[...]

        @pl.when(i < nb)
        def _():
            mask = _band_mask2(i > 0, keys_on_rows=True)
            lane = lax.broadcasted_iota(jnp.int32, (1, 128), 1)
            lane2 = lax.broadcasted_iota(jnp.int32, (1, 2 * BLOCK), 1)
            lo = lax.broadcasted_iota(jnp.int32, (2 * BLOCK, 128), 1) < HEAD_DIM
            lse_t = lse_ref[...].T
            prod_all = do_ref[...].astype(F32) * o_ref[...].astype(F32)
            seg = (lax.broadcasted_iota(jnp.int32, (N_HEADS, ATTN_W), 1) // HEAD_DIM
                   == lax.broadcasted_iota(jnp.int32, (N_HEADS, ATTN_W), 0)).astype(BF16)
            prod_hi = prod_all.astype(BF16)
            prod_lo = (prod_all - prod_hi.astype(F32)).astype(BF16)
            delta_t = (lax.dot_general(seg, prod_hi, _NT, preferred_element_type=F32)
                       + lax.dot_general(seg, prod_lo, _NT, preferred_element_type=F32))
            kt = [_half_tiles(jnp.concatenate([kp_ref[:, a * 128:(a + 1) * 128], kc_ref[:, a * 128:(a + 1) * 128]],
                                              axis=0).astype(F32)) for a in range(2)]
            vt = [_half_tiles(jnp.concatenate([vp_ref[:, a * 128:(a + 1) * 128], vc_ref[:, a * 128:(a + 1) * 128]],
                                              axis=0)) for a in range(2)]
            gs = jnp.zeros((1, 128), F32)
            dq_parts = []
            dk_acc = [jnp.zeros((2 * BLOCK, 128), F32) for _ in range(2)]
            dv_acc = [jnp.zeros((2 * BLOCK, 128), F32) for _ in range(2)]
            pairs = []
            for g in range(N_KV):
                k_lo, k_hi, v_lo, v_hi = _group_tiles(g, kt, vt)
                t0, t1 = slice(2 * g * 128, (2 * g + 1) * 128), slice((2 * g + 1) * 128, (2 * g + 2) * 128)
                q2 = jnp.concatenate([q_ref[:, t0], q_ref[:, t1]], axis=0)
                do2_b = jnp.concatenate([do_ref[:, t0], do_ref[:, t1]], axis=0).astype(BF16)
                for half, (kh, vh) in enumerate(((k_lo, v_lo), (k_hi, v_hi))):
                    pairs.append(dict(g=g, half=half, kh=kh, q2=q2, do2_b=do2_b,
                                      s=lax.dot_general(kh, q2, _NT, preferred_element_type=F32),
                                      dp=lax.dot_general(vh, do2_b, _NT, preferred_element_type=F32)))
            for pr in pairs:
                h_top, h_bot = 4 * pr["g"] + pr["half"], 4 * pr["g"] + 2 + pr["half"]
                pick = lambda t: jnp.concatenate([t[h_top:h_top + 1, :], t[h_bot:h_bot + 1, :]], axis=1)
                lse, delta = pick(lse_t), pick(delta_t)
                sink = jnp.where(lane2 < BLOCK, sink_ref[h_top], sink_ref[h_bot])
                p = jnp.exp(jnp.where(mask, pr["s"], -1e30) - lse)
                pr["ds_b"] = (p * (pr["dp"] - delta)).astype(BF16)
                pr["p_b"] = p.astype(BF16)
                gsink = -jnp.exp(sink - lse) * delta
                gs = gs + jnp.where(lane == h_top, jnp.sum(jnp.where(lane2 < BLOCK, gsink, 0.0)), 0.0)
                gs = gs + jnp.where(lane == h_bot, jnp.sum(jnp.where(lane2 >= BLOCK, gsink, 0.0)), 0.0)
            for g in range(N_KV):
                a, b = divmod(g, 2)
                dq2 = jnp.zeros((2 * BLOCK, 128), F32)
                dk_h, dv_h = [], []
                for pr in pairs[2 * g:2 * g + 2]:
                    dq2 = dq2 + lax.dot_general(pr["ds_b"], pr["kh"], _TN, preferred_element_type=F32)
                    dk_h.append(jnp.dot(pr["ds_b"], pr["q2"], preferred_element_type=F32))
                    dv_h.append(jnp.dot(pr["p_b"], pr["do2_b"], preferred_element_type=F32))
                dq_parts += [dq2[:BLOCK], dq2[BLOCK:]]
                for acc, parts in ((dk_acc, dk_h), (dv_acc, dv_h)):
                    t = jnp.where(lo, parts[0], parts[1])
                    t = t + pltpu.roll(t, HEAD_DIM, 1)
                    acc[a] = acc[a] + jnp.where(lo == (b == 0), t, 0.0)
            dq_ref[...] = jnp.concatenate(dq_parts, axis=1)
            dk_full = jnp.concatenate(dk_acc, axis=1)
            dv_full = jnp.concatenate(dv_acc, axis=1)
            dk_ref[...] = ck_scr[...] + dk_full[:BLOCK]
            dv_ref[...] = cv_scr[...] + dv_full[:BLOCK]
            ck_scr[...] = dk_full[BLOCK:]
            cv_scr[...] = dv_full[BLOCK:]
            gs_ref[...] += gs

    cur = lambda i: jnp.minimum(i, nb - 1)
    prev = lambda i: jnp.maximum(jnp.minimum(i, nb - 1) - 1, 0)
    done = lambda i: jnp.maximum(i - 1, 0)
    bs = pl.BlockSpec
    return _call(
        body, "attn_bwd", (nb + 1,),
        [bs((BLOCK, ATTN_W), lambda i: (cur(i), 0)),
         bs((BLOCK, KV_W), lambda i: (cur(i), 0)), bs((BLOCK, KV_W), lambda i: (prev(i), 0)),
         bs((BLOCK, KV_W), lambda i: (cur(i), 5)), bs((BLOCK, KV_W), lambda i: (prev(i), 5)),
         bs((BLOCK, ATTN_W), lambda i: (cur(i), 0)), bs((BLOCK, ATTN_W), lambda i: (cur(i), 0)),
         bs((BLOCK, 128), lambda i: (cur(i), 0)), bs(memory_space=pltpu.SMEM)],
        [bs((BLOCK, ATTN_W), lambda i: (cur(i), 0)),
         bs((BLOCK, KV_W), lambda i: (done(i), 0)), bs((BLOCK, KV_W), lambda i: (done(i), 0)),
         bs((1, 128), lambda i: (0, 0))],
        [jax.ShapeDtypeStruct((L, ATTN_W), F32), jax.ShapeDtypeStruct((L, KV_W), F32),
         jax.ShapeDtypeStruct((L, KV_W), F32), jax.ShapeDtypeStruct((1, 128), F32)],
        (q, k, k, proj, proj, d_o, o, lse, sinks),
        [pltpu.VMEM((BLOCK, KV_W), F32), pltpu.VMEM((BLOCK, KV_W), F32)], ride)


def _qk_prep_bwd(proj, tab, qw, kw, d_q, d_k, d_v, d_za, d_u, d_zs):
    L = proj.shape[0]
    tm = _tile(L, 512)
    z0 = ATTN_W + 2 * KV_W

    def body(q_ref, k_ref, t_ref, qw_ref, kw_ref, dq_ref, dk_ref, dv_ref, dza_ref, du_ref, dzs_ref,
             out_ref, gq_ref, gk_ref):
        i = pl.program_id(0)

        @pl.when(i == 0)
        def _():
            gq_ref[...] = jnp.zeros_like(gq_ref)
            gk_ref[...] = jnp.zeros_like(gk_ref)

        cos, sin = t_ref[:, :128], t_ref[:, 128:]
        gq = jnp.zeros((1, 128), F32)
        gk = jnp.zeros((1, 128), F32)
        for c in range(ATTN_W // 128):
            cs = slice(c * 128, (c + 1) * 128)
            d_raw, gw = _norm_rope_bwd(dq_ref[:, cs] * _SCALE, q_ref[:, cs], qw_ref[...], cos, sin)
            out_ref[:, cs] = d_raw.astype(BF16)
            gq = gq + jnp.sum(gw, axis=0, keepdims=True)
        for c in range(KV_W // 128):
            cs = slice(c * 128, (c + 1) * 128)
            d_raw, gw = _norm_rope_bwd(dk_ref[:, cs], k_ref[:, cs], kw_ref[...], cos, sin)
            out_ref[:, ATTN_W + c * 128:ATTN_W + (c + 1) * 128] = d_raw.astype(BF16)
            gk = gk + jnp.sum(gw, axis=0, keepdims=True)
        out_ref[:, ATTN_W + KV_W:z0] = dv_ref[...].astype(BF16)
        out_ref[:, z0:z0 + ATTN_W] = dza_ref[...]
        out_ref[:, z0 + ATTN_W:z0 + ATTN_W + SSM_W] = du_ref[...].astype(BF16)
        out_ref[:, z0 + ATTN_W + SSM_W:] = dzs_ref[...]
        gq_ref[...] += gq
        gk_ref[...] += gk

    row = pl.BlockSpec((1, 128), lambda i: (0, 0))
    blk = lambda w, c: pl.BlockSpec((tm, w), lambda i: (i, c))
    return pl.pallas_call(
        body,
        name="qk_prep_bwd",
        grid=(L // tm,),
        in_specs=[blk(ATTN_W, 0), blk(KV_W, 4), blk(256, 0), row, row, blk(ATTN_W, 0), blk(KV_W, 0), blk(KV_W, 0),
                  blk(ATTN_W, 0), blk(SSM_W, 0), blk(SSM_W, 0)],
        out_specs=[blk(IN_W, 0), row, row],
        out_shape=[jax.ShapeDtypeStruct((L, IN_W), BF16), jax.ShapeDtypeStruct((1, 128), F32),
                   jax.ShapeDtypeStruct((1, 128), F32)],
        compiler_params=_cp(("arbitrary",)),
    )(proj, proj, tab, jnp.tile(qw, 2).reshape(1, 128), jnp.tile(kw, 2).reshape(1, 128), d_q, d_k, d_v,
      d_za, d_u, d_zs)


def _cmul(a, b):
    return a[0] * b[0] - a[1] * b[1], a[0] * b[1] + a[1] * b[0]


def _cmul_conj(a, b):
    return a[0] * b[0] + a[1] * b[1], a[1] * b[0] - a[0] * b[1]


def _cadd(a, b):
    return a[0] + b[0], a[1] + b[1]


def _dot3(a, b, dn):
    ah, bh = a.astype(BF16), b.astype(BF16)
    al, bl = (a - ah.astype(F32)).astype(BF16), (b - bh.astype(F32)).astype(BF16)
    d = lambda u, v: lax.dot_general(u, v, dn, preferred_element_type=F32)
    return d(ah, bh) + d(ah, bl) + d(al, bh)


def _s5_discretise(a_re, a_im, ls, cosx, sinx, bt):
    delta = jnp.exp(ls)
    er = jnp.exp(a_re * delta)
    lb = (er * cosx, er * sinx)
    den = a_re * a_re + a_im * a_im
    coef = _cmul_conj((lb[0] - 1.0, lb[1]), (a_re, a_im))
    coef = (coef[0] / den, coef[1] / den)
    return delta, lb, coef, den, _cmul(coef, bt)


def _powers(lb):
    pw = [(jnp.ones_like(lb[0]), jnp.zeros_like(lb[0]))]
    for _ in range(CHUNK):
        pw.append(_cmul(pw[-1], lb))
    return pw


def _block_rows(a, pw, idx):
    blocks = [_cmul(a, pw[i]) for i in idx]
    return (jnp.concatenate([b[0] for b in blocks], axis=-2), jnp.concatenate([b[1] for b in blocks], axis=-2))


def _block_rows_bwd(g, a, pw, idx, g_pw):
    g_a = (jnp.zeros_like(a[0]), jnp.zeros_like(a[0]))
    for j, i in enumerate(idx):
        gj = (g[0][..., j * SSM_H:(j + 1) * SSM_H, :], g[1][..., j * SSM_H:(j + 1) * SSM_H, :])
        g_a = _cadd(g_a, _cmul_conj(gj, pw[i]))
        gp = _cmul_conj(gj, a)
        g_pw[i] = _cadd(g_pw[i], (jnp.sum(gp[0], axis=-2, keepdims=True), jnp.sum(gp[1], axis=-2, keepdims=True)))
    return g_a


_IDX_S = [CHUNK - 1 - s for s in range(CHUNK)]
_IDX_C = list(range(CHUNK + 1))


def _prep_args(p):
    row = lambda t: t.reshape(SSM_G, 1, SSM_P)
    xi = p["a_im"] * jnp.exp(p["log_step"])[:, None]
    return (row(p["a_re"]), row(p["a_im"]), row(jnp.broadcast_to(p["log_step"][:, None], (SSM_G, SSM_P))),
            row(jnp.cos(xi)), row(jnp.sin(xi)), p["b_re"].transpose(0, 2, 1), p["b_im"].transpose(0, 2, 1),
            p["c_re"], p["c_im"])


PREP_GROUPS = 8


def _prep_specs():
    r1 = pl.BlockSpec((PREP_GROUPS, 1, SSM_P), lambda g: (g, 0, 0))
    r16 = pl.BlockSpec((PREP_GROUPS, SSM_H, SSM_P), lambda g: (g, 0, 0))
    return [r1] * 5 + [r16] * 4, r1, r16


def _ssm_prep(p):
    def one_group(q, are, aim, ls, cosx, sinx, btr, bti, cre, cim, mt_ref, s_ref, o_ref, a_ref):
        _, lb, _, _, bb = _s5_discretise(are[q], aim[q], ls[q], cosx[q], sinx[q], (btr[q], bti[q]))
        pw = _powers(lb)
        c = (cre[q], cim[q])
        sc = _block_rows(bb, pw, _IDX_S)
        cl = _block_rows(c, pw, _IDX_C)
        ok = (cl[0][:CW], cl[1][:CW])
        ot = (cl[0][SSM_H:], cl[1][SSM_H:])
        s_ref[q] = jnp.concatenate([sc[0], sc[1]], axis=1).astype(BF16)
        o_ref[q] = jnp.concatenate([ot[0], -ot[1]], axis=1).astype(BF16)
        a_ref[q] = jnp.concatenate([pw[CHUNK][0], pw[CHUNK][1]], axis=1)
        kt = _dot3(jnp.concatenate([bb[0], -bb[1]], axis=1), jnp.concatenate([ok[0], ok[1]], axis=1), _NT)
        lane = lax.broadcasted_iota(jnp.int32, kt.shape, 1)
        for s in range(CHUNK):
            blk = kt if s == 0 else jnp.where(lane >= SSM_H * s, pltpu.roll(kt, SSM_H * s, 1), 0.0)
            mt_ref[q, s * SSM_H:(s + 1) * SSM_H, :] = blk.astype(BF16)

    def body(*refs):
        for q in range(PREP_GROUPS):
            one_group(q, *refs)

    in_specs, r1, _ = _prep_specs()
    g3 = lambda r, c: pl.BlockSpec((PREP_GROUPS, r, c), lambda g: (g, 0, 0))
    return pl.pallas_call(
        body,
        name="ssm_prep",
        grid=(SSM_G // PREP_GROUPS,),
        in_specs=in_specs,
        out_specs=[g3(CW, CW), g3(CW, 2 * SSM_P), g3(CW, 2 * SSM_P), g3(1, 2 * SSM_P)],
        out_shape=[jax.ShapeDtypeStruct((SSM_G, CW, CW), BF16), jax.ShapeDtypeStruct((SSM_G, CW, 2 * SSM_P), BF16),
                   jax.ShapeDtypeStruct((SSM_G, CW, 2 * SSM_P), BF16),
                   jax.ShapeDtypeStruct((SSM_G, 1, 2 * SSM_P), F32)],
        compiler_params=_cp(("parallel",)),
    )(*_prep_args(p))


def _ssm_prep_bwd(p, g_mt, g_scat, g_ocat, g_a16, ride):
    def body(are, aim, ls, cosx, sinx, btr, bti, cre, cim, gmt_ref, gs_ref, go_ref, ga_ref,
             g_are, g_aim, g_ls, g_btr, g_bti, g_cre, g_cim, ga1_scr, gb1_scr):
        lam = (are[...], aim[...])
        bt = (btr[...], bti[...])
        delta, lb, coef, den, bb = _s5_discretise(lam[0], lam[1], ls[...], cosx[...], sinx[...], bt)
        pw = _powers(lb)
        c = (cre[...], cim[...])
        ok = _block_rows(c, pw, _IDX_C[:CHUNK])
        g_pw =[(jnp.zeros_like(lb[0]), jnp.zeros_like(lb[0])) for _ in range(CHUNK + 1)]
        lane = lax.broadcasted_iota(jnp.int32, (SSM_H, CW), 1)
        for q in range(PREP_GROUPS):
            g_kt = gmt_ref[q, :SSM_H, :]
            for s in range(1, CHUNK):
                blk = gmt_ref[q, s * SSM_H:(s + 1) * SSM_H, :]
                g_kt = g_kt + jnp.where(lane < CW - SSM_H * s, pltpu.roll(blk, CW - SSM_H * s, 1), 0.0)
            a1 = jnp.concatenate([bb[0][q], -bb[1][q]], axis=1)
            b1 = jnp.concatenate([ok[0][q], ok[1][q]], axis=1)
            ga1_scr[q] = _dot3(g_kt, b1, _NN)
            gb1_scr[q] = _dot3(g_kt, a1, _TN)
        g_a1, g_b1 = ga1_scr[...], gb1_scr[...]
        g_bb = (g_a1[..., :SSM_P], -g_a1[..., SSM_P:])
        gs = gs_ref[...]
        g_bb = _cadd(g_bb, _block_rows_bwd((gs[..., :SSM_P], gs[..., SSM_P:]), bb, pw, _IDX_S, g_pw))
        go = go_ref[...]
        pad = jnp.zeros_like(go[..., :SSM_H, :SSM_P])
        g_cl = (jnp.concatenate([g_b1[..., :SSM_P], pad], axis=-2) + jnp.concatenate([pad, go[..., :SSM_P]], axis=-2),
                jnp.concatenate([g_b1[..., SSM_P:], pad], axis=-2) - jnp.concatenate([pad, go[..., SSM_P:]], axis=-2))
        g_c = _block_rows_bwd(g_cl, c, pw, _IDX_C, g_pw)
        ga = ga_ref[...]
        g_pw[CHUNK] = _cadd(g_pw[CHUNK], (ga[..., :SSM_P], ga[..., SSM_P:]))
        g_lb = (jnp.zeros_like(lb[0]), jnp.zeros_like(lb[0]))
        for l in range(CHUNK - 1, -1, -1):
            g_lb = _cadd(g_lb, _cmul_conj(g_pw[l + 1], pw[l]))
            g_pw[l] = _cadd(g_pw[l], _cmul_conj(g_pw[l + 1], lb))
        g_bt = _cmul_conj(g_bb, coef)
        gc = _cmul_conj(g_bb, bt)
        g_coef = (jnp.sum(gc[0], axis=-2, keepdims=True), jnp.sum(gc[1], axis=-2, keepdims=True))
        lam_den = (lam[0] / den, lam[1] / den)
        g_lb = _cadd(g_lb, _cmul(g_coef, lam_den))
        t = _cmul(_cmul_conj(g_coef, coef), lam_den)
        g_x = _cmul_conj(g_lb, lb)
        g_are[...] = g_x[0] * delta - t[0]
        g_aim[...] = g_x[1] * delta - t[1]
        g_ls[...] = (g_x[0] * lam[0] + g_x[1] * lam[1]) * delta
        g_btr[...] = g_bt[0]
        g_bti[...] = g_bt[1]
        g_cre[...] = g_c[0]
        g_cim[...] = g_c[1]

    in_specs, r1, r16 = _prep_specs()
    g3 = lambda r, c: pl.BlockSpec((PREP_GROUPS, r, c), lambda g: (g, 0, 0))
    rows = jax.ShapeDtypeStruct((SSM_G, 1, SSM_P), F32)
    mats = jax.ShapeDtypeStruct((SSM_G, SSM_H, SSM_P), F32)
    (g_are, g_aim, g_ls, g_btr, g_bti, g_cre, g_cim), landed = _call(
        body, "ssm_prep_bwd", (SSM_G // PREP_GROUPS,),
        in_specs + [g3(CW, CW), g3(CW, 2 * SSM_P), g3(CW, 2 * SSM_P), g3(1, 2 * SSM_P)],
        [r1] * 3 + [r16] * 4, [rows] * 3 + [mats] * 4, (*_prep_args(p), g_mt, g_scat, g_ocat, g_a16),
        [pltpu.VMEM((PREP_GROUPS, SSM_H, 2 * SSM_P), F32), pltpu.VMEM((PREP_GROUPS, CW, 2 * SSM_P), F32)], ride)
    grads = dict(a_re=g_are.reshape(SSM_G, SSM_P), a_im=g_aim.reshape(SSM_G, SSM_P),
                 log_step=jnp.sum(g_ls.reshape(SSM_G, SSM_P), axis=1),
                 b_re=g_btr.transpose(0, 2, 1), b_im=g_bti.transpose(0, 2, 1), c_re=g_cre, c_im=g_cim)
    return grads, landed


def _cmul_const(xv, ar, ai):
    return xv * ar + pltpu.roll(xv, SSM_P, 1) * ai


def _chunk_scan(inc, a_row, reverse):
    n = inc.shape[0]
    lane = lax.broadcasted_iota(jnp.int32, (1, 2 * SSM_P), 1)
    row = lax.broadcasted_iota(jnp.int32, inc.shape, 0)
    sign = jnp.where(lane < SSM_P, -1.0, 1.0)
    ar = jnp.where(lane < SSM_P, a_row, pltpu.roll(a_row, SSM_P, 1))
    ai = jnp.where(lane < SSM_P, pltpu.roll(a_row, SSM_P, 1), a_row)
    if reverse:
        ai = -ai
    xv = inc
    s = 1
    while s < n:
        if reverse:
            sh = jnp.where(row < n - s, pltpu.roll(xv, n - s, 0), 0.0)
        else:
            sh = jnp.where(row >= s, pltpu.roll(xv, s, 0), 0.0)
        xv = xv + _cmul_const(sh, ar, ai * sign)
        ar, ai = ar * ar - ai * ai, 2.0 * ar * ai
        s *= 2
    return xv


def _shift_rows(xv, reverse):
    n = xv.shape[0]
    row = lax.broadcasted_iota(jnp.int32, xv.shape, 0)
    if reverse:
        return jnp.where(row < n - 1, pltpu.roll(xv, n - 1, 0), 0.0)
    return jnp.where(row >= 1, pltpu.roll(xv, 1, 0), 0.0)


GB = 128 // SSM_H
U_COL0 = (ATTN_W + 2 * KV_W + ATTN_W) // 128


HALF = CHUNK // 2


def _chunk_perm():
    r = jnp.arange(HALF * 128)
    t, g8, h = r // 128, (r % 128) // SSM_H, r % SSM_H
    return ((g8 * 128 + t * SSM_H + h)[:, None] == jnp.arange(GB * 128)[None, :]).astype(BF16)


def _load_perm(p_hbm, p_scr, sem):
    @pl.when(pl.program_id(0) == 0)
    def _():
        cp = pltpu.make_async_copy(p_hbm, p_scr, sem)
        cp.start()
        cp.wait()


def _rows_to_chunks(pieces, perm):
    halves = [jnp.dot(jnp.concatenate(pieces[k * HALF:(k + 1) * HALF], axis=1).astype(BF16), perm,
                      preferred_element_type=F32).astype(BF16) for k in range(2)]
    return [jnp.concatenate([hv[:, g * 128:(g + 1) * 128] for hv in halves], axis=1) for g in range(GB)]


def _chunks_to_rows(groups, perm, two_pass):
    pieces = []
    for k in range(2):
        v = jnp.concatenate([gv[:, k * 128:(k + 1) * 128] for gv in groups], axis=1)
        hi = v.astype(BF16)
        out = lax.dot_general(hi, perm, _NT, preferred_element_type=F32)
        if two_pass:
            lo = (v - hi.astype(F32)).astype(BF16)
            out = out + lax.dot_general(lo, perm, _NT, preferred_element_type=F32)
        pieces += [out[:, t * 128:(t + 1) * 128] for t in range(HALF)]
    return pieces


def _ssm_fwd(proj, perm, mt, scat, ocat, a16, d_skip, ride):
    L = proj.shape[0]
    nc = L // CHUNK

    def body(u_ref, p_hbm, mt_ref, s_ref, o_ref, a_ref, d_ref, y_ref, yg_ref, h_ref, p_scr, sem):
        _load_perm(p_hbm, p_scr, sem)
        perm = p_scr[...]
        rows = [pl.ds(t, nc, stride=CHUNK) for t in range(CHUNK)]
        us = [u_ref[r, :] for r in rows]
        ua = _rows_to_chunks(us, perm)
        incs = [jnp.dot(ua[g], s_ref[g], preferred_element_type=F32) for g in range(GB)]
        intra = [jnp.dot(ua[g], mt_ref[g], preferred_element_type=F32) for g in range(GB)]
        hxs = [_shift_rows(_chunk_scan(incs[g], a_ref[g], False), False) for g in range(GB)]
        ys = []
        for g in range(GB):
            h_ref[g] = hxs[g]
            ys.append(intra[g] + lax.dot_general(hxs[g].astype(BF16), o_ref[g], _NT, preferred_element_type=F32))
        yp = _chunks_to_rows(ys, perm, True)
        for t, r in enumerate(rows):
            y = yp[t] + d_ref[...] * us[t]
            y_ref[r, :] = y
            yg_ref[r, :] = _gelu(y)

    g3 = lambda r, c: pl.BlockSpec((GB, r, c), lambda g: (g, 0, 0))
    col = pl.BlockSpec((L, 128), lambda g: (0, g))
    return _call(
        body, "ssm_fwd", (SSM_G // GB,),
        [pl.BlockSpec((L, 128), lambda g: (0, U_COL0 + g)), _ANY,
         g3(CW, CW), g3(CW, 2 * SSM_P), g3(CW, 2 * SSM_P), g3(1, 2 * SSM_P),
         pl.BlockSpec((1, 128), lambda g: (0, g))],
        [col, col, g3(nc, 2 * SSM_P)],
        [jax.ShapeDtypeStruct((L, SSM_W), F32), jax.ShapeDtypeStruct((L, SSM_W), F32),
         jax.ShapeDtypeStruct((SSM_G, nc, 2 * SSM_P), F32)],
        (proj, perm, mt, scat, ocat, a16, d_skip.reshape(1, SSM_W)),
        [pltpu.VMEM((HALF * 128, GB * 128), BF16), pltpu.SemaphoreType.DMA], ride)


def _ssm_bwd(d_yg, y, proj, hx, perm, mt, scat, ocat, a16, d_skip, ride):
    L = proj.shape[0]
    nc = L // CHUNK

    def body(dg_ref, y_ref, u_ref, h_ref, p_hbm, mt_ref, s_ref, o_ref, a_ref, d_ref,
             du_ref, gmt_ref, gs_ref, go_ref, ga_ref, gd_ref, p_scr, sem):
        _load_perm(p_hbm, p_scr, sem)
        perm = p_scr[...]
        rows = [pl.ds(t, nc, stride=CHUNK) for t in range(CHUNK)]
        us = [u_ref[r, :] for r in rows]
        dys = [dg_ref[r, :] * _dgelu(y_ref[r, :]) for r in rows]
        gd = jnp.zeros((1, 128), F32)
        for uv, dy in zip(us, dys):
            gd = gd + jnp.sum(dy * uv, axis=0, keepdims=True)
        gd_ref[...] = gd
        ua = _rows_to_chunks(us, perm)
        dya = _rows_to_chunks(dys, perm)
        lane = lax.broadcasted_iota(jnp.int32, (1, 2 * SSM_P), 1)
        dhs = [jnp.dot(dya[g], o_ref[g], preferred_element_type=F32) for g in range(GB)]
        intra = [lax.dot_general(dya[g], mt_ref[g], _NT, preferred_element_type=F32) for g in range(GB)]
        for g in range(GB):
            gmt_ref[g] = lax.dot_general(ua[g], dya[g], _TN, preferred_element_type=F32)
            go_ref[g] = lax.dot_general(dya[g], h_ref[g].astype(BF16), _TN, preferred_element_type=F32)
        dincs = [_shift_rows(_chunk_scan(dhs[g], a_ref[g], True), True) for g in range(GB)]
        dus = []
        for g in range(GB):
            dinc, hx_v = dincs[g], h_ref[g]
            dinc_b = dinc.astype(BF16)
            dus.append(intra[g] + lax.dot_general(dinc_b, s_ref[g], _NT, preferred_element_type=F32))
            gs_ref[g] = lax.dot_general(ua[g], dinc_b, _TN, preferred_element_type=F32)
            p1 = dinc * hx_v
            p2 = pltpu.roll(dinc, SSM_P, 1) * hx_v
            t1 = jnp.sum(p1 + pltpu.roll(p1, SSM_P, 1), axis=0, keepdims=True)
            t2 = jnp.sum(p2 - pltpu.roll(p2, SSM_P, 1), axis=0, keepdims=True)
            ga_ref[g] = jnp.where(lane < SSM_P, t1, pltpu.roll(t2, SSM_P, 1))
        dup = _chunks_to_rows(dus, perm, False)
        for t, r in enumerate(rows):
            du_ref[r, :] = dup[t] + d_ref[...] * dys[t]

    g3 = lambda r, c: pl.BlockSpec((GB, r, c), lambda g: (g, 0, 0))
    col = pl.BlockSpec((L, 128), lambda g: (0, g))
    row = pl.BlockSpec((1, 128), lambda g: (0, g))
    return _call(
        body, "ssm_bwd", (SSM_G // GB,),
        [col, col, pl.BlockSpec((L, 128), lambda g: (0, U_COL0 + g)), g3(nc, 2 * SSM_P), _ANY,
         g3(CW, CW), g3(CW, 2 * SSM_P), g3(CW, 2 * SSM_P), g3(1, 2 * SSM_P), row],
        [col, g3(CW, CW), g3(CW, 2 * SSM_P), g3(CW, 2 * SSM_P), g3(1, 2 * SSM_P), row],
        [jax.ShapeDtypeStruct((L, SSM_W), F32), jax.ShapeDtypeStruct((SSM_G, CW, CW), F32),
         jax.ShapeDtypeStruct((SSM_G, CW, 2 * SSM_P), F32), jax.ShapeDtypeStruct((SSM_G, CW, 2 * SSM_P), F32),
         jax.ShapeDtypeStruct((SSM_G, 1, 2 * SSM_P), F32), jax.ShapeDtypeStruct((1, SSM_W), F32)],
        (d_yg, y, proj, hx, perm, mt, scat, ocat, a16, d_skip.reshape(1, SSM_W)),
        [pltpu.VMEM((HALF * 128, GB * 128), BF16), pltpu.SemaphoreType.DMA], ride)


def _merge(og, yg, w_glu, proj, b_glu, wa, ws):
    L = og.shape[0]
    tm = _tile(L, 256)

    def body(og_ref, yg_ref, wg_ref, z0_ref, z1_ref, b_ref, wa_ref, ws_ref, m_ref, gp_ref):
        zs = jnp.concatenate([z0_ref[...], z1_ref[...]], axis=1)
        ygv = yg_ref[...]
        gpre = jnp.dot(ygv.astype(BF16), wg_ref[...], preferred_element_type=F32)
        gp_ref[...] = gpre
        os_ = ygv * _sigmoid(gpre + b_ref[...]) * _silu(zs)
        ogv = og_ref[...].astype(F32)
        ra = lax.rsqrt(jnp.mean(ogv * ogv, axis=-1, keepdims=True) + NORM_EPS)
        rs = lax.rsqrt(jnp.mean(os_ * os_, axis=-1, keepdims=True) + NORM_EPS)
        m_ref[:, :ATTN_W] = (ogv * ra * wa_ref[...]).astype(BF16)
        m_ref[:, ATTN_W:] = (os_ * rs * ws_ref[...]).astype(BF16)

    row = lambda w: pl.BlockSpec((1, w), lambda i: (0, 0))
    return pl.pallas_call(
        body,
        name="merge",
        grid=(L // tm,),
        in_specs=[pl.BlockSpec((tm, ATTN_W), lambda i: (i, 0)), pl.BlockSpec((tm, SSM_W), lambda i: (i, 0)),
                  pl.BlockSpec((SSM_W, SSM_W), lambda i: (0, 0)),
                  pl.BlockSpec((tm, 512), lambda i: (i, 7)), pl.BlockSpec((tm, 512), lambda i: (i, 8)),
                  row(SSM_W), row(ATTN_W), row(SSM_W)],
        out_specs=[pl.BlockSpec((tm, D_MODEL), lambda i: (i, 0)), pl.BlockSpec((tm, SSM_W), lambda i: (i, 0))],
        out_shape=[jax.ShapeDtypeStruct((L, D_MODEL), BF16), jax.ShapeDtypeStruct((L, SSM_W), F32)],
        compiler_params=_cp(("parallel",)),
    )(og, yg, w_glu, proj, proj, b_glu.reshape(1, SSM_W), wa.reshape(1, ATTN_W), ws.reshape(1, SSM_W))


def _outproj_loss(merged, w_out, x, target):
    L = x.shape[0]
    tm, tn = _tile(L, 256), D_MODEL
    ni, nj = L // tm, D_MODEL // tn

    def body(m_ref, w_ref, x_ref, t_ref, d_ref, db_ref, l_ref):
        out = x_ref[...] + jnp.dot(m_ref[...], w_ref[...], preferred_element_type=F32)
        diff = out - t_ref[...]
        d = diff * (1.0 / D_MODEL)
        d_ref[...] = d
        db_ref[...] = d.astype(BF16)
        l_ref[...] = jnp.full((1, 8, 128), jnp.sum(diff * diff), F32)

    return pl.pallas_call(
        body,
        name="outproj_loss",
        grid=(nj, ni),
        in_specs=[pl.BlockSpec((tm, D_MODEL), lambda j, i: (i, 0)),
                  pl.BlockSpec((D_MODEL, tn), lambda j, i: (0, j)),
                  pl.BlockSpec((tm, tn), lambda j, i: (i, j)),
                  pl.BlockSpec((tm, tn), lambda j, i: (i, j))],
        out_specs=[pl.BlockSpec((tm, tn), lambda j, i: (i, j)), pl.BlockSpec((tm, tn), lambda j, i: (i, j)),
                   pl.BlockSpec((1, 8, 128), lambda j, i: (i * nj + j, 0, 0))],
        out_shape=[jax.ShapeDtypeStruct((L, D_MODEL), F32), jax.ShapeDtypeStruct((L, D_MODEL), BF16),
                   jax.ShapeDtypeStruct((ni * nj, 8, 128), F32)],
        compiler_params=_cp(("parallel", "parallel")),
    )(merged, w_out, x, target)


def _merge_bwd(d_out_b, w_out, w_glu, og, o, yg, gpre, proj, b_glu, wa, ws):
    L = og.shape[0]
    tm = _tile(L, 256)

    def body(dout_ref, wo_ref, wg_ref, og_ref, o_ref, yg_ref, gp_ref, za0_ref, za1_ref, zs0_ref, zs1_ref, b_ref,
             wa_ref, ws_ref, do_ref, dza_ref, dzs_ref, dg_ref, dyg_ref, gwa_ref, gws_ref, gb_ref):
        i = pl.program_id(0)

        @pl.when(i == 0)
        def _():
            gwa_ref[...] = jnp.zeros_like(gwa_ref)
            gws_ref[...] = jnp.zeros_like(gws_ref)
            gb_ref[...] = jnp.zeros_like(gb_ref)

        dm = lax.dot_general(dout_ref[...], wo_ref[...], _NT, preferred_element_type=F32)
        za = jnp.concatenate([za0_ref[...], za1_ref[...]], axis=1)
        zs = jnp.concatenate([zs0_ref[...], zs1_ref[...]], axis=1)
        ogv, dma = og_ref[...].astype(F32), dm[:, :ATTN_W]
        ra = lax.rsqrt(jnp.mean(ogv * ogv, axis=-1, keepdims=True) + NORM_EPS)
        xh = ogv * ra
        gwa_ref[...] += jnp.sum(dma * xh, axis=0, keepdims=True)
        gx = dma * wa_ref[...]
        d_og = ra * (gx - xh * jnp.mean(gx * xh, axis=-1, keepdims=True))
        do_ref[...] = (d_og * _silu(za)).astype(BF16)
        dza_ref[...] = (d_og * o_ref[...].astype(F32) * _dsilu(za)).astype(BF16)
        ygv = yg_ref[...]
        sg = _sigmoid(gp_ref[...] + b_ref[...])
        y2 = ygv * sg
        sz = _silu(zs)
        os_ = y2 * sz
        dms = dm[:, ATTN_W:]
        rs = lax.rsqrt(jnp.mean(os_ * os_, axis=-1, keepdims=True) + NORM_EPS)
        xs = os_ * rs
        gws_ref[...] += jnp.sum(dms * xs, axis=0, keepdims=True)
        gxs = dms * ws_ref[...]
        d_os = rs * (gxs - xs * jnp.mean(gxs * xs, axis=-1, keepdims=True))
        dzs_ref[...] = (d_os * y2 * _dsilu(zs)).astype(BF16)
        d_y2 = d_os * sz
        d_g = d_y2 * ygv * sg * (1.0 - sg)
        d_g_b = d_g.astype(BF16)
        dg_ref[...] = d_g_b
        gb_ref[...] += jnp.sum(d_g, axis=0, keepdims=True)
        dyg_ref[...] = d_y2 * sg + lax.dot_general(d_g_b, wg_ref[...], _NT, preferred_element_type=F32)

    row = lambda w: pl.BlockSpec((1, w), lambda i: (0, 0))
    full = lambda w: pl.BlockSpec((tm, w), lambda i: (i, 0))
    half = lambda c: pl.BlockSpec((tm, 512), lambda i: (i, c))
    return pl.pallas_call(
        body,
        name="merge_bwd",
        grid=(L // tm,),
        in_specs=[full(D_MODEL), pl.BlockSpec((D_MODEL, D_MODEL), lambda i: (0, 0)),
                  pl.BlockSpec((SSM_W, SSM_W), lambda i: (0, 0)),
                  full(ATTN_W), full(ATTN_W), full(SSM_W), full(SSM_W),
                  half(3), half(4), half(7), half(8), row(SSM_W), row(ATTN_W), row(SSM_W)],
        out_specs=[full(ATTN_W), full(ATTN_W), full(SSM_W), full(SSM_W), full(SSM_W),
                   row(ATTN_W), row(SSM_W), row(SSM_W)],
        out_shape=[jax.ShapeDtypeStruct((L, ATTN_W), BF16), jax.ShapeDtypeStruct((L, ATTN_W), BF16),
                   jax.ShapeDtypeStruct((L, SSM_W), BF16), jax.ShapeDtypeStruct((L, SSM_W), BF16),
                   jax.ShapeDtypeStruct((L, SSM_W), F32),
                   jax.ShapeDtypeStruct((1, ATTN_W), F32), jax.ShapeDtypeStruct((1, SSM_W), F32),
                   jax.ShapeDtypeStruct((1, SSM_W), F32)],
        compiler_params=_cp(("arbitrary",)),
    )(d_out_b, w_out, w_glu, og, o, yg, gpre, proj, proj, proj, proj, b_glu.reshape(1, SSM_W), wa.reshape(1, ATTN_W),
      ws.reshape(1, SSM_W))


def _rms_bwd_x(x, norm_w, d_hn, d_out, ride):
    L = x.shape[0]
    tm = _tile(L, 256)

    def body(x_ref, w_ref, dh_ref, do_ref, gx_ref, gw_ref):
        i = pl.program_id(0)

        @pl.when(i == 0)
        def _():
            gw_ref[...] = jnp.zeros_like(gw_ref)

        xv, dh = x_ref[...], dh_ref[...]
        r = lax.rsqrt(jnp.mean(xv * xv, axis=-1, keepdims=True) + NORM_EPS)
        xh = xv * r
        gw_ref[...] += jnp.sum(dh * xh, axis=0, keepdims=True)
        gx = dh * w_ref[...]
        gx_ref[...] = do_ref[...] + r * (gx - xh * jnp.mean(gx * xh, axis=-1, keepdims=True))

    blk = pl.BlockSpec((tm, D_MODEL), lambda i: (i, 0))
    row = pl.BlockSpec((1, D_MODEL), lambda i: (0, 0))
    return _call(body, "rms_bwd_x", (L // tm,), [blk, row, blk, blk], [blk, row],
                 [jax.ShapeDtypeStruct((L, D_MODEL), F32), jax.ShapeDtypeStruct((1, D_MODEL), F32)],
                 (x, norm_w.reshape(1, D_MODEL), d_hn, d_out), ride=ride)


def _rope_table(positions):
    inv_freq = ROPE_THETA ** (-jnp.arange(0, HEAD_DIM, 2, dtype=F32) / HEAD_DIM)
    ang = positions.astype(F32)[:, None] * inv_freq
    sign = jnp.where(jnp.arange(128) % HEAD_DIM < HEAD_DIM // 2, -1.0, 1.0)
    return jnp.concatenate([jnp.tile(jnp.cos(ang), (1, 4)), jnp.tile(jnp.sin(ang), (1, 4)) * sign], axis=1)


def _step(x, positions, target, w, core, chip):
    small = {n: w[n] for n in _SMALL}
    tab = _rope_table(positions)
    mt_b, scat_b, ocat_b, a16 = _ssm_prep(small)
    perm = _chunk_perm()
    blocks = lambda t: t.reshape(N_DEV, t.shape[0] // N_DEV, t.shape[1])

    proj, hn, wt_in = _rms_inproj_gather(x, small["norm_w"], w["w_in"].T.astype(BF16), chip)
    wt_in = wt_in.reshape(IN_W, D_MODEL)
    q_rot, k_rot = _qk_prep(proj, tab, small["q_norm_w"], small["k_norm_w"])
    (og, o, lse), (w_glu,) = _attn_fwd(q_rot, k_rot, proj, small["sinks"],
                                       _gather_exchange([w["w_glu"].astype(BF16)]))
    (y, yg, hx), (w_out,) = _ssm_fwd(proj, perm, mt_b, scat_b, ocat_b, a16, small["d_skip"],
                                     _gather_exchange([w["w_out"].astype(BF16)]))
    w_glu, w_out = w_glu.reshape(SSM_W, SSM_W), w_out.reshape(D_MODEL, D_MODEL)
    merged, gpre = _merge(og, yg, w_glu, proj, small["b_glu"], small["attn_out_norm_w"], small["ssm_out_norm_w"])
    d_out, d_out_b, loss_parts = _outproj_loss(merged, w_out, x, target)
    loss = 0.5 * jnp.sum(loss_parts[:, 0, 0]) / D_MODEL

    g_w_out = blocks(_mm(merged, d_out_b, "tn", F32, "grad_w_out", tm=1024))
    d_o, d_za, d_zs, d_g, d_yg, g_wa, g_ws, g_bglu = _merge_bwd(
        d_out_b, w_out, w_glu, og, o, yg, gpre, proj, small["b_glu"], small["attn_out_norm_w"],
        small["ssm_out_norm_w"])
    g_w_glu = blocks(_mm(yg, d_g, "tn", F32, "grad_w_glu"))
    (d_u, g_mt, g_scat, g_ocat, g_a16, g_dskip), (ra_out, ra_glu) = _ssm_bwd(
        d_yg, y, proj, hx, perm, mt_b, scat_b, ocat_b, a16, small["d_skip"], _pair_exchange([g_w_out, g_w_glu]))
    p_out = _pair_sum(g_w_out, ra_out, core, BF16, "pair_sum_out")
    p_glu = _pair_sum(g_w_glu, ra_glu, core, BF16, "pair_sum_glu")
    (d_q, d_k, d_v, g_sinks), (rb_out, rb_glu) = _attn_bwd(
        q_rot, k_rot, proj, small["sinks"], d_o, o, lse, _chip_exchange([p_out, p_glu]))
    d_proj, g_qw, g_kw = _qk_prep_bwd(proj, tab, small["q_norm_w"], small["k_norm_w"], d_q, d_k, d_v,
                                      d_za, d_u, d_zs)
    g_qw = g_qw[0, :HEAD_DIM] + g_qw[0, HEAD_DIM:]
    g_kw = g_kw[0, :HEAD_DIM] + g_kw[0, HEAD_DIM:]
    g_in_a = blocks(_mm(d_proj, hn, "tn", F32, "grad_w_in_a", tm=1152, panel=0))
    g_in_b, (ra_a,) = _mm(d_proj, hn, "tn", F32, "grad_w_in_b", tm=1152, panel=1, ride=_pair_exchange([g_in_a]))
    g_in_b = blocks(g_in_b)
    p_a = _pair_sum(g_in_a, ra_a, core, BF16, "pair_sum_in_a")
    d_hn, (rb_a, ra_b) = _mm(d_proj, wt_in, "nn", F32, "d_hn", tm=1024,
                             ride=_both(_chip_exchange([p_a]), _pair_exchange([g_in_b])))
    p_b = _pair_sum(g_in_b, ra_b, core, BF16, "pair_sum_in_b")
    g_small, (rb_b,) = _ssm_prep_bwd(small, g_mt, g_scat, g_ocat, g_a16, _chip_exchange([p_b]))
    (grad_x, g_nw), _ = _rms_bwd_x(x, small["norm_w"], d_hn, d_out, None)

    g_small.update(norm_w=g_nw.reshape(-1), q_norm_w=g_qw.reshape(-1), k_norm_w=g_kw.reshape(-1),
                   sinks=g_sinks[0, :N_HEADS], d_skip=g_dskip.reshape(-1), b_glu=g_bglu.reshape(-1),
                   attn_out_norm_w=g_wa.reshape(-1), ssm_out_norm_w=g_ws.reshape(-1))
    g_packed = _slab_all_reduce(_pack(g_small, loss).reshape(N_DEV, _PACK_ROWS // N_DEV, 128))
    g_packed = g_packed.reshape(_PACK_ROWS, 128)
    grads = _unpack(g_packed, w)
    parts = dict(w_in=([p_a, p_b], [rb_a, rb_b]), w_glu=([p_glu], [rb_glu]), w_out=([p_out], [rb_out]))
    return g_packed[_LOSS_ROW, 0], grad_x, grads, parts


_ANY = pl.BlockSpec(memory_space=pl.ANY)


class _Exchange:
    def __init__(self, arrays, out_shape, sems, start, finish, relay=None):
        self.arrays, self.out_shape, self.sems, self.start, self.finish = arrays, out_shape, sems, start, finish
        self.relay = relay if relay is not None else (lambda ins, outs, sems: None)


def _gather_exchange(blocks):
    n = len(blocks)

    def parts(ins, outs, sems):
        send_sems, recv_sems, local_sems = sems
        x, y, c = lax.axis_index("x"), lax.axis_index("y"), lax.axis_index("c")
        me, sibling = (x, y, c), (x, y, 1 - c)
        chips = [(1 - x, y), (x, 1 - y), (1 - x, 1 - y)]

        def slot(k, dev):
            return outs[k].at[4 * dev[0] + 2 * dev[1] + dev[2]]

        def copy(k, q, block, to, src=None):
            return pltpu.make_async_remote_copy(
                src_ref=slot(k, block) if src is None else src, dst_ref=slot(k, block),
                send_sem=send_sems.at[k, q], recv_sem=recv_sems.at[k, q], device_id=to, device_id_type=MESH)

        mine = [pltpu.make_async_copy(ins[k], slot(k, me), local_sems.at[k]) for k in range(n)]
        first = []
        for k in range(n):
            first.append(copy(k, 0, me, sibling, src=ins[k]))
            first += [copy(k, 1 + j, me, (*chip, c), src=ins[k]) for j, chip in enumerate(chips)]
        return me, sibling, chips, c, copy, mine, first

    def start(ins, outs, sems):
        *_, mine, first = parts(ins, outs, sems)
        for cp in mine + first:
            cp.start()

    def relay(ins, outs, sems):
        me, sibling, chips, c, copy, _, _ = parts(ins, outs, sems)
        for j, chip in enumerate(chips):
            for k in range(n):
                copy(k, 1 + j, (*chip, c), me).wait_recv()
                copy(k, 4 + j, (*chip, c), sibling).start()

    def finish(ins, outs, sems):
        me, sibling, chips, c, copy, mine, first = parts(ins, outs, sems)
        for k in range(n):
            copy(k, 0, sibling, me).wait_recv()
            for j, chip in enumerate(chips):
                copy(k, 4 + j, (*chip, 1 - c), me).wait_recv()
        for cp in first + [copy(k, 4 + j, (*chip, c), sibling) for k in range(n) for j, chip in enumerate(chips)]:
            cp.wait_send()
        for cp in mine:
            cp.wait()

    return _Exchange(blocks, [jax.ShapeDtypeStruct((N_DEV,) + b.shape, b.dtype) for b in blocks],
                     [pltpu.SemaphoreType.DMA((n, 7)), pltpu.SemaphoreType.DMA((n, 7)), pltpu.SemaphoreType.DMA((n,))],
                     start, finish, relay)


def _direct_exchange(arrays, out_lead, fan, route):
    n = len(arrays)

    def copies(ins, outs, sems):
        send_sems, recv_sems = sems
        legs = route(lax.axis_index("x"), lax.axis_index("y"), lax.axis_index("c"))
        return [pltpu.make_async_remote_copy(
            src_ref=ins[k].at[src], dst_ref=outs[k].at[q], send_sem=send_sems.at[k, q], recv_sem=recv_sems.at[k, q],
            device_id=to, device_id_type=MESH) for k in range(n) for src, q, to in legs]

    def start(ins, outs, sems):
        for cp in copies(ins, outs, sems):
            cp.start()

    def finish(ins, outs, sems):
        for cp in copies(ins, outs, sems):
            cp.wait()

    return _Exchange(arrays, [jax.ShapeDtypeStruct((out_lead,) + a.shape[1:], a.dtype) for a in arrays],
                     [pltpu.SemaphoreType.DMA((n, fan)), pltpu.SemaphoreType.DMA((n, fan))], start, finish)


def _pair_exchange(grads):
    return _direct_exchange(grads, 4, 4, lambda x, y, c: [(2 * chip + (1 - c), chip, (x, y, 1 - c))
                                                          for chip in range(4)])


def _chip_exchange(parts):
    def route(x, y, c):
        chips = [(1 - x, y), (x, 1 - y), (1 - x, 1 - y)]
        return [(2 * chip[0] + chip[1], q, (*chip, c)) for q, chip in enumerate(chips)]
    return _direct_exchange(parts, 3, 3, route)


def _both(ex1, ex2):
    n1, s1 = len(ex1.arrays), len(ex1.sems)

    def halves(ins, outs, sems):
        return (ins[:n1], outs[:n1], sems[:s1]), (ins[n1:], outs[n1:], sems[s1:])

    def start(ins, outs, sems):
        h1, h2 = halves(ins, outs, sems)
        ex1.start(*h1)
        ex2.start(*h2)

    def relay(ins, outs, sems):
        h1, h2 = halves(ins, outs, sems)
        ex1.relay(*h1)
        ex2.relay(*h2)

    def finish(ins, outs, sems):
        h1, h2 = halves(ins, outs, sems)
        ex1.finish(*h1)
        ex2.finish(*h2)

    return _Exchange(list(ex1.arrays) + list(ex2.arrays), list(ex1.out_shape) + list(ex2.out_shape),
                     list(ex1.sems) + list(ex2.sems), start, finish, relay)


def _call(body, name, grid, in_specs, out_specs, out_shape, args, scratch_shapes=(), ride=None):
    if ride is None:
        sem = ("arbitrary",) * len(grid)
        return pl.pallas_call(body, name=name, grid=grid, in_specs=in_specs, out_specs=out_specs, out_shape=out_shape,
                              scratch_shapes=list(scratch_shapes), compiler_params=_cp(sem))(*args), None
    n_in, n_out, n_scr, n_x = len(in_specs), len(out_specs), len(scratch_shapes), len(ride.arrays)

    def wrapped(*refs):
        ins, refs = refs[:n_in], refs[n_in:]
        x_in, refs = refs[:n_x], refs[n_x:]
        outs, refs = refs[:n_out], refs[n_out:]
        x_out, refs = refs[:n_x], refs[n_x:]
        scr, sems = refs[:n_scr], refs[n_scr:]
        step, total = pl.program_id(0), grid[0]
        for a in range(1, len(grid)):
            step, total = step * grid[a] + pl.program_id(a), total * grid[a]
        @pl.when(step == 0)
        def _():
            ride.start(x_in, x_out, sems)

        @pl.when(step == max(total - 2, 0))
        def _():
            ride.relay(x_in, x_out, sems)

        body(*ins, *outs, *scr)

        @pl.when(step == total - 1)
        def _():
            ride.finish(x_in, x_out, sems)

    res = pl.pallas_call(
        wrapped, name=name, grid=grid, in_specs=list(in_specs) + [_ANY] * n_x,
        out_specs=list(out_specs) + [_ANY] * n_x, out_shape=list(out_shape) + list(ride.out_shape),
        scratch_shapes=list(scratch_shapes) + list(ride.sems),
        compiler_params=_cp(("arbitrary",) * len(grid)))(*args, *ride.arrays)
    return res[:n_out], list(res[n_out:])


def _pair_sum(g, ra, core, out_dtype, name):
    _, r, C = g.shape
    tr = _tile(r, 576)

    def body(c_ref, g_ref, ra_ref, p_ref):
        p_ref[...] = (g_ref[...] + ra_ref[...]).astype(p_ref.dtype)

    return pl.pallas_call(
        body,
        name=name,
        grid_spec=pltpu.PrefetchScalarGridSpec(
            num_scalar_prefetch=1,
            grid=(4, r // tr),
            in_specs=[pl.BlockSpec((1, tr, C), lambda j, t, c_ref: (2 * j + c_ref[0], t, 0)),
                      pl.BlockSpec((1, tr, C), lambda j, t, c_ref: (j, t, 0))],
            out_specs=pl.BlockSpec((1, tr, C), lambda j, t, c_ref: (j, t, 0)),
        ),
        out_shape=jax.ShapeDtypeStruct((4, r, C), out_dtype),
        compiler_params=_cp(("parallel", "parallel")),
    )(core, g, ra)


def _slab_all_reduce(slab):
    _, r, lanes = slab.shape

    def body(s_ref, o_ref, ra, rb, ps, sems_a, sems_b, sems_c):
        x, y, c = lax.axis_index("x"), lax.axis_index("y"), lax.axis_index("c")
        chips = [(1 - x, y), (x, 1 - y), (1 - x, 1 - y)]
        pair = [pltpu.make_async_remote_copy(
            src_ref=s_ref.at[2 * k + (1 - c)], dst_ref=ra.at[k], send_sem=sems_a.at[0, k], recv_sem=sems_a.at[1, k],
            device_id=(x, y, 1 - c), device_id_type=MESH) for k in range(4)]
        for cp in pair:
            cp.start()
        for cp in pair:
            cp.wait()
        for k in range(4):
            ps[k] = s_ref[2 * k + c] + ra[k]
        cross = [pltpu.make_async_remote_copy(
            src_ref=ps.at[2 * ch[0] + ch[1]], dst_ref=rb.at[q], send_sem=sems_b.at[0, q], recv_sem=sems_b.at[1, q],
            device_id=(*ch, c), device_id_type=MESH) for q, ch in enumerate(chips)]
        for cp in cross:
            cp.start()
        for cp in cross:
            cp.wait()
        me = 4 * x + 2 * y + c
        o_ref[me] = ((ps[2 * x + y] + rb[0]) + rb[1]) + rb[2]
        flips = [(dx, dy, dc) for dx in (0, 1) for dy in (0, 1) for dc in (0, 1) if dx + dy + dc]
        spread = [pltpu.make_async_remote_copy(
            src_ref=o_ref.at[me], dst_ref=o_ref.at[me], send_sem=sems_c.at[0, q], recv_sem=sems_c.at[1, q],
            device_id=(x + dx - 2 * x * dx, y + dy - 2 * y * dy, c + dc - 2 * c * dc), device_id_type=MESH)
            for q, (dx, dy, dc) in enumerate(flips)]
        for cp in spread:
            cp.start()
        for q, (dx, dy, dc) in enumerate(flips):
            peer = 4 * (x + dx - 2 * x * dx) + 2 * (y + dy - 2 * y * dy) + (c + dc - 2 * c * dc)
            pltpu.make_async_remote_copy(
                src_ref=o_ref.at[peer], dst_ref=o_ref.at[peer], send_sem=sems_c.at[0, q], recv_sem=sems_c.at[1, q],
                device_id=(x, y, c), device_id_type=MESH).wait_recv()
        for cp in spread:
            cp.wait_send()

    whole = pl.BlockSpec(memory_space=pltpu.VMEM)
    return pl.pallas_call(
        body, name="slab_all_reduce", in_specs=[whole], out_specs=whole,
        out_shape=jax.ShapeDtypeStruct(slab.shape, F32),
        scratch_shapes=[pltpu.VMEM((4, r, lanes), F32), pltpu.VMEM((3, r, lanes), F32), pltpu.VMEM((4, r, lanes), F32),
                        pltpu.SemaphoreType.DMA((2, 4)), pltpu.SemaphoreType.DMA((2, 3)),
                        pltpu.SemaphoreType.DMA((2, 7))],
        compiler_params=_cp(),
    )(slab)


def _adamw_reduced(ps, rbs, chip, w, m, v, name):
    nh = len(ps)
    R, C = w.shape
    ch = C // nh
    tr = _tile(R, 288)
    nt = R // tr
    c1 = 1.0 - ADAM_B1 ** ADAM_STEP
    c2 = 1.0 - ADAM_B2 ** ADAM_STEP

    def body(c_ref, *refs):
        p_refs, rb_refs = refs[:nh], refs[nh:2 * nh]
        w_ref, m_ref, v_ref, g_ref, d_ref, nm_ref, nv_ref = refs[2 * nh:]
        for h in range(nh):
            @pl.when(pl.program_id(0) == h)
            def _(h=h):
                rb = rb_refs[h]
                gv = p_refs[h][0].astype(F32) + rb[0].astype(F32)
                gv = gv + rb[1].astype(F32)
                gv = gv + rb[2].astype(F32)
                nm = ADAM_B1 * m_ref[...] + (1.0 - ADAM_B1) * gv
                nv = ADAM_B2 * v_ref[...] + (1.0 - ADAM_B2) * (gv * gv)
                g_ref[...] = gv
                nm_ref[...] = nm
                nv_ref[...] = nv
                d_ref[...] = -ADAM_LR * ((nm / c1) / (jnp.sqrt(nv / c2) + ADAM_EPS) + ADAM_WD * w_ref[...])

    def held(h):
        return lambda hh, tt: jnp.where(hh == h, tt, jnp.where(hh < h, 0, nt - 1))

    p_specs = [pl.BlockSpec((1, tr, ch), lambda hh, tt, c_ref, f=held(h): (c_ref[0], f(hh, tt), 0))
               for h in range(nh)]
    rb_specs = [pl.BlockSpec((3, tr, ch), lambda hh, tt, c_ref, f=held(h): (0, f(hh, tt), 0)) for h in range(nh)]
    blk = pl.BlockSpec((tr, ch), lambda hh, tt, c_ref: (tt, hh))
    return pl.pallas_call(
        body,
        name=name,
        grid_spec=pltpu.PrefetchScalarGridSpec(
            num_scalar_prefetch=1, grid=(nh, nt), in_specs=p_specs + rb_specs + [blk] * 3, out_specs=[blk] * 4),
        out_shape=[jax.ShapeDtypeStruct((R, C), F32)] * 4,
        compiler_params=_cp(("arbitrary", "arbitrary")),
    )(chip, *ps, *rbs, w, m, v)


_SMALL = ("norm_w", "q_norm_w", "k_norm_w", "sinks", "a_re", "a_im", "log_step", "b_re", "b_im", "c_re", "c_im",
          "d_skip", "b_glu", "attn_out_norm_w", "ssm_out_norm_w")
_WEIGHTS = ("norm_w", "w_in", "q_norm_w", "k_norm_w", "sinks", "a_re", "a_im", "log_step", "b_re", "b_im", "c_re",
            "c_im", "d_skip", "w_glu", "b_glu", "attn_out_norm_w", "ssm_out_norm_w", "w_out")
_SMALL_2D = dict(norm_w=(1, 2048), q_norm_w=(1, 64), k_norm_w=(1, 64), sinks=(1, 16), a_re=(64, 64), a_im=(64, 64),
                 log_step=(1, 64), b_re=(1024, 64), b_im=(1024, 64), c_re=(1024, 64), c_im=(1024, 64),
                 d_skip=(1, 1024), b_glu=(1, 1024), attn_out_norm_w=(1, 1024), ssm_out_norm_w=(1, 1024))
_P_MINOR = ("b_re", "b_im")


def _flat_form(n, t):
    return t.transpose(0, 2, 1) if n in _P_MINOR else t


def _own_form(n, t, shape):
    if n in _P_MINOR:
        return t.reshape(shape[0], shape[2], shape[1]).transpose(0, 2, 1)
    return t.reshape(shape)


def _slab_rows(n):
    return -(-n // 1024) * 8


_PACK_ROWS = 2304


_LOSS_ROW = 2192


def _pack(d, loss):
    parts = []
    for n in _SMALL:
        flat = _flat_form(n, d[n]).reshape(-1).astype(F32)
        rows = _slab_rows(flat.shape[0])
        parts.append(jnp.pad(flat, (0, rows * 128 - flat.shape[0])).reshape(rows, 128))
    assert sum(p.shape[0] for p in parts) == _LOSS_ROW
    parts.append(jnp.pad(loss.reshape(1, 1), ((0, _PACK_ROWS - _LOSS_ROW - 1), (0, 127))))
    return jnp.concatenate(parts, axis=0)


def _unpack(packed, like):
    out, off = {}, 0
    for n in _SMALL:
        size = math.prod(like[n].shape)
        rows = _slab_rows(size)
        out[n] = _own_form(n, packed[off:off + rows].reshape(-1)[:size], like[n].shape)
        off += rows
    return out


def _adamw_small(g, w, m, v):
    c1 = 1.0 - ADAM_B1 ** ADAM_STEP
    c2 = 1.0 - ADAM_B2 ** ADAM_STEP
    k = len(_SMALL)

    def body(*refs):
        ins, outs = refs[:4 * k], refs[4 * k:]
        for j in range(k):
            gv, wv, mv, vv = (ins[q * k + j][...] for q in range(4))
            nm = ADAM_B1 * mv + (1.0 - ADAM_B1) * gv
            nv = ADAM_B2 * vv + (1.0 - ADAM_B2) * (gv * gv)
            outs[j][...] = -ADAM_LR * ((nm / c1) / (jnp.sqrt(nv / c2) + ADAM_EPS) + ADAM_WD * wv)
            outs[k + j][...] = nm
            outs[2 * k + j][...] = nv

    args = [_flat_form(n, d[n]).reshape(_SMALL_2D[n]) for d in (g, w, m, v) for n in _SMALL]
    shapes = [jax.ShapeDtypeStruct(_SMALL_2D[n], F32) for _ in range(3) for n in _SMALL]
    outs = pl.pallas_call(body, name="adamw_small", out_shape=shapes, compiler_params=_cp())(*args)
    res = []
    for q in range(3):
        res.append({n: _own_form(n, outs[q * k + j], w[n].shape) for j, n in enumerate(_SMALL)})
    return res


def kernel(x, positions, norm_w, w_in, q_norm_w, k_norm_w, sinks, a_re, a_im, log_step, b_re, b_im, c_re, c_im, d_skip, w_glu, b_glu, attn_out_norm_w, ssm_out_norm_w, w_out, loss_target, m_norm_w, m_w_in, m_q_norm_w, m_k_norm_w, m_sinks, m_a_re, m_a_im, m_log_step, m_b_re, m_b_im, m_c_re, m_c_im, m_d_skip, m_w_glu, m_b_glu, m_attn_out_norm_w, m_ssm_out_norm_w, m_w_out, v_norm_w, v_w_in, v_q_norm_w, v_k_norm_w, v_sinks, v_a_re, v_a_im, v_log_step, v_b_re, v_b_im, v_c_re, v_c_im, v_d_skip, v_w_glu, v_b_glu, v_attn_out_norm_w, v_ssm_out_norm_w, v_w_out):
    w = dict(norm_w=norm_w, w_in=w_in, q_norm_w=q_norm_w, k_norm_w=k_norm_w, sinks=sinks, a_re=a_re, a_im=a_im,
             log_step=log_step, b_re=b_re, b_im=b_im, c_re=c_re, c_im=c_im, d_skip=d_skip, w_glu=w_glu, b_glu=b_glu,
             attn_out_norm_w=attn_out_norm_w, ssm_out_norm_w=ssm_out_norm_w, w_out=w_out)
    m = dict(norm_w=m_norm_w, w_in=m_w_in, q_norm_w=m_q_norm_w, k_norm_w=m_k_norm_w, sinks=m_sinks, a_re=m_a_re,
             a_im=m_a_im, log_step=m_log_step, b_re=m_b_re, b_im=m_b_im, c_re=m_c_re, c_im=m_c_im, d_skip=m_d_skip,
             w_glu=m_w_glu, b_glu=m_b_glu, attn_out_norm_w=m_attn_out_norm_w, ssm_out_norm_w=m_ssm_out_norm_w,
             w_out=m_w_out)
    v = dict(norm_w=v_norm_w, w_in=v_w_in, q_norm_w=v_q_norm_w, k_norm_w=v_k_norm_w, sinks=v_sinks, a_re=v_a_re,
             a_im=v_a_im, log_step=v_log_step, b_re=v_b_re, b_im=v_b_im, c_re=v_c_re, c_im=v_c_im, d_skip=v_d_skip,
             w_glu=v_w_glu, b_glu=v_b_glu, attn_out_norm_w=v_attn_out_norm_w, ssm_out_norm_w=v_ssm_out_norm_w,
             w_out=v_w_out)
    core = lax.axis_index("c").astype(jnp.int32).reshape(1)
    chip = (2 * lax.axis_index("x") + lax.axis_index("y")).astype(jnp.int32).reshape(1)

    loss, grad_x, grads, parts = _step(x[0], positions[0], loss_target[0], w, core, chip)
    delta, new_m, new_v = {}, {}, {}
    for n in ("w_glu", "w_out"):
        grads[n], delta[n], new_m[n], new_v[n] = _adamw_reduced(*parts[n], chip, w[n], m[n], v[n], f"adamw_{n}")
    g_t, d_t, m_t, v_t = _adamw_reduced(*parts["w_in"], chip, w["w_in"].T, m["w_in"].T, v["w_in"].T, "adamw_w_in")
    grads["w_in"], delta["w_in"], new_m["w_in"], new_v["w_in"] = g_t.T, d_t.T, m_t.T, v_t.T
    d_s, m_s, v_s = _adamw_small(grads, w, m, v)
    delta.update(d_s)
    new_m.update(m_s)
    new_v.update(v_s)

    return (loss, grad_x[None], *[grads[n] for n in _WEIGHTS], *[delta[n] for n in _WEIGHTS],
            *[new_m[n] for n in _WEIGHTS], *[new_v[n] for n in _WEIGHTS])
```

```python
import math

import jax
import jax.numpy as jnp
from jax import lax
from jax.experimental import pallas as pl
from jax.experimental.pallas import tpu as pltpu

F32 = jnp.float32
BF16 = jnp.bfloat16

D_MODEL = 2048
ATTN_W = 1024
KV_W = 256
SSM_W = 1024
HEAD_DIM = 64
N_HEADS = 16
N_KV = 4
IN_W = 4608
BLOCK = 128
ROPE_THETA = 10000.0
NORM_EPS = 1e-6
SSM_G = 64
SSM_P = 64
SSM_H = 16
CHUNK = 16
CW = CHUNK * SSM_H
N_DEV = 8

ADAM_LR = 0.001
ADAM_B1 = 0.9
ADAM_B2 = 0.999
ADAM_EPS = 1e-08
ADAM_WD = 0.01
ADAM_STEP = 10

VMEM_LIMIT = 56 * 1024 * 1024
MESH = pl.DeviceIdType.MESH


def _cp(sem=None):
    if sem is None:
        return pltpu.CompilerParams(vmem_limit_bytes=VMEM_LIMIT)
    return pltpu.CompilerParams(vmem_limit_bytes=VMEM_LIMIT, dimension_semantics=sem)


def _sigmoid(x):
    return 0.5 * jnp.tanh(0.5 * x) + 0.5


def _silu(x):
    return x * _sigmoid(x)


def _dsilu(x):
    s = _sigmoid(x)
    return s * (1.0 + x * (1.0 - s))


_GELU_C = math.sqrt(2.0 / math.pi)


def _gelu(y):
    t = jnp.tanh(_GELU_C * (y + 0.044715 * y * y * y))
    return 0.5 * y * (1.0 + t)


def _dgelu(y):
    t = jnp.tanh(_GELU_C * (y + 0.044715 * y * y * y))
    return 0.5 * (1.0 + t) + 0.5 * y * (1.0 - t * t) * _GELU_C * (1.0 + 3.0 * 0.044715 * y * y)


def _tile(n, want):
    if n <= want:
        return n
    for t in range(want - want % 16, 0, -16):
        if n % t == 0:
            return t
    raise ValueError((n, want))


def _mm(a, b, mode, out_dtype, name, tm=512, tn=1024, ride=None, panel=None):
    if mode == "nn":
        (M, K), (K2, N) = a.shape, b.shape
    else:
        (K, M), (K2, N) = a.shape, b.shape
    assert K == K2
    tm, tn = _tile(M, tm), _tile(N, tn)
    p0 = 0
    if panel is not None:
        p0, N = panel, tn
    dn = {"nn": _NN, "tn": _TN}[mode]

    def body(a_ref, b_ref, o_ref):
        acc = lax.dot_general(a_ref[...].astype(BF16), b_ref[...].astype(BF16), dn, preferred_element_type=F32)
        o_ref[...] = acc.astype(o_ref.dtype)

    a_spec = pl.BlockSpec((K, tm), lambda j, i: (0, i)) if mode == "tn" else pl.BlockSpec((tm, K), lambda j, i: (i, 0))
    b_spec = pl.BlockSpec((K, tn), lambda j, i: (0, j + p0))
    o_spec = pl.BlockSpec((tm, tn), lambda j, i: (i, j))
    if ride is not None:
        (out,), landed = _call(body, name, (N // tn, M // tm), [a_spec, b_spec], [o_spec],
                               [jax.ShapeDtypeStruct((M, N), out_dtype)], (a, b), ride=ride)
        return out, landed
    return pl.pallas_call(
        body,
        name=name,
        grid=(N // tn, M // tm),
        in_specs=[a_spec, b_spec],
        out_specs=o_spec,
        out_shape=jax.ShapeDtypeStruct((M, N), out_dtype),
        compiler_params=_cp(("parallel", "parallel")),
    )(a, b)


_CHIP_ORDER = (0, 2, 1, 3)


def _rms_inproj_gather(x, norm_w, wt_shard, chip):
    L = x.shape[0]
    tm = _tile(L, 1024)
    ni = L // tm
    r = IN_W // N_DEV
    tn = 2 * r

    def body(chip_ref, x_ref, nw_ref, shard, proj_ref, hn_hbm, wt_hbm, hn_scr, w_scr, send_sems, recv_sems, loc_sems):
        jc, i = pl.program_id(0), pl.program_id(1)
        xx, yy, c = lax.axis_index("x"), lax.axis_index("y"), lax.axis_index("c")
        me, sibling = (xx, yy, c), (xx, yy, 1 - c)
        chips = [(1 - xx, yy), (xx, 1 - yy), (1 - xx, 1 - yy)]

        def slot(dev):
            return wt_hbm.at[4 * dev[0] + 2 * dev[1] + dev[2]]

        def copy(q, block, to, src=None):
            return pltpu.make_async_remote_copy(
                src_ref=slot(block) if src is None else src, dst_ref=slot(block),
                send_sem=send_sems.at[q], recv_sem=recv_sems.at[q], device_id=to, device_id_type=MESH)

        def rows_of(buf, core):
            return w_scr.at[buf, pl.ds(pl.multiple_of(core * r, 16), r)]

        mine = pltpu.make_async_copy(shard, slot(me), loc_sems.at[0])
        sends = [copy(0, me, sibling, src=shard)] + [copy(1 + j, me, (*ch, c), src=shard) for j, ch in enumerate(chips[:2])]
        relay_block = (xx + (1 - c) * (1 - 2 * xx), yy + c * (1 - 2 * yy), c)
        relay = copy(3, relay_block, (xx + c * (1 - 2 * xx), yy + (1 - c) * (1 - 2 * yy), c))
        first = jnp.logical_and(jc == 0, i == 0)

        @pl.when(first)
        def _():
            mine.start()
            for cp in sends:
                cp.start()
            own = pltpu.make_async_copy(shard, rows_of(0, c), loc_sems.at[1])
            own.start()
            copy(0, sibling, me).wait_recv()
            sib = pltpu.make_async_copy(slot(sibling), rows_of(0, 1 - c), loc_sems.at[2])
            sib.start()
            own.wait()
            sib.wait()

        def to_vmem(j, ch):
            pltpu.make_async_copy(slot((*ch, c)), rows_of((1 + j) % 2, c), loc_sems.at[1 + j]).start()

        @pl.when(jnp.logical_and(jc == 1, i == 0))
        def _():
            for j in range(2):
                copy(1 + j, (*chips[j], c), me).wait_recv()
                copy(4 + j, (*chips[j], c), sibling).start()
            relay.start()
            to_vmem(0, chips[0])

        @pl.when(jnp.logical_and(jc == 1, i == ni // 2))
        def _():
            to_vmem(1, chips[1])

        @pl.when(jnp.logical_and(jc == 2, i == ni // 2))
        def _():
            copy(3, (*chips[2], c), me).wait_recv()
            copy(6, (*chips[2], c), sibling).start()
            to_vmem(2, chips[2])

        for j, ch in enumerate(chips):
            @pl.when(jnp.logical_and(jc == 1 + j, i == 0))
            def _(j=j, ch=ch):
                buf = (1 + j) % 2
                copy(4 + j, (*ch, 1 - c), me).wait_recv()
                passed = pltpu.make_async_copy(slot((*ch, 1 - c)), rows_of(buf, 1 - c), loc_sems.at[4 + j])
                passed.start()
                pltpu.make_async_copy(slot((*ch, c)), rows_of(buf, c), loc_sems.at[1 + j]).wait()
                passed.wait()

        rows = pl.ds(pl.multiple_of(i * tm, tm), tm)

        @pl.when(jc == 0)
        def _():
            xv = x_ref[...]
            rstd = lax.rsqrt(jnp.mean(xv * xv, axis=-1, keepdims=True) + NORM_EPS)
            hn_scr[rows, :] = (xv * rstd * nw_ref[...]).astype(BF16)

        keep_hn = pltpu.make_async_copy(hn_scr, hn_hbm, loc_sems.at[7])

        @pl.when(jnp.logical_and(jc == 1, i == 0))
        def _():
            keep_hn.start()

        for buf in range(2):
            @pl.when(jc % 2 == buf)
            def _(buf=buf):
                proj_ref[...] = lax.dot_general(hn_scr[rows, :], w_scr[buf], _NT, preferred_element_type=F32)

        @pl.when(jnp.logical_and(jc == 3, i == ni - 1))
        def _():
            for cp in sends + [relay]:
                cp.wait_send()
            for j, ch in enumerate(chips):
                copy(4 + j, (*ch, c), sibling).wait_send()
            mine.wait()
            keep_hn.wait()

    def tile_of(jc, chip_ref):
        mask = jnp.where(jc == 1, _CHIP_ORDER[1], jnp.where(jc == 2, _CHIP_ORDER[2], jnp.where(jc == 3, _CHIP_ORDER[3], 0)))
        return jnp.bitwise_xor(chip_ref[0], mask)

    held = lambda jc, i: jnp.where(jc == 0, i, ni - 1)
    return pl.pallas_call(
        body,
        name="rms_inproj_gather",
        grid_spec=pltpu.PrefetchScalarGridSpec(
            num_scalar_prefetch=1,
            grid=(4, ni),
            in_specs=[pl.BlockSpec((tm, D_MODEL), lambda jc, i, ch: (held(jc, i), 0)),
                      pl.BlockSpec((1, D_MODEL), lambda jc, i, ch: (0, 0)), _ANY],
            out_specs=[pl.BlockSpec((tm, tn), lambda jc, i, ch: (i, tile_of(jc, ch))), _ANY, _ANY],
            scratch_shapes=[pltpu.VMEM((L, D_MODEL), BF16), pltpu.VMEM((2, tn, D_MODEL), BF16),
                            pltpu.SemaphoreType.DMA((7,)), pltpu.SemaphoreType.DMA((7,)), pltpu.SemaphoreType.DMA((8,))],
        ),
        out_shape=[jax.ShapeDtypeStruct((L, IN_W), F32), jax.ShapeDtypeStruct((L, D_MODEL), BF16),
                   jax.ShapeDtypeStruct((N_DEV, r, D_MODEL), BF16)],
        compiler_params=_cp(("arbitrary", "arbitrary")),
    )(chip, x, norm_w.reshape(1, D_MODEL), wt_shard)


def _seg_sum(v):
    a = lax.broadcasted_iota(jnp.int32, (128, 128), 0) // HEAD_DIM
    b = lax.broadcasted_iota(jnp.int32, (128, 128), 1) // HEAD_DIM
    ones = jnp.where(a == b, 1.0, 0.0).astype(BF16)
    hi = v.astype(BF16)
    lo = (v - hi.astype(F32)).astype(BF16)
    return jnp.dot(hi, ones, preferred_element_type=F32) + jnp.dot(lo, ones, preferred_element_type=F32)


def _rot_half(t):
    lane = lax.broadcasted_iota(jnp.int32, t.shape, 1)
    return jnp.where(lane % HEAD_DIM < HEAD_DIM // 2, pltpu.roll(t, 128 - HEAD_DIM // 2, 1),
                     pltpu.roll(t, HEAD_DIM // 2, 1))


def _norm_rope(raw, w, cos, sin):
    r = lax.rsqrt(_seg_sum(raw * raw) * (1.0 / HEAD_DIM) + NORM_EPS)
    tn = raw * r * w
    return r, tn * cos + _rot_half(tn) * sin


def _norm_rope_bwd(d_rot, raw, w, cos, sin):
    r = lax.rsqrt(_seg_sum(raw * raw) * (1.0 / HEAD_DIM) + NORM_EPS)
    d_tn = d_rot * cos + _rot_half(d_rot * sin)
    xh = raw * r
    gw = d_tn * w
    d_raw = r * (gw - xh * (_seg_sum(gw * xh) * (1.0 / HEAD_DIM)))
    return d_raw, d_tn * xh


def _band_mask2(has_prev, keys_on_rows=False):
    qd, kd = (1, 0) if keys_on_rows else (0, 1)
    qi = lax.broadcasted_iota(jnp.int32, (2 * BLOCK, 2 * BLOCK), qd) % BLOCK + BLOCK
    kj = lax.broadcasted_iota(jnp.int32, (2 * BLOCK, 2 * BLOCK), kd)
    rel = qi - kj
    return (rel >= 0) & (rel < BLOCK) & ((kj >= BLOCK) | has_prev)


def _half_tiles(pair):
    lo = lax.broadcasted_iota(jnp.int32, pair.shape, 1) < HEAD_DIM
    sw = pltpu.roll(pair, HEAD_DIM, 1)
    z = jnp.zeros_like(pair)
    return (jnp.where(lo, pair, z).astype(BF16), jnp.where(lo, z, sw).astype(BF16),
            jnp.where(lo, sw, z).astype(BF16), jnp.where(lo, z, pair).astype(BF16))


def _two_rows(top, bottom):
    row = lax.broadcasted_iota(jnp.int32, (2 * BLOCK, 1), 0)
    return jnp.where(row < BLOCK, top, bottom)


_SCALE = 1.0 / math.sqrt(HEAD_DIM)
_NT = (((1,), (1,)), ((), ()))
_NN = (((1,), (0,)), ((), ()))
_TN = (((0,), (0,)), ((), ()))


def _qk_prep(proj, tab, qw, kw):
    L = proj.shape[0]
    tm = _tile(L, 512)

    def body(q_ref, k_ref, t_ref, qw_ref, kw_ref, qo_ref, ko_ref):
        cos, sin = t_ref[:, :128], t_ref[:, 128:]
        for c in range(ATTN_W // 128):
            _, qr = _norm_rope(q_ref[:, c * 128:(c + 1) * 128], qw_ref[...], cos, sin)
            qo_ref[:, c * 128:(c + 1) * 128] = (qr * _SCALE).astype(BF16)
        for c in range(KV_W // 128):
            _, kr = _norm_rope(k_ref[:, c * 128:(c + 1) * 128], kw_ref[...], cos, sin)
            ko_ref[:, c * 128:(c + 1) * 128] = kr.astype(BF16)

    row = pl.BlockSpec((1, 128), lambda i: (0, 0))
    return pl.pallas_call(
        body,
        name="qk_prep",
        grid=(L // tm,),
        in_specs=[pl.BlockSpec((tm, ATTN_W), lambda i: (i, 0)), pl.BlockSpec((tm, KV_W), lambda i: (i, 4)),
                  pl.BlockSpec((tm, 256), lambda i: (i, 0)), row, row],
        out_specs=[pl.BlockSpec((tm, ATTN_W), lambda i: (i, 0)), pl.BlockSpec((tm, KV_W), lambda i: (i, 0))],
        out_shape=[jax.ShapeDtypeStruct((L, ATTN_W), BF16), jax.ShapeDtypeStruct((L, KV_W), BF16)],
        compiler_params=_cp(("parallel",)),
    )(proj, proj, tab, jnp.tile(qw, 2).reshape(1, 128), jnp.tile(kw, 2).reshape(1, 128))


def _group_tiles(g, kt, vt):
    a, b = divmod(g, 2)
    return kt[a][2 * b], kt[a][2 * b + 1], vt[a][2 * b], vt[a][2 * b + 1]


def _attn_fwd(q, k, proj, sinks, ride):
    L = proj.shape[0]
    nb = L // BLOCK

    def body(q_ref, kc_ref, kp_ref, vc_ref, vp_ref, z0_ref, z1_ref, sink_ref, og_ref, o_ref, lse_ref):
        i = pl.program_id(0)
        mask = _band_mask2(i > 0)
        z = jnp.concatenate([z0_ref[...], z1_ref[...]], axis=1)
        lane = lax.broadcasted_iota(jnp.int32, (BLOCK, 128), 1)
        kt = [_half_tiles(jnp.concatenate([kp_ref[:, a * 128:(a + 1) * 128], kc_ref[:, a * 128:(a + 1) * 128]],
                                          axis=0).astype(F32)) for a in range(2)]
        vt = [_half_tiles(jnp.concatenate([vp_ref[:, a * 128:(a + 1) * 128], vc_ref[:, a * 128:(a + 1) * 128]],
                                          axis=0)) for a in range(2)]
        lse_mat = jnp.zeros((BLOCK, 128), F32)
        pairs = []
        for g in range(N_KV):
            k_lo, k_hi, v_lo, v_hi = _group_tiles(g, kt, vt)
            q2 = jnp.concatenate([q_ref[:, 2 * g * 128:(2 * g + 1) * 128],
                                  q_ref[:, (2 * g + 1) * 128:(2 * g + 2) * 128]], axis=0)
            for half, (kh, vh) in enumerate(((k_lo, v_lo), (k_hi, v_hi))):
                pairs.append(dict(g=g, half=half, vh=vh, s=lax.dot_general(q2, kh, _NT, preferred_element_type=F32)))
        for pr in pairs:
            h_top, h_bot = 4 * pr["g"] + pr["half"], 4 * pr["g"] + 2 + pr["half"]
            s = jnp.where(mask, pr["s"], -1e30)
            sink = _two_rows(sink_ref[h_top], sink_ref[h_bot])
            m = jnp.maximum(jnp.max(s, axis=-1, keepdims=True), sink)
            e = jnp.exp(s - m)
            den = jnp.sum(e, axis=-1, keepdims=True) + jnp.exp(sink - m)
            pr["p_b"] = (e * (1.0 / den)).astype(BF16)
            lse = m + jnp.log(den)
            lse_mat = jnp.where(lane == h_top, lse[:BLOCK], lse_mat)
            lse_mat = jnp.where(lane == h_bot, lse[BLOCK:], lse_mat)
        outs = []
        for g in range(N_KV):
            acc = (jnp.dot(pairs[2 * g]["p_b"], pairs[2 * g]["vh"], preferred_element_type=F32)
                   + jnp.dot(pairs[2 * g + 1]["p_b"], pairs[2 * g + 1]["vh"], preferred_element_type=F32))
            outs += [acc[:BLOCK], acc[BLOCK:]]
        o = jnp.concatenate(outs, axis=1)
        o_ref[...] = o.astype(BF16)
        og_ref[...] = (o * _silu(z)).astype(BF16)
        lse_ref[...] = lse_mat

    prev = lambda i: jnp.maximum(i - 1, 0)
    return _call(
        body, "attn_fwd", (nb,),
        [pl.BlockSpec((BLOCK, ATTN_W), lambda i: (i, 0)),
         pl.BlockSpec((BLOCK, KV_W), lambda i: (i, 0)),
         pl.BlockSpec((BLOCK, KV_W), lambda i: (prev(i), 0)),
         pl.BlockSpec((BLOCK, KV_W), lambda i: (i, 5)),
         pl.BlockSpec((BLOCK, KV_W), lambda i: (prev(i), 5)),
         pl.BlockSpec((BLOCK, 512), lambda i: (i, 3)),
         pl.BlockSpec((BLOCK, 512), lambda i: (i, 4)),
         pl.BlockSpec(memory_space=pltpu.SMEM)],
        [pl.BlockSpec((BLOCK, ATTN_W), lambda i: (i, 0)),
         pl.BlockSpec((BLOCK, ATTN_W), lambda i: (i, 0)),
         pl.BlockSpec((BLOCK, 128), lambda i: (i, 0))],
        [jax.ShapeDtypeStruct((L, ATTN_W), BF16), jax.ShapeDtypeStruct((L, ATTN_W), BF16),
         jax.ShapeDtypeStruct((L, 128), F32)],
        (q, k, k, proj, proj, proj, proj, sinks), ride=ride)


def _attn_bwd(q, k, proj, sinks, d_o, o, lse, ride):
    L = proj.shape[0]
    nb = L // BLOCK

    def body(q_ref, kc_ref, kp_ref, vc_ref, vp_ref, do_ref, o_ref, lse_ref, sink_ref,
             dq_ref, dk_ref, dv_ref, gs_ref, ck_scr, cv_scr):
        i = pl.program_id(0)

        @pl.when(i == 0)
        def _():
            gs_ref[...] = jnp.zeros_like(gs_ref)
            ck_scr[...] = jnp.zeros_like(ck_scr)
            cv_scr[...] = jnp.zeros_like(cv_scr)

        @pl.when(i == nb)
        def _():
            dk_ref[...] = ck_scr[...]
            dv_ref[...] = cv_scr[...]

        @pl.when(i < nb)
        def _():
            mask = _band_mask2(i > 0, keys_on_rows=True)
            lane = lax.broadcasted_iota(jnp.int32, (1, 128), 1)
            lane2 = lax.broadcasted_iota(jnp.int32, (1, 2 * BLOCK), 1)
            lo = lax.broadcasted_iota(jnp.int32, (2 * BLOCK, 128), 1) < HEAD_DIM
            lse_t = lse_ref[...].T
            prod_all = do_ref[...].astype(F32) * o_ref[...].astype(F32)
            seg = (lax.broadcasted_iota(jnp.int32, (N_HEADS, ATTN_W), 1) // HEAD_DIM
                   == lax.broadcasted_iota(jnp.int32, (N_HEADS, ATTN_W), 0)).astype(BF16)
            prod_hi = prod_all.astype(BF16)
            prod_lo = (prod_all - prod_hi.astype(F32)).astype(BF16)
            delta_t = (lax.dot_general(seg, prod_hi, _NT, preferred_element_type=F32)
                       + lax.dot_general(seg, prod_lo, _NT, preferred_element_type=F32))
            kt = [_half_tiles(jnp.concatenate([kp_ref[:, a * 128:(a + 1) * 128], kc_ref[:, a * 128:(a + 1) * 128]],
                                              axis=0).astype(F32)) for a in range(2)]
            vt = [_half_tiles(jnp.concatenate([vp_ref[:, a * 128:(a + 1) * 128], vc_ref[:, a * 128:(a + 1) * 128]],
                                              axis=0)) for a in range(2)]
            gs = jnp.zeros((1, 128), F32)
            dq_parts = []
            dk_acc = [jnp.zeros((2 * BLOCK, 128), F32) for _ in range(2)]
            dv_acc = [jnp.zeros((2 * BLOCK, 128), F32) for _ in range(2)]
            pairs = []
            for g in range(N_KV):
                k_lo, k_hi, v_lo, v_hi = _group_tiles(g, kt, vt)
                t0, t1 = slice(2 * g * 128, (2 * g + 1) * 128), slice((2 * g + 1) * 128, (2 * g + 2) * 128)
                q2 = jnp.concatenate([q_ref[:, t0], q_ref[:, t1]], axis=0)
                do2_b = jnp.concatenate([do_ref[:, t0], do_ref[:, t1]], axis=0).astype(BF16)
                for half, (kh, vh) in enumerate(((k_lo, v_lo), (k_hi, v_hi))):
                    pairs.append(dict(g=g, half=half, kh=kh, q2=q2, do2_b=do2_b,
                                      s=lax.dot_general(kh, q2, _NT, preferred_element_type=F32),
                                      dp=lax.dot_general(vh, do2_b, _NT, preferred_element_type=F32)))
            for pr in pairs:
                h_top, h_bot = 4 * pr["g"] + pr["half"], 4 * pr["g"] + 2 + pr["half"]
                pick = lambda t: jnp.concatenate([t[h_top:h_top + 1, :], t[h_bot:h_bot + 1, :]], axis=1)
                lse, delta = pick(lse_t), pick(delta_t)
                sink = jnp.where(lane2 < BLOCK, sink_ref[h_top], sink_ref[h_bot])
                p = jnp.exp(jnp.where(mask, pr["s"], -1e30) - lse)
                pr["ds_b"] = (p * (pr["dp"] - delta)).astype(BF16)
                pr["p_b"] = p.astype(BF16)
                gsink = -jnp.exp(sink - lse) * delta
                gs = gs + jnp.where(lane == h_top, jnp.sum(jnp.where(lane2 < BLOCK, gsink, 0.0)), 0.0)
                gs = gs + jnp.where(lane == h_bot, jnp.sum(jnp.where(lane2 >= BLOCK, gsink, 0.0)), 0.0)
            for g in range(N_KV):
                a, b = divmod(g, 2)
                dq2 = jnp.zeros((2 * BLOCK, 128), F32)
                dk_h, dv_h = [], []
                for pr in pairs[2 * g:2 * g + 2]:
                    dq2 = dq2 + lax.dot_general(pr["ds_b"], pr["kh"], _TN, preferred_element_type=F32)
                    dk_h.append(jnp.dot(pr["ds_b"], pr["q2"], preferred_element_type=F32))
                    dv_h.append(jnp.dot(pr["p_b"], pr["do2_b"], preferred_element_type=F32))
                dq_parts += [dq2[:BLOCK], dq2[BLOCK:]]
                for acc, parts in ((dk_acc, dk_h), (dv_acc, dv_h)):
                    t = jnp.where(lo, parts[0], parts[1])
                    t = t + pltpu.roll(t, HEAD_DIM, 1)
                    acc[a] = acc[a] + jnp.where(lo == (b == 0), t, 0.0)
            dq_ref[...] = jnp.concatenate(dq_parts, axis=1)
            dk_full = jnp.concatenate(dk_acc, axis=1)
            dv_full = jnp.concatenate(dv_acc, axis=1)
            dk_ref[...] = ck_scr[...] + dk_full[:BLOCK]
            dv_ref[...] = cv_scr[...] + dv_full[:BLOCK]
            ck_scr[...] = dk_full[BLOCK:]
            cv_scr[...] = dv_full[BLOCK:]
            gs_ref[...] += gs

    cur = lambda i: jnp.minimum(i, nb - 1)
    prev = lambda i: jnp.maximum(jnp.minimum(i, nb - 1) - 1, 0)
    done = lambda i: jnp.maximum(i - 1, 0)
    bs = pl.BlockSpec
    return _call(
        body, "attn_bwd", (nb + 1,),
        [bs((BLOCK, ATTN_W), lambda i: (cur(i), 0)),
         bs((BLOCK, KV_W), lambda i: (cur(i), 0)), bs((BLOCK, KV_W), lambda i: (prev(i), 0)),
         bs((BLOCK, KV_W), lambda i: (cur(i), 5)), bs((BLOCK, KV_W), lambda i: (prev(i), 5)),
         bs((BLOCK, ATTN_W), lambda i: (cur(i), 0)), bs((BLOCK, ATTN_W), lambda i: (cur(i), 0)),
         bs((BLOCK, 128), lambda i: (cur(i), 0)), bs(memory_space=pltpu.SMEM)],
        [bs((BLOCK, ATTN_W), lambda i: (cur(i), 0)),
         bs((BLOCK, KV_W), lambda i: (done(i), 0)), bs((BLOCK, KV_W), lambda i: (done(i), 0)),
         bs((1, 128), lambda i: (0, 0))],
        [jax.ShapeDtypeStruct((L, ATTN_W), F32), jax.ShapeDtypeStruct((L, KV_W), F32),
         jax.ShapeDtypeStruct((L, KV_W), F32), jax.ShapeDtypeStruct((1, 128), F32)],
        (q, k, k, proj, proj, d_o, o, lse, sinks),
        [pltpu.VMEM((BLOCK, KV_W), F32), pltpu.VMEM((BLOCK, KV_W), F32)], ride)


def _qk_prep_bwd(proj, tab, qw, kw, d_q, d_k, d_v, d_za, d_u, d_zs):
    L = proj.shape[0]
    tm = _tile(L, 512)
    z0 = ATTN_W + 2 * KV_W

    def body(q_ref, k_ref, t_ref, qw_ref, kw_ref, dq_ref, dk_ref, dv_ref, dza_ref, du_ref, dzs_ref,
             out_ref, gq_ref, gk_ref):
        i = pl.program_id(0)

        @pl.when(i == 0)
        def _():
            gq_ref[...] = jnp.zeros_like(gq_ref)
            gk_ref[...] = jnp.zeros_like(gk_ref)

        cos, sin = t_ref[:, :128], t_ref[:, 128:]
        gq = jnp.zeros((1, 128), F32)
        gk = jnp.zeros((1, 128), F32)
        for c in range(ATTN_W // 128):
            cs = slice(c * 128, (c + 1) * 128)
            d_raw, gw = _norm_rope_bwd(dq_ref[:, cs] * _SCALE, q_ref[:, cs], qw_ref[...], cos, sin)
            out_ref[:, cs] = d_raw.astype(BF16)
            gq = gq + jnp.sum(gw, axis=0, keepdims=True)
        for c in range(KV_W // 128):
            cs = slice(c * 128, (c + 1) * 128)
            d_raw, gw = _norm_rope_bwd(dk_ref[:, cs], k_ref[:, cs], kw_ref[...], cos, sin)
            out_ref[:, ATTN_W + c * 128:ATTN_W + (c + 1) * 128] = d_raw.astype(BF16)
            gk = gk + jnp.sum(gw, axis=0, keepdims=True)
        out_ref[:, ATTN_W + KV_W:z0] = dv_ref[...].astype(BF16)
        out_ref[:, z0:z0 + ATTN_W] = dza_ref[...]
        out_ref[:, z0 + ATTN_W:z0 + ATTN_W + SSM_W] = du_ref[...].astype(BF16)
        out_ref[:, z0 + ATTN_W + SSM_W:] = dzs_ref[...]
        gq_ref[...] += gq
        gk_ref[...] += gk

    row = pl.BlockSpec((1, 128), lambda i: (0, 0))
    blk = lambda w, c: pl.BlockSpec((tm, w), lambda i: (i, c))
    return pl.pallas_call(
        body,
        name="qk_prep_bwd",
        grid=(L // tm,),
        in_specs=[blk(ATTN_W, 0), blk(KV_W, 4), blk(256, 0), row, row, blk(ATTN_W, 0), blk(KV_W, 0), blk(KV_W, 0),
                  blk(ATTN_W, 0), blk(SSM_W, 0), blk(SSM_W, 0)],
        out_specs=[blk(IN_W, 0), row, row],
        out_shape=[jax.ShapeDtypeStruct((L, IN_W), BF16), jax.ShapeDtypeStruct((1, 128), F32),
                   jax.ShapeDtypeStruct((1, 128), F32)],
        compiler_params=_cp(("arbitrary",)),
    )(proj, proj, tab, jnp.tile(qw, 2).reshape(1, 128), jnp.tile(kw, 2).reshape(1, 128), d_q, d_k, d_v,
      d_za, d_u, d_zs)


def _cmul(a, b):
    return a[0] * b[0] - a[1] * b[1], a[0] * b[1] + a[1] * b[0]


def _cmul_conj(a, b):
    return a[0] * b[0] + a[1] * b[1], a[1] * b[0] - a[0] * b[1]


def _cadd(a, b):
    return a[0] + b[0], a[1] + b[1]


def _dot3(a, b, dn):
    ah, bh = a.astype(BF16), b.astype(BF16)
    al, bl = (a - ah.astype(F32)).astype(BF16), (b - bh.astype(F32)).astype(BF16)
    d = lambda u, v: lax.dot_general(u, v, dn, preferred_element_type=F32)
    return d(ah, bh) + d(ah, bl) + d(al, bh)


def _s5_discretise(a_re, a_im, ls, cosx, sinx, bt):
    delta = jnp.exp(ls)
    er = jnp.exp(a_re * delta)
    lb = (er * cosx, er * sinx)
    den = a_re * a_re + a_im * a_im
    coef = _cmul_conj((lb[0] - 1.0, lb[1]), (a_re, a_im))
    coef = (coef[0] / den, coef[1] / den)
    return delta, lb, coef, den, _cmul(coef, bt)


def _powers(lb):
    pw = [(jnp.ones_like(lb[0]), jnp.zeros_like(lb[0]))]
    for _ in range(CHUNK):
        pw.append(_cmul(pw[-1], lb))
    return pw


def _block_rows(a, pw, idx):
    blocks = [_cmul(a, pw[i]) for i in idx]
    return (jnp.concatenate([b[0] for b in blocks], axis=-2), jnp.concatenate([b[1] for b in blocks], axis=-2))


def _block_rows_bwd(g, a, pw, idx, g_pw):
    g_a = (jnp.zeros_like(a[0]), jnp.zeros_like(a[0]))
    for j, i in enumerate(idx):
        gj = (g[0][..., j * SSM_H:(j + 1) * SSM_H, :], g[1][..., j * SSM_H:(j + 1) * SSM_H, :])
        g_a = _cadd(g_a, _cmul_conj(gj, pw[i]))
        gp = _cmul_conj(gj, a)
        g_pw[i] = _cadd(g_pw[i], (jnp.sum(gp[0], axis=-2, keepdims=True), jnp.sum(gp[1], axis=-2, keepdims=True)))
    return g_a


_IDX_S = [CHUNK - 1 - s for s in range(CHUNK)]
_IDX_C = list(range(CHUNK + 1))


def _prep_args(p):
    row = lambda t: t.reshape(SSM_G, 1, SSM_P)
    xi = p["a_im"] * jnp.exp(p["log_step"])[:, None]
    return (row(p["a_re"]), row(p["a_im"]), row(jnp.broadcast_to(p["log_step"][:, None], (SSM_G, SSM_P))),
            row(jnp.cos(xi)), row(jnp.sin(xi)), p["b_re"].transpose(0, 2, 1), p["b_im"].transpose(0, 2, 1),
            p["c_re"], p["c_im"])


PREP_GROUPS = 8


def _prep_specs():
    r1 = pl.BlockSpec((PREP_GROUPS, 1, SSM_P), lambda g: (g, 0, 0))
    r16 = pl.BlockSpec((PREP_GROUPS, SSM_H, SSM_P), lambda g: (g, 0, 0))
    return [r1] * 5 + [r16] * 4, r1, r16


def _ssm_prep(p):
    def one_group(q, are, aim, ls, cosx, sinx, btr, bti, cre, cim, mt_ref, s_ref, o_ref, a_ref):
        _, lb, _, _, bb = _s5_discretise(are[q], aim[q], ls[q], cosx[q], sinx[q], (btr[q], bti[q]))
        pw = _powers(lb)
        c = (cre[q], cim[q])
        sc = _block_rows(bb, pw, _IDX_S)
        cl = _block_rows(c, pw, _IDX_C)
        ok = (cl[0][:CW], cl[1][:CW])
        ot = (cl[0][SSM_H:], cl[1][SSM_H:])
        s_ref[q] = jnp.concatenate([sc[0], sc[1]], axis=1).astype(BF16)
        o_ref[q] = jnp.concatenate([ot[0], -ot[1]], axis=1).astype(BF16)
        a_ref[q] = jnp.concatenate([pw[CHUNK][0], pw[CHUNK][1]], axis=1)
        kt = _dot3(jnp.concatenate([bb[0], -bb[1]], axis=1), jnp.concatenate([ok[0], ok[1]], axis=1), _NT)
        lane = lax.broadcasted_iota(jnp.int32, kt.shape, 1)
        for s in range(CHUNK):
            blk = kt if s == 0 else jnp.where(lane >= SSM_H * s, pltpu.roll(kt, SSM_H * s, 1), 0.0)
            mt_ref[q, s * SSM_H:(s + 1) * SSM_H, :] = blk.astype(BF16)

    def body(*refs):
        for q in range(PREP_GROUPS):
            one_group(q, *refs)

    in_specs, r1, _ = _prep_specs()
    g3 = lambda r, c: pl.BlockSpec((PREP_GROUPS, r, c), lambda g: (g, 0, 0))
    return pl.pallas_call(
        body,
        name="ssm_prep",
        grid=(SSM_G // PREP_GROUPS,),
        in_specs=in_specs,
        out_specs=[g3(CW, CW), g3(CW, 2 * SSM_P), g3(CW, 2 * SSM_P), g3(1, 2 * SSM_P)],
        out_shape=[jax.ShapeDtypeStruct((SSM_G, CW, CW), BF16), jax.ShapeDtypeStruct((SSM_G, CW, 2 * SSM_P), BF16),
                   jax.ShapeDtypeStruct((SSM_G, CW, 2 * SSM_P), BF16),
                   jax.ShapeDtypeStruct((SSM_G, 1, 2 * SSM_P), F32)],
        compiler_params=_cp(("parallel",)),
    )(*_prep_args(p))


def _ssm_prep_bwd(p, g_mt, g_scat, g_ocat, g_a16, ride):
    def body(are, aim, ls, cosx, sinx, btr, bti, cre, cim, gmt_ref, gs_ref, go_ref, ga_ref,
             g_are, g_aim, g_ls, g_btr, g_bti, g_cre, g_cim, ga1_scr, gb1_scr):
        lam = (are[...], aim[...])
        bt = (btr[...], bti[...])
        delta, lb, coef, den, bb = _s5_discretise(lam[0], lam[1], ls[...], cosx[...], sinx[...], bt)
        pw = _powers(lb)
        c = (cre[...], cim[...])
        ok = _block_rows(c, pw, _IDX_C[:CHUNK])
        g_pw =[(jnp.zeros_like(lb[0]), jnp.zeros_like(lb[0])) for _ in range(CHUNK + 1)]
        lane = lax.broadcasted_iota(jnp.int32, (SSM_H, CW), 1)
        for q in range(PREP_GROUPS):
            g_kt = gmt_ref[q, :SSM_H, :]
            for s in range(1, CHUNK):
                blk = gmt_ref[q, s * SSM_H:(s + 1) * SSM_H, :]
                g_kt = g_kt + jnp.where(lane < CW - SSM_H * s, pltpu.roll(blk, CW - SSM_H * s, 1), 0.0)
            a1 = jnp.concatenate([bb[0][q], -bb[1][q]], axis=1)
            b1 = jnp.concatenate([ok[0][q], ok[1][q]], axis=1)
            ga1_scr[q] = _dot3(g_kt, b1, _NN)
            gb1_scr[q] = _dot3(g_kt, a1, _TN)
        g_a1, g_b1 = ga1_scr[...], gb1_scr[...]
        g_bb = (g_a1[..., :SSM_P], -g_a1[..., SSM_P:])
        gs = gs_ref[...]
        g_bb = _cadd(g_bb, _block_rows_bwd((gs[..., :SSM_P], gs[..., SSM_P:]), bb, pw, _IDX_S, g_pw))
        go = go_ref[...]
        pad = jnp.zeros_like(go[..., :SSM_H, :SSM_P])
        g_cl = (jnp.concatenate([g_b1[..., :SSM_P], pad], axis=-2) + jnp.concatenate([pad, go[..., :SSM_P]], axis=-2),
                jnp.concatenate([g_b1[..., SSM_P:], pad], axis=-2) - jnp.concatenate([pad, go[..., SSM_P:]], axis=-2))
        g_c = _block_rows_bwd(g_cl, c, pw, _IDX_C, g_pw)
        ga = ga_ref[...]
        g_pw[CHUNK] = _cadd(g_pw[CHUNK], (ga[..., :SSM_P], ga[..., SSM_P:]))
        g_lb = (jnp.zeros_like(lb[0]), jnp.zeros_like(lb[0]))
        for l in range(CHUNK - 1, -1, -1):
            g_lb = _cadd(g_lb, _cmul_conj(g_pw[l + 1], pw[l]))
            g_pw[l] = _cadd(g_pw[l], _cmul_conj(g_pw[l + 1], lb))
        g_bt = _cmul_conj(g_bb, coef)
        gc = _cmul_conj(g_bb, bt)
        g_coef = (jnp.sum(gc[0], axis=-2, keepdims=True), jnp.sum(gc[1], axis=-2, keepdims=True))
        lam_den = (lam[0] / den, lam[1] / den)
        g_lb = _cadd(g_lb, _cmul(g_coef, lam_den))
        t = _cmul(_cmul_conj(g_coef, coef), lam_den)
        g_x = _cmul_conj(g_lb, lb)
        g_are[...] = g_x[0] * delta - t[0]
        g_aim[...] = g_x[1] * delta - t[1]
        g_ls[...] = (g_x[0] * lam[0] + g_x[1] * lam[1]) * delta
        g_btr[...] = g_bt[0]
        g_bti[...] = g_bt[1]
        g_cre[...] = g_c[0]
        g_cim[...] = g_c[1]

    in_specs, r1, r16 = _prep_specs()
    g3 = lambda r, c: pl.BlockSpec((PREP_GROUPS, r, c), lambda g: (g, 0, 0))
    rows = jax.ShapeDtypeStruct((SSM_G, 1, SSM_P), F32)
    mats = jax.ShapeDtypeStruct((SSM_G, SSM_H, SSM_P), F32)
    (g_are, g_aim, g_ls, g_btr, g_bti, g_cre, g_cim), landed = _call(
        body, "ssm_prep_bwd", (SSM_G // PREP_GROUPS,),
        in_specs + [g3(CW, CW), g3(CW, 2 * SSM_P), g3(CW, 2 * SSM_P), g3(1, 2 * SSM_P)],
        [r1] * 3 + [r16] * 4, [rows] * 3 + [mats] * 4, (*_prep_args(p), g_mt, g_scat, g_ocat, g_a16),
        [pltpu.VMEM((PREP_GROUPS, SSM_H, 2 * SSM_P), F32), pltpu.VMEM((PREP_GROUPS, CW, 2 * SSM_P), F32)], ride)
    grads = dict(a_re=g_are.reshape(SSM_G, SSM_P), a_im=g_aim.reshape(SSM_G, SSM_P),
                 log_step=jnp.sum(g_ls.reshape(SSM_G, SSM_P), axis=1),
                 b_re=g_btr.transpose(0, 2, 1), b_im=g_bti.transpose(0, 2, 1), c_re=g_cre, c_im=g_cim)
    return grads, landed


def _cmul_const(xv, ar, ai):
    return xv * ar + pltpu.roll(xv, SSM_P, 1) * ai


def _chunk_scan(inc, a_row, reverse):
    n = inc.shape[0]
    lane = lax.broadcasted_iota(jnp.int32, (1, 2 * SSM_P), 1)
    row = lax.broadcasted_iota(jnp.int32, inc.shape, 0)
    sign = jnp.where(lane < SSM_P, -1.0, 1.0)
    ar = jnp.where(lane < SSM_P, a_row, pltpu.roll(a_row, SSM_P, 1))
    ai = jnp.where(lane < SSM_P, pltpu.roll(a_row, SSM_P, 1), a_row)
    if reverse:
        ai = -ai
    xv = inc
    s = 1
    while s < n:
        if reverse:
            sh = jnp.where(row < n - s, pltpu.roll(xv, n - s, 0), 0.0)
        else:
            sh = jnp.where(row >= s, pltpu.roll(xv, s, 0), 0.0)
        xv = xv + _cmul_const(sh, ar, ai * sign)
        ar, ai = ar * ar - ai * ai, 2.0 * ar * ai
        s *= 2
    return xv


def _shift_rows(xv, reverse):
    n = xv.shape[0]
    row = lax.broadcasted_iota(jnp.int32, xv.shape, 0)
    if reverse:
        return jnp.where(row < n - 1, pltpu.roll(xv, n - 1, 0), 0.0)
    return jnp.where(row >= 1, pltpu.roll(xv, 1, 0), 0.0)


GB = 128 // SSM_H
U_COL0 = (ATTN_W + 2 * KV_W + ATTN_W) // 128


HALF = CHUNK // 2


def _chunk_perm():
    r = jnp.arange(HALF * 128)
    t, g8, h = r // 128, (r % 128) // SSM_H, r % SSM_H
    return ((g8 * 128 + t * SSM_H + h)[:, None] == jnp.arange(GB * 128)[None, :]).astype(BF16)


def _load_perm(p_hbm, p_scr, sem):
    @pl.when(pl.program_id(0) == 0)
    def _():
        cp = pltpu.make_async_copy(p_hbm, p_scr, sem)
        cp.start()
        cp.wait()


def _rows_to_chunks(pieces, perm):
    halves = [jnp.dot(jnp.concatenate(pieces[k * HALF:(k + 1) * HALF], axis=1).astype(BF16), perm,
                      preferred_element_type=F32).astype(BF16) for k in range(2)]
    return [jnp.concatenate([hv[:, g * 128:(g + 1) * 128] for hv in halves], axis=1) for g in range(GB)]


def _chunks_to_rows(groups, perm, two_pass):
    pieces = []
    for k in range(2):
        v = jnp.concatenate([gv[:, k * 128:(k + 1) * 128] for gv in groups], axis=1)
        hi = v.astype(BF16)
        out = lax.dot_general(hi, perm, _NT, preferred_element_type=F32)
        if two_pass:
            lo = (v - hi.astype(F32)).astype(BF16)
            out = out + lax.dot_general(lo, perm, _NT, preferred_element_type=F32)
        pieces += [out[:, t * 128:(t + 1) * 128] for t in range(HALF)]
    return pieces


def _ssm_fwd(proj, perm, mt, scat, ocat, a16, d_skip, ride):
    L = proj.shape[0]
    nc = L // CHUNK

    def body(u_ref, p_hbm, mt_ref, s_ref, o_ref, a_ref, d_ref, y_ref, yg_ref, h_ref, p_scr, sem):
        _load_perm(p_hbm, p_scr, sem)
        perm = p_scr[...]
        rows = [pl.ds(t, nc, stride=CHUNK) for t in range(CHUNK)]
        us = [u_ref[r, :] for r in rows]
        ua = _rows_to_chunks(us, perm)
        incs = [jnp.dot(ua[g], s_ref[g], preferred_element_type=F32) for g in range(GB)]
        intra = [jnp.dot(ua[g], mt_ref[g], preferred_element_type=F32) for g in range(GB)]
        hxs = [_shift_rows(_chunk_scan(incs[g], a_ref[g], False), False) for g in range(GB)]
        ys = []
        for g in range(GB):
            h_ref[g] = hxs[g]
            ys.append(intra[g] + lax.dot_general(hxs[g].astype(BF16), o_ref[g], _NT, preferred_element_type=F32))
        yp = _chunks_to_rows(ys, perm, True)
        for t, r in enumerate(rows):
            y = yp[t] + d_ref[...] * us[t]
            y_ref[r, :] = y
            yg_ref[r, :] = _gelu(y)

    g3 = lambda r, c: pl.BlockSpec((GB, r, c), lambda g: (g, 0, 0))
    col = pl.BlockSpec((L, 128), lambda g: (0, g))
    return _call(
        body, "ssm_fwd", (SSM_G // GB,),
        [pl.BlockSpec((L, 128), lambda g: (0, U_COL0 + g)), _ANY,
         g3(CW, CW), g3(CW, 2 * SSM_P), g3(CW, 2 * SSM_P), g3(1, 2 * SSM_P),
         pl.BlockSpec((1, 128), lambda g: (0, g))],
        [col, col, g3(nc, 2 * SSM_P)],
        [jax.ShapeDtypeStruct((L, SSM_W), F32), jax.ShapeDtypeStruct((L, SSM_W), F32),
         jax.ShapeDtypeStruct((SSM_G, nc, 2 * SSM_P), F32)],
        (proj, perm, mt, scat, ocat, a16, d_skip.reshape(1, SSM_W)),
        [pltpu.VMEM((HALF * 128, GB * 128), BF16), pltpu.SemaphoreType.DMA], ride)


def _ssm_bwd(d_yg, y, proj, hx, perm, mt, scat, ocat, a16, d_skip, ride):
    L = proj.shape[0]
    nc = L // CHUNK

    def body(dg_ref, y_ref, u_ref, h_ref, p_hbm, mt_ref, s_ref, o_ref, a_ref, d_ref,
             du_ref, gmt_ref, gs_ref, go_ref, ga_ref, gd_ref, p_scr, sem):
        _load_perm(p_hbm, p_scr, sem)
        perm = p_scr[...]
        rows = [pl.ds(t, nc, stride=CHUNK) for t in range(CHUNK)]
        us = [u_ref[r, :] for r in rows]
        dys = [dg_ref[r, :] * _dgelu(y_ref[r, :]) for r in rows]
        gd = jnp.zeros((1, 128), F32)
        for uv, dy in zip(us, dys):
            gd = gd + jnp.sum(dy * uv, axis=0, keepdims=True)
        gd_ref[...] = gd
        ua = _rows_to_chunks(us, perm)
        dya = _rows_to_chunks(dys, perm)
        lane = lax.broadcasted_iota(jnp.int32, (1, 2 * SSM_P), 1)
        dhs = [jnp.dot(dya[g], o_ref[g], preferred_element_type=F32) for g in range(GB)]
        intra = [lax.dot_general(dya[g], mt_ref[g], _NT, preferred_element_type=F32) for g in range(GB)]
        for g in range(GB):
            gmt_ref[g] = lax.dot_general(ua[g], dya[g], _TN, preferred_element_type=F32)
            go_ref[g] = lax.dot_general(dya[g], h_ref[g].astype(BF16), _TN, preferred_element_type=F32)
        dincs = [_shift_rows(_chunk_scan(dhs[g], a_ref[g], True), True) for g in range(GB)]
        dus = []
        for g in range(GB):
            dinc, hx_v = dincs[g], h_ref[g]
            dinc_b = dinc.astype(BF16)
            dus.append(intra[g] + lax.dot_general(dinc_b, s_ref[g], _NT, preferred_element_type=F32))
            gs_ref[g] = lax.dot_general(ua[g], dinc_b, _TN, preferred_element_type=F32)
            p1 = dinc * hx_v
            p2 = pltpu.roll(dinc, SSM_P, 1) * hx_v
            t1 = jnp.sum(p1 + pltpu.roll(p1, SSM_P, 1), axis=0, keepdims=True)
            t2 = jnp.sum(p2 - pltpu.roll(p2, SSM_P, 1), axis=0, keepdims=True)
            ga_ref[g] = jnp.where(lane < SSM_P, t1, pltpu.roll(t2, SSM_P, 1))
        dup = _chunks_to_rows(dus, perm, False)
        for t, r in enumerate(rows):
            du_ref[r, :] = dup[t] + d_ref[...] * dys[t]

    g3 = lambda r, c: pl.BlockSpec((GB, r, c), lambda g: (g, 0, 0))
    col = pl.BlockSpec((L, 128), lambda g: (0, g))
    row = pl.BlockSpec((1, 128), lambda g: (0, g))
    return _call(
        body, "ssm_bwd", (SSM_G // GB,),
        [col, col, pl.BlockSpec((L, 128), lambda g: (0, U_COL0 + g)), g3(nc, 2 * SSM_P), _ANY,
         g3(CW, CW), g3(CW, 2 * SSM_P), g3(CW, 2 * SSM_P), g3(1, 2 * SSM_P), row],
        [col, g3(CW, CW), g3(CW, 2 * SSM_P), g3(CW, 2 * SSM_P), g3(1, 2 * SSM_P), row],
        [jax.ShapeDtypeStruct((L, SSM_W), F32), jax.ShapeDtypeStruct((SSM_G, CW, CW), F32),
         jax.ShapeDtypeStruct((SSM_G, CW, 2 * SSM_P), F32), jax.ShapeDtypeStruct((SSM_G, CW, 2 * SSM_P), F32),
         jax.ShapeDtypeStruct((SSM_G, 1, 2 * SSM_P), F32), jax.ShapeDtypeStruct((1, SSM_W), F32)],
        (d_yg, y, proj, hx, perm, mt, scat, ocat, a16, d_skip.reshape(1, SSM_W)),
        [pltpu.VMEM((HALF * 128, GB * 128), BF16), pltpu.SemaphoreType.DMA], ride)


def _merge(og, yg, w_glu, proj, b_glu, wa, ws):
    L = og.shape[0]
    tm = _tile(L, 256)

    def body(og_ref, yg_ref, wg_ref, z0_ref, z1_ref, b_ref, wa_ref, ws_ref, m_ref, gp_ref):
        zs = jnp.concatenate([z0_ref[...], z1_ref[...]], axis=1)
        ygv = yg_ref[...]
        gpre = jnp.dot(ygv.astype(BF16), wg_ref[...], preferred_element_type=F32)
        gp_ref[...] = gpre
        os_ = ygv * _sigmoid(gpre + b_ref[...]) * _silu(zs)
        ogv = og_ref[...].astype(F32)
        ra = lax.rsqrt(jnp.mean(ogv * ogv, axis=-1, keepdims=True) + NORM_EPS)
        rs = lax.rsqrt(jnp.mean(os_ * os_, axis=-1, keepdims=True) + NORM_EPS)
        m_ref[:, :ATTN_W] = (ogv * ra * wa_ref[...]).astype(BF16)
        m_ref[:, ATTN_W:] = (os_ * rs * ws_ref[...]).astype(BF16)

    row = lambda w: pl.BlockSpec((1, w), lambda i: (0, 0))
    return pl.pallas_call(
        body,
        name="merge",
        grid=(L // tm,),
        in_specs=[pl.BlockSpec((tm, ATTN_W), lambda i: (i, 0)), pl.BlockSpec((tm, SSM_W), lambda i: (i, 0)),
                  pl.BlockSpec((SSM_W, SSM_W), lambda i: (0, 0)),
                  pl.BlockSpec((tm, 512), lambda i: (i, 7)), pl.BlockSpec((tm, 512), lambda i: (i, 8)),
                  row(SSM_W), row(ATTN_W), row(SSM_W)],
        out_specs=[pl.BlockSpec((tm, D_MODEL), lambda i: (i, 0)), pl.BlockSpec((tm, SSM_W), lambda i: (i, 0))],
        out_shape=[jax.ShapeDtypeStruct((L, D_MODEL), BF16), jax.ShapeDtypeStruct((L, SSM_W), F32)],
        compiler_params=_cp(("parallel",)),
    )(og, yg, w_glu, proj, proj, b_glu.reshape(1, SSM_W), wa.reshape(1, ATTN_W), ws.reshape(1, SSM_W))


def _outproj_loss(merged, w_out, x, target):
    L = x.shape[0]
    tm, tn = _tile(L, 256), D_MODEL
    ni, nj = L // tm, D_MODEL // tn

    def body(m_ref, w_ref, x_ref, t_ref, d_ref, db_ref, l_ref):
        out = x_ref[...] + jnp.dot(m_ref[...], w_ref[...], preferred_element_type=F32)
        diff = out - t_ref[...]
        d = diff * (1.0 / D_MODEL)
        d_ref[...] = d
        db_ref[...] = d.astype(BF16)
        l_ref[...] = jnp.full((1, 8, 128), jnp.sum(diff * diff), F32)

    return pl.pallas_call(
        body,
        name="outproj_loss",
        grid=(nj, ni),
        in_specs=[pl.BlockSpec((tm, D_MODEL), lambda j, i: (i, 0)),
                  pl.BlockSpec((D_MODEL, tn), lambda j, i: (0, j)),
                  pl.BlockSpec((tm, tn), lambda j, i: (i, j)),
                  pl.BlockSpec((tm, tn), lambda j, i: (i, j))],
        out_specs=[pl.BlockSpec((tm, tn), lambda j, i: (i, j)), pl.BlockSpec((tm, tn), lambda j, i: (i, j)),
                   pl.BlockSpec((1, 8, 128), lambda j, i: (i * nj + j, 0, 0))],
        out_shape=[jax.ShapeDtypeStruct((L, D_MODEL), F32), jax.ShapeDtypeStruct((L, D_MODEL), BF16),
                   jax.ShapeDtypeStruct((ni * nj, 8, 128), F32)],
        compiler_params=_cp(("parallel", "parallel")),
    )(merged, w_out, x, target)


def _merge_bwd(d_out_b, w_out, w_glu, og, o, yg, gpre, proj, b_glu, wa, ws):
    L = og.shape[0]
    tm = _tile(L, 256)

    def body(dout_ref, wo_ref, wg_ref, og_ref, o_ref, yg_ref, gp_ref, za0_ref, za1_ref, zs0_ref, zs1_ref, b_ref,
             wa_ref, ws_ref, do_ref, dza_ref, dzs_ref, dg_ref, dyg_ref, gwa_ref, gws_ref, gb_ref):
        i = pl.program_id(0)

        @pl.when(i == 0)
        def _():
            gwa_ref[...] = jnp.zeros_like(gwa_ref)
            gws_ref[...] = jnp.zeros_like(gws_ref)
            gb_ref[...] = jnp.zeros_like(gb_ref)

        dm = lax.dot_general(dout_ref[...], wo_ref[...], _NT, preferred_element_type=F32)
        za = jnp.concatenate([za0_ref[...], za1_ref[...]], axis=1)
        zs = jnp.concatenate([zs0_ref[...], zs1_ref[...]], axis=1)
        ogv, dma = og_ref[...].astype(F32), dm[:, :ATTN_W]
        ra = lax.rsqrt(jnp.mean(ogv * ogv, axis=-1, keepdims=True) + NORM_EPS)
        xh = ogv * ra
        gwa_ref[...] += jnp.sum(dma * xh, axis=0, keepdims=True)
        gx = dma * wa_ref[...]
        d_og = ra * (gx - xh * jnp.mean(gx * xh, axis=-1, keepdims=True))
        do_ref[...] = (d_og * _silu(za)).astype(BF16)
        dza_ref[...] = (d_og * o_ref[...].astype(F32) * _dsilu(za)).astype(BF16)
        ygv = yg_ref[...]
        sg = _sigmoid(gp_ref[...] + b_ref[...])
        y2 = ygv * sg
        sz = _silu(zs)
        os_ = y2 * sz
        dms = dm[:, ATTN_W:]
        rs = lax.rsqrt(jnp.mean(os_ * os_, axis=-1, keepdims=True) + NORM_EPS)
        xs = os_ * rs
        gws_ref[...] += jnp.sum(dms * xs, axis=0, keepdims=True)
        gxs = dms * ws_ref[...]
        d_os = rs * (gxs - xs * jnp.mean(gxs * xs, axis=-1, keepdims=True))
        dzs_ref[...] = (d_os * y2 * _dsilu(zs)).astype(BF16)
        d_y2 = d_os * sz
        d_g = d_y2 * ygv * sg * (1.0 - sg)
        d_g_b = d_g.astype(BF16)
        dg_ref[...] = d_g_b
        gb_ref[...] += jnp.sum(d_g, axis=0, keepdims=True)
        dyg_ref[...] = d_y2 * sg + lax.dot_general(d_g_b, wg_ref[...], _NT, preferred_element_type=F32)

    row = lambda w: pl.BlockSpec((1, w), lambda i: (0, 0))
    full = lambda w: pl.BlockSpec((tm, w), lambda i: (i, 0))
    half = lambda c: pl.BlockSpec((tm, 512), lambda i: (i, c))
    return pl.pallas_call(
        body,
        name="merge_bwd",
        grid=(L // tm,),
        in_specs=[full(D_MODEL), pl.BlockSpec((D_MODEL, D_MODEL), lambda i: (0, 0)),
                  pl.BlockSpec((SSM_W, SSM_W), lambda i: (0, 0)),
                  full(ATTN_W), full(ATTN_W), full(SSM_W), full(SSM_W),
                  half(3), half(4), half(7), half(8), row(SSM_W), row(ATTN_W), row(SSM_W)],
        out_specs=[full(ATTN_W), full(ATTN_W), full(SSM_W), full(SSM_W), full(SSM_W),
                   row(ATTN_W), row(SSM_W), row(SSM_W)],
        out_shape=[jax.ShapeDtypeStruct((L, ATTN_W), BF16), jax.ShapeDtypeStruct((L, ATTN_W), BF16),
                   jax.ShapeDtypeStruct((L, SSM_W), BF16), jax.ShapeDtypeStruct((L, SSM_W), BF16),
                   jax.ShapeDtypeStruct((L, SSM_W), F32),
                   jax.ShapeDtypeStruct((1, ATTN_W), F32), jax.ShapeDtypeStruct((1, SSM_W), F32),
                   jax.ShapeDtypeStruct((1, SSM_W), F32)],
        compiler_params=_cp(("arbitrary",)),
    )(d_out_b, w_out, w_glu, og, o, yg, gpre, proj, proj, proj, proj, b_glu.reshape(1, SSM_W), wa.reshape(1, ATTN_W),
      ws.reshape(1, SSM_W))


def _rms_bwd_x(x, norm_w, d_proj, wt_in, d_out, ride):
    L = x.shape[0]
    tm = _tile(L, 256)

    def body(x_ref, w_ref, dp_ref, wt_ref, do_ref, gx_ref, gw_ref):
        i = pl.program_id(0)

        @pl.when(i == 0)
        def _():
            gw_ref[...] = jnp.zeros_like(gw_ref)

        xv = x_ref[...]
        dh = jnp.dot(dp_ref[...], wt_ref[...], preferred_element_type=F32)
        r = lax.rsqrt(jnp.mean(xv * xv, axis=-1, keepdims=True) + NORM_EPS)
        xh = xv * r
        gw_ref[...] += jnp.sum(dh * xh, axis=0, keepdims=True)
        gx = dh * w_ref[...]
        gx_ref[...] = do_ref[...] + r * (gx - xh * jnp.mean(gx * xh, axis=-1, keepdims=True))

    blk = pl.BlockSpec((tm, D_MODEL), lambda i: (i, 0))
    row = pl.BlockSpec((1, D_MODEL), lambda i: (0, 0))
    dp_blk = pl.BlockSpec((tm, IN_W), lambda i: (i, 0))
    wt_blk = pl.BlockSpec((IN_W, D_MODEL), lambda i: (0, 0), pipeline_mode=pl.Buffered(1))
    return _call(body, "d_hn_rms_bwd_x", (L // tm,), [blk, row, dp_blk, wt_blk, blk], [blk, row],
                 [jax.ShapeDtypeStruct((L, D_MODEL), F32), jax.ShapeDtypeStruct((1, D_MODEL), F32)],
                 (x, norm_w.reshape(1, D_MODEL), d_proj, wt_in, d_out), ride=ride)


def _rope_table(positions):
    inv_freq = ROPE_THETA ** (-jnp.arange(0, HEAD_DIM, 2, dtype=F32) / HEAD_DIM)
    ang = positions.astype(F32)[:, None] * inv_freq
    sign = jnp.where(jnp.arange(128) % HEAD_DIM < HEAD_DIM // 2, -1.0, 1.0)
    return jnp.concatenate([jnp.tile(jnp.cos(ang), (1, 4)), jnp.tile(jnp.sin(ang), (1, 4)) * sign], axis=1)


def _step(x, positions, target, w, core, chip):
    small = {n: w[n] for n in _SMALL}
    tab = _rope_table(positions)
    mt_b, scat_b, ocat_b, a16 = _ssm_prep(small)
    perm = _chunk_perm()
    blocks = lambda t: t.reshape(N_DEV, t.shape[0] // N_DEV, t.shape[1])

    proj, hn, wt_in = _rms_inproj_gather(x, small["norm_w"], w["w_in"].T.astype(BF16), chip)
    wt_in = wt_in.reshape(IN_W, D_MODEL)
    q_rot, k_rot = _qk_prep(proj, tab, small["q_norm_w"], small["k_norm_w"])
    (og, o, lse), (w_glu,) = _attn_fwd(q_rot, k_rot, proj, small["sinks"],
                                       _gather_exchange([w["w_glu"].astype(BF16)]))
    (y, yg, hx), (w_out,) = _ssm_fwd(proj, perm, mt_b, scat_b, ocat_b, a16, small["d_skip"],
                                     _gather_exchange([w["w_out"].astype(BF16)]))
    w_glu, w_out = w_glu.reshape(SSM_W, SSM_W), w_out.reshape(D_MODEL, D_MODEL)
    merged, gpre = _merge(og, yg, w_glu, proj, small["b_glu"], small["attn_out_norm_w"], small["ssm_out_norm_w"])
    d_out, d_out_b, loss_parts = _outproj_loss(merged, w_out, x, target)
    loss = 0.5 * jnp.sum(loss_parts[:, 0, 0]) / D_MODEL

    g_w_out = blocks(_mm(merged, d_out_b, "tn", F32, "grad_w_out", tm=1024))
    d_o, d_za, d_zs, d_g, d_yg, g_wa, g_ws, g_bglu = _merge_bwd(
        d_out_b, w_out, w_glu, og, o, yg, gpre, proj, small["b_glu"], small["attn_out_norm_w"],
        small["ssm_out_norm_w"])
    g_w_glu = blocks(_mm(yg, d_g, "tn", F32, "grad_w_glu"))
    (d_u, g_mt, g_scat, g_ocat, g_a16, g_dskip), (ra_out, ra_glu) = _ssm_bwd(
        d_yg, y, proj, hx, perm, mt_b, scat_b, ocat_b, a16, small["d_skip"], _pair_exchange([g_w_out, g_w_glu]))
    p_out = _pair_sum(g_w_out, ra_out, core, BF16, "pair_sum_out")
    p_glu = _pair_sum(g_w_glu, ra_glu, core, BF16, "pair_sum_glu")
    (d_q, d_k, d_v, g_sinks), (rb_out, rb_glu) = _attn_bwd(
        q_rot, k_rot, proj, small["sinks"], d_o, o, lse, _chip_exchange([p_out, p_glu]))
    d_proj, g_qw, g_kw = _qk_prep_bwd(proj, tab, small["q_norm_w"], small["k_norm_w"], d_q, d_k, d_v,
                                      d_za, d_u, d_zs)
    g_qw = g_qw[0, :HEAD_DIM] + g_qw[0, HEAD_DIM:]
    g_kw = g_kw[0, :HEAD_DIM] + g_kw[0, HEAD_DIM:]
    g_in_a = blocks(_mm(d_proj, hn, "tn", F32, "grad_w_in_a", tm=1152, panel=0))
    g_in_b, (ra_a,) = _mm(d_proj, hn, "tn", F32, "grad_w_in_b", tm=1152, panel=1, ride=_pair_exchange([g_in_a]))
    g_in_b = blocks(g_in_b)
    p_a = _pair_sum(g_in_a, ra_a, core, BF16, "pair_sum_in_a")
    (grad_x, g_nw), (rb_a, ra_b) = _rms_bwd_x(x, small["norm_w"], d_proj, wt_in, d_out,
                                              _both(_chip_exchange([p_a]), _pair_exchange([g_in_b])))
    p_b = _pair_sum(g_in_b, ra_b, core, BF16, "pair_sum_in_b")
    g_small, (rb_b,) = _ssm_prep_bwd(small, g_mt, g_scat, g_ocat, g_a16, _chip_exchange([p_b]))

    g_small.update(norm_w=g_nw.reshape(-1), q_norm_w=g_qw.reshape(-1), k_norm_w=g_kw.reshape(-1),
                   sinks=g_sinks[0, :N_HEADS], d_skip=g_dskip.reshape(-1), b_glu=g_bglu.reshape(-1),
                   attn_out_norm_w=g_wa.reshape(-1), ssm_out_norm_w=g_ws.reshape(-1))
    g_packed = _slab_all_reduce(_pack(g_small, loss).reshape(N_DEV, _PACK_ROWS // N_DEV, 128))
    g_packed = g_packed.reshape(_PACK_ROWS, 128)
    grads = _unpack(g_packed, w)
    parts = dict(w_in=([p_a, p_b], [rb_a, rb_b]), w_glu=([p_glu], [rb_glu]), w_out=([p_out], [rb_out]))
    return g_packed[_LOSS_ROW, 0], grad_x, grads, parts


_ANY = pl.BlockSpec(memory_space=pl.ANY)


class _Exchange:
    def __init__(self, arrays, out_shape, sems, start, finish, relay=None):
        self.arrays, self.out_shape, self.sems, self.start, self.finish = arrays, out_shape, sems, start, finish
        self.relay = relay if relay is not None else (lambda ins, outs, sems: None)


def _gather_exchange(blocks):
    n = len(blocks)

    def parts(ins, outs, sems):
        send_sems, recv_sems, local_sems = sems
        x, y, c = lax.axis_index("x"), lax.axis_index("y"), lax.axis_index("c")
        me, sibling = (x, y, c), (x, y, 1 - c)
        chips = [(1 - x, y), (x, 1 - y), (1 - x, 1 - y)]

        def slot(k, dev):
            return outs[k].at[4 * dev[0] + 2 * dev[1] + dev[2]]

        def copy(k, q, block, to, src=None):
            return pltpu.make_async_remote_copy(
                src_ref=slot(k, block) if src is None else src, dst_ref=slot(k, block),
                send_sem=send_sems.at[k, q], recv_sem=recv_sems.at[k, q], device_id=to, device_id_type=MESH)

        mine = [pltpu.make_async_copy(ins[k], slot(k, me), local_sems.at[k]) for k in range(n)]
        first = []
        for k in range(n):
            first.append(copy(k, 0, me, sibling, src=ins[k]))
            first += [copy(k, 1 + j, me, (*chip, c), src=ins[k]) for j, chip in enumerate(chips)]
        return me, sibling, chips, c, copy, mine, first

    def start(ins, outs, sems):
        *_, mine, first = parts(ins, outs, sems)
        for cp in mine + first:
            cp.start()

    def relay(ins, outs, sems):
        me, sibling, chips, c, copy, _, _ = parts(ins, outs, sems)
        for j, chip in enumerate(chips):
            for k in range(n):
                copy(k, 1 + j, (*chip, c), me).wait_recv()
                copy(k, 4 + j, (*chip, c), sibling).start()

    def finish(ins, outs, sems):
        me, sibling, chips, c, copy, mine, first = parts(ins, outs, sems)
        for k in range(n):
            copy(k, 0, sibling, me).wait_recv()
            for j, chip in enumerate(chips):
                copy(k, 4 + j, (*chip, 1 - c), me).wait_recv()
        for cp in first + [copy(k, 4 + j, (*chip, c), sibling) for k in range(n) for j, chip in enumerate(chips)]:
            cp.wait_send()
        for cp in mine:
            cp.wait()

    return _Exchange(blocks, [jax.ShapeDtypeStruct((N_DEV,) + b.shape, b.dtype) for b in blocks],
                     [pltpu.SemaphoreType.DMA((n, 7)), pltpu.SemaphoreType.DMA((n, 7)), pltpu.SemaphoreType.DMA((n,))],
                     start, finish, relay)


def _direct_exchange(arrays, out_lead, fan, route):
    n = len(arrays)

    def copies(ins, outs, sems):
        send_sems, recv_sems = sems
        legs = route(lax.axis_index("x"), lax.axis_index("y"), lax.axis_index("c"))
        return [pltpu.make_async_remote_copy(
            src_ref=ins[k].at[src], dst_ref=outs[k].at[q], send_sem=send_sems.at[k, q], recv_sem=recv_sems.at[k, q],
            device_id=to, device_id_type=MESH) for k in range(n) for src, q, to in legs]

    def start(ins, outs, sems):
        for cp in copies(ins, outs, sems):
            cp.start()

    def finish(ins, outs, sems):
        for cp in copies(ins, outs, sems):
            cp.wait()

    return _Exchange(arrays, [jax.ShapeDtypeStruct((out_lead,) + a.shape[1:], a.dtype) for a in arrays],
                     [pltpu.SemaphoreType.DMA((n, fan)), pltpu.SemaphoreType.DMA((n, fan))], start, finish)


def _pair_exchange(grads):
    return _direct_exchange(grads, 4, 4, lambda x, y, c: [(2 * chip + (1 - c), chip, (x, y, 1 - c))
                                                          for chip in range(4)])


def _chip_exchange(parts):
    def route(x, y, c):
        chips = [(1 - x, y), (x, 1 - y), (1 - x, 1 - y)]
        return [(2 * chip[0] + chip[1], q, (*chip, c)) for q, chip in enumerate(chips)]
    return _direct_exchange(parts, 3, 3, route)


def _both(ex1, ex2):
    n1, s1 = len(ex1.arrays), len(ex1.sems)

    def halves(ins, outs, sems):
        return (ins[:n1], outs[:n1], sems[:s1]), (ins[n1:], outs[n1:], sems[s1:])

    def start(ins, outs, sems):
        h1, h2 = halves(ins, outs, sems)
        ex1.start(*h1)
        ex2.start(*h2)

    def relay(ins, outs, sems):
        h1, h2 = halves(ins, outs, sems)
        ex1.relay(*h1)
        ex2.relay(*h2)

    def finish(ins, outs, sems):
        h1, h2 = halves(ins, outs, sems)
        ex1.finish(*h1)
        ex2.finish(*h2)

    return _Exchange(list(ex1.arrays) + list(ex2.arrays), list(ex1.out_shape) + list(ex2.out_shape),
                     list(ex1.sems) + list(ex2.sems), start, finish, relay)


def _call(body, name, grid, in_specs, out_specs, out_shape, args, scratch_shapes=(), ride=None):
    if ride is None:
        sem = ("arbitrary",) * len(grid)
        return pl.pallas_call(body, name=name, grid=grid, in_specs=in_specs, out_specs=out_specs, out_shape=out_shape,
                              scratch_shapes=list(scratch_shapes), compiler_params=_cp(sem))(*args), None
    n_in, n_out, n_scr, n_x = len(in_specs), len(out_specs), len(scratch_shapes), len(ride.arrays)

    def wrapped(*refs):
        ins, refs = refs[:n_in], refs[n_in:]
        x_in, refs = refs[:n_x], refs[n_x:]
        outs, refs = refs[:n_out], refs[n_out:]
        x_out, refs = refs[:n_x], refs[n_x:]
        scr, sems = refs[:n_scr], refs[n_scr:]
        step, total = pl.program_id(0), grid[0]
        for a in range(1, len(grid)):
            step, total = step * grid[a] + pl.program_id(a), total * grid[a]
        @pl.when(step == 0)
        def _():
            ride.start(x_in, x_out, sems)

        @pl.when(step == max(total - 2, 0))
        def _():
            ride.relay(x_in, x_out, sems)

        body(*ins, *outs, *scr)

        @pl.when(step == total - 1)
        def _():
            ride.finish(x_in, x_out, sems)

    res = pl.pallas_call(
        wrapped, name=name, grid=grid, in_specs=list(in_specs) + [_ANY] * n_x,
        out_specs=list(out_specs) + [_ANY] * n_x, out_shape=list(out_shape) + list(ride.out_shape),
        scratch_shapes=list(scratch_shapes) + list(ride.sems),
        compiler_params=_cp(("arbitrary",) * len(grid)))(*args, *ride.arrays)
    return res[:n_out], list(res[n_out:])


def _pair_sum(g, ra, core, out_dtype, name):
    _, r, C = g.shape
    tr = _tile(r, 576)

    def body(c_ref, g_ref, ra_ref, p_ref):
        p_ref[...] = (g_ref[...] + ra_ref[...]).astype(p_ref.dtype)

    return pl.pallas_call(
        body,
        name=name,
        grid_spec=pltpu.PrefetchScalarGridSpec(
            num_scalar_prefetch=1,
            grid=(4, r // tr),
            in_specs=[pl.BlockSpec((1, tr, C), lambda j, t, c_ref: (2 * j + c_ref[0], t, 0)),
                      pl.BlockSpec((1, tr, C), lambda j, t, c_ref: (j, t, 0))],
            out_specs=pl.BlockSpec((1, tr, C), lambda j, t, c_ref: (j, t, 0)),
        ),
        out_shape=jax.ShapeDtypeStruct((4, r, C), out_dtype),
        compiler_params=_cp(("parallel", "parallel")),
    )(core, g, ra)


def _slab_all_reduce(slab):
    _, r, lanes = slab.shape

    def body(s_ref, o_ref, ra, rb, ps, sems_a, sems_b, sems_c):
        x, y, c = lax.axis_index("x"), lax.axis_index("y"), lax.axis_index("c")
        chips = [(1 - x, y), (x, 1 - y), (1 - x, 1 - y)]
        pair = [pltpu.make_async_remote_copy(
            src_ref=s_ref.at[2 * k + (1 - c)], dst_ref=ra.at[k], send_sem=sems_a.at[0, k], recv_sem=sems_a.at[1, k],
            device_id=(x, y, 1 - c), device_id_type=MESH) for k in range(4)]
        for cp in pair:
            cp.start()
        for cp in pair:
            cp.wait()
        for k in range(4):
            ps[k] = s_ref[2 * k + c] + ra[k]
        cross = [pltpu.make_async_remote_copy(
            src_ref=ps.at[2 * ch[0] + ch[1]], dst_ref=rb.at[q], send_sem=sems_b.at[0, q], recv_sem=sems_b.at[1, q],
            device_id=(*ch, c), device_id_type=MESH) for q, ch in enumerate(chips)]
        for cp in cross:
            cp.start()
        for cp in cross:
            cp.wait()
        me = 4 * x + 2 * y + c
        o_ref[me] = ((ps[2 * x + y] + rb[0]) + rb[1]) + rb[2]
        flips = [(dx, dy, dc) for dx in (0, 1) for dy in (0, 1) for dc in (0, 1) if dx + dy + dc]
        spread = [pltpu.make_async_remote_copy(
            src_ref=o_ref.at[me], dst_ref=o_ref.at[me], send_sem=sems_c.at[0, q], recv_sem=sems_c.at[1, q],
            device_id=(x + dx - 2 * x * dx, y + dy - 2 * y * dy, c + dc - 2 * c * dc), device_id_type=MESH)
            for q, (dx, dy, dc) in enumerate(flips)]
        for cp in spread:
            cp.start()
        for q, (dx, dy, dc) in enumerate(flips):
            peer = 4 * (x + dx - 2 * x * dx) + 2 * (y + dy - 2 * y * dy) + (c + dc - 2 * c * dc)
            pltpu.make_async_remote_copy(
                src_ref=o_ref.at[peer], dst_ref=o_ref.at[peer], send_sem=sems_c.at[0, q], recv_sem=sems_c.at[1, q],
                device_id=(x, y, c), device_id_type=MESH).wait_recv()
        for cp in spread:
            cp.wait_send()

    whole = pl.BlockSpec(memory_space=pltpu.VMEM)
    return pl.pallas_call(
        body, name="slab_all_reduce", in_specs=[whole], out_specs=whole,
        out_shape=jax.ShapeDtypeStruct(slab.shape, F32),
        scratch_shapes=[pltpu.VMEM((4, r, lanes), F32), pltpu.VMEM((3, r, lanes), F32), pltpu.VMEM((4, r, lanes), F32),
                        pltpu.SemaphoreType.DMA((2, 4)), pltpu.SemaphoreType.DMA((2, 3)),
                        pltpu.SemaphoreType.DMA((2, 7))],
        compiler_params=_cp(),
    )(slab)


def _adamw_reduced(ps, rbs, chip, w, m, v, name):
    nh = len(ps)
    R, C = w.shape
    ch = C // nh
    tr = _tile(R, 288)
    nt = R // tr
    c1 = 1.0 - ADAM_B1 ** ADAM_STEP
    c2 = 1.0 - ADAM_B2 ** ADAM_STEP

    def body(c_ref, *refs):
        p_refs, rb_refs = refs[:nh], refs[nh:2 * nh]
        w_ref, m_ref, v_ref, g_ref, d_ref, nm_ref, nv_ref = refs[2 * nh:]
        for h in range(nh):
            @pl.when(pl.program_id(0) == h)
            def _(h=h):
                rb = rb_refs[h]
                gv = p_refs[h][0].astype(F32) + rb[0].astype(F32)
                gv = gv + rb[1].astype(F32)
                gv = gv + rb[2].astype(F32)
                nm = ADAM_B1 * m_ref[...] + (1.0 - ADAM_B1) * gv
                nv = ADAM_B2 * v_ref[...] + (1.0 - ADAM_B2) * (gv * gv)
                g_ref[...] = gv
                nm_ref[...] = nm
                nv_ref[...] = nv
                d_ref[...] = -ADAM_LR * ((nm / c1) / (jnp.sqrt(nv / c2) + ADAM_EPS) + ADAM_WD * w_ref[...])

    def held(h):
        return lambda hh, tt: jnp.where(hh == h, tt, jnp.where(hh < h, 0, nt - 1))

    p_specs = [pl.BlockSpec((1, tr, ch), lambda hh, tt, c_ref, f=held(h): (c_ref[0], f(hh, tt), 0))
               for h in range(nh)]
    rb_specs = [pl.BlockSpec((3, tr, ch), lambda hh, tt, c_ref, f=held(h): (0, f(hh, tt), 0)) for h in range(nh)]
    blk = pl.BlockSpec((tr, ch), lambda hh, tt, c_ref: (tt, hh))
    return pl.pallas_call(
        body,
        name=name,
        grid_spec=pltpu.PrefetchScalarGridSpec(
            num_scalar_prefetch=1, grid=(nh, nt), in_specs=p_specs + rb_specs + [blk] * 3, out_specs=[blk] * 4),
        out_shape=[jax.ShapeDtypeStruct((R, C), F32)] * 4,
        compiler_params=_cp(("arbitrary", "arbitrary")),
    )(chip, *ps, *rbs, w, m, v)


_SMALL = ("norm_w", "q_norm_w", "k_norm_w", "sinks", "a_re", "a_im", "log_step", "b_re", "b_im", "c_re", "c_im",
          "d_skip", "b_glu", "attn_out_norm_w", "ssm_out_norm_w")
_WEIGHTS = ("norm_w", "w_in", "q_norm_w", "k_norm_w", "sinks", "a_re", "a_im", "log_step", "b_re", "b_im", "c_re",
            "c_im", "d_skip", "w_glu", "b_glu", "attn_out_norm_w", "ssm_out_norm_w", "w_out")
_SMALL_2D = dict(norm_w=(1, 2048), q_norm_w=(1, 64), k_norm_w=(1, 64), sinks=(1, 16), a_re=(64, 64), a_im=(64, 64),
                 log_step=(1, 64), b_re=(1024, 64), b_im=(1024, 64), c_re=(1024, 64), c_im=(1024, 64),
                 d_skip=(1, 1024), b_glu=(1, 1024), attn_out_norm_w=(1, 1024), ssm_out_norm_w=(1, 1024))
_P_MINOR = ("b_re", "b_im")


def _flat_form(n, t):
    return t.transpose(0, 2, 1) if n in _P_MINOR else t


def _own_form(n, t, shape):
    if n in _P_MINOR:
        return t.reshape(shape[0], shape[2], shape[1]).transpose(0, 2, 1)
    return t.reshape(shape)


def _slab_rows(n):
    return -(-n // 1024) * 8


_PACK_ROWS = 2304


_LOSS_ROW = 2192


def _pack(d, loss):
    parts = []
    for n in _SMALL:
        flat = _flat_form(n, d[n]).reshape(-1).astype(F32)
        rows = _slab_rows(flat.shape[0])
        parts.append(jnp.pad(flat, (0, rows * 128 - flat.shape[0])).reshape(rows, 128))
    assert sum(p.shape[0] for p in parts) == _LOSS_ROW
    parts.append(jnp.pad(loss.reshape(1, 1), ((0, _PACK_ROWS - _LOSS_ROW - 1), (0, 127))))
    return jnp.concatenate(parts, axis=0)


def _unpack(packed, like):
    out, off = {}, 0
    for n in _SMALL:
        size = math.prod(like[n].shape)
        rows = _slab_rows(size)
        out[n] = _own_form(n, packed[off:off + rows].reshape(-1)[:size], like[n].shape)
        off += rows
    return out


def _adamw_small(g, w, m, v):
    c1 = 1.0 - ADAM_B1 ** ADAM_STEP
    c2 = 1.0 - ADAM_B2 ** ADAM_STEP
    k = len(_SMALL)

    def body(*refs):
        ins, outs = refs[:4 * k], refs[4 * k:]
        for j in range(k):
            gv, wv, mv, vv = (ins[q * k + j][...] for q in range(4))
            nm = ADAM_B1 * mv + (1.0 - ADAM_B1) * gv
            nv = ADAM_B2 * vv + (1.0 - ADAM_B2) * (gv * gv)
            outs[j][...] = -ADAM_LR * ((nm / c1) / (jnp.sqrt(nv / c2) + ADAM_EPS) + ADAM_WD * wv)
            outs[k + j][...] = nm
            outs[2 * k + j][...] = nv

    args = [_flat_form(n, d[n]).reshape(_SMALL_2D[n]) for d in (g, w, m, v) for n in _SMALL]
    shapes = [jax.ShapeDtypeStruct(_SMALL_2D[n], F32) for _ in range(3) for n in _SMALL]
    outs = pl.pallas_call(body, name="adamw_small", out_shape=shapes, compiler_params=_cp())(*args)
    res = []
    for q in range(3):
        res.append({n: _own_form(n, outs[q * k + j], w[n].shape) for j, n in enumerate(_SMALL)})
    return res


def kernel(x, positions, norm_w, w_in, q_norm_w, k_norm_w, sinks, a_re, a_im, log_step, b_re, b_im, c_re, c_im, d_skip, w_glu, b_glu, attn_out_norm_w, ssm_out_norm_w, w_out, loss_target, m_norm_w, m_w_in, m_q_norm_w, m_k_norm_w, m_sinks, m_a_re, m_a_im, m_log_step, m_b_re, m_b_im, m_c_re, m_c_im, m_d_skip, m_w_glu, m_b_glu, m_attn_out_norm_w, m_ssm_out_norm_w, m_w_out, v_norm_w, v_w_in, v_q_norm_w, v_k_norm_w, v_sinks, v_a_re, v_a_im, v_log_step, v_b_re, v_b_im, v_c_re, v_c_im, v_d_skip, v_w_glu, v_b_glu, v_attn_out_norm_w, v_ssm_out_norm_w, v_w_out):
    w = dict(norm_w=norm_w, w_in=w_in, q_norm_w=q_norm_w, k_norm_w=k_norm_w, sinks=sinks, a_re=a_re, a_im=a_im,
             log_step=log_step, b_re=b_re, b_im=b_im, c_re=c_re, c_im=c_im, d_skip=d_skip, w_glu=w_glu, b_glu=b_glu,
             attn_out_norm_w=attn_out_norm_w, ssm_out_norm_w=ssm_out_norm_w, w_out=w_out)
    m = dict(norm_w=m_norm_w, w_in=m_w_in, q_norm_w=m_q_norm_w, k_norm_w=m_k_norm_w, sinks=m_sinks, a_re=m_a_re,
             a_im=m_a_im, log_step=m_log_step, b_re=m_b_re, b_im=m_b_im, c_re=m_c_re, c_im=m_c_im, d_skip=m_d_skip,
             w_glu=m_w_glu, b_glu=m_b_glu, attn_out_norm_w=m_attn_out_norm_w, ssm_out_norm_w=m_ssm_out_norm_w,
             w_out=m_w_out)
    v = dict(norm_w=v_norm_w, w_in=v_w_in, q_norm_w=v_q_norm_w, k_norm_w=v_k_norm_w, sinks=v_sinks, a_re=v_a_re,
             a_im=v_a_im, log_step=v_log_step, b_re=v_b_re, b_im=v_b_im, c_re=v_c_re, c_im=v_c_im, d_skip=v_d_skip,
             w_glu=v_w_glu, b_glu=v_b_glu, attn_out_norm_w=v_attn_out_norm_w, ssm_out_norm_w=v_ssm_out_norm_w,
             w_out=v_w_out)
    core = lax.axis_index("c").astype(jnp.int32).reshape(1)
    chip = (2 * lax.axis_index("x") + lax.axis_index("y")).astype(jnp.int32).reshape(1)

    loss, grad_x, grads, parts = _step(x[0], positions[0], loss_target[0], w, core, chip)
    delta, new_m, new_v = {}, {}, {}
    for n in ("w_glu", "w_out"):
        grads[n], delta[n], new_m[n], new_v[n] = _adamw_reduced(*parts[n], chip, w[n], m[n], v[n], f"adamw_{n}")
    g_t, d_t, m_t, v_t = _adamw_reduced(*parts["w_in"], chip, w["w_in"].T, m["w_in"].T, v["w_in"].T, "adamw_w_in")
    grads["w_in"], delta["w_in"], new_m["w_in"], new_v["w_in"] = g_t.T, d_t.T, m_t.T, v_t.T
    d_s, m_s, v_s = _adamw_small(grads, w, m, v)
    delta.update(d_s)
    new_m.update(m_s)
    new_v.update(v_s)

    return (loss, grad_x[None], *[grads[n] for n in _WEIGHTS], *[delta[n] for n in _WEIGHTS],
            *[new_m[n] for n in _WEIGHTS], *[new_v[n] for n in _WEIGHTS])
```

```python
import math

import jax
import jax.numpy as jnp
from jax import lax
from jax.experimental import pallas as pl
from jax.experimental.pallas import tpu as pltpu

F32 = jnp.float32
BF16 = jnp.bfloat16

D_MODEL = 2048
ATTN_W = 1024
KV_W = 256
SSM_W = 1024
HEAD_DIM = 64
N_HEADS = 16
N_KV = 4
IN_W = 4608
BLOCK = 128
ROPE_THETA = 10000.0
NORM_EPS = 1e-6
SSM_G = 64
SSM_P = 64
SSM_H = 16
CHUNK = 16
CW = CHUNK * SSM_H
N_DEV = 8

ADAM_LR = 0.001
ADAM_B1 = 0.9
ADAM_B2 = 0.999
ADAM_EPS = 1e-08
ADAM_WD = 0.01
ADAM_STEP = 10

VMEM_LIMIT = 56 * 1024 * 1024
MESH = pl.DeviceIdType.MESH


def _cp(sem=None):
    if sem is None:
        return pltpu.CompilerParams(vmem_limit_bytes=VMEM_LIMIT)
    return pltpu.CompilerParams(vmem_limit_bytes=VMEM_LIMIT, dimension_semantics=sem)


def _sigmoid(x):
    return 0.5 * jnp.tanh(0.5 * x) + 0.5


def _silu(x):
    return x * _sigmoid(x)


def _dsilu(x):
    s = _sigmoid(x)
    return s * (1.0 + x * (1.0 - s))


_GELU_C = math.sqrt(2.0 / math.pi)


def _gelu(y):
    t = jnp.tanh(_GELU_C * (y + 0.044715 * y * y * y))
    return 0.5 * y * (1.0 + t)


def _dgelu(y):
    t = jnp.tanh(_GELU_C * (y + 0.044715 * y * y * y))
    return 0.5 * (1.0 + t) + 0.5 * y * (1.0 - t * t) * _GELU_C * (1.0 + 3.0 * 0.044715 * y * y)


def _tile(n, want):
    if n <= want:
        return n
    for t in range(want - want % 16, 0, -16):
        if n % t == 0:
            return t
    raise ValueError((n, want))


def _mm_tn(a, b, name, tm=512, tn=1024, ride=None, panel=None):
    (K, M), (K2, N) = a.shape, b.shape
    assert K == K2
    tm, tn = _tile(M, tm), _tile(N, tn)
    p0 = 0
    if panel is not None:
        p0, N = panel, tn

    def body(a_ref, b_ref, o_ref):
        o_ref[...] = lax.dot_general(a_ref[...].astype(BF16), b_ref[...].astype(BF16), _TN,
                                     preferred_element_type=F32)

    a_spec = pl.BlockSpec((K, tm), lambda j, i: (0, i))
    b_spec = pl.BlockSpec((K, tn), lambda j, i: (0, j + p0))
    o_spec = pl.BlockSpec((tm, tn), lambda j, i: (i, j))
    if ride is not None:
        (out,), landed = _call(body, name, (N // tn, M // tm), [a_spec, b_spec], [o_spec],
                               [jax.ShapeDtypeStruct((M, N), F32)], (a, b), ride=ride)
        return out, landed
    return pl.pallas_call(
        body,
        name=name,
        grid=(N // tn, M // tm),
        in_specs=[a_spec, b_spec],
        out_specs=o_spec,
        out_shape=jax.ShapeDtypeStruct((M, N), F32),
        compiler_params=_cp(("parallel", "parallel")),
    )(a, b)


_CHIP_ORDER = (0, 2, 1, 3)


def _rms_inproj_gather(x, norm_w, wt_shard, chip):
    L = x.shape[0]
    tm = _tile(L, 1024)
    ni = L // tm
    r = IN_W // N_DEV
    tn = 2 * r

    def body(chip_ref, x_ref, nw_ref, shard, proj_ref, hn_hbm, wt_hbm, hn_scr, w_scr, send_sems, recv_sems, loc_sems):
        jc, i = pl.program_id(0), pl.program_id(1)
        xx, yy, c = lax.axis_index("x"), lax.axis_index("y"), lax.axis_index("c")
        me, sibling = (xx, yy, c), (xx, yy, 1 - c)
        chips = [(1 - xx, yy), (xx, 1 - yy), (1 - xx, 1 - yy)]

        def slot(dev):
            return wt_hbm.at[4 * dev[0] + 2 * dev[1] + dev[2]]

        def copy(q, block, to, src=None):
            return pltpu.make_async_remote_copy(
                src_ref=slot(block) if src is None else src, dst_ref=slot(block),
                send_sem=send_sems.at[q], recv_sem=recv_sems.at[q], device_id=to, device_id_type=MESH)

        def rows_of(buf, core):
            return w_scr.at[buf, pl.ds(pl.multiple_of(core * r, 16), r)]

        mine = pltpu.make_async_copy(shard, slot(me), loc_sems.at[0])
        sends = [copy(0, me, sibling, src=shard)] + [copy(1 + j, me, (*ch, c), src=shard) for j, ch in enumerate(chips[:2])]
        relay_block = (xx + (1 - c) * (1 - 2 * xx), yy + c * (1 - 2 * yy), c)
        relay = copy(3, relay_block, (xx + c * (1 - 2 * xx), yy + (1 - c) * (1 - 2 * yy), c))
        first = jnp.logical_and(jc == 0, i == 0)

        @pl.when(first)
        def _():
            mine.start()
            for cp in sends:
                cp.start()
            own = pltpu.make_async_copy(shard, rows_of(0, c), loc_sems.at[1])
            own.start()
            copy(0, sibling, me).wait_recv()
            sib = pltpu.make_async_copy(slot(sibling), rows_of(0, 1 - c), loc_sems.at[2])
            sib.start()
            own.wait()
            sib.wait()

        def to_vmem(j, ch):
            pltpu.make_async_copy(slot((*ch, c)), rows_of((1 + j) % 2, c), loc_sems.at[1 + j]).start()

        @pl.when(jnp.logical_and(jc == 1, i == 0))
        def _():
            for j in range(2):
                copy(1 + j, (*chips[j], c), me).wait_recv()
                copy(4 + j, (*chips[j], c), sibling).start()
            relay.start()
            to_vmem(0, chips[0])

        @pl.when(jnp.logical_and(jc == 1, i == ni // 2))
        def _():
            to_vmem(1, chips[1])

        @pl.when(jnp.logical_and(jc == 2, i == ni // 2))
        def _():
            copy(3, (*chips[2], c), me).wait_recv()
            copy(6, (*chips[2], c), sibling).start()
            to_vmem(2, chips[2])

        for j, ch in enumerate(chips):
            @pl.when(jnp.logical_and(jc == 1 + j, i == 0))
            def _(j=j, ch=ch):
                buf = (1 + j) % 2
                copy(4 + j, (*ch, 1 - c), me).wait_recv()
                passed = pltpu.make_async_copy(slot((*ch, 1 - c)), rows_of(buf, 1 - c), loc_sems.at[4 + j])
                passed.start()
                pltpu.make_async_copy(slot((*ch, c)), rows_of(buf, c), loc_sems.at[1 + j]).wait()
                passed.wait()

        rows = pl.ds(pl.multiple_of(i * tm, tm), tm)

        @pl.when(jc == 0)
        def _():
            xv = x_ref[...]
            rstd = lax.rsqrt(jnp.mean(xv * xv, axis=-1, keepdims=True) + NORM_EPS)
            hn_scr[rows, :] = (xv * rstd * nw_ref[...]).astype(BF16)

        keep_hn = pltpu.make_async_copy(hn_scr, hn_hbm, loc_sems.at[7])

        @pl.when(jnp.logical_and(jc == 1, i == 0))
        def _():
            keep_hn.start()

        for buf in range(2):
            @pl.when(jc % 2 == buf)
            def _(buf=buf):
                proj_ref[...] = lax.dot_general(hn_scr[rows, :], w_scr[buf], _NT, preferred_element_type=F32)

        @pl.when(jnp.logical_and(jc == 3, i == ni - 1))
        def _():
            for cp in sends + [relay]:
                cp.wait_send()
            for j, ch in enumerate(chips):
                copy(4 + j, (*ch, c), sibling).wait_send()
            mine.wait()
            keep_hn.wait()

    def tile_of(jc, chip_ref):
        mask = jnp.where(jc == 1, _CHIP_ORDER[1], jnp.where(jc == 2, _CHIP_ORDER[2], jnp.where(jc == 3, _CHIP_ORDER[3], 0)))
        return jnp.bitwise_xor(chip_ref[0], mask)

    held = lambda jc, i: jnp.where(jc == 0, i, ni - 1)
    return pl.pallas_call(
        body,
        name="rms_inproj_gather",
        grid_spec=pltpu.PrefetchScalarGridSpec(
            num_scalar_prefetch=1,
            grid=(4, ni),
            in_specs=[pl.BlockSpec((tm, D_MODEL), lambda jc, i, ch: (held(jc, i), 0)),
                      pl.BlockSpec((1, D_MODEL), lambda jc, i, ch: (0, 0)), _ANY],
            out_specs=[pl.BlockSpec((tm, tn), lambda jc, i, ch: (i, tile_of(jc, ch))), _ANY, _ANY],
            scratch_shapes=[pltpu.VMEM((L, D_MODEL), BF16), pltpu.VMEM((2, tn, D_MODEL), BF16),
                            pltpu.SemaphoreType.DMA((7,)), pltpu.SemaphoreType.DMA((7,)), pltpu.SemaphoreType.DMA((8,))],
        ),
        out_shape=[jax.ShapeDtypeStruct((L, IN_W), F32), jax.ShapeDtypeStruct((L, D_MODEL), BF16),
                   jax.ShapeDtypeStruct((N_DEV, r, D_MODEL), BF16)],
        compiler_params=_cp(("arbitrary", "arbitrary")),
    )(chip, x, norm_w.reshape(1, D_MODEL), wt_shard)


def _seg_sum(v):
    a = lax.broadcasted_iota(jnp.int32, (128, 128), 0) // HEAD_DIM
    b = lax.broadcasted_iota(jnp.int32, (128, 128), 1) // HEAD_DIM
    ones = jnp.where(a == b, 1.0, 0.0).astype(BF16)
    hi = v.astype(BF16)
    lo = (v - hi.astype(F32)).astype(BF16)
    return jnp.dot(hi, ones, preferred_element_type=F32) + jnp.dot(lo, ones, preferred_element_type=F32)


def _rot_half(t):
    lane = lax.broadcasted_iota(jnp.int32, t.shape, 1)
    return jnp.where(lane % HEAD_DIM < HEAD_DIM // 2, pltpu.roll(t, 128 - HEAD_DIM // 2, 1),
                     pltpu.roll(t, HEAD_DIM // 2, 1))


def _norm_rope(raw, w, cos, sin):
    r = lax.rsqrt(_seg_sum(raw * raw) * (1.0 / HEAD_DIM) + NORM_EPS)
    tn = raw * r * w
    return r, tn * cos + _rot_half(tn) * sin


def _norm_rope_bwd(d_rot, raw, w, cos, sin):
    r = lax.rsqrt(_seg_sum(raw * raw) * (1.0 / HEAD_DIM) + NORM_EPS)
    d_tn = d_rot * cos + _rot_half(d_rot * sin)
    xh = raw * r
    gw = d_tn * w
    d_raw = r * (gw - xh * (_seg_sum(gw * xh) * (1.0 / HEAD_DIM)))
    return d_raw, d_tn * xh


def _band_mask2(has_prev, keys_on_rows=False):
    qd, kd = (1, 0) if keys_on_rows else (0, 1)
    qi = lax.broadcasted_iota(jnp.int32, (2 * BLOCK, 2 * BLOCK), qd) % BLOCK + BLOCK
    kj = lax.broadcasted_iota(jnp.int32, (2 * BLOCK, 2 * BLOCK), kd)
    rel = qi - kj
    return (rel >= 0) & (rel < BLOCK) & ((kj >= BLOCK) | has_prev)


def _half_tiles(pair):
    lo = lax.broadcasted_iota(jnp.int32, pair.shape, 1) < HEAD_DIM
    sw = pltpu.roll(pair, HEAD_DIM, 1)
    z = jnp.zeros_like(pair)
    return (jnp.where(lo, pair, z).astype(BF16), jnp.where(lo, z, sw).astype(BF16),
            jnp.where(lo, sw, z).astype(BF16), jnp.where(lo, z, pair).astype(BF16))


def _two_rows(top, bottom):
    row = lax.broadcasted_iota(jnp.int32, (2 * BLOCK, 1), 0)
    return jnp.where(row < BLOCK, top, bottom)


_SCALE = 1.0 / math.sqrt(HEAD_DIM)
_NT = (((1,), (1,)), ((), ()))
_NN = (((1,), (0,)), ((), ()))
_TN = (((0,), (0,)), ((), ()))


def _qk_prep(proj, tab, qw, kw):
    L = proj.shape[0]
    tm = _tile(L, 512)

    def body(q_ref, k_ref, t_ref, qw_ref, kw_ref, qo_ref, ko_ref):
        cos, sin = t_ref[:, :128], t_ref[:, 128:]
        for c in range(ATTN_W // 128):
            _, qr = _norm_rope(q_ref[:, c * 128:(c + 1) * 128], qw_ref[...], cos, sin)
            qo_ref[:, c * 128:(c + 1) * 128] = (qr * _SCALE).astype(BF16)
        for c in range(KV_W // 128):
            _, kr = _norm_rope(k_ref[:, c * 128:(c + 1) * 128], kw_ref[...], cos, sin)
            ko_ref[:, c * 128:(c + 1) * 128] = kr.astype(BF16)

    row = pl.BlockSpec((1, 128), lambda i: (0, 0))
    return pl.pallas_call(
        body,
        name="qk_prep",
        grid=(L // tm,),
        in_specs=[pl.BlockSpec((tm, ATTN_W), lambda i: (i, 0)), pl.BlockSpec((tm, KV_W), lambda i: (i, 4)),
                  pl.BlockSpec((tm, 256), lambda i: (i, 0)), row, row],
        out_specs=[pl.BlockSpec((tm, ATTN_W), lambda i: (i, 0)), pl.BlockSpec((tm, KV_W), lambda i: (i, 0))],
        out_shape=[jax.ShapeDtypeStruct((L, ATTN_W), BF16), jax.ShapeDtypeStruct((L, KV_W), BF16)],
        compiler_params=_cp(("parallel",)),
    )(proj, proj, tab, jnp.tile(qw, 2).reshape(1, 128), jnp.tile(kw, 2).reshape(1, 128))


def _group_tiles(g, kt, vt):
    a, b = divmod(g, 2)
    return kt[a][2 * b], kt[a][2 * b + 1], vt[a][2 * b], vt[a][2 * b + 1]


def _attn_fwd(q, k, proj, sinks, ride):
    L = proj.shape[0]
    nb = L // BLOCK

    def body(q_ref, kc_ref, kp_ref, vc_ref, vp_ref, z0_ref, z1_ref, sink_ref, og_ref, o_ref, lse_ref):
        i = pl.program_id(0)
        mask = _band_mask2(i > 0)
        z = jnp.concatenate([z0_ref[...], z1_ref[...]], axis=1)
        lane = lax.broadcasted_iota(jnp.int32, (BLOCK, 128), 1)
        kt = [_half_tiles(jnp.concatenate([kp_ref[:, a * 128:(a + 1) * 128], kc_ref[:, a * 128:(a + 1) * 128]],
                                          axis=0).astype(F32)) for a in range(2)]
        vt = [_half_tiles(jnp.concatenate([vp_ref[:, a * 128:(a + 1) * 128], vc_ref[:, a * 128:(a + 1) * 128]],
                                          axis=0)) for a in range(2)]
        lse_mat = jnp.zeros((BLOCK, 128), F32)
        pairs = []
        for g in range(N_KV):
            k_lo, k_hi, v_lo, v_hi = _group_tiles(g, kt, vt)
            q2 = jnp.concatenate([q_ref[:, 2 * g * 128:(2 * g + 1) * 128],
                                  q_ref[:, (2 * g + 1) * 128:(2 * g + 2) * 128]], axis=0)
            for half, (kh, vh) in enumerate(((k_lo, v_lo), (k_hi, v_hi))):
                pairs.append(dict(g=g, half=half, vh=vh, s=lax.dot_general(q2, kh, _NT, preferred_element_type=F32)))
        for pr in pairs:
            h_top, h_bot = 4 * pr["g"] + pr["half"], 4 * pr["g"] + 2 + pr["half"]
            s = jnp.where(mask, pr["s"], -1e30)
            sink = _two_rows(sink_ref[h_top], sink_ref[h_bot])
            m = jnp.maximum(jnp.max(s, axis=-1, keepdims=True), sink)
            e = jnp.exp(s - m)
            den = jnp.sum(e, axis=-1, keepdims=True) + jnp.exp(sink - m)
            pr["p_b"] = (e * (1.0 / den)).astype(BF16)
            lse = m + jnp.log(den)
            lse_mat = jnp.where(lane == h_top, lse[:BLOCK], lse_mat)
            lse_mat = jnp.where(lane == h_bot, lse[BLOCK:], lse_mat)
        outs = []
        for g in range(N_KV):
            acc = (jnp.dot(pairs[2 * g]["p_b"], pairs[2 * g]["vh"], preferred_element_type=F32)
                   + jnp.dot(pairs[2 * g + 1]["p_b"], pairs[2 * g + 1]["vh"], preferred_element_type=F32))
            outs += [acc[:BLOCK], acc[BLOCK:]]
        o = jnp.concatenate(outs, axis=1)
        o_ref[...] = o.astype(BF16)
        og_ref[...] = (o * _silu(z)).astype(BF16)
        lse_ref[...] = lse_mat

    prev = lambda i: jnp.maximum(i - 1, 0)
    return _call(
        body, "attn_fwd", (nb,),
        [pl.BlockSpec((BLOCK, ATTN_W), lambda i: (i, 0)),
         pl.BlockSpec((BLOCK, KV_W), lambda i: (i, 0)),
         pl.BlockSpec((BLOCK, KV_W), lambda i: (prev(i), 0)),
         pl.BlockSpec((BLOCK, KV_W), lambda i: (i, 5)),
         pl.BlockSpec((BLOCK, KV_W), lambda i: (prev(i), 5)),
         pl.BlockSpec((BLOCK, 512), lambda i: (i, 3)),
         pl.BlockSpec((BLOCK, 512), lambda i: (i, 4)),
         pl.BlockSpec(memory_space=pltpu.SMEM)],
        [pl.BlockSpec((BLOCK, ATTN_W), lambda i: (i, 0)),
         pl.BlockSpec((BLOCK, ATTN_W), lambda i: (i, 0)),
         pl.BlockSpec((BLOCK, 128), lambda i: (i, 0))],
        [jax.ShapeDtypeStruct((L, ATTN_W), BF16), jax.ShapeDtypeStruct((L, ATTN_W), BF16),
         jax.ShapeDtypeStruct((L, 128), F32)],
        (q, k, k, proj, proj, proj, proj, sinks), ride=ride)


def _attn_bwd(q, k, proj, sinks, d_o, o, lse, ride):
    L = proj.shape[0]
    nb = L // BLOCK

    def body(q_ref, kc_ref, kp_ref, vc_ref, vp_ref, do_ref, o_ref, lse_ref, sink_ref,
             dq_ref, dk_ref, dv_ref, gs_ref, ck_scr, cv_scr):
        i = pl.program_id(0)

        @pl.when(i == 0)
        def _():
            gs_ref[...] = jnp.zeros_like(gs_ref)
            ck_scr[...] = jnp.zeros_like(ck_scr)
            cv_scr[...] = jnp.zeros_like(cv_scr)

        @pl.when(i == nb)
        def _():
            dk_ref[...] = ck_scr[...]
            dv_ref[...] = cv_scr[...]

        @pl.when(i < nb)
        def _():
            mask = _band_mask2(i > 0, keys_on_rows=True)
            lane = lax.broadcasted_iota(jnp.int32, (1, 128), 1)
            lane2 = lax.broadcasted_iota(jnp.int32, (1, 2 * BLOCK), 1)
            lo = lax.broadcasted_iota(jnp.int32, (2 * BLOCK, 128), 1) < HEAD_DIM
            lse_t = lse_ref[...].T
            prod_all = do_ref[...].astype(F32) * o_ref[...].astype(F32)
            seg = (lax.broadcasted_iota(jnp.int32, (N_HEADS, ATTN_W), 1) // HEAD_DIM
                   == lax.broadcasted_iota(jnp.int32, (N_HEADS, ATTN_W), 0)).astype(BF16)
            prod_hi = prod_all.astype(BF16)
            prod_lo = (prod_all - prod_hi.astype(F32)).astype(BF16)
            delta_t = (lax.dot_general(seg, prod_hi, _NT, preferred_element_type=F32)
                       + lax.dot_general(seg, prod_lo, _NT, preferred_element_type=F32))
            kt = [_half_tiles(jnp.concatenate([kp_ref[:, a * 128:(a + 1) * 128], kc_ref[:, a * 128:(a + 1) * 128]],
                                              axis=0).astype(F32)) for a in range(2)]
            vt = [_half_tiles(jnp.concatenate([vp_ref[:, a * 128:(a + 1) * 128], vc_ref[:, a * 128:(a + 1) * 128]],
                                              axis=0)) for a in range(2)]
            gs = jnp.zeros((1, 128), F32)
            dq_parts = []
            dk_acc = [jnp.zeros((2 * BLOCK, 128), F32) for _ in range(2)]
            dv_acc = [jnp.zeros((2 * BLOCK, 128), F32) for _ in range(2)]
            pairs = []
            for g in range(N_KV):
                k_lo, k_hi, v_lo, v_hi = _group_tiles(g, kt, vt)
                t0, t1 = slice(2 * g * 128, (2 * g + 1) * 128), slice((2 * g + 1) * 128, (2 * g + 2) * 128)
                q2 = jnp.concatenate([q_ref[:, t0], q_ref[:, t1]], axis=0)
                do2_b = jnp.concatenate([do_ref[:, t0], do_ref[:, t1]], axis=0).astype(BF16)
                for half, (kh, vh) in enumerate(((k_lo, v_lo), (k_hi, v_hi))):
                    pairs.append(dict(g=g, half=half, kh=kh, q2=q2, do2_b=do2_b,
                                      s=lax.dot_general(kh, q2, _NT, preferred_element_type=F32),
                                      dp=lax.dot_general(vh, do2_b, _NT, preferred_element_type=F32)))
            for pr in pairs:
                h_top, h_bot = 4 * pr["g"] + pr["half"], 4 * pr["g"] + 2 + pr["half"]
                pick = lambda t: jnp.concatenate([t[h_top:h_top + 1, :], t[h_bot:h_bot + 1, :]], axis=1)
                lse, delta = pick(lse_t), pick(delta_t)
                sink = jnp.where(lane2 < BLOCK, sink_ref[h_top], sink_ref[h_bot])
                p = jnp.exp(jnp.where(mask, pr["s"], -1e30) - lse)
                pr["ds_b"] = (p * (pr["dp"] - delta)).astype(BF16)
                pr["p_b"] = p.astype(BF16)
                gsink = -jnp.exp(sink - lse) * delta
                gs = gs + jnp.where(lane == h_top, jnp.sum(jnp.where(lane2 < BLOCK, gsink, 0.0)), 0.0)
                gs = gs + jnp.where(lane == h_bot, jnp.sum(jnp.where(lane2 >= BLOCK, gsink, 0.0)), 0.0)
            for g in range(N_KV):
                a, b = divmod(g, 2)
                dq2 = jnp.zeros((2 * BLOCK, 128), F32)
                dk_h, dv_h = [], []
                for pr in pairs[2 * g:2 * g + 2]:
                    dq2 = dq2 + lax.dot_general(pr["ds_b"], pr["kh"], _TN, preferred_element_type=F32)
                    dk_h.append(jnp.dot(pr["ds_b"], pr["q2"], preferred_element_type=F32))
                    dv_h.append(jnp.dot(pr["p_b"], pr["do2_b"], preferred_element_type=F32))
                dq_parts += [dq2[:BLOCK], dq2[BLOCK:]]
                for acc, parts in ((dk_acc, dk_h), (dv_acc, dv_h)):
                    t = jnp.where(lo, parts[0], parts[1])
                    t = t + pltpu.roll(t, HEAD_DIM, 1)
                    acc[a] = acc[a] + jnp.where(lo == (b == 0), t, 0.0)
            dq_ref[...] = jnp.concatenate(dq_parts, axis=1)
            dk_full = jnp.concatenate(dk_acc, axis=1)
            dv_full = jnp.concatenate(dv_acc, axis=1)
            dk_ref[...] = ck_scr[...] + dk_full[:BLOCK]
            dv_ref[...] = cv_scr[...] + dv_full[:BLOCK]
            ck_scr[...] = dk_full[BLOCK:]
            cv_scr[...] = dv_full[BLOCK:]
            gs_ref[...] += gs

    cur = lambda i: jnp.minimum(i, nb - 1)
    prev = lambda i: jnp.maximum(jnp.minimum(i, nb - 1) - 1, 0)
    done = lambda i: jnp.maximum(i - 1, 0)
    bs = pl.BlockSpec
    return _call(
        body, "attn_bwd", (nb + 1,),
        [bs((BLOCK, ATTN_W), lambda i: (cur(i), 0)),
         bs((BLOCK, KV_W), lambda i: (cur(i), 0)), bs((BLOCK, KV_W), lambda i: (prev(i), 0)),
         bs((BLOCK, KV_W), lambda i: (cur(i), 5)), bs((BLOCK, KV_W), lambda i: (prev(i), 5)),
         bs((BLOCK, ATTN_W), lambda i: (cur(i), 0)), bs((BLOCK, ATTN_W), lambda i: (cur(i), 0)),
         bs((BLOCK, 128), lambda i: (cur(i), 0)), bs(memory_space=pltpu.SMEM)],
        [bs((BLOCK, ATTN_W), lambda i: (cur(i), 0)),
         bs((BLOCK, KV_W), lambda i: (done(i), 0)), bs((BLOCK, KV_W), lambda i: (done(i), 0)),
         bs((1, 128), lambda i: (0, 0))],
        [jax.ShapeDtypeStruct((L, ATTN_W), F32), jax.ShapeDtypeStruct((L, KV_W), F32),
         jax.ShapeDtypeStruct((L, KV_W), F32), jax.ShapeDtypeStruct((1, 128), F32)],
        (q, k, k, proj, proj, d_o, o, lse, sinks),
        [pltpu.VMEM((BLOCK, KV_W), F32), pltpu.VMEM((BLOCK, KV_W), F32)], ride)


def _qk_prep_bwd(proj, tab, qw, kw, d_q, d_k, d_v, d_za, d_u, d_zs):
    L = proj.shape[0]
    tm = _tile(L, 512)
    z0 = ATTN_W + 2 * KV_W

    def body(q_ref, k_ref, t_ref, qw_ref, kw_ref, dq_ref, dk_ref, dv_ref, dza_ref, du_ref, dzs_ref,
             out_ref, gq_ref, gk_ref):
        i = pl.program_id(0)

        @pl.when(i == 0)
        def _():
            gq_ref[...] = jnp.zeros_like(gq_ref)
            gk_ref[...] = jnp.zeros_like(gk_ref)

        cos, sin = t_ref[:, :128], t_ref[:, 128:]
        gq = jnp.zeros((1, 128), F32)
        gk = jnp.zeros((1, 128), F32)
        for c in range(ATTN_W // 128):
            cs = slice(c * 128, (c + 1) * 128)
            d_raw, gw = _norm_rope_bwd(dq_ref[:, cs] * _SCALE, q_ref[:, cs], qw_ref[...], cos, sin)
            out_ref[:, cs] = d_raw.astype(BF16)
            gq = gq + jnp.sum(gw, axis=0, keepdims=True)
        for c in range(KV_W // 128):
            cs = slice(c * 128, (c + 1) * 128)
            d_raw, gw = _norm_rope_bwd(dk_ref[:, cs], k_ref[:, cs], kw_ref[...], cos, sin)
            out_ref[:, ATTN_W + c * 128:ATTN_W + (c + 1) * 128] = d_raw.astype(BF16)
            gk = gk + jnp.sum(gw, axis=0, keepdims=True)
        out_ref[:, ATTN_W + KV_W:z0] = dv_ref[...].astype(BF16)
        out_ref[:, z0:z0 + ATTN_W] = dza_ref[...]
        out_ref[:, z0 + ATTN_W:z0 + ATTN_W + SSM_W] = du_ref[...].astype(BF16)
        out_ref[:, z0 + ATTN_W + SSM_W:] = dzs_ref[...]
        gq_ref[...] += gq
        gk_ref[...] += gk

    row = pl.BlockSpec((1, 128), lambda i: (0, 0))
    blk = lambda w, c: pl.BlockSpec((tm, w), lambda i: (i, c))
    return pl.pallas_call(
        body,
        name="qk_prep_bwd",
        grid=(L // tm,),
        in_specs=[blk(ATTN_W, 0), blk(KV_W, 4), blk(256, 0), row, row, blk(ATTN_W, 0), blk(KV_W, 0), blk(KV_W, 0),
                  blk(ATTN_W, 0), blk(SSM_W, 0), blk(SSM_W, 0)],
        out_specs=[blk(IN_W, 0), row, row],
        out_shape=[jax.ShapeDtypeStruct((L, IN_W), BF16), jax.ShapeDtypeStruct((1, 128), F32),
                   jax.ShapeDtypeStruct((1, 128), F32)],
        compiler_params=_cp(("arbitrary",)),
    )(proj, proj, tab, jnp.tile(qw, 2).reshape(1, 128), jnp.tile(kw, 2).reshape(1, 128), d_q, d_k, d_v,
      d_za, d_u, d_zs)


def _cmul(a, b):
    return a[0] * b[0] - a[1] * b[1], a[0] * b[1] + a[1] * b[0]


def _cmul_conj(a, b):
    return a[0] * b[0] + a[1] * b[1], a[1] * b[0] - a[0] * b[1]


def _cadd(a, b):
    return a[0] + b[0], a[1] + b[1]


def _dot3(a, b, dn):
    ah, bh = a.astype(BF16), b.astype(BF16)
    al, bl = (a - ah.astype(F32)).astype(BF16), (b - bh.astype(F32)).astype(BF16)
    d = lambda u, v: lax.dot_general(u, v, dn, preferred_element_type=F32)
    return d(ah, bh) + d(ah, bl) + d(al, bh)


def _s5_discretise(a_re, a_im, ls, cosx, sinx, bt):
    delta = jnp.exp(ls)
    er = jnp.exp(a_re * delta)
    lb = (er * cosx, er * sinx)
    den = a_re * a_re + a_im * a_im
    coef = _cmul_conj((lb[0] - 1.0, lb[1]), (a_re, a_im))
    coef = (coef[0] / den, coef[1] / den)
    return delta, lb, coef, den, _cmul(coef, bt)


def _powers(lb):
    pw = [(jnp.ones_like(lb[0]), jnp.zeros_like(lb[0]))]
    for _ in range(CHUNK):
        pw.append(_cmul(pw[-1], lb))
    return pw


def _block_rows(a, pw, idx):
    blocks = [_cmul(a, pw[i]) for i in idx]
    return (jnp.concatenate([b[0] for b in blocks], axis=-2), jnp.concatenate([b[1] for b in blocks], axis=-2))


def _block_rows_bwd(g, a, pw, idx, g_pw):
    g_a = (jnp.zeros_like(a[0]), jnp.zeros_like(a[0]))
    for j, i in enumerate(idx):
        gj = (g[0][..., j * SSM_H:(j + 1) * SSM_H, :], g[1][..., j * SSM_H:(j + 1) * SSM_H, :])
        g_a = _cadd(g_a, _cmul_conj(gj, pw[i]))
        gp = _cmul_conj(gj, a)
        g_pw[i] = _cadd(g_pw[i], (jnp.sum(gp[0], axis=-2, keepdims=True), jnp.sum(gp[1], axis=-2, keepdims=True)))
    return g_a


_IDX_S = [CHUNK - 1 - s for s in range(CHUNK)]
_IDX_C = list(range(CHUNK + 1))


def _prep_args(p):
    row = lambda t: t.reshape(SSM_G, 1, SSM_P)
    xi = p["a_im"] * jnp.exp(p["log_step"])[:, None]
    return (row(p["a_re"]), row(p["a_im"]), row(jnp.broadcast_to(p["log_step"][:, None], (SSM_G, SSM_P))),
            row(jnp.cos(xi)), row(jnp.sin(xi)), p["b_re"].transpose(0, 2, 1), p["b_im"].transpose(0, 2, 1),
            p["c_re"], p["c_im"])


PREP_GROUPS = 8


def _prep_specs():
    r1 = pl.BlockSpec((PREP_GROUPS, 1, SSM_P), lambda g: (g, 0, 0))
    r16 = pl.BlockSpec((PREP_GROUPS, SSM_H, SSM_P), lambda g: (g, 0, 0))
    return [r1] * 5 + [r16] * 4, r1, r16


def _ssm_prep(p):
    def one_group(q, are, aim, ls, cosx, sinx, btr, bti, cre, cim, mt_ref, s_ref, o_ref, a_ref):
        _, lb, _, _, bb = _s5_discretise(are[q], aim[q], ls[q], cosx[q], sinx[q], (btr[q], bti[q]))
        pw = _powers(lb)
        c = (cre[q], cim[q])
        sc = _block_rows(bb, pw, _IDX_S)
        cl = _block_rows(c, pw, _IDX_C)
        ok = (cl[0][:CW], cl[1][:CW])
        ot = (cl[0][SSM_H:], cl[1][SSM_H:])
        s_ref[q] = jnp.concatenate([sc[0], sc[1]], axis=1).astype(BF16)
        o_ref[q] = jnp.concatenate([ot[0], -ot[1]], axis=1).astype(BF16)
        a_ref[q] = jnp.concatenate([pw[CHUNK][0], pw[CHUNK][1]], axis=1)
        kt = _dot3(jnp.concatenate([bb[0], -bb[1]], axis=1), jnp.concatenate([ok[0], ok[1]], axis=1), _NT)
        lane = lax.broadcasted_iota(jnp.int32, kt.shape, 1)
        for s in range(CHUNK):
            blk = kt if s == 0 else jnp.where(lane >= SSM_H * s, pltpu.roll(kt, SSM_H * s, 1), 0.0)
            mt_ref[q, s * SSM_H:(s + 1) * SSM_H, :] = blk.astype(BF16)

    def body(*refs):
        for q in range(PREP_GROUPS):
            one_group(q, *refs)

    in_specs, r1, _ = _prep_specs()
    g3 = lambda r, c: pl.BlockSpec((PREP_GROUPS, r, c), lambda g: (g, 0, 0))
    return pl.pallas_call(
        body,
        name="ssm_prep",
        grid=(SSM_G // PREP_GROUPS,),
        in_specs=in_specs,
        out_specs=[g3(CW, CW), g3(CW, 2 * SSM_P), g3(CW, 2 * SSM_P), g3(1, 2 * SSM_P)],
        out_shape=[jax.ShapeDtypeStruct((SSM_G, CW, CW), BF16), jax.ShapeDtypeStruct((SSM_G, CW, 2 * SSM_P), BF16),
                   jax.ShapeDtypeStruct((SSM_G, CW, 2 * SSM_P), BF16),
                   jax.ShapeDtypeStruct((SSM_G, 1, 2 * SSM_P), F32)],
        compiler_params=_cp(("parallel",)),
    )(*_prep_args(p))


def _ssm_prep_bwd(p, g_mt, g_scat, g_ocat, g_a16, ride):
    def body(are, aim, ls, cosx, sinx, btr, bti, cre, cim, gmt_ref, gs_ref, go_ref, ga_ref,
             g_are, g_aim, g_ls, g_btr, g_bti, g_cre, g_cim, ga1_scr, gb1_scr):
        lam = (are[...], aim[...])
        bt = (btr[...], bti[...])
        delta, lb, coef, den, bb = _s5_discretise(lam[0], lam[1], ls[...], cosx[...], sinx[...], bt)
        pw = _powers(lb)
        c = (cre[...], cim[...])
        ok = _block_rows(c, pw, _IDX_C[:CHUNK])
        g_pw =[(jnp.zeros_like(lb[0]), jnp.zeros_like(lb[0])) for _ in range(CHUNK + 1)]
        lane = lax.broadcasted_iota(jnp.int32, (SSM_H, CW), 1)
        for q in range(PREP_GROUPS):
            g_kt = gmt_ref[q, :SSM_H, :]
            for s in range(1, CHUNK):
                blk = gmt_ref[q, s * SSM_H:(s + 1) * SSM_H, :]
                g_kt = g_kt + jnp.where(lane < CW - SSM_H * s, pltpu.roll(blk, CW - SSM_H * s, 1), 0.0)
            a1 = jnp.concatenate([bb[0][q], -bb[1][q]], axis=1)
            b1 = jnp.concatenate([ok[0][q], ok[1][q]], axis=1)
            ga1_scr[q] = _dot3(g_kt, b1, _NN)
            gb1_scr[q] = _dot3(g_kt, a1, _TN)
        g_a1, g_b1 = ga1_scr[...], gb1_scr[...]
        g_bb = (g_a1[..., :SSM_P], -g_a1[..., SSM_P:])
        gs = gs_ref[...]
        g_bb = _cadd(g_bb, _block_rows_bwd((gs[..., :SSM_P], gs[..., SSM_P:]), bb, pw, _IDX_S, g_pw))
        go = go_ref[...]
        pad = jnp.zeros_like(go[..., :SSM_H, :SSM_P])
        g_cl = (jnp.concatenate([g_b1[..., :SSM_P], pad], axis=-2) + jnp.concatenate([pad, go[..., :SSM_P]], axis=-2),
                jnp.concatenate([g_b1[..., SSM_P:], pad], axis=-2) - jnp.concatenate([pad, go[..., SSM_P:]], axis=-2))
        g_c = _block_rows_bwd(g_cl, c, pw, _IDX_C, g_pw)
        ga = ga_ref[...]
        g_pw[CHUNK] = _cadd(g_pw[CHUNK], (ga[..., :SSM_P], ga[..., SSM_P:]))
        g_lb = (jnp.zeros_like(lb[0]), jnp.zeros_like(lb[0]))
        for l in range(CHUNK - 1, -1, -1):
            g_lb = _cadd(g_lb, _cmul_conj(g_pw[l + 1], pw[l]))
            g_pw[l] = _cadd(g_pw[l], _cmul_conj(g_pw[l + 1], lb))
        g_bt = _cmul_conj(g_bb, coef)
        gc = _cmul_conj(g_bb, bt)
        g_coef = (jnp.sum(gc[0], axis=-2, keepdims=True), jnp.sum(gc[1], axis=-2, keepdims=True))
        lam_den = (lam[0] / den, lam[1] / den)
        g_lb = _cadd(g_lb, _cmul(g_coef, lam_den))
        t = _cmul(_cmul_conj(g_coef, coef), lam_den)
        g_x = _cmul_conj(g_lb, lb)
        g_are[...] = g_x[0] * delta - t[0]
        g_aim[...] = g_x[1] * delta - t[1]
        g_ls[...] = (g_x[0] * lam[0] + g_x[1] * lam[1]) * delta
        g_btr[...] = g_bt[0]
        g_bti[...] = g_bt[1]
        g_cre[...] = g_c[0]
        g_cim[...] = g_c[1]

    in_specs, r1, r16 = _prep_specs()
    g3 = lambda r, c: pl.BlockSpec((PREP_GROUPS, r, c), lambda g: (g, 0, 0))
    rows = jax.ShapeDtypeStruct((SSM_G, 1, SSM_P), F32)
    mats = jax.ShapeDtypeStruct((SSM_G, SSM_H, SSM_P), F32)
    (g_are, g_aim, g_ls, g_btr, g_bti, g_cre, g_cim), landed = _call(
        body, "ssm_prep_bwd", (SSM_G // PREP_GROUPS,),
        in_specs + [g3(CW, CW), g3(CW, 2 * SSM_P), g3(CW, 2 * SSM_P), g3(1, 2 * SSM_P)],
        [r1] * 3 + [r16] * 4, [rows] * 3 + [mats] * 4, (*_prep_args(p), g_mt, g_scat, g_ocat, g_a16),
        [pltpu.VMEM((PREP_GROUPS, SSM_H, 2 * SSM_P), F32), pltpu.VMEM((PREP_GROUPS, CW, 2 * SSM_P), F32)], ride)
    grads = dict(a_re=g_are.reshape(SSM_G, SSM_P), a_im=g_aim.reshape(SSM_G, SSM_P),
                 log_step=jnp.sum(g_ls.reshape(SSM_G, SSM_P), axis=1),
                 b_re=g_btr.transpose(0, 2, 1), b_im=g_bti.transpose(0, 2, 1), c_re=g_cre, c_im=g_cim)
    return grads, landed


def _cmul_const(xv, ar, ai):
    return xv * ar + pltpu.roll(xv, SSM_P, 1) * ai


def _chunk_scan(inc, a_row, reverse):
    n = inc.shape[0]
    lane = lax.broadcasted_iota(jnp.int32, (1, 2 * SSM_P), 1)
    row = lax.broadcasted_iota(jnp.int32, inc.shape, 0)
    sign = jnp.where(lane < SSM_P, -1.0, 1.0)
    ar = jnp.where(lane < SSM_P, a_row, pltpu.roll(a_row, SSM_P, 1))
    ai = jnp.where(lane < SSM_P, pltpu.roll(a_row, SSM_P, 1), a_row)
    if reverse:
        ai = -ai
    xv = inc
    s = 1
    while s < n:
        if reverse:
            sh = jnp.where(row < n - s, pltpu.roll(xv, n - s, 0), 0.0)
        else:
            sh = jnp.where(row >= s, pltpu.roll(xv, s, 0), 0.0)
        xv = xv + _cmul_const(sh, ar, ai * sign)
        ar, ai = ar * ar - ai * ai, 2.0 * ar * ai
        s *= 2
    return xv


def _shift_rows(xv, reverse):
    n = xv.shape[0]
    row = lax.broadcasted_iota(jnp.int32, xv.shape, 0)
    if reverse:
        return jnp.where(row < n - 1, pltpu.roll(xv, n - 1, 0), 0.0)
    return jnp.where(row >= 1, pltpu.roll(xv, 1, 0), 0.0)


GB = 128 // SSM_H
U_COL0 = (ATTN_W + 2 * KV_W + ATTN_W) // 128


HALF = CHUNK // 2


def _chunk_perm():
    r = jnp.arange(HALF * 128)
    t, g8, h = r // 128, (r % 128) // SSM_H, r % SSM_H
    return ((g8 * 128 + t * SSM_H + h)[:, None] == jnp.arange(GB * 128)[None, :]).astype(BF16)


def _load_perm(p_hbm, p_scr, sem):
    @pl.when(pl.program_id(0) == 0)
    def _():
        cp = pltpu.make_async_copy(p_hbm, p_scr, sem)
        cp.start()
        cp.wait()


def _rows_to_chunks(pieces, perm):
    halves = [jnp.dot(jnp.concatenate(pieces[k * HALF:(k + 1) * HALF], axis=1).astype(BF16), perm,
                      preferred_element_type=F32).astype(BF16) for k in range(2)]
    return [jnp.concatenate([hv[:, g * 128:(g + 1) * 128] for hv in halves], axis=1) for g in range(GB)]


def _chunks_to_rows(groups, perm, two_pass):
    pieces = []
    for k in range(2):
        v = jnp.concatenate([gv[:, k * 128:(k + 1) * 128] for gv in groups], axis=1)
        hi = v.astype(BF16)
        out = lax.dot_general(hi, perm, _NT, preferred_element_type=F32)
        if two_pass:
            lo = (v - hi.astype(F32)).astype(BF16)
            out = out + lax.dot_general(lo, perm, _NT, preferred_element_type=F32)
        pieces += [out[:, t * 128:(t + 1) * 128] for t in range(HALF)]
    return pieces


def _ssm_fwd(proj, perm, mt, scat, ocat, a16, d_skip, ride):
    L = proj.shape[0]
    nc = L // CHUNK

    def body(u_ref, p_hbm, mt_ref, s_ref, o_ref, a_ref, d_ref, y_ref, yg_ref, h_ref, p_scr, sem):
        _load_perm(p_hbm, p_scr, sem)
        perm = p_scr[...]
        rows = [pl.ds(t, nc, stride=CHUNK) for t in range(CHUNK)]
        us = [u_ref[r, :] for r in rows]
        ua = _rows_to_chunks(us, perm)
        incs = [jnp.dot(ua[g], s_ref[g], preferred_element_type=F32) for g in range(GB)]
        intra = [jnp.dot(ua[g], mt_ref[g], preferred_element_type=F32) for g in range(GB)]
        hxs = [_shift_rows(_chunk_scan(incs[g], a_ref[g], False), False) for g in range(GB)]
        ys = []
        for g in range(GB):
            h_ref[g] = hxs[g]
            ys.append(intra[g] + lax.dot_general(hxs[g].astype(BF16), o_ref[g], _NT, preferred_element_type=F32))
        yp = _chunks_to_rows(ys, perm, True)
        for t, r in enumerate(rows):
            y = yp[t] + d_ref[...] * us[t]
            y_ref[r, :] = y
            yg_ref[r, :] = _gelu(y)

    g3 = lambda r, c: pl.BlockSpec((GB, r, c), lambda g: (g, 0, 0))
    col = pl.BlockSpec((L, 128), lambda g: (0, g))
    return _call(
        body, "ssm_fwd", (SSM_G // GB,),
        [pl.BlockSpec((L, 128), lambda g: (0, U_COL0 + g)), _ANY,
         g3(CW, CW), g3(CW, 2 * SSM_P), g3(CW, 2 * SSM_P), g3(1, 2 * SSM_P),
         pl.BlockSpec((1, 128), lambda g: (0, g))],
        [col, col, g3(nc, 2 * SSM_P)],
        [jax.ShapeDtypeStruct((L, SSM_W), F32), jax.ShapeDtypeStruct((L, SSM_W), F32),
         jax.ShapeDtypeStruct((SSM_G, nc, 2 * SSM_P), F32)],
        (proj, perm, mt, scat, ocat, a16, d_skip.reshape(1, SSM_W)),
        [pltpu.VMEM((HALF * 128, GB * 128), BF16), pltpu.SemaphoreType.DMA], ride)


def _ssm_bwd(d_yg, y, proj, hx, perm, mt, scat, ocat, a16, d_skip, ride):
    L = proj.shape[0]
    nc = L // CHUNK

    def body(dg_ref, y_ref, u_ref, h_ref, p_hbm, mt_ref, s_ref, o_ref, a_ref, d_ref,
             du_ref, gmt_ref, gs_ref, go_ref, ga_ref, gd_ref, p_scr, sem):
        _load_perm(p_hbm, p_scr, sem)
        perm = p_scr[...]
        rows = [pl.ds(t, nc, stride=CHUNK) for t in range(CHUNK)]
        us = [u_ref[r, :] for r in rows]
        dys = [dg_ref[r, :] * _dgelu(y_ref[r, :]) for r in rows]
        gd = jnp.zeros((1, 128), F32)
        for uv, dy in zip(us, dys):
            gd = gd + jnp.sum(dy * uv, axis=0, keepdims=True)
        gd_ref[...] = gd
        ua = _rows_to_chunks(us, perm)
        dya = _rows_to_chunks(dys, perm)
        lane = lax.broadcasted_iota(jnp.int32, (1, 2 * SSM_P), 1)
        dhs = [jnp.dot(dya[g], o_ref[g], preferred_element_type=F32) for g in range(GB)]
        intra = [lax.dot_general(dya[g], mt_ref[g], _NT, preferred_element_type=F32) for g in range(GB)]
        for g in range(GB):
            gmt_ref[g] = lax.dot_general(ua[g], dya[g], _TN, preferred_element_type=F32)
            go_ref[g] = lax.dot_general(dya[g], h_ref[g].astype(BF16), _TN, preferred_element_type=F32)
        dincs = [_shift_rows(_chunk_scan(dhs[g], a_ref[g], True), True) for g in range(GB)]
        dus = []
        for g in range(GB):
            dinc, hx_v = dincs[g], h_ref[g]
            dinc_b = dinc.astype(BF16)
            dus.append(intra[g] + lax.dot_general(dinc_b, s_ref[g], _NT, preferred_element_type=F32))
            gs_ref[g] = lax.dot_general(ua[g], dinc_b, _TN, preferred_element_type=F32)
            p1 = dinc * hx_v
            p2 = pltpu.roll(dinc, SSM_P, 1) * hx_v
            t1 = jnp.sum(p1 + pltpu.roll(p1, SSM_P, 1), axis=0, keepdims=True)
            t2 = jnp.sum(p2 - pltpu.roll(p2, SSM_P, 1), axis=0, keepdims=True)
            ga_ref[g] = jnp.where(lane < SSM_P, t1, pltpu.roll(t2, SSM_P, 1))
        dup = _chunks_to_rows(dus, perm, False)
        for t, r in enumerate(rows):
            du_ref[r, :] = dup[t] + d_ref[...] * dys[t]

    g3 = lambda r, c: pl.BlockSpec((GB, r, c), lambda g: (g, 0, 0))
    col = pl.BlockSpec((L, 128), lambda g: (0, g))
    row = pl.BlockSpec((1, 128), lambda g: (0, g))
    return _call(
        body, "ssm_bwd", (SSM_G // GB,),
        [col, col, pl.BlockSpec((L, 128), lambda g: (0, U_COL0 + g)), g3(nc, 2 * SSM_P), _ANY,
         g3(CW, CW), g3(CW, 2 * SSM_P), g3(CW, 2 * SSM_P), g3(1, 2 * SSM_P), row],
        [col, g3(CW, CW), g3(CW, 2 * SSM_P), g3(CW, 2 * SSM_P), g3(1, 2 * SSM_P), row],
        [jax.ShapeDtypeStruct((L, SSM_W), F32), jax.ShapeDtypeStruct((SSM_G, CW, CW), F32),
         jax.ShapeDtypeStruct((SSM_G, CW, 2 * SSM_P), F32), jax.ShapeDtypeStruct((SSM_G, CW, 2 * SSM_P), F32),
         jax.ShapeDtypeStruct((SSM_G, 1, 2 * SSM_P), F32), jax.ShapeDtypeStruct((1, SSM_W), F32)],
        (d_yg, y, proj, hx, perm, mt, scat, ocat, a16, d_skip.reshape(1, SSM_W)),
        [pltpu.VMEM((HALF * 128, GB * 128), BF16), pltpu.SemaphoreType.DMA], ride)


def _merge(og, yg, w_glu, proj, b_glu, wa, ws):
    L = og.shape[0]
    tm = _tile(L, 256)

    def body(og_ref, yg_ref, wg_ref, z0_ref, z1_ref, b_ref, wa_ref, ws_ref, m_ref, gp_ref):
        zs = jnp.concatenate([z0_ref[...], z1_ref[...]], axis=1)
        ygv = yg_ref[...]
        gpre = jnp.dot(ygv.astype(BF16), wg_ref[...], preferred_element_type=F32)
        gp_ref[...] = gpre
        os_ = ygv * _sigmoid(gpre + b_ref[...]) * _silu(zs)
        ogv = og_ref[...].astype(F32)
        ra = lax.rsqrt(jnp.mean(ogv * ogv, axis=-1, keepdims=True) + NORM_EPS)
        rs = lax.rsqrt(jnp.mean(os_ * os_, axis=-1, keepdims=True) + NORM_EPS)
        m_ref[:, :ATTN_W] = (ogv * ra * wa_ref[...]).astype(BF16)
        m_ref[:, ATTN_W:] = (os_ * rs * ws_ref[...]).astype(BF16)

    row = lambda w: pl.BlockSpec((1, w), lambda i: (0, 0))
    return pl.pallas_call(
        body,
        name="merge",
        grid=(L // tm,),
        in_specs=[pl.BlockSpec((tm, ATTN_W), lambda i: (i, 0)), pl.BlockSpec((tm, SSM_W), lambda i: (i, 0)),
                  pl.BlockSpec((SSM_W, SSM_W), lambda i: (0, 0)),
                  pl.BlockSpec((tm, 512), lambda i: (i, 7)), pl.BlockSpec((tm, 512), lambda i: (i, 8)),
                  row(SSM_W), row(ATTN_W), row(SSM_W)],
        out_specs=[pl.BlockSpec((tm, D_MODEL), lambda i: (i, 0)), pl.BlockSpec((tm, SSM_W), lambda i: (i, 0))],
        out_shape=[jax.ShapeDtypeStruct((L, D_MODEL), BF16), jax.ShapeDtypeStruct((L, SSM_W), F32)],
        compiler_params=_cp(("parallel",)),
    )(og, yg, w_glu, proj, proj, b_glu.reshape(1, SSM_W), wa.reshape(1, ATTN_W), ws.reshape(1, SSM_W))


def _outproj_loss(merged, w_out, x, target):
    L = x.shape[0]
    tm, tn = _tile(L, 256), D_MODEL
    ni, nj = L // tm, D_MODEL // tn

    def body(m_ref, w_ref, x_ref, t_ref, d_ref, db_ref, l_ref):
        out = x_ref[...] + jnp.dot(m_ref[...], w_ref[...], preferred_element_type=F32)
        diff = out - t_ref[...]
        d = diff * (1.0 / D_MODEL)
        d_ref[...] = d
        db_ref[...] = d.astype(BF16)
        l_ref[...] = jnp.full((1, 8, 128), jnp.sum(diff * diff), F32)

    return pl.pallas_call(
        body,
        name="outproj_loss",
        grid=(nj, ni),
        in_specs=[pl.BlockSpec((tm, D_MODEL), lambda j, i: (i, 0)),
                  pl.BlockSpec((D_MODEL, tn), lambda j, i: (0, j)),
                  pl.BlockSpec((tm, tn), lambda j, i: (i, j)),
                  pl.BlockSpec((tm, tn), lambda j, i: (i, j))],
        out_specs=[pl.BlockSpec((tm, tn), lambda j, i: (i, j)), pl.BlockSpec((tm, tn), lambda j, i: (i, j)),
                   pl.BlockSpec((1, 8, 128), lambda j, i: (i * nj + j, 0, 0))],
        out_shape=[jax.ShapeDtypeStruct((L, D_MODEL), F32), jax.ShapeDtypeStruct((L, D_MODEL), BF16),
                   jax.ShapeDtypeStruct((ni * nj, 8, 128), F32)],
        compiler_params=_cp(("parallel", "parallel")),
    )(merged, w_out, x, target)


def _merge_bwd(d_out_b, w_out, w_glu, og, o, yg, gpre, proj, b_glu, wa, ws):
    L = og.shape[0]
    tm = _tile(L, 256)

    def body(dout_ref, wo_ref, wg_ref, og_ref, o_ref, yg_ref, gp_ref, za0_ref, za1_ref, zs0_ref, zs1_ref, b_ref,
             wa_ref, ws_ref, do_ref, dza_ref, dzs_ref, dg_ref, dyg_ref, gwa_ref, gws_ref, gb_ref):
        i = pl.program_id(0)

        @pl.when(i == 0)
        def _():
            gwa_ref[...] = jnp.zeros_like(gwa_ref)
            gws_ref[...] = jnp.zeros_like(gws_ref)
            gb_ref[...] = jnp.zeros_like(gb_ref)

        dm = lax.dot_general(dout_ref[...], wo_ref[...], _NT, preferred_element_type=F32)
        za = jnp.concatenate([za0_ref[...], za1_ref[...]], axis=1)
        zs = jnp.concatenate([zs0_ref[...], zs1_ref[...]], axis=1)
        ogv, dma = og_ref[...].astype(F32), dm[:, :ATTN_W]
        ra = lax.rsqrt(jnp.mean(ogv * ogv, axis=-1, keepdims=True) + NORM_EPS)
        xh = ogv * ra
        gwa_ref[...] += jnp.sum(dma * xh, axis=0, keepdims=True)
        gx = dma * wa_ref[...]
        d_og = ra * (gx - xh * jnp.mean(gx * xh, axis=-1, keepdims=True))
        do_ref[...] = (d_og * _silu(za)).astype(BF16)
        dza_ref[...] = (d_og * o_ref[...].astype(F32) * _dsilu(za)).astype(BF16)
        ygv = yg_ref[...]
        sg = _sigmoid(gp_ref[...] + b_ref[...])
        y2 = ygv * sg
        sz = _silu(zs)
        os_ = y2 * sz
        dms = dm[:, ATTN_W:]
        rs = lax.rsqrt(jnp.mean(os_ * os_, axis=-1, keepdims=True) + NORM_EPS)
        xs = os_ * rs
        gws_ref[...] += jnp.sum(dms * xs, axis=0, keepdims=True)
        gxs = dms * ws_ref[...]
        d_os = rs * (gxs - xs * jnp.mean(gxs * xs, axis=-1, keepdims=True))
        dzs_ref[...] = (d_os * y2 * _dsilu(zs)).astype(BF16)
        d_y2 = d_os * sz
        d_g = d_y2 * ygv * sg * (1.0 - sg)
        d_g_b = d_g.astype(BF16)
        dg_ref[...] = d_g_b
        gb_ref[...] += jnp.sum(d_g, axis=0, keepdims=True)
        dyg_ref[...] = d_y2 * sg + lax.dot_general(d_g_b, wg_ref[...], _NT, preferred_element_type=F32)

    row = lambda w: pl.BlockSpec((1, w), lambda i: (0, 0))
    full = lambda w: pl.BlockSpec((tm, w), lambda i: (i, 0))
    half = lambda c: pl.BlockSpec((tm, 512), lambda i: (i, c))
    return pl.pallas_call(
        body,
        name="merge_bwd",
        grid=(L // tm,),
        in_specs=[full(D_MODEL), pl.BlockSpec((D_MODEL, D_MODEL), lambda i: (0, 0)),
                  pl.BlockSpec((SSM_W, SSM_W), lambda i: (0, 0)),
                  full(ATTN_W), full(ATTN_W), full(SSM_W), full(SSM_W),
                  half(3), half(4), half(7), half(8), row(SSM_W), row(ATTN_W), row(SSM_W)],
        out_specs=[full(ATTN_W), full(ATTN_W), full(SSM_W), full(SSM_W), full(SSM_W),
                   row(ATTN_W), row(SSM_W), row(SSM_W)],
        out_shape=[jax.ShapeDtypeStruct((L, ATTN_W), BF16), jax.ShapeDtypeStruct((L, ATTN_W), BF16),
                   jax.ShapeDtypeStruct((L, SSM_W), BF16), jax.ShapeDtypeStruct((L, SSM_W), BF16),
                   jax.ShapeDtypeStruct((L, SSM_W), F32),
                   jax.ShapeDtypeStruct((1, ATTN_W), F32), jax.ShapeDtypeStruct((1, SSM_W), F32),
                   jax.ShapeDtypeStruct((1, SSM_W), F32)],
        compiler_params=_cp(("arbitrary",)),
    )(d_out_b, w_out, w_glu, og, o, yg, gpre, proj, proj, proj, proj, b_glu.reshape(1, SSM_W), wa.reshape(1, ATTN_W),
      ws.reshape(1, SSM_W))


def _rms_bwd_x(x, norm_w, d_proj, wt_in, d_out, ride):
    L = x.shape[0]
    tm = _tile(L, 256)

    def body(x_ref, w_ref, dp_ref, wt_ref, do_ref, gx_ref, gw_ref):
        i = pl.program_id(0)

        @pl.when(i == 0)
        def _():
            gw_ref[...] = jnp.zeros_like(gw_ref)

        xv = x_ref[...]
        dh = jnp.dot(dp_ref[...], wt_ref[...], preferred_element_type=F32)
        r = lax.rsqrt(jnp.mean(xv * xv, axis=-1, keepdims=True) + NORM_EPS)
        xh = xv * r
        gw_ref[...] += jnp.sum(dh * xh, axis=0, keepdims=True)
        gx = dh * w_ref[...]
        gx_ref[...] = do_ref[...] + r * (gx - xh * jnp.mean(gx * xh, axis=-1, keepdims=True))

    blk = pl.BlockSpec((tm, D_MODEL), lambda i: (i, 0))
    row = pl.BlockSpec((1, D_MODEL), lambda i: (0, 0))
    dp_blk = pl.BlockSpec((tm, IN_W), lambda i: (i, 0))
    wt_blk = pl.BlockSpec((IN_W, D_MODEL), lambda i: (0, 0), pipeline_mode=pl.Buffered(1))
    return _call(body, "d_hn_rms_bwd_x", (L // tm,), [blk, row, dp_blk, wt_blk, blk], [blk, row],
                 [jax.ShapeDtypeStruct((L, D_MODEL), F32), jax.ShapeDtypeStruct((1, D_MODEL), F32)],
                 (x, norm_w.reshape(1, D_MODEL), d_proj, wt_in, d_out), ride=ride)


def _rope_table(positions):
    inv_freq = ROPE_THETA ** (-jnp.arange(0, HEAD_DIM, 2, dtype=F32) / HEAD_DIM)
    ang = positions.astype(F32)[:, None] * inv_freq
    sign = jnp.where(jnp.arange(128) % HEAD_DIM < HEAD_DIM // 2, -1.0, 1.0)
    return jnp.concatenate([jnp.tile(jnp.cos(ang), (1, 4)), jnp.tile(jnp.sin(ang), (1, 4)) * sign], axis=1)


def _step(x, positions, target, w, core, chip):
    small = {n: w[n] for n in _SMALL}
    tab = _rope_table(positions)
    mt_b, scat_b, ocat_b, a16 = _ssm_prep(small)
    perm = _chunk_perm()
    blocks = lambda t: t.reshape(N_DEV, t.shape[0] // N_DEV, t.shape[1])

    proj, hn, wt_in = _rms_inproj_gather(x, small["norm_w"], w["w_in"].T.astype(BF16), chip)
    wt_in = wt_in.reshape(IN_W, D_MODEL)
    q_rot, k_rot = _qk_prep(proj, tab, small["q_norm_w"], small["k_norm_w"])
    (og, o, lse), (w_glu,) = _attn_fwd(q_rot, k_rot, proj, small["sinks"],
                                       _gather_exchange([w["w_glu"].astype(BF16)]))
    (y, yg, hx), (w_out,) = _ssm_fwd(proj, perm, mt_b, scat_b, ocat_b, a16, small["d_skip"],
                                     _gather_exchange([w["w_out"].astype(BF16)]))
    w_glu, w_out = w_glu.reshape(SSM_W, SSM_W), w_out.reshape(D_MODEL, D_MODEL)
    merged, gpre = _merge(og, yg, w_glu, proj, small["b_glu"], small["attn_out_norm_w"], small["ssm_out_norm_w"])
    d_out, d_out_b, loss_parts = _outproj_loss(merged, w_out, x, target)
    loss = 0.5 * jnp.sum(loss_parts[:, 0, 0]) / D_MODEL

    g_w_out = blocks(_mm_tn(merged, d_out_b, "grad_w_out", tm=1024))
    d_o, d_za, d_zs, d_g, d_yg, g_wa, g_ws, g_bglu = _merge_bwd(
        d_out_b, w_out, w_glu, og, o, yg, gpre, proj, small["b_glu"], small["attn_out_norm_w"],
        small["ssm_out_norm_w"])
    g_w_glu = blocks(_mm_tn(yg, d_g, "grad_w_glu", tm=256))
    (d_u, g_mt, g_scat, g_ocat, g_a16, g_dskip), (ra_out, ra_glu) = _ssm_bwd(
        d_yg, y, proj, hx, perm, mt_b, scat_b, ocat_b, a16, small["d_skip"], _pair_exchange([g_w_out, g_w_glu]))
    p_out = _pair_sum(g_w_out, ra_out, core, BF16, "pair_sum_out")
    p_glu = _pair_sum(g_w_glu, ra_glu, core, BF16, "pair_sum_glu")
    (d_q, d_k, d_v, g_sinks), (rb_out, rb_glu) = _attn_bwd(
        q_rot, k_rot, proj, small["sinks"], d_o, o, lse, _chip_exchange([p_out, p_glu]))
    d_proj, g_qw, g_kw = _qk_prep_bwd(proj, tab, small["q_norm_w"], small["k_norm_w"], d_q, d_k, d_v,
                                      d_za, d_u, d_zs)
    g_qw = g_qw[0, :HEAD_DIM] + g_qw[0, HEAD_DIM:]
    g_kw = g_kw[0, :HEAD_DIM] + g_kw[0, HEAD_DIM:]
    g_in_a = blocks(_mm_tn(d_proj, hn, "grad_w_in_a", tm=1152, panel=0))
    g_in_b, (ra_a,) = _mm_tn(d_proj, hn, "grad_w_in_b", tm=1152, panel=1, ride=_pair_exchange([g_in_a]))
    g_in_b = blocks(g_in_b)
    p_a = _pair_sum(g_in_a, ra_a, core, BF16, "pair_sum_in_a")
    (grad_x, g_nw), (rb_a, ra_b) = _rms_bwd_x(x, small["norm_w"], d_proj, wt_in, d_out,
                                              _both(_chip_exchange([p_a]), _pair_exchange([g_in_b])))
    p_b = _pair_sum(g_in_b, ra_b, core, BF16, "pair_sum_in_b")
    g_small, (rb_b,) = _ssm_prep_bwd(small, g_mt, g_scat, g_ocat, g_a16, _chip_exchange([p_b]))

    g_small.update(norm_w=g_nw.reshape(-1), q_norm_w=g_qw.reshape(-1), k_norm_w=g_kw.reshape(-1),
                   sinks=g_sinks[0, :N_HEADS], d_skip=g_dskip.reshape(-1), b_glu=g_bglu.reshape(-1),
                   attn_out_norm_w=g_wa.reshape(-1), ssm_out_norm_w=g_ws.reshape(-1))
    g_packed = _slab_all_reduce(_pack(g_small, loss).reshape(N_DEV, _PACK_ROWS // N_DEV, 128))
    g_packed = g_packed.reshape(_PACK_ROWS, 128)
    grads = _unpack(g_packed, w)
    parts = dict(w_in=([p_a, p_b], [rb_a, rb_b]), w_glu=([p_glu], [rb_glu]), w_out=([p_out], [rb_out]))
    return g_packed[_LOSS_ROW, 0], grad_x, grads, parts


_ANY = pl.BlockSpec(memory_space=pl.ANY)


class _Exchange:
    def __init__(self, arrays, out_shape, sems, start, finish, relay=None):
        self.arrays, self.out_shape, self.sems, self.start, self.finish = arrays, out_shape, sems, start, finish
        self.relay = relay if relay is not None else (lambda ins, outs, sems: None)


def _gather_exchange(blocks):
    n = len(blocks)

    def parts(ins, outs, sems):
        send_sems, recv_sems, local_sems = sems
        x, y, c = lax.axis_index("x"), lax.axis_index("y"), lax.axis_index("c")
        me, sibling = (x, y, c), (x, y, 1 - c)
        chips = [(1 - x, y), (x, 1 - y), (1 - x, 1 - y)]

        def slot(k, dev):
            return outs[k].at[4 * dev[0] + 2 * dev[1] + dev[2]]

        def copy(k, q, block, to, src=None):
            return pltpu.make_async_remote_copy(
                src_ref=slot(k, block) if src is None else src, dst_ref=slot(k, block),
                send_sem=send_sems.at[k, q], recv_sem=recv_sems.at[k, q], device_id=to, device_id_type=MESH)

        mine = [pltpu.make_async_copy(ins[k], slot(k, me), local_sems.at[k]) for k in range(n)]
        first = []
        for k in range(n):
            first.append(copy(k, 0, me, sibling, src=ins[k]))
            first += [copy(k, 1 + j, me, (*chip, c), src=ins[k]) for j, chip in enumerate(chips)]
        return me, sibling, chips, c, copy, mine, first

    def start(ins, outs, sems):
        *_, mine, first = parts(ins, outs, sems)
        for cp in mine + first:
            cp.start()

    def relay(ins, outs, sems):
        me, sibling, chips, c, copy, _, _ = parts(ins, outs, sems)
        for j, chip in enumerate(chips):
            for k in range(n):
                copy(k, 1 + j, (*chip, c), me).wait_recv()
                copy(k, 4 + j, (*chip, c), sibling).start()

    def finish(ins, outs, sems):
        me, sibling, chips, c, copy, mine, first = parts(ins, outs, sems)
        for k in range(n):
            copy(k, 0, sibling, me).wait_recv()
            for j, chip in enumerate(chips):
                copy(k, 4 + j, (*chip, 1 - c), me).wait_recv()
        for cp in first + [copy(k, 4 + j, (*chip, c), sibling) for k in range(n) for j, chip in enumerate(chips)]:
            cp.wait_send()
        for cp in mine:
            cp.wait()

    return _Exchange(blocks, [jax.ShapeDtypeStruct((N_DEV,) + b.shape, b.dtype) for b in blocks],
                     [pltpu.SemaphoreType.DMA((n, 7)), pltpu.SemaphoreType.DMA((n, 7)), pltpu.SemaphoreType.DMA((n,))],
                     start, finish, relay)


def _direct_exchange(arrays, out_lead, fan, route):
    n = len(arrays)

    def copies(ins, outs, sems):
        send_sems, recv_sems = sems
        legs = route(lax.axis_index("x"), lax.axis_index("y"), lax.axis_index("c"))
        return [pltpu.make_async_remote_copy(
            src_ref=ins[k].at[src], dst_ref=outs[k].at[q], send_sem=send_sems.at[k, q], recv_sem=recv_sems.at[k, q],
            device_id=to, device_id_type=MESH) for k in range(n) for src, q, to in legs]

    def start(ins, outs, sems):
        for cp in copies(ins, outs, sems):
            cp.start()

    def finish(ins, outs, sems):
        for cp in copies(ins, outs, sems):
            cp.wait()

    return _Exchange(arrays, [jax.ShapeDtypeStruct((out_lead,) + a.shape[1:], a.dtype) for a in arrays],
                     [pltpu.SemaphoreType.DMA((n, fan)), pltpu.SemaphoreType.DMA((n, fan))], start, finish)


def _pair_exchange(grads):
    return _direct_exchange(grads, 4, 4, lambda x, y, c: [(2 * chip + (1 - c), chip, (x, y, 1 - c))
                                                          for chip in range(4)])


def _chip_exchange(parts):
    def route(x, y, c):
        chips = [(1 - x, y), (x, 1 - y), (1 - x, 1 - y)]
        return [(2 * chip[0] + chip[1], q, (*chip, c)) for q, chip in enumerate(chips)]
    return _direct_exchange(parts, 3, 3, route)


def _both(ex1, ex2):
    n1, s1 = len(ex1.arrays), len(ex1.sems)

    def halves(ins, outs, sems):
        return (ins[:n1], outs[:n1], sems[:s1]), (ins[n1:], outs[n1:], sems[s1:])

    def start(ins, outs, sems):
        h1, h2 = halves(ins, outs, sems)
        ex1.start(*h1)
        ex2.start(*h2)

    def relay(ins, outs, sems):
        h1, h2 = halves(ins, outs, sems)
        ex1.relay(*h1)
        ex2.relay(*h2)

    def finish(ins, outs, sems):
        h1, h2 = halves(ins, outs, sems)
        ex1.finish(*h1)
        ex2.finish(*h2)

    return _Exchange(list(ex1.arrays) + list(ex2.arrays), list(ex1.out_shape) + list(ex2.out_shape),
                     list(ex1.sems) + list(ex2.sems), start, finish, relay)


def _call(body, name, grid, in_specs, out_specs, out_shape, args, scratch_shapes=(), ride=None):
    if ride is None:
        sem = ("arbitrary",) * len(grid)
        return pl.pallas_call(body, name=name, grid=grid, in_specs=in_specs, out_specs=out_specs, out_shape=out_shape,
                              scratch_shapes=list(scratch_shapes), compiler_params=_cp(sem))(*args), None
    n_in, n_out, n_scr, n_x = len(in_specs), len(out_specs), len(scratch_shapes), len(ride.arrays)

    def wrapped(*refs):
        ins, refs = refs[:n_in], refs[n_in:]
        x_in, refs = refs[:n_x], refs[n_x:]
        outs, refs = refs[:n_out], refs[n_out:]
        x_out, refs = refs[:n_x], refs[n_x:]
        scr, sems = refs[:n_scr], refs[n_scr:]
        step, total = pl.program_id(0), grid[0]
        for a in range(1, len(grid)):
            step, total = step * grid[a] + pl.program_id(a), total * grid[a]
        @pl.when(step == 0)
        def _():
            ride.start(x_in, x_out, sems)

        @pl.when(step == max(total - 2, 0))
        def _():
            ride.relay(x_in, x_out, sems)

        body(*ins, *outs, *scr)

        @pl.when(step == total - 1)
        def _():
            ride.finish(x_in, x_out, sems)

    res = pl.pallas_call(
        wrapped, name=name, grid=grid, in_specs=list(in_specs) + [_ANY] * n_x,
        out_specs=list(out_specs) + [_ANY] * n_x, out_shape=list(out_shape) + list(ride.out_shape),
        scratch_shapes=list(scratch_shapes) + list(ride.sems),
        compiler_params=_cp(("arbitrary",) * len(grid)))(*args, *ride.arrays)
    return res[:n_out], list(res[n_out:])


def _pair_sum(g, ra, core, out_dtype, name):
    _, r, C = g.shape
    tr = _tile(r, 576)

    def body(c_ref, g_ref, ra_ref, p_ref):
        p_ref[...] = (g_ref[...] + ra_ref[...]).astype(p_ref.dtype)

    return pl.pallas_call(
        body,
        name=name,
        grid_spec=pltpu.PrefetchScalarGridSpec(
            num_scalar_prefetch=1,
            grid=(4, r // tr),
            in_specs=[pl.BlockSpec((1, tr, C), lambda j, t, c_ref: (2 * j + c_ref[0], t, 0)),
                      pl.BlockSpec((1, tr, C), lambda j, t, c_ref: (j, t, 0))],
            out_specs=pl.BlockSpec((1, tr, C), lambda j, t, c_ref: (j, t, 0)),
        ),
        out_shape=jax.ShapeDtypeStruct((4, r, C), out_dtype),
        compiler_params=_cp(("parallel", "parallel")),
    )(core, g, ra)


def _slab_all_reduce(slab):
    _, r, lanes = slab.shape

    def body(s_ref, o_ref, ra, rb, ps, sems_a, sems_b, sems_c):
        x, y, c = lax.axis_index("x"), lax.axis_index("y"), lax.axis_index("c")
        chips = [(1 - x, y), (x, 1 - y), (1 - x, 1 - y)]
        pair = [pltpu.make_async_remote_copy(
            src_ref=s_ref.at[2 * k + (1 - c)], dst_ref=ra.at[k], send_sem=sems_a.at[0, k], recv_sem=sems_a.at[1, k],
            device_id=(x, y, 1 - c), device_id_type=MESH) for k in range(4)]
        for cp in pair:
            cp.start()
        for cp in pair:
            cp.wait()
        for k in range(4):
            ps[k] = s_ref[2 * k + c] + ra[k]
        cross = [pltpu.make_async_remote_copy(
            src_ref=ps.at[2 * ch[0] + ch[1]], dst_ref=rb.at[q], send_sem=sems_b.at[0, q], recv_sem=sems_b.at[1, q],
            device_id=(*ch, c), device_id_type=MESH) for q, ch in enumerate(chips)]
        for cp in cross:
            cp.start()
        for cp in cross:
            cp.wait()
        me = 4 * x + 2 * y + c
        o_ref[me] = ((ps[2 * x + y] + rb[0]) + rb[1]) + rb[2]
        flips = [(dx, dy, dc) for dx in (0, 1) for dy in (0, 1) for dc in (0, 1) if dx + dy + dc]
        spread = [pltpu.make_async_remote_copy(
            src_ref=o_ref.at[me], dst_ref=o_ref.at[me], send_sem=sems_c.at[0, q], recv_sem=sems_c.at[1, q],
            device_id=(x + dx - 2 * x * dx, y + dy - 2 * y * dy, c + dc - 2 * c * dc), device_id_type=MESH)
            for q, (dx, dy, dc) in enumerate(flips)]
        for cp in spread:
            cp.start()
        for q, (dx, dy, dc) in enumerate(flips):
            peer = 4 * (x + dx - 2 * x * dx) + 2 * (y + dy - 2 * y * dy) + (c + dc - 2 * c * dc)
            pltpu.make_async_remote_copy(
                src_ref=o_ref.at[peer], dst_ref=o_ref.at[peer], send_sem=sems_c.at[0, q], recv_sem=sems_c.at[1, q],
                device_id=(x, y, c), device_id_type=MESH).wait_recv()
        for cp in spread:
            cp.wait_send()

    whole = pl.BlockSpec(memory_space=pltpu.VMEM)
    return pl.pallas_call(
        body, name="slab_all_reduce", in_specs=[whole], out_specs=whole,
        out_shape=jax.ShapeDtypeStruct(slab.shape, F32),
        scratch_shapes=[pltpu.VMEM((4, r, lanes), F32), pltpu.VMEM((3, r, lanes), F32), pltpu.VMEM((4, r, lanes), F32),
                        pltpu.SemaphoreType.DMA((2, 4)), pltpu.SemaphoreType.DMA((2, 3)),
                        pltpu.SemaphoreType.DMA((2, 7))],
        compiler_params=_cp(),
    )(slab)


def _adamw_reduced(ps, rbs, chip, w, m, v, name):
    nh = len(ps)
    R, C = w.shape
    ch = C // nh
    tr = _tile(R, 288)
    nt = R // tr
    c1 = 1.0 - ADAM_B1 ** ADAM_STEP
    c2 = 1.0 - ADAM_B2 ** ADAM_STEP

    def body(c_ref, *refs):
        p_refs, rb_refs = refs[:nh], refs[nh:2 * nh]
        w_ref, m_ref, v_ref, g_ref, d_ref, nm_ref, nv_ref = refs[2 * nh:]
        for h in range(nh):
            @pl.when(pl.program_id(0) == h)
            def _(h=h):
                rb = rb_refs[h]
                gv = p_refs[h][0].astype(F32) + rb[0].astype(F32)
                gv = gv + rb[1].astype(F32)
                gv = gv + rb[2].astype(F32)
                nm = ADAM_B1 * m_ref[...] + (1.0 - ADAM_B1) * gv
                nv = ADAM_B2 * v_ref[...] + (1.0 - ADAM_B2) * (gv * gv)
                g_ref[...] = gv
                nm_ref[...] = nm
                nv_ref[...] = nv
                d_ref[...] = -ADAM_LR * ((nm / c1) / (jnp.sqrt(nv / c2) + ADAM_EPS) + ADAM_WD * w_ref[...])

    def held(h):
        return lambda hh, tt: jnp.where(hh == h, tt, jnp.where(hh < h, 0, nt - 1))

    p_specs = [pl.BlockSpec((1, tr, ch), lambda hh, tt, c_ref, f=held(h): (c_ref[0], f(hh, tt), 0))
               for h in range(nh)]
    rb_specs = [pl.BlockSpec((3, tr, ch), lambda hh, tt, c_ref, f=held(h): (0, f(hh, tt), 0)) for h in range(nh)]
    blk = pl.BlockSpec((tr, ch), lambda hh, tt, c_ref: (tt, hh))
    return pl.pallas_call(
        body,
        name=name,
        grid_spec=pltpu.PrefetchScalarGridSpec(
            num_scalar_prefetch=1, grid=(nh, nt), in_specs=p_specs + rb_specs + [blk] * 3, out_specs=[blk] * 4),
        out_shape=[jax.ShapeDtypeStruct((R, C), F32)] * 4,
        compiler_params=_cp(("arbitrary", "arbitrary")),
    )(chip, *ps, *rbs, w, m, v)


_SMALL = ("norm_w", "q_norm_w", "k_norm_w", "sinks", "a_re", "a_im", "log_step", "b_re", "b_im", "c_re", "c_im",
          "d_skip", "b_glu", "attn_out_norm_w", "ssm_out_norm_w")
_WEIGHTS = ("norm_w", "w_in", "q_norm_w", "k_norm_w", "sinks", "a_re", "a_im", "log_step", "b_re", "b_im", "c_re",
            "c_im", "d_skip", "w_glu", "b_glu", "attn_out_norm_w", "ssm_out_norm_w", "w_out")
_SMALL_2D = dict(norm_w=(1, 2048), q_norm_w=(1, 64), k_norm_w=(1, 64), sinks=(1, 16), a_re=(64, 64), a_im=(64, 64),
                 log_step=(1, 64), b_re=(1024, 64), b_im=(1024, 64), c_re=(1024, 64), c_im=(1024, 64),
                 d_skip=(1, 1024), b_glu=(1, 1024), attn_out_norm_w=(1, 1024), ssm_out_norm_w=(1, 1024))
_P_MINOR = ("b_re", "b_im")


def _flat_form(n, t):
    return t.transpose(0, 2, 1) if n in _P_MINOR else t


def _own_form(n, t, shape):
    if n in _P_MINOR:
        return t.reshape(shape[0], shape[2], shape[1]).transpose(0, 2, 1)
    return t.reshape(shape)


def _slab_rows(n):
    return -(-n // 1024) * 8


_PACK_ROWS = 2304


_LOSS_ROW = 2192


def _pack(d, loss):
    parts = []
    for n in _SMALL:
        flat = _flat_form(n, d[n]).reshape(-1).astype(F32)
        rows = _slab_rows(flat.shape[0])
        parts.append(jnp.pad(flat, (0, rows * 128 - flat.shape[0])).reshape(rows, 128))
    assert sum(p.shape[0] for p in parts) == _LOSS_ROW
    parts.append(jnp.pad(loss.reshape(1, 1), ((0, _PACK_ROWS - _LOSS_ROW - 1), (0, 127))))
    return jnp.concatenate(parts, axis=0)


def _unpack(packed, like):
    out, off = {}, 0
    for n in _SMALL:
        size = math.prod(like[n].shape)
        rows = _slab_rows(size)
        out[n] = _own_form(n, packed[off:off + rows].reshape(-1)[:size], like[n].shape)
        off += rows
    return out


def _adamw_small(g, w, m, v):
    c1 = 1.0 - ADAM_B1 ** ADAM_STEP
    c2 = 1.0 - ADAM_B2 ** ADAM_STEP
    k = len(_SMALL)

    def body(*refs):
        ins, outs = refs[:4 * k], refs[4 * k:]
        for j in range(k):
            gv, wv, mv, vv = (ins[q * k + j][...] for q in range(4))
            nm = ADAM_B1 * mv + (1.0 - ADAM_B1) * gv
            nv = ADAM_B2 * vv + (1.0 - ADAM_B2) * (gv * gv)
            outs[j][...] = -ADAM_LR * ((nm / c1) / (jnp.sqrt(nv / c2) + ADAM_EPS) + ADAM_WD * wv)
            outs[k + j][...] = nm
            outs[2 * k + j][...] = nv

    args = [_flat_form(n, d[n]).reshape(_SMALL_2D[n]) for d in (g, w, m, v) for n in _SMALL]
    shapes = [jax.ShapeDtypeStruct(_SMALL_2D[n], F32) for _ in range(3) for n in _SMALL]
    outs = pl.pallas_call(body, name="adamw_small", out_shape=shapes, compiler_params=_cp())(*args)
    res = []
    for q in range(3):
        res.append({n: _own_form(n, outs[q * k + j], w[n].shape) for j, n in enumerate(_SMALL)})
    return res


def kernel(x, positions, norm_w, w_in, q_norm_w, k_norm_w, sinks, a_re, a_im, log_step, b_re, b_im, c_re, c_im, d_skip, w_glu, b_glu, attn_out_norm_w, ssm_out_norm_w, w_out, loss_target, m_norm_w, m_w_in, m_q_norm_w, m_k_norm_w, m_sinks, m_a_re, m_a_im, m_log_step, m_b_re, m_b_im, m_c_re, m_c_im, m_d_skip, m_w_glu, m_b_glu, m_attn_out_norm_w, m_ssm_out_norm_w, m_w_out, v_norm_w, v_w_in, v_q_norm_w, v_k_norm_w, v_sinks, v_a_re, v_a_im, v_log_step, v_b_re, v_b_im, v_c_re, v_c_im, v_d_skip, v_w_glu, v_b_glu, v_attn_out_norm_w, v_ssm_out_norm_w, v_w_out):
    w = dict(norm_w=norm_w, w_in=w_in, q_norm_w=q_norm_w, k_norm_w=k_norm_w, sinks=sinks, a_re=a_re, a_im=a_im,
             log_step=log_step, b_re=b_re, b_im=b_im, c_re=c_re, c_im=c_im, d_skip=d_skip, w_glu=w_glu, b_glu=b_glu,
             attn_out_norm_w=attn_out_norm_w, ssm_out_norm_w=ssm_out_norm_w, w_out=w_out)
    m = dict(norm_w=m_norm_w, w_in=m_w_in, q_norm_w=m_q_norm_w, k_norm_w=m_k_norm_w, sinks=m_sinks, a_re=m_a_re,
             a_im=m_a_im, log_step=m_log_step, b_re=m_b_re, b_im=m_b_im, c_re=m_c_re, c_im=m_c_im, d_skip=m_d_skip,
             w_glu=m_w_glu, b_glu=m_b_glu, attn_out_norm_w=m_attn_out_norm_w, ssm_out_norm_w=m_ssm_out_norm_w,
             w_out=m_w_out)
    v = dict(norm_w=v_norm_w, w_in=v_w_in, q_norm_w=v_q_norm_w, k_norm_w=v_k_norm_w, sinks=v_sinks, a_re=v_a_re,
             a_im=v_a_im, log_step=v_log_step, b_re=v_b_re, b_im=v_b_im, c_re=v_c_re, c_im=v_c_im, d_skip=v_d_skip,
             w_glu=v_w_glu, b_glu=v_b_glu, attn_out_norm_w=v_attn_out_norm_w, ssm_out_norm_w=v_ssm_out_norm_w,
             w_out=v_w_out)
    core = lax.axis_index("c").astype(jnp.int32).reshape(1)
    chip = (2 * lax.axis_index("x") + lax.axis_index("y")).astype(jnp.int32).reshape(1)

    loss, grad_x, grads, parts = _step(x[0], positions[0], loss_target[0], w, core, chip)
    delta, new_m, new_v = {}, {}, {}
    for n in ("w_glu", "w_out"):
        grads[n], delta[n], new_m[n], new_v[n] = _adamw_reduced(*parts[n], chip, w[n], m[n], v[n], f"adamw_{n}")
    g_t, d_t, m_t, v_t = _adamw_reduced(*parts["w_in"], chip, w["w_in"].T, m["w_in"].T, v["w_in"].T, "adamw_w_in")
    grads["w_in"], delta["w_in"], new_m["w_in"], new_v["w_in"] = g_t.T, d_t.T, m_t.T, v_t.T
    d_s, m_s, v_s = _adamw_small(grads, w, m, v)
    delta.update(d_s)
    new_m.update(m_s)
    new_v.update(v_s)

    return (loss, grad_x[None], *[grads[n] for n in _WEIGHTS], *[delta[n] for n in _WEIGHTS],
            *[new_m[n] for n in _WEIGHTS], *[new_v[n] for n in _WEIGHTS])
```

```python
import math

import jax
import jax.numpy as jnp
from jax import lax
from jax.experimental import pallas as pl
from jax.experimental.pallas import tpu as pltpu

F32 = jnp.float32
BF16 = jnp.bfloat16

D_MODEL = 2048
ATTN_W = 1024
KV_W = 256
SSM_W = 1024
HEAD_DIM = 64
N_HEADS = 16
N_KV = 4
IN_W = 4608
BLOCK = 128
ROPE_THETA = 10000.0
NORM_EPS = 1e-6
SSM_G = 64
SSM_P = 64
SSM_H = 16
CHUNK = 16
CW = CHUNK * SSM_H
N_DEV = 8

ADAM_LR = 0.001
ADAM_B1 = 0.9
ADAM_B2 = 0.999
ADAM_EPS = 1e-08
ADAM_WD = 0.01
ADAM_STEP = 10

VMEM_LIMIT = 56 * 1024 * 1024
MESH = pl.DeviceIdType.MESH


def _cp(sem=None):
    if sem is None:
        return pltpu.CompilerParams(vmem_limit_bytes=VMEM_LIMIT)
    return pltpu.CompilerParams(vmem_limit_bytes=VMEM_LIMIT, dimension_semantics=sem)


def _sigmoid(x):
    return 0.5 * jnp.tanh(0.5 * x) + 0.5


def _silu(x):
    return x * _sigmoid(x)


def _dsilu(x):
    s = _sigmoid(x)
    return s * (1.0 + x * (1.0 - s))


_GELU_C = math.sqrt(2.0 / math.pi)


def _gelu(y):
    t = jnp.tanh(_GELU_C * (y + 0.044715 * y * y * y))
    return 0.5 * y * (1.0 + t)


def _dgelu(y):
    t = jnp.tanh(_GELU_C * (y + 0.044715 * y * y * y))
    return 0.5 * (1.0 + t) + 0.5 * y * (1.0 - t * t) * _GELU_C * (1.0 + 3.0 * 0.044715 * y * y)


def _tile(n, want):
    if n <= want:
        return n
    for t in range(want - want % 16, 0, -16):
        if n % t == 0:
            return t
    raise ValueError((n, want))


def _mm_tn(a, b, name, tm=512, tn=1024, ride=None, panel=None):
    (K, M), (K2, N) = a.shape, b.shape
    assert K == K2
    tm, tn = _tile(M, tm), _tile(N, tn)
    p0 = 0
    if panel is not None:
        p0, N = panel, tn

    def body(a_ref, b_ref, o_ref):
        o_ref[...] = lax.dot_general(a_ref[...].astype(BF16), b_ref[...].astype(BF16), _TN,
                                     preferred_element_type=F32)

    a_spec = pl.BlockSpec((K, tm), lambda j, i: (0, i))
    b_spec = pl.BlockSpec((K, tn), lambda j, i: (0, j + p0))
    o_spec = pl.BlockSpec((tm, tn), lambda j, i: (i, j))
    if ride is not None:
        (out,), landed = _call(body, name, (N // tn, M // tm), [a_spec, b_spec], [o_spec],
                               [jax.ShapeDtypeStruct((M, N), F32)], (a, b), ride=ride)
        return out, landed
    return pl.pallas_call(
        body,
        name=name,
        grid=(N // tn, M // tm),
        in_specs=[a_spec, b_spec],
        out_specs=o_spec,
        out_shape=jax.ShapeDtypeStruct((M, N), F32),
        compiler_params=_cp(("parallel", "parallel")),
    )(a, b)


_CHIP_ORDER = (0, 2, 1, 3)


def _rms_inproj_gather(x, norm_w, wt_shard, chip):
    L = x.shape[0]
    tm = _tile(L, 1024)
    ni = L // tm
    r = IN_W // N_DEV
    tn = 2 * r

    def body(chip_ref, x_ref, nw_ref, shard, proj_ref, hn_hbm, wt_hbm, hn_scr, w_scr, send_sems, recv_sems, loc_sems):
        jc, i = pl.program_id(0), pl.program_id(1)
        xx, yy, c = lax.axis_index("x"), lax.axis_index("y"), lax.axis_index("c")
        me, sibling = (xx, yy, c), (xx, yy, 1 - c)
        chips = [(1 - xx, yy), (xx, 1 - yy), (1 - xx, 1 - yy)]

        def slot(dev):
            return wt_hbm.at[4 * dev[0] + 2 * dev[1] + dev[2]]

        def copy(q, block, to, src=None):
            return pltpu.make_async_remote_copy(
                src_ref=slot(block) if src is None else src, dst_ref=slot(block),
                send_sem=send_sems.at[q], recv_sem=recv_sems.at[q], device_id=to, device_id_type=MESH)

        def rows_of(buf, core):
            return w_scr.at[buf, pl.ds(pl.multiple_of(core * r, 16), r)]

        mine = pltpu.make_async_copy(shard, slot(me), loc_sems.at[0])
        sends = [copy(0, me, sibling, src=shard)] + [copy(1 + j, me, (*ch, c), src=shard) for j, ch in enumerate(chips[:2])]
        relay_block = (xx + (1 - c) * (1 - 2 * xx), yy + c * (1 - 2 * yy), c)
        relay = copy(3, relay_block, (xx + c * (1 - 2 * xx), yy + (1 - c) * (1 - 2 * yy), c))
        first = jnp.logical_and(jc == 0, i == 0)

        @pl.when(first)
        def _():
            mine.start()
            for cp in sends:
                cp.start()
            own = pltpu.make_async_copy(shard, rows_of(0, c), loc_sems.at[1])
            own.start()
            copy(0, sibling, me).wait_recv()
            sib = pltpu.make_async_copy(slot(sibling), rows_of(0, 1 - c), loc_sems.at[2])
            sib.start()
            own.wait()
            sib.wait()

        def to_vmem(j, ch):
            pltpu.make_async_copy(slot((*ch, c)), rows_of((1 + j) % 2, c), loc_sems.at[1 + j]).start()

        @pl.when(jnp.logical_and(jc == 1, i == 0))
        def _():
            for j in range(2):
                copy(1 + j, (*chips[j], c), me).wait_recv()
                copy(4 + j, (*chips[j], c), sibling).start()
            relay.start()
            to_vmem(0, chips[0])

        @pl.when(jnp.logical_and(jc == 1, i == ni // 2))
        def _():
            to_vmem(1, chips[1])

        @pl.when(jnp.logical_and(jc == 2, i == ni // 2))
        def _():
            copy(3, (*chips[2], c), me).wait_recv()
            copy(6, (*chips[2], c), sibling).start()
            to_vmem(2, chips[2])

        for j, ch in enumerate(chips):
            @pl.when(jnp.logical_and(jc == 1 + j, i == 0))
            def _(j=j, ch=ch):
                buf = (1 + j) % 2
                copy(4 + j, (*ch, 1 - c), me).wait_recv()
                passed = pltpu.make_async_copy(slot((*ch, 1 - c)), rows_of(buf, 1 - c), loc_sems.at[4 + j])
                passed.start()
                pltpu.make_async_copy(slot((*ch, c)), rows_of(buf, c), loc_sems.at[1 + j]).wait()
                passed.wait()

        rows = pl.ds(pl.multiple_of(i * tm, tm), tm)

        @pl.when(jc == 0)
        def _():
            xv = x_ref[...]
            rstd = lax.rsqrt(jnp.mean(xv * xv, axis=-1, keepdims=True) + NORM_EPS)
            hn_scr[rows, :] = (xv * rstd * nw_ref[...]).astype(BF16)

        keep_hn = pltpu.make_async_copy(hn_scr, hn_hbm, loc_sems.at[7])

        @pl.when(jnp.logical_and(jc == 1, i == 0))
        def _():
            keep_hn.start()

        for buf in range(2):
            @pl.when(jc % 2 == buf)
            def _(buf=buf):
                proj_ref[...] = lax.dot_general(hn_scr[rows, :], w_scr[buf], _NT, preferred_element_type=F32)

        @pl.when(jnp.logical_and(jc == 3, i == ni - 1))
        def _():
            for cp in sends + [relay]:
                cp.wait_send()
            for j, ch in enumerate(chips):
                copy(4 + j, (*ch, c), sibling).wait_send()
            mine.wait()
            keep_hn.wait()

    def tile_of(jc, chip_ref):
        mask = jnp.where(jc == 1, _CHIP_ORDER[1], jnp.where(jc == 2, _CHIP_ORDER[2], jnp.where(jc == 3, _CHIP_ORDER[3], 0)))
        return jnp.bitwise_xor(chip_ref[0], mask)

    held = lambda jc, i: jnp.where(jc == 0, i, ni - 1)
    return pl.pallas_call(
        body,
        name="rms_inproj_gather",
        grid_spec=pltpu.PrefetchScalarGridSpec(
            num_scalar_prefetch=1,
            grid=(4, ni),
            in_specs=[pl.BlockSpec((tm, D_MODEL), lambda jc, i, ch: (held(jc, i), 0)),
                      pl.BlockSpec((1, D_MODEL), lambda jc, i, ch: (0, 0)), _ANY],
            out_specs=[pl.BlockSpec((tm, tn), lambda jc, i, ch: (i, tile_of(jc, ch))), _ANY, _ANY],
            scratch_shapes=[pltpu.VMEM((L, D_MODEL), BF16), pltpu.VMEM((2, tn, D_MODEL), BF16),
                            pltpu.SemaphoreType.DMA((7,)), pltpu.SemaphoreType.DMA((7,)), pltpu.SemaphoreType.DMA((8,))],
        ),
        out_shape=[jax.ShapeDtypeStruct((L, IN_W), F32), jax.ShapeDtypeStruct((L, D_MODEL), BF16),
                   jax.ShapeDtypeStruct((N_DEV, r, D_MODEL), BF16)],
        compiler_params=_cp(("arbitrary", "arbitrary")),
    )(chip, x, norm_w.reshape(1, D_MODEL), wt_shard)


def _seg_sum(v):
    a = lax.broadcasted_iota(jnp.int32, (128, 128), 0) // HEAD_DIM
    b = lax.broadcasted_iota(jnp.int32, (128, 128), 1) // HEAD_DIM
    ones = jnp.where(a == b, 1.0, 0.0).astype(BF16)
    hi = v.astype(BF16)
    lo = (v - hi.astype(F32)).astype(BF16)
    return jnp.dot(hi, ones, preferred_element_type=F32) + jnp.dot(lo, ones, preferred_element_type=F32)


def _rot_half(t):
    lane = lax.broadcasted_iota(jnp.int32, t.shape, 1)
    return jnp.where(lane % HEAD_DIM < HEAD_DIM // 2, pltpu.roll(t, 128 - HEAD_DIM // 2, 1),
                     pltpu.roll(t, HEAD_DIM // 2, 1))


def _norm_rope(raw, w, cos, sin):
    r = lax.rsqrt(_seg_sum(raw * raw) * (1.0 / HEAD_DIM) + NORM_EPS)
    tn = raw * r * w
    return r, tn * cos + _rot_half(tn) * sin


def _norm_rope_bwd(d_rot, raw, w, cos, sin):
    r = lax.rsqrt(_seg_sum(raw * raw) * (1.0 / HEAD_DIM) + NORM_EPS)
    d_tn = d_rot * cos + _rot_half(d_rot * sin)
    xh = raw * r
    gw = d_tn * w
    d_raw = r * (gw - xh * (_seg_sum(gw * xh) * (1.0 / HEAD_DIM)))
    return d_raw, d_tn * xh


def _band_mask2(has_prev, keys_on_rows=False):
    qd, kd = (1, 0) if keys_on_rows else (0, 1)
    qi = lax.broadcasted_iota(jnp.int32, (2 * BLOCK, 2 * BLOCK), qd) % BLOCK + BLOCK
    kj = lax.broadcasted_iota(jnp.int32, (2 * BLOCK, 2 * BLOCK), kd)
    rel = qi - kj
    return (rel >= 0) & (rel < BLOCK) & ((kj >= BLOCK) | has_prev)


def _half_tiles(pair):
    lo = lax.broadcasted_iota(jnp.int32, pair.shape, 1) < HEAD_DIM
    sw = pltpu.roll(pair, HEAD_DIM, 1)
    z = jnp.zeros_like(pair)
    return (jnp.where(lo, pair, z).astype(BF16), jnp.where(lo, z, sw).astype(BF16),
            jnp.where(lo, sw, z).astype(BF16), jnp.where(lo, z, pair).astype(BF16))


def _two_rows(top, bottom):
    row = lax.broadcasted_iota(jnp.int32, (2 * BLOCK, 1), 0)
    return jnp.where(row < BLOCK, top, bottom)


_SCALE = 1.0 / math.sqrt(HEAD_DIM)
_NT = (((1,), (1,)), ((), ()))
_NN = (((1,), (0,)), ((), ()))
_TN = (((0,), (0,)), ((), ()))


def _qk_prep(proj, tab, qw, kw):
    L = proj.shape[0]
    tm = _tile(L, 512)

    def body(q_ref, k_ref, t_ref, qw_ref, kw_ref, qo_ref, ko_ref):
        cos, sin = t_ref[:, :128], t_ref[:, 128:]
        for c in range(ATTN_W // 128):
            _, qr = _norm_rope(q_ref[:, c * 128:(c + 1) * 128], qw_ref[...], cos, sin)
            qo_ref[:, c * 128:(c + 1) * 128] = (qr * _SCALE).astype(BF16)
        for c in range(KV_W // 128):
            _, kr = _norm_rope(k_ref[:, c * 128:(c + 1) * 128], kw_ref[...], cos, sin)
            ko_ref[:, c * 128:(c + 1) * 128] = kr.astype(BF16)

    row = pl.BlockSpec((1, 128), lambda i: (0, 0))
    return pl.pallas_call(
        body,
        name="qk_prep",
        grid=(L // tm,),
        in_specs=[pl.BlockSpec((tm, ATTN_W), lambda i: (i, 0)), pl.BlockSpec((tm, KV_W), lambda i: (i, 4)),
                  pl.BlockSpec((tm, 256), lambda i: (i, 0)), row, row],
        out_specs=[pl.BlockSpec((tm, ATTN_W), lambda i: (i, 0)), pl.BlockSpec((tm, KV_W), lambda i: (i, 0))],
        out_shape=[jax.ShapeDtypeStruct((L, ATTN_W), BF16), jax.ShapeDtypeStruct((L, KV_W), BF16)],
        compiler_params=_cp(("parallel",)),
    )(proj, proj, tab, jnp.tile(qw, 2).reshape(1, 128), jnp.tile(kw, 2).reshape(1, 128))


def _group_tiles(g, kt, vt):
    a, b = divmod(g, 2)
    return kt[a][2 * b], kt[a][2 * b + 1], vt[a][2 * b], vt[a][2 * b + 1]


def _attn_fwd(q, k, proj, sinks, ride):
    L = proj.shape[0]
    nb = L // BLOCK

    def body(q_ref, kc_ref, kp_ref, vc_ref, vp_ref, z0_ref, z1_ref, sink_ref, og_ref, o_ref, lse_ref):
        i = pl.program_id(0)
        mask = _band_mask2(i > 0)
        z = jnp.concatenate([z0_ref[...], z1_ref[...]], axis=1)
        lane = lax.broadcasted_iota(jnp.int32, (BLOCK, 128), 1)
        kt = [_half_tiles(jnp.concatenate([kp_ref[:, a * 128:(a + 1) * 128], kc_ref[:, a * 128:(a + 1) * 128]],
                                          axis=0).astype(F32)) for a in range(2)]
        vt = [_half_tiles(jnp.concatenate([vp_ref[:, a * 128:(a + 1) * 128], vc_ref[:, a * 128:(a + 1) * 128]],
                                          axis=0)) for a in range(2)]
        lse_mat = jnp.zeros((BLOCK, 128), F32)
        pairs = []
        for g in range(N_KV):
            k_lo, k_hi, v_lo, v_hi = _group_tiles(g, kt, vt)
            q2 = jnp.concatenate([q_ref[:, 2 * g * 128:(2 * g + 1) * 128],
                                  q_ref[:, (2 * g + 1) * 128:(2 * g + 2) * 128]], axis=0)
            for half, (kh, vh) in enumerate(((k_lo, v_lo), (k_hi, v_hi))):
                pairs.append(dict(g=g, half=half, vh=vh, s=lax.dot_general(q2, kh, _NT, preferred_element_type=F32)))
        for pr in pairs:
            h_top, h_bot = 4 * pr["g"] + pr["half"], 4 * pr["g"] + 2 + pr["half"]
            s = jnp.where(mask, pr["s"], -1e30)
            sink = _two_rows(sink_ref[h_top], sink_ref[h_bot])
            m = jnp.maximum(jnp.max(s, axis=-1, keepdims=True), sink)
            e = jnp.exp(s - m)
            den = jnp.sum(e, axis=-1, keepdims=True) + jnp.exp(sink - m)
            pr["p_b"] = (e * (1.0 / den)).astype(BF16)
            lse = m + jnp.log(den)
            lse_mat = jnp.where(lane == h_top, lse[:BLOCK], lse_mat)
            lse_mat = jnp.where(lane == h_bot, lse[BLOCK:], lse_mat)
        outs = []
        for g in range(N_KV):
            acc = (jnp.dot(pairs[2 * g]["p_b"], pairs[2 * g]["vh"], preferred_element_type=F32)
                   + jnp.dot(pairs[2 * g + 1]["p_b"], pairs[2 * g + 1]["vh"], preferred_element_type=F32))
            outs += [acc[:BLOCK], acc[BLOCK:]]
        o = jnp.concatenate(outs, axis=1)
        o_ref[...] = o.astype(BF16)
        og_ref[...] = (o * _silu(z)).astype(BF16)
        lse_ref[...] = lse_mat

    prev = lambda i: jnp.maximum(i - 1, 0)
    return _call(
        body, "attn_fwd", (nb,),
        [pl.BlockSpec((BLOCK, ATTN_W), lambda i: (i, 0)),
         pl.BlockSpec((BLOCK, KV_W), lambda i: (i, 0)),
         pl.BlockSpec((BLOCK, KV_W), lambda i: (prev(i), 0)),
         pl.BlockSpec((BLOCK, KV_W), lambda i: (i, 5)),
         pl.BlockSpec((BLOCK, KV_W), lambda i: (prev(i), 5)),
         pl.BlockSpec((BLOCK, 512), lambda i: (i, 3)),
         pl.BlockSpec((BLOCK, 512), lambda i: (i, 4)),
         pl.BlockSpec(memory_space=pltpu.SMEM)],
        [pl.BlockSpec((BLOCK, ATTN_W), lambda i: (i, 0)),
         pl.BlockSpec((BLOCK, ATTN_W), lambda i: (i, 0)),
         pl.BlockSpec((BLOCK, 128), lambda i: (i, 0))],
        [jax.ShapeDtypeStruct((L, ATTN_W), BF16), jax.ShapeDtypeStruct((L, ATTN_W), BF16),
         jax.ShapeDtypeStruct((L, 128), F32)],
        (q, k, k, proj, proj, proj, proj, sinks), ride=ride)


def _attn_bwd(q, k, proj, sinks, d_o, o, lse, ride):
    L = proj.shape[0]
    nb = L // BLOCK

    def body(q_ref, kc_ref, kp_ref, vc_ref, vp_ref, do_ref, o_ref, lse_ref, sink_ref,
             dq_ref, dk_ref, dv_ref, gs_ref, ck_scr, cv_scr):
        i = pl.program_id(0)

        @pl.when(i == 0)
        def _():
            gs_ref[...] = jnp.zeros_like(gs_ref)
            ck_scr[...] = jnp.zeros_like(ck_scr)
            cv_scr[...] = jnp.zeros_like(cv_scr)

        @pl.when(i == nb)
        def _():
            dk_ref[...] = ck_scr[...]
            dv_ref[...] = cv_scr[...]

        @pl.when(i < nb)
        def _():
            mask = _band_mask2(i > 0, keys_on_rows=True)
            lane = lax.broadcasted_iota(jnp.int32, (1, 128), 1)
            lane2 = lax.broadcasted_iota(jnp.int32, (1, 2 * BLOCK), 1)
            lo = lax.broadcasted_iota(jnp.int32, (2 * BLOCK, 128), 1) < HEAD_DIM
            lse_t = lse_ref[...].T
            prod_all = do_ref[...].astype(F32) * o_ref[...].astype(F32)
            seg = (lax.broadcasted_iota(jnp.int32, (N_HEADS, ATTN_W), 1) // HEAD_DIM
                   == lax.broadcasted_iota(jnp.int32, (N_HEADS, ATTN_W), 0)).astype(BF16)
            prod_hi = prod_all.astype(BF16)
            prod_lo = (prod_all - prod_hi.astype(F32)).astype(BF16)
            delta_t = (lax.dot_general(seg, prod_hi, _NT, preferred_element_type=F32)
                       + lax.dot_general(seg, prod_lo, _NT, preferred_element_type=F32))
            kt = [_half_tiles(jnp.concatenate([kp_ref[:, a * 128:(a + 1) * 128], kc_ref[:, a * 128:(a + 1) * 128]],
                                              axis=0).astype(F32)) for a in range(2)]
            vt = [_half_tiles(jnp.concatenate([vp_ref[:, a * 128:(a + 1) * 128], vc_ref[:, a * 128:(a + 1) * 128]],
                                              axis=0)) for a in range(2)]
            gs = jnp.zeros((1, 128), F32)
            dq_parts = []
            dk_acc = [jnp.zeros((2 * BLOCK, 128), F32) for _ in range(2)]
            dv_acc = [jnp.zeros((2 * BLOCK, 128), F32) for _ in range(2)]
            pairs = []
            for g in range(N_KV):
                k_lo, k_hi, v_lo, v_hi = _group_tiles(g, kt, vt)
                t0, t1 = slice(2 * g * 128, (2 * g + 1) * 128), slice((2 * g + 1) * 128, (2 * g + 2) * 128)
                q2 = jnp.concatenate([q_ref[:, t0], q_ref[:, t1]], axis=0)
                do2_b = jnp.concatenate([do_ref[:, t0], do_ref[:, t1]], axis=0).astype(BF16)
                for half, (kh, vh) in enumerate(((k_lo, v_lo), (k_hi, v_hi))):
                    pairs.append(dict(g=g, half=half, kh=kh, q2=q2, do2_b=do2_b,
                                      s=lax.dot_general(kh, q2, _NT, preferred_element_type=F32),
                                      dp=lax.dot_general(vh, do2_b, _NT, preferred_element_type=F32)))
            for pr in pairs:
                h_top, h_bot = 4 * pr["g"] + pr["half"], 4 * pr["g"] + 2 + pr["half"]
                pick = lambda t: jnp.concatenate([t[h_top:h_top + 1, :], t[h_bot:h_bot + 1, :]], axis=1)
                lse, delta = pick(lse_t), pick(delta_t)
                sink = jnp.where(lane2 < BLOCK, sink_ref[h_top], sink_ref[h_bot])
                p = jnp.exp(jnp.where(mask, pr["s"], -1e30) - lse)
                pr["ds_b"] = (p * (pr["dp"] - delta)).astype(BF16)
                pr["p_b"] = p.astype(BF16)
                gsink = -jnp.exp(sink - lse) * delta
                gs = gs + jnp.where(lane == h_top, jnp.sum(jnp.where(lane2 < BLOCK, gsink, 0.0)), 0.0)
                gs = gs + jnp.where(lane == h_bot, jnp.sum(jnp.where(lane2 >= BLOCK, gsink, 0.0)), 0.0)
            for g in range(N_KV):
                a, b = divmod(g, 2)
                dq2 = jnp.zeros((2 * BLOCK, 128), F32)
                dk_h, dv_h = [], []
                for pr in pairs[2 * g:2 * g + 2]:
                    dq2 = dq2 + lax.dot_general(pr["ds_b"], pr["kh"], _TN, preferred_element_type=F32)
                    dk_h.append(jnp.dot(pr["ds_b"], pr["q2"], preferred_element_type=F32))
                    dv_h.append(jnp.dot(pr["p_b"], pr["do2_b"], preferred_element_type=F32))
                dq_parts += [dq2[:BLOCK], dq2[BLOCK:]]
                for acc, parts in ((dk_acc, dk_h), (dv_acc, dv_h)):
                    t = jnp.where(lo, parts[0], parts[1])
                    t = t + pltpu.roll(t, HEAD_DIM, 1)
                    acc[a] = acc[a] + jnp.where(lo == (b == 0), t, 0.0)
            dq_ref[...] = jnp.concatenate(dq_parts, axis=1)
            dk_full = jnp.concatenate(dk_acc, axis=1)
            dv_full = jnp.concatenate(dv_acc, axis=1)
            dk_ref[...] = ck_scr[...] + dk_full[:BLOCK]
            dv_ref[...] = cv_scr[...] + dv_full[:BLOCK]
            ck_scr[...] = dk_full[BLOCK:]
            cv_scr[...] = dv_full[BLOCK:]
            gs_ref[...] += gs

    cur = lambda i: jnp.minimum(i, nb - 1)
    prev = lambda i: jnp.maximum(jnp.minimum(i, nb - 1) - 1, 0)
    done = lambda i: jnp.maximum(i - 1, 0)
    bs = pl.BlockSpec
    return _call(
        body, "attn_bwd", (nb + 1,),
        [bs((BLOCK, ATTN_W), lambda i: (cur(i), 0)),
         bs((BLOCK, KV_W), lambda i: (cur(i), 0)), bs((BLOCK, KV_W), lambda i: (prev(i), 0)),
         bs((BLOCK, KV_W), lambda i: (cur(i), 5)), bs((BLOCK, KV_W), lambda i: (prev(i), 5)),
         bs((BLOCK, ATTN_W), lambda i: (cur(i), 0)), bs((BLOCK, ATTN_W), lambda i: (cur(i), 0)),
         bs((BLOCK, 128), lambda i: (cur(i), 0)), bs(memory_space=pltpu.SMEM)],
        [bs((BLOCK, ATTN_W), lambda i: (cur(i), 0)),
         bs((BLOCK, KV_W), lambda i: (done(i), 0)), bs((BLOCK, KV_W), lambda i: (done(i), 0)),
         bs((1, 128), lambda i: (0, 0))],
        [jax.ShapeDtypeStruct((L, ATTN_W), F32), jax.ShapeDtypeStruct((L, KV_W), F32),
         jax.ShapeDtypeStruct((L, KV_W), F32), jax.ShapeDtypeStruct((1, 128), F32)],
        (q, k, k, proj, proj, d_o, o, lse, sinks),
        [pltpu.VMEM((BLOCK, KV_W), F32), pltpu.VMEM((BLOCK, KV_W), F32)], ride)


def _qk_prep_bwd(proj, tab, qw, kw, d_q, d_k, d_v, d_za, d_u, d_zs):
    L = proj.shape[0]
    tm = _tile(L, 512)
    z0 = ATTN_W + 2 * KV_W

    def body(q_ref, k_ref, t_ref, qw_ref, kw_ref, dq_ref, dk_ref, dv_ref, dza_ref, du_ref, dzs_ref,
             out_ref, gq_ref, gk_ref):
        i = pl.program_id(0)

        @pl.when(i == 0)
        def _():
            gq_ref[...] = jnp.zeros_like(gq_ref)
            gk_ref[...] = jnp.zeros_like(gk_ref)

        cos, sin = t_ref[:, :128], t_ref[:, 128:]
        gq = jnp.zeros((1, 128), F32)
        gk = jnp.zeros((1, 128), F32)
        for c in range(ATTN_W // 128):
            cs = slice(c * 128, (c + 1) * 128)
            d_raw, gw = _norm_rope_bwd(dq_ref[:, cs] * _SCALE, q_ref[:, cs], qw_ref[...], cos, sin)
            out_ref[:, cs] = d_raw.astype(BF16)
            gq = gq + jnp.sum(gw, axis=0, keepdims=True)
        for c in range(KV_W // 128):
            cs = slice(c * 128, (c + 1) * 128)
            d_raw, gw = _norm_rope_bwd(dk_ref[:, cs], k_ref[:, cs], kw_ref[...], cos, sin)
            out_ref[:, ATTN_W + c * 128:ATTN_W + (c + 1) * 128] = d_raw.astype(BF16)
            gk = gk + jnp.sum(gw, axis=0, keepdims=True)
        out_ref[:, ATTN_W + KV_W:z0] = dv_ref[...].astype(BF16)
        out_ref[:, z0:z0 + ATTN_W] = dza_ref[...]
        out_ref[:, z0 + ATTN_W:z0 + ATTN_W + SSM_W] = du_ref[...].astype(BF16)
        out_ref[:, z0 + ATTN_W + SSM_W:] = dzs_ref[...]
        gq_ref[...] += gq
        gk_ref[...] += gk

    row = pl.BlockSpec((1, 128), lambda i: (0, 0))
    blk = lambda w, c: pl.BlockSpec((tm, w), lambda i: (i, c))
    return pl.pallas_call(
        body,
        name="qk_prep_bwd",
        grid=(L // tm,),
        in_specs=[blk(ATTN_W, 0), blk(KV_W, 4), blk(256, 0), row, row, blk(ATTN_W, 0), blk(KV_W, 0), blk(KV_W, 0),
                  blk(ATTN_W, 0), blk(SSM_W, 0), blk(SSM_W, 0)],
        out_specs=[blk(IN_W, 0), row, row],
        out_shape=[jax.ShapeDtypeStruct((L, IN_W), BF16), jax.ShapeDtypeStruct((1, 128), F32),
                   jax.ShapeDtypeStruct((1, 128), F32)],
        compiler_params=_cp(("arbitrary",)),
    )(proj, proj, tab, jnp.tile(qw, 2).reshape(1, 128), jnp.tile(kw, 2).reshape(1, 128), d_q, d_k, d_v,
      d_za, d_u, d_zs)


def _cmul(a, b):
    return a[0] * b[0] - a[1] * b[1], a[0] * b[1] + a[1] * b[0]


def _cmul_conj(a, b):
    return a[0] * b[0] + a[1] * b[1], a[1] * b[0] - a[0] * b[1]


def _cadd(a, b):
    return a[0] + b[0], a[1] + b[1]


def _dot3(a, b, dn):
    ah, bh = a.astype(BF16), b.astype(BF16)
    al, bl = (a - ah.astype(F32)).astype(BF16), (b - bh.astype(F32)).astype(BF16)
    d = lambda u, v: lax.dot_general(u, v, dn, preferred_element_type=F32)
    return d(ah, bh) + d(ah, bl) + d(al, bh)


def _s5_discretise(a_re, a_im, ls, cosx, sinx, bt):
    delta = jnp.exp(ls)
    er = jnp.exp(a_re * delta)
    lb = (er * cosx, er * sinx)
    den = a_re * a_re + a_im * a_im
    coef = _cmul_conj((lb[0] - 1.0, lb[1]), (a_re, a_im))
    coef = (coef[0] / den, coef[1] / den)
    return delta, lb, coef, den, _cmul(coef, bt)


def _powers(lb):
    pw = [(jnp.ones_like(lb[0]), jnp.zeros_like(lb[0]))]
    for _ in range(CHUNK):
        pw.append(_cmul(pw[-1], lb))
    return pw


def _block_rows(a, pw, idx):
    blocks = [_cmul(a, pw[i]) for i in idx]
    return (jnp.concatenate([b[0] for b in blocks], axis=-2), jnp.concatenate([b[1] for b in blocks], axis=-2))


def _block_rows_bwd(g, a, pw, idx, g_pw):
    g_a = (jnp.zeros_like(a[0]), jnp.zeros_like(a[0]))
    for j, i in enumerate(idx):
        gj = (g[0][..., j * SSM_H:(j + 1) * SSM_H, :], g[1][..., j * SSM_H:(j + 1) * SSM_H, :])
        g_a = _cadd(g_a, _cmul_conj(gj, pw[i]))
        gp = _cmul_conj(gj, a)
        g_pw[i] = _cadd(g_pw[i], (jnp.sum(gp[0], axis=-2, keepdims=True), jnp.sum(gp[1], axis=-2, keepdims=True)))
    return g_a


_IDX_S = [CHUNK - 1 - s for s in range(CHUNK)]
_IDX_C = list(range(CHUNK + 1))


def _prep_args(p):
    row = lambda t: t.reshape(SSM_G, 1, SSM_P)
    xi = p["a_im"] * jnp.exp(p["log_step"])[:, None]
    return (row(p["a_re"]), row(p["a_im"]), row(jnp.broadcast_to(p["log_step"][:, None], (SSM_G, SSM_P))),
            row(jnp.cos(xi)), row(jnp.sin(xi)), p["b_re"].transpose(0, 2, 1), p["b_im"].transpose(0, 2, 1),
            p["c_re"], p["c_im"])


PREP_GROUPS = 8


def _prep_specs():
    r1 = pl.BlockSpec((PREP_GROUPS, 1, SSM_P), lambda g: (g, 0, 0))
    r16 = pl.BlockSpec((PREP_GROUPS, SSM_H, SSM_P), lambda g: (g, 0, 0))
    return [r1] * 5 + [r16] * 4, r1, r16


def _ssm_prep(p):
    def one_group(q, are, aim, ls, cosx, sinx, btr, bti, cre, cim, mt_ref, s_ref, o_ref, a_ref):
        _, lb, _, _, bb = _s5_discretise(are[q], aim[q], ls[q], cosx[q], sinx[q], (btr[q], bti[q]))
        pw = _powers(lb)
        c = (cre[q], cim[q])
        sc = _block_rows(bb, pw, _IDX_S)
        cl = _block_rows(c, pw, _IDX_C)
        ok = (cl[0][:CW], cl[1][:CW])
        ot = (cl[0][SSM_H:], cl[1][SSM_H:])
        s_ref[q] = jnp.concatenate([sc[0], sc[1]], axis=1).astype(BF16)
        o_ref[q] = jnp.concatenate([ot[0], -ot[1]], axis=1).astype(BF16)
        a_ref[q] = jnp.concatenate([pw[CHUNK][0], pw[CHUNK][1]], axis=1)
        kt = _dot3(jnp.concatenate([bb[0], -bb[1]], axis=1), jnp.concatenate([ok[0], ok[1]], axis=1), _NT)
        lane = lax.broadcasted_iota(jnp.int32, kt.shape, 1)
        for s in range(CHUNK):
            blk = kt if s == 0 else jnp.where(lane >= SSM_H * s, pltpu.roll(kt, SSM_H * s, 1), 0.0)
            mt_ref[q, s * SSM_H:(s + 1) * SSM_H, :] = blk.astype(BF16)

    def body(*refs):
        for q in range(PREP_GROUPS):
            one_group(q, *refs)

    in_specs, r1, _ = _prep_specs()
    g3 = lambda r, c: pl.BlockSpec((PREP_GROUPS, r, c), lambda g: (g, 0, 0))
    return pl.pallas_call(
        body,
        name="ssm_prep",
        grid=(SSM_G // PREP_GROUPS,),
        in_specs=in_specs,
        out_specs=[g3(CW, CW), g3(CW, 2 * SSM_P), g3(CW, 2 * SSM_P), g3(1, 2 * SSM_P)],
        out_shape=[jax.ShapeDtypeStruct((SSM_G, CW, CW), BF16), jax.ShapeDtypeStruct((SSM_G, CW, 2 * SSM_P), BF16),
                   jax.ShapeDtypeStruct((SSM_G, CW, 2 * SSM_P), BF16),
                   jax.ShapeDtypeStruct((SSM_G, 1, 2 * SSM_P), F32)],
        compiler_params=_cp(("parallel",)),
    )(*_prep_args(p))


def _ssm_prep_bwd(p, g_mt, g_scat, g_ocat, g_a16, ride):
    def body(are, aim, ls, cosx, sinx, btr, bti, cre, cim, gmt_ref, gs_ref, go_ref, ga_ref,
             g_are, g_aim, g_ls, g_btr, g_bti, g_cre, g_cim, ga1_scr, gb1_scr):
        lam = (are[...], aim[...])
        bt = (btr[...], bti[...])
        delta, lb, coef, den, bb = _s5_discretise(lam[0], lam[1], ls[...], cosx[...], sinx[...], bt)
        pw = _powers(lb)
        c = (cre[...], cim[...])
        ok = _block_rows(c, pw, _IDX_C[:CHUNK])
        g_pw =[(jnp.zeros_like(lb[0]), jnp.zeros_like(lb[0])) for _ in range(CHUNK + 1)]
        lane = lax.broadcasted_iota(jnp.int32, (SSM_H, CW), 1)
        for q in range(PREP_GROUPS):
            g_kt = gmt_ref[q, :SSM_H, :]
            for s in range(1, CHUNK):
                blk = gmt_ref[q, s * SSM_H:(s + 1) * SSM_H, :]
                g_kt = g_kt + jnp.where(lane < CW - SSM_H * s, pltpu.roll(blk, CW - SSM_H * s, 1), 0.0)
            a1 = jnp.concatenate([bb[0][q], -bb[1][q]], axis=1)
            b1 = jnp.concatenate([ok[0][q], ok[1][q]], axis=1)
            ga1_scr[q] = _dot3(g_kt, b1, _NN)
            gb1_scr[q] = _dot3(g_kt, a1, _TN)
        g_a1, g_b1 = ga1_scr[...], gb1_scr[...]
        g_bb = (g_a1[..., :SSM_P], -g_a1[..., SSM_P:])
        gs = gs_ref[...]
        g_bb = _cadd(g_bb, _block_rows_bwd((gs[..., :SSM_P], gs[..., SSM_P:]), bb, pw, _IDX_S, g_pw))
        go = go_ref[...]
        pad = jnp.zeros_like(go[..., :SSM_H, :SSM_P])
        g_cl = (jnp.concatenate([g_b1[..., :SSM_P], pad], axis=-2) + jnp.concatenate([pad, go[..., :SSM_P]], axis=-2),
                jnp.concatenate([g_b1[..., SSM_P:], pad], axis=-2) - jnp.concatenate([pad, go[..., SSM_P:]], axis=-2))
        g_c = _block_rows_bwd(g_cl, c, pw, _IDX_C, g_pw)
        ga = ga_ref[...]
        g_pw[CHUNK] = _cadd(g_pw[CHUNK], (ga[..., :SSM_P], ga[..., SSM_P:]))
        g_lb = (jnp.zeros_like(lb[0]), jnp.zeros_like(lb[0]))
        for l in range(CHUNK - 1, -1, -1):
            g_lb = _cadd(g_lb, _cmul_conj(g_pw[l + 1], pw[l]))
            g_pw[l] = _cadd(g_pw[l], _cmul_conj(g_pw[l + 1], lb))
        g_bt = _cmul_conj(g_bb, coef)
        gc = _cmul_conj(g_bb, bt)
        g_coef = (jnp.sum(gc[0], axis=-2, keepdims=True), jnp.sum(gc[1], axis=-2, keepdims=True))
        lam_den = (lam[0] / den, lam[1] / den)
        g_lb = _cadd(g_lb, _cmul(g_coef, lam_den))
        t = _cmul(_cmul_conj(g_coef, coef), lam_den)
        g_x = _cmul_conj(g_lb, lb)
        g_are[...] = g_x[0] * delta - t[0]
        g_aim[...] = g_x[1] * delta - t[1]
        g_ls[...] = (g_x[0] * lam[0] + g_x[1] * lam[1]) * delta
        g_btr[...] = g_bt[0]
        g_bti[...] = g_bt[1]
        g_cre[...] = g_c[0]
        g_cim[...] = g_c[1]

    in_specs, r1, r16 = _prep_specs()
    g3 = lambda r, c: pl.BlockSpec((PREP_GROUPS, r, c), lambda g: (g, 0, 0))
    rows = jax.ShapeDtypeStruct((SSM_G, 1, SSM_P), F32)
    mats = jax.ShapeDtypeStruct((SSM_G, SSM_H, SSM_P), F32)
    (g_are, g_aim, g_ls, g_btr, g_bti, g_cre, g_cim), landed = _call(
        body, "ssm_prep_bwd", (SSM_G // PREP_GROUPS,),
        in_specs + [g3(CW, CW), g3(CW, 2 * SSM_P), g3(CW, 2 * SSM_P), g3(1, 2 * SSM_P)],
        [r1] * 3 + [r16] * 4, [rows] * 3 + [mats] * 4, (*_prep_args(p), g_mt, g_scat, g_ocat, g_a16),
        [pltpu.VMEM((PREP_GROUPS, SSM_H, 2 * SSM_P), F32), pltpu.VMEM((PREP_GROUPS, CW, 2 * SSM_P), F32)], ride)
    grads = dict(a_re=g_are.reshape(SSM_G, SSM_P), a_im=g_aim.reshape(SSM_G, SSM_P),
                 log_step=jnp.sum(g_ls.reshape(SSM_G, SSM_P), axis=1),
                 b_re=g_btr.transpose(0, 2, 1), b_im=g_bti.transpose(0, 2, 1), c_re=g_cre, c_im=g_cim)
    return grads, landed


def _cmul_const(xv, ar, ai):
    return xv * ar + pltpu.roll(xv, SSM_P, 1) * ai


def _chunk_scan(inc, a_row, reverse):
    n = inc.shape[0]
    lane = lax.broadcasted_iota(jnp.int32, (1, 2 * SSM_P), 1)
    row = lax.broadcasted_iota(jnp.int32, inc.shape, 0)
    sign = jnp.where(lane < SSM_P, -1.0, 1.0)
    ar = jnp.where(lane < SSM_P, a_row, pltpu.roll(a_row, SSM_P, 1))
    ai = jnp.where(lane < SSM_P, pltpu.roll(a_row, SSM_P, 1), a_row)
    if reverse:
        ai = -ai
    xv = inc
    s = 1
    while s < n:
        if reverse:
            sh = jnp.where(row < n - s, pltpu.roll(xv, n - s, 0), 0.0)
        else:
            sh = jnp.where(row >= s, pltpu.roll(xv, s, 0), 0.0)
        xv = xv + _cmul_const(sh, ar, ai * sign)
        ar, ai = ar * ar - ai * ai, 2.0 * ar * ai
        s *= 2
    return xv


def _shift_rows(xv, reverse):
    n = xv.shape[0]
    row = lax.broadcasted_iota(jnp.int32, xv.shape, 0)
    if reverse:
        return jnp.where(row < n - 1, pltpu.roll(xv, n - 1, 0), 0.0)
    return jnp.where(row >= 1, pltpu.roll(xv, 1, 0), 0.0)


GB = 128 // SSM_H
U_COL0 = (ATTN_W + 2 * KV_W + ATTN_W) // 128


HALF = CHUNK // 2


def _chunk_perm():
    r = jnp.arange(HALF * 128)
    t, g8, h = r // 128, (r % 128) // SSM_H, r % SSM_H
    return ((g8 * 128 + t * SSM_H + h)[:, None] == jnp.arange(GB * 128)[None, :]).astype(BF16)


def _load_perm(p_hbm, p_scr, sem):
    @pl.when(pl.program_id(0) == 0)
    def _():
        cp = pltpu.make_async_copy(p_hbm, p_scr, sem)
        cp.start()
        cp.wait()


def _rows_to_chunks(pieces, perm):
    halves = [jnp.dot(jnp.concatenate(pieces[k * HALF:(k + 1) * HALF], axis=1).astype(BF16), perm,
                      preferred_element_type=F32).astype(BF16) for k in range(2)]
    return [jnp.concatenate([hv[:, g * 128:(g + 1) * 128] for hv in halves], axis=1) for g in range(GB)]


def _chunks_to_rows(groups, perm, two_pass):
    pieces = []
    for k in range(2):
        v = jnp.concatenate([gv[:, k * 128:(k + 1) * 128] for gv in groups], axis=1)
        hi = v.astype(BF16)
        out = lax.dot_general(hi, perm, _NT, preferred_element_type=F32)
        if two_pass:
            lo = (v - hi.astype(F32)).astype(BF16)
            out = out + lax.dot_general(lo, perm, _NT, preferred_element_type=F32)
        pieces += [out[:, t * 128:(t + 1) * 128] for t in range(HALF)]
    return pieces


def _ssm_fwd(proj, perm, mt, scat, ocat, a16, d_skip, ride):
    L = proj.shape[0]
    nc = L // CHUNK

    def body(u_ref, p_hbm, mt_ref, s_ref, o_ref, a_ref, d_ref, y_ref, yg_ref, h_ref, p_scr, sem):
        _load_perm(p_hbm, p_scr, sem)
        perm = p_scr[...]
        rows = [pl.ds(t, nc, stride=CHUNK) for t in range(CHUNK)]
        us = [u_ref[r, :] for r in rows]
        ua = _rows_to_chunks(us, perm)
        incs = [jnp.dot(ua[g], s_ref[g], preferred_element_type=F32) for g in range(GB)]
        intra = [jnp.dot(ua[g], mt_ref[g], preferred_element_type=F32) for g in range(GB)]
        hxs = [_shift_rows(_chunk_scan(incs[g], a_ref[g], False), False) for g in range(GB)]
        ys = []
        for g in range(GB):
            h_ref[g] = hxs[g]
            ys.append(intra[g] + lax.dot_general(hxs[g].astype(BF16), o_ref[g], _NT, preferred_element_type=F32))
        yp = _chunks_to_rows(ys, perm, True)
        for t, r in enumerate(rows):
            y = yp[t] + d_ref[...] * us[t]
            y_ref[r, :] = y
            yg_ref[r, :] = _gelu(y)

    g3 = lambda r, c: pl.BlockSpec((GB, r, c), lambda g: (g, 0, 0))
    col = pl.BlockSpec((L, 128), lambda g: (0, g))
    return _call(
        body, "ssm_fwd", (SSM_G // GB,),
        [pl.BlockSpec((L, 128), lambda g: (0, U_COL0 + g)), _ANY,
         g3(CW, CW), g3(CW, 2 * SSM_P), g3(CW, 2 * SSM_P), g3(1, 2 * SSM_P),
         pl.BlockSpec((1, 128), lambda g: (0, g))],
        [col, col, g3(nc, 2 * SSM_P)],
        [jax.ShapeDtypeStruct((L, SSM_W), F32), jax.ShapeDtypeStruct((L, SSM_W), F32),
         jax.ShapeDtypeStruct((SSM_G, nc, 2 * SSM_P), F32)],
        (proj, perm, mt, scat, ocat, a16, d_skip.reshape(1, SSM_W)),
        [pltpu.VMEM((HALF * 128, GB * 128), BF16), pltpu.SemaphoreType.DMA], ride)


def _ssm_bwd(d_yg, y, proj, hx, perm, mt, scat, ocat, a16, d_skip, ride):
    L = proj.shape[0]
    nc = L // CHUNK

    def body(dg_ref, y_ref, u_ref, h_ref, p_hbm, mt_ref, s_ref, o_ref, a_ref, d_ref,
             du_ref, gmt_ref, gs_ref, go_ref, ga_ref, gd_ref, p_scr, sem):
        _load_perm(p_hbm, p_scr, sem)
        perm = p_scr[...]
        rows = [pl.ds(t, nc, stride=CHUNK) for t in range(CHUNK)]
        us = [u_ref[r, :] for r in rows]
        dys = [dg_ref[r, :] * _dgelu(y_ref[r, :]) for r in rows]
        gd = jnp.zeros((1, 128), F32)
        for uv, dy in zip(us, dys):
            gd = gd + jnp.sum(dy * uv, axis=0, keepdims=True)
        gd_ref[...] = gd
        ua = _rows_to_chunks(us, perm)
        dya = _rows_to_chunks(dys, perm)
        lane = lax.broadcasted_iota(jnp.int32, (1, 2 * SSM_P), 1)
        dhs = [jnp.dot(dya[g], o_ref[g], preferred_element_type=F32) for g in range(GB)]
        intra = [lax.dot_general(dya[g], mt_ref[g], _NT, preferred_element_type=F32) for g in range(GB)]
        for g in range(GB):
            gmt_ref[g] = lax.dot_general(ua[g], dya[g], _TN, preferred_element_type=F32)
            go_ref[g] = lax.dot_general(dya[g], h_ref[g].astype(BF16), _TN, preferred_element_type=F32)
        dincs = [_shift_rows(_chunk_scan(dhs[g], a_ref[g], True), True) for g in range(GB)]
        dus = []
        for g in range(GB):
            dinc, hx_v = dincs[g], h_ref[g]
            dinc_b = dinc.astype(BF16)
            dus.append(intra[g] + lax.dot_general(dinc_b, s_ref[g], _NT, preferred_element_type=F32))
            gs_ref[g] = lax.dot_general(ua[g], dinc_b, _TN, preferred_element_type=F32)
            p1 = dinc * hx_v
            p2 = pltpu.roll(dinc, SSM_P, 1) * hx_v
            t1 = jnp.sum(p1 + pltpu.roll(p1, SSM_P, 1), axis=0, keepdims=True)
            t2 = jnp.sum(p2 - pltpu.roll(p2, SSM_P, 1), axis=0, keepdims=True)
            ga_ref[g] = jnp.where(lane < SSM_P, t1, pltpu.roll(t2, SSM_P, 1))
        dup = _chunks_to_rows(dus, perm, False)
        for t, r in enumerate(rows):
            du_ref[r, :] = dup[t] + d_ref[...] * dys[t]

    g3 = lambda r, c: pl.BlockSpec((GB, r, c), lambda g: (g, 0, 0))
    col = pl.BlockSpec((L, 128), lambda g: (0, g))
    row = pl.BlockSpec((1, 128), lambda g: (0, g))
    return _call(
        body, "ssm_bwd", (SSM_G // GB,),
        [col, col, pl.BlockSpec((L, 128), lambda g: (0, U_COL0 + g)), g3(nc, 2 * SSM_P), _ANY,
         g3(CW, CW), g3(CW, 2 * SSM_P), g3(CW, 2 * SSM_P), g3(1, 2 * SSM_P), row],
        [col, g3(CW, CW), g3(CW, 2 * SSM_P), g3(CW, 2 * SSM_P), g3(1, 2 * SSM_P), row],
        [jax.ShapeDtypeStruct((L, SSM_W), F32), jax.ShapeDtypeStruct((SSM_G, CW, CW), F32),
         jax.ShapeDtypeStruct((SSM_G, CW, 2 * SSM_P), F32), jax.ShapeDtypeStruct((SSM_G, CW, 2 * SSM_P), F32),
         jax.ShapeDtypeStruct((SSM_G, 1, 2 * SSM_P), F32), jax.ShapeDtypeStruct((1, SSM_W), F32)],
        (d_yg, y, proj, hx, perm, mt, scat, ocat, a16, d_skip.reshape(1, SSM_W)),
        [pltpu.VMEM((HALF * 128, GB * 128), BF16), pltpu.SemaphoreType.DMA], ride)


def _merge(og, yg, w_glu, proj, b_glu, wa, ws):
    L = og.shape[0]
    tm = _tile(L, 256)

    def body(og_ref, yg_ref, wg_ref, z0_ref, z1_ref, b_ref, wa_ref, ws_ref, m_ref, gp_ref):
        zs = jnp.concatenate([z0_ref[...], z1_ref[...]], axis=1)
        ygv = yg_ref[...]
        gpre = jnp.dot(ygv.astype(BF16), wg_ref[...], preferred_element_type=F32)
        gp_ref[...] = gpre
        os_ = ygv * _sigmoid(gpre + b_ref[...]) * _silu(zs)
        ogv = og_ref[...].astype(F32)
        ra = lax.rsqrt(jnp.mean(ogv * ogv, axis=-1, keepdims=True) + NORM_EPS)
        rs = lax.rsqrt(jnp.mean(os_ * os_, axis=-1, keepdims=True) + NORM_EPS)
        m_ref[:, :ATTN_W] = (ogv * ra * wa_ref[...]).astype(BF16)
        m_ref[:, ATTN_W:] = (os_ * rs * ws_ref[...]).astype(BF16)

    row = lambda w: pl.BlockSpec((1, w), lambda i: (0, 0))
    return pl.pallas_call(
        body,
        name="merge",
        grid=(L // tm,),
        in_specs=[pl.BlockSpec((tm, ATTN_W), lambda i: (i, 0)), pl.BlockSpec((tm, SSM_W), lambda i: (i, 0)),
                  pl.BlockSpec((SSM_W, SSM_W), lambda i: (0, 0)),
                  pl.BlockSpec((tm, 512), lambda i: (i, 7)), pl.BlockSpec((tm, 512), lambda i: (i, 8)),
                  row(SSM_W), row(ATTN_W), row(SSM_W)],
        out_specs=[pl.BlockSpec((tm, D_MODEL), lambda i: (i, 0)), pl.BlockSpec((tm, SSM_W), lambda i: (i, 0))],
        out_shape=[jax.ShapeDtypeStruct((L, D_MODEL), BF16), jax.ShapeDtypeStruct((L, SSM_W), F32)],
        compiler_params=_cp(("parallel",)),
    )(og, yg, w_glu, proj, proj, b_glu.reshape(1, SSM_W), wa.reshape(1, ATTN_W), ws.reshape(1, SSM_W))


def _outproj_loss(merged, w_out, x, target):
    L = x.shape[0]
    tm, tn = _tile(L, 256), D_MODEL
    ni, nj = L // tm, D_MODEL // tn

    def body(m_ref, w_ref, x_ref, t_ref, d_ref, db_ref, l_ref):
        out = x_ref[...] + jnp.dot(m_ref[...], w_ref[...], preferred_element_type=F32)
        diff = out - t_ref[...]
        d = diff * (1.0 / D_MODEL)
        d_ref[...] = d
        db_ref[...] = d.astype(BF16)
        l_ref[...] = jnp.full((1, 8, 128), jnp.sum(diff * diff), F32)

    return pl.pallas_call(
        body,
        name="outproj_loss",
        grid=(nj, ni),
        in_specs=[pl.BlockSpec((tm, D_MODEL), lambda j, i: (i, 0)),
                  pl.BlockSpec((D_MODEL, tn), lambda j, i: (0, j)),
                  pl.BlockSpec((tm, tn), lambda j, i: (i, j)),
                  pl.BlockSpec((tm, tn), lambda j, i: (i, j))],
        out_specs=[pl.BlockSpec((tm, tn), lambda j, i: (i, j)), pl.BlockSpec((tm, tn), lambda j, i: (i, j)),
                   pl.BlockSpec((1, 8, 128), lambda j, i: (i * nj + j, 0, 0))],
        out_shape=[jax.ShapeDtypeStruct((L, D_MODEL), F32), jax.ShapeDtypeStruct((L, D_MODEL), BF16),
                   jax.ShapeDtypeStruct((ni * nj, 8, 128), F32)],
        compiler_params=_cp(("parallel", "parallel")),
    )(merged, w_out, x, target)


def _merge_bwd(d_out_b, w_out, w_glu, og, o, yg, gpre, proj, b_glu, wa, ws):
    L = og.shape[0]
    tm = _tile(L, 256)

    def body(dout_ref, wo_ref, wg_ref, og_ref, o_ref, yg_ref, gp_ref, za0_ref, za1_ref, zs0_ref, zs1_ref, b_ref,
             wa_ref, ws_ref, do_ref, dza_ref, dzs_ref, dg_ref, dyg_ref, gwa_ref, gws_ref, gb_ref):
        i = pl.program_id(0)

        @pl.when(i == 0)
        def _():
            gwa_ref[...] = jnp.zeros_like(gwa_ref)
            gws_ref[...] = jnp.zeros_like(gws_ref)
            gb_ref[...] = jnp.zeros_like(gb_ref)

        dm = lax.dot_general(dout_ref[...], wo_ref[...], _NT, preferred_element_type=F32)
        za = jnp.concatenate([za0_ref[...], za1_ref[...]], axis=1)
        zs = jnp.concatenate([zs0_ref[...], zs1_ref[...]], axis=1)
        ogv, dma = og_ref[...].astype(F32), dm[:, :ATTN_W]
        ra = lax.rsqrt(jnp.mean(ogv * ogv, axis=-1, keepdims=True) + NORM_EPS)
        xh = ogv * ra
        gwa_ref[...] += jnp.sum(dma * xh, axis=0, keepdims=True)
        gx = dma * wa_ref[...]
        d_og = ra * (gx - xh * jnp.mean(gx * xh, axis=-1, keepdims=True))
        do_ref[...] = (d_og * _silu(za)).astype(BF16)
        dza_ref[...] = (d_og * o_ref[...].astype(F32) * _dsilu(za)).astype(BF16)
        ygv = yg_ref[...]
        sg = _sigmoid(gp_ref[...] + b_ref[...])
        y2 = ygv * sg
        sz = _silu(zs)
        os_ = y2 * sz
        dms = dm[:, ATTN_W:]
        rs = lax.rsqrt(jnp.mean(os_ * os_, axis=-1, keepdims=True) + NORM_EPS)
        xs = os_ * rs
        gws_ref[...] += jnp.sum(dms * xs, axis=0, keepdims=True)
        gxs = dms * ws_ref[...]
        d_os = rs * (gxs - xs * jnp.mean(gxs * xs, axis=-1, keepdims=True))
        dzs_ref[...] = (d_os * y2 * _dsilu(zs)).astype(BF16)
        d_y2 = d_os * sz
        d_g = d_y2 * ygv * sg * (1.0 - sg)
        d_g_b = d_g.astype(BF16)
        dg_ref[...] = d_g_b
        gb_ref[...] += jnp.sum(d_g, axis=0, keepdims=True)
        dyg_ref[...] = d_y2 * sg + lax.dot_general(d_g_b, wg_ref[...], _NT, preferred_element_type=F32)

    row = lambda w: pl.BlockSpec((1, w), lambda i: (0, 0))
    full = lambda w: pl.BlockSpec((tm, w), lambda i: (i, 0))
    half = lambda c: pl.BlockSpec((tm, 512), lambda i: (i, c))
    return pl.pallas_call(
        body,
        name="merge_bwd",
        grid=(L // tm,),
        in_specs=[full(D_MODEL), pl.BlockSpec((D_MODEL, D_MODEL), lambda i: (0, 0)),
                  pl.BlockSpec((SSM_W, SSM_W), lambda i: (0, 0)),
                  full(ATTN_W), full(ATTN_W), full(SSM_W), full(SSM_W),
                  half(3), half(4), half(7), half(8), row(SSM_W), row(ATTN_W), row(SSM_W)],
        out_specs=[full(ATTN_W), full(ATTN_W), full(SSM_W), full(SSM_W), full(SSM_W),
                   row(ATTN_W), row(SSM_W), row(SSM_W)],
        out_shape=[jax.ShapeDtypeStruct((L, ATTN_W), BF16), jax.ShapeDtypeStruct((L, ATTN_W), BF16),
                   jax.ShapeDtypeStruct((L, SSM_W), BF16), jax.ShapeDtypeStruct((L, SSM_W), BF16),
                   jax.ShapeDtypeStruct((L, SSM_W), F32),
                   jax.ShapeDtypeStruct((1, ATTN_W), F32), jax.ShapeDtypeStruct((1, SSM_W), F32),
                   jax.ShapeDtypeStruct((1, SSM_W), F32)],
        compiler_params=_cp(("arbitrary",)),
    )(d_out_b, w_out, w_glu, og, o, yg, gpre, proj, proj, proj, proj, b_glu.reshape(1, SSM_W), wa.reshape(1, ATTN_W),
      ws.reshape(1, SSM_W))


def _rms_bwd_x(x, norm_w, d_proj, wt_in, d_out, ride):
    L = x.shape[0]
    tm = _tile(L, 256)

    def body(x_ref, w_ref, dp_ref, wt_ref, do_ref, gx_ref, gw_ref):
        i = pl.program_id(0)

        @pl.when(i == 0)
        def _():
            gw_ref[...] = jnp.zeros_like(gw_ref)

        xv = x_ref[...]
        dh = jnp.dot(dp_ref[...], wt_ref[...], preferred_element_type=F32)
        r = lax.rsqrt(jnp.mean(xv * xv, axis=-1, keepdims=True) + NORM_EPS)
        xh = xv * r
        gw_ref[...] += jnp.sum(dh * xh, axis=0, keepdims=True)
        gx = dh * w_ref[...]
        gx_ref[...] = do_ref[...] + r * (gx - xh * jnp.mean(gx * xh, axis=-1, keepdims=True))

    blk = pl.BlockSpec((tm, D_MODEL), lambda i: (i, 0))
    row = pl.BlockSpec((1, D_MODEL), lambda i: (0, 0))
    dp_blk = pl.BlockSpec((tm, IN_W), lambda i: (i, 0))
    wt_blk = pl.BlockSpec((IN_W, D_MODEL), lambda i: (0, 0), pipeline_mode=pl.Buffered(1))
    return _call(body, "d_hn_rms_bwd_x", (L // tm,), [blk, row, dp_blk, wt_blk, blk], [blk, row],
                 [jax.ShapeDtypeStruct((L, D_MODEL), F32), jax.ShapeDtypeStruct((1, D_MODEL), F32)],
                 (x, norm_w.reshape(1, D_MODEL), d_proj, wt_in, d_out), ride=ride)


def _rope_table(positions):
    inv_freq = ROPE_THETA ** (-jnp.arange(0, HEAD_DIM, 2, dtype=F32) / HEAD_DIM)
    ang = positions.astype(F32)[:, None] * inv_freq
    sign = jnp.where(jnp.arange(128) % HEAD_DIM < HEAD_DIM // 2, -1.0, 1.0)
    return jnp.concatenate([jnp.tile(jnp.cos(ang), (1, 4)), jnp.tile(jnp.sin(ang), (1, 4)) * sign], axis=1)


def _step(x, positions, target, w, core, chip):
    small = {n: w[n] for n in _SMALL}
    tab = _rope_table(positions)
    mt_b, scat_b, ocat_b, a16 = _ssm_prep(small)
    perm = _chunk_perm()
    blocks = lambda t: t.reshape(N_DEV, t.shape[0] // N_DEV, t.shape[1])

    proj, hn, wt_in = _rms_inproj_gather(x, small["norm_w"], w["w_in"].T.astype(BF16), chip)
    wt_in = wt_in.reshape(IN_W, D_MODEL)
    q_rot, k_rot = _qk_prep(proj, tab, small["q_norm_w"], small["k_norm_w"])
    (og, o, lse), (w_glu,) = _attn_fwd(q_rot, k_rot, proj, small["sinks"],
                                       _gather_exchange([w["w_glu"].astype(BF16)]))
    (y, yg, hx), (w_out,) = _ssm_fwd(proj, perm, mt_b, scat_b, ocat_b, a16, small["d_skip"],
                                     _gather_exchange([w["w_out"].astype(BF16)]))
    w_glu, w_out = w_glu.reshape(SSM_W, SSM_W), w_out.reshape(D_MODEL, D_MODEL)
    merged, gpre = _merge(og, yg, w_glu, proj, small["b_glu"], small["attn_out_norm_w"], small["ssm_out_norm_w"])
    d_out, d_out_b, loss_parts = _outproj_loss(merged, w_out, x, target)
    loss = 0.5 * jnp.sum(loss_parts[:, 0, 0]) / D_MODEL

    g_w_out = blocks(_mm_tn(merged, d_out_b, "grad_w_out", tm=512))
    d_o, d_za, d_zs, d_g, d_yg, g_wa, g_ws, g_bglu = _merge_bwd(
        d_out_b, w_out, w_glu, og, o, yg, gpre, proj, small["b_glu"], small["attn_out_norm_w"],
        small["ssm_out_norm_w"])
    g_w_glu = blocks(_mm_tn(yg, d_g, "grad_w_glu", tm=256))
    (d_u, g_mt, g_scat, g_ocat, g_a16, g_dskip), (ra_out, ra_glu) = _ssm_bwd(
        d_yg, y, proj, hx, perm, mt_b, scat_b, ocat_b, a16, small["d_skip"], _pair_exchange([g_w_out, g_w_glu]))
    p_out = _pair_sum(g_w_out, ra_out, core, BF16, "pair_sum_out")
    p_glu = _pair_sum(g_w_glu, ra_glu, core, BF16, "pair_sum_glu")
    (d_q, d_k, d_v, g_sinks), (rb_out, rb_glu) = _attn_bwd(
        q_rot, k_rot, proj, small["sinks"], d_o, o, lse, _chip_exchange([p_out, p_glu]))
    d_proj, g_qw, g_kw = _qk_prep_bwd(proj, tab, small["q_norm_w"], small["k_norm_w"], d_q, d_k, d_v,
                                      d_za, d_u, d_zs)
    g_qw = g_qw[0, :HEAD_DIM] + g_qw[0, HEAD_DIM:]
    g_kw = g_kw[0, :HEAD_DIM] + g_kw[0, HEAD_DIM:]
    g_in_a = blocks(_mm_tn(d_proj, hn, "grad_w_in_a", tm=1152, panel=0))
    g_in_b, (ra_a,) = _mm_tn(d_proj, hn, "grad_w_in_b", tm=1152, panel=1, ride=_pair_exchange([g_in_a]))
    g_in_b = blocks(g_in_b)
    p_a = _pair_sum(g_in_a, ra_a, core, BF16, "pair_sum_in_a")
    (grad_x, g_nw), (rb_a, ra_b) = _rms_bwd_x(x, small["norm_w"], d_proj, wt_in, d_out,
                                              _both(_chip_exchange([p_a]), _pair_exchange([g_in_b])))
    p_b = _pair_sum(g_in_b, ra_b, core, BF16, "pair_sum_in_b")
    g_small, (rb_b,) = _ssm_prep_bwd(small, g_mt, g_scat, g_ocat, g_a16, _chip_exchange([p_b]))

    g_small.update(norm_w=g_nw.reshape(-1), q_norm_w=g_qw.reshape(-1), k_norm_w=g_kw.reshape(-1),
                   sinks=g_sinks[0, :N_HEADS], d_skip=g_dskip.reshape(-1), b_glu=g_bglu.reshape(-1),
                   attn_out_norm_w=g_wa.reshape(-1), ssm_out_norm_w=g_ws.reshape(-1))
    g_packed = _slab_all_reduce(_pack(g_small, loss).reshape(N_DEV, _PACK_ROWS // N_DEV, 128))
    g_packed = g_packed.reshape(_PACK_ROWS, 128)
    grads = _unpack(g_packed, w)
    parts = dict(w_in=([p_a, p_b], [rb_a, rb_b]), w_glu=([p_glu], [rb_glu]), w_out=([p_out], [rb_out]))
    return g_packed[_LOSS_ROW, 0], grad_x, grads, parts


_ANY = pl.BlockSpec(memory_space=pl.ANY)


class _Exchange:
    def __init__(self, arrays, out_shape, sems, start, finish, relay=None):
        self.arrays, self.out_shape, self.sems, self.start, self.finish = arrays, out_shape, sems, start, finish
        self.relay = relay if relay is not None else (lambda ins, outs, sems: None)


def _gather_exchange(blocks):
    n = len(blocks)

    def parts(ins, outs, sems):
        send_sems, recv_sems, local_sems = sems
        x, y, c = lax.axis_index("x"), lax.axis_index("y"), lax.axis_index("c")
        me, sibling = (x, y, c), (x, y, 1 - c)
        chips = [(1 - x, y), (x, 1 - y), (1 - x, 1 - y)]

        def slot(k, dev):
            return outs[k].at[4 * dev[0] + 2 * dev[1] + dev[2]]

        def copy(k, q, block, to, src=None):
            return pltpu.make_async_remote_copy(
                src_ref=slot(k, block) if src is None else src, dst_ref=slot(k, block),
                send_sem=send_sems.at[k, q], recv_sem=recv_sems.at[k, q], device_id=to, device_id_type=MESH)

        mine = [pltpu.make_async_copy(ins[k], slot(k, me), local_sems.at[k]) for k in range(n)]
        first = []
        for k in range(n):
            first.append(copy(k, 0, me, sibling, src=ins[k]))
            first += [copy(k, 1 + j, me, (*chip, c), src=ins[k]) for j, chip in enumerate(chips)]
        return me, sibling, chips, c, copy, mine, first

    def start(ins, outs, sems):
        *_, mine, first = parts(ins, outs, sems)
        for cp in mine + first:
            cp.start()

    def relay(ins, outs, sems):
        me, sibling, chips, c, copy, _, _ = parts(ins, outs, sems)
        for j, chip in enumerate(chips):
            for k in range(n):
                copy(k, 1 + j, (*chip, c), me).wait_recv()
                copy(k, 4 + j, (*chip, c), sibling).start()

    def finish(ins, outs, sems):
        me, sibling, chips, c, copy, mine, first = parts(ins, outs, sems)
        for k in range(n):
            copy(k, 0, sibling, me).wait_recv()
            for j, chip in enumerate(chips):
                copy(k, 4 + j, (*chip, 1 - c), me).wait_recv()
        for cp in first + [copy(k, 4 + j, (*chip, c), sibling) for k in range(n) for j, chip in enumerate(chips)]:
            cp.wait_send()
        for cp in mine:
            cp.wait()

    return _Exchange(blocks, [jax.ShapeDtypeStruct((N_DEV,) + b.shape, b.dtype) for b in blocks],
                     [pltpu.SemaphoreType.DMA((n, 7)), pltpu.SemaphoreType.DMA((n, 7)), pltpu.SemaphoreType.DMA((n,))],
                     start, finish, relay)


def _direct_exchange(arrays, out_lead, fan, route):
    n = len(arrays)

    def copies(ins, outs, sems):
        send_sems, recv_sems = sems
        legs = route(lax.axis_index("x"), lax.axis_index("y"), lax.axis_index("c"))
        return [pltpu.make_async_remote_copy(
            src_ref=ins[k].at[src], dst_ref=outs[k].at[q], send_sem=send_sems.at[k, q], recv_sem=recv_sems.at[k, q],
            device_id=to, device_id_type=MESH) for k in range(n) for src, q, to in legs]

    def start(ins, outs, sems):
        for cp in copies(ins, outs, sems):
            cp.start()

    def finish(ins, outs, sems):
        for cp in copies(ins, outs, sems):
            cp.wait()

    return _Exchange(arrays, [jax.ShapeDtypeStruct((out_lead,) + a.shape[1:], a.dtype) for a in arrays],
                     [pltpu.SemaphoreType.DMA((n, fan)), pltpu.SemaphoreType.DMA((n, fan))], start, finish)


def _pair_exchange(grads):
    return _direct_exchange(grads, 4, 4, lambda x, y, c: [(2 * chip + (1 - c), chip, (x, y, 1 - c))
                                                          for chip in range(4)])


def _chip_exchange(parts):
    def route(x, y, c):
        chips = [(1 - x, y), (x, 1 - y), (1 - x, 1 - y)]
        return [(2 * chip[0] + chip[1], q, (*chip, c)) for q, chip in enumerate(chips)]
    return _direct_exchange(parts, 3, 3, route)


def _both(ex1, ex2):
    n1, s1 = len(ex1.arrays), len(ex1.sems)

    def halves(ins, outs, sems):
        return (ins[:n1], outs[:n1], sems[:s1]), (ins[n1:], outs[n1:], sems[s1:])

    def start(ins, outs, sems):
        h1, h2 = halves(ins, outs, sems)
        ex1.start(*h1)
        ex2.start(*h2)

    def relay(ins, outs, sems):
        h1, h2 = halves(ins, outs, sems)
        ex1.relay(*h1)
        ex2.relay(*h2)

    def finish(ins, outs, sems):
        h1, h2 = halves(ins, outs, sems)
        ex1.finish(*h1)
        ex2.finish(*h2)

    return _Exchange(list(ex1.arrays) + list(ex2.arrays), list(ex1.out_shape) + list(ex2.out_shape),
                     list(ex1.sems) + list(ex2.sems), start, finish, relay)


def _call(body, name, grid, in_specs, out_specs, out_shape, args, scratch_shapes=(), ride=None):
    if ride is None:
        sem = ("arbitrary",) * len(grid)
        return pl.pallas_call(body, name=name, grid=grid, in_specs=in_specs, out_specs=out_specs, out_shape=out_shape,
                              scratch_shapes=list(scratch_shapes), compiler_params=_cp(sem))(*args), None
    n_in, n_out, n_scr, n_x = len(in_specs), len(out_specs), len(scratch_shapes), len(ride.arrays)

    def wrapped(*refs):
        ins, refs = refs[:n_in], refs[n_in:]
        x_in, refs = refs[:n_x], refs[n_x:]
        outs, refs = refs[:n_out], refs[n_out:]
        x_out, refs = refs[:n_x], refs[n_x:]
        scr, sems = refs[:n_scr], refs[n_scr:]
        step, total = pl.program_id(0), grid[0]
        for a in range(1, len(grid)):
            step, total = step * grid[a] + pl.program_id(a), total * grid[a]
        @pl.when(step == 0)
        def _():
            ride.start(x_in, x_out, sems)

        @pl.when(step == max(total - 2, 0))
        def _():
            ride.relay(x_in, x_out, sems)

        body(*ins, *outs, *scr)

        @pl.when(step == total - 1)
        def _():
            ride.finish(x_in, x_out, sems)

    res = pl.pallas_call(
        wrapped, name=name, grid=grid, in_specs=list(in_specs) + [_ANY] * n_x,
        out_specs=list(out_specs) + [_ANY] * n_x, out_shape=list(out_shape) + list(ride.out_shape),
        scratch_shapes=list(scratch_shapes) + list(ride.sems),
        compiler_params=_cp(("arbitrary",) * len(grid)))(*args, *ride.arrays)
    return res[:n_out], list(res[n_out:])


def _pair_sum(g, ra, core, out_dtype, name):
    _, r, C = g.shape
    tr = _tile(r, 576)

    def body(c_ref, g_ref, ra_ref, p_ref):
        p_ref[...] = (g_ref[...] + ra_ref[...]).astype(p_ref.dtype)

    return pl.pallas_call(
        body,
        name=name,
        grid_spec=pltpu.PrefetchScalarGridSpec(
            num_scalar_prefetch=1,
            grid=(4, r // tr),
            in_specs=[pl.BlockSpec((1, tr, C), lambda j, t, c_ref: (2 * j + c_ref[0], t, 0)),
                      pl.BlockSpec((1, tr, C), lambda j, t, c_ref: (j, t, 0))],
            out_specs=pl.BlockSpec((1, tr, C), lambda j, t, c_ref: (j, t, 0)),
        ),
        out_shape=jax.ShapeDtypeStruct((4, r, C), out_dtype),
        compiler_params=_cp(("parallel", "parallel")),
    )(core, g, ra)


def _slab_all_reduce(slab):
    _, r, lanes = slab.shape

    def body(s_ref, o_ref, ra, rb, ps, sems_a, sems_b, sems_c):
        x, y, c = lax.axis_index("x"), lax.axis_index("y"), lax.axis_index("c")
        chips = [(1 - x, y), (x, 1 - y), (1 - x, 1 - y)]
        pair = [pltpu.make_async_remote_copy(
            src_ref=s_ref.at[2 * k + (1 - c)], dst_ref=ra.at[k], send_sem=sems_a.at[0, k], recv_sem=sems_a.at[1, k],
            device_id=(x, y, 1 - c), device_id_type=MESH) for k in range(4)]
        for cp in pair:
            cp.start()
        for cp in pair:
            cp.wait()
        for k in range(4):
            ps[k] = s_ref[2 * k + c] + ra[k]
        cross = [pltpu.make_async_remote_copy(
            src_ref=ps.at[2 * ch[0] + ch[1]], dst_ref=rb.at[q], send_sem=sems_b.at[0, q], recv_sem=sems_b.at[1, q],
            device_id=(*ch, c), device_id_type=MESH) for q, ch in enumerate(chips)]
        for cp in cross:
            cp.start()
        for cp in cross:
            cp.wait()
        me = 4 * x + 2 * y + c
        o_ref[me] = ((ps[2 * x + y] + rb[0]) + rb[1]) + rb[2]
        flips = [(dx, dy, dc) for dx in (0, 1) for dy in (0, 1) for dc in (0, 1) if dx + dy + dc]
        spread = [pltpu.make_async_remote_copy(
            src_ref=o_ref.at[me], dst_ref=o_ref.at[me], send_sem=sems_c.at[0, q], recv_sem=sems_c.at[1, q],
            device_id=(x + dx - 2 * x * dx, y + dy - 2 * y * dy, c + dc - 2 * c * dc), device_id_type=MESH)
            for q, (dx, dy, dc) in enumerate(flips)]
        for cp in spread:
            cp.start()
        for q, (dx, dy, dc) in enumerate(flips):
            peer = 4 * (x + dx - 2 * x * dx) + 2 * (y + dy - 2 * y * dy) + (c + dc - 2 * c * dc)
            pltpu.make_async_remote_copy(
                src_ref=o_ref.at[peer], dst_ref=o_ref.at[peer], send_sem=sems_c.at[0, q], recv_sem=sems_c.at[1, q],
                device_id=(x, y, c), device_id_type=MESH).wait_recv()
        for cp in spread:
            cp.wait_send()

    whole = pl.BlockSpec(memory_space=pltpu.VMEM)
    return pl.pallas_call(
        body, name="slab_all_reduce", in_specs=[whole], out_specs=whole,
        out_shape=jax.ShapeDtypeStruct(slab.shape, F32),
        scratch_shapes=[pltpu.VMEM((4, r, lanes), F32), pltpu.VMEM((3, r, lanes), F32), pltpu.VMEM((4, r, lanes), F32),
                        pltpu.SemaphoreType.DMA((2, 4)), pltpu.SemaphoreType.DMA((2, 3)),
                        pltpu.SemaphoreType.DMA((2, 7))],
        compiler_params=_cp(),
    )(slab)


def _adamw_reduced(ps, rbs, chip, w, m, v, name):
    nh = len(ps)
    R, C = w.shape
    ch = C // nh
    tr = _tile(R, 288)
    nt = R // tr
    c1 = 1.0 - ADAM_B1 ** ADAM_STEP
    c2 = 1.0 - ADAM_B2 ** ADAM_STEP

    def body(c_ref, *refs):
        p_refs, rb_refs = refs[:nh], refs[nh:2 * nh]
        w_ref, m_ref, v_ref, g_ref, d_ref, nm_ref, nv_ref = refs[2 * nh:]
        for h in range(nh):
            @pl.when(pl.program_id(0) == h)
            def _(h=h):
                rb = rb_refs[h]
                gv = p_refs[h][0].astype(F32) + rb[0].astype(F32)
                gv = gv + rb[1].astype(F32)
                gv = gv + rb[2].astype(F32)
                nm = ADAM_B1 * m_ref[...] + (1.0 - ADAM_B1) * gv
                nv = ADAM_B2 * v_ref[...] + (1.0 - ADAM_B2) * (gv * gv)
                g_ref[...] = gv
                nm_ref[...] = nm
                nv_ref[...] = nv
                d_ref[...] = -ADAM_LR * ((nm / c1) / (jnp.sqrt(nv / c2) + ADAM_EPS) + ADAM_WD * w_ref[...])

    def held(h):
        return lambda hh, tt: jnp.where(hh == h, tt, jnp.where(hh < h, 0, nt - 1))

    p_specs = [pl.BlockSpec((1, tr, ch), lambda hh, tt, c_ref, f=held(h): (c_ref[0], f(hh, tt), 0))
               for h in range(nh)]
    rb_specs = [pl.BlockSpec((3, tr, ch), lambda hh, tt, c_ref, f=held(h): (0, f(hh, tt), 0)) for h in range(nh)]
    blk = pl.BlockSpec((tr, ch), lambda hh, tt, c_ref: (tt, hh))
    return pl.pallas_call(
        body,
        name=name,
        grid_spec=pltpu.PrefetchScalarGridSpec(
            num_scalar_prefetch=1, grid=(nh, nt), in_specs=p_specs + rb_specs + [blk] * 3, out_specs=[blk] * 4),
        out_shape=[jax.ShapeDtypeStruct((R, C), F32)] * 4,
        compiler_params=_cp(("arbitrary", "arbitrary")),
    )(chip, *ps, *rbs, w, m, v)


_SMALL = ("norm_w", "q_norm_w", "k_norm_w", "sinks", "a_re", "a_im", "log_step", "b_re", "b_im", "c_re", "c_im",
          "d_skip", "b_glu", "attn_out_norm_w", "ssm_out_norm_w")
_WEIGHTS = ("norm_w", "w_in", "q_norm_w", "k_norm_w", "sinks", "a_re", "a_im", "log_step", "b_re", "b_im", "c_re",
            "c_im", "d_skip", "w_glu", "b_glu", "attn_out_norm_w", "ssm_out_norm_w", "w_out")
_SMALL_2D = dict(norm_w=(1, 2048), q_norm_w=(1, 64), k_norm_w=(1, 64), sinks=(1, 16), a_re=(64, 64), a_im=(64, 64),
                 log_step=(1, 64), b_re=(1024, 64), b_im=(1024, 64), c_re=(1024, 64), c_im=(1024, 64),
                 d_skip=(1, 1024), b_glu=(1, 1024), attn_out_norm_w=(1, 1024), ssm_out_norm_w=(1, 1024))
_P_MINOR = ("b_re", "b_im")


def _flat_form(n, t):
    return t.transpose(0, 2, 1) if n in _P_MINOR else t


def _own_form(n, t, shape):
    if n in _P_MINOR:
        return t.reshape(shape[0], shape[2], shape[1]).transpose(0, 2, 1)
    return t.reshape(shape)


def _slab_rows(n):
    return -(-n // 1024) * 8


_PACK_ROWS = 2304


_LOSS_ROW = 2192


def _pack(d, loss):
    parts = []
    for n in _SMALL:
        flat = _flat_form(n, d[n]).reshape(-1).astype(F32)
        rows = _slab_rows(flat.shape[0])
        parts.append(jnp.pad(flat, (0, rows * 128 - flat.shape[0])).reshape(rows, 128))
    assert sum(p.shape[0] for p in parts) == _LOSS_ROW
    parts.append(jnp.pad(loss.reshape(1, 1), ((0, _PACK_ROWS - _LOSS_ROW - 1), (0, 127))))
    return jnp.concatenate(parts, axis=0)


def _unpack(packed, like):
    out, off = {}, 0
    for n in _SMALL:
        size = math.prod(like[n].shape)
        rows = _slab_rows(size)
        out[n] = _own_form(n, packed[off:off + rows].reshape(-1)[:size], like[n].shape)
        off += rows
    return out


def _adamw_small(g, w, m, v):
    c1 = 1.0 - ADAM_B1 ** ADAM_STEP
    c2 = 1.0 - ADAM_B2 ** ADAM_STEP
    k = len(_SMALL)

    def body(*refs):
        ins, outs = refs[:4 * k], refs[4 * k:]
        for j in range(k):
            gv, wv, mv, vv = (ins[q * k + j][...] for q in range(4))
            nm = ADAM_B1 * mv + (1.0 - ADAM_B1) * gv
            nv = ADAM_B2 * vv + (1.0 - ADAM_B2) * (gv * gv)
            outs[j][...] = -ADAM_LR * ((nm / c1) / (jnp.sqrt(nv / c2) + ADAM_EPS) + ADAM_WD * wv)
            outs[k + j][...] = nm
            outs[2 * k + j][...] = nv

    args = [_flat_form(n, d[n]).reshape(_SMALL_2D[n]) for d in (g, w, m, v) for n in _SMALL]
    shapes = [jax.ShapeDtypeStruct(_SMALL_2D[n], F32) for _ in range(3) for n in _SMALL]
    outs = pl.pallas_call(body, name="adamw_small", out_shape=shapes, compiler_params=_cp())(*args)
    res = []
    for q in range(3):
        res.append({n: _own_form(n, outs[q * k + j], w[n].shape) for j, n in enumerate(_SMALL)})
    return res


def kernel(x, positions, norm_w, w_in, q_norm_w, k_norm_w, sinks, a_re, a_im, log_step, b_re, b_im, c_re, c_im, d_skip, w_glu, b_glu, attn_out_norm_w, ssm_out_norm_w, w_out, loss_target, m_norm_w, m_w_in, m_q_norm_w, m_k_norm_w, m_sinks, m_a_re, m_a_im, m_log_step, m_b_re, m_b_im, m_c_re, m_c_im, m_d_skip, m_w_glu, m_b_glu, m_attn_out_norm_w, m_ssm_out_norm_w, m_w_out, v_norm_w, v_w_in, v_q_norm_w, v_k_norm_w, v_sinks, v_a_re, v_a_im, v_log_step, v_b_re, v_b_im, v_c_re, v_c_im, v_d_skip, v_w_glu, v_b_glu, v_attn_out_norm_w, v_ssm_out_norm_w, v_w_out):
    w = dict(norm_w=norm_w, w_in=w_in, q_norm_w=q_norm_w, k_norm_w=k_norm_w, sinks=sinks, a_re=a_re, a_im=a_im,
             log_step=log_step, b_re=b_re, b_im=b_im, c_re=c_re, c_im=c_im, d_skip=d_skip, w_glu=w_glu, b_glu=b_glu,
             attn_out_norm_w=attn_out_norm_w, ssm_out_norm_w=ssm_out_norm_w, w_out=w_out)
    m = dict(norm_w=m_norm_w, w_in=m_w_in, q_norm_w=m_q_norm_w, k_norm_w=m_k_norm_w, sinks=m_sinks, a_re=m_a_re,
             a_im=m_a_im, log_step=m_log_step, b_re=m_b_re, b_im=m_b_im, c_re=m_c_re, c_im=m_c_im, d_skip=m_d_skip,
             w_glu=m_w_glu, b_glu=m_b_glu, attn_out_norm_w=m_attn_out_norm_w, ssm_out_norm_w=m_ssm_out_norm_w,
             w_out=m_w_out)
    v = dict(norm_w=v_norm_w, w_in=v_w_in, q_norm_w=v_q_norm_w, k_norm_w=v_k_norm_w, sinks=v_sinks, a_re=v_a_re,
             a_im=v_a_im, log_step=v_log_step, b_re=v_b_re, b_im=v_b_im, c_re=v_c_re, c_im=v_c_im, d_skip=v_d_skip,
             w_glu=v_w_glu, b_glu=v_b_glu, attn_out_norm_w=v_attn_out_norm_w, ssm_out_norm_w=v_ssm_out_norm_w,
             w_out=v_w_out)
    core = lax.axis_index("c").astype(jnp.int32).reshape(1)
    chip = (2 * lax.axis_index("x") + lax.axis_index("y")).astype(jnp.int32).reshape(1)

    loss, grad_x, grads, parts = _step(x[0], positions[0], loss_target[0], w, core, chip)
    delta, new_m, new_v = {}, {}, {}
    for n in ("w_glu", "w_out"):
        grads[n], delta[n], new_m[n], new_v[n] = _adamw_reduced(*parts[n], chip, w[n], m[n], v[n], f"adamw_{n}")
    g_t, d_t, m_t, v_t = _adamw_reduced(*parts["w_in"], chip, w["w_in"].T, m["w_in"].T, v["w_in"].T, "adamw_w_in")
    grads["w_in"], delta["w_in"], new_m["w_in"], new_v["w_in"] = g_t.T, d_t.T, m_t.T, v_t.T
    d_s, m_s, v_s = _adamw_small(grads, w, m, v)
    delta.update(d_s)
    new_m.update(m_s)
    new_v.update(v_s)

    return (loss, grad_x[None], *[grads[n] for n in _WEIGHTS], *[delta[n] for n in _WEIGHTS],
            *[new_m[n] for n in _WEIGHTS], *[new_v[n] for n in _WEIGHTS])
```

```python
import math

import jax
import jax.numpy as jnp
from jax import lax
from jax.experimental import pallas as pl
from jax.experimental.pallas import tpu as pltpu

F32 = jnp.float32
BF16 = jnp.bfloat16

D_MODEL = 2048
ATTN_W = 1024
KV_W = 256
SSM_W = 1024
HEAD_DIM = 64
N_HEADS = 16
N_KV = 4
IN_W = 4608
BLOCK = 128
ROPE_THETA = 10000.0
NORM_EPS = 1e-6
SSM_G = 64
SSM_P = 64
SSM_H = 16
CHUNK = 16
CW = CHUNK * SSM_H
N_DEV = 8

ADAM_LR = 0.001
ADAM_B1 = 0.9
ADAM_B2 = 0.999
ADAM_EPS = 1e-08
ADAM_WD = 0.01
ADAM_STEP = 10

VMEM_LIMIT = 56 * 1024 * 1024
MESH = pl.DeviceIdType.MESH


def _cp(sem=None):
    if sem is None:
        return pltpu.CompilerParams(vmem_limit_bytes=VMEM_LIMIT)
    return pltpu.CompilerParams(vmem_limit_bytes=VMEM_LIMIT, dimension_semantics=sem)


def _sigmoid(x):
    return 0.5 * jnp.tanh(0.5 * x) + 0.5


def _silu(x):
    return x * _sigmoid(x)


def _dsilu(x):
    s = _sigmoid(x)
    return s * (1.0 + x * (1.0 - s))


_GELU_C = math.sqrt(2.0 / math.pi)


def _gelu(y):
    t = jnp.tanh(_GELU_C * (y + 0.044715 * y * y * y))
    return 0.5 * y * (1.0 + t)


def _dgelu(y):
    t = jnp.tanh(_GELU_C * (y + 0.044715 * y * y * y))
    return 0.5 * (1.0 + t) + 0.5 * y * (1.0 - t * t) * _GELU_C * (1.0 + 3.0 * 0.044715 * y * y)


def _tile(n, want):
    if n <= want:
        return n
    for t in range(want - want % 16, 0, -16):
        if n % t == 0:
            return t
    raise ValueError((n, want))


def _mm_tn(a, b, name, tm=512, tn=1024, ride=None, panel=None):
    (K, M), (K2, N) = a.shape, b.shape
    assert K == K2
    tm, tn = _tile(M, tm), _tile(N, tn)
    p0 = 0
    if panel is not None:
        p0, N = panel, tn

    def body(a_ref, b_ref, o_ref):
        o_ref[...] = lax.dot_general(a_ref[...].astype(BF16), b_ref[...].astype(BF16), _TN,
                                     preferred_element_type=F32)

    a_spec = pl.BlockSpec((K, tm), lambda j, i: (0, i))
    b_spec = pl.BlockSpec((K, tn), lambda j, i: (0, j + p0))
    o_spec = pl.BlockSpec((tm, tn), lambda j, i: (i, j))
    if ride is not None:
        (out,), landed = _call(body, name, (N // tn, M // tm), [a_spec, b_spec], [o_spec],
                               [jax.ShapeDtypeStruct((M, N), F32)], (a, b), ride=ride)
        return out, landed
    return pl.pallas_call(
        body,
        name=name,
        grid=(N // tn, M // tm),
        in_specs=[a_spec, b_spec],
        out_specs=o_spec,
        out_shape=jax.ShapeDtypeStruct((M, N), F32),
        compiler_params=_cp(("parallel", "parallel")),
    )(a, b)


_CHIP_ORDER = (0, 2, 1, 3)


def _rms_inproj_gather(x, norm_w, wt_shard, chip):
    L = x.shape[0]
    tm = _tile(L, 1024)
    ni = L // tm
    r = IN_W // N_DEV
    tn = 2 * r

    def body(chip_ref, x_ref, nw_ref, shard, proj_ref, hn_hbm, wt_hbm, hn_scr, w_scr, send_sems, recv_sems, loc_sems):
        jc, i = pl.program_id(0), pl.program_id(1)
        xx, yy, c = lax.axis_index("x"), lax.axis_index("y"), lax.axis_index("c")
        me, sibling = (xx, yy, c), (xx, yy, 1 - c)
        chips = [(1 - xx, yy), (xx, 1 - yy), (1 - xx, 1 - yy)]

        def slot(dev):
            return wt_hbm.at[4 * dev[0] + 2 * dev[1] + dev[2]]

        def copy(q, block, to, src=None):
            return pltpu.make_async_remote_copy(
                src_ref=slot(block) if src is None else src, dst_ref=slot(block),
                send_sem=send_sems.at[q], recv_sem=recv_sems.at[q], device_id=to, device_id_type=MESH)

        def rows_of(buf, core):
            return w_scr.at[buf, pl.ds(pl.multiple_of(core * r, 16), r)]

        mine = pltpu.make_async_copy(shard, slot(me), loc_sems.at[0])
        sends = [copy(0, me, sibling, src=shard)] + [copy(1 + j, me, (*ch, c), src=shard) for j, ch in enumerate(chips[:2])]
        relay_block = (xx + (1 - c) * (1 - 2 * xx), yy + c * (1 - 2 * yy), c)
        relay = copy(3, relay_block, (xx + c * (1 - 2 * xx), yy + (1 - c) * (1 - 2 * yy), c))
        first = jnp.logical_and(jc == 0, i == 0)

        @pl.when(first)
        def _():
            mine.start()
            for cp in sends:
                cp.start()
            own = pltpu.make_async_copy(shard, rows_of(0, c), loc_sems.at[1])
            own.start()
            copy(0, sibling, me).wait_recv()
            sib = pltpu.make_async_copy(slot(sibling), rows_of(0, 1 - c), loc_sems.at[2])
            sib.start()
            own.wait()
            sib.wait()

        def to_vmem(j, ch):
            pltpu.make_async_copy(slot((*ch, c)), rows_of((1 + j) % 2, c), loc_sems.at[1 + j]).start()

        @pl.when(jnp.logical_and(jc == 1, i == 0))
        def _():
            for j in range(2):
                copy(1 + j, (*chips[j], c), me).wait_recv()
                copy(4 + j, (*chips[j], c), sibling).start()
            relay.start()
            to_vmem(0, chips[0])

        @pl.when(jnp.logical_and(jc == 1, i == ni // 2))
        def _():
            to_vmem(1, chips[1])

        @pl.when(jnp.logical_and(jc == 2, i == ni // 2))
        def _():
            copy(3, (*chips[2], c), me).wait_recv()
            copy(6, (*chips[2], c), sibling).start()
            to_vmem(2, chips[2])

        for j, ch in enumerate(chips):
            @pl.when(jnp.logical_and(jc == 1 + j, i == 0))
            def _(j=j, ch=ch):
                buf = (1 + j) % 2
                copy(4 + j, (*ch, 1 - c), me).wait_recv()
                passed = pltpu.make_async_copy(slot((*ch, 1 - c)), rows_of(buf, 1 - c), loc_sems.at[4 + j])
                passed.start()
                pltpu.make_async_copy(slot((*ch, c)), rows_of(buf, c), loc_sems.at[1 + j]).wait()
                passed.wait()

        rows = pl.ds(pl.multiple_of(i * tm, tm), tm)

        @pl.when(jc == 0)
        def _():
            xv = x_ref[...]
            rstd = lax.rsqrt(jnp.mean(xv * xv, axis=-1, keepdims=True) + NORM_EPS)
            hn_scr[rows, :] = (xv * rstd * nw_ref[...]).astype(BF16)

        keep_hn = pltpu.make_async_copy(hn_scr, hn_hbm, loc_sems.at[7])

        @pl.when(jnp.logical_and(jc == 1, i == 0))
        def _():
            keep_hn.start()

        for buf in range(2):
            @pl.when(jc % 2 == buf)
            def _(buf=buf):
                proj_ref[...] = lax.dot_general(hn_scr[rows, :], w_scr[buf], _NT, preferred_element_type=F32)

        @pl.when(jnp.logical_and(jc == 3, i == ni - 1))
        def _():
            for cp in sends + [relay]:
                cp.wait_send()
            for j, ch in enumerate(chips):
                copy(4 + j, (*ch, c), sibling).wait_send()
            mine.wait()
            keep_hn.wait()

    def tile_of(jc, chip_ref):
        mask = jnp.where(jc == 1, _CHIP_ORDER[1], jnp.where(jc == 2, _CHIP_ORDER[2], jnp.where(jc == 3, _CHIP_ORDER[3], 0)))
        return jnp.bitwise_xor(chip_ref[0], mask)

    held = lambda jc, i: jnp.where(jc == 0, i, ni - 1)
    return pl.pallas_call(
        body,
        name="rms_inproj_gather",
        grid_spec=pltpu.PrefetchScalarGridSpec(
            num_scalar_prefetch=1,
            grid=(4, ni),
            in_specs=[pl.BlockSpec((tm, D_MODEL), lambda jc, i, ch: (held(jc, i), 0)),
                      pl.BlockSpec((1, D_MODEL), lambda jc, i, ch: (0, 0)), _ANY],
            out_specs=[pl.BlockSpec((tm, tn), lambda jc, i, ch: (i, tile_of(jc, ch))), _ANY, _ANY],
            scratch_shapes=[pltpu.VMEM((L, D_MODEL), BF16), pltpu.VMEM((2, tn, D_MODEL), BF16),
                            pltpu.SemaphoreType.DMA((7,)), pltpu.SemaphoreType.DMA((7,)), pltpu.SemaphoreType.DMA((8,))],
        ),
        out_shape=[jax.ShapeDtypeStruct((L, IN_W), F32), jax.ShapeDtypeStruct((L, D_MODEL), BF16),
                   jax.ShapeDtypeStruct((N_DEV, r, D_MODEL), BF16)],
        compiler_params=_cp(("arbitrary", "arbitrary")),
    )(chip, x, norm_w.reshape(1, D_MODEL), wt_shard)


def _seg_sum(v):
    a = lax.broadcasted_iota(jnp.int32, (128, 128), 0) // HEAD_DIM
    b = lax.broadcasted_iota(jnp.int32, (128, 128), 1) // HEAD_DIM
    ones = jnp.where(a == b, 1.0, 0.0).astype(BF16)
    hi = v.astype(BF16)
    lo = (v - hi.astype(F32)).astype(BF16)
    return jnp.dot(hi, ones, preferred_element_type=F32) + jnp.dot(lo, ones, preferred_element_type=F32)


def _rot_half(t):
    lane = lax.broadcasted_iota(jnp.int32, t.shape, 1)
    return jnp.where(lane % HEAD_DIM < HEAD_DIM // 2, pltpu.roll(t, 128 - HEAD_DIM // 2, 1),
                     pltpu.roll(t, HEAD_DIM // 2, 1))


def _norm_rope(raw, w, cos, sin):
    r = lax.rsqrt(_seg_sum(raw * raw) * (1.0 / HEAD_DIM) + NORM_EPS)
    tn = raw * r * w
    return r, tn * cos + _rot_half(tn) * sin


def _norm_rope_bwd(d_rot, raw, w, cos, sin):
    r = lax.rsqrt(_seg_sum(raw * raw) * (1.0 / HEAD_DIM) + NORM_EPS)
    d_tn = d_rot * cos + _rot_half(d_rot * sin)
    xh = raw * r
    gw = d_tn * w
    d_raw = r * (gw - xh * (_seg_sum(gw * xh) * (1.0 / HEAD_DIM)))
    return d_raw, d_tn * xh


def _band_mask2(has_prev, keys_on_rows=False):
    qd, kd = (1, 0) if keys_on_rows else (0, 1)
    qi = lax.broadcasted_iota(jnp.int32, (2 * BLOCK, 2 * BLOCK), qd) % BLOCK + BLOCK
    kj = lax.broadcasted_iota(jnp.int32, (2 * BLOCK, 2 * BLOCK), kd)
    rel = qi - kj
    return (rel >= 0) & (rel < BLOCK) & ((kj >= BLOCK) | has_prev)


def _half_tiles(pair):
    lo = lax.broadcasted_iota(jnp.int32, pair.shape, 1) < HEAD_DIM
    sw = pltpu.roll(pair, HEAD_DIM, 1)
    z = jnp.zeros_like(pair)
    return (jnp.where(lo, pair, z).astype(BF16), jnp.where(lo, z, sw).astype(BF16),
            jnp.where(lo, sw, z).astype(BF16), jnp.where(lo, z, pair).astype(BF16))


def _two_rows(top, bottom):
    row = lax.broadcasted_iota(jnp.int32, (2 * BLOCK, 1), 0)
    return jnp.where(row < BLOCK, top, bottom)


_SCALE = 1.0 / math.sqrt(HEAD_DIM)
_NT = (((1,), (1,)), ((), ()))
_NN = (((1,), (0,)), ((), ()))
_TN = (((0,), (0,)), ((), ()))


def _qk_prep(proj, tab, qw, kw):
    L = proj.shape[0]
    tm = _tile(L, 512)

    def body(q_ref, k_ref, t_ref, qw_ref, kw_ref, qo_ref, ko_ref):
        cos, sin = t_ref[:, :128], t_ref[:, 128:]
        for c in range(ATTN_W // 128):
            _, qr = _norm_rope(q_ref[:, c * 128:(c + 1) * 128], qw_ref[...], cos, sin)
            qo_ref[:, c * 128:(c + 1) * 128] = (qr * _SCALE).astype(BF16)
        for c in range(KV_W // 128):
            _, kr = _norm_rope(k_ref[:, c * 128:(c + 1) * 128], kw_ref[...], cos, sin)
            ko_ref[:, c * 128:(c + 1) * 128] = kr.astype(BF16)

    row = pl.BlockSpec((1, 128), lambda i: (0, 0))
    return pl.pallas_call(
        body,
        name="qk_prep",
        grid=(L // tm,),
        in_specs=[pl.BlockSpec((tm, ATTN_W), lambda i: (i, 0)), pl.BlockSpec((tm, KV_W), lambda i: (i, 4)),
                  pl.BlockSpec((tm, 256), lambda i: (i, 0)), row, row],
        out_specs=[pl.BlockSpec((tm, ATTN_W), lambda i: (i, 0)), pl.BlockSpec((tm, KV_W), lambda i: (i, 0))],
        out_shape=[jax.ShapeDtypeStruct((L, ATTN_W), BF16), jax.ShapeDtypeStruct((L, KV_W), BF16)],
        compiler_params=_cp(("parallel",)),
    )(proj, proj, tab, jnp.tile(qw, 2).reshape(1, 128), jnp.tile(kw, 2).reshape(1, 128))


def _group_tiles(g, kt, vt):
    a, b = divmod(g, 2)
    return kt[a][2 * b], kt[a][2 * b + 1], vt[a][2 * b], vt[a][2 * b + 1]


def _attn_fwd(q, k, proj, sinks, ride):
    L = proj.shape[0]
    nb = L // BLOCK

    def body(q_ref, kc_ref, kp_ref, vc_ref, vp_ref, z0_ref, z1_ref, sink_ref, og_ref, o_ref, lse_ref):
        i = pl.program_id(0)
        mask = _band_mask2(i > 0)
        z = jnp.concatenate([z0_ref[...], z1_ref[...]], axis=1)
        lane = lax.broadcasted_iota(jnp.int32, (BLOCK, 128), 1)
        kt = [_half_tiles(jnp.concatenate([kp_ref[:, a * 128:(a + 1) * 128], kc_ref[:, a * 128:(a + 1) * 128]],
                                          axis=0).astype(F32)) for a in range(2)]
        vt = [_half_tiles(jnp.concatenate([vp_ref[:, a * 128:(a + 1) * 128], vc_ref[:, a * 128:(a + 1) * 128]],
                                          axis=0)) for a in range(2)]
        lse_mat = jnp.zeros((BLOCK, 128), F32)
        pairs = []
        for g in range(N_KV):
            k_lo, k_hi, v_lo, v_hi = _group_tiles(g, kt, vt)
            q2 = jnp.concatenate([q_ref[:, 2 * g * 128:(2 * g + 1) * 128],
                                  q_ref[:, (2 * g + 1) * 128:(2 * g + 2) * 128]], axis=0)
            for half, (kh, vh) in enumerate(((k_lo, v_lo), (k_hi, v_hi))):
                pairs.append(dict(g=g, half=half, vh=vh, s=lax.dot_general(q2, kh, _NT, preferred_element_type=F32)))
        for pr in pairs:
            h_top, h_bot = 4 * pr["g"] + pr["half"], 4 * pr["g"] + 2 + pr["half"]
            s = jnp.where(mask, pr["s"], -1e30)
            sink = _two_rows(sink_ref[h_top], sink_ref[h_bot])
            m = jnp.maximum(jnp.max(s, axis=-1, keepdims=True), sink)
            e = jnp.exp(s - m)
            den = jnp.sum(e, axis=-1, keepdims=True) + jnp.exp(sink - m)
            pr["p_b"] = (e * (1.0 / den)).astype(BF16)
            lse = m + jnp.log(den)
            lse_mat = jnp.where(lane == h_top, lse[:BLOCK], lse_mat)
            lse_mat = jnp.where(lane == h_bot, lse[BLOCK:], lse_mat)
        outs = []
        for g in range(N_KV):
            acc = (jnp.dot(pairs[2 * g]["p_b"], pairs[2 * g]["vh"], preferred_element_type=F32)
                   + jnp.dot(pairs[2 * g + 1]["p_b"], pairs[2 * g + 1]["vh"], preferred_element_type=F32))
            outs += [acc[:BLOCK], acc[BLOCK:]]
        o = jnp.concatenate(outs, axis=1)
        o_ref[...] = o.astype(BF16)
        og_ref[...] = (o * _silu(z)).astype(BF16)
        lse_ref[...] = lse_mat

    prev = lambda i: jnp.maximum(i - 1, 0)
    return _call(
        body, "attn_fwd", (nb,),
        [pl.BlockSpec((BLOCK, ATTN_W), lambda i: (i, 0)),
         pl.BlockSpec((BLOCK, KV_W), lambda i: (i, 0)),
         pl.BlockSpec((BLOCK, KV_W), lambda i: (prev(i), 0)),
         pl.BlockSpec((BLOCK, KV_W), lambda i: (i, 5)),
         pl.BlockSpec((BLOCK, KV_W), lambda i: (prev(i), 5)),
         pl.BlockSpec((BLOCK, 512), lambda i: (i, 3)),
         pl.BlockSpec((BLOCK, 512), lambda i: (i, 4)),
         pl.BlockSpec(memory_space=pltpu.SMEM)],
        [pl.BlockSpec((BLOCK, ATTN_W), lambda i: (i, 0)),
         pl.BlockSpec((BLOCK, ATTN_W), lambda i: (i, 0)),
         pl.BlockSpec((BLOCK, 128), lambda i: (i, 0))],
        [jax.ShapeDtypeStruct((L, ATTN_W), BF16), jax.ShapeDtypeStruct((L, ATTN_W), BF16),
         jax.ShapeDtypeStruct((L, 128), F32)],
        (q, k, k, proj, proj, proj, proj, sinks), ride=ride)


def _attn_bwd(q, k, proj, sinks, d_o, o, lse, ride):
    L = proj.shape[0]
    nb = L // BLOCK

    def body(q_ref, kc_ref, kp_ref, vc_ref, vp_ref, do_ref, o_ref, lse_ref, sink_ref,
             dq_ref, dk_ref, dv_ref, gs_ref, ck_scr, cv_scr):
        i = pl.program_id(0)

        @pl.when(i == 0)
        def _():
            gs_ref[...] = jnp.zeros_like(gs_ref)
            ck_scr[...] = jnp.zeros_like(ck_scr)
            cv_scr[...] = jnp.zeros_like(cv_scr)

        @pl.when(i == nb)
        def _():
            dk_ref[...] = ck_scr[...]
            dv_ref[...] = cv_scr[...]

        @pl.when(i < nb)
        def _():
            mask = _band_mask2(i > 0, keys_on_rows=True)
            lane = lax.broadcasted_iota(jnp.int32, (1, 128), 1)
            lane2 = lax.broadcasted_iota(jnp.int32, (1, 2 * BLOCK), 1)
            lo = lax.broadcasted_iota(jnp.int32, (2 * BLOCK, 128), 1) < HEAD_DIM
            lse_t = lse_ref[...].T
            prod_all = do_ref[...].astype(F32) * o_ref[...].astype(F32)
            seg = (lax.broadcasted_iota(jnp.int32, (N_HEADS, ATTN_W), 1) // HEAD_DIM
                   == lax.broadcasted_iota(jnp.int32, (N_HEADS, ATTN_W), 0)).astype(BF16)
            prod_hi = prod_all.astype(BF16)
            prod_lo = (prod_all - prod_hi.astype(F32)).astype(BF16)
            delta_t = (lax.dot_general(seg, prod_hi, _NT, preferred_element_type=F32)
                       + lax.dot_general(seg, prod_lo, _NT, preferred_element_type=F32))
            kt = [_half_tiles(jnp.concatenate([kp_ref[:, a * 128:(a + 1) * 128], kc_ref[:, a * 128:(a + 1) * 128]],
                                              axis=0).astype(F32)) for a in range(2)]
            vt = [_half_tiles(jnp.concatenate([vp_ref[:, a * 128:(a + 1) * 128], vc_ref[:, a * 128:(a + 1) * 128]],
                                              axis=0)) for a in range(2)]
            gs = jnp.zeros((1, 128), F32)
            dq_parts = []
            dk_acc = [jnp.zeros((2 * BLOCK, 128), F32) for _ in range(2)]
            dv_acc = [jnp.zeros((2 * BLOCK, 128), F32) for _ in range(2)]
            pairs = []
            for g in range(N_KV):
                k_lo, k_hi, v_lo, v_hi = _group_tiles(g, kt, vt)
                t0, t1 = slice(2 * g * 128, (2 * g + 1) * 128), slice((2 * g + 1) * 128, (2 * g + 2) * 128)
                q2 = jnp.concatenate([q_ref[:, t0], q_ref[:, t1]], axis=0)
                do2_b = jnp.concatenate([do_ref[:, t0], do_ref[:, t1]], axis=0).astype(BF16)
                for half, (kh, vh) in enumerate(((k_lo, v_lo), (k_hi, v_hi))):
                    pairs.append(dict(g=g, half=half, kh=kh, q2=q2, do2_b=do2_b,
                                      s=lax.dot_general(kh, q2, _NT, preferred_element_type=F32),
                                      dp=lax.dot_general(vh, do2_b, _NT, preferred_element_type=F32)))
            for pr in pairs:
                h_top, h_bot = 4 * pr["g"] + pr["half"], 4 * pr["g"] + 2 + pr["half"]
                pick = lambda t: jnp.concatenate([t[h_top:h_top + 1, :], t[h_bot:h_bot + 1, :]], axis=1)
                lse, delta = pick(lse_t), pick(delta_t)
                sink = jnp.where(lane2 < BLOCK, sink_ref[h_top], sink_ref[h_bot])
                p = jnp.exp(jnp.where(mask, pr["s"], -1e30) - lse)
                pr["ds_b"] = (p * (pr["dp"] - delta)).astype(BF16)
                pr["p_b"] = p.astype(BF16)
                gsink = -jnp.exp(sink - lse) * delta
                gs = gs + jnp.where(lane == h_top, jnp.sum(jnp.where(lane2 < BLOCK, gsink, 0.0)), 0.0)
                gs = gs + jnp.where(lane == h_bot, jnp.sum(jnp.where(lane2 >= BLOCK, gsink, 0.0)), 0.0)
            for g in range(N_KV):
                a, b = divmod(g, 2)
                dq2 = jnp.zeros((2 * BLOCK, 128), F32)
                dk_h, dv_h = [], []
                for pr in pairs[2 * g:2 * g + 2]:
                    dq2 = dq2 + lax.dot_general(pr["ds_b"], pr["kh"], _TN, preferred_element_type=F32)
                    dk_h.append(jnp.dot(pr["ds_b"], pr["q2"], preferred_element_type=F32))
                    dv_h.append(jnp.dot(pr["p_b"], pr["do2_b"], preferred_element_type=F32))
                dq_parts += [dq2[:BLOCK], dq2[BLOCK:]]
                for acc, parts in ((dk_acc, dk_h), (dv_acc, dv_h)):
                    t = jnp.where(lo, parts[0], parts[1])
                    t = t + pltpu.roll(t, HEAD_DIM, 1)
                    acc[a] = acc[a] + jnp.where(lo == (b == 0), t, 0.0)
            dq_ref[...] = jnp.concatenate(dq_parts, axis=1)
            dk_full = jnp.concatenate(dk_acc, axis=1)
            dv_full = jnp.concatenate(dv_acc, axis=1)
            dk_ref[...] = ck_scr[...] + dk_full[:BLOCK]
            dv_ref[...] = cv_scr[...] + dv_full[:BLOCK]
            ck_scr[...] = dk_full[BLOCK:]
            cv_scr[...] = dv_full[BLOCK:]
            gs_ref[...] += gs

    cur = lambda i: jnp.minimum(i, nb - 1)
    prev = lambda i: jnp.maximum(jnp.minimum(i, nb - 1) - 1, 0)
    done = lambda i: jnp.maximum(i - 1, 0)
    bs = pl.BlockSpec
    return _call(
        body, "attn_bwd", (nb + 1,),
        [bs((BLOCK, ATTN_W), lambda i: (cur(i), 0)),
         bs((BLOCK, KV_W), lambda i: (cur(i), 0)), bs((BLOCK, KV_W), lambda i: (prev(i), 0)),
         bs((BLOCK, KV_W), lambda i: (cur(i), 5)), bs((BLOCK, KV_W), lambda i: (prev(i), 5)),
         bs((BLOCK, ATTN_W), lambda i: (cur(i), 0)), bs((BLOCK, ATTN_W), lambda i: (cur(i), 0)),
         bs((BLOCK, 128), lambda i: (cur(i), 0)), bs(memory_space=pltpu.SMEM)],
        [bs((BLOCK, ATTN_W), lambda i: (cur(i), 0)),
         bs((BLOCK, KV_W), lambda i: (done(i), 0)), bs((BLOCK, KV_W), lambda i: (done(i), 0)),
         bs((1, 128), lambda i: (0, 0))],
        [jax.ShapeDtypeStruct((L, ATTN_W), F32), jax.ShapeDtypeStruct((L, KV_W), F32),
         jax.ShapeDtypeStruct((L, KV_W), F32), jax.ShapeDtypeStruct((1, 128), F32)],
        (q, k, k, proj, proj, d_o, o, lse, sinks),
        [pltpu.VMEM((BLOCK, KV_W), F32), pltpu.VMEM((BLOCK, KV_W), F32)], ride)


def _qk_prep_bwd(proj, tab, qw, kw, d_q, d_k, d_v, d_za, d_u, d_zs):
    L = proj.shape[0]
    tm = _tile(L, 512)
    z0 = ATTN_W + 2 * KV_W

    def body(q_ref, k_ref, t_ref, qw_ref, kw_ref, dq_ref, dk_ref, dv_ref, dza_ref, du_ref, dzs_ref,
             out_ref, gq_ref, gk_ref):
        i = pl.program_id(0)

        @pl.when(i == 0)
        def _():
            gq_ref[...] = jnp.zeros_like(gq_ref)
            gk_ref[...] = jnp.zeros_like(gk_ref)

        cos, sin = t_ref[:, :128], t_ref[:, 128:]
        gq = jnp.zeros((1, 128), F32)
        gk = jnp.zeros((1, 128), F32)
        for c in range(ATTN_W // 128):
            cs = slice(c * 128, (c + 1) * 128)
            d_raw, gw = _norm_rope_bwd(dq_ref[:, cs] * _SCALE, q_ref[:, cs], qw_ref[...], cos, sin)
            out_ref[:, cs] = d_raw.astype(BF16)
            gq = gq + jnp.sum(gw, axis=0, keepdims=True)
        for c in range(KV_W // 128):
            cs = slice(c * 128, (c + 1) * 128)
            d_raw, gw = _norm_rope_bwd(dk_ref[:, cs], k_ref[:, cs], kw_ref[...], cos, sin)
            out_ref[:, ATTN_W + c * 128:ATTN_W + (c + 1) * 128] = d_raw.astype(BF16)
            gk = gk + jnp.sum(gw, axis=0, keepdims=True)
        out_ref[:, ATTN_W + KV_W:z0] = dv_ref[...].astype(BF16)
        out_ref[:, z0:z0 + ATTN_W] = dza_ref[...]
        out_ref[:, z0 + ATTN_W:z0 + ATTN_W + SSM_W] = du_ref[...].astype(BF16)
        out_ref[:, z0 + ATTN_W + SSM_W:] = dzs_ref[...]
        gq_ref[...] += gq
        gk_ref[...] += gk

    row = pl.BlockSpec((1, 128), lambda i: (0, 0))
    blk = lambda w, c: pl.BlockSpec((tm, w), lambda i: (i, c))
    return pl.pallas_call(
        body,
        name="qk_prep_bwd",
        grid=(L // tm,),
        in_specs=[blk(ATTN_W, 0), blk(KV_W, 4), blk(256, 0), row, row, blk(ATTN_W, 0), blk(KV_W, 0), blk(KV_W, 0),
                  blk(ATTN_W, 0), blk(SSM_W, 0), blk(SSM_W, 0)],
        out_specs=[blk(IN_W, 0), row, row],
        out_shape=[jax.ShapeDtypeStruct((L, IN_W), BF16), jax.ShapeDtypeStruct((1, 128), F32),
                   jax.ShapeDtypeStruct((1, 128), F32)],
        compiler_params=_cp(("arbitrary",)),
    )(proj, proj, tab, jnp.tile(qw, 2).reshape(1, 128), jnp.tile(kw, 2).reshape(1, 128), d_q, d_k, d_v,
      d_za, d_u, d_zs)


def _cmul(a, b):
    return a[0] * b[0] - a[1] * b[1], a[0] * b[1] + a[1] * b[0]


def _cmul_conj(a, b):
    return a[0] * b[0] + a[1] * b[1], a[1] * b[0] - a[0] * b[1]


def _cadd(a, b):
    return a[0] + b[0], a[1] + b[1]


def _dot3(a, b, dn):
    ah, bh = a.astype(BF16), b.astype(BF16)
    al, bl = (a - ah.astype(F32)).astype(BF16), (b - bh.astype(F32)).astype(BF16)
    d = lambda u, v: lax.dot_general(u, v, dn, preferred_element_type=F32)
    return d(ah, bh) + d(ah, bl) + d(al, bh)


def _s5_discretise(a_re, a_im, ls, cosx, sinx, bt):
    delta = jnp.exp(ls)
    er = jnp.exp(a_re * delta)
    lb = (er * cosx, er * sinx)
    den = a_re * a_re + a_im * a_im
    coef = _cmul_conj((lb[0] - 1.0, lb[1]), (a_re, a_im))
    coef = (coef[0] / den, coef[1] / den)
    return delta, lb, coef, den, _cmul(coef, bt)


def _powers(lb):
    pw = [(jnp.ones_like(lb[0]), jnp.zeros_like(lb[0]))]
    for _ in range(CHUNK):
        pw.append(_cmul(pw[-1], lb))
    return pw


def _block_rows(a, pw, idx):
    blocks = [_cmul(a, pw[i]) for i in idx]
    return (jnp.concatenate([b[0] for b in blocks], axis=-2), jnp.concatenate([b[1] for b in blocks], axis=-2))


def _block_rows_bwd(g, a, pw, idx, g_pw):
    g_a = (jnp.zeros_like(a[0]), jnp.zeros_like(a[0]))
    for j, i in enumerate(idx):
        gj = (g[0][..., j * SSM_H:(j + 1) * SSM_H, :], g[1][..., j * SSM_H:(j + 1) * SSM_H, :])
        g_a = _cadd(g_a, _cmul_conj(gj, pw[i]))
        gp = _cmul_conj(gj, a)
        g_pw[i] = _cadd(g_pw[i], (jnp.sum(gp[0], axis=-2, keepdims=True), jnp.sum(gp[1], axis=-2, keepdims=True)))
    return g_a


_IDX_S = [CHUNK - 1 - s for s in range(CHUNK)]
_IDX_C = list(range(CHUNK + 1))


def _prep_args(p):
    row = lambda t: t.reshape(SSM_G, 1, SSM_P)
    xi = p["a_im"] * jnp.exp(p["log_step"])[:, None]
    return (row(p["a_re"]), row(p["a_im"]), row(jnp.broadcast_to(p["log_step"][:, None], (SSM_G, SSM_P))),
            row(jnp.cos(xi)), row(jnp.sin(xi)), p["b_re"].transpose(0, 2, 1), p["b_im"].transpose(0, 2, 1),
            p["c_re"], p["c_im"])


PREP_GROUPS = 8


def _prep_specs():
    r1 = pl.BlockSpec((PREP_GROUPS, 1, SSM_P), lambda g: (g, 0, 0))
    r16 = pl.BlockSpec((PREP_GROUPS, SSM_H, SSM_P), lambda g: (g, 0, 0))
    return [r1] * 5 + [r16] * 4, r1, r16


def _ssm_prep(p):
    def one_group(q, are, aim, ls, cosx, sinx, btr, bti, cre, cim, mt_ref, s_ref, o_ref, a_ref):
        _, lb, _, _, bb = _s5_discretise(are[q], aim[q], ls[q], cosx[q], sinx[q], (btr[q], bti[q]))
        pw = _powers(lb)
        c = (cre[q], cim[q])
        sc = _block_rows(bb, pw, _IDX_S)
        cl = _block_rows(c, pw, _IDX_C)
        ok = (cl[0][:CW], cl[1][:CW])
        ot = (cl[0][SSM_H:], cl[1][SSM_H:])
        s_ref[q] = jnp.concatenate([sc[0], sc[1]], axis=1).astype(BF16)
        o_ref[q] = jnp.concatenate([ot[0], -ot[1]], axis=1).astype(BF16)
        a_ref[q] = jnp.concatenate([pw[CHUNK][0], pw[CHUNK][1]], axis=1)
        kt = _dot3(jnp.concatenate([bb[0], -bb[1]], axis=1), jnp.concatenate([ok[0], ok[1]], axis=1), _NT)
        lane = lax.broadcasted_iota(jnp.int32, kt.shape, 1)
        for s in range(CHUNK):
            blk = kt if s == 0 else jnp.where(lane >= SSM_H * s, pltpu.roll(kt, SSM_H * s, 1), 0.0)
            mt_ref[q, s * SSM_H:(s + 1) * SSM_H, :] = blk.astype(BF16)

    def body(*refs):
        for q in range(PREP_GROUPS):
            one_group(q, *refs)

    in_specs, r1, _ = _prep_specs()
    g3 = lambda r, c: pl.BlockSpec((PREP_GROUPS, r, c), lambda g: (g, 0, 0))
    return pl.pallas_call(
        body,
        name="ssm_prep",
        grid=(SSM_G // PREP_GROUPS,),
        in_specs=in_specs,
        out_specs=[g3(CW, CW), g3(CW, 2 * SSM_P), g3(CW, 2 * SSM_P), g3(1, 2 * SSM_P)],
        out_shape=[jax.ShapeDtypeStruct((SSM_G, CW, CW), BF16), jax.ShapeDtypeStruct((SSM_G, CW, 2 * SSM_P), BF16),
                   jax.ShapeDtypeStruct((SSM_G, CW, 2 * SSM_P), BF16),
                   jax.ShapeDtypeStruct((SSM_G, 1, 2 * SSM_P), F32)],
        compiler_params=_cp(("parallel",)),
    )(*_prep_args(p))


def _ssm_prep_bwd(p, g_mt, g_scat, g_ocat, g_a16, ride):
    def body(are, aim, ls, cosx, sinx, btr, bti, cre, cim, gmt_ref, gs_ref, go_ref, ga_ref,
             g_are, g_aim, g_ls, g_btr, g_bti, g_cre, g_cim, ga1_scr, gb1_scr):
        lam = (are[...], aim[...])
        bt = (btr[...], bti[...])
        delta, lb, coef, den, bb = _s5_discretise(lam[0], lam[1], ls[...], cosx[...], sinx[...], bt)
        pw = _powers(lb)
        c = (cre[...], cim[...])
        ok = _block_rows(c, pw, _IDX_C[:CHUNK])
        g_pw =[(jnp.zeros_like(lb[0]), jnp.zeros_like(lb[0])) for _ in range(CHUNK + 1)]
        lane = lax.broadcasted_iota(jnp.int32, (SSM_H, CW), 1)
        for q in range(PREP_GROUPS):
            g_kt = gmt_ref[q, :SSM_H, :]
            for s in range(1, CHUNK):
                blk = gmt_ref[q, s * SSM_H:(s + 1) * SSM_H, :]
                g_kt = g_kt + jnp.where(lane < CW - SSM_H * s, pltpu.roll(blk, CW - SSM_H * s, 1), 0.0)
            a1 = jnp.concatenate([bb[0][q], -bb[1][q]], axis=1)
            b1 = jnp.concatenate([ok[0][q], ok[1][q]], axis=1)
            ga1_scr[q] = _dot3(g_kt, b1, _NN)
            gb1_scr[q] = _dot3(g_kt, a1, _TN)
        g_a1, g_b1 = ga1_scr[...], gb1_scr[...]
        g_bb = (g_a1[..., :SSM_P], -g_a1[..., SSM_P:])
        gs = gs_ref[...]
        g_bb = _cadd(g_bb, _block_rows_bwd((gs[..., :SSM_P], gs[..., SSM_P:]), bb, pw, _IDX_S, g_pw))
        go = go_ref[...]
        pad = jnp.zeros_like(go[..., :SSM_H, :SSM_P])
        g_cl = (jnp.concatenate([g_b1[..., :SSM_P], pad], axis=-2) + jnp.concatenate([pad, go[..., :SSM_P]], axis=-2),
                jnp.concatenate([g_b1[..., SSM_P:], pad], axis=-2) - jnp.concatenate([pad, go[..., SSM_P:]], axis=-2))
        g_c = _block_rows_bwd(g_cl, c, pw, _IDX_C, g_pw)
        ga = ga_ref[...]
        g_pw[CHUNK] = _cadd(g_pw[CHUNK], (ga[..., :SSM_P], ga[..., SSM_P:]))
        g_lb = (jnp.zeros_like(lb[0]), jnp.zeros_like(lb[0]))
        for l in range(CHUNK - 1, -1, -1):
            g_lb = _cadd(g_lb, _cmul_conj(g_pw[l + 1], pw[l]))
            g_pw[l] = _cadd(g_pw[l], _cmul_conj(g_pw[l + 1], lb))
        g_bt = _cmul_conj(g_bb, coef)
        gc = _cmul_conj(g_bb, bt)
        g_coef = (jnp.sum(gc[0], axis=-2, keepdims=True), jnp.sum(gc[1], axis=-2, keepdims=True))
        lam_den = (lam[0] / den, lam[1] / den)
        g_lb = _cadd(g_lb, _cmul(g_coef, lam_den))
        t = _cmul(_cmul_conj(g_coef, coef), lam_den)
        g_x = _cmul_conj(g_lb, lb)
        g_are[...] = g_x[0] * delta - t[0]
        g_aim[...] = g_x[1] * delta - t[1]
        g_ls[...] = (g_x[0] * lam[0] + g_x[1] * lam[1]) * delta
        g_btr[...] = g_bt[0]
        g_bti[...] = g_bt[1]
        g_cre[...] = g_c[0]
        g_cim[...] = g_c[1]

    in_specs, r1, r16 = _prep_specs()
    g3 = lambda r, c: pl.BlockSpec((PREP_GROUPS, r, c), lambda g: (g, 0, 0))
    rows = jax.ShapeDtypeStruct((SSM_G, 1, SSM_P), F32)
    mats = jax.ShapeDtypeStruct((SSM_G, SSM_H, SSM_P), F32)
    (g_are, g_aim, g_ls, g_btr, g_bti, g_cre, g_cim), landed = _call(
        body, "ssm_prep_bwd", (SSM_G // PREP_GROUPS,),
        in_specs + [g3(CW, CW), g3(CW, 2 * SSM_P), g3(CW, 2 * SSM_P), g3(1, 2 * SSM_P)],
        [r1] * 3 + [r16] * 4, [rows] * 3 + [mats] * 4, (*_prep_args(p), g_mt, g_scat, g_ocat, g_a16),
        [pltpu.VMEM((PREP_GROUPS, SSM_H, 2 * SSM_P), F32), pltpu.VMEM((PREP_GROUPS, CW, 2 * SSM_P), F32)], ride)
    grads = dict(a_re=g_are.reshape(SSM_G, SSM_P), a_im=g_aim.reshape(SSM_G, SSM_P),
                 log_step=jnp.sum(g_ls.reshape(SSM_G, SSM_P), axis=1),
                 b_re=g_btr.transpose(0, 2, 1), b_im=g_bti.transpose(0, 2, 1), c_re=g_cre, c_im=g_cim)
    return grads, landed


def _cmul_const(xv, ar, ai):
    return xv * ar + pltpu.roll(xv, SSM_P, 1) * ai


def _chunk_scan(inc, a_row, reverse):
    n = inc.shape[0]
    lane = lax.broadcasted_iota(jnp.int32, (1, 2 * SSM_P), 1)
    row = lax.broadcasted_iota(jnp.int32, inc.shape, 0)
    sign = jnp.where(lane < SSM_P, -1.0, 1.0)
    ar = jnp.where(lane < SSM_P, a_row, pltpu.roll(a_row, SSM_P, 1))
    ai = jnp.where(lane < SSM_P, pltpu.roll(a_row, SSM_P, 1), a_row)
    if reverse:
        ai = -ai
    xv = inc
    s = 1
    while s < n:
        if reverse:
            sh = jnp.where(row < n - s, pltpu.roll(xv, n - s, 0), 0.0)
        else:
            sh = jnp.where(row >= s, pltpu.roll(xv, s, 0), 0.0)
        xv = xv + _cmul_const(sh, ar, ai * sign)
        ar, ai = ar * ar - ai * ai, 2.0 * ar * ai
        s *= 2
    return xv


def _shift_rows(xv, reverse):
    n = xv.shape[0]
    row = lax.broadcasted_iota(jnp.int32, xv.shape, 0)
    if reverse:
        return jnp.where(row < n - 1, pltpu.roll(xv, n - 1, 0), 0.0)
    return jnp.where(row >= 1, pltpu.roll(xv, 1, 0), 0.0)


GB = 128 // SSM_H
U_COL0 = (ATTN_W + 2 * KV_W + ATTN_W) // 128


HALF = CHUNK // 2


def _chunk_perm():
    r = jnp.arange(HALF * 128)
    t, g8, h = r // 128, (r % 128) // SSM_H, r % SSM_H
    return ((g8 * 128 + t * SSM_H + h)[:, None] == jnp.arange(GB * 128)[None, :]).astype(BF16)


def _load_perm(p_hbm, p_scr, sem):
    @pl.when(pl.program_id(0) == 0)
    def _():
        cp = pltpu.make_async_copy(p_hbm, p_scr, sem)
        cp.start()
        cp.wait()


def _rows_to_chunks(pieces, perm):
    halves = [jnp.dot(jnp.concatenate(pieces[k * HALF:(k + 1) * HALF], axis=1).astype(BF16), perm,
                      preferred_element_type=F32).astype(BF16) for k in range(2)]
    return [jnp.concatenate([hv[:, g * 128:(g + 1) * 128] for hv in halves], axis=1) for g in range(GB)]


def _chunks_to_rows(groups, perm, two_pass):
    pieces = []
    for k in range(2):
        v = jnp.concatenate([gv[:, k * 128:(k + 1) * 128] for gv in groups], axis=1)
        hi = v.astype(BF16)
        out = lax.dot_general(hi, perm, _NT, preferred_element_type=F32)
        if two_pass:
            lo = (v - hi.astype(F32)).astype(BF16)
            out = out + lax.dot_general(lo, perm, _NT, preferred_element_type=F32)
        pieces += [out[:, t * 128:(t + 1) * 128] for t in range(HALF)]
    return pieces


def _ssm_fwd(proj, perm, mt, scat, ocat, a16, d_skip, ride):
    L = proj.shape[0]
    nc = L // CHUNK

    def body(u_ref, p_hbm, mt_ref, s_ref, o_ref, a_ref, d_ref, y_ref, yg_ref, h_ref, p_scr, sem):
        _load_perm(p_hbm, p_scr, sem)
        perm = p_scr[...]
        rows = [pl.ds(t, nc, stride=CHUNK) for t in range(CHUNK)]
        us = [u_ref[r, :] for r in rows]
        ua = _rows_to_chunks(us, perm)
        incs = [jnp.dot(ua[g], s_ref[g], preferred_element_type=F32) for g in range(GB)]
        intra = [jnp.dot(ua[g], mt_ref[g], preferred_element_type=F32) for g in range(GB)]
        hxs = [_shift_rows(_chunk_scan(incs[g], a_ref[g], False), False) for g in range(GB)]
        ys = []
        for g in range(GB):
            h_ref[g] = hxs[g]
            ys.append(intra[g] + lax.dot_general(hxs[g].astype(BF16), o_ref[g], _NT, preferred_element_type=F32))
        yp = _chunks_to_rows(ys, perm, True)
        for t, r in enumerate(rows):
            y = yp[t] + d_ref[...] * us[t]
            y_ref[r, :] = y
            yg_ref[r, :] = _gelu(y)

    g3 = lambda r, c: pl.BlockSpec((GB, r, c), lambda g: (g, 0, 0))
    col = pl.BlockSpec((L, 128), lambda g: (0, g))
    return _call(
        body, "ssm_fwd", (SSM_G // GB,),
        [pl.BlockSpec((L, 128), lambda g: (0, U_COL0 + g)), _ANY,
         g3(CW, CW), g3(CW, 2 * SSM_P), g3(CW, 2 * SSM_P), g3(1, 2 * SSM_P),
         pl.BlockSpec((1, 128), lambda g: (0, g))],
        [col, col, g3(nc, 2 * SSM_P)],
        [jax.ShapeDtypeStruct((L, SSM_W), F32), jax.ShapeDtypeStruct((L, SSM_W), F32),
         jax.ShapeDtypeStruct((SSM_G, nc, 2 * SSM_P), F32)],
        (proj, perm, mt, scat, ocat, a16, d_skip.reshape(1, SSM_W)),
        [pltpu.VMEM((HALF * 128, GB * 128), BF16), pltpu.SemaphoreType.DMA], ride)


def _ssm_bwd(d_yg, y, proj, hx, perm, mt, scat, ocat, a16, d_skip, ride):
    L = proj.shape[0]
    nc = L // CHUNK

    def body(dg_ref, y_ref, u_ref, h_ref, p_hbm, mt_ref, s_ref, o_ref, a_ref, d_ref,
             du_ref, gmt_ref, gs_ref, go_ref, ga_ref, gd_ref, p_scr, sem):
        _load_perm(p_hbm, p_scr, sem)
        perm = p_scr[...]
        rows = [pl.ds(t, nc, stride=CHUNK) for t in range(CHUNK)]
        us = [u_ref[r, :] for r in rows]
        dys = [dg_ref[r, :] * _dgelu(y_ref[r, :]) for r in rows]
        gd = jnp.zeros((1, 128), F32)
        for uv, dy in zip(us, dys):
            gd = gd + jnp.sum(dy * uv, axis=0, keepdims=True)
        gd_ref[...] = gd
        ua = _rows_to_chunks(us, perm)
        dya = _rows_to_chunks(dys, perm)
        lane = lax.broadcasted_iota(jnp.int32, (1, 2 * SSM_P), 1)
        dhs = [jnp.dot(dya[g], o_ref[g], preferred_element_type=F32) for g in range(GB)]
        intra = [lax.dot_general(dya[g], mt_ref[g], _NT, preferred_element_type=F32) for g in range(GB)]
        for g in range(GB):
            gmt_ref[g] = lax.dot_general(ua[g], dya[g], _TN, preferred_element_type=F32)
            go_ref[g] = lax.dot_general(dya[g], h_ref[g].astype(BF16), _TN, preferred_element_type=F32)
        dincs = [_shift_rows(_chunk_scan(dhs[g], a_ref[g], True), True) for g in range(GB)]
        dus = []
        for g in range(GB):
            dinc, hx_v = dincs[g], h_ref[g]
            dinc_b = dinc.astype(BF16)
            dus.append(intra[g] + lax.dot_general(dinc_b, s_ref[g], _NT, preferred_element_type=F32))
            gs_ref[g] = lax.dot_general(ua[g], dinc_b, _TN, preferred_element_type=F32)
            p1 = dinc * hx_v
            p2 = pltpu.roll(dinc, SSM_P, 1) * hx_v
            t1 = jnp.sum(p1 + pltpu.roll(p1, SSM_P, 1), axis=0, keepdims=True)
            t2 = jnp.sum(p2 - pltpu.roll(p2, SSM_P, 1), axis=0, keepdims=True)
            ga_ref[g] = jnp.where(lane < SSM_P, t1, pltpu.roll(t2, SSM_P, 1))
        dup = _chunks_to_rows(dus, perm, False)
        for t, r in enumerate(rows):
            du_ref[r, :] = dup[t] + d_ref[...] * dys[t]

    g3 = lambda r, c: pl.BlockSpec((GB, r, c), lambda g: (g, 0, 0))
    col = pl.BlockSpec((L, 128), lambda g: (0, g))
    row = pl.BlockSpec((1, 128), lambda g: (0, g))
    return _call(
        body, "ssm_bwd", (SSM_G // GB,),
        [col, col, pl.BlockSpec((L, 128), lambda g: (0, U_COL0 + g)), g3(nc, 2 * SSM_P), _ANY,
         g3(CW, CW), g3(CW, 2 * SSM_P), g3(CW, 2 * SSM_P), g3(1, 2 * SSM_P), row],
        [col, g3(CW, CW), g3(CW, 2 * SSM_P), g3(CW, 2 * SSM_P), g3(1, 2 * SSM_P), row],
        [jax.ShapeDtypeStruct((L, SSM_W), F32), jax.ShapeDtypeStruct((SSM_G, CW, CW), F32),
         jax.ShapeDtypeStruct((SSM_G, CW, 2 * SSM_P), F32), jax.ShapeDtypeStruct((SSM_G, CW, 2 * SSM_P), F32),
         jax.ShapeDtypeStruct((SSM_G, 1, 2 * SSM_P), F32), jax.ShapeDtypeStruct((1, SSM_W), F32)],
        (d_yg, y, proj, hx, perm, mt, scat, ocat, a16, d_skip.reshape(1, SSM_W)),
        [pltpu.VMEM((HALF * 128, GB * 128), BF16), pltpu.SemaphoreType.DMA], ride)


def _merge(og, yg, w_glu, proj, b_glu, wa, ws):
    L = og.shape[0]
    tm = _tile(L, 256)

    def body(og_ref, yg_ref, wg_ref, z0_ref, z1_ref, b_ref, wa_ref, ws_ref, m_ref, gp_ref):
        zs = jnp.concatenate([z0_ref[...], z1_ref[...]], axis=1)
        ygv = yg_ref[...]
        gpre = jnp.dot(ygv.astype(BF16), wg_ref[...], preferred_element_type=F32)
        gp_ref[...] = gpre
        os_ = ygv * _sigmoid(gpre + b_ref[...]) * _silu(zs)
        ogv = og_ref[...].astype(F32)
        ra = lax.rsqrt(jnp.mean(ogv * ogv, axis=-1, keepdims=True) + NORM_EPS)
        rs = lax.rsqrt(jnp.mean(os_ * os_, axis=-1, keepdims=True) + NORM_EPS)
        m_ref[:, :ATTN_W] = (ogv * ra * wa_ref[...]).astype(BF16)
        m_ref[:, ATTN_W:] = (os_ * rs * ws_ref[...]).astype(BF16)

    row = lambda w: pl.BlockSpec((1, w), lambda i: (0, 0))
    return pl.pallas_call(
        body,
        name="merge",
        grid=(L // tm,),
        in_specs=[pl.BlockSpec((tm, ATTN_W), lambda i: (i, 0)), pl.BlockSpec((tm, SSM_W), lambda i: (i, 0)),
                  pl.BlockSpec((SSM_W, SSM_W), lambda i: (0, 0)),
                  pl.BlockSpec((tm, 512), lambda i: (i, 7)), pl.BlockSpec((tm, 512), lambda i: (i, 8)),
                  row(SSM_W), row(ATTN_W), row(SSM_W)],
        out_specs=[pl.BlockSpec((tm, D_MODEL), lambda i: (i, 0)), pl.BlockSpec((tm, SSM_W), lambda i: (i, 0))],
        out_shape=[jax.ShapeDtypeStruct((L, D_MODEL), BF16), jax.ShapeDtypeStruct((L, SSM_W), F32)],
        compiler_params=_cp(("parallel",)),
    )(og, yg, w_glu, proj, proj, b_glu.reshape(1, SSM_W), wa.reshape(1, ATTN_W), ws.reshape(1, SSM_W))


def _outproj_loss(merged, w_out, x, target):
    L = x.shape[0]
    tm = _tile(L, 256)
    ni = L // tm
    ring = 3

    def body(m_ref, w_ref, x_hbm, t_hbm, d_ref, db_ref, l_ref, x_buf, t_buf, sems):
        i = pl.program_id(0)

        def fetch(step, slot):
            rows = pl.ds(pl.multiple_of(step * tm, tm), tm)
            return (pltpu.make_async_copy(x_hbm.at[rows], x_buf.at[slot], sems.at[0, slot]),
                    pltpu.make_async_copy(t_hbm.at[rows], t_buf.at[slot], sems.at[1, slot]))

        @pl.when(i == 0)
        def _():
            for s in range(min(ring - 1, ni)):
                for cp in fetch(s, s):
                    cp.start()

        ahead = i + (ring - 1)

        @pl.when(ahead < ni)
        def _():
            for cp in fetch(ahead, ahead % ring):
                cp.start()

        slot = i % ring
        for cp in fetch(i, slot):
            cp.wait()
        out = x_buf[slot] + jnp.dot(m_ref[...], w_ref[...], preferred_element_type=F32)
        diff = out - t_buf[slot]
        d = diff * (1.0 / D_MODEL)
        d_ref[...] = d
        db_ref[...] = d.astype(BF16)
        l_ref[...] = jnp.full((1, 8, 128), jnp.sum(diff * diff), F32)

    blk = pl.BlockSpec((tm, D_MODEL), lambda i: (i, 0))
    return pl.pallas_call(
        body,
        name="outproj_loss",
        grid=(ni,),
        in_specs=[blk, pl.BlockSpec((D_MODEL, D_MODEL), lambda i: (0, 0)), _ANY, _ANY],
        out_specs=[blk, blk, pl.BlockSpec((1, 8, 128), lambda i: (i, 0, 0))],
        out_shape=[jax.ShapeDtypeStruct((L, D_MODEL), F32), jax.ShapeDtypeStruct((L, D_MODEL), BF16),
                   jax.ShapeDtypeStruct((ni, 8, 128), F32)],
        scratch_shapes=[pltpu.VMEM((ring, tm, D_MODEL), F32), pltpu.VMEM((ring, tm, D_MODEL), F32),
                        pltpu.SemaphoreType.DMA((2, ring))],
        compiler_params=_cp(("arbitrary",)),
    )(merged, w_out, x, target)


def _merge_bwd(d_out_b, w_out, w_glu, og, o, yg, gpre, proj, b_glu, wa, ws):
    L = og.shape[0]
    tm = _tile(L, 256)

    def body(dout_ref, wo_ref, wg_ref, og_ref, o_ref, yg_ref, gp_ref, za0_ref, za1_ref, zs0_ref, zs1_ref, b_ref,
             wa_ref, ws_ref, do_ref, dza_ref, dzs_ref, dg_ref, dyg_ref, gwa_ref, gws_ref, gb_ref):
        i = pl.program_id(0)

        @pl.when(i == 0)
        def _():
            gwa_ref[...] = jnp.zeros_like(gwa_ref)
            gws_ref[...] = jnp.zeros_like(gws_ref)
            gb_ref[...] = jnp.zeros_like(gb_ref)

        dm = lax.dot_general(dout_ref[...], wo_ref[...], _NT, preferred_element_type=F32)
        za = jnp.concatenate([za0_ref[...], za1_ref[...]], axis=1)
        zs = jnp.concatenate([zs0_ref[...], zs1_ref[...]], axis=1)
        ogv, dma = og_ref[...].astype(F32), dm[:, :ATTN_W]
        ra = lax.rsqrt(jnp.mean(ogv * ogv, axis=-1, keepdims=True) + NORM_EPS)
        xh = ogv * ra
        gwa_ref[...] += jnp.sum(dma * xh, axis=0, keepdims=True)
        gx = dma * wa_ref[...]
        d_og = ra * (gx - xh * jnp.mean(gx * xh, axis=-1, keepdims=True))
        do_ref[...] = (d_og * _silu(za)).astype(BF16)
        dza_ref[...] = (d_og * o_ref[...].astype(F32) * _dsilu(za)).astype(BF16)
        ygv = yg_ref[...]
        sg = _sigmoid(gp_ref[...] + b_ref[...])
        y2 = ygv * sg
        sz = _silu(zs)
        os_ = y2 * sz
        dms = dm[:, ATTN_W:]
        rs = lax.rsqrt(jnp.mean(os_ * os_, axis=-1, keepdims=True) + NORM_EPS)
        xs = os_ * rs
        gws_ref[...] += jnp.sum(dms * xs, axis=0, keepdims=True)
        gxs = dms * ws_ref[...]
        d_os = rs * (gxs - xs * jnp.mean(gxs * xs, axis=-1, keepdims=True))
        dzs_ref[...] = (d_os * y2 * _dsilu(zs)).astype(BF16)
        d_y2 = d_os * sz
        d_g = d_y2 * ygv * sg * (1.0 - sg)
        d_g_b = d_g.astype(BF16)
        dg_ref[...] = d_g_b
        gb_ref[...] += jnp.sum(d_g, axis=0, keepdims=True)
        dyg_ref[...] = d_y2 * sg + lax.dot_general(d_g_b, wg_ref[...], _NT, preferred_element_type=F32)

    row = lambda w: pl.BlockSpec((1, w), lambda i: (0, 0))
    full = lambda w: pl.BlockSpec((tm, w), lambda i: (i, 0))
    half = lambda c: pl.BlockSpec((tm, 512), lambda i: (i, c))
    return pl.pallas_call(
        body,
        name="merge_bwd",
        grid=(L // tm,),
        in_specs=[full(D_MODEL), pl.BlockSpec((D_MODEL, D_MODEL), lambda i: (0, 0)),
                  pl.BlockSpec((SSM_W, SSM_W), lambda i: (0, 0)),
                  full(ATTN_W), full(ATTN_W), full(SSM_W), full(SSM_W),
                  half(3), half(4), half(7), half(8), row(SSM_W), row(ATTN_W), row(SSM_W)],
        out_specs=[full(ATTN_W), full(ATTN_W), full(SSM_W), full(SSM_W), full(SSM_W),
                   row(ATTN_W), row(SSM_W), row(SSM_W)],
        out_shape=[jax.ShapeDtypeStruct((L, ATTN_W), BF16), jax.ShapeDtypeStruct((L, ATTN_W), BF16),
                   jax.ShapeDtypeStruct((L, SSM_W), BF16), jax.ShapeDtypeStruct((L, SSM_W), BF16),
                   jax.ShapeDtypeStruct((L, SSM_W), F32),
                   jax.ShapeDtypeStruct((1, ATTN_W), F32), jax.ShapeDtypeStruct((1, SSM_W), F32),
                   jax.ShapeDtypeStruct((1, SSM_W), F32)],
        compiler_params=_cp(("arbitrary",)),
    )(d_out_b, w_out, w_glu, og, o, yg, gpre, proj, proj, proj, proj, b_glu.reshape(1, SSM_W), wa.reshape(1, ATTN_W),
      ws.reshape(1, SSM_W))


def _rms_bwd_x(x, norm_w, d_proj, wt_in, d_out, ride):
    L = x.shape[0]
    tm = _tile(L, 256)

    def body(x_ref, w_ref, dp_ref, wt_ref, do_ref, gx_ref, gw_ref):
        i = pl.program_id(0)

        @pl.when(i == 0)
        def _():
            gw_ref[...] = jnp.zeros_like(gw_ref)

        xv = x_ref[...]
        dh = jnp.dot(dp_ref[...], wt_ref[...], preferred_element_type=F32)
        r = lax.rsqrt(jnp.mean(xv * xv, axis=-1, keepdims=True) + NORM_EPS)
        xh = xv * r
        gw_ref[...] += jnp.sum(dh * xh, axis=0, keepdims=True)
        gx = dh * w_ref[...]
        gx_ref[...] = do_ref[...] + r * (gx - xh * jnp.mean(gx * xh, axis=-1, keepdims=True))

    blk = pl.BlockSpec((tm, D_MODEL), lambda i: (i, 0))
    row = pl.BlockSpec((1, D_MODEL), lambda i: (0, 0))
    dp_blk = pl.BlockSpec((tm, IN_W), lambda i: (i, 0))
    wt_blk = pl.BlockSpec((IN_W, D_MODEL), lambda i: (0, 0), pipeline_mode=pl.Buffered(1))
    return _call(body, "d_hn_rms_bwd_x", (L // tm,), [blk, row, dp_blk, wt_blk, blk], [blk, row],
                 [jax.ShapeDtypeStruct((L, D_MODEL), F32), jax.ShapeDtypeStruct((1, D_MODEL), F32)],
                 (x, norm_w.reshape(1, D_MODEL), d_proj, wt_in, d_out), ride=ride)


def _rope_table(positions):
    inv_freq = ROPE_THETA ** (-jnp.arange(0, HEAD_DIM, 2, dtype=F32) / HEAD_DIM)
    ang = positions.astype(F32)[:, None] * inv_freq
    sign = jnp.where(jnp.arange(128) % HEAD_DIM < HEAD_DIM // 2, -1.0, 1.0)
    return jnp.concatenate([jnp.tile(jnp.cos(ang), (1, 4)), jnp.tile(jnp.sin(ang), (1, 4)) * sign], axis=1)


def _step(x, positions, target, w, core, chip):
    small = {n: w[n] for n in _SMALL}
    tab = _rope_table(positions)
    mt_b, scat_b, ocat_b, a16 = _ssm_prep(small)
    perm = _chunk_perm()
    blocks = lambda t: t.reshape(N_DEV, t.shape[0] // N_DEV, t.shape[1])

    proj, hn, wt_in = _rms_inproj_gather(x, small["norm_w"], w["w_in"].T.astype(BF16), chip)
    wt_in = wt_in.reshape(IN_W, D_MODEL)
    q_rot, k_rot = _qk_prep(proj, tab, small["q_norm_w"], small["k_norm_w"])
    (og, o, lse), (w_glu,) = _attn_fwd(q_rot, k_rot, proj, small["sinks"],
                                       _gather_exchange([w["w_glu"].astype(BF16)]))
    (y, yg, hx), (w_out,) = _ssm_fwd(proj, perm, mt_b, scat_b, ocat_b, a16, small["d_skip"],
                                     _gather_exchange([w["w_out"].astype(BF16)]))
    w_glu, w_out = w_glu.reshape(SSM_W, SSM_W), w_out.reshape(D_MODEL, D_MODEL)
    merged, gpre = _merge(og, yg, w_glu, proj, small["b_glu"], small["attn_out_norm_w"], small["ssm_out_norm_w"])
    d_out, d_out_b, loss_parts = _outproj_loss(merged, w_out, x, target)
    loss = 0.5 * jnp.sum(loss_parts[:, 0, 0]) / D_MODEL

    g_w_out = blocks(_mm_tn(merged, d_out_b, "grad_w_out", tm=1024))
    d_o, d_za, d_zs, d_g, d_yg, g_wa, g_ws, g_bglu = _merge_bwd(
        d_out_b, w_out, w_glu, og, o, yg, gpre, proj, small["b_glu"], small["attn_out_norm_w"],
        small["ssm_out_norm_w"])
    g_w_glu = blocks(_mm_tn(yg, d_g, "grad_w_glu", tm=256))
    (d_u, g_mt, g_scat, g_ocat, g_a16, g_dskip), (ra_out, ra_glu) = _ssm_bwd(
        d_yg, y, proj, hx, perm, mt_b, scat_b, ocat_b, a16, small["d_skip"], _pair_exchange([g_w_out, g_w_glu]))
    p_out = _pair_sum(g_w_out, ra_out, core, BF16, "pair_sum_out")
    p_glu = _pair_sum(g_w_glu, ra_glu, core, BF16, "pair_sum_glu")
    (d_q, d_k, d_v, g_sinks), (rb_out, rb_glu) = _attn_bwd(
        q_rot, k_rot, proj, small["sinks"], d_o, o, lse, _chip_exchange([p_out, p_glu]))
    d_proj, g_qw, g_kw = _qk_prep_bwd(proj, tab, small["q_norm_w"], small["k_norm_w"], d_q, d_k, d_v,
                                      d_za, d_u, d_zs)
    g_qw = g_qw[0, :HEAD_DIM] + g_qw[0, HEAD_DIM:]
    g_kw = g_kw[0, :HEAD_DIM] + g_kw[0, HEAD_DIM:]
    g_in_a = blocks(_mm_tn(d_proj, hn, "grad_w_in_a", tm=1152, panel=0))
    g_in_b, (ra_a,) = _mm_tn(d_proj, hn, "grad_w_in_b", tm=1152, panel=1, ride=_pair_exchange([g_in_a]))
    g_in_b = blocks(g_in_b)
    p_a = _pair_sum(g_in_a, ra_a, core, BF16, "pair_sum_in_a")
    (grad_x, g_nw), (rb_a, ra_b) = _rms_bwd_x(x, small["norm_w"], d_proj, wt_in, d_out,
                                              _both(_chip_exchange([p_a]), _pair_exchange([g_in_b])))
    p_b = _pair_sum(g_in_b, ra_b, core, BF16, "pair_sum_in_b")
    g_small, (rb_b,) = _ssm_prep_bwd(small, g_mt, g_scat, g_ocat, g_a16, _chip_exchange([p_b]))

    g_small.update(norm_w=g_nw.reshape(-1), q_norm_w=g_qw.reshape(-1), k_norm_w=g_kw.reshape(-1),
                   sinks=g_sinks[0, :N_HEADS], d_skip=g_dskip.reshape(-1), b_glu=g_bglu.reshape(-1),
                   attn_out_norm_w=g_wa.reshape(-1), ssm_out_norm_w=g_ws.reshape(-1))
    g_packed = _slab_all_reduce(_pack(g_small, loss).reshape(N_DEV, _PACK_ROWS // N_DEV, 128))
    g_packed = g_packed.reshape(_PACK_ROWS, 128)
    grads = _unpack(g_packed, w)
    parts = dict(w_in=([p_a, p_b], [rb_a, rb_b]), w_glu=([p_glu], [rb_glu]), w_out=([p_out], [rb_out]))
    return g_packed[_LOSS_ROW, 0], grad_x, grads, parts


_ANY = pl.BlockSpec(memory_space=pl.ANY)


class _Exchange:
    def __init__(self, arrays, out_shape, sems, start, finish, relay=None):
        self.arrays, self.out_shape, self.sems, self.start, self.finish = arrays, out_shape, sems, start, finish
        self.relay = relay if relay is not None else (lambda ins, outs, sems: None)


def _gather_exchange(blocks):
    n = len(blocks)

    def parts(ins, outs, sems):
        send_sems, recv_sems, local_sems = sems
        x, y, c = lax.axis_index("x"), lax.axis_index("y"), lax.axis_index("c")
        me, sibling = (x, y, c), (x, y, 1 - c)
        chips = [(1 - x, y), (x, 1 - y), (1 - x, 1 - y)]

        def slot(k, dev):
            return outs[k].at[4 * dev[0] + 2 * dev[1] + dev[2]]

        def copy(k, q, block, to, src=None):
            return pltpu.make_async_remote_copy(
                src_ref=slot(k, block) if src is None else src, dst_ref=slot(k, block),
                send_sem=send_sems.at[k, q], recv_sem=recv_sems.at[k, q], device_id=to, device_id_type=MESH)

        mine = [pltpu.make_async_copy(ins[k], slot(k, me), local_sems.at[k]) for k in range(n)]
        first = []
        for k in range(n):
            first.append(copy(k, 0, me, sibling, src=ins[k]))
            first += [copy(k, 1 + j, me, (*chip, c), src=ins[k]) for j, chip in enumerate(chips)]
        return me, sibling, chips, c, copy, mine, first

    def start(ins, outs, sems):
        *_, mine, first = parts(ins, outs, sems)
        for cp in mine + first:
            cp.start()

    def relay(ins, outs, sems):
        me, sibling, chips, c, copy, _, _ = parts(ins, outs, sems)
        for j, chip in enumerate(chips):
            for k in range(n):
                copy(k, 1 + j, (*chip, c), me).wait_recv()
                copy(k, 4 + j, (*chip, c), sibling).start()

    def finish(ins, outs, sems):
        me, sibling, chips, c, copy, mine, first = parts(ins, outs, sems)
        for k in range(n):
            copy(k, 0, sibling, me).wait_recv()
            for j, chip in enumerate(chips):
                copy(k, 4 + j, (*chip, 1 - c), me).wait_recv()
        for cp in first + [copy(k, 4 + j, (*chip, c), sibling) for k in range(n) for j, chip in enumerate(chips)]:
            cp.wait_send()
        for cp in mine:
            cp.wait()

    return _Exchange(blocks, [jax.ShapeDtypeStruct((N_DEV,) + b.shape, b.dtype) for b in blocks],
                     [pltpu.SemaphoreType.DMA((n, 7)), pltpu.SemaphoreType.DMA((n, 7)), pltpu.SemaphoreType.DMA((n,))],
                     start, finish, relay)


def _direct_exchange(arrays, out_lead, fan, route):
    n = len(arrays)

    def copies(ins, outs, sems):
        send_sems, recv_sems = sems
        legs = route(lax.axis_index("x"), lax.axis_index("y"), lax.axis_index("c"))
        return [pltpu.make_async_remote_copy(
            src_ref=ins[k].at[src], dst_ref=outs[k].at[q], send_sem=send_sems.at[k, q], recv_sem=recv_sems.at[k, q],
            device_id=to, device_id_type=MESH) for k in range(n) for src, q, to in legs]

    def start(ins, outs, sems):
        for cp in copies(ins, outs, sems):
            cp.start()

    def finish(ins, outs, sems):
        for cp in copies(ins, outs, sems):
            cp.wait()

    return _Exchange(arrays, [jax.ShapeDtypeStruct((out_lead,) + a.shape[1:], a.dtype) for a in arrays],
                     [pltpu.SemaphoreType.DMA((n, fan)), pltpu.SemaphoreType.DMA((n, fan))], start, finish)


def _pair_exchange(grads):
    return _direct_exchange(grads, 4, 4, lambda x, y, c: [(2 * chip + (1 - c), chip, (x, y, 1 - c))
                                                          for chip in range(4)])


def _chip_exchange(parts):
    def route(x, y, c):
        chips = [(1 - x, y), (x, 1 - y), (1 - x, 1 - y)]
        return [(2 * chip[0] + chip[1], q, (*chip, c)) for q, chip in enumerate(chips)]
    return _direct_exchange(parts, 3, 3, route)


def _both(ex1, ex2):
    n1, s1 = len(ex1.arrays), len(ex1.sems)

    def halves(ins, outs, sems):
        return (ins[:n1], outs[:n1], sems[:s1]), (ins[n1:], outs[n1:], sems[s1:])

    def start(ins, outs, sems):
        h1, h2 = halves(ins, outs, sems)
        ex1.start(*h1)
        ex2.start(*h2)

    def relay(ins, outs, sems):
        h1, h2 = halves(ins, outs, sems)
        ex1.relay(*h1)
        ex2.relay(*h2)

    def finish(ins, outs, sems):
        h1, h2 = halves(ins, outs, sems)
        ex1.finish(*h1)
        ex2.finish(*h2)

    return _Exchange(list(ex1.arrays) + list(ex2.arrays), list(ex1.out_shape) + list(ex2.out_shape),
                     list(ex1.sems) + list(ex2.sems), start, finish, relay)


def _call(body, name, grid, in_specs, out_specs, out_shape, args, scratch_shapes=(), ride=None):
    if ride is None:
        sem = ("arbitrary",) * len(grid)
        return pl.pallas_call(body, name=name, grid=grid, in_specs=in_specs, out_specs=out_specs, out_shape=out_shape,
                              scratch_shapes=list(scratch_shapes), compiler_params=_cp(sem))(*args), None
    n_in, n_out, n_scr, n_x = len(in_specs), len(out_specs), len(scratch_shapes), len(ride.arrays)

    def wrapped(*refs):
        ins, refs = refs[:n_in], refs[n_in:]
        x_in, refs = refs[:n_x], refs[n_x:]
        outs, refs = refs[:n_out], refs[n_out:]
        x_out, refs = refs[:n_x], refs[n_x:]
        scr, sems = refs[:n_scr], refs[n_scr:]
        step, total = pl.program_id(0), grid[0]
        for a in range(1, len(grid)):
            step, total = step * grid[a] + pl.program_id(a), total * grid[a]
        @pl.when(step == 0)
        def _():
            ride.start(x_in, x_out, sems)

        @pl.when(step == max(total - 2, 0))
        def _():
            ride.relay(x_in, x_out, sems)

        body(*ins, *outs, *scr)

        @pl.when(step == total - 1)
        def _():
            ride.finish(x_in, x_out, sems)

    res = pl.pallas_call(
        wrapped, name=name, grid=grid, in_specs=list(in_specs) + [_ANY] * n_x,
        out_specs=list(out_specs) + [_ANY] * n_x, out_shape=list(out_shape) + list(ride.out_shape),
        scratch_shapes=list(scratch_shapes) + list(ride.sems),
        compiler_params=_cp(("arbitrary",) * len(grid)))(*args, *ride.arrays)
    return res[:n_out], list(res[n_out:])


def _pair_sum(g, ra, core, out_dtype, name):
    _, r, C = g.shape
    tr = _tile(r, 576)

    def body(c_ref, g_ref, ra_ref, p_ref):
        p_ref[...] = (g_ref[...] + ra_ref[...]).astype(p_ref.dtype)

    return pl.pallas_call(
        body,
        name=name,
        grid_spec=pltpu.PrefetchScalarGridSpec(
            num_scalar_prefetch=1,
            grid=(4, r // tr),
            in_specs=[pl.BlockSpec((1, tr, C), lambda j, t, c_ref: (2 * j + c_ref[0], t, 0)),
                      pl.BlockSpec((1, tr, C), lambda j, t, c_ref: (j, t, 0))],
            out_specs=pl.BlockSpec((1, tr, C), lambda j, t, c_ref: (j, t, 0)),
        ),
        out_shape=jax.ShapeDtypeStruct((4, r, C), out_dtype),
        compiler_params=_cp(("parallel", "parallel")),
    )(core, g, ra)


def _slab_all_reduce(slab):
    _, r, lanes = slab.shape

    def body(s_ref, o_ref, ra, rb, ps, sems_a, sems_b, sems_c):
        x, y, c = lax.axis_index("x"), lax.axis_index("y"), lax.axis_index("c")
        chips = [(1 - x, y), (x, 1 - y), (1 - x, 1 - y)]
        pair = [pltpu.make_async_remote_copy(
            src_ref=s_ref.at[2 * k + (1 - c)], dst_ref=ra.at[k], send_sem=sems_a.at[0, k], recv_sem=sems_a.at[1, k],
            device_id=(x, y, 1 - c), device_id_type=MESH) for k in range(4)]
        for cp in pair:
            cp.start()
        for cp in pair:
            cp.wait()
        for k in range(4):
            ps[k] = s_ref[2 * k + c] + ra[k]
        cross = [pltpu.make_async_remote_copy(
            src_ref=ps.at[2 * ch[0] + ch[1]], dst_ref=rb.at[q], send_sem=sems_b.at[0, q], recv_sem=sems_b.at[1, q],
            device_id=(*ch, c), device_id_type=MESH) for q, ch in enumerate(chips)]
        for cp in cross:
            cp.start()
        for cp in cross:
            cp.wait()
        me = 4 * x + 2 * y + c
        o_ref[me] = ((ps[2 * x + y] + rb[0]) + rb[1]) + rb[2]
        flips = [(dx, dy, dc) for dx in (0, 1) for dy in (0, 1) for dc in (0, 1) if dx + dy + dc]
        spread = [pltpu.make_async_remote_copy(
            src_ref=o_ref.at[me], dst_ref=o_ref.at[me], send_sem=sems_c.at[0, q], recv_sem=sems_c.at[1, q],
            device_id=(x + dx - 2 * x * dx, y + dy - 2 * y * dy, c + dc - 2 * c * dc), device_id_type=MESH)
            for q, (dx, dy, dc) in enumerate(flips)]
        for cp in spread:
            cp.start()
        for q, (dx, dy, dc) in enumerate(flips):
            peer = 4 * (x + dx - 2 * x * dx) + 2 * (y + dy - 2 * y * dy) + (c + dc - 2 * c * dc)
            pltpu.make_async_remote_copy(
                src_ref=o_ref.at[peer], dst_ref=o_ref.at[peer], send_sem=sems_c.at[0, q], recv_sem=sems_c.at[1, q],
                device_id=(x, y, c), device_id_type=MESH).wait_recv()
        for cp in spread:
            cp.wait_send()

    whole = pl.BlockSpec(memory_space=pltpu.VMEM)
    return pl.pallas_call(
        body, name="slab_all_reduce", in_specs=[whole], out_specs=whole,
        out_shape=jax.ShapeDtypeStruct(slab.shape, F32),
        scratch_shapes=[pltpu.VMEM((4, r, lanes), F32), pltpu.VMEM((3, r, lanes), F32), pltpu.VMEM((4, r, lanes), F32),
                        pltpu.SemaphoreType.DMA((2, 4)), pltpu.SemaphoreType.DMA((2, 3)),
                        pltpu.SemaphoreType.DMA((2, 7))],
        compiler_params=_cp(),
    )(slab)


def _adamw_reduced(ps, rbs, chip, w, m, v, name):
    nh = len(ps)
    R, C = w.shape
    ch = C // nh
    tr = _tile(R, 288)
    nt = R // tr
    c1 = 1.0 - ADAM_B1 ** ADAM_STEP
    c2 = 1.0 - ADAM_B2 ** ADAM_STEP

    def body(c_ref, *refs):
        p_refs, rb_refs = refs[:nh], refs[nh:2 * nh]
        w_ref, m_ref, v_ref, g_ref, d_ref, nm_ref, nv_ref = refs[2 * nh:]
        for h in range(nh):
            @pl.when(pl.program_id(0) == h)
            def _(h=h):
                rb = rb_refs[h]
                gv = p_refs[h][0].astype(F32) + rb[0].astype(F32)
                gv = gv + rb[1].astype(F32)
                gv = gv + rb[2].astype(F32)
                nm = ADAM_B1 * m_ref[...] + (1.0 - ADAM_B1) * gv
                nv = ADAM_B2 * v_ref[...] + (1.0 - ADAM_B2) * (gv * gv)
                g_ref[...] = gv
                nm_ref[...] = nm
                nv_ref[...] = nv
                d_ref[...] = -ADAM_LR * ((nm / c1) / (jnp.sqrt(nv / c2) + ADAM_EPS) + ADAM_WD * w_ref[...])

    def held(h):
        return lambda hh, tt: jnp.where(hh == h, tt, jnp.where(hh < h, 0, nt - 1))

    p_specs = [pl.BlockSpec((1, tr, ch), lambda hh, tt, c_ref, f=held(h): (c_ref[0], f(hh, tt), 0))
               for h in range(nh)]
    rb_specs = [pl.BlockSpec((3, tr, ch), lambda hh, tt, c_ref, f=held(h): (0, f(hh, tt), 0)) for h in range(nh)]
    blk = pl.BlockSpec((tr, ch), lambda hh, tt, c_ref: (tt, hh))
    return pl.pallas_call(
        body,
        name=name,
        grid_spec=pltpu.PrefetchScalarGridSpec(
            num_scalar_prefetch=1, grid=(nh, nt), in_specs=p_specs + rb_specs + [blk] * 3, out_specs=[blk] * 4),
        out_shape=[jax.ShapeDtypeStruct((R, C), F32)] * 4,
        compiler_params=_cp(("arbitrary", "arbitrary")),
    )(chip, *ps, *rbs, w, m, v)


_SMALL = ("norm_w", "q_norm_w", "k_norm_w", "sinks", "a_re", "a_im", "log_step", "b_re", "b_im", "c_re", "c_im",
          "d_skip", "b_glu", "attn_out_norm_w", "ssm_out_norm_w")
_WEIGHTS = ("norm_w", "w_in", "q_norm_w", "k_norm_w", "sinks", "a_re", "a_im", "log_step", "b_re", "b_im", "c_re",
            "c_im", "d_skip", "w_glu", "b_glu", "attn_out_norm_w", "ssm_out_norm_w", "w_out")
_SMALL_2D = dict(norm_w=(1, 2048), q_norm_w=(1, 64), k_norm_w=(1, 64), sinks=(1, 16), a_re=(64, 64), a_im=(64, 64),
                 log_step=(1, 64), b_re=(1024, 64), b_im=(1024, 64), c_re=(1024, 64), c_im=(1024, 64),
                 d_skip=(1, 1024), b_glu=(1, 1024), attn_out_norm_w=(1, 1024), ssm_out_norm_w=(1, 1024))
_P_MINOR = ("b_re", "b_im")


def _flat_form(n, t):
    return t.transpose(0, 2, 1) if n in _P_MINOR else t


def _own_form(n, t, shape):
    if n in _P_MINOR:
        return t.reshape(shape[0], shape[2], shape[1]).transpose(0, 2, 1)
    return t.reshape(shape)


def _slab_rows(n):
    return -(-n // 1024) * 8


_PACK_ROWS = 2304


_LOSS_ROW = 2192


def _pack(d, loss):
    parts = []
    for n in _SMALL:
        flat = _flat_form(n, d[n]).reshape(-1).astype(F32)
        rows = _slab_rows(flat.shape[0])
        parts.append(jnp.pad(flat, (0, rows * 128 - flat.shape[0])).reshape(rows, 128))
    assert sum(p.shape[0] for p in parts) == _LOSS_ROW
    parts.append(jnp.pad(loss.reshape(1, 1), ((0, _PACK_ROWS - _LOSS_ROW - 1), (0, 127))))
    return jnp.concatenate(parts, axis=0)


def _unpack(packed, like):
    out, off = {}, 0
    for n in _SMALL:
        size = math.prod(like[n].shape)
        rows = _slab_rows(size)
        out[n] = _own_form(n, packed[off:off + rows].reshape(-1)[:size], like[n].shape)
        off += rows
    return out


def _adamw_small(g, w, m, v):
    c1 = 1.0 - ADAM_B1 ** ADAM_STEP
    c2 = 1.0 - ADAM_B2 ** ADAM_STEP
    k = len(_SMALL)

    def body(*refs):
        ins, outs = refs[:4 * k], refs[4 * k:]
        for j in range(k):
            gv, wv, mv, vv = (ins[q * k + j][...] for q in range(4))
            nm = ADAM_B1 * mv + (1.0 - ADAM_B1) * gv
            nv = ADAM_B2 * vv + (1.0 - ADAM_B2) * (gv * gv)
            outs[j][...] = -ADAM_LR * ((nm / c1) / (jnp.sqrt(nv / c2) + ADAM_EPS) + ADAM_WD * wv)
            outs[k + j][...] = nm
            outs[2 * k + j][...] = nv

    args = [_flat_form(n, d[n]).reshape(_SMALL_2D[n]) for d in (g, w, m, v) for n in _SMALL]
    shapes = [jax.ShapeDtypeStruct(_SMALL_2D[n], F32) for _ in range(3) for n in _SMALL]
    outs = pl.pallas_call(body, name="adamw_small", out_shape=shapes, compiler_params=_cp())(*args)
    res = []
    for q in range(3):
        res.append({n: _own_form(n, outs[q * k + j], w[n].shape) for j, n in enumerate(_SMALL)})
    return res


def kernel(x, positions, norm_w, w_in, q_norm_w, k_norm_w, sinks, a_re, a_im, log_step, b_re, b_im, c_re, c_im, d_skip, w_glu, b_glu, attn_out_norm_w, ssm_out_norm_w, w_out, loss_target, m_norm_w, m_w_in, m_q_norm_w, m_k_norm_w, m_sinks, m_a_re, m_a_im, m_log_step, m_b_re, m_b_im, m_c_re, m_c_im, m_d_skip, m_w_glu, m_b_glu, m_attn_out_norm_w, m_ssm_out_norm_w, m_w_out, v_norm_w, v_w_in, v_q_norm_w, v_k_norm_w, v_sinks, v_a_re, v_a_im, v_log_step, v_b_re, v_b_im, v_c_re, v_c_im, v_d_skip, v_w_glu, v_b_glu, v_attn_out_norm_w, v_ssm_out_norm_w, v_w_out):
    w = dict(norm_w=norm_w, w_in=w_in, q_norm_w=q_norm_w, k_norm_w=k_norm_w, sinks=sinks, a_re=a_re, a_im=a_im,
             log_step=log_step, b_re=b_re, b_im=b_im, c_re=c_re, c_im=c_im, d_skip=d_skip, w_glu=w_glu, b_glu=b_glu,
             attn_out_norm_w=attn_out_norm_w, ssm_out_norm_w=ssm_out_norm_w, w_out=w_out)
    m = dict(norm_w=m_norm_w, w_in=m_w_in, q_norm_w=m_q_norm_w, k_norm_w=m_k_norm_w, sinks=m_sinks, a_re=m_a_re,
             a_im=m_a_im, log_step=m_log_step, b_re=m_b_re, b_im=m_b_im, c_re=m_c_re, c_im=m_c_im, d_skip=m_d_skip,
             w_glu=m_w_glu, b_glu=m_b_glu, attn_out_norm_w=m_attn_out_norm_w, ssm_out_norm_w=m_ssm_out_norm_w,
             w_out=m_w_out)
    v = dict(norm_w=v_norm_w, w_in=v_w_in, q_norm_w=v_q_norm_w, k_norm_w=v_k_norm_w, sinks=v_sinks, a_re=v_a_re,
             a_im=v_a_im, log_step=v_log_step, b_re=v_b_re, b_im=v_b_im, c_re=v_c_re, c_im=v_c_im, d_skip=v_d_skip,
             w_glu=v_w_glu, b_glu=v_b_glu, attn_out_norm_w=v_attn_out_norm_w, ssm_out_norm_w=v_ssm_out_norm_w,
             w_out=v_w_out)
    core = lax.axis_index("c").astype(jnp.int32).reshape(1)
    chip = (2 * lax.axis_index("x") + lax.axis_index("y")).astype(jnp.int32).reshape(1)

    loss, grad_x, grads, parts = _step(x[0], positions[0], loss_target[0], w, core, chip)
    delta, new_m, new_v = {}, {}, {}
    for n in ("w_glu", "w_out"):
        grads[n], delta[n], new_m[n], new_v[n] = _adamw_reduced(*parts[n], chip, w[n], m[n], v[n], f"adamw_{n}")
    g_t, d_t, m_t, v_t = _adamw_reduced(*parts["w_in"], chip, w["w_in"].T, m["w_in"].T, v["w_in"].T, "adamw_w_in")
    grads["w_in"], delta["w_in"], new_m["w_in"], new_v["w_in"] = g_t.T, d_t.T, m_t.T, v_t.T
    d_s, m_s, v_s = _adamw_small(grads, w, m, v)
    delta.update(d_s)
    new_m.update(m_s)
    new_v.update(v_s)

    return (loss, grad_x[None], *[grads[n] for n in _WEIGHTS], *[delta[n] for n in _WEIGHTS],
            *[new_m[n] for n in _WEIGHTS], *[new_v[n] for n in _WEIGHTS])
```
